```python
import math
import jax, jax.numpy as jnp
from jax import lax
import numpy as np

D_MODEL = 1024
BATCH = 8
SEQ = 4096
DEPTH = 2

N_HEADS_A = 8
HEAD_DIM_A = 128
CONV_A = 4
CHUNK = 64
N_HEADS_B = 16
N_KV_B = 4
HEAD_DIM_B = 64
WINDOW = 128
D_FF = 2816
FFN_CONV = 3
PLE_DIM = 256
EPS = 1e-6
N_A_LAYERS = (DEPTH + 1) // 2
N_B_LAYERS = DEPTH // 2

kernel_name = "hybrid_deltanet_swa_convffn_ple"


def rmsnorm(x, w):
    x32 = x.astype(jnp.float32)
    y = x32 * lax.rsqrt(jnp.mean(x32 * x32, axis=-1, keepdims=True) + EPS)
    return (y * w.astype(jnp.float32)).astype(x.dtype)


def l2norm(x):
    x32 = x.astype(jnp.float32)
    return x32 * lax.rsqrt(jnp.sum(x32 * x32, axis=-1, keepdims=True) + EPS)


def causal_dwconv(x, w):
    k_w = w.shape[0]
    t = x.shape[1]
    xp = jnp.pad(x, ((0, 0), (k_w - 1, 0), (0, 0)))
    y = w[0] * xp[:, 0:t]
    for i in range(1, k_w):
        y = y + w[i] * xp[:, i:i + t]
    return y


def gated_delta_rule(q, k, v, g, beta):
    b_, t_, h_, dk = q.shape
    dv = v.shape[-1]
    n_ch = t_ // CHUNK
    f32 = jnp.float32

    def chunks(a):
        return a.astype(f32).reshape(b_, n_ch, CHUNK, h_, a.shape[-1]).transpose(1, 0, 3, 2, 4)

    qc = chunks(q) * (dk ** -0.5)
    kc = chunks(k)
    vc = chunks(v)
    gcs = jnp.cumsum(g.astype(f32).reshape(b_, n_ch, CHUNK, h_).transpose(1, 0, 3, 2), axis=-1)
    bc = beta.astype(f32).reshape(b_, n_ch, CHUNK, h_).transpose(1, 0, 3, 2)

    idx = jnp.arange(CHUNK)
    incl = idx[:, None] >= idx[None, :]
    strict = idx[:, None] > idx[None, :]
    diff = gcs[..., :, None] - gcs[..., None, :]
    decay = jnp.exp(jnp.where(incl, diff, -jnp.inf))

    kk = jnp.einsum('nbhik,nbhjk->nbhij', kc, kc)
    lmat = jnp.where(strict, bc[..., :, None] * kk * decay, 0.0)
    eye_l = jnp.eye(CHUNK, dtype=f32) + lmat
    rhs = jnp.concatenate([bc[..., None] * vc,
                           (bc * jnp.exp(gcs))[..., None] * kc], axis=-1)
    sol = lax.linalg.triangular_solve(eye_l, rhs, left_side=True, lower=True, unit_diagonal=True)
    u0 = sol[..., :dv]
    wk = sol[..., dv:]

    qk = jnp.einsum('nbhik,nbhjk->nbhij', qc, kc) * decay
    q_dec = qc * jnp.exp(gcs)[..., None]
    k_dec = kc * jnp.exp(gcs[..., -1:] - gcs)[..., None]
    g_last = jnp.exp(gcs[..., -1])

    def step(s, inp):
        qd, kd, qkc, u0c, wc, gl = inp
        u = u0c - jnp.einsum('bhck,bhkv->bhcv', wc, s)
        o = jnp.einsum('bhck,bhkv->bhcv', qd, s) + jnp.einsum('bhij,bhjv->bhiv', qkc, u)
        s = gl[..., None, None] * s + jnp.einsum('bhck,bhcv->bhkv', kd, u)
        return s, o

    s0 = jnp.zeros((b_, h_, dk, dv), f32)
    _, o = lax.scan(step, s0, (q_dec, k_dec, qk, u0, wk, g_last))
    return o.transpose(1, 0, 3, 2, 4).reshape(b_, t_, h_, dv)


def deltanet_mixer(x, w_in, conv_w, a_log, dt_bias, norm_w, w_out):
    b_, t_, _ = x.shape
    hk = N_HEADS_A * HEAD_DIM_A
    proj = x @ w_in
    qkv = jax.nn.silu(causal_dwconv(proj[..., :3 * hk], conv_w))
    z = proj[..., 3 * hk:4 * hk].reshape(b_, t_, N_HEADS_A, HEAD_DIM_A)
    b_logit = proj[..., 4 * hk:4 * hk + N_HEADS_A]
    a_in = proj[..., 4 * hk + N_HEADS_A:]
    q = l2norm(qkv[..., :hk].reshape(b_, t_, N_HEADS_A, HEAD_DIM_A))
    k = l2norm(qkv[..., hk:2 * hk].reshape(b_, t_, N_HEADS_A, HEAD_DIM_A))
    v = qkv[..., 2 * hk:].reshape(b_, t_, N_HEADS_A, HEAD_DIM_A)
    beta = jax.nn.sigmoid(b_logit.astype(jnp.float32))
    g = -jnp.exp(a_log.astype(jnp.float32)) * jax.nn.softplus(a_in.astype(jnp.float32) + dt_bias.astype(jnp.float32))
    o = gated_delta_rule(q, k, v, g, beta).astype(x.dtype)
    o = rmsnorm(o, norm_w) * jax.nn.silu(z)
    return o.reshape(b_, t_, hk) @ w_out


def alibi_slopes(n_heads):
    return 2.0 ** (-8.0 * jnp.arange(1, n_heads + 1, dtype=jnp.float32) / n_heads)


def swa_mixer(x, w_in, sinks, w_out):
    b_, t_, _ = x.shape
    n_blk = t_ // WINDOW
    grp = N_HEADS_B // N_KV_B
    qd = N_HEADS_B * HEAD_DIM_B
    kd = N_KV_B * HEAD_DIM_B
    proj = x @ w_in
    q = proj[..., :qd].reshape(b_, n_blk, WINDOW, N_KV_B, grp, HEAD_DIM_B)
    k = proj[..., qd:qd + kd].reshape(b_, n_blk, WINDOW, N_KV_B, HEAD_DIM_B)
    v = proj[..., qd + kd:].reshape(b_, n_blk, WINDOW, N_KV_B, HEAD_DIM_B)

    def with_prev(a):
        prev = jnp.pad(a[:, :-1], ((0, 0), (1, 0), (0, 0), (0, 0), (0, 0)))
        return jnp.concatenate([prev, a], axis=2)

    kb, vb = with_prev(k), with_prev(v)
    scores = jnp.einsum('bnqhgd,bnkhd->bhgnqk', q, kb).astype(jnp.float32) * (HEAD_DIM_B ** -0.5)
    qi = jnp.arange(WINDOW)[:, None]
    kj = jnp.arange(2 * WINDOW)[None, :]
    dist = qi + WINDOW - kj
    blk = jnp.arange(n_blk)[:, None, None]
    valid = (dist >= 0) & (dist < WINDOW) & (blk * WINDOW - WINDOW + kj >= 0)
    slopes = alibi_slopes(N_HEADS_B).reshape(N_KV_B, grp)
    logits = scores - slopes[None, :, :, None, None, None] * dist.astype(jnp.float32)
    logits = jnp.where(valid, logits, -jnp.inf)
    sink = sinks.astype(jnp.float32).reshape(1, N_KV_B, grp, 1, 1, 1)
    m = jnp.maximum(jnp.max(logits, axis=-1, keepdims=True), sink)
    e = jnp.exp(logits - m)
    probs = e / (jnp.sum(e, axis=-1, keepdims=True) + jnp.exp(sink - m))
    out = jnp.einsum('bhgnqk,bnkhd->bnqhgd', probs.astype(x.dtype), vb).reshape(b_, t_, qd)
    return out @ w_out


def conv_ffn(x, w_up, conv_w, w_down):
    u = causal_dwconv(x @ w_up, conv_w)
    gate, val = u[..., :D_FF], u[..., D_FF:]
    return (jax.nn.silu(gate) * val) @ w_down


def _fwd_setup_inputs(seed: int = 0) -> dict:
    key = jax.random.key(seed)
    ks = jax.random.split(key, 24)
    f32 = jnp.float32
    hk = N_HEADS_A * HEAD_DIM_A

    def dense(k, shape, fan_in):
        return jax.random.normal(k, shape, f32) * (fan_in ** -0.5)

    def gain(k, shape):
        return 1.0 + 0.02 * jax.random.normal(k, shape, f32)

    x = jax.random.normal(ks[0], (BATCH, SEQ, D_MODEL), f32)
    p = jax.random.normal(ks[1], (DEPTH, BATCH, SEQ, PLE_DIM), f32)
    norm_mix = gain(ks[2], (DEPTH, D_MODEL))
    norm_ffn = gain(ks[3], (DEPTH, D_MODEL))
    norm_ple = gain(ks[4], (DEPTH, D_MODEL))
    norm_final = gain(ks[5], (D_MODEL,))

    a_w_in = dense(ks[6], (N_A_LAYERS, D_MODEL, 4 * hk + 2 * N_HEADS_A), D_MODEL)
    a_conv = jax.random.normal(ks[7], (N_A_LAYERS, CONV_A, 3 * hk), f32) * (CONV_A ** -0.5)
    a_log = jnp.log(jax.random.uniform(ks[8], (N_A_LAYERS, N_HEADS_A), f32, 1.0, 16.0))
    dt = jnp.exp(jax.random.uniform(ks[9], (N_A_LAYERS, N_HEADS_A), f32, math.log(1e-3), math.log(1e-1)))
    a_dt_bias = dt + jnp.log(-jnp.expm1(-dt))
    a_norm = gain(ks[10], (N_A_LAYERS, HEAD_DIM_A))
    a_w_out = dense(ks[11], (N_A_LAYERS, hk, D_MODEL), hk)

    qd = N_HEADS_B * HEAD_DIM_B
    kd = N_KV_B * HEAD_DIM_B
    b_w_in = dense(ks[12], (N_B_LAYERS, D_MODEL, qd + 2 * kd), D_MODEL)
    b_sinks = jax.random.normal(ks[13], (N_B_LAYERS, N_HEADS_B), f32)
    b_w_out = dense(ks[14], (N_B_LAYERS, qd, D_MODEL), qd)

    f_w_up = dense(ks[15], (DEPTH, D_MODEL, 2 * D_FF), D_MODEL)
    f_conv = jax.random.normal(ks[16], (DEPTH, FFN_CONV, 2 * D_FF), f32) * (FFN_CONV ** -0.5)
    f_w_down = dense(ks[17], (DEPTH, D_FF, D_MODEL), D_FF)

    ple_w_proj = dense(ks[18], (DEPTH, PLE_DIM, D_MODEL), PLE_DIM)
    ple_w_gate = dense(ks[19], (DEPTH, D_MODEL, D_MODEL), D_MODEL)

    return {"x": x, "p": p, "norm_mix": norm_mix, "norm_ffn": norm_ffn,
            "norm_ple": norm_ple, "norm_final": norm_final,
            "a_w_in": a_w_in, "a_conv": a_conv, "a_log": a_log, "a_dt_bias": a_dt_bias,
            "a_norm": a_norm, "a_w_out": a_w_out,
            "b_w_in": b_w_in, "b_sinks": b_sinks, "b_w_out": b_w_out,
            "f_w_up": f_w_up, "f_conv": f_conv, "f_w_down": f_w_down,
            "ple_w_proj": ple_w_proj, "ple_w_gate": ple_w_gate}


def _fwd_reference(x, p, norm_mix, norm_ffn, norm_ple, norm_final,
              a_w_in, a_conv, a_log, a_dt_bias, a_norm, a_w_out,
              b_w_in, b_sinks, b_w_out,
              f_w_up, f_conv, f_w_down,
              ple_w_proj, ple_w_gate):
    h = x
    for i in range(DEPTH):
        hn = rmsnorm(h, norm_mix[i])
        j = i // 2
        if i % 2 == 0:
            mix = deltanet_mixer(hn, a_w_in[j], a_conv[j], a_log[j], a_dt_bias[j], a_norm[j], a_w_out[j])
        else:
            mix = swa_mixer(hn, b_w_in[j], b_sinks[j], b_w_out[j])
        h = h + mix
        h = h + conv_ffn(rmsnorm(h, norm_ffn[i]), f_w_up[i], f_conv[i], f_w_down[i])
        gate = jax.nn.sigmoid(rmsnorm(h, norm_ple[i]) @ ple_w_gate[i])
        h = h + gate * (p[i] @ ple_w_proj[i])
    return rmsnorm(h, norm_final)


import jax as _jax
import jax.numpy as _jnp

TWIN_FORMAT = 'train_step'
FWD_PARAMS = ['x', 'p', 'norm_mix', 'norm_ffn', 'norm_ple', 'norm_final', 'a_w_in', 'a_conv', 'a_log', 'a_dt_bias', 'a_norm', 'a_w_out', 'b_w_in', 'b_sinks', 'b_w_out', 'f_w_up', 'f_conv', 'f_w_down', 'ple_w_proj', 'ple_w_gate']
TWIN_WEIGHTS = ['norm_mix', 'norm_ffn', 'norm_ple', 'norm_final', 'a_w_in', 'a_conv', 'a_log', 'a_dt_bias', 'a_norm', 'a_w_out', 'b_w_in', 'b_sinks', 'b_w_out', 'f_w_up', 'f_conv', 'f_w_down', 'ple_w_proj', 'ple_w_gate']
TWIN_DIFF_INPUT = 'x'
TWIN_INPUTS = ['x', 'p', 'norm_mix', 'norm_ffn', 'norm_ple', 'norm_final', 'a_w_in', 'a_conv', 'a_log', 'a_dt_bias', 'a_norm', 'a_w_out', 'b_w_in', 'b_sinks', 'b_w_out', 'f_w_up', 'f_conv', 'f_w_down', 'ple_w_proj', 'ple_w_gate', 'loss_target', 'm_norm_mix', 'm_norm_ffn', 'm_norm_ple', 'm_norm_final', 'm_a_w_in', 'm_a_conv', 'm_a_log', 'm_a_dt_bias', 'm_a_norm', 'm_a_w_out', 'm_b_w_in', 'm_b_sinks', 'm_b_w_out', 'm_f_w_up', 'm_f_conv', 'm_f_w_down', 'm_ple_w_proj', 'm_ple_w_gate', 'v_norm_mix', 'v_norm_ffn', 'v_norm_ple', 'v_norm_final', 'v_a_w_in', 'v_a_conv', 'v_a_log', 'v_a_dt_bias', 'v_a_norm', 'v_a_w_out', 'v_b_w_in', 'v_b_sinks', 'v_b_w_out', 'v_f_w_up', 'v_f_conv', 'v_f_w_down', 'v_ple_w_proj', 'v_ple_w_gate']
TWIN_OUTPUTS = ['loss', 'grad_x', 'grad_norm_mix', 'grad_norm_ffn', 'grad_norm_ple', 'grad_norm_final', 'grad_a_w_in', 'grad_a_conv', 'grad_a_log', 'grad_a_dt_bias', 'grad_a_norm', 'grad_a_w_out', 'grad_b_w_in', 'grad_b_sinks', 'grad_b_w_out', 'grad_f_w_up', 'grad_f_conv', 'grad_f_w_down', 'grad_ple_w_proj', 'grad_ple_w_gate', 'delta_norm_mix', 'delta_norm_ffn', 'delta_norm_ple', 'delta_norm_final', 'delta_a_w_in', 'delta_a_conv', 'delta_a_log', 'delta_a_dt_bias', 'delta_a_norm', 'delta_a_w_out', 'delta_b_w_in', 'delta_b_sinks', 'delta_b_w_out', 'delta_f_w_up', 'delta_f_conv', 'delta_f_w_down', 'delta_ple_w_proj', 'delta_ple_w_gate', 'new_m_norm_mix', 'new_m_norm_ffn', 'new_m_norm_ple', 'new_m_norm_final', 'new_m_a_w_in', 'new_m_a_conv', 'new_m_a_log', 'new_m_a_dt_bias', 'new_m_a_norm', 'new_m_a_w_out', 'new_m_b_w_in', 'new_m_b_sinks', 'new_m_b_w_out', 'new_m_f_w_up', 'new_m_f_conv', 'new_m_f_w_down', 'new_m_ple_w_proj', 'new_m_ple_w_gate', 'new_v_norm_mix', 'new_v_norm_ffn', 'new_v_norm_ple', 'new_v_norm_final', 'new_v_a_w_in', 'new_v_a_conv', 'new_v_a_log', 'new_v_a_dt_bias', 'new_v_a_norm', 'new_v_a_w_out', 'new_v_b_w_in', 'new_v_b_sinks', 'new_v_b_w_out', 'new_v_f_w_up', 'new_v_f_conv', 'new_v_f_w_down', 'new_v_ple_w_proj', 'new_v_ple_w_gate']
TWIN_LEAF_KINDS = {'loss': 'loss', 'grad_x': 'grad_x', 'grad_norm_mix': 'grad_w', 'grad_norm_ffn': 'grad_w', 'grad_norm_ple': 'grad_w', 'grad_norm_final': 'grad_w', 'grad_a_w_in': 'grad_w', 'grad_a_conv': 'grad_w', 'grad_a_log': 'grad_w', 'grad_a_dt_bias': 'grad_w', 'grad_a_norm': 'grad_w', 'grad_a_w_out': 'grad_w', 'grad_b_w_in': 'grad_w', 'grad_b_sinks': 'grad_w', 'grad_b_w_out': 'grad_w', 'grad_f_w_up': 'grad_w', 'grad_f_conv': 'grad_w', 'grad_f_w_down': 'grad_w', 'grad_ple_w_proj': 'grad_w', 'grad_ple_w_gate': 'grad_w', 'delta_norm_mix': 'delta_w', 'delta_norm_ffn': 'delta_w', 'delta_norm_ple': 'delta_w', 'delta_norm_final': 'delta_w', 'delta_a_w_in': 'delta_w', 'delta_a_conv': 'delta_w', 'delta_a_log': 'delta_w', 'delta_a_dt_bias': 'delta_w', 'delta_a_norm': 'delta_w', 'delta_a_w_out': 'delta_w', 'delta_b_w_in': 'delta_w', 'delta_b_sinks': 'delta_w', 'delta_b_w_out': 'delta_w', 'delta_f_w_up': 'delta_w', 'delta_f_conv': 'delta_w', 'delta_f_w_down': 'delta_w', 'delta_ple_w_proj': 'delta_w', 'delta_ple_w_gate': 'delta_w', 'new_m_norm_mix': 'new_m', 'new_m_norm_ffn': 'new_m', 'new_m_norm_ple': 'new_m', 'new_m_norm_final': 'new_m', 'new_m_a_w_in': 'new_m', 'new_m_a_conv': 'new_m', 'new_m_a_log': 'new_m', 'new_m_a_dt_bias': 'new_m', 'new_m_a_norm': 'new_m', 'new_m_a_w_out': 'new_m', 'new_m_b_w_in': 'new_m', 'new_m_b_sinks': 'new_m', 'new_m_b_w_out': 'new_m', 'new_m_f_w_up': 'new_m', 'new_m_f_conv': 'new_m', 'new_m_f_w_down': 'new_m', 'new_m_ple_w_proj': 'new_m', 'new_m_ple_w_gate': 'new_m', 'new_v_norm_mix': 'new_v', 'new_v_norm_ffn': 'new_v', 'new_v_norm_ple': 'new_v', 'new_v_norm_final': 'new_v', 'new_v_a_w_in': 'new_v', 'new_v_a_conv': 'new_v', 'new_v_a_log': 'new_v', 'new_v_a_dt_bias': 'new_v', 'new_v_a_norm': 'new_v', 'new_v_a_w_out': 'new_v', 'new_v_b_w_in': 'new_v', 'new_v_b_sinks': 'new_v', 'new_v_b_w_out': 'new_v', 'new_v_f_w_up': 'new_v', 'new_v_f_conv': 'new_v', 'new_v_f_w_down': 'new_v', 'new_v_ple_w_proj': 'new_v', 'new_v_ple_w_gate': 'new_v'}


def _forward(args):
    return _fwd_reference(*[args[k] for k in FWD_PARAMS])


def _output_shape():
    out = _jax.eval_shape(lambda: _forward(_fwd_setup_inputs(0)))
    return out.shape, out.dtype

N_MICROBATCH = 1
ADAM_LR = 0.001
ADAM_B1 = 0.9
ADAM_B2 = 0.999
ADAM_EPS = 1e-08
ADAM_WD = 0.01
ADAM_STEP = 10
PER_EXAMPLE_BATCH_AXIS = {'x': 0, 'p': 1, 'loss_target': 0}
SHARED_INPUTS = []
_WEIGHT_DTYPES = {'norm_mix': _jnp.float32, 'norm_ffn': _jnp.float32, 'norm_ple': _jnp.float32, 'norm_final': _jnp.float32, 'a_w_in': _jnp.float32, 'a_conv': _jnp.float32, 'a_log': _jnp.float32, 'a_dt_bias': _jnp.float32, 'a_norm': _jnp.float32, 'a_w_out': _jnp.float32, 'b_w_in': _jnp.float32, 'b_sinks': _jnp.float32, 'b_w_out': _jnp.float32, 'f_w_up': _jnp.float32, 'f_conv': _jnp.float32, 'f_w_down': _jnp.float32, 'ple_w_proj': _jnp.float32, 'ple_w_gate': _jnp.float32}
MOMENT_SCALE = {'norm_mix': 1.247424e-01, 'norm_ffn': 1.173501e-01, 'norm_ple': 2.639357e-02, 'norm_final': 3.198395e+01, 'a_w_in': 8.221275e-02, 'a_conv': 7.724014e-02, 'a_log': 6.003685e-01, 'a_dt_bias': 5.347590e-01, 'a_norm': 3.907730e-01, 'a_w_out': 9.744476e-02, 'b_w_in': 4.538658e-02, 'b_sinks': 5.138073e-02, 'b_w_out': 3.858284e-02, 'f_w_up': 4.502725e-02, 'f_conv': 4.521786e-02, 'f_w_down': 7.378101e-02, 'ple_w_proj': 6.497710e-02, 'ple_w_gate': 2.541492e-02}


def _to_microbatches(a, axis):
    t = _jnp.moveaxis(a, axis, 0)
    t = t.reshape((N_MICROBATCH, t.shape[0] // N_MICROBATCH) + t.shape[1:])
    return _jnp.moveaxis(t, 1, axis + 1)


def setup_inputs(seed: int = 0) -> dict:
    inp = _fwd_setup_inputs(seed)
    key = _jax.random.fold_in(_jax.random.key(seed), 7919)
    shape, _ = _output_shape()
    out = dict(inp)
    out["loss_target"] = _jax.random.normal(_jax.random.fold_in(key, 0), shape, _jnp.float32)
    for i, name in enumerate(TWIN_WEIGHTS):
        w = inp[name].astype(_jnp.float32)
        if MOMENT_SCALE is None:
            s = _jnp.sqrt(_jnp.mean(_jnp.square(w)) + 1e-30)
        else:
            s = MOMENT_SCALE[name]
        km, kv = _jax.random.split(_jax.random.fold_in(key, i + 1))
        out[name] = w
        out["m_" + name] = s * _jax.random.normal(km, w.shape, _jnp.float32)
        out["v_" + name] = (s * s) * _jax.random.uniform(kv, w.shape, _jnp.float32, 0.5, 1.5)
    if N_MICROBATCH > 1:
        for name, axis in PER_EXAMPLE_BATCH_AXIS.items():
            out[name] = _to_microbatches(out[name], axis)
    return {'x': out['x'], 'p': out['p'], 'norm_mix': out['norm_mix'], 'norm_ffn': out['norm_ffn'], 'norm_ple': out['norm_ple'], 'norm_final': out['norm_final'], 'a_w_in': out['a_w_in'], 'a_conv': out['a_conv'], 'a_log': out['a_log'], 'a_dt_bias': out['a_dt_bias'], 'a_norm': out['a_norm'], 'a_w_out': out['a_w_out'], 'b_w_in': out['b_w_in'], 'b_sinks': out['b_sinks'], 'b_w_out': out['b_w_out'], 'f_w_up': out['f_w_up'], 'f_conv': out['f_conv'], 'f_w_down': out['f_w_down'], 'ple_w_proj': out['ple_w_proj'], 'ple_w_gate': out['ple_w_gate'], 'loss_target': out['loss_target'], 'm_norm_mix': out['m_norm_mix'], 'm_norm_ffn': out['m_norm_ffn'], 'm_norm_ple': out['m_norm_ple'], 'm_norm_final': out['m_norm_final'], 'm_a_w_in': out['m_a_w_in'], 'm_a_conv': out['m_a_conv'], 'm_a_log': out['m_a_log'], 'm_a_dt_bias': out['m_a_dt_bias'], 'm_a_norm': out['m_a_norm'], 'm_a_w_out': out['m_a_w_out'], 'm_b_w_in': out['m_b_w_in'], 'm_b_sinks': out['m_b_sinks'], 'm_b_w_out': out['m_b_w_out'], 'm_f_w_up': out['m_f_w_up'], 'm_f_conv': out['m_f_conv'], 'm_f_w_down': out['m_f_w_down'], 'm_ple_w_proj': out['m_ple_w_proj'], 'm_ple_w_gate': out['m_ple_w_gate'], 'v_norm_mix': out['v_norm_mix'], 'v_norm_ffn': out['v_norm_ffn'], 'v_norm_ple': out['v_norm_ple'], 'v_norm_final': out['v_norm_final'], 'v_a_w_in': out['v_a_w_in'], 'v_a_conv': out['v_a_conv'], 'v_a_log': out['v_a_log'], 'v_a_dt_bias': out['v_a_dt_bias'], 'v_a_norm': out['v_a_norm'], 'v_a_w_out': out['v_a_w_out'], 'v_b_w_in': out['v_b_w_in'], 'v_b_sinks': out['v_b_sinks'], 'v_b_w_out': out['v_b_w_out'], 'v_f_w_up': out['v_f_w_up'], 'v_f_conv': out['v_f_conv'], 'v_f_w_down': out['v_f_w_down'], 'v_ple_w_proj': out['v_ple_w_proj'], 'v_ple_w_gate': out['v_ple_w_gate']}


def _loss(weights, diff, rest, loss_target):
    with _jax.named_scope("forward"):
        args = {**rest, TWIN_DIFF_INPUT: diff, **{k: w.astype(_WEIGHT_DTYPES[k]) for k, w in weights.items()}}
        y = _forward(args)
    with _jax.named_scope("loss_head"):
        err = _jnp.square(y.astype(_jnp.float32) - loss_target)
        return 0.5 * _jnp.sum(_jnp.mean(err, axis=-1)) if err.ndim else 0.5 * err


def _adamw(w, g, m, v):
    m = ADAM_B1 * m + (1.0 - ADAM_B1) * g
    v = ADAM_B2 * v + (1.0 - ADAM_B2) * _jnp.square(g)
    m_hat = m / (1.0 - ADAM_B1 ** ADAM_STEP)
    v_hat = v / (1.0 - ADAM_B2 ** ADAM_STEP)
    delta = -ADAM_LR * (m_hat / (_jnp.sqrt(v_hat) + ADAM_EPS) + ADAM_WD * w)
    return delta, m, v


def reference(x, p, norm_mix, norm_ffn, norm_ple, norm_final, a_w_in, a_conv, a_log, a_dt_bias, a_norm, a_w_out, b_w_in, b_sinks, b_w_out, f_w_up, f_conv, f_w_down, ple_w_proj, ple_w_gate, loss_target, m_norm_mix, m_norm_ffn, m_norm_ple, m_norm_final, m_a_w_in, m_a_conv, m_a_log, m_a_dt_bias, m_a_norm, m_a_w_out, m_b_w_in, m_b_sinks, m_b_w_out, m_f_w_up, m_f_conv, m_f_w_down, m_ple_w_proj, m_ple_w_gate, v_norm_mix, v_norm_ffn, v_norm_ple, v_norm_final, v_a_w_in, v_a_conv, v_a_log, v_a_dt_bias, v_a_norm, v_a_w_out, v_b_w_in, v_b_sinks, v_b_w_out, v_f_w_up, v_f_conv, v_f_w_down, v_ple_w_proj, v_ple_w_gate):
    given = dict(x=x, p=p, norm_mix=norm_mix, norm_ffn=norm_ffn, norm_ple=norm_ple, norm_final=norm_final, a_w_in=a_w_in, a_conv=a_conv, a_log=a_log, a_dt_bias=a_dt_bias, a_norm=a_norm, a_w_out=a_w_out, b_w_in=b_w_in, b_sinks=b_sinks, b_w_out=b_w_out, f_w_up=f_w_up, f_conv=f_conv, f_w_down=f_w_down, ple_w_proj=ple_w_proj, ple_w_gate=ple_w_gate, loss_target=loss_target, m_norm_mix=m_norm_mix, m_norm_ffn=m_norm_ffn, m_norm_ple=m_norm_ple, m_norm_final=m_norm_final, m_a_w_in=m_a_w_in, m_a_conv=m_a_conv, m_a_log=m_a_log, m_a_dt_bias=m_a_dt_bias, m_a_norm=m_a_norm, m_a_w_out=m_a_w_out, m_b_w_in=m_b_w_in, m_b_sinks=m_b_sinks, m_b_w_out=m_b_w_out, m_f_w_up=m_f_w_up, m_f_conv=m_f_conv, m_f_w_down=m_f_w_down, m_ple_w_proj=m_ple_w_proj, m_ple_w_gate=m_ple_w_gate, v_norm_mix=v_norm_mix, v_norm_ffn=v_norm_ffn, v_norm_ple=v_norm_ple, v_norm_final=v_norm_final, v_a_w_in=v_a_w_in, v_a_conv=v_a_conv, v_a_log=v_a_log, v_a_dt_bias=v_a_dt_bias, v_a_norm=v_a_norm, v_a_w_out=v_a_w_out, v_b_w_in=v_b_w_in, v_b_sinks=v_b_sinks, v_b_w_out=v_b_w_out, v_f_w_up=v_f_w_up, v_f_conv=v_f_conv, v_f_w_down=v_f_w_down, v_ple_w_proj=v_ple_w_proj, v_ple_w_gate=v_ple_w_gate)
    weights = {n: given[n] for n in TWIN_WEIGHTS}
    shared = {n: given[n] for n in SHARED_INPUTS}
    per_example = {n: given[n] for n in ['x', 'p']}
    grad_fn = _jax.value_and_grad(_loss, argnums=(0, 1))

    def one_microbatch(ex, loss_target):
        ex = dict(ex)
        diff = ex.pop(TWIN_DIFF_INPUT)
        return grad_fn(weights, diff, {**shared, **ex}, loss_target)

    if N_MICROBATCH == 1:
        loss, (grad_w, grad_x) = one_microbatch(per_example, given["loss_target"])
    else:
        def body(carry, xs):
            loss_sum, grad_sum = carry
            l_k, (gw_k, gx_k) = one_microbatch(xs[0], xs[1])
            with _jax.named_scope("update"):
                return (loss_sum + l_k, _jax.tree.map(_jnp.add, grad_sum, gw_k)), gx_k

        init = (_jnp.zeros((), _jnp.float32), _jax.tree.map(_jnp.zeros_like, weights))
        (loss, grad_w), grad_x = _jax.lax.scan(body, init, (per_example, given["loss_target"]))
    with _jax.named_scope("update"):
        delta_w, new_m, new_v = {}, {}, {}
        for n in TWIN_WEIGHTS:
            delta_w[n], new_m[n], new_v[n] = _adamw(weights[n], grad_w[n], given["m_" + n], given["v_" + n])
    return (loss, grad_x, *[grad_w[n] for n in TWIN_WEIGHTS], *[delta_w[n] for n in TWIN_WEIGHTS],
            *[new_m[n] for n in TWIN_WEIGHTS], *[new_v[n] for n in TWIN_WEIGHTS])
```

```python
import functools
import math

import jax
import jax.numpy as jnp
from jax import lax
from jax.experimental import pallas as pl
from jax.experimental.pallas import tpu as pltpu

F32 = jnp.float32
BF16 = jnp.bfloat16
MESH = pl.DeviceIdType.MESH
HIGHEST = lax.Precision.HIGHEST

D_MODEL = 1024
N_HEADS_A = 8
HEAD_DIM_A = 128
CONV_A = 4
N_HEADS_B = 16
N_KV_B = 4
HEAD_DIM_B = 64
WINDOW = 128
D_FF = 2816
FFN_CONV = 3
PLE_DIM = 256
EPS = 1e-6
DEPTH = 2

ADAM_LR = 0.001
ADAM_B1 = 0.9
ADAM_B2 = 0.999
ADAM_EPS = 1e-08
ADAM_WD = 0.01
ADAM_STEP = 10

LANES = 128
SUBLANES = 8
BF16_ROWS = 16
CHUNK = 128
VMEM_LIMIT = 56 * 1024 * 1024
NEG = -1e30
N_CHIPS = 4
N_DEV = 8
PACK_COLS = 1024


def _params(sem=None):
    return pltpu.CompilerParams(dimension_semantics=sem, vmem_limit_bytes=VMEM_LIMIT)


def _tile(dim, cap):
    if dim % LANES:
        return dim
    best = LANES
    for t in range(LANES, min(dim, cap) + 1, LANES):
        if dim % t == 0:
            best = t
    return best


def _dot(a, b, dims=(((1,), (0,)), ((), ())), precision=None):
    return lax.dot_general(a, b, dims, precision=precision, preferred_element_type=F32)


NT = (((1,), (1,)), ((), ()))
TN = (((0,), (0,)), ((), ()))


def mm(a, b, *, name, ta=False, tb=False, out_dtype=F32, add=None, tm_cap=512, tn_cap=1408, tk_cap=1408):
    m, k = (a.shape[1], a.shape[0]) if ta else a.shape
    n = b.shape[0] if tb else b.shape[1]
    assert (b.shape[1] if tb else b.shape[0]) == k, (a.shape, b.shape, ta, tb)
    tm, tn, tk = _tile(m, tm_cap), _tile(n, tn_cap), _tile(k, tk_cap)
    nk = k // tk
    dims = (((0 if ta else 1,), (1 if tb else 0,)), ((), ()))
    has_add = add is not None

    def body(*refs):
        a_ref, b_ref = refs[0], refs[1]
        add_ref = refs[2] if has_add else None
        o_ref = refs[3] if has_add else refs[2]
        part = _dot(a_ref[...].astype(BF16), b_ref[...].astype(BF16), dims)
        if nk == 1:
            if has_add:
                part = part + add_ref[...].astype(F32)
            o_ref[...] = part.astype(o_ref.dtype)
            return
        acc = refs[-1]
        kk = pl.program_id(2)

        @pl.when(kk == 0)
        def _():
            acc[...] = part

        @pl.when(kk > 0)
        def _():
            acc[...] += part

        @pl.when(kk == nk - 1)
        def _():
            r = acc[...]
            if has_add:
                r = r + add_ref[...].astype(F32)
            o_ref[...] = r.astype(o_ref.dtype)

    a_spec = pl.BlockSpec((tk, tm), lambda i, j, kk: (kk, i)) if ta else pl.BlockSpec((tm, tk), lambda i, j, kk: (i, kk))
    b_spec = pl.BlockSpec((tn, tk), lambda i, j, kk: (j, kk)) if tb else pl.BlockSpec((tk, tn), lambda i, j, kk: (kk, j))
    o_spec = pl.BlockSpec((tm, tn), lambda i, j, kk: (i, j))
    in_specs = [a_spec, b_spec] + ([o_spec] if has_add else [])
    args = (a, b) + ((add,) if has_add else ())
    return pl.pallas_call(
        body, grid=(m // tm, n // tn, nk), in_specs=in_specs, out_specs=o_spec,
        out_shape=jax.ShapeDtypeStruct((m, n), out_dtype), name=name,
        scratch_shapes=[pltpu.VMEM((tm, tn), F32)] if nk > 1 else [],
        compiler_params=_params(("parallel", "parallel", "arbitrary")),
    )(*args)


def _row_spec(tm, cw, coff):
    return pl.BlockSpec((tm, cw), lambda i, j: (i, j + coff))


def _full_spec(shape):
    return pl.BlockSpec(shape, lambda i, j: (0,) * len(shape))


def tile_map(fn, rows, params, outs, *, tm, ncol, name):
    t = rows[0][0].shape[0]
    nin = len(rows) + len(params)

    def body(*refs):
        res = fn(*[r[...] for r in refs[:nin]])
        res = res if isinstance(res, (tuple, list)) else (res,)
        for o_ref, r in zip(refs[nin:], res):
            o_ref[...] = r.astype(o_ref.dtype)

    in_specs = [_row_spec(tm, cw, coff) for (_, cw, coff) in rows] + [_full_spec(p.shape) for p in params]
    res = pl.pallas_call(
        body, grid=(t // tm, ncol), in_specs=in_specs,
        out_specs=[_row_spec(tm, cw, 0) for (cw, _) in outs],
        out_shape=[jax.ShapeDtypeStruct((t, cw * ncol), dt) for (cw, dt) in outs], name=name,
        compiler_params=_params(("parallel", "parallel")),
    )(*[r[0] for r in rows], *params)
    return res


def tile_vjp(fn, rows, params, cts, *, n_diff, tm, ncol, name, add=None):
    t = rows[0][0].shape[0]
    nr, npar, nct = len(rows), len(params), len(cts)
    has_add = add is not None

    def body(*refs):
        vals = [r[...] for r in refs[:nr + npar + nct + (1 if has_add else 0)]]
        diff, rest, pars = vals[:n_diff], vals[n_diff:nr], vals[nr:nr + npar]
        ctv = vals[nr + npar:nr + npar + nct]
        outs_ref = refs[nr + npar + nct + (1 if has_add else 0):]

        def f(*a):
            res = fn(*a[:n_diff], *rest, *a[n_diff:])
            return tuple(res) if isinstance(res, (tuple, list)) else (res,)

        primal, vjp = jax.vjp(f, *[d.astype(F32) for d in diff], *pars)
        grads = vjp(tuple(c.astype(o.dtype) for c, o in zip(ctv, primal)))
        for q in range(n_diff):
            g = grads[q]
            if has_add and q == 0:
                g = g + vals[-1]
            outs_ref[q][...] = g.astype(outs_ref[q].dtype)
        first = (pl.program_id(0) == 0) & (pl.program_id(1) == 0)
        for q in range(npar):
            o_ref, g = outs_ref[n_diff + q], grads[n_diff + q]

            @pl.when(first)
            def _(o_ref=o_ref, g=g):
                o_ref[...] = g

            @pl.when(jnp.logical_not(first))
            def _(o_ref=o_ref, g=g):
                o_ref[...] += g

    ins = list(rows) + [None] * 0
    in_specs = [_row_spec(tm, cw, coff) for (_, cw, coff) in rows] + [_full_spec(p.shape) for p in params]
    in_specs += [_row_spec(tm, cw, coff) for (_, cw, coff) in cts]
    args = [r[0] for r in rows] + list(params) + [c[0] for c in cts]
    if has_add:
        in_specs.append(_row_spec(tm, add[1], add[2]))
        args.append(add[0])
    out_specs = [_row_spec(tm, rows[q][1], 0) for q in range(n_diff)] + [_full_spec(p.shape) for p in params]
    out_shape = [jax.ShapeDtypeStruct((t, rows[q][1] * ncol), F32) for q in range(n_diff)]
    out_shape += [jax.ShapeDtypeStruct(p.shape, F32) for p in params]
    del ins
    res = pl.pallas_call(
        body, grid=(t // tm, ncol), in_specs=in_specs, out_specs=out_specs, out_shape=out_shape, name=name,
        compiler_params=_params(("arbitrary", "arbitrary")),
    )(*args)
    return res[:n_diff], res[n_diff:]


def _silu(x):
    return x * jax.nn.sigmoid(x)


def _f_norm(h, w):
    return h * lax.rsqrt(jnp.mean(h * h, axis=-1, keepdims=True) + EPS) * w


def _f_gnorm(o, z, w):
    return _f_norm(o, w) * _silu(z)


def _f_act(gate, val):
    return _silu(gate) * val


def _f_ple(gl, pe):
    return jax.nn.sigmoid(gl) * pe


def _f_betag(pt, alog, dtb):
    lane = lax.broadcasted_iota(jnp.int32, (1, LANES), 1)
    z = pt + dtb
    softplus = jnp.maximum(z, 0.0) + jnp.log(1.0 + jnp.exp(-jnp.abs(z)))
    g = -jnp.exp(alog) * softplus
    return jnp.where(lane < N_HEADS_A, jax.nn.sigmoid(pt), jnp.where(lane < 2 * N_HEADS_A, g, 0.0))


CONV_TM = 256
CONV_CW = 512


def _shift_down(x, prev, s, row):
    rp = jnp.tile(pltpu.roll(prev, s, 0), (x.shape[0] // SUBLANES, 1))
    return jnp.where(row < s, rp, pltpu.roll(x, s, 0))


def _shift_up(x, nxt, s, row):
    tm = x.shape[0]
    rn = jnp.tile(pltpu.roll(nxt, SUBLANES - s, 0), (tm // SUBLANES, 1))
    return jnp.where(row >= tm - s, rn, pltpu.roll(x, tm - s, 0))


def conv_fwd(x, w, *, name):
    t = x.shape[0]
    k, c = w.shape
    tm, cw = min(CONV_TM, t), CONV_CW
    nb8 = tm // SUBLANES

    def body(x_ref, p_ref, w_ref, o_ref):
        i = pl.program_id(1)
        xv = x_ref[...]
        prev = jnp.where(i > 0, p_ref[...], 0.0)
        row = lax.broadcasted_iota(jnp.int32, xv.shape, 0)
        y = xv * w_ref[pl.ds(k - 1, 1), :]
        for s in range(1, k):
            y = y + _shift_down(xv, prev, s, row) * w_ref[pl.ds(k - 1 - s, 1), :]
        o_ref[...] = y

    return pl.pallas_call(
        body, grid=(c // cw, t // tm),
        in_specs=[pl.BlockSpec((tm, cw), lambda j, i: (i, j)),
                  pl.BlockSpec((SUBLANES, cw), lambda j, i: (jnp.maximum(i * nb8 - 1, 0), j)),
                  pl.BlockSpec((k, cw), lambda j, i: (0, j))],
        out_specs=pl.BlockSpec((tm, cw), lambda j, i: (i, j)),
        out_shape=jax.ShapeDtypeStruct((t, c), F32), name=name,
        compiler_params=_params(("parallel", "parallel")),
    )(x, x, w)


def conv_bwd(dy, x, w, *, name):
    t = x.shape[0]
    k, c = w.shape
    tm, cw = min(CONV_TM, t), CONV_CW
    nb8 = tm // SUBLANES
    ni = t // tm

    def body(dy_ref, dn_ref, x_ref, p_ref, w_ref, dx_ref, dw_ref):
        i = pl.program_id(1)
        dyv, xv = dy_ref[...], x_ref[...]
        nxt = jnp.where(i < ni - 1, dn_ref[...], 0.0)
        prev = jnp.where(i > 0, p_ref[...], 0.0)
        row = lax.broadcasted_iota(jnp.int32, xv.shape, 0)
        dx = dyv * w_ref[pl.ds(k - 1, 1), :]
        dws = [jnp.sum(dyv * xv, axis=0, keepdims=True)]
        for s in range(1, k):
            dx = dx + _shift_up(dyv, nxt, s, row) * w_ref[pl.ds(k - 1 - s, 1), :]
            dws.append(jnp.sum(dyv * _shift_down(xv, prev, s, row), axis=0, keepdims=True))
        dx_ref[...] = dx
        for s in range(k):
            @pl.when(i == 0)
            def _(s=s):
                dw_ref[pl.ds(k - 1 - s, 1), :] = dws[s]

            @pl.when(i > 0)
            def _(s=s):
                dw_ref[pl.ds(k - 1 - s, 1), :] += dws[s]

    return pl.pallas_call(
        body, grid=(c // cw, ni),
        in_specs=[pl.BlockSpec((tm, cw), lambda j, i: (i, j)),
                  pl.BlockSpec((SUBLANES, cw), lambda j, i: (jnp.minimum((i + 1) * nb8, t // SUBLANES - 1), j)),
                  pl.BlockSpec((tm, cw), lambda j, i: (i, j)),
                  pl.BlockSpec((SUBLANES, cw), lambda j, i: (jnp.maximum(i * nb8 - 1, 0), j)),
                  pl.BlockSpec((k, cw), lambda j, i: (0, j))],
        out_specs=[pl.BlockSpec((tm, cw), lambda j, i: (i, j)), pl.BlockSpec((k, cw), lambda j, i: (0, j))],
        out_shape=[jax.ShapeDtypeStruct((t, c), F32), jax.ShapeDtypeStruct((k, c), F32)], name=name,
        compiler_params=_params(("parallel", "arbitrary")),
    )(dy, dy, x, x, w)


@jax.custom_vjp
def _inv_unit_lower(lm):
    return _inv_series(lm)


def _inv_series(lm):
    c = lm.shape[0]
    ri = lax.broadcasted_iota(jnp.int32, (c, c), 0)
    ci = lax.broadcasted_iota(jnp.int32, (c, c), 1)
    xm = (ri == ci).astype(F32) - lm
    pw = lm
    for _ in range(int(math.log2(c)) - 1):
        pw = _dot(pw, pw, precision=HIGHEST)
        xm = xm + _dot(xm, pw, precision=HIGHEST)
    return xm


def _inv_fwd(lm):
    tm = _inv_series(lm)
    return tm, tm


def _inv_bwd(tm, dt):
    a = _dot(tm, dt, TN, precision=HIGHEST)
    return (-_dot(a, tm, NT, precision=HIGHEST),)


_inv_unit_lower.defvjp(_inv_fwd, _inv_bwd)


def _l2n(x):
    return x * lax.rsqrt(jnp.sum(x * x, axis=-1, keepdims=True) + EPS)


def _prep_fn(cq, ck, cv, bg, hb, hg):
    c = cq.shape[0]
    q, k, v = _l2n(_silu(cq)), _l2n(_silu(ck)), _silu(cv)
    beta = jnp.sum(bg * hb, axis=1, keepdims=True)
    g = jnp.sum(bg * hg, axis=1, keepdims=True)
    ri = lax.broadcasted_iota(jnp.int32, (c, c), 0)
    ci = lax.broadcasted_iota(jnp.int32, (c, c), 1)
    eye = (ri == ci).astype(F32)
    incl, strict = ci <= ri, ci < ri
    g_row = jnp.sum(g * eye, axis=0, keepdims=True)
    gcs = jnp.sum(jnp.where(incl, g_row, 0.0), axis=1, keepdims=True)
    gcs_row = jnp.sum(gcs * eye, axis=0, keepdims=True)
    beta_row = jnp.sum(beta * eye, axis=0, keepdims=True)
    last = lax.broadcasted_iota(jnp.int32, (c, 1), 0) == c - 1
    gtot = jnp.sum(jnp.where(last, gcs, 0.0), axis=0, keepdims=True)
    decay = jnp.exp(jnp.where(incl, gcs - gcs_row, NEG))
    kb = k.astype(BF16)
    lm = jnp.where(strict, beta * _dot(kb, kb, NT) * decay, 0.0)
    am = (_inv_unit_lower(lm) * beta_row).astype(BF16)
    gam = jnp.exp(gcs)
    u0 = _dot(am, v.astype(BF16))
    wk = _dot(am, (gam * k).astype(BF16))
    qs = q * (HEAD_DIM_A ** -0.5)
    qk = _dot(qs.astype(BF16), kb, NT) * decay
    gl = jnp.exp(gtot) * jnp.ones((SUBLANES, LANES), F32)
    return u0, wk, qs * gam, k * jnp.exp(gtot - gcs), qk, gl


def _head_masks(h):
    lane = lax.broadcasted_iota(jnp.int32, (1, LANES), 1)
    return (lane == h).astype(F32), (lane == h + N_HEADS_A).astype(F32)


def delta_prep(cqkv, bg, *, name):
    t = cqkv.shape[0]
    nh, hd, n = N_HEADS_A, HEAD_DIM_A, t // CHUNK

    def body(cq_ref, ck_ref, cv_ref, bg_ref, u0_ref, wk_ref, qd_ref, kd_ref, qk_ref, gl_ref):
        hb, hg = _head_masks(pl.program_id(1))
        res = _prep_fn(cq_ref[...], ck_ref[...], cv_ref[...], bg_ref[...], hb, hg)
        for o_ref, r in zip((u0_ref, wk_ref, qd_ref, kd_ref, qk_ref, gl_ref), res):
            o_ref[...] = r

    blk = lambda off: pl.BlockSpec((CHUNK, hd), lambda i, h: (i, h + off))
    return pl.pallas_call(
        body, grid=(n, nh),
        in_specs=[blk(0), blk(nh), blk(2 * nh), pl.BlockSpec((CHUNK, LANES), lambda i, h: (i, 0))],
        out_specs=[blk(0)] * 5 + [pl.BlockSpec((SUBLANES, LANES), lambda i, h: (i * nh + h, 0))],
        out_shape=[jax.ShapeDtypeStruct((t, nh * hd), F32)] * 5 + [jax.ShapeDtypeStruct((n * nh * SUBLANES, LANES), F32)],
        name=name, compiler_params=_params(("parallel", "parallel")),
    )(cqkv, cqkv, cqkv, bg)


def delta_prep_bwd(cqkv, bg, cts, *, name):
    t = cqkv.shape[0]
    nh, hd, n = N_HEADS_A, HEAD_DIM_A, t // CHUNK

    def body(cq_ref, ck_ref, cv_ref, bg_ref, c0, c1, c2, c3, c4, c5, dq_ref, dk_ref, dv_ref, dbg_ref):
        h = pl.program_id(1)
        hb, hg = _head_masks(h)
        _, vjp = jax.vjp(lambda a, b, c, d: _prep_fn(a, b, c, d, hb, hg), cq_ref[...], ck_ref[...], cv_ref[...], bg_ref[...])
        dq, dk, dv, dbg = vjp((c0[...], c1[...], c2[...], c3[...], c4[...], c5[...]))
        dq_ref[...] = dq
        dk_ref[...] = dk
        dv_ref[...] = dv

        @pl.when(h == 0)
        def _():
            dbg_ref[...] = dbg

        @pl.when(h > 0)
        def _():
            dbg_ref[...] += dbg

    blk = lambda off: pl.BlockSpec((CHUNK, hd), lambda i, h: (i, h + off))
    gl_spec = pl.BlockSpec((SUBLANES, LANES), lambda i, h: (i * nh + h, 0))
    bg_spec = pl.BlockSpec((CHUNK, LANES), lambda i, h: (i, 0))
    dq, dk, dv, dbg = pl.pallas_call(
        body, grid=(n, nh),
        in_specs=[blk(0), blk(nh), blk(2 * nh), bg_spec] + [blk(0)] * 5 + [gl_spec],
        out_specs=[blk(0)] * 3 + [bg_spec],
        out_shape=[jax.ShapeDtypeStruct((t, nh * hd), F32)] * 3 + [jax.ShapeDtypeStruct((t, LANES), F32)],
        name=name, compiler_params=_params(("parallel", "arbitrary")),
    )(cqkv, cqkv, cqkv, bg, *cts)
    return dq, dk, dv, dbg


def delta_scan(u0, wk, qd, kd, qk, gl, *, name):
    t = u0.shape[0]
    nh, hd, n = N_HEADS_A, HEAD_DIM_A, t // CHUNK

    def body(u0_ref, wk_ref, qd_ref, kd_ref, qk_ref, gl_ref, o_ref, sin_ref, s_ref):
        @pl.when(pl.program_id(1) == 0)
        def _():
            s_ref[...] = jnp.zeros_like(s_ref)

        s = s_ref[...]
        sin_ref[...] = s
        sb = s.astype(BF16)
        u = u0_ref[...] - _dot(wk_ref[...].astype(BF16), sb)
        ub = u.astype(BF16)
        o_ref[...] = _dot(qd_ref[...].astype(BF16), sb) + _dot(qk_ref[...].astype(BF16), ub)
        s_ref[...] = gl_ref[pl.ds(0, 1), :] * s + _dot(kd_ref[...].astype(BF16), ub, TN)

    blk = pl.BlockSpec((CHUNK, hd), lambda h, i: (i, h))
    return pl.pallas_call(
        body, grid=(nh, n),
        in_specs=[blk] * 5 + [pl.BlockSpec((SUBLANES, LANES), lambda h, i: (i * nh + h, 0))],
        out_specs=[blk, pl.BlockSpec((None, hd, hd), lambda h, i: (h * n + i, 0, 0))],
        out_shape=[jax.ShapeDtypeStruct((t, nh * hd), F32), jax.ShapeDtypeStruct((nh * n, hd, hd), F32)],
        scratch_shapes=[pltpu.VMEM((hd, hd), F32)], name=name,
        compiler_params=_params(("parallel", "arbitrary")),
    )(u0, wk, qd, kd, qk, gl)


def delta_scan_bwd(do, u0, wk, qd, kd, qk, gl, s_in, *, name):
    t = u0.shape[0]
    nh, hd, n = N_HEADS_A, HEAD_DIM_A, t // CHUNK

    def body(do_ref, u0_ref, wk_ref, qd_ref, kd_ref, qk_ref, gl_ref, sin_ref,
             du0_ref, dwk_ref, dqd_ref, dkd_ref, dqk_ref, dgl_ref, ds_ref):
        @pl.when(pl.program_id(1) == 0)
        def _():
            ds_ref[...] = jnp.zeros_like(ds_ref)

        s, dsn = sin_ref[...], ds_ref[...]
        sb, dsb = s.astype(BF16), dsn.astype(BF16)
        dob = do_ref[...].astype(BF16)
        wkb, qdb, kdb, qkb = (r[...].astype(BF16) for r in (wk_ref, qd_ref, kd_ref, qk_ref))
        u = u0_ref[...] - _dot(wkb, sb)
        ub = u.astype(BF16)
        du = _dot(qkb, dob, TN) + _dot(kdb, dsb)
        dub = du.astype(BF16)
        du0_ref[...] = du
        dwk_ref[...] = -_dot(dub, sb, NT)
        dqd_ref[...] = _dot(dob, sb, NT)
        dkd_ref[...] = _dot(ub, dsb, NT)
        dqk_ref[...] = _dot(dob, ub, NT)
        dgl = jnp.sum(jnp.sum(s * dsn, axis=1, keepdims=True), axis=0, keepdims=True)
        corner = (lax.broadcasted_iota(jnp.int32, (SUBLANES, LANES), 0) == 0) & (lax.broadcasted_iota(jnp.int32, (SUBLANES, LANES), 1) == 0)
        dgl_ref[...] = jnp.where(corner, dgl, 0.0)
        ds_ref[...] = gl_ref[pl.ds(0, 1), :] * dsn + _dot(qdb, dob, TN) - _dot(wkb, dub, TN)

    blk = pl.BlockSpec((CHUNK, hd), lambda h, i: (n - 1 - i, h))
    gl_spec = pl.BlockSpec((SUBLANES, LANES), lambda h, i: ((n - 1 - i) * nh + h, 0))
    return pl.pallas_call(
        body, grid=(nh, n),
        in_specs=[blk] * 6 + [gl_spec, pl.BlockSpec((None, hd, hd), lambda h, i: (h * n + n - 1 - i, 0, 0))],
        out_specs=[blk] * 5 + [gl_spec],
        out_shape=[jax.ShapeDtypeStruct((t, nh * hd), F32)] * 5 + [jax.ShapeDtypeStruct((n * nh * SUBLANES, LANES), F32)],
        scratch_shapes=[pltpu.VMEM((hd, hd), F32)], name=name,
        compiler_params=_params(("parallel", "arbitrary")),
    )(do, u0, wk, qd, kd, qk, gl, s_in)


N_PAIRS = N_HEADS_B // 2


def _att_fn(qp, kc, kp, vc, vp, sinks, prf, first):
    w = WINDOW
    lane = lax.broadcasted_iota(jnp.int32, (1, LANES), 1)
    lo = (lane < HEAD_DIM_B).astype(F32)
    qi = lax.broadcasted_iota(jnp.int32, (w, w), 0)
    kj = lax.broadcasted_iota(jnp.int32, (w, w), 1)
    dist_c = (qi - kj).astype(F32)
    valid_c = kj <= qi
    valid_p = (kj > qi) & (first < 0.5)
    kcb, kpb, vcb, vpb = (a.astype(BF16) for a in (kc, kp, vc, vp))
    out = jnp.zeros((w, LANES), F32)
    for half, hmask in ((0, lo), (1, 1.0 - lo)):
        hd = 2.0 * prf + half
        slope = jnp.exp(-(hd + 1.0) * (8.0 / N_HEADS_B * math.log(2.0)))
        sink = jnp.sum(sinks * (lane.astype(F32) == hd).astype(F32), axis=1, keepdims=True)
        qh = (qp * hmask).astype(BF16)
        lc = jnp.where(valid_c, _dot(qh, kcb, NT) * (HEAD_DIM_B ** -0.5) - slope * dist_c, NEG)
        lp = jnp.where(valid_p, _dot(qh, kpb, NT) * (HEAD_DIM_B ** -0.5) - slope * (dist_c + w), NEG)
        m = jnp.maximum(jnp.maximum(jnp.max(lc, axis=1, keepdims=True), jnp.max(lp, axis=1, keepdims=True)), sink)
        m = lax.stop_gradient(m)
        ec, ep = jnp.exp(lc - m), jnp.exp(lp - m)
        inv = 1.0 / (jnp.sum(ec, axis=1, keepdims=True) + jnp.sum(ep, axis=1, keepdims=True) + jnp.exp(sink - m))
        oh = _dot((ec * inv).astype(BF16), vcb) + _dot((ep * inv).astype(BF16), vpb)
        out = out + oh * hmask
    return out


def _scalar11(v):
    return jnp.full((1, 1), v, F32)


def swa_fwd(qsrc, kd, vd, sinks, *, name):
    t = kd.shape[0]
    nb = t // WINDOW

    def body(q_ref, kc_ref, kp_ref, vc_ref, vp_ref, s_ref, o_ref):
        first = _scalar11((pl.program_id(0) == 0).astype(F32))
        prf = _scalar11(pl.program_id(1).astype(F32))
        o_ref[...] = _att_fn(q_ref[...], kc_ref[...], kp_ref[...], vc_ref[...], vp_ref[...], s_ref[...], prf, first).astype(o_ref.dtype)

    cur = pl.BlockSpec((WINDOW, LANES), lambda i, pr: (i, pr // 2))
    prev = pl.BlockSpec((WINDOW, LANES), lambda i, pr: (jnp.maximum(i - 1, 0), pr // 2))
    qs = pl.BlockSpec((WINDOW, LANES), lambda i, pr: (i, pr))
    return pl.pallas_call(
        body, grid=(nb, N_PAIRS),
        in_specs=[qs, cur, prev, cur, prev, pl.BlockSpec((1, LANES), lambda i, pr: (0, 0))],
        out_specs=qs, out_shape=jax.ShapeDtypeStruct((t, N_PAIRS * LANES), BF16), name=name,
        compiler_params=_params(("parallel", "parallel")),
    )(qsrc, kd, kd, vd, vd, sinks)


def swa_bwd(do, qsrc, kd, vd, sinks, *, name):
    t = kd.shape[0]
    nb = t // WINDOW

    def body(do_ref, q_ref, kc_ref, kp_ref, vc_ref, vp_ref, s_ref, dq_ref, dk_ref, dv_ref, ds_ref,
             carry_k, carry_v, acc):
        step, pr = pl.program_id(0), pl.program_id(1)
        first = _scalar11((step == nb - 1).astype(F32))
        prf = _scalar11(pr.astype(F32))

        @pl.when((step == 0) & (pr == 0))
        def _():
            carry_k[...] = jnp.zeros_like(carry_k)
            carry_v[...] = jnp.zeros_like(carry_v)
            ds_ref[...] = jnp.zeros_like(ds_ref)

        _, vjp = jax.vjp(lambda *a: _att_fn(*a, prf, first), q_ref[...], kc_ref[...], kp_ref[...], vc_ref[...], vp_ref[...], s_ref[...])
        dq, dkc, dkp, dvc, dvp, dsk = vjp(do_ref[...].astype(F32))
        dq_ref[...] = dq
        ds_ref[...] += dsk
        fold = lambda g: g + pltpu.roll(g, HEAD_DIM_B, 1)
        hh = pr // 2

        @pl.when(pr % 2 == 0)
        def _():
            acc[0], acc[1], acc[2], acc[3] = dkc, dkp, dvc, dvp

        @pl.when(pr % 2 == 1)
        def _():
            dk_ref[...] = fold(acc[0] + dkc + carry_k[hh])
            dv_ref[...] = fold(acc[2] + dvc + carry_v[hh])
            carry_k[hh] = acc[1] + dkp
            carry_v[hh] = acc[3] + dvp

    rev = lambda i: nb - 1 - i
    cur = pl.BlockSpec((WINDOW, LANES), lambda i, pr: (rev(i), pr // 2))
    prev = pl.BlockSpec((WINDOW, LANES), lambda i, pr: (jnp.maximum(rev(i) - 1, 0), pr // 2))
    qs = pl.BlockSpec((WINDOW, LANES), lambda i, pr: (rev(i), pr))
    sk = pl.BlockSpec((1, LANES), lambda i, pr: (0, 0))
    return pl.pallas_call(
        body, grid=(nb, N_PAIRS),
        in_specs=[qs, qs, cur, prev, cur, prev, sk],
        out_specs=[qs, cur, cur, sk],
        out_shape=[jax.ShapeDtypeStruct((t, N_PAIRS * LANES), F32), jax.ShapeDtypeStruct((t, N_KV_B * LANES), F32),
                   jax.ShapeDtypeStruct((t, N_KV_B * LANES), F32), jax.ShapeDtypeStruct((1, LANES), F32)],
        scratch_shapes=[pltpu.VMEM((N_KV_B, WINDOW, LANES), F32), pltpu.VMEM((N_KV_B, WINDOW, LANES), F32),
                        pltpu.VMEM((4, WINDOW, LANES), F32)],
        name=name, compiler_params=_params(("arbitrary", "arbitrary")),
    )(do, qsrc, kd, kd, vd, vd, sinks)


def loss_head(h, tgt, w, *, name):
    t, d = h.shape
    tm = min(256, t)

    def body(h_ref, t_ref, w_ref, dh_ref, dw_ref, l_ref):
        tg = t_ref[...]

        def f(hv, wv):
            err = _f_norm(hv, wv) - tg
            return 0.5 * jnp.sum(jnp.sum(err * err, axis=1, keepdims=True), axis=0, keepdims=True) * (1.0 / d)

        lv, vjp = jax.vjp(f, h_ref[...], w_ref[...])
        dh, dw = vjp(jnp.ones((1, 1), F32))
        dh_ref[...] = dh
        first = pl.program_id(0) == 0

        @pl.when(first)
        def _():
            dw_ref[...] = dw
            l_ref[...] = lv * jnp.ones((1, LANES), F32)

        @pl.when(jnp.logical_not(first))
        def _():
            dw_ref[...] += dw
            l_ref[...] += lv * jnp.ones((1, LANES), F32)

    rows = pl.BlockSpec((tm, d), lambda i: (i, 0))
    one = lambda c: pl.BlockSpec((1, c), lambda i: (0, 0))
    return pl.pallas_call(
        body, grid=(t // tm,), in_specs=[rows, rows, one(d)], out_specs=[rows, one(d), one(LANES)],
        out_shape=[jax.ShapeDtypeStruct((t, d), F32), jax.ShapeDtypeStruct((1, d), F32), jax.ShapeDtypeStruct((1, LANES), F32)],
        name=name, compiler_params=_params(("arbitrary",)),
    )(h, tgt, w)


def adamw(w, g, m, v, *, name):
    r, c = w.shape
    tr = r
    if r % SUBLANES == 0:
        for cand in range(SUBLANES, min(r, 256) + 1, SUBLANES):
            if r % cand == 0:
                tr = cand

    def body(w_ref, g_ref, m_ref, v_ref, d_ref, mo_ref, vo_ref):
        gv = g_ref[...]
        mn = ADAM_B1 * m_ref[...] + (1.0 - ADAM_B1) * gv
        vn = ADAM_B2 * v_ref[...] + (1.0 - ADAM_B2) * jnp.square(gv)
        m_hat = mn / (1.0 - ADAM_B1 ** ADAM_STEP)
        v_hat = vn / (1.0 - ADAM_B2 ** ADAM_STEP)
        d_ref[...] = -ADAM_LR * (m_hat / (jnp.sqrt(v_hat) + ADAM_EPS) + ADAM_WD * w_ref[...])
        mo_ref[...] = mn
        vo_ref[...] = vn

    spec = pl.BlockSpec((tr, c), lambda i: (i, 0))
    return pl.pallas_call(
        body, grid=(r // tr,), in_specs=[spec] * 4, out_specs=[spec] * 3,
        out_shape=[jax.ShapeDtypeStruct((r, c), F32)] * 3, name=name, compiler_params=_params(("parallel",)),
    )(w, g, m, v)


def _place():
    return lax.axis_index("x"), lax.axis_index("y"), lax.axis_index("c")


def allgather8(blk, *, name):
    def body(x_ref, out_ref, send_sems, recv_sems, local_sem):
        x, y, c = _place()
        me = 4 * x + 2 * y + c
        mine = pltpu.make_async_copy(x_ref, out_ref.at[me], local_sem)
        mine.start()
        sent = []
        for k in range(1, N_DEV):
            to = (x ^ ((k >> 2) & 1), y ^ ((k >> 1) & 1), c ^ (k & 1))
            cp = pltpu.make_async_remote_copy(src_ref=x_ref, dst_ref=out_ref.at[me], send_sem=send_sems.at[k - 1],
                                              recv_sem=recv_sems.at[k - 1], device_id=to, device_id_type=MESH)
            cp.start()
            sent.append(cp)
        for k in range(1, N_DEV):
            frm = me ^ k
            pltpu.make_async_remote_copy(src_ref=x_ref, dst_ref=out_ref.at[frm], send_sem=send_sems.at[k - 1],
                                         recv_sem=recv_sems.at[k - 1], device_id=(x, y, c), device_id_type=MESH).wait_recv()
        for cp in sent:
            cp.wait_send()
        mine.wait()

    vm = pl.BlockSpec(memory_space=pltpu.VMEM)
    return pl.pallas_call(
        body, in_specs=[vm], out_specs=vm, out_shape=jax.ShapeDtypeStruct((N_DEV,) + blk.shape, blk.dtype), name=name,
        scratch_shapes=[pltpu.SemaphoreType.DMA((N_DEV - 1,)), pltpu.SemaphoreType.DMA((N_DEV - 1,)), pltpu.SemaphoreType.DMA],
    )(blk)


def _other_chips(x, y):
    return [(1 - x, y), (x, 1 - y), (1 - x, 1 - y)]


def allgather_shards(packed, *, name):
    r = packed.shape[0]
    rh = r // 2

    def body(in_ref, out_ref, send_sems, recv_sems, local_sem):
        x, y, c = _place()
        me_chip = 2 * x + y
        sib = (x, y, 1 - c)
        chips = _other_chips(x, y)

        def rows(chip, half):
            return out_ref.at[chip, pl.ds(pl.multiple_of(half * rh, BF16_ROWS), rh), :]

        def copy(k, src, dst, to):
            return pltpu.make_async_remote_copy(src_ref=src, dst_ref=dst, send_sem=send_sems.at[k], recv_sem=recv_sems.at[k],
                                                device_id=to, device_id_type=MESH)

        mine = pltpu.make_async_copy(in_ref, out_ref.at[me_chip], local_sem)
        mine.start()
        my_half = in_ref.at[pl.ds(pl.multiple_of(c * rh, BF16_ROWS), rh), :]
        first = [copy(j, my_half, rows(me_chip, c), (cx, cy, c)) for j, (cx, cy) in enumerate(chips)]
        for cp in first:
            cp.start()
        passed = [copy(3 + j, rows(2 * cx + cy, c), rows(2 * cx + cy, c), sib) for j, (cx, cy) in enumerate(chips)]
        for j, (cx, cy) in enumerate(chips):
            copy(j, my_half, rows(2 * cx + cy, c), sib).wait_recv()
            passed[j].start()
        for j, (cx, cy) in enumerate(chips):
            copy(3 + j, my_half, rows(2 * cx + cy, 1 - c), sib).wait_recv()
        for cp in first + passed:
            cp.wait_send()
        mine.wait()

    hbm = pl.BlockSpec(memory_space=pl.ANY)
    return pl.pallas_call(
        body, in_specs=[hbm], out_specs=hbm, out_shape=jax.ShapeDtypeStruct((N_CHIPS, r, PACK_COLS), packed.dtype), name=name,
        scratch_shapes=[pltpu.SemaphoreType.DMA((6,)), pltpu.SemaphoreType.DMA((6,)), pltpu.SemaphoreType.DMA],
    )(packed)


def swap_halves(g, *, name):
    r = g.shape[1]
    rh = r // 2

    def body(g_ref, out_ref, send_sem, recv_sem):
        x, y, c = _place()
        cp = pltpu.make_async_remote_copy(src_ref=g_ref.at[:, pl.ds(pl.multiple_of((1 - c) * rh, BF16_ROWS), rh), :], dst_ref=out_ref,
                                          send_sem=send_sem, recv_sem=recv_sem, device_id=(x, y, 1 - c), device_id_type=MESH)
        cp.start()
        cp.wait()

    hbm = pl.BlockSpec(memory_space=pl.ANY)
    return pl.pallas_call(
        body, in_specs=[hbm], out_specs=hbm, out_shape=jax.ShapeDtypeStruct((N_CHIPS, rh, PACK_COLS), g.dtype), name=name,
        scratch_shapes=[pltpu.SemaphoreType.DMA, pltpu.SemaphoreType.DMA],
    )(g)


def scatter_chips(hp, *, name):
    rh = hp.shape[1]

    def body(h_ref, out_ref, send_sems, recv_sems):
        x, y, c = _place()
        chips = _other_chips(x, y)
        cps = [pltpu.make_async_remote_copy(src_ref=h_ref.at[2 * cx + cy], dst_ref=out_ref.at[j], send_sem=send_sems.at[j],
                                            recv_sem=recv_sems.at[j], device_id=(cx, cy, c), device_id_type=MESH)
               for j, (cx, cy) in enumerate(chips)]
        for cp in cps:
            cp.start()
        for cp in cps:
            cp.wait()

    hbm = pl.BlockSpec(memory_space=pl.ANY)
    return pl.pallas_call(
        body, in_specs=[hbm], out_specs=hbm, out_shape=jax.ShapeDtypeStruct((3, rh, PACK_COLS), hp.dtype), name=name,
        scratch_shapes=[pltpu.SemaphoreType.DMA((3,)), pltpu.SemaphoreType.DMA((3,))],
    )(hp)


def join_halves(half, *, name):
    rh = half.shape[0]

    def body(h_ref, out_ref, send_sem, recv_sem, local_sem):
        x, y, c = _place()

        def rows(hf):
            return out_ref.at[pl.ds(pl.multiple_of(hf * rh, SUBLANES), rh), :]

        mine = pltpu.make_async_copy(h_ref, rows(c), local_sem)
        mine.start()
        cp = pltpu.make_async_remote_copy(src_ref=h_ref, dst_ref=rows(c), send_sem=send_sem, recv_sem=recv_sem,
                                          device_id=(x, y, 1 - c), device_id_type=MESH)
        cp.start()
        pltpu.make_async_remote_copy(src_ref=h_ref, dst_ref=rows(1 - c), send_sem=send_sem, recv_sem=recv_sem,
                                     device_id=(x, y, 1 - c), device_id_type=MESH).wait_recv()
        cp.wait_send()
        mine.wait()

    hbm = pl.BlockSpec(memory_space=pl.ANY)
    return pl.pallas_call(
        body, in_specs=[hbm], out_specs=hbm, out_shape=jax.ShapeDtypeStruct((2 * rh, PACK_COLS), half.dtype), name=name,
        scratch_shapes=[pltpu.SemaphoreType.DMA, pltpu.SemaphoreType.DMA, pltpu.SemaphoreType.DMA],
    )(half)


def add_parts(parts, *, out_dtype, name):
    rows = parts[0].shape[0]
    tr = rows
    for cand in range(BF16_ROWS, min(rows, 1200) + 1, BF16_ROWS):
        if rows % cand == 0:
            tr = cand

    def body(*refs):
        acc = refs[0][...].astype(F32)
        for r in refs[1:-1]:
            acc = acc + r[...].astype(F32)
        refs[-1][...] = acc.astype(refs[-1].dtype)

    spec = pl.BlockSpec((tr, PACK_COLS), lambda i: (i, 0))
    return pl.pallas_call(
        body, grid=(rows // tr,), in_specs=[spec] * len(parts), out_specs=spec,
        out_shape=jax.ShapeDtypeStruct((rows, PACK_COLS), out_dtype), name=name, compiler_params=_params(("parallel",)),
    )(*parts)


def sum8(g, *, name):
    def body(g_ref, o_ref):
        acc = g_ref[0]
        for d in range(1, N_DEV):
            acc = acc + g_ref[d]
        o_ref[...] = acc

    return pl.pallas_call(body, out_shape=jax.ShapeDtypeStruct(g.shape[1:], F32), name=name)(g)


def _dup_halves(a):
    t = a.shape[0]
    a = a.reshape(t, N_KV_B, HEAD_DIM_B)
    return jnp.concatenate([a, a], axis=-1).reshape(t, N_KV_B * LANES)


def _undup(a):
    t = a.shape[0]
    return a.reshape(t, N_KV_B, LANES)[:, :, :HEAD_DIM_B].reshape(t, N_KV_B * HEAD_DIM_B)


def _lane_pad(v, offset=0):
    return jnp.zeros((1, LANES), F32).at[0, offset:offset + v.shape[0]].set(v)


def local_step(x, p, tgt, sm, wt):
    t = x.shape[0]
    rtm = min(256, t)
    hk = N_HEADS_A * HEAD_DIM_A
    qd_b = N_HEADS_B * HEAD_DIM_B
    kd_b = N_KV_B * HEAD_DIM_B
    gw, gs = {}, {}
    norm = lambda h, w, nm: tile_map(_f_norm, [(h, D_MODEL, 0)], [w], [(D_MODEL, BF16)], tm=rtm, ncol=1, name=nm)[0]

    def norm_bwd(h, w, dy, add, nm):
        (dh,), (dw,) = tile_vjp(_f_norm, [(h, D_MODEL, 0)], [w], [(dy, D_MODEL, 0)], n_diff=1, tm=rtm, ncol=1, name=nm,
                                add=(add, D_MODEL, 0))
        return dh, dw

    p_bf = p.astype(BF16)
    a_main, a_tail = wt["a_w_in"][:, :4 * hk], jnp.pad(wt["a_w_in"][:, 4 * hk:], ((0, 0), (0, LANES - 2 * N_HEADS_A)))
    alog_p = _lane_pad(sm["a_log"][0], N_HEADS_A)
    dtb_p = _lane_pad(sm["a_dt_bias"][0], N_HEADS_A)
    sinks_p = _lane_pad(sm["b_sinks"][0])
    nw = lambda name, i: sm[name][i:i + 1]

    saved = []
    h = x
    for i in range(DEPTH):
        s = {"h0": h}
        s["hn"] = norm(h, nw("norm_mix", i), f"norm_mix{i}")
        if i % 2 == 0:
            s["pm"] = mm(s["hn"], a_main, name="a_in_main")
            s["pt"] = mm(s["hn"], a_tail, name="a_in_tail")
            s["c"] = conv_fwd(s["pm"], wt["a_conv"], name="a_conv")
            s["bg"] = tile_map(_f_betag, [(s["pt"], LANES, 0)], [alog_p, dtb_p], [(LANES, F32)], tm=rtm, ncol=1, name="a_betag")[0]
            s["prep"] = delta_prep(s["c"], s["bg"], name="a_prep")
            s["o"], s["s_in"] = delta_scan(*s["prep"], name="a_scan")
            s["on"] = tile_map(_f_gnorm, [(s["o"], HEAD_DIM_A, 0), (s["pm"], HEAD_DIM_A, 3 * N_HEADS_A)], [sm["a_norm"]],
                               [(HEAD_DIM_A, BF16)], tm=rtm, ncol=N_HEADS_A, name="a_gnorm")[0]
            h = mm(s["on"], wt["a_w_out"], add=h, name="a_out")
        else:
            s["pb"] = mm(s["hn"], wt["b_w_in"], name="b_in")
            s["kd"], s["vd"] = _dup_halves(s["pb"][:, qd_b:qd_b + kd_b]), _dup_halves(s["pb"][:, qd_b + kd_b:])
            s["ao"] = swa_fwd(s["pb"], s["kd"], s["vd"], sinks_p, name="b_att")
            h = mm(s["ao"], wt["b_w_out"], add=h, name="b_out")
        s["h1"] = h
        s["hf"] = norm(h, nw("norm_ffn", i), f"norm_ffn{i}")
        s["u"] = mm(s["hf"], wt["f_w_up"][i], name=f"f_up{i}")
        s["uc"] = conv_fwd(s["u"], wt["f_conv"][i], name=f"f_conv{i}")
        nff = D_FF // 1408
        s["act"] = tile_map(_f_act, [(s["uc"], 1408, 0), (s["uc"], 1408, nff)], [], [(1408, BF16)], tm=rtm, ncol=nff, name=f"f_act{i}")[0]
        h = mm(s["act"], wt["f_w_down"][i], add=h, name=f"f_down{i}")
        s["h2"] = h
        s["hp"] = norm(h, nw("norm_ple", i), f"norm_ple{i}")
        s["gl"] = mm(s["hp"], wt["ple_w_gate"][i], name=f"ple_gate{i}")
        s["pe"] = mm(p_bf[i], wt["ple_w_proj"][i], name=f"ple_proj{i}")
        h = tile_map(lambda hv, g, e: hv + _f_ple(g, e), [(h, D_MODEL, 0), (s["gl"], D_MODEL, 0), (s["pe"], D_MODEL, 0)], [],
                     [(D_MODEL, F32)], tm=rtm, ncol=1, name=f"ple_mix{i}")[0]
        saved.append(s)

    dh, gnf, loss = loss_head(h, tgt, sm["norm_final"][None, :], name="loss_head")
    gs["norm_final"] = gnf[0]

    g_mix, g_ffn, g_ple = [None] * DEPTH, [None] * DEPTH, [None] * DEPTH
    g_up, g_conv, g_down, g_proj, g_gate = ([None] * DEPTH for _ in range(5))
    for i in reversed(range(DEPTH)):
        s = saved[i]
        (dgl, dpe), _ = tile_vjp(_f_ple, [(s["gl"], D_MODEL, 0), (s["pe"], D_MODEL, 0)], [], [(dh, D_MODEL, 0)], n_diff=2,
                                 tm=rtm, ncol=1, name=f"ple_mix_bwd{i}")
        g_proj[i] = mm(p_bf[i], dpe, ta=True, name=f"ple_proj_dw{i}")
        g_gate[i] = mm(s["hp"], dgl, ta=True, name=f"ple_gate_dw{i}")
        dhp = mm(dgl, wt["ple_w_gate"][i], tb=True, name=f"ple_gate_dx{i}")
        dh, g_ple[i] = norm_bwd(s["h2"], nw("norm_ple", i), dhp, dh, f"norm_ple_bwd{i}")

        dact = mm(dh, wt["f_w_down"][i], tb=True, name=f"f_down_dx{i}")
        g_down[i] = mm(s["act"], dh, ta=True, name=f"f_down_dw{i}")
        nff = D_FF // 1408
        (dgate, dval), _ = tile_vjp(_f_act, [(s["uc"], 1408, 0), (s["uc"], 1408, nff)], [], [(dact, 1408, 0)], n_diff=2,
                                    tm=rtm, ncol=nff, name=f"f_act_bwd{i}")
        du, g_conv[i] = conv_bwd(jnp.concatenate([dgate, dval], axis=1), s["u"], wt["f_conv"][i], name=f"f_conv_bwd{i}")
        g_up[i] = mm(s["hf"], du, ta=True, name=f"f_up_dw{i}")
        dhf = mm(du, wt["f_w_up"][i], tb=True, name=f"f_up_dx{i}")
        dh, g_ffn[i] = norm_bwd(s["h1"], nw("norm_ffn", i), dhf, dh, f"norm_ffn_bwd{i}")

        if i % 2 == 0:
            don = mm(dh, wt["a_w_out"], tb=True, name="a_out_dx")
            gw["a_w_out"] = mm(s["on"], dh, ta=True, name="a_out_dw")
            (do, dz), (gan,) = tile_vjp(_f_gnorm, [(s["o"], HEAD_DIM_A, 0), (s["pm"], HEAD_DIM_A, 3 * N_HEADS_A)], [sm["a_norm"]],
                                        [(don, HEAD_DIM_A, 0)], n_diff=2, tm=rtm, ncol=N_HEADS_A, name="a_gnorm_bwd")
            gs["a_norm"] = gan
            dprep = delta_scan_bwd(do, *s["prep"], s["s_in"], name="a_scan_bwd")
            dcq, dck, dcv, dbg = delta_prep_bwd(s["c"], s["bg"], dprep, name="a_prep_bwd")
            (dpt,), (galog, gdtb) = tile_vjp(_f_betag, [(s["pt"], LANES, 0)], [alog_p, dtb_p], [(dbg, LANES, 0)], n_diff=1,
                                             tm=rtm, ncol=1, name="a_betag_bwd")
            gs["a_log"] = galog[:, N_HEADS_A:2 * N_HEADS_A]
            gs["a_dt_bias"] = gdtb[:, N_HEADS_A:2 * N_HEADS_A]
            dqkv, gw["a_conv"] = conv_bwd(jnp.concatenate([dcq, dck, dcv], axis=1), s["pm"], wt["a_conv"], name="a_conv_bwd")
            dpm = jnp.concatenate([dqkv, dz], axis=1)
            dhn = mm(dpm, a_main, tb=True, name="a_in_main_dx")
            dhn = mm(dpt, a_tail, tb=True, add=dhn, name="a_in_tail_dx")
            g_main = mm(s["hn"], dpm, ta=True, name="a_in_main_dw")
            g_tail = mm(s["hn"], dpt, ta=True, name="a_in_tail_dw")
            gw["a_w_in"] = jnp.concatenate([g_main, g_tail[:, :2 * N_HEADS_A]], axis=1)
        else:
            dao = mm(dh, wt["b_w_out"], tb=True, name="b_out_dx")
            gw["b_w_out"] = mm(s["ao"], dh, ta=True, name="b_out_dw")
            dq, dkd, dvd, gsk = swa_bwd(dao, s["pb"], s["kd"], s["vd"], sinks_p, name="b_att_bwd")
            gs["b_sinks"] = gsk[:, :N_HEADS_B]
            dpb = jnp.concatenate([dq, _undup(dkd), _undup(dvd)], axis=1)
            dhn = mm(dpb, wt["b_w_in"], tb=True, name="b_in_dx")
            gw["b_w_in"] = mm(s["hn"], dpb, ta=True, name="b_in_dw")
        dh, g_mix[i] = norm_bwd(s["h0"], nw("norm_mix", i), dhn, dh, f"norm_mix_bwd{i}")

    gs["norm_mix"], gs["norm_ffn"], gs["norm_ple"] = (jnp.concatenate(g, axis=0) for g in (g_mix, g_ffn, g_ple))
    gw["f_w_up"], gw["f_conv"], gw["f_w_down"] = jnp.stack(g_up), jnp.stack(g_conv), jnp.stack(g_down)
    gw["ple_w_proj"], gw["ple_w_gate"] = jnp.stack(g_proj), jnp.stack(g_gate)
    return loss, dh, gw, gs


BIG = [("a_w_in", 2), ("a_w_out", 1), ("b_w_in", 2), ("b_w_out", 1), ("f_w_up", 2), ("f_w_down", 1),
       ("ple_w_proj", 2), ("ple_w_gate", 1)]
CONVS = [("a_conv", 2), ("f_conv", 2)]
SMALL = ["norm_mix", "norm_ffn", "norm_ple", "norm_final", "a_log", "a_dt_bias", "a_norm", "b_sinks"]
SMALL_ROWS = 8
CONV_ROWS = 16


def _pack_rows(arrs, rows, dtype):
    flat = jnp.concatenate([a.reshape(-1).astype(dtype) for a in arrs])
    return jnp.pad(flat, (0, rows * PACK_COLS - flat.shape[0])).reshape(rows, PACK_COLS)


def _unpack(flat, shapes):
    out, off = [], 0
    for shp in shapes:
        n = math.prod(shp)
        out.append(flat[off:off + n].reshape(shp))
        off += n
    return out


def _pack_small(d, loss=None):
    tail = jnp.concatenate([d["a_log"].reshape(-1), d["a_dt_bias"].reshape(-1), d["a_norm"].reshape(-1), d["b_sinks"].reshape(-1)])
    if loss is not None:
        tail = jnp.concatenate([tail, loss.reshape(-1)[:1]])
    tail = jnp.pad(tail, (0, PACK_COLS - tail.shape[0]))
    return jnp.concatenate([d["norm_mix"], d["norm_ffn"], d["norm_ple"], d["norm_final"][None, :], tail[None, :]], axis=0)


def _unpack_small(a, like):
    out = {"norm_mix": a[0:2], "norm_ffn": a[2:4], "norm_ple": a[4:6], "norm_final": a[6]}
    off = 0
    for nm in ("a_log", "a_dt_bias", "a_norm", "b_sinks"):
        n = like[nm].size
        out[nm] = a[7, off:off + n].reshape(like[nm].shape)
        off += n
    return out, a[7, off]


def _shard_rows(shapes):
    n = sum(math.prod(s) for s in shapes)
    rows = -(-n // PACK_COLS)
    return -(-rows // (2 * BF16_ROWS)) * (2 * BF16_ROWS)


def _as2d(a):
    return a.reshape(-1, a.shape[-1])


def kernel(x, p, norm_mix, norm_ffn, norm_ple, norm_final, a_w_in, a_conv, a_log, a_dt_bias, a_norm, a_w_out, b_w_in, b_sinks, b_w_out, f_w_up, f_conv, f_w_down, ple_w_proj, ple_w_gate, loss_target, m_norm_mix, m_norm_ffn, m_norm_ple, m_norm_final, m_a_w_in, m_a_conv, m_a_log, m_a_dt_bias, m_a_norm, m_a_w_out, m_b_w_in, m_b_sinks, m_b_w_out, m_f_w_up, m_f_conv, m_f_w_down, m_ple_w_proj, m_ple_w_gate, v_norm_mix, v_norm_ffn, v_norm_ple, v_norm_final, v_a_w_in, v_a_conv, v_a_log, v_a_dt_bias, v_a_norm, v_a_w_out, v_b_w_in, v_b_sinks, v_b_w_out, v_f_w_up, v_f_conv, v_f_w_down, v_ple_w_proj, v_ple_w_gate):
    w = dict(norm_mix=norm_mix, norm_ffn=norm_ffn, norm_ple=norm_ple, norm_final=norm_final, a_w_in=a_w_in, a_conv=a_conv,
             a_log=a_log, a_dt_bias=a_dt_bias, a_norm=a_norm, a_w_out=a_w_out, b_w_in=b_w_in, b_sinks=b_sinks, b_w_out=b_w_out,
             f_w_up=f_w_up, f_conv=f_conv, f_w_down=f_w_down, ple_w_proj=ple_w_proj, ple_w_gate=ple_w_gate)
    m = dict(norm_mix=m_norm_mix, norm_ffn=m_norm_ffn, norm_ple=m_norm_ple, norm_final=m_norm_final, a_w_in=m_a_w_in,
             a_conv=m_a_conv, a_log=m_a_log, a_dt_bias=m_a_dt_bias, a_norm=m_a_norm, a_w_out=m_a_w_out, b_w_in=m_b_w_in,
             b_sinks=m_b_sinks, b_w_out=m_b_w_out, f_w_up=m_f_w_up, f_conv=m_f_conv, f_w_down=m_f_w_down,
             ple_w_proj=m_ple_w_proj, ple_w_gate=m_ple_w_gate)
    v = dict(norm_mix=v_norm_mix, norm_ffn=v_norm_ffn, norm_ple=v_norm_ple, norm_final=v_norm_final, a_w_in=v_a_w_in,
             a_conv=v_a_conv, a_log=v_a_log, a_dt_bias=v_a_dt_bias, a_norm=v_a_norm, a_w_out=v_a_w_out, b_w_in=v_b_w_in,
             b_sinks=v_b_sinks, b_w_out=v_b_w_out, f_w_up=v_f_w_up, f_conv=v_f_conv, f_w_down=v_f_w_down,
             ple_w_proj=v_ple_w_proj, ple_w_gate=v_ple_w_gate)
    xc, yc, cc = _place()
    my_chip = 2 * xc + yc

    big_shapes = [w[n].shape for n, _ in BIG]
    rows_big = _shard_rows(big_shapes)
    gathered = allgather_shards(_pack_rows([w[n] for n, _ in BIG], rows_big, BF16), name="gather_weights")
    conv_shapes = [w[n].shape for n, _ in CONVS]
    convs = allgather8(_pack_rows([w[n] for n, _ in CONVS], CONV_ROWS, F32), name="gather_convs")
    full = {}
    for src, names, shapes, pick in ((gathered, BIG, big_shapes, lambda j: j), (convs, CONVS, conv_shapes, lambda j: 2 * j)):
        parts = [_unpack(src[pick(j)].reshape(-1), shapes) for j in range(N_CHIPS)]
        for q, (nm, ax) in enumerate(names):
            full[nm] = jnp.concatenate([parts[j][q] for j in range(N_CHIPS)], axis=ax)
    wt = {"a_w_in": full["a_w_in"][0], "a_conv": full["a_conv"][0], "a_w_out": full["a_w_out"][0], "b_w_in": full["b_w_in"][0],
          "b_w_out": full["b_w_out"][0], "f_w_up": full["f_w_up"], "f_conv": full["f_conv"], "f_w_down": full["f_w_down"],
          "ple_w_proj": full["ple_w_proj"], "ple_w_gate": full["ple_w_gate"]}
    sm = {n: w[n] for n in SMALL}

    loss, grad_x, gw, gs = local_step(x[0], p[:, 0], loss_target[0], sm, wt)
    gw = {"a_w_in": gw["a_w_in"][None], "a_conv": gw["a_conv"][None], "a_w_out": gw["a_w_out"][None], "b_w_in": gw["b_w_in"][None],
          "b_w_out": gw["b_w_out"][None], "f_w_up": gw["f_w_up"], "f_conv": gw["f_conv"], "f_w_down": gw["f_w_down"],
          "ple_w_proj": gw["ple_w_proj"], "ple_w_gate": gw["ple_w_gate"]}

    sharded = BIG + CONVS
    sh_shapes = [w[n].shape for n, _ in sharded]
    rows_all = _shard_rows(sh_shapes)
    rh = rows_all // 2

    def dest(j):
        return _pack_rows([lax.slice_in_dim(gw[n], j * w[n].shape[ax], (j + 1) * w[n].shape[ax], axis=ax) for n, ax in sharded],
                          rows_all, BF16)

    gpack = jnp.stack([dest(j) for j in range(N_CHIPS)])
    from_sib = swap_halves(gpack, name="rs_swap")
    mine = lax.dynamic_slice_in_dim(gpack, cc * rh, rh, axis=1)
    pair = add_parts([mine.reshape(N_CHIPS * rh, PACK_COLS), from_sib.reshape(N_CHIPS * rh, PACK_COLS)], out_dtype=BF16,
                     name="rs_add_pair").reshape(N_CHIPS, rh, PACK_COLS)
    others = scatter_chips(pair, name="rs_scatter")
    own = lax.dynamic_index_in_dim(pair, my_chip, axis=0, keepdims=False)
    half = add_parts([own, others[0], others[1], others[2]], out_dtype=F32, name="rs_add_chips")
    gfull = join_halves(half, name="rs_join")
    g_sh = dict(zip([n for n, _ in sharded], _unpack(gfull.reshape(-1), sh_shapes)))

    small_sum = sum8(allgather8(_pack_small(gs, loss), name="gather_small"), name="sum_small")
    g_sm, loss_sum = _unpack_small(small_sum, sm)

    grads, delta, new_m, new_v = {}, {}, {}, {}
    for n, _ in sharded:
        d2, m2, v2 = adamw(_as2d(w[n]), _as2d(g_sh[n]), _as2d(m[n]), _as2d(v[n]), name=f"adamw_{n}")
        grads[n], delta[n], new_m[n], new_v[n] = g_sh[n], d2.reshape(w[n].shape), m2.reshape(w[n].shape), v2.reshape(w[n].shape)
    pk = lambda d: _pack_small(d)
    d2, m2, v2 = adamw(pk(sm), pk(g_sm), pk({n: m[n] for n in SMALL}), pk({n: v[n] for n in SMALL}), name="adamw_small")
    for src, dst in ((d2, delta), (m2, new_m), (v2, new_v)):
        dst.update(_unpack_small(src, sm)[0])
    grads.update(g_sm)

    order = ["norm_mix", "norm_ffn", "norm_ple", "norm_final", "a_w_in", "a_conv", "a_log", "a_dt_bias", "a_norm", "a_w_out",
             "b_w_in", "b_sinks", "b_w_out", "f_w_up", "f_conv", "f_w_down", "ple_w_proj", "ple_w_gate"]
    return (loss_sum, grad_x[None], *[grads[n] for n in order], *[delta[n] for n in order],
            *[new_m[n] for n in order], *[new_v[n] for n in order])
```

```python
import functools
import math

import jax
import jax.numpy as jnp
from jax import lax
from jax.experimental import pallas as pl
from jax.experimental.pallas import tpu as pltpu

F32 = jnp.float32
BF16 = jnp.bfloat16
MESH = pl.DeviceIdType.MESH

D_MODEL = 1024
N_HEADS_A = 8
HEAD_DIM_A = 128
CONV_A = 4
N_HEADS_B = 16
N_KV_B = 4
HEAD_DIM_B = 64
WINDOW = 128
D_FF = 2816
FFN_CONV = 3
PLE_DIM = 256
EPS = 1e-6
DEPTH = 2

ADAM_LR = 0.001
ADAM_B1 = 0.9
ADAM_B2 = 0.999
ADAM_EPS = 1e-08
ADAM_WD = 0.01
ADAM_STEP = 10

LANES = 128
SUBLANES = 8
BF16_ROWS = 16
CHUNK = 128
VMEM_LIMIT = 56 * 1024 * 1024
NEG = -1e30
N_CHIPS = 4
N_DEV = 8
PACK_COLS = 1024


def _params(sem=None):
    return pltpu.CompilerParams(dimension_semantics=sem, vmem_limit_bytes=VMEM_LIMIT)


def _tile(dim, cap):
    if dim % LANES:
        return dim
    best = LANES
    for t in range(LANES, min(dim, cap) + 1, LANES):
        if dim % t == 0:
            best = t
    return best


def _dot(a, b, dims=(((1,), (0,)), ((), ())), precision=None):
    return lax.dot_general(a, b, dims, precision=precision, preferred_element_type=F32)


NN = (((1,), (0,)), ((), ()))
NT = (((1,), (1,)), ((), ()))
TN = (((0,), (0,)), ((), ()))


def mm(a, b, *, name, ta=False, tb=False, out_dtype=F32, add=None, tm_cap=512, tn_cap=1408, tk_cap=1408):
    m, k = (a.shape[1], a.shape[0]) if ta else a.shape
    n = b.shape[0] if tb else b.shape[1]
    assert (b.shape[1] if tb else b.shape[0]) == k, (a.shape, b.shape, ta, tb)
    tm, tn, tk = _tile(m, tm_cap), _tile(n, tn_cap), _tile(k, tk_cap)
    nk = k // tk
    dims = (((0 if ta else 1,), (1 if tb else 0,)), ((), ()))
    has_add = add is not None

    def body(*refs):
        a_ref, b_ref = refs[0], refs[1]
        add_ref = refs[2] if has_add else None
        o_ref = refs[3] if has_add else refs[2]
        part = _dot(a_ref[...].astype(BF16), b_ref[...].astype(BF16), dims)
        if nk == 1:
            if has_add:
                part = part + add_ref[...].astype(F32)
            o_ref[...] = part.astype(o_ref.dtype)
            return
        acc = refs[-1]
        kk = pl.program_id(2)

        @pl.when(kk == 0)
        def _():
            acc[...] = part

        @pl.when(kk > 0)
        def _():
            acc[...] += part

        @pl.when(kk == nk - 1)
        def _():
            r = acc[...]
            if has_add:
                r = r + add_ref[...].astype(F32)
            o_ref[...] = r.astype(o_ref.dtype)

    a_spec = pl.BlockSpec((tk, tm), lambda i, j, kk: (kk, i)) if ta else pl.BlockSpec((tm, tk), lambda i, j, kk: (i, kk))
    b_spec = pl.BlockSpec((tn, tk), lambda i, j, kk: (j, kk)) if tb else pl.BlockSpec((tk, tn), lambda i, j, kk: (kk, j))
    o_spec = pl.BlockSpec((tm, tn), lambda i, j, kk: (i, j))
    in_specs = [a_spec, b_spec] + ([o_spec] if has_add else [])
    args = (a, b) + ((add,) if has_add else ())
    return pl.pallas_call(
        body, grid=(m // tm, n // tn, nk), in_specs=in_specs, out_specs=o_spec,
        out_shape=jax.ShapeDtypeStruct((m, n), out_dtype), name=name,
        scratch_shapes=[pltpu.VMEM((tm, tn), F32)] if nk > 1 else [],
        compiler_params=_params(("parallel", "parallel", "arbitrary")),
    )(*args)


def _row_spec(tm, cw, coff):
    return pl.BlockSpec((tm, cw), lambda i, j: (i, j + coff))


def _full_spec(shape):
    return pl.BlockSpec(shape, lambda i, j: (0,) * len(shape))


def tile_map(fn, rows, params, outs, *, tm, ncol, name):
    t = rows[0][0].shape[0]
    nin = len(rows) + len(params)

    def body(*refs):
        res = fn(*[r[...] for r in refs[:nin]])
        res = res if isinstance(res, (tuple, list)) else (res,)
        for o_ref, r in zip(refs[nin:], res):
            o_ref[...] = r.astype(o_ref.dtype)

    in_specs = [_row_spec(tm, cw, coff) for (_, cw, coff) in rows] + [_full_spec(p.shape) for p in params]
    res = pl.pallas_call(
        body, grid=(t // tm, ncol), in_specs=in_specs,
        out_specs=[_row_spec(tm, cw, 0) for (cw, _) in outs],
        out_shape=[jax.ShapeDtypeStruct((t, cw * ncol), dt) for (cw, dt) in outs], name=name,
        compiler_params=_params(("parallel", "parallel")),
    )(*[r[0] for r in rows], *params)
    return res


def tile_vjp(fn, rows, params, cts, *, n_diff, tm, ncol, name, add=None):
    t = rows[0][0].shape[0]
    nr, npar, nct = len(rows), len(params), len(cts)
    has_add = add is not None

    def body(*refs):
        vals = [r[...] for r in refs[:nr + npar + nct + (1 if has_add else 0)]]
        diff, rest, pars = vals[:n_diff], vals[n_diff:nr], vals[nr:nr + npar]
        ctv = vals[nr + npar:nr + npar + nct]
        outs_ref = refs[nr + npar + nct + (1 if has_add else 0):]

        def f(*a):
            res = fn(*a[:n_diff], *rest, *a[n_diff:])
            return tuple(res) if isinstance(res, (tuple, list)) else (res,)

        primal, vjp = jax.vjp(f, *[d.astype(F32) for d in diff], *pars)
        grads = vjp(tuple(c.astype(o.dtype) for c, o in zip(ctv, primal)))
        for q in range(n_diff):
            g = grads[q]
            if has_add and q == 0:
                g = g + vals[-1]
            outs_ref[q][...] = g.astype(outs_ref[q].dtype)
        first = (pl.program_id(0) == 0) & (pl.program_id(1) == 0)
        for q in range(npar):
            o_ref, g = outs_ref[n_diff + q], grads[n_diff + q]

            @pl.when(first)
            def _(o_ref=o_ref, g=g):
                o_ref[...] = g

            @pl.when(jnp.logical_not(first))
            def _(o_ref=o_ref, g=g):
                o_ref[...] += g

    ins = list(rows) + [None] * 0
    in_specs = [_row_spec(tm, cw, coff) for (_, cw, coff) in rows] + [_full_spec(p.shape) for p in params]
    in_specs += [_row_spec(tm, cw, coff) for (_, cw, coff) in cts]
    args = [r[0] for r in rows] + list(params) + [c[0] for c in cts]
    if has_add:
        in_specs.append(_row_spec(tm, add[1], add[2]))
        args.append(add[0])
    out_specs = [_row_spec(tm, rows[q][1], 0) for q in range(n_diff)] + [_full_spec(p.shape) for p in params]
    out_shape = [jax.ShapeDtypeStruct((t, rows[q][1] * ncol), F32) for q in range(n_diff)]
    out_shape += [jax.ShapeDtypeStruct(p.shape, F32) for p in params]
    del ins
    res = pl.pallas_call(
        body, grid=(t // tm, ncol), in_specs=in_specs, out_specs=out_specs, out_shape=out_shape, name=name,
        compiler_params=_params(("arbitrary", "arbitrary")),
    )(*args)
    return res[:n_diff], res[n_diff:]


def _silu(x):
    return x * jax.nn.sigmoid(x)


def _f_norm(h, w):
    return h * lax.rsqrt(jnp.mean(h * h, axis=-1, keepdims=True) + EPS) * w


def _f_gnorm(o, z, w):
    return _f_norm(o, w) * _silu(z)


def _f_act(gate, val):
    return _silu(gate) * val


def _f_ple(gl, pe):
    return jax.nn.sigmoid(gl) * pe


def _f_betag(pt, alog, dtb):
    lane = lax.broadcasted_iota(jnp.int32, (1, LANES), 1)
    z = pt + dtb
    softplus = jnp.maximum(z, 0.0) + jnp.log(1.0 + jnp.exp(-jnp.abs(z)))
    g = -jnp.exp(alog) * softplus
    return jnp.where(lane < N_HEADS_A, jax.nn.sigmoid(pt), jnp.where(lane < 2 * N_HEADS_A, g, 0.0))


CONV_TM = 256
CONV_CW = 512


def _shift_down(x, prev, s, row):
    rp = jnp.tile(pltpu.roll(prev, s, 0), (x.shape[0] // SUBLANES, 1))
    return jnp.where(row < s, rp, pltpu.roll(x, s, 0))


def _shift_up(x, nxt, s, row):
    tm = x.shape[0]
    rn = jnp.tile(pltpu.roll(nxt, SUBLANES - s, 0), (tm // SUBLANES, 1))
    return jnp.where(row >= tm - s, rn, pltpu.roll(x, tm - s, 0))


def conv_fwd(x, w, *, name):
    t = x.shape[0]
    k, c = w.shape
    tm, cw = min(CONV_TM, t), CONV_CW
    nb8 = tm // SUBLANES

    def body(x_ref, p_ref, w_ref, o_ref):
        i = pl.program_id(1)
        xv = x_ref[...]
        prev = jnp.where(i > 0, p_ref[...], 0.0)
        row = lax.broadcasted_iota(jnp.int32, xv.shape, 0)
        y = xv * w_ref[pl.ds(k - 1, 1), :]
        for s in range(1, k):
            y = y + _shift_down(xv, prev, s, row) * w_ref[pl.ds(k - 1 - s, 1), :]
        o_ref[...] = y

    return pl.pallas_call(
        body, grid=(c // cw, t // tm),
        in_specs=[pl.BlockSpec((tm, cw), lambda j, i: (i, j)),
                  pl.BlockSpec((SUBLANES, cw), lambda j, i: (jnp.maximum(i * nb8 - 1, 0), j)),
                  pl.BlockSpec((k, cw), lambda j, i: (0, j))],
        out_specs=pl.BlockSpec((tm, cw), lambda j, i: (i, j)),
        out_shape=jax.ShapeDtypeStruct((t, c), F32), name=name,
        compiler_params=_params(("parallel", "parallel")),
    )(x, x, w)


def conv_bwd(dy, x, w, *, name):
    t = x.shape[0]
    k, c = w.shape
    tm, cw = min(CONV_TM, t), CONV_CW
    nb8 = tm // SUBLANES
    ni = t // tm

    def body(dy_ref, dn_ref, x_ref, p_ref, w_ref, dx_ref, dw_ref):
        i = pl.program_id(1)
        dyv, xv = dy_ref[...], x_ref[...]
        nxt = jnp.where(i < ni - 1, dn_ref[...], 0.0)
        prev = jnp.where(i > 0, p_ref[...], 0.0)
        row = lax.broadcasted_iota(jnp.int32, xv.shape, 0)
        dx = dyv * w_ref[pl.ds(k - 1, 1), :]
        dws = [jnp.sum(dyv * xv, axis=0, keepdims=True)]
        for s in range(1, k):
            dx = dx + _shift_up(dyv, nxt, s, row) * w_ref[pl.ds(k - 1 - s, 1), :]
            dws.append(jnp.sum(dyv * _shift_down(xv, prev, s, row), axis=0, keepdims=True))
        dx_ref[...] = dx
        for s in range(k):
            @pl.when(i == 0)
            def _(s=s):
                dw_ref[pl.ds(k - 1 - s, 1), :] = dws[s]

            @pl.when(i > 0)
            def _(s=s):
                dw_ref[pl.ds(k - 1 - s, 1), :] += dws[s]

    return pl.pallas_call(
        body, grid=(c // cw, ni),
        in_specs=[pl.BlockSpec((tm, cw), lambda j, i: (i, j)),
                  pl.BlockSpec((SUBLANES, cw), lambda j, i: (jnp.minimum((i + 1) * nb8, t // SUBLANES - 1), j)),
                  pl.BlockSpec((tm, cw), lambda j, i: (i, j)),
                  pl.BlockSpec((SUBLANES, cw), lambda j, i: (jnp.maximum(i * nb8 - 1, 0), j)),
                  pl.BlockSpec((k, cw), lambda j, i: (0, j))],
        out_specs=[pl.BlockSpec((tm, cw), lambda j, i: (i, j)), pl.BlockSpec((k, cw), lambda j, i: (0, j))],
        out_shape=[jax.ShapeDtypeStruct((t, c), F32), jax.ShapeDtypeStruct((k, c), F32)], name=name,
        compiler_params=_params(("parallel", "arbitrary")),
    )(dy, dy, x, x, w)


@jax.custom_vjp
def _inv_unit_lower(lm):
    return _inv_series(lm)


def _inv_series(lm):
    c = lm.shape[0]
    ri = lax.broadcasted_iota(jnp.int32, (c, c), 0)
    ci = lax.broadcasted_iota(jnp.int32, (c, c), 1)
    dm = (ri == ci).astype(F32) - jnp.where((ri >> 1) == (ci >> 1), lm, 0.0)
    for lv in range(1, int(math.log2(c))):
        below = ((ri >> (lv + 1)) == (ci >> (lv + 1))) & ((ri >> lv) != (ci >> lv))
        db = dm.astype(BF16)
        dm = dm - _dot(db, _dot(jnp.where(below, lm, 0.0).astype(BF16), db).astype(BF16))
    return dm


def _inv_fwd(lm):
    tm = _inv_series(lm)
    return tm, tm


def _inv_bwd(tm, dt):
    tb = tm.astype(BF16)
    return (-_dot(_dot(tb, dt.astype(BF16), TN).astype(BF16), tb, NT),)


_inv_unit_lower.defvjp(_inv_fwd, _inv_bwd)


def _l2n(x):
    return x * lax.rsqrt(jnp.sum(x * x, axis=-1, keepdims=True) + EPS)


def _prep_fn(cq, ck, cv, bg, hb, hg):
    c = cq.shape[0]
    q, k, v = _l2n(_silu(cq)), _l2n(_silu(ck)), _silu(cv)
    beta = jnp.sum(bg * hb, axis=1, keepdims=True)
    g = jnp.sum(bg * hg, axis=1, keepdims=True)
    ri = lax.broadcasted_iota(jnp.int32, (c, c), 0)
    ci = lax.broadcasted_iota(jnp.int32, (c, c), 1)
    eye = (ri == ci).astype(F32)
    incl, strict = ci <= ri, ci < ri
    g_row = jnp.sum(g * eye, axis=0, keepdims=True)
    gcs = jnp.sum(jnp.where(incl, g_row, 0.0), axis=1, keepdims=True)
    gcs_row = jnp.sum(gcs * eye, axis=0, keepdims=True)
    beta_row = jnp.sum(beta * eye, axis=0, keepdims=True)
    last = lax.broadcasted_iota(jnp.int32, (c, 1), 0) == c - 1
    gtot = jnp.sum(jnp.where(last, gcs, 0.0), axis=0, keepdims=True)
    decay = jnp.exp(jnp.where(incl, gcs - gcs_row, NEG))
    kb = k.astype(BF16)
    lm = jnp.where(strict, beta * _dot(kb, kb, NT) * decay, 0.0)
    am = (_inv_unit_lower(lm) * beta_row).astype(BF16)
    gam = jnp.exp(gcs)
    u0 = _dot(am, v.astype(BF16))
    wk = _dot(am, (gam * k).astype(BF16))
    qs = q * (HEAD_DIM_A ** -0.5)
    qk = _dot(qs.astype(BF16), kb, NT) * decay
    gl = jnp.exp(gtot) * jnp.ones((SUBLANES, LANES), F32)
    return u0, wk, qs * gam, k * jnp.exp(gtot - gcs), qk, gl


def _head_masks(h):
    lane = lax.broadcasted_iota(jnp.int32, (1, LANES), 1)
    return (lane == h).astype(F32), (lane == h + N_HEADS_A).astype(F32)


PREP_HEADS = 4


def _hsl(j):
    return slice(j * HEAD_DIM_A, (j + 1) * HEAD_DIM_A)


def delta_prep(cqkv, bg, *, name):
    t = cqkv.shape[0]
    nh, hd, n, hb = N_HEADS_A, HEAD_DIM_A, t // CHUNK, PREP_HEADS
    ng = nh // hb

    def body(cq_ref, ck_ref, cv_ref, bg_ref, u0_ref, wk_ref, qd_ref, kd_ref, qk_ref, gl_ref):
        bgv = bg_ref[...]
        for j in range(hb):
            mb, mg = _head_masks(pl.program_id(1) * hb + j)
            res = _prep_fn(cq_ref[:, _hsl(j)], ck_ref[:, _hsl(j)], cv_ref[:, _hsl(j)], bgv, mb, mg)
            for o_ref, r in zip((u0_ref, wk_ref, qd_ref, kd_ref, qk_ref), res[:5]):
                o_ref[:, _hsl(j)] = r
            gl_ref[j * SUBLANES:(j + 1) * SUBLANES, :] = res[5]

    blk = lambda off: pl.BlockSpec((CHUNK, hb * hd), lambda i, g: (i, g + off))
    return pl.pallas_call(
        body, grid=(n, ng),
        in_specs=[blk(0), blk(ng), blk(2 * ng), pl.BlockSpec((CHUNK, LANES), lambda i, g: (i, 0))],
        out_specs=[blk(0)] * 5 + [pl.BlockSpec((hb * SUBLANES, LANES), lambda i, g: (i * ng + g, 0))],
        out_shape=[jax.ShapeDtypeStruct((t, nh * hd), F32)] * 5 + [jax.ShapeDtypeStruct((n * nh * SUBLANES, LANES), F32)],
        name=name, compiler_params=_params(("parallel", "parallel")),
    )(cqkv, cqkv, cqkv, bg)


def delta_prep_bwd(cqkv, bg, cts, *, name):
    t = cqkv.shape[0]
    nh, hd, n, hb = N_HEADS_A, HEAD_DIM_A, t // CHUNK, PREP_HEADS
    ng = nh // hb

    def body(cq_ref, ck_ref, cv_ref, bg_ref, c0, c1, c2, c3, c4, c5, dq_ref, dk_ref, dv_ref, dbg_ref):
        g = pl.program_id(1)
        bgv = bg_ref[...]
        dbg = jnp.zeros_like(bgv)
        for j in range(hb):
            mb, mg = _head_masks(g * hb + j)
            _, vjp = jax.vjp(lambda a, b, c, d, mb=mb, mg=mg: _prep_fn(a, b, c, d, mb, mg),
                             cq_ref[:, _hsl(j)], ck_ref[:, _hsl(j)], cv_ref[:, _hsl(j)], bgv)
            dq, dk, dv, dbj = vjp(tuple(c[:, _hsl(j)] for c in (c0, c1, c2, c3, c4)) + (c5[j * SUBLANES:(j + 1) * SUBLANES, :],))
            dq_ref[:, _hsl(j)] = dq
            dk_ref[:, _hsl(j)] = dk
            dv_ref[:, _hsl(j)] = dv
            dbg = dbg + dbj

        @pl.when(g == 0)
        def _():
            dbg_ref[...] = dbg

        @pl.when(g > 0)
        def _():
            dbg_ref[...] += dbg

    blk = lambda off: pl.BlockSpec((CHUNK, hb * hd), lambda i, g: (i, g + off))
    gl_spec = pl.BlockSpec((hb * SUBLANES, LANES), lambda i, g: (i * ng + g, 0))
    bg_spec = pl.BlockSpec((CHUNK, LANES), lambda i, g: (i, 0))
    dq, dk, dv, dbg = pl.pallas_call(
        body, grid=(n, ng),
        in_specs=[blk(0), blk(ng), blk(2 * ng), bg_spec] + [blk(0)] * 5 + [gl_spec],
        out_specs=[blk(0)] * 3 + [bg_spec],
        out_shape=[jax.ShapeDtypeStruct((t, nh * hd), F32)] * 3 + [jax.ShapeDtypeStruct((t, LANES), F32)],
        name=name, compiler_params=_params(("parallel", "arbitrary")),
    )(cqkv, cqkv, cqkv, bg, *cts)
    return dq, dk, dv, dbg


def delta_scan(u0, wk, qd, kd, qk, gl, *, name):
    t = u0.shape[0]
    nh, hd, n = N_HEADS_A, HEAD_DIM_A, t // CHUNK

    def body(u0_ref, wk_ref, qd_ref, kd_ref, qk_ref, gl_ref, o_ref, sin_ref, s_ref):
        @pl.when(pl.program_id(0) == 0)
        def _():
            s_ref[...] = jnp.zeros_like(s_ref)

        for h in range(nh):
            s = s_ref[h]
            sin_ref[h] = s
            sb = s.astype(BF16)
            u = u0_ref[:, _hsl(h)] - _dot(wk_ref[:, _hsl(h)].astype(BF16), sb)
            ub = u.astype(BF16)
            o_ref[:, _hsl(h)] = _dot(qd_ref[:, _hsl(h)].astype(BF16), sb) + _dot(qk_ref[:, _hsl(h)].astype(BF16), ub)
            s_ref[h] = gl_ref[pl.ds(h * SUBLANES, 1), :] * s + _dot(kd_ref[:, _hsl(h)].astype(BF16), ub, TN)

    blk = pl.BlockSpec((CHUNK, nh * hd), lambda i: (i, 0))
    return pl.pallas_call(
        body, grid=(n,),
        in_specs=[blk] * 5 + [pl.BlockSpec((nh * SUBLANES, LANES), lambda i: (i, 0))],
        out_specs=[blk, pl.BlockSpec((None, nh, hd, hd), lambda i: (i, 0, 0, 0))],
        out_shape=[jax.ShapeDtypeStruct((t, nh * hd), F32), jax.ShapeDtypeStruct((n, nh, hd, hd), F32)],
        scratch_shapes=[pltpu.VMEM((nh, hd, hd), F32)], name=name,
        compiler_params=_params(("arbitrary",)),
    )(u0, wk, qd, kd, qk, gl)


def delta_scan_bwd(do, u0, wk, qd, kd, qk, gl, s_in, *, name):
    t = u0.shape[0]
    nh, hd, n = N_HEADS_A, HEAD_DIM_A, t // CHUNK

    def body(do_ref, u0_ref, wk_ref, qd_ref, kd_ref, qk_ref, gl_ref, sin_ref,
             du0_ref, dwk_ref, dqd_ref, dkd_ref, dqk_ref, dgl_ref, ds_ref):
        @pl.when(pl.program_id(0) == 0)
        def _():
            ds_ref[...] = jnp.zeros_like(ds_ref)

        corner = (lax.broadcasted_iota(jnp.int32, (SUBLANES, LANES), 0) == 0) & (lax.broadcasted_iota(jnp.int32, (SUBLANES, LANES), 1) == 0)
        for h in range(nh):
            s, dsn = sin_ref[h], ds_ref[h]
            sb, dsb = s.astype(BF16), dsn.astype(BF16)
            dob = do_ref[:, _hsl(h)].astype(BF16)
            wkb, qdb, kdb, qkb = (r[:, _hsl(h)].astype(BF16) for r in (wk_ref, qd_ref, kd_ref, qk_ref))
            u = u0_ref[:, _hsl(h)] - _dot(wkb, sb)
            ub = u.astype(BF16)
            du = _dot(qkb, dob, TN) + _dot(kdb, dsb)
            dub = du.astype(BF16)
            du0_ref[:, _hsl(h)] = du
            dwk_ref[:, _hsl(h)] = -_dot(dub, sb, NT)
            dqd_ref[:, _hsl(h)] = _dot(dob, sb, NT)
            dkd_ref[:, _hsl(h)] = _dot(ub, dsb, NT)
            dqk_ref[:, _hsl(h)] = _dot(dob, ub, NT)
            dgl = jnp.sum(jnp.sum(s * dsn, axis=1, keepdims=True), axis=0, keepdims=True)
            dgl_ref[h * SUBLANES:(h + 1) * SUBLANES, :] = jnp.where(corner, dgl, 0.0)
            ds_ref[h] = gl_ref[pl.ds(h * SUBLANES, 1), :] * dsn + _dot(qdb, dob, TN) - _dot(wkb, dub, TN)

    blk = pl.BlockSpec((CHUNK, nh * hd), lambda i: (n - 1 - i, 0))
    gl_spec = pl.BlockSpec((nh * SUBLANES, LANES), lambda i: (n - 1 - i, 0))
    return pl.pallas_call(
        body, grid=(n,),
        in_specs=[blk] * 6 + [gl_spec, pl.BlockSpec((None, nh, hd, hd), lambda i: (n - 1 - i, 0, 0, 0))],
        out_specs=[blk] * 5 + [gl_spec],
        out_shape=[jax.ShapeDtypeStruct((t, nh * hd), F32)] * 5 + [jax.ShapeDtypeStruct((n * nh * SUBLANES, LANES), F32)],
        scratch_shapes=[pltpu.VMEM((nh, hd, hd), F32)], name=name,
        compiler_params=_params(("arbitrary",)),
    )(do, u0, wk, qd, kd, qk, gl, s_in)


N_PAIRS = N_HEADS_B // 2
PAIRS_PER_KV = N_PAIRS // N_KV_B


def _psl(j):
    return slice(j * LANES, (j + 1) * LANES)


def _att_fn(qp, kc, kp, vc, vp, sinks, prf, first):
    w = WINDOW
    lane = lax.broadcasted_iota(jnp.int32, (1, LANES), 1)
    lo = (lane < HEAD_DIM_B).astype(F32)
    qi = lax.broadcasted_iota(jnp.int32, (w, w), 0)
    kj = lax.broadcasted_iota(jnp.int32, (w, w), 1)
    dist_c = (qi - kj).astype(F32)
    valid_c = kj <= qi
    valid_p = (kj > qi) & (first < 0.5)
    kcb, kpb, vcb, vpb = (a.astype(BF16) for a in (kc, kp, vc, vp))
    out = jnp.zeros((w, LANES), F32)
    for half, hmask in ((0, lo), (1, 1.0 - lo)):
        hd = 2.0 * prf + half
        slope = jnp.exp(-(hd + 1.0) * (8.0 / N_HEADS_B * math.log(2.0)))
        sink = jnp.sum(sinks * (lane.astype(F32) == hd).astype(F32), axis=1, keepdims=True)
        qh = (qp * hmask).astype(BF16)
        lc = jnp.where(valid_c, _dot(qh, kcb, NT) * (HEAD_DIM_B ** -0.5) - slope * dist_c, NEG)
        lp = jnp.where(valid_p, _dot(qh, kpb, NT) * (HEAD_DIM_B ** -0.5) - slope * (dist_c + w), NEG)
        m = jnp.maximum(jnp.maximum(jnp.max(lc, axis=1, keepdims=True), jnp.max(lp, axis=1, keepdims=True)), sink)
        m = lax.stop_gradient(m)
        ec, ep = jnp.exp(lc - m), jnp.exp(lp - m)
        inv = 1.0 / (jnp.sum(ec, axis=1, keepdims=True) + jnp.sum(ep, axis=1, keepdims=True) + jnp.exp(sink - m))
        oh = _dot((ec * inv).astype(BF16), vcb) + _dot((ep * inv).astype(BF16), vpb)
        out = out + oh * hmask
    return out


def _scalar11(v):
    return jnp.full((1, 1), v, F32)


def swa_fwd(qsrc, kd, vd, sinks, *, name):
    t = kd.shape[0]
    nb = t // WINDOW

    def body(q_ref, kc_ref, kp_ref, vc_ref, vp_ref, s_ref, o_ref):
        first = _scalar11((pl.program_id(0) == 0).astype(F32))
        kc, kp, vc, vp, sk = kc_ref[...], kp_ref[...], vc_ref[...], vp_ref[...], s_ref[...]
        for j in range(PAIRS_PER_KV):
            prf = _scalar11((pl.program_id(1) * PAIRS_PER_KV + j).astype(F32))
            o_ref[:, _psl(j)] = _att_fn(q_ref[:, _psl(j)], kc, kp, vc, vp, sk, prf, first).astype(o_ref.dtype)

    cur = pl.BlockSpec((WINDOW, LANES), lambda i, kv: (i, kv))
    prev = pl.BlockSpec((WINDOW, LANES), lambda i, kv: (jnp.maximum(i - 1, 0), kv))
    qs = pl.BlockSpec((WINDOW, PAIRS_PER_KV * LANES), lambda i, kv: (i, kv))
    return pl.pallas_call(
        body, grid=(nb, N_KV_B),
        in_specs=[qs, cur, prev, cur, prev, pl.BlockSpec((1, LANES), lambda i, kv: (0, 0))],
        out_specs=qs, out_shape=jax.ShapeDtypeStruct((t, N_PAIRS * LANES), BF16), name=name,
        compiler_params=_params(("parallel", "parallel")),
    )(qsrc, kd, kd, vd, vd, sinks)


def swa_bwd(do, qsrc, kd, vd, sinks, *, name):
    t = kd.shape[0]
    nb = t // WINDOW

    def body(do_ref, q_ref, kc_ref, kp_ref, vc_ref, vp_ref, s_ref, dq_ref, dk_ref, dv_ref, ds_ref, carry_k, carry_v):
        step, kv = pl.program_id(0), pl.program_id(1)
        first = _scalar11((step == nb - 1).astype(F32))

        @pl.when((step == 0) & (kv == 0))
        def _():
            carry_k[...] = jnp.zeros_like(carry_k)
            carry_v[...] = jnp.zeros_like(carry_v)
            ds_ref[...] = jnp.zeros_like(ds_ref)

        kc, kp, vc, vp, sk = kc_ref[...], kp_ref[...], vc_ref[...], vp_ref[...], s_ref[...]
        tot = None
        for j in range(PAIRS_PER_KV):
            prf = _scalar11((kv * PAIRS_PER_KV + j).astype(F32))
            _, vjp = jax.vjp(lambda *a, prf=prf: _att_fn(*a, prf, first), q_ref[:, _psl(j)], kc, kp, vc, vp, sk)
            dq, *rest = vjp(do_ref[:, _psl(j)].astype(F32))
            dq_ref[:, _psl(j)] = dq
            tot = rest if tot is None else [a + b for a, b in zip(tot, rest)]
        dkc, dkp, dvc, dvp, dsk = tot
        ds_ref[...] += dsk
        fold = lambda g: g + pltpu.roll(g, HEAD_DIM_B, 1)
        dk_ref[...] = fold(dkc + carry_k[kv])
        dv_ref[...] = fold(dvc + carry_v[kv])
        carry_k[kv] = dkp
        carry_v[kv] = dvp

    rev = lambda i: nb - 1 - i
    cur = pl.BlockSpec((WINDOW, LANES), lambda i, kv: (rev(i), kv))
    prev = pl.BlockSpec((WINDOW, LANES), lambda i, kv: (jnp.maximum(rev(i) - 1, 0), kv))
    qs = pl.BlockSpec((WINDOW, PAIRS_PER_KV * LANES), lambda i, kv: (rev(i), kv))
    sk = pl.BlockSpec((1, LANES), lambda i, kv: (0, 0))
    return pl.pallas_call(
        body, grid=(nb, N_KV_B),
        in_specs=[qs, qs, cur, prev, cur, prev, sk],
        out_specs=[qs, cur, cur, sk],
        out_shape=[jax.ShapeDtypeStruct((t, N_PAIRS * LANES), F32), jax.ShapeDtypeStruct((t, N_KV_B * LANES), F32),
                   jax.ShapeDtypeStruct((t, N_KV_B * LANES), F32), jax.ShapeDtypeStruct((1, LANES), F32)],
        scratch_shapes=[pltpu.VMEM((N_KV_B, WINDOW, LANES), F32), pltpu.VMEM((N_KV_B, WINDOW, LANES), F32)],
        name=name, compiler_params=_params(("arbitrary", "arbitrary")),
    )(do, qsrc, kd, kd, vd, vd, sinks)


def loss_head(h, tgt, w, *, name):
    t, d = h.shape
    tm = min(256, t)

    def body(h_ref, t_ref, w_ref, dh_ref, dw_ref, l_ref):
        tg = t_ref[...]

        def f(hv, wv):
            err = _f_norm(hv, wv) - tg
            return 0.5 * jnp.sum(jnp.sum(err * err, axis=1, keepdims=True), axis=0, keepdims=True) * (1.0 / d)

        lv, vjp = jax.vjp(f, h_ref[...], w_ref[...])
        dh, dw = vjp(jnp.ones((1, 1), F32))
        dh_ref[...] = dh
        first = pl.program_id(0) == 0

        @pl.when(first)
        def _():
            dw_ref[...] = dw
            l_ref[...] = lv * jnp.ones((1, LANES), F32)

        @pl.when(jnp.logical_not(first))
        def _():
            dw_ref[...] += dw
            l_ref[...] += lv * jnp.ones((1, LANES), F32)

    rows = pl.BlockSpec((tm, d), lambda i: (i, 0))
    one = lambda c: pl.BlockSpec((1, c), lambda i: (0, 0))
    return pl.pallas_call(
        body, grid=(t // tm,), in_specs=[rows, rows, one(d)], out_specs=[rows, one(d), one(LANES)],
        out_shape=[jax.ShapeDtypeStruct((t, d), F32), jax.ShapeDtypeStruct((1, d), F32), jax.ShapeDtypeStruct((1, LANES), F32)],
        name=name, compiler_params=_params(("arbitrary",)),
    )(h, tgt, w)


def adamw(w, g, m, v, *, name):
    r, c = w.shape
    tr = r
    if r % SUBLANES == 0:
        for cand in range(SUBLANES, min(r, 256) + 1, SUBLANES):
            if r % cand == 0:
                tr = cand

    def body(w_ref, g_ref, m_ref, v_ref, d_ref, mo_ref, vo_ref):
        gv = g_ref[...]
        mn = ADAM_B1 * m_ref[...] + (1.0 - ADAM_B1) * gv
        vn = ADAM_B2 * v_ref[...] + (1.0 - ADAM_B2) * jnp.square(gv)
        m_hat = mn / (1.0 - ADAM_B1 ** ADAM_STEP)
        v_hat = vn / (1.0 - ADAM_B2 ** ADAM_STEP)
        d_ref[...] = -ADAM_LR * (m_hat / (jnp.sqrt(v_hat) + ADAM_EPS) + ADAM_WD * w_ref[...])
        mo_ref[...] = mn
        vo_ref[...] = vn

    spec = pl.BlockSpec((tr, c), lambda i: (i, 0))
    return pl.pallas_call(
        body, grid=(r // tr,), in_specs=[spec] * 4, out_specs=[spec] * 3,
        out_shape=[jax.ShapeDtypeStruct((r, c), F32)] * 3, name=name, compiler_params=_params(("parallel",)),
    )(w, g, m, v)


def _place():
    return lax.axis_index("x"), lax.axis_index("y"), lax.axis_index("c")


def allgather8(blk, *, name):
    def body(x_ref, out_ref, send_sems, recv_sems, local_sem):
        x, y, c = _place()
        me = 4 * x + 2 * y + c
        mine = pltpu.make_async_copy(x_ref, out_ref.at[me], local_sem)
        mine.start()
        sent = []
        for k in range(1, N_DEV):
            to = (x ^ ((k >> 2) & 1), y ^ ((k >> 1) & 1), c ^ (k & 1))
            cp = pltpu.make_async_remote_copy(src_ref=x_ref, dst_ref=out_ref.at[me], send_sem=send_sems.at[k - 1],
                                              recv_sem=recv_sems.at[k - 1], device_id=to, device_id_type=MESH)
            cp.start()
            sent.append(cp)
        for k in range(1, N_DEV):
            frm = me ^ k
            pltpu.make_async_remote_copy(src_ref=x_ref, dst_ref=out_ref.at[frm], send_sem=send_sems.at[k - 1],
                                         recv_sem=recv_sems.at[k - 1], device_id=(x, y, c), device_id_type=MESH).wait_recv()
        for cp in sent:
            cp.wait_send()
        mine.wait()

    vm = pl.BlockSpec(memory_space=pltpu.VMEM)
    return pl.pallas_call(
        body, in_specs=[vm], out_specs=vm, out_shape=jax.ShapeDtypeStruct((N_DEV,) + blk.shape, blk.dtype), name=name,
        scratch_shapes=[pltpu.SemaphoreType.DMA((N_DEV - 1,)), pltpu.SemaphoreType.DMA((N_DEV - 1,)), pltpu.SemaphoreType.DMA],
    )(blk)


def _other_chips(x, y):
    return [(1 - x, y), (x, 1 - y), (1 - x, 1 - y)]


def allgather_shards(packed, *, name):
    r = packed.shape[0]
    rh = r // 2

    def body(in_ref, out_ref, send_sems, recv_sems):
        x, y, c = _place()
        me_chip = 2 * x + y
        sib = (x, y, 1 - c)
        chips = _other_chips(x, y)

        def rows(chip, half):
            return out_ref.at[chip, pl.ds(pl.multiple_of(half * rh, BF16_ROWS), rh), :]

        def copy(k, src, dst, to):
            return pltpu.make_async_remote_copy(src_ref=src, dst_ref=dst, send_sem=send_sems.at[k], recv_sem=recv_sems.at[k],
                                                device_id=to, device_id_type=MESH)

        my_half = in_ref.at[pl.ds(pl.multiple_of(c * rh, BF16_ROWS), rh), :]
        first = [copy(j, my_half, rows(me_chip, c), (cx, cy, c)) for j, (cx, cy) in enumerate(chips)]
        for cp in first:
            cp.start()
        passed = [copy(3 + j, rows(2 * cx + cy, c), rows(2 * cx + cy, c), sib) for j, (cx, cy) in enumerate(chips)]
        for j, (cx, cy) in enumerate(chips):
            copy(j, my_half, rows(2 * cx + cy, c), sib).wait_recv()
            passed[j].start()
        for j, (cx, cy) in enumerate(chips):
            copy(3 + j, my_half, rows(2 * cx + cy, 1 - c), sib).wait_recv()
        for cp in first + passed:
            cp.wait_send()

    hbm = pl.BlockSpec(memory_space=pl.ANY)
    return pl.pallas_call(
        body, in_specs=[hbm], out_specs=hbm, out_shape=jax.ShapeDtypeStruct((N_CHIPS, r, PACK_COLS), packed.dtype), name=name,
        scratch_shapes=[pltpu.SemaphoreType.DMA((6,)), pltpu.SemaphoreType.DMA((6,))],
    )(packed)


def swap_halves(g, *, name):
    r = g.shape[1]
    rh = r // 2

    def body(g_ref, out_ref, send_sem, recv_sem):
        x, y, c = _place()
        cp = pltpu.make_async_remote_copy(src_ref=g_ref.at[:, pl.ds(pl.multiple_of((1 - c) * rh, BF16_ROWS), rh), :], dst_ref=out_ref,
                                          send_sem=send_sem, recv_sem=recv_sem, device_id=(x, y, 1 - c), device_id_type=MESH)
        cp.start()
        cp.wait()

    hbm = pl.BlockSpec(memory_space=pl.ANY)
    return pl.pallas_call(
        body, in_specs=[hbm], out_specs=hbm, out_shape=jax.ShapeDtypeStruct((N_CHIPS, rh, PACK_COLS), g.dtype), name=name,
        scratch_shapes=[pltpu.SemaphoreType.DMA, pltpu.SemaphoreType.DMA],
    )(g)


def scatter_chips(hp, *, name):
    rh = hp.shape[1]

    def body(h_ref, out_ref, send_sems, recv_sems):
        x, y, c = _place()
        chips = _other_chips(x, y)
        cps = [pltpu.make_async_remote_copy(src_ref=h_ref.at[2 * cx + cy], dst_ref=out_ref.at[j], send_sem=send_sems.at[j],
                                            recv_sem=recv_sems.at[j], device_id=(cx, cy, c), device_id_type=MESH)
               for j, (cx, cy) in enumerate(chips)]
        for cp in cps:
            cp.start()
        for cp in cps:
            cp.wait()

    hbm = pl.BlockSpec(memory_space=pl.ANY)
    return pl.pallas_call(
        body, in_specs=[hbm], out_specs=hbm, out_shape=jax.ShapeDtypeStruct((3, rh, PACK_COLS), hp.dtype), name=name,
        scratch_shapes=[pltpu.SemaphoreType.DMA((3,)), pltpu.SemaphoreType.DMA((3,))],
    )(hp)


def join_halves(half, *, name):
    def body(h_ref, out_ref, send_sem, recv_sem):
        x, y, c = _place()
        cp = pltpu.make_async_remote_copy(src_ref=h_ref, dst_ref=out_ref, send_sem=send_sem, recv_sem=recv_sem,
                                          device_id=(x, y, 1 - c), device_id_type=MESH)
        cp.start()
        cp.wait()

    hbm = pl.BlockSpec(memory_space=pl.ANY)
    return pl.pallas_call(
        body, in_specs=[hbm], out_specs=hbm, out_shape=jax.ShapeDtypeStruct(half.shape, half.dtype), name=name,
        scratch_shapes=[pltpu.SemaphoreType.DMA, pltpu.SemaphoreType.DMA],
    )(half)


def add_parts(parts, *, out_dtype, name):
    rows = parts[0].shape[0]
    tr = rows
    for cand in range(BF16_ROWS, min(rows, 1200) + 1, BF16_ROWS):
        if rows % cand == 0:
            tr = cand

    def body(*refs):
        acc = refs[0][...].astype(F32)
        for r in refs[1:-1]:
            acc = acc + r[...].astype(F32)
        refs[-1][...] = acc.astype(refs[-1].dtype)

    spec = pl.BlockSpec((tr, PACK_COLS), lambda i: (i, 0))
    return pl.pallas_call(
        body, grid=(rows // tr,), in_specs=[spec] * len(parts), out_specs=spec,
        out_shape=jax.ShapeDtypeStruct((rows, PACK_COLS), out_dtype), name=name, compiler_params=_params(("parallel",)),
    )(*parts)


def sum8(g, *, name):
    def body(g_ref, o_ref):
        acc = g_ref[0]
        for d in range(1, N_DEV):
            acc = acc + g_ref[d]
        o_ref[...] = acc

    return pl.pallas_call(body, out_shape=jax.ShapeDtypeStruct(g.shape[1:], F32), name=name)(g)


def _dup_halves(a):
    t = a.shape[0]
    a = a.reshape(t, N_KV_B, HEAD_DIM_B)
    return jnp.concatenate([a, a], axis=-1).reshape(t, N_KV_B * LANES)


def _undup(a):
    t = a.shape[0]
    return a.reshape(t, N_KV_B, LANES)[:, :, :HEAD_DIM_B].reshape(t, N_KV_B * HEAD_DIM_B)


def _lane_pad(v, offset=0):
    return jnp.zeros((1, LANES), F32).at[0, offset:offset + v.shape[0]].set(v)


def local_step(x, p, tgt, sm, wt):
    t = x.shape[0]
    rtm = min(256, t)
    hk = N_HEADS_A * HEAD_DIM_A
    qd_b = N_HEADS_B * HEAD_DIM_B
    kd_b = N_KV_B * HEAD_DIM_B
    gw, gs = {}, {}
    norm = lambda h, w, nm: tile_map(_f_norm, [(h, D_MODEL, 0)], [w], [(D_MODEL, BF16)], tm=rtm, ncol=1, name=nm)[0]

    def norm_bwd(h, w, dy, add, nm):
        (dh,), (dw,) = tile_vjp(_f_norm, [(h, D_MODEL, 0)], [w], [(dy, D_MODEL, 0)], n_diff=1, tm=rtm, ncol=1, name=nm,
                                add=(add, D_MODEL, 0))
        return dh, dw

    p_bf = p.astype(BF16)
    a_main, a_tail = wt["a_w_in"][:, :4 * hk], jnp.pad(wt["a_w_in"][:, 4 * hk:], ((0, 0), (0, LANES - 2 * N_HEADS_A)))
    alog_p = _lane_pad(sm["a_log"][0], N_HEADS_A)
    dtb_p = _lane_pad(sm["a_dt_bias"][0], N_HEADS_A)
    sinks_p = _lane_pad(sm["b_sinks"][0])
    nw = lambda name, i: sm[name][i:i + 1]

    saved = []
    h = x
    for i in range(DEPTH):
        s = {"h0": h}
        s["hn"] = norm(h, nw("norm_mix", i), f"norm_mix{i}")
        if i % 2 == 0:
            s["pm"] = mm(s["hn"], a_main, name="a_in_main")
            s["pt"] = mm(s["hn"], a_tail, name="a_in_tail")
            s["c"] = conv_fwd(s["pm"], wt["a_conv"], name="a_conv")
            s["bg"] = tile_map(_f_betag, [(s["pt"], LANES, 0)], [alog_p, dtb_p], [(LANES, F32)], tm=rtm, ncol=1, name="a_betag")[0]
            s["prep"] = delta_prep(s["c"], s["bg"], name="a_prep")
            s["o"], s["s_in"] = delta_scan(*s["prep"], name="a_scan")
            s["on"] = tile_map(_f_gnorm, [(s["o"], HEAD_DIM_A, 0), (s["pm"], HEAD_DIM_A, 3 * N_HEADS_A)], [sm["a_norm"]],
                               [(HEAD_DIM_A, BF16)], tm=rtm, ncol=N_HEADS_A, name="a_gnorm")[0]
            h = mm(s["on"], wt["a_w_out"], add=h, name="a_out")
        else:
            s["pb"] = mm(s["hn"], wt["b_w_in"], name="b_in")
            s["kd"], s["vd"] = _dup_halves(s["pb"][:, qd_b:qd_b + kd_b]), _dup_halves(s["pb"][:, qd_b + kd_b:])
            s["ao"] = swa_fwd(s["pb"], s["kd"], s["vd"], sinks_p, name="b_att")
            h = mm(s["ao"], wt["b_w_out"], add=h, name="b_out")
        s["h1"] = h
        s["hf"] = norm(h, nw("norm_ffn", i), f"norm_ffn{i}")
        s["u"] = mm(s["hf"], wt["f_w_up"][i], name=f"f_up{i}")
        s["uc"] = conv_fwd(s["u"], wt["f_conv"][i], name=f"f_conv{i}")
        nff = D_FF // 1408
        s["act"] = tile_map(_f_act, [(s["uc"], 1408, 0), (s["uc"], 1408, nff)], [], [(1408, BF16)], tm=rtm, ncol=nff, name=f"f_act{i}")[0]
        h = mm(s["act"], wt["f_w_down"][i], add=h, name=f"f_down{i}")
        s["h2"] = h
        s["hp"] = norm(h, nw("norm_ple", i), f"norm_ple{i}")
        s["gl"] = mm(s["hp"], wt["ple_w_gate"][i], name=f"ple_gate{i}")
        s["pe"] = mm(p_bf[i], wt["ple_w_proj"][i], name=f"ple_proj{i}")
        h = tile_map(lambda hv, g, e: hv + _f_ple(g, e), [(h, D_MODEL, 0), (s["gl"], D_MODEL, 0), (s["pe"], D_MODEL, 0)], [],
                     [(D_MODEL, F32)], tm=rtm, ncol=1, name=f"ple_mix{i}")[0]
        saved.append(s)

    dh, gnf, loss = loss_head(h, tgt, sm["norm_final"][None, :], name="loss_head")
    gs["norm_final"] = gnf[0]

    g_mix, g_ffn, g_ple = [None] * DEPTH, [None] * DEPTH, [None] * DEPTH
    g_up, g_conv, g_down, g_proj, g_gate = ([None] * DEPTH for _ in range(5))
    for i in reversed(range(DEPTH)):
        s = saved[i]
        (dgl, dpe), _ = tile_vjp(_f_ple, [(s["gl"], D_MODEL, 0), (s["pe"], D_MODEL, 0)], [], [(dh, D_MODEL, 0)], n_diff=2,
                                 tm=rtm, ncol=1, name=f"ple_mix_bwd{i}")
        g_proj[i] = mm(p_bf[i], dpe, ta=True, name=f"ple_proj_dw{i}")
        g_gate[i] = mm(s["hp"], dgl, ta=True, name=f"ple_gate_dw{i}")
        dhp = mm(dgl, wt["ple_w_gate"][i], tb=True, name=f"ple_gate_dx{i}")
        dh, g_ple[i] = norm_bwd(s["h2"], nw("norm_ple", i), dhp, dh, f"norm_ple_bwd{i}")

        dact = mm(dh, wt["f_w_down"][i], tb=True, name=f"f_down_dx{i}")
        g_down[i] = mm(s["act"], dh, ta=True, name=f"f_down_dw{i}")
        nff = D_FF // 1408
        (dgate, dval), _ = tile_vjp(_f_act, [(s["uc"], 1408, 0), (s["uc"], 1408, nff)], [], [(dact, 1408, 0)], n_diff=2,
                                    tm=rtm, ncol=nff, name=f"f_act_bwd{i}")
        du, g_conv[i] = conv_bwd(jnp.concatenate([dgate, dval], axis=1), s["u"], wt["f_conv"][i], name=f"f_conv_bwd{i}")
        g_up[i] = mm(s["hf"], du, ta=True, name=f"f_up_dw{i}")
        dhf = mm(du, wt["f_w_up"][i], tb=True, name=f"f_up_dx{i}")
        dh, g_ffn[i] = norm_bwd(s["h1"], nw("norm_ffn", i), dhf, dh, f"norm_ffn_bwd{i}")

        if i % 2 == 0:
            don = mm(dh, wt["a_w_out"], tb=True, name="a_out_dx")
            gw["a_w_out"] = mm(s["on"], dh, ta=True, name="a_out_dw")
            (do, dz), (gan,) = tile_vjp(_f_gnorm, [(s["o"], HEAD_DIM_A, 0), (s["pm"], HEAD_DIM_A, 3 * N_HEADS_A)], [sm["a_norm"]],
                                        [(don, HEAD_DIM_A, 0)], n_diff=2, tm=rtm, ncol=N_HEADS_A, name="a_gnorm_bwd")
            gs["a_norm"] = gan
            dprep = delta_scan_bwd(do, *s["prep"], s["s_in"], name="a_scan_bwd")
            dcq, dck, dcv, dbg = delta_prep_bwd(s["c"], s["bg"], dprep, name="a_prep_bwd")
            (dpt,), (galog, gdtb) = tile_vjp(_f_betag, [(s["pt"], LANES, 0)], [alog_p, dtb_p], [(dbg, LANES, 0)], n_diff=1,
                                             tm=rtm, ncol=1, name="a_betag_bwd")
            gs["a_log"] = galog[:, N_HEADS_A:2 * N_HEADS_A]
            gs["a_dt_bias"] = gdtb[:, N_HEADS_A:2 * N_HEADS_A]
            dqkv, gw["a_conv"] = conv_bwd(jnp.concatenate([dcq, dck, dcv], axis=1), s["pm"], wt["a_conv"], name="a_conv_bwd")
            dpm = jnp.concatenate([dqkv, dz], axis=1)
            dhn = mm(dpm, a_main, tb=True, name="a_in_main_dx")
            dhn = mm(dpt, a_tail, tb=True, add=dhn, name="a_in_tail_dx")
            g_main = mm(s["hn"], dpm, ta=True, name="a_in_main_dw")
            g_tail = mm(s["hn"], dpt, ta=True, name="a_in_tail_dw")
            gw["a_w_in"] = jnp.concatenate([g_main, g_tail[:, :2 * N_HEADS_A]], axis=1)
        else:
            dao = mm(dh, wt["b_w_out"], tb=True, name="b_out_dx")
            gw["b_w_out"] = mm(s["ao"], dh, ta=True, name="b_out_dw")
            dq, dkd, dvd, gsk = swa_bwd(dao, s["pb"], s["kd"], s["vd"], sinks_p, name="b_att_bwd")
            gs["b_sinks"] = gsk[:, :N_HEADS_B]
            dpb = jnp.concatenate([dq, _undup(dkd), _undup(dvd)], axis=1)
            dhn = mm(dpb, wt["b_w_in"], tb=True, name="b_in_dx")
            gw["b_w_in"] = mm(s["hn"], dpb, ta=True, name="b_in_dw")
        dh, g_mix[i] = norm_bwd(s["h0"], nw("norm_mix", i), dhn, dh, f"norm_mix_bwd{i}")

    gs["norm_mix"], gs["norm_ffn"], gs["norm_ple"] = (jnp.concatenate(g, axis=0) for g in (g_mix, g_ffn, g_ple))
    gw["f_w_up"], gw["f_conv"], gw["f_w_down"] = jnp.stack(g_up), jnp.stack(g_conv), jnp.stack(g_down)
    gw["ple_w_proj"], gw["ple_w_gate"] = jnp.stack(g_proj), jnp.stack(g_gate)
    return loss, dh, gw, gs


BIG = [("a_w_in", 2), ("a_w_out", 1), ("b_w_in", 2), ("b_w_out", 1), ("f_w_up", 2), ("f_w_down", 1),
       ("ple_w_proj", 2), ("ple_w_gate", 1)]
CONVS = [("a_conv", 2), ("f_conv", 2)]
SMALL = ["norm_mix", "norm_ffn", "norm_ple", "norm_final", "a_log", "a_dt_bias", "a_norm", "b_sinks"]
SMALL_ROWS = 8
CONV_ROWS = 16


def _pack_rows(arrs, rows, dtype):
    flat = jnp.concatenate([a.reshape(-1).astype(dtype) for a in arrs])
    return jnp.pad(flat, (0, rows * PACK_COLS - flat.shape[0])).reshape(rows, PACK_COLS)


def _unpack(flat, shapes):
    out, off = [], 0
    for shp in shapes:
        n = math.prod(shp)
        out.append(flat[off:off + n].reshape(shp))
        off += n
    return out


def _pack_small(d, loss=None):
    tail = jnp.concatenate([d["a_log"].reshape(-1), d["a_dt_bias"].reshape(-1), d["a_norm"].reshape(-1), d["b_sinks"].reshape(-1)])
    if loss is not None:
        tail = jnp.concatenate([tail, loss.reshape(-1)[:1]])
    tail = jnp.pad(tail, (0, PACK_COLS - tail.shape[0]))
    return jnp.concatenate([d["norm_mix"], d["norm_ffn"], d["norm_ple"], d["norm_final"][None, :], tail[None, :]], axis=0)


def _unpack_small(a, like):
    out = {"norm_mix": a[0:2], "norm_ffn": a[2:4], "norm_ple": a[4:6], "norm_final": a[6]}
    off = 0
    for nm in ("a_log", "a_dt_bias", "a_norm", "b_sinks"):
        n = like[nm].size
        out[nm] = a[7, off:off + n].reshape(like[nm].shape)
        off += n
    return out, a[7, off]


def _shard_rows(shapes):
    n = sum(math.prod(s) for s in shapes)
    rows = -(-n // PACK_COLS)
    return -(-rows // (2 * BF16_ROWS)) * (2 * BF16_ROWS)


def _as2d(a):
    return a.reshape(-1, a.shape[-1])


def kernel(x, p, norm_mix, norm_ffn, norm_ple, norm_final, a_w_in, a_conv, a_log, a_dt_bias, a_norm, a_w_out, b_w_in, b_sinks, b_w_out, f_w_up, f_conv, f_w_down, ple_w_proj, ple_w_gate, loss_target, m_norm_mix, m_norm_ffn, m_norm_ple, m_norm_final, m_a_w_in, m_a_conv, m_a_log, m_a_dt_bias, m_a_norm, m_a_w_out, m_b_w_in, m_b_sinks, m_b_w_out, m_f_w_up, m_f_conv, m_f_w_down, m_ple_w_proj, m_ple_w_gate, v_norm_mix, v_norm_ffn, v_norm_ple, v_norm_final, v_a_w_in, v_a_conv, v_a_log, v_a_dt_bias, v_a_norm, v_a_w_out, v_b_w_in, v_b_sinks, v_b_w_out, v_f_w_up, v_f_conv, v_f_w_down, v_ple_w_proj, v_ple_w_gate):
    w = dict(norm_mix=norm_mix, norm_ffn=norm_ffn, norm_ple=norm_ple, norm_final=norm_final, a_w_in=a_w_in, a_conv=a_conv,
             a_log=a_log, a_dt_bias=a_dt_bias, a_norm=a_norm, a_w_out=a_w_out, b_w_in=b_w_in, b_sinks=b_sinks, b_w_out=b_w_out,
             f_w_up=f_w_up, f_conv=f_conv, f_w_down=f_w_down, ple_w_proj=ple_w_proj, ple_w_gate=ple_w_gate)
    m = dict(norm_mix=m_norm_mix, norm_ffn=m_norm_ffn, norm_ple=m_norm_ple, norm_final=m_norm_final, a_w_in=m_a_w_in,
             a_conv=m_a_conv, a_log=m_a_log, a_dt_bias=m_a_dt_bias, a_norm=m_a_norm, a_w_out=m_a_w_out, b_w_in=m_b_w_in,
             b_sinks=m_b_sinks, b_w_out=m_b_w_out, f_w_up=m_f_w_up, f_conv=m_f_conv, f_w_down=m_f_w_down,
             ple_w_proj=m_ple_w_proj, ple_w_gate=m_ple_w_gate)
    v = dict(norm_mix=v_norm_mix, norm_ffn=v_norm_ffn, norm_ple=v_norm_ple, norm_final=v_norm_final, a_w_in=v_a_w_in,
             a_conv=v_a_conv, a_log=v_a_log, a_dt_bias=v_a_dt_bias, a_norm=v_a_norm, a_w_out=v_a_w_out, b_w_in=v_b_w_in,
             b_sinks=v_b_sinks, b_w_out=v_b_w_out, f_w_up=v_f_w_up, f_conv=v_f_conv, f_w_down=v_f_w_down,
             ple_w_proj=v_ple_w_proj, ple_w_gate=v_ple_w_gate)
    xc, yc, cc = _place()
    my_chip = 2 * xc + yc

    big_shapes = [w[n].shape for n, _ in BIG]
    rows_big = _shard_rows(big_shapes)
    packed = _pack_rows([w[n] for n, _ in BIG], rows_big, BF16)
    gathered = lax.dynamic_update_index_in_dim(allgather_shards(packed, name="gather_weights"), packed, my_chip, 0)
    conv_shapes = [w[n].shape for n, _ in CONVS]
    convs = allgather8(_pack_rows([w[n] for n, _ in CONVS], CONV_ROWS, F32), name="gather_convs")
    full = {}
    for src, names, shapes, pick in ((gathered, BIG, big_shapes, lambda j: j), (convs, CONVS, conv_shapes, lambda j: 2 * j)):
        parts = [_unpack(src[pick(j)].reshape(-1), shapes) for j in range(N_CHIPS)]
        for q, (nm, ax) in enumerate(names):
            full[nm] = jnp.concatenate([parts[j][q] for j in range(N_CHIPS)], axis=ax)
    wt = {"a_w_in": full["a_w_in"][0], "a_conv": full["a_conv"][0], "a_w_out": full["a_w_out"][0], "b_w_in": full["b_w_in"][0],
          "b_w_out": full["b_w_out"][0], "f_w_up": full["f_w_up"], "f_conv": full["f_conv"], "f_w_down": full["f_w_down"],
          "ple_w_proj": full["ple_w_proj"], "ple_w_gate": full["ple_w_gate"]}
    sm = {n: w[n] for n in SMALL}

    loss, grad_x, gw, gs = local_step(x[0], p[:, 0], loss_target[0], sm, wt)
    gw = {"a_w_in": gw["a_w_in"][None], "a_conv": gw["a_conv"][None], "a_w_out": gw["a_w_out"][None], "b_w_in": gw["b_w_in"][None],
          "b_w_out": gw["b_w_out"][None], "f_w_up": gw["f_w_up"], "f_conv": gw["f_conv"], "f_w_down": gw["f_w_down"],
          "ple_w_proj": gw["ple_w_proj"], "ple_w_gate": gw["ple_w_gate"]}

    sharded = BIG + CONVS
    sh_shapes = [w[n].shape for n, _ in sharded]
    rows_all = _shard_rows(sh_shapes)
    rh = rows_all // 2

    def dest(j):
        return _pack_rows([lax.slice_in_dim(gw[n], j * w[n].shape[ax], (j + 1) * w[n].shape[ax], axis=ax) for n, ax in sharded],
                          rows_all, BF16)

    gpack = jnp.stack([dest(j) for j in range(N_CHIPS)])
    from_sib = swap_halves(gpack, name="rs_swap")
    mine = lax.dynamic_slice_in_dim(gpack, cc * rh, rh, axis=1)
    pair = add_parts([mine.reshape(N_CHIPS * rh, PACK_COLS), from_sib.reshape(N_CHIPS * rh, PACK_COLS)], out_dtype=BF16,
                     name="rs_add_pair").reshape(N_CHIPS, rh, PACK_COLS)
    others = scatter_chips(pair, name="rs_scatter")
    own = lax.dynamic_index_in_dim(pair, my_chip, axis=0, keepdims=False)
    half = add_parts([own, others[0], others[1], others[2]], out_dtype=F32, name="rs_add_chips")
    other = join_halves(half, name="rs_join")
    gfull = jnp.where(cc == 0, jnp.concatenate([half, other]), jnp.concatenate([other, half]))
    g_sh = dict(zip([n for n, _ in sharded], _unpack(gfull.reshape(-1), sh_shapes)))

    small_sum = sum8(allgather8(_pack_small(gs, loss), name="gather_small"), name="sum_small")
    g_sm, loss_sum = _unpack_small(small_sum, sm)

    grads, delta, new_m, new_v = {}, {}, {}, {}
    for n, _ in sharded:
        d2, m2, v2 = adamw(_as2d(w[n]), _as2d(g_sh[n]), _as2d(m[n]), _as2d(v[n]), name=f"adamw_{n}")
        grads[n], delta[n], new_m[n], new_v[n] = g_sh[n], d2.reshape(w[n].shape), m2.reshape(w[n].shape), v2.reshape(w[n].shape)
    pk = lambda d: _pack_small(d)
    d2, m2, v2 = adamw(pk(sm), pk(g_sm), pk({n: m[n] for n in SMALL}), pk({n: v[n] for n in SMALL}), name="adamw_small")
    for src, dst in ((d2, delta), (m2, new_m), (v2, new_v)):
        dst.update(_unpack_small(src, sm)[0])
    grads.update(g_sm)

    order = ["norm_mix", "norm_ffn", "norm_ple", "norm_final", "a_w_in", "a_conv", "a_log", "a_dt_bias", "a_norm", "a_w_out",
             "b_w_in", "b_sinks", "b_w_out", "f_w_up", "f_conv", "f_w_down", "ple_w_proj", "ple_w_gate"]
    return (loss_sum, grad_x[None], *[grads[n] for n in order], *[delta[n] for n in order],
            *[new_m[n] for n in order], *[new_v[n] for n in order])
```

```python
import functools
import math

import jax
import jax.numpy as jnp
from jax import lax
from jax.experimental import pallas as pl
from jax.experimental.pallas import tpu as pltpu

F32 = jnp.float32
BF16 = jnp.bfloat16
MESH = pl.DeviceIdType.MESH

D_MODEL = 1024
N_HEADS_A = 8
HEAD_DIM_A = 128
CONV_A = 4
N_HEADS_B = 16
N_KV_B = 4
HEAD_DIM_B = 64
WINDOW = 128
D_FF = 2816
FFN_CONV = 3
PLE_DIM = 256
EPS = 1e-6
DEPTH = 2

ADAM_LR = 0.001
ADAM_B1 = 0.9
ADAM_B2 = 0.999
ADAM_EPS = 1e-08
ADAM_WD = 0.01
ADAM_STEP = 10

LANES = 128
SUBLANES = 8
BF16_ROWS = 16
CHUNK = 128
VMEM_LIMIT = 56 * 1024 * 1024
NEG = -1e30
N_CHIPS = 4
N_DEV = 8
PACK_COLS = 1024


def _params(sem=None):
    return pltpu.CompilerParams(dimension_semantics=sem, vmem_limit_bytes=VMEM_LIMIT)


def _tile(dim, cap):
    if dim % LANES:
        return dim
    best = LANES
    for t in range(LANES, min(dim, cap) + 1, LANES):
        if dim % t == 0:
            best = t
    return best


def _dot(a, b, dims=(((1,), (0,)), ((), ())), precision=None):
    return lax.dot_general(a, b, dims, precision=precision, preferred_element_type=F32)


NN = (((1,), (0,)), ((), ()))
NT = (((1,), (1,)), ((), ()))
TN = (((0,), (0,)), ((), ()))


def mm(a, b, *, name, ta=False, tb=False, out_dtype=F32, add=None, tm_cap=512, tn_cap=1408, tk_cap=1408):
    m, k = (a.shape[1], a.shape[0]) if ta else a.shape
    n = b.shape[0] if tb else b.shape[1]
    assert (b.shape[1] if tb else b.shape[0]) == k, (a.shape, b.shape, ta, tb)
    tm, tn, tk = _tile(m, tm_cap), _tile(n, tn_cap), _tile(k, tk_cap)
    nk = k // tk
    dims = (((0 if ta else 1,), (1 if tb else 0,)), ((), ()))
    has_add = add is not None

    def body(*refs):
        a_ref, b_ref = refs[0], refs[1]
        add_ref = refs[2] if has_add else None
        o_ref = refs[3] if has_add else refs[2]
        part = _dot(a_ref[...].astype(BF16), b_ref[...].astype(BF16), dims)
        if nk == 1:
            if has_add:
                part = part + add_ref[...].astype(F32)
            o_ref[...] = part.astype(o_ref.dtype)
            return
        acc = refs[-1]
        kk = pl.program_id(2)

        @pl.when(kk == 0)
        def _():
            acc[...] = part

        @pl.when(kk > 0)
        def _():
            acc[...] += part

        @pl.when(kk == nk - 1)
        def _():
            r = acc[...]
            if has_add:
                r = r + add_ref[...].astype(F32)
            o_ref[...] = r.astype(o_ref.dtype)

    a_spec = pl.BlockSpec((tk, tm), lambda i, j, kk: (kk, i)) if ta else pl.BlockSpec((tm, tk), lambda i, j, kk: (i, kk))
    b_spec = pl.BlockSpec((tn, tk), lambda i, j, kk: (j, kk)) if tb else pl.BlockSpec((tk, tn), lambda i, j, kk: (kk, j))
    o_spec = pl.BlockSpec((tm, tn), lambda i, j, kk: (i, j))
    in_specs = [a_spec, b_spec] + ([o_spec] if has_add else [])
    args = (a, b) + ((add,) if has_add else ())
    return pl.pallas_call(
        body, grid=(m // tm, n // tn, nk), in_specs=in_specs, out_specs=o_spec,
        out_shape=jax.ShapeDtypeStruct((m, n), out_dtype), name=name,
        scratch_shapes=[pltpu.VMEM((tm, tn), F32)] if nk > 1 else [],
        compiler_params=_params(("parallel", "parallel", "arbitrary")),
    )(*args)


def _row_spec(tm, cw, coff):
    return pl.BlockSpec((tm, cw), lambda i, j: (i, j + coff))


def _full_spec(shape):
    return pl.BlockSpec(shape, lambda i, j: (0,) * len(shape))


def tile_map(fn, rows, params, outs, *, tm, ncol, name):
    t = rows[0][0].shape[0]
    nin = len(rows) + len(params)

    def body(*refs):
        res = fn(*[r[...] for r in refs[:nin]])
        res = res if isinstance(res, (tuple, list)) else (res,)
        for o_ref, r in zip(refs[nin:], res):
            o_ref[...] = r.astype(o_ref.dtype)

    in_specs = [_row_spec(tm, cw, coff) for (_, cw, coff) in rows] + [_full_spec(p.shape) for p in params]
    res = pl.pallas_call(
        body, grid=(t // tm, ncol), in_specs=in_specs,
        out_specs=[_row_spec(tm, cw, 0) for (cw, _) in outs],
        out_shape=[jax.ShapeDtypeStruct((t, cw * ncol), dt) for (cw, dt) in outs], name=name,
        compiler_params=_params(("parallel", "parallel")),
    )(*[r[0] for r in rows], *params)
    return res


def tile_vjp(fn, rows, params, cts, *, n_diff, tm, ncol, name, add=None):
    t = rows[0][0].shape[0]
    nr, npar, nct = len(rows), len(params), len(cts)
    has_add = add is not None

    def body(*refs):
        vals = [r[...] for r in refs[:nr + npar + nct + (1 if has_add else 0)]]
        diff, rest, pars = vals[:n_diff], vals[n_diff:nr], vals[nr:nr + npar]
        ctv = vals[nr + npar:nr + npar + nct]
        outs_ref = refs[nr + npar + nct + (1 if has_add else 0):]

        def f(*a):
            res = fn(*a[:n_diff], *rest, *a[n_diff:])
            return tuple(res) if isinstance(res, (tuple, list)) else (res,)

        primal, vjp = jax.vjp(f, *[d.astype(F32) for d in diff], *pars)
        grads = vjp(tuple(c.astype(o.dtype) for c, o in zip(ctv, primal)))
        for q in range(n_diff):
            g = grads[q]
            if has_add and q == 0:
                g = g + vals[-1]
            outs_ref[q][...] = g.astype(outs_ref[q].dtype)
        first = (pl.program_id(0) == 0) & (pl.program_id(1) == 0)
        for q in range(npar):
            o_ref, g = outs_ref[n_diff + q], grads[n_diff + q]

            @pl.when(first)
            def _(o_ref=o_ref, g=g):
                o_ref[...] = g

            @pl.when(jnp.logical_not(first))
            def _(o_ref=o_ref, g=g):
                o_ref[...] += g

    ins = list(rows) + [None] * 0
    in_specs = [_row_spec(tm, cw, coff) for (_, cw, coff) in rows] + [_full_spec(p.shape) for p in params]
    in_specs += [_row_spec(tm, cw, coff) for (_, cw, coff) in cts]
    args = [r[0] for r in rows] + list(params) + [c[0] for c in cts]
    if has_add:
        in_specs.append(_row_spec(tm, add[1], add[2]))
        args.append(add[0])
    out_specs = [_row_spec(tm, rows[q][1], 0) for q in range(n_diff)] + [_full_spec(p.shape) for p in params]
    out_shape = [jax.ShapeDtypeStruct((t, rows[q][1] * ncol), F32) for q in range(n_diff)]
    out_shape += [jax.ShapeDtypeStruct(p.shape, F32) for p in params]
    del ins
    res = pl.pallas_call(
        body, grid=(t // tm, ncol), in_specs=in_specs, out_specs=out_specs, out_shape=out_shape, name=name,
        compiler_params=_params(("arbitrary", "arbitrary")),
    )(*args)
    return res[:n_diff], res[n_diff:]


def _silu(x):
    return x * jax.nn.sigmoid(x)


def _f_norm(h, w):
    return h * lax.rsqrt(jnp.mean(h * h, axis=-1, keepdims=True) + EPS) * w


def _f_gnorm(o, z, w):
    return _f_norm(o, w) * _silu(z)


def _f_act(gate, val):
    return _silu(gate) * val


def _f_ple(gl, pe):
    return jax.nn.sigmoid(gl) * pe


def _f_betag(pt, alog, dtb):
    lane = lax.broadcasted_iota(jnp.int32, (1, LANES), 1)
    z = pt + dtb
    softplus = jnp.maximum(z, 0.0) + jnp.log(1.0 + jnp.exp(-jnp.abs(z)))
    g = -jnp.exp(alog) * softplus
    return jnp.where(lane < N_HEADS_A, jax.nn.sigmoid(pt), jnp.where(lane < 2 * N_HEADS_A, g, 0.0))


CONV_TM = 256
CONV_CW = 512


def _shift_down(x, prev, s, row):
    rp = jnp.tile(pltpu.roll(prev, s, 0), (x.shape[0] // SUBLANES, 1))
    return jnp.where(row < s, rp, pltpu.roll(x, s, 0))


def _shift_up(x, nxt, s, row):
    tm = x.shape[0]
    rn = jnp.tile(pltpu.roll(nxt, SUBLANES - s, 0), (tm // SUBLANES, 1))
    return jnp.where(row >= tm - s, rn, pltpu.roll(x, tm - s, 0))


def conv_fwd(x, w, *, name):
    t = x.shape[0]
    k, c = w.shape
    tm, cw = min(CONV_TM, t), CONV_CW
    nb8 = tm // SUBLANES

    def body(x_ref, p_ref, w_ref, o_ref):
        i = pl.program_id(1)
        xv = x_ref[...]
        prev = jnp.where(i > 0, p_ref[...], 0.0)
        row = lax.broadcasted_iota(jnp.int32, xv.shape, 0)
        y = xv * w_ref[pl.ds(k - 1, 1), :]
        for s in range(1, k):
            y = y + _shift_down(xv, prev, s, row) * w_ref[pl.ds(k - 1 - s, 1), :]
        o_ref[...] = y

    return pl.pallas_call(
        body, grid=(c // cw, t // tm),
        in_specs=[pl.BlockSpec((tm, cw), lambda j, i: (i, j)),
                  pl.BlockSpec((SUBLANES, cw), lambda j, i: (jnp.maximum(i * nb8 - 1, 0), j)),
                  pl.BlockSpec((k, cw), lambda j, i: (0, j))],
        out_specs=pl.BlockSpec((tm, cw), lambda j, i: (i, j)),
        out_shape=jax.ShapeDtypeStruct((t, c), F32), name=name,
        compiler_params=_params(("parallel", "parallel")),
    )(x, x, w)


def conv_bwd(dy, x, w, *, name):
    t = x.shape[0]
    k, c = w.shape
    tm, cw = min(CONV_TM, t), CONV_CW
    nb8 = tm // SUBLANES
    ni = t // tm

    def body(dy_ref, dn_ref, x_ref, p_ref, w_ref, dx_ref, dw_ref):
        i = pl.program_id(1)
        dyv, xv = dy_ref[...], x_ref[...]
        nxt = jnp.where(i < ni - 1, dn_ref[...], 0.0)
        prev = jnp.where(i > 0, p_ref[...], 0.0)
        row = lax.broadcasted_iota(jnp.int32, xv.shape, 0)
        dx = dyv * w_ref[pl.ds(k - 1, 1), :]
        dws = [jnp.sum(dyv * xv, axis=0, keepdims=True)]
        for s in range(1, k):
            dx = dx + _shift_up(dyv, nxt, s, row) * w_ref[pl.ds(k - 1 - s, 1), :]
            dws.append(jnp.sum(dyv * _shift_down(xv, prev, s, row), axis=0, keepdims=True))
        dx_ref[...] = dx
        for s in range(k):
            @pl.when(i == 0)
            def _(s=s):
                dw_ref[pl.ds(k - 1 - s, 1), :] = dws[s]

            @pl.when(i > 0)
            def _(s=s):
                dw_ref[pl.ds(k - 1 - s, 1), :] += dws[s]

    return pl.pallas_call(
        body, grid=(c // cw, ni),
        in_specs=[pl.BlockSpec((tm, cw), lambda j, i: (i, j)),
                  pl.BlockSpec((SUBLANES, cw), lambda j, i: (jnp.minimum((i + 1) * nb8, t // SUBLANES - 1), j)),
                  pl.BlockSpec((tm, cw), lambda j, i: (i, j)),
                  pl.BlockSpec((SUBLANES, cw), lambda j, i: (jnp.maximum(i * nb8 - 1, 0), j)),
                  pl.BlockSpec((k, cw), lambda j, i: (0, j))],
        out_specs=[pl.BlockSpec((tm, cw), lambda j, i: (i, j)), pl.BlockSpec((k, cw), lambda j, i: (0, j))],
        out_shape=[jax.ShapeDtypeStruct((t, c), F32), jax.ShapeDtypeStruct((k, c), F32)], name=name,
        compiler_params=_params(("parallel", "arbitrary")),
    )(dy, dy, x, x, w)


def _each(f, *lists):
    return [f(*a) for a in zip(*lists)]


@jax.custom_vjp
def _inv_unit_lower(lms):
    return _inv_blocks(lms)


def _inv_blocks(lms):
    c = lms[0].shape[0]
    ri = lax.broadcasted_iota(jnp.int32, (c, c), 0)
    ci = lax.broadcasted_iota(jnp.int32, (c, c), 1)
    eye = (ri == ci).astype(F32)
    dms = _each(lambda lm: eye - jnp.where((ri >> 1) == (ci >> 1), lm, 0.0), lms)
    for lv in range(1, int(math.log2(c))):
        below = ((ri >> (lv + 1)) == (ci >> (lv + 1))) & ((ri >> lv) != (ci >> lv))
        dbs = _each(lambda dm: dm.astype(BF16), dms)
        ods = _each(lambda lm, db: _dot(jnp.where(below, lm, 0.0).astype(BF16), db).astype(BF16), lms, dbs)
        dms = _each(lambda dm, db, od: dm - _dot(db, od), dms, dbs, ods)
    return dms


def _inv_fwd(lms):
    tms = _inv_blocks(lms)
    return tms, tms


def _inv_bwd(tms, dts):
    tbs = _each(lambda tm: tm.astype(BF16), tms)
    mid = _each(lambda tb, dt: _dot(tb, dt.astype(BF16), TN).astype(BF16), tbs, dts)
    return (_each(lambda m, tb: -_dot(m, tb, NT), mid, tbs),)


_inv_unit_lower.defvjp(_inv_fwd, _inv_bwd)


def _l2n(x):
    return x * lax.rsqrt(jnp.sum(x * x, axis=-1, keepdims=True) + EPS)


def _prep_fn(cqs, cks, cvs, bg, sel_b, sel_g):
    c = cqs[0].shape[0]
    ri = lax.broadcasted_iota(jnp.int32, (c, c), 0)
    ci = lax.broadcasted_iota(jnp.int32, (c, c), 1)
    eye = (ri == ci).astype(F32)
    incl, strict = ci <= ri, ci < ri
    last = lax.broadcasted_iota(jnp.int32, (c, 1), 0) == c - 1
    to_row = lambda col: jnp.sum(col * eye, axis=0, keepdims=True)
    qs = _each(lambda a: _l2n(_silu(a)) * (HEAD_DIM_A ** -0.5), cqs)
    ks = _each(lambda a: _l2n(_silu(a)), cks)
    vbs = _each(lambda a: _silu(a).astype(BF16), cvs)
    betas = _each(lambda m: jnp.sum(bg * m, axis=1, keepdims=True), sel_b)
    gs = _each(lambda m: jnp.sum(bg * m, axis=1, keepdims=True), sel_g)
    gcss = _each(lambda g: jnp.sum(jnp.where(incl, to_row(g), 0.0), axis=1, keepdims=True), gs)
    gtots = _each(lambda gcs: jnp.sum(jnp.where(last, gcs, 0.0), axis=0, keepdims=True), gcss)
    decays = _each(lambda gcs: jnp.exp(jnp.where(incl, gcs - to_row(gcs), NEG)), gcss)
    kbs = _each(lambda k: k.astype(BF16), ks)
    lms = _each(lambda beta, kb, dec: jnp.where(strict, beta * _dot(kb, kb, NT) * dec, 0.0), betas, kbs, decays)
    ams = _each(lambda tm, beta: (tm * to_row(beta)).astype(BF16), _inv_unit_lower(lms), betas)
    gams = _each(jnp.exp, gcss)
    u0s = _each(_dot, ams, vbs)
    wks = _each(lambda am, gam, k: _dot(am, (gam * k).astype(BF16)), ams, gams, ks)
    qks = _each(lambda q, kb, dec: _dot(q.astype(BF16), kb, NT) * dec, qs, kbs, decays)
    qds = _each(lambda q, gam: q * gam, qs, gams)
    kds = _each(lambda k, gtot, gcs: k * jnp.exp(gtot - gcs), ks, gtots, gcss)
    gls = _each(lambda gtot: jnp.exp(gtot) * jnp.ones((SUBLANES, LANES), F32), gtots)
    return u0s, wks, qds, kds, qks, gls


def _head_masks(h):
    lane = lax.broadcasted_iota(jnp.int32, (1, LANES), 1)
    return (lane == h).astype(F32), (lane == h + N_HEADS_A).astype(F32)


PREP_HEADS = 8


def _hsl(j):
    return slice(j * HEAD_DIM_A, (j + 1) * HEAD_DIM_A)


def delta_prep(cqkv, bg, *, name):
    t = cqkv.shape[0]
    nh, hd, n, hb = N_HEADS_A, HEAD_DIM_A, t // CHUNK, PREP_HEADS
    ng = nh // hb

    def body(cq_ref, ck_ref, cv_ref, bg_ref, u0_ref, wk_ref, qd_ref, kd_ref, qk_ref, gl_ref):
        heads = range(hb)
        masks = [_head_masks(pl.program_id(1) * hb + j) for j in heads]
        res = _prep_fn([cq_ref[:, _hsl(j)] for j in heads], [ck_ref[:, _hsl(j)] for j in heads],
                       [cv_ref[:, _hsl(j)] for j in heads], bg_ref[...], [m[0] for m in masks], [m[1] for m in masks])
        for o_ref, rs in zip((u0_ref, wk_ref, qd_ref, kd_ref, qk_ref), res[:5]):
            for j in heads:
                o_ref[:, _hsl(j)] = rs[j]
        for j in heads:
            gl_ref[j * SUBLANES:(j + 1) * SUBLANES, :] = res[5][j]

    blk = lambda off: pl.BlockSpec((CHUNK, hb * hd), lambda i, g: (i, g + off))
    return pl.pallas_call(
        body, grid=(n, ng),
        in_specs=[blk(0), blk(ng), blk(2 * ng), pl.BlockSpec((CHUNK, LANES), lambda i, g: (i, 0))],
        out_specs=[blk(0)] * 5 + [pl.BlockSpec((hb * SUBLANES, LANES), lambda i, g: (i * ng + g, 0))],
        out_shape=[jax.ShapeDtypeStruct((t, nh * hd), F32)] * 5 + [jax.ShapeDtypeStruct((n * nh * SUBLANES, LANES), F32)],
        name=name, compiler_params=_params(("parallel", "parallel")),
    )(cqkv, cqkv, cqkv, bg)


def delta_prep_bwd(cqkv, bg, cts, *, name):
    t = cqkv.shape[0]
    nh, hd, n, hb = N_HEADS_A, HEAD_DIM_A, t // CHUNK, PREP_HEADS
    ng = nh // hb

    def body(cq_ref, ck_ref, cv_ref, bg_ref, c0, c1, c2, c3, c4, c5, dq_ref, dk_ref, dv_ref, dbg_ref):
        g = pl.program_id(1)
        heads = range(hb)
        masks = [_head_masks(g * hb + j) for j in heads]
        _, vjp = jax.vjp(lambda a, b, c, d: _prep_fn(a, b, c, d, [m[0] for m in masks], [m[1] for m in masks]),
                         [cq_ref[:, _hsl(j)] for j in heads], [ck_ref[:, _hsl(j)] for j in heads],
                         [cv_ref[:, _hsl(j)] for j in heads], bg_ref[...])
        cts = tuple([c[:, _hsl(j)] for j in heads] for c in (c0, c1, c2, c3, c4))
        dqs, dks, dvs, dbg = vjp(cts + ([c5[j * SUBLANES:(j + 1) * SUBLANES, :] for j in heads],))
        for j in heads:
            dq_ref[:, _hsl(j)] = dqs[j]
            dk_ref[:, _hsl(j)] = dks[j]
            dv_ref[:, _hsl(j)] = dvs[j]

        @pl.when(g == 0)
        def _():
            dbg_ref[...] = dbg

        @pl.when(g > 0)
        def _():
            dbg_ref[...] += dbg

    blk = lambda off: pl.BlockSpec((CHUNK, hb * hd), lambda i, g: (i, g + off))
    gl_spec = pl.BlockSpec((hb * SUBLANES, LANES), lambda i, g: (i * ng + g, 0))
    bg_spec = pl.BlockSpec((CHUNK, LANES), lambda i, g: (i, 0))
    dq, dk, dv, dbg = pl.pallas_call(
        body, grid=(n, ng),
        in_specs=[blk(0), blk(ng), blk(2 * ng), bg_spec] + [blk(0)] * 5 + [gl_spec],
        out_specs=[blk(0)] * 3 + [bg_spec],
        out_shape=[jax.ShapeDtypeStruct((t, nh * hd), F32)] * 3 + [jax.ShapeDtypeStruct((t, LANES), F32)],
        name=name, compiler_params=_params(("parallel", "arbitrary")),
    )(cqkv, cqkv, cqkv, bg, *cts)
    return dq, dk, dv, dbg


def delta_scan(u0, wk, qd, kd, qk, gl, *, name):
    t = u0.shape[0]
    nh, hd, n = N_HEADS_A, HEAD_DIM_A, t // CHUNK

    def body(u0_ref, wk_ref, qd_ref, kd_ref, qk_ref, gl_ref, o_ref, sin_ref, s_ref):
        @pl.when(pl.program_id(0) == 0)
        def _():
            s_ref[...] = jnp.zeros_like(s_ref)

        heads = list(range(nh))
        cols = lambda ref: [ref[:, _hsl(h)].astype(BF16) for h in heads]
        ss = [s_ref[h] for h in heads]
        for h in heads:
            sin_ref[h] = ss[h]
        sbs = _each(lambda s: s.astype(BF16), ss)
        ubs = _each(lambda h, wkb, sb: (u0_ref[:, _hsl(h)] - _dot(wkb, sb)).astype(BF16), heads, cols(wk_ref), sbs)
        os_ = _each(lambda qdb, sb, qkb, ub: _dot(qdb, sb) + _dot(qkb, ub), cols(qd_ref), sbs, cols(qk_ref), ubs)
        sn = _each(lambda h, s, kdb, ub: gl_ref[pl.ds(h * SUBLANES, 1), :] * s + _dot(kdb, ub, TN), heads, ss, cols(kd_ref), ubs)
        for h in heads:
            o_ref[:, _hsl(h)] = os_[h]
            s_ref[h] = sn[h]

    blk = pl.BlockSpec((CHUNK, nh * hd), lambda i: (i, 0))
    return pl.pallas_call(
        body, grid=(n,),
        in_specs=[blk] * 5 + [pl.BlockSpec((nh * SUBLANES, LANES), lambda i: (i, 0))],
        out_specs=[blk, pl.BlockSpec((None, nh, hd, hd), lambda i: (i, 0, 0, 0))],
        out_shape=[jax.ShapeDtypeStruct((t, nh * hd), F32), jax.ShapeDtypeStruct((n, nh, hd, hd), F32)],
        scratch_shapes=[pltpu.VMEM((nh, hd, hd), F32)], name=name,
        compiler_params=_params(("arbitrary",)),
    )(u0, wk, qd, kd, qk, gl)


def delta_scan_bwd(do, u0, wk, qd, kd, qk, gl, s_in, *, name):
    t = u0.shape[0]
    nh, hd, n = N_HEADS_A, HEAD_DIM_A, t // CHUNK

    def body(do_ref, u0_ref, wk_ref, qd_ref, kd_ref, qk_ref, gl_ref, sin_ref,
             du0_ref, dwk_ref, dqd_ref, dkd_ref, dqk_ref, dgl_ref, ds_ref):
        @pl.when(pl.program_id(0) == 0)
        def _():
            ds_ref[...] = jnp.zeros_like(ds_ref)

        corner = (lax.broadcasted_iota(jnp.int32, (SUBLANES, LANES), 0) == 0) & (lax.broadcasted_iota(jnp.int32, (SUBLANES, LANES), 1) == 0)
        heads = list(range(nh))
        cols = lambda ref: [ref[:, _hsl(h)].astype(BF16) for h in heads]
        ss, dss = [sin_ref[h] for h in heads], [ds_ref[h] for h in heads]
        sbs, dsbs = _each(lambda s: s.astype(BF16), ss), _each(lambda d: d.astype(BF16), dss)
        dobs, wkbs, qdbs, kdbs, qkbs = cols(do_ref), cols(wk_ref), cols(qd_ref), cols(kd_ref), cols(qk_ref)
        ubs = _each(lambda h, wkb, sb: (u0_ref[:, _hsl(h)] - _dot(wkb, sb)).astype(BF16), heads, wkbs, sbs)
        dus = _each(lambda qkb, dob, kdb, dsb: _dot(qkb, dob, TN) + _dot(kdb, dsb), qkbs, dobs, kdbs, dsbs)
        dubs = _each(lambda du: du.astype(BF16), dus)
        dwks = _each(lambda dub, sb: -_dot(dub, sb, NT), dubs, sbs)
        dqds = _each(lambda dob, sb: _dot(dob, sb, NT), dobs, sbs)
        dkds = _each(lambda ub, dsb: _dot(ub, dsb, NT), ubs, dsbs)
        dqks = _each(lambda dob, ub: _dot(dob, ub, NT), dobs, ubs)
        dgls = _each(lambda s, d: jnp.sum(jnp.sum(s * d, axis=1, keepdims=True), axis=0, keepdims=True), ss, dss)
        dsn = _each(lambda h, d, qdb, dob, wkb, dub: gl_ref[pl.ds(h * SUBLANES, 1), :] * d + _dot(qdb, dob, TN) - _dot(wkb, dub, TN),
                    heads, dss, qdbs, dobs, wkbs, dubs)
        for h in heads:
            du0_ref[:, _hsl(h)] = dus[h]
            dwk_ref[:, _hsl(h)] = dwks[h]
            dqd_ref[:, _hsl(h)] = dqds[h]
            dkd_ref[:, _hsl(h)] = dkds[h]
            dqk_ref[:, _hsl(h)] = dqks[h]
            dgl_ref[h * SUBLANES:(h + 1) * SUBLANES, :] = jnp.where(corner, dgls[h], 0.0)
            ds_ref[h] = dsn[h]

    blk = pl.BlockSpec((CHUNK, nh * hd), lambda i: (n - 1 - i, 0))
    gl_spec = pl.BlockSpec((nh * SUBLANES, LANES), lambda i: (n - 1 - i, 0))
    return pl.pallas_call(
        body, grid=(n,),
        in_specs=[blk] * 6 + [gl_spec, pl.BlockSpec((None, nh, hd, hd), lambda i: (n - 1 - i, 0, 0, 0))],
        out_specs=[blk] * 5 + [gl_spec],
        out_shape=[jax.ShapeDtypeStruct((t, nh * hd), F32)] * 5 + [jax.ShapeDtypeStruct((n * nh * SUBLANES, LANES), F32)],
        scratch_shapes=[pltpu.VMEM((nh, hd, hd), F32)], name=name,
        compiler_params=_params(("arbitrary",)),
    )(do, u0, wk, qd, kd, qk, gl, s_in)


N_PAIRS = N_HEADS_B // 2
PAIRS_PER_KV = N_PAIRS // N_KV_B


def _psl(j):
    return slice(j * LANES, (j + 1) * LANES)


def _att_fn(qps, kc, kp, vc, vp, sinks, kvf, first):
    w = WINDOW
    lane = lax.broadcasted_iota(jnp.int32, (1, LANES), 1)
    lo = (lane < HEAD_DIM_B).astype(F32)
    qi = lax.broadcasted_iota(jnp.int32, (w, w), 0)
    kj = lax.broadcasted_iota(jnp.int32, (w, w), 1)
    dist_c = (qi - kj).astype(F32)
    valid_c = kj <= qi
    valid_p = (kj > qi) & (first < 0.5)
    kcb, kpb, vcb, vpb = (a.astype(BF16) for a in (kc, kp, vc, vp))
    scale = HEAD_DIM_B ** -0.5
    heads = [(j, half) for j in range(PAIRS_PER_KV) for half in range(2)]
    hmasks = [lo if half == 0 else 1.0 - lo for _, half in heads]
    hds = [2.0 * (PAIRS_PER_KV * kvf + j) + half for j, half in heads]
    slopes = _each(lambda hd: jnp.exp(-(hd + 1.0) * (8.0 / N_HEADS_B * math.log(2.0))), hds)
    snks = _each(lambda hd: jnp.sum(sinks * (lane.astype(F32) == hd).astype(F32), axis=1, keepdims=True), hds)
    qhs = _each(lambda jh, hm: (qps[jh[0]] * hm).astype(BF16), heads, hmasks)
    lcs = _each(lambda qh, sl: jnp.where(valid_c, _dot(qh, kcb, NT) * scale - sl * dist_c, NEG), qhs, slopes)
    lps = _each(lambda qh, sl: jnp.where(valid_p, _dot(qh, kpb, NT) * scale - sl * (dist_c + w), NEG), qhs, slopes)
    ms = _each(lambda lc, lp, sk: lax.stop_gradient(jnp.maximum(jnp.maximum(jnp.max(lc, axis=1, keepdims=True),
                                                                            jnp.max(lp, axis=1, keepdims=True)), sk)), lcs, lps, snks)
    ecs = _each(lambda lc, m: jnp.exp(lc - m), lcs, ms)
    eps = _each(lambda lp, m: jnp.exp(lp - m), lps, ms)
    invs = _each(lambda ec, ep, sk, m: 1.0 / (jnp.sum(ec, axis=1, keepdims=True) + jnp.sum(ep, axis=1, keepdims=True) + jnp.exp(sk - m)),
                 ecs, eps, snks, ms)
    ohs = _each(lambda ec, ep, inv, hm: (_dot((ec * inv).astype(BF16), vcb) + _dot((ep * inv).astype(BF16), vpb)) * hm,
                ecs, eps, invs, hmasks)
    return [ohs[2 * j] + ohs[2 * j + 1] for j in range(PAIRS_PER_KV)]


def _scalar11(v):
    return jnp.full((1, 1), v, F32)


def swa_fwd(qsrc, kd, vd, sinks, *, name):
    t = kd.shape[0]
    nb = t // WINDOW

    def body(q_ref, kc_ref, kp_ref, vc_ref, vp_ref, s_ref, o_ref):
        first = _scalar11((pl.program_id(0) == 0).astype(F32))
        kvf = _scalar11(pl.program_id(1).astype(F32))
        outs = _att_fn([q_ref[:, _psl(j)] for j in range(PAIRS_PER_KV)], kc_ref[...], kp_ref[...], vc_ref[...], vp_ref[...],
                       s_ref[...], kvf, first)
        for j in range(PAIRS_PER_KV):
            o_ref[:, _psl(j)] = outs[j].astype(o_ref.dtype)

    cur = pl.BlockSpec((WINDOW, LANES), lambda i, kv: (i, kv))
    prev = pl.BlockSpec((WINDOW, LANES), lambda i, kv: (jnp.maximum(i - 1, 0), kv))
    qs = pl.BlockSpec((WINDOW, PAIRS_PER_KV * LANES), lambda i, kv: (i, kv))
    return pl.pallas_call(
        body, grid=(nb, N_KV_B),
        in_specs=[qs, cur, prev, cur, prev, pl.BlockSpec((1, LANES), lambda i, kv: (0, 0))],
        out_specs=qs, out_shape=jax.ShapeDtypeStruct((t, N_PAIRS * LANES), BF16), name=name,
        compiler_params=_params(("parallel", "parallel")),
    )(qsrc, kd, kd, vd, vd, sinks)


def swa_bwd(do, qsrc, kd, vd, sinks, *, name):
    t = kd.shape[0]
    nb = t // WINDOW

    def body(do_ref, q_ref, kc_ref, kp_ref, vc_ref, vp_ref, s_ref, dq_ref, dk_ref, dv_ref, ds_ref, carry_k, carry_v):
        step, kv = pl.program_id(0), pl.program_id(1)
        first = _scalar11((step == nb - 1).astype(F32))

        @pl.when((step == 0) & (kv == 0))
        def _():
            carry_k[...] = jnp.zeros_like(carry_k)
            carry_v[...] = jnp.zeros_like(carry_v)
            ds_ref[...] = jnp.zeros_like(ds_ref)

        kvf = _scalar11(kv.astype(F32))
        pairs = range(PAIRS_PER_KV)
        _, vjp = jax.vjp(lambda *a: _att_fn(*a, kvf, first), [q_ref[:, _psl(j)] for j in pairs],
                         kc_ref[...], kp_ref[...], vc_ref[...], vp_ref[...], s_ref[...])
        dqs, dkc, dkp, dvc, dvp, dsk = vjp([do_ref[:, _psl(j)].astype(F32) for j in pairs])
        for j in pairs:
            dq_ref[:, _psl(j)] = dqs[j]
        ds_ref[...] += dsk
        fold = lambda g: g + pltpu.roll(g, HEAD_DIM_B, 1)
        dk_ref[...] = fold(dkc + carry_k[kv])
        dv_ref[...] = fold(dvc + carry_v[kv])
        carry_k[kv] = dkp
        carry_v[kv] = dvp

    rev = lambda i: nb - 1 - i
    cur = pl.BlockSpec((WINDOW, LANES), lambda i, kv: (rev(i), kv))
    prev = pl.BlockSpec((WINDOW, LANES), lambda i, kv: (jnp.maximum(rev(i) - 1, 0), kv))
    qs = pl.BlockSpec((WINDOW, PAIRS_PER_KV * LANES), lambda i, kv: (rev(i), kv))
    sk = pl.BlockSpec((1, LANES), lambda i, kv: (0, 0))
    return pl.pallas_call(
        body, grid=(nb, N_KV_B),
        in_specs=[qs, qs, cur, prev, cur, prev, sk],
        out_specs=[qs, cur, cur, sk],
        out_shape=[jax.ShapeDtypeStruct((t, N_PAIRS * LANES), F32), jax.ShapeDtypeStruct((t, N_KV_B * LANES), F32),
                   jax.ShapeDtypeStruct((t, N_KV_B * LANES), F32), jax.ShapeDtypeStruct((1, LANES), F32)],
        scratch_shapes=[pltpu.VMEM((N_KV_B, WINDOW, LANES), F32), pltpu.VMEM((N_KV_B, WINDOW, LANES), F32)],
        name=name, compiler_params=_params(("arbitrary", "arbitrary")),
    )(do, qsrc, kd, kd, vd, vd, sinks)


def loss_head(h, tgt, w, *, name):
    t, d = h.shape
    tm = min(256, t)

    def body(h_ref, t_ref, w_ref, dh_ref, dw_ref, l_ref):
        tg = t_ref[...]

        def f(hv, wv):
            err = _f_norm(hv, wv) - tg
            return 0.5 * jnp.sum(jnp.sum(err * err, axis=1, keepdims=True), axis=0, keepdims=True) * (1.0 / d)

        lv, vjp = jax.vjp(f, h_ref[...], w_ref[...])
        dh, dw = vjp(jnp.ones((1, 1), F32))
        dh_ref[...] = dh
        first = pl.program_id(0) == 0

        @pl.when(first)
        def _():
            dw_ref[...] = dw
            l_ref[...] = lv * jnp.ones((1, LANES), F32)

        @pl.when(jnp.logical_not(first))
        def _():
            dw_ref[...] += dw
            l_ref[...] += lv * jnp.ones((1, LANES), F32)

    rows = pl.BlockSpec((tm, d), lambda i: (i, 0))
    one = lambda c: pl.BlockSpec((1, c), lambda i: (0, 0))
    return pl.pallas_call(
        body, grid=(t // tm,), in_specs=[rows, rows, one(d)], out_specs=[rows, one(d), one(LANES)],
        out_shape=[jax.ShapeDtypeStruct((t, d), F32), jax.ShapeDtypeStruct((1, d), F32), jax.ShapeDtypeStruct((1, LANES), F32)],
        name=name, compiler_params=_params(("arbitrary",)),
    )(h, tgt, w)


def adamw(w, g, m, v, *, name):
    r, c = w.shape
    tr = r
    if r % SUBLANES == 0:
        for cand in range(SUBLANES, min(r, 256) + 1, SUBLANES):
            if r % cand == 0:
                tr = cand

    def body(w_ref, g_ref, m_ref, v_ref, d_ref, mo_ref, vo_ref):
        gv = g_ref[...]
        mn = ADAM_B1 * m_ref[...] + (1.0 - ADAM_B1) * gv
        vn = ADAM_B2 * v_ref[...] + (1.0 - ADAM_B2) * jnp.square(gv)
        m_hat = mn / (1.0 - ADAM_B1 ** ADAM_STEP)
        v_hat = vn / (1.0 - ADAM_B2 ** ADAM_STEP)
        d_ref[...] = -ADAM_LR * (m_hat / (jnp.sqrt(v_hat) + ADAM_EPS) + ADAM_WD * w_ref[...])
        mo_ref[...] = mn
        vo_ref[...] = vn

    spec = pl.BlockSpec((tr, c), lambda i: (i, 0))
    return pl.pallas_call(
        body, grid=(r // tr,), in_specs=[spec] * 4, out_specs=[spec] * 3,
        out_shape=[jax.ShapeDtypeStruct((r, c), F32)] * 3, name=name, compiler_params=_params(("parallel",)),
    )(w, g, m, v)


def _place():
    return lax.axis_index("x"), lax.axis_index("y"), lax.axis_index("c")


def allgather8(blk, *, name):
    def body(x_ref, out_ref, send_sems, recv_sems, local_sem):
        x, y, c = _place()
        me = 4 * x + 2 * y + c
        mine = pltpu.make_async_copy(x_ref, out_ref.at[me], local_sem)
        mine.start()
        sent = []
        for k in range(1, N_DEV):
            to = (x ^ ((k >> 2) & 1), y ^ ((k >> 1) & 1), c ^ (k & 1))
            cp = pltpu.make_async_remote_copy(src_ref=x_ref, dst_ref=out_ref.at[me], send_sem=send_sems.at[k - 1],
                                              recv_sem=recv_sems.at[k - 1], device_id=to, device_id_type=MESH)
            cp.start()
            sent.append(cp)
        for k in range(1, N_DEV):
            frm = me ^ k
            pltpu.make_async_remote_copy(src_ref=x_ref, dst_ref=out_ref.at[frm], send_sem=send_sems.at[k - 1],
                                         recv_sem=recv_sems.at[k - 1], device_id=(x, y, c), device_id_type=MESH).wait_recv()
        for cp in sent:
            cp.wait_send()
        mine.wait()

    vm = pl.BlockSpec(memory_space=pltpu.VMEM)
    return pl.pallas_call(
        body, in_specs=[vm], out_specs=vm, out_shape=jax.ShapeDtypeStruct((N_DEV,) + blk.shape, blk.dtype), name=name,
        scratch_shapes=[pltpu.SemaphoreType.DMA((N_DEV - 1,)), pltpu.SemaphoreType.DMA((N_DEV - 1,)), pltpu.SemaphoreType.DMA],
    )(blk)


def _other_chips(x, y):
    return [(1 - x, y), (x, 1 - y), (1 - x, 1 - y)]


def allgather_shards(packed, *, name):
    r = packed.shape[0]
    rh = r // 2

    def body(in_ref, out_ref, send_sems, recv_sems):
        x, y, c = _place()
        me_chip = 2 * x + y
        sib = (x, y, 1 - c)
        chips = _other_chips(x, y)

        def rows(chip, half):
            return out_ref.at[chip, pl.ds(pl.multiple_of(half * rh, BF16_ROWS), rh), :]

        def copy(k, src, dst, to):
            return pltpu.make_async_remote_copy(src_ref=src, dst_ref=dst, send_sem=send_sems.at[k], recv_sem=recv_sems.at[k],
                                                device_id=to, device_id_type=MESH)

        my_half = in_ref.at[pl.ds(pl.multiple_of(c * rh, BF16_ROWS), rh), :]
        first = [copy(j, my_half, rows(me_chip, c), (cx, cy, c)) for j, (cx, cy) in enumerate(chips)]
        for cp in first:
            cp.start()
        passed = [copy(3 + j, rows(2 * cx + cy, c), rows(2 * cx + cy, c), sib) for j, (cx, cy) in enumerate(chips)]
        for j, (cx, cy) in enumerate(chips):
            copy(j, my_half, rows(2 * cx + cy, c), sib).wait_recv()
            passed[j].start()
        for j, (cx, cy) in enumerate(chips):
            copy(3 + j, my_half, rows(2 * cx + cy, 1 - c), sib).wait_recv()
        for cp in first + passed:
            cp.wait_send()

    hbm = pl.BlockSpec(memory_space=pl.ANY)
    return pl.pallas_call(
        body, in_specs=[hbm], out_specs=hbm, out_shape=jax.ShapeDtypeStruct((N_CHIPS, r, PACK_COLS), packed.dtype), name=name,
        scratch_shapes=[pltpu.SemaphoreType.DMA((6,)), pltpu.SemaphoreType.DMA((6,))],
    )(packed)


def swap_halves(g, *, name):
    r = g.shape[1]
    rh = r // 2

    def body(g_ref, out_ref, send_sem, recv_sem):
        x, y, c = _place()
        cp = pltpu.make_async_remote_copy(src_ref=g_ref.at[:, pl.ds(pl.multiple_of((1 - c) * rh, BF16_ROWS), rh), :], dst_ref=out_ref,
                                          send_sem=send_sem, recv_sem=recv_sem, device_id=(x, y, 1 - c), device_id_type=MESH)
        cp.start()
        cp.wait()

    hbm = pl.BlockSpec(memory_space=pl.ANY)
    return pl.pallas_call(
        body, in_specs=[hbm], out_specs=hbm, out_shape=jax.ShapeDtypeStruct((N_CHIPS, rh, PACK_COLS), g.dtype), name=name,
        scratch_shapes=[pltpu.SemaphoreType.DMA, pltpu.SemaphoreType.DMA],
    )(g)


def scatter_chips(hp, *, name):
    rh = hp.shape[1]

    def body(h_ref, out_ref, send_sems, recv_sems):
        x, y, c = _place()
        chips = _other_chips(x, y)
        cps = [pltpu.make_async_remote_copy(src_ref=h_ref.at[2 * cx + cy], dst_ref=out_ref.at[j], send_sem=send_sems.at[j],
                                            recv_sem=recv_sems.at[j], device_id=(cx, cy, c), device_id_type=MESH)
               for j, (cx, cy) in enumerate(chips)]
        for cp in cps:
            cp.start()
        for cp in cps:
            cp.wait()

    hbm = pl.BlockSpec(memory_space=pl.ANY)
    return pl.pallas_call(
        body, in_specs=[hbm], out_specs=hbm, out_shape=jax.ShapeDtypeStruct((3, rh, PACK_COLS), hp.dtype), name=name,
        scratch_shapes=[pltpu.SemaphoreType.DMA((3,)), pltpu.SemaphoreType.DMA((3,))],
    )(hp)


def join_halves(half, *, name):
    def body(h_ref, out_ref, send_sem, recv_sem):
        x, y, c = _place()
        cp = pltpu.make_async_remote_copy(src_ref=h_ref, dst_ref=out_ref, send_sem=send_sem, recv_sem=recv_sem,
                                          device_id=(x, y, 1 - c), device_id_type=MESH)
        cp.start()
        cp.wait()

    hbm = pl.BlockSpec(memory_space=pl.ANY)
    return pl.pallas_call(
        body, in_specs=[hbm], out_specs=hbm, out_shape=jax.ShapeDtypeStruct(half.shape, half.dtype), name=name,
        scratch_shapes=[pltpu.SemaphoreType.DMA, pltpu.SemaphoreType.DMA],
    )(half)


def add_parts(parts, *, out_dtype, name):
    rows = parts[0].shape[0]
    tr = rows
    for cand in range(BF16_ROWS, min(rows, 1200) + 1, BF16_ROWS):
        if rows % cand == 0:
            tr = cand

    def body(*refs):
        acc = refs[0][...].astype(F32)
        for r in refs[1:-1]:
            acc = acc + r[...].astype(F32)
        refs[-1][...] = acc.astype(refs[-1].dtype)

    spec = pl.BlockSpec((tr, PACK_COLS), lambda i: (i, 0))
    return pl.pallas_call(
        body, grid=(rows // tr,), in_specs=[spec] * len(parts), out_specs=spec,
        out_shape=jax.ShapeDtypeStruct((rows, PACK_COLS), out_dtype), name=name, compiler_params=_params(("parallel",)),
    )(*parts)


def sum8(g, *, name):
    def body(g_ref, o_ref):
        acc = g_ref[0]
        for d in range(1, N_DEV):
            acc = acc + g_ref[d]
        o_ref[...] = acc

    return pl.pallas_call(body, out_shape=jax.ShapeDtypeStruct(g.shape[1:], F32), name=name)(g)


def _dup_halves(a):
    t = a.shape[0]
    a = a.reshape(t, N_KV_B, HEAD_DIM_B)
    return jnp.concatenate([a, a], axis=-1).reshape(t, N_KV_B * LANES)


def _undup(a):
    t = a.shape[0]
    return a.reshape(t, N_KV_B, LANES)[:, :, :HEAD_DIM_B].reshape(t, N_KV_B * HEAD_DIM_B)


def _lane_pad(v, offset=0):
    return jnp.zeros((1, LANES), F32).at[0, offset:offset + v.shape[0]].set(v)


def local_step(x, p, tgt, sm, wt):
    t = x.shape[0]
    rtm = min(256, t)
    hk = N_HEADS_A * HEAD_DIM_A
    qd_b = N_HEADS_B * HEAD_DIM_B
    kd_b = N_KV_B * HEAD_DIM_B
    gw, gs = {}, {}
    norm = lambda h, w, nm: tile_map(_f_norm, [(h, D_MODEL, 0)], [w], [(D_MODEL, BF16)], tm=rtm, ncol=1, name=nm)[0]

    def norm_bwd(h, w, dy, add, nm):
        (dh,), (dw,) = tile_vjp(_f_norm, [(h, D_MODEL, 0)], [w], [(dy, D_MODEL, 0)], n_diff=1, tm=rtm, ncol=1, name=nm,
                                add=(add, D_MODEL, 0))
        return dh, dw

    p_bf = p.astype(BF16)
    a_main, a_tail = wt["a_w_in"][:, :4 * hk], jnp.pad(wt["a_w_in"][:, 4 * hk:], ((0, 0), (0, LANES - 2 * N_HEADS_A)))
    alog_p = _lane_pad(sm["a_log"][0], N_HEADS_A)
    dtb_p = _lane_pad(sm["a_dt_bias"][0], N_HEADS_A)
    sinks_p = _lane_pad(sm["b_sinks"][0])
    nw = lambda name, i: sm[name][i:i + 1]

    saved = []
    h = x
    for i in range(DEPTH):
        s = {"h0": h}
        s["hn"] = norm(h, nw("norm_mix", i), f"norm_mix{i}")
        if i % 2 == 0:
            s["pm"] = mm(s["hn"], a_main, name="a_in_main")
            s["pt"] = mm(s["hn"], a_tail, name="a_in_tail")
            s["c"] = conv_fwd(s["pm"], wt["a_conv"], name="a_conv")
            s["bg"] = tile_map(_f_betag, [(s["pt"], LANES, 0)], [alog_p, dtb_p], [(LANES, F32)], tm=rtm, ncol=1, name="a_betag")[0]
            s["prep"] = delta_prep(s["c"], s["bg"], name="a_prep")
            s["o"], s["s_in"] = delta_scan(*s["prep"], name="a_scan")
            s["on"] = tile_map(_f_gnorm, [(s["o"], HEAD_DIM_A, 0), (s["pm"], HEAD_DIM_A, 3 * N_HEADS_A)], [sm["a_norm"]],
                               [(HEAD_DIM_A, BF16)], tm=rtm, ncol=N_HEADS_A, name="a_gnorm")[0]
            h = mm(s["on"], wt["a_w_out"], add=h, name="a_out")
        else:
            s["pb"] = mm(s["hn"], wt["b_w_in"], name="b_in")
            s["kd"], s["vd"] = _dup_halves(s["pb"][:, qd_b:qd_b + kd_b]), _dup_halves(s["pb"][:, qd_b + kd_b:])
            s["ao"] = swa_fwd(s["pb"], s["kd"], s["vd"], sinks_p, name="b_att")
            h = mm(s["ao"], wt["b_w_out"], add=h, name="b_out")
        s["h1"] = h
        s["hf"] = norm(h, nw("norm_ffn", i), f"norm_ffn{i}")
        s["u"] = mm(s["hf"], wt["f_w_up"][i], name=f"f_up{i}")
        s["uc"] = conv_fwd(s["u"], wt["f_conv"][i], name=f"f_conv{i}")
        nff = D_FF // 1408
        s["act"] = tile_map(_f_act, [(s["uc"], 1408, 0), (s["uc"], 1408, nff)], [], [(1408, BF16)], tm=rtm, ncol=nff, name=f"f_act{i}")[0]
        h = mm(s["act"], wt["f_w_down"][i], add=h, name=f"f_down{i}")
        s["h2"] = h
        s["hp"] = norm(h, nw("norm_ple", i), f"norm_ple{i}")
        s["gl"] = mm(s["hp"], wt["ple_w_gate"][i], name=f"ple_gate{i}")
        s["pe"] = mm(p_bf[i], wt["ple_w_proj"][i], name=f"ple_proj{i}")
        h = tile_map(lambda hv, g, e: hv + _f_ple(g, e), [(h, D_MODEL, 0), (s["gl"], D_MODEL, 0), (s["pe"], D_MODEL, 0)], [],
                     [(D_MODEL, F32)], tm=rtm, ncol=1, name=f"ple_mix{i}")[0]
        saved.append(s)

    dh, gnf, loss = loss_head(h, tgt, sm["norm_final"][None, :], name="loss_head")
    gs["norm_final"] = gnf[0]

    g_mix, g_ffn, g_ple = [None] * DEPTH, [None] * DEPTH, [None] * DEPTH
    g_up, g_conv, g_down, g_proj, g_gate = ([None] * DEPTH for _ in range(5))
    for i in reversed(range(DEPTH)):
        s = saved[i]
        (dgl, dpe), _ = tile_vjp(_f_ple, [(s["gl"], D_MODEL, 0), (s["pe"], D_MODEL, 0)], [], [(dh, D_MODEL, 0)], n_diff=2,
                                 tm=rtm, ncol=1, name=f"ple_mix_bwd{i}")
        g_proj[i] = mm(p_bf[i], dpe, ta=True, name=f"ple_proj_dw{i}")
        g_gate[i] = mm(s["hp"], dgl, ta=True, name=f"ple_gate_dw{i}")
        dhp = mm(dgl, wt["ple_w_gate"][i], tb=True, name=f"ple_gate_dx{i}")
        dh, g_ple[i] = norm_bwd(s["h2"], nw("norm_ple", i), dhp, dh, f"norm_ple_bwd{i}")

        dact = mm(dh, wt["f_w_down"][i], tb=True, name=f"f_down_dx{i}")
        g_down[i] = mm(s["act"], dh, ta=True, name=f"f_down_dw{i}")
        nff = D_FF // 1408
        (dgate, dval), _ = tile_vjp(_f_act, [(s["uc"], 1408, 0), (s["uc"], 1408, nff)], [], [(dact, 1408, 0)], n_diff=2,
                                    tm=rtm, ncol=nff, name=f"f_act_bwd{i}")
        du, g_conv[i] = conv_bwd(jnp.concatenate([dgate, dval], axis=1), s["u"], wt["f_conv"][i], name=f"f_conv_bwd{i}")
        g_up[i] = mm(s["hf"], du, ta=True, name=f"f_up_dw{i}")
        dhf = mm(du, wt["f_w_up"][i], tb=True, name=f"f_up_dx{i}")
        dh, g_ffn[i] = norm_bwd(s["h1"], nw("norm_ffn", i), dhf, dh, f"norm_ffn_bwd{i}")

        if i % 2 == 0:
            don = mm(dh, wt["a_w_out"], tb=True, name="a_out_dx")
            gw["a_w_out"] = mm(s["on"], dh, ta=True, name="a_out_dw")
            (do, dz), (gan,) = tile_vjp(_f_gnorm, [(s["o"], HEAD_DIM_A, 0), (s["pm"], HEAD_DIM_A, 3 * N_HEADS_A)], [sm["a_norm"]],
                                        [(don, HEAD_DIM_A, 0)], n_diff=2, tm=rtm, ncol=N_HEADS_A, name="a_gnorm_bwd")
            gs["a_norm"] = gan
            dprep = delta_scan_bwd(do, *s["prep"], s["s_in"], name="a_scan_bwd")
            dcq, dck, dcv, dbg = delta_prep_bwd(s["c"], s["bg"], dprep, name="a_prep_bwd")
            (dpt,), (galog, gdtb) = tile_vjp(_f_betag, [(s["pt"], LANES, 0)], [alog_p, dtb_p], [(dbg, LANES, 0)], n_diff=1,
                                             tm=rtm, ncol=1, name="a_betag_bwd")
            gs["a_log"] = galog[:, N_HEADS_A:2 * N_HEADS_A]
            gs["a_dt_bias"] = gdtb[:, N_HEADS_A:2 * N_HEADS_A]
            dqkv, gw["a_conv"] = conv_bwd(jnp.concatenate([dcq, dck, dcv], axis=1), s["pm"], wt["a_conv"], name="a_conv_bwd")
            dpm = jnp.concatenate([dqkv, dz], axis=1)
            dhn = mm(dpm, a_main, tb=True, name="a_in_main_dx")
            dhn = mm(dpt, a_tail, tb=True, add=dhn, name="a_in_tail_dx")
            g_main = mm(s["hn"], dpm, ta=True, name="a_in_main_dw")
            g_tail = mm(s["hn"], dpt, ta=True, name="a_in_tail_dw")
            gw["a_w_in"] = jnp.concatenate([g_main, g_tail[:, :2 * N_HEADS_A]], axis=1)
        else:
            dao = mm(dh, wt["b_w_out"], tb=True, name="b_out_dx")
            gw["b_w_out"] = mm(s["ao"], dh, ta=True, name="b_out_dw")
            dq, dkd, dvd, gsk = swa_bwd(dao, s["pb"], s["kd"], s["vd"], sinks_p, name="b_att_bwd")
            gs["b_sinks"] = gsk[:, :N_HEADS_B]
            dpb = jnp.concatenate([dq, _undup(dkd), _undup(dvd)], axis=1)
            dhn = mm(dpb, wt["b_w_in"], tb=True, name="b_in_dx")
            gw["b_w_in"] = mm(s["hn"], dpb, ta=True, name="b_in_dw")
        dh, g_mix[i] = norm_bwd(s["h0"], nw("norm_mix", i), dhn, dh, f"norm_mix_bwd{i}")

    gs["norm_mix"], gs["norm_ffn"], gs["norm_ple"] = (jnp.concatenate(g, axis=0) for g in (g_mix, g_ffn, g_ple))
    gw["f_w_up"], gw["f_conv"], gw["f_w_down"] = jnp.stack(g_up), jnp.stack(g_conv), jnp.stack(g_down)
    gw["ple_w_proj"], gw["ple_w_gate"] = jnp.stack(g_proj), jnp.stack(g_gate)
    return loss, dh, gw, gs


BIG = [("a_w_in", 2), ("a_w_out", 1), ("b_w_in", 2), ("b_w_out", 1), ("f_w_up", 2), ("f_w_down", 1),
       ("ple_w_proj", 2), ("ple_w_gate", 1)]
CONVS = [("a_conv", 2), ("f_conv", 2)]
SMALL = ["norm_mix", "norm_ffn", "norm_ple", "norm_final", "a_log", "a_dt_bias", "a_norm", "b_sinks"]
SMALL_ROWS = 8
CONV_ROWS = 16


def _pack_rows(arrs, rows, dtype):
    flat = jnp.concatenate([a.reshape(-1).astype(dtype) for a in arrs])
    return jnp.pad(flat, (0, rows * PACK_COLS - flat.shape[0])).reshape(rows, PACK_COLS)


def _unpack(flat, shapes):
    out, off = [], 0
    for shp in shapes:
        n = math.prod(shp)
        out.append(flat[off:off + n].reshape(shp))
        off += n
    return out


def _pack_small(d, loss=None):
    tail = jnp.concatenate([d["a_log"].reshape(-1), d["a_dt_bias"].reshape(-1), d["a_norm"].reshape(-1), d["b_sinks"].reshape(-1)])
    if loss is not None:
        tail = jnp.concatenate([tail, loss.reshape(-1)[:1]])
    tail = jnp.pad(tail, (0, PACK_COLS - tail.shape[0]))
    return jnp.concatenate([d["norm_mix"], d["norm_ffn"], d["norm_ple"], d["norm_final"][None, :], tail[None, :]], axis=0)


def _unpack_small(a, like):
    out = {"norm_mix": a[0:2], "norm_ffn": a[2:4], "norm_ple": a[4:6], "norm_final": a[6]}
    off = 0
    for nm in ("a_log", "a_dt_bias", "a_norm", "b_sinks"):
        n = like[nm].size
        out[nm] = a[7, off:off + n].reshape(like[nm].shape)
        off += n
    return out, a[7, off]


def _shard_rows(shapes):
    n = sum(math.prod(s) for s in shapes)
    rows = -(-n // PACK_COLS)
    return -(-rows // (2 * BF16_ROWS)) * (2 * BF16_ROWS)


def _as2d(a):
    return a.reshape(-1, a.shape[-1])


def kernel(x, p, norm_mix, norm_ffn, norm_ple, norm_final, a_w_in, a_conv, a_log, a_dt_bias, a_norm, a_w_out, b_w_in, b_sinks, b_w_out, f_w_up, f_conv, f_w_down, ple_w_proj, ple_w_gate, loss_target, m_norm_mix, m_norm_ffn, m_norm_ple, m_norm_final, m_a_w_in, m_a_conv, m_a_log, m_a_dt_bias, m_a_norm, m_a_w_out, m_b_w_in, m_b_sinks, m_b_w_out, m_f_w_up, m_f_conv, m_f_w_down, m_ple_w_proj, m_ple_w_gate, v_norm_mix, v_norm_ffn, v_norm_ple, v_norm_final, v_a_w_in, v_a_conv, v_a_log, v_a_dt_bias, v_a_norm, v_a_w_out, v_b_w_in, v_b_sinks, v_b_w_out, v_f_w_up, v_f_conv, v_f_w_down, v_ple_w_proj, v_ple_w_gate):
    w = dict(norm_mix=norm_mix, norm_ffn=norm_ffn, norm_ple=norm_ple, norm_final=norm_final, a_w_in=a_w_in, a_conv=a_conv,
             a_log=a_log, a_dt_bias=a_dt_bias, a_norm=a_norm, a_w_out=a_w_out, b_w_in=b_w_in, b_sinks=b_sinks, b_w_out=b_w_out,
             f_w_up=f_w_up, f_conv=f_conv, f_w_down=f_w_down, ple_w_proj=ple_w_proj, ple_w_gate=ple_w_gate)
    m = dict(norm_mix=m_norm_mix, norm_ffn=m_norm_ffn, norm_ple=m_norm_ple, norm_final=m_norm_final, a_w_in=m_a_w_in,
             a_conv=m_a_conv, a_log=m_a_log, a_dt_bias=m_a_dt_bias, a_norm=m_a_norm, a_w_out=m_a_w_out, b_w_in=m_b_w_in,
             b_sinks=m_b_sinks, b_w_out=m_b_w_out, f_w_up=m_f_w_up, f_conv=m_f_conv, f_w_down=m_f_w_down,
             ple_w_proj=m_ple_w_proj, ple_w_gate=m_ple_w_gate)
    v = dict(norm_mix=v_norm_mix, norm_ffn=v_norm_ffn, norm_ple=v_norm_ple, norm_final=v_norm_final, a_w_in=v_a_w_in,
             a_conv=v_a_conv, a_log=v_a_log, a_dt_bias=v_a_dt_bias, a_norm=v_a_norm, a_w_out=v_a_w_out, b_w_in=v_b_w_in,
             b_sinks=v_b_sinks, b_w_out=v_b_w_out, f_w_up=v_f_w_up, f_conv=v_f_conv, f_w_down=v_f_w_down,
             ple_w_proj=v_ple_w_proj, ple_w_gate=v_ple_w_gate)
    xc, yc, cc = _place()
    my_chip = 2 * xc + yc

    big_shapes = [w[n].shape for n, _ in BIG]
    rows_big = _shard_rows(big_shapes)
    packed = _pack_rows([w[n] for n, _ in BIG], rows_big, BF16)
    gathered = lax.dynamic_update_index_in_dim(allgather_shards(packed, name="gather_weights"), packed, my_chip, 0)
    conv_shapes = [w[n].shape for n, _ in CONVS]
    convs = allgather8(_pack_rows([w[n] for n, _ in CONVS], CONV_ROWS, F32), name="gather_convs")
    full = {}
    for src, names, shapes, pick in ((gathered, BIG, big_shapes, lambda j: j), (convs, CONVS, conv_shapes, lambda j: 2 * j)):
        parts = [_unpack(src[pick(j)].reshape(-1), shapes) for j in range(N_CHIPS)]
        for q, (nm, ax) in enumerate(names):
            full[nm] = jnp.concatenate([parts[j][q] for j in range(N_CHIPS)], axis=ax)
    wt = {"a_w_in": full["a_w_in"][0], "a_conv": full["a_conv"][0], "a_w_out": full["a_w_out"][0], "b_w_in": full["b_w_in"][0],
          "b_w_out": full["b_w_out"][0], "f_w_up": full["f_w_up"], "f_conv": full["f_conv"], "f_w_down": full["f_w_down"],
          "ple_w_proj": full["ple_w_proj"], "ple_w_gate": full["ple_w_gate"]}
    sm = {n: w[n] for n in SMALL}

    loss, grad_x, gw, gs = local_step(x[0], p[:, 0], loss_target[0], sm, wt)
    gw = {"a_w_in": gw["a_w_in"][None], "a_conv": gw["a_conv"][None], "a_w_out": gw["a_w_out"][None], "b_w_in": gw["b_w_in"][None],
          "b_w_out": gw["b_w_out"][None], "f_w_up": gw["f_w_up"], "f_conv": gw["f_conv"], "f_w_down": gw["f_w_down"],
          "ple_w_proj": gw["ple_w_proj"], "ple_w_gate": gw["ple_w_gate"]}

    sharded = BIG + CONVS
    sh_shapes = [w[n].shape for n, _ in sharded]
    rows_all = _shard_rows(sh_shapes)
    rh = rows_all // 2

    def dest(j):
        return _pack_rows([lax.slice_in_dim(gw[n], j * w[n].shape[ax], (j + 1) * w[n].shape[ax], axis=ax) for n, ax in sharded],
                          rows_all, BF16)

    gpack = jnp.stack([dest(j) for j in range(N_CHIPS)])
    from_sib = swap_halves(gpack, name="rs_swap")
    mine = lax.dynamic_slice_in_dim(gpack, cc * rh, rh, axis=1)
    pair = add_parts([mine.reshape(N_CHIPS * rh, PACK_COLS), from_sib.reshape(N_CHIPS * rh, PACK_COLS)], out_dtype=BF16,
                     name="rs_add_pair").reshape(N_CHIPS, rh, PACK_COLS)
    others = scatter_chips(pair, name="rs_scatter")
    own = lax.dynamic_index_in_dim(pair, my_chip, axis=0, keepdims=False)
    half = add_parts([own, others[0], others[1], others[2]], out_dtype=F32, name="rs_add_chips")
    other = join_halves(half, name="rs_join")
    gfull = jnp.where(cc == 0, jnp.concatenate([half, other]), jnp.concatenate([other, half]))
    g_sh = dict(zip([n for n, _ in sharded], _unpack(gfull.reshape(-1), sh_shapes)))

    small_sum = sum8(allgather8(_pack_small(gs, loss), name="gather_small"), name="sum_small")
    g_sm, loss_sum = _unpack_small(small_sum, sm)

    grads, delta, new_m, new_v = {}, {}, {}, {}
    for n, _ in sharded:
        d2, m2, v2 = adamw(_as2d(w[n]), _as2d(g_sh[n]), _as2d(m[n]), _as2d(v[n]), name=f"adamw_{n}")
        grads[n], delta[n], new_m[n], new_v[n] = g_sh[n], d2.reshape(w[n].shape), m2.reshape(w[n].shape), v2.reshape(w[n].shape)
    pk = lambda d: _pack_small(d)
    d2, m2, v2 = adamw(pk(sm), pk(g_sm), pk({n: m[n] for n in SMALL}), pk({n: v[n] for n in SMALL}), name="adamw_small")
    for src, dst in ((d2, delta), (m2, new_m), (v2, new_v)):
        dst.update(_unpack_small(src, sm)[0])
    grads.update(g_sm)

    order = ["norm_mix", "norm_ffn", "norm_ple", "norm_final", "a_w_in", "a_conv", "a_log", "a_dt_bias", "a_norm", "a_w_out",
             "b_w_in", "b_sinks", "b_w_out", "f_w_up", "f_conv", "f_w_down", "ple_w_proj", "ple_w_gate"]
    return (loss_sum, grad_x[None], *[grads[n] for n in order], *[delta[n] for n in order],
            *[new_m[n] for n in order], *[new_v[n] for n in order])
```

```python
import functools
import math

import jax
import jax.numpy as jnp
from jax import lax
from jax.experimental import pallas as pl
from jax.experimental.pallas import tpu as pltpu

F32 = jnp.float32
BF16 = jnp.bfloat16
MESH = pl.DeviceIdType.MESH

D_MODEL = 1024
N_HEADS_A = 8
HEAD_DIM_A = 128
CONV_A = 4
N_HEADS_B = 16
N_KV_B = 4
HEAD_DIM_B = 64
WINDOW = 128
D_FF = 2816
FFN_CONV = 3
PLE_DIM = 256
EPS = 1e-6
DEPTH = 2

ADAM_LR = 0.001
ADAM_B1 = 0.9
ADAM_B2 = 0.999
ADAM_EPS = 1e-08
ADAM_WD = 0.01
ADAM_STEP = 10

LANES = 128
SUBLANES = 8
BF16_ROWS = 16
CHUNK = 128
VMEM_LIMIT = 56 * 1024 * 1024
NEG = -1e30
N_CHIPS = 4
N_DEV = 8
PACK_COLS = 1024


def _params(sem=None):
    return pltpu.CompilerParams(dimension_semantics=sem, vmem_limit_bytes=VMEM_LIMIT)


def _tile(dim, cap):
    if dim % LANES:
        return dim
    best = LANES
    for t in range(LANES, min(dim, cap) + 1, LANES):
        if dim % t == 0:
            best = t
    return best


def _dot(a, b, dims=(((1,), (0,)), ((), ())), precision=None):
    return lax.dot_general(a, b, dims, precision=precision, preferred_element_type=F32)


NN = (((1,), (0,)), ((), ()))
NT = (((1,), (1,)), ((), ()))
TN = (((0,), (0,)), ((), ()))


def mm(a, b, *, name, ta=False, tb=False, out_dtype=F32, add=None, tm_cap=512, tn_cap=1408, tk_cap=1408,
       n=None, tn=None, tk=None, b_spec=None, o_spec=None, o_shape=None, into=None):
    m, k = (a.shape[1], a.shape[0]) if ta else a.shape
    if b_spec is None:
        n = b.shape[0] if tb else b.shape[1]
        assert (b.shape[1] if tb else b.shape[0]) == k, (a.shape, b.shape, ta, tb)
    tm, tn, tk = _tile(m, tm_cap), tn or _tile(n, tn_cap), tk or _tile(k, tk_cap)
    assert n % tn == 0 and k % tk == 0, (n, tn, k, tk)
    nk = k // tk
    dims = (((0 if ta else 1,), (1 if tb else 0,)), ((), ()))
    has_add = add is not None
    n_in = 2 + has_add + (into is not None)

    def body(*refs):
        a_ref, b_ref = refs[0], refs[1]
        add_ref = refs[2] if has_add else None
        o_ref = refs[n_in]
        part = _dot(a_ref[...].astype(BF16), b_ref[...].astype(BF16), dims)
        if nk == 1:
            if has_add:
                part = part + add_ref[...].astype(F32)
            o_ref[...] = part.astype(o_ref.dtype)
            return
        acc = refs[-1]
        kk = pl.program_id(2)

        @pl.when(kk == 0)
        def _():
            acc[...] = part

        @pl.when(kk > 0)
        def _():
            acc[...] += part

        @pl.when(kk == nk - 1)
        def _():
            r = acc[...]
            if has_add:
                r = r + add_ref[...].astype(F32)
            o_ref[...] = r.astype(o_ref.dtype)

    a_spec = pl.BlockSpec((tk, tm), lambda i, j, kk: (kk, i)) if ta else pl.BlockSpec((tm, tk), lambda i, j, kk: (i, kk))
    if b_spec is None:
        b_spec = pl.BlockSpec((tn, tk), lambda i, j, kk: (j, kk)) if tb else pl.BlockSpec((tk, tn), lambda i, j, kk: (kk, j))
    plain_o = pl.BlockSpec((tm, tn), lambda i, j, kk: (i, j))
    if o_spec is None:
        o_spec, o_shape = plain_o, (m, n)
    in_specs = [a_spec, b_spec] + ([plain_o] if has_add else [])
    args = (a, b) + ((add,) if has_add else ())
    aliases = {}
    if into is not None:
        assert into.shape == tuple(o_shape) and into.dtype == out_dtype, (into.shape, o_shape)
        in_specs.append(pl.BlockSpec(memory_space=pl.ANY))
        args += (into,)
        aliases = {n_in - 1: 0}
    return pl.pallas_call(
        body, grid=(m // tm, n // tn, nk), in_specs=in_specs, out_specs=o_spec,
        out_shape=jax.ShapeDtypeStruct(tuple(o_shape), out_dtype), name=name, input_output_aliases=aliases,
        scratch_shapes=[pltpu.VMEM((tm, tn), F32)] if nk > 1 else [],
        compiler_params=_params(("parallel", "parallel", "arbitrary")),
    )(*args)


def _row_spec(tm, cw, coff):
    return pl.BlockSpec((tm, cw), lambda i, j: (i, j + coff))


def _full_spec(shape):
    return pl.BlockSpec(shape, lambda i, j: (0,) * len(shape))


def tile_map(fn, rows, params, outs, *, tm, ncol, name):
    t = rows[0][0].shape[0]
    nin = len(rows) + len(params)

    def body(*refs):
        res = fn(*[r[...] for r in refs[:nin]])
        res = res if isinstance(res, (tuple, list)) else (res,)
        for o_ref, r in zip(refs[nin:], res):
            o_ref[...] = r.astype(o_ref.dtype)

    in_specs = [_row_spec(tm, cw, coff) for (_, cw, coff) in rows] + [_full_spec(p.shape) for p in params]
    res = pl.pallas_call(
        body, grid=(t // tm, ncol), in_specs=in_specs,
        out_specs=[_row_spec(tm, cw, 0) for (cw, _) in outs],
        out_shape=[jax.ShapeDtypeStruct((t, cw * ncol), dt) for (cw, dt) in outs], name=name,
        compiler_params=_params(("parallel", "parallel")),
    )(*[r[0] for r in rows], *params)
    return res


def tile_vjp(fn, rows, params, cts, *, n_diff, tm, ncol, name, add=None):
    t = rows[0][0].shape[0]
    nr, npar, nct = len(rows), len(params), len(cts)
    has_add = add is not None

    def body(*refs):
        vals = [r[...] for r in refs[:nr + npar + nct + (1 if has_add else 0)]]
        diff, rest, pars = vals[:n_diff], vals[n_diff:nr], vals[nr:nr + npar]
        ctv = vals[nr + npar:nr + npar + nct]
        outs_ref = refs[nr + npar + nct + (1 if has_add else 0):]

        def f(*a):
            res = fn(*a[:n_diff], *rest, *a[n_diff:])
            return tuple(res) if isinstance(res, (tuple, list)) else (res,)

        primal, vjp = jax.vjp(f, *[d.astype(F32) for d in diff], *pars)
        grads = vjp(tuple(c.astype(o.dtype) for c, o in zip(ctv, primal)))
        for q in range(n_diff):
            g = grads[q]
            if has_add and q == 0:
                g = g + vals[-1]
            outs_ref[q][...] = g.astype(outs_ref[q].dtype)
        first = (pl.program_id(0) == 0) & (pl.program_id(1) == 0)
        for q in range(npar):
            o_ref, g = outs_ref[n_diff + q], grads[n_diff + q]

            @pl.when(first)
            def _(o_ref=o_ref, g=g):
                o_ref[...] = g

            @pl.when(jnp.logical_not(first))
            def _(o_ref=o_ref, g=g):
                o_ref[...] += g

    ins = list(rows) + [None] * 0
    in_specs = [_row_spec(tm, cw, coff) for (_, cw, coff) in rows] + [_full_spec(p.shape) for p in params]
    in_specs += [_row_spec(tm, cw, coff) for (_, cw, coff) in cts]
    args = [r[0] for r in rows] + list(params) + [c[0] for c in cts]
    if has_add:
        in_specs.append(_row_spec(tm, add[1], add[2]))
        args.append(add[0])
    out_specs = [_row_spec(tm, rows[q][1], 0) for q in range(n_diff)] + [_full_spec(p.shape) for p in params]
    out_shape = [jax.ShapeDtypeStruct((t, rows[q][1] * ncol), F32) for q in range(n_diff)]
    out_shape += [jax.ShapeDtypeStruct(p.shape, F32) for p in params]
    del ins
    res = pl.pallas_call(
        body, grid=(t // tm, ncol), in_specs=in_specs, out_specs=out_specs, out_shape=out_shape, name=name,
        compiler_params=_params(("arbitrary", "arbitrary")),
    )(*args)
    return res[:n_diff], res[n_diff:]


def _silu(x):
    return x * jax.nn.sigmoid(x)


def _f_norm(h, w):
    return h * lax.rsqrt(jnp.mean(h * h, axis=-1, keepdims=True) + EPS) * w


def _f_gnorm(o, z, w):
    return _f_norm(o, w) * _silu(z)


def _f_act(gate, val):
    return _silu(gate) * val


def _f_ple(gl, pe):
    return jax.nn.sigmoid(gl) * pe


def _f_betag(pt, alog, dtb):
    lane = lax.broadcasted_iota(jnp.int32, (1, LANES), 1)
    z = pt + dtb
    softplus = jnp.maximum(z, 0.0) + jnp.log(1.0 + jnp.exp(-jnp.abs(z)))
    g = -jnp.exp(alog) * softplus
    return jnp.where(lane < N_HEADS_A, jax.nn.sigmoid(pt), jnp.where(lane < 2 * N_HEADS_A, g, 0.0))


def act_bwd(uc, dact, *, name):
    t = uc.shape[0]
    tm, cw = min(128, t), 256

    def body(uc_ref, d_ref, o_ref):
        for cb in range(D_FF // cw):
            gcols, vcols = slice(cb * cw, (cb + 1) * cw), slice(D_FF + cb * cw, D_FF + (cb + 1) * cw)
            _, vjp = jax.vjp(_f_act, uc_ref[:, gcols], uc_ref[:, vcols])
            o_ref[:, gcols], o_ref[:, vcols] = vjp(d_ref[:, gcols])

    return pl.pallas_call(
        body, grid=(t // tm,),
        in_specs=[pl.BlockSpec((tm, 2 * D_FF), lambda i: (i, 0)), pl.BlockSpec((tm, D_FF), lambda i: (i, 0))],
        out_specs=pl.BlockSpec((tm, 2 * D_FF), lambda i: (i, 0)),
        out_shape=jax.ShapeDtypeStruct((t, 2 * D_FF), F32), name=name, compiler_params=_params(("parallel",)),
    )(uc, dact)


CONV_TM = 256
CONV_CW = 512


def _shift_down(x, prev, s, row):
    rp = jnp.tile(pltpu.roll(prev, s, 0), (x.shape[0] // SUBLANES, 1))
    return jnp.where(row < s, rp, pltpu.roll(x, s, 0))


def _shift_up(x, nxt, s, row):
    tm = x.shape[0]
    rn = jnp.tile(pltpu.roll(nxt, SUBLANES - s, 0), (tm // SUBLANES, 1))
    return jnp.where(row >= tm - s, rn, pltpu.roll(x, tm - s, 0))


def conv_fwd(x, w, *, name):
    t = x.shape[0]
    k, c = w.shape
    tm, cw = min(CONV_TM, t), CONV_CW
    nb8 = tm // SUBLANES

    def body(x_ref, p_ref, w_ref, o_ref):
        i = pl.program_id(1)
        xv = x_ref[...]
        prev = jnp.where(i > 0, p_ref[...], 0.0)
        row = lax.broadcasted_iota(jnp.int32, xv.shape, 0)
        y = xv * w_ref[pl.ds(k - 1, 1), :]
        for s in range(1, k):
            y = y + _shift_down(xv, prev, s, row) * w_ref[pl.ds(k - 1 - s, 1), :]
        o_ref[...] = y

    return pl.pallas_call(
        body, grid=(c // cw, t // tm),
        in_specs=[pl.BlockSpec((tm, cw), lambda j, i: (i, j)),
                  pl.BlockSpec((SUBLANES, cw), lambda j, i: (jnp.maximum(i * nb8 - 1, 0), j)),
                  pl.BlockSpec((k, cw), lambda j, i: (0, j))],
        out_specs=pl.BlockSpec((tm, cw), lambda j, i: (i, j)),
        out_shape=jax.ShapeDtypeStruct((t, c), F32), name=name,
        compiler_params=_params(("parallel", "parallel")),
    )(x, x, w)


def conv_bwd(dy, x, w, *, name):
    t = x.shape[0]
    k, c = w.shape
    tm, cw = min(CONV_TM, t), CONV_CW
    nb8 = tm // SUBLANES
    ni = t // tm

    def body(dy_ref, dn_ref, x_ref, p_ref, w_ref, dx_ref, dw_ref):
        i = pl.program_id(1)
        dyv, xv = dy_ref[...], x_ref[...]
        nxt = jnp.where(i < ni - 1, dn_ref[...], 0.0)
        prev = jnp.where(i > 0, p_ref[...], 0.0)
        row = lax.broadcasted_iota(jnp.int32, xv.shape, 0)
        dx = dyv * w_ref[pl.ds(k - 1, 1), :]
        dws = [jnp.sum(dyv * xv, axis=0, keepdims=True)]
        for s in range(1, k):
            dx = dx + _shift_up(dyv, nxt, s, row) * w_ref[pl.ds(k - 1 - s, 1), :]
            dws.append(jnp.sum(dyv * _shift_down(xv, prev, s, row), axis=0, keepdims=True))
        dx_ref[...] = dx
        for s in range(k):
            @pl.when(i == 0)
            def _(s=s):
                dw_ref[pl.ds(k - 1 - s, 1), :] = dws[s]

            @pl.when(i > 0)
            def _(s=s):
                dw_ref[pl.ds(k - 1 - s, 1), :] += dws[s]

    return pl.pallas_call(
        body, grid=(c // cw, ni),
        in_specs=[pl.BlockSpec((tm, cw), lambda j, i: (i, j)),
                  pl.BlockSpec((SUBLANES, cw), lambda j, i: (jnp.minimum((i + 1) * nb8, t // SUBLANES - 1), j)),
                  pl.BlockSpec((tm, cw), lambda j, i: (i, j)),
                  pl.BlockSpec((SUBLANES, cw), lambda j, i: (jnp.maximum(i * nb8 - 1, 0), j)),
                  pl.BlockSpec((k, cw), lambda j, i: (0, j))],
        out_specs=[pl.BlockSpec((tm, cw), lambda j, i: (i, j)), pl.BlockSpec((k, cw), lambda j, i: (0, j))],
        out_shape=[jax.ShapeDtypeStruct((t, c), F32), jax.ShapeDtypeStruct((k, c), F32)], name=name,
        compiler_params=_params(("parallel", "arbitrary")),
    )(dy, dy, x, x, w)


def _each(f, *lists):
    return [f(*a) for a in zip(*lists)]


@jax.custom_vjp
def _inv_unit_lower(lms):
    return _inv_blocks(lms)


def _inv_blocks(lms):
    c = lms[0].shape[0]
    ri = lax.broadcasted_iota(jnp.int32, (c, c), 0)
    ci = lax.broadcasted_iota(jnp.int32, (c, c), 1)
    eye = (ri == ci).astype(F32)
    dms = _each(lambda lm: eye - jnp.where((ri >> 1) == (ci >> 1), lm, 0.0), lms)
    for lv in range(1, int(math.log2(c))):
        below = ((ri >> (lv + 1)) == (ci >> (lv + 1))) & ((ri >> lv) != (ci >> lv))
        dbs = _each(lambda dm: dm.astype(BF16), dms)
        ods = _each(lambda lm, db: _dot(jnp.where(below, lm, 0.0).astype(BF16), db).astype(BF16), lms, dbs)
        dms = _each(lambda dm, db, od: dm - _dot(db, od), dms, dbs, ods)
    return dms


def _inv_fwd(lms):
    tms = _inv_blocks(lms)
    return tms, tms


def _inv_bwd(tms, dts):
    tbs = _each(lambda tm: tm.astype(BF16), tms)
    mid = _each(lambda tb, dt: _dot(tb, dt.astype(BF16), TN).astype(BF16), tbs, dts)
    return (_each(lambda m, tb: -_dot(m, tb, NT), mid, tbs),)


_inv_unit_lower.defvjp(_inv_fwd, _inv_bwd)


def _l2n(x):
    return x * lax.rsqrt(jnp.sum(x * x, axis=-1, keepdims=True) + EPS)


def _prep_fn(cqs, cks, cvs, bg, sel_b, sel_g):
    c = cqs[0].shape[0]
    ri = lax.broadcasted_iota(jnp.int32, (c, c), 0)
    ci = lax.broadcasted_iota(jnp.int32, (c, c), 1)
    eye = (ri == ci).astype(F32)
    incl, strict = ci <= ri, ci < ri
    last = lax.broadcasted_iota(jnp.int32, (c, 1), 0) == c - 1
    to_row = lambda col: jnp.sum(col * eye, axis=0, keepdims=True)
    qs = _each(lambda a: _l2n(_silu(a)) * (HEAD_DIM_A ** -0.5), cqs)
    ks = _each(lambda a: _l2n(_silu(a)), cks)
    vbs = _each(lambda a: _silu(a).astype(BF16), cvs)
    betas = _each(lambda m: jnp.sum(bg * m, axis=1, keepdims=True), sel_b)
    gs = _each(lambda m: jnp.sum(bg * m, axis=1, keepdims=True), sel_g)
    gcss = _each(lambda g: jnp.sum(jnp.where(incl, to_row(g), 0.0), axis=1, keepdims=True), gs)
    gtots = _each(lambda gcs: jnp.sum(jnp.where(last, gcs, 0.0), axis=0, keepdims=True), gcss)
    decays = _each(lambda gcs: jnp.exp(jnp.where(incl, gcs - to_row(gcs), NEG)), gcss)
    kbs = _each(lambda k: k.astype(BF16), ks)
    lms = _each(lambda beta, kb, dec: jnp.where(strict, beta * _dot(kb, kb, NT) * dec, 0.0), betas, kbs, decays)
    ams = _each(lambda tm, beta: (tm * to_row(beta)).astype(BF16), _inv_unit_lower(lms), betas)
    gams = _each(jnp.exp, gcss)
    u0s = _each(_dot, ams, vbs)
    wks = _each(lambda am, gam, k: _dot(am, (gam * k).astype(BF16)), ams, gams, ks)
    qks = _each(lambda q, kb, dec: _dot(q.astype(BF16), kb, NT) * dec, qs, kbs, decays)
    qds = _each(lambda q, gam: q * gam, qs, gams)
    kds = _each(lambda k, gtot, gcs: k * jnp.exp(gtot - gcs), ks, gtots, gcss)
    gls = _each(lambda gtot: jnp.exp(gtot) * jnp.ones((SUBLANES, LANES), F32), gtots)
    return u0s, wks, qds, kds, qks, gls


def _head_masks(h):
    lane = lax.broadcasted_iota(jnp.int32, (1, LANES), 1)
    return (lane == h).astype(F32), (lane == h + N_HEADS_A).astype(F32)


def _hsl(j):
    return slice(j * HEAD_DIM_A, (j + 1) * HEAD_DIM_A)


def delta_prep(cqkv, bg, *, name):
    t = cqkv.shape[0]
    nh, hd, n = N_HEADS_A, HEAD_DIM_A, t // CHUNK

    def body(cq_ref, ck_ref, cv_ref, bg_ref, u0_ref, wk_ref, qd_ref, kd_ref, qk_ref, gl_ref):
        heads = range(nh)
        masks = [_head_masks(j) for j in heads]
        res = _prep_fn([cq_ref[:, _hsl(j)] for j in heads], [ck_ref[:, _hsl(j)] for j in heads],
                       [cv_ref[:, _hsl(j)] for j in heads], bg_ref[...], [m[0] for m in masks], [m[1] for m in masks])
        for o_ref, rs in zip((u0_ref, wk_ref, qd_ref, kd_ref, qk_ref), res[:5]):
            for j in heads:
                o_ref[:, _hsl(j)] = rs[j]
        for j in heads:
            gl_ref[j * SUBLANES:(j + 1) * SUBLANES, :] = res[5][j]

    blk = lambda off: pl.BlockSpec((CHUNK, nh * hd), lambda i: (i, off))
    return pl.pallas_call(
        body, grid=(n,),
        in_specs=[blk(0), blk(1), blk(2), pl.BlockSpec((CHUNK, LANES), lambda i: (i, 0))],
        out_specs=[blk(0)] * 5 + [pl.BlockSpec((nh * SUBLANES, LANES), lambda i: (i, 0))],
        out_shape=[jax.ShapeDtypeStruct((t, nh * hd), F32)] * 5 + [jax.ShapeDtypeStruct((n * nh * SUBLANES, LANES), F32)],
        name=name, compiler_params=_params(("parallel",)),
    )(cqkv, cqkv, cqkv, bg)


def delta_prep_bwd(cqkv, bg, cts, *, name):
    t = cqkv.shape[0]
    nh, hd, n = N_HEADS_A, HEAD_DIM_A, t // CHUNK

    def body(cq_ref, ck_ref, cv_ref, bg_ref, c0, c1, c2, c3, c4, c5, dc_ref, dbg_ref):
        heads = range(nh)
        masks = [_head_masks(j) for j in heads]
        _, vjp = jax.vjp(lambda a, b, c, d: _prep_fn(a, b, c, d, [m[0] for m in masks], [m[1] for m in masks]),
                         [cq_ref[:, _hsl(j)] for j in heads], [ck_ref[:, _hsl(j)] for j in heads],
                         [cv_ref[:, _hsl(j)] for j in heads], bg_ref[...])
        cts = tuple([c[:, _hsl(j)] for j in heads] for c in (c0, c1, c2, c3, c4))
        dqs, dks, dvs, dbg = vjp(cts + ([c5[j * SUBLANES:(j + 1) * SUBLANES, :] for j in heads],))
        for part, ds in enumerate((dqs, dks, dvs)):
            for j in heads:
                dc_ref[:, _hsl(part * nh + j)] = ds[j]
        dbg_ref[...] = dbg

    blk = lambda off: pl.BlockSpec((CHUNK, nh * hd), lambda i: (i, off))
    gl_spec = pl.BlockSpec((nh * SUBLANES, LANES), lambda i: (i, 0))
    bg_spec = pl.BlockSpec((CHUNK, LANES), lambda i: (i, 0))
    return pl.pallas_call(
        body, grid=(n,),
        in_specs=[blk(0), blk(1), blk(2), bg_spec] + [blk(0)] * 5 + [gl_spec],
        out_specs=[pl.BlockSpec((CHUNK, 3 * nh * hd), lambda i: (i, 0)), bg_spec],
        out_shape=[jax.ShapeDtypeStruct((t, 3 * nh * hd), F32), jax.ShapeDtypeStruct((t, LANES), F32)],
        name=name, compiler_params=_params(("parallel",)),
    )(cqkv, cqkv, cqkv, bg, *cts)


def delta_scan(u0, wk, qd, kd, qk, gl, *, name):
    t = u0.shape[0]
    nh, hd, n = N_HEADS_A, HEAD_DIM_A, t // CHUNK

    def body(u0_ref, wk_ref, qd_ref, kd_ref, qk_ref, gl_ref, o_ref, sin_ref, s_ref):
        @pl.when(pl.program_id(0) == 0)
        def _():
            s_ref[...] = jnp.zeros_like(s_ref)

        heads = list(range(nh))
        cols = lambda ref: [ref[:, _hsl(h)].astype(BF16) for h in heads]
        ss = [s_ref[h] for h in heads]
        for h in heads:
            sin_ref[h] = ss[h]
        sbs = _each(lambda s: s.astype(BF16), ss)
        ubs = _each(lambda h, wkb, sb: (u0_ref[:, _hsl(h)] - _dot(wkb, sb)).astype(BF16), heads, cols(wk_ref), sbs)
        os_ = _each(lambda qdb, sb, qkb, ub: _dot(qdb, sb) + _dot(qkb, ub), cols(qd_ref), sbs, cols(qk_ref), ubs)
        sn = _each(lambda h, s, kdb, ub: gl_ref[pl.ds(h * SUBLANES, 1), :] * s + _dot(kdb, ub, TN), heads, ss, cols(kd_ref), ubs)
        for h in heads:
            o_ref[:, _hsl(h)] = os_[h]
            s_ref[h] = sn[h]

    blk = pl.BlockSpec((CHUNK, nh * hd), lambda i: (i, 0))
    return pl.pallas_call(
        body, grid=(n,),
        in_specs=[blk] * 5 + [pl.BlockSpec((nh * SUBLANES, LANES), lambda i: (i, 0))],
        out_specs=[blk, pl.BlockSpec((None, nh, hd, hd), lambda i: (i, 0, 0, 0))],
        out_shape=[jax.ShapeDtypeStruct((t, nh * hd), F32), jax.ShapeDtypeStruct((n, nh, hd, hd), F32)],
        scratch_shapes=[pltpu.VMEM((nh, hd, hd), F32)], name=name,
        compiler_params=_params(("arbitrary",)),
    )(u0, wk, qd, kd, qk, gl)


def delta_scan_bwd(do, u0, wk, qd, kd, qk, gl, s_in, *, name):
    t = u0.shape[0]
    nh, hd, n = N_HEADS_A, HEAD_DIM_A, t // CHUNK

    def body(do_ref, u0_ref, wk_ref, qd_ref, kd_ref, qk_ref, gl_ref, sin_ref,
             du0_ref, dwk_ref, dqd_ref, dkd_ref, dqk_ref, dgl_ref, ds_ref):
        @pl.when(pl.program_id(0) == 0)
        def _():
            ds_ref[...] = jnp.zeros_like(ds_ref)

        corner = (lax.broadcasted_iota(jnp.int32, (SUBLANES, LANES), 0) == 0) & (lax.broadcasted_iota(jnp.int32, (SUBLANES, LANES), 1) == 0)
        heads = list(range(nh))
        cols = lambda ref: [ref[:, _hsl(h)].astype(BF16) for h in heads]
        ss, dss = [sin_ref[h] for h in heads], [ds_ref[h] for h in heads]
        sbs, dsbs = _each(lambda s: s.astype(BF16), ss), _each(lambda d: d.astype(BF16), dss)
        dobs, wkbs, qdbs, kdbs, qkbs = cols(do_ref), cols(wk_ref), cols(qd_ref), cols(kd_ref), cols(qk_ref)
        ubs = _each(lambda h, wkb, sb: (u0_ref[:, _hsl(h)] - _dot(wkb, sb)).astype(BF16), heads, wkbs, sbs)
        dus = _each(lambda qkb, dob, kdb, dsb: _dot(qkb, dob, TN) + _dot(kdb, dsb), qkbs, dobs, kdbs, dsbs)
        dubs = _each(lambda du: du.astype(BF16), dus)
        dwks = _each(lambda dub, sb: -_dot(dub, sb, NT), dubs, sbs)
        dqds = _each(lambda dob, sb: _dot(dob, sb, NT), dobs, sbs)
        dkds = _each(lambda ub, dsb: _dot(ub, dsb, NT), ubs, dsbs)
        dqks = _each(lambda dob, ub: _dot(dob, ub, NT), dobs, ubs)
        dgls = _each(lambda s, d: jnp.sum(jnp.sum(s * d, axis=1, keepdims=True), axis=0, keepdims=True), ss, dss)
        dsn = _each(lambda h, d, qdb, dob, wkb, dub: gl_ref[pl.ds(h * SUBLANES, 1), :] * d + _dot(qdb, dob, TN) - _dot(wkb, dub, TN),
                    heads, dss, qdbs, dobs, wkbs, dubs)
        for h in heads:
            du0_ref[:, _hsl(h)] = dus[h]
            dwk_ref[:, _hsl(h)] = dwks[h]
            dqd_ref[:, _hsl(h)] = dqds[h]
            dkd_ref[:, _hsl(h)] = dkds[h]
            dqk_ref[:, _hsl(h)] = dqks[h]
            dgl_ref[h * SUBLANES:(h + 1) * SUBLANES, :] = jnp.where(corner, dgls[h], 0.0)
            ds_ref[h] = dsn[h]

    blk = pl.BlockSpec((CHUNK, nh * hd), lambda i: (n - 1 - i, 0))
    gl_spec = pl.BlockSpec((nh * SUBLANES, LANES), lambda i: (n - 1 - i, 0))
    return pl.pallas_call(
        body, grid=(n,),
        in_specs=[blk] * 6 + [gl_spec, pl.BlockSpec((None, nh, hd, hd), lambda i: (n - 1 - i, 0, 0, 0))],
        out_specs=[blk] * 5 + [gl_spec],
        out_shape=[jax.ShapeDtypeStruct((t, nh * hd), F32)] * 5 + [jax.ShapeDtypeStruct((n * nh * SUBLANES, LANES), F32)],
        scratch_shapes=[pltpu.VMEM((nh, hd, hd), F32)], name=name,
        compiler_params=_params(("arbitrary",)),
    )(do, u0, wk, qd, kd, qk, gl, s_in)


N_PAIRS = N_HEADS_B // 2
PAIRS_PER_KV = N_PAIRS // N_KV_B


def _psl(j):
    return slice(j * LANES, (j + 1) * LANES)


def _att_fn(qps, kc, kp, vc, vp, sinks, kvf, first):
    w = WINDOW
    lane = lax.broadcasted_iota(jnp.int32, (1, LANES), 1)
    lo = (lane < HEAD_DIM_B).astype(F32)
    qi = lax.broadcasted_iota(jnp.int32, (w, w), 0)
    kj = lax.broadcasted_iota(jnp.int32, (w, w), 1)
    dist_c = (qi - kj).astype(F32)
    valid_c = kj <= qi
    valid_p = (kj > qi) & (first < 0.5)
    kcb, kpb, vcb, vpb = (a.astype(BF16) for a in (kc, kp, vc, vp))
    scale = HEAD_DIM_B ** -0.5
    heads = [(j, half) for j in range(PAIRS_PER_KV) for half in range(2)]
    hmasks = [lo if half == 0 else 1.0 - lo for _, half in heads]
    hds = [2.0 * (PAIRS_PER_KV * kvf + j) + half for j, half in heads]
    slopes = _each(lambda hd: jnp.exp(-(hd + 1.0) * (8.0 / N_HEADS_B * math.log(2.0))), hds)
    snks = _each(lambda hd: jnp.sum(sinks * (lane.astype(F32) == hd).astype(F32), axis=1, keepdims=True), hds)
    qhs = _each(lambda jh, hm: (qps[jh[0]] * hm).astype(BF16), heads, hmasks)
    lcs = _each(lambda qh, sl: jnp.where(valid_c, _dot(qh, kcb, NT) * scale - sl * dist_c, NEG), qhs, slopes)
    lps = _each(lambda qh, sl: jnp.where(valid_p, _dot(qh, kpb, NT) * scale - sl * (dist_c + w), NEG), qhs, slopes)
    ms = _each(lambda lc, lp, sk: lax.stop_gradient(jnp.maximum(jnp.maximum(jnp.max(lc, axis=1, keepdims=True),
                                                                            jnp.max(lp, axis=1, keepdims=True)), sk)), lcs, lps, snks)
    ecs = _each(lambda lc, m: jnp.exp(lc - m), lcs, ms)
    eps = _each(lambda lp, m: jnp.exp(lp - m), lps, ms)
    invs = _each(lambda ec, ep, sk, m: 1.0 / (jnp.sum(ec, axis=1, keepdims=True) + jnp.sum(ep, axis=1, keepdims=True) + jnp.exp(sk - m)),
                 ecs, eps, snks, ms)
    ohs = _each(lambda ec, ep, inv, hm: (_dot((ec * inv).astype(BF16), vcb) + _dot((ep * inv).astype(BF16), vpb)) * hm,
                ecs, eps, invs, hmasks)
    return [ohs[2 * j] + ohs[2 * j + 1] for j in range(PAIRS_PER_KV)]


def _scalar11(v):
    return jnp.full((1, 1), v, F32)


def swa_fwd(qsrc, kd, vd, sinks, *, name):
    t = kd.shape[0]
    nb = t // WINDOW

    def body(q_ref, kc_ref, kp_ref, vc_ref, vp_ref, s_ref, o_ref):
        first = _scalar11((pl.program_id(0) == 0).astype(F32))
        kvf = _scalar11(pl.program_id(1).astype(F32))
        outs = _att_fn([q_ref[:, _psl(j)] for j in range(PAIRS_PER_KV)], kc_ref[...], kp_ref[...], vc_ref[...], vp_ref[...],
                       s_ref[...], kvf, first)
        for j in range(PAIRS_PER_KV):
            o_ref[:, _psl(j)] = outs[j].astype(o_ref.dtype)

    cur = pl.BlockSpec((WINDOW, LANES), lambda i, kv: (i, kv))
    prev = pl.BlockSpec((WINDOW, LANES), lambda i, kv: (jnp.maximum(i - 1, 0), kv))
    qs = pl.BlockSpec((WINDOW, PAIRS_PER_KV * LANES), lambda i, kv: (i, kv))
    return pl.pallas_call(
        body, grid=(nb, N_KV_B),
        in_specs=[qs, cur, prev, cur, prev, pl.BlockSpec((1, LANES), lambda i, kv: (0, 0))],
        out_specs=qs, out_shape=jax.ShapeDtypeStruct((t, N_PAIRS * LANES), BF16), name=name,
        compiler_params=_params(("parallel", "parallel")),
    )(qsrc, kd, kd, vd, vd, sinks)


def swa_bwd(do, qsrc, kd, vd, sinks, *, name):
    t = kd.shape[0]
    nb = t // WINDOW

    def body(do_ref, q_ref, kc_ref, kp_ref, vc_ref, vp_ref, s_ref, dq_ref, dk_ref, dv_ref, ds_ref, carry_k, carry_v):
        step, kv = pl.program_id(0), pl.program_id(1)
        first = _scalar11((step == nb - 1).astype(F32))

        @pl.when((step == 0) & (kv == 0))
        def _():
            carry_k[...] = jnp.zeros_like(carry_k)
            carry_v[...] = jnp.zeros_like(carry_v)
            ds_ref[...] = jnp.zeros_like(ds_ref)

        kvf = _scalar11(kv.astype(F32))
        pairs = range(PAIRS_PER_KV)
        _, vjp = jax.vjp(lambda *a: _att_fn(*a, kvf, first), [q_ref[:, _psl(j)] for j in pairs],
                         kc_ref[...], kp_ref[...], vc_ref[...], vp_ref[...], s_ref[...])
        dqs, dkc, dkp, dvc, dvp, dsk = vjp([do_ref[:, _psl(j)].astype(F32) for j in pairs])
        for j in pairs:
            dq_ref[:, _psl(j)] = dqs[j]
        ds_ref[...] += dsk
        fold = lambda g: g + pltpu.roll(g, HEAD_DIM_B, 1)
        dk_ref[...] = fold(dkc + carry_k[kv])
        dv_ref[...] = fold(dvc + carry_v[kv])
        carry_k[kv] = dkp
        carry_v[kv] = dvp

    rev = lambda i: nb - 1 - i
    cur = pl.BlockSpec((WINDOW, LANES), lambda i, kv: (rev(i), kv))
    prev = pl.BlockSpec((WINDOW, LANES), lambda i, kv: (jnp.maximum(rev(i) - 1, 0), kv))
    qs = pl.BlockSpec((WINDOW, PAIRS_PER_KV * LANES), lambda i, kv: (rev(i), kv))
    sk = pl.BlockSpec((1, LANES), lambda i, kv: (0, 0))
    return pl.pallas_call(
        body, grid=(nb, N_KV_B),
        in_specs=[qs, qs, cur, prev, cur, prev, sk],
        out_specs=[qs, cur, cur, sk],
        out_shape=[jax.ShapeDtypeStruct((t, N_PAIRS * LANES), F32), jax.ShapeDtypeStruct((t, N_KV_B * LANES), F32),
                   jax.ShapeDtypeStruct((t, N_KV_B * LANES), F32), jax.ShapeDtypeStruct((1, LANES), F32)],
        scratch_shapes=[pltpu.VMEM((N_KV_B, WINDOW, LANES), F32), pltpu.VMEM((N_KV_B, WINDOW, LANES), F32)],
        name=name, compiler_params=_params(("arbitrary", "arbitrary")),
    )(do, qsrc, kd, kd, vd, vd, sinks)


def loss_head(h, tgt, w, *, name):
    t, d = h.shape
    tm = min(256, t)

    def body(h_ref, t_ref, w_ref, dh_ref, dw_ref, l_ref):
        tg = t_ref[...]

        def f(hv, wv):
            err = _f_norm(hv, wv) - tg
            return 0.5 * jnp.sum(jnp.sum(err * err, axis=1, keepdims=True), axis=0, keepdims=True) * (1.0 / d)

        lv, vjp = jax.vjp(f, h_ref[...], w_ref[...])
        dh, dw = vjp(jnp.ones((1, 1), F32))
        dh_ref[...] = dh
        first = pl.program_id(0) == 0

        @pl.when(first)
        def _():
            dw_ref[...] = dw
            l_ref[...] = lv * jnp.ones((1, LANES), F32)

        @pl.when(jnp.logical_not(first))
        def _():
            dw_ref[...] += dw
            l_ref[...] += lv * jnp.ones((1, LANES), F32)

    rows = pl.BlockSpec((tm, d), lambda i: (i, 0))
    one = lambda c: pl.BlockSpec((1, c), lambda i: (0, 0))
    return pl.pallas_call(
        body, grid=(t // tm,), in_specs=[rows, rows, one(d)], out_specs=[rows, one(d), one(LANES)],
        out_shape=[jax.ShapeDtypeStruct((t, d), F32), jax.ShapeDtypeStruct((1, d), F32), jax.ShapeDtypeStruct((1, LANES), F32)],
        name=name, compiler_params=_params(("arbitrary",)),
    )(h, tgt, w)


def adamw(w, g, m, v, *, name):
    r, c = w.shape
    tr = r
    if r % SUBLANES == 0:
        for cand in range(SUBLANES, min(r, 256) + 1, SUBLANES):
            if r % cand == 0:
                tr = cand

    def body(w_ref, g_ref, m_ref, v_ref, d_ref, mo_ref, vo_ref):
        gv = g_ref[...]
        mn = ADAM_B1 * m_ref[...] + (1.0 - ADAM_B1) * gv
        vn = ADAM_B2 * v_ref[...] + (1.0 - ADAM_B2) * jnp.square(gv)
        m_hat = mn / (1.0 - ADAM_B1 ** ADAM_STEP)
        v_hat = vn / (1.0 - ADAM_B2 ** ADAM_STEP)
        d_ref[...] = -ADAM_LR * (m_hat / (jnp.sqrt(v_hat) + ADAM_EPS) + ADAM_WD * w_ref[...])
        mo_ref[...] = mn
        vo_ref[...] = vn

    spec = pl.BlockSpec((tr, c), lambda i: (i, 0))
    return pl.pallas_call(
        body, grid=(r // tr,), in_specs=[spec] * 4, out_specs=[spec] * 3,
        out_shape=[jax.ShapeDtypeStruct((r, c), F32)] * 3, name=name, compiler_params=_params(("parallel",)),
    )(w, g, m, v)


def _place():
    return lax.axis_index("x"), lax.axis_index("y"), lax.axis_index("c")


def allgather8(blk, *, name):
    def body(x_ref, out_ref, send_sems, recv_sems, local_sem):
        x, y, c = _place()
        me = 4 * x + 2 * y + c
        mine = pltpu.make_async_copy(x_ref, out_ref.at[me], local_sem)
        mine.start()
        sent = []
        for k in range(1, N_DEV):
            to = (x ^ ((k >> 2) & 1), y ^ ((k >> 1) & 1), c ^ (k & 1))
            cp = pltpu.make_async_remote_copy(src_ref=x_ref, dst_ref=out_ref.at[me], send_sem=send_sems.at[k - 1],
                                              recv_sem=recv_sems.at[k - 1], device_id=to, device_id_type=MESH)
            cp.start()
            sent.append(cp)
        for k in range(1, N_DEV):
            frm = me ^ k
            pltpu.make_async_remote_copy(src_ref=x_ref, dst_ref=out_ref.at[frm], send_sem=send_sems.at[k - 1],
                                         recv_sem=recv_sems.at[k - 1], device_id=(x, y, c), device_id_type=MESH).wait_recv()
        for cp in sent:
            cp.wait_send()
        mine.wait()

    vm = pl.BlockSpec(memory_space=pltpu.VMEM)
    return pl.pallas_call(
        body, in_specs=[vm], out_specs=vm, out_shape=jax.ShapeDtypeStruct((N_DEV,) + blk.shape, blk.dtype), name=name,
        scratch_shapes=[pltpu.SemaphoreType.DMA((N_DEV - 1,)), pltpu.SemaphoreType.DMA((N_DEV - 1,)), pltpu.SemaphoreType.DMA],
    )(blk)


def _other_chips(x, y):
    return [(1 - x, y), (x, 1 - y), (1 - x, 1 - y)]


def _hbm_call(body, ins, out_shapes, n_sems, name):
    hbm = pl.BlockSpec(memory_space=pl.ANY)
    return pl.pallas_call(
        body, in_specs=[hbm] * len(ins), out_specs=[hbm] * len(out_shapes), out_shape=out_shapes, name=name,
        scratch_shapes=[pltpu.SemaphoreType.DMA((n_sems,)), pltpu.SemaphoreType.DMA((n_sems,))],
    )(*ins)


def _half_rows(c, rh):
    return pl.ds(pl.multiple_of(c * rh, BF16_ROWS), rh)


def gather_units(units, *, name):
    nu = len(units)
    shapes = []
    for arr, layer_major in units:
        r, cols = arr.shape
        shapes.append(jax.ShapeDtypeStruct((2, N_CHIPS, r // 2, cols) if layer_major else (N_CHIPS, r, cols), arr.dtype))

    def body(*refs):
        in_refs, out_refs, send_sems, recv_sems = refs[:nu], refs[nu:2 * nu], refs[2 * nu], refs[2 * nu + 1]
        x, y, c = _place()
        me_chip = 2 * x + y
        sib = (x, y, 1 - c)
        chips = _other_chips(x, y)

        def copy(k, src, dst, to):
            return pltpu.make_async_remote_copy(src_ref=src, dst_ref=dst, send_sem=send_sems.at[k], recv_sem=recv_sems.at[k],
                                                device_id=to, device_id_type=MESH)

        first, passed, landing = [], [], []
        for u, (arr, layer_major) in enumerate(units):
            rh = arr.shape[0] // 2
            out_ref = out_refs[u]
            slot = (lambda chip, half, o=out_ref: o.at[half, chip]) if layer_major else \
                   (lambda chip, half, o=out_ref, rh=rh: o.at[chip, _half_rows(half, rh), :])
            my_half = in_refs[u].at[_half_rows(c, rh), :]
            for j, (cx, cy) in enumerate(chips):
                k = 6 * u + j
                first.append(copy(k, my_half, slot(me_chip, c), (cx, cy, c)))
                passed.append(copy(k + 3, slot(2 * cx + cy, c), slot(2 * cx + cy, c), sib))
                landing.append((copy(k, my_half, slot(2 * cx + cy, c), sib), copy(k + 3, my_half, slot(2 * cx + cy, 1 - c), sib)))
        for cp in first:
            cp.start()
        for (over_ici, _), fwd in zip(landing, passed):
            over_ici.wait_recv()
            fwd.start()
        for _, from_sibling in landing:
            from_sibling.wait_recv()
        for cp in first + passed:
            cp.wait_send()

    return _hbm_call(body, [a for a, _ in units], shapes, 6 * nu, name)


def swap_units(units, *, name):
    nu = len(units)

    def body(*refs):
        g_refs, out_refs, send_sems, recv_sems = refs[:nu], refs[nu:2 * nu], refs[2 * nu], refs[2 * nu + 1]
        x, y, c = _place()
        cps = [pltpu.make_async_remote_copy(src_ref=g_refs[u].at[:, _half_rows(1 - c, units[u].shape[1] // 2), :], dst_ref=out_refs[u],
                                            send_sem=send_sems.at[u], recv_sem=recv_sems.at[u], device_id=(x, y, 1 - c),
                                            device_id_type=MESH) for u in range(nu)]
        for cp in cps:
            cp.start()
        for cp in cps:
            cp.wait()

    shapes = [jax.ShapeDtypeStruct((N_CHIPS, g.shape[1] // 2, g.shape[2]), g.dtype) for g in units]
    return _hbm_call(body, units, shapes, nu, name)


def scatter_units(units, *, name):
    nu = len(units)

    def body(*refs):
        h_refs, out_refs, send_sems, recv_sems = refs[:nu], refs[nu:2 * nu], refs[2 * nu], refs[2 * nu + 1]
        x, y, c = _place()
        cps = [pltpu.make_async_remote_copy(src_ref=h_refs[u].at[2 * cx + cy], dst_ref=out_refs[u].at[j], send_sem=send_sems.at[3 * u + j],
                                            recv_sem=recv_sems.at[3 * u + j], device_id=(cx, cy, c), device_id_type=MESH)
               for u in range(nu) for j, (cx, cy) in enumerate(_other_chips(x, y))]
        for cp in cps:
            cp.start()
        for cp in cps:
            cp.wait()

    shapes = [jax.ShapeDtypeStruct((3,) + h.shape[1:], h.dtype) for h in units]
    return _hbm_call(body, units, shapes, 3 * nu, name)


def join_units(units, *, name):
    nu = len(units)

    def body(*refs):
        h_refs, out_refs, send_sems, recv_sems = refs[:nu], refs[nu:2 * nu], refs[2 * nu], refs[2 * nu + 1]
        x, y, c = _place()
        cps = [pltpu.make_async_remote_copy(src_ref=h_refs[u], dst_ref=out_refs[u], send_sem=send_sems.at[u], recv_sem=recv_sems.at[u],
                                            device_id=(x, y, 1 - c), device_id_type=MESH) for u in range(nu)]
        for cp in cps:
            cp.start()
        for cp in cps:
            cp.wait()

    return _hbm_call(body, units, [jax.ShapeDtypeStruct(h.shape, h.dtype) for h in units], nu, name)


def add_parts(parts, *, out_dtype, name):
    rows, cols = parts[0].shape
    tr = rows
    for cand in range(BF16_ROWS, min(rows, 512) + 1, BF16_ROWS):
        if rows % cand == 0:
            tr = cand

    def body(*refs):
        acc = refs[0][...].astype(F32)
        for r in refs[1:-1]:
            acc = acc + r[...].astype(F32)
        refs[-1][...] = acc.astype(refs[-1].dtype)

    spec = pl.BlockSpec((tr, cols), lambda i: (i, 0))
    return pl.pallas_call(
        body, grid=(rows // tr,), in_specs=[spec] * len(parts), out_specs=spec,
        out_shape=jax.ShapeDtypeStruct((rows, cols), out_dtype), name=name, compiler_params=_params(("parallel",)),
    )(*parts)


def sum8(g, *, name):
    def body(g_ref, o_ref):
        acc = g_ref[0]
        for d in range(1, N_DEV):
            acc = acc + g_ref[d]
        o_ref[...] = acc

    return pl.pallas_call(body, out_shape=jax.ShapeDtypeStruct(g.shape[1:], F32), name=name)(g)


def _dup_halves(a):
    t = a.shape[0]
    a = a.reshape(t, N_KV_B, HEAD_DIM_B)
    return jnp.concatenate([a, a], axis=-1).reshape(t, N_KV_B * LANES)


def _undup(a):
    t = a.shape[0]
    return a.reshape(t, N_KV_B, LANES)[:, :, :HEAD_DIM_B].reshape(t, N_KV_B * HEAD_DIM_B)


def _lane_pad(v, offset=0):
    return jnp.zeros((1, LANES), F32).at[0, offset:offset + v.shape[0]].set(v)


SHARD_UP = 2 * D_FF // N_CHIPS
SHARD_BIN = (N_HEADS_B + 2 * N_KV_B) * HEAD_DIM_B // N_CHIPS
SHARD_PROJ = D_MODEL // N_CHIPS


def local_step(x, p, tgt, sm, wt):
    t = x.shape[0]
    rtm = min(256, t)
    hk = N_HEADS_A * HEAD_DIM_A
    qd_b = N_HEADS_B * HEAD_DIM_B
    kd_b = N_KV_B * HEAD_DIM_B
    gw, gs = {}, {}
    norm = lambda h, w, nm: tile_map(_f_norm, [(h, D_MODEL, 0)], [w], [(D_MODEL, BF16)], tm=rtm, ncol=1, name=nm)[0]

    def norm_bwd(h, w, dy, add, nm):
        (dh,), (dw,) = tile_vjp(_f_norm, [(h, D_MODEL, 0)], [w], [(dy, D_MODEL, 0)], n_diff=1, tm=rtm, ncol=1, name=nm,
                                add=(add, D_MODEL, 0))
        return dh, dw

    spec = pl.BlockSpec
    mtm = _tile(D_MODEL, 512)
    p_bf = p.astype(BF16)
    a_main, a_tail = wt["a_main"], wt["a_tail"]
    alog_p = _lane_pad(sm["a_log"][0], N_HEADS_A)
    dtb_p = _lane_pad(sm["a_dt_bias"][0], N_HEADS_A)
    sinks_p = _lane_pad(sm["b_sinks"][0])
    nw = lambda name, i: sm[name][i:i + 1]

    saved = []
    h = x
    for i in range(DEPTH):
        s = {"h0": h}
        s["hn"] = norm(h, nw("norm_mix", i), f"norm_mix{i}")
        if i % 2 == 0:
            s["pm"] = mm(s["hn"], a_main, name="a_in_main")
            s["pt"] = mm(s["hn"], a_tail, name="a_in_tail")
            s["c"] = conv_fwd(s["pm"], wt["a_conv"], name="a_conv")
            s["bg"] = tile_map(_f_betag, [(s["pt"], LANES, 0)], [alog_p, dtb_p], [(LANES, F32)], tm=rtm, ncol=1, name="a_betag")[0]
            s["prep"] = delta_prep(s["c"], s["bg"], name="a_prep")
            s["o"], s["s_in"] = delta_scan(*s["prep"], name="a_scan")
            s["on"] = tile_map(_f_gnorm, [(s["o"], HEAD_DIM_A, 0), (s["pm"], HEAD_DIM_A, 3 * N_HEADS_A)], [sm["a_norm"]],
                               [(HEAD_DIM_A, BF16)], tm=rtm, ncol=N_HEADS_A, name="a_gnorm")[0]
            h = mm(s["on"], wt["a_w_out"], add=h, name="a_out")
        else:
            s["pb"] = mm(s["hn"], wt["b_w_in"], name="b_in", n=N_CHIPS * SHARD_BIN, tn=SHARD_BIN, tk=D_MODEL,
                         b_spec=spec((None, D_MODEL, SHARD_BIN), lambda r, j, kk: (j, kk, 0)))
            s["kd"], s["vd"] = _dup_halves(s["pb"][:, qd_b:qd_b + kd_b]), _dup_halves(s["pb"][:, qd_b + kd_b:])
            s["ao"] = swa_fwd(s["pb"], s["kd"], s["vd"], sinks_p, name="b_att")
            h = mm(s["ao"], wt["b_w_out"], add=h, name="b_out")
        s["h1"] = h
        s["hf"] = norm(h, nw("norm_ffn", i), f"norm_ffn{i}")
        s["u"] = mm(s["hf"], wt["f_w_up"], name=f"f_up{i}", n=2 * D_FF, tn=SHARD_UP, tk=D_MODEL,
                    b_spec=spec((None, None, D_MODEL, SHARD_UP), lambda r, j, kk, i=i: (j, i, kk, 0)))
        s["uc"] = conv_fwd(s["u"], wt["f_conv"][i], name=f"f_conv{i}")
        nff = D_FF // 1408
        s["act"] = tile_map(_f_act, [(s["uc"], 1408, 0), (s["uc"], 1408, nff)], [], [(1408, BF16)], tm=rtm, ncol=nff, name=f"f_act{i}")[0]
        h = mm(s["act"], wt["f_w_down"], add=h, name=f"f_down{i}", n=D_MODEL, tn=D_MODEL, tk=D_FF // 2,
               b_spec=spec((None, D_FF // 2, D_MODEL), lambda r, j, kk, i=i: (i, kk, j)))
        s["h2"] = h
        s["hp"] = norm(h, nw("norm_ple", i), f"norm_ple{i}")
        s["gl"] = mm(s["hp"], wt["ple_w_gate"], name=f"ple_gate{i}", n=D_MODEL, tn=D_MODEL, tk=D_MODEL,
                     b_spec=spec((None, D_MODEL, D_MODEL), lambda r, j, kk, i=i: (i, kk, j)))
        s["pe"] = mm(p_bf[i], wt["ple_w_proj"], name=f"ple_proj{i}", n=D_MODEL, tn=SHARD_PROJ, tk=PLE_DIM,
                     b_spec=spec((None, None, PLE_DIM, SHARD_PROJ), lambda r, j, kk, i=i: (j, i, kk, 0)))
        h = tile_map(lambda hv, g, e: hv + _f_ple(g, e), [(h, D_MODEL, 0), (s["gl"], D_MODEL, 0), (s["pe"], D_MODEL, 0)], [],
                     [(D_MODEL, F32)], tm=rtm, ncol=1, name=f"ple_mix{i}")[0]
        saved.append(s)

    dh, gnf, loss = loss_head(h, tgt, sm["norm_final"][None, :], name="loss_head")
    gs["norm_final"] = gnf[0]

    g_mix, g_ffn, g_ple = [None] * DEPTH, [None] * DEPTH, [None] * DEPTH
    g_conv, g_down = [None] * DEPTH, [None] * DEPTH
    g_up = g_proj = g_gate = None
    for i in reversed(range(DEPTH)):
        s = saved[i]
        (dgl, dpe), _ = tile_vjp(_f_ple, [(s["gl"], D_MODEL, 0), (s["pe"], D_MODEL, 0)], [], [(dh, D_MODEL, 0)], n_diff=2,
                                 tm=rtm, ncol=1, name=f"ple_mix_bwd{i}")
        g_proj = mm(p_bf[i], dpe, ta=True, name=f"ple_proj_dw{i}", out_dtype=BF16, tn=SHARD_PROJ, into=g_proj,
                    o_shape=(N_CHIPS, DEPTH, PLE_DIM, SHARD_PROJ),
                    o_spec=spec((None, None, PLE_DIM, SHARD_PROJ), lambda r, j, kk, i=i: (j, i, r, 0)))
        g_gate = mm(s["hp"], dgl, ta=True, name=f"ple_gate_dw{i}", out_dtype=BF16, tm_cap=SHARD_PROJ, tn=D_MODEL, into=g_gate,
                    o_shape=(N_CHIPS, DEPTH, SHARD_PROJ, D_MODEL),
                    o_spec=spec((None, None, SHARD_PROJ, D_MODEL), lambda r, j, kk, i=i: (r, i, 0, j)))
        dhp = mm(dgl, wt["ple_w_gate"], tb=True, name=f"ple_gate_dx{i}", n=D_MODEL, tn=D_MODEL, tk=D_MODEL,
                 b_spec=spec((None, D_MODEL, D_MODEL), lambda r, j, kk, i=i: (i, j, kk)))
        dh, g_ple[i] = norm_bwd(s["h2"], nw("norm_ple", i), dhp, dh, f"norm_ple_bwd{i}")

        dact = mm(dh, wt["f_w_down"], tb=True, name=f"f_down_dx{i}", n=D_FF, tn=D_FF // 2, tk=D_MODEL,
                  b_spec=spec((None, D_FF // 2, D_MODEL), lambda r, j, kk, i=i: (i, j, kk)))
        g_down[i] = mm(s["act"], dh, ta=True, name=f"f_down_dw{i}", out_dtype=BF16)
        du, g_conv[i] = conv_bwd(act_bwd(s["uc"], dact, name=f"f_act_bwd{i}"), s["u"], wt["f_conv"][i], name=f"f_conv_bwd{i}")
        g_up = mm(s["hf"], du, ta=True, name=f"f_up_dw{i}", out_dtype=BF16, tn=SHARD_UP, into=g_up,
                  o_shape=(N_CHIPS, DEPTH, D_MODEL, SHARD_UP),
                  o_spec=spec((None, None, mtm, SHARD_UP), lambda r, j, kk, i=i: (j, i, r, 0)))
        dhf = mm(du, wt["f_w_up"], tb=True, name=f"f_up_dx{i}", n=D_MODEL, tn=D_MODEL, tk=SHARD_UP,
                 b_spec=spec((None, None, D_MODEL, SHARD_UP), lambda r, j, kk, i=i: (kk, i, j, 0)))
        dh, g_ffn[i] = norm_bwd(s["h1"], nw("norm_ffn", i), dhf, dh, f"norm_ffn_bwd{i}")

        if i % 2 == 0:
            don = mm(dh, wt["a_w_out"], tb=True, name="a_out_dx")
            gw["a_w_out"] = mm(s["on"], dh, ta=True, name="a_out_dw", out_dtype=BF16).reshape(N_CHIPS, SHARD_PROJ, D_MODEL)
            (do, dz), (gan,) = tile_vjp(_f_gnorm, [(s["o"], HEAD_DIM_A, 0), (s["pm"], HEAD_DIM_A, 3 * N_HEADS_A)], [sm["a_norm"]],
                                        [(don, HEAD_DIM_A, 0)], n_diff=2, tm=rtm, ncol=N_HEADS_A, name="a_gnorm_bwd")
            gs["a_norm"] = gan
            dprep = delta_scan_bwd(do, *s["prep"], s["s_in"], name="a_scan_bwd")
            dc, dbg = delta_prep_bwd(s["c"], s["bg"], dprep, name="a_prep_bwd")
            (dpt,), (galog, gdtb) = tile_vjp(_f_betag, [(s["pt"], LANES, 0)], [alog_p, dtb_p], [(dbg, LANES, 0)], n_diff=1,
                                             tm=rtm, ncol=1, name="a_betag_bwd")
            gs["a_log"] = galog[:, N_HEADS_A:2 * N_HEADS_A]
            gs["a_dt_bias"] = gdtb[:, N_HEADS_A:2 * N_HEADS_A]
            dqkv, gs["a_conv"] = conv_bwd(dc, s["pm"], wt["a_conv"], name="a_conv_bwd")
            dpm = jnp.concatenate([dqkv, dz], axis=1)
            dhn = mm(dpm, a_main, tb=True, name="a_in_main_dx")
            dhn = mm(dpt, a_tail, tb=True, add=dhn, name="a_in_tail_dx")
            g_main = mm(s["hn"], dpm, ta=True, name="a_in_main_dw", out_dtype=BF16)
            g_tail = mm(s["hn"], dpt, ta=True, name="a_in_tail_dw", out_dtype=BF16)
            g_in = jnp.concatenate([g_main, g_tail[:, :2 * N_HEADS_A]], axis=1)
            gw["a_w_in"] = g_in.reshape(D_MODEL, N_CHIPS, g_in.shape[1] // N_CHIPS).transpose(1, 0, 2)
        else:
            dao = mm(dh, wt["b_w_out"], tb=True, name="b_out_dx")
            gw["b_w_out"] = mm(s["ao"], dh, ta=True, name="b_out_dw", out_dtype=BF16).reshape(N_CHIPS, SHARD_PROJ, D_MODEL)
            dq, dkd, dvd, gsk = swa_bwd(dao, s["pb"], s["kd"], s["vd"], sinks_p, name="b_att_bwd")
            gs["b_sinks"] = gsk[:, :N_HEADS_B]
            dpb = jnp.concatenate([dq, _undup(dkd), _undup(dvd)], axis=1)
            dhn = mm(dpb, wt["b_w_in"], tb=True, name="b_in_dx", n=D_MODEL, tn=D_MODEL, tk=SHARD_BIN,
                     b_spec=spec((None, D_MODEL, SHARD_BIN), lambda r, j, kk: (kk, j, 0)))
            gw["b_w_in"] = mm(s["hn"], dpb, ta=True, name="b_in_dw", out_dtype=BF16, tn=SHARD_BIN,
                              o_shape=(N_CHIPS, D_MODEL, SHARD_BIN), o_spec=spec((None, mtm, SHARD_BIN), lambda r, j, kk: (j, r, 0)))
        dh, g_mix[i] = norm_bwd(s["h0"], nw("norm_mix", i), dhn, dh, f"norm_mix_bwd{i}")

    gs["norm_mix"], gs["norm_ffn"], gs["norm_ple"] = (jnp.concatenate(g, axis=0) for g in (g_mix, g_ffn, g_ple))
    gs["f_conv"] = jnp.stack(g_conv)
    gw["f_w_up"] = g_up.reshape(N_CHIPS, DEPTH * D_MODEL, SHARD_UP)
    gw["f_w_down"] = jnp.stack([g.reshape(N_CHIPS, D_FF // N_CHIPS, D_MODEL) for g in g_down], axis=1).reshape(N_CHIPS, -1, D_MODEL)
    gw["ple_w_proj"] = g_proj.reshape(N_CHIPS, DEPTH * PLE_DIM, SHARD_PROJ)
    gw["ple_w_gate"] = g_gate.reshape(N_CHIPS, DEPTH * SHARD_PROJ, D_MODEL)
    return loss, dh, gw, gs


BIG = [("a_w_in", False), ("a_w_out", False), ("b_w_in", False), ("b_w_out", False), ("f_w_up", False), ("f_w_down", True),
       ("ple_w_proj", False), ("ple_w_gate", True)]
CONVS = ["a_conv", "f_conv"]
SMALL = ["norm_mix", "norm_ffn", "norm_ple", "norm_final", "a_log", "a_dt_bias", "a_norm", "b_sinks"]
SMALL_ROWS = 8
CONV_ROWS = 16
CONV_GRAD_ROWS = 48


def _pack_rows(arrs, rows, dtype):
    flat = jnp.concatenate([a.reshape(-1).astype(dtype) for a in arrs])
    return jnp.pad(flat, (0, rows * PACK_COLS - flat.shape[0])).reshape(rows, PACK_COLS)


def _unpack(flat, shapes):
    out, off = [], 0
    for shp in shapes:
        n = math.prod(shp)
        out.append(flat[off:off + n].reshape(shp))
        off += n
    return out


def _pack_small(d, loss=None):
    tail = jnp.concatenate([d["a_log"].reshape(-1), d["a_dt_bias"].reshape(-1), d["a_norm"].reshape(-1), d["b_sinks"].reshape(-1)])
    if loss is not None:
        tail = jnp.concatenate([tail, loss.reshape(-1)[:1]])
    tail = jnp.pad(tail, (0, PACK_COLS - tail.shape[0]))
    return jnp.concatenate([d["norm_mix"], d["norm_ffn"], d["norm_ple"], d["norm_final"][None, :], tail[None, :]], axis=0)


def _unpack_small(a, like):
    out = {"norm_mix": a[0:2], "norm_ffn": a[2:4], "norm_ple": a[4:6], "norm_final": a[6]}
    off = 0
    for nm in ("a_log", "a_dt_bias", "a_norm", "b_sinks"):
        n = like[nm].size
        out[nm] = a[7, off:off + n].reshape(like[nm].shape)
        off += n
    return out, a[7, off]


def _as2d(a):
    return a.reshape(-1, a.shape[-1])


def kernel(x, p, norm_mix, norm_ffn, norm_ple, norm_final, a_w_in, a_conv, a_log, a_dt_bias, a_norm, a_w_out, b_w_in, b_sinks, b_w_out, f_w_up, f_conv, f_w_down, ple_w_proj, ple_w_gate, loss_target, m_norm_mix, m_norm_ffn, m_norm_ple, m_norm_final, m_a_w_in, m_a_conv, m_a_log, m_a_dt_bias, m_a_norm, m_a_w_out, m_b_w_in, m_b_sinks, m_b_w_out, m_f_w_up, m_f_conv, m_f_w_down, m_ple_w_proj, m_ple_w_gate, v_norm_mix, v_norm_ffn, v_norm_ple, v_norm_final, v_a_w_in, v_a_conv, v_a_log, v_a_dt_bias, v_a_norm, v_a_w_out, v_b_w_in, v_b_sinks, v_b_w_out, v_f_w_up, v_f_conv, v_f_w_down, v_ple_w_proj, v_ple_w_gate):
    w = dict(norm_mix=norm_mix, norm_ffn=norm_ffn, norm_ple=norm_ple, norm_final=norm_final, a_w_in=a_w_in, a_conv=a_conv,
             a_log=a_log, a_dt_bias=a_dt_bias, a_norm=a_norm, a_w_out=a_w_out, b_w_in=b_w_in, b_sinks=b_sinks, b_w_out=b_w_out,
             f_w_up=f_w_up, f_conv=f_conv, f_w_down=f_w_down, ple_w_proj=ple_w_proj, ple_w_gate=ple_w_gate)
    m = dict(norm_mix=m_norm_mix, norm_ffn=m_norm_ffn, norm_ple=m_norm_ple, norm_final=m_norm_final, a_w_in=m_a_w_in,
             a_conv=m_a_conv, a_log=m_a_log, a_dt_bias=m_a_dt_bias, a_norm=m_a_norm, a_w_out=m_a_w_out, b_w_in=m_b_w_in,
             b_sinks=m_b_sinks, b_w_out=m_b_w_out, f_w_up=m_f_w_up, f_conv=m_f_conv, f_w_down=m_f_w_down,
             ple_w_proj=m_ple_w_proj, ple_w_gate=m_ple_w_gate)
    v = dict(norm_mix=v_norm_mix, norm_ffn=v_norm_ffn, norm_ple=v_norm_ple, norm_final=v_norm_final, a_w_in=v_a_w_in,
             a_conv=v_a_conv, a_log=v_a_log, a_dt_bias=v_a_dt_bias, a_norm=v_a_norm, a_w_out=v_a_w_out, b_w_in=v_b_w_in,
             b_sinks=v_b_sinks, b_w_out=v_b_w_out, f_w_up=v_f_w_up, f_conv=v_f_conv, f_w_down=v_f_w_down,
             ple_w_proj=v_ple_w_proj, ple_w_gate=v_ple_w_gate)
    xc, yc, cc = _place()
    my_chip = 2 * xc + yc

    own = [(_as2d(w[n]).astype(BF16), lm) for n, lm in BIG]
    gathered = {}
    for (n, lm), (arr, _), g in zip(BIG, own, gather_units(own, name="gather_weights")):
        mine = arr.reshape(2, arr.shape[0] // 2, arr.shape[1]) if lm else arr
        gathered[n] = lax.dynamic_update_index_in_dim(g, mine, my_chip, 1 if lm else 0)
    a_in = jnp.concatenate([gathered["a_w_in"][j] for j in range(N_CHIPS)], axis=1)
    n_main = 4 * N_HEADS_A * HEAD_DIM_A
    conv_shapes = [w[n].shape for n in CONVS]
    convs = allgather8(_pack_rows([w[n] for n in CONVS], CONV_ROWS, F32), name="gather_convs")
    conv_parts = [_unpack(convs[2 * j].reshape(-1), conv_shapes) for j in range(N_CHIPS)]
    a_conv_full, f_conv_full = (jnp.concatenate([conv_parts[j][q] for j in range(N_CHIPS)], axis=2) for q in range(2))
    wt = {"a_main": a_in[:, :n_main], "a_tail": jnp.pad(a_in[:, n_main:], ((0, 0), (0, LANES - 2 * N_HEADS_A))),
          "a_w_out": gathered["a_w_out"].reshape(D_MODEL, D_MODEL), "b_w_out": gathered["b_w_out"].reshape(D_MODEL, D_MODEL),
          "b_w_in": gathered["b_w_in"], "f_w_up": gathered["f_w_up"].reshape(N_CHIPS, DEPTH, D_MODEL, SHARD_UP),
          "f_w_down": gathered["f_w_down"].reshape(DEPTH, D_FF, D_MODEL),
          "ple_w_proj": gathered["ple_w_proj"].reshape(N_CHIPS, DEPTH, PLE_DIM, SHARD_PROJ),
          "ple_w_gate": gathered["ple_w_gate"].reshape(DEPTH, D_MODEL, D_MODEL),
          "a_conv": a_conv_full[0], "f_conv": f_conv_full}
    sm = {n: w[n] for n in SMALL}

    loss, grad_x, gw, gs = local_step(x[0], p[:, 0], loss_target[0], sm, wt)

    units = [gw[n] for n, _ in BIG]
    from_sib = swap_units(units, name="rs_swap")
    pairs = []
    for (n, _), g, sib in zip(BIG, units, from_sib):
        rh, cols = g.shape[1] // 2, g.shape[2]
        mine = lax.dynamic_slice_in_dim(g, cc * rh, rh, axis=1)
        pairs.append(add_parts([mine.reshape(N_CHIPS * rh, cols), sib.reshape(N_CHIPS * rh, cols)], out_dtype=BF16,
                               name=f"rs_add_pair_{n}").reshape(N_CHIPS, rh, cols))
    others = scatter_units(pairs, name="rs_scatter")
    halves = [add_parts([lax.dynamic_index_in_dim(pr, my_chip, axis=0, keepdims=False), o[0], o[1], o[2]], out_dtype=F32,
                        name=f"rs_add_chips_{n}") for (n, _), pr, o in zip(BIG, pairs, others)]
    from_sib = join_units(halves, name="rs_join")
    g_sh = {n: jnp.where(cc == 0, jnp.concatenate([hf, ot]), jnp.concatenate([ot, hf])).reshape(w[n].shape)
            for (n, _), hf, ot in zip(BIG, halves, from_sib)}

    conv_grads = _pack_rows([gs[n] for n in CONVS], CONV_GRAD_ROWS, F32)
    small_sum = sum8(allgather8(jnp.concatenate([_pack_small(gs, loss), conv_grads]), name="gather_small"), name="sum_small")
    g_sm, loss_sum = _unpack_small(small_sum[:SMALL_ROWS], sm)
    for n, full in zip(CONVS, _unpack(small_sum[SMALL_ROWS:].reshape(-1), [gs[n].shape for n in CONVS])):
        g_sh[n] = lax.dynamic_slice_in_dim(full, my_chip * w[n].shape[-1], w[n].shape[-1], axis=full.ndim - 1).reshape(w[n].shape)

    grads, delta, new_m, new_v = {}, {}, {}, {}
    for n in [n for n, _ in BIG] + CONVS:
        d2, m2, v2 = adamw(_as2d(w[n]), _as2d(g_sh[n]), _as2d(m[n]), _as2d(v[n]), name=f"adamw_{n}")
        grads[n], delta[n], new_m[n], new_v[n] = g_sh[n], d2.reshape(w[n].shape), m2.reshape(w[n].shape), v2.reshape(w[n].shape)
    pk = lambda d: _pack_small(d)
    d2, m2, v2 = adamw(pk(sm), pk(g_sm), pk({n: m[n] for n in SMALL}), pk({n: v[n] for n in SMALL}), name="adamw_small")
    for src, dst in ((d2, delta), (m2, new_m), (v2, new_v)):
        dst.update(_unpack_small(src, sm)[0])
    grads.update(g_sm)

    order = ["norm_mix", "norm_ffn", "norm_ple", "norm_final", "a_w_in", "a_conv", "a_log", "a_dt_bias", "a_norm", "a_w_out",
             "b_w_in", "b_sinks", "b_w_out", "f_w_up", "f_conv", "f_w_down", "ple_w_proj", "ple_w_gate"]
    return (loss_sum, grad_x[None], *[grads[n] for n in order], *[delta[n] for n in order],
            *[new_m[n] for n in order], *[new_v[n] for n in order])
```

```python
import functools
import math

import jax
import jax.numpy as jnp
from jax import lax
from jax.experimental import pallas as pl
from jax.experimental.pallas import tpu as pltpu

F32 = jnp.float32
BF16 = jnp.bfloat16
MESH = pl.DeviceIdType.MESH

D_MODEL = 1024
N_HEADS_A = 8
HEAD_DIM_A = 128
CONV_A = 4
N_HEADS_B = 16
N_KV_B = 4
HEAD_DIM_B = 64
WINDOW = 128
D_FF = 2816
FFN_CONV = 3
PLE_DIM = 256
EPS = 1e-6
DEPTH = 2

ADAM_LR = 0.001
ADAM_B1 = 0.9
ADAM_B2 = 0.999
ADAM_EPS = 1e-08
ADAM_WD = 0.01
ADAM_STEP = 10

LANES = 128
SUBLANES = 8
BF16_ROWS = 16
CHUNK = 128
VMEM_LIMIT = 56 * 1024 * 1024
NEG = -1e30
N_CHIPS = 4
N_DEV = 8
PACK_COLS = 1024


def _params(sem=None):
    return pltpu.CompilerParams(dimension_semantics=sem, vmem_limit_bytes=VMEM_LIMIT)


def _tile(dim, cap):
    if dim % LANES:
        return dim
    best = LANES
    for t in range(LANES, min(dim, cap) + 1, LANES):
        if dim % t == 0:
            best = t
    return best


def _dot(a, b, dims=(((1,), (0,)), ((), ())), precision=None):
    return lax.dot_general(a, b, dims, precision=precision, preferred_element_type=F32)


NN = (((1,), (0,)), ((), ()))
NT = (((1,), (1,)), ((), ()))
TN = (((0,), (0,)), ((), ()))


MM_TM_CAP = 1024


def mm(a, b, *, name, ta=False, tb=False, out_dtype=F32, add=None, norm_w=None, tm_cap=MM_TM_CAP, tn_cap=1408, tk_cap=1408,
       n=None, tn=None, tk=None, b_spec=None, o_spec=None, o_shape=None, into=None):
    m, k = (a.shape[1], a.shape[0]) if ta else a.shape
    if b_spec is None:
        n = b.shape[0] if tb else b.shape[1]
        assert (b.shape[1] if tb else b.shape[0]) == k, (a.shape, b.shape, ta, tb)
    tm, tn, tk = _tile(m, tm_cap), tn or _tile(n, tn_cap), tk or _tile(k, tk_cap)
    assert n % tn == 0 and k % tk == 0, (n, tn, k, tk)
    nk = k // tk
    dims = (((0 if ta else 1,), (1 if tb else 0,)), ((), ()))
    has_add, has_norm = add is not None, norm_w is not None
    assert not has_norm or (tn == n and o_spec is None), "the norm epilogue needs whole rows"
    n_in = 2 + has_add + has_norm + (into is not None)

    def body(*refs):
        a_ref, b_ref = refs[0], refs[1]
        add_ref = refs[2] if has_add else None
        o_ref = refs[n_in]
        part = _dot(a_ref[...].astype(BF16), b_ref[...].astype(BF16), dims)

        def finish(r):
            if has_add:
                r = r + add_ref[...].astype(F32)
            o_ref[...] = r.astype(o_ref.dtype)
            if has_norm:
                refs[n_in + 1][...] = _f_norm(r, refs[2 + has_add][...]).astype(BF16)

        if nk == 1:
            finish(part)
            return
        acc = refs[-1]
        kk = pl.program_id(2)

        @pl.when(kk == 0)
        def _():
            acc[...] = part

        @pl.when(kk > 0)
        def _():
            acc[...] += part

        @pl.when(kk == nk - 1)
        def _():
            finish(acc[...])

    a_spec = pl.BlockSpec((tk, tm), lambda i, j, kk: (kk, i)) if ta else pl.BlockSpec((tm, tk), lambda i, j, kk: (i, kk))
    if b_spec is None:
        b_spec = pl.BlockSpec((tn, tk), lambda i, j, kk: (j, kk)) if tb else pl.BlockSpec((tk, tn), lambda i, j, kk: (kk, j))
    plain_o = pl.BlockSpec((tm, tn), lambda i, j, kk: (i, j))
    if o_spec is None:
        o_spec, o_shape = plain_o, (m, n)
    in_specs = [a_spec, b_spec] + ([plain_o] if has_add else [])
    args = (a, b) + ((add,) if has_add else ())
    out_specs, out_shapes = o_spec, jax.ShapeDtypeStruct(tuple(o_shape), out_dtype)
    if has_norm:
        in_specs.append(pl.BlockSpec((1, n), lambda i, j, kk: (0, 0)))
        args += (norm_w,)
        out_specs, out_shapes = [o_spec, plain_o], [out_shapes, jax.ShapeDtypeStruct((m, n), BF16)]
    aliases = {}
    if into is not None:
        assert into.shape == tuple(o_shape) and into.dtype == out_dtype, (into.shape, o_shape)
        in_specs.append(pl.BlockSpec(memory_space=pl.ANY))
        args += (into,)
        aliases = {n_in - 1: 0}
    return pl.pallas_call(
        body, grid=(m // tm, n // tn, nk), in_specs=in_specs, out_specs=out_specs,
        out_shape=out_shapes, name=name, input_output_aliases=aliases,
        scratch_shapes=[pltpu.VMEM((tm, tn), F32)] if nk > 1 else [],
        compiler_params=_params(("parallel", "parallel", "arbitrary")),
    )(*args)


def _row_spec(tm, cw, coff):
    return pl.BlockSpec((tm, cw), lambda i, j: (i, j + coff))


def _full_spec(shape):
    return pl.BlockSpec(shape, lambda i, j: (0,) * len(shape))


def tile_map(fn, rows, params, outs, *, tm, ncol, name):
    t = rows[0][0].shape[0]
    nin = len(rows) + len(params)

    def body(*refs):
        res = fn(*[r[...] for r in refs[:nin]])
        res = res if isinstance(res, (tuple, list)) else (res,)
        for o_ref, r in zip(refs[nin:], res):
            o_ref[...] = r.astype(o_ref.dtype)

    in_specs = [_row_spec(tm, cw, coff) for (_, cw, coff) in rows] + [_full_spec(p.shape) for p in params]
    res = pl.pallas_call(
        body, grid=(t // tm, ncol), in_specs=in_specs,
        out_specs=[_row_spec(tm, cw, 0) for (cw, _) in outs],
        out_shape=[jax.ShapeDtypeStruct((t, cw * ncol), dt) for (cw, dt) in outs], name=name,
        compiler_params=_params(("parallel", "parallel")),
    )(*[r[0] for r in rows], *params)
    return res


def tile_vjp(fn, rows, params, cts, *, n_diff, tm, ncol, name, add=None):
    t = rows[0][0].shape[0]
    nr, npar, nct = len(rows), len(params), len(cts)
    has_add = add is not None

    def body(*refs):
        vals = [r[...] for r in refs[:nr + npar + nct + (1 if has_add else 0)]]
        diff, rest, pars = vals[:n_diff], vals[n_diff:nr], vals[nr:nr + npar]
        ctv = vals[nr + npar:nr + npar + nct]
        outs_ref = refs[nr + npar + nct + (1 if has_add else 0):]

        def f(*a):
            res = fn(*a[:n_diff], *rest, *a[n_diff:])
            return tuple(res) if isinstance(res, (tuple, list)) else (res,)

        primal, vjp = jax.vjp(f, *[d.astype(F32) for d in diff], *pars)
        grads = vjp(tuple(c.astype(o.dtype) for c, o in zip(ctv, primal)))
        for q in range(n_diff):
            g = grads[q]
            if has_add and q == 0:
                g = g + vals[-1]
            outs_ref[q][...] = g.astype(outs_ref[q].dtype)
        first = (pl.program_id(0) == 0) & (pl.program_id(1) == 0)
        for q in range(npar):
            o_ref, g = outs_ref[n_diff + q], grads[n_diff + q]

            @pl.when(first)
            def _(o_ref=o_ref, g=g):
                o_ref[...] = g

            @pl.when(jnp.logical_not(first))
            def _(o_ref=o_ref, g=g):
                o_ref[...] += g

    ins = list(rows) + [None] * 0
    in_specs = [_row_spec(tm, cw, coff) for (_, cw, coff) in rows] + [_full_spec(p.shape) for p in params]
    in_specs += [_row_spec(tm, cw, coff) for (_, cw, coff) in cts]
    args = [r[0] for r in rows] + list(params) + [c[0] for c in cts]
    if has_add:
        in_specs.append(_row_spec(tm, add[1], add[2]))
        args.append(add[0])
    out_specs = [_row_spec(tm, rows[q][1], 0) for q in range(n_diff)] + [_full_spec(p.shape) for p in params]
    out_shape = [jax.ShapeDtypeStruct((t, rows[q][1] * ncol), F32) for q in range(n_diff)]
    out_shape += [jax.ShapeDtypeStruct(p.shape, F32) for p in params]
    del ins
    res = pl.pallas_call(
        body, grid=(t // tm, ncol), in_specs=in_specs, out_specs=out_specs, out_shape=out_shape, name=name,
        compiler_params=_params(("arbitrary", "arbitrary")),
    )(*args)
    return res[:n_diff], res[n_diff:]


def _silu(x):
    return x * jax.nn.sigmoid(x)


def _f_norm(h, w):
    return h * lax.rsqrt(jnp.mean(h * h, axis=-1, keepdims=True) + EPS) * w


def _f_gnorm(o, z, w):
    return _f_norm(o, w) * _silu(z)


def _f_act(gate, val):
    return _silu(gate) * val


def _f_ple(gl, pe):
    return jax.nn.sigmoid(gl) * pe


def _f_betag(pt, alog, dtb):
    lane = lax.broadcasted_iota(jnp.int32, (1, LANES), 1)
    z = pt + dtb
    softplus = jnp.maximum(z, 0.0) + jnp.log(1.0 + jnp.exp(-jnp.abs(z)))
    g = -jnp.exp(alog) * softplus
    return jnp.where(lane < N_HEADS_A, jax.nn.sigmoid(pt), jnp.where(lane < 2 * N_HEADS_A, g, 0.0))


CONV_TM = 256
CONV_CW = 512


def _shift_down(x, prev, s, row):
    rp = jnp.tile(pltpu.roll(prev, s, 0), (x.shape[0] // SUBLANES, 1))
    return jnp.where(row < s, rp, pltpu.roll(x, s, 0))


def _shift_up(x, nxt, s, row):
    tm = x.shape[0]
    rn = jnp.tile(pltpu.roll(nxt, SUBLANES - s, 0), (tm // SUBLANES, 1))
    return jnp.where(row >= tm - s, rn, pltpu.roll(x, tm - s, 0))


def conv_fwd(x, w, *, name):
    t = x.shape[0]
    k, c = w.shape
    tm, cw = min(CONV_TM, t), CONV_CW
    nb8 = tm // SUBLANES

    def body(x_ref, p_ref, w_ref, o_ref):
        i = pl.program_id(1)
        xv = x_ref[...]
        prev = jnp.where(i > 0, p_ref[...], 0.0)
        row = lax.broadcasted_iota(jnp.int32, xv.shape, 0)
        y = xv * w_ref[pl.ds(k - 1, 1), :]
        for s in range(1, k):
            y = y + _shift_down(xv, prev, s, row) * w_ref[pl.ds(k - 1 - s, 1), :]
        o_ref[...] = y

    return pl.pallas_call(
        body, grid=(c // cw, t // tm),
        in_specs=[pl.BlockSpec((tm, cw), lambda j, i: (i, j)),
                  pl.BlockSpec((SUBLANES, cw), lambda j, i: (jnp.maximum(i * nb8 - 1, 0), j)),
                  pl.BlockSpec((k, cw), lambda j, i: (0, j))],
        out_specs=pl.BlockSpec((tm, cw), lambda j, i: (i, j)),
        out_shape=jax.ShapeDtypeStruct((t, c), F32), name=name,
        compiler_params=_params(("parallel", "parallel")),
    )(x, x, w)


def conv_bwd(dy, x, w, *, name):
    t = x.shape[0]
    k, c = w.shape
    tm, cw = min(CONV_TM, t), CONV_CW
    nb8 = tm // SUBLANES
    ni = t // tm

    def body(dy_ref, dn_ref, x_ref, p_ref, w_ref, dx_ref, dw_ref):
        i = pl.program_id(1)
        dyv, xv = dy_ref[...], x_ref[...]
        nxt = jnp.where(i < ni - 1, dn_ref[...], 0.0)
        prev = jnp.where(i > 0, p_ref[...], 0.0)
        row = lax.broadcasted_iota(jnp.int32, xv.shape, 0)
        dx = dyv * w_ref[pl.ds(k - 1, 1), :]
        dws = [jnp.sum(dyv * xv, axis=0, keepdims=True)]
        for s in range(1, k):
            dx = dx + _shift_up(dyv, nxt, s, row) * w_ref[pl.ds(k - 1 - s, 1), :]
            dws.append(jnp.sum(dyv * _shift_down(xv, prev, s, row), axis=0, keepdims=True))
        dx_ref[...] = dx
        for s in range(k):
            @pl.when(i == 0)
            def _(s=s):
                dw_ref[pl.ds(k - 1 - s, 1), :] = dws[s]

            @pl.when(i > 0)
            def _(s=s):
                dw_ref[pl.ds(k - 1 - s, 1), :] += dws[s]

    return pl.pallas_call(
        body, grid=(c // cw, ni),
        in_specs=[pl.BlockSpec((tm, cw), lambda j, i: (i, j)),
                  pl.BlockSpec((SUBLANES, cw), lambda j, i: (jnp.minimum((i + 1) * nb8, t // SUBLANES - 1), j)),
                  pl.BlockSpec((tm, cw), lambda j, i: (i, j)),
                  pl.BlockSpec((SUBLANES, cw), lambda j, i: (jnp.maximum(i * nb8 - 1, 0), j)),
                  pl.BlockSpec((k, cw), lambda j, i: (0, j))],
        out_specs=[pl.BlockSpec((tm, cw), lambda j, i: (i, j)), pl.BlockSpec((k, cw), lambda j, i: (0, j))],
        out_shape=[jax.ShapeDtypeStruct((t, c), F32), jax.ShapeDtypeStruct((k, c), F32)], name=name,
        compiler_params=_params(("parallel", "arbitrary")),
    )(dy, dy, x, x, w)


FFN_TM = 128
FFN_CW = D_FF // 2


def _conv_taps(x, prev, w_ref, cols, row):
    k = w_ref.shape[0]
    y = x * w_ref[pl.ds(k - 1, 1), cols]
    for s in range(1, k):
        y = y + _shift_down(x, prev, s, row) * w_ref[pl.ds(k - 1 - s, 1), cols]
    return y


def _ffn_specs(t, tm, cw, k):
    nb8, ncol = tm // SUBLANES, D_FF // cw
    cur = lambda off: pl.BlockSpec((tm, cw), lambda j, i: (i, j + off))
    prev = lambda off: pl.BlockSpec((SUBLANES, cw), lambda j, i: (jnp.maximum(i * nb8 - 1, 0), j + off))
    nxt = lambda off: pl.BlockSpec((SUBLANES, cw), lambda j, i: (jnp.minimum((i + 1) * nb8, t // SUBLANES - 1), j + off))
    taps = lambda off: pl.BlockSpec((k, cw), lambda j, i: (0, j + off))
    return cur, prev, nxt, taps, ncol


def conv_act_fwd(u, w, *, name):
    t, k = u.shape[0], w.shape[0]
    tm, cw = min(FFN_TM, t), FFN_CW
    cur, prev, _, taps, ncol = _ffn_specs(t, tm, cw, k)

    def body(ug_ref, pg_ref, uv_ref, pv_ref, wg_ref, wv_ref, o_ref):
        first = pl.program_id(1) == 0
        row = lax.broadcasted_iota(jnp.int32, (tm, LANES), 0)
        for cb in range(cw // LANES):
            cols = slice(cb * LANES, (cb + 1) * LANES)
            cg = _conv_taps(ug_ref[:, cols], jnp.where(first, 0.0, pg_ref[:, cols]), wg_ref, cols, row)
            cv = _conv_taps(uv_ref[:, cols], jnp.where(first, 0.0, pv_ref[:, cols]), wv_ref, cols, row)
            o_ref[:, cols] = _f_act(cg, cv).astype(o_ref.dtype)

    return pl.pallas_call(
        body, grid=(ncol, t // tm),
        in_specs=[cur(0), prev(0), cur(ncol), prev(ncol), taps(0), taps(ncol)],
        out_specs=cur(0), out_shape=jax.ShapeDtypeStruct((t, D_FF), BF16), name=name,
        compiler_params=_params(("parallel", "parallel")),
    )(u, u, u, u, w, w)


def conv_act_bwd(u, dact, w, *, name):
    t, k = u.shape[0], w.shape[0]
    tm, cw = min(FFN_TM, t), FFN_CW
    cur, prev, nxt, taps, ncol = _ffn_specs(t, tm, cw, k)
    ni = t // tm

    def body(ug_ref, pg_ref, ng_ref, uv_ref, pv_ref, nv_ref, d_ref, dn_ref, wg_ref, wv_ref, dg_ref, dv_ref, dwg_ref, dwv_ref):
        i = pl.program_id(1)
        first, last = i == 0, i == ni - 1
        row = lax.broadcasted_iota(jnp.int32, (tm, LANES), 0)
        row8 = lax.broadcasted_iota(jnp.int32, (SUBLANES, LANES), 0)
        for cb in range(cw // LANES):
            cols = slice(cb * LANES, (cb + 1) * LANES)
            ug, uv = ug_ref[:, cols], uv_ref[:, cols]
            pg, pv = jnp.where(first, 0.0, pg_ref[:, cols]), jnp.where(first, 0.0, pv_ref[:, cols])
            sg = [ug] + [_shift_down(ug, pg, s, row) for s in range(1, k)]
            sv = [uv] + [_shift_down(uv, pv, s, row) for s in range(1, k)]
            taps = lambda xs, w_ref: sum(xs[s] * w_ref[pl.ds(k - 1 - s, 1), cols] for s in range(k))
            _, vjp = jax.vjp(_f_act, taps(sg, wg_ref), taps(sv, wv_ref))
            dcg, dcv = vjp(d_ref[:, cols])
            _, vjp_n = jax.vjp(_f_act, _conv_taps(ng_ref[:, cols], ug[tm - SUBLANES:], wg_ref, cols, row8),
                               _conv_taps(nv_ref[:, cols], uv[tm - SUBLANES:], wv_ref, cols, row8))
            dcgn, dcvn = vjp_n(jnp.where(last, 0.0, dn_ref[:, cols]))
            for dc, dcn, xs, w_ref, dx_ref, dw_ref in ((dcg, dcgn, sg, wg_ref, dg_ref, dwg_ref),
                                                       (dcv, dcvn, sv, wv_ref, dv_ref, dwv_ref)):
                dx = dc * w_ref[pl.ds(k - 1, 1), cols]
                dws = [jnp.sum(dc * xs[0], axis=0, keepdims=True)]
                for s in range(1, k):
                    dx = dx + _shift_up(dc, dcn, s, row) * w_ref[pl.ds(k - 1 - s, 1), cols]
                    dws.append(jnp.sum(dc * xs[s], axis=0, keepdims=True))
                dx_ref[:, cols] = dx
                for s in range(k):
                    @pl.when(first)
                    def _(s=s, dw_ref=dw_ref, dws=dws):
                        dw_ref[pl.ds(k - 1 - s, 1), cols] = dws[s]

                    @pl.when(jnp.logical_not(first))
                    def _(s=s, dw_ref=dw_ref, dws=dws):
                        dw_ref[pl.ds(k - 1 - s, 1), cols] += dws[s]

    half = jax.ShapeDtypeStruct((t, D_FF), F32)
    dwh = jax.ShapeDtypeStruct((k, D_FF), F32)
    return pl.pallas_call(
        body, grid=(ncol, ni),
        in_specs=[cur(0), prev(0), nxt(0), cur(ncol), prev(ncol), nxt(ncol), cur(0), nxt(0), taps(0), taps(ncol)],
        out_specs=[cur(0), cur(0), taps(0), taps(0)], out_shape=[half, half, dwh, dwh], name=name,
        compiler_params=_params(("parallel", "arbitrary")),
    )(u, u, u, u, u, u, dact, dact, w, w)


def _each(f, *lists):
    return [f(*a) for a in zip(*lists)]


@jax.custom_vjp
def _inv_unit_lower(lms):
    return _inv_blocks(lms)


def _inv_blocks(lms):
    c = lms[0].shape[0]
    ri = lax.broadcasted_iota(jnp.int32, (c, c), 0)
    ci = lax.broadcasted_iota(jnp.int32, (c, c), 1)
    eye = (ri == ci).astype(F32)
    dms = _each(lambda lm: eye - jnp.where((ri >> 1) == (ci >> 1), lm, 0.0), lms)
    for lv in range(1, int(math.log2(c))):
        below = ((ri >> (lv + 1)) == (ci >> (lv + 1))) & ((ri >> lv) != (ci >> lv))
        dbs = _each(lambda dm: dm.astype(BF16), dms)
        ods = _each(lambda lm, db: _dot(jnp.where(below, lm, 0.0).astype(BF16), db).astype(BF16), lms, dbs)
        dms = _each(lambda dm, db, od: dm - _dot(db, od), dms, dbs, ods)
    return dms


def _inv_fwd(lms):
    tms = _inv_blocks(lms)
    return tms, tms


def _inv_bwd(tms, dts):
    tbs = _each(lambda tm: tm.astype(BF16), tms)
    mid = _each(lambda tb, dt: _dot(tb, dt.astype(BF16), TN).astype(BF16), tbs, dts)
    return (_each(lambda m, tb: -_dot(m, tb, NT), mid, tbs),)


_inv_unit_lower.defvjp(_inv_fwd, _inv_bwd)


def _l2n(x):
    return x * lax.rsqrt(jnp.sum(x * x, axis=-1, keepdims=True) + EPS)


def _prep_fn(cqs, cks, cvs, bg, sel_b, sel_g):
    c = cqs[0].shape[0]
    ri = lax.broadcasted_iota(jnp.int32, (c, c), 0)
    ci = lax.broadcasted_iota(jnp.int32, (c, c), 1)
    eye = (ri == ci).astype(F32)
    incl, strict = ci <= ri, ci < ri
    last = lax.broadcasted_iota(jnp.int32, (c, 1), 0) == c - 1
    to_row = lambda col: jnp.sum(col * eye, axis=0, keepdims=True)
    qs = _each(lambda a: _l2n(_silu(a)) * (HEAD_DIM_A ** -0.5), cqs)
    ks = _each(lambda a: _l2n(_silu(a)), cks)
    vbs = _each(lambda a: _silu(a).astype(BF16), cvs)
    betas = _each(lambda m: jnp.sum(bg * m, axis=1, keepdims=True), sel_b)
    gs = _each(lambda m: jnp.sum(bg * m, axis=1, keepdims=True), sel_g)
    gcss = _each(lambda g: jnp.sum(jnp.where(incl, to_row(g), 0.0), axis=1, keepdims=True), gs)
    gtots = _each(lambda gcs: jnp.sum(jnp.where(last, gcs, 0.0), axis=0, keepdims=True), gcss)
    decays = _each(lambda gcs: jnp.exp(jnp.where(incl, gcs - to_row(gcs), NEG)), gcss)
    kbs = _each(lambda k: k.astype(BF16), ks)
    lms = _each(lambda beta, kb, dec: jnp.where(strict, beta * _dot(kb, kb, NT) * dec, 0.0), betas, kbs, decays)
    ams = _each(lambda tm, beta: (tm * to_row(beta)).astype(BF16), _inv_unit_lower(lms), betas)
    gams = _each(jnp.exp, gcss)
    u0s = _each(_dot, ams, vbs)
    wks = _each(lambda am, gam, k: _dot(am, (gam * k).astype(BF16)), ams, gams, ks)
    qks = _each(lambda q, kb, dec: _dot(q.astype(BF16), kb, NT) * dec, qs, kbs, decays)
    qds = _each(lambda q, gam: q * gam, qs, gams)
    kds = _each(lambda k, gtot, gcs: k * jnp.exp(gtot - gcs), ks, gtots, gcss)
    gls = _each(lambda gtot: jnp.exp(gtot) * jnp.ones((SUBLANES, LANES), F32), gtots)
    return u0s, wks, qds, kds, qks, gls


def _head_masks(h):
    lane = lax.broadcasted_iota(jnp.int32, (1, LANES), 1)
    return (lane == h).astype(F32), (lane == h + N_HEADS_A).astype(F32)


def _hsl(j):
    return slice(j * HEAD_DIM_A, (j + 1) * HEAD_DIM_A)


def delta_prep(cqkv, bg, *, name):
    t = cqkv.shape[0]
    nh, hd, n = N_HEADS_A, HEAD_DIM_A, t // CHUNK

    def body(cq_ref, ck_ref, cv_ref, bg_ref, u0_ref, wk_ref, qd_ref, kd_ref, qk_ref, gl_ref):
        heads = range(nh)
        masks = [_head_masks(j) for j in heads]
        res = _prep_fn([cq_ref[:, _hsl(j)] for j in heads], [ck_ref[:, _hsl(j)] for j in heads],
                       [cv_ref[:, _hsl(j)] for j in heads], bg_ref[...], [m[0] for m in masks], [m[1] for m in masks])
        for o_ref, rs in zip((u0_ref, wk_ref, qd_ref, kd_ref, qk_ref), res[:5]):
            for j in heads:
                o_ref[:, _hsl(j)] = rs[j]
        for j in heads:
            gl_ref[j * SUBLANES:(j + 1) * SUBLANES, :] = res[5][j]

    blk = lambda off: pl.BlockSpec((CHUNK, nh * hd), lambda i: (i, off))
    return pl.pallas_call(
        body, grid=(n,),
        in_specs=[blk(0), blk(1), blk(2), pl.BlockSpec((CHUNK, LANES), lambda i: (i, 0))],
        out_specs=[blk(0)] * 5 + [pl.BlockSpec((nh * SUBLANES, LANES), lambda i: (i, 0))],
        out_shape=[jax.ShapeDtypeStruct((t, nh * hd), F32)] * 5 + [jax.ShapeDtypeStruct((n * nh * SUBLANES, LANES), F32)],
        name=name, compiler_params=_params(("parallel",)),
    )(cqkv, cqkv, cqkv, bg)


def delta_prep_bwd(cqkv, bg, cts, *, name):
    t = cqkv.shape[0]
    nh, hd, n = N_HEADS_A, HEAD_DIM_A, t // CHUNK

    def body(cq_ref, ck_ref, cv_ref, bg_ref, c0, c1, c2, c3, c4, c5, dc_ref, dbg_ref):
        heads = range(nh)
        masks = [_head_masks(j) for j in heads]
        _, vjp = jax.vjp(lambda a, b, c, d: _prep_fn(a, b, c, d, [m[0] for m in masks], [m[1] for m in masks]),
                         [cq_ref[:, _hsl(j)] for j in heads], [ck_ref[:, _hsl(j)] for j in heads],
                         [cv_ref[:, _hsl(j)] for j in heads], bg_ref[...])
        cts = tuple([c[:, _hsl(j)] for j in heads] for c in (c0, c1, c2, c3, c4))
        dqs, dks, dvs, dbg = vjp(cts + ([c5[j * SUBLANES:(j + 1) * SUBLANES, :] for j in heads],))
        for part, ds in enumerate((dqs, dks, dvs)):
            for j in heads:
                dc_ref[:, _hsl(part * nh + j)] = ds[j]
        dbg_ref[...] = dbg

    blk = lambda off: pl.BlockSpec((CHUNK, nh * hd), lambda i: (i, off))
    gl_spec = pl.BlockSpec((nh * SUBLANES, LANES), lambda i: (i, 0))
    bg_spec = pl.BlockSpec((CHUNK, LANES), lambda i: (i, 0))
    return pl.pallas_call(
        body, grid=(n,),
        in_specs=[blk(0), blk(1), blk(2), bg_spec] + [blk(0)] * 5 + [gl_spec],
        out_specs=[pl.BlockSpec((CHUNK, 3 * nh * hd), lambda i: (i, 0)), bg_spec],
        out_shape=[jax.ShapeDtypeStruct((t, 3 * nh * hd), F32), jax.ShapeDtypeStruct((t, LANES), F32)],
        name=name, compiler_params=_params(("parallel",)),
    )(cqkv, cqkv, cqkv, bg, *cts)


def delta_scan(u0, wk, qd, kd, qk, gl, *, name):
    t = u0.shape[0]
    nh, hd, n = N_HEADS_A, HEAD_DIM_A, t // CHUNK

    def body(u0_ref, wk_ref, qd_ref, kd_ref, qk_ref, gl_ref, o_ref, sin_ref, s_ref):
        @pl.when(pl.program_id(0) == 0)
        def _():
            s_ref[...] = jnp.zeros_like(s_ref)

        heads = list(range(nh))
        cols = lambda ref: [ref[:, _hsl(h)].astype(BF16) for h in heads]
        ss = [s_ref[h] for h in heads]
        for h in heads:
            sin_ref[h] = ss[h]
        sbs = _each(lambda s: s.astype(BF16), ss)
        ubs = _each(lambda h, wkb, sb: (u0_ref[:, _hsl(h)] - _dot(wkb, sb)).astype(BF16), heads, cols(wk_ref), sbs)
        os_ = _each(lambda qdb, sb, qkb, ub: _dot(qdb, sb) + _dot(qkb, ub), cols(qd_ref), sbs, cols(qk_ref), ubs)
        sn = _each(lambda h, s, kdb, ub: gl_ref[pl.ds(h * SUBLANES, 1), :] * s + _dot(kdb, ub, TN), heads, ss, cols(kd_ref), ubs)
        for h in heads:
            o_ref[:, _hsl(h)] = os_[h]
            s_ref[h] = sn[h]

    blk = pl.BlockSpec((CHUNK, nh * hd), lambda i: (i, 0))
    return pl.pallas_call(
        body, grid=(n,),
        in_specs=[blk] * 5 + [pl.BlockSpec((nh * SUBLANES, LANES), lambda i: (i, 0))],
        out_specs=[blk, pl.BlockSpec((None, nh, hd, hd), lambda i: (i, 0, 0, 0))],
        out_shape=[jax.ShapeDtypeStruct((t, nh * hd), F32), jax.ShapeDtypeStruct((n, nh, hd, hd), F32)],
        scratch_shapes=[pltpu.VMEM((nh, hd, hd), F32)], name=name,
        compiler_params=_params(("arbitrary",)),
    )(u0, wk, qd, kd, qk, gl)


def delta_scan_bwd(do, u0, wk, qd, kd, qk, gl, s_in, *, name):
    t = u0.shape[0]
    nh, hd, n = N_HEADS_A, HEAD_DIM_A, t // CHUNK

    def body(do_ref, u0_ref, wk_ref, qd_ref, kd_ref, qk_ref, gl_ref, sin_ref,
             du0_ref, dwk_ref, dqd_ref, dkd_ref, dqk_ref, dgl_ref, ds_ref):
        @pl.when(pl.program_id(0) == 0)
        def _():
            ds_ref[...] = jnp.zeros_like(ds_ref)

        corner = (lax.broadcasted_iota(jnp.int32, (SUBLANES, LANES), 0) == 0) & (lax.broadcasted_iota(jnp.int32, (SUBLANES, LANES), 1) == 0)
        heads = list(range(nh))
        cols = lambda ref: [ref[:, _hsl(h)].astype(BF16) for h in heads]
        ss, dss = [sin_ref[h] for h in heads], [ds_ref[h] for h in heads]
        sbs, dsbs = _each(lambda s: s.astype(BF16), ss), _each(lambda d: d.astype(BF16), dss)
        dobs, wkbs, qdbs, kdbs, qkbs = cols(do_ref), cols(wk_ref), cols(qd_ref), cols(kd_ref), cols(qk_ref)
        ubs = _each(lambda h, wkb, sb: (u0_ref[:, _hsl(h)] - _dot(wkb, sb)).astype(BF16), heads, wkbs, sbs)
        dus = _each(lambda qkb, dob, kdb, dsb: _dot(qkb, dob, TN) + _dot(kdb, dsb), qkbs, dobs, kdbs, dsbs)
        dubs = _each(lambda du: du.astype(BF16), dus)
        dwks = _each(lambda dub, sb: -_dot(dub, sb, NT), dubs, sbs)
        dqds = _each(lambda dob, sb: _dot(dob, sb, NT), dobs, sbs)
        dkds = _each(lambda ub, dsb: _dot(ub, dsb, NT), ubs, dsbs)
        dqks = _each(lambda dob, ub: _dot(dob, ub, NT), dobs, ubs)
        dgls = _each(lambda s, d: jnp.sum(jnp.sum(s * d, axis=1, keepdims=True), axis=0, keepdims=True), ss, dss)
        dsn = _each(lambda h, d, qdb, dob, wkb, dub: gl_ref[pl.ds(h * SUBLANES, 1), :] * d + _dot(qdb, dob, TN) - _dot(wkb, dub, TN),
                    heads, dss, qdbs, dobs, wkbs, dubs)
        for h in heads:
            du0_ref[:, _hsl(h)] = dus[h]
            dwk_ref[:, _hsl(h)] = dwks[h]
            dqd_ref[:, _hsl(h)] = dqds[h]
            dkd_ref[:, _hsl(h)] = dkds[h]
            dqk_ref[:, _hsl(h)] = dqks[h]
            dgl_ref[h * SUBLANES:(h + 1) * SUBLANES, :] = jnp.where(corner, dgls[h], 0.0)
            ds_ref[h] = dsn[h]

    blk = pl.BlockSpec((CHUNK, nh * hd), lambda i: (n - 1 - i, 0))
    gl_spec = pl.BlockSpec((nh * SUBLANES, LANES), lambda i: (n - 1 - i, 0))
    return pl.pallas_call(
        body, grid=(n,),
        in_specs=[blk] * 6 + [gl_spec, pl.BlockSpec((None, nh, hd, hd), lambda i: (n - 1 - i, 0, 0, 0))],
        out_specs=[blk] * 5 + [gl_spec],
        out_shape=[jax.ShapeDtypeStruct((t, nh * hd), F32)] * 5 + [jax.ShapeDtypeStruct((n * nh * SUBLANES, LANES), F32)],
        scratch_shapes=[pltpu.VMEM((nh, hd, hd), F32)], name=name,
        compiler_params=_params(("arbitrary",)),
    )(do, u0, wk, qd, kd, qk, gl, s_in)


N_PAIRS = N_HEADS_B // 2
PAIRS_PER_KV = N_PAIRS // N_KV_B


def _psl(j):
    return slice(j * LANES, (j + 1) * LANES)


def _att_fn(qps, kc, kp, vc, vp, sinks, kvf, first):
    w = WINDOW
    lane = lax.broadcasted_iota(jnp.int32, (1, LANES), 1)
    lo = (lane < HEAD_DIM_B).astype(F32)
    qi = lax.broadcasted_iota(jnp.int32, (w, w), 0)
    kj = lax.broadcasted_iota(jnp.int32, (w, w), 1)
    dist_c = (qi - kj).astype(F32)
    valid_c = kj <= qi
    valid_p = (kj > qi) & (first < 0.5)
    kcb, kpb, vcb, vpb = (a.astype(BF16) for a in (kc, kp, vc, vp))
    scale = HEAD_DIM_B ** -0.5
    heads = [(j, half) for j in range(PAIRS_PER_KV) for half in range(2)]
    hmasks = [lo if half == 0 else 1.0 - lo for _, half in heads]
    hds = [2.0 * (PAIRS_PER_KV * kvf + j) + half for j, half in heads]
    slopes = _each(lambda hd: jnp.exp(-(hd + 1.0) * (8.0 / N_HEADS_B * math.log(2.0))), hds)
    snks = _each(lambda hd: jnp.sum(sinks * (lane.astype(F32) == hd).astype(F32), axis=1, keepdims=True), hds)
    qhs = _each(lambda jh, hm: (qps[jh[0]] * hm).astype(BF16), heads, hmasks)
    lcs = _each(lambda qh, sl: jnp.where(valid_c, _dot(qh, kcb, NT) * scale - sl * dist_c, NEG), qhs, slopes)
    lps = _each(lambda qh, sl: jnp.where(valid_p, _dot(qh, kpb, NT) * scale - sl * (dist_c + w), NEG), qhs, slopes)
    ms = _each(lambda lc, lp, sk: lax.stop_gradient(jnp.maximum(jnp.maximum(jnp.max(lc, axis=1, keepdims=True),
                                                                            jnp.max(lp, axis=1, keepdims=True)), sk)), lcs, lps, snks)
    ecs = _each(lambda lc, m: jnp.exp(lc - m), lcs, ms)
    eps = _each(lambda lp, m: jnp.exp(lp - m), lps, ms)
    invs = _each(lambda ec, ep, sk, m: 1.0 / (jnp.sum(ec, axis=1, keepdims=True) + jnp.sum(ep, axis=1, keepdims=True) + jnp.exp(sk - m)),
                 ecs, eps, snks, ms)
    ohs = _each(lambda ec, ep, inv, hm: (_dot((ec * inv).astype(BF16), vcb) + _dot((ep * inv).astype(BF16), vpb)) * hm,
                ecs, eps, invs, hmasks)
    return [ohs[2 * j] + ohs[2 * j + 1] for j in range(PAIRS_PER_KV)]


def _scalar11(v):
    return jnp.full((1, 1), v, F32)


def swa_fwd(qsrc, kd, vd, sinks, *, name):
    t = kd.shape[0]
    nb = t // WINDOW

    def body(q_ref, kc_ref, kp_ref, vc_ref, vp_ref, s_ref, o_ref):
        first = _scalar11((pl.program_id(0) == 0).astype(F32))
        kvf = _scalar11(pl.program_id(1).astype(F32))
        outs = _att_fn([q_ref[:, _psl(j)] for j in range(PAIRS_PER_KV)], kc_ref[...], kp_ref[...], vc_ref[...], vp_ref[...],
                       s_ref[...], kvf, first)
        for j in range(PAIRS_PER_KV):
            o_ref[:, _psl(j)] = outs[j].astype(o_ref.dtype)

    cur = pl.BlockSpec((WINDOW, LANES), lambda i, kv: (i, kv))
    prev = pl.BlockSpec((WINDOW, LANES), lambda i, kv: (jnp.maximum(i - 1, 0), kv))
    qs = pl.BlockSpec((WINDOW, PAIRS_PER_KV * LANES), lambda i, kv: (i, kv))
    return pl.pallas_call(
        body, grid=(nb, N_KV_B),
        in_specs=[qs, cur, prev, cur, prev, pl.BlockSpec((1, LANES), lambda i, kv: (0, 0))],
        out_specs=qs, out_shape=jax.ShapeDtypeStruct((t, N_PAIRS * LANES), BF16), name=name,
        compiler_params=_params(("parallel", "parallel")),
    )(qsrc, kd, kd, vd, vd, sinks)


def swa_bwd(do, qsrc, kd, vd, sinks, *, name):
    t = kd.shape[0]
    nb = t // WINDOW

    def body(do_ref, q_ref, kc_ref, kp_ref, vc_ref, vp_ref, s_ref, dq_ref, dk_ref, dv_ref, ds_ref, carry_k, carry_v):
        step, kv = pl.program_id(0), pl.program_id(1)
        first = _scalar11((step == nb - 1).astype(F32))

        @pl.when((step == 0) & (kv == 0))
        def _():
            carry_k[...] = jnp.zeros_like(carry_k)
            carry_v[...] = jnp.zeros_like(carry_v)
            ds_ref[...] = jnp.zeros_like(ds_ref)

        kvf = _scalar11(kv.astype(F32))
        pairs = range(PAIRS_PER_KV)
        _, vjp = jax.vjp(lambda *a: _att_fn(*a, kvf, first), [q_ref[:, _psl(j)] for j in pairs],
                         kc_ref[...], kp_ref[...], vc_ref[...], vp_ref[...], s_ref[...])
        dqs, dkc, dkp, dvc, dvp, dsk = vjp([do_ref[:, _psl(j)].astype(F32) for j in pairs])
        for j in pairs:
            dq_ref[:, _psl(j)] = dqs[j]
        ds_ref[...] += dsk
        fold = lambda g: g + pltpu.roll(g, HEAD_DIM_B, 1)
        dk_ref[...] = fold(dkc + carry_k[kv])
        dv_ref[...] = fold(dvc + carry_v[kv])
        carry_k[kv] = dkp
        carry_v[kv] = dvp

    rev = lambda i: nb - 1 - i
    cur = pl.BlockSpec((WINDOW, LANES), lambda i, kv: (rev(i), kv))
    prev = pl.BlockSpec((WINDOW, LANES), lambda i, kv: (jnp.maximum(rev(i) - 1, 0), kv))
    qs = pl.BlockSpec((WINDOW, PAIRS_PER_KV * LANES), lambda i, kv: (rev(i), kv))
    sk = pl.BlockSpec((1, LANES), lambda i, kv: (0, 0))
    return pl.pallas_call(
        body, grid=(nb, N_KV_B),
        in_specs=[qs, qs, cur, prev, cur, prev, sk],
        out_specs=[qs, cur, cur, sk],
        out_shape=[jax.ShapeDtypeStruct((t, N_PAIRS * LANES), F32), jax.ShapeDtypeStruct((t, N_KV_B * LANES), F32),
                   jax.ShapeDtypeStruct((t, N_KV_B * LANES), F32), jax.ShapeDtypeStruct((1, LANES), F32)],
        scratch_shapes=[pltpu.VMEM((N_KV_B, WINDOW, LANES), F32), pltpu.VMEM((N_KV_B, WINDOW, LANES), F32)],
        name=name, compiler_params=_params(("arbitrary", "arbitrary")),
    )(do, qsrc, kd, kd, vd, vd, sinks)


def loss_head(h, tgt, w, *, name):
    t, d = h.shape
    tm = min(256, t)

    def body(h_ref, t_ref, w_ref, dh_ref, dw_ref, l_ref):
        tg = t_ref[...]

        def f(hv, wv):
            err = _f_norm(hv, wv) - tg
            return 0.5 * jnp.sum(jnp.sum(err * err, axis=1, keepdims=True), axis=0, keepdims=True) * (1.0 / d)

        lv, vjp = jax.vjp(f, h_ref[...], w_ref[...])
        dh, dw = vjp(jnp.ones((1, 1), F32))
        dh_ref[...] = dh
        first = pl.program_id(0) == 0

        @pl.when(first)
        def _():
            dw_ref[...] = dw
            l_ref[...] = lv * jnp.ones((1, LANES), F32)

        @pl.when(jnp.logical_not(first))
        def _():
            dw_ref[...] += dw
            l_ref[...] += lv * jnp.ones((1, LANES), F32)

    rows = pl.BlockSpec((tm, d), lambda i: (i, 0))
    one = lambda c: pl.BlockSpec((1, c), lambda i: (0, 0))
    return pl.pallas_call(
        body, grid=(t // tm,), in_specs=[rows, rows, one(d)], out_specs=[rows, one(d), one(LANES)],
        out_shape=[jax.ShapeDtypeStruct((t, d), F32), jax.ShapeDtypeStruct((1, d), F32), jax.ShapeDtypeStruct((1, LANES), F32)],
        name=name, compiler_params=_params(("arbitrary",)),
    )(h, tgt, w)


def adamw(w, g, m, v, *, name):
    r, c = w.shape
    tr = r
    if r % SUBLANES == 0:
        for cand in range(SUBLANES, min(r, 256) + 1, SUBLANES):
            if r % cand == 0:
                tr = cand

    def body(w_ref, g_ref, m_ref, v_ref, d_ref, mo_ref, vo_ref):
        gv = g_ref[...]
        mn = ADAM_B1 * m_ref[...] + (1.0 - ADAM_B1) * gv
        vn = ADAM_B2 * v_ref[...] + (1.0 - ADAM_B2) * jnp.square(gv)
        m_hat = mn / (1.0 - ADAM_B1 ** ADAM_STEP)
        v_hat = vn / (1.0 - ADAM_B2 ** ADAM_STEP)
        d_ref[...] = -ADAM_LR * (m_hat / (jnp.sqrt(v_hat) + ADAM_EPS) + ADAM_WD * w_ref[...])
        mo_ref[...] = mn
        vo_ref[...] = vn

    spec = pl.BlockSpec((tr, c), lambda i: (i, 0))
    return pl.pallas_call(
        body, grid=(r // tr,), in_specs=[spec] * 4, out_specs=[spec] * 3,
        out_shape=[jax.ShapeDtypeStruct((r, c), F32)] * 3, name=name, compiler_params=_params(("parallel",)),
    )(w, g, m, v)


def _place():
    return lax.axis_index("x"), lax.axis_index("y"), lax.axis_index("c")


def allgather8(blk, *, name):
    def body(x_ref, out_ref, send_sems, recv_sems, local_sem):
        x, y, c = _place()
        me = 4 * x + 2 * y + c
        mine = pltpu.make_async_copy(x_ref, out_ref.at[me], local_sem)
        mine.start()
        sent = []
        for k in range(1, N_DEV):
            to = (x ^ ((k >> 2) & 1), y ^ ((k >> 1) & 1), c ^ (k & 1))
            cp = pltpu.make_async_remote_copy(src_ref=x_ref, dst_ref=out_ref.at[me], send_sem=send_sems.at[k - 1],
                                              recv_sem=recv_sems.at[k - 1], device_id=to, device_id_type=MESH)
            cp.start()
            sent.append(cp)
        for k in range(1, N_DEV):
            frm = me ^ k
            pltpu.make_async_remote_copy(src_ref=x_ref, dst_ref=out_ref.at[frm], send_sem=send_sems.at[k - 1],
                                         recv_sem=recv_sems.at[k - 1], device_id=(x, y, c), device_id_type=MESH).wait_recv()
        for cp in sent:
            cp.wait_send()
        mine.wait()

    vm = pl.BlockSpec(memory_space=pltpu.VMEM)
    return pl.pallas_call(
        body, in_specs=[vm], out_specs=vm, out_shape=jax.ShapeDtypeStruct((N_DEV,) + blk.shape, blk.dtype), name=name,
        scratch_shapes=[pltpu.SemaphoreType.DMA((N_DEV - 1,)), pltpu.SemaphoreType.DMA((N_DEV - 1,)), pltpu.SemaphoreType.DMA],
    )(blk)


def _other_chips(x, y):
    return [(1 - x, y), (x, 1 - y), (1 - x, 1 - y)]


def _hbm_call(body, ins, out_shapes, n_sems, name):
    hbm = pl.BlockSpec(memory_space=pl.ANY)
    return pl.pallas_call(
        body, in_specs=[hbm] * len(ins), out_specs=[hbm] * len(out_shapes), out_shape=out_shapes, name=name,
        scratch_shapes=[pltpu.SemaphoreType.DMA((n_sems,)), pltpu.SemaphoreType.DMA((n_sems,))],
    )(*ins)


def _half_rows(c, rh):
    return pl.ds(pl.multiple_of(c * rh, BF16_ROWS), rh)


def gather_units(units, *, name):
    nu = len(units)
    shapes = []
    for arr, layer_major in units:
        r, cols = arr.shape
        shapes.append(jax.ShapeDtypeStruct((2, N_CHIPS, r // 2, cols) if layer_major else (N_CHIPS, r, cols), arr.dtype))

    def body(*refs):
        in_refs, out_refs, send_sems, recv_sems = refs[:nu], refs[nu:2 * nu], refs[2 * nu], refs[2 * nu + 1]
        x, y, c = _place()
        me_chip = 2 * x + y
        sib = (x, y, 1 - c)
        chips = _other_chips(x, y)

        def copy(k, src, dst, to):
            return pltpu.make_async_remote_copy(src_ref=src, dst_ref=dst, send_sem=send_sems.at[k], recv_sem=recv_sems.at[k],
                                                device_id=to, device_id_type=MESH)

        first, passed, landing = [], [], []
        for u, (arr, layer_major) in enumerate(units):
            rh = arr.shape[0] // 2
            out_ref = out_refs[u]
            slot = (lambda chip, half, o=out_ref: o.at[half, chip]) if layer_major else \
                   (lambda chip, half, o=out_ref, rh=rh: o.at[chip, _half_rows(half, rh), :])
            my_half = in_refs[u].at[_half_rows(c, rh), :]
            for j, (cx, cy) in enumerate(chips):
                k = 6 * u + j
                first.append(copy(k, my_half, slot(me_chip, c), (cx, cy, c)))
                passed.append(copy(k + 3, slot(2 * cx + cy, c), slot(2 * cx + cy, c), sib))
                landing.append((copy(k, my_half, slot(2 * cx + cy, c), sib), copy(k + 3, my_half, slot(2 * cx + cy, 1 - c), sib)))
        for cp in first:
            cp.start()
        for (over_ici, _), fwd in zip(landing, passed):
            over_ici.wait_recv()
            fwd.start()
        for _, from_sibling in landing:
            from_sibling.wait_recv()
        for cp in first + passed:
            cp.wait_send()

    return _hbm_call(body, [a for a, _ in units], shapes, 6 * nu, name)


def swap_units(units, *, name):
    nu = len(units)

    def body(*refs):
        g_refs, out_refs, send_sems, recv_sems = refs[:nu], refs[nu:2 * nu], refs[2 * nu], refs[2 * nu + 1]
        x, y, c = _place()
        cps = [pltpu.make_async_remote_copy(src_ref=g_refs[u].at[:, _half_rows(1 - c, units[u].shape[1] // 2), :], dst_ref=out_refs[u],
                                            send_sem=send_sems.at[u], recv_sem=recv_sems.at[u], device_id=(x, y, 1 - c),
                                            device_id_type=MESH) for u in range(nu)]
        for cp in cps:
            cp.start()
        for cp in cps:
            cp.wait()

    shapes = [jax.ShapeDtypeStruct((N_CHIPS, g.shape[1] // 2, g.shape[2]), g.dtype) for g in units]
    return _hbm_call(body, units, shapes, nu, name)


def scatter_units(units, *, name):
    nu = len(units)

    def body(*refs):
        h_refs, out_refs, send_sems, recv_sems = refs[:nu], refs[nu:2 * nu], refs[2 * nu], refs[2 * nu + 1]
        x, y, c = _place()
        cps = [pltpu.make_async_remote_copy(src_ref=h_refs[u].at[2 * cx + cy], dst_ref=out_refs[u].at[j], send_sem=send_sems.at[3 * u + j],
                                            recv_sem=recv_sems.at[3 * u + j], device_id=(cx, cy, c), device_id_type=MESH)
               for u in range(nu) for j, (cx, cy) in enumerate(_other_chips(x, y))]
        for cp in cps:
            cp.start()
        for cp in cps:
            cp.wait()

    shapes = [jax.ShapeDtypeStruct((3,) + h.shape[1:], h.dtype) for h in units]
    return _hbm_call(body, units, shapes, 3 * nu, name)


def join_units(units, *, name):
    nu = len(units)

    def body(*refs):
        h_refs, out_refs, send_sems, recv_sems = refs[:nu], refs[nu:2 * nu], refs[2 * nu], refs[2 * nu + 1]
        x, y, c = _place()
        cps = [pltpu.make_async_remote_copy(src_ref=h_refs[u], dst_ref=out_refs[u], send_sem=send_sems.at[u], recv_sem=recv_sems.at[u],
                                            device_id=(x, y, 1 - c), device_id_type=MESH) for u in range(nu)]
        for cp in cps:
            cp.start()
        for cp in cps:
            cp.wait()

    return _hbm_call(body, units, [jax.ShapeDtypeStruct(h.shape, h.dtype) for h in units], nu, name)


def add_parts(parts, *, out_dtype, name):
    rows, cols = parts[0].shape
    tr = rows
    for cand in range(BF16_ROWS, min(rows, 512) + 1, BF16_ROWS):
        if rows % cand == 0:
            tr = cand

    def body(*refs):
        acc = refs[0][...].astype(F32)
        for r in refs[1:-1]:
            acc = acc + r[...].astype(F32)
        refs[-1][...] = acc.astype(refs[-1].dtype)

    spec = pl.BlockSpec((tr, cols), lambda i: (i, 0))
    return pl.pallas_call(
        body, grid=(rows // tr,), in_specs=[spec] * len(parts), out_specs=spec,
        out_shape=jax.ShapeDtypeStruct((rows, cols), out_dtype), name=name, compiler_params=_params(("parallel",)),
    )(*parts)


def sum8(g, *, name):
    def body(g_ref, o_ref):
        acc = g_ref[0]
        for d in range(1, N_DEV):
            acc = acc + g_ref[d]
        o_ref[...] = acc

    return pl.pallas_call(body, out_shape=jax.ShapeDtypeStruct(g.shape[1:], F32), name=name)(g)


def _dup_halves(a):
    t = a.shape[0]
    a = a.reshape(t, N_KV_B, HEAD_DIM_B)
    return jnp.concatenate([a, a], axis=-1).reshape(t, N_KV_B * LANES)


def _undup(a):
    t = a.shape[0]
    return a.reshape(t, N_KV_B, LANES)[:, :, :HEAD_DIM_B].reshape(t, N_KV_B * HEAD_DIM_B)


def _lane_pad(v, offset=0):
    return jnp.zeros((1, LANES), F32).at[0, offset:offset + v.shape[0]].set(v)


SHARD_UP = 2 * D_FF // N_CHIPS
SHARD_BIN = (N_HEADS_B + 2 * N_KV_B) * HEAD_DIM_B // N_CHIPS
SHARD_PROJ = D_MODEL // N_CHIPS


def local_step(x, p, tgt, sm, wt):
    t = x.shape[0]
    rtm = min(256, t)
    hk = N_HEADS_A * HEAD_DIM_A
    qd_b = N_HEADS_B * HEAD_DIM_B
    kd_b = N_KV_B * HEAD_DIM_B
    gw, gs = {}, {}
    norm = lambda h, w, nm: tile_map(_f_norm, [(h, D_MODEL, 0)], [w], [(D_MODEL, BF16)], tm=rtm, ncol=1, name=nm)[0]

    def norm_bwd(h, w, dy, add, nm):
        (dh,), (dw,) = tile_vjp(_f_norm, [(h, D_MODEL, 0)], [w], [(dy, D_MODEL, 0)], n_diff=1, tm=rtm, ncol=1, name=nm,
                                add=(add, D_MODEL, 0))
        return dh, dw

    spec = pl.BlockSpec
    mtm = _tile(D_MODEL, MM_TM_CAP)
    p_bf = p.astype(BF16)
    a_main, a_tail = wt["a_main"], wt["a_tail"]
    alog_p = _lane_pad(sm["a_log"][0], N_HEADS_A)
    dtb_p = _lane_pad(sm["a_dt_bias"][0], N_HEADS_A)
    sinks_p = _lane_pad(sm["b_sinks"][0])
    nw = lambda name, i: sm[name][i:i + 1]

    saved = []
    h = x
    hn_next = norm(h, nw("norm_mix", 0), "norm_mix0")
    for i in range(DEPTH):
        s = {"h0": h, "hn": hn_next}
        if i % 2 == 0:
            s["pm"] = mm(s["hn"], a_main, name="a_in_main")
            s["pt"] = mm(s["hn"], a_tail, name="a_in_tail")
            s["c"] = conv_fwd(s["pm"], wt["a_conv"], name="a_conv")
            s["bg"] = tile_map(_f_betag, [(s["pt"], LANES, 0)], [alog_p, dtb_p], [(LANES, F32)], tm=rtm, ncol=1, name="a_betag")[0]
            s["prep"] = delta_prep(s["c"], s["bg"], name="a_prep")
            s["o"], s["s_in"] = delta_scan(*s["prep"], name="a_scan")
            s["on"] = tile_map(_f_gnorm, [(s["o"], HEAD_DIM_A, 0), (s["pm"], HEAD_DIM_A, 3 * N_HEADS_A)], [sm["a_norm"]],
                               [(HEAD_DIM_A, BF16)], tm=rtm, ncol=N_HEADS_A, name="a_gnorm")[0]
            h, s["hf"] = mm(s["on"], wt["a_w_out"], add=h, norm_w=nw("norm_ffn", i), name="a_out")
        else:
            s["pb"] = mm(s["hn"], wt["b_w_in"], name="b_in", n=N_CHIPS * SHARD_BIN, tn=SHARD_BIN, tk=D_MODEL,
                         b_spec=spec((None, D_MODEL, SHARD_BIN), lambda r, j, kk: (j, kk, 0)))
            s["kd"], s["vd"] = _dup_halves(s["pb"][:, qd_b:qd_b + kd_b]), _dup_halves(s["pb"][:, qd_b + kd_b:])
            s["ao"] = swa_fwd(s["pb"], s["kd"], s["vd"], sinks_p, name="b_att")
            h, s["hf"] = mm(s["ao"], wt["b_w_out"], add=h, norm_w=nw("norm_ffn", i), name="b_out")
        s["h1"] = h
        s["u"] = mm(s["hf"], wt["f_w_up"], name=f"f_up{i}", n=2 * D_FF, tn=SHARD_UP, tk=D_MODEL,
                    b_spec=spec((None, None, D_MODEL, SHARD_UP), lambda r, j, kk, i=i: (j, i, kk, 0)))
        s["act"] = conv_act_fwd(s["u"], wt["f_conv"][i], name=f"f_conv_act{i}")
        h, s["hp"] = mm(s["act"], wt["f_w_down"], add=h, norm_w=nw("norm_ple", i), name=f"f_down{i}", n=D_MODEL, tn=D_MODEL,
                        tk=D_FF // 2, b_spec=spec((None, D_FF // 2, D_MODEL), lambda r, j, kk, i=i: (i, kk, j)))
        s["h2"] = h
        s["gl"] = mm(s["hp"], wt["ple_w_gate"], name=f"ple_gate{i}", n=D_MODEL, tn=D_MODEL, tk=D_MODEL,
                     b_spec=spec((None, D_MODEL, D_MODEL), lambda r, j, kk, i=i: (i, kk, j)))
        s["pe"] = mm(p_bf[i], wt["ple_w_proj"], name=f"ple_proj{i}", n=D_MODEL, tn=SHARD_PROJ, tk=PLE_DIM,
                     b_spec=spec((None, None, PLE_DIM, SHARD_PROJ), lambda r, j, kk, i=i: (j, i, kk, 0)))
        rows3 = [(h, D_MODEL, 0), (s["gl"], D_MODEL, 0), (s["pe"], D_MODEL, 0)]
        if i + 1 < DEPTH:
            def mix_norm(hv, g, e, wn):
                hn = hv + _f_ple(g, e)
                return hn, _f_norm(hn, wn)
            h, hn_next = tile_map(mix_norm, rows3, [nw("norm_mix", i + 1)], [(D_MODEL, F32), (D_MODEL, BF16)], tm=rtm, ncol=1,
                                  name=f"ple_mix{i}")
        else:
            h = tile_map(lambda hv, g, e: hv + _f_ple(g, e), rows3, [], [(D_MODEL, F32)], tm=rtm, ncol=1, name=f"ple_mix{i}")[0]
        saved.append(s)

    dh, gnf, loss = loss_head(h, tgt, sm["norm_final"][None, :], name="loss_head")
    gs["norm_final"] = gnf[0]

    g_mix, g_ffn, g_ple = [None] * DEPTH, [None] * DEPTH, [None] * DEPTH
    g_conv, g_down = [None] * DEPTH, [None] * DEPTH
    g_up = g_proj = g_gate = None
    for i in reversed(range(DEPTH)):
        s = saved[i]
        (dgl, dpe), _ = tile_vjp(_f_ple, [(s["gl"], D_MODEL, 0), (s["pe"], D_MODEL, 0)], [], [(dh, D_MODEL, 0)], n_diff=2,
                                 tm=rtm, ncol=1, name=f"ple_mix_bwd{i}")
        g_proj = mm(p_bf[i], dpe, ta=True, name=f"ple_proj_dw{i}", out_dtype=BF16, tn=SHARD_PROJ, into=g_proj,
                    o_shape=(N_CHIPS, DEPTH, PLE_DIM, SHARD_PROJ),
                    o_spec=spec((None, None, PLE_DIM, SHARD_PROJ), lambda r, j, kk, i=i: (j, i, r, 0)))
        g_gate = mm(s["hp"], dgl, ta=True, name=f"ple_gate_dw{i}", out_dtype=BF16, tm_cap=SHARD_PROJ, tn=D_MODEL, into=g_gate,
                    o_shape=(N_CHIPS, DEPTH, SHARD_PROJ, D_MODEL),
                    o_spec=spec((None, None, SHARD_PROJ, D_MODEL), lambda r, j, kk, i=i: (r, i, 0, j)))
        dhp = mm(dgl, wt["ple_w_gate"], tb=True, name=f"ple_gate_dx{i}", n=D_MODEL, tn=D_MODEL, tk=D_MODEL,
                 b_spec=spec((None, D_MODEL, D_MODEL), lambda r, j, kk, i=i: (i, j, kk)))
        dh, g_ple[i] = norm_bwd(s["h2"], nw("norm_ple", i), dhp, dh, f"norm_ple_bwd{i}")

        dact = mm(dh, wt["f_w_down"], tb=True, name=f"f_down_dx{i}", n=D_FF, tn=D_FF // 2, tk=D_MODEL,
                  b_spec=spec((None, D_FF // 2, D_MODEL), lambda r, j, kk, i=i: (i, j, kk)))
        g_down[i] = mm(s["act"], dh, ta=True, name=f"f_down_dw{i}", out_dtype=BF16, tm_cap=D_FF // 2)
        du_halves = conv_act_bwd(s["u"], dact, wt["f_conv"][i], name=f"f_conv_act_bwd{i}")
        g_conv[i] = jnp.concatenate(du_halves[2:], axis=1)
        dhf = None
        for half, du in enumerate(du_halves[:2]):
            c0 = half * (N_CHIPS // 2)
            g_up = mm(s["hf"], du, ta=True, name=f"f_up_dw{i}_{half}", out_dtype=BF16, tn=SHARD_UP, into=g_up,
                      o_shape=(N_CHIPS, DEPTH, D_MODEL, SHARD_UP),
                      o_spec=spec((None, None, mtm, SHARD_UP), lambda r, j, kk, i=i, c0=c0: (c0 + j, i, r, 0)))
            dhf = mm(du, wt["f_w_up"], tb=True, name=f"f_up_dx{i}_{half}", n=D_MODEL, tn=D_MODEL, tk=SHARD_UP, add=dhf,
                     b_spec=spec((None, None, D_MODEL, SHARD_UP), lambda r, j, kk, i=i, c0=c0: (c0 + kk, i, j, 0)))
        dh, g_ffn[i] = norm_bwd(s["h1"], nw("norm_ffn", i), dhf, dh, f"norm_ffn_bwd{i}")

        if i % 2 == 0:
            don = mm(dh, wt["a_w_out"], tb=True, name="a_out_dx")
            gw["a_w_out"] = mm(s["on"], dh, ta=True, name="a_out_dw", out_dtype=BF16).reshape(N_CHIPS, SHARD_PROJ, D_MODEL)
            (do, dz), (gan,) = tile_vjp(_f_gnorm, [(s["o"], HEAD_DIM_A, 0), (s["pm"], HEAD_DIM_A, 3 * N_HEADS_A)], [sm["a_norm"]],
                                        [(don, HEAD_DIM_A, 0)], n_diff=2, tm=rtm, ncol=N_HEADS_A, name="a_gnorm_bwd")
            gs["a_norm"] = gan
            dprep = delta_scan_bwd(do, *s["prep"], s["s_in"], name="a_scan_bwd")
            dc, dbg = delta_prep_bwd(s["c"], s["bg"], dprep, name="a_prep_bwd")
            (dpt,), (galog, gdtb) = tile_vjp(_f_betag, [(s["pt"], LANES, 0)], [alog_p, dtb_p], [(dbg, LANES, 0)], n_diff=1,
                                             tm=rtm, ncol=1, name="a_betag_bwd")
            gs["a_log"] = galog[:, N_HEADS_A:2 * N_HEADS_A]
            gs["a_dt_bias"] = gdtb[:, N_HEADS_A:2 * N_HEADS_A]
            dqkv, gs["a_conv"] = conv_bwd(dc, s["pm"], wt["a_conv"], name="a_conv_bwd")
            dpm = jnp.concatenate([dqkv, dz], axis=1)
            dhn = mm(dpm, a_main, tb=True, name="a_in_main_dx")
            dhn = mm(dpt, a_tail, tb=True, add=dhn, name="a_in_tail_dx")
            g_main = mm(s["hn"], dpm, ta=True, name="a_in_main_dw", out_dtype=BF16)
            g_tail = mm(s["hn"], dpt, ta=True, name="a_in_tail_dw", out_dtype=BF16)
            g_in = jnp.concatenate([g_main, g_tail[:, :2 * N_HEADS_A]], axis=1)
            gw["a_w_in"] = g_in.reshape(D_MODEL, N_CHIPS, g_in.shape[1] // N_CHIPS).transpose(1, 0, 2)
        else:
            dao = mm(dh, wt["b_w_out"], tb=True, name="b_out_dx")
            gw["b_w_out"] = mm(s["ao"], dh, ta=True, name="b_out_dw", out_dtype=BF16).reshape(N_CHIPS, SHARD_PROJ, D_MODEL)
            dq, dkd, dvd, gsk = swa_bwd(dao, s["pb"], s["kd"], s["vd"], sinks_p, name="b_att_bwd")
            gs["b_sinks"] = gsk[:, :N_HEADS_B]
            dpb = jnp.concatenate([dq, _undup(dkd), _undup(dvd)], axis=1)
            dhn = mm(dpb, wt["b_w_in"], tb=True, name="b_in_dx", n=D_MODEL, tn=D_MODEL, tk=SHARD_BIN,
                     b_spec=spec((None, D_MODEL, SHARD_BIN), lambda r, j, kk: (kk, j, 0)))
            gw["b_w_in"] = mm(s["hn"], dpb, ta=True, name="b_in_dw", out_dtype=BF16, tn=SHARD_BIN,
                              o_shape=(N_CHIPS, D_MODEL, SHARD_BIN), o_spec=spec((None, mtm, SHARD_BIN), lambda r, j, kk: (j, r, 0)))
        dh, g_mix[i] = norm_bwd(s["h0"], nw("norm_mix", i), dhn, dh, f"norm_mix_bwd{i}")

    gs["norm_mix"], gs["norm_ffn"], gs["norm_ple"] = (jnp.concatenate(g, axis=0) for g in (g_mix, g_ffn, g_ple))
    gs["f_conv"] = jnp.stack(g_conv)
    gw["f_w_up"] = g_up.reshape(N_CHIPS, DEPTH * D_MODEL, SHARD_UP)
    gw["f_w_down"] = jnp.stack([g.reshape(N_CHIPS, D_FF // N_CHIPS, D_MODEL) for g in g_down], axis=1).reshape(N_CHIPS, -1, D_MODEL)
    gw["ple_w_proj"] = g_proj.reshape(N_CHIPS, DEPTH * PLE_DIM, SHARD_PROJ)
    gw["ple_w_gate"] = g_gate.reshape(N_CHIPS, DEPTH * SHARD_PROJ, D_MODEL)
    return loss, dh, gw, gs


BIG = [("a_w_in", False), ("a_w_out", False), ("b_w_in", False), ("b_w_out", False), ("f_w_up", False), ("f_w_down", True),
       ("ple_w_proj", False), ("ple_w_gate", True)]
CONVS = ["a_conv", "f_conv"]
SMALL = ["norm_mix", "norm_ffn", "norm_ple", "norm_final", "a_log", "a_dt_bias", "a_norm", "b_sinks"]
SMALL_ROWS = 8
CONV_ROWS = 16
CONV_GRAD_ROWS = 48


def _pack_rows(arrs, rows, dtype):
    flat = jnp.concatenate([a.reshape(-1).astype(dtype) for a in arrs])
    return jnp.pad(flat, (0, rows * PACK_COLS - flat.shape[0])).reshape(rows, PACK_COLS)


def _unpack(flat, shapes):
    out, off = [], 0
    for shp in shapes:
        n = math.prod(shp)
        out.append(flat[off:off + n].reshape(shp))
        off += n
    return out


def _pack_small(d, loss=None):
    tail = jnp.concatenate([d["a_log"].reshape(-1), d["a_dt_bias"].reshape(-1), d["a_norm"].reshape(-1), d["b_sinks"].reshape(-1)])
    if loss is not None:
        tail = jnp.concatenate([tail, loss.reshape(-1)[:1]])
    tail = jnp.pad(tail, (0, PACK_COLS - tail.shape[0]))
    return jnp.concatenate([d["norm_mix"], d["norm_ffn"], d["norm_ple"], d["norm_final"][None, :], tail[None, :]], axis=0)


def _unpack_small(a, like):
    out = {"norm_mix": a[0:2], "norm_ffn": a[2:4], "norm_ple": a[4:6], "norm_final": a[6]}
    off = 0
    for nm in ("a_log", "a_dt_bias", "a_norm", "b_sinks"):
        n = like[nm].size
        out[nm] = a[7, off:off + n].reshape(like[nm].shape)
        off += n
    return out, a[7, off]


def _as2d(a):
    return a.reshape(-1, a.shape[-1])


def kernel(x, p, norm_mix, norm_ffn, norm_ple, norm_final, a_w_in, a_conv, a_log, a_dt_bias, a_norm, a_w_out, b_w_in, b_sinks, b_w_out, f_w_up, f_conv, f_w_down, ple_w_proj, ple_w_gate, loss_target, m_norm_mix, m_norm_ffn, m_norm_ple, m_norm_final, m_a_w_in, m_a_conv, m_a_log, m_a_dt_bias, m_a_norm, m_a_w_out, m_b_w_in, m_b_sinks, m_b_w_out, m_f_w_up, m_f_conv, m_f_w_down, m_ple_w_proj, m_ple_w_gate, v_norm_mix, v_norm_ffn, v_norm_ple, v_norm_final, v_a_w_in, v_a_conv, v_a_log, v_a_dt_bias, v_a_norm, v_a_w_out, v_b_w_in, v_b_sinks, v_b_w_out, v_f_w_up, v_f_conv, v_f_w_down, v_ple_w_proj, v_ple_w_gate):
    w = dict(norm_mix=norm_mix, norm_ffn=norm_ffn, norm_ple=norm_ple, norm_final=norm_final, a_w_in=a_w_in, a_conv=a_conv,
             a_log=a_log, a_dt_bias=a_dt_bias, a_norm=a_norm, a_w_out=a_w_out, b_w_in=b_w_in, b_sinks=b_sinks, b_w_out=b_w_out,
             f_w_up=f_w_up, f_conv=f_conv, f_w_down=f_w_down, ple_w_proj=ple_w_proj, ple_w_gate=ple_w_gate)
    m = dict(norm_mix=m_norm_mix, norm_ffn=m_norm_ffn, norm_ple=m_norm_ple, norm_final=m_norm_final, a_w_in=m_a_w_in,
             a_conv=m_a_conv, a_log=m_a_log, a_dt_bias=m_a_dt_bias, a_norm=m_a_norm, a_w_out=m_a_w_out, b_w_in=m_b_w_in,
             b_sinks=m_b_sinks, b_w_out=m_b_w_out, f_w_up=m_f_w_up, f_conv=m_f_conv, f_w_down=m_f_w_down,
             ple_w_proj=m_ple_w_proj, ple_w_gate=m_ple_w_gate)
    v = dict(norm_mix=v_norm_mix, norm_ffn=v_norm_ffn, norm_ple=v_norm_ple, norm_final=v_norm_final, a_w_in=v_a_w_in,
             a_conv=v_a_conv, a_log=v_a_log, a_dt_bias=v_a_dt_bias, a_norm=v_a_norm, a_w_out=v_a_w_out, b_w_in=v_b_w_in,
             b_sinks=v_b_sinks, b_w_out=v_b_w_out, f_w_up=v_f_w_up, f_conv=v_f_conv, f_w_down=v_f_w_down,
             ple_w_proj=v_ple_w_proj, ple_w_gate=v_ple_w_gate)
    xc, yc, cc = _place()
    my_chip = 2 * xc + yc

    own = [(_as2d(w[n]).astype(BF16), lm) for n, lm in BIG]
    gathered = {}
    for (n, lm), (arr, _), g in zip(BIG, own, gather_units(own, name="gather_weights")):
        mine = arr.reshape(2, arr.shape[0] // 2, arr.shape[1]) if lm else arr
        gathered[n] = lax.dynamic_update_index_in_dim(g, mine, my_chip, 1 if lm else 0)
    a_in = jnp.concatenate([gathered["a_w_in"][j] for j in range(N_CHIPS)], axis=1)
    n_main = 4 * N_HEADS_A * HEAD_DIM_A
    conv_shapes = [w[n].shape for n in CONVS]
    convs = allgather8(_pack_rows([w[n] for n in CONVS], CONV_ROWS, F32), name="gather_convs")
    conv_parts = [_unpack(convs[2 * j].reshape(-1), conv_shapes) for j in range(N_CHIPS)]
    a_conv_full, f_conv_full = (jnp.concatenate([conv_parts[j][q] for j in range(N_CHIPS)], axis=2) for q in range(2))
    wt = {"a_main": a_in[:, :n_main], "a_tail": jnp.pad(a_in[:, n_main:], ((0, 0), (0, LANES - 2 * N_HEADS_A))),
          "a_w_out": gathered["a_w_out"].reshape(D_MODEL, D_MODEL), "b_w_out": gathered["b_w_out"].reshape(D_MODEL, D_MODEL),
          "b_w_in": gathered["b_w_in"], "f_w_up": gathered["f_w_up"].reshape(N_CHIPS, DEPTH, D_MODEL, SHARD_UP),
          "f_w_down": gathered["f_w_down"].reshape(DEPTH, D_FF, D_MODEL),
          "ple_w_proj": gathered["ple_w_proj"].reshape(N_CHIPS, DEPTH, PLE_DIM, SHARD_PROJ),
          "ple_w_gate": gathered["ple_w_gate"].reshape(DEPTH, D_MODEL, D_MODEL),
          "a_conv": a_conv_full[0], "f_conv": f_conv_full}
    sm = {n: w[n] for n in SMALL}

    loss, grad_x, gw, gs = local_step(x[0], p[:, 0], loss_target[0], sm, wt)

    units = [gw[n] for n, _ in BIG]
    from_sib = swap_units(units, name="rs_swap")
    pairs = []
    for (n, _), g, sib in zip(BIG, units, from_sib):
        rh, cols = g.shape[1] // 2, g.shape[2]
        mine = lax.dynamic_slice_in_dim(g, cc * rh, rh, axis=1)
        pairs.append(add_parts([mine.reshape(N_CHIPS * rh, cols), sib.reshape(N_CHIPS * rh, cols)], out_dtype=BF16,
                               name=f"rs_add_pair_{n}").reshape(N_CHIPS, rh, cols))
    others = scatter_units(pairs, name="rs_scatter")
    halves = [add_parts([lax.dynamic_index_in_dim(pr, my_chip, axis=0, keepdims=False), o[0], o[1], o[2]], out_dtype=F32,
                        name=f"rs_add_chips_{n}") for (n, _), pr, o in zip(BIG, pairs, others)]
    from_sib = join_units(halves, name="rs_join")
    g_sh = {n: jnp.where(cc == 0, jnp.concatenate([hf, ot]), jnp.concatenate([ot, hf])).reshape(w[n].shape)
            for (n, _), hf, ot in zip(BIG, halves, from_sib)}

    conv_grads = _pack_rows([gs[n] for n in CONVS], CONV_GRAD_ROWS, F32)
    small_sum = sum8(allgather8(jnp.concatenate([_pack_small(gs, loss), conv_grads]), name="gather_small"), name="sum_small")
    g_sm, loss_sum = _unpack_small(small_sum[:SMALL_ROWS], sm)
    for n, full in zip(CONVS, _unpack(small_sum[SMALL_ROWS:].reshape(-1), [gs[n].shape for n in CONVS])):
        g_sh[n] = lax.dynamic_slice_in_dim(full, my_chip * w[n].shape[-1], w[n].shape[-1], axis=full.ndim - 1).reshape(w[n].shape)

    grads, delta, new_m, new_v = {}, {}, {}, {}
    for n in [n for n, _ in BIG] + CONVS:
        d2, m2, v2 = adamw(_as2d(w[n]), _as2d(g_sh[n]), _as2d(m[n]), _as2d(v[n]), name=f"adamw_{n}")
        grads[n], delta[n], new_m[n], new_v[n] = g_sh[n], d2.reshape(w[n].shape), m2.reshape(w[n].shape), v2.reshape(w[n].shape)
    pk = lambda d: _pack_small(d)
    d2, m2, v2 = adamw(pk(sm), pk(g_sm), pk({n: m[n] for n in SMALL}), pk({n: v[n] for n in SMALL}), name="adamw_small")
    for src, dst in ((d2, delta), (m2, new_m), (v2, new_v)):
        dst.update(_unpack_small(src, sm)[0])
    grads.update(g_sm)

    order = ["norm_mix", "norm_ffn", "norm_ple", "norm_final", "a_w_in", "a_conv", "a_log", "a_dt_bias", "a_norm", "a_w_out",
             "b_w_in", "b_sinks", "b_w_out", "f_w_up", "f_conv", "f_w_down", "ple_w_proj", "ple_w_gate"]
    return (loss_sum, grad_x[None], *[grads[n] for n in order], *[delta[n] for n in order],
            *[new_m[n] for n in order], *[new_v[n] for n in order])
```

```python
import functools
import math

import jax
import jax.numpy as jnp
from jax import lax
from jax.experimental import pallas as pl
from jax.experimental.pallas import tpu as pltpu

F32 = jnp.float32
BF16 = jnp.bfloat16
MESH = pl.DeviceIdType.MESH

D_MODEL = 1024
N_HEADS_A = 8
HEAD_DIM_A = 128
CONV_A = 4
N_HEADS_B = 16
N_KV_B = 4
HEAD_DIM_B = 64
WINDOW = 128
D_FF = 2816
FFN_CONV = 3
PLE_DIM = 256
EPS = 1e-6
DEPTH = 2

ADAM_LR = 0.001
ADAM_B1 = 0.9
ADAM_B2 = 0.999
ADAM_EPS = 1e-08
ADAM_WD = 0.01
ADAM_STEP = 10

LANES = 128
SUBLANES = 8
BF16_ROWS = 16
CHUNK = 128
VMEM_LIMIT = 56 * 1024 * 1024
NEG = -1e30
N_CHIPS = 4
N_DEV = 8
PACK_COLS = 1024


def _params(sem=None):
    return pltpu.CompilerParams(dimension_semantics=sem, vmem_limit_bytes=VMEM_LIMIT)


def _tile(dim, cap):
    if dim % LANES:
        return dim
    best = LANES
    for t in range(LANES, min(dim, cap) + 1, LANES):
        if dim % t == 0:
            best = t
    return best


def _dot(a, b, dims=(((1,), (0,)), ((), ())), precision=None):
    return lax.dot_general(a, b, dims, precision=precision, preferred_element_type=F32)


NN = (((1,), (0,)), ((), ()))
NT = (((1,), (1,)), ((), ()))
TN = (((0,), (0,)), ((), ()))


MM_TM_CAP = 1024


def mm(a, b, *, name, ta=False, tb=False, out_dtype=F32, add=None, norm_w=None, tm_cap=MM_TM_CAP, tn_cap=1408, tk_cap=1408,
       n=None, tn=None, tk=None, b_spec=None, o_spec=None, o_shape=None, into=None):
    m, k = (a.shape[1], a.shape[0]) if ta else a.shape
    if b_spec is None:
        n = b.shape[0] if tb else b.shape[1]
        assert (b.shape[1] if tb else b.shape[0]) == k, (a.shape, b.shape, ta, tb)
    tm, tn, tk = _tile(m, tm_cap), tn or _tile(n, tn_cap), tk or _tile(k, tk_cap)
    assert n % tn == 0 and k % tk == 0, (n, tn, k, tk)
    nk = k // tk
    dims = (((0 if ta else 1,), (1 if tb else 0,)), ((), ()))
    has_add, has_norm = add is not None, norm_w is not None
    assert not has_norm or (tn == n and o_spec is None), "the norm epilogue needs whole rows"
    n_in = 2 + has_add + has_norm + (into is not None)

    def body(*refs):
        a_ref, b_ref = refs[0], refs[1]
        add_ref = refs[2] if has_add else None
        o_ref = refs[n_in]
        part = _dot(a_ref[...].astype(BF16), b_ref[...].astype(BF16), dims)

        def finish(r):
            if has_add:
                r = r + add_ref[...].astype(F32)
            o_ref[...] = r.astype(o_ref.dtype)
            if has_norm:
                refs[n_in + 1][...] = _f_norm(r, refs[2 + has_add][...]).astype(BF16)

        if nk == 1:
            finish(part)
            return
        acc = refs[-1]
        kk = pl.program_id(2)

        @pl.when(kk == 0)
        def _():
            acc[...] = part

        @pl.when(kk > 0)
        def _():
            acc[...] += part

        @pl.when(kk == nk - 1)
        def _():
            finish(acc[...])

    a_spec = pl.BlockSpec((tk, tm), lambda i, j, kk: (kk, i)) if ta else pl.BlockSpec((tm, tk), lambda i, j, kk: (i, kk))
    if b_spec is None:
        b_spec = pl.BlockSpec((tn, tk), lambda i, j, kk: (j, kk)) if tb else pl.BlockSpec((tk, tn), lambda i, j, kk: (kk, j))
    plain_o = pl.BlockSpec((tm, tn), lambda i, j, kk: (i, j))
    if o_spec is None:
        o_spec, o_shape = plain_o, (m, n)
    in_specs = [a_spec, b_spec] + ([plain_o] if has_add else [])
    args = (a, b) + ((add,) if has_add else ())
    out_specs, out_shapes = o_spec, jax.ShapeDtypeStruct(tuple(o_shape), out_dtype)
    if has_norm:
        in_specs.append(pl.BlockSpec((1, n), lambda i, j, kk: (0, 0)))
        args += (norm_w,)
        out_specs, out_shapes = [o_spec, plain_o], [out_shapes, jax.ShapeDtypeStruct((m, n), BF16)]
    aliases = {}
    if into is not None:
        assert into.shape == tuple(o_shape) and into.dtype == out_dtype, (into.shape, o_shape)
        in_specs.append(pl.BlockSpec(memory_space=pl.ANY))
        args += (into,)
        aliases = {n_in - 1: 0}
    return pl.pallas_call(
        body, grid=(m // tm, n // tn, nk), in_specs=in_specs, out_specs=out_specs,
        out_shape=out_shapes, name=name, input_output_aliases=aliases,
        scratch_shapes=[pltpu.VMEM((tm, tn), F32)] if nk > 1 else [],
        compiler_params=_params(("parallel", "parallel", "arbitrary")),
    )(*args)


def _row_spec(tm, cw, coff):
    return pl.BlockSpec((tm, cw), lambda i, j: (i, j + coff))


def _full_spec(shape):
    return pl.BlockSpec(shape, lambda i, j: (0,) * len(shape))


def tile_map(fn, rows, params, outs, *, tm, ncol, name):
    t = rows[0][0].shape[0]
    nin = len(rows) + len(params)

    def body(*refs):
        res = fn(*[r[...] for r in refs[:nin]])
        res = res if isinstance(res, (tuple, list)) else (res,)
        for o_ref, r in zip(refs[nin:], res):
            o_ref[...] = r.astype(o_ref.dtype)

    in_specs = [_row_spec(tm, cw, coff) for (_, cw, coff) in rows] + [_full_spec(p.shape) for p in params]
    res = pl.pallas_call(
        body, grid=(t // tm, ncol), in_specs=in_specs,
        out_specs=[_row_spec(tm, cw, 0) for (cw, _) in outs],
        out_shape=[jax.ShapeDtypeStruct((t, cw * ncol), dt) for (cw, dt) in outs], name=name,
        compiler_params=_params(("parallel", "parallel")),
    )(*[r[0] for r in rows], *params)
    return res


def tile_vjp(fn, rows, params, cts, *, n_diff, tm, ncol, name, add=None, grad_dtypes=None):
    t = rows[0][0].shape[0]
    nr, npar, nct = len(rows), len(params), len(cts)
    has_add = add is not None

    def body(*refs):
        vals = [r[...] for r in refs[:nr + npar + nct + (1 if has_add else 0)]]
        diff, rest, pars = vals[:n_diff], vals[n_diff:nr], vals[nr:nr + npar]
        ctv = vals[nr + npar:nr + npar + nct]
        outs_ref = refs[nr + npar + nct + (1 if has_add else 0):]

        def f(*a):
            res = fn(*a[:n_diff], *rest, *a[n_diff:])
            return tuple(res) if isinstance(res, (tuple, list)) else (res,)

        primal, vjp = jax.vjp(f, *[d.astype(F32) for d in diff], *pars)
        grads = vjp(tuple(c.astype(o.dtype) for c, o in zip(ctv, primal)))
        for q in range(n_diff):
            g = grads[q]
            if has_add and q == 0:
                g = g + vals[-1]
            outs_ref[q][...] = g.astype(outs_ref[q].dtype)
        first = (pl.program_id(0) == 0) & (pl.program_id(1) == 0)
        for q in range(npar):
            o_ref, g = outs_ref[n_diff + q], grads[n_diff + q]

            @pl.when(first)
            def _(o_ref=o_ref, g=g):
                o_ref[...] = g

            @pl.when(jnp.logical_not(first))
            def _(o_ref=o_ref, g=g):
                o_ref[...] += g

    ins = list(rows) + [None] * 0
    in_specs = [_row_spec(tm, cw, coff) for (_, cw, coff) in rows] + [_full_spec(p.shape) for p in params]
    in_specs += [_row_spec(tm, cw, coff) for (_, cw, coff) in cts]
    args = [r[0] for r in rows] + list(params) + [c[0] for c in cts]
    if has_add:
        in_specs.append(_row_spec(tm, add[1], add[2]))
        args.append(add[0])
    out_specs = [_row_spec(tm, rows[q][1], 0) for q in range(n_diff)] + [_full_spec(p.shape) for p in params]
    grad_dtypes = grad_dtypes or [F32] * n_diff
    out_shape = [jax.ShapeDtypeStruct((t, rows[q][1] * ncol), grad_dtypes[q]) for q in range(n_diff)]
    out_shape += [jax.ShapeDtypeStruct(p.shape, F32) for p in params]
    del ins
    res = pl.pallas_call(
        body, grid=(t // tm, ncol), in_specs=in_specs, out_specs=out_specs, out_shape=out_shape, name=name,
        compiler_params=_params(("arbitrary", "arbitrary")),
    )(*args)
    return res[:n_diff], res[n_diff:]


def _silu(x):
    return x * jax.nn.sigmoid(x)


def _f_norm(h, w):
    return h * lax.rsqrt(jnp.mean(h * h, axis=-1, keepdims=True) + EPS) * w


def _f_gnorm(o, z, w):
    return _f_norm(o, w) * _silu(z)


def _f_act(gate, val):
    return _silu(gate) * val


def _f_ple(gl, pe):
    return jax.nn.sigmoid(gl) * pe


def _f_betag(pt, alog, dtb):
    lane = lax.broadcasted_iota(jnp.int32, (1, LANES), 1)
    z = pt + dtb
    softplus = jnp.maximum(z, 0.0) + jnp.log(1.0 + jnp.exp(-jnp.abs(z)))
    g = -jnp.exp(alog) * softplus
    return jnp.where(lane < N_HEADS_A, jax.nn.sigmoid(pt), jnp.where(lane < 2 * N_HEADS_A, g, 0.0))


CONV_TM = 256
CONV_CW = 1024


def _shift_down(x, prev, s, row):
    rp = jnp.tile(pltpu.roll(prev, s, 0), (x.shape[0] // SUBLANES, 1))
    return jnp.where(row < s, rp, pltpu.roll(x, s, 0))


def _shift_up(x, nxt, s, row):
    tm = x.shape[0]
    rn = jnp.tile(pltpu.roll(nxt, SUBLANES - s, 0), (tm // SUBLANES, 1))
    return jnp.where(row >= tm - s, rn, pltpu.roll(x, tm - s, 0))


def _conv_taps(x, prev, w_ref, cols, row):
    k = w_ref.shape[0]
    y = x * w_ref[pl.ds(k - 1, 1), cols]
    for s in range(1, k):
        y = y + _shift_down(x, prev, s, row) * w_ref[pl.ds(k - 1 - s, 1), cols]
    return y


def _lane_chunks(cw):
    return [slice(cb * LANES, (cb + 1) * LANES) for cb in range(cw // LANES)]


def conv_fwd(x, w, *, name):
    t = x.shape[0]
    k, c = w.shape
    tm, cw = min(CONV_TM, t), CONV_CW
    nb8 = tm // SUBLANES

    def body(x_ref, p_ref, w_ref, o_ref):
        first = pl.program_id(1) == 0
        row = lax.broadcasted_iota(jnp.int32, (tm, LANES), 0)
        for cols in _lane_chunks(cw):
            o_ref[:, cols] = _conv_taps(x_ref[:, cols], jnp.where(first, 0.0, p_ref[:, cols]), w_ref, cols, row)

    return pl.pallas_call(
        body, grid=(c // cw, t // tm),
        in_specs=[pl.BlockSpec((tm, cw), lambda j, i: (i, j)),
                  pl.BlockSpec((SUBLANES, cw), lambda j, i: (jnp.maximum(i * nb8 - 1, 0), j)),
                  pl.BlockSpec((k, cw), lambda j, i: (0, j))],
        out_specs=pl.BlockSpec((tm, cw), lambda j, i: (i, j)),
        out_shape=jax.ShapeDtypeStruct((t, c), F32), name=name,
        compiler_params=_params(("parallel", "parallel")),
    )(x, x, w)


def conv_bwd(dy, x, w, *, name):
    t = x.shape[0]
    k, c = w.shape
    tm, cw = min(CONV_TM, t), CONV_CW
    nb8 = tm // SUBLANES
    ni = t // tm

    def body(dy_ref, dn_ref, x_ref, p_ref, w_ref, dx_ref, dw_ref):
        i = pl.program_id(1)
        first, last = i == 0, i == ni - 1
        row = lax.broadcasted_iota(jnp.int32, (tm, LANES), 0)
        for cols in _lane_chunks(cw):
            dyv, xv = dy_ref[:, cols], x_ref[:, cols]
            nxt = jnp.where(last, 0.0, dn_ref[:, cols])
            prev = jnp.where(first, 0.0, p_ref[:, cols])
            dx = dyv * w_ref[pl.ds(k - 1, 1), cols]
            dws = [jnp.sum(dyv * xv, axis=0, keepdims=True)]
            for s in range(1, k):
                dx = dx + _shift_up(dyv, nxt, s, row) * w_ref[pl.ds(k - 1 - s, 1), cols]
                dws.append(jnp.sum(dyv * _shift_down(xv, prev, s, row), axis=0, keepdims=True))
            dx_ref[:, cols] = dx.astype(dx_ref.dtype)
            for s in range(k):
                @pl.when(first)
                def _(s=s, dws=dws, cols=cols):
                    dw_ref[pl.ds(k - 1 - s, 1), cols] = dws[s]

                @pl.when(jnp.logical_not(first))
                def _(s=s, dws=dws, cols=cols):
                    dw_ref[pl.ds(k - 1 - s, 1), cols] += dws[s]

    return pl.pallas_call(
        body, grid=(c // cw, ni),
        in_specs=[pl.BlockSpec((tm, cw), lambda j, i: (i, j)),
                  pl.BlockSpec((SUBLANES, cw), lambda j, i: (jnp.minimum((i + 1) * nb8, t // SUBLANES - 1), j)),
                  pl.BlockSpec((tm, cw), lambda j, i: (i, j)),
                  pl.BlockSpec((SUBLANES, cw), lambda j, i: (jnp.maximum(i * nb8 - 1, 0), j)),
                  pl.BlockSpec((k, cw), lambda j, i: (0, j))],
        out_specs=[pl.BlockSpec((tm, cw), lambda j, i: (i, j)), pl.BlockSpec((k, cw), lambda j, i: (0, j))],
        out_shape=[jax.ShapeDtypeStruct((t, c), BF16), jax.ShapeDtypeStruct((k, c), F32)], name=name,
        compiler_params=_params(("parallel", "arbitrary")),
    )(dy, dy, x, x, w)


FFN_TM = 128
FFN_CW = D_FF // 2


def _ffn_specs(t, tm, cw, k):
    nb8, ncol = tm // SUBLANES, D_FF // cw
    cur = lambda off: pl.BlockSpec((tm, cw), lambda j, i: (i, j + off))
    prev = lambda off: pl.BlockSpec((SUBLANES, cw), lambda j, i: (jnp.maximum(i * nb8 - 1, 0), j + off))
    nxt = lambda off: pl.BlockSpec((SUBLANES, cw), lambda j, i: (jnp.minimum((i + 1) * nb8, t // SUBLANES - 1), j + off))
    taps = lambda off: pl.BlockSpec((k, cw), lambda j, i: (0, j + off))
    return cur, prev, nxt, taps, ncol


def conv_act_fwd(u, w, *, name):
    t, k = u.shape[0], w.shape[0]
    tm, cw = min(FFN_TM, t), FFN_CW
    cur, prev, _, taps, ncol = _ffn_specs(t, tm, cw, k)

    def body(ug_ref, pg_ref, uv_ref, pv_ref, wg_ref, wv_ref, o_ref):
        first = pl.program_id(1) == 0
        row = lax.broadcasted_iota(jnp.int32, (tm, LANES), 0)
        for cb in range(cw // LANES):
            cols = slice(cb * LANES, (cb + 1) * LANES)
            cg = _conv_taps(ug_ref[:, cols], jnp.where(first, 0.0, pg_ref[:, cols]), wg_ref, cols, row)
            cv = _conv_taps(uv_ref[:, cols], jnp.where(first, 0.0, pv_ref[:, cols]), wv_ref, cols, row)
            o_ref[:, cols] = _f_act(cg, cv).astype(o_ref.dtype)

    return pl.pallas_call(
        body, grid=(ncol, t // tm),
        in_specs=[cur(0), prev(0), cur(ncol), prev(ncol), taps(0), taps(ncol)],
        out_specs=cur(0), out_shape=jax.ShapeDtypeStruct((t, D_FF), BF16), name=name,
        compiler_params=_params(("parallel", "parallel")),
    )(u, u, u, u, w, w)


def conv_act_bwd(u, dact, w, *, name):
    t, k = u.shape[0], w.shape[0]
    tm, cw = min(FFN_TM, t), FFN_CW
    cur, prev, nxt, taps, ncol = _ffn_specs(t, tm, cw, k)
    ni = t // tm

    def body(ug_ref, pg_ref, ng_ref, uv_ref, pv_ref, nv_ref, d_ref, dn_ref, wg_ref, wv_ref, dg_ref, dv_ref, dwg_ref, dwv_ref):
        i = pl.program_id(1)
        first, last = i == 0, i == ni - 1
        row = lax.broadcasted_iota(jnp.int32, (tm, LANES), 0)
        row8 = lax.broadcasted_iota(jnp.int32, (SUBLANES, LANES), 0)
        for cb in range(cw // LANES):
            cols = slice(cb * LANES, (cb + 1) * LANES)
            ug, uv = ug_ref[:, cols], uv_ref[:, cols]
            pg, pv = jnp.where(first, 0.0, pg_ref[:, cols]), jnp.where(first, 0.0, pv_ref[:, cols])
            sg = [ug] + [_shift_down(ug, pg, s, row) for s in range(1, k)]
            sv = [uv] + [_shift_down(uv, pv, s, row) for s in range(1, k)]
            taps = lambda xs, w_ref: sum(xs[s] * w_ref[pl.ds(k - 1 - s, 1), cols] for s in range(k))
            _, vjp = jax.vjp(_f_act, taps(sg, wg_ref), taps(sv, wv_ref))
            dcg, dcv = vjp(d_ref[:, cols])
            _, vjp_n = jax.vjp(_f_act, _conv_taps(ng_ref[:, cols], ug[tm - SUBLANES:], wg_ref, cols, row8),
                               _conv_taps(nv_ref[:, cols], uv[tm - SUBLANES:], wv_ref, cols, row8))
            dcgn, dcvn = vjp_n(jnp.where(last, 0.0, dn_ref[:, cols]))
            for dc, dcn, xs, w_ref, dx_ref, dw_ref in ((dcg, dcgn, sg, wg_ref, dg_ref, dwg_ref),
                                                       (dcv, dcvn, sv, wv_ref, dv_ref, dwv_ref)):
                dx = dc * w_ref[pl.ds(k - 1, 1), cols]
                dws = [jnp.sum(dc * xs[0], axis=0, keepdims=True)]
                for s in range(1, k):
                    dx = dx + _shift_up(dc, dcn, s, row) * w_ref[pl.ds(k - 1 - s, 1), cols]
                    dws.append(jnp.sum(dc * xs[s], axis=0, keepdims=True))
                dx_ref[:, cols] = dx.astype(dx_ref.dtype)
                for s in range(k):
                    @pl.when(first)
                    def _(s=s, dw_ref=dw_ref, dws=dws):
                        dw_ref[pl.ds(k - 1 - s, 1), cols] = dws[s]

                    @pl.when(jnp.logical_not(first))
                    def _(s=s, dw_ref=dw_ref, dws=dws):
                        dw_ref[pl.ds(k - 1 - s, 1), cols] += dws[s]

    half = jax.ShapeDtypeStruct((t, D_FF), BF16)
    dwh = jax.ShapeDtypeStruct((k, D_FF), F32)
    return pl.pallas_call(
        body, grid=(ncol, ni),
        in_specs=[cur(0), prev(0), nxt(0), cur(ncol), prev(ncol), nxt(ncol), cur(0), nxt(0), taps(0), taps(ncol)],
        out_specs=[cur(0), cur(0), taps(0), taps(0)], out_shape=[half, half, dwh, dwh], name=name,
        compiler_params=_params(("parallel", "arbitrary")),
    )(u, u, u, u, u, u, dact, dact, w, w)


def _each(f, *lists):
    return [f(*a) for a in zip(*lists)]


@jax.custom_vjp
def _inv_unit_lower(lms):
    return _inv_blocks(lms)


def _inv_blocks(lms):
    c = lms[0].shape[0]
    ri = lax.broadcasted_iota(jnp.int32, (c, c), 0)
    ci = lax.broadcasted_iota(jnp.int32, (c, c), 1)
    eye = (ri == ci).astype(F32)
    dms = _each(lambda lm: eye - jnp.where((ri >> 1) == (ci >> 1), lm, 0.0), lms)
    for lv in range(1, int(math.log2(c))):
        below = ((ri >> (lv + 1)) == (ci >> (lv + 1))) & ((ri >> lv) != (ci >> lv))
        dbs = _each(lambda dm: dm.astype(BF16), dms)
        ods = _each(lambda lm, db: _dot(jnp.where(below, lm, 0.0).astype(BF16), db).astype(BF16), lms, dbs)
        dms = _each(lambda dm, db, od: dm - _dot(db, od), dms, dbs, ods)
    return dms


def _inv_fwd(lms):
    tms = _inv_blocks(lms)
    return tms, tms


def _inv_bwd(tms, dts):
    tbs = _each(lambda tm: tm.astype(BF16), tms)
    mid = _each(lambda tb, dt: _dot(tb, dt.astype(BF16), TN).astype(BF16), tbs, dts)
    return (_each(lambda m, tb: -_dot(m, tb, NT), mid, tbs),)


_inv_unit_lower.defvjp(_inv_fwd, _inv_bwd)


def _l2n(x):
    return x * lax.rsqrt(jnp.sum(x * x, axis=-1, keepdims=True) + EPS)


def _prep_fn(cqs, cks, cvs, bg, sel_b, sel_g):
    c = cqs[0].shape[0]
    ri = lax.broadcasted_iota(jnp.int32, (c, c), 0)
    ci = lax.broadcasted_iota(jnp.int32, (c, c), 1)
    eye = (ri == ci).astype(F32)
    incl, strict = ci <= ri, ci < ri
    last = lax.broadcasted_iota(jnp.int32, (c, 1), 0) == c - 1
    to_row = lambda col: jnp.sum(col * eye, axis=0, keepdims=True)
    qs = _each(lambda a: _l2n(_silu(a)) * (HEAD_DIM_A ** -0.5), cqs)
    ks = _each(lambda a: _l2n(_silu(a)), cks)
    vbs = _each(lambda a: _silu(a).astype(BF16), cvs)
    betas = _each(lambda m: jnp.sum(bg * m, axis=1, keepdims=True), sel_b)
    gs = _each(lambda m: jnp.sum(bg * m, axis=1, keepdims=True), sel_g)
    gcss = _each(lambda g: jnp.sum(jnp.where(incl, to_row(g), 0.0), axis=1, keepdims=True), gs)
    gtots = _each(lambda gcs: jnp.sum(jnp.where(last, gcs, 0.0), axis=0, keepdims=True), gcss)
    decays = _each(lambda gcs: jnp.exp(jnp.where(incl, gcs - to_row(gcs), NEG)), gcss)
    kbs = _each(lambda k: k.astype(BF16), ks)
    lms = _each(lambda beta, kb, dec: jnp.where(strict, beta * _dot(kb, kb, NT) * dec, 0.0), betas, kbs, decays)
    ams = _each(lambda tm, beta: (tm * to_row(beta)).astype(BF16), _inv_unit_lower(lms), betas)
    gams = _each(jnp.exp, gcss)
    u0s = _each(_dot, ams, vbs)
    wks = _each(lambda am, gam, k: _dot(am, (gam * k).astype(BF16)), ams, gams, ks)
    qks = _each(lambda q, kb, dec: _dot(q.astype(BF16), kb, NT) * dec, qs, kbs, decays)
    qds = _each(lambda q, gam: q * gam, qs, gams)
    kds = _each(lambda k, gtot, gcs: k * jnp.exp(gtot - gcs), ks, gtots, gcss)
    gls = _each(lambda gtot: jnp.exp(gtot) * jnp.ones((SUBLANES, LANES), F32), gtots)
    return u0s, wks, qds, kds, qks, gls


def _head_masks(h):
    lane = lax.broadcasted_iota(jnp.int32, (1, LANES), 1)
    return (lane == h).astype(F32), (lane == h + N_HEADS_A).astype(F32)


def _hsl(j):
    return slice(j * HEAD_DIM_A, (j + 1) * HEAD_DIM_A)


def gnorm_fwd(o, zsrc, w, *, name):
    t, width = o.shape
    tm = min(256, t)
    zoff = zsrc.shape[1] // width - 1

    def body(o_ref, z_ref, w_ref, out_ref):
        for h in range(N_HEADS_A):
            out_ref[:, _hsl(h)] = _f_gnorm(o_ref[:, _hsl(h)], z_ref[:, _hsl(h)], w_ref[...]).astype(out_ref.dtype)

    rows = pl.BlockSpec((tm, width), lambda i: (i, 0))
    return pl.pallas_call(
        body, grid=(t // tm,),
        in_specs=[rows, pl.BlockSpec((tm, width), lambda i: (i, zoff)), pl.BlockSpec(w.shape, lambda i: (0, 0))],
        out_specs=rows, out_shape=jax.ShapeDtypeStruct((t, width), BF16), name=name, compiler_params=_params(("parallel",)),
    )(o, zsrc, w)


def gnorm_bwd(o, zsrc, w, don, *, name):
    t, width = o.shape
    tm = min(256, t)
    zoff = zsrc.shape[1] // width - 1

    def body(o_ref, z_ref, w_ref, d_ref, do_ref, dz_ref, dw_ref):
        dw = jnp.zeros(w.shape, F32)
        for h in range(N_HEADS_A):
            _, vjp = jax.vjp(_f_gnorm, o_ref[:, _hsl(h)], z_ref[:, _hsl(h)], w_ref[...])
            do, dz, dwh = vjp(d_ref[:, _hsl(h)])
            do_ref[:, _hsl(h)] = do.astype(do_ref.dtype)
            dz_ref[:, _hsl(h)] = dz.astype(dz_ref.dtype)
            dw = dw + dwh
        first = pl.program_id(0) == 0

        @pl.when(first)
        def _():
            dw_ref[...] = dw

        @pl.when(jnp.logical_not(first))
        def _():
            dw_ref[...] += dw

    rows = pl.BlockSpec((tm, width), lambda i: (i, 0))
    wspec = pl.BlockSpec(w.shape, lambda i: (0, 0))
    return pl.pallas_call(
        body, grid=(t // tm,),
        in_specs=[rows, pl.BlockSpec((tm, width), lambda i: (i, zoff)), wspec, rows],
        out_specs=[rows, rows, wspec],
        out_shape=[jax.ShapeDtypeStruct((t, width), BF16)] * 2 + [jax.ShapeDtypeStruct(w.shape, F32)], name=name,
        compiler_params=_params(("arbitrary",)),
    )(o, zsrc, w, don)


def delta_prep(cqkv, bg, *, name):
    t = cqkv.shape[0]
    nh, hd, n = N_HEADS_A, HEAD_DIM_A, t // CHUNK

    def body(cq_ref, ck_ref, cv_ref, bg_ref, u0_ref, wk_ref, qd_ref, kd_ref, qk_ref, gl_ref):
        heads = range(nh)
        masks = [_head_masks(j) for j in heads]
        res = _prep_fn([cq_ref[:, _hsl(j)] for j in heads], [ck_ref[:, _hsl(j)] for j in heads],
                       [cv_ref[:, _hsl(j)] for j in heads], bg_ref[...], [m[0] for m in masks], [m[1] for m in masks])
        for o_ref, rs in zip((u0_ref, wk_ref, qd_ref, kd_ref, qk_ref), res[:5]):
            for j in heads:
                o_ref[:, _hsl(j)] = rs[j]
        for j in heads:
            gl_ref[j * SUBLANES:(j + 1) * SUBLANES, :] = res[5][j]

    blk = lambda off: pl.BlockSpec((CHUNK, nh * hd), lambda i: (i, off))
    return pl.pallas_call(
        body, grid=(n,),
        in_specs=[blk(0), blk(1), blk(2), pl.BlockSpec((CHUNK, LANES), lambda i: (i, 0))],
        out_specs=[blk(0)] * 5 + [pl.BlockSpec((nh * SUBLANES, LANES), lambda i: (i, 0))],
        out_shape=[jax.ShapeDtypeStruct((t, nh * hd), F32)] * 5 + [jax.ShapeDtypeStruct((n * nh * SUBLANES, LANES), F32)],
        name=name, compiler_params=_params(("parallel",)),
    )(cqkv, cqkv, cqkv, bg)


def delta_prep_bwd(cqkv, bg, cts, *, name):
    t = cqkv.shape[0]
    nh, hd, n = N_HEADS_A, HEAD_DIM_A, t // CHUNK

    def body(cq_ref, ck_ref, cv_ref, bg_ref, c0, c1, c2, c3, c4, c5, dc_ref, dbg_ref):
        heads = range(nh)
        masks = [_head_masks(j) for j in heads]
        _, vjp = jax.vjp(lambda a, b, c, d: _prep_fn(a, b, c, d, [m[0] for m in masks], [m[1] for m in masks]),
                         [cq_ref[:, _hsl(j)] for j in heads], [ck_ref[:, _hsl(j)] for j in heads],
                         [cv_ref[:, _hsl(j)] for j in heads], bg_ref[...])
        cts = tuple([c[:, _hsl(j)] for j in heads] for c in (c0, c1, c2, c3, c4))
        dqs, dks, dvs, dbg = vjp(cts + ([c5[j * SUBLANES:(j + 1) * SUBLANES, :] for j in heads],))
        for part, ds in enumerate((dqs, dks, dvs)):
            for j in heads:
                dc_ref[:, _hsl(part * nh + j)] = ds[j]
        dbg_ref[...] = dbg

    blk = lambda off: pl.BlockSpec((CHUNK, nh * hd), lambda i: (i, off))
    gl_spec = pl.BlockSpec((nh * SUBLANES, LANES), lambda i: (i, 0))
    bg_spec = pl.BlockSpec((CHUNK, LANES), lambda i: (i, 0))
    return pl.pallas_call(
        body, grid=(n,),
        in_specs=[blk(0), blk(1), blk(2), bg_spec] + [blk(0)] * 5 + [gl_spec],
        out_specs=[pl.BlockSpec((CHUNK, 3 * nh * hd), lambda i: (i, 0)), bg_spec],
        out_shape=[jax.ShapeDtypeStruct((t, 3 * nh * hd), F32), jax.ShapeDtypeStruct((t, LANES), F32)],
        name=name, compiler_params=_params(("parallel",)),
    )(cqkv, cqkv, cqkv, bg, *cts)


def delta_scan(u0, wk, qd, kd, qk, gl, *, name):
    t = u0.shape[0]
    nh, hd, n = N_HEADS_A, HEAD_DIM_A, t // CHUNK

    def body(u0_ref, wk_ref, qd_ref, kd_ref, qk_ref, gl_ref, o_ref, sin_ref, s_ref):
        @pl.when(pl.program_id(0) == 0)
        def _():
            s_ref[...] = jnp.zeros_like(s_ref)

        heads = list(range(nh))
        cols = lambda ref: [ref[:, _hsl(h)].astype(BF16) for h in heads]
        ss = [s_ref[h] for h in heads]
        for h in heads:
            sin_ref[h] = ss[h]
        sbs = _each(lambda s: s.astype(BF16), ss)
        ubs = _each(lambda h, wkb, sb: (u0_ref[:, _hsl(h)] - _dot(wkb, sb)).astype(BF16), heads, cols(wk_ref), sbs)
        os_ = _each(lambda qdb, sb, qkb, ub: _dot(qdb, sb) + _dot(qkb, ub), cols(qd_ref), sbs, cols(qk_ref), ubs)
        sn = _each(lambda h, s, kdb, ub: gl_ref[pl.ds(h * SUBLANES, 1), :] * s + _dot(kdb, ub, TN), heads, ss, cols(kd_ref), ubs)
        for h in heads:
            o_ref[:, _hsl(h)] = os_[h]
            s_ref[h] = sn[h]

    blk = pl.BlockSpec((CHUNK, nh * hd), lambda i: (i, 0))
    return pl.pallas_call(
        body, grid=(n,),
        in_specs=[blk] * 5 + [pl.BlockSpec((nh * SUBLANES, LANES), lambda i: (i, 0))],
        out_specs=[blk, pl.BlockSpec((None, nh, hd, hd), lambda i: (i, 0, 0, 0))],
        out_shape=[jax.ShapeDtypeStruct((t, nh * hd), F32), jax.ShapeDtypeStruct((n, nh, hd, hd), F32)],
        scratch_shapes=[pltpu.VMEM((nh, hd, hd), F32)], name=name,
        compiler_params=_params(("arbitrary",)),
    )(u0, wk, qd, kd, qk, gl)


def delta_scan_bwd(do, u0, wk, qd, kd, qk, gl, s_in, *, name):
    t = u0.shape[0]
    nh, hd, n = N_HEADS_A, HEAD_DIM_A, t // CHUNK

    def body(do_ref, u0_ref, wk_ref, qd_ref, kd_ref, qk_ref, gl_ref, sin_ref,
             du0_ref, dwk_ref, dqd_ref, dkd_ref, dqk_ref, dgl_ref, ds_ref):
        @pl.when(pl.program_id(0) == 0)
        def _():
            ds_ref[...] = jnp.zeros_like(ds_ref)

        corner = (lax.broadcasted_iota(jnp.int32, (SUBLANES, LANES), 0) == 0) & (lax.broadcasted_iota(jnp.int32, (SUBLANES, LANES), 1) == 0)
        heads = list(range(nh))
        cols = lambda ref: [ref[:, _hsl(h)].astype(BF16) for h in heads]
        ss, dss = [sin_ref[h] for h in heads], [ds_ref[h] for h in heads]
        sbs, dsbs = _each(lambda s: s.astype(BF16), ss), _each(lambda d: d.astype(BF16), dss)
        dobs, wkbs, qdbs, kdbs, qkbs = cols(do_ref), cols(wk_ref), cols(qd_ref), cols(kd_ref), cols(qk_ref)
        ubs = _each(lambda h, wkb, sb: (u0_ref[:, _hsl(h)] - _dot(wkb, sb)).astype(BF16), heads, wkbs, sbs)
        dus = _each(lambda qkb, dob, kdb, dsb: _dot(qkb, dob, TN) + _dot(kdb, dsb), qkbs, dobs, kdbs, dsbs)
        dubs = _each(lambda du: du.astype(BF16), dus)
        dwks = _each(lambda dub, sb: -_dot(dub, sb, NT), dubs, sbs)
        dqds = _each(lambda dob, sb: _dot(dob, sb, NT), dobs, sbs)
        dkds = _each(lambda ub, dsb: _dot(ub, dsb, NT), ubs, dsbs)
        dqks = _each(lambda dob, ub: _dot(dob, ub, NT), dobs, ubs)
        dgls = _each(lambda s, d: jnp.sum(jnp.sum(s * d, axis=1, keepdims=True), axis=0, keepdims=True), ss, dss)
        dsn = _each(lambda h, d, qdb, dob, wkb, dub: gl_ref[pl.ds(h * SUBLANES, 1), :] * d + _dot(qdb, dob, TN) - _dot(wkb, dub, TN),
                    heads, dss, qdbs, dobs, wkbs, dubs)
        for h in heads:
            du0_ref[:, _hsl(h)] = dus[h]
            dwk_ref[:, _hsl(h)] = dwks[h]
            dqd_ref[:, _hsl(h)] = dqds[h]
            dkd_ref[:, _hsl(h)] = dkds[h]
            dqk_ref[:, _hsl(h)] = dqks[h]
            dgl_ref[h * SUBLANES:(h + 1) * SUBLANES, :] = jnp.where(corner, dgls[h], 0.0)
            ds_ref[h] = dsn[h]

    blk = pl.BlockSpec((CHUNK, nh * hd), lambda i: (n - 1 - i, 0))
    gl_spec = pl.BlockSpec((nh * SUBLANES, LANES), lambda i: (n - 1 - i, 0))
    return pl.pallas_call(
        body, grid=(n,),
        in_specs=[blk] * 6 + [gl_spec, pl.BlockSpec((None, nh, hd, hd), lambda i: (n - 1 - i, 0, 0, 0))],
        out_specs=[blk] * 5 + [gl_spec],
        out_shape=[jax.ShapeDtypeStruct((t, nh * hd), F32)] * 5 + [jax.ShapeDtypeStruct((n * nh * SUBLANES, LANES), F32)],
        scratch_shapes=[pltpu.VMEM((nh, hd, hd), F32)], name=name,
        compiler_params=_params(("arbitrary",)),
    )(do, u0, wk, qd, kd, qk, gl, s_in)


N_PAIRS = N_HEADS_B // 2
PAIRS_PER_KV = N_PAIRS // N_KV_B


def _psl(j):
    return slice(j * LANES, (j + 1) * LANES)


def _att_fn(qps, kc, kp, vc, vp, sinks, kvf, first):
    w = WINDOW
    lane = lax.broadcasted_iota(jnp.int32, (1, LANES), 1)
    lo = (lane < HEAD_DIM_B).astype(F32)
    qi = lax.broadcasted_iota(jnp.int32, (w, w), 0)
    kj = lax.broadcasted_iota(jnp.int32, (w, w), 1)
    dist_c = (qi - kj).astype(F32)
    valid_c = kj <= qi
    valid_p = (kj > qi) & (first < 0.5)
    kcb, kpb, vcb, vpb = (a.astype(BF16) for a in (kc, kp, vc, vp))
    scale = HEAD_DIM_B ** -0.5
    heads = [(j, half) for j in range(PAIRS_PER_KV) for half in range(2)]
    hmasks = [lo if half == 0 else 1.0 - lo for _, half in heads]
    hds = [2.0 * (PAIRS_PER_KV * kvf + j) + half for j, half in heads]
    slopes = _each(lambda hd: jnp.exp(-(hd + 1.0) * (8.0 / N_HEADS_B * math.log(2.0))), hds)
    snks = _each(lambda hd: jnp.sum(sinks * (lane.astype(F32) == hd).astype(F32), axis=1, keepdims=True), hds)
    qhs = _each(lambda jh, hm: (qps[jh[0]] * hm).astype(BF16), heads, hmasks)
    lcs = _each(lambda qh, sl: jnp.where(valid_c, _dot(qh, kcb, NT) * scale - sl * dist_c, NEG), qhs, slopes)
    lps = _each(lambda qh, sl: jnp.where(valid_p, _dot(qh, kpb, NT) * scale - sl * (dist_c + w), NEG), qhs, slopes)
    ms = _each(lambda lc, lp, sk: lax.stop_gradient(jnp.maximum(jnp.maximum(jnp.max(lc, axis=1, keepdims=True),
                                                                            jnp.max(lp, axis=1, keepdims=True)), sk)), lcs, lps, snks)
    ecs = _each(lambda lc, m: jnp.exp(lc - m), lcs, ms)
    eps = _each(lambda lp, m: jnp.exp(lp - m), lps, ms)
    invs = _each(lambda ec, ep, sk, m: 1.0 / (jnp.sum(ec, axis=1, keepdims=True) + jnp.sum(ep, axis=1, keepdims=True) + jnp.exp(sk - m)),
                 ecs, eps, snks, ms)
    ohs = _each(lambda ec, ep, inv, hm: (_dot((ec * inv).astype(BF16), vcb) + _dot((ep * inv).astype(BF16), vpb)) * hm,
                ecs, eps, invs, hmasks)
    return [ohs[2 * j] + ohs[2 * j + 1] for j in range(PAIRS_PER_KV)]


def _scalar11(v):
    return jnp.full((1, 1), v, F32)


def swa_fwd(qsrc, kd, vd, sinks, *, name):
    t = kd.shape[0]
    nb = t // WINDOW

    def body(q_ref, kc_ref, kp_ref, vc_ref, vp_ref, s_ref, o_ref):
        first = _scalar11((pl.program_id(0) == 0).astype(F32))
        kvf = _scalar11(pl.program_id(1).astype(F32))
        outs = _att_fn([q_ref[:, _psl(j)] for j in range(PAIRS_PER_KV)], kc_ref[...], kp_ref[...], vc_ref[...], vp_ref[...],
                       s_ref[...], kvf, first)
        for j in range(PAIRS_PER_KV):
            o_ref[:, _psl(j)] = outs[j].astype(o_ref.dtype)

    cur = pl.BlockSpec((WINDOW, LANES), lambda i, kv: (i, kv))
    prev = pl.BlockSpec((WINDOW, LANES), lambda i, kv: (jnp.maximum(i - 1, 0), kv))
    qs = pl.BlockSpec((WINDOW, PAIRS_PER_KV * LANES), lambda i, kv: (i, kv))
    return pl.pallas_call(
        body, grid=(nb, N_KV_B),
        in_specs=[qs, cur, prev, cur, prev, pl.BlockSpec((1, LANES), lambda i, kv: (0, 0))],
        out_specs=qs, out_shape=jax.ShapeDtypeStruct((t, N_PAIRS * LANES), BF16), name=name,
        compiler_params=_params(("parallel", "parallel")),
    )(qsrc, kd, kd, vd, vd, sinks)


def swa_bwd(do, qsrc, kd, vd, sinks, *, name):
    t = kd.shape[0]
    nb = t // WINDOW

    def body(do_ref, q_ref, kc_ref, kp_ref, vc_ref, vp_ref, s_ref, dq_ref, dk_ref, dv_ref, ds_ref, carry_k, carry_v):
        step, kv = pl.program_id(0), pl.program_id(1)
        first = _scalar11((step == nb - 1).astype(F32))

        @pl.when((step == 0) & (kv == 0))
        def _():
            carry_k[...] = jnp.zeros_like(carry_k)
            carry_v[...] = jnp.zeros_like(carry_v)
            ds_ref[...] = jnp.zeros_like(ds_ref)

        kvf = _scalar11(kv.astype(F32))
        pairs = range(PAIRS_PER_KV)
        f32 = lambda ref: ref[...].astype(F32)
        _, vjp = jax.vjp(lambda *a: _att_fn(*a, kvf, first), [q_ref[:, _psl(j)].astype(F32) for j in pairs],
                         f32(kc_ref), f32(kp_ref), f32(vc_ref), f32(vp_ref), s_ref[...])
        dqs, dkc, dkp, dvc, dvp, dsk = vjp([do_ref[:, _psl(j)].astype(F32) for j in pairs])
        for j in pairs:
            dq_ref[:, _psl(j)] = dqs[j].astype(dq_ref.dtype)
        ds_ref[...] += dsk
        fold = lambda g: g + pltpu.roll(g, HEAD_DIM_B, 1)
        dk_ref[...] = fold(dkc + carry_k[kv]).astype(dk_ref.dtype)
        dv_ref[...] = fold(dvc + carry_v[kv]).astype(dv_ref.dtype)
        carry_k[kv] = dkp
        carry_v[kv] = dvp

    rev = lambda i: nb - 1 - i
    cur = pl.BlockSpec((WINDOW, LANES), lambda i, kv: (rev(i), kv))
    prev = pl.BlockSpec((WINDOW, LANES), lambda i, kv: (jnp.maximum(rev(i) - 1, 0), kv))
    qs = pl.BlockSpec((WINDOW, PAIRS_PER_KV * LANES), lambda i, kv: (rev(i), kv))
    sk = pl.BlockSpec((1, LANES), lambda i, kv: (0, 0))
    return pl.pallas_call(
        body, grid=(nb, N_KV_B),
        in_specs=[qs, qs, cur, prev, cur, prev, sk],
        out_specs=[qs, cur, cur, sk],
        out_shape=[jax.ShapeDtypeStruct((t, N_PAIRS * LANES), BF16), jax.ShapeDtypeStruct((t, N_KV_B * LANES), BF16),
                   jax.ShapeDtypeStruct((t, N_KV_B * LANES), BF16), jax.ShapeDtypeStruct((1, LANES), F32)],
        scratch_shapes=[pltpu.VMEM((N_KV_B, WINDOW, LANES), F32), pltpu.VMEM((N_KV_B, WINDOW, LANES), F32)],
        name=name, compiler_params=_params(("arbitrary", "arbitrary")),
    )(do, qsrc, kd, kd, vd, vd, sinks)


def loss_head(h, tgt, w, *, name):
    t, d = h.shape
    tm = min(256, t)

    def body(h_ref, t_ref, w_ref, dh_ref, dw_ref, l_ref):
        tg = t_ref[...]

        def f(hv, wv):
            err = _f_norm(hv, wv) - tg
            return 0.5 * jnp.sum(jnp.sum(err * err, axis=1, keepdims=True), axis=0, keepdims=True) * (1.0 / d)

        lv, vjp = jax.vjp(f, h_ref[...], w_ref[...])
        dh, dw = vjp(jnp.ones((1, 1), F32))
        dh_ref[...] = dh
        first = pl.program_id(0) == 0

        @pl.when(first)
        def _():
            dw_ref[...] = dw
            l_ref[...] = lv * jnp.ones((1, LANES), F32)

        @pl.when(jnp.logical_not(first))
        def _():
            dw_ref[...] += dw
            l_ref[...] += lv * jnp.ones((1, LANES), F32)

    rows = pl.BlockSpec((tm, d), lambda i: (i, 0))
    one = lambda c: pl.BlockSpec((1, c), lambda i: (0, 0))
    return pl.pallas_call(
        body, grid=(t // tm,), in_specs=[rows, rows, one(d)], out_specs=[rows, one(d), one(LANES)],
        out_shape=[jax.ShapeDtypeStruct((t, d), F32), jax.ShapeDtypeStruct((1, d), F32), jax.ShapeDtypeStruct((1, LANES), F32)],
        name=name, compiler_params=_params(("arbitrary",)),
    )(h, tgt, w)


def adamw(w, g, m, v, *, name):
    r, c = w.shape
    tr = r
    if r % SUBLANES == 0:
        for cand in range(SUBLANES, min(r, 256) + 1, SUBLANES):
            if r % cand == 0:
                tr = cand

    def body(w_ref, g_ref, m_ref, v_ref, d_ref, mo_ref, vo_ref):
        gv = g_ref[...]
        mn = ADAM_B1 * m_ref[...] + (1.0 - ADAM_B1) * gv
        vn = ADAM_B2 * v_ref[...] + (1.0 - ADAM_B2) * jnp.square(gv)
        m_hat = mn / (1.0 - ADAM_B1 ** ADAM_STEP)
        v_hat = vn / (1.0 - ADAM_B2 ** ADAM_STEP)
        d_ref[...] = -ADAM_LR * (m_hat / (jnp.sqrt(v_hat) + ADAM_EPS) + ADAM_WD * w_ref[...])
        mo_ref[...] = mn
        vo_ref[...] = vn

    spec = pl.BlockSpec((tr, c), lambda i: (i, 0))
    return pl.pallas_call(
        body, grid=(r // tr,), in_specs=[spec] * 4, out_specs=[spec] * 3,
        out_shape=[jax.ShapeDtypeStruct((r, c), F32)] * 3, name=name, compiler_params=_params(("parallel",)),
    )(w, g, m, v)


def _place():
    return lax.axis_index("x"), lax.axis_index("y"), lax.axis_index("c")


def allgather8(blk, *, name):
    def body(x_ref, out_ref, send_sems, recv_sems, local_sem):
        x, y, c = _place()
        me = 4 * x + 2 * y + c
        mine = pltpu.make_async_copy(x_ref, out_ref.at[me], local_sem)
        mine.start()
        sent = []
        for k in range(1, N_DEV):
            to = (x ^ ((k >> 2) & 1), y ^ ((k >> 1) & 1), c ^ (k & 1))
            cp = pltpu.make_async_remote_copy(src_ref=x_ref, dst_ref=out_ref.at[me], send_sem=send_sems.at[k - 1],
                                              recv_sem=recv_sems.at[k - 1], device_id=to, device_id_type=MESH)
            cp.start()
            sent.append(cp)
        for k in range(1, N_DEV):
            frm = me ^ k
            pltpu.make_async_remote_copy(src_ref=x_ref, dst_ref=out_ref.at[frm], send_sem=send_sems.at[k - 1],
                                         recv_sem=recv_sems.at[k - 1], device_id=(x, y, c), device_id_type=MESH).wait_recv()
        for cp in sent:
            cp.wait_send()
        mine.wait()

    vm = pl.BlockSpec(memory_space=pltpu.VMEM)
    return pl.pallas_call(
        body, in_specs=[vm], out_specs=vm, out_shape=jax.ShapeDtypeStruct((N_DEV,) + blk.shape, blk.dtype), name=name,
        scratch_shapes=[pltpu.SemaphoreType.DMA((N_DEV - 1,)), pltpu.SemaphoreType.DMA((N_DEV - 1,)), pltpu.SemaphoreType.DMA],
    )(blk)


def _other_chips(x, y):
    return [(1 - x, y), (x, 1 - y), (1 - x, 1 - y)]


def _hbm_call(body, ins, out_shapes, n_sems, name):
    hbm = pl.BlockSpec(memory_space=pl.ANY)
    return pl.pallas_call(
        body, in_specs=[hbm] * len(ins), out_specs=[hbm] * len(out_shapes), out_shape=out_shapes, name=name,
        scratch_shapes=[pltpu.SemaphoreType.DMA((n_sems,)), pltpu.SemaphoreType.DMA((n_sems,))],
    )(*ins)


def _half_rows(c, rh):
    return pl.ds(pl.multiple_of(c * rh, BF16_ROWS), rh)


def gather_units(units, *, name):
    nu = len(units)
    shapes = []
    for arr, layer_major in units:
        r, cols = arr.shape
        shapes.append(jax.ShapeDtypeStruct((2, N_CHIPS, r // 2, cols) if layer_major else (N_CHIPS, r, cols), arr.dtype))

    def body(*refs):
        in_refs, out_refs, send_sems, recv_sems = refs[:nu], refs[nu:2 * nu], refs[2 * nu], refs[2 * nu + 1]
        x, y, c = _place()
        me_chip = 2 * x + y
        sib = (x, y, 1 - c)
        chips = _other_chips(x, y)

        def copy(k, src, dst, to):
            return pltpu.make_async_remote_copy(src_ref=src, dst_ref=dst, send_sem=send_sems.at[k], recv_sem=recv_sems.at[k],
                                                device_id=to, device_id_type=MESH)

        first, passed, landing = [], [], []
        for u, (arr, layer_major) in enumerate(units):
            rh = arr.shape[0] // 2
            out_ref = out_refs[u]
            slot = (lambda chip, half, o=out_ref: o.at[half, chip]) if layer_major else \
                   (lambda chip, half, o=out_ref, rh=rh: o.at[chip, _half_rows(half, rh), :])
            my_half = in_refs[u].at[_half_rows(c, rh), :]
            for j, (cx, cy) in enumerate(chips):
                k = 6 * u + j
                first.append(copy(k, my_half, slot(me_chip, c), (cx, cy, c)))
                passed.append(copy(k + 3, slot(2 * cx + cy, c), slot(2 * cx + cy, c), sib))
                landing.append((copy(k, my_half, slot(2 * cx + cy, c), sib), copy(k + 3, my_half, slot(2 * cx + cy, 1 - c), sib)))
        for cp in first:
            cp.start()
        for (over_ici, _), fwd in zip(landing, passed):
            over_ici.wait_recv()
            fwd.start()
        for _, from_sibling in landing:
            from_sibling.wait_recv()
        for cp in first + passed:
            cp.wait_send()

    return _hbm_call(body, [a for a, _ in units], shapes, 6 * nu, name)


def swap_units(units, *, name):
    nu = len(units)

    def body(*refs):
        g_refs, out_refs, send_sems, recv_sems = refs[:nu], refs[nu:2 * nu], refs[2 * nu], refs[2 * nu + 1]
        x, y, c = _place()
        cps = [pltpu.make_async_remote_copy(src_ref=g_refs[u].at[:, _half_rows(1 - c, units[u].shape[1] // 2), :], dst_ref=out_refs[u],
                                            send_sem=send_sems.at[u], recv_sem=recv_sems.at[u], device_id=(x, y, 1 - c),
                                            device_id_type=MESH) for u in range(nu)]
        for cp in cps:
            cp.start()
        for cp in cps:
            cp.wait()

    shapes = [jax.ShapeDtypeStruct((N_CHIPS, g.shape[1] // 2, g.shape[2]), g.dtype) for g in units]
    return _hbm_call(body, units, shapes, nu, name)


def scatter_units(units, *, name):
    nu = len(units)

    def body(*refs):
        h_refs, out_refs, send_sems, recv_sems = refs[:nu], refs[nu:2 * nu], refs[2 * nu], refs[2 * nu + 1]
        x, y, c = _place()
        cps = [pltpu.make_async_remote_copy(src_ref=h_refs[u].at[2 * cx + cy], dst_ref=out_refs[u].at[j], send_sem=send_sems.at[3 * u + j],
                                            recv_sem=recv_sems.at[3 * u + j], device_id=(cx, cy, c), device_id_type=MESH)
               for u in range(nu) for j, (cx, cy) in enumerate(_other_chips(x, y))]
        for cp in cps:
            cp.start()
        for cp in cps:
            cp.wait()

    shapes = [jax.ShapeDtypeStruct((3,) + h.shape[1:], h.dtype) for h in units]
    return _hbm_call(body, units, shapes, 3 * nu, name)


def join_units(units, *, name):
    nu = len(units)

    def body(*refs):
        h_refs, out_refs, send_sems, recv_sems = refs[:nu], refs[nu:2 * nu], refs[2 * nu], refs[2 * nu + 1]
        x, y, c = _place()
        cps = [pltpu.make_async_remote_copy(src_ref=h_refs[u], dst_ref=out_refs[u], send_sem=send_sems.at[u], recv_sem=recv_sems.at[u],
                                            device_id=(x, y, 1 - c), device_id_type=MESH) for u in range(nu)]
        for cp in cps:
            cp.start()
        for cp in cps:
            cp.wait()

    return _hbm_call(body, units, [jax.ShapeDtypeStruct(h.shape, h.dtype) for h in units], nu, name)


def add_parts(parts, *, out_dtype, name):
    rows, cols = parts[0].shape
    tr = rows
    for cand in range(BF16_ROWS, min(rows, 512) + 1, BF16_ROWS):
        if rows % cand == 0:
            tr = cand

    def body(*refs):
        acc = refs[0][...].astype(F32)
        for r in refs[1:-1]:
            acc = acc + r[...].astype(F32)
        refs[-1][...] = acc.astype(refs[-1].dtype)

    spec = pl.BlockSpec((tr, cols), lambda i: (i, 0))
    return pl.pallas_call(
        body, grid=(rows // tr,), in_specs=[spec] * len(parts), out_specs=spec,
        out_shape=jax.ShapeDtypeStruct((rows, cols), out_dtype), name=name, compiler_params=_params(("parallel",)),
    )(*parts)


def sum8(g, *, name):
    def body(g_ref, o_ref):
        acc = g_ref[0]
        for d in range(1, N_DEV):
            acc = acc + g_ref[d]
        o_ref[...] = acc

    return pl.pallas_call(body, out_shape=jax.ShapeDtypeStruct(g.shape[1:], F32), name=name)(g)


def _dup_halves(a):
    t = a.shape[0]
    a = a.reshape(t, N_KV_B, HEAD_DIM_B)
    return jnp.concatenate([a, a], axis=-1).reshape(t, N_KV_B * LANES)


def _undup(a):
    t = a.shape[0]
    return a.reshape(t, N_KV_B, LANES)[:, :, :HEAD_DIM_B].reshape(t, N_KV_B * HEAD_DIM_B)


def _lane_pad(v, offset=0):
    return jnp.zeros((1, LANES), F32).at[0, offset:offset + v.shape[0]].set(v)


SHARD_UP = 2 * D_FF // N_CHIPS
SHARD_BIN = (N_HEADS_B + 2 * N_KV_B) * HEAD_DIM_B // N_CHIPS
SHARD_PROJ = D_MODEL // N_CHIPS


def local_step(x, p, tgt, sm, wt):
    t = x.shape[0]
    rtm = min(256, t)
    hk = N_HEADS_A * HEAD_DIM_A
    qd_b = N_HEADS_B * HEAD_DIM_B
    kd_b = N_KV_B * HEAD_DIM_B
    gw, gs = {}, {}
    norm = lambda h, w, nm: tile_map(_f_norm, [(h, D_MODEL, 0)], [w], [(D_MODEL, BF16)], tm=rtm, ncol=1, name=nm)[0]

    def norm_bwd(h, w, dy, add, nm):
        (dh,), (dw,) = tile_vjp(_f_norm, [(h, D_MODEL, 0)], [w], [(dy, D_MODEL, 0)], n_diff=1, tm=rtm, ncol=1, name=nm,
                                add=(add, D_MODEL, 0))
        return dh, dw

    spec = pl.BlockSpec
    mtm = _tile(D_MODEL, MM_TM_CAP)
    p_bf = p.astype(BF16)
    a_main, a_tail = wt["a_main"], wt["a_tail"]
    alog_p = _lane_pad(sm["a_log"][0], N_HEADS_A)
    dtb_p = _lane_pad(sm["a_dt_bias"][0], N_HEADS_A)
    sinks_p = _lane_pad(sm["b_sinks"][0])
    nw = lambda name, i: sm[name][i:i + 1]

    saved = []
    h = x
    hn_next = norm(h, nw("norm_mix", 0), "norm_mix0")
    for i in range(DEPTH):
        s = {"h0": h, "hn": hn_next}
        if i % 2 == 0:
            s["pm"] = mm(s["hn"], a_main, name="a_in_main")
            s["pt"] = mm(s["hn"], a_tail, name="a_in_tail")
            s["c"] = conv_fwd(s["pm"], wt["a_conv"], name="a_conv")
            s["bg"] = tile_map(_f_betag, [(s["pt"], LANES, 0)], [alog_p, dtb_p], [(LANES, F32)], tm=rtm, ncol=1, name="a_betag")[0]
            s["prep"] = delta_prep(s["c"], s["bg"], name="a_prep")
            s["o"], s["s_in"] = delta_scan(*s["prep"], name="a_scan")
            s["on"] = gnorm_fwd(s["o"], s["pm"], sm["a_norm"], name="a_gnorm")
            h, s["hf"] = mm(s["on"], wt["a_w_out"], add=h, norm_w=nw("norm_ffn", i), name="a_out")
        else:
            s["pb"] = mm(s["hn"], wt["b_w_in"], name="b_in", out_dtype=BF16, n=N_CHIPS * SHARD_BIN, tn=SHARD_BIN, tk=D_MODEL,
                         b_spec=spec((None, D_MODEL, SHARD_BIN), lambda r, j, kk: (j, kk, 0)))
            s["kd"], s["vd"] = _dup_halves(s["pb"][:, qd_b:qd_b + kd_b]), _dup_halves(s["pb"][:, qd_b + kd_b:])
            s["ao"] = swa_fwd(s["pb"], s["kd"], s["vd"], sinks_p, name="b_att")
            h, s["hf"] = mm(s["ao"], wt["b_w_out"], add=h, norm_w=nw("norm_ffn", i), name="b_out")
        s["h1"] = h
        s["u"] = mm(s["hf"], wt["f_w_up"], name=f"f_up{i}", n=2 * D_FF, tn=SHARD_UP, tk=D_MODEL,
                    b_spec=spec((None, None, D_MODEL, SHARD_UP), lambda r, j, kk, i=i: (j, i, kk, 0)))
        s["act"] = conv_act_fwd(s["u"], wt["f_conv"][i], name=f"f_conv_act{i}")
        h, s["hp"] = mm(s["act"], wt["f_w_down"], add=h, norm_w=nw("norm_ple", i), name=f"f_down{i}", n=D_MODEL, tn=D_MODEL,
                        tk=D_FF // 2, b_spec=spec((None, D_FF // 2, D_MODEL), lambda r, j, kk, i=i: (i, kk, j)))
        s["h2"] = h
        s["gl"] = mm(s["hp"], wt["ple_w_gate"], name=f"ple_gate{i}", n=D_MODEL, tn=D_MODEL, tk=D_MODEL,
                     b_spec=spec((None, D_MODEL, D_MODEL), lambda r, j, kk, i=i: (i, kk, j)))
        s["pe"] = mm(p_bf[i], wt["ple_w_proj"], name=f"ple_proj{i}", n=D_MODEL, tn=SHARD_PROJ, tk=PLE_DIM,
                     b_spec=spec((None, None, PLE_DIM, SHARD_PROJ), lambda r, j, kk, i=i: (j, i, kk, 0)))
        rows3 = [(h, D_MODEL, 0), (s["gl"], D_MODEL, 0), (s["pe"], D_MODEL, 0)]
        if i + 1 < DEPTH:
            def mix_norm(hv, g, e, wn):
                hn = hv + _f_ple(g, e)
                return hn, _f_norm(hn, wn)
            h, hn_next = tile_map(mix_norm, rows3, [nw("norm_mix", i + 1)], [(D_MODEL, F32), (D_MODEL, BF16)], tm=rtm, ncol=1,
                                  name=f"ple_mix{i}")
        else:
            h = tile_map(lambda hv, g, e: hv + _f_ple(g, e), rows3, [], [(D_MODEL, F32)], tm=rtm, ncol=1, name=f"ple_mix{i}")[0]
        saved.append(s)

    dh, gnf, loss = loss_head(h, tgt, sm["norm_final"][None, :], name="loss_head")
    gs["norm_final"] = gnf[0]

    g_mix, g_ffn, g_ple = [None] * DEPTH, [None] * DEPTH, [None] * DEPTH
    g_conv, g_down = [None] * DEPTH, [None] * DEPTH
    g_up = g_proj = g_gate = None
    for i in reversed(range(DEPTH)):
        s = saved[i]
        (dgl, dpe), _ = tile_vjp(_f_ple, [(s["gl"], D_MODEL, 0), (s["pe"], D_MODEL, 0)], [], [(dh, D_MODEL, 0)], n_diff=2,
                                 tm=rtm, ncol=1, name=f"ple_mix_bwd{i}", grad_dtypes=[BF16, BF16])
        g_proj = mm(p_bf[i], dpe, ta=True, name=f"ple_proj_dw{i}", out_dtype=BF16, tn=SHARD_PROJ, into=g_proj,
                    o_shape=(N_CHIPS, DEPTH, PLE_DIM, SHARD_PROJ),
                    o_spec=spec((None, None, PLE_DIM, SHARD_PROJ), lambda r, j, kk, i=i: (j, i, r, 0)))
        g_gate = mm(s["hp"], dgl, ta=True, name=f"ple_gate_dw{i}", out_dtype=BF16, tm_cap=SHARD_PROJ, tn=D_MODEL, into=g_gate,
                    o_shape=(N_CHIPS, DEPTH, SHARD_PROJ, D_MODEL),
                    o_spec=spec((None, None, SHARD_PROJ, D_MODEL), lambda r, j, kk, i=i: (r, i, 0, j)))
        dhp = mm(dgl, wt["ple_w_gate"], tb=True, name=f"ple_gate_dx{i}", n=D_MODEL, tn=D_MODEL, tk=D_MODEL,
                 b_spec=spec((None, D_MODEL, D_MODEL), lambda r, j, kk, i=i: (i, j, kk)))
        dh, g_ple[i] = norm_bwd(s["h2"], nw("norm_ple", i), dhp, dh, f"norm_ple_bwd{i}")

        dact = mm(dh, wt["f_w_down"], tb=True, name=f"f_down_dx{i}", n=D_FF, tn=D_FF // 2, tk=D_MODEL,
                  b_spec=spec((None, D_FF // 2, D_MODEL), lambda r, j, kk, i=i: (i, j, kk)))
        g_down[i] = mm(s["act"], dh, ta=True, name=f"f_down_dw{i}", out_dtype=BF16, tm_cap=D_FF // 2)
        du_halves = conv_act_bwd(s["u"], dact, wt["f_conv"][i], name=f"f_conv_act_bwd{i}")
        g_conv[i] = jnp.concatenate(du_halves[2:], axis=1)
        dhf = None
        for half, du in enumerate(du_halves[:2]):
            c0 = half * (N_CHIPS // 2)
            g_up = mm(s["hf"], du, ta=True, name=f"f_up_dw{i}_{half}", out_dtype=BF16, tn=SHARD_UP, into=g_up,
                      o_shape=(N_CHIPS, DEPTH, D_MODEL, SHARD_UP),
                      o_spec=spec((None, None, mtm, SHARD_UP), lambda r, j, kk, i=i, c0=c0: (c0 + j, i, r, 0)))
            dhf = mm(du, wt["f_w_up"], tb=True, name=f"f_up_dx{i}_{half}", n=D_MODEL, tn=D_MODEL, tk=SHARD_UP, add=dhf,
                     b_spec=spec((None, None, D_MODEL, SHARD_UP), lambda r, j, kk, i=i, c0=c0: (c0 + kk, i, j, 0)))
        dh, g_ffn[i] = norm_bwd(s["h1"], nw("norm_ffn", i), dhf, dh, f"norm_ffn_bwd{i}")

        if i % 2 == 0:
            don = mm(dh, wt["a_w_out"], tb=True, name="a_out_dx")
            gw["a_w_out"] = mm(s["on"], dh, ta=True, name="a_out_dw", out_dtype=BF16).reshape(N_CHIPS, SHARD_PROJ, D_MODEL)
            do, dz, gs["a_norm"] = gnorm_bwd(s["o"], s["pm"], sm["a_norm"], don, name="a_gnorm_bwd")
            dprep = delta_scan_bwd(do, *s["prep"], s["s_in"], name="a_scan_bwd")
            dc, dbg = delta_prep_bwd(s["c"], s["bg"], dprep, name="a_prep_bwd")
            (dpt,), (galog, gdtb) = tile_vjp(_f_betag, [(s["pt"], LANES, 0)], [alog_p, dtb_p], [(dbg, LANES, 0)], n_diff=1,
                                             tm=rtm, ncol=1, name="a_betag_bwd", grad_dtypes=[BF16])
            gs["a_log"] = galog[:, N_HEADS_A:2 * N_HEADS_A]
            gs["a_dt_bias"] = gdtb[:, N_HEADS_A:2 * N_HEADS_A]
            dqkv, gs["a_conv"] = conv_bwd(dc, s["pm"], wt["a_conv"], name="a_conv_bwd")
            dpm = jnp.concatenate([dqkv, dz], axis=1)
            dhn = mm(dpm, a_main, tb=True, name="a_in_main_dx")
            dhn = mm(dpt, a_tail, tb=True, add=dhn, name="a_in_tail_dx")
            g_main = mm(s["hn"], dpm, ta=True, name="a_in_main_dw", out_dtype=BF16)
            g_tail = mm(s["hn"], dpt, ta=True, name="a_in_tail_dw", out_dtype=BF16)
            g_in = jnp.concatenate([g_main, g_tail[:, :2 * N_HEADS_A]], axis=1)
            gw["a_w_in"] = g_in.reshape(D_MODEL, N_CHIPS, g_in.shape[1] // N_CHIPS).transpose(1, 0, 2)
        else:
            dao = mm(dh, wt["b_w_out"], tb=True, name="b_out_dx")
            gw["b_w_out"] = mm(s["ao"], dh, ta=True, name="b_out_dw", out_dtype=BF16).reshape(N_CHIPS, SHARD_PROJ, D_MODEL)
            dq, dkd, dvd, gsk = swa_bwd(dao, s["pb"], s["kd"], s["vd"], sinks_p, name="b_att_bwd")
            gs["b_sinks"] = gsk[:, :N_HEADS_B]
            dpb = jnp.concatenate([dq, _undup(dkd), _undup(dvd)], axis=1)
            dhn = mm(dpb, wt["b_w_in"], tb=True, name="b_in_dx", n=D_MODEL, tn=D_MODEL, tk=SHARD_BIN,
                     b_spec=spec((None, D_MODEL, SHARD_BIN), lambda r, j, kk: (kk, j, 0)))
            gw["b_w_in"] = mm(s["hn"], dpb, ta=True, name="b_in_dw", out_dtype=BF16, tn=SHARD_BIN,
                              o_shape=(N_CHIPS, D_MODEL, SHARD_BIN), o_spec=spec((None, mtm, SHARD_BIN), lambda r, j, kk: (j, r, 0)))
        dh, g_mix[i] = norm_bwd(s["h0"], nw("norm_mix", i), dhn, dh, f"norm_mix_bwd{i}")

    gs["norm_mix"], gs["norm_ffn"], gs["norm_ple"] = (jnp.concatenate(g, axis=0) for g in (g_mix, g_ffn, g_ple))
    gs["f_conv"] = jnp.stack(g_conv)
    gw["f_w_up"] = g_up.reshape(N_CHIPS, DEPTH * D_MODEL, SHARD_UP)
    gw["f_w_down"] = jnp.stack([g.reshape(N_CHIPS, D_FF // N_CHIPS, D_MODEL) for g in g_down], axis=1).reshape(N_CHIPS, -1, D_MODEL)
    gw["ple_w_proj"] = g_proj.reshape(N_CHIPS, DEPTH * PLE_DIM, SHARD_PROJ)
    gw["ple_w_gate"] = g_gate.reshape(N_CHIPS, DEPTH * SHARD_PROJ, D_MODEL)
    return loss, dh, gw, gs


BIG = [("a_w_in", False), ("a_w_out", False), ("b_w_in", False), ("b_w_out", False), ("f_w_up", False), ("f_w_down", True),
       ("ple_w_proj", False), ("ple_w_gate", True)]
CONVS = ["a_conv", "f_conv"]
SMALL = ["norm_mix", "norm_ffn", "norm_ple", "norm_final", "a_log", "a_dt_bias", "a_norm", "b_sinks"]
SMALL_ROWS = 8
CONV_ROWS = 16
CONV_GRAD_ROWS = 48


def _pack_rows(arrs, rows, dtype):
    flat = jnp.concatenate([a.reshape(-1).astype(dtype) for a in arrs])
    return jnp.pad(flat, (0, rows * PACK_COLS - flat.shape[0])).reshape(rows, PACK_COLS)


def _unpack(flat, shapes):
    out, off = [], 0
    for shp in shapes:
        n = math.prod(shp)
        out.append(flat[off:off + n].reshape(shp))
        off += n
    return out


def _pack_small(d, loss=None):
    tail = jnp.concatenate([d["a_log"].reshape(-1), d["a_dt_bias"].reshape(-1), d["a_norm"].reshape(-1), d["b_sinks"].reshape(-1)])
    if loss is not None:
        tail = jnp.concatenate([tail, loss.reshape(-1)[:1]])
    tail = jnp.pad(tail, (0, PACK_COLS - tail.shape[0]))
    return jnp.concatenate([d["norm_mix"], d["norm_ffn"], d["norm_ple"], d["norm_final"][None, :], tail[None, :]], axis=0)


def _unpack_small(a, like):
    out = {"norm_mix": a[0:2], "norm_ffn": a[2:4], "norm_ple": a[4:6], "norm_final": a[6]}
    off = 0
    for nm in ("a_log", "a_dt_bias", "a_norm", "b_sinks"):
        n = like[nm].size
        out[nm] = a[7, off:off + n].reshape(like[nm].shape)
        off += n
    return out, a[7, off]


def _as2d(a):
    return a.reshape(-1, a.shape[-1])


def kernel(x, p, norm_mix, norm_ffn, norm_ple, norm_final, a_w_in, a_conv, a_log, a_dt_bias, a_norm, a_w_out, b_w_in, b_sinks, b_w_out, f_w_up, f_conv, f_w_down, ple_w_proj, ple_w_gate, loss_target, m_norm_mix, m_norm_ffn, m_norm_ple, m_norm_final, m_a_w_in, m_a_conv, m_a_log, m_a_dt_bias, m_a_norm, m_a_w_out, m_b_w_in, m_b_sinks, m_b_w_out, m_f_w_up, m_f_conv, m_f_w_down, m_ple_w_proj, m_ple_w_gate, v_norm_mix, v_norm_ffn, v_norm_ple, v_norm_final, v_a_w_in, v_a_conv, v_a_log, v_a_dt_bias, v_a_norm, v_a_w_out, v_b_w_in, v_b_sinks, v_b_w_out, v_f_w_up, v_f_conv, v_f_w_down, v_ple_w_proj, v_ple_w_gate):
    w = dict(norm_mix=norm_mix, norm_ffn=norm_ffn, norm_ple=norm_ple, norm_final=norm_final, a_w_in=a_w_in, a_conv=a_conv,
             a_log=a_log, a_dt_bias=a_dt_bias, a_norm=a_norm, a_w_out=a_w_out, b_w_in=b_w_in, b_sinks=b_sinks, b_w_out=b_w_out,
             f_w_up=f_w_up, f_conv=f_conv, f_w_down=f_w_down, ple_w_proj=ple_w_proj, ple_w_gate=ple_w_gate)
    m = dict(norm_mix=m_norm_mix, norm_ffn=m_norm_ffn, norm_ple=m_norm_ple, norm_final=m_norm_final, a_w_in=m_a_w_in,
             a_conv=m_a_conv, a_log=m_a_log, a_dt_bias=m_a_dt_bias, a_norm=m_a_norm, a_w_out=m_a_w_out, b_w_in=m_b_w_in,
             b_sinks=m_b_sinks, b_w_out=m_b_w_out, f_w_up=m_f_w_up, f_conv=m_f_conv, f_w_down=m_f_w_down,
             ple_w_proj=m_ple_w_proj, ple_w_gate=m_ple_w_gate)
    v = dict(norm_mix=v_norm_mix, norm_ffn=v_norm_ffn, norm_ple=v_norm_ple, norm_final=v_norm_final, a_w_in=v_a_w_in,
             a_conv=v_a_conv, a_log=v_a_log, a_dt_bias=v_a_dt_bias, a_norm=v_a_norm, a_w_out=v_a_w_out, b_w_in=v_b_w_in,
             b_sinks=v_b_sinks, b_w_out=v_b_w_out, f_w_up=v_f_w_up, f_conv=v_f_conv, f_w_down=v_f_w_down,
             ple_w_proj=v_ple_w_proj, ple_w_gate=v_ple_w_gate)
    xc, yc, cc = _place()
    my_chip = 2 * xc + yc

    own = [(_as2d(w[n]).astype(BF16), lm) for n, lm in BIG]
    gathered = {}
    for (n, lm), (arr, _), g in zip(BIG, own, gather_units(own, name="gather_weights")):
        mine = arr.reshape(2, arr.shape[0] // 2, arr.shape[1]) if lm else arr
        gathered[n] = lax.dynamic_update_index_in_dim(g, mine, my_chip, 1 if lm else 0)
    a_in = jnp.concatenate([gathered["a_w_in"][j] for j in range(N_CHIPS)], axis=1)
    n_main = 4 * N_HEADS_A * HEAD_DIM_A
    conv_shapes = [w[n].shape for n in CONVS]
    convs = allgather8(_pack_rows([w[n] for n in CONVS], CONV_ROWS, F32), name="gather_convs")
    conv_parts = [_unpack(convs[2 * j].reshape(-1), conv_shapes) for j in range(N_CHIPS)]
    a_conv_full, f_conv_full = (jnp.concatenate([conv_parts[j][q] for j in range(N_CHIPS)], axis=2) for q in range(2))
    wt = {"a_main": a_in[:, :n_main], "a_tail": jnp.pad(a_in[:, n_main:], ((0, 0), (0, LANES - 2 * N_HEADS_A))),
          "a_w_out": gathered["a_w_out"].reshape(D_MODEL, D_MODEL), "b_w_out": gathered["b_w_out"].reshape(D_MODEL, D_MODEL),
          "b_w_in": gathered["b_w_in"], "f_w_up": gathered["f_w_up"].reshape(N_CHIPS, DEPTH, D_MODEL, SHARD_UP),
          "f_w_down": gathered["f_w_down"].reshape(DEPTH, D_FF, D_MODEL),
          "ple_w_proj": gathered["ple_w_proj"].reshape(N_CHIPS, DEPTH, PLE_DIM, SHARD_PROJ),
          "ple_w_gate": gathered["ple_w_gate"].reshape(DEPTH, D_MODEL, D_MODEL),
          "a_conv": a_conv_full[0], "f_conv": f_conv_full}
    sm = {n: w[n] for n in SMALL}

    loss, grad_x, gw, gs = local_step(x[0], p[:, 0], loss_target[0], sm, wt)

    units = [gw[n] for n, _ in BIG]
    from_sib = swap_units(units, name="rs_swap")
    pairs = []
    for (n, _), g, sib in zip(BIG, units, from_sib):
        rh, cols = g.shape[1] // 2, g.shape[2]
        mine = lax.dynamic_slice_in_dim(g, cc * rh, rh, axis=1)
        pairs.append(add_parts([mine.reshape(N_CHIPS * rh, cols), sib.reshape(N_CHIPS * rh, cols)], out_dtype=BF16,
                               name=f"rs_add_pair_{n}").reshape(N_CHIPS, rh, cols))
    others = scatter_units(pairs, name="rs_scatter")
    halves = [add_parts([lax.dynamic_index_in_dim(pr, my_chip, axis=0, keepdims=False), o[0], o[1], o[2]], out_dtype=F32,
                        name=f"rs_add_chips_{n}") for (n, _), pr, o in zip(BIG, pairs, others)]
    from_sib = join_units(halves, name="rs_join")
    g_sh = {n: jnp.where(cc == 0, jnp.concatenate([hf, ot]), jnp.concatenate([ot, hf])).reshape(w[n].shape)
            for (n, _), hf, ot in zip(BIG, halves, from_sib)}

    conv_grads = _pack_rows([gs[n] for n in CONVS], CONV_GRAD_ROWS, F32)
    small_sum = sum8(allgather8(jnp.concatenate([_pack_small(gs, loss), conv_grads]), name="gather_small"), name="sum_small")
    g_sm, loss_sum = _unpack_small(small_sum[:SMALL_ROWS], sm)
    for n, full in zip(CONVS, _unpack(small_sum[SMALL_ROWS:].reshape(-1), [gs[n].shape for n in CONVS])):
        g_sh[n] = lax.dynamic_slice_in_dim(full, my_chip * w[n].shape[-1], w[n].shape[-1], axis=full.ndim - 1).reshape(w[n].shape)

    grads, delta, new_m, new_v = {}, {}, {}, {}
    for n in [n for n, _ in BIG] + CONVS:
        d2, m2, v2 = adamw(_as2d(w[n]), _as2d(g_sh[n]), _as2d(m[n]), _as2d(v[n]), name=f"adamw_{n}")
        grads[n], delta[n], new_m[n], new_v[n] = g_sh[n], d2.reshape(w[n].shape), m2.reshape(w[n].shape), v2.reshape(w[n].shape)
    pk = lambda d: _pack_small(d)
    d2, m2, v2 = adamw(pk(sm), pk(g_sm), pk({n: m[n] for n in SMALL}), pk({n: v[n] for n in SMALL}), name="adamw_small")
    for src, dst in ((d2, delta), (m2, new_m), (v2, new_v)):
        dst.update(_unpack_small(src, sm)[0])
    grads.update(g_sm)

    order = ["norm_mix", "norm_ffn", "norm_ple", "norm_final", "a_w_in", "a_conv", "a_log", "a_dt_bias", "a_norm", "a_w_out",
             "b_w_in", "b_sinks", "b_w_out", "f_w_up", "f_conv", "f_w_down", "ple_w_proj", "ple_w_gate"]
    return (loss_sum, grad_x[None], *[grads[n] for n in order], *[delta[n] for n in order],
            *[new_m[n] for n in order], *[new_v[n] for n in order])
```

```python
import functools
import math

import jax
import jax.numpy as jnp
from jax import lax
from jax.experimental import pallas as pl
from jax.experimental.pallas import tpu as pltpu

F32 = jnp.float32
BF16 = jnp.bfloat16
MESH = pl.DeviceIdType.MESH

D_MODEL = 1024
N_HEADS_A = 8
HEAD_DIM_A = 128
CONV_A = 4
N_HEADS_B = 16
N_KV_B = 4
HEAD_DIM_B = 64
WINDOW = 128
D_FF = 2816
FFN_CONV = 3
PLE_DIM = 256
EPS = 1e-6
DEPTH = 2

ADAM_LR = 0.001
ADAM_B1 = 0.9
ADAM_B2 = 0.999
ADAM_EPS = 1e-08
ADAM_WD = 0.01
ADAM_STEP = 10

LANES = 128
SUBLANES = 8
BF16_ROWS = 16
CHUNK = 128
VMEM_LIMIT = 56 * 1024 * 1024
NEG = -1e30
N_CHIPS = 4
N_DEV = 8
PACK_COLS = 1024


def _params(sem=None):
    return pltpu.CompilerParams(dimension_semantics=sem, vmem_limit_bytes=VMEM_LIMIT)


def _tile(dim, cap):
    if dim % LANES:
        return dim
    best = LANES
    for t in range(LANES, min(dim, cap) + 1, LANES):
        if dim % t == 0:
            best = t
    return best


def _dot(a, b, dims=(((1,), (0,)), ((), ())), precision=None):
    return lax.dot_general(a, b, dims, precision=precision, preferred_element_type=F32)


NN = (((1,), (0,)), ((), ()))
NT = (((1,), (1,)), ((), ()))
TN = (((0,), (0,)), ((), ()))


MM_TM_CAP = 1024


def mm(a, b, *, name, ta=False, tb=False, out_dtype=F32, add=None, norm_w=None, tm_cap=MM_TM_CAP, tn_cap=1408, tk_cap=1408,
       n=None, tn=None, tk=None, b_spec=None, o_spec=None, o_shape=None, into=None):
    m, k = (a.shape[1], a.shape[0]) if ta else a.shape
    if b_spec is None:
        n = b.shape[0] if tb else b.shape[1]
        assert (b.shape[1] if tb else b.shape[0]) == k, (a.shape, b.shape, ta, tb)
    tm, tn, tk = _tile(m, tm_cap), tn or _tile(n, tn_cap), tk or _tile(k, tk_cap)
    assert n % tn == 0 and k % tk == 0, (n, tn, k, tk)
    nk = k // tk
    dims = (((0 if ta else 1,), (1 if tb else 0,)), ((), ()))
    has_add, has_norm = add is not None, norm_w is not None
    assert not has_norm or (tn == n and o_spec is None), "the norm epilogue needs whole rows"
    n_in = 2 + has_add + has_norm + (into is not None)

    def body(*refs):
        a_ref, b_ref = refs[0], refs[1]
        add_ref = refs[2] if has_add else None
        o_ref = refs[n_in]
        part = _dot(a_ref[...].astype(BF16), b_ref[...].astype(BF16), dims)

        def finish(r):
            if has_add:
                r = r + add_ref[...].astype(F32)
            o_ref[...] = r.astype(o_ref.dtype)
            if has_norm:
                refs[n_in + 1][...] = _f_norm(r, refs[2 + has_add][...]).astype(BF16)

        if nk == 1:
            finish(part)
            return
        acc = refs[-1]
        kk = pl.program_id(2)

        @pl.when(kk == 0)
        def _():
            acc[...] = part

        @pl.when(kk > 0)
        def _():
            acc[...] += part

        @pl.when(kk == nk - 1)
        def _():
            finish(acc[...])

    a_spec = pl.BlockSpec((tk, tm), lambda i, j, kk: (kk, i)) if ta else pl.BlockSpec((tm, tk), lambda i, j, kk: (i, kk))
    if b_spec is None:
        b_spec = pl.BlockSpec((tn, tk), lambda i, j, kk: (j, kk)) if tb else pl.BlockSpec((tk, tn), lambda i, j, kk: (kk, j))
    plain_o = pl.BlockSpec((tm, tn), lambda i, j, kk: (i, j))
    if o_spec is None:
        o_spec, o_shape = plain_o, (m, n)
    in_specs = [a_spec, b_spec] + ([plain_o] if has_add else [])
    args = (a, b) + ((add,) if has_add else ())
    out_specs, out_shapes = o_spec, jax.ShapeDtypeStruct(tuple(o_shape), out_dtype)
    if has_norm:
        in_specs.append(pl.BlockSpec((1, n), lambda i, j, kk: (0, 0)))
        args += (norm_w,)
        out_specs, out_shapes = [o_spec, plain_o], [out_shapes, jax.ShapeDtypeStruct((m, n), BF16)]
    aliases = {}
    if into is not None:
        assert into.shape == tuple(o_shape) and into.dtype == out_dtype, (into.shape, o_shape)
        in_specs.append(pl.BlockSpec(memory_space=pl.ANY))
        args += (into,)
        aliases = {n_in - 1: 0}
    return pl.pallas_call(
        body, grid=(m // tm, n // tn, nk), in_specs=in_specs, out_specs=out_specs,
        out_shape=out_shapes, name=name, input_output_aliases=aliases,
        scratch_shapes=[pltpu.VMEM((tm, tn), F32)] if nk > 1 else [],
        compiler_params=_params(("parallel", "parallel", "arbitrary")),
    )(*args)


def _row_spec(tm, cw, coff):
    return pl.BlockSpec((tm, cw), lambda i, j: (i, j + coff))


def _full_spec(shape):
    return pl.BlockSpec(shape, lambda i, j: (0,) * len(shape))


def tile_map(fn, rows, params, outs, *, tm, ncol, name):
    t = rows[0][0].shape[0]
    nin = len(rows) + len(params)

    def body(*refs):
        res = fn(*[r[...] for r in refs[:nin]])
        res = res if isinstance(res, (tuple, list)) else (res,)
        for o_ref, r in zip(refs[nin:], res):
            o_ref[...] = r.astype(o_ref.dtype)

    in_specs = [_row_spec(tm, cw, coff) for (_, cw, coff) in rows] + [_full_spec(p.shape) for p in params]
    res = pl.pallas_call(
        body, grid=(t // tm, ncol), in_specs=in_specs,
        out_specs=[_row_spec(tm, cw, 0) for (cw, _) in outs],
        out_shape=[jax.ShapeDtypeStruct((t, cw * ncol), dt) for (cw, dt) in outs], name=name,
        compiler_params=_params(("parallel", "parallel")),
    )(*[r[0] for r in rows], *params)
    return res


def tile_vjp(fn, rows, params, cts, *, n_diff, tm, ncol, name, add=None, grad_dtypes=None):
    t = rows[0][0].shape[0]
    nr, npar, nct = len(rows), len(params), len(cts)
    has_add = add is not None

    def body(*refs):
        vals = [r[...] for r in refs[:nr + npar + nct + (1 if has_add else 0)]]
        diff, rest, pars = vals[:n_diff], vals[n_diff:nr], vals[nr:nr + npar]
        ctv = vals[nr + npar:nr + npar + nct]
        outs_ref = refs[nr + npar + nct + (1 if has_add else 0):]

        def f(*a):
            res = fn(*a[:n_diff], *rest, *a[n_diff:])
            return tuple(res) if isinstance(res, (tuple, list)) else (res,)

        primal, vjp = jax.vjp(f, *[d.astype(F32) for d in diff], *pars)
        grads = vjp(tuple(c.astype(o.dtype) for c, o in zip(ctv, primal)))
        for q in range(n_diff):
            g = grads[q]
            if has_add and q == 0:
                g = g + vals[-1]
            outs_ref[q][...] = g.astype(outs_ref[q].dtype)
        first = (pl.program_id(0) == 0) & (pl.program_id(1) == 0)
        for q in range(npar):
            o_ref, g = outs_ref[n_diff + q], grads[n_diff + q]

            @pl.when(first)
            def _(o_ref=o_ref, g=g):
                o_ref[...] = g

            @pl.when(jnp.logical_not(first))
            def _(o_ref=o_ref, g=g):
                o_ref[...] += g

    ins = list(rows) + [None] * 0
    in_specs = [_row_spec(tm, cw, coff) for (_, cw, coff) in rows] + [_full_spec(p.shape) for p in params]
    in_specs += [_row_spec(tm, cw, coff) for (_, cw, coff) in cts]
    args = [r[0] for r in rows] + list(params) + [c[0] for c in cts]
    if has_add:
        in_specs.append(_row_spec(tm, add[1], add[2]))
        args.append(add[0])
    out_specs = [_row_spec(tm, rows[q][1], 0) for q in range(n_diff)] + [_full_spec(p.shape) for p in params]
    grad_dtypes = grad_dtypes or [F32] * n_diff
    out_shape = [jax.ShapeDtypeStruct((t, rows[q][1] * ncol), grad_dtypes[q]) for q in range(n_diff)]
    out_shape += [jax.ShapeDtypeStruct(p.shape, F32) for p in params]
    del ins
    res = pl.pallas_call(
        body, grid=(t // tm, ncol), in_specs=in_specs, out_specs=out_specs, out_shape=out_shape, name=name,
        compiler_params=_params(("arbitrary", "arbitrary")),
    )(*args)
    return res[:n_diff], res[n_diff:]


def _silu(x):
    return x * jax.nn.sigmoid(x)


def _f_norm(h, w):
    return h * lax.rsqrt(jnp.mean(h * h, axis=-1, keepdims=True) + EPS) * w


def _f_gnorm(o, z, w):
    return _f_norm(o, w) * _silu(z)


def _f_act(gate, val):
    return _silu(gate) * val


def _f_ple(gl, pe):
    return jax.nn.sigmoid(gl) * pe


def _f_betag(pt, alog, dtb):
    lane = lax.broadcasted_iota(jnp.int32, (1, LANES), 1)
    z = pt + dtb
    softplus = jnp.maximum(z, 0.0) + jnp.log(1.0 + jnp.exp(-jnp.abs(z)))
    g = -jnp.exp(alog) * softplus
    return jnp.where(lane < N_HEADS_A, jax.nn.sigmoid(pt), jnp.where(lane < 2 * N_HEADS_A, g, 0.0))


CONV_TM = 256
CONV_CW = 1024


def _shift_down(x, prev, s, row):
    rp = jnp.tile(pltpu.roll(prev, s, 0), (x.shape[0] // SUBLANES, 1))
    return jnp.where(row < s, rp, pltpu.roll(x, s, 0))


def _shift_up(x, nxt, s, row):
    tm = x.shape[0]
    rn = jnp.tile(pltpu.roll(nxt, SUBLANES - s, 0), (tm // SUBLANES, 1))
    return jnp.where(row >= tm - s, rn, pltpu.roll(x, tm - s, 0))


def _conv_taps(x, prev, w_ref, cols, row):
    k = w_ref.shape[0]
    y = x * w_ref[pl.ds(k - 1, 1), cols]
    for s in range(1, k):
        y = y + _shift_down(x, prev, s, row) * w_ref[pl.ds(k - 1 - s, 1), cols]
    return y


def _lane_chunks(cw):
    return [slice(cb * LANES, (cb + 1) * LANES) for cb in range(cw // LANES)]


def conv_fwd(x, w, *, name):
    t = x.shape[0]
    k, c = w.shape
    tm, cw = min(CONV_TM, t), CONV_CW
    nb8 = tm // SUBLANES

    def body(x_ref, p_ref, w_ref, o_ref):
        first = pl.program_id(1) == 0
        row = lax.broadcasted_iota(jnp.int32, (tm, LANES), 0)
        for cols in _lane_chunks(cw):
            o_ref[:, cols] = _conv_taps(x_ref[:, cols], jnp.where(first, 0.0, p_ref[:, cols]), w_ref, cols, row)

    return pl.pallas_call(
        body, grid=(c // cw, t // tm),
        in_specs=[pl.BlockSpec((tm, cw), lambda j, i: (i, j)),
                  pl.BlockSpec((SUBLANES, cw), lambda j, i: (jnp.maximum(i * nb8 - 1, 0), j)),
                  pl.BlockSpec((k, cw), lambda j, i: (0, j))],
        out_specs=pl.BlockSpec((tm, cw), lambda j, i: (i, j)),
        out_shape=jax.ShapeDtypeStruct((t, c), F32), name=name,
        compiler_params=_params(("parallel", "parallel")),
    )(x, x, w)


def conv_bwd(dy, x, w, *, name):
    t = x.shape[0]
    k, c = w.shape
    tm, cw = min(CONV_TM, t), CONV_CW
    nb8 = tm // SUBLANES
    ni = t // tm

    def body(dy_ref, dn_ref, x_ref, p_ref, w_ref, dx_ref, dw_ref):
        i = pl.program_id(1)
        first, last = i == 0, i == ni - 1
        row = lax.broadcasted_iota(jnp.int32, (tm, LANES), 0)
        for cols in _lane_chunks(cw):
            dyv, xv = dy_ref[:, cols], x_ref[:, cols]
            nxt = jnp.where(last, 0.0, dn_ref[:, cols])
            prev = jnp.where(first, 0.0, p_ref[:, cols])
            dx = dyv * w_ref[pl.ds(k - 1, 1), cols]
            dws = [jnp.sum(dyv * xv, axis=0, keepdims=True)]
            for s in range(1, k):
                dx = dx + _shift_up(dyv, nxt, s, row) * w_ref[pl.ds(k - 1 - s, 1), cols]
                dws.append(jnp.sum(dyv * _shift_down(xv, prev, s, row), axis=0, keepdims=True))
            dx_ref[:, cols] = dx.astype(dx_ref.dtype)
            for s in range(k):
                @pl.when(first)
                def _(s=s, dws=dws, cols=cols):
                    dw_ref[pl.ds(k - 1 - s, 1), cols] = dws[s]

                @pl.when(jnp.logical_not(first))
                def _(s=s, dws=dws, cols=cols):
                    dw_ref[pl.ds(k - 1 - s, 1), cols] += dws[s]

    return pl.pallas_call(
        body, grid=(c // cw, ni),
        in_specs=[pl.BlockSpec((tm, cw), lambda j, i: (i, j)),
                  pl.BlockSpec((SUBLANES, cw), lambda j, i: (jnp.minimum((i + 1) * nb8, t // SUBLANES - 1), j)),
                  pl.BlockSpec((tm, cw), lambda j, i: (i, j)),
                  pl.BlockSpec((SUBLANES, cw), lambda j, i: (jnp.maximum(i * nb8 - 1, 0), j)),
                  pl.BlockSpec((k, cw), lambda j, i: (0, j))],
        out_specs=[pl.BlockSpec((tm, cw), lambda j, i: (i, j)), pl.BlockSpec((k, cw), lambda j, i: (0, j))],
        out_shape=[jax.ShapeDtypeStruct((t, c), BF16), jax.ShapeDtypeStruct((k, c), F32)], name=name,
        compiler_params=_params(("parallel", "arbitrary")),
    )(dy, dy, x, x, w)


FFN_TM = 128
FFN_CW = D_FF // 2


def _ffn_specs(t, tm, cw, k):
    nb8, ncol = tm // SUBLANES, D_FF // cw
    cur = lambda off: pl.BlockSpec((tm, cw), lambda j, i: (i, j + off))
    prev = lambda off: pl.BlockSpec((SUBLANES, cw), lambda j, i: (jnp.maximum(i * nb8 - 1, 0), j + off))
    nxt = lambda off: pl.BlockSpec((SUBLANES, cw), lambda j, i: (jnp.minimum((i + 1) * nb8, t // SUBLANES - 1), j + off))
    taps = lambda off: pl.BlockSpec((k, cw), lambda j, i: (0, j + off))
    return cur, prev, nxt, taps, ncol


def conv_act_fwd(u, w, *, name):
    t, k = u.shape[0], w.shape[0]
    tm, cw = min(FFN_TM, t), FFN_CW
    cur, prev, _, taps, ncol = _ffn_specs(t, tm, cw, k)

    def body(ug_ref, pg_ref, uv_ref, pv_ref, wg_ref, wv_ref, o_ref):
        first = pl.program_id(1) == 0
        row = lax.broadcasted_iota(jnp.int32, (tm, LANES), 0)
        for cb in range(cw // LANES):
            cols = slice(cb * LANES, (cb + 1) * LANES)
            cg = _conv_taps(ug_ref[:, cols], jnp.where(first, 0.0, pg_ref[:, cols]), wg_ref, cols, row)
            cv = _conv_taps(uv_ref[:, cols], jnp.where(first, 0.0, pv_ref[:, cols]), wv_ref, cols, row)
            o_ref[:, cols] = _f_act(cg, cv).astype(o_ref.dtype)

    return pl.pallas_call(
        body, grid=(ncol, t // tm),
        in_specs=[cur(0), prev(0), cur(ncol), prev(ncol), taps(0), taps(ncol)],
        out_specs=cur(0), out_shape=jax.ShapeDtypeStruct((t, D_FF), BF16), name=name,
        compiler_params=_params(("parallel", "parallel")),
    )(u, u, u, u, w, w)


def conv_act_bwd(u, dact, w, *, name):
    t, k = u.shape[0], w.shape[0]
    tm, cw = min(FFN_TM, t), FFN_CW
    cur, prev, nxt, taps, ncol = _ffn_specs(t, tm, cw, k)
    ni = t // tm

    def body(ug_ref, pg_ref, ng_ref, uv_ref, pv_ref, nv_ref, d_ref, dn_ref, wg_ref, wv_ref, dg_ref, dv_ref, dwg_ref, dwv_ref):
        i = pl.program_id(1)
        first, last = i == 0, i == ni - 1
        row = lax.broadcasted_iota(jnp.int32, (tm, LANES), 0)
        row8 = lax.broadcasted_iota(jnp.int32, (SUBLANES, LANES), 0)
        for cb in range(cw // LANES):
            cols = slice(cb * LANES, (cb + 1) * LANES)
            ug, uv = ug_ref[:, cols], uv_ref[:, cols]
            pg, pv = jnp.where(first, 0.0, pg_ref[:, cols]), jnp.where(first, 0.0, pv_ref[:, cols])
            sg = [ug] + [_shift_down(ug, pg, s, row) for s in range(1, k)]
            sv = [uv] + [_shift_down(uv, pv, s, row) for s in range(1, k)]
            taps = lambda xs, w_ref: sum(xs[s] * w_ref[pl.ds(k - 1 - s, 1), cols] for s in range(k))
            _, vjp = jax.vjp(_f_act, taps(sg, wg_ref), taps(sv, wv_ref))
            dcg, dcv = vjp(d_ref[:, cols])
            _, vjp_n = jax.vjp(_f_act, _conv_taps(ng_ref[:, cols], ug[tm - SUBLANES:], wg_ref, cols, row8),
                               _conv_taps(nv_ref[:, cols], uv[tm - SUBLANES:], wv_ref, cols, row8))
            dcgn, dcvn = vjp_n(jnp.where(last, 0.0, dn_ref[:, cols]))
            for dc, dcn, xs, w_ref, dx_ref, dw_ref in ((dcg, dcgn, sg, wg_ref, dg_ref, dwg_ref),
                                                       (dcv, dcvn, sv, wv_ref, dv_ref, dwv_ref)):
                dx = dc * w_ref[pl.ds(k - 1, 1), cols]
                dws = [jnp.sum(dc * xs[0], axis=0, keepdims=True)]
                for s in range(1, k):
                    dx = dx + _shift_up(dc, dcn, s, row) * w_ref[pl.ds(k - 1 - s, 1), cols]
                    dws.append(jnp.sum(dc * xs[s], axis=0, keepdims=True))
                dx_ref[:, cols] = dx.astype(dx_ref.dtype)
                for s in range(k):
                    @pl.when(first)
                    def _(s=s, dw_ref=dw_ref, dws=dws):
                        dw_ref[pl.ds(k - 1 - s, 1), cols] = dws[s]

                    @pl.when(jnp.logical_not(first))
                    def _(s=s, dw_ref=dw_ref, dws=dws):
                        dw_ref[pl.ds(k - 1 - s, 1), cols] += dws[s]

    half = jax.ShapeDtypeStruct((t, D_FF), BF16)
    dwh = jax.ShapeDtypeStruct((k, D_FF), F32)
    return pl.pallas_call(
        body, grid=(ncol, ni),
        in_specs=[cur(0), prev(0), nxt(0), cur(ncol), prev(ncol), nxt(ncol), cur(0), nxt(0), taps(0), taps(ncol)],
        out_specs=[cur(0), cur(0), taps(0), taps(0)], out_shape=[half, half, dwh, dwh], name=name,
        compiler_params=_params(("parallel", "arbitrary")),
    )(u, u, u, u, u, u, dact, dact, w, w)


def _each(f, *lists):
    return [f(*a) for a in zip(*lists)]


@jax.custom_vjp
def _inv_unit_lower(lms):
    return _inv_blocks(lms)


def _inv_blocks(lms):
    c = lms[0].shape[0]
    ri = lax.broadcasted_iota(jnp.int32, (c, c), 0)
    ci = lax.broadcasted_iota(jnp.int32, (c, c), 1)
    eye = (ri == ci).astype(F32)
    dms = _each(lambda lm: eye - jnp.where((ri >> 1) == (ci >> 1), lm, 0.0), lms)
    for lv in range(1, int(math.log2(c))):
        below = ((ri >> (lv + 1)) == (ci >> (lv + 1))) & ((ri >> lv) != (ci >> lv))
        dbs = _each(lambda dm: dm.astype(BF16), dms)
        ods = _each(lambda lm, db: _dot(jnp.where(below, lm, 0.0).astype(BF16), db).astype(BF16), lms, dbs)
        dms = _each(lambda dm, db, od: dm - _dot(db, od), dms, dbs, ods)
    return dms


def _inv_fwd(lms):
    tms = _inv_blocks(lms)
    return tms, tms


def _inv_bwd(tms, dts):
    tbs = _each(lambda tm: tm.astype(BF16), tms)
    mid = _each(lambda tb, dt: _dot(tb, dt.astype(BF16), TN).astype(BF16), tbs, dts)
    return (_each(lambda m, tb: -_dot(m, tb, NT), mid, tbs),)


_inv_unit_lower.defvjp(_inv_fwd, _inv_bwd)


def _l2n(x):
    return x * lax.rsqrt(jnp.sum(x * x, axis=-1, keepdims=True) + EPS)


def _prep_fn(cqs, cks, cvs, bg, sel_b, sel_g):
    c = cqs[0].shape[0]
    ri = lax.broadcasted_iota(jnp.int32, (c, c), 0)
    ci = lax.broadcasted_iota(jnp.int32, (c, c), 1)
    eye = (ri == ci).astype(F32)
    incl, strict = ci <= ri, ci < ri
    last = lax.broadcasted_iota(jnp.int32, (c, 1), 0) == c - 1
    to_row = lambda col: jnp.sum(col * eye, axis=0, keepdims=True)
    qs = _each(lambda a: _l2n(_silu(a)) * (HEAD_DIM_A ** -0.5), cqs)
    ks = _each(lambda a: _l2n(_silu(a)), cks)
    vbs = _each(lambda a: _silu(a).astype(BF16), cvs)
    betas = _each(lambda m: jnp.sum(bg * m, axis=1, keepdims=True), sel_b)
    gs = _each(lambda m: jnp.sum(bg * m, axis=1, keepdims=True), sel_g)
    gcss = _each(lambda g: jnp.sum(jnp.where(incl, to_row(g), 0.0), axis=1, keepdims=True), gs)
    gtots = _each(lambda gcs: jnp.sum(jnp.where(last, gcs, 0.0), axis=0, keepdims=True), gcss)
    decays = _each(lambda gcs: jnp.exp(jnp.where(incl, gcs - to_row(gcs), NEG)), gcss)
    kbs = _each(lambda k: k.astype(BF16), ks)
    lms = _each(lambda beta, kb, dec: jnp.where(strict, beta * _dot(kb, kb, NT) * dec, 0.0), betas, kbs, decays)
    ams = _each(lambda tm, beta: (tm * to_row(beta)).astype(BF16), _inv_unit_lower(lms), betas)
    gams = _each(jnp.exp, gcss)
    u0s = _each(_dot, ams, vbs)
    wks = _each(lambda am, gam, k: _dot(am, (gam * k).astype(BF16)), ams, gams, ks)
    qks = _each(lambda q, kb, dec: _dot(q.astype(BF16), kb, NT) * dec, qs, kbs, decays)
    qds = _each(lambda q, gam: q * gam, qs, gams)
    kds = _each(lambda k, gtot, gcs: k * jnp.exp(gtot - gcs), ks, gtots, gcss)
    gls = _each(lambda gtot: jnp.exp(gtot) * jnp.ones((SUBLANES, LANES), F32), gtots)
    return u0s, wks, qds, kds, qks, gls


def _head_masks(h):
    lane = lax.broadcasted_iota(jnp.int32, (1, LANES), 1)
    return (lane == h).astype(F32), (lane == h + N_HEADS_A).astype(F32)


def _hsl(j):
    return slice(j * HEAD_DIM_A, (j + 1) * HEAD_DIM_A)


def gnorm_fwd(o, zsrc, w, *, name):
    t, width = o.shape
    tm = min(256, t)
    zoff = zsrc.shape[1] // width - 1

    def body(o_ref, z_ref, w_ref, out_ref):
        for h in range(N_HEADS_A):
            out_ref[:, _hsl(h)] = _f_gnorm(o_ref[:, _hsl(h)], z_ref[:, _hsl(h)], w_ref[...]).astype(out_ref.dtype)

    rows = pl.BlockSpec((tm, width), lambda i: (i, 0))
    return pl.pallas_call(
        body, grid=(t // tm,),
        in_specs=[rows, pl.BlockSpec((tm, width), lambda i: (i, zoff)), pl.BlockSpec(w.shape, lambda i: (0, 0))],
        out_specs=rows, out_shape=jax.ShapeDtypeStruct((t, width), BF16), name=name, compiler_params=_params(("parallel",)),
    )(o, zsrc, w)


def gnorm_bwd(o, zsrc, w, don, *, name):
    t, width = o.shape
    tm = min(256, t)
    zoff = zsrc.shape[1] // width - 1

    def body(o_ref, z_ref, w_ref, d_ref, do_ref, dz_ref, dw_ref):
        dw = jnp.zeros(w.shape, F32)
        for h in range(N_HEADS_A):
            _, vjp = jax.vjp(_f_gnorm, o_ref[:, _hsl(h)], z_ref[:, _hsl(h)], w_ref[...])
            do, dz, dwh = vjp(d_ref[:, _hsl(h)])
            do_ref[:, _hsl(h)] = do.astype(do_ref.dtype)
            dz_ref[:, _hsl(h)] = dz.astype(dz_ref.dtype)
            dw = dw + dwh
        first = pl.program_id(0) == 0

        @pl.when(first)
        def _():
            dw_ref[...] = dw

        @pl.when(jnp.logical_not(first))
        def _():
            dw_ref[...] += dw

    rows = pl.BlockSpec((tm, width), lambda i: (i, 0))
    wspec = pl.BlockSpec(w.shape, lambda i: (0, 0))
    return pl.pallas_call(
        body, grid=(t // tm,),
        in_specs=[rows, pl.BlockSpec((tm, width), lambda i: (i, zoff)), wspec, rows],
        out_specs=[rows, rows, wspec],
        out_shape=[jax.ShapeDtypeStruct((t, width), BF16)] * 2 + [jax.ShapeDtypeStruct(w.shape, F32)], name=name,
        compiler_params=_params(("arbitrary",)),
    )(o, zsrc, w, don)


def delta_prep(cqkv, bg, *, name):
    t = cqkv.shape[0]
    nh, hd, n = N_HEADS_A, HEAD_DIM_A, t // CHUNK

    def body(cq_ref, ck_ref, cv_ref, bg_ref, u0_ref, wk_ref, qd_ref, kd_ref, qk_ref, gl_ref):
        heads = range(nh)
        masks = [_head_masks(j) for j in heads]
        res = _prep_fn([cq_ref[:, _hsl(j)] for j in heads], [ck_ref[:, _hsl(j)] for j in heads],
                       [cv_ref[:, _hsl(j)] for j in heads], bg_ref[...], [m[0] for m in masks], [m[1] for m in masks])
        for o_ref, rs in zip((u0_ref, wk_ref, qd_ref, kd_ref, qk_ref), res[:5]):
            for j in heads:
                o_ref[:, _hsl(j)] = rs[j]
        for j in heads:
            gl_ref[j * SUBLANES:(j + 1) * SUBLANES, :] = res[5][j]

    blk = lambda off: pl.BlockSpec((CHUNK, nh * hd), lambda i: (i, off))
    return pl.pallas_call(
        body, grid=(n,),
        in_specs=[blk(0), blk(1), blk(2), pl.BlockSpec((CHUNK, LANES), lambda i: (i, 0))],
        out_specs=[blk(0)] * 5 + [pl.BlockSpec((nh * SUBLANES, LANES), lambda i: (i, 0))],
        out_shape=[jax.ShapeDtypeStruct((t, nh * hd), F32)] * 5 + [jax.ShapeDtypeStruct((n * nh * SUBLANES, LANES), F32)],
        name=name, compiler_params=_params(("parallel",)),
    )(cqkv, cqkv, cqkv, bg)


def delta_prep_bwd(cqkv, bg, cts, *, name):
    t = cqkv.shape[0]
    nh, hd, n = N_HEADS_A, HEAD_DIM_A, t // CHUNK

    def body(cq_ref, ck_ref, cv_ref, bg_ref, c0, c1, c2, c3, c4, c5, dc_ref, dbg_ref):
        heads = range(nh)
        masks = [_head_masks(j) for j in heads]
        _, vjp = jax.vjp(lambda a, b, c, d: _prep_fn(a, b, c, d, [m[0] for m in masks], [m[1] for m in masks]),
                         [cq_ref[:, _hsl(j)] for j in heads], [ck_ref[:, _hsl(j)] for j in heads],
                         [cv_ref[:, _hsl(j)] for j in heads], bg_ref[...])
        cts = tuple([c[:, _hsl(j)] for j in heads] for c in (c0, c1, c2, c3, c4))
        dqs, dks, dvs, dbg = vjp(cts + ([c5[j * SUBLANES:(j + 1) * SUBLANES, :] for j in heads],))
        for part, ds in enumerate((dqs, dks, dvs)):
            for j in heads:
                dc_ref[:, _hsl(part * nh + j)] = ds[j]
        dbg_ref[...] = dbg

    blk = lambda off: pl.BlockSpec((CHUNK, nh * hd), lambda i: (i, off))
    gl_spec = pl.BlockSpec((nh * SUBLANES, LANES), lambda i: (i, 0))
    bg_spec = pl.BlockSpec((CHUNK, LANES), lambda i: (i, 0))
    return pl.pallas_call(
        body, grid=(n,),
        in_specs=[blk(0), blk(1), blk(2), bg_spec] + [blk(0)] * 5 + [gl_spec],
        out_specs=[pl.BlockSpec((CHUNK, 3 * nh * hd), lambda i: (i, 0)), bg_spec],
        out_shape=[jax.ShapeDtypeStruct((t, 3 * nh * hd), F32), jax.ShapeDtypeStruct((t, LANES), F32)],
        name=name, compiler_params=_params(("parallel",)),
    )(cqkv, cqkv, cqkv, bg, *cts)


def delta_scan(u0, wk, qd, kd, qk, gl, *, name):
    t = u0.shape[0]
    nh, hd, n = N_HEADS_A, HEAD_DIM_A, t // CHUNK

    def body(u0_ref, wk_ref, qd_ref, kd_ref, qk_ref, gl_ref, o_ref, sin_ref, s_ref):
        @pl.when(pl.program_id(0) == 0)
        def _():
            s_ref[...] = jnp.zeros_like(s_ref)

        heads = list(range(nh))
        cols = lambda ref: [ref[:, _hsl(h)].astype(BF16) for h in heads]
        ss = [s_ref[h] for h in heads]
        for h in heads:
            sin_ref[h] = ss[h]
        sbs = _each(lambda s: s.astype(BF16), ss)
        ubs = _each(lambda h, wkb, sb: (u0_ref[:, _hsl(h)] - _dot(wkb, sb)).astype(BF16), heads, cols(wk_ref), sbs)
        os_ = _each(lambda qdb, sb, qkb, ub: _dot(qdb, sb) + _dot(qkb, ub), cols(qd_ref), sbs, cols(qk_ref), ubs)
        sn = _each(lambda h, s, kdb, ub: gl_ref[pl.ds(h * SUBLANES, 1), :] * s + _dot(kdb, ub, TN), heads, ss, cols(kd_ref), ubs)
        for h in heads:
            o_ref[:, _hsl(h)] = os_[h]
            s_ref[h] = sn[h]

    blk = pl.BlockSpec((CHUNK, nh * hd), lambda i: (i, 0))
    return pl.pallas_call(
        body, grid=(n,),
        in_specs=[blk] * 5 + [pl.BlockSpec((nh * SUBLANES, LANES), lambda i: (i, 0))],
        out_specs=[blk, pl.BlockSpec((None, nh, hd, hd), lambda i: (i, 0, 0, 0))],
        out_shape=[jax.ShapeDtypeStruct((t, nh * hd), F32), jax.ShapeDtypeStruct((n, nh, hd, hd), F32)],
        scratch_shapes=[pltpu.VMEM((nh, hd, hd), F32)], name=name,
        compiler_params=_params(("arbitrary",)),
    )(u0, wk, qd, kd, qk, gl)


def delta_scan_bwd(do, u0, wk, qd, kd, qk, gl, s_in, *, name):
    t = u0.shape[0]
    nh, hd, n = N_HEADS_A, HEAD_DIM_A, t // CHUNK

    def body(do_ref, u0_ref, wk_ref, qd_ref, kd_ref, qk_ref, gl_ref, sin_ref,
             du0_ref, dwk_ref, dqd_ref, dkd_ref, dqk_ref, dgl_ref, ds_ref):
        @pl.when(pl.program_id(0) == 0)
        def _():
            ds_ref[...] = jnp.zeros_like(ds_ref)

        corner = (lax.broadcasted_iota(jnp.int32, (SUBLANES, LANES), 0) == 0) & (lax.broadcasted_iota(jnp.int32, (SUBLANES, LANES), 1) == 0)
        heads = list(range(nh))
        cols = lambda ref: [ref[:, _hsl(h)].astype(BF16) for h in heads]
        ss, dss = [sin_ref[h] for h in heads], [ds_ref[h] for h in heads]
        sbs, dsbs = _each(lambda s: s.astype(BF16), ss), _each(lambda d: d.astype(BF16), dss)
        dobs, wkbs, qdbs, kdbs, qkbs = cols(do_ref), cols(wk_ref), cols(qd_ref), cols(kd_ref), cols(qk_ref)
        ubs = _each(lambda h, wkb, sb: (u0_ref[:, _hsl(h)] - _dot(wkb, sb)).astype(BF16), heads, wkbs, sbs)
        dus = _each(lambda qkb, dob, kdb, dsb: _dot(qkb, dob, TN) + _dot(kdb, dsb), qkbs, dobs, kdbs, dsbs)
        dubs = _each(lambda du: du.astype(BF16), dus)
        dwks = _each(lambda dub, sb: -_dot(dub, sb, NT), dubs, sbs)
        dqds = _each(lambda dob, sb: _dot(dob, sb, NT), dobs, sbs)
        dkds = _each(lambda ub, dsb: _dot(ub, dsb, NT), ubs, dsbs)
        dqks = _each(lambda dob, ub: _dot(dob, ub, NT), dobs, ubs)
        dgls = _each(lambda s, d: jnp.sum(jnp.sum(s * d, axis=1, keepdims=True), axis=0, keepdims=True), ss, dss)
        dsn = _each(lambda h, d, qdb, dob, wkb, dub: gl_ref[pl.ds(h * SUBLANES, 1), :] * d + _dot(qdb, dob, TN) - _dot(wkb, dub, TN),
                    heads, dss, qdbs, dobs, wkbs, dubs)
        for h in heads:
            du0_ref[:, _hsl(h)] = dus[h]
            dwk_ref[:, _hsl(h)] = dwks[h]
            dqd_ref[:, _hsl(h)] = dqds[h]
            dkd_ref[:, _hsl(h)] = dkds[h]
            dqk_ref[:, _hsl(h)] = dqks[h]
            dgl_ref[h * SUBLANES:(h + 1) * SUBLANES, :] = jnp.where(corner, dgls[h], 0.0)
            ds_ref[h] = dsn[h]

    blk = pl.BlockSpec((CHUNK, nh * hd), lambda i: (n - 1 - i, 0))
    gl_spec = pl.BlockSpec((nh * SUBLANES, LANES), lambda i: (n - 1 - i, 0))
    return pl.pallas_call(
        body, grid=(n,),
        in_specs=[blk] * 6 + [gl_spec, pl.BlockSpec((None, nh, hd, hd), lambda i: (n - 1 - i, 0, 0, 0))],
        out_specs=[blk] * 5 + [gl_spec],
        out_shape=[jax.ShapeDtypeStruct((t, nh * hd), F32)] * 5 + [jax.ShapeDtypeStruct((n * nh * SUBLANES, LANES), F32)],
        scratch_shapes=[pltpu.VMEM((nh, hd, hd), F32)], name=name,
        compiler_params=_params(("arbitrary",)),
    )(do, u0, wk, qd, kd, qk, gl, s_in)


N_PAIRS = N_HEADS_B // 2
PAIRS_PER_KV = N_PAIRS // N_KV_B


def _psl(j):
    return slice(j * LANES, (j + 1) * LANES)


def _att_fn(qps, kc, kp, vc, vp, sinks, kvf, first):
    w = WINDOW
    lane = lax.broadcasted_iota(jnp.int32, (1, LANES), 1)
    lo = (lane < HEAD_DIM_B).astype(F32)
    qi = lax.broadcasted_iota(jnp.int32, (w, w), 0)
    kj = lax.broadcasted_iota(jnp.int32, (w, w), 1)
    dist_c = (qi - kj).astype(F32)
    valid_c = kj <= qi
    valid_p = (kj > qi) & (first < 0.5)
    kcb, kpb, vcb, vpb = (a.astype(BF16) for a in (kc, kp, vc, vp))
    scale = HEAD_DIM_B ** -0.5
    heads = [(j, half) for j in range(PAIRS_PER_KV) for half in range(2)]
    hmasks = [lo if half == 0 else 1.0 - lo for _, half in heads]
    hds = [2.0 * (PAIRS_PER_KV * kvf + j) + half for j, half in heads]
    slopes = _each(lambda hd: jnp.exp(-(hd + 1.0) * (8.0 / N_HEADS_B * math.log(2.0))), hds)
    snks = _each(lambda hd: jnp.sum(sinks * (lane.astype(F32) == hd).astype(F32), axis=1, keepdims=True), hds)
    qhs = _each(lambda jh, hm: (qps[jh[0]] * hm).astype(BF16), heads, hmasks)
    lcs = _each(lambda qh, sl: jnp.where(valid_c, _dot(qh, kcb, NT) * scale - sl * dist_c, NEG), qhs, slopes)
    lps = _each(lambda qh, sl: jnp.where(valid_p, _dot(qh, kpb, NT) * scale - sl * (dist_c + w), NEG), qhs, slopes)
    ms = _each(lambda lc, lp, sk: lax.stop_gradient(jnp.maximum(jnp.maximum(jnp.max(lc, axis=1, keepdims=True),
                                                                            jnp.max(lp, axis=1, keepdims=True)), sk)), lcs, lps, snks)
    ecs = _each(lambda lc, m: jnp.exp(lc - m), lcs, ms)
    eps = _each(lambda lp, m: jnp.exp(lp - m), lps, ms)
    invs = _each(lambda ec, ep, sk, m: 1.0 / (jnp.sum(ec, axis=1, keepdims=True) + jnp.sum(ep, axis=1, keepdims=True) + jnp.exp(sk - m)),
                 ecs, eps, snks, ms)
    ohs = _each(lambda ec, ep, inv, hm: (_dot((ec * inv).astype(BF16), vcb) + _dot((ep * inv).astype(BF16), vpb)) * hm,
                ecs, eps, invs, hmasks)
    return [ohs[2 * j] + ohs[2 * j + 1] for j in range(PAIRS_PER_KV)]


def _scalar11(v):
    return jnp.full((1, 1), v, F32)


def swa_fwd(qsrc, kd, vd, sinks, *, name):
    t = kd.shape[0]
    nb = t // WINDOW

    def body(q_ref, kc_ref, kp_ref, vc_ref, vp_ref, s_ref, o_ref):
        first = _scalar11((pl.program_id(0) == 0).astype(F32))
        kvf = _scalar11(pl.program_id(1).astype(F32))
        outs = _att_fn([q_ref[:, _psl(j)] for j in range(PAIRS_PER_KV)], kc_ref[...], kp_ref[...], vc_ref[...], vp_ref[...],
                       s_ref[...], kvf, first)
        for j in range(PAIRS_PER_KV):
            o_ref[:, _psl(j)] = outs[j].astype(o_ref.dtype)

    cur = pl.BlockSpec((WINDOW, LANES), lambda i, kv: (i, kv))
    prev = pl.BlockSpec((WINDOW, LANES), lambda i, kv: (jnp.maximum(i - 1, 0), kv))
    qs = pl.BlockSpec((WINDOW, PAIRS_PER_KV * LANES), lambda i, kv: (i, kv))
    return pl.pallas_call(
        body, grid=(nb, N_KV_B),
        in_specs=[qs, cur, prev, cur, prev, pl.BlockSpec((1, LANES), lambda i, kv: (0, 0))],
        out_specs=qs, out_shape=jax.ShapeDtypeStruct((t, N_PAIRS * LANES), BF16), name=name,
        compiler_params=_params(("parallel", "parallel")),
    )(qsrc, kd, kd, vd, vd, sinks)


def swa_bwd(do, qsrc, kd, vd, sinks, *, name):
    t = kd.shape[0]
    nb = t // WINDOW

    def body(do_ref, q_ref, kc_ref, kp_ref, vc_ref, vp_ref, s_ref, dq_ref, dk_ref, dv_ref, ds_ref, carry_k, carry_v):
        step, kv = pl.program_id(0), pl.program_id(1)
        first = _scalar11((step == nb - 1).astype(F32))

        @pl.when((step == 0) & (kv == 0))
        def _():
            carry_k[...] = jnp.zeros_like(carry_k)
            carry_v[...] = jnp.zeros_like(carry_v)
            ds_ref[...] = jnp.zeros_like(ds_ref)

        kvf = _scalar11(kv.astype(F32))
        pairs = range(PAIRS_PER_KV)
        f32 = lambda ref: ref[...].astype(F32)
        _, vjp = jax.vjp(lambda *a: _att_fn(*a, kvf, first), [q_ref[:, _psl(j)].astype(F32) for j in pairs],
                         f32(kc_ref), f32(kp_ref), f32(vc_ref), f32(vp_ref), s_ref[...])
        dqs, dkc, dkp, dvc, dvp, dsk = vjp([do_ref[:, _psl(j)].astype(F32) for j in pairs])
        for j in pairs:
            dq_ref[:, _psl(j)] = dqs[j].astype(dq_ref.dtype)
        ds_ref[...] += dsk
        fold = lambda g: g + pltpu.roll(g, HEAD_DIM_B, 1)
        dk_ref[...] = fold(dkc + carry_k[kv]).astype(dk_ref.dtype)
        dv_ref[...] = fold(dvc + carry_v[kv]).astype(dv_ref.dtype)
        carry_k[kv] = dkp
        carry_v[kv] = dvp

    rev = lambda i: nb - 1 - i
    cur = pl.BlockSpec((WINDOW, LANES), lambda i, kv: (rev(i), kv))
    prev = pl.BlockSpec((WINDOW, LANES), lambda i, kv: (jnp.maximum(rev(i) - 1, 0), kv))
    qs = pl.BlockSpec((WINDOW, PAIRS_PER_KV * LANES), lambda i, kv: (rev(i), kv))
    sk = pl.BlockSpec((1, LANES), lambda i, kv: (0, 0))
    return pl.pallas_call(
        body, grid=(nb, N_KV_B),
        in_specs=[qs, qs, cur, prev, cur, prev, sk],
        out_specs=[qs, cur, cur, sk],
        out_shape=[jax.ShapeDtypeStruct((t, N_PAIRS * LANES), BF16), jax.ShapeDtypeStruct((t, N_KV_B * LANES), BF16),
                   jax.ShapeDtypeStruct((t, N_KV_B * LANES), BF16), jax.ShapeDtypeStruct((1, LANES), F32)],
        scratch_shapes=[pltpu.VMEM((N_KV_B, WINDOW, LANES), F32), pltpu.VMEM((N_KV_B, WINDOW, LANES), F32)],
        name=name, compiler_params=_params(("arbitrary", "arbitrary")),
    )(do, qsrc, kd, kd, vd, vd, sinks)


def loss_head(h, tgt, w, *, name):
    t, d = h.shape
    tm = min(256, t)

    def body(h_ref, t_ref, w_ref, dh_ref, dw_ref, l_ref):
        tg = t_ref[...]

        def f(hv, wv):
            err = _f_norm(hv, wv) - tg
            return 0.5 * jnp.sum(jnp.sum(err * err, axis=1, keepdims=True), axis=0, keepdims=True) * (1.0 / d)

        lv, vjp = jax.vjp(f, h_ref[...], w_ref[...])
        dh, dw = vjp(jnp.ones((1, 1), F32))
        dh_ref[...] = dh
        first = pl.program_id(0) == 0

        @pl.when(first)
        def _():
            dw_ref[...] = dw
            l_ref[...] = lv * jnp.ones((1, LANES), F32)

        @pl.when(jnp.logical_not(first))
        def _():
            dw_ref[...] += dw
            l_ref[...] += lv * jnp.ones((1, LANES), F32)

    rows = pl.BlockSpec((tm, d), lambda i: (i, 0))
    one = lambda c: pl.BlockSpec((1, c), lambda i: (0, 0))
    return pl.pallas_call(
        body, grid=(t // tm,), in_specs=[rows, rows, one(d)], out_specs=[rows, one(d), one(LANES)],
        out_shape=[jax.ShapeDtypeStruct((t, d), F32), jax.ShapeDtypeStruct((1, d), F32), jax.ShapeDtypeStruct((1, LANES), F32)],
        name=name, compiler_params=_params(("arbitrary",)),
    )(h, tgt, w)


def _row_tile(r, cap=256):
    tr = r
    if r % SUBLANES == 0:
        for cand in range(SUBLANES, min(r, cap) + 1, SUBLANES):
            if r % cand == 0:
                tr = cand
    return tr


def _adamw_update(wv, gv, mv, vv):
    mn = ADAM_B1 * mv + (1.0 - ADAM_B1) * gv
    vn = ADAM_B2 * vv + (1.0 - ADAM_B2) * jnp.square(gv)
    m_hat = mn / (1.0 - ADAM_B1 ** ADAM_STEP)
    v_hat = vn / (1.0 - ADAM_B2 ** ADAM_STEP)
    return -ADAM_LR * (m_hat / (jnp.sqrt(v_hat) + ADAM_EPS) + ADAM_WD * wv), mn, vn


def adamw_layers(w, gs, m, v, *, name):
    nl, r, c = w.shape
    tr = _row_tile(r)

    def body(*refs):
        w_ref, g_refs, m_ref, v_ref = refs[0], refs[1:1 + nl], refs[1 + nl], refs[2 + nl]
        d_ref, mo_ref, vo_ref, go_ref = refs[3 + nl:]
        layer = pl.program_id(0)
        gv = g_refs[0][...]
        for k in range(1, nl):
            gv = jnp.where(layer == k, g_refs[k][...], gv)
        d_ref[...], mo_ref[...], vo_ref[...] = _adamw_update(w_ref[...], gv, m_ref[...], v_ref[...])
        go_ref[...] = gv

    spec3 = pl.BlockSpec((None, tr, c), lambda k, i: (k, i, 0))
    g_specs = [pl.BlockSpec((tr, c), lambda k, i, q=q: (jnp.where(k == q, i, 0), 0)) for q in range(nl)]
    return pl.pallas_call(
        body, grid=(nl, r // tr), in_specs=[spec3] + g_specs + [spec3, spec3], out_specs=[spec3] * 4,
        out_shape=[jax.ShapeDtypeStruct((nl, r, c), F32)] * 4, name=name, compiler_params=_params(("arbitrary", "arbitrary")),
    )(w, *gs, m, v)


def adamw(w, g, m, v, *, name):
    r, c = w.shape
    tr = _row_tile(r)

    def body(w_ref, g_ref, m_ref, v_ref, d_ref, mo_ref, vo_ref):
        d_ref[...], mo_ref[...], vo_ref[...] = _adamw_update(w_ref[...], g_ref[...], m_ref[...], v_ref[...])

    spec = pl.BlockSpec((tr, c), lambda i: (i, 0))
    return pl.pallas_call(
        body, grid=(r // tr,), in_specs=[spec] * 4, out_specs=[spec] * 3,
        out_shape=[jax.ShapeDtypeStruct((r, c), F32)] * 3, name=name, compiler_params=_params(("parallel",)),
    )(w, g, m, v)


def _place():
    return lax.axis_index("x"), lax.axis_index("y"), lax.axis_index("c")


def allgather8(blk, *, name):
    def body(x_ref, out_ref, send_sems, recv_sems, local_sem):
        x, y, c = _place()
        me = 4 * x + 2 * y + c
        mine = pltpu.make_async_copy(x_ref, out_ref.at[me], local_sem)
        mine.start()
        sent = []
        for k in range(1, N_DEV):
            to = (x ^ ((k >> 2) & 1), y ^ ((k >> 1) & 1), c ^ (k & 1))
            cp = pltpu.make_async_remote_copy(src_ref=x_ref, dst_ref=out_ref.at[me], send_sem=send_sems.at[k - 1],
                                              recv_sem=recv_sems.at[k - 1], device_id=to, device_id_type=MESH)
            cp.start()
            sent.append(cp)
        for k in range(1, N_DEV):
            frm = me ^ k
            pltpu.make_async_remote_copy(src_ref=x_ref, dst_ref=out_ref.at[frm], send_sem=send_sems.at[k - 1],
                                         recv_sem=recv_sems.at[k - 1], device_id=(x, y, c), device_id_type=MESH).wait_recv()
        for cp in sent:
            cp.wait_send()
        mine.wait()

    vm = pl.BlockSpec(memory_space=pltpu.VMEM)
    return pl.pallas_call(
        body, in_specs=[vm], out_specs=vm, out_shape=jax.ShapeDtypeStruct((N_DEV,) + blk.shape, blk.dtype), name=name,
        scratch_shapes=[pltpu.SemaphoreType.DMA((N_DEV - 1,)), pltpu.SemaphoreType.DMA((N_DEV - 1,)), pltpu.SemaphoreType.DMA],
    )(blk)


def _other_chips(x, y):
    return [(1 - x, y), (x, 1 - y), (1 - x, 1 - y)]


def _hbm_call(body, ins, out_shapes, n_sems, name):
    hbm = pl.BlockSpec(memory_space=pl.ANY)
    return pl.pallas_call(
        body, in_specs=[hbm] * len(ins), out_specs=[hbm] * len(out_shapes), out_shape=out_shapes, name=name,
        scratch_shapes=[pltpu.SemaphoreType.DMA((n_sems,)), pltpu.SemaphoreType.DMA((n_sems,))],
    )(*ins)


def _half_rows(c, rh):
    return pl.ds(pl.multiple_of(c * rh, BF16_ROWS), rh)


def gather_units(units, *, name):
    nu = len(units)
    shapes = []
    for arr, layer_major in units:
        r, cols = arr.shape
        shapes.append(jax.ShapeDtypeStruct((2, N_CHIPS, r // 2, cols) if layer_major else (N_CHIPS, r, cols), arr.dtype))

    def body(*refs):
        in_refs, out_refs, send_sems, recv_sems = refs[:nu], refs[nu:2 * nu], refs[2 * nu], refs[2 * nu + 1]
        x, y, c = _place()
        me_chip = 2 * x + y
        sib = (x, y, 1 - c)
        chips = _other_chips(x, y)

        def copy(k, src, dst, to):
            return pltpu.make_async_remote_copy(src_ref=src, dst_ref=dst, send_sem=send_sems.at[k], recv_sem=recv_sems.at[k],
                                                device_id=to, device_id_type=MESH)

        first, passed, landing = [], [], []
        for u, (arr, layer_major) in enumerate(units):
            rh = arr.shape[0] // 2
            out_ref = out_refs[u]
            slot = (lambda chip, half, o=out_ref: o.at[half, chip]) if layer_major else \
                   (lambda chip, half, o=out_ref, rh=rh: o.at[chip, _half_rows(half, rh), :])
            my_half = in_refs[u].at[_half_rows(c, rh), :]
            for j, (cx, cy) in enumerate(chips):
                k = 6 * u + j
                first.append(copy(k, my_half, slot(me_chip, c), (cx, cy, c)))
                passed.append(copy(k + 3, slot(2 * cx + cy, c), slot(2 * cx + cy, c), sib))
                landing.append((copy(k, my_half, slot(2 * cx + cy, c), sib), copy(k + 3, my_half, slot(2 * cx + cy, 1 - c), sib)))
        for cp in first:
            cp.start()
        for (over_ici, _), fwd in zip(landing, passed):
            over_ici.wait_recv()
            fwd.start()
        for _, from_sibling in landing:
            from_sibling.wait_recv()
        for cp in first + passed:
            cp.wait_send()

    return _hbm_call(body, [a for a, _ in units], shapes, 6 * nu, name)


HBM_SPEC = pl.BlockSpec(memory_space=pltpu.HBM)
SEM_SPEC = pl.BlockSpec(memory_space=pltpu.SEMAPHORE)
ORDERED_EFFECT = pltpu.SideEffectType.DATAFLOW_SIDE_EFFECTING


def _split_start(body, srcs, land_shapes, after, *, name):
    nu = len(srcs)
    lands = [lax.empty(s.shape, s.dtype) for s in land_shapes]

    def whole(*refs):
        body(refs[:nu], refs[nu:2 * nu], refs[2 * nu + 1], refs[2 * nu + 2])
        refs[-1][...] = jnp.zeros((SUBLANES, LANES), F32)

    hbm = lambda a: pltpu.with_memory_space_constraint(a, pltpu.HBM)
    sems = pltpu.SemaphoreType.DMA((nu,))
    res = pl.pallas_call(
        whole, name=name, in_specs=[HBM_SPEC] * (2 * nu) + [pl.BlockSpec(memory_space=pl.ANY)],
        out_shape=[sems, sems] + [pltpu.HBM(a.shape, a.dtype) for a in srcs] + [pltpu.HBM(s.shape, s.dtype) for s in land_shapes]
        + [jax.ShapeDtypeStruct((SUBLANES, LANES), F32)],
        out_specs=[SEM_SPEC, SEM_SPEC] + [HBM_SPEC] * (2 * nu) + [pl.BlockSpec(memory_space=pltpu.VMEM)],
        input_output_aliases={q: 2 + q for q in range(2 * nu)},
        compiler_params=pltpu.CompilerParams(has_side_effects=ORDERED_EFFECT),
    )(*[hbm(a) for a in srcs], *[hbm(a) for a in lands], after)
    return res[0], res[1], res[2:2 + nu], res[2 + nu:2 + 2 * nu], res[-1]


def _split_wait(pending, moved, after, *, name):
    send_sems, recv_sems, srcs, lands, _ = pending
    nu = len(srcs)

    def body(*refs):
        land_refs, ssem, rsem = refs[nu:2 * nu], refs[2 * nu], refs[2 * nu + 1]
        x, y, c = _place()
        for u in range(nu):
            size = moved(land_refs[u])
            cp = pltpu.make_async_remote_copy(src_ref=size, dst_ref=size, send_sem=ssem.at[u], recv_sem=rsem.at[u],
                                              device_id=(x, y, c), device_id_type=MESH)
            cp.wait_send()
            cp.wait_recv()

    res = pl.pallas_call(
        body, name=name, in_specs=[HBM_SPEC] * (2 * nu) + [SEM_SPEC, SEM_SPEC, pl.BlockSpec(memory_space=pl.ANY)],
        out_shape=[pltpu.HBM(a.shape, a.dtype) for a in srcs] + [pltpu.HBM(a.shape, a.dtype) for a in lands],
        out_specs=[HBM_SPEC] * (2 * nu), input_output_aliases={q: q for q in range(2 * nu)},
        compiler_params=pltpu.CompilerParams(has_side_effects=ORDERED_EFFECT),
    )(*srcs, *lands, send_sems, recv_sems, after)
    return res[nu:]


def gather_start(shards, after, *, name):
    def body(src_refs, land_refs, send_sems, recv_sems):
        x, y, c = _place()
        for u, shard in enumerate(shards):
            rows = _half_rows(c, shard.shape[0] // 2)
            for cx, cy in _other_chips(x, y):
                for core in range(2):
                    pltpu.make_async_remote_copy(src_ref=src_refs[u].at[rows, :], dst_ref=land_refs[u].at[2 * x + y, rows, :],
                                                 send_sem=send_sems.at[u], recv_sem=recv_sems.at[u], device_id=(cx, cy, core),
                                                 device_id_type=MESH).start()

    return _split_start(body, shards, [jax.ShapeDtypeStruct((N_CHIPS,) + s.shape, s.dtype) for s in shards], after, name=name)


def gather_wait(pending, after, *, name):
    return _split_wait(pending, lambda land: land.at[pl.ds(0, N_CHIPS - 1)], after, name=name)


def scatter_start(pairs, *, name):
    def body(src_refs, land_refs, send_sems, recv_sems):
        x, y, c = _place()
        for u in range(len(pairs)):
            for j, (cx, cy) in enumerate(_other_chips(x, y)):
                pltpu.make_async_remote_copy(src_ref=src_refs[u].at[2 * cx + cy], dst_ref=land_refs[u].at[j], send_sem=send_sems.at[u],
                                             recv_sem=recv_sems.at[u], device_id=(cx, cy, c), device_id_type=MESH).start()

    return _split_start(body, pairs, [jax.ShapeDtypeStruct((N_CHIPS - 1,) + p.shape[1:], p.dtype) for p in pairs], pairs[0], name=name)


def scatter_wait(pending, after, *, name):
    return _split_wait(pending, lambda land: land, after, name=name)


def swap_units(units, *, name):
    nu = len(units)

    def body(*refs):
        g_refs, out_refs, send_sems, recv_sems = refs[:nu], refs[nu:2 * nu], refs[2 * nu], refs[2 * nu + 1]
        x, y, c = _place()
        cps = [pltpu.make_async_remote_copy(src_ref=g_refs[u].at[:, _half_rows(1 - c, units[u].shape[1] // 2), :], dst_ref=out_refs[u],
                                            send_sem=send_sems.at[u], recv_sem=recv_sems.at[u], device_id=(x, y, 1 - c),
                                            device_id_type=MESH) for u in range(nu)]
        for cp in cps:
            cp.start()
        for cp in cps:
            cp.wait()

    shapes = [jax.ShapeDtypeStruct((N_CHIPS, g.shape[1] // 2, g.shape[2]), g.dtype) for g in units]
    return _hbm_call(body, units, shapes, nu, name)


def scatter_units(units, *, name):
    nu = len(units)

    def body(*refs):
        h_refs, out_refs, send_sems, recv_sems = refs[:nu], refs[nu:2 * nu], refs[2 * nu], refs[2 * nu + 1]
        x, y, c = _place()
        cps = [pltpu.make_async_remote_copy(src_ref=h_refs[u].at[2 * cx + cy], dst_ref=out_refs[u].at[j], send_sem=send_sems.at[3 * u + j],
                                            recv_sem=recv_sems.at[3 * u + j], device_id=(cx, cy, c), device_id_type=MESH)
               for u in range(nu) for j, (cx, cy) in enumerate(_other_chips(x, y))]
        for cp in cps:
            cp.start()
        for cp in cps:
            cp.wait()

    shapes = [jax.ShapeDtypeStruct((3,) + h.shape[1:], h.dtype) for h in units]
    return _hbm_call(body, units, shapes, 3 * nu, name)


def join_units(units, *, name):
    nu = len(units)

    def body(*refs):
        h_refs, out_refs, send_sems, recv_sems = refs[:nu], refs[nu:2 * nu], refs[2 * nu], refs[2 * nu + 1]
        x, y, c = _place()
        cps = [pltpu.make_async_remote_copy(src_ref=h_refs[u], dst_ref=out_refs[u], send_sem=send_sems.at[u], recv_sem=recv_sems.at[u],
                                            device_id=(x, y, 1 - c), device_id_type=MESH) for u in range(nu)]
        for cp in cps:
            cp.start()
        for cp in cps:
            cp.wait()

    return _hbm_call(body, units, [jax.ShapeDtypeStruct(h.shape, h.dtype) for h in units], nu, name)


def add_parts(parts, *, out_dtype, name):
    parts = [p if isinstance(p, tuple) else (p, None) for p in parts]
    rows, cols = parts[0][0].shape[-2:]
    tr = rows
    for cand in range(BF16_ROWS, min(rows, 512) + 1, BF16_ROWS):
        if rows % cand == 0:
            tr = cand

    def body(*refs):
        acc = refs[0][...].astype(F32)
        for r in refs[1:-1]:
            acc = acc + r[...].astype(F32)
        refs[-1][...] = acc.astype(refs[-1].dtype)

    spec = pl.BlockSpec((tr, cols), lambda i: (i, 0))
    in_specs = [spec if q is None else pl.BlockSpec((None, tr, cols), lambda i, q=q: (q, i, 0)) for _, q in parts]
    return pl.pallas_call(
        body, grid=(rows // tr,), in_specs=in_specs, out_specs=spec,
        out_shape=jax.ShapeDtypeStruct((rows, cols), out_dtype), name=name, compiler_params=_params(("parallel",)),
    )(*[a for a, _ in parts])


def sum8(g, *, name):
    def body(g_ref, o_ref):
        acc = g_ref[0]
        for d in range(1, N_DEV):
            acc = acc + g_ref[d]
        o_ref[...] = acc

    return pl.pallas_call(body, out_shape=jax.ShapeDtypeStruct(g.shape[1:], F32), name=name)(g)


def _dup_halves(a):
    t = a.shape[0]
    a = a.reshape(t, N_KV_B, HEAD_DIM_B)
    return jnp.concatenate([a, a], axis=-1).reshape(t, N_KV_B * LANES)


def _undup(a):
    t = a.shape[0]
    return a.reshape(t, N_KV_B, LANES)[:, :, :HEAD_DIM_B].reshape(t, N_KV_B * HEAD_DIM_B)


def _lane_pad(v, offset=0):
    return jnp.zeros((1, LANES), F32).at[0, offset:offset + v.shape[0]].set(v)


SHARD_UP = 2 * D_FF // N_CHIPS
SHARD_BIN = (N_HEADS_B + 2 * N_KV_B) * HEAD_DIM_B // N_CHIPS
SHARD_PROJ = D_MODEL // N_CHIPS


def local_step(x, p, tgt, sm, weight, on_grads):
    t = x.shape[0]
    rtm = min(256, t)
    hk = N_HEADS_A * HEAD_DIM_A
    qd_b = N_HEADS_B * HEAD_DIM_B
    kd_b = N_KV_B * HEAD_DIM_B
    gs = {}
    norm = lambda h, w, nm: tile_map(_f_norm, [(h, D_MODEL, 0)], [w], [(D_MODEL, BF16)], tm=rtm, ncol=1, name=nm)[0]

    def norm_bwd(h, w, dy, add, nm):
        (dh,), (dw,) = tile_vjp(_f_norm, [(h, D_MODEL, 0)], [w], [(dy, D_MODEL, 0)], n_diff=1, tm=rtm, ncol=1, name=nm,
                                add=(add, D_MODEL, 0))
        return dh, dw

    spec = pl.BlockSpec
    mtm = _tile(D_MODEL, MM_TM_CAP)
    p_bf = p.astype(BF16)
    alog_p = _lane_pad(sm["a_log"][0], N_HEADS_A)
    dtb_p = _lane_pad(sm["a_dt_bias"][0], N_HEADS_A)
    sinks_p = _lane_pad(sm["b_sinks"][0])
    nw = lambda name, i: sm[name][i:i + 1]
    by_chip = lambda kdim, ns: dict(tn=ns, tk=kdim, b_spec=spec((None, kdim, ns), lambda r, j, kk: (j, kk, 0)))
    by_chip_t = lambda ndim, ns: dict(n=ndim, tn=ndim, tk=ns, b_spec=spec((None, ndim, ns), lambda r, j, kk: (kk, j, 0)))
    cache = {}

    def wgt(name, i, after):
        if (name, i) not in cache:
            cache[name, i] = weight(name, i, after)
        return cache[name, i]

    saved = []
    h = x
    hn_next = norm(h, nw("norm_mix", 0), "norm_mix0")
    for i in range(DEPTH):
        s = {"h0": h, "hn": hn_next}
        if i % 2 == 0:
            s["pm"] = mm(s["hn"], wgt("a_main", i, h), name="a_in_main")
            s["pt"] = mm(s["hn"], wgt("a_tail", i, h), name="a_in_tail")
            s["c"] = conv_fwd(s["pm"], wgt("a_conv", i, h), name="a_conv")
            s["bg"] = tile_map(_f_betag, [(s["pt"], LANES, 0)], [alog_p, dtb_p], [(LANES, F32)], tm=rtm, ncol=1, name="a_betag")[0]
            s["prep"] = delta_prep(s["c"], s["bg"], name="a_prep")
            s["o"], s["s_in"] = delta_scan(*s["prep"], name="a_scan")
            s["on"] = gnorm_fwd(s["o"], s["pm"], sm["a_norm"], name="a_gnorm")
            h, s["hf"] = mm(s["on"], wgt("a_w_out", i, s["on"]), add=h, norm_w=nw("norm_ffn", i), name="a_out")
        else:
            s["pb"] = mm(s["hn"], wgt("b_w_in", i, s["hn"]), name="b_in", out_dtype=BF16, n=N_CHIPS * SHARD_BIN,
                         **by_chip(D_MODEL, SHARD_BIN))
            s["kd"], s["vd"] = _dup_halves(s["pb"][:, qd_b:qd_b + kd_b]), _dup_halves(s["pb"][:, qd_b + kd_b:])
            s["ao"] = swa_fwd(s["pb"], s["kd"], s["vd"], sinks_p, name="b_att")
            h, s["hf"] = mm(s["ao"], wgt("b_w_out", i, s["ao"]), add=h, norm_w=nw("norm_ffn", i), name="b_out")
        s["h1"] = h
        s["u"] = mm(s["hf"], wgt("f_w_up", i, s["hf"]), name=f"f_up{i}", n=2 * D_FF, **by_chip(D_MODEL, SHARD_UP))
        s["act"] = conv_act_fwd(s["u"], wgt("f_conv", i, s["hf"]), name=f"f_conv_act{i}")
        h, s["hp"] = mm(s["act"], wgt("f_w_down", i, s["act"]), add=h, norm_w=nw("norm_ple", i), name=f"f_down{i}")
        s["h2"] = h
        s["gl"] = mm(s["hp"], wgt("ple_w_gate", i, s["hp"]), name=f"ple_gate{i}")
        s["pe"] = mm(p_bf[i], wgt("ple_w_proj", i, s["hp"]), name=f"ple_proj{i}", n=D_MODEL, **by_chip(PLE_DIM, SHARD_PROJ))
        rows3 = [(h, D_MODEL, 0), (s["gl"], D_MODEL, 0), (s["pe"], D_MODEL, 0)]
        if i + 1 < DEPTH:
            def mix_norm(hv, g, e, wn):
                hn = hv + _f_ple(g, e)
                return hn, _f_norm(hn, wn)
            h, hn_next = tile_map(mix_norm, rows3, [nw("norm_mix", i + 1)], [(D_MODEL, F32), (D_MODEL, BF16)], tm=rtm, ncol=1,
                                  name=f"ple_mix{i}")
        else:
            h = tile_map(lambda hv, g, e: hv + _f_ple(g, e), rows3, [], [(D_MODEL, F32)], tm=rtm, ncol=1, name=f"ple_mix{i}")[0]
        saved.append(s)

    dh, gnf, loss = loss_head(h, tgt, sm["norm_final"][None, :], name="loss_head")
    gs["norm_final"] = gnf[0]

    g_mix, g_ffn, g_ple, g_conv = ([None] * DEPTH for _ in range(4))
    zero = jnp.zeros((1, 1), F32)
    for i in reversed(range(DEPTH)):
        s, gw = saved[i], {}
        by_rows = lambda g: g.reshape(N_CHIPS, g.shape[0] // N_CHIPS, g.shape[1])
        (dgl, dpe), _ = tile_vjp(_f_ple, [(s["gl"], D_MODEL, 0), (s["pe"], D_MODEL, 0)], [], [(dh, D_MODEL, 0)], n_diff=2,
                                 tm=rtm, ncol=1, name=f"ple_mix_bwd{i}", grad_dtypes=[BF16, BF16])
        gw["ple_w_proj"] = mm(p_bf[i], dpe, ta=True, name=f"ple_proj_dw{i}", out_dtype=BF16, tn=SHARD_PROJ,
                              o_shape=(N_CHIPS, PLE_DIM, SHARD_PROJ), o_spec=spec((None, PLE_DIM, SHARD_PROJ), lambda r, j, kk: (j, r, 0)))
        gw["ple_w_gate"] = by_rows(mm(s["hp"], dgl, ta=True, name=f"ple_gate_dw{i}", out_dtype=BF16))
        dhp = mm(dgl, cache["ple_w_gate", i], tb=True, name=f"ple_gate_dx{i}")
        dh, g_ple[i] = norm_bwd(s["h2"], nw("norm_ple", i) + zero, dhp, dh, f"norm_ple_bwd{i}")

        dact = mm(dh, cache["f_w_down", i], tb=True, name=f"f_down_dx{i}")
        gw["f_w_down"] = by_rows(mm(s["act"], dh, ta=True, name=f"f_down_dw{i}", out_dtype=BF16, tm_cap=D_FF // 2))
        du_halves = conv_act_bwd(s["u"], dact, cache["f_conv", i], name=f"f_conv_act_bwd{i}")
        g_conv[i] = jnp.concatenate(du_halves[2:], axis=1)
        dhf = g_up = None
        for half, du in enumerate(du_halves[:2]):
            c0 = half * (N_CHIPS // 2)
            g_up = mm(s["hf"], du, ta=True, name=f"f_up_dw{i}_{half}", out_dtype=BF16, tn=SHARD_UP, into=g_up,
                      o_shape=(N_CHIPS, D_MODEL, SHARD_UP), o_spec=spec((None, mtm, SHARD_UP), lambda r, j, kk, c0=c0: (c0 + j, r, 0)))
            dhf = mm(du, cache["f_w_up", i], tb=True, name=f"f_up_dx{i}_{half}", n=D_MODEL, tn=D_MODEL, tk=SHARD_UP, add=dhf,
                     b_spec=spec((None, D_MODEL, SHARD_UP), lambda r, j, kk, c0=c0: (c0 + kk, j, 0)))
        gw["f_w_up"] = g_up
        dh, g_ffn[i] = norm_bwd(s["h1"], nw("norm_ffn", i), dhf, dh, f"norm_ffn_bwd{i}")

        if i % 2 == 0:
            don = mm(dh, cache["a_w_out", i], tb=True, name="a_out_dx")
            gw["a_w_out"] = by_rows(mm(s["on"], dh, ta=True, name="a_out_dw", out_dtype=BF16))
            do, dz, gs["a_norm"] = gnorm_bwd(s["o"], s["pm"], sm["a_norm"], don, name="a_gnorm_bwd")
            dprep = delta_scan_bwd(do, *s["prep"], s["s_in"], name="a_scan_bwd")
            dc, dbg = delta_prep_bwd(s["c"], s["bg"], dprep, name="a_prep_bwd")
            (dpt,), (galog, gdtb) = tile_vjp(_f_betag, [(s["pt"], LANES, 0)], [alog_p, dtb_p], [(dbg, LANES, 0)], n_diff=1,
                                             tm=rtm, ncol=1, name="a_betag_bwd", grad_dtypes=[BF16])
            gs["a_log"] = galog[:, N_HEADS_A:2 * N_HEADS_A]
            gs["a_dt_bias"] = gdtb[:, N_HEADS_A:2 * N_HEADS_A]
            dqkv, gs["a_conv"] = conv_bwd(dc, s["pm"], cache["a_conv", i], name="a_conv_bwd")
            dpm = jnp.concatenate([dqkv, dz], axis=1)
            dhn = mm(dpm, cache["a_main", i], tb=True, name="a_in_main_dx")
            dhn = mm(dpt, cache["a_tail", i], tb=True, add=dhn, name="a_in_tail_dx")
            g_main = mm(s["hn"], dpm, ta=True, name="a_in_main_dw", out_dtype=BF16)
            g_tail = mm(s["hn"], dpt, ta=True, name="a_in_tail_dw", out_dtype=BF16)
            g_in = jnp.concatenate([g_main, g_tail[:, :2 * N_HEADS_A]], axis=1)
            gw["a_w_in"] = g_in.reshape(D_MODEL, N_CHIPS, g_in.shape[1] // N_CHIPS).transpose(1, 0, 2)
        else:
            dao = mm(dh, cache["b_w_out", i], tb=True, name="b_out_dx")
            gw["b_w_out"] = by_rows(mm(s["ao"], dh, ta=True, name="b_out_dw", out_dtype=BF16))
            dq, dkd, dvd, gsk = swa_bwd(dao, s["pb"], s["kd"], s["vd"], sinks_p, name="b_att_bwd")
            gs["b_sinks"] = gsk[:, :N_HEADS_B]
            dpb = jnp.concatenate([dq, _undup(dkd), _undup(dvd)], axis=1)
            dhn = mm(dpb, cache["b_w_in", i], tb=True, name="b_in_dx", **by_chip_t(D_MODEL, SHARD_BIN))
            gw["b_w_in"] = mm(s["hn"], dpb, ta=True, name="b_in_dw", out_dtype=BF16, tn=SHARD_BIN,
                              o_shape=(N_CHIPS, D_MODEL, SHARD_BIN), o_spec=spec((None, mtm, SHARD_BIN), lambda r, j, kk: (j, r, 0)))
        dh, g_mix[i] = norm_bwd(s["h0"], nw("norm_mix", i), dhn, dh, f"norm_mix_bwd{i}")
        token = on_grads(i, gw)
        if token is not None:
            zero = token[:1, :1]

    gs["norm_mix"], gs["norm_ffn"], gs["norm_ple"] = (jnp.concatenate(g, axis=0) for g in (g_mix, g_ffn, g_ple))
    gs["f_conv"] = jnp.stack(g_conv)
    return loss, dh, gs


BIG = ["a_w_in", "a_w_out", "b_w_in", "b_w_out", "f_w_up", "f_w_down", "ple_w_proj", "ple_w_gate"]
LAYERED = {"f_w_up", "f_w_down", "ple_w_proj", "ple_w_gate"}
BY_CHIP = {"b_w_in", "f_w_up", "ple_w_proj"}
LAYER_UNITS = [[("a_w_in", 0), ("a_w_out", 0)] + [(n, 0) for n in sorted(LAYERED)],
               [("b_w_in", 1), ("b_w_out", 1)] + [(n, 1) for n in sorted(LAYERED)]]
CONVS = ["a_conv", "f_conv"]
SMALL = ["norm_mix", "norm_ffn", "norm_ple", "norm_final", "a_log", "a_dt_bias", "a_norm", "b_sinks"]
SMALL_ROWS = 8
CONV_ROWS = 16
CONV_GRAD_ROWS = 48


def _pack_rows(arrs, rows, dtype):
    flat = jnp.concatenate([a.reshape(-1).astype(dtype) for a in arrs])
    return jnp.pad(flat, (0, rows * PACK_COLS - flat.shape[0])).reshape(rows, PACK_COLS)


def _unpack(flat, shapes):
    out, off = [], 0
    for shp in shapes:
        n = math.prod(shp)
        out.append(flat[off:off + n].reshape(shp))
        off += n
    return out


def _pack_small(d, loss=None):
    tail = jnp.concatenate([d["a_log"].reshape(-1), d["a_dt_bias"].reshape(-1), d["a_norm"].reshape(-1), d["b_sinks"].reshape(-1)])
    if loss is not None:
        tail = jnp.concatenate([tail, loss.reshape(-1)[:1]])
    tail = jnp.pad(tail, (0, PACK_COLS - tail.shape[0]))
    return jnp.concatenate([d["norm_mix"], d["norm_ffn"], d["norm_ple"], d["norm_final"][None, :], tail[None, :]], axis=0)


def _unpack_small(a, like):
    out = {"norm_mix": a[0:2], "norm_ffn": a[2:4], "norm_ple": a[4:6], "norm_final": a[6]}
    off = 0
    for nm in ("a_log", "a_dt_bias", "a_norm", "b_sinks"):
        n = like[nm].size
        out[nm] = a[7, off:off + n].reshape(like[nm].shape)
        off += n
    return out, a[7, off]


def _as2d(a):
    return a.reshape(-1, a.shape[-1])


def kernel(x, p, norm_mix, norm_ffn, norm_ple, norm_final, a_w_in, a_conv, a_log, a_dt_bias, a_norm, a_w_out, b_w_in, b_sinks, b_w_out, f_w_up, f_conv, f_w_down, ple_w_proj, ple_w_gate, loss_target, m_norm_mix, m_norm_ffn, m_norm_ple, m_norm_final, m_a_w_in, m_a_conv, m_a_log, m_a_dt_bias, m_a_norm, m_a_w_out, m_b_w_in, m_b_sinks, m_b_w_out, m_f_w_up, m_f_conv, m_f_w_down, m_ple_w_proj, m_ple_w_gate, v_norm_mix, v_norm_ffn, v_norm_ple, v_norm_final, v_a_w_in, v_a_conv, v_a_log, v_a_dt_bias, v_a_norm, v_a_w_out, v_b_w_in, v_b_sinks, v_b_w_out, v_f_w_up, v_f_conv, v_f_w_down, v_ple_w_proj, v_ple_w_gate):
    w = dict(norm_mix=norm_mix, norm_ffn=norm_ffn, norm_ple=norm_ple, norm_final=norm_final, a_w_in=a_w_in, a_conv=a_conv,
             a_log=a_log, a_dt_bias=a_dt_bias, a_norm=a_norm, a_w_out=a_w_out, b_w_in=b_w_in, b_sinks=b_sinks, b_w_out=b_w_out,
             f_w_up=f_w_up, f_conv=f_conv, f_w_down=f_w_down, ple_w_proj=ple_w_proj, ple_w_gate=ple_w_gate)
    m = dict(norm_mix=m_norm_mix, norm_ffn=m_norm_ffn, norm_ple=m_norm_ple, norm_final=m_norm_final, a_w_in=m_a_w_in,
             a_conv=m_a_conv, a_log=m_a_log, a_dt_bias=m_a_dt_bias, a_norm=m_a_norm, a_w_out=m_a_w_out, b_w_in=m_b_w_in,
             b_sinks=m_b_sinks, b_w_out=m_b_w_out, f_w_up=m_f_w_up, f_conv=m_f_conv, f_w_down=m_f_w_down,
             ple_w_proj=m_ple_w_proj, ple_w_gate=m_ple_w_gate)
    v = dict(norm_mix=v_norm_mix, norm_ffn=v_norm_ffn, norm_ple=v_norm_ple, norm_final=v_norm_final, a_w_in=v_a_w_in,
             a_conv=v_a_conv, a_log=v_a_log, a_dt_bias=v_a_dt_bias, a_norm=v_a_norm, a_w_out=v_a_w_out, b_w_in=v_b_w_in,
             b_sinks=v_b_sinks, b_w_out=v_b_w_out, f_w_up=v_f_w_up, f_conv=v_f_conv, f_w_down=v_f_w_down,
             ple_w_proj=v_ple_w_proj, ple_w_gate=v_ple_w_gate)
    xc, yc, cc = _place()
    my_chip = 2 * xc + yc

    shard = {(n, i): w[n][i if n in LAYERED else 0].astype(BF16) for n, i in LAYER_UNITS[0] + LAYER_UNITS[1]}
    first = shard["a_w_in", 0]
    (ga,) = gather_units([(first, False)], name="gather_first")
    ga = lax.dynamic_update_index_in_dim(ga, first, my_chip, 0)
    a_in = jnp.concatenate([ga[j] for j in range(N_CHIPS)], axis=1)
    n_main = 4 * N_HEADS_A * HEAD_DIM_A
    conv_shapes = [w[n].shape for n in CONVS]
    convs = allgather8(_pack_rows([w[n] for n in CONVS], CONV_ROWS, F32), name="gather_convs")
    conv_parts = [_unpack(convs[2 * j].reshape(-1), conv_shapes) for j in range(N_CHIPS)]
    a_conv_full, f_conv_full = (jnp.concatenate([conv_parts[j][q] for j in range(N_CHIPS)], axis=2) for q in range(2))
    ready = {("a_main", 0): a_in[:, :n_main], ("a_tail", 0): jnp.pad(a_in[:, n_main:], ((0, 0), (0, LANES - 2 * N_HEADS_A))),
             ("a_conv", 0): a_conv_full[0], ("f_conv", 0): f_conv_full[0], ("f_conv", 1): f_conv_full[1]}
    later = [[k for k in units if k != ("a_w_in", 0)] for units in LAYER_UNITS]
    pending, after = [], ga
    for layer, keys in enumerate(later):
        pending.append(gather_start([shard[k] for k in keys], after, name=f"gather_start{layer}"))
        after = pending[-1][4]
    sm = {n: w[n] for n in SMALL}
    sm["norm_mix"] = sm["norm_mix"] + after[:1, :1]

    def weight(name, layer, act):
        if (name, layer) not in ready:
            landed = gather_wait(pending[layer], act, name=f"gather_wait{layer}")
            for k, g in zip(later[layer], landed):
                g = lax.dynamic_update_index_in_dim(g, shard[k], my_chip, 0)
                ready[k] = g if k[0] in BY_CHIP else g.reshape(N_CHIPS * g.shape[1], g.shape[2])
        return ready[name, layer]

    pairs, scattered = {}, {}

    def on_grads(layer, gw):
        keys = LAYER_UNITS[layer]
        from_sib = swap_units([gw[n] for n, _ in keys], name=f"rs_swap{layer}")
        for (n, _), sib in zip(keys, from_sib):
            rh, cols = sib.shape[1:]
            mine = lax.dynamic_slice_in_dim(gw[n], cc * rh, rh, axis=1)
            pairs[n, layer] = add_parts([mine.reshape(N_CHIPS * rh, cols), sib.reshape(N_CHIPS * rh, cols)], out_dtype=BF16,
                                        name=f"rs_add_pair_{n}{layer}").reshape(N_CHIPS, rh, cols)
        if layer == 0:
            scattered.update(zip(keys, scatter_units([pairs[k] for k in keys], name="rs_scatter0")))
            return None
        scattered[layer] = scatter_start([pairs[k] for k in keys], name=f"rs_scatter_start{layer}")
        return scattered[layer][4]

    loss, grad_x, gs = local_step(x[0], p[:, 0], loss_target[0], sm, weight, on_grads)

    for layer in range(1, DEPTH):
        scattered.update(zip(LAYER_UNITS[layer], scatter_wait(scattered.pop(layer), grad_x, name=f"rs_scatter_wait{layer}")))
    all_units = LAYER_UNITS[0] + LAYER_UNITS[1]
    halves = [add_parts([lax.dynamic_index_in_dim(pairs[k], my_chip, axis=0, keepdims=False)] + [(scattered[k], j) for j in range(N_CHIPS - 1)],
                        out_dtype=F32, name=f"rs_add_chips_{k[0]}{k[1]}") for k in all_units]
    from_sib = join_units(halves, name="rs_join")
    g_unit = {k: jnp.where(cc == 0, jnp.concatenate([hf, ot]), jnp.concatenate([ot, hf])) for k, hf, ot in zip(all_units, halves, from_sib)}

    conv_grads = _pack_rows([gs[n] for n in CONVS], CONV_GRAD_ROWS, F32)
    small_sum = sum8(allgather8(jnp.concatenate([_pack_small(gs, loss), conv_grads]), name="gather_small"), name="sum_small")
    g_sm, loss_sum = _unpack_small(small_sum[:SMALL_ROWS], sm)

    grads, delta, new_m, new_v = {}, {}, {}, {}
    for n in BIG:
        g_layers = [g_unit[n, i] for i in range(DEPTH) if (n, i) in g_unit]
        shape3 = (len(g_layers),) + g_layers[0].shape
        res = adamw_layers(w[n].reshape(shape3), g_layers, m[n].reshape(shape3), v[n].reshape(shape3), name=f"adamw_{n}")
        delta[n], new_m[n], new_v[n], grads[n] = (r.reshape(w[n].shape) for r in res)
    for n, full in zip(CONVS, _unpack(small_sum[SMALL_ROWS:].reshape(-1), [gs[n].shape for n in CONVS])):
        g2 = _as2d(lax.dynamic_slice_in_dim(full, my_chip * w[n].shape[-1], w[n].shape[-1], axis=full.ndim - 1))
        d2, m2, v2 = adamw(_as2d(w[n]), g2, _as2d(m[n]), _as2d(v[n]), name=f"adamw_{n}")
        grads[n], delta[n], new_m[n], new_v[n] = (r.reshape(w[n].shape) for r in (g2, d2, m2, v2))
    pk = lambda d: _pack_small(d)
    d2, m2, v2 = adamw(pk(sm), pk(g_sm), pk({n: m[n] for n in SMALL}), pk({n: v[n] for n in SMALL}), name="adamw_small")
    for src, dst in ((d2, delta), (m2, new_m), (v2, new_v)):
        dst.update(_unpack_small(src, sm)[0])
    grads.update(g_sm)

    order = ["norm_mix", "norm_ffn", "norm_ple", "norm_final", "a_w_in", "a_conv", "a_log", "a_dt_bias", "a_norm", "a_w_out",
             "b_w_in", "b_sinks", "b_w_out", "f_w_up", "f_conv", "f_w_down", "ple_w_proj", "ple_w_gate"]
    return (loss_sum, grad_x[None], *[grads[n] for n in order], *[delta[n] for n in order],
            *[new_m[n] for n in order], *[new_v[n] for n in order])
```

```python
import functools
import math

import jax
import jax.numpy as jnp
from jax import lax
from jax.experimental import pallas as pl
from jax.experimental.pallas import tpu as pltpu

F32 = jnp.float32
BF16 = jnp.bfloat16
MESH = pl.DeviceIdType.MESH

D_MODEL = 1024
N_HEADS_A = 8
HEAD_DIM_A = 128
CONV_A = 4
N_HEADS_B = 16
N_KV_B = 4
HEAD_DIM_B = 64
WINDOW = 128
D_FF = 2816
FFN_CONV = 3
PLE_DIM = 256
EPS = 1e-6
DEPTH = 2

ADAM_LR = 0.001
ADAM_B1 = 0.9
ADAM_B2 = 0.999
ADAM_EPS = 1e-08
ADAM_WD = 0.01
ADAM_STEP = 10

LANES = 128
SUBLANES = 8
BF16_ROWS = 16
CHUNK = 128
VMEM_LIMIT = 56 * 1024 * 1024
NEG = -1e30
N_CHIPS = 4
N_DEV = 8
PACK_COLS = 1024


def _params(sem=None):
    return pltpu.CompilerParams(dimension_semantics=sem, vmem_limit_bytes=VMEM_LIMIT)


def _tile(dim, cap):
    if dim % LANES:
        return dim
    best = LANES
    for t in range(LANES, min(dim, cap) + 1, LANES):
        if dim % t == 0:
            best = t
    return best


def _dot(a, b, dims=(((1,), (0,)), ((), ())), precision=None):
    return lax.dot_general(a, b, dims, precision=precision, preferred_element_type=F32)


NN = (((1,), (0,)), ((), ()))
NT = (((1,), (1,)), ((), ()))
TN = (((0,), (0,)), ((), ()))


MM_TM_CAP = 1024


def mm(a, b, *, name, ta=False, tb=False, out_dtype=F32, add=None, norm_w=None, tm_cap=MM_TM_CAP, tn_cap=1408, tk_cap=1408,
       n=None, tn=None, tk=None, b_spec=None, o_spec=None, o_shape=None, into=None):
    m, k = (a.shape[1], a.shape[0]) if ta else a.shape
    if b_spec is None:
        n = b.shape[0] if tb else b.shape[1]
        assert (b.shape[1] if tb else b.shape[0]) == k, (a.shape, b.shape, ta, tb)
    tm, tn, tk = _tile(m, tm_cap), tn or _tile(n, tn_cap), tk or _tile(k, tk_cap)
    assert n % tn == 0 and k % tk == 0, (n, tn, k, tk)
    nk = k // tk
    dims = (((0 if ta else 1,), (1 if tb else 0,)), ((), ()))
    has_add, has_norm = add is not None, norm_w is not None
    assert not has_norm or (tn == n and o_spec is None), "the norm epilogue needs whole rows"
    n_in = 2 + has_add + has_norm + (into is not None)

    def body(*refs):
        a_ref, b_ref = refs[0], refs[1]
        add_ref = refs[2] if has_add else None
        o_ref = refs[n_in]
        part = _dot(a_ref[...].astype(BF16), b_ref[...].astype(BF16), dims)

        def finish(r):
            if has_add:
                r = r + add_ref[...].astype(F32)
            o_ref[...] = r.astype(o_ref.dtype)
            if has_norm:
                refs[n_in + 1][...] = _f_norm(r, refs[2 + has_add][...]).astype(BF16)

        if nk == 1:
            finish(part)
            return
        acc = refs[-1]
        kk = pl.program_id(2)

        @pl.when(kk == 0)
        def _():
            acc[...] = part

        @pl.when(kk > 0)
        def _():
            acc[...] += part

        @pl.when(kk == nk - 1)
        def _():
            finish(acc[...])

    a_spec = pl.BlockSpec((tk, tm), lambda i, j, kk: (kk, i)) if ta else pl.BlockSpec((tm, tk), lambda i, j, kk: (i, kk))
    if b_spec is None:
        b_spec = pl.BlockSpec((tn, tk), lambda i, j, kk: (j, kk)) if tb else pl.BlockSpec((tk, tn), lambda i, j, kk: (kk, j))
    plain_o = pl.BlockSpec((tm, tn), lambda i, j, kk: (i, j))
    if o_spec is None:
        o_spec, o_shape = plain_o, (m, n)
    in_specs = [a_spec, b_spec] + ([plain_o] if has_add else [])
    args = (a, b) + ((add,) if has_add else ())
    out_specs, out_shapes = o_spec, jax.ShapeDtypeStruct(tuple(o_shape), out_dtype)
    if has_norm:
        in_specs.append(pl.BlockSpec((1, n), lambda i, j, kk: (0, 0)))
        args += (norm_w,)
        out_specs, out_shapes = [o_spec, plain_o], [out_shapes, jax.ShapeDtypeStruct((m, n), BF16)]
    aliases = {}
    if into is not None:
        assert into.shape == tuple(o_shape) and into.dtype == out_dtype, (into.shape, o_shape)
        in_specs.append(pl.BlockSpec(memory_space=pl.ANY))
        args += (into,)
        aliases = {n_in - 1: 0}
    return pl.pallas_call(
        body, grid=(m // tm, n // tn, nk), in_specs=in_specs, out_specs=out_specs,
        out_shape=out_shapes, name=name, input_output_aliases=aliases,
        scratch_shapes=[pltpu.VMEM((tm, tn), F32)] if nk > 1 else [],
        compiler_params=_params(("parallel", "parallel", "arbitrary")),
    )(*args)


def _row_spec(tm, cw, coff):
    return pl.BlockSpec((tm, cw), lambda i, j: (i, j + coff))


def _full_spec(shape):
    return pl.BlockSpec(shape, lambda i, j: (0,) * len(shape))


def tile_map(fn, rows, params, outs, *, tm, ncol, name):
    t = rows[0][0].shape[0]
    nin = len(rows) + len(params)

    def body(*refs):
        res = fn(*[r[...] for r in refs[:nin]])
        res = res if isinstance(res, (tuple, list)) else (res,)
        for o_ref, r in zip(refs[nin:], res):
            o_ref[...] = r.astype(o_ref.dtype)

    in_specs = [_row_spec(tm, cw, coff) for (_, cw, coff) in rows] + [_full_spec(p.shape) for p in params]
    res = pl.pallas_call(
        body, grid=(t // tm, ncol), in_specs=in_specs,
        out_specs=[_row_spec(tm, cw, 0) for (cw, _) in outs],
        out_shape=[jax.ShapeDtypeStruct((t, cw * ncol), dt) for (cw, dt) in outs], name=name,
        compiler_params=_params(("parallel", "parallel")),
    )(*[r[0] for r in rows], *params)
    return res


def tile_vjp(fn, rows, params, cts, *, n_diff, tm, ncol, name, add=None, grad_dtypes=None):
    t = rows[0][0].shape[0]
    nr, npar, nct = len(rows), len(params), len(cts)
    has_add = add is not None

    def body(*refs):
        vals = [r[...] for r in refs[:nr + npar + nct + (1 if has_add else 0)]]
        diff, rest, pars = vals[:n_diff], vals[n_diff:nr], vals[nr:nr + npar]
        ctv = vals[nr + npar:nr + npar + nct]
        outs_ref = refs[nr + npar + nct + (1 if has_add else 0):]

        def f(*a):
            res = fn(*a[:n_diff], *rest, *a[n_diff:])
            return tuple(res) if isinstance(res, (tuple, list)) else (res,)

        primal, vjp = jax.vjp(f, *[d.astype(F32) for d in diff], *pars)
        grads = vjp(tuple(c.astype(o.dtype) for c, o in zip(ctv, primal)))
        for q in range(n_diff):
            g = grads[q]
            if has_add and q == 0:
                g = g + vals[-1]
            outs_ref[q][...] = g.astype(outs_ref[q].dtype)
        first = (pl.program_id(0) == 0) & (pl.program_id(1) == 0)
        for q in range(npar):
            o_ref, g = outs_ref[n_diff + q], grads[n_diff + q]

            @pl.when(first)
            def _(o_ref=o_ref, g=g):
                o_ref[...] = g

            @pl.when(jnp.logical_not(first))
            def _(o_ref=o_ref, g=g):
                o_ref[...] += g

    ins = list(rows) + [None] * 0
    in_specs = [_row_spec(tm, cw, coff) for (_, cw, coff) in rows] + [_full_spec(p.shape) for p in params]
    in_specs += [_row_spec(tm, cw, coff) for (_, cw, coff) in cts]
    args = [r[0] for r in rows] + list(params) + [c[0] for c in cts]
    if has_add:
        in_specs.append(_row_spec(tm, add[1], add[2]))
        args.append(add[0])
    out_specs = [_row_spec(tm, rows[q][1], 0) for q in range(n_diff)] + [_full_spec(p.shape) for p in params]
    grad_dtypes = grad_dtypes or [F32] * n_diff
    out_shape = [jax.ShapeDtypeStruct((t, rows[q][1] * ncol), grad_dtypes[q]) for q in range(n_diff)]
    out_shape += [jax.ShapeDtypeStruct(p.shape, F32) for p in params]
    del ins
    res = pl.pallas_call(
        body, grid=(t // tm, ncol), in_specs=in_specs, out_specs=out_specs, out_shape=out_shape, name=name,
        compiler_params=_params(("arbitrary", "arbitrary")),
    )(*args)
    return res[:n_diff], res[n_diff:]


def _silu(x):
    return x * jax.nn.sigmoid(x)


def _f_norm(h, w):
    return h * lax.rsqrt(jnp.mean(h * h, axis=-1, keepdims=True) + EPS) * w


def _f_gnorm(o, z, w):
    return _f_norm(o, w) * _silu(z)


def _f_act(gate, val):
    return _silu(gate) * val


def _f_ple(gl, pe):
    return jax.nn.sigmoid(gl) * pe


def _f_betag(pt, alog, dtb):
    lane = lax.broadcasted_iota(jnp.int32, (1, LANES), 1)
    z = pt + dtb
    softplus = jnp.maximum(z, 0.0) + jnp.log(1.0 + jnp.exp(-jnp.abs(z)))
    g = -jnp.exp(alog) * softplus
    return jnp.where(lane < N_HEADS_A, jax.nn.sigmoid(pt), jnp.where(lane < 2 * N_HEADS_A, g, 0.0))


CONV_TM = 256
CONV_CW = 1024


def _shift_down(x, prev, s, row):
    rp = jnp.tile(pltpu.roll(prev, s, 0), (x.shape[0] // SUBLANES, 1))
    return jnp.where(row < s, rp, pltpu.roll(x, s, 0))


def _shift_up(x, nxt, s, row):
    tm = x.shape[0]
    rn = jnp.tile(pltpu.roll(nxt, SUBLANES - s, 0), (tm // SUBLANES, 1))
    return jnp.where(row >= tm - s, rn, pltpu.roll(x, tm - s, 0))


def _conv_taps(x, prev, w_ref, cols, row):
    k = w_ref.shape[0]
    y = x * w_ref[pl.ds(k - 1, 1), cols]
    for s in range(1, k):
        y = y + _shift_down(x, prev, s, row) * w_ref[pl.ds(k - 1 - s, 1), cols]
    return y


def _lane_chunks(cw):
    return [slice(cb * LANES, (cb + 1) * LANES) for cb in range(cw // LANES)]


def conv_fwd(x, w, *, name):
    t = x.shape[0]
    k, c = w.shape
    tm, cw = min(CONV_TM, t), CONV_CW
    nb8 = tm // SUBLANES

    def body(x_ref, p_ref, w_ref, o_ref):
        first = pl.program_id(1) == 0
        row = lax.broadcasted_iota(jnp.int32, (tm, LANES), 0)
        for cols in _lane_chunks(cw):
            o_ref[:, cols] = _conv_taps(x_ref[:, cols], jnp.where(first, 0.0, p_ref[:, cols]), w_ref, cols, row)

    return pl.pallas_call(
        body, grid=(c // cw, t // tm),
        in_specs=[pl.BlockSpec((tm, cw), lambda j, i: (i, j)),
                  pl.BlockSpec((SUBLANES, cw), lambda j, i: (jnp.maximum(i * nb8 - 1, 0), j)),
                  pl.BlockSpec((k, cw), lambda j, i: (0, j))],
        out_specs=pl.BlockSpec((tm, cw), lambda j, i: (i, j)),
        out_shape=jax.ShapeDtypeStruct((t, c), F32), name=name,
        compiler_params=_params(("parallel", "parallel")),
    )(x, x, w)


def conv_bwd(dy, x, w, *, name):
    t = x.shape[0]
    k, c = w.shape
    tm, cw = min(CONV_TM, t), CONV_CW
    nb8 = tm // SUBLANES
    ni = t // tm

    def body(dy_ref, dn_ref, x_ref, p_ref, w_ref, dx_ref, dw_ref):
        i = pl.program_id(1)
        first, last = i == 0, i == ni - 1
        row = lax.broadcasted_iota(jnp.int32, (tm, LANES), 0)
        for cols in _lane_chunks(cw):
            dyv, xv = dy_ref[:, cols], x_ref[:, cols]
            nxt = jnp.where(last, 0.0, dn_ref[:, cols])
            prev = jnp.where(first, 0.0, p_ref[:, cols])
            dx = dyv * w_ref[pl.ds(k - 1, 1), cols]
            dws = [jnp.sum(dyv * xv, axis=0, keepdims=True)]
            for s in range(1, k):
                dx = dx + _shift_up(dyv, nxt, s, row) * w_ref[pl.ds(k - 1 - s, 1), cols]
                dws.append(jnp.sum(dyv * _shift_down(xv, prev, s, row), axis=0, keepdims=True))
            dx_ref[:, cols] = dx.astype(dx_ref.dtype)
            for s in range(k):
                @pl.when(first)
                def _(s=s, dws=dws, cols=cols):
                    dw_ref[pl.ds(k - 1 - s, 1), cols] = dws[s]

                @pl.when(jnp.logical_not(first))
                def _(s=s, dws=dws, cols=cols):
                    dw_ref[pl.ds(k - 1 - s, 1), cols] += dws[s]

    return pl.pallas_call(
        body, grid=(c // cw, ni),
        in_specs=[pl.BlockSpec((tm, cw), lambda j, i: (i, j)),
                  pl.BlockSpec((SUBLANES, cw), lambda j, i: (jnp.minimum((i + 1) * nb8, t // SUBLANES - 1), j)),
                  pl.BlockSpec((tm, cw), lambda j, i: (i, j)),
                  pl.BlockSpec((SUBLANES, cw), lambda j, i: (jnp.maximum(i * nb8 - 1, 0), j)),
                  pl.BlockSpec((k, cw), lambda j, i: (0, j))],
        out_specs=[pl.BlockSpec((tm, cw), lambda j, i: (i, j)), pl.BlockSpec((k, cw), lambda j, i: (0, j))],
        out_shape=[jax.ShapeDtypeStruct((t, c), BF16), jax.ShapeDtypeStruct((k, c), F32)], name=name,
        compiler_params=_params(("parallel", "arbitrary")),
    )(dy, dy, x, x, w)


FFN_TM = 128
FFN_CW = D_FF // 2


def _ffn_specs(t, tm, cw, k):
    nb8, ncol = tm // SUBLANES, D_FF // cw
    cur = lambda off: pl.BlockSpec((tm, cw), lambda j, i: (i, j + off))
    prev = lambda off: pl.BlockSpec((SUBLANES, cw), lambda j, i: (jnp.maximum(i * nb8 - 1, 0), j + off))
    nxt = lambda off: pl.BlockSpec((SUBLANES, cw), lambda j, i: (jnp.minimum((i + 1) * nb8, t // SUBLANES - 1), j + off))
    taps = lambda off: pl.BlockSpec((k, cw), lambda j, i: (0, j + off))
    return cur, prev, nxt, taps, ncol


def conv_act_fwd(u, w, *, name):
    t, k = u.shape[0], w.shape[0]
    tm, cw = min(FFN_TM, t), FFN_CW
    cur, prev, _, taps, ncol = _ffn_specs(t, tm, cw, k)

    def body(ug_ref, pg_ref, uv_ref, pv_ref, wg_ref, wv_ref, o_ref):
        first = pl.program_id(1) == 0
        row = lax.broadcasted_iota(jnp.int32, (tm, LANES), 0)
        for cb in range(cw // LANES):
            cols = slice(cb * LANES, (cb + 1) * LANES)
            cg = _conv_taps(ug_ref[:, cols], jnp.where(first, 0.0, pg_ref[:, cols]), wg_ref, cols, row)
            cv = _conv_taps(uv_ref[:, cols], jnp.where(first, 0.0, pv_ref[:, cols]), wv_ref, cols, row)
            o_ref[:, cols] = _f_act(cg, cv).astype(o_ref.dtype)

    return pl.pallas_call(
        body, grid=(ncol, t // tm),
        in_specs=[cur(0), prev(0), cur(ncol), prev(ncol), taps(0), taps(ncol)],
        out_specs=cur(0), out_shape=jax.ShapeDtypeStruct((t, D_FF), BF16), name=name,
        compiler_params=_params(("parallel", "parallel")),
    )(u, u, u, u, w, w)


def conv_act_bwd(u, dact, w, *, name):
    t, k = u.shape[0], w.shape[0]
    tm, cw = min(FFN_TM, t), FFN_CW
    cur, prev, nxt, taps, ncol = _ffn_specs(t, tm, cw, k)
    ni = t // tm

    def body(ug_ref, pg_ref, ng_ref, uv_ref, pv_ref, nv_ref, d_ref, dn_ref, wg_ref, wv_ref, dg_ref, dv_ref, dwg_ref, dwv_ref):
        i = pl.program_id(1)
        first, last = i == 0, i == ni - 1
        row = lax.broadcasted_iota(jnp.int32, (tm, LANES), 0)
        row8 = lax.broadcasted_iota(jnp.int32, (SUBLANES, LANES), 0)
        for cb in range(cw // LANES):
            cols = slice(cb * LANES, (cb + 1) * LANES)
            ug, uv = ug_ref[:, cols], uv_ref[:, cols]
            pg, pv = jnp.where(first, 0.0, pg_ref[:, cols]), jnp.where(first, 0.0, pv_ref[:, cols])
            sg = [ug] + [_shift_down(ug, pg, s, row) for s in range(1, k)]
            sv = [uv] + [_shift_down(uv, pv, s, row) for s in range(1, k)]
            taps = lambda xs, w_ref: sum(xs[s] * w_ref[pl.ds(k - 1 - s, 1), cols] for s in range(k))
            _, vjp = jax.vjp(_f_act, taps(sg, wg_ref), taps(sv, wv_ref))
            dcg, dcv = vjp(d_ref[:, cols])
            _, vjp_n = jax.vjp(_f_act, _conv_taps(ng_ref[:, cols], ug[tm - SUBLANES:], wg_ref, cols, row8),
                               _conv_taps(nv_ref[:, cols], uv[tm - SUBLANES:], wv_ref, cols, row8))
            dcgn, dcvn = vjp_n(jnp.where(last, 0.0, dn_ref[:, cols]))
            for dc, dcn, xs, w_ref, dx_ref, dw_ref in ((dcg, dcgn, sg, wg_ref, dg_ref, dwg_ref),
                                                       (dcv, dcvn, sv, wv_ref, dv_ref, dwv_ref)):
                dx = dc * w_ref[pl.ds(k - 1, 1), cols]
                dws = [jnp.sum(dc * xs[0], axis=0, keepdims=True)]
                for s in range(1, k):
                    dx = dx + _shift_up(dc, dcn, s, row) * w_ref[pl.ds(k - 1 - s, 1), cols]
                    dws.append(jnp.sum(dc * xs[s], axis=0, keepdims=True))
                dx_ref[:, cols] = dx.astype(dx_ref.dtype)
                for s in range(k):
                    @pl.when(first)
                    def _(s=s, dw_ref=dw_ref, dws=dws):
                        dw_ref[pl.ds(k - 1 - s, 1), cols] = dws[s]

                    @pl.when(jnp.logical_not(first))
                    def _(s=s, dw_ref=dw_ref, dws=dws):
                        dw_ref[pl.ds(k - 1 - s, 1), cols] += dws[s]

    half = jax.ShapeDtypeStruct((t, D_FF), BF16)
    dwh = jax.ShapeDtypeStruct((k, D_FF), F32)
    return pl.pallas_call(
        body, grid=(ncol, ni),
        in_specs=[cur(0), prev(0), nxt(0), cur(ncol), prev(ncol), nxt(ncol), cur(0), nxt(0), taps(0), taps(ncol)],
        out_specs=[cur(0), cur(0), taps(0), taps(0)], out_shape=[half, half, dwh, dwh], name=name,
        compiler_params=_params(("parallel", "arbitrary")),
    )(u, u, u, u, u, u, dact, dact, w, w)


def _each(f, *lists):
    return [f(*a) for a in zip(*lists)]


@jax.custom_vjp
def _inv_unit_lower(lms):
    return _inv_blocks(lms)


def _inv_blocks(lms):
    c = lms[0].shape[0]
    ri = lax.broadcasted_iota(jnp.int32, (c, c), 0)
    ci = lax.broadcasted_iota(jnp.int32, (c, c), 1)
    eye = (ri == ci).astype(F32)
    dms = _each(lambda lm: eye - jnp.where((ri >> 1) == (ci >> 1), lm, 0.0), lms)
    for lv in range(1, int(math.log2(c))):
        below = ((ri >> (lv + 1)) == (ci >> (lv + 1))) & ((ri >> lv) != (ci >> lv))
        dbs = _each(lambda dm: dm.astype(BF16), dms)
        ods = _each(lambda lm, db: _dot(jnp.where(below, lm, 0.0).astype(BF16), db).astype(BF16), lms, dbs)
        dms = _each(lambda dm, db, od: dm - _dot(db, od), dms, dbs, ods)
    return dms


def _inv_fwd(lms):
    tms = _inv_blocks(lms)
    return tms, tms


def _inv_bwd(tms, dts):
    tbs = _each(lambda tm: tm.astype(BF16), tms)
    mid = _each(lambda tb, dt: _dot(tb, dt.astype(BF16), TN).astype(BF16), tbs, dts)
    return (_each(lambda m, tb: -_dot(m, tb, NT), mid, tbs),)


_inv_unit_lower.defvjp(_inv_fwd, _inv_bwd)


def _l2n(x):
    return x * lax.rsqrt(jnp.sum(x * x, axis=-1, keepdims=True) + EPS)


def _prep_fn(cqs, cks, cvs, bg, sel_b, sel_g):
    c = cqs[0].shape[0]
    ri = lax.broadcasted_iota(jnp.int32, (c, c), 0)
    ci = lax.broadcasted_iota(jnp.int32, (c, c), 1)
    eye = (ri == ci).astype(F32)
    incl, strict = ci <= ri, ci < ri
    last = lax.broadcasted_iota(jnp.int32, (c, 1), 0) == c - 1
    to_row = lambda col: jnp.sum(col * eye, axis=0, keepdims=True)
    qs = _each(lambda a: _l2n(_silu(a)) * (HEAD_DIM_A ** -0.5), cqs)
    ks = _each(lambda a: _l2n(_silu(a)), cks)
    vbs = _each(lambda a: _silu(a).astype(BF16), cvs)
    betas = _each(lambda m: jnp.sum(bg * m, axis=1, keepdims=True), sel_b)
    gs = _each(lambda m: jnp.sum(bg * m, axis=1, keepdims=True), sel_g)
    gcss = _each(lambda g: jnp.sum(jnp.where(incl, to_row(g), 0.0), axis=1, keepdims=True), gs)
    gtots = _each(lambda gcs: jnp.sum(jnp.where(last, gcs, 0.0), axis=0, keepdims=True), gcss)
    decays = _each(lambda gcs: jnp.exp(jnp.where(incl, gcs - to_row(gcs), NEG)), gcss)
    kbs = _each(lambda k: k.astype(BF16), ks)
    lms = _each(lambda beta, kb, dec: jnp.where(strict, beta * _dot(kb, kb, NT) * dec, 0.0), betas, kbs, decays)
    ams = _each(lambda tm, beta: (tm * to_row(beta)).astype(BF16), _inv_unit_lower(lms), betas)
    gams = _each(jnp.exp, gcss)
    u0s = _each(_dot, ams, vbs)
    wks = _each(lambda am, gam, k: _dot(am, (gam * k).astype(BF16)), ams, gams, ks)
    qks = _each(lambda q, kb, dec: _dot(q.astype(BF16), kb, NT) * dec, qs, kbs, decays)
    qds = _each(lambda q, gam: q * gam, qs, gams)
    kds = _each(lambda k, gtot, gcs: k * jnp.exp(gtot - gcs), ks, gtots, gcss)
    gls = _each(lambda gtot: jnp.exp(gtot) * jnp.ones((SUBLANES, LANES), F32), gtots)
    return u0s, wks, qds, kds, qks, gls


def _head_masks(h):
    lane = lax.broadcasted_iota(jnp.int32, (1, LANES), 1)
    return (lane == h).astype(F32), (lane == h + N_HEADS_A).astype(F32)


def _hsl(j):
    return slice(j * HEAD_DIM_A, (j + 1) * HEAD_DIM_A)


def gnorm_fwd(o, zsrc, w, *, name):
    t, width = o.shape
    tm = min(256, t)
    zoff = zsrc.shape[1] // width - 1

    def body(o_ref, z_ref, w_ref, out_ref):
        for h in range(N_HEADS_A):
            out_ref[:, _hsl(h)] = _f_gnorm(o_ref[:, _hsl(h)], z_ref[:, _hsl(h)], w_ref[...]).astype(out_ref.dtype)

    rows = pl.BlockSpec((tm, width), lambda i: (i, 0))
    return pl.pallas_call(
        body, grid=(t // tm,),
        in_specs=[rows, pl.BlockSpec((tm, width), lambda i: (i, zoff)), pl.BlockSpec(w.shape, lambda i: (0, 0))],
        out_specs=rows, out_shape=jax.ShapeDtypeStruct((t, width), BF16), name=name, compiler_params=_params(("parallel",)),
    )(o, zsrc, w)


def gnorm_bwd(o, zsrc, w, don, *, name):
    t, width = o.shape
    tm = min(256, t)
    zoff = zsrc.shape[1] // width - 1

    def body(o_ref, z_ref, w_ref, d_ref, do_ref, dz_ref, dw_ref):
        dw = jnp.zeros(w.shape, F32)
        for h in range(N_HEADS_A):
            _, vjp = jax.vjp(_f_gnorm, o_ref[:, _hsl(h)], z_ref[:, _hsl(h)], w_ref[...])
            do, dz, dwh = vjp(d_ref[:, _hsl(h)])
            do_ref[:, _hsl(h)] = do.astype(do_ref.dtype)
            dz_ref[:, _hsl(h)] = dz.astype(dz_ref.dtype)
            dw = dw + dwh
        first = pl.program_id(0) == 0

        @pl.when(first)
        def _():
            dw_ref[...] = dw

        @pl.when(jnp.logical_not(first))
        def _():
            dw_ref[...] += dw

    rows = pl.BlockSpec((tm, width), lambda i: (i, 0))
    wspec = pl.BlockSpec(w.shape, lambda i: (0, 0))
    return pl.pallas_call(
        body, grid=(t // tm,),
        in_specs=[rows, pl.BlockSpec((tm, width), lambda i: (i, zoff)), wspec, rows],
        out_specs=[rows, rows, wspec],
        out_shape=[jax.ShapeDtypeStruct((t, width), BF16)] * 2 + [jax.ShapeDtypeStruct(w.shape, F32)], name=name,
        compiler_params=_params(("arbitrary",)),
    )(o, zsrc, w, don)


def delta_prep(cqkv, bg, *, name):
    t = cqkv.shape[0]
    nh, hd, n = N_HEADS_A, HEAD_DIM_A, t // CHUNK

    def body(cq_ref, ck_ref, cv_ref, bg_ref, u0_ref, wk_ref, qd_ref, kd_ref, qk_ref, gl_ref):
        heads = range(nh)
        masks = [_head_masks(j) for j in heads]
        res = _prep_fn([cq_ref[:, _hsl(j)] for j in heads], [ck_ref[:, _hsl(j)] for j in heads],
                       [cv_ref[:, _hsl(j)] for j in heads], bg_ref[...], [m[0] for m in masks], [m[1] for m in masks])
        for o_ref, rs in zip((u0_ref, wk_ref, qd_ref, kd_ref, qk_ref), res[:5]):
            for j in heads:
                o_ref[:, _hsl(j)] = rs[j]
        for j in heads:
            gl_ref[j * SUBLANES:(j + 1) * SUBLANES, :] = res[5][j]

    blk = lambda off: pl.BlockSpec((CHUNK, nh * hd), lambda i: (i, off))
    return pl.pallas_call(
        body, grid=(n,),
        in_specs=[blk(0), blk(1), blk(2), pl.BlockSpec((CHUNK, LANES), lambda i: (i, 0))],
        out_specs=[blk(0)] * 5 + [pl.BlockSpec((nh * SUBLANES, LANES), lambda i: (i, 0))],
        out_shape=[jax.ShapeDtypeStruct((t, nh * hd), F32)] * 5 + [jax.ShapeDtypeStruct((n * nh * SUBLANES, LANES), F32)],
        name=name, compiler_params=_params(("parallel",)),
    )(cqkv, cqkv, cqkv, bg)


def delta_prep_bwd(cqkv, bg, cts, *, name):
    t = cqkv.shape[0]
    nh, hd, n = N_HEADS_A, HEAD_DIM_A, t // CHUNK

    def body(cq_ref, ck_ref, cv_ref, bg_ref, c0, c1, c2, c3, c4, c5, dc_ref, dbg_ref):
        heads = range(nh)
        masks = [_head_masks(j) for j in heads]
        _, vjp = jax.vjp(lambda a, b, c, d: _prep_fn(a, b, c, d, [m[0] for m in masks], [m[1] for m in masks]),
                         [cq_ref[:, _hsl(j)] for j in heads], [ck_ref[:, _hsl(j)] for j in heads],
                         [cv_ref[:, _hsl(j)] for j in heads], bg_ref[...])
        cts = tuple([c[:, _hsl(j)] for j in heads] for c in (c0, c1, c2, c3, c4))
        dqs, dks, dvs, dbg = vjp(cts + ([c5[j * SUBLANES:(j + 1) * SUBLANES, :] for j in heads],))
        for part, ds in enumerate((dqs, dks, dvs)):
            for j in heads:
                dc_ref[:, _hsl(part * nh + j)] = ds[j]
        dbg_ref[...] = dbg

    blk = lambda off: pl.BlockSpec((CHUNK, nh * hd), lambda i: (i, off))
    gl_spec = pl.BlockSpec((nh * SUBLANES, LANES), lambda i: (i, 0))
    bg_spec = pl.BlockSpec((CHUNK, LANES), lambda i: (i, 0))
    return pl.pallas_call(
        body, grid=(n,),
        in_specs=[blk(0), blk(1), blk(2), bg_spec] + [blk(0)] * 5 + [gl_spec],
        out_specs=[pl.BlockSpec((CHUNK, 3 * nh * hd), lambda i: (i, 0)), bg_spec],
        out_shape=[jax.ShapeDtypeStruct((t, 3 * nh * hd), F32), jax.ShapeDtypeStruct((t, LANES), F32)],
        name=name, compiler_params=_params(("parallel",)),
    )(cqkv, cqkv, cqkv, bg, *cts)


def delta_scan(u0, wk, qd, kd, qk, gl, *, name):
    t = u0.shape[0]
    nh, hd, n = N_HEADS_A, HEAD_DIM_A, t // CHUNK

    def body(u0_ref, wk_ref, qd_ref, kd_ref, qk_ref, gl_ref, o_ref, sin_ref, s_ref):
        @pl.when(pl.program_id(0) == 0)
        def _():
            s_ref[...] = jnp.zeros_like(s_ref)

        heads = list(range(nh))
        cols = lambda ref: [ref[:, _hsl(h)].astype(BF16) for h in heads]
        ss = [s_ref[h] for h in heads]
        for h in heads:
            sin_ref[h] = ss[h]
        sbs = _each(lambda s: s.astype(BF16), ss)
        ubs = _each(lambda h, wkb, sb: (u0_ref[:, _hsl(h)] - _dot(wkb, sb)).astype(BF16), heads, cols(wk_ref), sbs)
        os_ = _each(lambda qdb, sb, qkb, ub: _dot(qdb, sb) + _dot(qkb, ub), cols(qd_ref), sbs, cols(qk_ref), ubs)
        sn = _each(lambda h, s, kdb, ub: gl_ref[pl.ds(h * SUBLANES, 1), :] * s + _dot(kdb, ub, TN), heads, ss, cols(kd_ref), ubs)
        for h in heads:
            o_ref[:, _hsl(h)] = os_[h]
            s_ref[h] = sn[h]

    blk = pl.BlockSpec((CHUNK, nh * hd), lambda i: (i, 0))
    return pl.pallas_call(
        body, grid=(n,),
        in_specs=[blk] * 5 + [pl.BlockSpec((nh * SUBLANES, LANES), lambda i: (i, 0))],
        out_specs=[blk, pl.BlockSpec((None, nh, hd, hd), lambda i: (i, 0, 0, 0))],
        out_shape=[jax.ShapeDtypeStruct((t, nh * hd), F32), jax.ShapeDtypeStruct((n, nh, hd, hd), F32)],
        scratch_shapes=[pltpu.VMEM((nh, hd, hd), F32)], name=name,
        compiler_params=_params(("arbitrary",)),
    )(u0, wk, qd, kd, qk, gl)


def delta_scan_bwd(do, u0, wk, qd, kd, qk, gl, s_in, *, name):
    t = u0.shape[0]
    nh, hd, n = N_HEADS_A, HEAD_DIM_A, t // CHUNK

    def body(do_ref, u0_ref, wk_ref, qd_ref, kd_ref, qk_ref, gl_ref, sin_ref,
             du0_ref, dwk_ref, dqd_ref, dkd_ref, dqk_ref, dgl_ref, ds_ref):
        @pl.when(pl.program_id(0) == 0)
        def _():
            ds_ref[...] = jnp.zeros_like(ds_ref)

        corner = (lax.broadcasted_iota(jnp.int32, (SUBLANES, LANES), 0) == 0) & (lax.broadcasted_iota(jnp.int32, (SUBLANES, LANES), 1) == 0)
        heads = list(range(nh))
        cols = lambda ref: [ref[:, _hsl(h)].astype(BF16) for h in heads]
        ss, dss = [sin_ref[h] for h in heads], [ds_ref[h] for h in heads]
        sbs, dsbs = _each(lambda s: s.astype(BF16), ss), _each(lambda d: d.astype(BF16), dss)
        dobs, wkbs, qdbs, kdbs, qkbs = cols(do_ref), cols(wk_ref), cols(qd_ref), cols(kd_ref), cols(qk_ref)
        ubs = _each(lambda h, wkb, sb: (u0_ref[:, _hsl(h)] - _dot(wkb, sb)).astype(BF16), heads, wkbs, sbs)
        dus = _each(lambda qkb, dob, kdb, dsb: _dot(qkb, dob, TN) + _dot(kdb, dsb), qkbs, dobs, kdbs, dsbs)
        dubs = _each(lambda du: du.astype(BF16), dus)
        dwks = _each(lambda dub, sb: -_dot(dub, sb, NT), dubs, sbs)
        dqds = _each(lambda dob, sb: _dot(dob, sb, NT), dobs, sbs)
        dkds = _each(lambda ub, dsb: _dot(ub, dsb, NT), ubs, dsbs)
        dqks = _each(lambda dob, ub: _dot(dob, ub, NT), dobs, ubs)
        dgls = _each(lambda s, d: jnp.sum(jnp.sum(s * d, axis=1, keepdims=True), axis=0, keepdims=True), ss, dss)
        dsn = _each(lambda h, d, qdb, dob, wkb, dub: gl_ref[pl.ds(h * SUBLANES, 1), :] * d + _dot(qdb, dob, TN) - _dot(wkb, dub, TN),
                    heads, dss, qdbs, dobs, wkbs, dubs)
        for h in heads:
            du0_ref[:, _hsl(h)] = dus[h]
            dwk_ref[:, _hsl(h)] = dwks[h]
            dqd_ref[:, _hsl(h)] = dqds[h]
            dkd_ref[:, _hsl(h)] = dkds[h]
            dqk_ref[:, _hsl(h)] = dqks[h]
            dgl_ref[h * SUBLANES:(h + 1) * SUBLANES, :] = jnp.where(corner, dgls[h], 0.0)
            ds_ref[h] = dsn[h]

    blk = pl.BlockSpec((CHUNK, nh * hd), lambda i: (n - 1 - i, 0))
    gl_spec = pl.BlockSpec((nh * SUBLANES, LANES), lambda i: (n - 1 - i, 0))
    return pl.pallas_call(
        body, grid=(n,),
        in_specs=[blk] * 6 + [gl_spec, pl.BlockSpec((None, nh, hd, hd), lambda i: (n - 1 - i, 0, 0, 0))],
        out_specs=[blk] * 5 + [gl_spec],
        out_shape=[jax.ShapeDtypeStruct((t, nh * hd), F32)] * 5 + [jax.ShapeDtypeStruct((n * nh * SUBLANES, LANES), F32)],
        scratch_shapes=[pltpu.VMEM((nh, hd, hd), F32)], name=name,
        compiler_params=_params(("arbitrary",)),
    )(do, u0, wk, qd, kd, qk, gl, s_in)


N_PAIRS = N_HEADS_B // 2
PAIRS_PER_KV = N_PAIRS // N_KV_B


def _psl(j):
    return slice(j * LANES, (j + 1) * LANES)


def _att_fn(qps, kc, kp, vc, vp, sinks, kvf, first):
    w = WINDOW
    lane = lax.broadcasted_iota(jnp.int32, (1, LANES), 1)
    lo = (lane < HEAD_DIM_B).astype(F32)
    qi = lax.broadcasted_iota(jnp.int32, (w, w), 0)
    kj = lax.broadcasted_iota(jnp.int32, (w, w), 1)
    dist_c = (qi - kj).astype(F32)
    valid_c = kj <= qi
    valid_p = (kj > qi) & (first < 0.5)
    kcb, kpb, vcb, vpb = (a.astype(BF16) for a in (kc, kp, vc, vp))
    scale = HEAD_DIM_B ** -0.5
    heads = [(j, half) for j in range(PAIRS_PER_KV) for half in range(2)]
    hmasks = [lo if half == 0 else 1.0 - lo for _, half in heads]
    hds = [2.0 * (PAIRS_PER_KV * kvf + j) + half for j, half in heads]
    slopes = _each(lambda hd: jnp.exp(-(hd + 1.0) * (8.0 / N_HEADS_B * math.log(2.0))), hds)
    snks = _each(lambda hd: jnp.sum(sinks * (lane.astype(F32) == hd).astype(F32), axis=1, keepdims=True), hds)
    qhs = _each(lambda jh, hm: (qps[jh[0]] * hm).astype(BF16), heads, hmasks)
    lcs = _each(lambda qh, sl: jnp.where(valid_c, _dot(qh, kcb, NT) * scale - sl * dist_c, NEG), qhs, slopes)
    lps = _each(lambda qh, sl: jnp.where(valid_p, _dot(qh, kpb, NT) * scale - sl * (dist_c + w), NEG), qhs, slopes)
    ms = _each(lambda lc, lp, sk: lax.stop_gradient(jnp.maximum(jnp.maximum(jnp.max(lc, axis=1, keepdims=True),
                                                                            jnp.max(lp, axis=1, keepdims=True)), sk)), lcs, lps, snks)
    ecs = _each(lambda lc, m: jnp.exp(lc - m), lcs, ms)
    eps = _each(lambda lp, m: jnp.exp(lp - m), lps, ms)
    invs = _each(lambda ec, ep, sk, m: 1.0 / (jnp.sum(ec, axis=1, keepdims=True) + jnp.sum(ep, axis=1, keepdims=True) + jnp.exp(sk - m)),
                 ecs, eps, snks, ms)
    ohs = _each(lambda ec, ep, inv, hm: (_dot((ec * inv).astype(BF16), vcb) + _dot((ep * inv).astype(BF16), vpb)) * hm,
                ecs, eps, invs, hmasks)
    return [ohs[2 * j] + ohs[2 * j + 1] for j in range(PAIRS_PER_KV)]


def _scalar11(v):
    return jnp.full((1, 1), v, F32)


def swa_fwd(qsrc, kd, vd, sinks, *, name):
    t = kd.shape[0]
    nb = t // WINDOW

    def body(q_ref, kc_ref, kp_ref, vc_ref, vp_ref, s_ref, o_ref):
        first = _scalar11((pl.program_id(0) == 0).astype(F32))
        kvf = _scalar11(pl.program_id(1).astype(F32))
        outs = _att_fn([q_ref[:, _psl(j)] for j in range(PAIRS_PER_KV)], kc_ref[...], kp_ref[...], vc_ref[...], vp_ref[...],
                       s_ref[...], kvf, first)
        for j in range(PAIRS_PER_KV):
            o_ref[:, _psl(j)] = outs[j].astype(o_ref.dtype)

    cur = pl.BlockSpec((WINDOW, LANES), lambda i, kv: (i, kv))
    prev = pl.BlockSpec((WINDOW, LANES), lambda i, kv: (jnp.maximum(i - 1, 0), kv))
    qs = pl.BlockSpec((WINDOW, PAIRS_PER_KV * LANES), lambda i, kv: (i, kv))
    return pl.pallas_call(
        body, grid=(nb, N_KV_B),
        in_specs=[qs, cur, prev, cur, prev, pl.BlockSpec((1, LANES), lambda i, kv: (0, 0))],
        out_specs=qs, out_shape=jax.ShapeDtypeStruct((t, N_PAIRS * LANES), BF16), name=name,
        compiler_params=_params(("parallel", "parallel")),
    )(qsrc, kd, kd, vd, vd, sinks)


def swa_bwd(do, qsrc, kd, vd, sinks, *, name):
    t = kd.shape[0]
    nb = t // WINDOW

    def body(do_ref, q_ref, kc_ref, kp_ref, vc_ref, vp_ref, s_ref, dq_ref, dk_ref, dv_ref, ds_ref, carry_k, carry_v):
        step, kv = pl.program_id(0), pl.program_id(1)
        first = _scalar11((step == nb - 1).astype(F32))

        @pl.when((step == 0) & (kv == 0))
        def _():
            carry_k[...] = jnp.zeros_like(carry_k)
            carry_v[...] = jnp.zeros_like(carry_v)
            ds_ref[...] = jnp.zeros_like(ds_ref)

        kvf = _scalar11(kv.astype(F32))
        pairs = range(PAIRS_PER_KV)
        f32 = lambda ref: ref[...].astype(F32)
        _, vjp = jax.vjp(lambda *a: _att_fn(*a, kvf, first), [q_ref[:, _psl(j)].astype(F32) for j in pairs],
                         f32(kc_ref), f32(kp_ref), f32(vc_ref), f32(vp_ref), s_ref[...])
        dqs, dkc, dkp, dvc, dvp, dsk = vjp([do_ref[:, _psl(j)].astype(F32) for j in pairs])
        for j in pairs:
            dq_ref[:, _psl(j)] = dqs[j].astype(dq_ref.dtype)
        ds_ref[...] += dsk
        fold = lambda g: g + pltpu.roll(g, HEAD_DIM_B, 1)
        dk_ref[...] = fold(dkc + carry_k[kv]).astype(dk_ref.dtype)
        dv_ref[...] = fold(dvc + carry_v[kv]).astype(dv_ref.dtype)
        carry_k[kv] = dkp
        carry_v[kv] = dvp

    rev = lambda i: nb - 1 - i
    cur = pl.BlockSpec((WINDOW, LANES), lambda i, kv: (rev(i), kv))
    prev = pl.BlockSpec((WINDOW, LANES), lambda i, kv: (jnp.maximum(rev(i) - 1, 0), kv))
    qs = pl.BlockSpec((WINDOW, PAIRS_PER_KV * LANES), lambda i, kv: (rev(i), kv))
    sk = pl.BlockSpec((1, LANES), lambda i, kv: (0, 0))
    return pl.pallas_call(
        body, grid=(nb, N_KV_B),
        in_specs=[qs, qs, cur, prev, cur, prev, sk],
        out_specs=[qs, cur, cur, sk],
        out_shape=[jax.ShapeDtypeStruct((t, N_PAIRS * LANES), BF16), jax.ShapeDtypeStruct((t, N_KV_B * LANES), BF16),
                   jax.ShapeDtypeStruct((t, N_KV_B * LANES), BF16), jax.ShapeDtypeStruct((1, LANES), F32)],
        scratch_shapes=[pltpu.VMEM((N_KV_B, WINDOW, LANES), F32), pltpu.VMEM((N_KV_B, WINDOW, LANES), F32)],
        name=name, compiler_params=_params(("arbitrary", "arbitrary")),
    )(do, qsrc, kd, kd, vd, vd, sinks)


def loss_head(h, tgt, w, *, name):
    t, d = h.shape
    tm = min(256, t)

    def body(h_ref, t_ref, w_ref, dh_ref, dw_ref, l_ref):
        tg = t_ref[...]

        def f(hv, wv):
            err = _f_norm(hv, wv) - tg
            return 0.5 * jnp.sum(jnp.sum(err * err, axis=1, keepdims=True), axis=0, keepdims=True) * (1.0 / d)

        lv, vjp = jax.vjp(f, h_ref[...], w_ref[...])
        dh, dw = vjp(jnp.ones((1, 1), F32))
        dh_ref[...] = dh
        first = pl.program_id(0) == 0

        @pl.when(first)
        def _():
            dw_ref[...] = dw
            l_ref[...] = lv * jnp.ones((1, LANES), F32)

        @pl.when(jnp.logical_not(first))
        def _():
            dw_ref[...] += dw
            l_ref[...] += lv * jnp.ones((1, LANES), F32)

    rows = pl.BlockSpec((tm, d), lambda i: (i, 0))
    one = lambda c: pl.BlockSpec((1, c), lambda i: (0, 0))
    return pl.pallas_call(
        body, grid=(t // tm,), in_specs=[rows, rows, one(d)], out_specs=[rows, one(d), one(LANES)],
        out_shape=[jax.ShapeDtypeStruct((t, d), F32), jax.ShapeDtypeStruct((1, d), F32), jax.ShapeDtypeStruct((1, LANES), F32)],
        name=name, compiler_params=_params(("arbitrary",)),
    )(h, tgt, w)


def _row_tile(r, cap=256):
    tr = r
    if r % SUBLANES == 0:
        for cand in range(SUBLANES, min(r, cap) + 1, SUBLANES):
            if r % cand == 0:
                tr = cand
    return tr


def _adamw_update(wv, gv, mv, vv):
    mn = ADAM_B1 * mv + (1.0 - ADAM_B1) * gv
    vn = ADAM_B2 * vv + (1.0 - ADAM_B2) * jnp.square(gv)
    m_hat = mn / (1.0 - ADAM_B1 ** ADAM_STEP)
    v_hat = vn / (1.0 - ADAM_B2 ** ADAM_STEP)
    return -ADAM_LR * (m_hat / (jnp.sqrt(v_hat) + ADAM_EPS) + ADAM_WD * wv), mn, vn


def adamw_layers(w, gs, m, v, *, name):
    nl, r, c = w.shape
    tr = _row_tile(r)

    def body(*refs):
        w_ref, g_refs, m_ref, v_ref = refs[0], refs[1:1 + nl], refs[1 + nl], refs[2 + nl]
        d_ref, mo_ref, vo_ref, go_ref = refs[3 + nl:]
        layer = pl.program_id(0)
        gv = g_refs[0][...]
        for k in range(1, nl):
            gv = jnp.where(layer == k, g_refs[k][...], gv)
        d_ref[...], mo_ref[...], vo_ref[...] = _adamw_update(w_ref[...], gv, m_ref[...], v_ref[...])
        go_ref[...] = gv

    spec3 = pl.BlockSpec((None, tr, c), lambda k, i: (k, i, 0))
    g_specs = [pl.BlockSpec((tr, c), lambda k, i, q=q: (jnp.where(k == q, i, 0), 0)) for q in range(nl)]
    return pl.pallas_call(
        body, grid=(nl, r // tr), in_specs=[spec3] + g_specs + [spec3, spec3], out_specs=[spec3] * 4,
        out_shape=[jax.ShapeDtypeStruct((nl, r, c), F32)] * 4, name=name, compiler_params=_params(("arbitrary", "arbitrary")),
    )(w, *gs, m, v)


def adamw(w, g, m, v, *, name):
    r, c = w.shape
    tr = _row_tile(r)

    def body(w_ref, g_ref, m_ref, v_ref, d_ref, mo_ref, vo_ref):
        d_ref[...], mo_ref[...], vo_ref[...] = _adamw_update(w_ref[...], g_ref[...], m_ref[...], v_ref[...])

    spec = pl.BlockSpec((tr, c), lambda i: (i, 0))
    return pl.pallas_call(
        body, grid=(r // tr,), in_specs=[spec] * 4, out_specs=[spec] * 3,
        out_shape=[jax.ShapeDtypeStruct((r, c), F32)] * 3, name=name, compiler_params=_params(("parallel",)),
    )(w, g, m, v)


def _place():
    return lax.axis_index("x"), lax.axis_index("y"), lax.axis_index("c")


def allgather8(blk, *, name):
    def body(x_ref, out_ref, send_sems, recv_sems, local_sem):
        x, y, c = _place()
        me = 4 * x + 2 * y + c
        mine = pltpu.make_async_copy(x_ref, out_ref.at[me], local_sem)
        mine.start()
        sent = []
        for k in range(1, N_DEV):
            to = (x ^ ((k >> 2) & 1), y ^ ((k >> 1) & 1), c ^ (k & 1))
            cp = pltpu.make_async_remote_copy(src_ref=x_ref, dst_ref=out_ref.at[me], send_sem=send_sems.at[k - 1],
                                              recv_sem=recv_sems.at[k - 1], device_id=to, device_id_type=MESH)
            cp.start()
            sent.append(cp)
        for k in range(1, N_DEV):
            frm = me ^ k
            pltpu.make_async_remote_copy(src_ref=x_ref, dst_ref=out_ref.at[frm], send_sem=send_sems.at[k - 1],
                                         recv_sem=recv_sems.at[k - 1], device_id=(x, y, c), device_id_type=MESH).wait_recv()
        for cp in sent:
            cp.wait_send()
        mine.wait()

    vm = pl.BlockSpec(memory_space=pltpu.VMEM)
    return pl.pallas_call(
        body, in_specs=[vm], out_specs=vm, out_shape=jax.ShapeDtypeStruct((N_DEV,) + blk.shape, blk.dtype), name=name,
        scratch_shapes=[pltpu.SemaphoreType.DMA((N_DEV - 1,)), pltpu.SemaphoreType.DMA((N_DEV - 1,)), pltpu.SemaphoreType.DMA],
    )(blk)


def _other_chips(x, y):
    return [(1 - x, y), (x, 1 - y), (1 - x, 1 - y)]


def _hbm_call(body, ins, out_shapes, n_sems, name):
    hbm = pl.BlockSpec(memory_space=pl.ANY)
    return pl.pallas_call(
        body, in_specs=[hbm] * len(ins), out_specs=[hbm] * len(out_shapes), out_shape=out_shapes, name=name,
        scratch_shapes=[pltpu.SemaphoreType.DMA((n_sems,)), pltpu.SemaphoreType.DMA((n_sems,))],
    )(*ins)


def _half_rows(c, rh):
    return pl.ds(pl.multiple_of(c * rh, BF16_ROWS), rh)


def gather_units(units, *, name):
    nu = len(units)
    shapes = []
    for arr, layer_major in units:
        r, cols = arr.shape
        shapes.append(jax.ShapeDtypeStruct((2, N_CHIPS, r // 2, cols) if layer_major else (N_CHIPS, r, cols), arr.dtype))

    def body(*refs):
        in_refs, out_refs, send_sems, recv_sems = refs[:nu], refs[nu:2 * nu], refs[2 * nu], refs[2 * nu + 1]
        x, y, c = _place()
        me_chip = 2 * x + y
        sib = (x, y, 1 - c)
        chips = _other_chips(x, y)

        def copy(k, src, dst, to):
            return pltpu.make_async_remote_copy(src_ref=src, dst_ref=dst, send_sem=send_sems.at[k], recv_sem=recv_sems.at[k],
                                                device_id=to, device_id_type=MESH)

        first, passed, landing = [], [], []
        for u, (arr, layer_major) in enumerate(units):
            rh = arr.shape[0] // 2
            out_ref = out_refs[u]
            slot = (lambda chip, half, o=out_ref: o.at[half, chip]) if layer_major else \
                   (lambda chip, half, o=out_ref, rh=rh: o.at[chip, _half_rows(half, rh), :])
            my_half = in_refs[u].at[_half_rows(c, rh), :]
            for j, (cx, cy) in enumerate(chips):
                k = 6 * u + j
                first.append(copy(k, my_half, slot(me_chip, c), (cx, cy, c)))
                passed.append(copy(k + 3, slot(2 * cx + cy, c), slot(2 * cx + cy, c), sib))
                landing.append((copy(k, my_half, slot(2 * cx + cy, c), sib), copy(k + 3, my_half, slot(2 * cx + cy, 1 - c), sib)))
        for cp in first:
            cp.start()
        for (over_ici, _), fwd in zip(landing, passed):
            over_ici.wait_recv()
            fwd.start()
        for _, from_sibling in landing:
            from_sibling.wait_recv()
        for cp in first + passed:
            cp.wait_send()

    return _hbm_call(body, [a for a, _ in units], shapes, 6 * nu, name)


HBM_SPEC = pl.BlockSpec(memory_space=pltpu.HBM)
SEM_SPEC = pl.BlockSpec(memory_space=pltpu.SEMAPHORE)
ORDERED_EFFECT = pltpu.SideEffectType.DATAFLOW_SIDE_EFFECTING


def _split_start(body, srcs, land_shapes, after, *, name):
    nu = len(srcs)
    lands = [lax.empty(s.shape, s.dtype) for s in land_shapes]

    def whole(*refs):
        body(refs[:nu], refs[nu:2 * nu], refs[2 * nu + 1], refs[2 * nu + 2])
        refs[-1][...] = jnp.zeros((SUBLANES, LANES), F32)

    hbm = lambda a: pltpu.with_memory_space_constraint(a, pltpu.HBM)
    sems = pltpu.SemaphoreType.DMA((nu,))
    res = pl.pallas_call(
        whole, name=name, in_specs=[HBM_SPEC] * (2 * nu) + [pl.BlockSpec(memory_space=pl.ANY)],
        out_shape=[sems, sems] + [pltpu.HBM(a.shape, a.dtype) for a in srcs] + [pltpu.HBM(s.shape, s.dtype) for s in land_shapes]
        + [jax.ShapeDtypeStruct((SUBLANES, LANES), F32)],
        out_specs=[SEM_SPEC, SEM_SPEC] + [HBM_SPEC] * (2 * nu) + [pl.BlockSpec(memory_space=pltpu.VMEM)],
        input_output_aliases={q: 2 + q for q in range(2 * nu)},
        compiler_params=pltpu.CompilerParams(has_side_effects=ORDERED_EFFECT),
    )(*[hbm(a) for a in srcs], *[hbm(a) for a in lands], after)
    return res[0], res[1], res[2:2 + nu], res[2 + nu:2 + 2 * nu], res[-1]


def _split_wait(pending, moved, after, *, name):
    send_sems, recv_sems, srcs, lands, _ = pending
    nu = len(srcs)

    def body(*refs):
        land_refs, ssem, rsem = refs[nu:2 * nu], refs[2 * nu], refs[2 * nu + 1]
        x, y, c = _place()
        for u in range(nu):
            size = moved(land_refs[u])
            cp = pltpu.make_async_remote_copy(src_ref=size, dst_ref=size, send_sem=ssem.at[u], recv_sem=rsem.at[u],
                                              device_id=(x, y, c), device_id_type=MESH)
            cp.wait_send()
            cp.wait_recv()

    res = pl.pallas_call(
        body, name=name, in_specs=[HBM_SPEC] * (2 * nu) + [SEM_SPEC, SEM_SPEC, pl.BlockSpec(memory_space=pl.ANY)],
        out_shape=[pltpu.HBM(a.shape, a.dtype) for a in srcs] + [pltpu.HBM(a.shape, a.dtype) for a in lands],
        out_specs=[HBM_SPEC] * (2 * nu), input_output_aliases={q: q for q in range(2 * nu)},
        compiler_params=pltpu.CompilerParams(has_side_effects=ORDERED_EFFECT),
    )(*srcs, *lands, send_sems, recv_sems, after)
    return res[nu:]


def gather_start(shards, after, *, name):
    def body(src_refs, land_refs, send_sems, recv_sems):
        x, y, c = _place()
        for u, shard in enumerate(shards):
            rows = _half_rows(c, shard.shape[0] // 2)
            for cx, cy in _other_chips(x, y):
                for core in range(2):
                    pltpu.make_async_remote_copy(src_ref=src_refs[u].at[rows, :], dst_ref=land_refs[u].at[2 * x + y, rows, :],
                                                 send_sem=send_sems.at[u], recv_sem=recv_sems.at[u], device_id=(cx, cy, core),
                                                 device_id_type=MESH).start()

    return _split_start(body, shards, [jax.ShapeDtypeStruct((N_CHIPS,) + s.shape, s.dtype) for s in shards], after, name=name)


def gather_wait(pending, after, *, name):
    return _split_wait(pending, lambda land: land.at[pl.ds(0, N_CHIPS - 1)], after, name=name)


def scatter_start(pairs, *, name):
    def body(src_refs, land_refs, send_sems, recv_sems):
        x, y, c = _place()
        for u in range(len(pairs)):
            for j, (cx, cy) in enumerate(_other_chips(x, y)):
                pltpu.make_async_remote_copy(src_ref=src_refs[u].at[2 * cx + cy], dst_ref=land_refs[u].at[j], send_sem=send_sems.at[u],
                                             recv_sem=recv_sems.at[u], device_id=(cx, cy, c), device_id_type=MESH).start()

    return _split_start(body, pairs, [jax.ShapeDtypeStruct((N_CHIPS - 1,) + p.shape[1:], p.dtype) for p in pairs], pairs[0], name=name)


def scatter_wait(pending, after, *, name):
    return _split_wait(pending, lambda land: land, after, name=name)


def swap_units(units, *, name):
    nu = len(units)

    def body(*refs):
        g_refs, out_refs, send_sems, recv_sems = refs[:nu], refs[nu:2 * nu], refs[2 * nu], refs[2 * nu + 1]
        x, y, c = _place()
        cps = [pltpu.make_async_remote_copy(src_ref=g_refs[u].at[:, _half_rows(1 - c, units[u].shape[1] // 2), :], dst_ref=out_refs[u],
                                            send_sem=send_sems.at[u], recv_sem=recv_sems.at[u], device_id=(x, y, 1 - c),
                                            device_id_type=MESH) for u in range(nu)]
        for cp in cps:
            cp.start()
        for cp in cps:
            cp.wait()

    shapes = [jax.ShapeDtypeStruct((N_CHIPS, g.shape[1] // 2, g.shape[2]), g.dtype) for g in units]
    return _hbm_call(body, units, shapes, nu, name)


def scatter_units(units, *, name):
    nu = len(units)

    def body(*refs):
        h_refs, out_refs, send_sems, recv_sems = refs[:nu], refs[nu:2 * nu], refs[2 * nu], refs[2 * nu + 1]
        x, y, c = _place()
        cps = [pltpu.make_async_remote_copy(src_ref=h_refs[u].at[2 * cx + cy], dst_ref=out_refs[u].at[j], send_sem=send_sems.at[3 * u + j],
                                            recv_sem=recv_sems.at[3 * u + j], device_id=(cx, cy, c), device_id_type=MESH)
               for u in range(nu) for j, (cx, cy) in enumerate(_other_chips(x, y))]
        for cp in cps:
            cp.start()
        for cp in cps:
            cp.wait()

    shapes = [jax.ShapeDtypeStruct((3,) + h.shape[1:], h.dtype) for h in units]
    return _hbm_call(body, units, shapes, 3 * nu, name)


def join_units(units, *, name):
    nu = len(units)

    def body(*refs):
        h_refs, out_refs, send_sems, recv_sems = refs[:nu], refs[nu:2 * nu], refs[2 * nu], refs[2 * nu + 1]
        x, y, c = _place()
        cps = [pltpu.make_async_remote_copy(src_ref=h_refs[u], dst_ref=out_refs[u], send_sem=send_sems.at[u], recv_sem=recv_sems.at[u],
                                            device_id=(x, y, 1 - c), device_id_type=MESH) for u in range(nu)]
        for cp in cps:
            cp.start()
        for cp in cps:
            cp.wait()

    return _hbm_call(body, units, [jax.ShapeDtypeStruct(h.shape, h.dtype) for h in units], nu, name)


def add_parts(parts, *, out_dtype, name):
    parts = [p if isinstance(p, tuple) else (p, None) for p in parts]
    rows, cols = parts[0][0].shape[-2:]
    tr = rows
    for cand in range(BF16_ROWS, min(rows, 512) + 1, BF16_ROWS):
        if rows % cand == 0:
            tr = cand

    def body(*refs):
        acc = refs[0][...].astype(F32)
        for r in refs[1:-1]:
            acc = acc + r[...].astype(F32)
        refs[-1][...] = acc.astype(refs[-1].dtype)

    spec = pl.BlockSpec((tr, cols), lambda i: (i, 0))
    in_specs = [spec if q is None else pl.BlockSpec((None, tr, cols), lambda i, q=q: (q, i, 0)) for _, q in parts]
    return pl.pallas_call(
        body, grid=(rows // tr,), in_specs=in_specs, out_specs=spec,
        out_shape=jax.ShapeDtypeStruct((rows, cols), out_dtype), name=name, compiler_params=_params(("parallel",)),
    )(*[a for a, _ in parts])


def sum8(g, *, name):
    def body(g_ref, o_ref):
        acc = g_ref[0]
        for d in range(1, N_DEV):
            acc = acc + g_ref[d]
        o_ref[...] = acc

    return pl.pallas_call(body, out_shape=jax.ShapeDtypeStruct(g.shape[1:], F32), name=name)(g)


def _dup_halves(a):
    t = a.shape[0]
    a = a.reshape(t, N_KV_B, HEAD_DIM_B)
    return jnp.concatenate([a, a], axis=-1).reshape(t, N_KV_B * LANES)


def _undup(a):
    t = a.shape[0]
    return a.reshape(t, N_KV_B, LANES)[:, :, :HEAD_DIM_B].reshape(t, N_KV_B * HEAD_DIM_B)


def _lane_pad(v, offset=0):
    return jnp.zeros((1, LANES), F32).at[0, offset:offset + v.shape[0]].set(v)


SHARD_UP = 2 * D_FF // N_CHIPS
SHARD_BIN = (N_HEADS_B + 2 * N_KV_B) * HEAD_DIM_B // N_CHIPS
SHARD_PROJ = D_MODEL // N_CHIPS


def local_step(x, p, tgt, sm, weight, on_grads):
    t = x.shape[0]
    rtm = min(256, t)
    hk = N_HEADS_A * HEAD_DIM_A
    qd_b = N_HEADS_B * HEAD_DIM_B
    kd_b = N_KV_B * HEAD_DIM_B
    gs = {}
    norm = lambda h, w, nm: tile_map(_f_norm, [(h, D_MODEL, 0)], [w], [(D_MODEL, BF16)], tm=rtm, ncol=1, name=nm)[0]

    def norm_bwd(h, w, dy, add, nm):
        (dh,), (dw,) = tile_vjp(_f_norm, [(h, D_MODEL, 0)], [w], [(dy, D_MODEL, 0)], n_diff=1, tm=rtm, ncol=1, name=nm,
                                add=(add, D_MODEL, 0))
        return dh, dw

    spec = pl.BlockSpec
    mtm = _tile(D_MODEL, MM_TM_CAP)
    p_bf = p.astype(BF16)
    alog_p = _lane_pad(sm["a_log"][0], N_HEADS_A)
    dtb_p = _lane_pad(sm["a_dt_bias"][0], N_HEADS_A)
    sinks_p = _lane_pad(sm["b_sinks"][0])
    nw = lambda name, i: sm[name][i:i + 1]
    by_chip = lambda kdim, ns: dict(tn=ns, tk=kdim, b_spec=spec((None, kdim, ns), lambda r, j, kk: (j, kk, 0)))
    by_chip_t = lambda ndim, ns: dict(n=ndim, tn=ndim, tk=ns, b_spec=spec((None, ndim, ns), lambda r, j, kk: (kk, j, 0)))
    cache = {}

    def wgt(name, i, after):
        if (name, i) not in cache:
            cache[name, i] = weight(name, i, after)
        return cache[name, i]

    saved = []
    h = x
    hn_next = norm(h, nw("norm_mix", 0), "norm_mix0")
    for i in range(DEPTH):
        s = {"h0": h, "hn": hn_next}
        if i % 2 == 0:
            s["pm"] = mm(s["hn"], wgt("a_main", i, h), name="a_in_main")
            s["pt"] = mm(s["hn"], wgt("a_tail", i, h), name="a_in_tail")
            s["c"] = conv_fwd(s["pm"], wgt("a_conv", i, h), name="a_conv")
            s["bg"] = tile_map(_f_betag, [(s["pt"], LANES, 0)], [alog_p, dtb_p], [(LANES, F32)], tm=rtm, ncol=1, name="a_betag")[0]
            s["prep"] = delta_prep(s["c"], s["bg"], name="a_prep")
            s["o"], s["s_in"] = delta_scan(*s["prep"], name="a_scan")
            s["on"] = gnorm_fwd(s["o"], s["pm"], sm["a_norm"], name="a_gnorm")
            h, s["hf"] = mm(s["on"], wgt("a_w_out", i, s["on"]), add=h, norm_w=nw("norm_ffn", i), name="a_out")
        else:
            s["pb"] = mm(s["hn"], wgt("b_w_in", i, s["hn"]), name="b_in", out_dtype=BF16, n=N_CHIPS * SHARD_BIN,
                         **by_chip(D_MODEL, SHARD_BIN))
            s["kd"], s["vd"] = _dup_halves(s["pb"][:, qd_b:qd_b + kd_b]), _dup_halves(s["pb"][:, qd_b + kd_b:])
            s["ao"] = swa_fwd(s["pb"], s["kd"], s["vd"], sinks_p, name="b_att")
            h, s["hf"] = mm(s["ao"], wgt("b_w_out", i, s["ao"]), add=h, norm_w=nw("norm_ffn", i), name="b_out")
        s["h1"] = h
        s["u"] = mm(s["hf"], wgt("f_w_up", i, s["hf"]), name=f"f_up{i}", n=2 * D_FF, **by_chip(D_MODEL, SHARD_UP))
        s["act"] = conv_act_fwd(s["u"], wgt("f_conv", i, s["hf"]), name=f"f_conv_act{i}")
        h, s["hp"] = mm(s["act"], wgt("f_w_down", i, s["act"]), add=h, norm_w=nw("norm_ple", i), name=f"f_down{i}")
        s["h2"] = h
        s["gl"] = mm(s["hp"], wgt("ple_w_gate", i, s["hp"]), name=f"ple_gate{i}")
        s["pe"] = mm(p_bf[i], wgt("ple_w_proj", i, s["hp"]), name=f"ple_proj{i}", n=D_MODEL, **by_chip(PLE_DIM, SHARD_PROJ))
        rows3 = [(h, D_MODEL, 0), (s["gl"], D_MODEL, 0), (s["pe"], D_MODEL, 0)]
        if i + 1 < DEPTH:
            def mix_norm(hv, g, e, wn):
                hn = hv + _f_ple(g, e)
                return hn, _f_norm(hn, wn)
            h, hn_next = tile_map(mix_norm, rows3, [nw("norm_mix", i + 1)], [(D_MODEL, F32), (D_MODEL, BF16)], tm=rtm, ncol=1,
                                  name=f"ple_mix{i}")
        else:
            h = tile_map(lambda hv, g, e: hv + _f_ple(g, e), rows3, [], [(D_MODEL, F32)], tm=rtm, ncol=1, name=f"ple_mix{i}")[0]
        saved.append(s)

    dh, gnf, loss = loss_head(h, tgt, sm["norm_final"][None, :], name="loss_head")
    gs["norm_final"] = gnf[0]

    g_mix, g_ffn, g_ple, g_conv = ([None] * DEPTH for _ in range(4))
    zero = jnp.zeros((1, 1), F32)
    for i in reversed(range(DEPTH)):
        s, gw = saved[i], {}
        by_rows = lambda g: g.reshape(N_CHIPS, g.shape[0] // N_CHIPS, g.shape[1])
        (dgl, dpe), _ = tile_vjp(_f_ple, [(s["gl"], D_MODEL, 0), (s["pe"], D_MODEL, 0)], [], [(dh, D_MODEL, 0)], n_diff=2,
                                 tm=rtm, ncol=1, name=f"ple_mix_bwd{i}", grad_dtypes=[BF16, BF16])
        gw["ple_w_proj"] = mm(p_bf[i], dpe, ta=True, name=f"ple_proj_dw{i}", out_dtype=BF16, tn=SHARD_PROJ,
                              o_shape=(N_CHIPS, PLE_DIM, SHARD_PROJ), o_spec=spec((None, PLE_DIM, SHARD_PROJ), lambda r, j, kk: (j, r, 0)))
        gw["ple_w_gate"] = by_rows(mm(s["hp"], dgl, ta=True, name=f"ple_gate_dw{i}", out_dtype=BF16))
        dhp = mm(dgl, cache["ple_w_gate", i], tb=True, name=f"ple_gate_dx{i}")
        dh, g_ple[i] = norm_bwd(s["h2"], nw("norm_ple", i) + zero, dhp, dh, f"norm_ple_bwd{i}")

        dact = mm(dh, cache["f_w_down", i], tb=True, name=f"f_down_dx{i}")
        gw["f_w_down"] = by_rows(mm(s["act"], dh, ta=True, name=f"f_down_dw{i}", out_dtype=BF16, tm_cap=D_FF // 2))
        du_halves = conv_act_bwd(s["u"], dact, cache["f_conv", i], name=f"f_conv_act_bwd{i}")
        g_conv[i] = jnp.concatenate(du_halves[2:], axis=1)
        dhf = g_up = None
        for half, du in enumerate(du_halves[:2]):
            c0 = half * (N_CHIPS // 2)
            g_up = mm(s["hf"], du, ta=True, name=f"f_up_dw{i}_{half}", out_dtype=BF16, tn=SHARD_UP, into=g_up,
                      o_shape=(N_CHIPS, D_MODEL, SHARD_UP), o_spec=spec((None, mtm, SHARD_UP), lambda r, j, kk, c0=c0: (c0 + j, r, 0)))
            dhf = mm(du, cache["f_w_up", i], tb=True, name=f"f_up_dx{i}_{half}", n=D_MODEL, tn=D_MODEL, tk=SHARD_UP, add=dhf,
                     b_spec=spec((None, D_MODEL, SHARD_UP), lambda r, j, kk, c0=c0: (c0 + kk, j, 0)))
        gw["f_w_up"] = g_up
        dh, g_ffn[i] = norm_bwd(s["h1"], nw("norm_ffn", i), dhf, dh, f"norm_ffn_bwd{i}")
        token, gw = on_grads(i, "ffn", gw), {}
        w_out = cache["a_w_out" if i % 2 == 0 else "b_w_out", i]
        if token is not None:
            w_out = w_out + token[:1, :1].astype(BF16)

        if i % 2 == 0:
            don = mm(dh, w_out, tb=True, name="a_out_dx")
            gw["a_w_out"] = by_rows(mm(s["on"], dh, ta=True, name="a_out_dw", out_dtype=BF16))
            do, dz, gs["a_norm"] = gnorm_bwd(s["o"], s["pm"], sm["a_norm"], don, name="a_gnorm_bwd")
            dprep = delta_scan_bwd(do, *s["prep"], s["s_in"], name="a_scan_bwd")
            dc, dbg = delta_prep_bwd(s["c"], s["bg"], dprep, name="a_prep_bwd")
            (dpt,), (galog, gdtb) = tile_vjp(_f_betag, [(s["pt"], LANES, 0)], [alog_p, dtb_p], [(dbg, LANES, 0)], n_diff=1,
                                             tm=rtm, ncol=1, name="a_betag_bwd", grad_dtypes=[BF16])
            gs["a_log"] = galog[:, N_HEADS_A:2 * N_HEADS_A]
            gs["a_dt_bias"] = gdtb[:, N_HEADS_A:2 * N_HEADS_A]
            dqkv, gs["a_conv"] = conv_bwd(dc, s["pm"], cache["a_conv", i], name="a_conv_bwd")
            dpm = jnp.concatenate([dqkv, dz], axis=1)
            dhn = mm(dpm, cache["a_main", i], tb=True, name="a_in_main_dx")
            dhn = mm(dpt, cache["a_tail", i], tb=True, add=dhn, name="a_in_tail_dx")
            g_main = mm(s["hn"], dpm, ta=True, name="a_in_main_dw", out_dtype=BF16)
            g_tail = mm(s["hn"], dpt, ta=True, name="a_in_tail_dw", out_dtype=BF16)
            g_in = jnp.concatenate([g_main, g_tail[:, :2 * N_HEADS_A]], axis=1)
            gw["a_w_in"] = g_in.reshape(D_MODEL, N_CHIPS, g_in.shape[1] // N_CHIPS).transpose(1, 0, 2)
        else:
            dao = mm(dh, w_out, tb=True, name="b_out_dx")
            gw["b_w_out"] = by_rows(mm(s["ao"], dh, ta=True, name="b_out_dw", out_dtype=BF16))
            dq, dkd, dvd, gsk = swa_bwd(dao, s["pb"], s["kd"], s["vd"], sinks_p, name="b_att_bwd")
            gs["b_sinks"] = gsk[:, :N_HEADS_B]
            dpb = jnp.concatenate([dq, _undup(dkd), _undup(dvd)], axis=1)
            dhn = mm(dpb, cache["b_w_in", i], tb=True, name="b_in_dx", **by_chip_t(D_MODEL, SHARD_BIN))
            gw["b_w_in"] = mm(s["hn"], dpb, ta=True, name="b_in_dw", out_dtype=BF16, tn=SHARD_BIN,
                              o_shape=(N_CHIPS, D_MODEL, SHARD_BIN), o_spec=spec((None, mtm, SHARD_BIN), lambda r, j, kk: (j, r, 0)))
        dh, g_mix[i] = norm_bwd(s["h0"], nw("norm_mix", i), dhn, dh, f"norm_mix_bwd{i}")
        token = on_grads(i, "mix", gw)
        if token is not None:
            zero = token[:1, :1]

    gs["norm_mix"], gs["norm_ffn"], gs["norm_ple"] = (jnp.concatenate(g, axis=0) for g in (g_mix, g_ffn, g_ple))
    gs["f_conv"] = jnp.stack(g_conv)
    return loss, dh, gs


BIG = ["a_w_in", "a_w_out", "b_w_in", "b_w_out", "f_w_up", "f_w_down", "ple_w_proj", "ple_w_gate"]
LAYERED = {"f_w_up", "f_w_down", "ple_w_proj", "ple_w_gate"}
BY_CHIP = {"b_w_in", "f_w_up", "ple_w_proj"}
LAYER_UNITS = [[("a_w_in", 0), ("a_w_out", 0)] + [(n, 0) for n in sorted(LAYERED)],
               [("b_w_in", 1), ("b_w_out", 1)] + [(n, 1) for n in sorted(LAYERED)]]
CONVS = ["a_conv", "f_conv"]
SMALL = ["norm_mix", "norm_ffn", "norm_ple", "norm_final", "a_log", "a_dt_bias", "a_norm", "b_sinks"]
SMALL_ROWS = 8
CONV_ROWS = 16
CONV_GRAD_ROWS = 48


def _pack_rows(arrs, rows, dtype):
    flat = jnp.concatenate([a.reshape(-1).astype(dtype) for a in arrs])
    return jnp.pad(flat, (0, rows * PACK_COLS - flat.shape[0])).reshape(rows, PACK_COLS)


def _unpack(flat, shapes):
    out, off = [], 0
    for shp in shapes:
        n = math.prod(shp)
        out.append(flat[off:off + n].reshape(shp))
        off += n
    return out


def _pack_small(d, loss=None):
    tail = jnp.concatenate([d["a_log"].reshape(-1), d["a_dt_bias"].reshape(-1), d["a_norm"].reshape(-1), d["b_sinks"].reshape(-1)])
    if loss is not None:
        tail = jnp.concatenate([tail, loss.reshape(-1)[:1]])
    tail = jnp.pad(tail, (0, PACK_COLS - tail.shape[0]))
    return jnp.concatenate([d["norm_mix"], d["norm_ffn"], d["norm_ple"], d["norm_final"][None, :], tail[None, :]], axis=0)


def _unpack_small(a, like):
    out = {"norm_mix": a[0:2], "norm_ffn": a[2:4], "norm_ple": a[4:6], "norm_final": a[6]}
    off = 0
    for nm in ("a_log", "a_dt_bias", "a_norm", "b_sinks"):
        n = like[nm].size
        out[nm] = a[7, off:off + n].reshape(like[nm].shape)
        off += n
    return out, a[7, off]


def _as2d(a):
    return a.reshape(-1, a.shape[-1])


def kernel(x, p, norm_mix, norm_ffn, norm_ple, norm_final, a_w_in, a_conv, a_log, a_dt_bias, a_norm, a_w_out, b_w_in, b_sinks, b_w_out, f_w_up, f_conv, f_w_down, ple_w_proj, ple_w_gate, loss_target, m_norm_mix, m_norm_ffn, m_norm_ple, m_norm_final, m_a_w_in, m_a_conv, m_a_log, m_a_dt_bias, m_a_norm, m_a_w_out, m_b_w_in, m_b_sinks, m_b_w_out, m_f_w_up, m_f_conv, m_f_w_down, m_ple_w_proj, m_ple_w_gate, v_norm_mix, v_norm_ffn, v_norm_ple, v_norm_final, v_a_w_in, v_a_conv, v_a_log, v_a_dt_bias, v_a_norm, v_a_w_out, v_b_w_in, v_b_sinks, v_b_w_out, v_f_w_up, v_f_conv, v_f_w_down, v_ple_w_proj, v_ple_w_gate):
    w = dict(norm_mix=norm_mix, norm_ffn=norm_ffn, norm_ple=norm_ple, norm_final=norm_final, a_w_in=a_w_in, a_conv=a_conv,
             a_log=a_log, a_dt_bias=a_dt_bias, a_norm=a_norm, a_w_out=a_w_out, b_w_in=b_w_in, b_sinks=b_sinks, b_w_out=b_w_out,
             f_w_up=f_w_up, f_conv=f_conv, f_w_down=f_w_down, ple_w_proj=ple_w_proj, ple_w_gate=ple_w_gate)
    m = dict(norm_mix=m_norm_mix, norm_ffn=m_norm_ffn, norm_ple=m_norm_ple, norm_final=m_norm_final, a_w_in=m_a_w_in,
             a_conv=m_a_conv, a_log=m_a_log, a_dt_bias=m_a_dt_bias, a_norm=m_a_norm, a_w_out=m_a_w_out, b_w_in=m_b_w_in,
             b_sinks=m_b_sinks, b_w_out=m_b_w_out, f_w_up=m_f_w_up, f_conv=m_f_conv, f_w_down=m_f_w_down,
             ple_w_proj=m_ple_w_proj, ple_w_gate=m_ple_w_gate)
    v = dict(norm_mix=v_norm_mix, norm_ffn=v_norm_ffn, norm_ple=v_norm_ple, norm_final=v_norm_final, a_w_in=v_a_w_in,
             a_conv=v_a_conv, a_log=v_a_log, a_dt_bias=v_a_dt_bias, a_norm=v_a_norm, a_w_out=v_a_w_out, b_w_in=v_b_w_in,
             b_sinks=v_b_sinks, b_w_out=v_b_w_out, f_w_up=v_f_w_up, f_conv=v_f_conv, f_w_down=v_f_w_down,
             ple_w_proj=v_ple_w_proj, ple_w_gate=v_ple_w_gate)
    xc, yc, cc = _place()
    my_chip = 2 * xc + yc

    shard = {(n, i): w[n][i if n in LAYERED else 0].astype(BF16) for n, i in LAYER_UNITS[0] + LAYER_UNITS[1]}
    first = shard["a_w_in", 0]
    (ga,) = gather_units([(first, False)], name="gather_first")
    ga = lax.dynamic_update_index_in_dim(ga, first, my_chip, 0)
    a_in = jnp.concatenate([ga[j] for j in range(N_CHIPS)], axis=1)
    n_main = 4 * N_HEADS_A * HEAD_DIM_A
    conv_shapes = [w[n].shape for n in CONVS]
    convs = allgather8(_pack_rows([w[n] for n in CONVS], CONV_ROWS, F32), name="gather_convs")
    conv_parts = [_unpack(convs[2 * j].reshape(-1), conv_shapes) for j in range(N_CHIPS)]
    a_conv_full, f_conv_full = (jnp.concatenate([conv_parts[j][q] for j in range(N_CHIPS)], axis=2) for q in range(2))
    ready = {("a_main", 0): a_in[:, :n_main], ("a_tail", 0): jnp.pad(a_in[:, n_main:], ((0, 0), (0, LANES - 2 * N_HEADS_A))),
             ("a_conv", 0): a_conv_full[0], ("f_conv", 0): f_conv_full[0], ("f_conv", 1): f_conv_full[1]}
    later = [[k for k in units if k != ("a_w_in", 0)] for units in LAYER_UNITS]
    pending, after = [], ga
    for layer, keys in enumerate(later):
        pending.append(gather_start([shard[k] for k in keys], after, name=f"gather_start{layer}"))
        after = pending[-1][4]
    sm = {n: w[n] for n in SMALL}
    sm["norm_mix"] = sm["norm_mix"] + after[:1, :1]

    def weight(name, layer, act):
        if (name, layer) not in ready:
            landed = gather_wait(pending[layer], act, name=f"gather_wait{layer}")
            for k, g in zip(later[layer], landed):
                g = lax.dynamic_update_index_in_dim(g, shard[k], my_chip, 0)
                ready[k] = g if k[0] in BY_CHIP else g.reshape(N_CHIPS * g.shape[1], g.shape[2])
        return ready[name, layer]

    pairs, scattered, started = {}, {}, []

    def on_grads(layer, part, gw):
        keys = [k for k in LAYER_UNITS[layer] if (k[0] in LAYERED) == (part == "ffn")]
        from_sib = swap_units([gw[n] for n, _ in keys], name=f"rs_swap_{part}{layer}")
        for (n, _), sib in zip(keys, from_sib):
            rh, cols = sib.shape[1:]
            mine = lax.dynamic_slice_in_dim(gw[n], cc * rh, rh, axis=1)
            pairs[n, layer] = add_parts([mine.reshape(N_CHIPS * rh, cols), sib.reshape(N_CHIPS * rh, cols)], out_dtype=BF16,
                                        name=f"rs_add_pair_{n}{layer}").reshape(N_CHIPS, rh, cols)
        if (layer, part) == (0, "mix"):
            scattered.update(zip(keys, scatter_units([pairs[k] for k in keys], name="rs_scatter_last")))
            return None
        started.append((keys, scatter_start([pairs[k] for k in keys], name=f"rs_scatter_start_{part}{layer}"), f"{part}{layer}"))
        return started[-1][1][4]

    loss, grad_x, gs = local_step(x[0], p[:, 0], loss_target[0], sm, weight, on_grads)

    for keys, pend, tag in started:
        scattered.update(zip(keys, scatter_wait(pend, grad_x, name=f"rs_scatter_wait_{tag}")))
    all_units = LAYER_UNITS[0] + LAYER_UNITS[1]
    halves = [add_parts([lax.dynamic_index_in_dim(pairs[k], my_chip, axis=0, keepdims=False)] + [(scattered[k], j) for j in range(N_CHIPS - 1)],
                        out_dtype=F32, name=f"rs_add_chips_{k[0]}{k[1]}") for k in all_units]
    from_sib = join_units(halves, name="rs_join")
    g_unit = {k: jnp.where(cc == 0, jnp.concatenate([hf, ot]), jnp.concatenate([ot, hf])) for k, hf, ot in zip(all_units, halves, from_sib)}

    conv_grads = _pack_rows([gs[n] for n in CONVS], CONV_GRAD_ROWS, F32)
    small_sum = sum8(allgather8(jnp.concatenate([_pack_small(gs, loss), conv_grads]), name="gather_small"), name="sum_small")
    g_sm, loss_sum = _unpack_small(small_sum[:SMALL_ROWS], sm)

    grads, delta, new_m, new_v = {}, {}, {}, {}
    for n in BIG:
        g_layers = [g_unit[n, i] for i in range(DEPTH) if (n, i) in g_unit]
        shape3 = (len(g_layers),) + g_layers[0].shape
        res = adamw_layers(w[n].reshape(shape3), g_layers, m[n].reshape(shape3), v[n].reshape(shape3), name=f"adamw_{n}")
        delta[n], new_m[n], new_v[n], grads[n] = (r.reshape(w[n].shape) for r in res)
    for n, full in zip(CONVS, _unpack(small_sum[SMALL_ROWS:].reshape(-1), [gs[n].shape for n in CONVS])):
        g2 = _as2d(lax.dynamic_slice_in_dim(full, my_chip * w[n].shape[-1], w[n].shape[-1], axis=full.ndim - 1))
        d2, m2, v2 = adamw(_as2d(w[n]), g2, _as2d(m[n]), _as2d(v[n]), name=f"adamw_{n}")
        grads[n], delta[n], new_m[n], new_v[n] = (r.reshape(w[n].shape) for r in (g2, d2, m2, v2))
    pk = lambda d: _pack_small(d)
    d2, m2, v2 = adamw(pk(sm), pk(g_sm), pk({n: m[n] for n in SMALL}), pk({n: v[n] for n in SMALL}), name="adamw_small")
    for src, dst in ((d2, delta), (m2, new_m), (v2, new_v)):
        dst.update(_unpack_small(src, sm)[0])
    grads.update(g_sm)

    order = ["norm_mix", "norm_ffn", "norm_ple", "norm_final", "a_w_in", "a_conv", "a_log", "a_dt_bias", "a_norm", "a_w_out",
             "b_w_in", "b_sinks", "b_w_out", "f_w_up", "f_conv", "f_w_down", "ple_w_proj", "ple_w_gate"]
    return (loss_sum, grad_x[None], *[grads[n] for n in order], *[delta[n] for n in order],
            *[new_m[n] for n in order], *[new_v[n] for n in order])
```

```python
import functools
import math

import jax
import jax.numpy as jnp
from jax import lax
from jax.experimental import pallas as pl
from jax.experimental.pallas import tpu as pltpu

F32 = jnp.float32
BF16 = jnp.bfloat16
MESH = pl.DeviceIdType.MESH

D_MODEL = 1024
N_HEADS_A = 8
HEAD_DIM_A = 128
CONV_A = 4
N_HEADS_B = 16
N_KV_B = 4
HEAD_DIM_B = 64
WINDOW = 128
D_FF = 2816
FFN_CONV = 3
PLE_DIM = 256
EPS = 1e-6
DEPTH = 2

ADAM_LR = 0.001
ADAM_B1 = 0.9
ADAM_B2 = 0.999
ADAM_EPS = 1e-08
ADAM_WD = 0.01
ADAM_STEP = 10

LANES = 128
SUBLANES = 8
BF16_ROWS = 16
CHUNK = 128
VMEM_LIMIT = 56 * 1024 * 1024
NEG = -1e30
N_CHIPS = 4
N_DEV = 8
PACK_COLS = 1024


def _params(sem=None):
    return pltpu.CompilerParams(dimension_semantics=sem, vmem_limit_bytes=VMEM_LIMIT)


def _tile(dim, cap):
    if dim % LANES:
        return dim
    best = LANES
    for t in range(LANES, min(dim, cap) + 1, LANES):
        if dim % t == 0:
            best = t
    return best


def _dot(a, b, dims=(((1,), (0,)), ((), ())), precision=None):
    return lax.dot_general(a, b, dims, precision=precision, preferred_element_type=F32)


NN = (((1,), (0,)), ((), ()))
NT = (((1,), (1,)), ((), ()))
TN = (((0,), (0,)), ((), ()))


MM_TM_CAP = 1024
MM_TK_CAP_TOKENS = 2048


def mm(a, b, *, name, ta=False, tb=False, out_dtype=F32, add=None, norm_w=None, tm_cap=MM_TM_CAP, tn_cap=1408, tk_cap=1408,
       n=None, tn=None, tk=None, b_spec=None, o_spec=None, o_shape=None, into=None):
    m, k = (a.shape[1], a.shape[0]) if ta else a.shape
    if b_spec is None:
        n = b.shape[0] if tb else b.shape[1]
        assert (b.shape[1] if tb else b.shape[0]) == k, (a.shape, b.shape, ta, tb)
    tm, tn, tk = _tile(m, tm_cap), tn or _tile(n, tn_cap), tk or _tile(k, MM_TK_CAP_TOKENS if ta else tk_cap)
    assert n % tn == 0 and k % tk == 0, (n, tn, k, tk)
    nk = k // tk
    dims = (((0 if ta else 1,), (1 if tb else 0,)), ((), ()))
    has_add, has_norm = add is not None, norm_w is not None
    assert not has_norm or (tn == n and o_spec is None), "the norm epilogue needs whole rows"
    n_in = 2 + has_add + has_norm + (into is not None)

    def body(*refs):
        a_ref, b_ref = refs[0], refs[1]
        add_ref = refs[2] if has_add else None
        o_ref = refs[n_in]
        part = _dot(a_ref[...].astype(BF16), b_ref[...].astype(BF16), dims)

        def finish(r):
            if has_add:
                r = r + add_ref[...].astype(F32)
            o_ref[...] = r.astype(o_ref.dtype)
            if has_norm:
                refs[n_in + 1][...] = _f_norm(r, refs[2 + has_add][...]).astype(BF16)

        if nk == 1:
            finish(part)
            return
        acc = refs[-1]
        kk = pl.program_id(2)

        @pl.when(kk == 0)
        def _():
            acc[...] = part

        @pl.when(kk > 0)
        def _():
            acc[...] += part

        @pl.when(kk == nk - 1)
        def _():
            finish(acc[...])

    a_spec = pl.BlockSpec((tk, tm), lambda i, j, kk: (kk, i)) if ta else pl.BlockSpec((tm, tk), lambda i, j, kk: (i, kk))
    if b_spec is None:
        b_spec = pl.BlockSpec((tn, tk), lambda i, j, kk: (j, kk)) if tb else pl.BlockSpec((tk, tn), lambda i, j, kk: (kk, j))
    plain_o = pl.BlockSpec((tm, tn), lambda i, j, kk: (i, j))
    if o_spec is None:
        o_spec, o_shape = plain_o, (m, n)
    in_specs = [a_spec, b_spec] + ([plain_o] if has_add else [])
    args = (a, b) + ((add,) if has_add else ())
    out_specs, out_shapes = o_spec, jax.ShapeDtypeStruct(tuple(o_shape), out_dtype)
    if has_norm:
        in_specs.append(pl.BlockSpec((1, n), lambda i, j, kk: (0, 0)))
        args += (norm_w,)
        out_specs, out_shapes = [o_spec, plain_o], [out_shapes, jax.ShapeDtypeStruct((m, n), BF16)]
    aliases = {}
    if into is not None:
        assert into.shape == tuple(o_shape) and into.dtype == out_dtype, (into.shape, o_shape)
        in_specs.append(pl.BlockSpec(memory_space=pl.ANY))
        args += (into,)
        aliases = {n_in - 1: 0}
    return pl.pallas_call(
        body, grid=(m // tm, n // tn, nk), in_specs=in_specs, out_specs=out_specs,
        out_shape=out_shapes, name=name, input_output_aliases=aliases,
        scratch_shapes=[pltpu.VMEM((tm, tn), F32)] if nk > 1 else [],
        compiler_params=_params(("parallel", "parallel", "arbitrary")),
    )(*args)


def _row_spec(tm, cw, coff):
    return pl.BlockSpec((tm, cw), lambda i, j: (i, j + coff))


def _full_spec(shape):
    return pl.BlockSpec(shape, lambda i, j: (0,) * len(shape))


def tile_map(fn, rows, params, outs, *, tm, ncol, name):
    t = rows[0][0].shape[0]
    nin = len(rows) + len(params)

    def body(*refs):
        res = fn(*[r[...] for r in refs[:nin]])
        res = res if isinstance(res, (tuple, list)) else (res,)
        for o_ref, r in zip(refs[nin:], res):
            o_ref[...] = r.astype(o_ref.dtype)

    in_specs = [_row_spec(tm, cw, coff) for (_, cw, coff) in rows] + [_full_spec(p.shape) for p in params]
    res = pl.pallas_call(
        body, grid=(t // tm, ncol), in_specs=in_specs,
        out_specs=[_row_spec(tm, cw, 0) for (cw, _) in outs],
        out_shape=[jax.ShapeDtypeStruct((t, cw * ncol), dt) for (cw, dt) in outs], name=name,
        compiler_params=_params(("parallel", "parallel")),
    )(*[r[0] for r in rows], *params)
    return res


def tile_vjp(fn, rows, params, cts, *, n_diff, tm, ncol, name, add=None, grad_dtypes=None):
    t = rows[0][0].shape[0]
    nr, npar, nct = len(rows), len(params), len(cts)
    has_add = add is not None

    def body(*refs):
        vals = [r[...] for r in refs[:nr + npar + nct + (1 if has_add else 0)]]
        diff, rest, pars = vals[:n_diff], vals[n_diff:nr], vals[nr:nr + npar]
        ctv = vals[nr + npar:nr + npar + nct]
        outs_ref = refs[nr + npar + nct + (1 if has_add else 0):]

        def f(*a):
            res = fn(*a[:n_diff], *rest, *a[n_diff:])
            return tuple(res) if isinstance(res, (tuple, list)) else (res,)

        primal, vjp = jax.vjp(f, *[d.astype(F32) for d in diff], *pars)
        grads = vjp(tuple(c.astype(o.dtype) for c, o in zip(ctv, primal)))
        for q in range(n_diff):
            g = grads[q]
            if has_add and q == 0:
                g = g + vals[-1]
            outs_ref[q][...] = g.astype(outs_ref[q].dtype)
        first = (pl.program_id(0) == 0) & (pl.program_id(1) == 0)
        for q in range(npar):
            o_ref, g = outs_ref[n_diff + q], grads[n_diff + q]

            @pl.when(first)
            def _(o_ref=o_ref, g=g):
                o_ref[...] = g

            @pl.when(jnp.logical_not(first))
            def _(o_ref=o_ref, g=g):
                o_ref[...] += g

    ins = list(rows) + [None] * 0
    in_specs = [_row_spec(tm, cw, coff) for (_, cw, coff) in rows] + [_full_spec(p.shape) for p in params]
    in_specs += [_row_spec(tm, cw, coff) for (_, cw, coff) in cts]
    args = [r[0] for r in rows] + list(params) + [c[0] for c in cts]
    if has_add:
        in_specs.append(_row_spec(tm, add[1], add[2]))
        args.append(add[0])
    out_specs = [_row_spec(tm, rows[q][1], 0) for q in range(n_diff)] + [_full_spec(p.shape) for p in params]
    grad_dtypes = grad_dtypes or [F32] * n_diff
    out_shape = [jax.ShapeDtypeStruct((t, rows[q][1] * ncol), grad_dtypes[q]) for q in range(n_diff)]
    out_shape += [jax.ShapeDtypeStruct(p.shape, F32) for p in params]
    del ins
    res = pl.pallas_call(
        body, grid=(t // tm, ncol), in_specs=in_specs, out_specs=out_specs, out_shape=out_shape, name=name,
        compiler_params=_params(("arbitrary", "arbitrary")),
    )(*args)
    return res[:n_diff], res[n_diff:]


def _silu(x):
    return x * jax.nn.sigmoid(x)


def _f_norm(h, w):
    return h * lax.rsqrt(jnp.mean(h * h, axis=-1, keepdims=True) + EPS) * w


def _f_gnorm(o, z, w):
    return _f_norm(o, w) * _silu(z)


def _f_act(gate, val):
    return _silu(gate) * val


def _f_ple(gl, pe):
    return jax.nn.sigmoid(gl) * pe


def _f_betag(pt, alog, dtb):
    lane = lax.broadcasted_iota(jnp.int32, (1, LANES), 1)
    z = pt + dtb
    softplus = jnp.maximum(z, 0.0) + jnp.log(1.0 + jnp.exp(-jnp.abs(z)))
    g = -jnp.exp(alog) * softplus
    return jnp.where(lane < N_HEADS_A, jax.nn.sigmoid(pt), jnp.where(lane < 2 * N_HEADS_A, g, 0.0))


CONV_TM = 256
CONV_CW = 1024


def _shift_down(x, prev, s, row):
    rp = jnp.tile(pltpu.roll(prev, s, 0), (x.shape[0] // SUBLANES, 1))
    return jnp.where(row < s, rp, pltpu.roll(x, s, 0))


def _shift_up(x, nxt, s, row):
    tm = x.shape[0]
    rn = jnp.tile(pltpu.roll(nxt, SUBLANES - s, 0), (tm // SUBLANES, 1))
    return jnp.where(row >= tm - s, rn, pltpu.roll(x, tm - s, 0))


def _conv_taps(x, prev, w_ref, cols, row):
    k = w_ref.shape[0]
    y = x * w_ref[pl.ds(k - 1, 1), cols]
    for s in range(1, k):
        y = y + _shift_down(x, prev, s, row) * w_ref[pl.ds(k - 1 - s, 1), cols]
    return y


def _lane_chunks(cw):
    return [slice(cb * LANES, (cb + 1) * LANES) for cb in range(cw // LANES)]


def conv_fwd(x, w, *, name):
    t = x.shape[0]
    k, c = w.shape
    tm, cw = min(CONV_TM, t), CONV_CW
    nb8 = tm // SUBLANES

    def body(x_ref, p_ref, w_ref, o_ref):
        first = pl.program_id(1) == 0
        row = lax.broadcasted_iota(jnp.int32, (tm, LANES), 0)
        for cols in _lane_chunks(cw):
            o_ref[:, cols] = _conv_taps(x_ref[:, cols], jnp.where(first, 0.0, p_ref[:, cols]), w_ref, cols, row)

    return pl.pallas_call(
        body, grid=(c // cw, t // tm),
        in_specs=[pl.BlockSpec((tm, cw), lambda j, i: (i, j)),
                  pl.BlockSpec((SUBLANES, cw), lambda j, i: (jnp.maximum(i * nb8 - 1, 0), j)),
                  pl.BlockSpec((k, cw), lambda j, i: (0, j))],
        out_specs=pl.BlockSpec((tm, cw), lambda j, i: (i, j)),
        out_shape=jax.ShapeDtypeStruct((t, c), F32), name=name,
        compiler_params=_params(("parallel", "parallel")),
    )(x, x, w)


def conv_bwd(dy, x, w, *, name):
    t = x.shape[0]
    k, c = w.shape
    tm, cw = min(CONV_TM, t), CONV_CW
    nb8 = tm // SUBLANES
    ni = t // tm

    def body(dy_ref, dn_ref, x_ref, p_ref, w_ref, dx_ref, dw_ref):
        i = pl.program_id(1)
        first, last = i == 0, i == ni - 1
        row = lax.broadcasted_iota(jnp.int32, (tm, LANES), 0)
        for cols in _lane_chunks(cw):
            dyv, xv = dy_ref[:, cols], x_ref[:, cols]
            nxt = jnp.where(last, 0.0, dn_ref[:, cols])
            prev = jnp.where(first, 0.0, p_ref[:, cols])
            dx = dyv * w_ref[pl.ds(k - 1, 1), cols]
            dws = [jnp.sum(dyv * xv, axis=0, keepdims=True)]
            for s in range(1, k):
                dx = dx + _shift_up(dyv, nxt, s, row) * w_ref[pl.ds(k - 1 - s, 1), cols]
                dws.append(jnp.sum(dyv * _shift_down(xv, prev, s, row), axis=0, keepdims=True))
            dx_ref[:, cols] = dx.astype(dx_ref.dtype)
            for s in range(k):
                @pl.when(first)
                def _(s=s, dws=dws, cols=cols):
                    dw_ref[pl.ds(k - 1 - s, 1), cols] = dws[s]

                @pl.when(jnp.logical_not(first))
                def _(s=s, dws=dws, cols=cols):
                    dw_ref[pl.ds(k - 1 - s, 1), cols] += dws[s]

    return pl.pallas_call(
        body, grid=(c // cw, ni),
        in_specs=[pl.BlockSpec((tm, cw), lambda j, i: (i, j)),
                  pl.BlockSpec((SUBLANES, cw), lambda j, i: (jnp.minimum((i + 1) * nb8, t // SUBLANES - 1), j)),
                  pl.BlockSpec((tm, cw), lambda j, i: (i, j)),
                  pl.BlockSpec((SUBLANES, cw), lambda j, i: (jnp.maximum(i * nb8 - 1, 0), j)),
                  pl.BlockSpec((k, cw), lambda j, i: (0, j))],
        out_specs=[pl.BlockSpec((tm, cw), lambda j, i: (i, j)), pl.BlockSpec((k, cw), lambda j, i: (0, j))],
        out_shape=[jax.ShapeDtypeStruct((t, c), BF16), jax.ShapeDtypeStruct((k, c), F32)], name=name,
        compiler_params=_params(("parallel", "arbitrary")),
    )(dy, dy, x, x, w)


FFN_TM = 128
FFN_CW = D_FF // 2


def _ffn_specs(t, tm, cw, k):
    nb8, ncol = tm // SUBLANES, D_FF // cw
    cur = lambda off: pl.BlockSpec((tm, cw), lambda j, i: (i, j + off))
    prev = lambda off: pl.BlockSpec((SUBLANES, cw), lambda j, i: (jnp.maximum(i * nb8 - 1, 0), j + off))
    nxt = lambda off: pl.BlockSpec((SUBLANES, cw), lambda j, i: (jnp.minimum((i + 1) * nb8, t // SUBLANES - 1), j + off))
    taps = lambda off: pl.BlockSpec((k, cw), lambda j, i: (0, j + off))
    return cur, prev, nxt, taps, ncol


def conv_act_fwd(u, w, *, name):
    t, k = u.shape[0], w.shape[0]
    tm, cw = min(FFN_TM, t), FFN_CW
    cur, prev, _, taps, ncol = _ffn_specs(t, tm, cw, k)

    def body(ug_ref, pg_ref, uv_ref, pv_ref, wg_ref, wv_ref, o_ref):
        first = pl.program_id(1) == 0
        row = lax.broadcasted_iota(jnp.int32, (tm, LANES), 0)
        for cb in range(cw // LANES):
            cols = slice(cb * LANES, (cb + 1) * LANES)
            cg = _conv_taps(ug_ref[:, cols], jnp.where(first, 0.0, pg_ref[:, cols]), wg_ref, cols, row)
            cv = _conv_taps(uv_ref[:, cols], jnp.where(first, 0.0, pv_ref[:, cols]), wv_ref, cols, row)
            o_ref[:, cols] = _f_act(cg, cv).astype(o_ref.dtype)

    return pl.pallas_call(
        body, grid=(ncol, t // tm),
        in_specs=[cur(0), prev(0), cur(ncol), prev(ncol), taps(0), taps(ncol)],
        out_specs=cur(0), out_shape=jax.ShapeDtypeStruct((t, D_FF), BF16), name=name,
        compiler_params=_params(("parallel", "parallel")),
    )(u, u, u, u, w, w)


def conv_act_bwd(u, dact, w, *, name):
    t, k = u.shape[0], w.shape[0]
    tm, cw = min(FFN_TM, t), FFN_CW
    cur, prev, nxt, taps, ncol = _ffn_specs(t, tm, cw, k)
    ni = t // tm

    def body(ug_ref, pg_ref, ng_ref, uv_ref, pv_ref, nv_ref, d_ref, dn_ref, wg_ref, wv_ref, dg_ref, dv_ref, dwg_ref, dwv_ref):
        i = pl.program_id(1)
        first, last = i == 0, i == ni - 1
        row = lax.broadcasted_iota(jnp.int32, (tm, LANES), 0)
        row8 = lax.broadcasted_iota(jnp.int32, (SUBLANES, LANES), 0)
        for cb in range(cw // LANES):
            cols = slice(cb * LANES, (cb + 1) * LANES)
            ug, uv = ug_ref[:, cols], uv_ref[:, cols]
            pg, pv = jnp.where(first, 0.0, pg_ref[:, cols]), jnp.where(first, 0.0, pv_ref[:, cols])
            sg = [ug] + [_shift_down(ug, pg, s, row) for s in range(1, k)]
            sv = [uv] + [_shift_down(uv, pv, s, row) for s in range(1, k)]
            taps = lambda xs, w_ref: sum(xs[s] * w_ref[pl.ds(k - 1 - s, 1), cols] for s in range(k))
            _, vjp = jax.vjp(_f_act, taps(sg, wg_ref), taps(sv, wv_ref))
            dcg, dcv = vjp(d_ref[:, cols])
            _, vjp_n = jax.vjp(_f_act, _conv_taps(ng_ref[:, cols], ug[tm - SUBLANES:], wg_ref, cols, row8),
                               _conv_taps(nv_ref[:, cols], uv[tm - SUBLANES:], wv_ref, cols, row8))
            dcgn, dcvn = vjp_n(jnp.where(last, 0.0, dn_ref[:, cols]))
            for dc, dcn, xs, w_ref, dx_ref, dw_ref in ((dcg, dcgn, sg, wg_ref, dg_ref, dwg_ref),
                                                       (dcv, dcvn, sv, wv_ref, dv_ref, dwv_ref)):
                dx = dc * w_ref[pl.ds(k - 1, 1), cols]
                dws = [jnp.sum(dc * xs[0], axis=0, keepdims=True)]
                for s in range(1, k):
                    dx = dx + _shift_up(dc, dcn, s, row) * w_ref[pl.ds(k - 1 - s, 1), cols]
                    dws.append(jnp.sum(dc * xs[s], axis=0, keepdims=True))
                dx_ref[:, cols] = dx.astype(dx_ref.dtype)
                for s in range(k):
                    @pl.when(first)
                    def _(s=s, dw_ref=dw_ref, dws=dws):
                        dw_ref[pl.ds(k - 1 - s, 1), cols] = dws[s]

                    @pl.when(jnp.logical_not(first))
                    def _(s=s, dw_ref=dw_ref, dws=dws):
                        dw_ref[pl.ds(k - 1 - s, 1), cols] += dws[s]

    half = jax.ShapeDtypeStruct((t, D_FF), BF16)
    dwh = jax.ShapeDtypeStruct((k, D_FF), F32)
    return pl.pallas_call(
        body, grid=(ncol, ni),
        in_specs=[cur(0), prev(0), nxt(0), cur(ncol), prev(ncol), nxt(ncol), cur(0), nxt(0), taps(0), taps(ncol)],
        out_specs=[cur(0), cur(0), taps(0), taps(0)], out_shape=[half, half, dwh, dwh], name=name,
        compiler_params=_params(("parallel", "arbitrary")),
    )(u, u, u, u, u, u, dact, dact, w, w)


def _each(f, *lists):
    return [f(*a) for a in zip(*lists)]


@jax.custom_vjp
def _inv_unit_lower(lms):
    return _inv_blocks(lms)


def _inv_blocks(lms):
    c = lms[0].shape[0]
    ri = lax.broadcasted_iota(jnp.int32, (c, c), 0)
    ci = lax.broadcasted_iota(jnp.int32, (c, c), 1)
    eye = (ri == ci).astype(F32)
    dms = _each(lambda lm: eye - jnp.where((ri >> 1) == (ci >> 1), lm, 0.0), lms)
    for lv in range(1, int(math.log2(c))):
        below = ((ri >> (lv + 1)) == (ci >> (lv + 1))) & ((ri >> lv) != (ci >> lv))
        dbs = _each(lambda dm: dm.astype(BF16), dms)
        ods = _each(lambda lm, db: _dot(jnp.where(below, lm, 0.0).astype(BF16), db).astype(BF16), lms, dbs)
        dms = _each(lambda dm, db, od: dm - _dot(db, od), dms, dbs, ods)
    return dms


def _inv_fwd(lms):
    tms = _inv_blocks(lms)
    return tms, tms


def _inv_bwd(tms, dts):
    tbs = _each(lambda tm: tm.astype(BF16), tms)
    mid = _each(lambda tb, dt: _dot(tb, dt.astype(BF16), TN).astype(BF16), tbs, dts)
    return (_each(lambda m, tb: -_dot(m, tb, NT), mid, tbs),)


_inv_unit_lower.defvjp(_inv_fwd, _inv_bwd)


def _l2n(x):
    return x * lax.rsqrt(jnp.sum(x * x, axis=-1, keepdims=True) + EPS)


def _prep_fn(cqs, cks, cvs, bg, sel_b, sel_g):
    c = cqs[0].shape[0]
    ri = lax.broadcasted_iota(jnp.int32, (c, c), 0)
    ci = lax.broadcasted_iota(jnp.int32, (c, c), 1)
    eye = (ri == ci).astype(F32)
    incl, strict = ci <= ri, ci < ri
    last = lax.broadcasted_iota(jnp.int32, (c, 1), 0) == c - 1
    to_row = lambda col: jnp.sum(col * eye, axis=0, keepdims=True)
    qs = _each(lambda a: _l2n(_silu(a)) * (HEAD_DIM_A ** -0.5), cqs)
    ks = _each(lambda a: _l2n(_silu(a)), cks)
    vbs = _each(lambda a: _silu(a).astype(BF16), cvs)
    betas = _each(lambda m: jnp.sum(bg * m, axis=1, keepdims=True), sel_b)
    gs = _each(lambda m: jnp.sum(bg * m, axis=1, keepdims=True), sel_g)
    gcss = _each(lambda g: jnp.sum(jnp.where(incl, to_row(g), 0.0), axis=1, keepdims=True), gs)
    gtots = _each(lambda gcs: jnp.sum(jnp.where(last, gcs, 0.0), axis=0, keepdims=True), gcss)
    decays = _each(lambda gcs: jnp.exp(jnp.where(incl, gcs - to_row(gcs), NEG)), gcss)
    kbs = _each(lambda k: k.astype(BF16), ks)
    lms = _each(lambda beta, kb, dec: jnp.where(strict, beta * _dot(kb, kb, NT) * dec, 0.0), betas, kbs, decays)
    ams = _each(lambda tm, beta: (tm * to_row(beta)).astype(BF16), _inv_unit_lower(lms), betas)
    gams = _each(jnp.exp, gcss)
    u0s = _each(_dot, ams, vbs)
    wks = _each(lambda am, gam, k: _dot(am, (gam * k).astype(BF16)), ams, gams, ks)
    qks = _each(lambda q, kb, dec: _dot(q.astype(BF16), kb, NT) * dec, qs, kbs, decays)
    qds = _each(lambda q, gam: q * gam, qs, gams)
    kds = _each(lambda k, gtot, gcs: k * jnp.exp(gtot - gcs), ks, gtots, gcss)
    gls = _each(lambda gtot: jnp.exp(gtot) * jnp.ones((SUBLANES, LANES), F32), gtots)
    return u0s, wks, qds, kds, qks, gls


def _head_masks(h):
    lane = lax.broadcasted_iota(jnp.int32, (1, LANES), 1)
    return (lane == h).astype(F32), (lane == h + N_HEADS_A).astype(F32)


def _hsl(j):
    return slice(j * HEAD_DIM_A, (j + 1) * HEAD_DIM_A)


def gnorm_fwd(o, zsrc, w, *, name):
    t, width = o.shape
    tm = min(256, t)
    zoff = zsrc.shape[1] // width - 1

    def body(o_ref, z_ref, w_ref, out_ref):
        for h in range(N_HEADS_A):
            out_ref[:, _hsl(h)] = _f_gnorm(o_ref[:, _hsl(h)], z_ref[:, _hsl(h)], w_ref[...]).astype(out_ref.dtype)

    rows = pl.BlockSpec((tm, width), lambda i: (i, 0))
    return pl.pallas_call(
        body, grid=(t // tm,),
        in_specs=[rows, pl.BlockSpec((tm, width), lambda i: (i, zoff)), pl.BlockSpec(w.shape, lambda i: (0, 0))],
        out_specs=rows, out_shape=jax.ShapeDtypeStruct((t, width), BF16), name=name, compiler_params=_params(("parallel",)),
    )(o, zsrc, w)


def gnorm_bwd(o, zsrc, w, don, *, name):
    t, width = o.shape
    tm = min(256, t)
    zoff = zsrc.shape[1] // width - 1

    def body(o_ref, z_ref, w_ref, d_ref, do_ref, dz_ref, dw_ref):
        dw = jnp.zeros(w.shape, F32)
        for h in range(N_HEADS_A):
            _, vjp = jax.vjp(_f_gnorm, o_ref[:, _hsl(h)], z_ref[:, _hsl(h)], w_ref[...])
            do, dz, dwh = vjp(d_ref[:, _hsl(h)])
            do_ref[:, _hsl(h)] = do.astype(do_ref.dtype)
            dz_ref[:, _hsl(h)] = dz.astype(dz_ref.dtype)
            dw = dw + dwh
        first = pl.program_id(0) == 0

        @pl.when(first)
        def _():
            dw_ref[...] = dw

        @pl.when(jnp.logical_not(first))
        def _():
            dw_ref[...] += dw

    rows = pl.BlockSpec((tm, width), lambda i: (i, 0))
    wspec = pl.BlockSpec(w.shape, lambda i: (0, 0))
    return pl.pallas_call(
        body, grid=(t // tm,),
        in_specs=[rows, pl.BlockSpec((tm, width), lambda i: (i, zoff)), wspec, rows],
        out_specs=[rows, rows, wspec],
        out_shape=[jax.ShapeDtypeStruct((t, width), BF16)] * 2 + [jax.ShapeDtypeStruct(w.shape, F32)], name=name,
        compiler_params=_params(("arbitrary",)),
    )(o, zsrc, w, don)


def delta_prep(cqkv, bg, *, name):
    t = cqkv.shape[0]
    nh, hd, n = N_HEADS_A, HEAD_DIM_A, t // CHUNK

    def body(cq_ref, ck_ref, cv_ref, bg_ref, u0_ref, wk_ref, qd_ref, kd_ref, qk_ref, gl_ref):
        heads = range(nh)
        masks = [_head_masks(j) for j in heads]
        res = _prep_fn([cq_ref[:, _hsl(j)] for j in heads], [ck_ref[:, _hsl(j)] for j in heads],
                       [cv_ref[:, _hsl(j)] for j in heads], bg_ref[...], [m[0] for m in masks], [m[1] for m in masks])
        for o_ref, rs in zip((u0_ref, wk_ref, qd_ref, kd_ref, qk_ref), res[:5]):
            for j in heads:
                o_ref[:, _hsl(j)] = rs[j]
        for j in heads:
            gl_ref[j * SUBLANES:(j + 1) * SUBLANES, :] = res[5][j]

    blk = lambda off: pl.BlockSpec((CHUNK, nh * hd), lambda i: (i, off))
    return pl.pallas_call(
        body, grid=(n,),
        in_specs=[blk(0), blk(1), blk(2), pl.BlockSpec((CHUNK, LANES), lambda i: (i, 0))],
        out_specs=[blk(0)] * 5 + [pl.BlockSpec((nh * SUBLANES, LANES), lambda i: (i, 0))],
        out_shape=[jax.ShapeDtypeStruct((t, nh * hd), F32)] * 5 + [jax.ShapeDtypeStruct((n * nh * SUBLANES, LANES), F32)],
        name=name, compiler_params=_params(("parallel",)),
    )(cqkv, cqkv, cqkv, bg)


def delta_prep_bwd(cqkv, bg, cts, *, name):
    t = cqkv.shape[0]
    nh, hd, n = N_HEADS_A, HEAD_DIM_A, t // CHUNK

    def body(cq_ref, ck_ref, cv_ref, bg_ref, c0, c1, c2, c3, c4, c5, dc_ref, dbg_ref):
        heads = range(nh)
        masks = [_head_masks(j) for j in heads]
        _, vjp = jax.vjp(lambda a, b, c, d: _prep_fn(a, b, c, d, [m[0] for m in masks], [m[1] for m in masks]),
                         [cq_ref[:, _hsl(j)] for j in heads], [ck_ref[:, _hsl(j)] for j in heads],
                         [cv_ref[:, _hsl(j)] for j in heads], bg_ref[...])
        cts = tuple([c[:, _hsl(j)] for j in heads] for c in (c0, c1, c2, c3, c4))
        dqs, dks, dvs, dbg = vjp(cts + ([c5[j * SUBLANES:(j + 1) * SUBLANES, :] for j in heads],))
        for part, ds in enumerate((dqs, dks, dvs)):
            for j in heads:
                dc_ref[:, _hsl(part * nh + j)] = ds[j]
        dbg_ref[...] = dbg

    blk = lambda off: pl.BlockSpec((CHUNK, nh * hd), lambda i: (i, off))
    gl_spec = pl.BlockSpec((nh * SUBLANES, LANES), lambda i: (i, 0))
    bg_spec = pl.BlockSpec((CHUNK, LANES), lambda i: (i, 0))
    return pl.pallas_call(
        body, grid=(n,),
        in_specs=[blk(0), blk(1), blk(2), bg_spec] + [blk(0)] * 5 + [gl_spec],
        out_specs=[pl.BlockSpec((CHUNK, 3 * nh * hd), lambda i: (i, 0)), bg_spec],
        out_shape=[jax.ShapeDtypeStruct((t, 3 * nh * hd), F32), jax.ShapeDtypeStruct((t, LANES), F32)],
        name=name, compiler_params=_params(("parallel",)),
    )(cqkv, cqkv, cqkv, bg, *cts)


def delta_scan(u0, wk, qd, kd, qk, gl, *, name):
    t = u0.shape[0]
    nh, hd, n = N_HEADS_A, HEAD_DIM_A, t // CHUNK

    def body(u0_ref, wk_ref, qd_ref, kd_ref, qk_ref, gl_ref, o_ref, sin_ref, s_ref):
        @pl.when(pl.program_id(0) == 0)
        def _():
            s_ref[...] = jnp.zeros_like(s_ref)

        heads = list(range(nh))
        cols = lambda ref: [ref[:, _hsl(h)].astype(BF16) for h in heads]
        ss = [s_ref[h] for h in heads]
        for h in heads:
            sin_ref[h] = ss[h]
        sbs = _each(lambda s: s.astype(BF16), ss)
        ubs = _each(lambda h, wkb, sb: (u0_ref[:, _hsl(h)] - _dot(wkb, sb)).astype(BF16), heads, cols(wk_ref), sbs)
        os_ = _each(lambda qdb, sb, qkb, ub: _dot(qdb, sb) + _dot(qkb, ub), cols(qd_ref), sbs, cols(qk_ref), ubs)
        sn = _each(lambda h, s, kdb, ub: gl_ref[pl.ds(h * SUBLANES, 1), :] * s + _dot(kdb, ub, TN), heads, ss, cols(kd_ref), ubs)
        for h in heads:
            o_ref[:, _hsl(h)] = os_[h]
            s_ref[h] = sn[h]

    blk = pl.BlockSpec((CHUNK, nh * hd), lambda i: (i, 0))
    return pl.pallas_call(
        body, grid=(n,),
        in_specs=[blk] * 5 + [pl.BlockSpec((nh * SUBLANES, LANES), lambda i: (i, 0))],
        out_specs=[blk, pl.BlockSpec((None, nh, hd, hd), lambda i: (i, 0, 0, 0))],
        out_shape=[jax.ShapeDtypeStruct((t, nh * hd), F32), jax.ShapeDtypeStruct((n, nh, hd, hd), F32)],
        scratch_shapes=[pltpu.VMEM((nh, hd, hd), F32)], name=name,
        compiler_params=_params(("arbitrary",)),
    )(u0, wk, qd, kd, qk, gl)


def delta_scan_bwd(do, u0, wk, qd, kd, qk, gl, s_in, *, name):
    t = u0.shape[0]
    nh, hd, n = N_HEADS_A, HEAD_DIM_A, t // CHUNK

    def body(do_ref, u0_ref, wk_ref, qd_ref, kd_ref, qk_ref, gl_ref, sin_ref,
             du0_ref, dwk_ref, dqd_ref, dkd_ref, dqk_ref, dgl_ref, ds_ref):
        @pl.when(pl.program_id(0) == 0)
        def _():
            ds_ref[...] = jnp.zeros_like(ds_ref)

        corner = (lax.broadcasted_iota(jnp.int32, (SUBLANES, LANES), 0) == 0) & (lax.broadcasted_iota(jnp.int32, (SUBLANES, LANES), 1) == 0)
        heads = list(range(nh))
        cols = lambda ref: [ref[:, _hsl(h)].astype(BF16) for h in heads]
        ss, dss = [sin_ref[h] for h in heads], [ds_ref[h] for h in heads]
        sbs, dsbs = _each(lambda s: s.astype(BF16), ss), _each(lambda d: d.astype(BF16), dss)
        dobs, wkbs, qdbs, kdbs, qkbs = cols(do_ref), cols(wk_ref), cols(qd_ref), cols(kd_ref), cols(qk_ref)
        ubs = _each(lambda h, wkb, sb: (u0_ref[:, _hsl(h)] - _dot(wkb, sb)).astype(BF16), heads, wkbs, sbs)
        dus = _each(lambda qkb, dob, kdb, dsb: _dot(qkb, dob, TN) + _dot(kdb, dsb), qkbs, dobs, kdbs, dsbs)
        dubs = _each(lambda du: du.astype(BF16), dus)
        dwks = _each(lambda dub, sb: -_dot(dub, sb, NT), dubs, sbs)
        dqds = _each(lambda dob, sb: _dot(dob, sb, NT), dobs, sbs)
        dkds = _each(lambda ub, dsb: _dot(ub, dsb, NT), ubs, dsbs)
        dqks = _each(lambda dob, ub: _dot(dob, ub, NT), dobs, ubs)
        dgls = _each(lambda s, d: jnp.sum(jnp.sum(s * d, axis=1, keepdims=True), axis=0, keepdims=True), ss, dss)
        dsn = _each(lambda h, d, qdb, dob, wkb, dub: gl_ref[pl.ds(h * SUBLANES, 1), :] * d + _dot(qdb, dob, TN) - _dot(wkb, dub, TN),
                    heads, dss, qdbs, dobs, wkbs, dubs)
        for h in heads:
            du0_ref[:, _hsl(h)] = dus[h]
            dwk_ref[:, _hsl(h)] = dwks[h]
            dqd_ref[:, _hsl(h)] = dqds[h]
            dkd_ref[:, _hsl(h)] = dkds[h]
            dqk_ref[:, _hsl(h)] = dqks[h]
            dgl_ref[h * SUBLANES:(h + 1) * SUBLANES, :] = jnp.where(corner, dgls[h], 0.0)
            ds_ref[h] = dsn[h]

    blk = pl.BlockSpec((CHUNK, nh * hd), lambda i: (n - 1 - i, 0))
    gl_spec = pl.BlockSpec((nh * SUBLANES, LANES), lambda i: (n - 1 - i, 0))
    return pl.pallas_call(
        body, grid=(n,),
        in_specs=[blk] * 6 + [gl_spec, pl.BlockSpec((None, nh, hd, hd), lambda i: (n - 1 - i, 0, 0, 0))],
        out_specs=[blk] * 5 + [gl_spec],
        out_shape=[jax.ShapeDtypeStruct((t, nh * hd), F32)] * 5 + [jax.ShapeDtypeStruct((n * nh * SUBLANES, LANES), F32)],
        scratch_shapes=[pltpu.VMEM((nh, hd, hd), F32)], name=name,
        compiler_params=_params(("arbitrary",)),
    )(do, u0, wk, qd, kd, qk, gl, s_in)


N_PAIRS = N_HEADS_B // 2
PAIRS_PER_KV = N_PAIRS // N_KV_B


def _psl(j):
    return slice(j * LANES, (j + 1) * LANES)


def _att_fn(qps, kc, kp, vc, vp, sinks, kvf, first):
    w = WINDOW
    lane = lax.broadcasted_iota(jnp.int32, (1, LANES), 1)
    lo = (lane < HEAD_DIM_B).astype(F32)
    qi = lax.broadcasted_iota(jnp.int32, (w, w), 0)
    kj = lax.broadcasted_iota(jnp.int32, (w, w), 1)
    dist_c = (qi - kj).astype(F32)
    valid_c = kj <= qi
    valid_p = (kj > qi) & (first < 0.5)
    kcb, kpb, vcb, vpb = (a.astype(BF16) for a in (kc, kp, vc, vp))
    scale = HEAD_DIM_B ** -0.5
    heads = [(j, half) for j in range(PAIRS_PER_KV) for half in range(2)]
    hmasks = [lo if half == 0 else 1.0 - lo for _, half in heads]
    hds = [2.0 * (PAIRS_PER_KV * kvf + j) + half for j, half in heads]
    slopes = _each(lambda hd: jnp.exp(-(hd + 1.0) * (8.0 / N_HEADS_B * math.log(2.0))), hds)
    snks = _each(lambda hd: jnp.sum(sinks * (lane.astype(F32) == hd).astype(F32), axis=1, keepdims=True), hds)
    qhs = _each(lambda jh, hm: (qps[jh[0]] * hm).astype(BF16), heads, hmasks)
    lcs = _each(lambda qh, sl: jnp.where(valid_c, _dot(qh, kcb, NT) * scale - sl * dist_c, NEG), qhs, slopes)
    lps = _each(lambda qh, sl: jnp.where(valid_p, _dot(qh, kpb, NT) * scale - sl * (dist_c + w), NEG), qhs, slopes)
    ms = _each(lambda lc, lp, sk: lax.stop_gradient(jnp.maximum(jnp.maximum(jnp.max(lc, axis=1, keepdims=True),
                                                                            jnp.max(lp, axis=1, keepdims=True)), sk)), lcs, lps, snks)
    ecs = _each(lambda lc, m: jnp.exp(lc - m), lcs, ms)
    eps = _each(lambda lp, m: jnp.exp(lp - m), lps, ms)
    invs = _each(lambda ec, ep, sk, m: 1.0 / (jnp.sum(ec, axis=1, keepdims=True) + jnp.sum(ep, axis=1, keepdims=True) + jnp.exp(sk - m)),
                 ecs, eps, snks, ms)
    ohs = _each(lambda ec, ep, inv, hm: (_dot((ec * inv).astype(BF16), vcb) + _dot((ep * inv).astype(BF16), vpb)) * hm,
                ecs, eps, invs, hmasks)
    return [ohs[2 * j] + ohs[2 * j + 1] for j in range(PAIRS_PER_KV)]


def _scalar11(v):
    return jnp.full((1, 1), v, F32)


def swa_fwd(qsrc, kd, vd, sinks, *, name):
    t = kd.shape[0]
    nb = t // WINDOW

    def body(q_ref, kc_ref, kp_ref, vc_ref, vp_ref, s_ref, o_ref):
        first = _scalar11((pl.program_id(0) == 0).astype(F32))
        kvf = _scalar11(pl.program_id(1).astype(F32))
        outs = _att_fn([q_ref[:, _psl(j)] for j in range(PAIRS_PER_KV)], kc_ref[...], kp_ref[...], vc_ref[...], vp_ref[...],
                       s_ref[...], kvf, first)
        for j in range(PAIRS_PER_KV):
            o_ref[:, _psl(j)] = outs[j].astype(o_ref.dtype)

    cur = pl.BlockSpec((WINDOW, LANES), lambda i, kv: (i, kv))
    prev = pl.BlockSpec((WINDOW, LANES), lambda i, kv: (jnp.maximum(i - 1, 0), kv))
    qs = pl.BlockSpec((WINDOW, PAIRS_PER_KV * LANES), lambda i, kv: (i, kv))
    return pl.pallas_call(
        body, grid=(nb, N_KV_B),
        in_specs=[qs, cur, prev, cur, prev, pl.BlockSpec((1, LANES), lambda i, kv: (0, 0))],
        out_specs=qs, out_shape=jax.ShapeDtypeStruct((t, N_PAIRS * LANES), BF16), name=name,
        compiler_params=_params(("parallel", "parallel")),
    )(qsrc, kd, kd, vd, vd, sinks)


def swa_bwd(do, qsrc, kd, vd, sinks, *, name):
    t = kd.shape[0]
    nb = t // WINDOW

    def body(do_ref, q_ref, kc_ref, kp_ref, vc_ref, vp_ref, s_ref, dq_ref, dk_ref, dv_ref, ds_ref, carry_k, carry_v):
        step, kv = pl.program_id(0), pl.program_id(1)
        first = _scalar11((step == nb - 1).astype(F32))

        @pl.when((step == 0) & (kv == 0))
        def _():
            carry_k[...] = jnp.zeros_like(carry_k)
            carry_v[...] = jnp.zeros_like(carry_v)
            ds_ref[...] = jnp.zeros_like(ds_ref)

        kvf = _scalar11(kv.astype(F32))
        pairs = range(PAIRS_PER_KV)
        f32 = lambda ref: ref[...].astype(F32)
        _, vjp = jax.vjp(lambda *a: _att_fn(*a, kvf, first), [q_ref[:, _psl(j)].astype(F32) for j in pairs],
                         f32(kc_ref), f32(kp_ref), f32(vc_ref), f32(vp_ref), s_ref[...])
        dqs, dkc, dkp, dvc, dvp, dsk = vjp([do_ref[:, _psl(j)].astype(F32) for j in pairs])
        for j in pairs:
            dq_ref[:, _psl(j)] = dqs[j].astype(dq_ref.dtype)
        ds_ref[...] += dsk
        fold = lambda g: g + pltpu.roll(g, HEAD_DIM_B, 1)
        dk_ref[...] = fold(dkc + carry_k[kv]).astype(dk_ref.dtype)
        dv_ref[...] = fold(dvc + carry_v[kv]).astype(dv_ref.dtype)
        carry_k[kv] = dkp
        carry_v[kv] = dvp

    rev = lambda i: nb - 1 - i
    cur = pl.BlockSpec((WINDOW, LANES), lambda i, kv: (rev(i), kv))
    prev = pl.BlockSpec((WINDOW, LANES), lambda i, kv: (jnp.maximum(rev(i) - 1, 0), kv))
    qs = pl.BlockSpec((WINDOW, PAIRS_PER_KV * LANES), lambda i, kv: (rev(i), kv))
    sk = pl.BlockSpec((1, LANES), lambda i, kv: (0, 0))
    return pl.pallas_call(
        body, grid=(nb, N_KV_B),
        in_specs=[qs, qs, cur, prev, cur, prev, sk],
        out_specs=[qs, cur, cur, sk],
        out_shape=[jax.ShapeDtypeStruct((t, N_PAIRS * LANES), BF16), jax.ShapeDtypeStruct((t, N_KV_B * LANES), BF16),
                   jax.ShapeDtypeStruct((t, N_KV_B * LANES), BF16), jax.ShapeDtypeStruct((1, LANES), F32)],
        scratch_shapes=[pltpu.VMEM((N_KV_B, WINDOW, LANES), F32), pltpu.VMEM((N_KV_B, WINDOW, LANES), F32)],
        name=name, compiler_params=_params(("arbitrary", "arbitrary")),
    )(do, qsrc, kd, kd, vd, vd, sinks)


def loss_head(h, tgt, w, *, name):
    t, d = h.shape
    tm = min(256, t)

    def body(h_ref, t_ref, w_ref, dh_ref, dw_ref, l_ref):
        tg = t_ref[...]

        def f(hv, wv):
            err = _f_norm(hv, wv) - tg
            return 0.5 * jnp.sum(jnp.sum(err * err, axis=1, keepdims=True), axis=0, keepdims=True) * (1.0 / d)

        lv, vjp = jax.vjp(f, h_ref[...], w_ref[...])
        dh, dw = vjp(jnp.ones((1, 1), F32))
        dh_ref[...] = dh
        first = pl.program_id(0) == 0

        @pl.when(first)
        def _():
            dw_ref[...] = dw
            l_ref[...] = lv * jnp.ones((1, LANES), F32)

        @pl.when(jnp.logical_not(first))
        def _():
            dw_ref[...] += dw
            l_ref[...] += lv * jnp.ones((1, LANES), F32)

    rows = pl.BlockSpec((tm, d), lambda i: (i, 0))
    one = lambda c: pl.BlockSpec((1, c), lambda i: (0, 0))
    return pl.pallas_call(
        body, grid=(t // tm,), in_specs=[rows, rows, one(d)], out_specs=[rows, one(d), one(LANES)],
        out_shape=[jax.ShapeDtypeStruct((t, d), F32), jax.ShapeDtypeStruct((1, d), F32), jax.ShapeDtypeStruct((1, LANES), F32)],
        name=name, compiler_params=_params(("arbitrary",)),
    )(h, tgt, w)


def _row_tile(r, cap=256):
    tr = r
    if r % SUBLANES == 0:
        for cand in range(SUBLANES, min(r, cap) + 1, SUBLANES):
            if r % cand == 0:
                tr = cand
    return tr


def _adamw_update(wv, gv, mv, vv):
    mn = ADAM_B1 * mv + (1.0 - ADAM_B1) * gv
    vn = ADAM_B2 * vv + (1.0 - ADAM_B2) * jnp.square(gv)
    m_hat = mn / (1.0 - ADAM_B1 ** ADAM_STEP)
    v_hat = vn / (1.0 - ADAM_B2 ** ADAM_STEP)
    return -ADAM_LR * (m_hat / (jnp.sqrt(v_hat) + ADAM_EPS) + ADAM_WD * wv), mn, vn


def adamw_layers(w, halves, m, v, *, name):
    nl, r, c = w.shape
    tr = _row_tile(r // 2)
    nbh = r // 2 // tr

    def body(w_ref, *rest):
        g_refs, m_ref, v_ref = rest[:2 * nl], rest[2 * nl], rest[2 * nl + 1]
        d_ref, mo_ref, vo_ref, go_ref = rest[2 * nl + 2:]
        layer, i = pl.program_id(0), pl.program_id(1)
        mine = (i < nbh) == (lax.axis_index("c") == 0)
        gv = jnp.where(mine, g_refs[0][...], g_refs[1][...])
        for k in range(1, nl):
            gv = jnp.where(layer == k, jnp.where(mine, g_refs[2 * k][...], g_refs[2 * k + 1][...]), gv)
        d_ref[...], mo_ref[...], vo_ref[...] = _adamw_update(w_ref[...], gv, m_ref[...], v_ref[...])
        go_ref[...] = gv

    spec3 = pl.BlockSpec((None, tr, c), lambda k, i: (k, i, 0))
    g_specs = [pl.BlockSpec((tr, c), lambda k, i, q=q: (jnp.where(k == q, i % nbh, 0), 0)) for q in range(nl) for _ in range(2)]
    return pl.pallas_call(
        body, grid=(nl, r // tr), in_specs=[spec3] + g_specs + [spec3, spec3], out_specs=[spec3] * 4,
        out_shape=[jax.ShapeDtypeStruct((nl, r, c), F32)] * 4, name=name, compiler_params=_params(("arbitrary", "arbitrary")),
    )(w, *[h for pair in halves for h in pair], m, v)


def adamw(w, g, m, v, *, name):
    r, c = w.shape
    tr = _row_tile(r)

    def body(w_ref, g_ref, m_ref, v_ref, d_ref, mo_ref, vo_ref):
        d_ref[...], mo_ref[...], vo_ref[...] = _adamw_update(w_ref[...], g_ref[...], m_ref[...], v_ref[...])

    spec = pl.BlockSpec((tr, c), lambda i: (i, 0))
    return pl.pallas_call(
        body, grid=(r // tr,), in_specs=[spec] * 4, out_specs=[spec] * 3,
        out_shape=[jax.ShapeDtypeStruct((r, c), F32)] * 3, name=name, compiler_params=_params(("parallel",)),
    )(w, g, m, v)


def _place():
    return lax.axis_index("x"), lax.axis_index("y"), lax.axis_index("c")


def allgather8(blk, *, name):
    def body(x_ref, out_ref, send_sems, recv_sems, local_sem):
        x, y, c = _place()
        me = 4 * x + 2 * y + c
        mine = pltpu.make_async_copy(x_ref, out_ref.at[me], local_sem)
        mine.start()
        sent = []
        for k in range(1, N_DEV):
            to = (x ^ ((k >> 2) & 1), y ^ ((k >> 1) & 1), c ^ (k & 1))
            cp = pltpu.make_async_remote_copy(src_ref=x_ref, dst_ref=out_ref.at[me], send_sem=send_sems.at[k - 1],
                                              recv_sem=recv_sems.at[k - 1], device_id=to, device_id_type=MESH)
            cp.start()
            sent.append(cp)
        for k in range(1, N_DEV):
            frm = me ^ k
            pltpu.make_async_remote_copy(src_ref=x_ref, dst_ref=out_ref.at[frm], send_sem=send_sems.at[k - 1],
                                         recv_sem=recv_sems.at[k - 1], device_id=(x, y, c), device_id_type=MESH).wait_recv()
        for cp in sent:
            cp.wait_send()
        mine.wait()

    vm = pl.BlockSpec(memory_space=pltpu.VMEM)
    return pl.pallas_call(
        body, in_specs=[vm], out_specs=vm, out_shape=jax.ShapeDtypeStruct((N_DEV,) + blk.shape, blk.dtype), name=name,
        scratch_shapes=[pltpu.SemaphoreType.DMA((N_DEV - 1,)), pltpu.SemaphoreType.DMA((N_DEV - 1,)), pltpu.SemaphoreType.DMA],
    )(blk)


def _other_chips(x, y):
    return [(1 - x, y), (x, 1 - y), (1 - x, 1 - y)]


def _hbm_call(body, ins, out_shapes, n_sems, name):
    hbm = pl.BlockSpec(memory_space=pl.ANY)
    return pl.pallas_call(
        body, in_specs=[hbm] * len(ins), out_specs=[hbm] * len(out_shapes), out_shape=out_shapes, name=name,
        scratch_shapes=[pltpu.SemaphoreType.DMA((n_sems,)), pltpu.SemaphoreType.DMA((n_sems,))],
    )(*ins)


def _half_rows(c, rh):
    return pl.ds(pl.multiple_of(c * rh, BF16_ROWS), rh)


def gather_units(units, *, name):
    nu = len(units)
    shapes = []
    for arr, layer_major in units:
        r, cols = arr.shape
        shapes.append(jax.ShapeDtypeStruct((2, N_CHIPS, r // 2, cols) if layer_major else (N_CHIPS, r, cols), arr.dtype))

    def body(*refs):
        in_refs, out_refs, send_sems, recv_sems = refs[:nu], refs[nu:2 * nu], refs[2 * nu], refs[2 * nu + 1]
        x, y, c = _place()
        me_chip = 2 * x + y
        sib = (x, y, 1 - c)
        chips = _other_chips(x, y)

        def copy(k, src, dst, to):
            return pltpu.make_async_remote_copy(src_ref=src, dst_ref=dst, send_sem=send_sems.at[k], recv_sem=recv_sems.at[k],
                                                device_id=to, device_id_type=MESH)

        first, passed, landing = [], [], []
        for u, (arr, layer_major) in enumerate(units):
            rh = arr.shape[0] // 2
            out_ref = out_refs[u]
            slot = (lambda chip, half, o=out_ref: o.at[half, chip]) if layer_major else \
                   (lambda chip, half, o=out_ref, rh=rh: o.at[chip, _half_rows(half, rh), :])
            my_half = in_refs[u].at[_half_rows(c, rh), :]
            for j, (cx, cy) in enumerate(chips):
                k = 6 * u + j
                first.append(copy(k, my_half, slot(me_chip, c), (cx, cy, c)))
                passed.append(copy(k + 3, slot(2 * cx + cy, c), slot(2 * cx + cy, c), sib))
                landing.append((copy(k, my_half, slot(2 * cx + cy, c), sib), copy(k + 3, my_half, slot(2 * cx + cy, 1 - c), sib)))
        for cp in first:
            cp.start()
        for (over_ici, _), fwd in zip(landing, passed):
            over_ici.wait_recv()
            fwd.start()
        for _, from_sibling in landing:
            from_sibling.wait_recv()
        for cp in first + passed:
            cp.wait_send()

    return _hbm_call(body, [a for a, _ in units], shapes, 6 * nu, name)


HBM_SPEC = pl.BlockSpec(memory_space=pltpu.HBM)
SEM_SPEC = pl.BlockSpec(memory_space=pltpu.SEMAPHORE)
ORDERED_EFFECT = pltpu.SideEffectType.DATAFLOW_SIDE_EFFECTING


def _split_start(body, srcs, land_shapes, after, *, name):
    nu = len(srcs)
    lands = [lax.empty(s.shape, s.dtype) for s in land_shapes]

    def whole(*refs):
        body(refs[:nu], refs[nu:2 * nu], refs[2 * nu + 1], refs[2 * nu + 2])
        refs[-1][...] = jnp.zeros((SUBLANES, LANES), F32)

    hbm = lambda a: pltpu.with_memory_space_constraint(a, pltpu.HBM)
    sems = pltpu.SemaphoreType.DMA((nu,))
    res = pl.pallas_call(
        whole, name=name, in_specs=[HBM_SPEC] * (2 * nu) + [pl.BlockSpec(memory_space=pl.ANY)],
        out_shape=[sems, sems] + [pltpu.HBM(a.shape, a.dtype) for a in srcs] + [pltpu.HBM(s.shape, s.dtype) for s in land_shapes]
        + [jax.ShapeDtypeStruct((SUBLANES, LANES), F32)],
        out_specs=[SEM_SPEC, SEM_SPEC] + [HBM_SPEC] * (2 * nu) + [pl.BlockSpec(memory_space=pltpu.VMEM)],
        input_output_aliases={q: 2 + q for q in range(2 * nu)},
        compiler_params=pltpu.CompilerParams(has_side_effects=ORDERED_EFFECT),
    )(*[hbm(a) for a in srcs], *[hbm(a) for a in lands], after)
    return res[0], res[1], res[2:2 + nu], res[2 + nu:2 + 2 * nu], res[-1]


def _split_wait(pending, moved, after, *, name):
    send_sems, recv_sems, srcs, lands, _ = pending
    nu = len(srcs)

    def body(*refs):
        land_refs, ssem, rsem = refs[nu:2 * nu], refs[2 * nu], refs[2 * nu + 1]
        x, y, c = _place()
        for u in range(nu):
            size = moved(land_refs[u])
            cp = pltpu.make_async_remote_copy(src_ref=size, dst_ref=size, send_sem=ssem.at[u], recv_sem=rsem.at[u],
                                              device_id=(x, y, c), device_id_type=MESH)
            cp.wait_send()
            cp.wait_recv()

    res = pl.pallas_call(
        body, name=name, in_specs=[HBM_SPEC] * (2 * nu) + [SEM_SPEC, SEM_SPEC, pl.BlockSpec(memory_space=pl.ANY)],
        out_shape=[pltpu.HBM(a.shape, a.dtype) for a in srcs] + [pltpu.HBM(a.shape, a.dtype) for a in lands],
        out_specs=[HBM_SPEC] * (2 * nu), input_output_aliases={q: q for q in range(2 * nu)},
        compiler_params=pltpu.CompilerParams(has_side_effects=ORDERED_EFFECT),
    )(*srcs, *lands, send_sems, recv_sems, after)
    return res[nu:]


def gather_start(shards, after, *, name):
    def body(src_refs, land_refs, send_sems, recv_sems):
        x, y, c = _place()
        for u, shard in enumerate(shards):
            rows = _half_rows(c, shard.shape[0] // 2)
            for cx, cy in _other_chips(x, y):
                for core in range(2):
                    pltpu.make_async_remote_copy(src_ref=src_refs[u].at[rows, :], dst_ref=land_refs[u].at[2 * x + y, rows, :],
                                                 send_sem=send_sems.at[u], recv_sem=recv_sems.at[u], device_id=(cx, cy, core),
                                                 device_id_type=MESH).start()

    return _split_start(body, shards, [jax.ShapeDtypeStruct((N_CHIPS,) + s.shape, s.dtype) for s in shards], after, name=name)


def gather_wait(pending, after, *, name):
    return _split_wait(pending, lambda land: land.at[pl.ds(0, N_CHIPS - 1)], after, name=name)


def scatter_start(pairs, *, name):
    def body(src_refs, land_refs, send_sems, recv_sems):
        x, y, c = _place()
        for u in range(len(pairs)):
            for j, (cx, cy) in enumerate(_other_chips(x, y)):
                pltpu.make_async_remote_copy(src_ref=src_refs[u].at[2 * cx + cy], dst_ref=land_refs[u].at[j], send_sem=send_sems.at[u],
                                             recv_sem=recv_sems.at[u], device_id=(cx, cy, c), device_id_type=MESH).start()

    return _split_start(body, pairs, [jax.ShapeDtypeStruct((N_CHIPS - 1,) + p.shape[1:], p.dtype) for p in pairs], pairs[0], name=name)


def scatter_wait(pending, after, *, name):
    return _split_wait(pending, lambda land: land, after, name=name)


def swap_units(units, *, name):
    nu = len(units)

    def body(*refs):
        g_refs, out_refs, send_sems, recv_sems = refs[:nu], refs[nu:2 * nu], refs[2 * nu], refs[2 * nu + 1]
        x, y, c = _place()
        cps = [pltpu.make_async_remote_copy(src_ref=g_refs[u].at[:, _half_rows(1 - c, units[u].shape[1] // 2), :], dst_ref=out_refs[u],
                                            send_sem=send_sems.at[u], recv_sem=recv_sems.at[u], device_id=(x, y, 1 - c),
                                            device_id_type=MESH) for u in range(nu)]
        for cp in cps:
            cp.start()
        for cp in cps:
            cp.wait()

    shapes = [jax.ShapeDtypeStruct((N_CHIPS, g.shape[1] // 2, g.shape[2]), g.dtype) for g in units]
    return _hbm_call(body, units, shapes, nu, name)


def scatter_units(units, *, name):
    nu = len(units)

    def body(*refs):
        h_refs, out_refs, send_sems, recv_sems = refs[:nu], refs[nu:2 * nu], refs[2 * nu], refs[2 * nu + 1]
        x, y, c = _place()
        cps = [pltpu.make_async_remote_copy(src_ref=h_refs[u].at[2 * cx + cy], dst_ref=out_refs[u].at[j], send_sem=send_sems.at[3 * u + j],
                                            recv_sem=recv_sems.at[3 * u + j], device_id=(cx, cy, c), device_id_type=MESH)
               for u in range(nu) for j, (cx, cy) in enumerate(_other_chips(x, y))]
        for cp in cps:
            cp.start()
        for cp in cps:
            cp.wait()

    shapes = [jax.ShapeDtypeStruct((3,) + h.shape[1:], h.dtype) for h in units]
    return _hbm_call(body, units, shapes, 3 * nu, name)


def join_units(units, *, name):
    nu = len(units)

    def body(*refs):
        h_refs, out_refs, send_sems, recv_sems = refs[:nu], refs[nu:2 * nu], refs[2 * nu], refs[2 * nu + 1]
        x, y, c = _place()
        cps = [pltpu.make_async_remote_copy(src_ref=h_refs[u], dst_ref=out_refs[u], send_sem=send_sems.at[u], recv_sem=recv_sems.at[u],
                                            device_id=(x, y, 1 - c), device_id_type=MESH) for u in range(nu)]
        for cp in cps:
            cp.start()
        for cp in cps:
            cp.wait()

    return _hbm_call(body, units, [jax.ShapeDtypeStruct(h.shape, h.dtype) for h in units], nu, name)


def _half_tile(rh):
    tr = rh
    for cand in range(BF16_ROWS, min(rh, 512) + 1, BF16_ROWS):
        if rh % cand == 0:
            tr = cand
    return tr


def pair_add(g, sib, *, name):
    nc, rh, cols = sib.shape
    tr = _half_tile(rh)
    nbh = rh // tr

    def body(g0_ref, g1_ref, s_ref, o_ref):
        mine = jnp.where(lax.axis_index("c") == 0, g0_ref[...], g1_ref[...])
        o_ref[...] = (mine.astype(F32) + s_ref[...].astype(F32)).astype(o_ref.dtype)

    blk = lambda off: pl.BlockSpec((None, tr, cols), lambda j, i: (j, off + i, 0))
    return pl.pallas_call(
        body, grid=(nc, nbh), in_specs=[blk(0), blk(nbh), blk(0)], out_specs=blk(0),
        out_shape=jax.ShapeDtypeStruct(sib.shape, BF16), name=name, compiler_params=_params(("parallel", "parallel")),
    )(g, g, sib)


def chips_add(pair, landed, *, name):
    nc, rh, cols = pair.shape
    tr = _half_tile(rh)

    def body(*refs):
        chip = 2 * lax.axis_index("x") + lax.axis_index("y")
        acc = refs[0][...]
        for j in range(1, nc):
            acc = jnp.where(chip == j, refs[j][...], acc)
        acc = acc.astype(F32)
        for r in refs[nc:-1]:
            acc = acc + r[...].astype(F32)
        refs[-1][...] = acc

    part = lambda q: pl.BlockSpec((None, tr, cols), lambda i, q=q: (q, i, 0))
    return pl.pallas_call(
        body, grid=(rh // tr,), in_specs=[part(q) for q in range(nc)] + [part(q) for q in range(landed.shape[0])],
        out_specs=pl.BlockSpec((tr, cols), lambda i: (i, 0)),
        out_shape=jax.ShapeDtypeStruct((rh, cols), F32), name=name, compiler_params=_params(("parallel",)),
    )(*[pair] * nc, *[landed] * landed.shape[0])


def sum8(g, *, name):
    def body(g_ref, o_ref):
        acc = g_ref[0]
        for d in range(1, N_DEV):
            acc = acc + g_ref[d]
        o_ref[...] = acc

    return pl.pallas_call(body, out_shape=jax.ShapeDtypeStruct(g.shape[1:], F32), name=name)(g)


def _dup_halves(a):
    t = a.shape[0]
    a = a.reshape(t, N_KV_B, HEAD_DIM_B)
    return jnp.concatenate([a, a], axis=-1).reshape(t, N_KV_B * LANES)


def _undup(a):
    t = a.shape[0]
    return a.reshape(t, N_KV_B, LANES)[:, :, :HEAD_DIM_B].reshape(t, N_KV_B * HEAD_DIM_B)


def _lane_pad(v, offset=0):
    return jnp.zeros((1, LANES), F32).at[0, offset:offset + v.shape[0]].set(v)


SHARD_UP = 2 * D_FF // N_CHIPS
SHARD_BIN = (N_HEADS_B + 2 * N_KV_B) * HEAD_DIM_B // N_CHIPS
SHARD_PROJ = D_MODEL // N_CHIPS


def local_step(x, p, tgt, sm, weight, on_grads):
    t = x.shape[0]
    rtm = min(256, t)
    hk = N_HEADS_A * HEAD_DIM_A
    qd_b = N_HEADS_B * HEAD_DIM_B
    kd_b = N_KV_B * HEAD_DIM_B
    gs = {}
    norm = lambda h, w, nm: tile_map(_f_norm, [(h, D_MODEL, 0)], [w], [(D_MODEL, BF16)], tm=rtm, ncol=1, name=nm)[0]

    def norm_bwd(h, w, dy, add, nm):
        (dh,), (dw,) = tile_vjp(_f_norm, [(h, D_MODEL, 0)], [w], [(dy, D_MODEL, 0)], n_diff=1, tm=rtm, ncol=1, name=nm,
                                add=(add, D_MODEL, 0))
        return dh, dw

    spec = pl.BlockSpec
    mtm = _tile(D_MODEL, MM_TM_CAP)
    p_bf = p.astype(BF16)
    alog_p = _lane_pad(sm["a_log"][0], N_HEADS_A)
    dtb_p = _lane_pad(sm["a_dt_bias"][0], N_HEADS_A)
    sinks_p = _lane_pad(sm["b_sinks"][0])
    nw = lambda name, i: sm[name][i:i + 1]
    by_chip = lambda kdim, ns: dict(tn=ns, tk=kdim, b_spec=spec((None, kdim, ns), lambda r, j, kk: (j, kk, 0)))
    by_chip_t = lambda ndim, ns: dict(n=ndim, tn=ndim, tk=ns, b_spec=spec((None, ndim, ns), lambda r, j, kk: (kk, j, 0)))
    cache = {}

    def wgt(name, i, after):
        if (name, i) not in cache:
            cache[name, i] = weight(name, i, after)
        return cache[name, i]

    saved = []
    h = x
    hn_next = norm(h, nw("norm_mix", 0), "norm_mix0")
    for i in range(DEPTH):
        s = {"h0": h, "hn": hn_next}
        if i % 2 == 0:
            s["pm"] = mm(s["hn"], wgt("a_main", i, h), name="a_in_main")
            s["pt"] = mm(s["hn"], wgt("a_tail", i, h), name="a_in_tail")
            s["c"] = conv_fwd(s["pm"], wgt("a_conv", i, h), name="a_conv")
            s["bg"] = tile_map(_f_betag, [(s["pt"], LANES, 0)], [alog_p, dtb_p], [(LANES, F32)], tm=rtm, ncol=1, name="a_betag")[0]
            s["prep"] = delta_prep(s["c"], s["bg"], name="a_prep")
            s["o"], s["s_in"] = delta_scan(*s["prep"], name="a_scan")
            s["on"] = gnorm_fwd(s["o"], s["pm"], sm["a_norm"], name="a_gnorm")
            h, s["hf"] = mm(s["on"], wgt("a_w_out", i, s["on"]), add=h, norm_w=nw("norm_ffn", i), name="a_out")
        else:
            s["pb"] = mm(s["hn"], wgt("b_w_in", i, s["hn"]), name="b_in", out_dtype=BF16, n=N_CHIPS * SHARD_BIN,
                         **by_chip(D_MODEL, SHARD_BIN))
            s["kd"], s["vd"] = _dup_halves(s["pb"][:, qd_b:qd_b + kd_b]), _dup_halves(s["pb"][:, qd_b + kd_b:])
            s["ao"] = swa_fwd(s["pb"], s["kd"], s["vd"], sinks_p, name="b_att")
            h, s["hf"] = mm(s["ao"], wgt("b_w_out", i, s["ao"]), add=h, norm_w=nw("norm_ffn", i), name="b_out")
        s["h1"] = h
        s["u"] = mm(s["hf"], wgt("f_w_up", i, s["hf"]), name=f"f_up{i}", n=2 * D_FF, **by_chip(D_MODEL, SHARD_UP))
        s["act"] = conv_act_fwd(s["u"], wgt("f_conv", i, s["hf"]), name=f"f_conv_act{i}")
        h, s["hp"] = mm(s["act"], wgt("f_w_down", i, s["act"]), add=h, norm_w=nw("norm_ple", i), name=f"f_down{i}")
        s["h2"] = h
        s["gl"] = mm(s["hp"], wgt("ple_w_gate", i, s["hp"]), name=f"ple_gate{i}")
        s["pe"] = mm(p_bf[i], wgt("ple_w_proj", i, s["hp"]), name=f"ple_proj{i}", n=D_MODEL, **by_chip(PLE_DIM, SHARD_PROJ))
        rows3 = [(h, D_MODEL, 0), (s["gl"], D_MODEL, 0), (s["pe"], D_MODEL, 0)]
        if i + 1 < DEPTH:
            def mix_norm(hv, g, e, wn):
                hn = hv + _f_ple(g, e)
                return hn, _f_norm(hn, wn)
            h, hn_next = tile_map(mix_norm, rows3, [nw("norm_mix", i + 1)], [(D_MODEL, F32), (D_MODEL, BF16)], tm=rtm, ncol=1,
                                  name=f"ple_mix{i}")
        else:
            h = tile_map(lambda hv, g, e: hv + _f_ple(g, e), rows3, [], [(D_MODEL, F32)], tm=rtm, ncol=1, name=f"ple_mix{i}")[0]
        saved.append(s)

    dh, gnf, loss = loss_head(h, tgt, sm["norm_final"][None, :], name="loss_head")
    gs["norm_final"] = gnf[0]

    g_mix, g_ffn, g_ple, g_conv = ([None] * DEPTH for _ in range(4))
    zero = jnp.zeros((1, 1), F32)
    for i in reversed(range(DEPTH)):
        s, gw = saved[i], {}
        by_rows = lambda g: g.reshape(N_CHIPS, g.shape[0] // N_CHIPS, g.shape[1])
        (dgl, dpe), _ = tile_vjp(_f_ple, [(s["gl"], D_MODEL, 0), (s["pe"], D_MODEL, 0)], [], [(dh, D_MODEL, 0)], n_diff=2,
                                 tm=rtm, ncol=1, name=f"ple_mix_bwd{i}", grad_dtypes=[BF16, BF16])
        gw["ple_w_proj"] = mm(p_bf[i], dpe, ta=True, name=f"ple_proj_dw{i}", out_dtype=BF16, tn=SHARD_PROJ,
                              o_shape=(N_CHIPS, PLE_DIM, SHARD_PROJ), o_spec=spec((None, PLE_DIM, SHARD_PROJ), lambda r, j, kk: (j, r, 0)))
        gw["ple_w_gate"] = by_rows(mm(s["hp"], dgl, ta=True, name=f"ple_gate_dw{i}", out_dtype=BF16))
        dhp = mm(dgl, cache["ple_w_gate", i], tb=True, name=f"ple_gate_dx{i}")
        dh, g_ple[i] = norm_bwd(s["h2"], nw("norm_ple", i) + zero, dhp, dh, f"norm_ple_bwd{i}")

        dact = mm(dh, cache["f_w_down", i], tb=True, name=f"f_down_dx{i}")
        gw["f_w_down"] = by_rows(mm(s["act"], dh, ta=True, name=f"f_down_dw{i}", out_dtype=BF16, tm_cap=D_FF // 2))
        du_halves = conv_act_bwd(s["u"], dact, cache["f_conv", i], name=f"f_conv_act_bwd{i}")
        g_conv[i] = jnp.concatenate(du_halves[2:], axis=1)
        dhf = g_up = None
        for half, du in enumerate(du_halves[:2]):
            c0 = half * (N_CHIPS // 2)
            g_up = mm(s["hf"], du, ta=True, name=f"f_up_dw{i}_{half}", out_dtype=BF16, tn=SHARD_UP, into=g_up,
                      o_shape=(N_CHIPS, D_MODEL, SHARD_UP), o_spec=spec((None, mtm, SHARD_UP), lambda r, j, kk, c0=c0: (c0 + j, r, 0)))
            dhf = mm(du, cache["f_w_up", i], tb=True, name=f"f_up_dx{i}_{half}", n=D_MODEL, tn=D_MODEL, tk=SHARD_UP, add=dhf,
                     b_spec=spec((None, D_MODEL, SHARD_UP), lambda r, j, kk, c0=c0: (c0 + kk, j, 0)))
        gw["f_w_up"] = g_up
        dh, g_ffn[i] = norm_bwd(s["h1"], nw("norm_ffn", i), dhf, dh, f"norm_ffn_bwd{i}")
        token, gw = on_grads(i, "ffn", gw), {}
        w_out = cache["a_w_out" if i % 2 == 0 else "b_w_out", i]
        if token is not None:
            w_out = w_out + token[:1, :1].astype(BF16)

        if i % 2 == 0:
            don = mm(dh, w_out, tb=True, name="a_out_dx")
            gw["a_w_out"] = by_rows(mm(s["on"], dh, ta=True, name="a_out_dw", out_dtype=BF16))
            do, dz, gs["a_norm"] = gnorm_bwd(s["o"], s["pm"], sm["a_norm"], don, name="a_gnorm_bwd")
            dprep = delta_scan_bwd(do, *s["prep"], s["s_in"], name="a_scan_bwd")
            dc, dbg = delta_prep_bwd(s["c"], s["bg"], dprep, name="a_prep_bwd")
            (dpt,), (galog, gdtb) = tile_vjp(_f_betag, [(s["pt"], LANES, 0)], [alog_p, dtb_p], [(dbg, LANES, 0)], n_diff=1,
                                             tm=rtm, ncol=1, name="a_betag_bwd", grad_dtypes=[BF16])
            gs["a_log"] = galog[:, N_HEADS_A:2 * N_HEADS_A]
            gs["a_dt_bias"] = gdtb[:, N_HEADS_A:2 * N_HEADS_A]
            dqkv, gs["a_conv"] = conv_bwd(dc, s["pm"], cache["a_conv", i], name="a_conv_bwd")
            dpm = jnp.concatenate([dqkv, dz], axis=1)
            dhn = mm(dpm, cache["a_main", i], tb=True, name="a_in_main_dx")
            dhn = mm(dpt, cache["a_tail", i], tb=True, add=dhn, name="a_in_tail_dx")
            g_main = mm(s["hn"], dpm, ta=True, name="a_in_main_dw", out_dtype=BF16)
            g_tail = mm(s["hn"], dpt, ta=True, name="a_in_tail_dw", out_dtype=BF16)
            g_in = jnp.concatenate([g_main, g_tail[:, :2 * N_HEADS_A]], axis=1)
            gw["a_w_in"] = g_in.reshape(D_MODEL, N_CHIPS, g_in.shape[1] // N_CHIPS).transpose(1, 0, 2)
        else:
            dao = mm(dh, w_out, tb=True, name="b_out_dx")
            gw["b_w_out"] = by_rows(mm(s["ao"], dh, ta=True, name="b_out_dw", out_dtype=BF16))
            dq, dkd, dvd, gsk = swa_bwd(dao, s["pb"], s["kd"], s["vd"], sinks_p, name="b_att_bwd")
            gs["b_sinks"] = gsk[:, :N_HEADS_B]
            dpb = jnp.concatenate([dq, _undup(dkd), _undup(dvd)], axis=1)
            dhn = mm(dpb, cache["b_w_in", i], tb=True, name="b_in_dx", **by_chip_t(D_MODEL, SHARD_BIN))
            gw["b_w_in"] = mm(s["hn"], dpb, ta=True, name="b_in_dw", out_dtype=BF16, tn=SHARD_BIN,
                              o_shape=(N_CHIPS, D_MODEL, SHARD_BIN), o_spec=spec((None, mtm, SHARD_BIN), lambda r, j, kk: (j, r, 0)))
        dh, g_mix[i] = norm_bwd(s["h0"], nw("norm_mix", i), dhn, dh, f"norm_mix_bwd{i}")
        token = on_grads(i, "mix", gw)
        if token is not None:
            zero = token[:1, :1]

    gs["norm_mix"], gs["norm_ffn"], gs["norm_ple"] = (jnp.concatenate(g, axis=0) for g in (g_mix, g_ffn, g_ple))
    gs["f_conv"] = jnp.stack(g_conv)
    return loss, dh, gs


BIG = ["a_w_in", "a_w_out", "b_w_in", "b_w_out", "f_w_up", "f_w_down", "ple_w_proj", "ple_w_gate"]
LAYERED = {"f_w_up", "f_w_down", "ple_w_proj", "ple_w_gate"}
BY_CHIP = {"b_w_in", "f_w_up", "ple_w_proj"}
LAYER_UNITS = [[("a_w_in", 0), ("a_w_out", 0)] + [(n, 0) for n in sorted(LAYERED)],
               [("b_w_in", 1), ("b_w_out", 1)] + [(n, 1) for n in sorted(LAYERED)]]
CONVS = ["a_conv", "f_conv"]
SMALL = ["norm_mix", "norm_ffn", "norm_ple", "norm_final", "a_log", "a_dt_bias", "a_norm", "b_sinks"]
SMALL_ROWS = 8
CONV_ROWS = 16
CONV_GRAD_ROWS = 48


def _pack_rows(arrs, rows, dtype):
    flat = jnp.concatenate([a.reshape(-1).astype(dtype) for a in arrs])
    return jnp.pad(flat, (0, rows * PACK_COLS - flat.shape[0])).reshape(rows, PACK_COLS)


def _unpack(flat, shapes):
    out, off = [], 0
    for shp in shapes:
        n = math.prod(shp)
        out.append(flat[off:off + n].reshape(shp))
        off += n
    return out


def _pack_small(d, loss=None):
    tail = jnp.concatenate([d["a_log"].reshape(-1), d["a_dt_bias"].reshape(-1), d["a_norm"].reshape(-1), d["b_sinks"].reshape(-1)])
    if loss is not None:
        tail = jnp.concatenate([tail, loss.reshape(-1)[:1]])
    tail = jnp.pad(tail, (0, PACK_COLS - tail.shape[0]))
    return jnp.concatenate([d["norm_mix"], d["norm_ffn"], d["norm_ple"], d["norm_final"][None, :], tail[None, :]], axis=0)


def _unpack_small(a, like):
    out = {"norm_mix": a[0:2], "norm_ffn": a[2:4], "norm_ple": a[4:6], "norm_final": a[6]}
    off = 0
    for nm in ("a_log", "a_dt_bias", "a_norm", "b_sinks"):
        n = like[nm].size
        out[nm] = a[7, off:off + n].reshape(like[nm].shape)
        off += n
    return out, a[7, off]


def _as2d(a):
    return a.reshape(-1, a.shape[-1])


def kernel(x, p, norm_mix, norm_ffn, norm_ple, norm_final, a_w_in, a_conv, a_log, a_dt_bias, a_norm, a_w_out, b_w_in, b_sinks, b_w_out, f_w_up, f_conv, f_w_down, ple_w_proj, ple_w_gate, loss_target, m_norm_mix, m_norm_ffn, m_norm_ple, m_norm_final, m_a_w_in, m_a_conv, m_a_log, m_a_dt_bias, m_a_norm, m_a_w_out, m_b_w_in, m_b_sinks, m_b_w_out, m_f_w_up, m_f_conv, m_f_w_down, m_ple_w_proj, m_ple_w_gate, v_norm_mix, v_norm_ffn, v_norm_ple, v_norm_final, v_a_w_in, v_a_conv, v_a_log, v_a_dt_bias, v_a_norm, v_a_w_out, v_b_w_in, v_b_sinks, v_b_w_out, v_f_w_up, v_f_conv, v_f_w_down, v_ple_w_proj, v_ple_w_gate):
    w = dict(norm_mix=norm_mix, norm_ffn=norm_ffn, norm_ple=norm_ple, norm_final=norm_final, a_w_in=a_w_in, a_conv=a_conv,
             a_log=a_log, a_dt_bias=a_dt_bias, a_norm=a_norm, a_w_out=a_w_out, b_w_in=b_w_in, b_sinks=b_sinks, b_w_out=b_w_out,
             f_w_up=f_w_up, f_conv=f_conv, f_w_down=f_w_down, ple_w_proj=ple_w_proj, ple_w_gate=ple_w_gate)
    m = dict(norm_mix=m_norm_mix, norm_ffn=m_norm_ffn, norm_ple=m_norm_ple, norm_final=m_norm_final, a_w_in=m_a_w_in,
             a_conv=m_a_conv, a_log=m_a_log, a_dt_bias=m_a_dt_bias, a_norm=m_a_norm, a_w_out=m_a_w_out, b_w_in=m_b_w_in,
             b_sinks=m_b_sinks, b_w_out=m_b_w_out, f_w_up=m_f_w_up, f_conv=m_f_conv, f_w_down=m_f_w_down,
             ple_w_proj=m_ple_w_proj, ple_w_gate=m_ple_w_gate)
    v = dict(norm_mix=v_norm_mix, norm_ffn=v_norm_ffn, norm_ple=v_norm_ple, norm_final=v_norm_final, a_w_in=v_a_w_in,
             a_conv=v_a_conv, a_log=v_a_log, a_dt_bias=v_a_dt_bias, a_norm=v_a_norm, a_w_out=v_a_w_out, b_w_in=v_b_w_in,
             b_sinks=v_b_sinks, b_w_out=v_b_w_out, f_w_up=v_f_w_up, f_conv=v_f_conv, f_w_down=v_f_w_down,
             ple_w_proj=v_ple_w_proj, ple_w_gate=v_ple_w_gate)
    xc, yc, cc = _place()
    my_chip = 2 * xc + yc

    shard = {(n, i): w[n][i if n in LAYERED else 0].astype(BF16) for n, i in LAYER_UNITS[0] + LAYER_UNITS[1]}
    first = shard["a_w_in", 0]
    (ga,) = gather_units([(first, False)], name="gather_first")
    ga = lax.dynamic_update_index_in_dim(ga, first, my_chip, 0)
    a_in = jnp.concatenate([ga[j] for j in range(N_CHIPS)], axis=1)
    n_main = 4 * N_HEADS_A * HEAD_DIM_A
    conv_shapes = [w[n].shape for n in CONVS]
    convs = allgather8(_pack_rows([w[n] for n in CONVS], CONV_ROWS, F32), name="gather_convs")
    conv_parts = [_unpack(convs[2 * j].reshape(-1), conv_shapes) for j in range(N_CHIPS)]
    a_conv_full, f_conv_full = (jnp.concatenate([conv_parts[j][q] for j in range(N_CHIPS)], axis=2) for q in range(2))
    ready = {("a_main", 0): a_in[:, :n_main], ("a_tail", 0): jnp.pad(a_in[:, n_main:], ((0, 0), (0, LANES - 2 * N_HEADS_A))),
             ("a_conv", 0): a_conv_full[0], ("f_conv", 0): f_conv_full[0], ("f_conv", 1): f_conv_full[1]}
    later = [[k for k in units if k != ("a_w_in", 0)] for units in LAYER_UNITS]
    pending, after = [], ga
    for layer, keys in enumerate(later):
        pending.append(gather_start([shard[k] for k in keys], after, name=f"gather_start{layer}"))
        after = pending[-1][4]
    sm = {n: w[n] for n in SMALL}
    sm["norm_mix"] = sm["norm_mix"] + after[:1, :1]

    def weight(name, layer, act):
        if (name, layer) not in ready:
            landed = gather_wait(pending[layer], act, name=f"gather_wait{layer}")
            for k, g in zip(later[layer], landed):
                g = lax.dynamic_update_index_in_dim(g, shard[k], my_chip, 0)
                ready[k] = g if k[0] in BY_CHIP else g.reshape(N_CHIPS * g.shape[1], g.shape[2])
        return ready[name, layer]

    pairs, scattered, started = {}, {}, []

    def on_grads(layer, part, gw):
        keys = [k for k in LAYER_UNITS[layer] if (k[0] in LAYERED) == (part == "ffn")]
        from_sib = swap_units([gw[n] for n, _ in keys], name=f"rs_swap_{part}{layer}")
        for (n, _), sib in zip(keys, from_sib):
            pairs[n, layer] = pair_add(gw[n], sib, name=f"rs_add_pair_{n}{layer}")
        if (layer, part) == (0, "mix"):
            scattered.update(zip(keys, scatter_units([pairs[k] for k in keys], name="rs_scatter_last")))
            return None
        started.append((keys, scatter_start([pairs[k] for k in keys], name=f"rs_scatter_start_{part}{layer}"), f"{part}{layer}"))
        return started[-1][1][4]

    loss, grad_x, gs = local_step(x[0], p[:, 0], loss_target[0], sm, weight, on_grads)

    for keys, pend, tag in started:
        scattered.update(zip(keys, scatter_wait(pend, grad_x, name=f"rs_scatter_wait_{tag}")))
    all_units = LAYER_UNITS[0] + LAYER_UNITS[1]
    halves = [chips_add(pairs[k], scattered[k], name=f"rs_add_chips_{k[0]}{k[1]}") for k in all_units]
    g_unit = dict(zip(all_units, zip(halves, join_units(halves, name="rs_join"))))

    conv_grads = _pack_rows([gs[n] for n in CONVS], CONV_GRAD_ROWS, F32)
    small_sum = sum8(allgather8(jnp.concatenate([_pack_small(gs, loss), conv_grads]), name="gather_small"), name="sum_small")
    g_sm, loss_sum = _unpack_small(small_sum[:SMALL_ROWS], sm)

    grads, delta, new_m, new_v = {}, {}, {}, {}
    for n in BIG:
        g_layers = [g_unit[n, i] for i in range(DEPTH) if (n, i) in g_unit]
        shape3 = (len(g_layers), 2 * g_layers[0][0].shape[0], g_layers[0][0].shape[1])
        res = adamw_layers(w[n].reshape(shape3), g_layers, m[n].reshape(shape3), v[n].reshape(shape3), name=f"adamw_{n}")
        delta[n], new_m[n], new_v[n], grads[n] = (r.reshape(w[n].shape) for r in res)
    for n, full in zip(CONVS, _unpack(small_sum[SMALL_ROWS:].reshape(-1), [gs[n].shape for n in CONVS])):
        g2 = _as2d(lax.dynamic_slice_in_dim(full, my_chip * w[n].shape[-1], w[n].shape[-1], axis=full.ndim - 1))
        d2, m2, v2 = adamw(_as2d(w[n]), g2, _as2d(m[n]), _as2d(v[n]), name=f"adamw_{n}")
        grads[n], delta[n], new_m[n], new_v[n] = (r.reshape(w[n].shape) for r in (g2, d2, m2, v2))
    pk = lambda d: _pack_small(d)
    d2, m2, v2 = adamw(pk(sm), pk(g_sm), pk({n: m[n] for n in SMALL}), pk({n: v[n] for n in SMALL}), name="adamw_small")
    for src, dst in ((d2, delta), (m2, new_m), (v2, new_v)):
        dst.update(_unpack_small(src, sm)[0])
    grads.update(g_sm)

    order = ["norm_mix", "norm_ffn", "norm_ple", "norm_final", "a_w_in", "a_conv", "a_log", "a_dt_bias", "a_norm", "a_w_out",
             "b_w_in", "b_sinks", "b_w_out", "f_w_up", "f_conv", "f_w_down", "ple_w_proj", "ple_w_gate"]
    return (loss_sum, grad_x[None], *[grads[n] for n in order], *[delta[n] for n in order],
            *[new_m[n] for n in order], *[new_v[n] for n in order])
```

```python
import functools
import math

import jax
import jax.numpy as jnp
from jax import lax
from jax.experimental import pallas as pl
from jax.experimental.pallas import tpu as pltpu

F32 = jnp.float32
BF16 = jnp.bfloat16
MESH = pl.DeviceIdType.MESH

D_MODEL = 1024
N_HEADS_A = 8
HEAD_DIM_A = 128
CONV_A = 4
N_HEADS_B = 16
N_KV_B = 4
HEAD_DIM_B = 64
WINDOW = 128
D_FF = 2816
FFN_CONV = 3
PLE_DIM = 256
EPS = 1e-6
DEPTH = 2

ADAM_LR = 0.001
ADAM_B1 = 0.9
ADAM_B2 = 0.999
ADAM_EPS = 1e-08
ADAM_WD = 0.01
ADAM_STEP = 10

LANES = 128
SUBLANES = 8
BF16_ROWS = 16
CHUNK = 128
VMEM_LIMIT = 56 * 1024 * 1024
NEG = -1e30
N_CHIPS = 4
N_DEV = 8
PACK_COLS = 1024


def _params(sem=None):
    return pltpu.CompilerParams(dimension_semantics=sem, vmem_limit_bytes=VMEM_LIMIT)


def _tile(dim, cap):
    if dim % LANES:
        return dim
    best = LANES
    for t in range(LANES, min(dim, cap) + 1, LANES):
        if dim % t == 0:
            best = t
    return best


def _dot(a, b, dims=(((1,), (0,)), ((), ())), precision=None):
    return lax.dot_general(a, b, dims, precision=precision, preferred_element_type=F32)


NN = (((1,), (0,)), ((), ()))
NT = (((1,), (1,)), ((), ()))
TN = (((0,), (0,)), ((), ()))


MM_TM_CAP = 1024
MM_TK_CAP_TOKENS = 2048


def mm(a, b, *, name, ta=False, tb=False, out_dtype=F32, add=None, norm_w=None, tm_cap=MM_TM_CAP, tn_cap=1408, tk_cap=1408,
       n=None, tn=None, tk=None, b_spec=None, o_spec=None, o_shape=None, into=None):
    m, k = (a.shape[1], a.shape[0]) if ta else a.shape
    if b_spec is None:
        n = b.shape[0] if tb else b.shape[1]
        assert (b.shape[1] if tb else b.shape[0]) == k, (a.shape, b.shape, ta, tb)
    tm, tn, tk = _tile(m, tm_cap), tn or _tile(n, tn_cap), tk or _tile(k, MM_TK_CAP_TOKENS if ta else tk_cap)
    assert n % tn == 0 and k % tk == 0, (n, tn, k, tk)
    nk = k // tk
    dims = (((0 if ta else 1,), (1 if tb else 0,)), ((), ()))
    has_add, has_norm = add is not None, norm_w is not None
    assert not has_norm or (tn == n and o_spec is None), "the norm epilogue needs whole rows"
    n_in = 2 + has_add + has_norm + (into is not None)

    def body(*refs):
        a_ref, b_ref = refs[0], refs[1]
        add_ref = refs[2] if has_add else None
        o_ref = refs[n_in]
        part = _dot(a_ref[...].astype(BF16), b_ref[...].astype(BF16), dims)

        def finish(r):
            if has_add:
                r = r + add_ref[...].astype(F32)
            o_ref[...] = r.astype(o_ref.dtype)
            if has_norm:
                refs[n_in + 1][...] = _f_norm(r, refs[2 + has_add][...]).astype(BF16)

        if nk == 1:
            finish(part)
            return
        acc = refs[-1]
        kk = pl.program_id(2)

        @pl.when(kk == 0)
        def _():
            acc[...] = part

        @pl.when(kk > 0)
        def _():
            acc[...] += part

        @pl.when(kk == nk - 1)
        def _():
            finish(acc[...])

    a_spec = pl.BlockSpec((tk, tm), lambda i, j, kk: (kk, i)) if ta else pl.BlockSpec((tm, tk), lambda i, j, kk: (i, kk))
    if b_spec is None:
        b_spec = pl.BlockSpec((tn, tk), lambda i, j, kk: (j, kk)) if tb else pl.BlockSpec((tk, tn), lambda i, j, kk: (kk, j))
    plain_o = pl.BlockSpec((tm, tn), lambda i, j, kk: (i, j))
    if o_spec is None:
        o_spec, o_shape = plain_o, (m, n)
    in_specs = [a_spec, b_spec] + ([plain_o] if has_add else [])
    args = (a, b) + ((add,) if has_add else ())
    out_specs, out_shapes = o_spec, jax.ShapeDtypeStruct(tuple(o_shape), out_dtype)
    if has_norm:
        in_specs.append(pl.BlockSpec((1, n), lambda i, j, kk: (0, 0)))
        args += (norm_w,)
        out_specs, out_shapes = [o_spec, plain_o], [out_shapes, jax.ShapeDtypeStruct((m, n), BF16)]
    aliases = {}
    if into is not None:
        assert into.shape == tuple(o_shape) and into.dtype == out_dtype, (into.shape, o_shape)
        in_specs.append(pl.BlockSpec(memory_space=pl.ANY))
        args += (into,)
        aliases = {n_in - 1: 0}
    return pl.pallas_call(
        body, grid=(m // tm, n // tn, nk), in_specs=in_specs, out_specs=out_specs,
        out_shape=out_shapes, name=name, input_output_aliases=aliases,
        scratch_shapes=[pltpu.VMEM((tm, tn), F32)] if nk > 1 else [],
        compiler_params=_params(("parallel", "parallel", "arbitrary")),
    )(*args)


def _row_spec(tm, cw, coff):
    return pl.BlockSpec((tm, cw), lambda i, j: (i, j + coff))


def _full_spec(shape):
    return pl.BlockSpec(shape, lambda i, j: (0,) * len(shape))


def tile_map(fn, rows, params, outs, *, tm, ncol, name):
    t = rows[0][0].shape[0]
    nin = len(rows) + len(params)

    def body(*refs):
        res = fn(*[r[...] for r in refs[:nin]])
        res = res if isinstance(res, (tuple, list)) else (res,)
        for o_ref, r in zip(refs[nin:], res):
            o_ref[...] = r.astype(o_ref.dtype)

    in_specs = [_row_spec(tm, cw, coff) for (_, cw, coff) in rows] + [_full_spec(p.shape) for p in params]
    res = pl.pallas_call(
        body, grid=(t // tm, ncol), in_specs=in_specs,
        out_specs=[_row_spec(tm, cw, 0) for (cw, _) in outs],
        out_shape=[jax.ShapeDtypeStruct((t, cw * ncol), dt) for (cw, dt) in outs], name=name,
        compiler_params=_params(("parallel", "parallel")),
    )(*[r[0] for r in rows], *params)
    return res


def tile_vjp(fn, rows, params, cts, *, n_diff, tm, ncol, name, add=None, grad_dtypes=None):
    t = rows[0][0].shape[0]
    nr, npar, nct = len(rows), len(params), len(cts)
    has_add = add is not None

    def body(*refs):
        vals = [r[...] for r in refs[:nr + npar + nct + (1 if has_add else 0)]]
        diff, rest, pars = vals[:n_diff], vals[n_diff:nr], vals[nr:nr + npar]
        ctv = vals[nr + npar:nr + npar + nct]
        outs_ref = refs[nr + npar + nct + (1 if has_add else 0):]

        def f(*a):
            res = fn(*a[:n_diff], *rest, *a[n_diff:])
            return tuple(res) if isinstance(res, (tuple, list)) else (res,)

        primal, vjp = jax.vjp(f, *[d.astype(F32) for d in diff], *pars)
        grads = vjp(tuple(c.astype(o.dtype) for c, o in zip(ctv, primal)))
        for q in range(n_diff):
            g = grads[q]
            if has_add and q == 0:
                g = g + vals[-1]
            outs_ref[q][...] = g.astype(outs_ref[q].dtype)
        first = (pl.program_id(0) == 0) & (pl.program_id(1) == 0)
        for q in range(npar):
            o_ref, g = outs_ref[n_diff + q], grads[n_diff + q]

            @pl.when(first)
            def _(o_ref=o_ref, g=g):
                o_ref[...] = g

            @pl.when(jnp.logical_not(first))
            def _(o_ref=o_ref, g=g):
                o_ref[...] += g

    ins = list(rows) + [None] * 0
    in_specs = [_row_spec(tm, cw, coff) for (_, cw, coff) in rows] + [_full_spec(p.shape) for p in params]
    in_specs += [_row_spec(tm, cw, coff) for (_, cw, coff) in cts]
    args = [r[0] for r in rows] + list(params) + [c[0] for c in cts]
    if has_add:
        in_specs.append(_row_spec(tm, add[1], add[2]))
        args.append(add[0])
    out_specs = [_row_spec(tm, rows[q][1], 0) for q in range(n_diff)] + [_full_spec(p.shape) for p in params]
    grad_dtypes = grad_dtypes or [F32] * n_diff
    out_shape = [jax.ShapeDtypeStruct((t, rows[q][1] * ncol), grad_dtypes[q]) for q in range(n_diff)]
    out_shape += [jax.ShapeDtypeStruct(p.shape, F32) for p in params]
    del ins
    res = pl.pallas_call(
        body, grid=(t // tm, ncol), in_specs=in_specs, out_specs=out_specs, out_shape=out_shape, name=name,
        compiler_params=_params(("arbitrary", "arbitrary")),
    )(*args)
    return res[:n_diff], res[n_diff:]


def _silu(x):
    return x * jax.nn.sigmoid(x)


def _f_norm(h, w):
    return h * lax.rsqrt(jnp.mean(h * h, axis=-1, keepdims=True) + EPS) * w


def _f_gnorm(o, z, w):
    return _f_norm(o, w) * _silu(z)


def _f_act(gate, val):
    return _silu(gate) * val


def _f_ple(gl, pe):
    return jax.nn.sigmoid(gl) * pe


def _f_betag(pt, alog, dtb):
    lane = lax.broadcasted_iota(jnp.int32, (1, LANES), 1)
    z = pt + dtb
    softplus = jnp.maximum(z, 0.0) + jnp.log(1.0 + jnp.exp(-jnp.abs(z)))
    g = -jnp.exp(alog) * softplus
    return jnp.where(lane < N_HEADS_A, jax.nn.sigmoid(pt), jnp.where(lane < 2 * N_HEADS_A, g, 0.0))


CONV_TM = 256
CONV_CW = 1024


def _shift_down(x, prev, s, row):
    rp = jnp.tile(pltpu.roll(prev, s, 0), (x.shape[0] // SUBLANES, 1))
    return jnp.where(row < s, rp, pltpu.roll(x, s, 0))


def _shift_up(x, nxt, s, row):
    tm = x.shape[0]
    rn = jnp.tile(pltpu.roll(nxt, SUBLANES - s, 0), (tm // SUBLANES, 1))
    return jnp.where(row >= tm - s, rn, pltpu.roll(x, tm - s, 0))


def _conv_taps(x, prev, w_ref, cols, row):
    k = w_ref.shape[0]
    y = x * w_ref[pl.ds(k - 1, 1), cols]
    for s in range(1, k):
        y = y + _shift_down(x, prev, s, row) * w_ref[pl.ds(k - 1 - s, 1), cols]
    return y


def _lane_chunks(cw):
    return [slice(cb * LANES, (cb + 1) * LANES) for cb in range(cw // LANES)]


def conv_fwd(x, w, *, name):
    t = x.shape[0]
    k, c = w.shape
    tm, cw = min(CONV_TM, t), CONV_CW
    nb8 = tm // SUBLANES

    def body(x_ref, p_ref, w_ref, o_ref):
        first = pl.program_id(1) == 0
        row = lax.broadcasted_iota(jnp.int32, (tm, LANES), 0)
        for cols in _lane_chunks(cw):
            o_ref[:, cols] = _conv_taps(x_ref[:, cols], jnp.where(first, 0.0, p_ref[:, cols]), w_ref, cols, row)

    return pl.pallas_call(
        body, grid=(c // cw, t // tm),
        in_specs=[pl.BlockSpec((tm, cw), lambda j, i: (i, j)),
                  pl.BlockSpec((SUBLANES, cw), lambda j, i: (jnp.maximum(i * nb8 - 1, 0), j)),
                  pl.BlockSpec((k, cw), lambda j, i: (0, j))],
        out_specs=pl.BlockSpec((tm, cw), lambda j, i: (i, j)),
        out_shape=jax.ShapeDtypeStruct((t, c), F32), name=name,
        compiler_params=_params(("parallel", "parallel")),
    )(x, x, w)


def conv_bwd(dy, x, w, *, name):
    t = x.shape[0]
    k, c = w.shape
    tm, cw = min(CONV_TM, t), CONV_CW
    nb8 = tm // SUBLANES
    ni = t // tm

    def body(dy_ref, dn_ref, x_ref, p_ref, w_ref, dx_ref, dw_ref):
        i = pl.program_id(1)
        first, last = i == 0, i == ni - 1
        row = lax.broadcasted_iota(jnp.int32, (tm, LANES), 0)
        for cols in _lane_chunks(cw):
            dyv, xv = dy_ref[:, cols], x_ref[:, cols]
            nxt = jnp.where(last, 0.0, dn_ref[:, cols])
            prev = jnp.where(first, 0.0, p_ref[:, cols])
            dx = dyv * w_ref[pl.ds(k - 1, 1), cols]
            dws = [jnp.sum(dyv * xv, axis=0, keepdims=True)]
            for s in range(1, k):
                dx = dx + _shift_up(dyv, nxt, s, row) * w_ref[pl.ds(k - 1 - s, 1), cols]
                dws.append(jnp.sum(dyv * _shift_down(xv, prev, s, row), axis=0, keepdims=True))
            dx_ref[:, cols] = dx.astype(dx_ref.dtype)
            for s in range(k):
                @pl.when(first)
                def _(s=s, dws=dws, cols=cols):
                    dw_ref[pl.ds(k - 1 - s, 1), cols] = dws[s]

                @pl.when(jnp.logical_not(first))
                def _(s=s, dws=dws, cols=cols):
                    dw_ref[pl.ds(k - 1 - s, 1), cols] += dws[s]

    return pl.pallas_call(
        body, grid=(c // cw, ni),
        in_specs=[pl.BlockSpec((tm, cw), lambda j, i: (i, j)),
                  pl.BlockSpec((SUBLANES, cw), lambda j, i: (jnp.minimum((i + 1) * nb8, t // SUBLANES - 1), j)),
                  pl.BlockSpec((tm, cw), lambda j, i: (i, j)),
                  pl.BlockSpec((SUBLANES, cw), lambda j, i: (jnp.maximum(i * nb8 - 1, 0), j)),
                  pl.BlockSpec((k, cw), lambda j, i: (0, j))],
        out_specs=[pl.BlockSpec((tm, cw), lambda j, i: (i, j)), pl.BlockSpec((k, cw), lambda j, i: (0, j))],
        out_shape=[jax.ShapeDtypeStruct((t, c), BF16), jax.ShapeDtypeStruct((k, c), F32)], name=name,
        compiler_params=_params(("parallel", "arbitrary")),
    )(dy, dy, x, x, w)


FFN_TM = 128
FFN_CW = D_FF // 2


def _ffn_specs(t, tm, cw, k):
    nb8, ncol = tm // SUBLANES, D_FF // cw
    cur = lambda off: pl.BlockSpec((tm, cw), lambda j, i: (i, j + off))
    prev = lambda off: pl.BlockSpec((SUBLANES, cw), lambda j, i: (jnp.maximum(i * nb8 - 1, 0), j + off))
    nxt = lambda off: pl.BlockSpec((SUBLANES, cw), lambda j, i: (jnp.minimum((i + 1) * nb8, t // SUBLANES - 1), j + off))
    taps = lambda off: pl.BlockSpec((k, cw), lambda j, i: (0, j + off))
    return cur, prev, nxt, taps, ncol


def conv_act_fwd(u, w, *, name):
    t, k = u.shape[0], w.shape[0]
    tm, cw = min(FFN_TM, t), FFN_CW
    cur, prev, _, taps, ncol = _ffn_specs(t, tm, cw, k)

    def body(ug_ref, pg_ref, uv_ref, pv_ref, wg_ref, wv_ref, o_ref):
        first = pl.program_id(1) == 0
        row = lax.broadcasted_iota(jnp.int32, (tm, LANES), 0)
        for cb in range(cw // LANES):
            cols = slice(cb * LANES, (cb + 1) * LANES)
            cg = _conv_taps(ug_ref[:, cols], jnp.where(first, 0.0, pg_ref[:, cols]), wg_ref, cols, row)
            cv = _conv_taps(uv_ref[:, cols], jnp.where(first, 0.0, pv_ref[:, cols]), wv_ref, cols, row)
            o_ref[:, cols] = _f_act(cg, cv).astype(o_ref.dtype)

    return pl.pallas_call(
        body, grid=(ncol, t // tm),
        in_specs=[cur(0), prev(0), cur(ncol), prev(ncol), taps(0), taps(ncol)],
        out_specs=cur(0), out_shape=jax.ShapeDtypeStruct((t, D_FF), BF16), name=name,
        compiler_params=_params(("parallel", "parallel")),
    )(u, u, u, u, w, w)


def conv_act_bwd(u, dact, w, *, name):
    t, k = u.shape[0], w.shape[0]
    tm, cw = min(FFN_TM, t), FFN_CW
    cur, prev, nxt, taps, ncol = _ffn_specs(t, tm, cw, k)
    ni = t // tm

    def body(ug_ref, pg_ref, ng_ref, uv_ref, pv_ref, nv_ref, d_ref, dn_ref, wg_ref, wv_ref, dg_ref, dv_ref, dwg_ref, dwv_ref):
        i = pl.program_id(1)
        first, last = i == 0, i == ni - 1
        row = lax.broadcasted_iota(jnp.int32, (tm, LANES), 0)
        row8 = lax.broadcasted_iota(jnp.int32, (SUBLANES, LANES), 0)
        for cb in range(cw // LANES):
            cols = slice(cb * LANES, (cb + 1) * LANES)
            ug, uv = ug_ref[:, cols], uv_ref[:, cols]
            pg, pv = jnp.where(first, 0.0, pg_ref[:, cols]), jnp.where(first, 0.0, pv_ref[:, cols])
            sg = [ug] + [_shift_down(ug, pg, s, row) for s in range(1, k)]
            sv = [uv] + [_shift_down(uv, pv, s, row) for s in range(1, k)]
            taps = lambda xs, w_ref: sum(xs[s] * w_ref[pl.ds(k - 1 - s, 1), cols] for s in range(k))
            _, vjp = jax.vjp(_f_act, taps(sg, wg_ref), taps(sv, wv_ref))
            dcg, dcv = vjp(d_ref[:, cols])
            _, vjp_n = jax.vjp(_f_act, _conv_taps(ng_ref[:, cols], ug[tm - SUBLANES:], wg_ref, cols, row8),
                               _conv_taps(nv_ref[:, cols], uv[tm - SUBLANES:], wv_ref, cols, row8))
            dcgn, dcvn = vjp_n(jnp.where(last, 0.0, dn_ref[:, cols]))
            for dc, dcn, xs, w_ref, dx_ref, dw_ref in ((dcg, dcgn, sg, wg_ref, dg_ref, dwg_ref),
                                                       (dcv, dcvn, sv, wv_ref, dv_ref, dwv_ref)):
                dx = dc * w_ref[pl.ds(k - 1, 1), cols]
                dws = [jnp.sum(dc * xs[0], axis=0, keepdims=True)]
                for s in range(1, k):
                    dx = dx + _shift_up(dc, dcn, s, row) * w_ref[pl.ds(k - 1 - s, 1), cols]
                    dws.append(jnp.sum(dc * xs[s], axis=0, keepdims=True))
                dx_ref[:, cols] = dx.astype(dx_ref.dtype)
                for s in range(k):
                    @pl.when(first)
                    def _(s=s, dw_ref=dw_ref, dws=dws):
                        dw_ref[pl.ds(k - 1 - s, 1), cols] = dws[s]

                    @pl.when(jnp.logical_not(first))
                    def _(s=s, dw_ref=dw_ref, dws=dws):
                        dw_ref[pl.ds(k - 1 - s, 1), cols] += dws[s]

    half = jax.ShapeDtypeStruct((t, D_FF), BF16)
    dwh = jax.ShapeDtypeStruct((k, D_FF), F32)
    return pl.pallas_call(
        body, grid=(ncol, ni),
        in_specs=[cur(0), prev(0), nxt(0), cur(ncol), prev(ncol), nxt(ncol), cur(0), nxt(0), taps(0), taps(ncol)],
        out_specs=[cur(0), cur(0), taps(0), taps(0)], out_shape=[half, half, dwh, dwh], name=name,
        compiler_params=_params(("parallel", "arbitrary")),
    )(u, u, u, u, u, u, dact, dact, w, w)


def _each(f, *lists):
    return [f(*a) for a in zip(*lists)]


@jax.custom_vjp
def _inv_unit_lower(lms):
    return _inv_blocks(lms)


def _inv_blocks(lms):
    c = lms[0].shape[0]
    ri = lax.broadcasted_iota(jnp.int32, (c, c), 0)
    ci = lax.broadcasted_iota(jnp.int32, (c, c), 1)
    eye = (ri == ci).astype(F32)
    dms = _each(lambda lm: eye - jnp.where((ri >> 1) == (ci >> 1), lm, 0.0), lms)
    for lv in range(1, int(math.log2(c))):
        below = ((ri >> (lv + 1)) == (ci >> (lv + 1))) & ((ri >> lv) != (ci >> lv))
        dbs = _each(lambda dm: dm.astype(BF16), dms)
        ods = _each(lambda lm, db: _dot(jnp.where(below, lm, 0.0).astype(BF16), db).astype(BF16), lms, dbs)
        dms = _each(lambda dm, db, od: dm - _dot(db, od), dms, dbs, ods)
    return dms


def _inv_fwd(lms):
    tms = _inv_blocks(lms)
    return tms, tms


def _inv_bwd(tms, dts):
    tbs = _each(lambda tm: tm.astype(BF16), tms)
    mid = _each(lambda tb, dt: _dot(tb, dt.astype(BF16), TN).astype(BF16), tbs, dts)
    return (_each(lambda m, tb: -_dot(m, tb, NT), mid, tbs),)


_inv_unit_lower.defvjp(_inv_fwd, _inv_bwd)


def _l2n(x):
    return x * lax.rsqrt(jnp.sum(x * x, axis=-1, keepdims=True) + EPS)


def _prep_fn(cqs, cks, cvs, bg, sel_b, sel_g):
    c = cqs[0].shape[0]
    ri = lax.broadcasted_iota(jnp.int32, (c, c), 0)
    ci = lax.broadcasted_iota(jnp.int32, (c, c), 1)
    eye = (ri == ci).astype(F32)
    incl, strict = ci <= ri, ci < ri
    last = lax.broadcasted_iota(jnp.int32, (c, 1), 0) == c - 1
    to_row = lambda col: jnp.sum(col * eye, axis=0, keepdims=True)
    qs = _each(lambda a: _l2n(_silu(a)) * (HEAD_DIM_A ** -0.5), cqs)
    ks = _each(lambda a: _l2n(_silu(a)), cks)
    vbs = _each(lambda a: _silu(a).astype(BF16), cvs)
    betas = _each(lambda m: jnp.sum(bg * m, axis=1, keepdims=True), sel_b)
    gs = _each(lambda m: jnp.sum(bg * m, axis=1, keepdims=True), sel_g)
    gcss = _each(lambda g: jnp.sum(jnp.where(incl, to_row(g), 0.0), axis=1, keepdims=True), gs)
    gtots = _each(lambda gcs: jnp.sum(jnp.where(last, gcs, 0.0), axis=0, keepdims=True), gcss)
    decays = _each(lambda gcs: jnp.exp(jnp.where(incl, gcs - to_row(gcs), NEG)), gcss)
    kbs = _each(lambda k: k.astype(BF16), ks)
    lms = _each(lambda beta, kb, dec: jnp.where(strict, beta * _dot(kb, kb, NT) * dec, 0.0), betas, kbs, decays)
    ams = _each(lambda tm, beta: (tm * to_row(beta)).astype(BF16), _inv_unit_lower(lms), betas)
    gams = _each(jnp.exp, gcss)
    u0s = _each(_dot, ams, vbs)
    wks = _each(lambda am, gam, k: _dot(am, (gam * k).astype(BF16)), ams, gams, ks)
    qks = _each(lambda q, kb, dec: _dot(q.astype(BF16), kb, NT) * dec, qs, kbs, decays)
    qds = _each(lambda q, gam: q * gam, qs, gams)
    kds = _each(lambda k, gtot, gcs: k * jnp.exp(gtot - gcs), ks, gtots, gcss)
    gls = _each(lambda gtot: jnp.exp(gtot) * jnp.ones((SUBLANES, LANES), F32), gtots)
    return u0s, wks, qds, kds, qks, gls


def _head_masks(h):
    lane = lax.broadcasted_iota(jnp.int32, (1, LANES), 1)
    return (lane == h).astype(F32), (lane == h + N_HEADS_A).astype(F32)


def _hsl(j):
    return slice(j * HEAD_DIM_A, (j + 1) * HEAD_DIM_A)


def gnorm_fwd(o, zsrc, w, *, name):
    t, width = o.shape
    tm = min(256, t)
    zoff = zsrc.shape[1] // width - 1

    def body(o_ref, z_ref, w_ref, out_ref):
        for h in range(N_HEADS_A):
            out_ref[:, _hsl(h)] = _f_gnorm(o_ref[:, _hsl(h)], z_ref[:, _hsl(h)], w_ref[...]).astype(out_ref.dtype)

    rows = pl.BlockSpec((tm, width), lambda i: (i, 0))
    return pl.pallas_call(
        body, grid=(t // tm,),
        in_specs=[rows, pl.BlockSpec((tm, width), lambda i: (i, zoff)), pl.BlockSpec(w.shape, lambda i: (0, 0))],
        out_specs=rows, out_shape=jax.ShapeDtypeStruct((t, width), BF16), name=name, compiler_params=_params(("parallel",)),
    )(o, zsrc, w)


def gnorm_bwd(o, zsrc, w, don, *, name):
    t, width = o.shape
    tm = min(256, t)
    zoff = zsrc.shape[1] // width - 1

    def body(o_ref, z_ref, w_ref, d_ref, do_ref, dz_ref, dw_ref):
        dw = jnp.zeros(w.shape, F32)
        for h in range(N_HEADS_A):
            _, vjp = jax.vjp(_f_gnorm, o_ref[:, _hsl(h)], z_ref[:, _hsl(h)], w_ref[...])
            do, dz, dwh = vjp(d_ref[:, _hsl(h)])
            do_ref[:, _hsl(h)] = do.astype(do_ref.dtype)
            dz_ref[:, _hsl(h)] = dz.astype(dz_ref.dtype)
            dw = dw + dwh
        first = pl.program_id(0) == 0

        @pl.when(first)
        def _():
            dw_ref[...] = dw

        @pl.when(jnp.logical_not(first))
        def _():
            dw_ref[...] += dw

    rows = pl.BlockSpec((tm, width), lambda i: (i, 0))
    wspec = pl.BlockSpec(w.shape, lambda i: (0, 0))
    return pl.pallas_call(
        body, grid=(t // tm,),
        in_specs=[rows, pl.BlockSpec((tm, width), lambda i: (i, zoff)), wspec, rows],
        out_specs=[rows, rows, wspec],
        out_shape=[jax.ShapeDtypeStruct((t, width), BF16)] * 2 + [jax.ShapeDtypeStruct(w.shape, F32)], name=name,
        compiler_params=_params(("arbitrary",)),
    )(o, zsrc, w, don)


def delta_prep(cqkv, bg, *, name):
    t = cqkv.shape[0]
    nh, hd, n = N_HEADS_A, HEAD_DIM_A, t // CHUNK

    def body(cq_ref, ck_ref, cv_ref, bg_ref, u0_ref, wk_ref, qd_ref, kd_ref, qk_ref, gl_ref):
        heads = range(nh)
        masks = [_head_masks(j) for j in heads]
        res = _prep_fn([cq_ref[:, _hsl(j)] for j in heads], [ck_ref[:, _hsl(j)] for j in heads],
                       [cv_ref[:, _hsl(j)] for j in heads], bg_ref[...], [m[0] for m in masks], [m[1] for m in masks])
        for o_ref, rs in zip((u0_ref, wk_ref, qd_ref, kd_ref, qk_ref), res[:5]):
            for j in heads:
                o_ref[:, _hsl(j)] = rs[j]
        for j in heads:
            gl_ref[j * SUBLANES:(j + 1) * SUBLANES, :] = res[5][j]

    blk = lambda off: pl.BlockSpec((CHUNK, nh * hd), lambda i: (i, off))
    return pl.pallas_call(
        body, grid=(n,),
        in_specs=[blk(0), blk(1), blk(2), pl.BlockSpec((CHUNK, LANES), lambda i: (i, 0))],
        out_specs=[blk(0)] * 5 + [pl.BlockSpec((nh * SUBLANES, LANES), lambda i: (i, 0))],
        out_shape=[jax.ShapeDtypeStruct((t, nh * hd), F32)] * 5 + [jax.ShapeDtypeStruct((n * nh * SUBLANES, LANES), F32)],
        name=name, compiler_params=_params(("parallel",)),
    )(cqkv, cqkv, cqkv, bg)


def delta_prep_bwd(cqkv, bg, cts, *, name):
    t = cqkv.shape[0]
    nh, hd, n = N_HEADS_A, HEAD_DIM_A, t // CHUNK

    def body(cq_ref, ck_ref, cv_ref, bg_ref, c0, c1, c2, c3, c4, c5, dc_ref, dbg_ref):
        heads = range(nh)
        masks = [_head_masks(j) for j in heads]
        _, vjp = jax.vjp(lambda a, b, c, d: _prep_fn(a, b, c, d, [m[0] for m in masks], [m[1] for m in masks]),
                         [cq_ref[:, _hsl(j)] for j in heads], [ck_ref[:, _hsl(j)] for j in heads],
                         [cv_ref[:, _hsl(j)] for j in heads], bg_ref[...])
        cts = tuple([c[:, _hsl(j)] for j in heads] for c in (c0, c1, c2, c3, c4))
        dqs, dks, dvs, dbg = vjp(cts + ([c5[j * SUBLANES:(j + 1) * SUBLANES, :] for j in heads],))
        for part, ds in enumerate((dqs, dks, dvs)):
            for j in heads:
                dc_ref[:, _hsl(part * nh + j)] = ds[j]
        dbg_ref[...] = dbg

    blk = lambda off: pl.BlockSpec((CHUNK, nh * hd), lambda i: (i, off))
    gl_spec = pl.BlockSpec((nh * SUBLANES, LANES), lambda i: (i, 0))
    bg_spec = pl.BlockSpec((CHUNK, LANES), lambda i: (i, 0))
    return pl.pallas_call(
        body, grid=(n,),
        in_specs=[blk(0), blk(1), blk(2), bg_spec] + [blk(0)] * 5 + [gl_spec],
        out_specs=[pl.BlockSpec((CHUNK, 3 * nh * hd), lambda i: (i, 0)), bg_spec],
        out_shape=[jax.ShapeDtypeStruct((t, 3 * nh * hd), F32), jax.ShapeDtypeStruct((t, LANES), F32)],
        name=name, compiler_params=_params(("parallel",)),
    )(cqkv, cqkv, cqkv, bg, *cts)


def delta_scan(u0, wk, qd, kd, qk, gl, *, name):
    t = u0.shape[0]
    nh, hd, n = N_HEADS_A, HEAD_DIM_A, t // CHUNK

    def body(u0_ref, wk_ref, qd_ref, kd_ref, qk_ref, gl_ref, o_ref, sin_ref, s_ref):
        @pl.when(pl.program_id(0) == 0)
        def _():
            s_ref[...] = jnp.zeros_like(s_ref)

        heads = list(range(nh))
        cols = lambda ref: [ref[:, _hsl(h)].astype(BF16) for h in heads]
        ss = [s_ref[h] for h in heads]
        for h in heads:
            sin_ref[h] = ss[h]
        sbs = _each(lambda s: s.astype(BF16), ss)
        ubs = _each(lambda h, wkb, sb: (u0_ref[:, _hsl(h)] - _dot(wkb, sb)).astype(BF16), heads, cols(wk_ref), sbs)
        os_ = _each(lambda qdb, sb, qkb, ub: _dot(qdb, sb) + _dot(qkb, ub), cols(qd_ref), sbs, cols(qk_ref), ubs)
        sn = _each(lambda h, s, kdb, ub: gl_ref[pl.ds(h * SUBLANES, 1), :] * s + _dot(kdb, ub, TN), heads, ss, cols(kd_ref), ubs)
        for h in heads:
            o_ref[:, _hsl(h)] = os_[h]
            s_ref[h] = sn[h]

    blk = pl.BlockSpec((CHUNK, nh * hd), lambda i: (i, 0))
    return pl.pallas_call(
        body, grid=(n,),
        in_specs=[blk] * 5 + [pl.BlockSpec((nh * SUBLANES, LANES), lambda i: (i, 0))],
        out_specs=[blk, pl.BlockSpec((None, nh, hd, hd), lambda i: (i, 0, 0, 0))],
        out_shape=[jax.ShapeDtypeStruct((t, nh * hd), F32), jax.ShapeDtypeStruct((n, nh, hd, hd), F32)],
        scratch_shapes=[pltpu.VMEM((nh, hd, hd), F32)], name=name,
        compiler_params=_params(("arbitrary",)),
    )(u0, wk, qd, kd, qk, gl)


def delta_scan_bwd(do, u0, wk, qd, kd, qk, gl, s_in, *, name):
    t = u0.shape[0]
    nh, hd, n = N_HEADS_A, HEAD_DIM_A, t // CHUNK

    def body(do_ref, u0_ref, wk_ref, qd_ref, kd_ref, qk_ref, gl_ref, sin_ref,
             du0_ref, dwk_ref, dqd_ref, dkd_ref, dqk_ref, dgl_ref, ds_ref):
        @pl.when(pl.program_id(0) == 0)
        def _():
            ds_ref[...] = jnp.zeros_like(ds_ref)

        corner = (lax.broadcasted_iota(jnp.int32, (SUBLANES, LANES), 0) == 0) & (lax.broadcasted_iota(jnp.int32, (SUBLANES, LANES), 1) == 0)
        heads = list(range(nh))
        cols = lambda ref: [ref[:, _hsl(h)].astype(BF16) for h in heads]
        ss, dss = [sin_ref[h] for h in heads], [ds_ref[h] for h in heads]
        sbs, dsbs = _each(lambda s: s.astype(BF16), ss), _each(lambda d: d.astype(BF16), dss)
        dobs, wkbs, qdbs, kdbs, qkbs = cols(do_ref), cols(wk_ref), cols(qd_ref), cols(kd_ref), cols(qk_ref)
        ubs = _each(lambda h, wkb, sb: (u0_ref[:, _hsl(h)] - _dot(wkb, sb)).astype(BF16), heads, wkbs, sbs)
        dus = _each(lambda qkb, dob, kdb, dsb: _dot(qkb, dob, TN) + _dot(kdb, dsb), qkbs, dobs, kdbs, dsbs)
        dubs = _each(lambda du: du.astype(BF16), dus)
        dwks = _each(lambda dub, sb: -_dot(dub, sb, NT), dubs, sbs)
        dqds = _each(lambda dob, sb: _dot(dob, sb, NT), dobs, sbs)
        dkds = _each(lambda ub, dsb: _dot(ub, dsb, NT), ubs, dsbs)
        dqks = _each(lambda dob, ub: _dot(dob, ub, NT), dobs, ubs)
        dgls = _each(lambda s, d: jnp.sum(jnp.sum(s * d, axis=1, keepdims=True), axis=0, keepdims=True), ss, dss)
        dsn = _each(lambda h, d, qdb, dob, wkb, dub: gl_ref[pl.ds(h * SUBLANES, 1), :] * d + _dot(qdb, dob, TN) - _dot(wkb, dub, TN),
                    heads, dss, qdbs, dobs, wkbs, dubs)
        for h in heads:
            du0_ref[:, _hsl(h)] = dus[h]
            dwk_ref[:, _hsl(h)] = dwks[h]
            dqd_ref[:, _hsl(h)] = dqds[h]
            dkd_ref[:, _hsl(h)] = dkds[h]
            dqk_ref[:, _hsl(h)] = dqks[h]
            dgl_ref[h * SUBLANES:(h + 1) * SUBLANES, :] = jnp.where(corner, dgls[h], 0.0)
            ds_ref[h] = dsn[h]

    blk = pl.BlockSpec((CHUNK, nh * hd), lambda i: (n - 1 - i, 0))
    gl_spec = pl.BlockSpec((nh * SUBLANES, LANES), lambda i: (n - 1 - i, 0))
    return pl.pallas_call(
        body, grid=(n,),
        in_specs=[blk] * 6 + [gl_spec, pl.BlockSpec((None, nh, hd, hd), lambda i: (n - 1 - i, 0, 0, 0))],
        out_specs=[blk] * 5 + [gl_spec],
        out_shape=[jax.ShapeDtypeStruct((t, nh * hd), F32)] * 5 + [jax.ShapeDtypeStruct((n * nh * SUBLANES, LANES), F32)],
        scratch_shapes=[pltpu.VMEM((nh, hd, hd), F32)], name=name,
        compiler_params=_params(("arbitrary",)),
    )(do, u0, wk, qd, kd, qk, gl, s_in)


N_PAIRS = N_HEADS_B // 2
PAIRS_PER_KV = N_PAIRS // N_KV_B


def _psl(j):
    return slice(j * LANES, (j + 1) * LANES)


KV_STEP = 2


def _att_fn(qps, kcs, kps, vcs, vps, sinks, kv0, first):
    w = WINDOW
    lane = lax.broadcasted_iota(jnp.int32, (1, LANES), 1)
    lo = (lane < HEAD_DIM_B).astype(F32)
    qi = lax.broadcasted_iota(jnp.int32, (w, w), 0)
    kj = lax.broadcasted_iota(jnp.int32, (w, w), 1)
    dist_c = (qi - kj).astype(F32)
    valid_c = kj <= qi
    valid_p = (kj > qi) & (first < 0.5)
    bf = lambda xs: [a.astype(BF16) for a in xs]
    kcb, kpb, vcb, vpb = bf(kcs), bf(kps), bf(vcs), bf(vps)
    scale = HEAD_DIM_B ** -0.5
    heads = [(g, j, half) for g in range(len(kcs)) for j in range(PAIRS_PER_KV) for half in range(2)]
    kvs = [g for g, _, _ in heads]
    hmasks = [lo if half == 0 else 1.0 - lo for _, _, half in heads]
    hds = [2.0 * (PAIRS_PER_KV * (kv0 + g) + j) + half for g, j, half in heads]
    slopes = _each(lambda hd: jnp.exp(-(hd + 1.0) * (8.0 / N_HEADS_B * math.log(2.0))), hds)
    snks = _each(lambda hd: jnp.sum(sinks * (lane.astype(F32) == hd).astype(F32), axis=1, keepdims=True), hds)
    qhs = _each(lambda h, hm: (qps[h[0] * PAIRS_PER_KV + h[1]] * hm).astype(BF16), heads, hmasks)
    lcs = _each(lambda qh, g, sl: jnp.where(valid_c, _dot(qh, kcb[g], NT) * scale - sl * dist_c, NEG), qhs, kvs, slopes)
    lps = _each(lambda qh, g, sl: jnp.where(valid_p, _dot(qh, kpb[g], NT) * scale - sl * (dist_c + w), NEG), qhs, kvs, slopes)
    ms = _each(lambda lc, lp, sk: lax.stop_gradient(jnp.maximum(jnp.maximum(jnp.max(lc, axis=1, keepdims=True),
                                                                            jnp.max(lp, axis=1, keepdims=True)), sk)), lcs, lps, snks)
    ecs = _each(lambda lc, m: jnp.exp(lc - m), lcs, ms)
    eps = _each(lambda lp, m: jnp.exp(lp - m), lps, ms)
    invs = _each(lambda ec, ep, sk, m: 1.0 / (jnp.sum(ec, axis=1, keepdims=True) + jnp.sum(ep, axis=1, keepdims=True) + jnp.exp(sk - m)),
                 ecs, eps, snks, ms)
    ohs = _each(lambda ec, ep, inv, g, hm: (_dot((ec * inv).astype(BF16), vcb[g]) + _dot((ep * inv).astype(BF16), vpb[g])) * hm,
                ecs, eps, invs, kvs, hmasks)
    return [ohs[2 * j] + ohs[2 * j + 1] for j in range(len(qps))]


def _scalar11(v):
    return jnp.full((1, 1), v, F32)


def _att_specs(row_of):
    cur = pl.BlockSpec((WINDOW, KV_STEP * LANES), lambda i, kv: (row_of(i), kv))
    prev = pl.BlockSpec((WINDOW, KV_STEP * LANES), lambda i, kv: (jnp.maximum(row_of(i) - 1, 0), kv))
    qs = pl.BlockSpec((WINDOW, KV_STEP * PAIRS_PER_KV * LANES), lambda i, kv: (row_of(i), kv))
    return qs, cur, prev, pl.BlockSpec((1, LANES), lambda i, kv: (0, 0))


def swa_fwd(qsrc, kd, vd, sinks, *, name):
    t = kd.shape[0]
    nb = t // WINDOW
    npair = KV_STEP * PAIRS_PER_KV

    def body(q_ref, kc_ref, kp_ref, vc_ref, vp_ref, s_ref, o_ref):
        first = _scalar11((pl.program_id(0) == 0).astype(F32))
        kv0 = _scalar11((pl.program_id(1) * KV_STEP).astype(F32))
        per_kv = lambda ref: [ref[:, _psl(g)] for g in range(KV_STEP)]
        outs = _att_fn([q_ref[:, _psl(j)] for j in range(npair)], per_kv(kc_ref), per_kv(kp_ref), per_kv(vc_ref), per_kv(vp_ref),
                       s_ref[...], kv0, first)
        for j in range(npair):
            o_ref[:, _psl(j)] = outs[j].astype(o_ref.dtype)

    qs, cur, prev, sk = _att_specs(lambda i: i)
    return pl.pallas_call(
        body, grid=(nb, N_KV_B // KV_STEP), in_specs=[qs, cur, prev, cur, prev, sk],
        out_specs=qs, out_shape=jax.ShapeDtypeStruct((t, N_PAIRS * LANES), BF16), name=name,
        compiler_params=_params(("parallel", "parallel")),
    )(qsrc, kd, kd, vd, vd, sinks)


def swa_bwd(do, qsrc, kd, vd, sinks, *, name):
    t = kd.shape[0]
    nb = t // WINDOW

    npair = KV_STEP * PAIRS_PER_KV

    def body(do_ref, q_ref, kc_ref, kp_ref, vc_ref, vp_ref, s_ref, dq_ref, dk_ref, dv_ref, ds_ref, carry_k, carry_v):
        step, kvg = pl.program_id(0), pl.program_id(1)
        first = _scalar11((step == nb - 1).astype(F32))

        @pl.when((step == 0) & (kvg == 0))
        def _():
            carry_k[...] = jnp.zeros_like(carry_k)
            carry_v[...] = jnp.zeros_like(carry_v)
            ds_ref[...] = jnp.zeros_like(ds_ref)

        kv0 = _scalar11((kvg * KV_STEP).astype(F32))
        per_kv = lambda ref: [ref[:, _psl(g)].astype(F32) for g in range(KV_STEP)]
        _, vjp = jax.vjp(lambda *a: _att_fn(*a, kv0, first), [q_ref[:, _psl(j)].astype(F32) for j in range(npair)],
                         per_kv(kc_ref), per_kv(kp_ref), per_kv(vc_ref), per_kv(vp_ref), s_ref[...])
        dqs, dkc, dkp, dvc, dvp, dsk = vjp([do_ref[:, _psl(j)].astype(F32) for j in range(npair)])
        for j in range(npair):
            dq_ref[:, _psl(j)] = dqs[j].astype(dq_ref.dtype)
        ds_ref[...] += dsk
        fold = lambda g: g + pltpu.roll(g, HEAD_DIM_B, 1)
        for g in range(KV_STEP):
            kv = kvg * KV_STEP + g
            dk_ref[:, _psl(g)] = fold(dkc[g] + carry_k[kv]).astype(dk_ref.dtype)
            dv_ref[:, _psl(g)] = fold(dvc[g] + carry_v[kv]).astype(dv_ref.dtype)
            carry_k[kv] = dkp[g]
            carry_v[kv] = dvp[g]

    qs, cur, prev, sk = _att_specs(lambda i: nb - 1 - i)
    return pl.pallas_call(
        body, grid=(nb, N_KV_B // KV_STEP),
        in_specs=[qs, qs, cur, prev, cur, prev, sk],
        out_specs=[qs, cur, cur, sk],
        out_shape=[jax.ShapeDtypeStruct((t, N_PAIRS * LANES), BF16), jax.ShapeDtypeStruct((t, N_KV_B * LANES), BF16),
                   jax.ShapeDtypeStruct((t, N_KV_B * LANES), BF16), jax.ShapeDtypeStruct((1, LANES), F32)],
        scratch_shapes=[pltpu.VMEM((N_KV_B, WINDOW, LANES), F32), pltpu.VMEM((N_KV_B, WINDOW, LANES), F32)],
        name=name, compiler_params=_params(("arbitrary", "arbitrary")),
    )(do, qsrc, kd, kd, vd, vd, sinks)


def loss_head(h, tgt, w, *, name):
    t, d = h.shape
    tm = min(256, t)

    def body(h_ref, t_ref, w_ref, dh_ref, dw_ref, l_ref):
        tg = t_ref[...]

        def f(hv, wv):
            err = _f_norm(hv, wv) - tg
            return 0.5 * jnp.sum(jnp.sum(err * err, axis=1, keepdims=True), axis=0, keepdims=True) * (1.0 / d)

        lv, vjp = jax.vjp(f, h_ref[...], w_ref[...])
        dh, dw = vjp(jnp.ones((1, 1), F32))
        dh_ref[...] = dh
        first = pl.program_id(0) == 0

        @pl.when(first)
        def _():
            dw_ref[...] = dw
            l_ref[...] = lv * jnp.ones((1, LANES), F32)

        @pl.when(jnp.logical_not(first))
        def _():
            dw_ref[...] += dw
            l_ref[...] += lv * jnp.ones((1, LANES), F32)

    rows = pl.BlockSpec((tm, d), lambda i: (i, 0))
    one = lambda c: pl.BlockSpec((1, c), lambda i: (0, 0))
    return pl.pallas_call(
        body, grid=(t // tm,), in_specs=[rows, rows, one(d)], out_specs=[rows, one(d), one(LANES)],
        out_shape=[jax.ShapeDtypeStruct((t, d), F32), jax.ShapeDtypeStruct((1, d), F32), jax.ShapeDtypeStruct((1, LANES), F32)],
        name=name, compiler_params=_params(("arbitrary",)),
    )(h, tgt, w)


def _row_tile(r, cap=256):
    tr = r
    if r % SUBLANES == 0:
        for cand in range(SUBLANES, min(r, cap) + 1, SUBLANES):
            if r % cand == 0:
                tr = cand
    return tr


def _adamw_update(wv, gv, mv, vv):
    mn = ADAM_B1 * mv + (1.0 - ADAM_B1) * gv
    vn = ADAM_B2 * vv + (1.0 - ADAM_B2) * jnp.square(gv)
    m_hat = mn / (1.0 - ADAM_B1 ** ADAM_STEP)
    v_hat = vn / (1.0 - ADAM_B2 ** ADAM_STEP)
    return -ADAM_LR * (m_hat / (jnp.sqrt(v_hat) + ADAM_EPS) + ADAM_WD * wv), mn, vn


def adamw_layers(w, halves, m, v, *, name):
    nl, r, c = w.shape
    tr = _row_tile(r // 2)
    nbh = r // 2 // tr

    def body(w_ref, *rest):
        g_refs, m_ref, v_ref = rest[:2 * nl], rest[2 * nl], rest[2 * nl + 1]
        d_ref, mo_ref, vo_ref, go_ref = rest[2 * nl + 2:]
        layer, i = pl.program_id(0), pl.program_id(1)
        mine = (i < nbh) == (lax.axis_index("c") == 0)
        gv = jnp.where(mine, g_refs[0][...], g_refs[1][...])
        for k in range(1, nl):
            gv = jnp.where(layer == k, jnp.where(mine, g_refs[2 * k][...], g_refs[2 * k + 1][...]), gv)
        d_ref[...], mo_ref[...], vo_ref[...] = _adamw_update(w_ref[...], gv, m_ref[...], v_ref[...])
        go_ref[...] = gv

    spec3 = pl.BlockSpec((None, tr, c), lambda k, i: (k, i, 0))
    g_specs = [pl.BlockSpec((tr, c), lambda k, i, q=q: (jnp.where(k == q, i % nbh, 0), 0)) for q in range(nl) for _ in range(2)]
    return pl.pallas_call(
        body, grid=(nl, r // tr), in_specs=[spec3] + g_specs + [spec3, spec3], out_specs=[spec3] * 4,
        out_shape=[jax.ShapeDtypeStruct((nl, r, c), F32)] * 4, name=name, compiler_params=_params(("arbitrary", "arbitrary")),
    )(w, *[h for pair in halves for h in pair], m, v)


def adamw(w, g, m, v, *, name):
    r, c = w.shape
    tr = _row_tile(r)

    def body(w_ref, g_ref, m_ref, v_ref, d_ref, mo_ref, vo_ref):
        d_ref[...], mo_ref[...], vo_ref[...] = _adamw_update(w_ref[...], g_ref[...], m_ref[...], v_ref[...])

    spec = pl.BlockSpec((tr, c), lambda i: (i, 0))
    return pl.pallas_call(
        body, grid=(r // tr,), in_specs=[spec] * 4, out_specs=[spec] * 3,
        out_shape=[jax.ShapeDtypeStruct((r, c), F32)] * 3, name=name, compiler_params=_params(("parallel",)),
    )(w, g, m, v)


def _place():
    return lax.axis_index("x"), lax.axis_index("y"), lax.axis_index("c")


def allgather8(blk, *, name):
    def body(x_ref, out_ref, send_sems, recv_sems, local_sem):
        x, y, c = _place()
        me = 4 * x + 2 * y + c
        mine = pltpu.make_async_copy(x_ref, out_ref.at[me], local_sem)
        mine.start()
        sent = []
        for k in range(1, N_DEV):
            to = (x ^ ((k >> 2) & 1), y ^ ((k >> 1) & 1), c ^ (k & 1))
            cp = pltpu.make_async_remote_copy(src_ref=x_ref, dst_ref=out_ref.at[me], send_sem=send_sems.at[k - 1],
                                              recv_sem=recv_sems.at[k - 1], device_id=to, device_id_type=MESH)
            cp.start()
            sent.append(cp)
        for k in range(1, N_DEV):
            frm = me ^ k
            pltpu.make_async_remote_copy(src_ref=x_ref, dst_ref=out_ref.at[frm], send_sem=send_sems.at[k - 1],
                                         recv_sem=recv_sems.at[k - 1], device_id=(x, y, c), device_id_type=MESH).wait_recv()
        for cp in sent:
            cp.wait_send()
        mine.wait()

    vm = pl.BlockSpec(memory_space=pltpu.VMEM)
    return pl.pallas_call(
        body, in_specs=[vm], out_specs=vm, out_shape=jax.ShapeDtypeStruct((N_DEV,) + blk.shape, blk.dtype), name=name,
        scratch_shapes=[pltpu.SemaphoreType.DMA((N_DEV - 1,)), pltpu.SemaphoreType.DMA((N_DEV - 1,)), pltpu.SemaphoreType.DMA],
    )(blk)


def _other_chips(x, y):
    return [(1 - x, y), (x, 1 - y), (1 - x, 1 - y)]


def _hbm_call(body, ins, out_shapes, n_sems, name):
    hbm = pl.BlockSpec(memory_space=pl.ANY)
    return pl.pallas_call(
        body, in_specs=[hbm] * len(ins), out_specs=[hbm] * len(out_shapes), out_shape=out_shapes, name=name,
        scratch_shapes=[pltpu.SemaphoreType.DMA((n_sems,)), pltpu.SemaphoreType.DMA((n_sems,))],
    )(*ins)


def _half_rows(c, rh):
    return pl.ds(pl.multiple_of(c * rh, BF16_ROWS), rh)


def gather_units(units, *, name):
    nu = len(units)
    shapes = []
    for arr, layer_major in units:
        r, cols = arr.shape
        shapes.append(jax.ShapeDtypeStruct((2, N_CHIPS, r // 2, cols) if layer_major else (N_CHIPS, r, cols), arr.dtype))

    def body(*refs):
        in_refs, out_refs, send_sems, recv_sems = refs[:nu], refs[nu:2 * nu], refs[2 * nu], refs[2 * nu + 1]
        x, y, c = _place()
        me_chip = 2 * x + y
        sib = (x, y, 1 - c)
        chips = _other_chips(x, y)

        def copy(k, src, dst, to):
            return pltpu.make_async_remote_copy(src_ref=src, dst_ref=dst, send_sem=send_sems.at[k], recv_sem=recv_sems.at[k],
                                                device_id=to, device_id_type=MESH)

        first, passed, landing = [], [], []
        for u, (arr, layer_major) in enumerate(units):
            rh = arr.shape[0] // 2
            out_ref = out_refs[u]
            slot = (lambda chip, half, o=out_ref: o.at[half, chip]) if layer_major else \
                   (lambda chip, half, o=out_ref, rh=rh: o.at[chip, _half_rows(half, rh), :])
            my_half = in_refs[u].at[_half_rows(c, rh), :]
            for j, (cx, cy) in enumerate(chips):
                k = 6 * u + j
                first.append(copy(k, my_half, slot(me_chip, c), (cx, cy, c)))
                passed.append(copy(k + 3, slot(2 * cx + cy, c), slot(2 * cx + cy, c), sib))
                landing.append((copy(k, my_half, slot(2 * cx + cy, c), sib), copy(k + 3, my_half, slot(2 * cx + cy, 1 - c), sib)))
        for cp in first:
            cp.start()
        for (over_ici, _), fwd in zip(landing, passed):
            over_ici.wait_recv()
            fwd.start()
        for _, from_sibling in landing:
            from_sibling.wait_recv()
        for cp in first + passed:
            cp.wait_send()

    return _hbm_call(body, [a for a, _ in units], shapes, 6 * nu, name)


HBM_SPEC = pl.BlockSpec(memory_space=pltpu.HBM)
SEM_SPEC = pl.BlockSpec(memory_space=pltpu.SEMAPHORE)
ORDERED_EFFECT = pltpu.SideEffectType.DATAFLOW_SIDE_EFFECTING


def _split_start(body, srcs, land_shapes, after, *, name):
    nu = len(srcs)
    lands = [lax.empty(s.shape, s.dtype) for s in land_shapes]

    def whole(*refs):
        body(refs[:nu], refs[nu:2 * nu], refs[2 * nu + 1], refs[2 * nu + 2])
        refs[-1][...] = jnp.zeros((SUBLANES, LANES), F32)

    hbm = lambda a: pltpu.with_memory_space_constraint(a, pltpu.HBM)
    sems = pltpu.SemaphoreType.DMA((nu,))
    res = pl.pallas_call(
        whole, name=name, in_specs=[HBM_SPEC] * (2 * nu) + [pl.BlockSpec(memory_space=pl.ANY)],
        out_shape=[sems, sems] + [pltpu.HBM(a.shape, a.dtype) for a in srcs] + [pltpu.HBM(s.shape, s.dtype) for s in land_shapes]
        + [jax.ShapeDtypeStruct((SUBLANES, LANES), F32)],
        out_specs=[SEM_SPEC, SEM_SPEC] + [HBM_SPEC] * (2 * nu) + [pl.BlockSpec(memory_space=pltpu.VMEM)],
        input_output_aliases={q: 2 + q for q in range(2 * nu)},
        compiler_params=pltpu.CompilerParams(has_side_effects=ORDERED_EFFECT),
    )(*[hbm(a) for a in srcs], *[hbm(a) for a in lands], after)
    return res[0], res[1], res[2:2 + nu], res[2 + nu:2 + 2 * nu], res[-1]


def _split_wait(pending, moved, after, *, name):
    send_sems, recv_sems, srcs, lands, _ = pending
    nu = len(srcs)

    def body(*refs):
        land_refs, ssem, rsem = refs[nu:2 * nu], refs[2 * nu], refs[2 * nu + 1]
        x, y, c = _place()
        for u in range(nu):
            size = moved(land_refs[u])
            cp = pltpu.make_async_remote_copy(src_ref=size, dst_ref=size, send_sem=ssem.at[u], recv_sem=rsem.at[u],
                                              device_id=(x, y, c), device_id_type=MESH)
            cp.wait_send()
            cp.wait_recv()

    res = pl.pallas_call(
        body, name=name, in_specs=[HBM_SPEC] * (2 * nu) + [SEM_SPEC, SEM_SPEC, pl.BlockSpec(memory_space=pl.ANY)],
        out_shape=[pltpu.HBM(a.shape, a.dtype) for a in srcs] + [pltpu.HBM(a.shape, a.dtype) for a in lands],
        out_specs=[HBM_SPEC] * (2 * nu), input_output_aliases={q: q for q in range(2 * nu)},
        compiler_params=pltpu.CompilerParams(has_side_effects=ORDERED_EFFECT),
    )(*srcs, *lands, send_sems, recv_sems, after)
    return res[nu:]


def gather_start(shards, after, *, name):
    def body(src_refs, land_refs, send_sems, recv_sems):
        x, y, c = _place()
        for u, shard in enumerate(shards):
            rows = _half_rows(c, shard.shape[0] // 2)
            for cx, cy in _other_chips(x, y):
                for core in range(2):
                    pltpu.make_async_remote_copy(src_ref=src_refs[u].at[rows, :], dst_ref=land_refs[u].at[2 * x + y, rows, :],
                                                 send_sem=send_sems.at[u], recv_sem=recv_sems.at[u], device_id=(cx, cy, core),
                                                 device_id_type=MESH).start()

    return _split_start(body, shards, [jax.ShapeDtypeStruct((N_CHIPS,) + s.shape, s.dtype) for s in shards], after, name=name)


def gather_wait(pending, after, *, name):
    return _split_wait(pending, lambda land: land.at[pl.ds(0, N_CHIPS - 1)], after, name=name)


def scatter_start(pairs, *, name):
    def body(src_refs, land_refs, send_sems, recv_sems):
        x, y, c = _place()
        for u in range(len(pairs)):
            for j, (cx, cy) in enumerate(_other_chips(x, y)):
                pltpu.make_async_remote_copy(src_ref=src_refs[u].at[2 * cx + cy], dst_ref=land_refs[u].at[j], send_sem=send_sems.at[u],
                                             recv_sem=recv_sems.at[u], device_id=(cx, cy, c), device_id_type=MESH).start()

    return _split_start(body, pairs, [jax.ShapeDtypeStruct((N_CHIPS - 1,) + p.shape[1:], p.dtype) for p in pairs], pairs[0], name=name)


def scatter_wait(pending, after, *, name):
    return _split_wait(pending, lambda land: land, after, name=name)


def swap_units(units, *, name):
    nu = len(units)

    def body(*refs):
        g_refs, out_refs, send_sems, recv_sems = refs[:nu], refs[nu:2 * nu], refs[2 * nu], refs[2 * nu + 1]
        x, y, c = _place()
        cps = [pltpu.make_async_remote_copy(src_ref=g_refs[u].at[:, _half_rows(1 - c, units[u].shape[1] // 2), :], dst_ref=out_refs[u],
                                            send_sem=send_sems.at[u], recv_sem=recv_sems.at[u], device_id=(x, y, 1 - c),
                                            device_id_type=MESH) for u in range(nu)]
        for cp in cps:
            cp.start()
        for cp in cps:
            cp.wait()

    shapes = [jax.ShapeDtypeStruct((N_CHIPS, g.shape[1] // 2, g.shape[2]), g.dtype) for g in units]
    return _hbm_call(body, units, shapes, nu, name)


def join_units(units, *, name):
    nu = len(units)

    def body(*refs):
        h_refs, out_refs, send_sems, recv_sems = refs[:nu], refs[nu:2 * nu], refs[2 * nu], refs[2 * nu + 1]
        x, y, c = _place()
        cps = [pltpu.make_async_remote_copy(src_ref=h_refs[u], dst_ref=out_refs[u], send_sem=send_sems.at[u], recv_sem=recv_sems.at[u],
                                            device_id=(x, y, 1 - c), device_id_type=MESH) for u in range(nu)]
        for cp in cps:
            cp.start()
        for cp in cps:
            cp.wait()

    return _hbm_call(body, units, [jax.ShapeDtypeStruct(h.shape, h.dtype) for h in units], nu, name)


def _half_tile(rh):
    tr = rh
    for cand in range(BF16_ROWS, min(rh, 512) + 1, BF16_ROWS):
        if rh % cand == 0:
            tr = cand
    return tr


def pair_add(g, sib, *, name):
    nc, rh, cols = sib.shape
    tr = _half_tile(rh)
    nbh = rh // tr

    def body(g0_ref, g1_ref, s_ref, o_ref):
        mine = jnp.where(lax.axis_index("c") == 0, g0_ref[...], g1_ref[...])
        o_ref[...] = (mine.astype(F32) + s_ref[...].astype(F32)).astype(o_ref.dtype)

    blk = lambda off: pl.BlockSpec((None, tr, cols), lambda j, i: (j, off + i, 0))
    return pl.pallas_call(
        body, grid=(nc, nbh), in_specs=[blk(0), blk(nbh), blk(0)], out_specs=blk(0),
        out_shape=jax.ShapeDtypeStruct(sib.shape, BF16), name=name, compiler_params=_params(("parallel", "parallel")),
    )(g, g, sib)


def chips_add(pair, landed, *, name):
    nc, rh, cols = pair.shape
    tr = _half_tile(rh)

    def body(*refs):
        chip = 2 * lax.axis_index("x") + lax.axis_index("y")
        acc = refs[0][...]
        for j in range(1, nc):
            acc = jnp.where(chip == j, refs[j][...], acc)
        acc = acc.astype(F32)
        for r in refs[nc:-1]:
            acc = acc + r[...].astype(F32)
        refs[-1][...] = acc

    part = lambda q: pl.BlockSpec((None, tr, cols), lambda i, q=q: (q, i, 0))
    return pl.pallas_call(
        body, grid=(rh // tr,), in_specs=[part(q) for q in range(nc)] + [part(q) for q in range(landed.shape[0])],
        out_specs=pl.BlockSpec((tr, cols), lambda i: (i, 0)),
        out_shape=jax.ShapeDtypeStruct((rh, cols), F32), name=name, compiler_params=_params(("parallel",)),
    )(*[pair] * nc, *[landed] * landed.shape[0])


def sum8(g, *, name):
    def body(g_ref, o_ref):
        acc = g_ref[0]
        for d in range(1, N_DEV):
            acc = acc + g_ref[d]
        o_ref[...] = acc

    return pl.pallas_call(body, out_shape=jax.ShapeDtypeStruct(g.shape[1:], F32), name=name)(g)


def _dup_halves(a):
    t = a.shape[0]
    a = a.reshape(t, N_KV_B, HEAD_DIM_B)
    return jnp.concatenate([a, a], axis=-1).reshape(t, N_KV_B * LANES)


def _undup(a):
    t = a.shape[0]
    return a.reshape(t, N_KV_B, LANES)[:, :, :HEAD_DIM_B].reshape(t, N_KV_B * HEAD_DIM_B)


def _lane_pad(v, offset=0):
    return jnp.zeros((1, LANES), F32).at[0, offset:offset + v.shape[0]].set(v)


SHARD_UP = 2 * D_FF // N_CHIPS
SHARD_BIN = (N_HEADS_B + 2 * N_KV_B) * HEAD_DIM_B // N_CHIPS
SHARD_PROJ = D_MODEL // N_CHIPS


def local_step(x, p, tgt, sm, weight, on_grads):
    t = x.shape[0]
    rtm = min(256, t)
    hk = N_HEADS_A * HEAD_DIM_A
    qd_b = N_HEADS_B * HEAD_DIM_B
    kd_b = N_KV_B * HEAD_DIM_B
    gs = {}
    norm = lambda h, w, nm: tile_map(_f_norm, [(h, D_MODEL, 0)], [w], [(D_MODEL, BF16)], tm=rtm, ncol=1, name=nm)[0]

    def norm_bwd(h, w, dy, add, nm):
        (dh,), (dw,) = tile_vjp(_f_norm, [(h, D_MODEL, 0)], [w], [(dy, D_MODEL, 0)], n_diff=1, tm=rtm, ncol=1, name=nm,
                                add=(add, D_MODEL, 0))
        return dh, dw

    spec = pl.BlockSpec
    mtm = _tile(D_MODEL, MM_TM_CAP)
    p_bf = p.astype(BF16)
    alog_p = _lane_pad(sm["a_log"][0], N_HEADS_A)
    dtb_p = _lane_pad(sm["a_dt_bias"][0], N_HEADS_A)
    sinks_p = _lane_pad(sm["b_sinks"][0])
    nw = lambda name, i: sm[name][i:i + 1]
    by_chip = lambda kdim, ns: dict(tn=ns, tk=kdim, b_spec=spec((None, kdim, ns), lambda r, j, kk: (j, kk, 0)))
    by_chip_t = lambda ndim, ns: dict(n=ndim, tn=ndim, tk=ns, b_spec=spec((None, ndim, ns), lambda r, j, kk: (kk, j, 0)))
    cache = {}

    def wgt(name, i, after):
        if (name, i) not in cache:
            cache[name, i] = weight(name, i, after)
        return cache[name, i]

    saved = []
    h = x
    hn_next = norm(h, nw("norm_mix", 0), "norm_mix0")
    for i in range(DEPTH):
        s = {"h0": h, "hn": hn_next}
        if i % 2 == 0:
            s["pm"] = mm(s["hn"], wgt("a_main", i, h), name="a_in_main")
            s["pt"] = mm(s["hn"], wgt("a_tail", i, h), name="a_in_tail")
            s["c"] = conv_fwd(s["pm"], wgt("a_conv", i, h), name="a_conv")
            s["bg"] = tile_map(_f_betag, [(s["pt"], LANES, 0)], [alog_p, dtb_p], [(LANES, F32)], tm=rtm, ncol=1, name="a_betag")[0]
            s["prep"] = delta_prep(s["c"], s["bg"], name="a_prep")
            s["o"], s["s_in"] = delta_scan(*s["prep"], name="a_scan")
            s["on"] = gnorm_fwd(s["o"], s["pm"], sm["a_norm"], name="a_gnorm")
            h, s["hf"] = mm(s["on"], wgt("a_w_out", i, s["on"]), add=h, norm_w=nw("norm_ffn", i), name="a_out")
        else:
            s["pb"] = mm(s["hn"], wgt("b_w_in", i, s["hn"]), name="b_in", out_dtype=BF16, n=N_CHIPS * SHARD_BIN,
                         **by_chip(D_MODEL, SHARD_BIN))
            s["kd"], s["vd"] = _dup_halves(s["pb"][:, qd_b:qd_b + kd_b]), _dup_halves(s["pb"][:, qd_b + kd_b:])
            s["ao"] = swa_fwd(s["pb"], s["kd"], s["vd"], sinks_p, name="b_att")
            h, s["hf"] = mm(s["ao"], wgt("b_w_out", i, s["ao"]), add=h, norm_w=nw("norm_ffn", i), name="b_out")
        s["h1"] = h
        s["u"] = mm(s["hf"], wgt("f_w_up", i, s["hf"]), name=f"f_up{i}", n=2 * D_FF, **by_chip(D_MODEL, SHARD_UP))
        s["act"] = conv_act_fwd(s["u"], wgt("f_conv", i, s["hf"]), name=f"f_conv_act{i}")
        h, s["hp"] = mm(s["act"], wgt("f_w_down", i, s["act"]), add=h, norm_w=nw("norm_ple", i), name=f"f_down{i}")
        s["h2"] = h
        s["gl"] = mm(s["hp"], wgt("ple_w_gate", i, s["hp"]), name=f"ple_gate{i}")
        s["pe"] = mm(p_bf[i], wgt("ple_w_proj", i, s["hp"]), name=f"ple_proj{i}", n=D_MODEL, **by_chip(PLE_DIM, SHARD_PROJ))
        rows3 = [(h, D_MODEL, 0), (s["gl"], D_MODEL, 0), (s["pe"], D_MODEL, 0)]
        if i + 1 < DEPTH:
            def mix_norm(hv, g, e, wn):
                hn = hv + _f_ple(g, e)
                return hn, _f_norm(hn, wn)
            h, hn_next = tile_map(mix_norm, rows3, [nw("norm_mix", i + 1)], [(D_MODEL, F32), (D_MODEL, BF16)], tm=rtm, ncol=1,
                                  name=f"ple_mix{i}")
        else:
            h = tile_map(lambda hv, g, e: hv + _f_ple(g, e), rows3, [], [(D_MODEL, F32)], tm=rtm, ncol=1, name=f"ple_mix{i}")[0]
        saved.append(s)

    dh, gnf, loss = loss_head(h, tgt, sm["norm_final"][None, :], name="loss_head")
    gs["norm_final"] = gnf[0]

    g_mix, g_ffn, g_ple, g_conv = ([None] * DEPTH for _ in range(4))
    zero = jnp.zeros((1, 1), F32)
    for i in reversed(range(DEPTH)):
        s, gw = saved[i], {}
        by_rows = lambda g: g.reshape(N_CHIPS, g.shape[0] // N_CHIPS, g.shape[1])
        (dgl, dpe), _ = tile_vjp(_f_ple, [(s["gl"], D_MODEL, 0), (s["pe"], D_MODEL, 0)], [], [(dh, D_MODEL, 0)], n_diff=2,
                                 tm=rtm, ncol=1, name=f"ple_mix_bwd{i}", grad_dtypes=[BF16, BF16])
        gw["ple_w_proj"] = mm(p_bf[i], dpe, ta=True, name=f"ple_proj_dw{i}", out_dtype=BF16, tn=SHARD_PROJ,
                              o_shape=(N_CHIPS, PLE_DIM, SHARD_PROJ), o_spec=spec((None, PLE_DIM, SHARD_PROJ), lambda r, j, kk: (j, r, 0)))
        gw["ple_w_gate"] = by_rows(mm(s["hp"], dgl, ta=True, name=f"ple_gate_dw{i}", out_dtype=BF16))
        dhp = mm(dgl, cache["ple_w_gate", i], tb=True, name=f"ple_gate_dx{i}")
        dh, g_ple[i] = norm_bwd(s["h2"], nw("norm_ple", i) + zero, dhp, dh, f"norm_ple_bwd{i}")

        dact = mm(dh, cache["f_w_down", i], tb=True, name=f"f_down_dx{i}")
        gw["f_w_down"] = by_rows(mm(s["act"], dh, ta=True, name=f"f_down_dw{i}", out_dtype=BF16, tm_cap=D_FF // 2))
        du_halves = conv_act_bwd(s["u"], dact, cache["f_conv", i], name=f"f_conv_act_bwd{i}")
        g_conv[i] = jnp.concatenate(du_halves[2:], axis=1)
        dhf = g_up = None
        for half, du in enumerate(du_halves[:2]):
            c0 = half * (N_CHIPS // 2)
            g_up = mm(s["hf"], du, ta=True, name=f"f_up_dw{i}_{half}", out_dtype=BF16, tn=SHARD_UP, into=g_up,
                      o_shape=(N_CHIPS, D_MODEL, SHARD_UP), o_spec=spec((None, mtm, SHARD_UP), lambda r, j, kk, c0=c0: (c0 + j, r, 0)))
            dhf = mm(du, cache["f_w_up", i], tb=True, name=f"f_up_dx{i}_{half}", n=D_MODEL, tn=D_MODEL, tk=SHARD_UP, add=dhf,
                     b_spec=spec((None, D_MODEL, SHARD_UP), lambda r, j, kk, c0=c0: (c0 + kk, j, 0)))
        gw["f_w_up"] = g_up
        dh, g_ffn[i] = norm_bwd(s["h1"], nw("norm_ffn", i), dhf, dh, f"norm_ffn_bwd{i}")
        token, gw = on_grads(i, "ffn", gw), {}
        w_out = cache["a_w_out" if i % 2 == 0 else "b_w_out", i]
        if token is not None:
            w_out = w_out + token[:1, :1].astype(BF16)

        if i % 2 == 0:
            don = mm(dh, w_out, tb=True, name="a_out_dx")
            gw["a_w_out"] = by_rows(mm(s["on"], dh, ta=True, name="a_out_dw", out_dtype=BF16))
            do, dz, gs["a_norm"] = gnorm_bwd(s["o"], s["pm"], sm["a_norm"], don, name="a_gnorm_bwd")
            dprep = delta_scan_bwd(do, *s["prep"], s["s_in"], name="a_scan_bwd")
            dc, dbg = delta_prep_bwd(s["c"], s["bg"], dprep, name="a_prep_bwd")
            (dpt,), (galog, gdtb) = tile_vjp(_f_betag, [(s["pt"], LANES, 0)], [alog_p, dtb_p], [(dbg, LANES, 0)], n_diff=1,
                                             tm=rtm, ncol=1, name="a_betag_bwd", grad_dtypes=[BF16])
            gs["a_log"] = galog[:, N_HEADS_A:2 * N_HEADS_A]
            gs["a_dt_bias"] = gdtb[:, N_HEADS_A:2 * N_HEADS_A]
            dqkv, gs["a_conv"] = conv_bwd(dc, s["pm"], cache["a_conv", i], name="a_conv_bwd")
            dpm = jnp.concatenate([dqkv, dz], axis=1)
            dhn = mm(dpm, cache["a_main", i], tb=True, name="a_in_main_dx")
            dhn = mm(dpt, cache["a_tail", i], tb=True, add=dhn, name="a_in_tail_dx")
            g_main = mm(s["hn"], dpm, ta=True, name="a_in_main_dw", out_dtype=BF16)
            g_tail = mm(s["hn"], dpt, ta=True, name="a_in_tail_dw", out_dtype=BF16)
            g_in = jnp.concatenate([g_main, g_tail[:, :2 * N_HEADS_A]], axis=1)
            gw["a_w_in"] = g_in.reshape(D_MODEL, N_CHIPS, g_in.shape[1] // N_CHIPS).transpose(1, 0, 2)
        else:
            dao = mm(dh, w_out, tb=True, name="b_out_dx")
            gw["b_w_out"] = by_rows(mm(s["ao"], dh, ta=True, name="b_out_dw", out_dtype=BF16))
            dq, dkd, dvd, gsk = swa_bwd(dao, s["pb"], s["kd"], s["vd"], sinks_p, name="b_att_bwd")
            gs["b_sinks"] = gsk[:, :N_HEADS_B]
            dpb = jnp.concatenate([dq, _undup(dkd), _undup(dvd)], axis=1)
            dhn = mm(dpb, cache["b_w_in", i], tb=True, name="b_in_dx", **by_chip_t(D_MODEL, SHARD_BIN))
            gw["b_w_in"] = mm(s["hn"], dpb, ta=True, name="b_in_dw", out_dtype=BF16, tn=SHARD_BIN,
                              o_shape=(N_CHIPS, D_MODEL, SHARD_BIN), o_spec=spec((None, mtm, SHARD_BIN), lambda r, j, kk: (j, r, 0)))
        dh, g_mix[i] = norm_bwd(s["h0"], nw("norm_mix", i), dhn, dh, f"norm_mix_bwd{i}")
        token = on_grads(i, "mix", gw)
        if token is not None:
            zero = token[:1, :1]

    gs["norm_mix"], gs["norm_ffn"], gs["norm_ple"] = (jnp.concatenate(g, axis=0) for g in (g_mix, g_ffn, g_ple))
    gs["f_conv"] = jnp.stack(g_conv)
    return loss, dh, gs


BIG = ["a_w_in", "a_w_out", "b_w_in", "b_w_out", "f_w_up", "f_w_down", "ple_w_proj", "ple_w_gate"]
LAYERED = {"f_w_up", "f_w_down", "ple_w_proj", "ple_w_gate"}
BY_CHIP = {"b_w_in", "f_w_up", "ple_w_proj"}
LAYER_UNITS = [[("a_w_in", 0), ("a_w_out", 0)] + [(n, 0) for n in sorted(LAYERED)],
               [("b_w_in", 1), ("b_w_out", 1)] + [(n, 1) for n in sorted(LAYERED)]]
CONVS = ["a_conv", "f_conv"]
SMALL = ["norm_mix", "norm_ffn", "norm_ple", "norm_final", "a_log", "a_dt_bias", "a_norm", "b_sinks"]
SMALL_ROWS = 8
CONV_ROWS = 16
CONV_GRAD_ROWS = 48


def _pack_rows(arrs, rows, dtype):
    flat = jnp.concatenate([a.reshape(-1).astype(dtype) for a in arrs])
    return jnp.pad(flat, (0, rows * PACK_COLS - flat.shape[0])).reshape(rows, PACK_COLS)


def _unpack(flat, shapes):
    out, off = [], 0
    for shp in shapes:
        n = math.prod(shp)
        out.append(flat[off:off + n].reshape(shp))
        off += n
    return out


def _pack_small(d, loss=None):
    tail = jnp.concatenate([d["a_log"].reshape(-1), d["a_dt_bias"].reshape(-1), d["a_norm"].reshape(-1), d["b_sinks"].reshape(-1)])
    if loss is not None:
        tail = jnp.concatenate([tail, loss.reshape(-1)[:1]])
    tail = jnp.pad(tail, (0, PACK_COLS - tail.shape[0]))
    return jnp.concatenate([d["norm_mix"], d["norm_ffn"], d["norm_ple"], d["norm_final"][None, :], tail[None, :]], axis=0)


def _unpack_small(a, like):
    out = {"norm_mix": a[0:2], "norm_ffn": a[2:4], "norm_ple": a[4:6], "norm_final": a[6]}
    off = 0
    for nm in ("a_log", "a_dt_bias", "a_norm", "b_sinks"):
        n = like[nm].size
        out[nm] = a[7, off:off + n].reshape(like[nm].shape)
        off += n
    return out, a[7, off]


def _as2d(a):
    return a.reshape(-1, a.shape[-1])


def kernel(x, p, norm_mix, norm_ffn, norm_ple, norm_final, a_w_in, a_conv, a_log, a_dt_bias, a_norm, a_w_out, b_w_in, b_sinks, b_w_out, f_w_up, f_conv, f_w_down, ple_w_proj, ple_w_gate, loss_target, m_norm_mix, m_norm_ffn, m_norm_ple, m_norm_final, m_a_w_in, m_a_conv, m_a_log, m_a_dt_bias, m_a_norm, m_a_w_out, m_b_w_in, m_b_sinks, m_b_w_out, m_f_w_up, m_f_conv, m_f_w_down, m_ple_w_proj, m_ple_w_gate, v_norm_mix, v_norm_ffn, v_norm_ple, v_norm_final, v_a_w_in, v_a_conv, v_a_log, v_a_dt_bias, v_a_norm, v_a_w_out, v_b_w_in, v_b_sinks, v_b_w_out, v_f_w_up, v_f_conv, v_f_w_down, v_ple_w_proj, v_ple_w_gate):
    w = dict(norm_mix=norm_mix, norm_ffn=norm_ffn, norm_ple=norm_ple, norm_final=norm_final, a_w_in=a_w_in, a_conv=a_conv,
             a_log=a_log, a_dt_bias=a_dt_bias, a_norm=a_norm, a_w_out=a_w_out, b_w_in=b_w_in, b_sinks=b_sinks, b_w_out=b_w_out,
             f_w_up=f_w_up, f_conv=f_conv, f_w_down=f_w_down, ple_w_proj=ple_w_proj, ple_w_gate=ple_w_gate)
    m = dict(norm_mix=m_norm_mix, norm_ffn=m_norm_ffn, norm_ple=m_norm_ple, norm_final=m_norm_final, a_w_in=m_a_w_in,
             a_conv=m_a_conv, a_log=m_a_log, a_dt_bias=m_a_dt_bias, a_norm=m_a_norm, a_w_out=m_a_w_out, b_w_in=m_b_w_in,
             b_sinks=m_b_sinks, b_w_out=m_b_w_out, f_w_up=m_f_w_up, f_conv=m_f_conv, f_w_down=m_f_w_down,
             ple_w_proj=m_ple_w_proj, ple_w_gate=m_ple_w_gate)
    v = dict(norm_mix=v_norm_mix, norm_ffn=v_norm_ffn, norm_ple=v_norm_ple, norm_final=v_norm_final, a_w_in=v_a_w_in,
             a_conv=v_a_conv, a_log=v_a_log, a_dt_bias=v_a_dt_bias, a_norm=v_a_norm, a_w_out=v_a_w_out, b_w_in=v_b_w_in,
             b_sinks=v_b_sinks, b_w_out=v_b_w_out, f_w_up=v_f_w_up, f_conv=v_f_conv, f_w_down=v_f_w_down,
             ple_w_proj=v_ple_w_proj, ple_w_gate=v_ple_w_gate)
    xc, yc, cc = _place()
    my_chip = 2 * xc + yc

    shard = {(n, i): w[n][i if n in LAYERED else 0].astype(BF16) for n, i in LAYER_UNITS[0] + LAYER_UNITS[1]}
    first = shard["a_w_in", 0]
    (ga,) = gather_units([(first, False)], name="gather_first")
    ga = lax.dynamic_update_index_in_dim(ga, first, my_chip, 0)
    a_in = jnp.concatenate([ga[j] for j in range(N_CHIPS)], axis=1)
    n_main = 4 * N_HEADS_A * HEAD_DIM_A
    conv_shapes = [w[n].shape for n in CONVS]
    convs = allgather8(_pack_rows([w[n] for n in CONVS], CONV_ROWS, F32), name="gather_convs")
    conv_parts = [_unpack(convs[2 * j].reshape(-1), conv_shapes) for j in range(N_CHIPS)]
    a_conv_full, f_conv_full = (jnp.concatenate([conv_parts[j][q] for j in range(N_CHIPS)], axis=2) for q in range(2))
    ready = {("a_main", 0): a_in[:, :n_main], ("a_tail", 0): jnp.pad(a_in[:, n_main:], ((0, 0), (0, LANES - 2 * N_HEADS_A))),
             ("a_conv", 0): a_conv_full[0], ("f_conv", 0): f_conv_full[0], ("f_conv", 1): f_conv_full[1]}
    later = [[k for k in units if k != ("a_w_in", 0)] for units in LAYER_UNITS]
    pending, after = [], ga
    for layer, keys in enumerate(later):
        pending.append(gather_start([shard[k] for k in keys], after, name=f"gather_start{layer}"))
        after = pending[-1][4]
    sm = {n: w[n] for n in SMALL}
    sm["norm_mix"] = sm["norm_mix"] + after[:1, :1]

    def weight(name, layer, act):
        if (name, layer) not in ready:
            landed = gather_wait(pending[layer], act, name=f"gather_wait{layer}")
            for k, g in zip(later[layer], landed):
                g = lax.dynamic_update_index_in_dim(g, shard[k], my_chip, 0)
                ready[k] = g if k[0] in BY_CHIP else g.reshape(N_CHIPS * g.shape[1], g.shape[2])
        return ready[name, layer]

    pairs, scattered, started = {}, {}, []

    def on_grads(layer, part, gw):
        keys = [k for k in LAYER_UNITS[layer] if (k[0] in LAYERED) == (part == "ffn")]
        from_sib = swap_units([gw[n] for n, _ in keys], name=f"rs_swap_{part}{layer}")
        for (n, _), sib in zip(keys, from_sib):
            pairs[n, layer] = pair_add(gw[n], sib, name=f"rs_add_pair_{n}{layer}")
        started.append((keys, scatter_start([pairs[k] for k in keys], name=f"rs_scatter_start_{part}{layer}"), f"{part}{layer}"))
        return started[-1][1][4]

    loss, grad_x, gs = local_step(x[0], p[:, 0], loss_target[0], sm, weight, on_grads)

    grads, delta, new_m, new_v, g_unit = {}, {}, {}, {}, {}

    def finish(keys, tag):
        halves = [chips_add(pairs[k], scattered[k], name=f"rs_add_chips_{k[0]}{k[1]}") for k in keys]
        g_unit.update(zip(keys, zip(halves, join_units(halves, name=f"rs_join_{tag}"))))
        for n in BIG:
            mine = [(n, i) for i in range(DEPTH) if (n, i) in LAYER_UNITS[i]]
            if n not in delta and all(k in g_unit for k in mine):
                g_layers = [g_unit[k] for k in mine]
                shape3 = (len(g_layers), 2 * g_layers[0][0].shape[0], g_layers[0][0].shape[1])
                res = adamw_layers(w[n].reshape(shape3), g_layers, m[n].reshape(shape3), v[n].reshape(shape3), name=f"adamw_{n}")
                delta[n], new_m[n], new_v[n], grads[n] = (r.reshape(w[n].shape) for r in res)

    last_keys, last_pending, last_tag = started[-1]
    for keys, pend, tag in started[:-1]:
        scattered.update(zip(keys, scatter_wait(pend, last_pending[4], name=f"rs_scatter_wait_{tag}")))
    finish([k for keys, _, _ in started[:-1] for k in keys], "first")

    conv_grads = _pack_rows([gs[n] for n in CONVS], CONV_GRAD_ROWS, F32)
    small_sum = sum8(allgather8(jnp.concatenate([_pack_small(gs, loss), conv_grads]), name="gather_small"), name="sum_small")
    g_sm, loss_sum = _unpack_small(small_sum[:SMALL_ROWS], sm)

    scattered.update(zip(last_keys, scatter_wait(last_pending, small_sum, name=f"rs_scatter_wait_{last_tag}")))
    finish(last_keys, "last")

    for n, full in zip(CONVS, _unpack(small_sum[SMALL_ROWS:].reshape(-1), [gs[n].shape for n in CONVS])):
        g2 = _as2d(lax.dynamic_slice_in_dim(full, my_chip * w[n].shape[-1], w[n].shape[-1], axis=full.ndim - 1))
        d2, m2, v2 = adamw(_as2d(w[n]), g2, _as2d(m[n]), _as2d(v[n]), name=f"adamw_{n}")
        grads[n], delta[n], new_m[n], new_v[n] = (r.reshape(w[n].shape) for r in (g2, d2, m2, v2))
    pk = lambda d: _pack_small(d)
    d2, m2, v2 = adamw(pk(sm), pk(g_sm), pk({n: m[n] for n in SMALL}), pk({n: v[n] for n in SMALL}), name="adamw_small")
    for src, dst in ((d2, delta), (m2, new_m), (v2, new_v)):
        dst.update(_unpack_small(src, sm)[0])
    grads.update(g_sm)

    order = ["norm_mix", "norm_ffn", "norm_ple", "norm_final", "a_w_in", "a_conv", "a_log", "a_dt_bias", "a_norm", "a_w_out",
             "b_w_in", "b_sinks", "b_w_out", "f_w_up", "f_conv", "f_w_down", "ple_w_proj", "ple_w_gate"]
    return (loss_sum, grad_x[None], *[grads[n] for n in order], *[delta[n] for n in order],
            *[new_m[n] for n in order], *[new_v[n] for n in order])
```

```python
import functools
import math

import jax
import jax.numpy as jnp
from jax import lax
from jax.experimental import pallas as pl
from jax.experimental.pallas import tpu as pltpu

F32 = jnp.float32
BF16 = jnp.bfloat16
MESH = pl.DeviceIdType.MESH

D_MODEL = 1024
N_HEADS_A = 8
HEAD_DIM_A = 128
CONV_A = 4
N_HEADS_B = 16
N_KV_B = 4
HEAD_DIM_B = 64
WINDOW = 128
D_FF = 2816
FFN_CONV = 3
PLE_DIM = 256
EPS = 1e-6
DEPTH = 2

ADAM_LR = 0.001
ADAM_B1 = 0.9
ADAM_B2 = 0.999
ADAM_EPS = 1e-08
ADAM_WD = 0.01
ADAM_STEP = 10

LANES = 128
SUBLANES = 8
BF16_ROWS = 16
CHUNK = 128
VMEM_LIMIT = 56 * 1024 * 1024
NEG = -1e30
N_CHIPS = 4
N_DEV = 8
PACK_COLS = 1024


def _params(sem=None):
    return pltpu.CompilerParams(dimension_semantics=sem, vmem_limit_bytes=VMEM_LIMIT)


def _tile(dim, cap):
    if dim % LANES:
        return dim
    best = LANES
    for t in range(LANES, min(dim, cap) + 1, LANES):
        if dim % t == 0:
            best = t
    return best


def _dot(a, b, dims=(((1,), (0,)), ((), ())), precision=None):
    return lax.dot_general(a, b, dims, precision=precision, preferred_element_type=F32)


NN = (((1,), (0,)), ((), ()))
NT = (((1,), (1,)), ((), ()))
TN = (((0,), (0,)), ((), ()))


MM_TM_CAP = 1024
MM_TK_CAP_TOKENS = 2048


def mm(a, b, *, name, ta=False, tb=False, out_dtype=F32, add=None, norm_w=None, tm_cap=MM_TM_CAP, tn_cap=1408, tk_cap=1408,
       n=None, tn=None, tk=None, b_spec=None, o_spec=None, o_shape=None, into=None):
    m, k = (a.shape[1], a.shape[0]) if ta else a.shape
    if b_spec is None:
        n = b.shape[0] if tb else b.shape[1]
        assert (b.shape[1] if tb else b.shape[0]) == k, (a.shape, b.shape, ta, tb)
    tm, tn, tk = _tile(m, tm_cap), tn or _tile(n, tn_cap), tk or _tile(k, MM_TK_CAP_TOKENS if ta else tk_cap)
    assert n % tn == 0 and k % tk == 0, (n, tn, k, tk)
    nk = k // tk
    dims = (((0 if ta else 1,), (1 if tb else 0,)), ((), ()))
    has_add, has_norm = add is not None, norm_w is not None
    assert not has_norm or (tn == n and o_spec is None), "the norm epilogue needs whole rows"
    n_in = 2 + has_add + has_norm + (into is not None)

    def body(*refs):
        a_ref, b_ref = refs[0], refs[1]
        add_ref = refs[2] if has_add else None
        o_ref = refs[n_in]
        part = _dot(a_ref[...].astype(BF16), b_ref[...].astype(BF16), dims)

        def finish(r):
            if has_add:
                r = r + add_ref[...].astype(F32)
            o_ref[...] = r.astype(o_ref.dtype)
            if has_norm:
                refs[n_in + 1][...] = _f_norm(r, refs[2 + has_add][...]).astype(BF16)

        if nk == 1:
            finish(part)
            return
        acc = refs[-1]
        kk = pl.program_id(2)

        @pl.when(kk == 0)
        def _():
            acc[...] = part

        @pl.when(kk > 0)
        def _():
            acc[...] += part

        @pl.when(kk == nk - 1)
        def _():
            finish(acc[...])

    a_spec = pl.BlockSpec((tk, tm), lambda i, j, kk: (kk, i)) if ta else pl.BlockSpec((tm, tk), lambda i, j, kk: (i, kk))
    if b_spec is None:
        b_spec = pl.BlockSpec((tn, tk), lambda i, j, kk: (j, kk)) if tb else pl.BlockSpec((tk, tn), lambda i, j, kk: (kk, j))
    plain_o = pl.BlockSpec((tm, tn), lambda i, j, kk: (i, j))
    if o_spec is None:
        o_spec, o_shape = plain_o, (m, n)
    in_specs = [a_spec, b_spec] + ([plain_o] if has_add else [])
    args = (a, b) + ((add,) if has_add else ())
    out_specs, out_shapes = o_spec, jax.ShapeDtypeStruct(tuple(o_shape), out_dtype)
    if has_norm:
        in_specs.append(pl.BlockSpec((1, n), lambda i, j, kk: (0, 0)))
        args += (norm_w,)
        out_specs, out_shapes = [o_spec, plain_o], [out_shapes, jax.ShapeDtypeStruct((m, n), BF16)]
    aliases = {}
    if into is not None:
        assert into.shape == tuple(o_shape) and into.dtype == out_dtype, (into.shape, o_shape)
        in_specs.append(pl.BlockSpec(memory_space=pl.ANY))
        args += (into,)
        aliases = {n_in - 1: 0}
    return pl.pallas_call(
        body, grid=(m // tm, n // tn, nk), in_specs=in_specs, out_specs=out_specs,
        out_shape=out_shapes, name=name, input_output_aliases=aliases,
        scratch_shapes=[pltpu.VMEM((tm, tn), F32)] if nk > 1 else [],
        compiler_params=_params(("parallel", "parallel", "arbitrary")),
    )(*args)


def _row_spec(tm, cw, coff):
    return pl.BlockSpec((tm, cw), lambda i, j: (i, j + coff))


def _full_spec(shape):
    return pl.BlockSpec(shape, lambda i, j: (0,) * len(shape))


def tile_map(fn, rows, params, outs, *, tm, ncol, name):
    t = rows[0][0].shape[0]
    nin = len(rows) + len(params)

    def body(*refs):
        res = fn(*[r[...] for r in refs[:nin]])
        res = res if isinstance(res, (tuple, list)) else (res,)
        for o_ref, r in zip(refs[nin:], res):
            o_ref[...] = r.astype(o_ref.dtype)

    in_specs = [_row_spec(tm, cw, coff) for (_, cw, coff) in rows] + [_full_spec(p.shape) for p in params]
    res = pl.pallas_call(
        body, grid=(t // tm, ncol), in_specs=in_specs,
        out_specs=[_row_spec(tm, cw, 0) for (cw, _) in outs],
        out_shape=[jax.ShapeDtypeStruct((t, cw * ncol), dt) for (cw, dt) in outs], name=name,
        compiler_params=_params(("parallel", "parallel")),
    )(*[r[0] for r in rows], *params)
    return res


def tile_vjp(fn, rows, params, cts, *, n_diff, tm, ncol, name, add=None, grad_dtypes=None):
    t = rows[0][0].shape[0]
    nr, npar, nct = len(rows), len(params), len(cts)
    has_add = add is not None

    def body(*refs):
        vals = [r[...] for r in refs[:nr + npar + nct + (1 if has_add else 0)]]
        diff, rest, pars = vals[:n_diff], vals[n_diff:nr], vals[nr:nr + npar]
        ctv = vals[nr + npar:nr + npar + nct]
        outs_ref = refs[nr + npar + nct + (1 if has_add else 0):]

        def f(*a):
            res = fn(*a[:n_diff], *rest, *a[n_diff:])
            return tuple(res) if isinstance(res, (tuple, list)) else (res,)

        primal, vjp = jax.vjp(f, *[d.astype(F32) for d in diff], *pars)
        grads = vjp(tuple(c.astype(o.dtype) for c, o in zip(ctv, primal)))
        for q in range(n_diff):
            g = grads[q]
            if has_add and q == 0:
                g = g + vals[-1]
            outs_ref[q][...] = g.astype(outs_ref[q].dtype)
        first = (pl.program_id(0) == 0) & (pl.program_id(1) == 0)
        for q in range(npar):
            o_ref, g = outs_ref[n_diff + q], grads[n_diff + q]

            @pl.when(first)
            def _(o_ref=o_ref, g=g):
                o_ref[...] = g

            @pl.when(jnp.logical_not(first))
            def _(o_ref=o_ref, g=g):
                o_ref[...] += g

    ins = list(rows) + [None] * 0
    in_specs = [_row_spec(tm, cw, coff) for (_, cw, coff) in rows] + [_full_spec(p.shape) for p in params]
    in_specs += [_row_spec(tm, cw, coff) for (_, cw, coff) in cts]
    args = [r[0] for r in rows] + list(params) + [c[0] for c in cts]
    if has_add:
        in_specs.append(_row_spec(tm, add[1], add[2]))
        args.append(add[0])
    out_specs = [_row_spec(tm, rows[q][1], 0) for q in range(n_diff)] + [_full_spec(p.shape) for p in params]
    grad_dtypes = grad_dtypes or [F32] * n_diff
    out_shape = [jax.ShapeDtypeStruct((t, rows[q][1] * ncol), grad_dtypes[q]) for q in range(n_diff)]
    out_shape += [jax.ShapeDtypeStruct(p.shape, F32) for p in params]
    del ins
    res = pl.pallas_call(
        body, grid=(t // tm, ncol), in_specs=in_specs, out_specs=out_specs, out_shape=out_shape, name=name,
        compiler_params=_params(("arbitrary", "arbitrary")),
    )(*args)
    return res[:n_diff], res[n_diff:]


def _silu(x):
    return x * jax.nn.sigmoid(x)


def _f_norm(h, w):
    return h * lax.rsqrt(jnp.mean(h * h, axis=-1, keepdims=True) + EPS) * w


def _f_gnorm(o, z, w):
    return _f_norm(o, w) * _silu(z)


def _f_act(gate, val):
    return _silu(gate) * val


def _f_ple(gl, pe):
    return jax.nn.sigmoid(gl) * pe


def _f_betag(pt, alog, dtb):
    lane = lax.broadcasted_iota(jnp.int32, (1, LANES), 1)
    z = pt + dtb
    softplus = jnp.maximum(z, 0.0) + jnp.log(1.0 + jnp.exp(-jnp.abs(z)))
    g = -jnp.exp(alog) * softplus
    return jnp.where(lane < N_HEADS_A, jax.nn.sigmoid(pt), jnp.where(lane < 2 * N_HEADS_A, g, 0.0))


CONV_TM = 256
CONV_CW = 1024


def _shift_down(x, prev, s, row):
    rp = jnp.tile(pltpu.roll(prev, s, 0), (x.shape[0] // SUBLANES, 1))
    return jnp.where(row < s, rp, pltpu.roll(x, s, 0))


def _shift_up(x, nxt, s, row):
    tm = x.shape[0]
    rn = jnp.tile(pltpu.roll(nxt, SUBLANES - s, 0), (tm // SUBLANES, 1))
    return jnp.where(row >= tm - s, rn, pltpu.roll(x, tm - s, 0))


def _conv_taps(x, prev, w_ref, cols, row):
    k = w_ref.shape[0]
    y = x * w_ref[pl.ds(k - 1, 1), cols]
    for s in range(1, k):
        y = y + _shift_down(x, prev, s, row) * w_ref[pl.ds(k - 1 - s, 1), cols]
    return y


def _lane_chunks(cw):
    return [slice(cb * LANES, (cb + 1) * LANES) for cb in range(cw // LANES)]


def conv_fwd(x, w, *, name):
    t = x.shape[0]
    k, c = w.shape
    tm, cw = min(CONV_TM, t), CONV_CW
    nb8 = tm // SUBLANES

    def body(x_ref, p_ref, w_ref, o_ref):
        first = pl.program_id(1) == 0
        row = lax.broadcasted_iota(jnp.int32, (tm, LANES), 0)
        for cols in _lane_chunks(cw):
            o_ref[:, cols] = _conv_taps(x_ref[:, cols], jnp.where(first, 0.0, p_ref[:, cols]), w_ref, cols, row)

    return pl.pallas_call(
        body, grid=(c // cw, t // tm),
        in_specs=[pl.BlockSpec((tm, cw), lambda j, i: (i, j)),
                  pl.BlockSpec((SUBLANES, cw), lambda j, i: (jnp.maximum(i * nb8 - 1, 0), j)),
                  pl.BlockSpec((k, cw), lambda j, i: (0, j))],
        out_specs=pl.BlockSpec((tm, cw), lambda j, i: (i, j)),
        out_shape=jax.ShapeDtypeStruct((t, c), F32), name=name,
        compiler_params=_params(("parallel", "parallel")),
    )(x, x, w)


def conv_bwd(dy, x, w, *, name):
    t = x.shape[0]
    k, c = w.shape
    tm, cw = min(CONV_TM, t), CONV_CW
    nb8 = tm // SUBLANES
    ni = t // tm

    def body(dy_ref, dn_ref, x_ref, p_ref, w_ref, dx_ref, dw_ref):
        i = pl.program_id(1)
        first, last = i == 0, i == ni - 1
        row = lax.broadcasted_iota(jnp.int32, (tm, LANES), 0)
        for cols in _lane_chunks(cw):
            dyv, xv = dy_ref[:, cols], x_ref[:, cols]
            nxt = jnp.where(last, 0.0, dn_ref[:, cols])
            prev = jnp.where(first, 0.0, p_ref[:, cols])
            dx = dyv * w_ref[pl.ds(k - 1, 1), cols]
            dws = [jnp.sum(dyv * xv, axis=0, keepdims=True)]
            for s in range(1, k):
                dx = dx + _shift_up(dyv, nxt, s, row) * w_ref[pl.ds(k - 1 - s, 1), cols]
                dws.append(jnp.sum(dyv * _shift_down(xv, prev, s, row), axis=0, keepdims=True))
            dx_ref[:, cols] = dx.astype(dx_ref.dtype)
            for s in range(k):
                @pl.when(first)
                def _(s=s, dws=dws, cols=cols):
                    dw_ref[pl.ds(k - 1 - s, 1), cols] = dws[s]

                @pl.when(jnp.logical_not(first))
                def _(s=s, dws=dws, cols=cols):
                    dw_ref[pl.ds(k - 1 - s, 1), cols] += dws[s]

    return pl.pallas_call(
        body, grid=(c // cw, ni),
        in_specs=[pl.BlockSpec((tm, cw), lambda j, i: (i, j)),
                  pl.BlockSpec((SUBLANES, cw), lambda j, i: (jnp.minimum((i + 1) * nb8, t // SUBLANES - 1), j)),
                  pl.BlockSpec((tm, cw), lambda j, i: (i, j)),
                  pl.BlockSpec((SUBLANES, cw), lambda j, i: (jnp.maximum(i * nb8 - 1, 0), j)),
                  pl.BlockSpec((k, cw), lambda j, i: (0, j))],
        out_specs=[pl.BlockSpec((tm, cw), lambda j, i: (i, j)), pl.BlockSpec((k, cw), lambda j, i: (0, j))],
        out_shape=[jax.ShapeDtypeStruct((t, c), BF16), jax.ShapeDtypeStruct((k, c), F32)], name=name,
        compiler_params=_params(("parallel", "arbitrary")),
    )(dy, dy, x, x, w)


FFN_TM = 128
FFN_CW = D_FF // 2


def _ffn_specs(t, tm, cw, k):
    ncol = D_FF // cw
    cur = lambda off: pl.BlockSpec((tm, cw), lambda j, i: (i, j + off))
    prev = lambda off, hr: pl.BlockSpec((hr, cw), lambda j, i: (jnp.maximum(i * (tm // hr) - 1, 0), j + off))
    nxt = lambda off, hr: pl.BlockSpec((hr, cw), lambda j, i: (jnp.minimum((i + 1) * (tm // hr), t // hr - 1), j + off))
    taps = lambda off: pl.BlockSpec((k, cw), lambda j, i: (0, j + off))
    return cur, prev, nxt, taps, ncol


def _rows_before(ref, cols, first):
    return jnp.where(first, 0.0, ref[ref.shape[0] - SUBLANES:, cols].astype(F32))


def conv_act_fwd(u, w, *, name):
    t, k = u.shape[0], w.shape[0]
    tm, cw = min(FFN_TM, t), FFN_CW
    cur, prev, _, taps, ncol = _ffn_specs(t, tm, cw, k)

    def body(ug_ref, pg_ref, uv_ref, pv_ref, wg_ref, wv_ref, o_ref):
        first = pl.program_id(1) == 0
        row = lax.broadcasted_iota(jnp.int32, (tm, LANES), 0)
        for cb in range(cw // LANES):
            cols = slice(cb * LANES, (cb + 1) * LANES)
            cg = _conv_taps(ug_ref[:, cols].astype(F32), _rows_before(pg_ref, cols, first), wg_ref, cols, row)
            cv = _conv_taps(uv_ref[:, cols].astype(F32), _rows_before(pv_ref, cols, first), wv_ref, cols, row)
            o_ref[:, cols] = _f_act(cg, cv).astype(o_ref.dtype)

    return pl.pallas_call(
        body, grid=(ncol, t // tm),
        in_specs=[cur(0), prev(0, BF16_ROWS), cur(ncol), prev(ncol, BF16_ROWS), taps(0), taps(ncol)],
        out_specs=cur(0), out_shape=jax.ShapeDtypeStruct((t, D_FF), BF16), name=name,
        compiler_params=_params(("parallel", "parallel")),
    )(u, u, u, u, w, w)


def conv_act_bwd(u, dact, w, *, name):
    t, k = u.shape[0], w.shape[0]
    tm, cw = min(FFN_TM, t), FFN_CW
    cur, prev, nxt, taps, ncol = _ffn_specs(t, tm, cw, k)
    ni = t // tm

    def body(ug_ref, pg_ref, ng_ref, uv_ref, pv_ref, nv_ref, d_ref, dn_ref, wg_ref, wv_ref, dg_ref, dv_ref, dwg_ref, dwv_ref):
        i = pl.program_id(1)
        first, last = i == 0, i == ni - 1
        row = lax.broadcasted_iota(jnp.int32, (tm, LANES), 0)
        row8 = lax.broadcasted_iota(jnp.int32, (SUBLANES, LANES), 0)
        for cb in range(cw // LANES):
            cols = slice(cb * LANES, (cb + 1) * LANES)
            ug, uv = ug_ref[:, cols].astype(F32), uv_ref[:, cols].astype(F32)
            pg, pv = _rows_before(pg_ref, cols, first), _rows_before(pv_ref, cols, first)
            sg = [ug] + [_shift_down(ug, pg, s, row) for s in range(1, k)]
            sv = [uv] + [_shift_down(uv, pv, s, row) for s in range(1, k)]
            taps = lambda xs, w_ref: sum(xs[s] * w_ref[pl.ds(k - 1 - s, 1), cols] for s in range(k))
            _, vjp = jax.vjp(_f_act, taps(sg, wg_ref), taps(sv, wv_ref))
            dcg, dcv = vjp(d_ref[:, cols])
            after = lambda ref: ref[:SUBLANES, cols].astype(F32)
            _, vjp_n = jax.vjp(_f_act, _conv_taps(after(ng_ref), ug[tm - SUBLANES:], wg_ref, cols, row8),
                               _conv_taps(after(nv_ref), uv[tm - SUBLANES:], wv_ref, cols, row8))
            dcgn, dcvn = vjp_n(jnp.where(last, 0.0, dn_ref[:, cols]))
            for dc, dcn, xs, w_ref, dx_ref, dw_ref in ((dcg, dcgn, sg, wg_ref, dg_ref, dwg_ref),
                                                       (dcv, dcvn, sv, wv_ref, dv_ref, dwv_ref)):
                dx = dc * w_ref[pl.ds(k - 1, 1), cols]
                dws = [jnp.sum(dc * xs[0], axis=0, keepdims=True)]
                for s in range(1, k):
                    dx = dx + _shift_up(dc, dcn, s, row) * w_ref[pl.ds(k - 1 - s, 1), cols]
                    dws.append(jnp.sum(dc * xs[s], axis=0, keepdims=True))
                dx_ref[:, cols] = dx.astype(dx_ref.dtype)
                for s in range(k):
                    @pl.when(first)
                    def _(s=s, dw_ref=dw_ref, dws=dws):
                        dw_ref[pl.ds(k - 1 - s, 1), cols] = dws[s]

                    @pl.when(jnp.logical_not(first))
                    def _(s=s, dw_ref=dw_ref, dws=dws):
                        dw_ref[pl.ds(k - 1 - s, 1), cols] += dws[s]

    half = jax.ShapeDtypeStruct((t, D_FF), BF16)
    dwh = jax.ShapeDtypeStruct((k, D_FF), F32)
    return pl.pallas_call(
        body, grid=(ncol, ni),
        in_specs=[cur(0), prev(0, BF16_ROWS), nxt(0, BF16_ROWS), cur(ncol), prev(ncol, BF16_ROWS), nxt(ncol, BF16_ROWS),
                  cur(0), nxt(0, SUBLANES), taps(0), taps(ncol)],
        out_specs=[cur(0), cur(0), taps(0), taps(0)], out_shape=[half, half, dwh, dwh], name=name,
        compiler_params=_params(("parallel", "arbitrary")),
    )(u, u, u, u, u, u, dact, dact, w, w)


def _each(f, *lists):
    return [f(*a) for a in zip(*lists)]


@jax.custom_vjp
def _inv_unit_lower(lms):
    return _inv_blocks(lms)


def _inv_blocks(lms):
    c = lms[0].shape[0]
    ri = lax.broadcasted_iota(jnp.int32, (c, c), 0)
    ci = lax.broadcasted_iota(jnp.int32, (c, c), 1)
    eye = (ri == ci).astype(F32)
    dms = _each(lambda lm: eye - jnp.where((ri >> 1) == (ci >> 1), lm, 0.0), lms)
    for lv in range(1, int(math.log2(c))):
        below = ((ri >> (lv + 1)) == (ci >> (lv + 1))) & ((ri >> lv) != (ci >> lv))
        dbs = _each(lambda dm: dm.astype(BF16), dms)
        ods = _each(lambda lm, db: _dot(jnp.where(below, lm, 0.0).astype(BF16), db).astype(BF16), lms, dbs)
        dms = _each(lambda dm, db, od: dm - _dot(db, od), dms, dbs, ods)
    return dms


def _inv_fwd(lms):
    tms = _inv_blocks(lms)
    return tms, tms


def _inv_bwd(tms, dts):
    tbs = _each(lambda tm: tm.astype(BF16), tms)
    mid = _each(lambda tb, dt: _dot(tb, dt.astype(BF16), TN).astype(BF16), tbs, dts)
    return (_each(lambda m, tb: -_dot(m, tb, NT), mid, tbs),)


_inv_unit_lower.defvjp(_inv_fwd, _inv_bwd)


def _l2n(x):
    return x * lax.rsqrt(jnp.sum(x * x, axis=-1, keepdims=True) + EPS)


def _prep_fn(cqs, cks, cvs, bg, sel_b, sel_g):
    c = cqs[0].shape[0]
    ri = lax.broadcasted_iota(jnp.int32, (c, c), 0)
    ci = lax.broadcasted_iota(jnp.int32, (c, c), 1)
    eye = (ri == ci).astype(F32)
    incl, strict = ci <= ri, ci < ri
    last = lax.broadcasted_iota(jnp.int32, (c, 1), 0) == c - 1
    to_row = lambda col: jnp.sum(col * eye, axis=0, keepdims=True)
    qs = _each(lambda a: _l2n(_silu(a)) * (HEAD_DIM_A ** -0.5), cqs)
    ks = _each(lambda a: _l2n(_silu(a)), cks)
    vbs = _each(lambda a: _silu(a).astype(BF16), cvs)
    betas = _each(lambda m: jnp.sum(bg * m, axis=1, keepdims=True), sel_b)
    gs = _each(lambda m: jnp.sum(bg * m, axis=1, keepdims=True), sel_g)
    gcss = _each(lambda g: jnp.sum(jnp.where(incl, to_row(g), 0.0), axis=1, keepdims=True), gs)
    gtots = _each(lambda gcs: jnp.sum(jnp.where(last, gcs, 0.0), axis=0, keepdims=True), gcss)
    decays = _each(lambda gcs: jnp.exp(jnp.where(incl, gcs - to_row(gcs), NEG)), gcss)
    kbs = _each(lambda k: k.astype(BF16), ks)
    lms = _each(lambda beta, kb, dec: jnp.where(strict, beta * _dot(kb, kb, NT) * dec, 0.0), betas, kbs, decays)
    ams = _each(lambda tm, beta: (tm * to_row(beta)).astype(BF16), _inv_unit_lower(lms), betas)
    gams = _each(jnp.exp, gcss)
    u0s = _each(_dot, ams, vbs)
    wks = _each(lambda am, gam, k: _dot(am, (gam * k).astype(BF16)), ams, gams, ks)
    qks = _each(lambda q, kb, dec: _dot(q.astype(BF16), kb, NT) * dec, qs, kbs, decays)
    qds = _each(lambda q, gam: q * gam, qs, gams)
    kds = _each(lambda k, gtot, gcs: k * jnp.exp(gtot - gcs), ks, gtots, gcss)
    gls = _each(lambda gtot: jnp.exp(gtot) * jnp.ones((SUBLANES, LANES), F32), gtots)
    return u0s, wks, qds, kds, qks, gls


def _head_masks(h):
    lane = lax.broadcasted_iota(jnp.int32, (1, LANES), 1)
    return (lane == h).astype(F32), (lane == h + N_HEADS_A).astype(F32)


def _hsl(j):
    return slice(j * HEAD_DIM_A, (j + 1) * HEAD_DIM_A)


def gnorm_fwd(o, zsrc, w, *, name):
    t, width = o.shape
    tm = min(256, t)
    zoff = zsrc.shape[1] // width - 1

    def body(o_ref, z_ref, w_ref, out_ref):
        for h in range(N_HEADS_A):
            out_ref[:, _hsl(h)] = _f_gnorm(o_ref[:, _hsl(h)], z_ref[:, _hsl(h)], w_ref[...]).astype(out_ref.dtype)

    rows = pl.BlockSpec((tm, width), lambda i: (i, 0))
    return pl.pallas_call(
        body, grid=(t // tm,),
        in_specs=[rows, pl.BlockSpec((tm, width), lambda i: (i, zoff)), pl.BlockSpec(w.shape, lambda i: (0, 0))],
        out_specs=rows, out_shape=jax.ShapeDtypeStruct((t, width), BF16), name=name, compiler_params=_params(("parallel",)),
    )(o, zsrc, w)


def gnorm_bwd(o, zsrc, w, don, *, name):
    t, width = o.shape
    tm = min(256, t)
    zoff = zsrc.shape[1] // width - 1

    def body(o_ref, z_ref, w_ref, d_ref, do_ref, dz_ref, dw_ref):
        dw = jnp.zeros(w.shape, F32)
        for h in range(N_HEADS_A):
            _, vjp = jax.vjp(_f_gnorm, o_ref[:, _hsl(h)], z_ref[:, _hsl(h)], w_ref[...])
            do, dz, dwh = vjp(d_ref[:, _hsl(h)])
            do_ref[:, _hsl(h)] = do.astype(do_ref.dtype)
            dz_ref[:, _hsl(h)] = dz.astype(dz_ref.dtype)
            dw = dw + dwh
        first = pl.program_id(0) == 0

        @pl.when(first)
        def _():
            dw_ref[...] = dw

        @pl.when(jnp.logical_not(first))
        def _():
            dw_ref[...] += dw

    rows = pl.BlockSpec((tm, width), lambda i: (i, 0))
    wspec = pl.BlockSpec(w.shape, lambda i: (0, 0))
    return pl.pallas_call(
        body, grid=(t // tm,),
        in_specs=[rows, pl.BlockSpec((tm, width), lambda i: (i, zoff)), wspec, rows],
        out_specs=[rows, rows, wspec],
        out_shape=[jax.ShapeDtypeStruct((t, width), BF16)] * 2 + [jax.ShapeDtypeStruct(w.shape, F32)], name=name,
        compiler_params=_params(("arbitrary",)),
    )(o, zsrc, w, don)


def delta_prep(cqkv, bg, *, name):
    t = cqkv.shape[0]
    nh, hd, n = N_HEADS_A, HEAD_DIM_A, t // CHUNK

    def body(cq_ref, ck_ref, cv_ref, bg_ref, u0_ref, wk_ref, qd_ref, kd_ref, qk_ref, gl_ref):
        heads = range(nh)
        masks = [_head_masks(j) for j in heads]
        res = _prep_fn([cq_ref[:, _hsl(j)] for j in heads], [ck_ref[:, _hsl(j)] for j in heads],
                       [cv_ref[:, _hsl(j)] for j in heads], bg_ref[...], [m[0] for m in masks], [m[1] for m in masks])
        for o_ref, rs in zip((u0_ref, wk_ref, qd_ref, kd_ref, qk_ref), res[:5]):
            for j in heads:
                o_ref[:, _hsl(j)] = rs[j]
        for j in heads:
            gl_ref[j * SUBLANES:(j + 1) * SUBLANES, :] = res[5][j]

    blk = lambda off: pl.BlockSpec((CHUNK, nh * hd), lambda i: (i, off))
    return pl.pallas_call(
        body, grid=(n,),
        in_specs=[blk(0), blk(1), blk(2), pl.BlockSpec((CHUNK, LANES), lambda i: (i, 0))],
        out_specs=[blk(0)] * 5 + [pl.BlockSpec((nh * SUBLANES, LANES), lambda i: (i, 0))],
        out_shape=[jax.ShapeDtypeStruct((t, nh * hd), F32)] * 5 + [jax.ShapeDtypeStruct((n * nh * SUBLANES, LANES), F32)],
        name=name, compiler_params=_params(("parallel",)),
    )(cqkv, cqkv, cqkv, bg)


def delta_prep_bwd(cqkv, bg, cts, *, name):
    t = cqkv.shape[0]
    nh, hd, n = N_HEADS_A, HEAD_DIM_A, t // CHUNK

    def body(cq_ref, ck_ref, cv_ref, bg_ref, c0, c1, c2, c3, c4, c5, dc_ref, dbg_ref):
        heads = range(nh)
        masks = [_head_masks(j) for j in heads]
        _, vjp = jax.vjp(lambda a, b, c, d: _prep_fn(a, b, c, d, [m[0] for m in masks], [m[1] for m in masks]),
                         [cq_ref[:, _hsl(j)] for j in heads], [ck_ref[:, _hsl(j)] for j in heads],
                         [cv_ref[:, _hsl(j)] for j in heads], bg_ref[...])
        cts = tuple([c[:, _hsl(j)] for j in heads] for c in (c0, c1, c2, c3, c4))
        dqs, dks, dvs, dbg = vjp(cts + ([c5[j * SUBLANES:(j + 1) * SUBLANES, :] for j in heads],))
        for part, ds in enumerate((dqs, dks, dvs)):
            for j in heads:
                dc_ref[:, _hsl(part * nh + j)] = ds[j]
        dbg_ref[...] = dbg

    blk = lambda off: pl.BlockSpec((CHUNK, nh * hd), lambda i: (i, off))
    gl_spec = pl.BlockSpec((nh * SUBLANES, LANES), lambda i: (i, 0))
    bg_spec = pl.BlockSpec((CHUNK, LANES), lambda i: (i, 0))
    return pl.pallas_call(
        body, grid=(n,),
        in_specs=[blk(0), blk(1), blk(2), bg_spec] + [blk(0)] * 5 + [gl_spec],
        out_specs=[pl.BlockSpec((CHUNK, 3 * nh * hd), lambda i: (i, 0)), bg_spec],
        out_shape=[jax.ShapeDtypeStruct((t, 3 * nh * hd), F32), jax.ShapeDtypeStruct((t, LANES), F32)],
        name=name, compiler_params=_params(("parallel",)),
    )(cqkv, cqkv, cqkv, bg, *cts)


def delta_scan(u0, wk, qd, kd, qk, gl, *, name):
    t = u0.shape[0]
    nh, hd, n = N_HEADS_A, HEAD_DIM_A, t // CHUNK

    def body(u0_ref, wk_ref, qd_ref, kd_ref, qk_ref, gl_ref, o_ref, sin_ref, s_ref):
        @pl.when(pl.program_id(0) == 0)
        def _():
            s_ref[...] = jnp.zeros_like(s_ref)

        heads = list(range(nh))
        cols = lambda ref: [ref[:, _hsl(h)].astype(BF16) for h in heads]
        ss = [s_ref[h] for h in heads]
        for h in heads:
            sin_ref[h] = ss[h]
        sbs = _each(lambda s: s.astype(BF16), ss)
        ubs = _each(lambda h, wkb, sb: (u0_ref[:, _hsl(h)] - _dot(wkb, sb)).astype(BF16), heads, cols(wk_ref), sbs)
        os_ = _each(lambda qdb, sb, qkb, ub: _dot(qdb, sb) + _dot(qkb, ub), cols(qd_ref), sbs, cols(qk_ref), ubs)
        sn = _each(lambda h, s, kdb, ub: gl_ref[pl.ds(h * SUBLANES, 1), :] * s + _dot(kdb, ub, TN), heads, ss, cols(kd_ref), ubs)
        for h in heads:
            o_ref[:, _hsl(h)] = os_[h]
            s_ref[h] = sn[h]

    blk = pl.BlockSpec((CHUNK, nh * hd), lambda i: (i, 0))
    return pl.pallas_call(
        body, grid=(n,),
        in_specs=[blk] * 5 + [pl.BlockSpec((nh * SUBLANES, LANES), lambda i: (i, 0))],
        out_specs=[blk, pl.BlockSpec((None, nh, hd, hd), lambda i: (i, 0, 0, 0))],
        out_shape=[jax.ShapeDtypeStruct((t, nh * hd), F32), jax.ShapeDtypeStruct((n, nh, hd, hd), F32)],
        scratch_shapes=[pltpu.VMEM((nh, hd, hd), F32)], name=name,
        compiler_params=_params(("arbitrary",)),
    )(u0, wk, qd, kd, qk, gl)


def delta_scan_bwd(do, u0, wk, qd, kd, qk, gl, s_in, *, name):
    t = u0.shape[0]
    nh, hd, n = N_HEADS_A, HEAD_DIM_A, t // CHUNK

    def body(do_ref, u0_ref, wk_ref, qd_ref, kd_ref, qk_ref, gl_ref, sin_ref,
             du0_ref, dwk_ref, dqd_ref, dkd_ref, dqk_ref, dgl_ref, ds_ref):
        @pl.when(pl.program_id(0) == 0)
        def _():
            ds_ref[...] = jnp.zeros_like(ds_ref)

        corner = (lax.broadcasted_iota(jnp.int32, (SUBLANES, LANES), 0) == 0) & (lax.broadcasted_iota(jnp.int32, (SUBLANES, LANES), 1) == 0)
        heads = list(range(nh))
        cols = lambda ref: [ref[:, _hsl(h)].astype(BF16) for h in heads]
        ss, dss = [sin_ref[h] for h in heads], [ds_ref[h] for h in heads]
        sbs, dsbs = _each(lambda s: s.astype(BF16), ss), _each(lambda d: d.astype(BF16), dss)
        dobs, wkbs, qdbs, kdbs, qkbs = cols(do_ref), cols(wk_ref), cols(qd_ref), cols(kd_ref), cols(qk_ref)
        ubs = _each(lambda h, wkb, sb: (u0_ref[:, _hsl(h)] - _dot(wkb, sb)).astype(BF16), heads, wkbs, sbs)
        dus = _each(lambda qkb, dob, kdb, dsb: _dot(qkb, dob, TN) + _dot(kdb, dsb), qkbs, dobs, kdbs, dsbs)
        dubs = _each(lambda du: du.astype(BF16), dus)
        dwks = _each(lambda dub, sb: -_dot(dub, sb, NT), dubs, sbs)
        dqds = _each(lambda dob, sb: _dot(dob, sb, NT), dobs, sbs)
        dkds = _each(lambda ub, dsb: _dot(ub, dsb, NT), ubs, dsbs)
        dqks = _each(lambda dob, ub: _dot(dob, ub, NT), dobs, ubs)
        dgls = _each(lambda s, d: jnp.sum(jnp.sum(s * d, axis=1, keepdims=True), axis=0, keepdims=True), ss, dss)
        dsn = _each(lambda h, d, qdb, dob, wkb, dub: gl_ref[pl.ds(h * SUBLANES, 1), :] * d + _dot(qdb, dob, TN) - _dot(wkb, dub, TN),
                    heads, dss, qdbs, dobs, wkbs, dubs)
        for h in heads:
            du0_ref[:, _hsl(h)] = dus[h]
            dwk_ref[:, _hsl(h)] = dwks[h]
            dqd_ref[:, _hsl(h)] = dqds[h]
            dkd_ref[:, _hsl(h)] = dkds[h]
            dqk_ref[:, _hsl(h)] = dqks[h]
            dgl_ref[h * SUBLANES:(h + 1) * SUBLANES, :] = jnp.where(corner, dgls[h], 0.0)
            ds_ref[h] = dsn[h]

    blk = pl.BlockSpec((CHUNK, nh * hd), lambda i: (n - 1 - i, 0))
    gl_spec = pl.BlockSpec((nh * SUBLANES, LANES), lambda i: (n - 1 - i, 0))
    return pl.pallas_call(
        body, grid=(n,),
        in_specs=[blk] * 6 + [gl_spec, pl.BlockSpec((None, nh, hd, hd), lambda i: (n - 1 - i, 0, 0, 0))],
        out_specs=[blk] * 5 + [gl_spec],
        out_shape=[jax.ShapeDtypeStruct((t, nh * hd), F32)] * 5 + [jax.ShapeDtypeStruct((n * nh * SUBLANES, LANES), F32)],
        scratch_shapes=[pltpu.VMEM((nh, hd, hd), F32)], name=name,
        compiler_params=_params(("arbitrary",)),
    )(do, u0, wk, qd, kd, qk, gl, s_in)


N_PAIRS = N_HEADS_B // 2
PAIRS_PER_KV = N_PAIRS // N_KV_B


def _psl(j):
    return slice(j * LANES, (j + 1) * LANES)


KV_STEP = 4


def _att_fn(qps, kcs, kps, vcs, vps, sinks, kv0, first):
    w = WINDOW
    lane = lax.broadcasted_iota(jnp.int32, (1, LANES), 1)
    lo = (lane < HEAD_DIM_B).astype(F32)
    qi = lax.broadcasted_iota(jnp.int32, (w, w), 0)
    kj = lax.broadcasted_iota(jnp.int32, (w, w), 1)
    dist_c = (qi - kj).astype(F32)
    valid_c = kj <= qi
    valid_p = (kj > qi) & (first < 0.5)
    bf = lambda xs: [a.astype(BF16) for a in xs]
    kcb, kpb, vcb, vpb = bf(kcs), bf(kps), bf(vcs), bf(vps)
    scale = HEAD_DIM_B ** -0.5
    heads = [(g, j, half) for g in range(len(kcs)) for j in range(PAIRS_PER_KV) for half in range(2)]
    kvs = [g for g, _, _ in heads]
    hmasks = [lo if half == 0 else 1.0 - lo for _, _, half in heads]
    hds = [2.0 * (PAIRS_PER_KV * (kv0 + g) + j) + half for g, j, half in heads]
    slopes = _each(lambda hd: jnp.exp(-(hd + 1.0) * (8.0 / N_HEADS_B * math.log(2.0))), hds)
    snks = _each(lambda hd: jnp.sum(sinks * (lane.astype(F32) == hd).astype(F32), axis=1, keepdims=True), hds)
    qhs = _each(lambda h, hm: (qps[h[0] * PAIRS_PER_KV + h[1]] * hm).astype(BF16), heads, hmasks)
    lcs = _each(lambda qh, g, sl: jnp.where(valid_c, _dot(qh, kcb[g], NT) * scale - sl * dist_c, NEG), qhs, kvs, slopes)
    lps = _each(lambda qh, g, sl: jnp.where(valid_p, _dot(qh, kpb[g], NT) * scale - sl * (dist_c + w), NEG), qhs, kvs, slopes)
    ms = _each(lambda lc, lp, sk: lax.stop_gradient(jnp.maximum(jnp.maximum(jnp.max(lc, axis=1, keepdims=True),
                                                                            jnp.max(lp, axis=1, keepdims=True)), sk)), lcs, lps, snks)
    ecs = _each(lambda lc, m: jnp.exp(lc - m), lcs, ms)
    eps = _each(lambda lp, m: jnp.exp(lp - m), lps, ms)
    invs = _each(lambda ec, ep, sk, m: 1.0 / (jnp.sum(ec, axis=1, keepdims=True) + jnp.sum(ep, axis=1, keepdims=True) + jnp.exp(sk - m)),
                 ecs, eps, snks, ms)
    ohs = _each(lambda ec, ep, inv, g, hm: (_dot((ec * inv).astype(BF16), vcb[g]) + _dot((ep * inv).astype(BF16), vpb[g])) * hm,
                ecs, eps, invs, kvs, hmasks)
    return [ohs[2 * j] + ohs[2 * j + 1] for j in range(len(qps))]


def _scalar11(v):
    return jnp.full((1, 1), v, F32)


def _att_specs(row_of):
    cur = pl.BlockSpec((WINDOW, KV_STEP * LANES), lambda i, kv: (row_of(i), kv))
    prev = pl.BlockSpec((WINDOW, KV_STEP * LANES), lambda i, kv: (jnp.maximum(row_of(i) - 1, 0), kv))
    qs = pl.BlockSpec((WINDOW, KV_STEP * PAIRS_PER_KV * LANES), lambda i, kv: (row_of(i), kv))
    return qs, cur, prev, pl.BlockSpec((1, LANES), lambda i, kv: (0, 0))


def swa_fwd(qsrc, kd, vd, sinks, *, name):
    t = kd.shape[0]
    nb = t // WINDOW
    npair = KV_STEP * PAIRS_PER_KV

    def body(q_ref, kc_ref, kp_ref, vc_ref, vp_ref, s_ref, o_ref):
        first = _scalar11((pl.program_id(0) == 0).astype(F32))
        kv0 = _scalar11((pl.program_id(1) * KV_STEP).astype(F32))
        per_kv = lambda ref: [ref[:, _psl(g)] for g in range(KV_STEP)]
        outs = _att_fn([q_ref[:, _psl(j)] for j in range(npair)], per_kv(kc_ref), per_kv(kp_ref), per_kv(vc_ref), per_kv(vp_ref),
                       s_ref[...], kv0, first)
        for j in range(npair):
            o_ref[:, _psl(j)] = outs[j].astype(o_ref.dtype)

    qs, cur, prev, sk = _att_specs(lambda i: i)
    return pl.pallas_call(
        body, grid=(nb, N_KV_B // KV_STEP), in_specs=[qs, cur, prev, cur, prev, sk],
        out_specs=qs, out_shape=jax.ShapeDtypeStruct((t, N_PAIRS * LANES), BF16), name=name,
        compiler_params=_params(("parallel", "parallel")),
    )(qsrc, kd, kd, vd, vd, sinks)


def swa_bwd(do, qsrc, kd, vd, sinks, *, name):
    t = kd.shape[0]
    nb = t // WINDOW

    npair = KV_STEP * PAIRS_PER_KV

    def body(do_ref, q_ref, kc_ref, kp_ref, vc_ref, vp_ref, s_ref, dq_ref, dk_ref, dv_ref, ds_ref, carry_k, carry_v):
        step, kvg = pl.program_id(0), pl.program_id(1)
        first = _scalar11((step == nb - 1).astype(F32))

        @pl.when((step == 0) & (kvg == 0))
        def _():
            carry_k[...] = jnp.zeros_like(carry_k)
            carry_v[...] = jnp.zeros_like(carry_v)
            ds_ref[...] = jnp.zeros_like(ds_ref)

        kv0 = _scalar11((kvg * KV_STEP).astype(F32))
        per_kv = lambda ref: [ref[:, _psl(g)].astype(F32) for g in range(KV_STEP)]
        _, vjp = jax.vjp(lambda *a: _att_fn(*a, kv0, first), [q_ref[:, _psl(j)].astype(F32) for j in range(npair)],
                         per_kv(kc_ref), per_kv(kp_ref), per_kv(vc_ref), per_kv(vp_ref), s_ref[...])
        dqs, dkc, dkp, dvc, dvp, dsk = vjp([do_ref[:, _psl(j)].astype(F32) for j in range(npair)])
        for j in range(npair):
            dq_ref[:, _psl(j)] = dqs[j].astype(dq_ref.dtype)
        ds_ref[...] += dsk
        fold = lambda g: g + pltpu.roll(g, HEAD_DIM_B, 1)
        for g in range(KV_STEP):
            kv = kvg * KV_STEP + g
            dk_ref[:, _psl(g)] = fold(dkc[g] + carry_k[kv]).astype(dk_ref.dtype)
            dv_ref[:, _psl(g)] = fold(dvc[g] + carry_v[kv]).astype(dv_ref.dtype)
            carry_k[kv] = dkp[g]
            carry_v[kv] = dvp[g]

    qs, cur, prev, sk = _att_specs(lambda i: nb - 1 - i)
    return pl.pallas_call(
        body, grid=(nb, N_KV_B // KV_STEP),
        in_specs=[qs, qs, cur, prev, cur, prev, sk],
        out_specs=[qs, cur, cur, sk],
        out_shape=[jax.ShapeDtypeStruct((t, N_PAIRS * LANES), BF16), jax.ShapeDtypeStruct((t, N_KV_B * LANES), BF16),
                   jax.ShapeDtypeStruct((t, N_KV_B * LANES), BF16), jax.ShapeDtypeStruct((1, LANES), F32)],
        scratch_shapes=[pltpu.VMEM((N_KV_B, WINDOW, LANES), F32), pltpu.VMEM((N_KV_B, WINDOW, LANES), F32)],
        name=name, compiler_params=_params(("arbitrary", "arbitrary")),
    )(do, qsrc, kd, kd, vd, vd, sinks)


def loss_head(h, tgt, w, *, name):
    t, d = h.shape
    tm = min(256, t)

    def body(h_ref, t_ref, w_ref, dh_ref, dw_ref, l_ref):
        tg = t_ref[...]

        def f(hv, wv):
            err = _f_norm(hv, wv) - tg
            return 0.5 * jnp.sum(jnp.sum(err * err, axis=1, keepdims=True), axis=0, keepdims=True) * (1.0 / d)

        lv, vjp = jax.vjp(f, h_ref[...], w_ref[...])
        dh, dw = vjp(jnp.ones((1, 1), F32))
        dh_ref[...] = dh
        first = pl.program_id(0) == 0

        @pl.when(first)
        def _():
            dw_ref[...] = dw
            l_ref[...] = lv * jnp.ones((1, LANES), F32)

        @pl.when(jnp.logical_not(first))
        def _():
            dw_ref[...] += dw
            l_ref[...] += lv * jnp.ones((1, LANES), F32)

    rows = pl.BlockSpec((tm, d), lambda i: (i, 0))
    one = lambda c: pl.BlockSpec((1, c), lambda i: (0, 0))
    return pl.pallas_call(
        body, grid=(t // tm,), in_specs=[rows, rows, one(d)], out_specs=[rows, one(d), one(LANES)],
        out_shape=[jax.ShapeDtypeStruct((t, d), F32), jax.ShapeDtypeStruct((1, d), F32), jax.ShapeDtypeStruct((1, LANES), F32)],
        name=name, compiler_params=_params(("arbitrary",)),
    )(h, tgt, w)


def _row_tile(r, cap=256):
    tr = r
    if r % SUBLANES == 0:
        for cand in range(SUBLANES, min(r, cap) + 1, SUBLANES):
            if r % cand == 0:
                tr = cand
    return tr


def _adamw_update(wv, gv, mv, vv):
    mn = ADAM_B1 * mv + (1.0 - ADAM_B1) * gv
    vn = ADAM_B2 * vv + (1.0 - ADAM_B2) * jnp.square(gv)
    m_hat = mn / (1.0 - ADAM_B1 ** ADAM_STEP)
    v_hat = vn / (1.0 - ADAM_B2 ** ADAM_STEP)
    return -ADAM_LR * (m_hat / (jnp.sqrt(v_hat) + ADAM_EPS) + ADAM_WD * wv), mn, vn


def adamw_layers(w, halves, m, v, *, name):
    nl, r, c = w.shape
    tr = _row_tile(r // 2)
    nbh = r // 2 // tr

    def body(w_ref, *rest):
        g_refs, m_ref, v_ref = rest[:2 * nl], rest[2 * nl], rest[2 * nl + 1]
        d_ref, mo_ref, vo_ref, go_ref = rest[2 * nl + 2:]
        layer, i = pl.program_id(0), pl.program_id(1)
        mine = (i < nbh) == (lax.axis_index("c") == 0)
        gv = jnp.where(mine, g_refs[0][...], g_refs[1][...])
        for k in range(1, nl):
            gv = jnp.where(layer == k, jnp.where(mine, g_refs[2 * k][...], g_refs[2 * k + 1][...]), gv)
        d_ref[...], mo_ref[...], vo_ref[...] = _adamw_update(w_ref[...], gv, m_ref[...], v_ref[...])
        go_ref[...] = gv

    spec3 = pl.BlockSpec((None, tr, c), lambda k, i: (k, i, 0))
    g_specs = [pl.BlockSpec((tr, c), lambda k, i, q=q: (jnp.where(k == q, i % nbh, 0), 0)) for q in range(nl) for _ in range(2)]
    return pl.pallas_call(
        body, grid=(nl, r // tr), in_specs=[spec3] + g_specs + [spec3, spec3], out_specs=[spec3] * 4,
        out_shape=[jax.ShapeDtypeStruct((nl, r, c), F32)] * 4, name=name, compiler_params=_params(("arbitrary", "arbitrary")),
    )(w, *[h for pair in halves for h in pair], m, v)


def adamw(w, g, m, v, *, name):
    r, c = w.shape
    tr = _row_tile(r)

    def body(w_ref, g_ref, m_ref, v_ref, d_ref, mo_ref, vo_ref):
        d_ref[...], mo_ref[...], vo_ref[...] = _adamw_update(w_ref[...], g_ref[...], m_ref[...], v_ref[...])

    spec = pl.BlockSpec((tr, c), lambda i: (i, 0))
    return pl.pallas_call(
        body, grid=(r // tr,), in_specs=[spec] * 4, out_specs=[spec] * 3,
        out_shape=[jax.ShapeDtypeStruct((r, c), F32)] * 3, name=name, compiler_params=_params(("parallel",)),
    )(w, g, m, v)


def _place():
    return lax.axis_index("x"), lax.axis_index("y"), lax.axis_index("c")


def allgather8(blk, *, name):
    def body(x_ref, out_ref, send_sems, recv_sems, local_sem):
        x, y, c = _place()
        me = 4 * x + 2 * y + c
        mine = pltpu.make_async_copy(x_ref, out_ref.at[me], local_sem)
        mine.start()
        sent = []
        for k in range(1, N_DEV):
            to = (x ^ ((k >> 2) & 1), y ^ ((k >> 1) & 1), c ^ (k & 1))
            cp = pltpu.make_async_remote_copy(src_ref=x_ref, dst_ref=out_ref.at[me], send_sem=send_sems.at[k - 1],
                                              recv_sem=recv_sems.at[k - 1], device_id=to, device_id_type=MESH)
            cp.start()
            sent.append(cp)
        for k in range(1, N_DEV):
            frm = me ^ k
            pltpu.make_async_remote_copy(src_ref=x_ref, dst_ref=out_ref.at[frm], send_sem=send_sems.at[k - 1],
                                         recv_sem=recv_sems.at[k - 1], device_id=(x, y, c), device_id_type=MESH).wait_recv()
        for cp in sent:
            cp.wait_send()
        mine.wait()

    vm = pl.BlockSpec(memory_space=pltpu.VMEM)
    return pl.pallas_call(
        body, in_specs=[vm], out_specs=vm, out_shape=jax.ShapeDtypeStruct((N_DEV,) + blk.shape, blk.dtype), name=name,
        scratch_shapes=[pltpu.SemaphoreType.DMA((N_DEV - 1,)), pltpu.SemaphoreType.DMA((N_DEV - 1,)), pltpu.SemaphoreType.DMA],
    )(blk)


def _other_chips(x, y):
    return [(1 - x, y), (x, 1 - y), (1 - x, 1 - y)]


def _hbm_call(body, ins, out_shapes, n_sems, name):
    hbm = pl.BlockSpec(memory_space=pl.ANY)
    return pl.pallas_call(
        body, in_specs=[hbm] * len(ins), out_specs=[hbm] * len(out_shapes), out_shape=out_shapes, name=name,
        scratch_shapes=[pltpu.SemaphoreType.DMA((n_sems,)), pltpu.SemaphoreType.DMA((n_sems,))],
    )(*ins)


def _half_rows(c, rh):
    return pl.ds(pl.multiple_of(c * rh, BF16_ROWS), rh)


def gather_units(units, *, name):
    nu = len(units)
    shapes = []
    for arr, layer_major in units:
        r, cols = arr.shape
        shapes.append(jax.ShapeDtypeStruct((2, N_CHIPS, r // 2, cols) if layer_major else (N_CHIPS, r, cols), arr.dtype))

    def body(*refs):
        in_refs, out_refs, send_sems, recv_sems = refs[:nu], refs[nu:2 * nu], refs[2 * nu], refs[2 * nu + 1]
        x, y, c = _place()
        me_chip = 2 * x + y
        sib = (x, y, 1 - c)
        chips = _other_chips(x, y)

        def copy(k, src, dst, to):
            return pltpu.make_async_remote_copy(src_ref=src, dst_ref=dst, send_sem=send_sems.at[k], recv_sem=recv_sems.at[k],
                                                device_id=to, device_id_type=MESH)

        first, passed, landing = [], [], []
        for u, (arr, layer_major) in enumerate(units):
            rh = arr.shape[0] // 2
            out_ref = out_refs[u]
            slot = (lambda chip, half, o=out_ref: o.at[half, chip]) if layer_major else \
                   (lambda chip, half, o=out_ref, rh=rh: o.at[chip, _half_rows(half, rh), :])
            my_half = in_refs[u].at[_half_rows(c, rh), :]
            for j, (cx, cy) in enumerate(chips):
                k = 6 * u + j
                first.append(copy(k, my_half, slot(me_chip, c), (cx, cy, c)))
                passed.append(copy(k + 3, slot(2 * cx + cy, c), slot(2 * cx + cy, c), sib))
                landing.append((copy(k, my_half, slot(2 * cx + cy, c), sib), copy(k + 3, my_half, slot(2 * cx + cy, 1 - c), sib)))
        for cp in first:
            cp.start()
        for (over_ici, _), fwd in zip(landing, passed):
            over_ici.wait_recv()
            fwd.start()
        for _, from_sibling in landing:
            from_sibling.wait_recv()
        for cp in first + passed:
            cp.wait_send()

    return _hbm_call(body, [a for a, _ in units], shapes, 6 * nu, name)


HBM_SPEC = pl.BlockSpec(memory_space=pltpu.HBM)
SEM_SPEC = pl.BlockSpec(memory_space=pltpu.SEMAPHORE)
ORDERED_EFFECT = pltpu.SideEffectType.DATAFLOW_SIDE_EFFECTING


def _split_start(body, srcs, land_shapes, after, *, name):
    nu = len(srcs)
    lands = [lax.empty(s.shape, s.dtype) for s in land_shapes]

    def whole(*refs):
        body(refs[:nu], refs[nu:2 * nu], refs[2 * nu + 1], refs[2 * nu + 2])
        refs[-1][...] = jnp.zeros((SUBLANES, LANES), F32)

    hbm = lambda a: pltpu.with_memory_space_constraint(a, pltpu.HBM)
    sems = pltpu.SemaphoreType.DMA((nu,))
    res = pl.pallas_call(
        whole, name=name, in_specs=[HBM_SPEC] * (2 * nu) + [pl.BlockSpec(memory_space=pl.ANY)],
        out_shape=[sems, sems] + [pltpu.HBM(a.shape, a.dtype) for a in srcs] + [pltpu.HBM(s.shape, s.dtype) for s in land_shapes]
        + [jax.ShapeDtypeStruct((SUBLANES, LANES), F32)],
        out_specs=[SEM_SPEC, SEM_SPEC] + [HBM_SPEC] * (2 * nu) + [pl.BlockSpec(memory_space=pltpu.VMEM)],
        input_output_aliases={q: 2 + q for q in range(2 * nu)},
        compiler_params=pltpu.CompilerParams(has_side_effects=ORDERED_EFFECT),
    )(*[hbm(a) for a in srcs], *[hbm(a) for a in lands], after)
    return res[0], res[1], res[2:2 + nu], res[2 + nu:2 + 2 * nu], res[-1]


def _split_wait(pending, moved, after, *, name):
    send_sems, recv_sems, srcs, lands, _ = pending
    nu = len(srcs)

    def body(*refs):
        land_refs, ssem, rsem = refs[nu:2 * nu], refs[2 * nu], refs[2 * nu + 1]
        x, y, c = _place()
        for u in range(nu):
            size = moved(land_refs[u])
            cp = pltpu.make_async_remote_copy(src_ref=size, dst_ref=size, send_sem=ssem.at[u], recv_sem=rsem.at[u],
                                              device_id=(x, y, c), device_id_type=MESH)
            cp.wait_send()
            cp.wait_recv()

    res = pl.pallas_call(
        body, name=name, in_specs=[HBM_SPEC] * (2 * nu) + [SEM_SPEC, SEM_SPEC, pl.BlockSpec(memory_space=pl.ANY)],
        out_shape=[pltpu.HBM(a.shape, a.dtype) for a in srcs] + [pltpu.HBM(a.shape, a.dtype) for a in lands],
        out_specs=[HBM_SPEC] * (2 * nu), input_output_aliases={q: q for q in range(2 * nu)},
        compiler_params=pltpu.CompilerParams(has_side_effects=ORDERED_EFFECT),
    )(*srcs, *lands, send_sems, recv_sems, after)
    return res[nu:]


def gather_start(shards, after, *, name):
    def body(src_refs, land_refs, send_sems, recv_sems):
        x, y, c = _place()
        for u, shard in enumerate(shards):
            rows = _half_rows(c, shard.shape[0] // 2)
            for cx, cy in _other_chips(x, y):
                for core in range(2):
                    pltpu.make_async_remote_copy(src_ref=src_refs[u].at[rows, :], dst_ref=land_refs[u].at[2 * x + y, rows, :],
                                                 send_sem=send_sems.at[u], recv_sem=recv_sems.at[u], device_id=(cx, cy, core),
                                                 device_id_type=MESH).start()

    return _split_start(body, shards, [jax.ShapeDtypeStruct((N_CHIPS,) + s.shape, s.dtype) for s in shards], after, name=name)


def gather_wait(pending, after, *, name):
    return _split_wait(pending, lambda land: land.at[pl.ds(0, N_CHIPS - 1)], after, name=name)


def scatter_start(pairs, *, name):
    def body(src_refs, land_refs, send_sems, recv_sems):
        x, y, c = _place()
        for u in range(len(pairs)):
            for j, (cx, cy) in enumerate(_other_chips(x, y)):
                pltpu.make_async_remote_copy(src_ref=src_refs[u].at[2 * cx + cy], dst_ref=land_refs[u].at[j], send_sem=send_sems.at[u],
                                             recv_sem=recv_sems.at[u], device_id=(cx, cy, c), device_id_type=MESH).start()

    return _split_start(body, pairs, [jax.ShapeDtypeStruct((N_CHIPS - 1,) + p.shape[1:], p.dtype) for p in pairs], pairs[0], name=name)


def scatter_wait(pending, after, *, name):
    return _split_wait(pending, lambda land: land, after, name=name)


def swap_units(units, *, name):
    nu = len(units)

    def body(*refs):
        g_refs, out_refs, send_sems, recv_sems = refs[:nu], refs[nu:2 * nu], refs[2 * nu], refs[2 * nu + 1]
        x, y, c = _place()
        cps = [pltpu.make_async_remote_copy(src_ref=g_refs[u].at[:, _half_rows(1 - c, units[u].shape[1] // 2), :], dst_ref=out_refs[u],
                                            send_sem=send_sems.at[u], recv_sem=recv_sems.at[u], device_id=(x, y, 1 - c),
                                            device_id_type=MESH) for u in range(nu)]
        for cp in cps:
            cp.start()
        for cp in cps:
            cp.wait()

    shapes = [jax.ShapeDtypeStruct((N_CHIPS, g.shape[1] // 2, g.shape[2]), g.dtype) for g in units]
    return _hbm_call(body, units, shapes, nu, name)


def join_units(units, *, name):
    nu = len(units)

    def body(*refs):
        h_refs, out_refs, send_sems, recv_sems = refs[:nu], refs[nu:2 * nu], refs[2 * nu], refs[2 * nu + 1]
        x, y, c = _place()
        cps = [pltpu.make_async_remote_copy(src_ref=h_refs[u], dst_ref=out_refs[u], send_sem=send_sems.at[u], recv_sem=recv_sems.at[u],
                                            device_id=(x, y, 1 - c), device_id_type=MESH) for u in range(nu)]
        for cp in cps:
            cp.start()
        for cp in cps:
            cp.wait()

    return _hbm_call(body, units, [jax.ShapeDtypeStruct(h.shape, h.dtype) for h in units], nu, name)


def _half_tile(rh):
    tr = rh
    for cand in range(BF16_ROWS, min(rh, 512) + 1, BF16_ROWS):
        if rh % cand == 0:
            tr = cand
    return tr


def pair_add(g, sib, *, name):
    nc, rh, cols = sib.shape
    tr = _half_tile(rh)
    nbh = rh // tr

    def body(g0_ref, g1_ref, s_ref, o_ref):
        mine = jnp.where(lax.axis_index("c") == 0, g0_ref[...], g1_ref[...])
        o_ref[...] = (mine.astype(F32) + s_ref[...].astype(F32)).astype(o_ref.dtype)

    blk = lambda off: pl.BlockSpec((None, tr, cols), lambda j, i: (j, off + i, 0))
    return pl.pallas_call(
        body, grid=(nc, nbh), in_specs=[blk(0), blk(nbh), blk(0)], out_specs=blk(0),
        out_shape=jax.ShapeDtypeStruct(sib.shape, BF16), name=name, compiler_params=_params(("parallel", "parallel")),
    )(g, g, sib)


def chips_add(pair, landed, *, name):
    nc, rh, cols = pair.shape
    tr = _half_tile(rh)

    def body(*refs):
        chip = 2 * lax.axis_index("x") + lax.axis_index("y")
        acc = refs[0][...]
        for j in range(1, nc):
            acc = jnp.where(chip == j, refs[j][...], acc)
        acc = acc.astype(F32)
        for r in refs[nc:-1]:
            acc = acc + r[...].astype(F32)
        refs[-1][...] = acc

    part = lambda q: pl.BlockSpec((None, tr, cols), lambda i, q=q: (q, i, 0))
    return pl.pallas_call(
        body, grid=(rh // tr,), in_specs=[part(q) for q in range(nc)] + [part(q) for q in range(landed.shape[0])],
        out_specs=pl.BlockSpec((tr, cols), lambda i: (i, 0)),
        out_shape=jax.ShapeDtypeStruct((rh, cols), F32), name=name, compiler_params=_params(("parallel",)),
    )(*[pair] * nc, *[landed] * landed.shape[0])


def sum8(g, *, name):
    def body(g_ref, o_ref):
        acc = g_ref[0]
        for d in range(1, N_DEV):
            acc = acc + g_ref[d]
        o_ref[...] = acc

    return pl.pallas_call(body, out_shape=jax.ShapeDtypeStruct(g.shape[1:], F32), name=name)(g)


def _dup_halves(a):
    t = a.shape[0]
    a = a.reshape(t, N_KV_B, HEAD_DIM_B)
    return jnp.concatenate([a, a], axis=-1).reshape(t, N_KV_B * LANES)


def _undup(a):
    t = a.shape[0]
    return a.reshape(t, N_KV_B, LANES)[:, :, :HEAD_DIM_B].reshape(t, N_KV_B * HEAD_DIM_B)


def _lane_pad(v, offset=0):
    return jnp.zeros((1, LANES), F32).at[0, offset:offset + v.shape[0]].set(v)


SHARD_UP = 2 * D_FF // N_CHIPS
SHARD_BIN = (N_HEADS_B + 2 * N_KV_B) * HEAD_DIM_B // N_CHIPS
SHARD_PROJ = D_MODEL // N_CHIPS


def local_step(x, p, tgt, sm, weight, on_grads):
    t = x.shape[0]
    rtm = min(256, t)
    hk = N_HEADS_A * HEAD_DIM_A
    qd_b = N_HEADS_B * HEAD_DIM_B
    kd_b = N_KV_B * HEAD_DIM_B
    gs = {}
    norm = lambda h, w, nm: tile_map(_f_norm, [(h, D_MODEL, 0)], [w], [(D_MODEL, BF16)], tm=rtm, ncol=1, name=nm)[0]

    def norm_bwd(h, w, dy, add, nm):
        (dh,), (dw,) = tile_vjp(_f_norm, [(h, D_MODEL, 0)], [w], [(dy, D_MODEL, 0)], n_diff=1, tm=rtm, ncol=1, name=nm,
                                add=(add, D_MODEL, 0))
        return dh, dw

    spec = pl.BlockSpec
    mtm = _tile(D_MODEL, MM_TM_CAP)
    p_bf = p.astype(BF16)
    alog_p = _lane_pad(sm["a_log"][0], N_HEADS_A)
    dtb_p = _lane_pad(sm["a_dt_bias"][0], N_HEADS_A)
    sinks_p = _lane_pad(sm["b_sinks"][0])
    nw = lambda name, i: sm[name][i:i + 1]
    by_chip = lambda kdim, ns: dict(tn=ns, tk=kdim, b_spec=spec((None, kdim, ns), lambda r, j, kk: (j, kk, 0)))
    by_chip_t = lambda ndim, ns: dict(n=ndim, tn=ndim, tk=ns, b_spec=spec((None, ndim, ns), lambda r, j, kk: (kk, j, 0)))
    cache = {}

    def wgt(name, i, after):
        if (name, i) not in cache:
            cache[name, i] = weight(name, i, after)
        return cache[name, i]

    saved = []
    h = x
    hn_next = norm(h, nw("norm_mix", 0), "norm_mix0")
    for i in range(DEPTH):
        s = {"h0": h, "hn": hn_next}
        if i % 2 == 0:
            s["pm"] = mm(s["hn"], wgt("a_main", i, h), name="a_in_main")
            s["pt"] = mm(s["hn"], wgt("a_tail", i, h), name="a_in_tail")
            s["c"] = conv_fwd(s["pm"], wgt("a_conv", i, h), name="a_conv")
            s["bg"] = tile_map(_f_betag, [(s["pt"], LANES, 0)], [alog_p, dtb_p], [(LANES, F32)], tm=rtm, ncol=1, name="a_betag")[0]
            s["prep"] = delta_prep(s["c"], s["bg"], name="a_prep")
            s["o"], s["s_in"] = delta_scan(*s["prep"], name="a_scan")
            s["on"] = gnorm_fwd(s["o"], s["pm"], sm["a_norm"], name="a_gnorm")
            h, s["hf"] = mm(s["on"], wgt("a_w_out", i, s["on"]), add=h, norm_w=nw("norm_ffn", i), name="a_out")
        else:
            s["pb"] = mm(s["hn"], wgt("b_w_in", i, s["hn"]), name="b_in", out_dtype=BF16, n=N_CHIPS * SHARD_BIN,
                         **by_chip(D_MODEL, SHARD_BIN))
            s["kd"], s["vd"] = _dup_halves(s["pb"][:, qd_b:qd_b + kd_b]), _dup_halves(s["pb"][:, qd_b + kd_b:])
            s["ao"] = swa_fwd(s["pb"], s["kd"], s["vd"], sinks_p, name="b_att")
            h, s["hf"] = mm(s["ao"], wgt("b_w_out", i, s["ao"]), add=h, norm_w=nw("norm_ffn", i), name="b_out")
        s["h1"] = h
        s["u"] = mm(s["hf"], wgt("f_w_up", i, s["hf"]), name=f"f_up{i}", out_dtype=BF16, n=2 * D_FF, **by_chip(D_MODEL, SHARD_UP))
        s["act"] = conv_act_fwd(s["u"], wgt("f_conv", i, s["hf"]), name=f"f_conv_act{i}")
        h, s["hp"] = mm(s["act"], wgt("f_w_down", i, s["act"]), add=h, norm_w=nw("norm_ple", i), name=f"f_down{i}")
        s["h2"] = h
        s["gl"] = mm(s["hp"], wgt("ple_w_gate", i, s["hp"]), name=f"ple_gate{i}")
        s["pe"] = mm(p_bf[i], wgt("ple_w_proj", i, s["hp"]), name=f"ple_proj{i}", n=D_MODEL, **by_chip(PLE_DIM, SHARD_PROJ))
        rows3 = [(h, D_MODEL, 0), (s["gl"], D_MODEL, 0), (s["pe"], D_MODEL, 0)]
        if i + 1 < DEPTH:
            def mix_norm(hv, g, e, wn):
                hn = hv + _f_ple(g, e)
                return hn, _f_norm(hn, wn)
            h, hn_next = tile_map(mix_norm, rows3, [nw("norm_mix", i + 1)], [(D_MODEL, F32), (D_MODEL, BF16)], tm=rtm, ncol=1,
                                  name=f"ple_mix{i}")
        else:
            h = tile_map(lambda hv, g, e: hv + _f_ple(g, e), rows3, [], [(D_MODEL, F32)], tm=rtm, ncol=1, name=f"ple_mix{i}")[0]
        saved.append(s)

    dh, gnf, loss = loss_head(h, tgt, sm["norm_final"][None, :], name="loss_head")
    gs["norm_final"] = gnf[0]

    g_mix, g_ffn, g_ple, g_conv = ([None] * DEPTH for _ in range(4))
    zero = jnp.zeros((1, 1), F32)
    for i in reversed(range(DEPTH)):
        s, gw = saved[i], {}
        by_rows = lambda g: g.reshape(N_CHIPS, g.shape[0] // N_CHIPS, g.shape[1])
        (dgl, dpe), _ = tile_vjp(_f_ple, [(s["gl"], D_MODEL, 0), (s["pe"], D_MODEL, 0)], [], [(dh, D_MODEL, 0)], n_diff=2,
                                 tm=rtm, ncol=1, name=f"ple_mix_bwd{i}", grad_dtypes=[BF16, BF16])
        gw["ple_w_proj"] = mm(p_bf[i], dpe, ta=True, name=f"ple_proj_dw{i}", out_dtype=BF16, tn=SHARD_PROJ,
                              o_shape=(N_CHIPS, PLE_DIM, SHARD_PROJ), o_spec=spec((None, PLE_DIM, SHARD_PROJ), lambda r, j, kk: (j, r, 0)))
        gw["ple_w_gate"] = by_rows(mm(s["hp"], dgl, ta=True, name=f"ple_gate_dw{i}", out_dtype=BF16))
        dhp = mm(dgl, cache["ple_w_gate", i], tb=True, name=f"ple_gate_dx{i}")
        dh, g_ple[i] = norm_bwd(s["h2"], nw("norm_ple", i) + zero, dhp, dh, f"norm_ple_bwd{i}")

        dact = mm(dh, cache["f_w_down", i], tb=True, name=f"f_down_dx{i}")
        gw["f_w_down"] = by_rows(mm(s["act"], dh, ta=True, name=f"f_down_dw{i}", out_dtype=BF16, tm_cap=D_FF // 2))
        du_halves = conv_act_bwd(s["u"], dact, cache["f_conv", i], name=f"f_conv_act_bwd{i}")
        g_conv[i] = jnp.concatenate(du_halves[2:], axis=1)
        dhf = g_up = None
        for half, du in enumerate(du_halves[:2]):
            c0 = half * (N_CHIPS // 2)
            g_up = mm(s["hf"], du, ta=True, name=f"f_up_dw{i}_{half}", out_dtype=BF16, tn=SHARD_UP, into=g_up,
                      o_shape=(N_CHIPS, D_MODEL, SHARD_UP), o_spec=spec((None, mtm, SHARD_UP), lambda r, j, kk, c0=c0: (c0 + j, r, 0)))
            dhf = mm(du, cache["f_w_up", i], tb=True, name=f"f_up_dx{i}_{half}", n=D_MODEL, tn=D_MODEL, tk=SHARD_UP, add=dhf,
                     b_spec=spec((None, D_MODEL, SHARD_UP), lambda r, j, kk, c0=c0: (c0 + kk, j, 0)))
        gw["f_w_up"] = g_up
        dh, g_ffn[i] = norm_bwd(s["h1"], nw("norm_ffn", i), dhf, dh, f"norm_ffn_bwd{i}")
        token, gw = on_grads(i, "ffn", gw), {}
        w_out = cache["a_w_out" if i % 2 == 0 else "b_w_out", i]
        if token is not None:
            w_out = w_out + token[:1, :1].astype(BF16)

        if i % 2 == 0:
            don = mm(dh, w_out, tb=True, name="a_out_dx")
            gw["a_w_out"] = by_rows(mm(s["on"], dh, ta=True, name="a_out_dw", out_dtype=BF16))
            do, dz, gs["a_norm"] = gnorm_bwd(s["o"], s["pm"], sm["a_norm"], don, name="a_gnorm_bwd")
            dprep = delta_scan_bwd(do, *s["prep"], s["s_in"], name="a_scan_bwd")
            dc, dbg = delta_prep_bwd(s["c"], s["bg"], dprep, name="a_prep_bwd")
            (dpt,), (galog, gdtb) = tile_vjp(_f_betag, [(s["pt"], LANES, 0)], [alog_p, dtb_p], [(dbg, LANES, 0)], n_diff=1,
                                             tm=rtm, ncol=1, name="a_betag_bwd", grad_dtypes=[BF16])
            gs["a_log"] = galog[:, N_HEADS_A:2 * N_HEADS_A]
            gs["a_dt_bias"] = gdtb[:, N_HEADS_A:2 * N_HEADS_A]
            dqkv, gs["a_conv"] = conv_bwd(dc, s["pm"], cache["a_conv", i], name="a_conv_bwd")
            dpm = jnp.concatenate([dqkv, dz], axis=1)
            dhn = mm(dpm, cache["a_main", i], tb=True, name="a_in_main_dx")
            dhn = mm(dpt, cache["a_tail", i], tb=True, add=dhn, name="a_in_tail_dx")
            g_main = mm(s["hn"], dpm, ta=True, name="a_in_main_dw", out_dtype=BF16)
            g_tail = mm(s["hn"], dpt, ta=True, name="a_in_tail_dw", out_dtype=BF16)
            g_in = jnp.concatenate([g_main, g_tail[:, :2 * N_HEADS_A]], axis=1)
            gw["a_w_in"] = g_in.reshape(D_MODEL, N_CHIPS, g_in.shape[1] // N_CHIPS).transpose(1, 0, 2)
        else:
            dao = mm(dh, w_out, tb=True, name="b_out_dx")
            gw["b_w_out"] = by_rows(mm(s["ao"], dh, ta=True, name="b_out_dw", out_dtype=BF16))
            dq, dkd, dvd, gsk = swa_bwd(dao, s["pb"], s["kd"], s["vd"], sinks_p, name="b_att_bwd")
            gs["b_sinks"] = gsk[:, :N_HEADS_B]
            dpb = jnp.concatenate([dq, _undup(dkd), _undup(dvd)], axis=1)
            dhn = mm(dpb, cache["b_w_in", i], tb=True, name="b_in_dx", **by_chip_t(D_MODEL, SHARD_BIN))
            gw["b_w_in"] = mm(s["hn"], dpb, ta=True, name="b_in_dw", out_dtype=BF16, tn=SHARD_BIN,
                              o_shape=(N_CHIPS, D_MODEL, SHARD_BIN), o_spec=spec((None, mtm, SHARD_BIN), lambda r, j, kk: (j, r, 0)))
        dh, g_mix[i] = norm_bwd(s["h0"], nw("norm_mix", i), dhn, dh, f"norm_mix_bwd{i}")
        token = on_grads(i, "mix", gw)
        if token is not None:
            zero = token[:1, :1]

    gs["norm_mix"], gs["norm_ffn"], gs["norm_ple"] = (jnp.concatenate(g, axis=0) for g in (g_mix, g_ffn, g_ple))
    gs["f_conv"] = jnp.stack(g_conv)
    return loss, dh, gs


BIG = ["a_w_in", "a_w_out", "b_w_in", "b_w_out", "f_w_up", "f_w_down", "ple_w_proj", "ple_w_gate"]
LAYERED = {"f_w_up", "f_w_down", "ple_w_proj", "ple_w_gate"}
BY_CHIP = {"b_w_in", "f_w_up", "ple_w_proj"}
LAYER_UNITS = [[("a_w_in", 0), ("a_w_out", 0)] + [(n, 0) for n in sorted(LAYERED)],
               [("b_w_in", 1), ("b_w_out", 1)] + [(n, 1) for n in sorted(LAYERED)]]
CONVS = ["a_conv", "f_conv"]
SMALL = ["norm_mix", "norm_ffn", "norm_ple", "norm_final", "a_log", "a_dt_bias", "a_norm", "b_sinks"]
SMALL_ROWS = 8
CONV_ROWS = 16
CONV_GRAD_ROWS = 48


def _pack_rows(arrs, rows, dtype):
    flat = jnp.concatenate([a.reshape(-1).astype(dtype) for a in arrs])
    return jnp.pad(flat, (0, rows * PACK_COLS - flat.shape[0])).reshape(rows, PACK_COLS)


def _unpack(flat, shapes):
    out, off = [], 0
    for shp in shapes:
        n = math.prod(shp)
        out.append(flat[off:off + n].reshape(shp))
        off += n
    return out


def _pack_small(d, loss=None):
    tail = jnp.concatenate([d["a_log"].reshape(-1), d["a_dt_bias"].reshape(-1), d["a_norm"].reshape(-1), d["b_sinks"].reshape(-1)])
    if loss is not None:
        tail = jnp.concatenate([tail, loss.reshape(-1)[:1]])
    tail = jnp.pad(tail, (0, PACK_COLS - tail.shape[0]))
    return jnp.concatenate([d["norm_mix"], d["norm_ffn"], d["norm_ple"], d["norm_final"][None, :], tail[None, :]], axis=0)


def _unpack_small(a, like):
    out = {"norm_mix": a[0:2], "norm_ffn": a[2:4], "norm_ple": a[4:6], "norm_final": a[6]}
    off = 0
    for nm in ("a_log", "a_dt_bias", "a_norm", "b_sinks"):
        n = like[nm].size
        out[nm] = a[7, off:off + n].reshape(like[nm].shape)
        off += n
    return out, a[7, off]


def _as2d(a):
    return a.reshape(-1, a.shape[-1])


def kernel(x, p, norm_mix, norm_ffn, norm_ple, norm_final, a_w_in, a_conv, a_log, a_dt_bias, a_norm, a_w_out, b_w_in, b_sinks, b_w_out, f_w_up, f_conv, f_w_down, ple_w_proj, ple_w_gate, loss_target, m_norm_mix, m_norm_ffn, m_norm_ple, m_norm_final, m_a_w_in, m_a_conv, m_a_log, m_a_dt_bias, m_a_norm, m_a_w_out, m_b_w_in, m_b_sinks, m_b_w_out, m_f_w_up, m_f_conv, m_f_w_down, m_ple_w_proj, m_ple_w_gate, v_norm_mix, v_norm_ffn, v_norm_ple, v_norm_final, v_a_w_in, v_a_conv, v_a_log, v_a_dt_bias, v_a_norm, v_a_w_out, v_b_w_in, v_b_sinks, v_b_w_out, v_f_w_up, v_f_conv, v_f_w_down, v_ple_w_proj, v_ple_w_gate):
    w = dict(norm_mix=norm_mix, norm_ffn=norm_ffn, norm_ple=norm_ple, norm_final=norm_final, a_w_in=a_w_in, a_conv=a_conv,
             a_log=a_log, a_dt_bias=a_dt_bias, a_norm=a_norm, a_w_out=a_w_out, b_w_in=b_w_in, b_sinks=b_sinks, b_w_out=b_w_out,
             f_w_up=f_w_up, f_conv=f_conv, f_w_down=f_w_down, ple_w_proj=ple_w_proj, ple_w_gate=ple_w_gate)
    m = dict(norm_mix=m_norm_mix, norm_ffn=m_norm_ffn, norm_ple=m_norm_ple, norm_final=m_norm_final, a_w_in=m_a_w_in,
             a_conv=m_a_conv, a_log=m_a_log, a_dt_bias=m_a_dt_bias, a_norm=m_a_norm, a_w_out=m_a_w_out, b_w_in=m_b_w_in,
             b_sinks=m_b_sinks, b_w_out=m_b_w_out, f_w_up=m_f_w_up, f_conv=m_f_conv, f_w_down=m_f_w_down,
             ple_w_proj=m_ple_w_proj, ple_w_gate=m_ple_w_gate)
    v = dict(norm_mix=v_norm_mix, norm_ffn=v_norm_ffn, norm_ple=v_norm_ple, norm_final=v_norm_final, a_w_in=v_a_w_in,
             a_conv=v_a_conv, a_log=v_a_log, a_dt_bias=v_a_dt_bias, a_norm=v_a_norm, a_w_out=v_a_w_out, b_w_in=v_b_w_in,
             b_sinks=v_b_sinks, b_w_out=v_b_w_out, f_w_up=v_f_w_up, f_conv=v_f_conv, f_w_down=v_f_w_down,
             ple_w_proj=v_ple_w_proj, ple_w_gate=v_ple_w_gate)
    xc, yc, cc = _place()
    my_chip = 2 * xc + yc

    shard = {(n, i): w[n][i if n in LAYERED else 0].astype(BF16) for n, i in LAYER_UNITS[0] + LAYER_UNITS[1]}
    first = shard["a_w_in", 0]
    (ga,) = gather_units([(first, False)], name="gather_first")
    ga = lax.dynamic_update_index_in_dim(ga, first, my_chip, 0)
    a_in = jnp.concatenate([ga[j] for j in range(N_CHIPS)], axis=1)
    n_main = 4 * N_HEADS_A * HEAD_DIM_A
    conv_shapes = [w[n].shape for n in CONVS]
    convs = allgather8(_pack_rows([w[n] for n in CONVS], CONV_ROWS, F32), name="gather_convs")
    conv_parts = [_unpack(convs[2 * j].reshape(-1), conv_shapes) for j in range(N_CHIPS)]
    a_conv_full, f_conv_full = (jnp.concatenate([conv_parts[j][q] for j in range(N_CHIPS)], axis=2) for q in range(2))
    ready = {("a_main", 0): a_in[:, :n_main], ("a_tail", 0): jnp.pad(a_in[:, n_main:], ((0, 0), (0, LANES - 2 * N_HEADS_A))),
             ("a_conv", 0): a_conv_full[0], ("f_conv", 0): f_conv_full[0], ("f_conv", 1): f_conv_full[1]}
    later = [[k for k in units if k != ("a_w_in", 0)] for units in LAYER_UNITS]
    pending, after = [], ga
    for layer, keys in enumerate(later):
        pending.append(gather_start([shard[k] for k in keys], after, name=f"gather_start{layer}"))
        after = pending[-1][4]
    sm = {n: w[n] for n in SMALL}
    sm["norm_mix"] = sm["norm_mix"] + after[:1, :1]

    def weight(name, layer, act):
        if (name, layer) not in ready:
            landed = gather_wait(pending[layer], act, name=f"gather_wait{layer}")
            for k, g in zip(later[layer], landed):
                g = lax.dynamic_update_index_in_dim(g, shard[k], my_chip, 0)
                ready[k] = g if k[0] in BY_CHIP else g.reshape(N_CHIPS * g.shape[1], g.shape[2])
        return ready[name, layer]

    pairs, scattered, started = {}, {}, []

    def on_grads(layer, part, gw):
        keys = [k for k in LAYER_UNITS[layer] if (k[0] in LAYERED) == (part == "ffn")]
        from_sib = swap_units([gw[n] for n, _ in keys], name=f"rs_swap_{part}{layer}")
        for (n, _), sib in zip(keys, from_sib):
            pairs[n, layer] = pair_add(gw[n], sib, name=f"rs_add_pair_{n}{layer}")
        started.append((keys, scatter_start([pairs[k] for k in keys], name=f"rs_scatter_start_{part}{layer}"), f"{part}{layer}"))
        return started[-1][1][4]

    loss, grad_x, gs = local_step(x[0], p[:, 0], loss_target[0], sm, weight, on_grads)

    grads, delta, new_m, new_v, g_unit = {}, {}, {}, {}, {}

    def finish(keys, tag):
        halves = [chips_add(pairs[k], scattered[k], name=f"rs_add_chips_{k[0]}{k[1]}") for k in keys]
        g_unit.update(zip(keys, zip(halves, join_units(halves, name=f"rs_join_{tag}"))))
        for n in BIG:
            mine = [(n, i) for i in range(DEPTH) if (n, i) in LAYER_UNITS[i]]
            if n not in delta and all(k in g_unit for k in mine):
                g_layers = [g_unit[k] for k in mine]
                shape3 = (len(g_layers), 2 * g_layers[0][0].shape[0], g_layers[0][0].shape[1])
                res = adamw_layers(w[n].reshape(shape3), g_layers, m[n].reshape(shape3), v[n].reshape(shape3), name=f"adamw_{n}")
                delta[n], new_m[n], new_v[n], grads[n] = (r.reshape(w[n].shape) for r in res)

    last_keys, last_pending, last_tag = started[-1]
    for keys, pend, tag in started[:-1]:
        scattered.update(zip(keys, scatter_wait(pend, last_pending[4], name=f"rs_scatter_wait_{tag}")))
    finish([k for keys, _, _ in started[:-1] for k in keys], "first")

    conv_grads = _pack_rows([gs[n] for n in CONVS], CONV_GRAD_ROWS, F32)
    small_sum = sum8(allgather8(jnp.concatenate([_pack_small(gs, loss), conv_grads]), name="gather_small"), name="sum_small")
    g_sm, loss_sum = _unpack_small(small_sum[:SMALL_ROWS], sm)

    scattered.update(zip(last_keys, scatter_wait(last_pending, small_sum, name=f"rs_scatter_wait_{last_tag}")))
    finish(last_keys, "last")

    for n, full in zip(CONVS, _unpack(small_sum[SMALL_ROWS:].reshape(-1), [gs[n].shape for n in CONVS])):
        g2 = _as2d(lax.dynamic_slice_in_dim(full, my_chip * w[n].shape[-1], w[n].shape[-1], axis=full.ndim - 1))
        d2, m2, v2 = adamw(_as2d(w[n]), g2, _as2d(m[n]), _as2d(v[n]), name=f"adamw_{n}")
        grads[n], delta[n], new_m[n], new_v[n] = (r.reshape(w[n].shape) for r in (g2, d2, m2, v2))
    pk = lambda d: _pack_small(d)
    d2, m2, v2 = adamw(pk(sm), pk(g_sm), pk({n: m[n] for n in SMALL}), pk({n: v[n] for n in SMALL}), name="adamw_small")
    for src, dst in ((d2, delta), (m2, new_m), (v2, new_v)):
        dst.update(_unpack_small(src, sm)[0])
    grads.update(g_sm)

    order = ["norm_mix", "norm_ffn", "norm_ple", "norm_final", "a_w_in", "a_conv", "a_log", "a_dt_bias", "a_norm", "a_w_out",
             "b_w_in", "b_sinks", "b_w_out", "f_w_up", "f_conv", "f_w_down", "ple_w_proj", "ple_w_gate"]
    return (loss_sum, grad_x[None], *[grads[n] for n in order], *[delta[n] for n in order],
            *[new_m[n] for n in order], *[new_v[n] for n in order])
```

```python
import functools
import math

import jax
import jax.numpy as jnp
from jax import lax
from jax.experimental import pallas as pl
from jax.experimental.pallas import tpu as pltpu

F32 = jnp.float32
BF16 = jnp.bfloat16
MESH = pl.DeviceIdType.MESH

D_MODEL = 1024
N_HEADS_A = 8
HEAD_DIM_A = 128
CONV_A = 4
N_HEADS_B = 16
N_KV_B = 4
HEAD_DIM_B = 64
WINDOW = 128
D_FF = 2816
FFN_CONV = 3
PLE_DIM = 256
EPS = 1e-6
DEPTH = 2

ADAM_LR = 0.001
ADAM_B1 = 0.9
ADAM_B2 = 0.999
ADAM_EPS = 1e-08
ADAM_WD = 0.01
ADAM_STEP = 10

LANES = 128
SUBLANES = 8
BF16_ROWS = 16
CHUNK = 128
VMEM_LIMIT = 56 * 1024 * 1024
NEG = -1e30
N_CHIPS = 4
N_DEV = 8
PACK_COLS = 1024


def _params(sem=None):
    return pltpu.CompilerParams(dimension_semantics=sem, vmem_limit_bytes=VMEM_LIMIT)


def _tile(dim, cap):
    if dim % LANES:
        return dim
    best = LANES
    for t in range(LANES, min(dim, cap) + 1, LANES):
        if dim % t == 0:
            best = t
    return best


def _dot(a, b, dims=(((1,), (0,)), ((), ())), precision=None):
    return lax.dot_general(a, b, dims, precision=precision, preferred_element_type=F32)


NN = (((1,), (0,)), ((), ()))
NT = (((1,), (1,)), ((), ()))
TN = (((0,), (0,)), ((), ()))


MM_TM_CAP = 1024
MM_TK_CAP_TOKENS = 2048


def mm(a, b, *, name, ta=False, tb=False, out_dtype=F32, add=None, norm_w=None, norm_grad=None, tm_cap=MM_TM_CAP, tn_cap=1408,
       tk_cap=1408, n=None, tn=None, tk=None, b_spec=None, o_spec=None, o_shape=None, into=None):
    m, k = (a.shape[1], a.shape[0]) if ta else a.shape
    if b_spec is None:
        n = b.shape[0] if tb else b.shape[1]
        assert (b.shape[1] if tb else b.shape[0]) == k, (a.shape, b.shape, ta, tb)
    tm, tn, tk = _tile(m, tm_cap), tn or _tile(n, tn_cap), tk or _tile(k, MM_TK_CAP_TOKENS if ta else tk_cap)
    assert n % tn == 0 and k % tk == 0, (n, tn, k, tk)
    nk = k // tk
    dims = (((0 if ta else 1,), (1 if tb else 0,)), ((), ()))
    has_add, has_norm, has_grad = add is not None, norm_w is not None, norm_grad is not None
    assert not (has_norm or has_grad) or (tn == n and o_spec is None), "the norm epilogues need whole rows"
    n_in = 2 + has_add + has_norm + 3 * has_grad + (into is not None)

    def body(*refs):
        a_ref, b_ref = refs[0], refs[1]
        add_ref = refs[2] if has_add else None
        o_ref = refs[n_in]
        part = _dot(a_ref[...].astype(BF16), b_ref[...].astype(BF16), dims)
        first = pl.program_id(0) == 0

        def finish(r):
            if has_add:
                r = r + add_ref[...].astype(F32)
            if has_grad:
                h_ref, w_ref, prev_ref = refs[2 + has_add:5 + has_add]
                _, vjp = jax.vjp(_f_norm, h_ref[...], w_ref[...])
                r, dw = vjp(r)
                r = r + prev_ref[...]

                @pl.when(first)
                def _():
                    refs[n_in + 1][...] = dw

                @pl.when(jnp.logical_not(first))
                def _():
                    refs[n_in + 1][...] += dw
            o_ref[...] = r.astype(o_ref.dtype)
            if has_norm:
                refs[n_in + 1][...] = _f_norm(r, refs[2 + has_add][...]).astype(BF16)

        if nk == 1:
            finish(part)
            return
        acc = refs[-1]
        kk = pl.program_id(2)

        @pl.when(kk == 0)
        def _():
            acc[...] = part

        @pl.when(kk > 0)
        def _():
            acc[...] += part

        @pl.when(kk == nk - 1)
        def _():
            finish(acc[...])

    a_spec = pl.BlockSpec((tk, tm), lambda i, j, kk: (kk, i)) if ta else pl.BlockSpec((tm, tk), lambda i, j, kk: (i, kk))
    if b_spec is None:
        b_spec = pl.BlockSpec((tn, tk), lambda i, j, kk: (j, kk)) if tb else pl.BlockSpec((tk, tn), lambda i, j, kk: (kk, j))
    plain_o = pl.BlockSpec((tm, tn), lambda i, j, kk: (i, j))
    if o_spec is None:
        o_spec, o_shape = plain_o, (m, n)
    in_specs = [a_spec, b_spec] + ([plain_o] if has_add else [])
    args = (a, b) + ((add,) if has_add else ())
    out_specs, out_shapes = o_spec, jax.ShapeDtypeStruct(tuple(o_shape), out_dtype)
    one_row = pl.BlockSpec((1, n), lambda i, j, kk: (0, 0))
    if has_norm:
        in_specs.append(one_row)
        args += (norm_w,)
        out_specs, out_shapes = [o_spec, plain_o], [out_shapes, jax.ShapeDtypeStruct((m, n), BF16)]
    if has_grad:
        assert not has_norm
        in_specs += [plain_o, one_row, plain_o]
        args += tuple(norm_grad)
        out_specs, out_shapes = [o_spec, one_row], [out_shapes, jax.ShapeDtypeStruct((1, n), F32)]
    aliases = {}
    if into is not None:
        assert into.shape == tuple(o_shape) and into.dtype == out_dtype, (into.shape, o_shape)
        in_specs.append(pl.BlockSpec(memory_space=pl.ANY))
        args += (into,)
        aliases = {n_in - 1: 0}
    return pl.pallas_call(
        body, grid=(m // tm, n // tn, nk), in_specs=in_specs, out_specs=out_specs,
        out_shape=out_shapes, name=name, input_output_aliases=aliases,
        scratch_shapes=[pltpu.VMEM((tm, tn), F32)] if nk > 1 else [],
        compiler_params=_params(("arbitrary" if has_grad else "parallel", "parallel", "arbitrary")),
    )(*args)


def _row_spec(tm, cw, coff):
    return pl.BlockSpec((tm, cw), lambda i, j: (i, j + coff))


def _full_spec(shape):
    return pl.BlockSpec(shape, lambda i, j: (0,) * len(shape))


def tile_map(fn, rows, params, outs, *, tm, ncol, name):
    t = rows[0][0].shape[0]
    nin = len(rows) + len(params)

    def body(*refs):
        res = fn(*[r[...] for r in refs[:nin]])
        res = res if isinstance(res, (tuple, list)) else (res,)
        for o_ref, r in zip(refs[nin:], res):
            o_ref[...] = r.astype(o_ref.dtype)

    in_specs = [_row_spec(tm, cw, coff) for (_, cw, coff) in rows] + [_full_spec(p.shape) for p in params]
    res = pl.pallas_call(
        body, grid=(t // tm, ncol), in_specs=in_specs,
        out_specs=[_row_spec(tm, cw, 0) for (cw, _) in outs],
        out_shape=[jax.ShapeDtypeStruct((t, cw * ncol), dt) for (cw, dt) in outs], name=name,
        compiler_params=_params(("parallel", "parallel")),
    )(*[r[0] for r in rows], *params)
    return res


def tile_vjp(fn, rows, params, cts, *, n_diff, tm, ncol, name, add=None, grad_dtypes=None):
    t = rows[0][0].shape[0]
    nr, npar, nct = len(rows), len(params), len(cts)
    has_add = add is not None

    def body(*refs):
        vals = [r[...] for r in refs[:nr + npar + nct + (1 if has_add else 0)]]
        diff, rest, pars = vals[:n_diff], vals[n_diff:nr], vals[nr:nr + npar]
        ctv = vals[nr + npar:nr + npar + nct]
        outs_ref = refs[nr + npar + nct + (1 if has_add else 0):]

        def f(*a):
            res = fn(*a[:n_diff], *rest, *a[n_diff:])
            return tuple(res) if isinstance(res, (tuple, list)) else (res,)

        primal, vjp = jax.vjp(f, *[d.astype(F32) for d in diff], *pars)
        grads = vjp(tuple(c.astype(o.dtype) for c, o in zip(ctv, primal)))
        for q in range(n_diff):
            g = grads[q]
            if has_add and q == 0:
                g = g + vals[-1]
            outs_ref[q][...] = g.astype(outs_ref[q].dtype)
        first = (pl.program_id(0) == 0) & (pl.program_id(1) == 0)
        for q in range(npar):
            o_ref, g = outs_ref[n_diff + q], grads[n_diff + q]

            @pl.when(first)
            def _(o_ref=o_ref, g=g):
                o_ref[...] = g

            @pl.when(jnp.logical_not(first))
            def _(o_ref=o_ref, g=g):
                o_ref[...] += g

    ins = list(rows) + [None] * 0
    in_specs = [_row_spec(tm, cw, coff) for (_, cw, coff) in rows] + [_full_spec(p.shape) for p in params]
    in_specs += [_row_spec(tm, cw, coff) for (_, cw, coff) in cts]
    args = [r[0] for r in rows] + list(params) + [c[0] for c in cts]
    if has_add:
        in_specs.append(_row_spec(tm, add[1], add[2]))
        args.append(add[0])
    out_specs = [_row_spec(tm, rows[q][1], 0) for q in range(n_diff)] + [_full_spec(p.shape) for p in params]
    grad_dtypes = grad_dtypes or [F32] * n_diff
    out_shape = [jax.ShapeDtypeStruct((t, rows[q][1] * ncol), grad_dtypes[q]) for q in range(n_diff)]
    out_shape += [jax.ShapeDtypeStruct(p.shape, F32) for p in params]
    del ins
    res = pl.pallas_call(
        body, grid=(t // tm, ncol), in_specs=in_specs, out_specs=out_specs, out_shape=out_shape, name=name,
        compiler_params=_params(("arbitrary", "arbitrary")),
    )(*args)
    return res[:n_diff], res[n_diff:]


def _silu(x):
    return x * jax.nn.sigmoid(x)


def _f_norm(h, w):
    return h * lax.rsqrt(jnp.mean(h * h, axis=-1, keepdims=True) + EPS) * w


def _f_gnorm(o, z, w):
    return _f_norm(o, w) * _silu(z)


def _f_act(gate, val):
    return _silu(gate) * val


def _f_ple(gl, pe):
    return jax.nn.sigmoid(gl) * pe


def _f_betag(pt, alog, dtb):
    lane = lax.broadcasted_iota(jnp.int32, (1, LANES), 1)
    z = pt + dtb
    softplus = jnp.maximum(z, 0.0) + jnp.log(1.0 + jnp.exp(-jnp.abs(z)))
    g = -jnp.exp(alog) * softplus
    return jnp.where(lane < N_HEADS_A, jax.nn.sigmoid(pt), jnp.where(lane < 2 * N_HEADS_A, g, 0.0))


CONV_TM = 256
CONV_CW = 1024


def _shift_down(x, prev, s, row):
    rp = jnp.tile(pltpu.roll(prev, s, 0), (x.shape[0] // SUBLANES, 1))
    return jnp.where(row < s, rp, pltpu.roll(x, s, 0))


def _shift_up(x, nxt, s, row):
    tm = x.shape[0]
    rn = jnp.tile(pltpu.roll(nxt, SUBLANES - s, 0), (tm // SUBLANES, 1))
    return jnp.where(row >= tm - s, rn, pltpu.roll(x, tm - s, 0))


def _conv_taps(x, prev, w_ref, cols, row):
    k = w_ref.shape[0]
    y = x * w_ref[pl.ds(k - 1, 1), cols]
    for s in range(1, k):
        y = y + _shift_down(x, prev, s, row) * w_ref[pl.ds(k - 1 - s, 1), cols]
    return y


def _lane_chunks(cw):
    return [slice(cb * LANES, (cb + 1) * LANES) for cb in range(cw // LANES)]


def conv_fwd(x, w, *, name):
    t = x.shape[0]
    k, c = w.shape
    tm, cw = min(CONV_TM, t), CONV_CW
    nb8 = tm // SUBLANES

    def body(x_ref, p_ref, w_ref, o_ref):
        first = pl.program_id(1) == 0
        row = lax.broadcasted_iota(jnp.int32, (tm, LANES), 0)
        for cols in _lane_chunks(cw):
            o_ref[:, cols] = _conv_taps(x_ref[:, cols], jnp.where(first, 0.0, p_ref[:, cols]), w_ref, cols, row)

    return pl.pallas_call(
        body, grid=(c // cw, t // tm),
        in_specs=[pl.BlockSpec((tm, cw), lambda j, i: (i, j)),
                  pl.BlockSpec((SUBLANES, cw), lambda j, i: (jnp.maximum(i * nb8 - 1, 0), j)),
                  pl.BlockSpec((k, cw), lambda j, i: (0, j))],
        out_specs=pl.BlockSpec((tm, cw), lambda j, i: (i, j)),
        out_shape=jax.ShapeDtypeStruct((t, c), F32), name=name,
        compiler_params=_params(("parallel", "parallel")),
    )(x, x, w)


def conv_bwd(dy, x, w, *, name):
    t = x.shape[0]
    k, c = w.shape
    tm, cw = min(CONV_TM, t), CONV_CW
    nb8 = tm // SUBLANES
    ni = t // tm

    def body(dy_ref, dn_ref, x_ref, p_ref, w_ref, dx_ref, dw_ref):
        i = pl.program_id(1)
        first, last = i == 0, i == ni - 1
        row = lax.broadcasted_iota(jnp.int32, (tm, LANES), 0)
        for cols in _lane_chunks(cw):
            dyv, xv = dy_ref[:, cols], x_ref[:, cols]
            nxt = jnp.where(last, 0.0, dn_ref[:, cols])
            prev = jnp.where(first, 0.0, p_ref[:, cols])
            dx = dyv * w_ref[pl.ds(k - 1, 1), cols]
            dws = [jnp.sum(dyv * xv, axis=0, keepdims=True)]
            for s in range(1, k):
                dx = dx + _shift_up(dyv, nxt, s, row) * w_ref[pl.ds(k - 1 - s, 1), cols]
                dws.append(jnp.sum(dyv * _shift_down(xv, prev, s, row), axis=0, keepdims=True))
            dx_ref[:, cols] = dx.astype(dx_ref.dtype)
            for s in range(k):
                @pl.when(first)
                def _(s=s, dws=dws, cols=cols):
                    dw_ref[pl.ds(k - 1 - s, 1), cols] = dws[s]

                @pl.when(jnp.logical_not(first))
                def _(s=s, dws=dws, cols=cols):
                    dw_ref[pl.ds(k - 1 - s, 1), cols] += dws[s]

    return pl.pallas_call(
        body, grid=(c // cw, ni),
        in_specs=[pl.BlockSpec((tm, cw), lambda j, i: (i, j)),
                  pl.BlockSpec((SUBLANES, cw), lambda j, i: (jnp.minimum((i + 1) * nb8, t // SUBLANES - 1), j)),
                  pl.BlockSpec((tm, cw), lambda j, i: (i, j)),
                  pl.BlockSpec((SUBLANES, cw), lambda j, i: (jnp.maximum(i * nb8 - 1, 0), j)),
                  pl.BlockSpec((k, cw), lambda j, i: (0, j))],
        out_specs=[pl.BlockSpec((tm, cw), lambda j, i: (i, j)), pl.BlockSpec((k, cw), lambda j, i: (0, j))],
        out_shape=[jax.ShapeDtypeStruct((t, c), BF16), jax.ShapeDtypeStruct((k, c), F32)], name=name,
        compiler_params=_params(("parallel", "arbitrary")),
    )(dy, dy, x, x, w)


FFN_TM = 128
FFN_CW = D_FF // 2


def _ffn_specs(t, tm, cw, k):
    ncol = D_FF // cw
    cur = lambda off: pl.BlockSpec((tm, cw), lambda j, i: (i, j + off))
    prev = lambda off, hr: pl.BlockSpec((hr, cw), lambda j, i: (jnp.maximum(i * (tm // hr) - 1, 0), j + off))
    nxt = lambda off, hr: pl.BlockSpec((hr, cw), lambda j, i: (jnp.minimum((i + 1) * (tm // hr), t // hr - 1), j + off))
    taps = lambda off: pl.BlockSpec((k, cw), lambda j, i: (0, j + off))
    return cur, prev, nxt, taps, ncol


def _rows_before(ref, cols, first):
    return jnp.where(first, 0.0, ref[ref.shape[0] - SUBLANES:, cols].astype(F32))


def conv_act_fwd(u, w, *, name):
    t, k = u.shape[0], w.shape[0]
    tm, cw = min(FFN_TM, t), FFN_CW
    cur, prev, _, taps, ncol = _ffn_specs(t, tm, cw, k)

    def body(ug_ref, pg_ref, uv_ref, pv_ref, wg_ref, wv_ref, o_ref):
        first = pl.program_id(1) == 0
        row = lax.broadcasted_iota(jnp.int32, (tm, LANES), 0)
        for cb in range(cw // LANES):
            cols = slice(cb * LANES, (cb + 1) * LANES)
            cg = _conv_taps(ug_ref[:, cols].astype(F32), _rows_before(pg_ref, cols, first), wg_ref, cols, row)
            cv = _conv_taps(uv_ref[:, cols].astype(F32), _rows_before(pv_ref, cols, first), wv_ref, cols, row)
            o_ref[:, cols] = _f_act(cg, cv).astype(o_ref.dtype)

    return pl.pallas_call(
        body, grid=(ncol, t // tm),
        in_specs=[cur(0), prev(0, BF16_ROWS), cur(ncol), prev(ncol, BF16_ROWS), taps(0), taps(ncol)],
        out_specs=cur(0), out_shape=jax.ShapeDtypeStruct((t, D_FF), BF16), name=name,
        compiler_params=_params(("parallel", "parallel")),
    )(u, u, u, u, w, w)


def conv_act_bwd(u, dact, w, *, name):
    t, k = u.shape[0], w.shape[0]
    tm, cw = min(FFN_TM, t), FFN_CW
    cur, prev, nxt, taps, ncol = _ffn_specs(t, tm, cw, k)
    ni = t // tm

    def body(ug_ref, pg_ref, ng_ref, uv_ref, pv_ref, nv_ref, d_ref, dn_ref, wg_ref, wv_ref, dg_ref, dv_ref, dwg_ref, dwv_ref):
        i = pl.program_id(1)
        first, last = i == 0, i == ni - 1
        row = lax.broadcasted_iota(jnp.int32, (tm, LANES), 0)
        row8 = lax.broadcasted_iota(jnp.int32, (SUBLANES, LANES), 0)
        for cb in range(cw // LANES):
            cols = slice(cb * LANES, (cb + 1) * LANES)
            ug, uv = ug_ref[:, cols].astype(F32), uv_ref[:, cols].astype(F32)
            pg, pv = _rows_before(pg_ref, cols, first), _rows_before(pv_ref, cols, first)
            sg = [ug] + [_shift_down(ug, pg, s, row) for s in range(1, k)]
            sv = [uv] + [_shift_down(uv, pv, s, row) for s in range(1, k)]
            taps = lambda xs, w_ref: sum(xs[s] * w_ref[pl.ds(k - 1 - s, 1), cols] for s in range(k))
            _, vjp = jax.vjp(_f_act, taps(sg, wg_ref), taps(sv, wv_ref))
            dcg, dcv = vjp(d_ref[:, cols])
            after = lambda ref: ref[:SUBLANES, cols].astype(F32)
            _, vjp_n = jax.vjp(_f_act, _conv_taps(after(ng_ref), ug[tm - SUBLANES:], wg_ref, cols, row8),
                               _conv_taps(after(nv_ref), uv[tm - SUBLANES:], wv_ref, cols, row8))
            dcgn, dcvn = vjp_n(jnp.where(last, 0.0, dn_ref[:, cols]))
            for dc, dcn, xs, w_ref, dx_ref, dw_ref in ((dcg, dcgn, sg, wg_ref, dg_ref, dwg_ref),
                                                       (dcv, dcvn, sv, wv_ref, dv_ref, dwv_ref)):
                dx = dc * w_ref[pl.ds(k - 1, 1), cols]
                dws = [jnp.sum(dc * xs[0], axis=0, keepdims=True)]
                for s in range(1, k):
                    dx = dx + _shift_up(dc, dcn, s, row) * w_ref[pl.ds(k - 1 - s, 1), cols]
                    dws.append(jnp.sum(dc * xs[s], axis=0, keepdims=True))
                dx_ref[:, cols] = dx.astype(dx_ref.dtype)
                for s in range(k):
                    @pl.when(first)
                    def _(s=s, dw_ref=dw_ref, dws=dws):
                        dw_ref[pl.ds(k - 1 - s, 1), cols] = dws[s]

                    @pl.when(jnp.logical_not(first))
                    def _(s=s, dw_ref=dw_ref, dws=dws):
                        dw_ref[pl.ds(k - 1 - s, 1), cols] += dws[s]

    half = jax.ShapeDtypeStruct((t, D_FF), BF16)
    dwh = jax.ShapeDtypeStruct((k, D_FF), F32)
    return pl.pallas_call(
        body, grid=(ncol, ni),
        in_specs=[cur(0), prev(0, BF16_ROWS), nxt(0, BF16_ROWS), cur(ncol), prev(ncol, BF16_ROWS), nxt(ncol, BF16_ROWS),
                  cur(0), nxt(0, SUBLANES), taps(0), taps(ncol)],
        out_specs=[cur(0), cur(0), taps(0), taps(0)], out_shape=[half, half, dwh, dwh], name=name,
        compiler_params=_params(("parallel", "arbitrary")),
    )(u, u, u, u, u, u, dact, dact, w, w)


def _each(f, *lists):
    return [f(*a) for a in zip(*lists)]


@jax.custom_vjp
def _inv_unit_lower(lms):
    return _inv_blocks(lms)


def _inv_blocks(lms):
    c = lms[0].shape[0]
    ri = lax.broadcasted_iota(jnp.int32, (c, c), 0)
    ci = lax.broadcasted_iota(jnp.int32, (c, c), 1)
    eye = (ri == ci).astype(F32)
    dms = _each(lambda lm: eye - jnp.where((ri >> 1) == (ci >> 1), lm, 0.0), lms)
    for lv in range(1, int(math.log2(c))):
        below = ((ri >> (lv + 1)) == (ci >> (lv + 1))) & ((ri >> lv) != (ci >> lv))
        dbs = _each(lambda dm: dm.astype(BF16), dms)
        ods = _each(lambda lm, db: _dot(jnp.where(below, lm, 0.0).astype(BF16), db).astype(BF16), lms, dbs)
        dms = _each(lambda dm, db, od: dm - _dot(db, od), dms, dbs, ods)
    return dms


def _inv_fwd(lms):
    tms = _inv_blocks(lms)
    return tms, tms


def _inv_bwd(tms, dts):
    tbs = _each(lambda tm: tm.astype(BF16), tms)
    mid = _each(lambda tb, dt: _dot(tb, dt.astype(BF16), TN).astype(BF16), tbs, dts)
    return (_each(lambda m, tb: -_dot(m, tb, NT), mid, tbs),)


_inv_unit_lower.defvjp(_inv_fwd, _inv_bwd)


def _l2n(x):
    return x * lax.rsqrt(jnp.sum(x * x, axis=-1, keepdims=True) + EPS)


def _prep_fn(cqs, cks, cvs, bg, sel_b, sel_g):
    c = cqs[0].shape[0]
    ri = lax.broadcasted_iota(jnp.int32, (c, c), 0)
    ci = lax.broadcasted_iota(jnp.int32, (c, c), 1)
    eye = (ri == ci).astype(F32)
    incl, strict = ci <= ri, ci < ri
    last = lax.broadcasted_iota(jnp.int32, (c, 1), 0) == c - 1
    to_row = lambda col: jnp.sum(col * eye, axis=0, keepdims=True)
    qs = _each(lambda a: _l2n(_silu(a)) * (HEAD_DIM_A ** -0.5), cqs)
    ks = _each(lambda a: _l2n(_silu(a)), cks)
    vbs = _each(lambda a: _silu(a).astype(BF16), cvs)
    betas = _each(lambda m: jnp.sum(bg * m, axis=1, keepdims=True), sel_b)
    gs = _each(lambda m: jnp.sum(bg * m, axis=1, keepdims=True), sel_g)
    gcss = _each(lambda g: jnp.sum(jnp.where(incl, to_row(g), 0.0), axis=1, keepdims=True), gs)
    gtots = _each(lambda gcs: jnp.sum(jnp.where(last, gcs, 0.0), axis=0, keepdims=True), gcss)
    decays = _each(lambda gcs: jnp.exp(jnp.where(incl, gcs - to_row(gcs), NEG)), gcss)
    kbs = _each(lambda k: k.astype(BF16), ks)
    lms = _each(lambda beta, kb, dec: jnp.where(strict, beta * _dot(kb, kb, NT) * dec, 0.0), betas, kbs, decays)
    ams = _each(lambda tm, beta: (tm * to_row(beta)).astype(BF16), _inv_unit_lower(lms), betas)
    gams = _each(jnp.exp, gcss)
    u0s = _each(_dot, ams, vbs)
    wks = _each(lambda am, gam, k: _dot(am, (gam * k).astype(BF16)), ams, gams, ks)
    qks = _each(lambda q, kb, dec: _dot(q.astype(BF16), kb, NT) * dec, qs, kbs, decays)
    qds = _each(lambda q, gam: q * gam, qs, gams)
    kds = _each(lambda k, gtot, gcs: k * jnp.exp(gtot - gcs), ks, gtots, gcss)
    gls = _each(lambda gtot: jnp.exp(gtot) * jnp.ones((SUBLANES, LANES), F32), gtots)
    return u0s, wks, qds, kds, qks, gls


def _head_masks(h):
    lane = lax.broadcasted_iota(jnp.int32, (1, LANES), 1)
    return (lane == h).astype(F32), (lane == h + N_HEADS_A).astype(F32)


def _hsl(j):
    return slice(j * HEAD_DIM_A, (j + 1) * HEAD_DIM_A)


def gnorm_fwd(o, zsrc, w, *, name):
    t, width = o.shape
    tm = min(256, t)
    zoff = zsrc.shape[1] // width - 1

    def body(o_ref, z_ref, w_ref, out_ref):
        for h in range(N_HEADS_A):
            out_ref[:, _hsl(h)] = _f_gnorm(o_ref[:, _hsl(h)], z_ref[:, _hsl(h)], w_ref[...]).astype(out_ref.dtype)

    rows = pl.BlockSpec((tm, width), lambda i: (i, 0))
    return pl.pallas_call(
        body, grid=(t // tm,),
        in_specs=[rows, pl.BlockSpec((tm, width), lambda i: (i, zoff)), pl.BlockSpec(w.shape, lambda i: (0, 0))],
        out_specs=rows, out_shape=jax.ShapeDtypeStruct((t, width), BF16), name=name, compiler_params=_params(("parallel",)),
    )(o, zsrc, w)


def gnorm_bwd(o, zsrc, w, don, *, name):
    t, width = o.shape
    tm = min(256, t)
    zoff = zsrc.shape[1] // width - 1

    def body(o_ref, z_ref, w_ref, d_ref, do_ref, dz_ref, dw_ref):
        dw = jnp.zeros(w.shape, F32)
        for h in range(N_HEADS_A):
            _, vjp = jax.vjp(_f_gnorm, o_ref[:, _hsl(h)], z_ref[:, _hsl(h)], w_ref[...])
            do, dz, dwh = vjp(d_ref[:, _hsl(h)])
            do_ref[:, _hsl(h)] = do.astype(do_ref.dtype)
            dz_ref[:, _hsl(h)] = dz.astype(dz_ref.dtype)
            dw = dw + dwh
        first = pl.program_id(0) == 0

        @pl.when(first)
        def _():
            dw_ref[...] = dw

        @pl.when(jnp.logical_not(first))
        def _():
            dw_ref[...] += dw

    rows = pl.BlockSpec((tm, width), lambda i: (i, 0))
    wspec = pl.BlockSpec(w.shape, lambda i: (0, 0))
    return pl.pallas_call(
        body, grid=(t // tm,),
        in_specs=[rows, pl.BlockSpec((tm, width), lambda i: (i, zoff)), wspec, rows],
        out_specs=[rows, rows, wspec],
        out_shape=[jax.ShapeDtypeStruct((t, width), BF16)] * 2 + [jax.ShapeDtypeStruct(w.shape, F32)], name=name,
        compiler_params=_params(("arbitrary",)),
    )(o, zsrc, w, don)


def delta_prep(cqkv, bg, *, name):
    t = cqkv.shape[0]
    nh, hd, n = N_HEADS_A, HEAD_DIM_A, t // CHUNK

    def body(cq_ref, ck_ref, cv_ref, bg_ref, u0_ref, wk_ref, qd_ref, kd_ref, qk_ref, gl_ref):
        heads = range(nh)
        masks = [_head_masks(j) for j in heads]
        res = _prep_fn([cq_ref[:, _hsl(j)] for j in heads], [ck_ref[:, _hsl(j)] for j in heads],
                       [cv_ref[:, _hsl(j)] for j in heads], bg_ref[...], [m[0] for m in masks], [m[1] for m in masks])
        for o_ref, rs in zip((u0_ref, wk_ref, qd_ref, kd_ref, qk_ref), res[:5]):
            for j in heads:
                o_ref[:, _hsl(j)] = rs[j]
        for j in heads:
            gl_ref[j * SUBLANES:(j + 1) * SUBLANES, :] = res[5][j]

    blk = lambda off: pl.BlockSpec((CHUNK, nh * hd), lambda i: (i, off))
    return pl.pallas_call(
        body, grid=(n,),
        in_specs=[blk(0), blk(1), blk(2), pl.BlockSpec((CHUNK, LANES), lambda i: (i, 0))],
        out_specs=[blk(0)] * 5 + [pl.BlockSpec((nh * SUBLANES, LANES), lambda i: (i, 0))],
        out_shape=[jax.ShapeDtypeStruct((t, nh * hd), F32)] * 5 + [jax.ShapeDtypeStruct((n * nh * SUBLANES, LANES), F32)],
        name=name, compiler_params=_params(("parallel",)),
    )(cqkv, cqkv, cqkv, bg)


def delta_prep_bwd(cqkv, bg, cts, *, name):
    t = cqkv.shape[0]
    nh, hd, n = N_HEADS_A, HEAD_DIM_A, t // CHUNK

    def body(cq_ref, ck_ref, cv_ref, bg_ref, c0, c1, c2, c3, c4, c5, dc_ref, dbg_ref):
        heads = range(nh)
        masks = [_head_masks(j) for j in heads]
        _, vjp = jax.vjp(lambda a, b, c, d: _prep_fn(a, b, c, d, [m[0] for m in masks], [m[1] for m in masks]),
                         [cq_ref[:, _hsl(j)] for j in heads], [ck_ref[:, _hsl(j)] for j in heads],
                         [cv_ref[:, _hsl(j)] for j in heads], bg_ref[...])
        cts = tuple([c[:, _hsl(j)] for j in heads] for c in (c0, c1, c2, c3, c4))
        dqs, dks, dvs, dbg = vjp(cts + ([c5[j * SUBLANES:(j + 1) * SUBLANES, :] for j in heads],))
        for part, ds in enumerate((dqs, dks, dvs)):
            for j in heads:
                dc_ref[:, _hsl(part * nh + j)] = ds[j]
        dbg_ref[...] = dbg

    blk = lambda off: pl.BlockSpec((CHUNK, nh * hd), lambda i: (i, off))
    gl_spec = pl.BlockSpec((nh * SUBLANES, LANES), lambda i: (i, 0))
    bg_spec = pl.BlockSpec((CHUNK, LANES), lambda i: (i, 0))
    return pl.pallas_call(
        body, grid=(n,),
        in_specs=[blk(0), blk(1), blk(2), bg_spec] + [blk(0)] * 5 + [gl_spec],
        out_specs=[pl.BlockSpec((CHUNK, 3 * nh * hd), lambda i: (i, 0)), bg_spec],
        out_shape=[jax.ShapeDtypeStruct((t, 3 * nh * hd), F32), jax.ShapeDtypeStruct((t, LANES), F32)],
        name=name, compiler_params=_params(("parallel",)),
    )(cqkv, cqkv, cqkv, bg, *cts)


def delta_scan(u0, wk, qd, kd, qk, gl, *, name):
    t = u0.shape[0]
    nh, hd, n = N_HEADS_A, HEAD_DIM_A, t // CHUNK

    def body(u0_ref, wk_ref, qd_ref, kd_ref, qk_ref, gl_ref, o_ref, sin_ref, s_ref):
        @pl.when(pl.program_id(0) == 0)
        def _():
            s_ref[...] = jnp.zeros_like(s_ref)

        heads = list(range(nh))
        cols = lambda ref: [ref[:, _hsl(h)].astype(BF16) for h in heads]
        ss = [s_ref[h] for h in heads]
        for h in heads:
            sin_ref[h] = ss[h]
        sbs = _each(lambda s: s.astype(BF16), ss)
        ubs = _each(lambda h, wkb, sb: (u0_ref[:, _hsl(h)] - _dot(wkb, sb)).astype(BF16), heads, cols(wk_ref), sbs)
        os_ = _each(lambda qdb, sb, qkb, ub: _dot(qdb, sb) + _dot(qkb, ub), cols(qd_ref), sbs, cols(qk_ref), ubs)
        sn = _each(lambda h, s, kdb, ub: gl_ref[pl.ds(h * SUBLANES, 1), :] * s + _dot(kdb, ub, TN), heads, ss, cols(kd_ref), ubs)
        for h in heads:
            o_ref[:, _hsl(h)] = os_[h]
            s_ref[h] = sn[h]

    blk = pl.BlockSpec((CHUNK, nh * hd), lambda i: (i, 0))
    return pl.pallas_call(
        body, grid=(n,),
        in_specs=[blk] * 5 + [pl.BlockSpec((nh * SUBLANES, LANES), lambda i: (i, 0))],
        out_specs=[blk, pl.BlockSpec((None, nh, hd, hd), lambda i: (i, 0, 0, 0))],
        out_shape=[jax.ShapeDtypeStruct((t, nh * hd), F32), jax.ShapeDtypeStruct((n, nh, hd, hd), F32)],
        scratch_shapes=[pltpu.VMEM((nh, hd, hd), F32)], name=name,
        compiler_params=_params(("arbitrary",)),
    )(u0, wk, qd, kd, qk, gl)


def delta_scan_bwd(do, u0, wk, qd, kd, qk, gl, s_in, *, name):
    t = u0.shape[0]
    nh, hd, n = N_HEADS_A, HEAD_DIM_A, t // CHUNK

    def body(do_ref, u0_ref, wk_ref, qd_ref, kd_ref, qk_ref, gl_ref, sin_ref,
             du0_ref, dwk_ref, dqd_ref, dkd_ref, dqk_ref, dgl_ref, ds_ref):
        @pl.when(pl.program_id(0) == 0)
        def _():
            ds_ref[...] = jnp.zeros_like(ds_ref)

        corner = (lax.broadcasted_iota(jnp.int32, (SUBLANES, LANES), 0) == 0) & (lax.broadcasted_iota(jnp.int32, (SUBLANES, LANES), 1) == 0)
        heads = list(range(nh))
        cols = lambda ref: [ref[:, _hsl(h)].astype(BF16) for h in heads]
        ss, dss = [sin_ref[h] for h in heads], [ds_ref[h] for h in heads]
        sbs, dsbs = _each(lambda s: s.astype(BF16), ss), _each(lambda d: d.astype(BF16), dss)
        dobs, wkbs, qdbs, kdbs, qkbs = cols(do_ref), cols(wk_ref), cols(qd_ref), cols(kd_ref), cols(qk_ref)
        ubs = _each(lambda h, wkb, sb: (u0_ref[:, _hsl(h)] - _dot(wkb, sb)).astype(BF16), heads, wkbs, sbs)
        dus = _each(lambda qkb, dob, kdb, dsb: _dot(qkb, dob, TN) + _dot(kdb, dsb), qkbs, dobs, kdbs, dsbs)
        dubs = _each(lambda du: du.astype(BF16), dus)
        dwks = _each(lambda dub, sb: -_dot(dub, sb, NT), dubs, sbs)
        dqds = _each(lambda dob, sb: _dot(dob, sb, NT), dobs, sbs)
        dkds = _each(lambda ub, dsb: _dot(ub, dsb, NT), ubs, dsbs)
        dqks = _each(lambda dob, ub: _dot(dob, ub, NT), dobs, ubs)
        dgls = _each(lambda s, d: jnp.sum(jnp.sum(s * d, axis=1, keepdims=True), axis=0, keepdims=True), ss, dss)
        dsn = _each(lambda h, d, qdb, dob, wkb, dub: gl_ref[pl.ds(h * SUBLANES, 1), :] * d + _dot(qdb, dob, TN) - _dot(wkb, dub, TN),
                    heads, dss, qdbs, dobs, wkbs, dubs)
        for h in heads:
            du0_ref[:, _hsl(h)] = dus[h]
            dwk_ref[:, _hsl(h)] = dwks[h]
            dqd_ref[:, _hsl(h)] = dqds[h]
            dkd_ref[:, _hsl(h)] = dkds[h]
            dqk_ref[:, _hsl(h)] = dqks[h]
            dgl_ref[h * SUBLANES:(h + 1) * SUBLANES, :] = jnp.where(corner, dgls[h], 0.0)
            ds_ref[h] = dsn[h]

    blk = pl.BlockSpec((CHUNK, nh * hd), lambda i: (n - 1 - i, 0))
    gl_spec = pl.BlockSpec((nh * SUBLANES, LANES), lambda i: (n - 1 - i, 0))
    return pl.pallas_call(
        body, grid=(n,),
        in_specs=[blk] * 6 + [gl_spec, pl.BlockSpec((None, nh, hd, hd), lambda i: (n - 1 - i, 0, 0, 0))],
        out_specs=[blk] * 5 + [gl_spec],
        out_shape=[jax.ShapeDtypeStruct((t, nh * hd), F32)] * 5 + [jax.ShapeDtypeStruct((n * nh * SUBLANES, LANES), F32)],
        scratch_shapes=[pltpu.VMEM((nh, hd, hd), F32)], name=name,
        compiler_params=_params(("arbitrary",)),
    )(do, u0, wk, qd, kd, qk, gl, s_in)


N_PAIRS = N_HEADS_B // 2
PAIRS_PER_KV = N_PAIRS // N_KV_B


def _psl(j):
    return slice(j * LANES, (j + 1) * LANES)


KV_STEP = 4


def _att_fn(qps, kcs, kps, vcs, vps, sinks, kv0, first):
    w = WINDOW
    lane = lax.broadcasted_iota(jnp.int32, (1, LANES), 1)
    lo = (lane < HEAD_DIM_B).astype(F32)
    qi = lax.broadcasted_iota(jnp.int32, (w, w), 0)
    kj = lax.broadcasted_iota(jnp.int32, (w, w), 1)
    dist_c = (qi - kj).astype(F32)
    valid_c = kj <= qi
    valid_p = (kj > qi) & (first < 0.5)
    bf = lambda xs: [a.astype(BF16) for a in xs]
    kcb, kpb, vcb, vpb = bf(kcs), bf(kps), bf(vcs), bf(vps)
    scale = HEAD_DIM_B ** -0.5
    heads = [(g, j, half) for g in range(len(kcs)) for j in range(PAIRS_PER_KV) for half in range(2)]
    kvs = [g for g, _, _ in heads]
    hmasks = [lo if half == 0 else 1.0 - lo for _, _, half in heads]
    hds = [2.0 * (PAIRS_PER_KV * (kv0 + g) + j) + half for g, j, half in heads]
    slopes = _each(lambda hd: jnp.exp(-(hd + 1.0) * (8.0 / N_HEADS_B * math.log(2.0))), hds)
    snks = _each(lambda hd: jnp.sum(sinks * (lane.astype(F32) == hd).astype(F32), axis=1, keepdims=True), hds)
    qhs = _each(lambda h, hm: (qps[h[0] * PAIRS_PER_KV + h[1]] * hm).astype(BF16), heads, hmasks)
    lcs = _each(lambda qh, g, sl: jnp.where(valid_c, _dot(qh, kcb[g], NT) * scale - sl * dist_c, NEG), qhs, kvs, slopes)
    lps = _each(lambda qh, g, sl: jnp.where(valid_p, _dot(qh, kpb[g], NT) * scale - sl * (dist_c + w), NEG), qhs, kvs, slopes)
    ms = _each(lambda lc, lp, sk: lax.stop_gradient(jnp.maximum(jnp.maximum(jnp.max(lc, axis=1, keepdims=True),
                                                                            jnp.max(lp, axis=1, keepdims=True)), sk)), lcs, lps, snks)
    ecs = _each(lambda lc, m: jnp.exp(lc - m), lcs, ms)
    eps = _each(lambda lp, m: jnp.exp(lp - m), lps, ms)
    invs = _each(lambda ec, ep, sk, m: 1.0 / (jnp.sum(ec, axis=1, keepdims=True) + jnp.sum(ep, axis=1, keepdims=True) + jnp.exp(sk - m)),
                 ecs, eps, snks, ms)
    ohs = _each(lambda ec, ep, inv, g, hm: (_dot((ec * inv).astype(BF16), vcb[g]) + _dot((ep * inv).astype(BF16), vpb[g])) * hm,
                ecs, eps, invs, kvs, hmasks)
    return [ohs[2 * j] + ohs[2 * j + 1] for j in range(len(qps))]


def _scalar11(v):
    return jnp.full((1, 1), v, F32)


def _att_specs(row_of):
    cur = pl.BlockSpec((WINDOW, KV_STEP * LANES), lambda i, kv: (row_of(i), kv))
    prev = pl.BlockSpec((WINDOW, KV_STEP * LANES), lambda i, kv: (jnp.maximum(row_of(i) - 1, 0), kv))
    qs = pl.BlockSpec((WINDOW, KV_STEP * PAIRS_PER_KV * LANES), lambda i, kv: (row_of(i), kv))
    return qs, cur, prev, pl.BlockSpec((1, LANES), lambda i, kv: (0, 0))


def swa_fwd(qsrc, kd, vd, sinks, *, name):
    t = kd.shape[0]
    nb = t // WINDOW
    npair = KV_STEP * PAIRS_PER_KV

    def body(q_ref, kc_ref, kp_ref, vc_ref, vp_ref, s_ref, o_ref):
        first = _scalar11((pl.program_id(0) == 0).astype(F32))
        kv0 = _scalar11((pl.program_id(1) * KV_STEP).astype(F32))
        per_kv = lambda ref: [ref[:, _psl(g)] for g in range(KV_STEP)]
        outs = _att_fn([q_ref[:, _psl(j)] for j in range(npair)], per_kv(kc_ref), per_kv(kp_ref), per_kv(vc_ref), per_kv(vp_ref),
                       s_ref[...], kv0, first)
        for j in range(npair):
            o_ref[:, _psl(j)] = outs[j].astype(o_ref.dtype)

    qs, cur, prev, sk = _att_specs(lambda i: i)
    return pl.pallas_call(
        body, grid=(nb, N_KV_B // KV_STEP), in_specs=[qs, cur, prev, cur, prev, sk],
        out_specs=qs, out_shape=jax.ShapeDtypeStruct((t, N_PAIRS * LANES), BF16), name=name,
        compiler_params=_params(("parallel", "parallel")),
    )(qsrc, kd, kd, vd, vd, sinks)


def swa_bwd(do, qsrc, kd, vd, sinks, *, name):
    t = kd.shape[0]
    nb = t // WINDOW

    npair = KV_STEP * PAIRS_PER_KV

    def body(do_ref, q_ref, kc_ref, kp_ref, vc_ref, vp_ref, s_ref, dq_ref, dk_ref, dv_ref, ds_ref, carry_k, carry_v):
        step, kvg = pl.program_id(0), pl.program_id(1)
        first = _scalar11((step == nb - 1).astype(F32))

        @pl.when((step == 0) & (kvg == 0))
        def _():
            carry_k[...] = jnp.zeros_like(carry_k)
            carry_v[...] = jnp.zeros_like(carry_v)
            ds_ref[...] = jnp.zeros_like(ds_ref)

        kv0 = _scalar11((kvg * KV_STEP).astype(F32))
        per_kv = lambda ref: [ref[:, _psl(g)].astype(F32) for g in range(KV_STEP)]
        _, vjp = jax.vjp(lambda *a: _att_fn(*a, kv0, first), [q_ref[:, _psl(j)].astype(F32) for j in range(npair)],
                         per_kv(kc_ref), per_kv(kp_ref), per_kv(vc_ref), per_kv(vp_ref), s_ref[...])
        dqs, dkc, dkp, dvc, dvp, dsk = vjp([do_ref[:, _psl(j)].astype(F32) for j in range(npair)])
        for j in range(npair):
            dq_ref[:, _psl(j)] = dqs[j].astype(dq_ref.dtype)
        ds_ref[...] += dsk
        fold = lambda g: g + pltpu.roll(g, HEAD_DIM_B, 1)
        for g in range(KV_STEP):
            kv = kvg * KV_STEP + g
            dk_ref[:, _psl(g)] = fold(dkc[g] + carry_k[kv]).astype(dk_ref.dtype)
            dv_ref[:, _psl(g)] = fold(dvc[g] + carry_v[kv]).astype(dv_ref.dtype)
            carry_k[kv] = dkp[g]
            carry_v[kv] = dvp[g]

    qs, cur, prev, sk = _att_specs(lambda i: nb - 1 - i)
    return pl.pallas_call(
        body, grid=(nb, N_KV_B // KV_STEP),
        in_specs=[qs, qs, cur, prev, cur, prev, sk],
        out_specs=[qs, cur, cur, sk],
        out_shape=[jax.ShapeDtypeStruct((t, N_PAIRS * LANES), BF16), jax.ShapeDtypeStruct((t, N_KV_B * LANES), BF16),
                   jax.ShapeDtypeStruct((t, N_KV_B * LANES), BF16), jax.ShapeDtypeStruct((1, LANES), F32)],
        scratch_shapes=[pltpu.VMEM((N_KV_B, WINDOW, LANES), F32), pltpu.VMEM((N_KV_B, WINDOW, LANES), F32)],
        name=name, compiler_params=_params(("arbitrary", "arbitrary")),
    )(do, qsrc, kd, kd, vd, vd, sinks)


def loss_head(h, tgt, w, *, name):
    t, d = h.shape
    tm = min(256, t)

    def body(h_ref, t_ref, w_ref, dh_ref, dw_ref, l_ref):
        tg = t_ref[...]

        def f(hv, wv):
            err = _f_norm(hv, wv) - tg
            return 0.5 * jnp.sum(jnp.sum(err * err, axis=1, keepdims=True), axis=0, keepdims=True) * (1.0 / d)

        lv, vjp = jax.vjp(f, h_ref[...], w_ref[...])
        dh, dw = vjp(jnp.ones((1, 1), F32))
        dh_ref[...] = dh
        first = pl.program_id(0) == 0

        @pl.when(first)
        def _():
            dw_ref[...] = dw
            l_ref[...] = lv * jnp.ones((1, LANES), F32)

        @pl.when(jnp.logical_not(first))
        def _():
            dw_ref[...] += dw
            l_ref[...] += lv * jnp.ones((1, LANES), F32)

    rows = pl.BlockSpec((tm, d), lambda i: (i, 0))
    one = lambda c: pl.BlockSpec((1, c), lambda i: (0, 0))
    return pl.pallas_call(
        body, grid=(t // tm,), in_specs=[rows, rows, one(d)], out_specs=[rows, one(d), one(LANES)],
        out_shape=[jax.ShapeDtypeStruct((t, d), F32), jax.ShapeDtypeStruct((1, d), F32), jax.ShapeDtypeStruct((1, LANES), F32)],
        name=name, compiler_params=_params(("arbitrary",)),
    )(h, tgt, w)


def _row_tile(r, cap=256):
    tr = r
    if r % SUBLANES == 0:
        for cand in range(SUBLANES, min(r, cap) + 1, SUBLANES):
            if r % cand == 0:
                tr = cand
    return tr


def _adamw_update(wv, gv, mv, vv):
    mn = ADAM_B1 * mv + (1.0 - ADAM_B1) * gv
    vn = ADAM_B2 * vv + (1.0 - ADAM_B2) * jnp.square(gv)
    m_hat = mn / (1.0 - ADAM_B1 ** ADAM_STEP)
    v_hat = vn / (1.0 - ADAM_B2 ** ADAM_STEP)
    return -ADAM_LR * (m_hat / (jnp.sqrt(v_hat) + ADAM_EPS) + ADAM_WD * wv), mn, vn


def adamw_layers(w, halves, m, v, *, name):
    nl, r, c = w.shape
    tr = _row_tile(r // 2)
    nbh = r // 2 // tr

    def body(w_ref, *rest):
        g_refs, m_ref, v_ref = rest[:2 * nl], rest[2 * nl], rest[2 * nl + 1]
        d_ref, mo_ref, vo_ref, go_ref = rest[2 * nl + 2:]
        layer, i = pl.program_id(0), pl.program_id(1)
        mine = (i < nbh) == (lax.axis_index("c") == 0)
        gv = jnp.where(mine, g_refs[0][...], g_refs[1][...])
        for k in range(1, nl):
            gv = jnp.where(layer == k, jnp.where(mine, g_refs[2 * k][...], g_refs[2 * k + 1][...]), gv)
        d_ref[...], mo_ref[...], vo_ref[...] = _adamw_update(w_ref[...], gv, m_ref[...], v_ref[...])
        go_ref[...] = gv

    spec3 = pl.BlockSpec((None, tr, c), lambda k, i: (k, i, 0))
    g_specs = [pl.BlockSpec((tr, c), lambda k, i, q=q: (jnp.where(k == q, i % nbh, 0), 0)) for q in range(nl) for _ in range(2)]
    return pl.pallas_call(
        body, grid=(nl, r // tr), in_specs=[spec3] + g_specs + [spec3, spec3], out_specs=[spec3] * 4,
        out_shape=[jax.ShapeDtypeStruct((nl, r, c), F32)] * 4, name=name, compiler_params=_params(("arbitrary", "arbitrary")),
    )(w, *[h for pair in halves for h in pair], m, v)


def adamw(w, g, m, v, *, name):
    r, c = w.shape
    tr = _row_tile(r)

    def body(w_ref, g_ref, m_ref, v_ref, d_ref, mo_ref, vo_ref):
        d_ref[...], mo_ref[...], vo_ref[...] = _adamw_update(w_ref[...], g_ref[...], m_ref[...], v_ref[...])

    spec = pl.BlockSpec((tr, c), lambda i: (i, 0))
    return pl.pallas_call(
        body, grid=(r // tr,), in_specs=[spec] * 4, out_specs=[spec] * 3,
        out_shape=[jax.ShapeDtypeStruct((r, c), F32)] * 3, name=name, compiler_params=_params(("parallel",)),
    )(w, g, m, v)


def _place():
    return lax.axis_index("x"), lax.axis_index("y"), lax.axis_index("c")


def allgather8(blk, *, name):
    def body(x_ref, out_ref, send_sems, recv_sems, local_sem):
        x, y, c = _place()
        me = 4 * x + 2 * y + c
        mine = pltpu.make_async_copy(x_ref, out_ref.at[me], local_sem)
        mine.start()
        sent = []
        for k in range(1, N_DEV):
            to = (x ^ ((k >> 2) & 1), y ^ ((k >> 1) & 1), c ^ (k & 1))
            cp = pltpu.make_async_remote_copy(src_ref=x_ref, dst_ref=out_ref.at[me], send_sem=send_sems.at[k - 1],
                                              recv_sem=recv_sems.at[k - 1], device_id=to, device_id_type=MESH)
            cp.start()
            sent.append(cp)
        for k in range(1, N_DEV):
            frm = me ^ k
            pltpu.make_async_remote_copy(src_ref=x_ref, dst_ref=out_ref.at[frm], send_sem=send_sems.at[k - 1],
                                         recv_sem=recv_sems.at[k - 1], device_id=(x, y, c), device_id_type=MESH).wait_recv()
        for cp in sent:
            cp.wait_send()
        mine.wait()

    vm = pl.BlockSpec(memory_space=pltpu.VMEM)
    return pl.pallas_call(
        body, in_specs=[vm], out_specs=vm, out_shape=jax.ShapeDtypeStruct((N_DEV,) + blk.shape, blk.dtype), name=name,
        scratch_shapes=[pltpu.SemaphoreType.DMA((N_DEV - 1,)), pltpu.SemaphoreType.DMA((N_DEV - 1,)), pltpu.SemaphoreType.DMA],
    )(blk)


def _other_chips(x, y):
    return [(1 - x, y), (x, 1 - y), (1 - x, 1 - y)]


def _hbm_call(body, ins, out_shapes, n_sems, name):
    hbm = pl.BlockSpec(memory_space=pl.ANY)
    return pl.pallas_call(
        body, in_specs=[hbm] * len(ins), out_specs=[hbm] * len(out_shapes), out_shape=out_shapes, name=name,
        scratch_shapes=[pltpu.SemaphoreType.DMA((n_sems,)), pltpu.SemaphoreType.DMA((n_sems,))],
    )(*ins)


def _half_rows(c, rh):
    return pl.ds(pl.multiple_of(c * rh, BF16_ROWS), rh)


def gather_units(units, *, name):
    nu = len(units)
    shapes = []
    for arr, layer_major in units:
        r, cols = arr.shape
        shapes.append(jax.ShapeDtypeStruct((2, N_CHIPS, r // 2, cols) if layer_major else (N_CHIPS, r, cols), arr.dtype))

    def body(*refs):
        in_refs, out_refs, send_sems, recv_sems = refs[:nu], refs[nu:2 * nu], refs[2 * nu], refs[2 * nu + 1]
        x, y, c = _place()
        me_chip = 2 * x + y
        sib = (x, y, 1 - c)
        chips = _other_chips(x, y)

        def copy(k, src, dst, to):
            return pltpu.make_async_remote_copy(src_ref=src, dst_ref=dst, send_sem=send_sems.at[k], recv_sem=recv_sems.at[k],
                                                device_id=to, device_id_type=MESH)

        first, passed, landing = [], [], []
        for u, (arr, layer_major) in enumerate(units):
            rh = arr.shape[0] // 2
            out_ref = out_refs[u]
            slot = (lambda chip, half, o=out_ref: o.at[half, chip]) if layer_major else \
                   (lambda chip, half, o=out_ref, rh=rh: o.at[chip, _half_rows(half, rh), :])
            my_half = in_refs[u].at[_half_rows(c, rh), :]
            for j, (cx, cy) in enumerate(chips):
                k = 6 * u + j
                first.append(copy(k, my_half, slot(me_chip, c), (cx, cy, c)))
                passed.append(copy(k + 3, slot(2 * cx + cy, c), slot(2 * cx + cy, c), sib))
                landing.append((copy(k, my_half, slot(2 * cx + cy, c), sib), copy(k + 3, my_half, slot(2 * cx + cy, 1 - c), sib)))
        for cp in first:
            cp.start()
        for (over_ici, _), fwd in zip(landing, passed):
            over_ici.wait_recv()
            fwd.start()
        for _, from_sibling in landing:
            from_sibling.wait_recv()
        for cp in first + passed:
            cp.wait_send()

    return _hbm_call(body, [a for a, _ in units], shapes, 6 * nu, name)


HBM_SPEC = pl.BlockSpec(memory_space=pltpu.HBM)
SEM_SPEC = pl.BlockSpec(memory_space=pltpu.SEMAPHORE)
ORDERED_EFFECT = pltpu.SideEffectType.DATAFLOW_SIDE_EFFECTING


def _split_start(body, srcs, land_shapes, after, *, name):
    nu = len(srcs)
    lands = [lax.empty(s.shape, s.dtype) for s in land_shapes]

    def whole(*refs):
        body(refs[:nu], refs[nu:2 * nu], refs[2 * nu + 1], refs[2 * nu + 2])
        refs[-1][...] = jnp.zeros((SUBLANES, LANES), F32)

    hbm = lambda a: pltpu.with_memory_space_constraint(a, pltpu.HBM)
    sems = pltpu.SemaphoreType.DMA((nu,))
    res = pl.pallas_call(
        whole, name=name, in_specs=[HBM_SPEC] * (2 * nu) + [pl.BlockSpec(memory_space=pl.ANY)],
        out_shape=[sems, sems] + [pltpu.HBM(a.shape, a.dtype) for a in srcs] + [pltpu.HBM(s.shape, s.dtype) for s in land_shapes]
        + [jax.ShapeDtypeStruct((SUBLANES, LANES), F32)],
        out_specs=[SEM_SPEC, SEM_SPEC] + [HBM_SPEC] * (2 * nu) + [pl.BlockSpec(memory_space=pltpu.VMEM)],
        input_output_aliases={q: 2 + q for q in range(2 * nu)},
        compiler_params=pltpu.CompilerParams(has_side_effects=ORDERED_EFFECT),
    )(*[hbm(a) for a in srcs], *[hbm(a) for a in lands], after)
    return res[0], res[1], res[2:2 + nu], res[2 + nu:2 + 2 * nu], res[-1]


def _split_wait(pending, moved, after, *, name):
    send_sems, recv_sems, srcs, lands, _ = pending
    nu = len(srcs)

    def body(*refs):
        land_refs, ssem, rsem = refs[nu:2 * nu], refs[2 * nu], refs[2 * nu + 1]
        x, y, c = _place()
        for u in range(nu):
            size = moved(land_refs[u])
            cp = pltpu.make_async_remote_copy(src_ref=size, dst_ref=size, send_sem=ssem.at[u], recv_sem=rsem.at[u],
                                              device_id=(x, y, c), device_id_type=MESH)
            cp.wait_send()
            cp.wait_recv()

    res = pl.pallas_call(
        body, name=name, in_specs=[HBM_SPEC] * (2 * nu) + [SEM_SPEC, SEM_SPEC, pl.BlockSpec(memory_space=pl.ANY)],
        out_shape=[pltpu.HBM(a.shape, a.dtype) for a in srcs] + [pltpu.HBM(a.shape, a.dtype) for a in lands],
        out_specs=[HBM_SPEC] * (2 * nu), input_output_aliases={q: q for q in range(2 * nu)},
        compiler_params=pltpu.CompilerParams(has_side_effects=ORDERED_EFFECT),
    )(*srcs, *lands, send_sems, recv_sems, after)
    return res[nu:]


def gather_start(shards, after, *, name):
    def body(src_refs, land_refs, send_sems, recv_sems):
        x, y, c = _place()
        for u, shard in enumerate(shards):
            rows = _half_rows(c, shard.shape[0] // 2)
            for cx, cy in _other_chips(x, y):
                for core in range(2):
                    pltpu.make_async_remote_copy(src_ref=src_refs[u].at[rows, :], dst_ref=land_refs[u].at[2 * x + y, rows, :],
                                                 send_sem=send_sems.at[u], recv_sem=recv_sems.at[u], device_id=(cx, cy, core),
                                                 device_id_type=MESH).start()

    return _split_start(body, shards, [jax.ShapeDtypeStruct((N_CHIPS,) + s.shape, s.dtype) for s in shards], after, name=name)


def gather_wait(pending, after, *, name):
    return _split_wait(pending, lambda land: land.at[pl.ds(0, N_CHIPS - 1)], after, name=name)


def scatter_start(pairs, *, name):
    def body(src_refs, land_refs, send_sems, recv_sems):
        x, y, c = _place()
        for u in range(len(pairs)):
            for j, (cx, cy) in enumerate(_other_chips(x, y)):
                pltpu.make_async_remote_copy(src_ref=src_refs[u].at[2 * cx + cy], dst_ref=land_refs[u].at[j], send_sem=send_sems.at[u],
                                             recv_sem=recv_sems.at[u], device_id=(cx, cy, c), device_id_type=MESH).start()

    return _split_start(body, pairs, [jax.ShapeDtypeStruct((N_CHIPS - 1,) + p.shape[1:], p.dtype) for p in pairs], pairs[0], name=name)


def scatter_wait(pending, after, *, name):
    return _split_wait(pending, lambda land: land, after, name=name)


def swap_units(units, *, name):
    nu = len(units)

    def body(*refs):
        g_refs, out_refs, send_sems, recv_sems = refs[:nu], refs[nu:2 * nu], refs[2 * nu], refs[2 * nu + 1]
        x, y, c = _place()
        cps = [pltpu.make_async_remote_copy(src_ref=g_refs[u].at[:, _half_rows(1 - c, units[u].shape[1] // 2), :], dst_ref=out_refs[u],
                                            send_sem=send_sems.at[u], recv_sem=recv_sems.at[u], device_id=(x, y, 1 - c),
                                            device_id_type=MESH) for u in range(nu)]
        for cp in cps:
            cp.start()
        for cp in cps:
            cp.wait()

    shapes = [jax.ShapeDtypeStruct((N_CHIPS, g.shape[1] // 2, g.shape[2]), g.dtype) for g in units]
    return _hbm_call(body, units, shapes, nu, name)


def join_units(units, *, name):
    nu = len(units)

    def body(*refs):
        h_refs, out_refs, send_sems, recv_sems = refs[:nu], refs[nu:2 * nu], refs[2 * nu], refs[2 * nu + 1]
        x, y, c = _place()
        cps = [pltpu.make_async_remote_copy(src_ref=h_refs[u], dst_ref=out_refs[u], send_sem=send_sems.at[u], recv_sem=recv_sems.at[u],
                                            device_id=(x, y, 1 - c), device_id_type=MESH) for u in range(nu)]
        for cp in cps:
            cp.start()
        for cp in cps:
            cp.wait()

    return _hbm_call(body, units, [jax.ShapeDtypeStruct(h.shape, h.dtype) for h in units], nu, name)


def _half_tile(rh):
    tr = rh
    for cand in range(BF16_ROWS, min(rh, 512) + 1, BF16_ROWS):
        if rh % cand == 0:
            tr = cand
    return tr


def pair_add(g, sib, *, name):
    nc, rh, cols = sib.shape
    tr = _half_tile(rh)
    nbh = rh // tr

    def body(g0_ref, g1_ref, s_ref, o_ref):
        mine = jnp.where(lax.axis_index("c") == 0, g0_ref[...], g1_ref[...])
        o_ref[...] = (mine.astype(F32) + s_ref[...].astype(F32)).astype(o_ref.dtype)

    blk = lambda off: pl.BlockSpec((None, tr, cols), lambda j, i: (j, off + i, 0))
    return pl.pallas_call(
        body, grid=(nc, nbh), in_specs=[blk(0), blk(nbh), blk(0)], out_specs=blk(0),
        out_shape=jax.ShapeDtypeStruct(sib.shape, BF16), name=name, compiler_params=_params(("parallel", "parallel")),
    )(g, g, sib)


def chips_add(pair, landed, *, name):
    nc, rh, cols = pair.shape
    tr = _half_tile(rh)

    def body(*refs):
        chip = 2 * lax.axis_index("x") + lax.axis_index("y")
        acc = refs[0][...]
        for j in range(1, nc):
            acc = jnp.where(chip == j, refs[j][...], acc)
        acc = acc.astype(F32)
        for r in refs[nc:-1]:
            acc = acc + r[...].astype(F32)
        refs[-1][...] = acc

    part = lambda q: pl.BlockSpec((None, tr, cols), lambda i, q=q: (q, i, 0))
    return pl.pallas_call(
        body, grid=(rh // tr,), in_specs=[part(q) for q in range(nc)] + [part(q) for q in range(landed.shape[0])],
        out_specs=pl.BlockSpec((tr, cols), lambda i: (i, 0)),
        out_shape=jax.ShapeDtypeStruct((rh, cols), F32), name=name, compiler_params=_params(("parallel",)),
    )(*[pair] * nc, *[landed] * landed.shape[0])


def sum8(g, *, name):
    def body(g_ref, o_ref):
        acc = g_ref[0]
        for d in range(1, N_DEV):
            acc = acc + g_ref[d]
        o_ref[...] = acc

    return pl.pallas_call(body, out_shape=jax.ShapeDtypeStruct(g.shape[1:], F32), name=name)(g)


def _dup_halves(a):
    t = a.shape[0]
    a = a.reshape(t, N_KV_B, HEAD_DIM_B)
    return jnp.concatenate([a, a], axis=-1).reshape(t, N_KV_B * LANES)


def _undup(a):
    t = a.shape[0]
    return a.reshape(t, N_KV_B, LANES)[:, :, :HEAD_DIM_B].reshape(t, N_KV_B * HEAD_DIM_B)


def _lane_pad(v, offset=0):
    return jnp.zeros((1, LANES), F32).at[0, offset:offset + v.shape[0]].set(v)


SHARD_UP = 2 * D_FF // N_CHIPS
SHARD_BIN = (N_HEADS_B + 2 * N_KV_B) * HEAD_DIM_B // N_CHIPS
SHARD_PROJ = D_MODEL // N_CHIPS


def local_step(x, p, tgt, sm, weight, on_grads):
    t = x.shape[0]
    rtm = min(256, t)
    hk = N_HEADS_A * HEAD_DIM_A
    qd_b = N_HEADS_B * HEAD_DIM_B
    kd_b = N_KV_B * HEAD_DIM_B
    gs = {}
    norm = lambda h, w, nm: tile_map(_f_norm, [(h, D_MODEL, 0)], [w], [(D_MODEL, BF16)], tm=rtm, ncol=1, name=nm)[0]

    spec = pl.BlockSpec
    mtm = _tile(D_MODEL, MM_TM_CAP)
    p_bf = p.astype(BF16)
    alog_p = _lane_pad(sm["a_log"][0], N_HEADS_A)
    dtb_p = _lane_pad(sm["a_dt_bias"][0], N_HEADS_A)
    sinks_p = _lane_pad(sm["b_sinks"][0])
    nw = lambda name, i: sm[name][i:i + 1]
    by_chip = lambda kdim, ns: dict(tn=ns, tk=kdim, b_spec=spec((None, kdim, ns), lambda r, j, kk: (j, kk, 0)))
    by_chip_t = lambda ndim, ns: dict(n=ndim, tn=ndim, tk=ns, b_spec=spec((None, ndim, ns), lambda r, j, kk: (kk, j, 0)))
    cache = {}

    def wgt(name, i, after):
        if (name, i) not in cache:
            cache[name, i] = weight(name, i, after)
        return cache[name, i]

    saved = []
    h = x
    hn_next = norm(h, nw("norm_mix", 0), "norm_mix0")
    for i in range(DEPTH):
        s = {"h0": h, "hn": hn_next}
        if i % 2 == 0:
            s["pm"] = mm(s["hn"], wgt("a_w_in", i, h), name="a_in")
            tail = (s["pm"], LANES, 4 * hk // LANES)
            s["c"] = conv_fwd(s["pm"], wgt("a_conv", i, h), name="a_conv")
            s["bg"] = tile_map(_f_betag, [tail], [alog_p, dtb_p], [(LANES, F32)], tm=rtm, ncol=1, name="a_betag")[0]
            s["prep"] = delta_prep(s["c"], s["bg"], name="a_prep")
            s["o"], s["s_in"] = delta_scan(*s["prep"], name="a_scan")
            s["on"] = gnorm_fwd(s["o"], s["pm"], sm["a_norm"], name="a_gnorm")
            h, s["hf"] = mm(s["on"], wgt("a_w_out", i, s["on"]), add=h, norm_w=nw("norm_ffn", i), name="a_out")
        else:
            s["pb"] = mm(s["hn"], wgt("b_w_in", i, s["hn"]), name="b_in", out_dtype=BF16, n=N_CHIPS * SHARD_BIN,
                         **by_chip(D_MODEL, SHARD_BIN))
            s["kd"], s["vd"] = _dup_halves(s["pb"][:, qd_b:qd_b + kd_b]), _dup_halves(s["pb"][:, qd_b + kd_b:])
            s["ao"] = swa_fwd(s["pb"], s["kd"], s["vd"], sinks_p, name="b_att")
            h, s["hf"] = mm(s["ao"], wgt("b_w_out", i, s["ao"]), add=h, norm_w=nw("norm_ffn", i), name="b_out")
        s["h1"] = h
        s["u"] = mm(s["hf"], wgt("f_w_up", i, s["hf"]), name=f"f_up{i}", out_dtype=BF16, n=2 * D_FF, **by_chip(D_MODEL, SHARD_UP))
        s["act"] = conv_act_fwd(s["u"], wgt("f_conv", i, s["hf"]), name=f"f_conv_act{i}")
        h, s["hp"] = mm(s["act"], wgt("f_w_down", i, s["act"]), add=h, norm_w=nw("norm_ple", i), name=f"f_down{i}")
        s["h2"] = h
        s["gl"] = mm(s["hp"], wgt("ple_w_gate", i, s["hp"]), name=f"ple_gate{i}")
        s["pe"] = mm(p_bf[i], wgt("ple_w_proj", i, s["hp"]), name=f"ple_proj{i}", n=D_MODEL, **by_chip(PLE_DIM, SHARD_PROJ))
        rows3 = [(h, D_MODEL, 0), (s["gl"], D_MODEL, 0), (s["pe"], D_MODEL, 0)]
        if i + 1 < DEPTH:
            def mix_norm(hv, g, e, wn):
                hn = hv + _f_ple(g, e)
                return hn, _f_norm(hn, wn)
            h, hn_next = tile_map(mix_norm, rows3, [nw("norm_mix", i + 1)], [(D_MODEL, F32), (D_MODEL, BF16)], tm=rtm, ncol=1,
                                  name=f"ple_mix{i}")
        else:
            h = tile_map(lambda hv, g, e: hv + _f_ple(g, e), rows3, [], [(D_MODEL, F32)], tm=rtm, ncol=1, name=f"ple_mix{i}")[0]
        saved.append(s)

    dh, gnf, loss = loss_head(h, tgt, sm["norm_final"][None, :], name="loss_head")
    gs["norm_final"] = gnf[0]

    g_mix, g_ffn, g_ple, g_conv = ([None] * DEPTH for _ in range(4))
    zero = jnp.zeros((1, 1), F32)
    for i in reversed(range(DEPTH)):
        s, gw = saved[i], {}
        by_rows = lambda g: g.reshape(N_CHIPS, g.shape[0] // N_CHIPS, g.shape[1])
        (dgl, dpe), _ = tile_vjp(_f_ple, [(s["gl"], D_MODEL, 0), (s["pe"], D_MODEL, 0)], [], [(dh, D_MODEL, 0)], n_diff=2,
                                 tm=rtm, ncol=1, name=f"ple_mix_bwd{i}", grad_dtypes=[BF16, BF16])
        gw["ple_w_proj"] = mm(p_bf[i], dpe, ta=True, name=f"ple_proj_dw{i}", out_dtype=BF16, tn=SHARD_PROJ,
                              o_shape=(N_CHIPS, PLE_DIM, SHARD_PROJ), o_spec=spec((None, PLE_DIM, SHARD_PROJ), lambda r, j, kk: (j, r, 0)))
        gw["ple_w_gate"] = by_rows(mm(s["hp"], dgl, ta=True, name=f"ple_gate_dw{i}", out_dtype=BF16))
        fused = dict(tb=True, tm_cap=MM_TM_CAP // 2)
        dh, g_ple[i] = mm(dgl, cache["ple_w_gate", i], name=f"ple_gate_dx{i}", norm_grad=(s["h2"], nw("norm_ple", i) + zero, dh), **fused)

        dact = mm(dh, cache["f_w_down", i], tb=True, name=f"f_down_dx{i}")
        gw["f_w_down"] = by_rows(mm(s["act"], dh, ta=True, name=f"f_down_dw{i}", out_dtype=BF16, tm_cap=D_FF // 2))
        du_halves = conv_act_bwd(s["u"], dact, cache["f_conv", i], name=f"f_conv_act_bwd{i}")
        g_conv[i] = jnp.concatenate(du_halves[2:], axis=1)
        dhf = g_up = None
        for half, du in enumerate(du_halves[:2]):
            c0 = half * (N_CHIPS // 2)
            g_up = mm(s["hf"], du, ta=True, name=f"f_up_dw{i}_{half}", out_dtype=BF16, tn=SHARD_UP, into=g_up,
                      o_shape=(N_CHIPS, D_MODEL, SHARD_UP), o_spec=spec((None, mtm, SHARD_UP), lambda r, j, kk, c0=c0: (c0 + j, r, 0)))
            dhf = mm(du, cache["f_w_up", i], name=f"f_up_dx{i}_{half}", n=D_MODEL, tn=D_MODEL, tk=SHARD_UP, add=dhf,
                     b_spec=spec((None, D_MODEL, SHARD_UP), lambda r, j, kk, c0=c0: (c0 + kk, j, 0)),
                     norm_grad=(s["h1"], nw("norm_ffn", i), dh) if half else None, **fused)
        gw["f_w_up"] = g_up
        dh, g_ffn[i] = dhf
        token, gw = on_grads(i, "ffn", gw), {}
        w_out = cache["a_w_out" if i % 2 == 0 else "b_w_out", i]
        if token is not None:
            w_out = w_out + token[:1, :1].astype(BF16)

        if i % 2 == 0:
            don = mm(dh, w_out, tb=True, name="a_out_dx")
            gw["a_w_out"] = by_rows(mm(s["on"], dh, ta=True, name="a_out_dw", out_dtype=BF16))
            do, dz, gs["a_norm"] = gnorm_bwd(s["o"], s["pm"], sm["a_norm"], don, name="a_gnorm_bwd")
            dprep = delta_scan_bwd(do, *s["prep"], s["s_in"], name="a_scan_bwd")
            dc, dbg = delta_prep_bwd(s["c"], s["bg"], dprep, name="a_prep_bwd")
            (dpt,), (galog, gdtb) = tile_vjp(_f_betag, [(s["pm"], LANES, 4 * hk // LANES)], [alog_p, dtb_p], [(dbg, LANES, 0)], n_diff=1,
                                             tm=rtm, ncol=1, name="a_betag_bwd", grad_dtypes=[BF16])
            gs["a_log"] = galog[:, N_HEADS_A:2 * N_HEADS_A]
            gs["a_dt_bias"] = gdtb[:, N_HEADS_A:2 * N_HEADS_A]
            dqkv, gs["a_conv"] = conv_bwd(dc, s["pm"], cache["a_conv", i], name="a_conv_bwd")
            dpm = jnp.concatenate([dqkv, dz, dpt], axis=1)
            g_in = mm(s["hn"], dpm, ta=True, name="a_in_dw", out_dtype=BF16)[:, :4 * hk + 2 * N_HEADS_A]
            gw["a_w_in"] = g_in.reshape(D_MODEL, N_CHIPS, g_in.shape[1] // N_CHIPS).transpose(1, 0, 2)
            dh, g_mix[i] = mm(dpm, cache["a_w_in", i], name="a_in_dx", norm_grad=(s["h0"], nw("norm_mix", i), dh), **fused)
        else:
            dao = mm(dh, w_out, tb=True, name="b_out_dx")
            gw["b_w_out"] = by_rows(mm(s["ao"], dh, ta=True, name="b_out_dw", out_dtype=BF16))
            dq, dkd, dvd, gsk = swa_bwd(dao, s["pb"], s["kd"], s["vd"], sinks_p, name="b_att_bwd")
            gs["b_sinks"] = gsk[:, :N_HEADS_B]
            dpb = jnp.concatenate([dq, _undup(dkd), _undup(dvd)], axis=1)
            gw["b_w_in"] = mm(s["hn"], dpb, ta=True, name="b_in_dw", out_dtype=BF16, tn=SHARD_BIN,
                              o_shape=(N_CHIPS, D_MODEL, SHARD_BIN), o_spec=spec((None, mtm, SHARD_BIN), lambda r, j, kk: (j, r, 0)))
            dh, g_mix[i] = mm(dpb, cache["b_w_in", i], name="b_in_dx", norm_grad=(s["h0"], nw("norm_mix", i), dh), **fused,
                              **by_chip_t(D_MODEL, SHARD_BIN))
        token = on_grads(i, "mix", gw)
        if token is not None:
            zero = token[:1, :1]

    gs["norm_mix"], gs["norm_ffn"], gs["norm_ple"] = (jnp.concatenate(g, axis=0) for g in (g_mix, g_ffn, g_ple))
    gs["f_conv"] = jnp.stack(g_conv)
    return loss, dh, gs


BIG = ["a_w_in", "a_w_out", "b_w_in", "b_w_out", "f_w_up", "f_w_down", "ple_w_proj", "ple_w_gate"]
LAYERED = {"f_w_up", "f_w_down", "ple_w_proj", "ple_w_gate"}
BY_CHIP = {"b_w_in", "f_w_up", "ple_w_proj"}
LAYER_UNITS = [[("a_w_in", 0), ("a_w_out", 0)] + [(n, 0) for n in sorted(LAYERED)],
               [("b_w_in", 1), ("b_w_out", 1)] + [(n, 1) for n in sorted(LAYERED)]]
CONVS = ["a_conv", "f_conv"]
SMALL = ["norm_mix", "norm_ffn", "norm_ple", "norm_final", "a_log", "a_dt_bias", "a_norm", "b_sinks"]
SMALL_ROWS = 8
CONV_ROWS = 16
CONV_GRAD_ROWS = 48


def _pack_rows(arrs, rows, dtype):
    flat = jnp.concatenate([a.reshape(-1).astype(dtype) for a in arrs])
    return jnp.pad(flat, (0, rows * PACK_COLS - flat.shape[0])).reshape(rows, PACK_COLS)


def _unpack(flat, shapes):
    out, off = [], 0
    for shp in shapes:
        n = math.prod(shp)
        out.append(flat[off:off + n].reshape(shp))
        off += n
    return out


def _pack_small(d, loss=None):
    tail = jnp.concatenate([d["a_log"].reshape(-1), d["a_dt_bias"].reshape(-1), d["a_norm"].reshape(-1), d["b_sinks"].reshape(-1)])
    if loss is not None:
        tail = jnp.concatenate([tail, loss.reshape(-1)[:1]])
    tail = jnp.pad(tail, (0, PACK_COLS - tail.shape[0]))
    return jnp.concatenate([d["norm_mix"], d["norm_ffn"], d["norm_ple"], d["norm_final"][None, :], tail[None, :]], axis=0)


def _unpack_small(a, like):
    out = {"norm_mix": a[0:2], "norm_ffn": a[2:4], "norm_ple": a[4:6], "norm_final": a[6]}
    off = 0
    for nm in ("a_log", "a_dt_bias", "a_norm", "b_sinks"):
        n = like[nm].size
        out[nm] = a[7, off:off + n].reshape(like[nm].shape)
        off += n
    return out, a[7, off]


def _as2d(a):
    return a.reshape(-1, a.shape[-1])


def kernel(x, p, norm_mix, norm_ffn, norm_ple, norm_final, a_w_in, a_conv, a_log, a_dt_bias, a_norm, a_w_out, b_w_in, b_sinks, b_w_out, f_w_up, f_conv, f_w_down, ple_w_proj, ple_w_gate, loss_target, m_norm_mix, m_norm_ffn, m_norm_ple, m_norm_final, m_a_w_in, m_a_conv, m_a_log, m_a_dt_bias, m_a_norm, m_a_w_out, m_b_w_in, m_b_sinks, m_b_w_out, m_f_w_up, m_f_conv, m_f_w_down, m_ple_w_proj, m_ple_w_gate, v_norm_mix, v_norm_ffn, v_norm_ple, v_norm_final, v_a_w_in, v_a_conv, v_a_log, v_a_dt_bias, v_a_norm, v_a_w_out, v_b_w_in, v_b_sinks, v_b_w_out, v_f_w_up, v_f_conv, v_f_w_down, v_ple_w_proj, v_ple_w_gate):
    w = dict(norm_mix=norm_mix, norm_ffn=norm_ffn, norm_ple=norm_ple, norm_final=norm_final, a_w_in=a_w_in, a_conv=a_conv,
             a_log=a_log, a_dt_bias=a_dt_bias, a_norm=a_norm, a_w_out=a_w_out, b_w_in=b_w_in, b_sinks=b_sinks, b_w_out=b_w_out,
             f_w_up=f_w_up, f_conv=f_conv, f_w_down=f_w_down, ple_w_proj=ple_w_proj, ple_w_gate=ple_w_gate)
    m = dict(norm_mix=m_norm_mix, norm_ffn=m_norm_ffn, norm_ple=m_norm_ple, norm_final=m_norm_final, a_w_in=m_a_w_in,
             a_conv=m_a_conv, a_log=m_a_log, a_dt_bias=m_a_dt_bias, a_norm=m_a_norm, a_w_out=m_a_w_out, b_w_in=m_b_w_in,
             b_sinks=m_b_sinks, b_w_out=m_b_w_out, f_w_up=m_f_w_up, f_conv=m_f_conv, f_w_down=m_f_w_down,
             ple_w_proj=m_ple_w_proj, ple_w_gate=m_ple_w_gate)
    v = dict(norm_mix=v_norm_mix, norm_ffn=v_norm_ffn, norm_ple=v_norm_ple, norm_final=v_norm_final, a_w_in=v_a_w_in,
             a_conv=v_a_conv, a_log=v_a_log, a_dt_bias=v_a_dt_bias, a_norm=v_a_norm, a_w_out=v_a_w_out, b_w_in=v_b_w_in,
             b_sinks=v_b_sinks, b_w_out=v_b_w_out, f_w_up=v_f_w_up, f_conv=v_f_conv, f_w_down=v_f_w_down,
             ple_w_proj=v_ple_w_proj, ple_w_gate=v_ple_w_gate)
    xc, yc, cc = _place()
    my_chip = 2 * xc + yc

    shard = {(n, i): w[n][i if n in LAYERED else 0].astype(BF16) for n, i in LAYER_UNITS[0] + LAYER_UNITS[1]}
    first = shard["a_w_in", 0]
    (ga,) = gather_units([(first, False)], name="gather_first")
    ga = lax.dynamic_update_index_in_dim(ga, first, my_chip, 0)
    a_in = jnp.concatenate([ga[j] for j in range(N_CHIPS)], axis=1)
    n_main = 4 * N_HEADS_A * HEAD_DIM_A
    conv_shapes = [w[n].shape for n in CONVS]
    convs = allgather8(_pack_rows([w[n] for n in CONVS], CONV_ROWS, F32), name="gather_convs")
    conv_parts = [_unpack(convs[2 * j].reshape(-1), conv_shapes) for j in range(N_CHIPS)]
    a_conv_full, f_conv_full = (jnp.concatenate([conv_parts[j][q] for j in range(N_CHIPS)], axis=2) for q in range(2))
    ready = {("a_w_in", 0): jnp.pad(a_in, ((0, 0), (0, n_main + LANES - a_in.shape[1]))), ("a_conv", 0): a_conv_full[0], ("f_conv", 0): f_conv_full[0], ("f_conv", 1): f_conv_full[1]}
    later = [[k for k in units if k != ("a_w_in", 0)] for units in LAYER_UNITS]
    pending, after = [], ga
    for layer, keys in enumerate(later):
        pending.append(gather_start([shard[k] for k in keys], after, name=f"gather_start{layer}"))
        after = pending[-1][4]
    sm = {n: w[n] for n in SMALL}
    sm["norm_mix"] = sm["norm_mix"] + after[:1, :1]

    def weight(name, layer, act):
        if (name, layer) not in ready:
            landed = gather_wait(pending[layer], act, name=f"gather_wait{layer}")
            for k, g in zip(later[layer], landed):
                g = lax.dynamic_update_index_in_dim(g, shard[k], my_chip, 0)
                ready[k] = g if k[0] in BY_CHIP else g.reshape(N_CHIPS * g.shape[1], g.shape[2])
        return ready[name, layer]

    pairs, scattered, started = {}, {}, []

    def on_grads(layer, part, gw):
        keys = [k for k in LAYER_UNITS[layer] if (k[0] in LAYERED) == (part == "ffn")]
        from_sib = swap_units([gw[n] for n, _ in keys], name=f"rs_swap_{part}{layer}")
        for (n, _), sib in zip(keys, from_sib):
            pairs[n, layer] = pair_add(gw[n], sib, name=f"rs_add_pair_{n}{layer}")
        started.append((keys, scatter_start([pairs[k] for k in keys], name=f"rs_scatter_start_{part}{layer}"), f"{part}{layer}"))
        return started[-1][1][4]

    loss, grad_x, gs = local_step(x[0], p[:, 0], loss_target[0], sm, weight, on_grads)

    grads, delta, new_m, new_v, g_unit = {}, {}, {}, {}, {}

    def finish(keys, tag):
        halves = [chips_add(pairs[k], scattered[k], name=f"rs_add_chips_{k[0]}{k[1]}") for k in keys]
        g_unit.update(zip(keys, zip(halves, join_units(halves, name=f"rs_join_{tag}"))))
        for n in BIG:
            mine = [(n, i) for i in range(DEPTH) if (n, i) in LAYER_UNITS[i]]
            if n not in delta and all(k in g_unit for k in mine):
                g_layers = [g_unit[k] for k in mine]
                shape3 = (len(g_layers), 2 * g_layers[0][0].shape[0], g_layers[0][0].shape[1])
                res = adamw_layers(w[n].reshape(shape3), g_layers, m[n].reshape(shape3), v[n].reshape(shape3), name=f"adamw_{n}")
                delta[n], new_m[n], new_v[n], grads[n] = (r.reshape(w[n].shape) for r in res)

    last_keys, last_pending, last_tag = started[-1]
    for keys, pend, tag in started[:-1]:
        scattered.update(zip(keys, scatter_wait(pend, last_pending[4], name=f"rs_scatter_wait_{tag}")))
    finish([k for keys, _, _ in started[:-1] for k in keys], "first")

    conv_grads = _pack_rows([gs[n] for n in CONVS], CONV_GRAD_ROWS, F32)
    small_sum = sum8(allgather8(jnp.concatenate([_pack_small(gs, loss), conv_grads]), name="gather_small"), name="sum_small")
    g_sm, loss_sum = _unpack_small(small_sum[:SMALL_ROWS], sm)

    scattered.update(zip(last_keys, scatter_wait(last_pending, small_sum, name=f"rs_scatter_wait_{last_tag}")))
    finish(last_keys, "last")

    for n, full in zip(CONVS, _unpack(small_sum[SMALL_ROWS:].reshape(-1), [gs[n].shape for n in CONVS])):
        g2 = _as2d(lax.dynamic_slice_in_dim(full, my_chip * w[n].shape[-1], w[n].shape[-1], axis=full.ndim - 1))
        d2, m2, v2 = adamw(_as2d(w[n]), g2, _as2d(m[n]), _as2d(v[n]), name=f"adamw_{n}")
        grads[n], delta[n], new_m[n], new_v[n] = (r.reshape(w[n].shape) for r in (g2, d2, m2, v2))
    pk = lambda d: _pack_small(d)
    d2, m2, v2 = adamw(pk(sm), pk(g_sm), pk({n: m[n] for n in SMALL}), pk({n: v[n] for n in SMALL}), name="adamw_small")
    for src, dst in ((d2, delta), (m2, new_m), (v2, new_v)):
        dst.update(_unpack_small(src, sm)[0])
    grads.update(g_sm)

    order = ["norm_mix", "norm_ffn", "norm_ple", "norm_final", "a_w_in", "a_conv", "a_log", "a_dt_bias", "a_norm", "a_w_out",
             "b_w_in", "b_sinks", "b_w_out", "f_w_up", "f_conv", "f_w_down", "ple_w_proj", "ple_w_gate"]
    return (loss_sum, grad_x[None], *[grads[n] for n in order], *[delta[n] for n in order],
            *[new_m[n] for n in order], *[new_v[n] for n in order])
```

```python
import functools
import math

import jax
import jax.numpy as jnp
from jax import lax
from jax.experimental import pallas as pl
from jax.experimental.pallas import tpu as pltpu

F32 = jnp.float32
BF16 = jnp.bfloat16
MESH = pl.DeviceIdType.MESH

D_MODEL = 1024
N_HEADS_A = 8
HEAD_DIM_A = 128
CONV_A = 4
N_HEADS_B = 16
N_KV_B = 4
HEAD_DIM_B = 64
WINDOW = 128
D_FF = 2816
FFN_CONV = 3
PLE_DIM = 256
EPS = 1e-6
DEPTH = 2

ADAM_LR = 0.001
ADAM_B1 = 0.9
ADAM_B2 = 0.999
ADAM_EPS = 1e-08
ADAM_WD = 0.01
ADAM_STEP = 10

LANES = 128
SUBLANES = 8
BF16_ROWS = 16
CHUNK = 128
VMEM_LIMIT = 56 * 1024 * 1024
NEG = -1e30
N_CHIPS = 4
N_DEV = 8
PACK_COLS = 1024


def _params(sem=None):
    return pltpu.CompilerParams(dimension_semantics=sem, vmem_limit_bytes=VMEM_LIMIT)


def _tile(dim, cap):
    if dim % LANES:
        return dim
    best = LANES
    for t in range(LANES, min(dim, cap) + 1, LANES):
        if dim % t == 0:
            best = t
    return best


def _dot(a, b, dims=(((1,), (0,)), ((), ())), precision=None):
    return lax.dot_general(a, b, dims, precision=precision, preferred_element_type=F32)


NN = (((1,), (0,)), ((), ()))
NT = (((1,), (1,)), ((), ()))
TN = (((0,), (0,)), ((), ()))


MM_TM_CAP = 1024
MM_TK_CAP_TOKENS = 2048


def mm(a, b, *, name, ta=False, tb=False, out_dtype=F32, add=None, norm_w=None, norm_grad=None, tm_cap=MM_TM_CAP, tn_cap=1408,
       tk_cap=1408, n=None, tn=None, tk=None, b_spec=None, o_spec=None, o_shape=None, into=None):
    m, k = (a.shape[1], a.shape[0]) if ta else a.shape
    if b_spec is None:
        n = b.shape[0] if tb else b.shape[1]
        assert (b.shape[1] if tb else b.shape[0]) == k, (a.shape, b.shape, ta, tb)
    tm, tn, tk = _tile(m, tm_cap), tn or _tile(n, tn_cap), tk or _tile(k, MM_TK_CAP_TOKENS if ta else tk_cap)
    assert n % tn == 0 and k % tk == 0, (n, tn, k, tk)
    nk = k // tk
    dims = (((0 if ta else 1,), (1 if tb else 0,)), ((), ()))
    has_add, has_norm, has_grad = add is not None, norm_w is not None, norm_grad is not None
    assert not (has_norm or has_grad) or (tn == n and o_spec is None), "the norm epilogues need whole rows"
    n_in = 2 + has_add + has_norm + 3 * has_grad + (into is not None)

    def body(*refs):
        a_ref, b_ref = refs[0], refs[1]
        add_ref = refs[2] if has_add else None
        o_ref = refs[n_in]
        part = _dot(a_ref[...].astype(BF16), b_ref[...].astype(BF16), dims)
        first = pl.program_id(0) == 0

        def finish(r):
            if has_add:
                r = r + add_ref[...].astype(F32)
            if has_grad:
                h_ref, w_ref, prev_ref = refs[2 + has_add:5 + has_add]
                _, vjp = jax.vjp(_f_norm, h_ref[...], w_ref[...])
                r, dw = vjp(r)
                r = r + prev_ref[...]

                @pl.when(first)
                def _():
                    refs[n_in + 1][...] = dw

                @pl.when(jnp.logical_not(first))
                def _():
                    refs[n_in + 1][...] += dw
            o_ref[...] = r.astype(o_ref.dtype)
            if has_norm:
                refs[n_in + 1][...] = _f_norm(r, refs[2 + has_add][...]).astype(BF16)

        if nk == 1:
            finish(part)
            return
        acc = refs[-1]
        kk = pl.program_id(2)

        @pl.when(kk == 0)
        def _():
            acc[...] = part

        @pl.when(kk > 0)
        def _():
            acc[...] += part

        @pl.when(kk == nk - 1)
        def _():
            finish(acc[...])

    a_spec = pl.BlockSpec((tk, tm), lambda i, j, kk: (kk, i)) if ta else pl.BlockSpec((tm, tk), lambda i, j, kk: (i, kk))
    if b_spec is None:
        b_spec = pl.BlockSpec((tn, tk), lambda i, j, kk: (j, kk)) if tb else pl.BlockSpec((tk, tn), lambda i, j, kk: (kk, j))
    plain_o = pl.BlockSpec((tm, tn), lambda i, j, kk: (i, j))
    if o_spec is None:
        o_spec, o_shape = plain_o, (m, n)
    in_specs = [a_spec, b_spec] + ([plain_o] if has_add else [])
    args = (a, b) + ((add,) if has_add else ())
    out_specs, out_shapes = o_spec, jax.ShapeDtypeStruct(tuple(o_shape), out_dtype)
    one_row = pl.BlockSpec((1, n), lambda i, j, kk: (0, 0))
    if has_norm:
        in_specs.append(one_row)
        args += (norm_w,)
        out_specs, out_shapes = [o_spec, plain_o], [out_shapes, jax.ShapeDtypeStruct((m, n), BF16)]
    if has_grad:
        assert not has_norm
        in_specs += [plain_o, one_row, plain_o]
        args += tuple(norm_grad)
        out_specs, out_shapes = [o_spec, one_row], [out_shapes, jax.ShapeDtypeStruct((1, n), F32)]
    aliases = {}
    if into is not None:
        assert into.shape == tuple(o_shape) and into.dtype == out_dtype, (into.shape, o_shape)
        in_specs.append(pl.BlockSpec(memory_space=pl.ANY))
        args += (into,)
        aliases = {n_in - 1: 0}
    return pl.pallas_call(
        body, grid=(m // tm, n // tn, nk), in_specs=in_specs, out_specs=out_specs,
        out_shape=out_shapes, name=name, input_output_aliases=aliases,
        scratch_shapes=[pltpu.VMEM((tm, tn), F32)] if nk > 1 else [],
        compiler_params=_params(("arbitrary" if has_grad else "parallel", "parallel", "arbitrary")),
    )(*args)


def _row_spec(tm, cw, coff):
    return pl.BlockSpec((tm, cw), lambda i, j: (i, j + coff))


def _full_spec(shape):
    return pl.BlockSpec(shape, lambda i, j: (0,) * len(shape))


def tile_map(fn, rows, params, outs, *, tm, ncol, name):
    t = rows[0][0].shape[0]
    nin = len(rows) + len(params)

    def body(*refs):
        res = fn(*[r[...] for r in refs[:nin]])
        res = res if isinstance(res, (tuple, list)) else (res,)
        for o_ref, r in zip(refs[nin:], res):
            o_ref[...] = r.astype(o_ref.dtype)

    in_specs = [_row_spec(tm, cw, coff) for (_, cw, coff) in rows] + [_full_spec(p.shape) for p in params]
    res = pl.pallas_call(
        body, grid=(t // tm, ncol), in_specs=in_specs,
        out_specs=[_row_spec(tm, cw, 0) for (cw, _) in outs],
        out_shape=[jax.ShapeDtypeStruct((t, cw * ncol), dt) for (cw, dt) in outs], name=name,
        compiler_params=_params(("parallel", "parallel")),
    )(*[r[0] for r in rows], *params)
    return res


def tile_vjp(fn, rows, params, cts, *, n_diff, tm, ncol, name, grad_dtypes=None):
    t = rows[0][0].shape[0]
    nr, npar, nct = len(rows), len(params), len(cts)

    def body(*refs):
        vals = [r[...] for r in refs[:nr + npar + nct]]
        diff, rest, pars = vals[:n_diff], vals[n_diff:nr], vals[nr:nr + npar]
        ctv = vals[nr + npar:nr + npar + nct]
        outs_ref = refs[nr + npar + nct:]

        def f(*a):
            res = fn(*a[:n_diff], *rest, *a[n_diff:])
            return tuple(res) if isinstance(res, (tuple, list)) else (res,)

        primal, vjp = jax.vjp(f, *[d.astype(F32) for d in diff], *pars)
        grads = vjp(tuple(c.astype(o.dtype) for c, o in zip(ctv, primal)))
        for q in range(n_diff):
            outs_ref[q][...] = grads[q].astype(outs_ref[q].dtype)
        first = (pl.program_id(0) == 0) & (pl.program_id(1) == 0)
        for q in range(npar):
            o_ref, g = outs_ref[n_diff + q], grads[n_diff + q]

            @pl.when(first)
            def _(o_ref=o_ref, g=g):
                o_ref[...] = g

            @pl.when(jnp.logical_not(first))
            def _(o_ref=o_ref, g=g):
                o_ref[...] += g

    in_specs = [_row_spec(tm, cw, coff) for (_, cw, coff) in rows] + [_full_spec(p.shape) for p in params]
    in_specs += [_row_spec(tm, cw, coff) for (_, cw, coff) in cts]
    args = [r[0] for r in rows] + list(params) + [c[0] for c in cts]
    out_specs = [_row_spec(tm, rows[q][1], 0) for q in range(n_diff)] + [_full_spec(p.shape) for p in params]
    grad_dtypes = grad_dtypes or [F32] * n_diff
    out_shape = [jax.ShapeDtypeStruct((t, rows[q][1] * ncol), grad_dtypes[q]) for q in range(n_diff)]
    out_shape += [jax.ShapeDtypeStruct(p.shape, F32) for p in params]
    res = pl.pallas_call(
        body, grid=(t // tm, ncol), in_specs=in_specs, out_specs=out_specs, out_shape=out_shape, name=name,
        compiler_params=_params(("arbitrary", "arbitrary")),
    )(*args)
    return res[:n_diff], res[n_diff:]


def _silu(x):
    return x * jax.nn.sigmoid(x)


def _f_norm(h, w):
    return h * lax.rsqrt(jnp.mean(h * h, axis=-1, keepdims=True) + EPS) * w


def _f_gnorm(o, z, w):
    return _f_norm(o, w) * _silu(z)


def _f_act(gate, val):
    return _silu(gate) * val


def _f_ple(gl, pe):
    return jax.nn.sigmoid(gl) * pe


def _f_betag(pt, alog, dtb):
    lane = lax.broadcasted_iota(jnp.int32, (1, LANES), 1)
    z = pt + dtb
    softplus = jnp.maximum(z, 0.0) + jnp.log(1.0 + jnp.exp(-jnp.abs(z)))
    g = -jnp.exp(alog) * softplus
    return jnp.where(lane < N_HEADS_A, jax.nn.sigmoid(pt), jnp.where(lane < 2 * N_HEADS_A, g, 0.0))


CONV_TM = 256
CONV_CW = 1024


def _shift_down(x, prev, s, row):
    rp = jnp.tile(pltpu.roll(prev, s, 0), (x.shape[0] // SUBLANES, 1))
    return jnp.where(row < s, rp, pltpu.roll(x, s, 0))


def _shift_up(x, nxt, s, row):
    tm = x.shape[0]
    rn = jnp.tile(pltpu.roll(nxt, SUBLANES - s, 0), (tm // SUBLANES, 1))
    return jnp.where(row >= tm - s, rn, pltpu.roll(x, tm - s, 0))


def _conv_taps(x, prev, w_ref, cols, row):
    k = w_ref.shape[0]
    y = x * w_ref[pl.ds(k - 1, 1), cols]
    for s in range(1, k):
        y = y + _shift_down(x, prev, s, row) * w_ref[pl.ds(k - 1 - s, 1), cols]
    return y


def _lane_chunks(cw):
    return [slice(cb * LANES, (cb + 1) * LANES) for cb in range(cw // LANES)]


def conv_fwd(x, w, *, name):
    t = x.shape[0]
    k, c = w.shape
    tm, cw = min(CONV_TM, t), CONV_CW
    nb8 = tm // SUBLANES

    def body(x_ref, p_ref, w_ref, o_ref):
        first = pl.program_id(1) == 0
        row = lax.broadcasted_iota(jnp.int32, (tm, LANES), 0)
        for cols in _lane_chunks(cw):
            o_ref[:, cols] = _conv_taps(x_ref[:, cols], jnp.where(first, 0.0, p_ref[:, cols]), w_ref, cols, row)

    return pl.pallas_call(
        body, grid=(c // cw, t // tm),
        in_specs=[pl.BlockSpec((tm, cw), lambda j, i: (i, j)),
                  pl.BlockSpec((SUBLANES, cw), lambda j, i: (jnp.maximum(i * nb8 - 1, 0), j)),
                  pl.BlockSpec((k, cw), lambda j, i: (0, j))],
        out_specs=pl.BlockSpec((tm, cw), lambda j, i: (i, j)),
        out_shape=jax.ShapeDtypeStruct((t, c), F32), name=name,
        compiler_params=_params(("parallel", "parallel")),
    )(x, x, w)


def conv_bwd(dy, x, w, *, name):
    t = x.shape[0]
    k, c = w.shape
    tm, cw = min(CONV_TM, t), CONV_CW
    nb8 = tm // SUBLANES
    ni = t // tm

    def body(dy_ref, dn_ref, x_ref, p_ref, w_ref, dx_ref, dw_ref):
        i = pl.program_id(1)
        first, last = i == 0, i == ni - 1
        row = lax.broadcasted_iota(jnp.int32, (tm, LANES), 0)
        for cols in _lane_chunks(cw):
            dyv, xv = dy_ref[:, cols], x_ref[:, cols]
            nxt = jnp.where(last, 0.0, dn_ref[:, cols])
            prev = jnp.where(first, 0.0, p_ref[:, cols])
            dx = dyv * w_ref[pl.ds(k - 1, 1), cols]
            dws = [jnp.sum(dyv * xv, axis=0, keepdims=True)]
            for s in range(1, k):
                dx = dx + _shift_up(dyv, nxt, s, row) * w_ref[pl.ds(k - 1 - s, 1), cols]
                dws.append(jnp.sum(dyv * _shift_down(xv, prev, s, row), axis=0, keepdims=True))
            dx_ref[:, cols] = dx.astype(dx_ref.dtype)
            for s in range(k):
                @pl.when(first)
                def _(s=s, dws=dws, cols=cols):
                    dw_ref[pl.ds(k - 1 - s, 1), cols] = dws[s]

                @pl.when(jnp.logical_not(first))
                def _(s=s, dws=dws, cols=cols):
                    dw_ref[pl.ds(k - 1 - s, 1), cols] += dws[s]

    return pl.pallas_call(
        body, grid=(c // cw, ni),
        in_specs=[pl.BlockSpec((tm, cw), lambda j, i: (i, j)),
                  pl.BlockSpec((SUBLANES, cw), lambda j, i: (jnp.minimum((i + 1) * nb8, t // SUBLANES - 1), j)),
                  pl.BlockSpec((tm, cw), lambda j, i: (i, j)),
                  pl.BlockSpec((SUBLANES, cw), lambda j, i: (jnp.maximum(i * nb8 - 1, 0), j)),
                  pl.BlockSpec((k, cw), lambda j, i: (0, j))],
        out_specs=[pl.BlockSpec((tm, cw), lambda j, i: (i, j)), pl.BlockSpec((k, cw), lambda j, i: (0, j))],
        out_shape=[jax.ShapeDtypeStruct((t, c), BF16), jax.ShapeDtypeStruct((k, c), F32)], name=name,
        compiler_params=_params(("parallel", "arbitrary")),
    )(dy, dy, x, x, w)


FFN_TM = 128
FFN_CW = D_FF // 2


def _ffn_specs(t, tm, cw, k):
    ncol = D_FF // cw
    cur = lambda off: pl.BlockSpec((tm, cw), lambda j, i: (i, j + off))
    prev = lambda off, hr: pl.BlockSpec((hr, cw), lambda j, i: (jnp.maximum(i * (tm // hr) - 1, 0), j + off))
    nxt = lambda off, hr: pl.BlockSpec((hr, cw), lambda j, i: (jnp.minimum((i + 1) * (tm // hr), t // hr - 1), j + off))
    taps = lambda off: pl.BlockSpec((k, cw), lambda j, i: (0, j + off))
    return cur, prev, nxt, taps, ncol


def _rows_before(ref, cols, first):
    return jnp.where(first, 0.0, ref[ref.shape[0] - SUBLANES:, cols].astype(F32))


def conv_act_fwd(u, w, *, name):
    t, k = u.shape[0], w.shape[0]
    tm, cw = min(FFN_TM, t), FFN_CW
    cur, prev, _, taps, ncol = _ffn_specs(t, tm, cw, k)

    def body(ug_ref, pg_ref, uv_ref, pv_ref, wg_ref, wv_ref, o_ref):
        first = pl.program_id(1) == 0
        row = lax.broadcasted_iota(jnp.int32, (tm, LANES), 0)
        for cb in range(cw // LANES):
            cols = slice(cb * LANES, (cb + 1) * LANES)
            cg = _conv_taps(ug_ref[:, cols].astype(F32), _rows_before(pg_ref, cols, first), wg_ref, cols, row)
            cv = _conv_taps(uv_ref[:, cols].astype(F32), _rows_before(pv_ref, cols, first), wv_ref, cols, row)
            o_ref[:, cols] = _f_act(cg, cv).astype(o_ref.dtype)

    return pl.pallas_call(
        body, grid=(ncol, t // tm),
        in_specs=[cur(0), prev(0, BF16_ROWS), cur(ncol), prev(ncol, BF16_ROWS), taps(0), taps(ncol)],
        out_specs=cur(0), out_shape=jax.ShapeDtypeStruct((t, D_FF), BF16), name=name,
        compiler_params=_params(("parallel", "parallel")),
    )(u, u, u, u, w, w)


def conv_act_bwd(u, dact, w, *, name):
    t, k = u.shape[0], w.shape[0]
    tm, cw = min(FFN_TM, t), FFN_CW
    cur, prev, nxt, taps, ncol = _ffn_specs(t, tm, cw, k)
    ni = t // tm

    def body(ug_ref, pg_ref, ng_ref, uv_ref, pv_ref, nv_ref, d_ref, dn_ref, wg_ref, wv_ref, dg_ref, dv_ref, dwg_ref, dwv_ref):
        i = pl.program_id(1)
        first, last = i == 0, i == ni - 1
        row = lax.broadcasted_iota(jnp.int32, (tm, LANES), 0)
        row8 = lax.broadcasted_iota(jnp.int32, (SUBLANES, LANES), 0)
        for cb in range(cw // LANES):
            cols = slice(cb * LANES, (cb + 1) * LANES)
            ug, uv = ug_ref[:, cols].astype(F32), uv_ref[:, cols].astype(F32)
            pg, pv = _rows_before(pg_ref, cols, first), _rows_before(pv_ref, cols, first)
            sg = [ug] + [_shift_down(ug, pg, s, row) for s in range(1, k)]
            sv = [uv] + [_shift_down(uv, pv, s, row) for s in range(1, k)]
            taps = lambda xs, w_ref: sum(xs[s] * w_ref[pl.ds(k - 1 - s, 1), cols] for s in range(k))
            _, vjp = jax.vjp(_f_act, taps(sg, wg_ref), taps(sv, wv_ref))
            dcg, dcv = vjp(d_ref[:, cols])
            after = lambda ref: ref[:SUBLANES, cols].astype(F32)
            _, vjp_n = jax.vjp(_f_act, _conv_taps(after(ng_ref), ug[tm - SUBLANES:], wg_ref, cols, row8),
                               _conv_taps(after(nv_ref), uv[tm - SUBLANES:], wv_ref, cols, row8))
            dcgn, dcvn = vjp_n(jnp.where(last, 0.0, dn_ref[:, cols]))
            for dc, dcn, xs, w_ref, dx_ref, dw_ref in ((dcg, dcgn, sg, wg_ref, dg_ref, dwg_ref),
                                                       (dcv, dcvn, sv, wv_ref, dv_ref, dwv_ref)):
                dx = dc * w_ref[pl.ds(k - 1, 1), cols]
                dws = [jnp.sum(dc * xs[0], axis=0, keepdims=True)]
                for s in range(1, k):
                    dx = dx + _shift_up(dc, dcn, s, row) * w_ref[pl.ds(k - 1 - s, 1), cols]
                    dws.append(jnp.sum(dc * xs[s], axis=0, keepdims=True))
                dx_ref[:, cols] = dx.astype(dx_ref.dtype)
                for s in range(k):
                    @pl.when(first)
                    def _(s=s, dw_ref=dw_ref, dws=dws):
                        dw_ref[pl.ds(k - 1 - s, 1), cols] = dws[s]

                    @pl.when(jnp.logical_not(first))
                    def _(s=s, dw_ref=dw_ref, dws=dws):
                        dw_ref[pl.ds(k - 1 - s, 1), cols] += dws[s]

    half = jax.ShapeDtypeStruct((t, D_FF), BF16)
    dwh = jax.ShapeDtypeStruct((k, D_FF), F32)
    return pl.pallas_call(
        body, grid=(ncol, ni),
        in_specs=[cur(0), prev(0, BF16_ROWS), nxt(0, BF16_ROWS), cur(ncol), prev(ncol, BF16_ROWS), nxt(ncol, BF16_ROWS),
                  cur(0), nxt(0, SUBLANES), taps(0), taps(ncol)],
        out_specs=[cur(0), cur(0), taps(0), taps(0)], out_shape=[half, half, dwh, dwh], name=name,
        compiler_params=_params(("parallel", "arbitrary")),
    )(u, u, u, u, u, u, dact, dact, w, w)


def _each(f, *lists):
    return [f(*a) for a in zip(*lists)]


@jax.custom_vjp
def _inv_unit_lower(lms):
    return _inv_blocks(lms)


def _inv_blocks(lms):
    c = lms[0].shape[0]
    ri = lax.broadcasted_iota(jnp.int32, (c, c), 0)
    ci = lax.broadcasted_iota(jnp.int32, (c, c), 1)
    eye = (ri == ci).astype(F32)
    dms = _each(lambda lm: eye - jnp.where((ri >> 1) == (ci >> 1), lm, 0.0), lms)
    for lv in range(1, int(math.log2(c))):
        below = ((ri >> (lv + 1)) == (ci >> (lv + 1))) & ((ri >> lv) != (ci >> lv))
        dbs = _each(lambda dm: dm.astype(BF16), dms)
        ods = _each(lambda lm, db: _dot(jnp.where(below, lm, 0.0).astype(BF16), db).astype(BF16), lms, dbs)
        dms = _each(lambda dm, db, od: dm - _dot(db, od), dms, dbs, ods)
    return dms


def _inv_fwd(lms):
    tms = _inv_blocks(lms)
    return tms, tms


def _inv_bwd(tms, dts):
    tbs = _each(lambda tm: tm.astype(BF16), tms)
    mid = _each(lambda tb, dt: _dot(tb, dt.astype(BF16), TN).astype(BF16), tbs, dts)
    return (_each(lambda m, tb: -_dot(m, tb, NT), mid, tbs),)


_inv_unit_lower.defvjp(_inv_fwd, _inv_bwd)


@jax.custom_vjp
def _inv_known(lms, tms):
    return tms


_inv_known.defvjp(lambda lms, tms: (tms, tms), lambda tms, dts: _inv_bwd(tms, dts) + (_each(jnp.zeros_like, tms),))


def _l2n(x):
    return x * lax.rsqrt(jnp.sum(x * x, axis=-1, keepdims=True) + EPS)


def _prep_fn(cqs, cks, cvs, bg, sel_b, sel_g, tms=None):
    c = cqs[0].shape[0]
    ri = lax.broadcasted_iota(jnp.int32, (c, c), 0)
    ci = lax.broadcasted_iota(jnp.int32, (c, c), 1)
    eye = (ri == ci).astype(F32)
    incl, strict = ci <= ri, ci < ri
    last = lax.broadcasted_iota(jnp.int32, (c, 1), 0) == c - 1
    to_row = lambda col: jnp.sum(col * eye, axis=0, keepdims=True)
    qs = _each(lambda a: _l2n(_silu(a)) * (HEAD_DIM_A ** -0.5), cqs)
    ks = _each(lambda a: _l2n(_silu(a)), cks)
    vbs = _each(lambda a: _silu(a).astype(BF16), cvs)
    betas = _each(lambda m: jnp.sum(bg * m, axis=1, keepdims=True), sel_b)
    gs = _each(lambda m: jnp.sum(bg * m, axis=1, keepdims=True), sel_g)
    gcss = _each(lambda g: jnp.sum(jnp.where(incl, to_row(g), 0.0), axis=1, keepdims=True), gs)
    gtots = _each(lambda gcs: jnp.sum(jnp.where(last, gcs, 0.0), axis=0, keepdims=True), gcss)
    decays = _each(lambda gcs: jnp.exp(jnp.where(incl, gcs - to_row(gcs), NEG)), gcss)
    kbs = _each(lambda k: k.astype(BF16), ks)
    lms = _each(lambda beta, kb, dec: jnp.where(strict, beta * _dot(kb, kb, NT) * dec, 0.0), betas, kbs, decays)
    tms = _inv_unit_lower(lms) if tms is None else _inv_known(lms, tms)
    ams = _each(lambda tm, beta: (tm * to_row(beta)).astype(BF16), tms, betas)
    gams = _each(jnp.exp, gcss)
    u0s = _each(_dot, ams, vbs)
    wks = _each(lambda am, gam, k: _dot(am, (gam * k).astype(BF16)), ams, gams, ks)
    qks = _each(lambda q, kb, dec: _dot(q.astype(BF16), kb, NT) * dec, qs, kbs, decays)
    qds = _each(lambda q, gam: q * gam, qs, gams)
    kds = _each(lambda k, gtot, gcs: k * jnp.exp(gtot - gcs), ks, gtots, gcss)
    gls = _each(lambda gtot: jnp.exp(gtot) * jnp.ones((SUBLANES, LANES), F32), gtots)
    return u0s, wks, qds, kds, qks, gls, tms


def _head_masks(h):
    lane = lax.broadcasted_iota(jnp.int32, (1, LANES), 1)
    return (lane == h).astype(F32), (lane == h + N_HEADS_A).astype(F32)


def _hsl(j):
    return slice(j * HEAD_DIM_A, (j + 1) * HEAD_DIM_A)


def gnorm_fwd(o, zsrc, w, *, name):
    t, width = o.shape
    tm = min(256, t)
    zoff = zsrc.shape[1] // width - 1

    def body(o_ref, z_ref, w_ref, out_ref):
        for h in range(N_HEADS_A):
            out_ref[:, _hsl(h)] = _f_gnorm(o_ref[:, _hsl(h)], z_ref[:, _hsl(h)], w_ref[...]).astype(out_ref.dtype)

    rows = pl.BlockSpec((tm, width), lambda i: (i, 0))
    return pl.pallas_call(
        body, grid=(t // tm,),
        in_specs=[rows, pl.BlockSpec((tm, width), lambda i: (i, zoff)), pl.BlockSpec(w.shape, lambda i: (0, 0))],
        out_specs=rows, out_shape=jax.ShapeDtypeStruct((t, width), BF16), name=name, compiler_params=_params(("parallel",)),
    )(o, zsrc, w)


def gnorm_bwd(o, zsrc, w, don, *, name):
    t, width = o.shape
    tm = min(256, t)
    zoff = zsrc.shape[1] // width - 1

    def body(o_ref, z_ref, w_ref, d_ref, do_ref, dz_ref, dw_ref):
        dw = jnp.zeros(w.shape, F32)
        for h in range(N_HEADS_A):
            _, vjp = jax.vjp(_f_gnorm, o_ref[:, _hsl(h)], z_ref[:, _hsl(h)], w_ref[...])
            do, dz, dwh = vjp(d_ref[:, _hsl(h)])
            do_ref[:, _hsl(h)] = do.astype(do_ref.dtype)
            dz_ref[:, _hsl(h)] = dz.astype(dz_ref.dtype)
            dw = dw + dwh
        first = pl.program_id(0) == 0

        @pl.when(first)
        def _():
            dw_ref[...] = dw

        @pl.when(jnp.logical_not(first))
        def _():
            dw_ref[...] += dw

    rows = pl.BlockSpec((tm, width), lambda i: (i, 0))
    wspec = pl.BlockSpec(w.shape, lambda i: (0, 0))
    return pl.pallas_call(
        body, grid=(t // tm,),
        in_specs=[rows, pl.BlockSpec((tm, width), lambda i: (i, zoff)), wspec, rows],
        out_specs=[rows, rows, wspec],
        out_shape=[jax.ShapeDtypeStruct((t, width), BF16)] * 2 + [jax.ShapeDtypeStruct(w.shape, F32)], name=name,
        compiler_params=_params(("arbitrary",)),
    )(o, zsrc, w, don)


def delta_prep(cqkv, bg, *, name):
    t = cqkv.shape[0]
    nh, hd, n = N_HEADS_A, HEAD_DIM_A, t // CHUNK

    def body(cq_ref, ck_ref, cv_ref, bg_ref, u0_ref, wk_ref, qd_ref, kd_ref, qk_ref, tm_ref, gl_ref):
        heads = range(nh)
        masks = [_head_masks(j) for j in heads]
        res = _prep_fn([cq_ref[:, _hsl(j)] for j in heads], [ck_ref[:, _hsl(j)] for j in heads],
                       [cv_ref[:, _hsl(j)] for j in heads], bg_ref[...], [m[0] for m in masks], [m[1] for m in masks])
        for o_ref, rs in zip((u0_ref, wk_ref, qd_ref, kd_ref, qk_ref, tm_ref), res[:5] + (res[6],)):
            for j in heads:
                o_ref[:, _hsl(j)] = rs[j]
        for j in heads:
            gl_ref[j * SUBLANES:(j + 1) * SUBLANES, :] = res[5][j]

    blk = lambda off: pl.BlockSpec((CHUNK, nh * hd), lambda i: (i, off))
    res = pl.pallas_call(
        body, grid=(n,),
        in_specs=[blk(0), blk(1), blk(2), pl.BlockSpec((CHUNK, LANES), lambda i: (i, 0))],
        out_specs=[blk(0)] * 6 + [pl.BlockSpec((nh * SUBLANES, LANES), lambda i: (i, 0))],
        out_shape=[jax.ShapeDtypeStruct((t, nh * hd), F32)] * 6 + [jax.ShapeDtypeStruct((n * nh * SUBLANES, LANES), F32)],
        name=name, compiler_params=_params(("parallel",)),
    )(cqkv, cqkv, cqkv, bg)
    return [*res[:5], res[6]], res[5]


def delta_prep_bwd(cqkv, bg, tms, cts, *, name):
    t = cqkv.shape[0]
    nh, hd, n = N_HEADS_A, HEAD_DIM_A, t // CHUNK

    def body(cq_ref, ck_ref, cv_ref, bg_ref, tm_ref, c0, c1, c2, c3, c4, c5, dc_ref, dbg_ref):
        heads = range(nh)
        masks = [_head_masks(j) for j in heads]
        known = [tm_ref[:, _hsl(j)] for j in heads]
        _, vjp = jax.vjp(lambda a, b, c, d: _prep_fn(a, b, c, d, [m[0] for m in masks], [m[1] for m in masks], known)[:6],
                         [cq_ref[:, _hsl(j)] for j in heads], [ck_ref[:, _hsl(j)] for j in heads],
                         [cv_ref[:, _hsl(j)] for j in heads], bg_ref[...])
        cts = tuple([c[:, _hsl(j)] for j in heads] for c in (c0, c1, c2, c3, c4))
        dqs, dks, dvs, dbg = vjp(cts + ([c5[j * SUBLANES:(j + 1) * SUBLANES, :] for j in heads],))
        for part, ds in enumerate((dqs, dks, dvs)):
            for j in heads:
                dc_ref[:, _hsl(part * nh + j)] = ds[j]
        dbg_ref[...] = dbg

    blk = lambda off: pl.BlockSpec((CHUNK, nh * hd), lambda i: (i, off))
    gl_spec = pl.BlockSpec((nh * SUBLANES, LANES), lambda i: (i, 0))
    bg_spec = pl.BlockSpec((CHUNK, LANES), lambda i: (i, 0))
    return pl.pallas_call(
        body, grid=(n,),
        in_specs=[blk(0), blk(1), blk(2), bg_spec] + [blk(0)] * 6 + [gl_spec],
        out_specs=[pl.BlockSpec((CHUNK, 3 * nh * hd), lambda i: (i, 0)), bg_spec],
        out_shape=[jax.ShapeDtypeStruct((t, 3 * nh * hd), F32), jax.ShapeDtypeStruct((t, LANES), F32)],
        name=name, compiler_params=_params(("parallel",)),
    )(cqkv, cqkv, cqkv, bg, tms, *cts)


def delta_scan(u0, wk, qd, kd, qk, gl, *, name):
    t = u0.shape[0]
    nh, hd, n = N_HEADS_A, HEAD_DIM_A, t // CHUNK

    def body(u0_ref, wk_ref, qd_ref, kd_ref, qk_ref, gl_ref, o_ref, sin_ref, s_ref):
        @pl.when(pl.program_id(0) == 0)
        def _():
            s_ref[...] = jnp.zeros_like(s_ref)

        heads = list(range(nh))
        cols = lambda ref: [ref[:, _hsl(h)].astype(BF16) for h in heads]
        ss = [s_ref[h] for h in heads]
        for h in heads:
            sin_ref[h] = ss[h]
        sbs = _each(lambda s: s.astype(BF16), ss)
        ubs = _each(lambda h, wkb, sb: (u0_ref[:, _hsl(h)] - _dot(wkb, sb)).astype(BF16), heads, cols(wk_ref), sbs)
        os_ = _each(lambda qdb, sb, qkb, ub: _dot(qdb, sb) + _dot(qkb, ub), cols(qd_ref), sbs, cols(qk_ref), ubs)
        sn = _each(lambda h, s, kdb, ub: gl_ref[pl.ds(h * SUBLANES, 1), :] * s + _dot(kdb, ub, TN), heads, ss, cols(kd_ref), ubs)
        for h in heads:
            o_ref[:, _hsl(h)] = os_[h]
            s_ref[h] = sn[h]

    blk = pl.BlockSpec((CHUNK, nh * hd), lambda i: (i, 0))
    return pl.pallas_call(
        body, grid=(n,),
        in_specs=[blk] * 5 + [pl.BlockSpec((nh * SUBLANES, LANES), lambda i: (i, 0))],
        out_specs=[blk, pl.BlockSpec((None, nh, hd, hd), lambda i: (i, 0, 0, 0))],
        out_shape=[jax.ShapeDtypeStruct((t, nh * hd), F32), jax.ShapeDtypeStruct((n, nh, hd, hd), F32)],
        scratch_shapes=[pltpu.VMEM((nh, hd, hd), F32)], name=name,
        compiler_params=_params(("arbitrary",)),
    )(u0, wk, qd, kd, qk, gl)


def delta_scan_bwd(do, u0, wk, qd, kd, qk, gl, s_in, *, name):
    t = u0.shape[0]
    nh, hd, n = N_HEADS_A, HEAD_DIM_A, t // CHUNK

    def body(do_ref, u0_ref, wk_ref, qd_ref, kd_ref, qk_ref, gl_ref, sin_ref,
             du0_ref, dwk_ref, dqd_ref, dkd_ref, dqk_ref, dgl_ref, ds_ref):
        @pl.when(pl.program_id(0) == 0)
        def _():
            ds_ref[...] = jnp.zeros_like(ds_ref)

        corner = (lax.broadcasted_iota(jnp.int32, (SUBLANES, LANES), 0) == 0) & (lax.broadcasted_iota(jnp.int32, (SUBLANES, LANES), 1) == 0)
        heads = list(range(nh))
        cols = lambda ref: [ref[:, _hsl(h)].astype(BF16) for h in heads]
        ss, dss = [sin_ref[h] for h in heads], [ds_ref[h] for h in heads]
        sbs, dsbs = _each(lambda s: s.astype(BF16), ss), _each(lambda d: d.astype(BF16), dss)
        dobs, wkbs, qdbs, kdbs, qkbs = cols(do_ref), cols(wk_ref), cols(qd_ref), cols(kd_ref), cols(qk_ref)
        ubs = _each(lambda h, wkb, sb: (u0_ref[:, _hsl(h)] - _dot(wkb, sb)).astype(BF16), heads, wkbs, sbs)
        dus = _each(lambda qkb, dob, kdb, dsb: _dot(qkb, dob, TN) + _dot(kdb, dsb), qkbs, dobs, kdbs, dsbs)
        dubs = _each(lambda du: du.astype(BF16), dus)
        dwks = _each(lambda dub, sb: -_dot(dub, sb, NT), dubs, sbs)
        dqds = _each(lambda dob, sb: _dot(dob, sb, NT), dobs, sbs)
        dkds = _each(lambda ub, dsb: _dot(ub, dsb, NT), ubs, dsbs)
        dqks = _each(lambda dob, ub: _dot(dob, ub, NT), dobs, ubs)
        dgls = _each(lambda s, d: jnp.sum(jnp.sum(s * d, axis=1, keepdims=True), axis=0, keepdims=True), ss, dss)
        dsn = _each(lambda h, d, qdb, dob, wkb, dub: gl_ref[pl.ds(h * SUBLANES, 1), :] * d + _dot(qdb, dob, TN) - _dot(wkb, dub, TN),
                    heads, dss, qdbs, dobs, wkbs, dubs)
        for h in heads:
            du0_ref[:, _hsl(h)] = dus[h]
            dwk_ref[:, _hsl(h)] = dwks[h]
            dqd_ref[:, _hsl(h)] = dqds[h]
            dkd_ref[:, _hsl(h)] = dkds[h]
            dqk_ref[:, _hsl(h)] = dqks[h]
            dgl_ref[h * SUBLANES:(h + 1) * SUBLANES, :] = jnp.where(corner, dgls[h], 0.0)
            ds_ref[h] = dsn[h]

    blk = pl.BlockSpec((CHUNK, nh * hd), lambda i: (n - 1 - i, 0))
    gl_spec = pl.BlockSpec((nh * SUBLANES, LANES), lambda i: (n - 1 - i, 0))
    return pl.pallas_call(
        body, grid=(n,),
        in_specs=[blk] * 6 + [gl_spec, pl.BlockSpec((None, nh, hd, hd), lambda i: (n - 1 - i, 0, 0, 0))],
        out_specs=[blk] * 5 + [gl_spec],
        out_shape=[jax.ShapeDtypeStruct((t, nh * hd), F32)] * 5 + [jax.ShapeDtypeStruct((n * nh * SUBLANES, LANES), F32)],
        scratch_shapes=[pltpu.VMEM((nh, hd, hd), F32)], name=name,
        compiler_params=_params(("arbitrary",)),
    )(do, u0, wk, qd, kd, qk, gl, s_in)


N_PAIRS = N_HEADS_B // 2
PAIRS_PER_KV = N_PAIRS // N_KV_B


def _psl(j):
    return slice(j * LANES, (j + 1) * LANES)


KV_STEP = 4


def _att_fn(qps, kcs, kps, vcs, vps, sinks, kv0, first):
    w = WINDOW
    lane = lax.broadcasted_iota(jnp.int32, (1, LANES), 1)
    lo = (lane < HEAD_DIM_B).astype(F32)
    qi = lax.broadcasted_iota(jnp.int32, (w, w), 0)
    kj = lax.broadcasted_iota(jnp.int32, (w, w), 1)
    dist_c = (qi - kj).astype(F32)
    valid_c = kj <= qi
    valid_p = (kj > qi) & (first < 0.5)
    bf = lambda xs: [a.astype(BF16) for a in xs]
    kcb, kpb, vcb, vpb = bf(kcs), bf(kps), bf(vcs), bf(vps)
    scale = HEAD_DIM_B ** -0.5
    heads = [(g, j, half) for g in range(len(kcs)) for j in range(PAIRS_PER_KV) for half in range(2)]
    kvs = [g for g, _, _ in heads]
    hmasks = [lo if half == 0 else 1.0 - lo for _, _, half in heads]
    hds = [2.0 * (PAIRS_PER_KV * (kv0 + g) + j) + half for g, j, half in heads]
    slopes = _each(lambda hd: jnp.exp(-(hd + 1.0) * (8.0 / N_HEADS_B * math.log(2.0))), hds)
    snks = _each(lambda hd: jnp.sum(sinks * (lane.astype(F32) == hd).astype(F32), axis=1, keepdims=True), hds)
    qhs = _each(lambda h, hm: (qps[h[0] * PAIRS_PER_KV + h[1]] * hm).astype(BF16), heads, hmasks)
    lcs = _each(lambda qh, g, sl: jnp.where(valid_c, _dot(qh, kcb[g], NT) * scale - sl * dist_c, NEG), qhs, kvs, slopes)
    lps = _each(lambda qh, g, sl: jnp.where(valid_p, _dot(qh, kpb[g], NT) * scale - sl * (dist_c + w), NEG), qhs, kvs, slopes)
    ms = _each(lambda lc, lp, sk: lax.stop_gradient(jnp.maximum(jnp.maximum(jnp.max(lc, axis=1, keepdims=True),
                                                                            jnp.max(lp, axis=1, keepdims=True)), sk)), lcs, lps, snks)
    ecs = _each(lambda lc, m: jnp.exp(lc - m), lcs, ms)
    eps = _each(lambda lp, m: jnp.exp(lp - m), lps, ms)
    invs = _each(lambda ec, ep, sk, m: 1.0 / (jnp.sum(ec, axis=1, keepdims=True) + jnp.sum(ep, axis=1, keepdims=True) + jnp.exp(sk - m)),
                 ecs, eps, snks, ms)
    ohs = _each(lambda ec, ep, inv, g, hm: (_dot((ec * inv).astype(BF16), vcb[g]) + _dot((ep * inv).astype(BF16), vpb[g])) * hm,
                ecs, eps, invs, kvs, hmasks)
    return [ohs[2 * j] + ohs[2 * j + 1] for j in range(len(qps))]


def _scalar11(v):
    return jnp.full((1, 1), v, F32)


def _att_specs(row_of):
    cur = pl.BlockSpec((WINDOW, KV_STEP * LANES), lambda i, kv: (row_of(i), kv))
    prev = pl.BlockSpec((WINDOW, KV_STEP * LANES), lambda i, kv: (jnp.maximum(row_of(i) - 1, 0), kv))
    qs = pl.BlockSpec((WINDOW, KV_STEP * PAIRS_PER_KV * LANES), lambda i, kv: (row_of(i), kv))
    return qs, cur, prev, pl.BlockSpec((1, LANES), lambda i, kv: (0, 0))


def swa_fwd(qsrc, kd, vd, sinks, *, name):
    t = kd.shape[0]
    nb = t // WINDOW
    npair = KV_STEP * PAIRS_PER_KV

    def body(q_ref, kc_ref, kp_ref, vc_ref, vp_ref, s_ref, o_ref):
        first = _scalar11((pl.program_id(0) == 0).astype(F32))
        kv0 = _scalar11((pl.program_id(1) * KV_STEP).astype(F32))
        per_kv = lambda ref: [ref[:, _psl(g)] for g in range(KV_STEP)]
        outs = _att_fn([q_ref[:, _psl(j)] for j in range(npair)], per_kv(kc_ref), per_kv(kp_ref), per_kv(vc_ref), per_kv(vp_ref),
                       s_ref[...], kv0, first)
        for j in range(npair):
            o_ref[:, _psl(j)] = outs[j].astype(o_ref.dtype)

    qs, cur, prev, sk = _att_specs(lambda i: i)
    return pl.pallas_call(
        body, grid=(nb, N_KV_B // KV_STEP), in_specs=[qs, cur, prev, cur, prev, sk],
        out_specs=qs, out_shape=jax.ShapeDtypeStruct((t, N_PAIRS * LANES), BF16), name=name,
        compiler_params=_params(("parallel", "parallel")),
    )(qsrc, kd, kd, vd, vd, sinks)


def swa_bwd(do, qsrc, kd, vd, sinks, *, name):
    t = kd.shape[0]
    nb = t // WINDOW

    npair = KV_STEP * PAIRS_PER_KV

    def body(do_ref, q_ref, kc_ref, kp_ref, vc_ref, vp_ref, s_ref, dq_ref, dk_ref, dv_ref, ds_ref, carry_k, carry_v):
        step, kvg = pl.program_id(0), pl.program_id(1)
        first = _scalar11((step == nb - 1).astype(F32))

        @pl.when((step == 0) & (kvg == 0))
        def _():
            carry_k[...] = jnp.zeros_like(carry_k)
            carry_v[...] = jnp.zeros_like(carry_v)
            ds_ref[...] = jnp.zeros_like(ds_ref)

        kv0 = _scalar11((kvg * KV_STEP).astype(F32))
        per_kv = lambda ref: [ref[:, _psl(g)].astype(F32) for g in range(KV_STEP)]
        _, vjp = jax.vjp(lambda *a: _att_fn(*a, kv0, first), [q_ref[:, _psl(j)].astype(F32) for j in range(npair)],
                         per_kv(kc_ref), per_kv(kp_ref), per_kv(vc_ref), per_kv(vp_ref), s_ref[...])
        dqs, dkc, dkp, dvc, dvp, dsk = vjp([do_ref[:, _psl(j)].astype(F32) for j in range(npair)])
        for j in range(npair):
            dq_ref[:, _psl(j)] = dqs[j].astype(dq_ref.dtype)
        ds_ref[...] += dsk
        fold = lambda g: g + pltpu.roll(g, HEAD_DIM_B, 1)
        for g in range(KV_STEP):
            kv = kvg * KV_STEP + g
            dk_ref[:, _psl(g)] = fold(dkc[g] + carry_k[kv]).astype(dk_ref.dtype)
            dv_ref[:, _psl(g)] = fold(dvc[g] + carry_v[kv]).astype(dv_ref.dtype)
            carry_k[kv] = dkp[g]
            carry_v[kv] = dvp[g]

    qs, cur, prev, sk = _att_specs(lambda i: nb - 1 - i)
    return pl.pallas_call(
        body, grid=(nb, N_KV_B // KV_STEP),
        in_specs=[qs, qs, cur, prev, cur, prev, sk],
        out_specs=[qs, cur, cur, sk],
        out_shape=[jax.ShapeDtypeStruct((t, N_PAIRS * LANES), BF16), jax.ShapeDtypeStruct((t, N_KV_B * LANES), BF16),
                   jax.ShapeDtypeStruct((t, N_KV_B * LANES), BF16), jax.ShapeDtypeStruct((1, LANES), F32)],
        scratch_shapes=[pltpu.VMEM((N_KV_B, WINDOW, LANES), F32), pltpu.VMEM((N_KV_B, WINDOW, LANES), F32)],
        name=name, compiler_params=_params(("arbitrary", "arbitrary")),
    )(do, qsrc, kd, kd, vd, vd, sinks)


def loss_head(h, tgt, w, *, name):
    t, d = h.shape
    tm = min(256, t)

    def body(h_ref, t_ref, w_ref, dh_ref, dw_ref, l_ref):
        tg = t_ref[...]

        def f(hv, wv):
            err = _f_norm(hv, wv) - tg
            return 0.5 * jnp.sum(jnp.sum(err * err, axis=1, keepdims=True), axis=0, keepdims=True) * (1.0 / d)

        lv, vjp = jax.vjp(f, h_ref[...], w_ref[...])
        dh, dw = vjp(jnp.ones((1, 1), F32))
        dh_ref[...] = dh
        first = pl.program_id(0) == 0

        @pl.when(first)
        def _():
            dw_ref[...] = dw
            l_ref[...] = lv * jnp.ones((1, LANES), F32)

        @pl.when(jnp.logical_not(first))
        def _():
            dw_ref[...] += dw
            l_ref[...] += lv * jnp.ones((1, LANES), F32)

    rows = pl.BlockSpec((tm, d), lambda i: (i, 0))
    one = lambda c: pl.BlockSpec((1, c), lambda i: (0, 0))
    return pl.pallas_call(
        body, grid=(t // tm,), in_specs=[rows, rows, one(d)], out_specs=[rows, one(d), one(LANES)],
        out_shape=[jax.ShapeDtypeStruct((t, d), F32), jax.ShapeDtypeStruct((1, d), F32), jax.ShapeDtypeStruct((1, LANES), F32)],
        name=name, compiler_params=_params(("arbitrary",)),
    )(h, tgt, w)


def _row_tile(r, cap=256):
    tr = r
    if r % SUBLANES == 0:
        for cand in range(SUBLANES, min(r, cap) + 1, SUBLANES):
            if r % cand == 0:
                tr = cand
    return tr


def _adamw_update(wv, gv, mv, vv):
    mn = ADAM_B1 * mv + (1.0 - ADAM_B1) * gv
    vn = ADAM_B2 * vv + (1.0 - ADAM_B2) * jnp.square(gv)
    m_hat = mn / (1.0 - ADAM_B1 ** ADAM_STEP)
    v_hat = vn / (1.0 - ADAM_B2 ** ADAM_STEP)
    return -ADAM_LR * (m_hat / (jnp.sqrt(v_hat) + ADAM_EPS) + ADAM_WD * wv), mn, vn


def adamw_layers(w, halves, m, v, *, name):
    nl, r, c = w.shape
    tr = _row_tile(r // 2)
    nbh = r // 2 // tr

    def body(w_ref, *rest):
        g_refs, m_ref, v_ref = rest[:2 * nl], rest[2 * nl], rest[2 * nl + 1]
        d_ref, mo_ref, vo_ref, go_ref = rest[2 * nl + 2:]
        layer, i = pl.program_id(0), pl.program_id(1)
        mine = (i < nbh) == (lax.axis_index("c") == 0)
        gv = jnp.where(mine, g_refs[0][...], g_refs[1][...])
        for k in range(1, nl):
            gv = jnp.where(layer == k, jnp.where(mine, g_refs[2 * k][...], g_refs[2 * k + 1][...]), gv)
        d_ref[...], mo_ref[...], vo_ref[...] = _adamw_update(w_ref[...], gv, m_ref[...], v_ref[...])
        go_ref[...] = gv

    spec3 = pl.BlockSpec((None, tr, c), lambda k, i: (k, i, 0))
    g_specs = [pl.BlockSpec((tr, c), lambda k, i, q=q: (jnp.where(k == q, i % nbh, 0), 0)) for q in range(nl) for _ in range(2)]
    return pl.pallas_call(
        body, grid=(nl, r // tr), in_specs=[spec3] + g_specs + [spec3, spec3], out_specs=[spec3] * 4,
        out_shape=[jax.ShapeDtypeStruct((nl, r, c), F32)] * 4, name=name, compiler_params=_params(("arbitrary", "arbitrary")),
    )(w, *[h for pair in halves for h in pair], m, v)


def adamw(w, g, m, v, *, name):
    r, c = w.shape
    tr = _row_tile(r)

    def body(w_ref, g_ref, m_ref, v_ref, d_ref, mo_ref, vo_ref):
        d_ref[...], mo_ref[...], vo_ref[...] = _adamw_update(w_ref[...], g_ref[...], m_ref[...], v_ref[...])

    spec = pl.BlockSpec((tr, c), lambda i: (i, 0))
    return pl.pallas_call(
        body, grid=(r // tr,), in_specs=[spec] * 4, out_specs=[spec] * 3,
        out_shape=[jax.ShapeDtypeStruct((r, c), F32)] * 3, name=name, compiler_params=_params(("parallel",)),
    )(w, g, m, v)


def _place():
    return lax.axis_index("x"), lax.axis_index("y"), lax.axis_index("c")


def allgather8(blk, *, name):
    def body(x_ref, out_ref, send_sems, recv_sems, local_sem):
        x, y, c = _place()
        me = 4 * x + 2 * y + c
        mine = pltpu.make_async_copy(x_ref, out_ref.at[me], local_sem)
        mine.start()
        sent = []
        for k in range(1, N_DEV):
            to = (x ^ ((k >> 2) & 1), y ^ ((k >> 1) & 1), c ^ (k & 1))
            cp = pltpu.make_async_remote_copy(src_ref=x_ref, dst_ref=out_ref.at[me], send_sem=send_sems.at[k - 1],
                                              recv_sem=recv_sems.at[k - 1], device_id=to, device_id_type=MESH)
            cp.start()
            sent.append(cp)
        for k in range(1, N_DEV):
            frm = me ^ k
            pltpu.make_async_remote_copy(src_ref=x_ref, dst_ref=out_ref.at[frm], send_sem=send_sems.at[k - 1],
                                         recv_sem=recv_sems.at[k - 1], device_id=(x, y, c), device_id_type=MESH).wait_recv()
        for cp in sent:
            cp.wait_send()
        mine.wait()

    vm = pl.BlockSpec(memory_space=pltpu.VMEM)
    return pl.pallas_call(
        body, in_specs=[vm], out_specs=vm, out_shape=jax.ShapeDtypeStruct((N_DEV,) + blk.shape, blk.dtype), name=name,
        scratch_shapes=[pltpu.SemaphoreType.DMA((N_DEV - 1,)), pltpu.SemaphoreType.DMA((N_DEV - 1,)), pltpu.SemaphoreType.DMA],
    )(blk)


def _other_chips(x, y):
    return [(1 - x, y), (x, 1 - y), (1 - x, 1 - y)]


def _hbm_call(body, ins, out_shapes, n_sems, name):
    hbm = pl.BlockSpec(memory_space=pl.ANY)
    return pl.pallas_call(
        body, in_specs=[hbm] * len(ins), out_specs=[hbm] * len(out_shapes), out_shape=out_shapes, name=name,
        scratch_shapes=[pltpu.SemaphoreType.DMA((n_sems,)), pltpu.SemaphoreType.DMA((n_sems,))],
    )(*ins)


def _half_rows(c, rh):
    return pl.ds(pl.multiple_of(c * rh, BF16_ROWS), rh)


def gather_units(units, *, name):
    nu = len(units)
    shapes = []
    for arr, layer_major in units:
        r, cols = arr.shape
        shapes.append(jax.ShapeDtypeStruct((2, N_CHIPS, r // 2, cols) if layer_major else (N_CHIPS, r, cols), arr.dtype))

    def body(*refs):
        in_refs, out_refs, send_sems, recv_sems = refs[:nu], refs[nu:2 * nu], refs[2 * nu], refs[2 * nu + 1]
        x, y, c = _place()
        me_chip = 2 * x + y
        sib = (x, y, 1 - c)
        chips = _other_chips(x, y)

        def copy(k, src, dst, to):
            return pltpu.make_async_remote_copy(src_ref=src, dst_ref=dst, send_sem=send_sems.at[k], recv_sem=recv_sems.at[k],
                                                device_id=to, device_id_type=MESH)

        first, passed, landing = [], [], []
        for u, (arr, layer_major) in enumerate(units):
            rh = arr.shape[0] // 2
            out_ref = out_refs[u]
            slot = (lambda chip, half, o=out_ref: o.at[half, chip]) if layer_major else \
                   (lambda chip, half, o=out_ref, rh=rh: o.at[chip, _half_rows(half, rh), :])
            my_half = in_refs[u].at[_half_rows(c, rh), :]
            for j, (cx, cy) in enumerate(chips):
                k = 6 * u + j
                first.append(copy(k, my_half, slot(me_chip, c), (cx, cy, c)))
                passed.append(copy(k + 3, slot(2 * cx + cy, c), slot(2 * cx + cy, c), sib))
                landing.append((copy(k, my_half, slot(2 * cx + cy, c), sib), copy(k + 3, my_half, slot(2 * cx + cy, 1 - c), sib)))
        for cp in first:
            cp.start()
        for (over_ici, _), fwd in zip(landing, passed):
            over_ici.wait_recv()
            fwd.start()
        for _, from_sibling in landing:
            from_sibling.wait_recv()
        for cp in first + passed:
            cp.wait_send()

    return _hbm_call(body, [a for a, _ in units], shapes, 6 * nu, name)


HBM_SPEC = pl.BlockSpec(memory_space=pltpu.HBM)
SEM_SPEC = pl.BlockSpec(memory_space=pltpu.SEMAPHORE)
ORDERED_EFFECT = pltpu.SideEffectType.DATAFLOW_SIDE_EFFECTING


def _split_start(body, srcs, land_shapes, after, *, name):
    nu = len(srcs)
    lands = [lax.empty(s.shape, s.dtype) for s in land_shapes]

    def whole(*refs):
        body(refs[:nu], refs[nu:2 * nu], refs[2 * nu + 1], refs[2 * nu + 2])
        refs[-1][...] = jnp.zeros((SUBLANES, LANES), F32)

    hbm = lambda a: pltpu.with_memory_space_constraint(a, pltpu.HBM)
    sems = pltpu.SemaphoreType.DMA((nu,))
    res = pl.pallas_call(
        whole, name=name, in_specs=[HBM_SPEC] * (2 * nu) + [pl.BlockSpec(memory_space=pl.ANY)],
        out_shape=[sems, sems] + [pltpu.HBM(a.shape, a.dtype) for a in srcs] + [pltpu.HBM(s.shape, s.dtype) for s in land_shapes]
        + [jax.ShapeDtypeStruct((SUBLANES, LANES), F32)],
        out_specs=[SEM_SPEC, SEM_SPEC] + [HBM_SPEC] * (2 * nu) + [pl.BlockSpec(memory_space=pltpu.VMEM)],
        input_output_aliases={q: 2 + q for q in range(2 * nu)},
        compiler_params=pltpu.CompilerParams(has_side_effects=ORDERED_EFFECT),
    )(*[hbm(a) for a in srcs], *[hbm(a) for a in lands], after)
    return res[0], res[1], res[2:2 + nu], res[2 + nu:2 + 2 * nu], res[-1]


def _split_wait(pending, moved, after, *, name):
    send_sems, recv_sems, srcs, lands, _ = pending
    nu = len(srcs)

    def body(*refs):
        land_refs, ssem, rsem = refs[nu:2 * nu], refs[2 * nu], refs[2 * nu + 1]
        x, y, c = _place()
        for u in range(nu):
            size = moved(land_refs[u])
            cp = pltpu.make_async_remote_copy(src_ref=size, dst_ref=size, send_sem=ssem.at[u], recv_sem=rsem.at[u],
                                              device_id=(x, y, c), device_id_type=MESH)
            cp.wait_send()
            cp.wait_recv()

    res = pl.pallas_call(
        body, name=name, in_specs=[HBM_SPEC] * (2 * nu) + [SEM_SPEC, SEM_SPEC, pl.BlockSpec(memory_space=pl.ANY)],
        out_shape=[pltpu.HBM(a.shape, a.dtype) for a in srcs] + [pltpu.HBM(a.shape, a.dtype) for a in lands],
        out_specs=[HBM_SPEC] * (2 * nu), input_output_aliases={q: q for q in range(2 * nu)},
        compiler_params=pltpu.CompilerParams(has_side_effects=ORDERED_EFFECT),
    )(*srcs, *lands, send_sems, recv_sems, after)
    return res[nu:]


def gather_start(shards, after, *, name):
    def body(src_refs, land_refs, send_sems, recv_sems):
        x, y, c = _place()
        for u, shard in enumerate(shards):
            rows = _half_rows(c, shard.shape[0] // 2)
            for cx, cy in _other_chips(x, y):
                for core in range(2):
                    pltpu.make_async_remote_copy(src_ref=src_refs[u].at[rows, :], dst_ref=land_refs[u].at[2 * x + y, rows, :],
                                                 send_sem=send_sems.at[u], recv_sem=recv_sems.at[u], device_id=(cx, cy, core),
                                                 device_id_type=MESH).start()

    return _split_start(body, shards, [jax.ShapeDtypeStruct((N_CHIPS,) + s.shape, s.dtype) for s in shards], after, name=name)


def gather_wait(pending, after, *, name):
    return _split_wait(pending, lambda land: land.at[pl.ds(0, N_CHIPS - 1)], after, name=name)


def scatter_start(pairs, *, name):
    def body(src_refs, land_refs, send_sems, recv_sems):
        x, y, c = _place()
        for u in range(len(pairs)):
            for j, (cx, cy) in enumerate(_other_chips(x, y)):
                pltpu.make_async_remote_copy(src_ref=src_refs[u].at[2 * cx + cy], dst_ref=land_refs[u].at[j], send_sem=send_sems.at[u],
                                             recv_sem=recv_sems.at[u], device_id=(cx, cy, c), device_id_type=MESH).start()

    return _split_start(body, pairs, [jax.ShapeDtypeStruct((N_CHIPS - 1,) + p.shape[1:], p.dtype) for p in pairs], pairs[0], name=name)


def scatter_wait(pending, after, *, name):
    return _split_wait(pending, lambda land: land, after, name=name)


def swap_units(units, *, name):
    nu = len(units)

    def body(*refs):
        g_refs, out_refs, send_sems, recv_sems = refs[:nu], refs[nu:2 * nu], refs[2 * nu], refs[2 * nu + 1]
        x, y, c = _place()
        cps = [pltpu.make_async_remote_copy(src_ref=g_refs[u].at[:, _half_rows(1 - c, units[u].shape[1] // 2), :], dst_ref=out_refs[u],
                                            send_sem=send_sems.at[u], recv_sem=recv_sems.at[u], device_id=(x, y, 1 - c),
                                            device_id_type=MESH) for u in range(nu)]
        for cp in cps:
            cp.start()
        for cp in cps:
            cp.wait()

    shapes = [jax.ShapeDtypeStruct((N_CHIPS, g.shape[1] // 2, g.shape[2]), g.dtype) for g in units]
    return _hbm_call(body, units, shapes, nu, name)


def join_units(units, *, name):
    nu = len(units)

    def body(*refs):
        h_refs, out_refs, send_sems, recv_sems = refs[:nu], refs[nu:2 * nu], refs[2 * nu], refs[2 * nu + 1]
        x, y, c = _place()
        cps = [pltpu.make_async_remote_copy(src_ref=h_refs[u], dst_ref=out_refs[u], send_sem=send_sems.at[u], recv_sem=recv_sems.at[u],
                                            device_id=(x, y, 1 - c), device_id_type=MESH) for u in range(nu)]
        for cp in cps:
            cp.start()
        for cp in cps:
            cp.wait()

    return _hbm_call(body, units, [jax.ShapeDtypeStruct(h.shape, h.dtype) for h in units], nu, name)


def _half_tile(rh):
    tr = rh
    for cand in range(BF16_ROWS, min(rh, 512) + 1, BF16_ROWS):
        if rh % cand == 0:
            tr = cand
    return tr


def pair_add(g, sib, *, name):
    nc, rh, cols = sib.shape
    tr = _half_tile(rh)
    nbh = rh // tr

    def body(g0_ref, g1_ref, s_ref, o_ref):
        mine = jnp.where(lax.axis_index("c") == 0, g0_ref[...], g1_ref[...])
        o_ref[...] = (mine.astype(F32) + s_ref[...].astype(F32)).astype(o_ref.dtype)

    blk = lambda off: pl.BlockSpec((None, tr, cols), lambda j, i: (j, off + i, 0))
    return pl.pallas_call(
        body, grid=(nc, nbh), in_specs=[blk(0), blk(nbh), blk(0)], out_specs=blk(0),
        out_shape=jax.ShapeDtypeStruct(sib.shape, BF16), name=name, compiler_params=_params(("parallel", "parallel")),
    )(g, g, sib)


def chips_add(pair, landed, *, name):
    nc, rh, cols = pair.shape
    tr = _half_tile(rh)

    def body(*refs):
        chip = 2 * lax.axis_index("x") + lax.axis_index("y")
        acc = refs[0][...]
        for j in range(1, nc):
            acc = jnp.where(chip == j, refs[j][...], acc)
        acc = acc.astype(F32)
        for r in refs[nc:-1]:
            acc = acc + r[...].astype(F32)
        refs[-1][...] = acc

    part = lambda q: pl.BlockSpec((None, tr, cols), lambda i, q=q: (q, i, 0))
    return pl.pallas_call(
        body, grid=(rh // tr,), in_specs=[part(q) for q in range(nc)] + [part(q) for q in range(landed.shape[0])],
        out_specs=pl.BlockSpec((tr, cols), lambda i: (i, 0)),
        out_shape=jax.ShapeDtypeStruct((rh, cols), F32), name=name, compiler_params=_params(("parallel",)),
    )(*[pair] * nc, *[landed] * landed.shape[0])


def sum8(g, *, name):
    def body(g_ref, o_ref):
        acc = g_ref[0]
        for d in range(1, N_DEV):
            acc = acc + g_ref[d]
        o_ref[...] = acc

    return pl.pallas_call(body, out_shape=jax.ShapeDtypeStruct(g.shape[1:], F32), name=name)(g)


def _dup_halves(a):
    t = a.shape[0]
    a = a.reshape(t, N_KV_B, HEAD_DIM_B)
    return jnp.concatenate([a, a], axis=-1).reshape(t, N_KV_B * LANES)


def _undup(a):
    t = a.shape[0]
    return a.reshape(t, N_KV_B, LANES)[:, :, :HEAD_DIM_B].reshape(t, N_KV_B * HEAD_DIM_B)


def _lane_pad(v, offset=0):
    return jnp.zeros((1, LANES), F32).at[0, offset:offset + v.shape[0]].set(v)


SHARD_UP = 2 * D_FF // N_CHIPS
SHARD_BIN = (N_HEADS_B + 2 * N_KV_B) * HEAD_DIM_B // N_CHIPS
SHARD_PROJ = D_MODEL // N_CHIPS


def local_step(x, p, tgt, sm, weight, on_grads):
    t = x.shape[0]
    rtm = min(256, t)
    hk = N_HEADS_A * HEAD_DIM_A
    qd_b = N_HEADS_B * HEAD_DIM_B
    kd_b = N_KV_B * HEAD_DIM_B
    gs = {}
    norm = lambda h, w, nm: tile_map(_f_norm, [(h, D_MODEL, 0)], [w], [(D_MODEL, BF16)], tm=rtm, ncol=1, name=nm)[0]

    spec = pl.BlockSpec
    mtm = _tile(D_MODEL, MM_TM_CAP)
    p_bf = p.astype(BF16)
    alog_p = _lane_pad(sm["a_log"][0], N_HEADS_A)
    dtb_p = _lane_pad(sm["a_dt_bias"][0], N_HEADS_A)
    sinks_p = _lane_pad(sm["b_sinks"][0])
    nw = lambda name, i: sm[name][i:i + 1]
    by_chip = lambda kdim, ns: dict(tn=ns, tk=kdim, b_spec=spec((None, kdim, ns), lambda r, j, kk: (j, kk, 0)))
    by_chip_t = lambda ndim, ns: dict(n=ndim, tn=ndim, tk=ns, b_spec=spec((None, ndim, ns), lambda r, j, kk: (kk, j, 0)))
    cache = {}

    def wgt(name, i, after):
        if (name, i) not in cache:
            cache[name, i] = weight(name, i, after)
        return cache[name, i]

    saved = []
    h = x
    hn_next = norm(h, nw("norm_mix", 0), "norm_mix0")
    for i in range(DEPTH):
        s = {"h0": h, "hn": hn_next}
        if i % 2 == 0:
            s["pm"] = mm(s["hn"], wgt("a_w_in", i, h), name="a_in")
            tail = (s["pm"], LANES, 4 * hk // LANES)
            s["c"] = conv_fwd(s["pm"], wgt("a_conv", i, h), name="a_conv")
            s["bg"] = tile_map(_f_betag, [tail], [alog_p, dtb_p], [(LANES, F32)], tm=rtm, ncol=1, name="a_betag")[0]
            s["prep"], s["tms"] = delta_prep(s["c"], s["bg"], name="a_prep")
            s["o"], s["s_in"] = delta_scan(*s["prep"], name="a_scan")
            s["on"] = gnorm_fwd(s["o"], s["pm"], sm["a_norm"], name="a_gnorm")
            h, s["hf"] = mm(s["on"], wgt("a_w_out", i, s["on"]), add=h, norm_w=nw("norm_ffn", i), name="a_out")
        else:
            s["pb"] = mm(s["hn"], wgt("b_w_in", i, s["hn"]), name="b_in", out_dtype=BF16, n=N_CHIPS * SHARD_BIN,
                         **by_chip(D_MODEL, SHARD_BIN))
            s["kd"], s["vd"] = _dup_halves(s["pb"][:, qd_b:qd_b + kd_b]), _dup_halves(s["pb"][:, qd_b + kd_b:])
            s["ao"] = swa_fwd(s["pb"], s["kd"], s["vd"], sinks_p, name="b_att")
            h, s["hf"] = mm(s["ao"], wgt("b_w_out", i, s["ao"]), add=h, norm_w=nw("norm_ffn", i), name="b_out")
        s["h1"] = h
        s["u"] = mm(s["hf"], wgt("f_w_up", i, s["hf"]), name=f"f_up{i}", out_dtype=BF16, n=2 * D_FF, **by_chip(D_MODEL, SHARD_UP))
        s["act"] = conv_act_fwd(s["u"], wgt("f_conv", i, s["hf"]), name=f"f_conv_act{i}")
        h, s["hp"] = mm(s["act"], wgt("f_w_down", i, s["act"]), add=h, norm_w=nw("norm_ple", i), name=f"f_down{i}")
        s["h2"] = h
        s["gl"] = mm(s["hp"], wgt("ple_w_gate", i, s["hp"]), name=f"ple_gate{i}")
        s["pe"] = mm(p_bf[i], wgt("ple_w_proj", i, s["hp"]), name=f"ple_proj{i}", n=D_MODEL, **by_chip(PLE_DIM, SHARD_PROJ))
        rows3 = [(h, D_MODEL, 0), (s["gl"], D_MODEL, 0), (s["pe"], D_MODEL, 0)]
        if i + 1 < DEPTH:
            def mix_norm(hv, g, e, wn):
                hn = hv + _f_ple(g, e)
                return hn, _f_norm(hn, wn)
            h, hn_next = tile_map(mix_norm, rows3, [nw("norm_mix", i + 1)], [(D_MODEL, F32), (D_MODEL, BF16)], tm=rtm, ncol=1,
                                  name=f"ple_mix{i}")
        else:
            h = tile_map(lambda hv, g, e: hv + _f_ple(g, e), rows3, [], [(D_MODEL, F32)], tm=rtm, ncol=1, name=f"ple_mix{i}")[0]
        saved.append(s)

    dh, gnf, loss = loss_head(h, tgt, sm["norm_final"][None, :], name="loss_head")
    gs["norm_final"] = gnf[0]

    g_mix, g_ffn, g_ple, g_conv = ([None] * DEPTH for _ in range(4))
    zero = jnp.zeros((1, 1), F32)
    for i in reversed(range(DEPTH)):
        s, gw = saved[i], {}
        by_rows = lambda g: g.reshape(N_CHIPS, g.shape[0] // N_CHIPS, g.shape[1])
        (dgl, dpe), _ = tile_vjp(_f_ple, [(s["gl"], D_MODEL, 0), (s["pe"], D_MODEL, 0)], [], [(dh, D_MODEL, 0)], n_diff=2,
                                 tm=rtm, ncol=1, name=f"ple_mix_bwd{i}", grad_dtypes=[BF16, BF16])
        gw["ple_w_proj"] = mm(p_bf[i], dpe, ta=True, name=f"ple_proj_dw{i}", out_dtype=BF16, tn=SHARD_PROJ,
                              o_shape=(N_CHIPS, PLE_DIM, SHARD_PROJ), o_spec=spec((None, PLE_DIM, SHARD_PROJ), lambda r, j, kk: (j, r, 0)))
        gw["ple_w_gate"] = by_rows(mm(s["hp"], dgl, ta=True, name=f"ple_gate_dw{i}", out_dtype=BF16))
        fused = dict(tb=True, tm_cap=MM_TM_CAP // 2)
        dh, g_ple[i] = mm(dgl, cache["ple_w_gate", i], name=f"ple_gate_dx{i}", norm_grad=(s["h2"], nw("norm_ple", i) + zero, dh), **fused)

        dact = mm(dh, cache["f_w_down", i], tb=True, name=f"f_down_dx{i}")
        gw["f_w_down"] = by_rows(mm(s["act"], dh, ta=True, name=f"f_down_dw{i}", out_dtype=BF16, tm_cap=D_FF // 2))
        du_halves = conv_act_bwd(s["u"], dact, cache["f_conv", i], name=f"f_conv_act_bwd{i}")
        g_conv[i] = jnp.concatenate(du_halves[2:], axis=1)
        dhf = g_up = None
        for half, du in enumerate(du_halves[:2]):
            c0 = half * (N_CHIPS // 2)
            g_up = mm(s["hf"], du, ta=True, name=f"f_up_dw{i}_{half}", out_dtype=BF16, tn=SHARD_UP, into=g_up,
                      o_shape=(N_CHIPS, D_MODEL, SHARD_UP), o_spec=spec((None, mtm, SHARD_UP), lambda r, j, kk, c0=c0: (c0 + j, r, 0)))
            last = dict(norm_grad=(s["h1"], nw("norm_ffn", i), dh), **fused) if half else dict(tb=True)
            dhf = mm(du, cache["f_w_up", i], name=f"f_up_dx{i}_{half}", n=D_MODEL, tn=D_MODEL, tk=SHARD_UP, add=dhf,
                     b_spec=spec((None, D_MODEL, SHARD_UP), lambda r, j, kk, c0=c0: (c0 + kk, j, 0)), **last)
        gw["f_w_up"] = g_up
        dh, g_ffn[i] = dhf
        token, gw = on_grads(i, "ffn", gw), {}
        w_out = cache["a_w_out" if i % 2 == 0 else "b_w_out", i]
        if token is not None:
            w_out = w_out + token[:1, :1].astype(BF16)

        if i % 2 == 0:
            don = mm(dh, w_out, tb=True, name="a_out_dx")
            gw["a_w_out"] = by_rows(mm(s["on"], dh, ta=True, name="a_out_dw", out_dtype=BF16))
            do, dz, gs["a_norm"] = gnorm_bwd(s["o"], s["pm"], sm["a_norm"], don, name="a_gnorm_bwd")
            dprep = delta_scan_bwd(do, *s["prep"], s["s_in"], name="a_scan_bwd")
            dc, dbg = delta_prep_bwd(s["c"], s["bg"], s["tms"], dprep, name="a_prep_bwd")
            (dpt,), (galog, gdtb) = tile_vjp(_f_betag, [(s["pm"], LANES, 4 * hk // LANES)], [alog_p, dtb_p], [(dbg, LANES, 0)], n_diff=1,
                                             tm=rtm, ncol=1, name="a_betag_bwd", grad_dtypes=[BF16])
            gs["a_log"] = galog[:, N_HEADS_A:2 * N_HEADS_A]
            gs["a_dt_bias"] = gdtb[:, N_HEADS_A:2 * N_HEADS_A]
            dqkv, gs["a_conv"] = conv_bwd(dc, s["pm"], cache["a_conv", i], name="a_conv_bwd")
            dpm = jnp.concatenate([dqkv, dz, dpt], axis=1)
            g_in = mm(s["hn"], dpm, ta=True, name="a_in_dw", out_dtype=BF16)[:, :4 * hk + 2 * N_HEADS_A]
            gw["a_w_in"] = g_in.reshape(D_MODEL, N_CHIPS, g_in.shape[1] // N_CHIPS).transpose(1, 0, 2)
            dh, g_mix[i] = mm(dpm, cache["a_w_in", i], name="a_in_dx", norm_grad=(s["h0"], nw("norm_mix", i), dh), **fused)
        else:
            dao = mm(dh, w_out, tb=True, name="b_out_dx")
            gw["b_w_out"] = by_rows(mm(s["ao"], dh, ta=True, name="b_out_dw", out_dtype=BF16))
            dq, dkd, dvd, gsk = swa_bwd(dao, s["pb"], s["kd"], s["vd"], sinks_p, name="b_att_bwd")
            gs["b_sinks"] = gsk[:, :N_HEADS_B]
            dpb = jnp.concatenate([dq, _undup(dkd), _undup(dvd)], axis=1)
            gw["b_w_in"] = mm(s["hn"], dpb, ta=True, name="b_in_dw", out_dtype=BF16, tn=SHARD_BIN,
                              o_shape=(N_CHIPS, D_MODEL, SHARD_BIN), o_spec=spec((None, mtm, SHARD_BIN), lambda r, j, kk: (j, r, 0)))
            dh, g_mix[i] = mm(dpb, cache["b_w_in", i], name="b_in_dx", norm_grad=(s["h0"], nw("norm_mix", i), dh), **fused,
                              **by_chip_t(D_MODEL, SHARD_BIN))
        token = on_grads(i, "mix", gw)
        if token is not None:
            zero = token[:1, :1]

    gs["norm_mix"], gs["norm_ffn"], gs["norm_ple"] = (jnp.concatenate(g, axis=0) for g in (g_mix, g_ffn, g_ple))
    gs["f_conv"] = jnp.stack(g_conv)
    return loss, dh, gs


BIG = ["a_w_in", "a_w_out", "b_w_in", "b_w_out", "f_w_up", "f_w_down", "ple_w_proj", "ple_w_gate"]
LAYERED = {"f_w_up", "f_w_down", "ple_w_proj", "ple_w_gate"}
BY_CHIP = {"b_w_in", "f_w_up", "ple_w_proj"}
LAYER_UNITS = [[("a_w_in", 0), ("a_w_out", 0)] + [(n, 0) for n in sorted(LAYERED)],
               [("b_w_in", 1), ("b_w_out", 1)] + [(n, 1) for n in sorted(LAYERED)]]
CONVS = ["a_conv", "f_conv"]
SMALL = ["norm_mix", "norm_ffn", "norm_ple", "norm_final", "a_log", "a_dt_bias", "a_norm", "b_sinks"]
SMALL_ROWS = 8
CONV_ROWS = 16
CONV_GRAD_ROWS = 48


def _pack_rows(arrs, rows, dtype):
    flat = jnp.concatenate([a.reshape(-1).astype(dtype) for a in arrs])
    return jnp.pad(flat, (0, rows * PACK_COLS - flat.shape[0])).reshape(rows, PACK_COLS)


def _unpack(flat, shapes):
    out, off = [], 0
    for shp in shapes:
        n = math.prod(shp)
        out.append(flat[off:off + n].reshape(shp))
        off += n
    return out


def _pack_small(d, loss=None):
    tail = jnp.concatenate([d["a_log"].reshape(-1), d["a_dt_bias"].reshape(-1), d["a_norm"].reshape(-1), d["b_sinks"].reshape(-1)])
    if loss is not None:
        tail = jnp.concatenate([tail, loss.reshape(-1)[:1]])
    tail = jnp.pad(tail, (0, PACK_COLS - tail.shape[0]))
    return jnp.concatenate([d["norm_mix"], d["norm_ffn"], d["norm_ple"], d["norm_final"][None, :], tail[None, :]], axis=0)


def _unpack_small(a, like):
    out = {"norm_mix": a[0:2], "norm_ffn": a[2:4], "norm_ple": a[4:6], "norm_final": a[6]}
    off = 0
    for nm in ("a_log", "a_dt_bias", "a_norm", "b_sinks"):
        n = like[nm].size
        out[nm] = a[7, off:off + n].reshape(like[nm].shape)
        off += n
    return out, a[7, off]


def _as2d(a):
    return a.reshape(-1, a.shape[-1])


def kernel(x, p, norm_mix, norm_ffn, norm_ple, norm_final, a_w_in, a_conv, a_log, a_dt_bias, a_norm, a_w_out, b_w_in, b_sinks, b_w_out, f_w_up, f_conv, f_w_down, ple_w_proj, ple_w_gate, loss_target, m_norm_mix, m_norm_ffn, m_norm_ple, m_norm_final, m_a_w_in, m_a_conv, m_a_log, m_a_dt_bias, m_a_norm, m_a_w_out, m_b_w_in, m_b_sinks, m_b_w_out, m_f_w_up, m_f_conv, m_f_w_down, m_ple_w_proj, m_ple_w_gate, v_norm_mix, v_norm_ffn, v_norm_ple, v_norm_final, v_a_w_in, v_a_conv, v_a_log, v_a_dt_bias, v_a_norm, v_a_w_out, v_b_w_in, v_b_sinks, v_b_w_out, v_f_w_up, v_f_conv, v_f_w_down, v_ple_w_proj, v_ple_w_gate):
    w = dict(norm_mix=norm_mix, norm_ffn=norm_ffn, norm_ple=norm_ple, norm_final=norm_final, a_w_in=a_w_in, a_conv=a_conv,
             a_log=a_log, a_dt_bias=a_dt_bias, a_norm=a_norm, a_w_out=a_w_out, b_w_in=b_w_in, b_sinks=b_sinks, b_w_out=b_w_out,
             f_w_up=f_w_up, f_conv=f_conv, f_w_down=f_w_down, ple_w_proj=ple_w_proj, ple_w_gate=ple_w_gate)
    m = dict(norm_mix=m_norm_mix, norm_ffn=m_norm_ffn, norm_ple=m_norm_ple, norm_final=m_norm_final, a_w_in=m_a_w_in,
             a_conv=m_a_conv, a_log=m_a_log, a_dt_bias=m_a_dt_bias, a_norm=m_a_norm, a_w_out=m_a_w_out, b_w_in=m_b_w_in,
             b_sinks=m_b_sinks, b_w_out=m_b_w_out, f_w_up=m_f_w_up, f_conv=m_f_conv, f_w_down=m_f_w_down,
             ple_w_proj=m_ple_w_proj, ple_w_gate=m_ple_w_gate)
    v = dict(norm_mix=v_norm_mix, norm_ffn=v_norm_ffn, norm_ple=v_norm_ple, norm_final=v_norm_final, a_w_in=v_a_w_in,
             a_conv=v_a_conv, a_log=v_a_log, a_dt_bias=v_a_dt_bias, a_norm=v_a_norm, a_w_out=v_a_w_out, b_w_in=v_b_w_in,
             b_sinks=v_b_sinks, b_w_out=v_b_w_out, f_w_up=v_f_w_up, f_conv=v_f_conv, f_w_down=v_f_w_down,
             ple_w_proj=v_ple_w_proj, ple_w_gate=v_ple_w_gate)
    xc, yc, cc = _place()
    my_chip = 2 * xc + yc

    shard = {(n, i): w[n][i if n in LAYERED else 0].astype(BF16) for n, i in LAYER_UNITS[0] + LAYER_UNITS[1]}
    first = shard["a_w_in", 0]
    (ga,) = gather_units([(first, False)], name="gather_first")
    ga = lax.dynamic_update_index_in_dim(ga, first, my_chip, 0)
    a_in = jnp.concatenate([ga[j] for j in range(N_CHIPS)], axis=1)
    n_main = 4 * N_HEADS_A * HEAD_DIM_A
    conv_shapes = [w[n].shape for n in CONVS]
    convs = allgather8(_pack_rows([w[n] for n in CONVS], CONV_ROWS, F32), name="gather_convs")
    conv_parts = [_unpack(convs[2 * j].reshape(-1), conv_shapes) for j in range(N_CHIPS)]
    a_conv_full, f_conv_full = (jnp.concatenate([conv_parts[j][q] for j in range(N_CHIPS)], axis=2) for q in range(2))
    ready = {("a_w_in", 0): jnp.pad(a_in, ((0, 0), (0, n_main + LANES - a_in.shape[1]))), ("a_conv", 0): a_conv_full[0], ("f_conv", 0): f_conv_full[0], ("f_conv", 1): f_conv_full[1]}
    later = [[k for k in units if k != ("a_w_in", 0)] for units in LAYER_UNITS]
    pending, after = [], ga
    for layer, keys in enumerate(later):
        pending.append(gather_start([shard[k] for k in keys], after, name=f"gather_start{layer}"))
        after = pending[-1][4]
    sm = {n: w[n] for n in SMALL}
    sm["norm_mix"] = sm["norm_mix"] + after[:1, :1]

    def weight(name, layer, act):
        if (name, layer) not in ready:
            landed = gather_wait(pending[layer], act, name=f"gather_wait{layer}")
            for k, g in zip(later[layer], landed):
                g = lax.dynamic_update_index_in_dim(g, shard[k], my_chip, 0)
                ready[k] = g if k[0] in BY_CHIP else g.reshape(N_CHIPS * g.shape[1], g.shape[2])
        return ready[name, layer]

    pairs, scattered, started = {}, {}, []

    def on_grads(layer, part, gw):
        keys = [k for k in LAYER_UNITS[layer] if (k[0] in LAYERED) == (part == "ffn")]
        from_sib = swap_units([gw[n] for n, _ in keys], name=f"rs_swap_{part}{layer}")
        for (n, _), sib in zip(keys, from_sib):
            pairs[n, layer] = pair_add(gw[n], sib, name=f"rs_add_pair_{n}{layer}")
        started.append((keys, scatter_start([pairs[k] for k in keys], name=f"rs_scatter_start_{part}{layer}"), f"{part}{layer}"))
        return started[-1][1][4]

    loss, grad_x, gs = local_step(x[0], p[:, 0], loss_target[0], sm, weight, on_grads)

    grads, delta, new_m, new_v, g_unit = {}, {}, {}, {}, {}

    def finish(keys, tag):
        halves = [chips_add(pairs[k], scattered[k], name=f"rs_add_chips_{k[0]}{k[1]}") for k in keys]
        g_unit.update(zip(keys, zip(halves, join_units(halves, name=f"rs_join_{tag}"))))
        for n in BIG:
            mine = [(n, i) for i in range(DEPTH) if (n, i) in LAYER_UNITS[i]]
            if n not in delta and all(k in g_unit for k in mine):
                g_layers = [g_unit[k] for k in mine]
                shape3 = (len(g_layers), 2 * g_layers[0][0].shape[0], g_layers[0][0].shape[1])
                res = adamw_layers(w[n].reshape(shape3), g_layers, m[n].reshape(shape3), v[n].reshape(shape3), name=f"adamw_{n}")
                delta[n], new_m[n], new_v[n], grads[n] = (r.reshape(w[n].shape) for r in res)

    last_keys, last_pending, last_tag = started[-1]
    for keys, pend, tag in started[:-1]:
        scattered.update(zip(keys, scatter_wait(pend, last_pending[4], name=f"rs_scatter_wait_{tag}")))
    finish([k for keys, _, _ in started[:-1] for k in keys], "first")

    conv_grads = _pack_rows([gs[n] for n in CONVS], CONV_GRAD_ROWS, F32)
    small_sum = sum8(allgather8(jnp.concatenate([_pack_small(gs, loss), conv_grads]), name="gather_small"), name="sum_small")
    g_sm, loss_sum = _unpack_small(small_sum[:SMALL_ROWS], sm)

    scattered.update(zip(last_keys, scatter_wait(last_pending, small_sum, name=f"rs_scatter_wait_{last_tag}")))
    finish(last_keys, "last")

    for n, full in zip(CONVS, _unpack(small_sum[SMALL_ROWS:].reshape(-1), [gs[n].shape for n in CONVS])):
        g2 = _as2d(lax.dynamic_slice_in_dim(full, my_chip * w[n].shape[-1], w[n].shape[-1], axis=full.ndim - 1))
        d2, m2, v2 = adamw(_as2d(w[n]), g2, _as2d(m[n]), _as2d(v[n]), name=f"adamw_{n}")
        grads[n], delta[n], new_m[n], new_v[n] = (r.reshape(w[n].shape) for r in (g2, d2, m2, v2))
    pk = lambda d: _pack_small(d)
    d2, m2, v2 = adamw(pk(sm), pk(g_sm), pk({n: m[n] for n in SMALL}), pk({n: v[n] for n in SMALL}), name="adamw_small")
    for src, dst in ((d2, delta), (m2, new_m), (v2, new_v)):
        dst.update(_unpack_small(src, sm)[0])
    grads.update(g_sm)

    order = ["norm_mix", "norm_ffn", "norm_ple", "norm_final", "a_w_in", "a_conv", "a_log", "a_dt_bias", "a_norm", "a_w_out",
             "b_w_in", "b_sinks", "b_w_out", "f_w_up", "f_conv", "f_w_down", "ple_w_proj", "ple_w_gate"]
    return (loss_sum, grad_x[None], *[grads[n] for n in order], *[delta[n] for n in order],
            *[new_m[n] for n in order], *[new_v[n] for n in order])
```

```python
import functools
import math

import jax
import jax.numpy as jnp
from jax import lax
from jax.experimental import pallas as pl
from jax.experimental.pallas import tpu as pltpu

F32 = jnp.float32
BF16 = jnp.bfloat16
MESH = pl.DeviceIdType.MESH

D_MODEL = 1024
N_HEADS_A = 8
HEAD_DIM_A = 128
CONV_A = 4
N_HEADS_B = 16
N_KV_B = 4
HEAD_DIM_B = 64
WINDOW = 128
D_FF = 2816
FFN_CONV = 3
PLE_DIM = 256
EPS = 1e-6
DEPTH = 2

ADAM_LR = 0.001
ADAM_B1 = 0.9
ADAM_B2 = 0.999
ADAM_EPS = 1e-08
ADAM_WD = 0.01
ADAM_STEP = 10

LANES = 128
SUBLANES = 8
BF16_ROWS = 16
CHUNK = 128
VMEM_LIMIT = 56 * 1024 * 1024
NEG = -1e30
N_CHIPS = 4
N_DEV = 8
PACK_COLS = 1024


def _params(sem=None):
    return pltpu.CompilerParams(dimension_semantics=sem, vmem_limit_bytes=VMEM_LIMIT)


def _tile(dim, cap):
    if dim % LANES:
        return dim
    best = LANES
    for t in range(LANES, min(dim, cap) + 1, LANES):
        if dim % t == 0:
            best = t
    return best


def _dot(a, b, dims=(((1,), (0,)), ((), ())), precision=None):
    return lax.dot_general(a, b, dims, precision=precision, preferred_element_type=F32)


NN = (((1,), (0,)), ((), ()))
NT = (((1,), (1,)), ((), ()))
TN = (((0,), (0,)), ((), ()))


MM_TM_CAP = 1024
MM_TK_CAP_TOKENS = 2048


def mm(a, b, *, name, ta=False, tb=False, out_dtype=F32, add=None, norm_w=None, norm_grad=None, tm_cap=MM_TM_CAP, tn_cap=1408,
       tk_cap=1408, n=None, tn=None, tk=None, b_spec=None, o_spec=None, o_shape=None, into=None):
    m, k = (a.shape[1], a.shape[0]) if ta else a.shape
    if b_spec is None:
        n = b.shape[0] if tb else b.shape[1]
        assert (b.shape[1] if tb else b.shape[0]) == k, (a.shape, b.shape, ta, tb)
    tm, tn, tk = _tile(m, tm_cap), tn or _tile(n, tn_cap), tk or _tile(k, MM_TK_CAP_TOKENS if ta else tk_cap)
    assert n % tn == 0 and k % tk == 0, (n, tn, k, tk)
    nk = k // tk
    dims = (((0 if ta else 1,), (1 if tb else 0,)), ((), ()))
    has_add, has_norm, has_grad = add is not None, norm_w is not None, norm_grad is not None
    assert not (has_norm or has_grad) or (tn == n and o_spec is None), "the norm epilogues need whole rows"
    n_in = 2 + has_add + has_norm + 3 * has_grad + (into is not None)

    def body(*refs):
        a_ref, b_ref = refs[0], refs[1]
        add_ref = refs[2] if has_add else None
        o_ref = refs[n_in]
        part = _dot(a_ref[...].astype(BF16), b_ref[...].astype(BF16), dims)
        first = pl.program_id(0) == 0

        def finish(r):
            if has_add:
                r = r + add_ref[...].astype(F32)
            if has_grad:
                h_ref, w_ref, prev_ref = refs[2 + has_add:5 + has_add]
                _, vjp = jax.vjp(_f_norm, h_ref[...], w_ref[...])
                r, dw = vjp(r)
                r = r + prev_ref[...]

                @pl.when(first)
                def _():
                    refs[n_in + 1][...] = dw

                @pl.when(jnp.logical_not(first))
                def _():
                    refs[n_in + 1][...] += dw
            o_ref[...] = r.astype(o_ref.dtype)
            if has_norm:
                refs[n_in + 1][...] = _f_norm(r, refs[2 + has_add][...]).astype(BF16)

        if nk == 1:
            finish(part)
            return
        acc = refs[-1]
        kk = pl.program_id(2)

        @pl.when(kk == 0)
        def _():
            acc[...] = part

        @pl.when(kk > 0)
        def _():
            acc[...] += part

        @pl.when(kk == nk - 1)
        def _():
            finish(acc[...])

    a_spec = pl.BlockSpec((tk, tm), lambda i, j, kk: (kk, i)) if ta else pl.BlockSpec((tm, tk), lambda i, j, kk: (i, kk))
    if b_spec is None:
        b_spec = pl.BlockSpec((tn, tk), lambda i, j, kk: (j, kk)) if tb else pl.BlockSpec((tk, tn), lambda i, j, kk: (kk, j))
    plain_o = pl.BlockSpec((tm, tn), lambda i, j, kk: (i, j))
    if o_spec is None:
        o_spec, o_shape = plain_o, (m, n)
    in_specs = [a_spec, b_spec] + ([plain_o] if has_add else [])
    args = (a, b) + ((add,) if has_add else ())
    out_specs, out_shapes = o_spec, jax.ShapeDtypeStruct(tuple(o_shape), out_dtype)
    one_row = pl.BlockSpec((1, n), lambda i, j, kk: (0, 0))
    if has_norm:
        in_specs.append(one_row)
        args += (norm_w,)
        out_specs, out_shapes = [o_spec, plain_o], [out_shapes, jax.ShapeDtypeStruct((m, n), BF16)]
    if has_grad:
        assert not has_norm
        in_specs += [plain_o, one_row, plain_o]
        args += tuple(norm_grad)
        out_specs, out_shapes = [o_spec, one_row], [out_shapes, jax.ShapeDtypeStruct((1, n), F32)]
    aliases = {}
    if into is not None:
        assert into.shape == tuple(o_shape) and into.dtype == out_dtype, (into.shape, o_shape)
        in_specs.append(pl.BlockSpec(memory_space=pl.ANY))
        args += (into,)
        aliases = {n_in - 1: 0}
    return pl.pallas_call(
        body, grid=(m // tm, n // tn, nk), in_specs=in_specs, out_specs=out_specs,
        out_shape=out_shapes, name=name, input_output_aliases=aliases,
        scratch_shapes=[pltpu.VMEM((tm, tn), F32)] if nk > 1 else [],
        compiler_params=_params(("arbitrary" if has_grad else "parallel", "parallel", "arbitrary")),
    )(*args)


def _row_spec(tm, cw, coff):
    return pl.BlockSpec((tm, cw), lambda i, j: (i, j + coff))


def _full_spec(shape):
    return pl.BlockSpec(shape, lambda i, j: (0,) * len(shape))


def tile_map(fn, rows, params, outs, *, tm, ncol, name):
    t = rows[0][0].shape[0]
    nin = len(rows) + len(params)

    def body(*refs):
        res = fn(*[r[...] for r in refs[:nin]])
        res = res if isinstance(res, (tuple, list)) else (res,)
        for o_ref, r in zip(refs[nin:], res):
            o_ref[...] = r.astype(o_ref.dtype)

    in_specs = [_row_spec(tm, cw, coff) for (_, cw, coff) in rows] + [_full_spec(p.shape) for p in params]
    res = pl.pallas_call(
        body, grid=(t // tm, ncol), in_specs=in_specs,
        out_specs=[_row_spec(tm, cw, 0) for (cw, _) in outs],
        out_shape=[jax.ShapeDtypeStruct((t, cw * ncol), dt) for (cw, dt) in outs], name=name,
        compiler_params=_params(("parallel", "parallel")),
    )(*[r[0] for r in rows], *params)
    return res


def tile_vjp(fn, rows, params, cts, *, n_diff, tm, ncol, name, grad_dtypes=None):
    t = rows[0][0].shape[0]
    nr, npar, nct = len(rows), len(params), len(cts)

    def body(*refs):
        vals = [r[...] for r in refs[:nr + npar + nct]]
        diff, rest, pars = vals[:n_diff], vals[n_diff:nr], vals[nr:nr + npar]
        ctv = vals[nr + npar:nr + npar + nct]
        outs_ref = refs[nr + npar + nct:]

        def f(*a):
            res = fn(*a[:n_diff], *rest, *a[n_diff:])
            return tuple(res) if isinstance(res, (tuple, list)) else (res,)

        primal, vjp = jax.vjp(f, *[d.astype(F32) for d in diff], *pars)
        grads = vjp(tuple(c.astype(o.dtype) for c, o in zip(ctv, primal)))
        for q in range(n_diff):
            outs_ref[q][...] = grads[q].astype(outs_ref[q].dtype)
        first = (pl.program_id(0) == 0) & (pl.program_id(1) == 0)
        for q in range(npar):
            o_ref, g = outs_ref[n_diff + q], grads[n_diff + q]

            @pl.when(first)
            def _(o_ref=o_ref, g=g):
                o_ref[...] = g

            @pl.when(jnp.logical_not(first))
            def _(o_ref=o_ref, g=g):
                o_ref[...] += g

    in_specs = [_row_spec(tm, cw, coff) for (_, cw, coff) in rows] + [_full_spec(p.shape) for p in params]
    in_specs += [_row_spec(tm, cw, coff) for (_, cw, coff) in cts]
    args = [r[0] for r in rows] + list(params) + [c[0] for c in cts]
    out_specs = [_row_spec(tm, rows[q][1], 0) for q in range(n_diff)] + [_full_spec(p.shape) for p in params]
    grad_dtypes = grad_dtypes or [F32] * n_diff
    out_shape = [jax.ShapeDtypeStruct((t, rows[q][1] * ncol), grad_dtypes[q]) for q in range(n_diff)]
    out_shape += [jax.ShapeDtypeStruct(p.shape, F32) for p in params]
    res = pl.pallas_call(
        body, grid=(t // tm, ncol), in_specs=in_specs, out_specs=out_specs, out_shape=out_shape, name=name,
        compiler_params=_params(("arbitrary", "arbitrary")),
    )(*args)
    return res[:n_diff], res[n_diff:]


def _silu(x):
    return x * jax.nn.sigmoid(x)


def _f_norm(h, w):
    return h * lax.rsqrt(jnp.mean(h * h, axis=-1, keepdims=True) + EPS) * w


def _f_gnorm(o, z, w):
    return _f_norm(o, w) * _silu(z)


def _f_act(gate, val):
    return _silu(gate) * val


def _f_ple(gl, pe):
    return jax.nn.sigmoid(gl) * pe


def _f_betag(pt, alog, dtb):
    lane = lax.broadcasted_iota(jnp.int32, (1, LANES), 1)
    z = pt + dtb
    softplus = jnp.maximum(z, 0.0) + jnp.log(1.0 + jnp.exp(-jnp.abs(z)))
    g = -jnp.exp(alog) * softplus
    return jnp.where(lane < N_HEADS_A, jax.nn.sigmoid(pt), jnp.where(lane < 2 * N_HEADS_A, g, 0.0))


CONV_TM = 256
CONV_CW = 1024


def _shift_down(x, prev, s, row):
    rp = jnp.tile(pltpu.roll(prev, s, 0), (x.shape[0] // SUBLANES, 1))
    return jnp.where(row < s, rp, pltpu.roll(x, s, 0))


def _shift_up(x, nxt, s, row):
    tm = x.shape[0]
    rn = jnp.tile(pltpu.roll(nxt, SUBLANES - s, 0), (tm // SUBLANES, 1))
    return jnp.where(row >= tm - s, rn, pltpu.roll(x, tm - s, 0))


def _conv_taps(x, prev, w_ref, cols, row):
    k = w_ref.shape[0]
    y = x * w_ref[pl.ds(k - 1, 1), cols]
    for s in range(1, k):
        y = y + _shift_down(x, prev, s, row) * w_ref[pl.ds(k - 1 - s, 1), cols]
    return y


def _lane_chunks(cw):
    return [slice(cb * LANES, (cb + 1) * LANES) for cb in range(cw // LANES)]


def conv_fwd(x, w, *, name):
    t = x.shape[0]
    k, c = w.shape
    tm, cw = min(CONV_TM, t), CONV_CW
    nb8 = tm // SUBLANES

    def body(x_ref, p_ref, w_ref, o_ref):
        first = pl.program_id(1) == 0
        row = lax.broadcasted_iota(jnp.int32, (tm, LANES), 0)
        for cols in _lane_chunks(cw):
            o_ref[:, cols] = _conv_taps(x_ref[:, cols], jnp.where(first, 0.0, p_ref[:, cols]), w_ref, cols, row)

    return pl.pallas_call(
        body, grid=(c // cw, t // tm),
        in_specs=[pl.BlockSpec((tm, cw), lambda j, i: (i, j)),
                  pl.BlockSpec((SUBLANES, cw), lambda j, i: (jnp.maximum(i * nb8 - 1, 0), j)),
                  pl.BlockSpec((k, cw), lambda j, i: (0, j))],
        out_specs=pl.BlockSpec((tm, cw), lambda j, i: (i, j)),
        out_shape=jax.ShapeDtypeStruct((t, c), F32), name=name,
        compiler_params=_params(("parallel", "parallel")),
    )(x, x, w)


def conv_bwd(dy, x, w, *, name):
    t = x.shape[0]
    k, c = w.shape
    tm, cw = min(CONV_TM, t), CONV_CW
    nb8 = tm // SUBLANES
    ni = t // tm

    def body(dy_ref, dn_ref, x_ref, p_ref, w_ref, dx_ref, dw_ref):
        i = pl.program_id(1)
        first, last = i == 0, i == ni - 1
        row = lax.broadcasted_iota(jnp.int32, (tm, LANES), 0)
        for cols in _lane_chunks(cw):
            dyv, xv = dy_ref[:, cols], x_ref[:, cols]
            nxt = jnp.where(last, 0.0, dn_ref[:, cols])
            prev = jnp.where(first, 0.0, p_ref[:, cols])
            dx = dyv * w_ref[pl.ds(k - 1, 1), cols]
            dws = [jnp.sum(dyv * xv, axis=0, keepdims=True)]
            for s in range(1, k):
                dx = dx + _shift_up(dyv, nxt, s, row) * w_ref[pl.ds(k - 1 - s, 1), cols]
                dws.append(jnp.sum(dyv * _shift_down(xv, prev, s, row), axis=0, keepdims=True))
            dx_ref[:, cols] = dx.astype(dx_ref.dtype)
            for s in range(k):
                @pl.when(first)
                def _(s=s, dws=dws, cols=cols):
                    dw_ref[pl.ds(k - 1 - s, 1), cols] = dws[s]

                @pl.when(jnp.logical_not(first))
                def _(s=s, dws=dws, cols=cols):
                    dw_ref[pl.ds(k - 1 - s, 1), cols] += dws[s]

    return pl.pallas_call(
        body, grid=(c // cw, ni),
        in_specs=[pl.BlockSpec((tm, cw), lambda j, i: (i, j)),
                  pl.BlockSpec((SUBLANES, cw), lambda j, i: (jnp.minimum((i + 1) * nb8, t // SUBLANES - 1), j)),
                  pl.BlockSpec((tm, cw), lambda j, i: (i, j)),
                  pl.BlockSpec((SUBLANES, cw), lambda j, i: (jnp.maximum(i * nb8 - 1, 0), j)),
                  pl.BlockSpec((k, cw), lambda j, i: (0, j))],
        out_specs=[pl.BlockSpec((tm, cw), lambda j, i: (i, j)), pl.BlockSpec((k, cw), lambda j, i: (0, j))],
        out_shape=[jax.ShapeDtypeStruct((t, c), BF16), jax.ShapeDtypeStruct((k, c), F32)], name=name,
        compiler_params=_params(("parallel", "arbitrary")),
    )(dy, dy, x, x, w)


FFN_TM = 128
FFN_CW = D_FF // 2


def _ffn_specs(t, tm, cw, k):
    ncol = D_FF // cw
    cur = lambda off: pl.BlockSpec((tm, cw), lambda j, i: (i, j + off))
    prev = lambda off, hr: pl.BlockSpec((hr, cw), lambda j, i: (jnp.maximum(i * (tm // hr) - 1, 0), j + off))
    nxt = lambda off, hr: pl.BlockSpec((hr, cw), lambda j, i: (jnp.minimum((i + 1) * (tm // hr), t // hr - 1), j + off))
    taps = lambda off: pl.BlockSpec((k, cw), lambda j, i: (0, j + off))
    return cur, prev, nxt, taps, ncol


def _rows_before(ref, cols, first):
    return jnp.where(first, 0.0, ref[ref.shape[0] - SUBLANES:, cols].astype(F32))


def conv_act_fwd(u, w, *, name):
    t, k = u.shape[0], w.shape[0]
    tm, cw = min(FFN_TM, t), FFN_CW
    cur, prev, _, taps, ncol = _ffn_specs(t, tm, cw, k)

    def body(ug_ref, pg_ref, uv_ref, pv_ref, wg_ref, wv_ref, o_ref):
        first = pl.program_id(1) == 0
        row = lax.broadcasted_iota(jnp.int32, (tm, LANES), 0)
        for cb in range(cw // LANES):
            cols = slice(cb * LANES, (cb + 1) * LANES)
            cg = _conv_taps(ug_ref[:, cols].astype(F32), _rows_before(pg_ref, cols, first), wg_ref, cols, row)
            cv = _conv_taps(uv_ref[:, cols].astype(F32), _rows_before(pv_ref, cols, first), wv_ref, cols, row)
            o_ref[:, cols] = _f_act(cg, cv).astype(o_ref.dtype)

    return pl.pallas_call(
        body, grid=(ncol, t // tm),
        in_specs=[cur(0), prev(0, BF16_ROWS), cur(ncol), prev(ncol, BF16_ROWS), taps(0), taps(ncol)],
        out_specs=cur(0), out_shape=jax.ShapeDtypeStruct((t, D_FF), BF16), name=name,
        compiler_params=_params(("parallel", "parallel")),
    )(u, u, u, u, w, w)


def conv_act_bwd(u, dact, w, *, name):
    t, k = u.shape[0], w.shape[0]
    tm, cw = min(FFN_TM, t), FFN_CW
    cur, prev, nxt, taps, ncol = _ffn_specs(t, tm, cw, k)
    ni = t // tm

    def body(ug_ref, pg_ref, ng_ref, uv_ref, pv_ref, nv_ref, d_ref, dn_ref, wg_ref, wv_ref, dg_ref, dv_ref, dwg_ref, dwv_ref):
        i = pl.program_id(1)
        first, last = i == 0, i == ni - 1
        row = lax.broadcasted_iota(jnp.int32, (tm, LANES), 0)
        row8 = lax.broadcasted_iota(jnp.int32, (SUBLANES, LANES), 0)
        for cb in range(cw // LANES):
            cols = slice(cb * LANES, (cb + 1) * LANES)
            ug, uv = ug_ref[:, cols].astype(F32), uv_ref[:, cols].astype(F32)
            pg, pv = _rows_before(pg_ref, cols, first), _rows_before(pv_ref, cols, first)
            sg = [ug] + [_shift_down(ug, pg, s, row) for s in range(1, k)]
            sv = [uv] + [_shift_down(uv, pv, s, row) for s in range(1, k)]
            taps = lambda xs, w_ref: sum(xs[s] * w_ref[pl.ds(k - 1 - s, 1), cols] for s in range(k))
            _, vjp = jax.vjp(_f_act, taps(sg, wg_ref), taps(sv, wv_ref))
            dcg, dcv = vjp(d_ref[:, cols])
            after = lambda ref: ref[:SUBLANES, cols].astype(F32)
            _, vjp_n = jax.vjp(_f_act, _conv_taps(after(ng_ref), ug[tm - SUBLANES:], wg_ref, cols, row8),
                               _conv_taps(after(nv_ref), uv[tm - SUBLANES:], wv_ref, cols, row8))
            dcgn, dcvn = vjp_n(jnp.where(last, 0.0, dn_ref[:, cols]))
            for dc, dcn, xs, w_ref, dx_ref, dw_ref in ((dcg, dcgn, sg, wg_ref, dg_ref, dwg_ref),
                                                       (dcv, dcvn, sv, wv_ref, dv_ref, dwv_ref)):
                dx = dc * w_ref[pl.ds(k - 1, 1), cols]
                dws = [jnp.sum(dc * xs[0], axis=0, keepdims=True)]
                for s in range(1, k):
                    dx = dx + _shift_up(dc, dcn, s, row) * w_ref[pl.ds(k - 1 - s, 1), cols]
                    dws.append(jnp.sum(dc * xs[s], axis=0, keepdims=True))
                dx_ref[:, cols] = dx.astype(dx_ref.dtype)
                for s in range(k):
                    @pl.when(first)
                    def _(s=s, dw_ref=dw_ref, dws=dws):
                        dw_ref[pl.ds(k - 1 - s, 1), cols] = dws[s]

                    @pl.when(jnp.logical_not(first))
                    def _(s=s, dw_ref=dw_ref, dws=dws):
                        dw_ref[pl.ds(k - 1 - s, 1), cols] += dws[s]

    half = jax.ShapeDtypeStruct((t, D_FF), BF16)
    dwh = jax.ShapeDtypeStruct((k, D_FF), F32)
    return pl.pallas_call(
        body, grid=(ncol, ni),
        in_specs=[cur(0), prev(0, BF16_ROWS), nxt(0, BF16_ROWS), cur(ncol), prev(ncol, BF16_ROWS), nxt(ncol, BF16_ROWS),
                  cur(0), nxt(0, SUBLANES), taps(0), taps(ncol)],
        out_specs=[cur(0), cur(0), taps(0), taps(0)], out_shape=[half, half, dwh, dwh], name=name,
        compiler_params=_params(("parallel", "arbitrary")),
    )(u, u, u, u, u, u, dact, dact, w, w)


def _each(f, *lists):
    return [f(*a) for a in zip(*lists)]


@jax.custom_vjp
def _inv_unit_lower(lms):
    return _inv_blocks(lms)


def _inv_blocks(lms):
    c = lms[0].shape[0]
    ri = lax.broadcasted_iota(jnp.int32, (c, c), 0)
    ci = lax.broadcasted_iota(jnp.int32, (c, c), 1)
    eye = (ri == ci).astype(F32)
    dms = _each(lambda lm: eye - jnp.where((ri >> 1) == (ci >> 1), lm, 0.0), lms)
    for lv in range(1, int(math.log2(c))):
        below = ((ri >> (lv + 1)) == (ci >> (lv + 1))) & ((ri >> lv) != (ci >> lv))
        dbs = _each(lambda dm: dm.astype(BF16), dms)
        ods = _each(lambda lm, db: _dot(jnp.where(below, lm, 0.0).astype(BF16), db).astype(BF16), lms, dbs)
        dms = _each(lambda dm, db, od: dm - _dot(db, od), dms, dbs, ods)
    return dms


def _inv_fwd(lms):
    tms = _inv_blocks(lms)
    return tms, tms


def _inv_bwd(tms, dts):
    tbs = _each(lambda tm: tm.astype(BF16), tms)
    mid = _each(lambda tb, dt: _dot(tb, dt.astype(BF16), TN).astype(BF16), tbs, dts)
    return (_each(lambda m, tb: -_dot(m, tb, NT), mid, tbs),)


_inv_unit_lower.defvjp(_inv_fwd, _inv_bwd)


@jax.custom_vjp
def _inv_known(lms, tms):
    return tms


_inv_known.defvjp(lambda lms, tms: (tms, tms), lambda tms, dts: _inv_bwd(tms, dts) + (_each(jnp.zeros_like, tms),))


def _l2n(x):
    return x * lax.rsqrt(jnp.sum(x * x, axis=-1, keepdims=True) + EPS)


def _prep_fn(cqs, cks, cvs, bg, sel_b, sel_g, tms=None):
    c = cqs[0].shape[0]
    ri = lax.broadcasted_iota(jnp.int32, (c, c), 0)
    ci = lax.broadcasted_iota(jnp.int32, (c, c), 1)
    eye = (ri == ci).astype(F32)
    incl, strict = ci <= ri, ci < ri
    last = lax.broadcasted_iota(jnp.int32, (c, 1), 0) == c - 1
    to_row = lambda col: jnp.sum(col * eye, axis=0, keepdims=True)
    qs = _each(lambda a: _l2n(_silu(a)) * (HEAD_DIM_A ** -0.5), cqs)
    ks = _each(lambda a: _l2n(_silu(a)), cks)
    vbs = _each(lambda a: _silu(a).astype(BF16), cvs)
    betas = _each(lambda m: jnp.sum(bg * m, axis=1, keepdims=True), sel_b)
    gs = _each(lambda m: jnp.sum(bg * m, axis=1, keepdims=True), sel_g)
    gcss = _each(lambda g: jnp.sum(jnp.where(incl, to_row(g), 0.0), axis=1, keepdims=True), gs)
    gtots = _each(lambda gcs: jnp.sum(jnp.where(last, gcs, 0.0), axis=0, keepdims=True), gcss)
    decays = _each(lambda gcs: jnp.exp(jnp.where(incl, gcs - to_row(gcs), NEG)), gcss)
    kbs = _each(lambda k: k.astype(BF16), ks)
    lms = _each(lambda beta, kb, dec: jnp.where(strict, beta * _dot(kb, kb, NT) * dec, 0.0), betas, kbs, decays)
    tms = _inv_unit_lower(lms) if tms is None else _inv_known(lms, tms)
    ams = _each(lambda tm, beta: (tm * to_row(beta)).astype(BF16), tms, betas)
    gams = _each(jnp.exp, gcss)
    u0s = _each(_dot, ams, vbs)
    wks = _each(lambda am, gam, k: _dot(am, (gam * k).astype(BF16)), ams, gams, ks)
    qks = _each(lambda q, kb, dec: _dot(q.astype(BF16), kb, NT) * dec, qs, kbs, decays)
    qds = _each(lambda q, gam: q * gam, qs, gams)
    kds = _each(lambda k, gtot, gcs: k * jnp.exp(gtot - gcs), ks, gtots, gcss)
    gls = _each(lambda gtot: jnp.exp(gtot) * jnp.ones((SUBLANES, LANES), F32), gtots)
    return u0s, wks, qds, kds, qks, gls, tms


def _head_masks(h):
    lane = lax.broadcasted_iota(jnp.int32, (1, LANES), 1)
    return (lane == h).astype(F32), (lane == h + N_HEADS_A).astype(F32)


def _hsl(j):
    return slice(j * HEAD_DIM_A, (j + 1) * HEAD_DIM_A)


def gnorm_fwd(o, zsrc, w, *, name):
    t, width = o.shape
    tm = min(256, t)
    zoff = zsrc.shape[1] // width - 1

    def body(o_ref, z_ref, w_ref, out_ref):
        for h in range(N_HEADS_A):
            out_ref[:, _hsl(h)] = _f_gnorm(o_ref[:, _hsl(h)], z_ref[:, _hsl(h)], w_ref[...]).astype(out_ref.dtype)

    rows = pl.BlockSpec((tm, width), lambda i: (i, 0))
    return pl.pallas_call(
        body, grid=(t // tm,),
        in_specs=[rows, pl.BlockSpec((tm, width), lambda i: (i, zoff)), pl.BlockSpec(w.shape, lambda i: (0, 0))],
        out_specs=rows, out_shape=jax.ShapeDtypeStruct((t, width), BF16), name=name, compiler_params=_params(("parallel",)),
    )(o, zsrc, w)


def gnorm_bwd(o, zsrc, w, don, *, name):
    t, width = o.shape
    tm = min(256, t)
    zoff = zsrc.shape[1] // width - 1

    def body(o_ref, z_ref, w_ref, d_ref, do_ref, dz_ref, dw_ref):
        dw = jnp.zeros(w.shape, F32)
        for h in range(N_HEADS_A):
            _, vjp = jax.vjp(_f_gnorm, o_ref[:, _hsl(h)], z_ref[:, _hsl(h)], w_ref[...])
            do, dz, dwh = vjp(d_ref[:, _hsl(h)])
            do_ref[:, _hsl(h)] = do.astype(do_ref.dtype)
            dz_ref[:, _hsl(h)] = dz.astype(dz_ref.dtype)
            dw = dw + dwh
        first = pl.program_id(0) == 0

        @pl.when(first)
        def _():
            dw_ref[...] = dw

        @pl.when(jnp.logical_not(first))
        def _():
            dw_ref[...] += dw

    rows = pl.BlockSpec((tm, width), lambda i: (i, 0))
    wspec = pl.BlockSpec(w.shape, lambda i: (0, 0))
    return pl.pallas_call(
        body, grid=(t // tm,),
        in_specs=[rows, pl.BlockSpec((tm, width), lambda i: (i, zoff)), wspec, rows],
        out_specs=[rows, rows, wspec],
        out_shape=[jax.ShapeDtypeStruct((t, width), BF16)] * 2 + [jax.ShapeDtypeStruct(w.shape, F32)], name=name,
        compiler_params=_params(("arbitrary",)),
    )(o, zsrc, w, don)


def delta_prep(cqkv, bg, *, name):
    t = cqkv.shape[0]
    nh, hd, n = N_HEADS_A, HEAD_DIM_A, t // CHUNK

    def body(cq_ref, ck_ref, cv_ref, bg_ref, u0_ref, wk_ref, qd_ref, kd_ref, qk_ref, tm_ref, gl_ref):
        heads = range(nh)
        masks = [_head_masks(j) for j in heads]
        res = _prep_fn([cq_ref[:, _hsl(j)] for j in heads], [ck_ref[:, _hsl(j)] for j in heads],
                       [cv_ref[:, _hsl(j)] for j in heads], bg_ref[...], [m[0] for m in masks], [m[1] for m in masks])
        for o_ref, rs in zip((u0_ref, wk_ref, qd_ref, kd_ref, qk_ref, tm_ref), res[:5] + (res[6],)):
            for j in heads:
                o_ref[:, _hsl(j)] = rs[j].astype(o_ref.dtype)
        for j in heads:
            gl_ref[j * SUBLANES:(j + 1) * SUBLANES, :] = res[5][j]

    blk = lambda off: pl.BlockSpec((CHUNK, nh * hd), lambda i: (i, off))
    res = pl.pallas_call(
        body, grid=(n,),
        in_specs=[blk(0), blk(1), blk(2), pl.BlockSpec((CHUNK, LANES), lambda i: (i, 0))],
        out_specs=[blk(0)] * 6 + [pl.BlockSpec((nh * SUBLANES, LANES), lambda i: (i, 0))],
        out_shape=[jax.ShapeDtypeStruct((t, nh * hd), dt) for dt in (F32, BF16, BF16, BF16, BF16, F32)]
        + [jax.ShapeDtypeStruct((n * nh * SUBLANES, LANES), F32)],
        name=name, compiler_params=_params(("parallel",)),
    )(cqkv, cqkv, cqkv, bg)
    return [*res[:5], res[6]], res[5]


def delta_prep_bwd(cqkv, bg, tms, cts, *, name):
    t = cqkv.shape[0]
    nh, hd, n = N_HEADS_A, HEAD_DIM_A, t // CHUNK

    def body(cq_ref, ck_ref, cv_ref, bg_ref, tm_ref, c0, c1, c2, c3, c4, c5, dc_ref, dbg_ref):
        heads = range(nh)
        masks = [_head_masks(j) for j in heads]
        known = [tm_ref[:, _hsl(j)] for j in heads]
        _, vjp = jax.vjp(lambda a, b, c, d: _prep_fn(a, b, c, d, [m[0] for m in masks], [m[1] for m in masks], known)[:6],
                         [cq_ref[:, _hsl(j)] for j in heads], [ck_ref[:, _hsl(j)] for j in heads],
                         [cv_ref[:, _hsl(j)] for j in heads], bg_ref[...])
        cts = tuple([c[:, _hsl(j)] for j in heads] for c in (c0, c1, c2, c3, c4))
        dqs, dks, dvs, dbg = vjp(cts + ([c5[j * SUBLANES:(j + 1) * SUBLANES, :] for j in heads],))
        for part, ds in enumerate((dqs, dks, dvs)):
            for j in heads:
                dc_ref[:, _hsl(part * nh + j)] = ds[j]
        dbg_ref[...] = dbg

    blk = lambda off: pl.BlockSpec((CHUNK, nh * hd), lambda i: (i, off))
    gl_spec = pl.BlockSpec((nh * SUBLANES, LANES), lambda i: (i, 0))
    bg_spec = pl.BlockSpec((CHUNK, LANES), lambda i: (i, 0))
    return pl.pallas_call(
        body, grid=(n,),
        in_specs=[blk(0), blk(1), blk(2), bg_spec] + [blk(0)] * 6 + [gl_spec],
        out_specs=[pl.BlockSpec((CHUNK, 3 * nh * hd), lambda i: (i, 0)), bg_spec],
        out_shape=[jax.ShapeDtypeStruct((t, 3 * nh * hd), F32), jax.ShapeDtypeStruct((t, LANES), F32)],
        name=name, compiler_params=_params(("parallel",)),
    )(cqkv, cqkv, cqkv, bg, tms, *cts)


def delta_scan(u0, wk, qd, kd, qk, gl, *, name):
    t = u0.shape[0]
    nh, hd, n = N_HEADS_A, HEAD_DIM_A, t // CHUNK

    def body(u0_ref, wk_ref, qd_ref, kd_ref, qk_ref, gl_ref, o_ref, sin_ref, s_ref):
        @pl.when(pl.program_id(0) == 0)
        def _():
            s_ref[...] = jnp.zeros_like(s_ref)

        heads = list(range(nh))
        cols = lambda ref: [ref[:, _hsl(h)].astype(BF16) for h in heads]
        ss = [s_ref[h] for h in heads]
        for h in heads:
            sin_ref[h] = ss[h]
        sbs = _each(lambda s: s.astype(BF16), ss)
        ubs = _each(lambda h, wkb, sb: (u0_ref[:, _hsl(h)] - _dot(wkb, sb)).astype(BF16), heads, cols(wk_ref), sbs)
        os_ = _each(lambda qdb, sb, qkb, ub: _dot(qdb, sb) + _dot(qkb, ub), cols(qd_ref), sbs, cols(qk_ref), ubs)
        sn = _each(lambda h, s, kdb, ub: gl_ref[pl.ds(h * SUBLANES, 1), :] * s + _dot(kdb, ub, TN), heads, ss, cols(kd_ref), ubs)
        for h in heads:
            o_ref[:, _hsl(h)] = os_[h]
            s_ref[h] = sn[h]

    blk = pl.BlockSpec((CHUNK, nh * hd), lambda i: (i, 0))
    return pl.pallas_call(
        body, grid=(n,),
        in_specs=[blk] * 5 + [pl.BlockSpec((nh * SUBLANES, LANES), lambda i: (i, 0))],
        out_specs=[blk, pl.BlockSpec((None, nh, hd, hd), lambda i: (i, 0, 0, 0))],
        out_shape=[jax.ShapeDtypeStruct((t, nh * hd), F32), jax.ShapeDtypeStruct((n, nh, hd, hd), F32)],
        scratch_shapes=[pltpu.VMEM((nh, hd, hd), F32)], name=name,
        compiler_params=_params(("arbitrary",)),
    )(u0, wk, qd, kd, qk, gl)


def delta_scan_bwd(do, u0, wk, qd, kd, qk, gl, s_in, *, name):
    t = u0.shape[0]
    nh, hd, n = N_HEADS_A, HEAD_DIM_A, t // CHUNK

    def body(do_ref, u0_ref, wk_ref, qd_ref, kd_ref, qk_ref, gl_ref, sin_ref,
             du0_ref, dwk_ref, dqd_ref, dkd_ref, dqk_ref, dgl_ref, ds_ref):
        @pl.when(pl.program_id(0) == 0)
        def _():
            ds_ref[...] = jnp.zeros_like(ds_ref)

        corner = (lax.broadcasted_iota(jnp.int32, (SUBLANES, LANES), 0) == 0) & (lax.broadcasted_iota(jnp.int32, (SUBLANES, LANES), 1) == 0)
        heads = list(range(nh))
        cols = lambda ref: [ref[:, _hsl(h)].astype(BF16) for h in heads]
        ss, dss = [sin_ref[h] for h in heads], [ds_ref[h] for h in heads]
        sbs, dsbs = _each(lambda s: s.astype(BF16), ss), _each(lambda d: d.astype(BF16), dss)
        dobs, wkbs, qdbs, kdbs, qkbs = cols(do_ref), cols(wk_ref), cols(qd_ref), cols(kd_ref), cols(qk_ref)
        ubs = _each(lambda h, wkb, sb: (u0_ref[:, _hsl(h)] - _dot(wkb, sb)).astype(BF16), heads, wkbs, sbs)
        dus = _each(lambda qkb, dob, kdb, dsb: _dot(qkb, dob, TN) + _dot(kdb, dsb), qkbs, dobs, kdbs, dsbs)
        dubs = _each(lambda du: du.astype(BF16), dus)
        dwks = _each(lambda dub, sb: -_dot(dub, sb, NT), dubs, sbs)
        dqds = _each(lambda dob, sb: _dot(dob, sb, NT), dobs, sbs)
        dkds = _each(lambda ub, dsb: _dot(ub, dsb, NT), ubs, dsbs)
        dqks = _each(lambda dob, ub: _dot(dob, ub, NT), dobs, ubs)
        dgls = _each(lambda s, d: jnp.sum(jnp.sum(s * d, axis=1, keepdims=True), axis=0, keepdims=True), ss, dss)
        dsn = _each(lambda h, d, qdb, dob, wkb, dub: gl_ref[pl.ds(h * SUBLANES, 1), :] * d + _dot(qdb, dob, TN) - _dot(wkb, dub, TN),
                    heads, dss, qdbs, dobs, wkbs, dubs)
        for h in heads:
            du0_ref[:, _hsl(h)] = dus[h]
            dwk_ref[:, _hsl(h)] = dwks[h]
            dqd_ref[:, _hsl(h)] = dqds[h]
            dkd_ref[:, _hsl(h)] = dkds[h]
            dqk_ref[:, _hsl(h)] = dqks[h]
            dgl_ref[h * SUBLANES:(h + 1) * SUBLANES, :] = jnp.where(corner, dgls[h], 0.0)
            ds_ref[h] = dsn[h]

    blk = pl.BlockSpec((CHUNK, nh * hd), lambda i: (n - 1 - i, 0))
    gl_spec = pl.BlockSpec((nh * SUBLANES, LANES), lambda i: (n - 1 - i, 0))
    return pl.pallas_call(
        body, grid=(n,),
        in_specs=[blk] * 6 + [gl_spec, pl.BlockSpec((None, nh, hd, hd), lambda i: (n - 1 - i, 0, 0, 0))],
        out_specs=[blk] * 5 + [gl_spec],
        out_shape=[jax.ShapeDtypeStruct((t, nh * hd), F32)] * 5 + [jax.ShapeDtypeStruct((n * nh * SUBLANES, LANES), F32)],
        scratch_shapes=[pltpu.VMEM((nh, hd, hd), F32)], name=name,
        compiler_params=_params(("arbitrary",)),
    )(do, u0, wk, qd, kd, qk, gl, s_in)


N_PAIRS = N_HEADS_B // 2
PAIRS_PER_KV = N_PAIRS // N_KV_B


def _psl(j):
    return slice(j * LANES, (j + 1) * LANES)


KV_STEP = 4


def _att_fn(qps, kcs, kps, vcs, vps, sinks, kv0, first):
    w = WINDOW
    lane = lax.broadcasted_iota(jnp.int32, (1, LANES), 1)
    lo = (lane < HEAD_DIM_B).astype(F32)
    qi = lax.broadcasted_iota(jnp.int32, (w, w), 0)
    kj = lax.broadcasted_iota(jnp.int32, (w, w), 1)
    dist_c = (qi - kj).astype(F32)
    valid_c = kj <= qi
    valid_p = (kj > qi) & (first < 0.5)
    bf = lambda xs: [a.astype(BF16) for a in xs]
    kcb, kpb, vcb, vpb = bf(kcs), bf(kps), bf(vcs), bf(vps)
    scale = HEAD_DIM_B ** -0.5
    heads = [(g, j, half) for g in range(len(kcs)) for j in range(PAIRS_PER_KV) for half in range(2)]
    kvs = [g for g, _, _ in heads]
    hmasks = [lo if half == 0 else 1.0 - lo for _, _, half in heads]
    hds = [2.0 * (PAIRS_PER_KV * (kv0 + g) + j) + half for g, j, half in heads]
    slopes = _each(lambda hd: jnp.exp(-(hd + 1.0) * (8.0 / N_HEADS_B * math.log(2.0))), hds)
    snks = _each(lambda hd: jnp.sum(sinks * (lane.astype(F32) == hd).astype(F32), axis=1, keepdims=True), hds)
    qhs = _each(lambda h, hm: (qps[h[0] * PAIRS_PER_KV + h[1]] * hm).astype(BF16), heads, hmasks)
    lcs = _each(lambda qh, g, sl: jnp.where(valid_c, _dot(qh, kcb[g], NT) * scale - sl * dist_c, NEG), qhs, kvs, slopes)
    lps = _each(lambda qh, g, sl: jnp.where(valid_p, _dot(qh, kpb[g], NT) * scale - sl * (dist_c + w), NEG), qhs, kvs, slopes)
    ms = _each(lambda lc, lp, sk: lax.stop_gradient(jnp.maximum(jnp.maximum(jnp.max(lc, axis=1, keepdims=True),
                                                                            jnp.max(lp, axis=1, keepdims=True)), sk)), lcs, lps, snks)
    ecs = _each(lambda lc, m: jnp.exp(lc - m), lcs, ms)
    eps = _each(lambda lp, m: jnp.exp(lp - m), lps, ms)
    invs = _each(lambda ec, ep, sk, m: 1.0 / (jnp.sum(ec, axis=1, keepdims=True) + jnp.sum(ep, axis=1, keepdims=True) + jnp.exp(sk - m)),
                 ecs, eps, snks, ms)
    ohs = _each(lambda ec, ep, inv, g, hm: (_dot((ec * inv).astype(BF16), vcb[g]) + _dot((ep * inv).astype(BF16), vpb[g])) * hm,
                ecs, eps, invs, kvs, hmasks)
    return [ohs[2 * j] + ohs[2 * j + 1] for j in range(len(qps))]


def _scalar11(v):
    return jnp.full((1, 1), v, F32)


def _att_specs(row_of):
    cur = pl.BlockSpec((WINDOW, KV_STEP * LANES), lambda i, kv: (row_of(i), kv))
    prev = pl.BlockSpec((WINDOW, KV_STEP * LANES), lambda i, kv: (jnp.maximum(row_of(i) - 1, 0), kv))
    qs = pl.BlockSpec((WINDOW, KV_STEP * PAIRS_PER_KV * LANES), lambda i, kv: (row_of(i), kv))
    return qs, cur, prev, pl.BlockSpec((1, LANES), lambda i, kv: (0, 0))


def swa_fwd(qsrc, kd, vd, sinks, *, name):
    t = kd.shape[0]
    nb = t // WINDOW
    npair = KV_STEP * PAIRS_PER_KV

    def body(q_ref, kc_ref, kp_ref, vc_ref, vp_ref, s_ref, o_ref):
        first = _scalar11((pl.program_id(0) == 0).astype(F32))
        kv0 = _scalar11((pl.program_id(1) * KV_STEP).astype(F32))
        per_kv = lambda ref: [ref[:, _psl(g)] for g in range(KV_STEP)]
        outs = _att_fn([q_ref[:, _psl(j)] for j in range(npair)], per_kv(kc_ref), per_kv(kp_ref), per_kv(vc_ref), per_kv(vp_ref),
                       s_ref[...], kv0, first)
        for j in range(npair):
            o_ref[:, _psl(j)] = outs[j].astype(o_ref.dtype)

    qs, cur, prev, sk = _att_specs(lambda i: i)
    return pl.pallas_call(
        body, grid=(nb, N_KV_B // KV_STEP), in_specs=[qs, cur, prev, cur, prev, sk],
        out_specs=qs, out_shape=jax.ShapeDtypeStruct((t, N_PAIRS * LANES), BF16), name=name,
        compiler_params=_params(("parallel", "parallel")),
    )(qsrc, kd, kd, vd, vd, sinks)


def swa_bwd(do, qsrc, kd, vd, sinks, *, name):
    t = kd.shape[0]
    nb = t // WINDOW

    npair = KV_STEP * PAIRS_PER_KV

    def body(do_ref, q_ref, kc_ref, kp_ref, vc_ref, vp_ref, s_ref, dq_ref, dk_ref, dv_ref, ds_ref, carry_k, carry_v):
        step, kvg = pl.program_id(0), pl.program_id(1)
        first = _scalar11((step == nb - 1).astype(F32))

        @pl.when((step == 0) & (kvg == 0))
        def _():
            carry_k[...] = jnp.zeros_like(carry_k)
            carry_v[...] = jnp.zeros_like(carry_v)
            ds_ref[...] = jnp.zeros_like(ds_ref)

        kv0 = _scalar11((kvg * KV_STEP).astype(F32))
        per_kv = lambda ref: [ref[:, _psl(g)].astype(F32) for g in range(KV_STEP)]
        _, vjp = jax.vjp(lambda *a: _att_fn(*a, kv0, first), [q_ref[:, _psl(j)].astype(F32) for j in range(npair)],
                         per_kv(kc_ref), per_kv(kp_ref), per_kv(vc_ref), per_kv(vp_ref), s_ref[...])
        dqs, dkc, dkp, dvc, dvp, dsk = vjp([do_ref[:, _psl(j)].astype(F32) for j in range(npair)])
        for j in range(npair):
            dq_ref[:, _psl(j)] = dqs[j].astype(dq_ref.dtype)
        ds_ref[...] += dsk
        fold = lambda g: g + pltpu.roll(g, HEAD_DIM_B, 1)
        for g in range(KV_STEP):
            kv = kvg * KV_STEP + g
            dk_ref[:, _psl(g)] = fold(dkc[g] + carry_k[kv]).astype(dk_ref.dtype)
            dv_ref[:, _psl(g)] = fold(dvc[g] + carry_v[kv]).astype(dv_ref.dtype)
            carry_k[kv] = dkp[g]
            carry_v[kv] = dvp[g]

    qs, cur, prev, sk = _att_specs(lambda i: nb - 1 - i)
    return pl.pallas_call(
        body, grid=(nb, N_KV_B // KV_STEP),
        in_specs=[qs, qs, cur, prev, cur, prev, sk],
        out_specs=[qs, cur, cur, sk],
        out_shape=[jax.ShapeDtypeStruct((t, N_PAIRS * LANES), BF16), jax.ShapeDtypeStruct((t, N_KV_B * LANES), BF16),
                   jax.ShapeDtypeStruct((t, N_KV_B * LANES), BF16), jax.ShapeDtypeStruct((1, LANES), F32)],
        scratch_shapes=[pltpu.VMEM((N_KV_B, WINDOW, LANES), F32), pltpu.VMEM((N_KV_B, WINDOW, LANES), F32)],
        name=name, compiler_params=_params(("arbitrary", "arbitrary")),
    )(do, qsrc, kd, kd, vd, vd, sinks)


def loss_head(h, tgt, w, *, name):
    t, d = h.shape
    tm = min(256, t)

    def body(h_ref, t_ref, w_ref, dh_ref, dw_ref, l_ref):
        tg = t_ref[...]

        def f(hv, wv):
            err = _f_norm(hv, wv) - tg
            return 0.5 * jnp.sum(jnp.sum(err * err, axis=1, keepdims=True), axis=0, keepdims=True) * (1.0 / d)

        lv, vjp = jax.vjp(f, h_ref[...], w_ref[...])
        dh, dw = vjp(jnp.ones((1, 1), F32))
        dh_ref[...] = dh
        first = pl.program_id(0) == 0

        @pl.when(first)
        def _():
            dw_ref[...] = dw
            l_ref[...] = lv * jnp.ones((1, LANES), F32)

        @pl.when(jnp.logical_not(first))
        def _():
            dw_ref[...] += dw
            l_ref[...] += lv * jnp.ones((1, LANES), F32)

    rows = pl.BlockSpec((tm, d), lambda i: (i, 0))
    one = lambda c: pl.BlockSpec((1, c), lambda i: (0, 0))
    return pl.pallas_call(
        body, grid=(t // tm,), in_specs=[rows, rows, one(d)], out_specs=[rows, one(d), one(LANES)],
        out_shape=[jax.ShapeDtypeStruct((t, d), F32), jax.ShapeDtypeStruct((1, d), F32), jax.ShapeDtypeStruct((1, LANES), F32)],
        name=name, compiler_params=_params(("arbitrary",)),
    )(h, tgt, w)


def _row_tile(r, cap=256):
    tr = r
    if r % SUBLANES == 0:
        for cand in range(SUBLANES, min(r, cap) + 1, SUBLANES):
            if r % cand == 0:
                tr = cand
    return tr


def _adamw_update(wv, gv, mv, vv):
    mn = ADAM_B1 * mv + (1.0 - ADAM_B1) * gv
    vn = ADAM_B2 * vv + (1.0 - ADAM_B2) * jnp.square(gv)
    m_hat = mn / (1.0 - ADAM_B1 ** ADAM_STEP)
    v_hat = vn / (1.0 - ADAM_B2 ** ADAM_STEP)
    return -ADAM_LR * (m_hat / (jnp.sqrt(v_hat) + ADAM_EPS) + ADAM_WD * wv), mn, vn


def adamw_layers(w, halves, m, v, *, name):
    nl, r, c = w.shape
    tr = _row_tile(r // 2)
    nbh = r // 2 // tr

    def body(w_ref, *rest):
        g_refs, m_ref, v_ref = rest[:2 * nl], rest[2 * nl], rest[2 * nl + 1]
        d_ref, mo_ref, vo_ref, go_ref = rest[2 * nl + 2:]
        layer, i = pl.program_id(0), pl.program_id(1)
        mine = (i < nbh) == (lax.axis_index("c") == 0)
        gv = jnp.where(mine, g_refs[0][...], g_refs[1][...])
        for k in range(1, nl):
            gv = jnp.where(layer == k, jnp.where(mine, g_refs[2 * k][...], g_refs[2 * k + 1][...]), gv)
        d_ref[...], mo_ref[...], vo_ref[...] = _adamw_update(w_ref[...], gv, m_ref[...], v_ref[...])
        go_ref[...] = gv

    spec3 = pl.BlockSpec((None, tr, c), lambda k, i: (k, i, 0))
    g_specs = [pl.BlockSpec((tr, c), lambda k, i, q=q: (jnp.where(k == q, i % nbh, 0), 0)) for q in range(nl) for _ in range(2)]
    return pl.pallas_call(
        body, grid=(nl, r // tr), in_specs=[spec3] + g_specs + [spec3, spec3], out_specs=[spec3] * 4,
        out_shape=[jax.ShapeDtypeStruct((nl, r, c), F32)] * 4, name=name, compiler_params=_params(("arbitrary", "arbitrary")),
    )(w, *[h for pair in halves for h in pair], m, v)


def adamw(w, g, m, v, *, name):
    r, c = w.shape
    tr = _row_tile(r)

    def body(w_ref, g_ref, m_ref, v_ref, d_ref, mo_ref, vo_ref):
        d_ref[...], mo_ref[...], vo_ref[...] = _adamw_update(w_ref[...], g_ref[...], m_ref[...], v_ref[...])

    spec = pl.BlockSpec((tr, c), lambda i: (i, 0))
    return pl.pallas_call(
        body, grid=(r // tr,), in_specs=[spec] * 4, out_specs=[spec] * 3,
        out_shape=[jax.ShapeDtypeStruct((r, c), F32)] * 3, name=name, compiler_params=_params(("parallel",)),
    )(w, g, m, v)


def _place():
    return lax.axis_index("x"), lax.axis_index("y"), lax.axis_index("c")


def allgather8(blk, *, name):
    def body(x_ref, out_ref, send_sems, recv_sems, local_sem):
        x, y, c = _place()
        me = 4 * x + 2 * y + c
        mine = pltpu.make_async_copy(x_ref, out_ref.at[me], local_sem)
        mine.start()
        sent = []
        for k in range(1, N_DEV):
            to = (x ^ ((k >> 2) & 1), y ^ ((k >> 1) & 1), c ^ (k & 1))
            cp = pltpu.make_async_remote_copy(src_ref=x_ref, dst_ref=out_ref.at[me], send_sem=send_sems.at[k - 1],
                                              recv_sem=recv_sems.at[k - 1], device_id=to, device_id_type=MESH)
            cp.start()
            sent.append(cp)
        for k in range(1, N_DEV):
            frm = me ^ k
            pltpu.make_async_remote_copy(src_ref=x_ref, dst_ref=out_ref.at[frm], send_sem=send_sems.at[k - 1],
                                         recv_sem=recv_sems.at[k - 1], device_id=(x, y, c), device_id_type=MESH).wait_recv()
        for cp in sent:
            cp.wait_send()
        mine.wait()

    vm = pl.BlockSpec(memory_space=pltpu.VMEM)
    return pl.pallas_call(
        body, in_specs=[vm], out_specs=vm, out_shape=jax.ShapeDtypeStruct((N_DEV,) + blk.shape, blk.dtype), name=name,
        scratch_shapes=[pltpu.SemaphoreType.DMA((N_DEV - 1,)), pltpu.SemaphoreType.DMA((N_DEV - 1,)), pltpu.SemaphoreType.DMA],
    )(blk)


def _other_chips(x, y):
    return [(1 - x, y), (x, 1 - y), (1 - x, 1 - y)]


def _hbm_call(body, ins, out_shapes, n_sems, name):
    hbm = pl.BlockSpec(memory_space=pl.ANY)
    return pl.pallas_call(
        body, in_specs=[hbm] * len(ins), out_specs=[hbm] * len(out_shapes), out_shape=out_shapes, name=name,
        scratch_shapes=[pltpu.SemaphoreType.DMA((n_sems,)), pltpu.SemaphoreType.DMA((n_sems,))],
    )(*ins)


def _half_rows(c, rh):
    return pl.ds(pl.multiple_of(c * rh, BF16_ROWS), rh)


def gather_units(units, *, name):
    nu = len(units)
    shapes = []
    for arr, layer_major in units:
        r, cols = arr.shape
        shapes.append(jax.ShapeDtypeStruct((2, N_CHIPS, r // 2, cols) if layer_major else (N_CHIPS, r, cols), arr.dtype))

    def body(*refs):
        in_refs, out_refs, send_sems, recv_sems = refs[:nu], refs[nu:2 * nu], refs[2 * nu], refs[2 * nu + 1]
        x, y, c = _place()
        me_chip = 2 * x + y
        sib = (x, y, 1 - c)
        chips = _other_chips(x, y)

        def copy(k, src, dst, to):
            return pltpu.make_async_remote_copy(src_ref=src, dst_ref=dst, send_sem=send_sems.at[k], recv_sem=recv_sems.at[k],
                                                device_id=to, device_id_type=MESH)

        first, passed, landing = [], [], []
        for u, (arr, layer_major) in enumerate(units):
            rh = arr.shape[0] // 2
            out_ref = out_refs[u]
            slot = (lambda chip, half, o=out_ref: o.at[half, chip]) if layer_major else \
                   (lambda chip, half, o=out_ref, rh=rh: o.at[chip, _half_rows(half, rh), :])
            my_half = in_refs[u].at[_half_rows(c, rh), :]
            for j, (cx, cy) in enumerate(chips):
                k = 6 * u + j
                first.append(copy(k, my_half, slot(me_chip, c), (cx, cy, c)))
                passed.append(copy(k + 3, slot(2 * cx + cy, c), slot(2 * cx + cy, c), sib))
                landing.append((copy(k, my_half, slot(2 * cx + cy, c), sib), copy(k + 3, my_half, slot(2 * cx + cy, 1 - c), sib)))
        for cp in first:
            cp.start()
        for (over_ici, _), fwd in zip(landing, passed):
            over_ici.wait_recv()
            fwd.start()
        for _, from_sibling in landing:
            from_sibling.wait_recv()
        for cp in first + passed:
            cp.wait_send()

    return _hbm_call(body, [a for a, _ in units], shapes, 6 * nu, name)


HBM_SPEC = pl.BlockSpec(memory_space=pltpu.HBM)
SEM_SPEC = pl.BlockSpec(memory_space=pltpu.SEMAPHORE)
ORDERED_EFFECT = pltpu.SideEffectType.DATAFLOW_SIDE_EFFECTING


def _split_start(body, srcs, land_shapes, after, *, name):
    nu = len(srcs)
    lands = [lax.empty(s.shape, s.dtype) for s in land_shapes]

    def whole(*refs):
        body(refs[:nu], refs[nu:2 * nu], refs[2 * nu + 1], refs[2 * nu + 2])
        refs[-1][...] = jnp.zeros((SUBLANES, LANES), F32)

    hbm = lambda a: pltpu.with_memory_space_constraint(a, pltpu.HBM)
    sems = pltpu.SemaphoreType.DMA((nu,))
    res = pl.pallas_call(
        whole, name=name, in_specs=[HBM_SPEC] * (2 * nu) + [pl.BlockSpec(memory_space=pl.ANY)],
        out_shape=[sems, sems] + [pltpu.HBM(a.shape, a.dtype) for a in srcs] + [pltpu.HBM(s.shape, s.dtype) for s in land_shapes]
        + [jax.ShapeDtypeStruct((SUBLANES, LANES), F32)],
        out_specs=[SEM_SPEC, SEM_SPEC] + [HBM_SPEC] * (2 * nu) + [pl.BlockSpec(memory_space=pltpu.VMEM)],
        input_output_aliases={q: 2 + q for q in range(2 * nu)},
        compiler_params=pltpu.CompilerParams(has_side_effects=ORDERED_EFFECT),
    )(*[hbm(a) for a in srcs], *[hbm(a) for a in lands], after)
    return res[0], res[1], res[2:2 + nu], res[2 + nu:2 + 2 * nu], res[-1]


def _split_wait(pending, moved, after, *, name):
    send_sems, recv_sems, srcs, lands, _ = pending
    nu = len(srcs)

    def body(*refs):
        land_refs, ssem, rsem = refs[nu:2 * nu], refs[2 * nu], refs[2 * nu + 1]
        x, y, c = _place()
        for u in range(nu):
            size = moved(land_refs[u])
            cp = pltpu.make_async_remote_copy(src_ref=size, dst_ref=size, send_sem=ssem.at[u], recv_sem=rsem.at[u],
                                              device_id=(x, y, c), device_id_type=MESH)
            cp.wait_send()
            cp.wait_recv()

    res = pl.pallas_call(
        body, name=name, in_specs=[HBM_SPEC] * (2 * nu) + [SEM_SPEC, SEM_SPEC, pl.BlockSpec(memory_space=pl.ANY)],
        out_shape=[pltpu.HBM(a.shape, a.dtype) for a in srcs] + [pltpu.HBM(a.shape, a.dtype) for a in lands],
        out_specs=[HBM_SPEC] * (2 * nu), input_output_aliases={q: q for q in range(2 * nu)},
        compiler_params=pltpu.CompilerParams(has_side_effects=ORDERED_EFFECT),
    )(*srcs, *lands, send_sems, recv_sems, after)
    return res[nu:]


def gather_start(shards, after, *, name):
    def body(src_refs, land_refs, send_sems, recv_sems):
        x, y, c = _place()
        for u, shard in enumerate(shards):
            rows = _half_rows(c, shard.shape[0] // 2)
            for cx, cy in _other_chips(x, y):
                for core in range(2):
                    pltpu.make_async_remote_copy(src_ref=src_refs[u].at[rows, :], dst_ref=land_refs[u].at[2 * x + y, rows, :],
                                                 send_sem=send_sems.at[u], recv_sem=recv_sems.at[u], device_id=(cx, cy, core),
                                                 device_id_type=MESH).start()

    return _split_start(body, shards, [jax.ShapeDtypeStruct((N_CHIPS,) + s.shape, s.dtype) for s in shards], after, name=name)


def gather_wait(pending, after, *, name):
    return _split_wait(pending, lambda land: land.at[pl.ds(0, N_CHIPS - 1)], after, name=name)


def scatter_start(pairs, *, name):
    def body(src_refs, land_refs, send_sems, recv_sems):
        x, y, c = _place()
        for u in range(len(pairs)):
            for j, (cx, cy) in enumerate(_other_chips(x, y)):
                pltpu.make_async_remote_copy(src_ref=src_refs[u].at[2 * cx + cy], dst_ref=land_refs[u].at[j], send_sem=send_sems.at[u],
                                             recv_sem=recv_sems.at[u], device_id=(cx, cy, c), device_id_type=MESH).start()

    return _split_start(body, pairs, [jax.ShapeDtypeStruct((N_CHIPS - 1,) + p.shape[1:], p.dtype) for p in pairs], pairs[0], name=name)


def scatter_wait(pending, after, *, name):
    return _split_wait(pending, lambda land: land, after, name=name)


def swap_units(units, *, name):
    nu = len(units)

    def body(*refs):
        g_refs, out_refs, send_sems, recv_sems = refs[:nu], refs[nu:2 * nu], refs[2 * nu], refs[2 * nu + 1]
        x, y, c = _place()
        cps = [pltpu.make_async_remote_copy(src_ref=g_refs[u].at[:, _half_rows(1 - c, units[u].shape[1] // 2), :], dst_ref=out_refs[u],
                                            send_sem=send_sems.at[u], recv_sem=recv_sems.at[u], device_id=(x, y, 1 - c),
                                            device_id_type=MESH) for u in range(nu)]
        for cp in cps:
            cp.start()
        for cp in cps:
            cp.wait()

    shapes = [jax.ShapeDtypeStruct((N_CHIPS, g.shape[1] // 2, g.shape[2]), g.dtype) for g in units]
    return _hbm_call(body, units, shapes, nu, name)


def join_units(units, *, name):
    nu = len(units)

    def body(*refs):
        h_refs, out_refs, send_sems, recv_sems = refs[:nu], refs[nu:2 * nu], refs[2 * nu], refs[2 * nu + 1]
        x, y, c = _place()
        cps = [pltpu.make_async_remote_copy(src_ref=h_refs[u], dst_ref=out_refs[u], send_sem=send_sems.at[u], recv_sem=recv_sems.at[u],
                                            device_id=(x, y, 1 - c), device_id_type=MESH) for u in range(nu)]
        for cp in cps:
            cp.start()
        for cp in cps:
            cp.wait()

    return _hbm_call(body, units, [jax.ShapeDtypeStruct(h.shape, h.dtype) for h in units], nu, name)


def _half_tile(rh):
    tr = rh
    for cand in range(BF16_ROWS, min(rh, 512) + 1, BF16_ROWS):
        if rh % cand == 0:
            tr = cand
    return tr


def pair_add(g, sib, *, name):
    nc, rh, cols = sib.shape
    tr = _half_tile(rh)
    nbh = rh // tr

    def body(g0_ref, g1_ref, s_ref, o_ref):
        mine = jnp.where(lax.axis_index("c") == 0, g0_ref[...], g1_ref[...])
        o_ref[...] = (mine.astype(F32) + s_ref[...].astype(F32)).astype(o_ref.dtype)

    blk = lambda off: pl.BlockSpec((None, tr, cols), lambda j, i: (j, off + i, 0))
    return pl.pallas_call(
        body, grid=(nc, nbh), in_specs=[blk(0), blk(nbh), blk(0)], out_specs=blk(0),
        out_shape=jax.ShapeDtypeStruct(sib.shape, BF16), name=name, compiler_params=_params(("parallel", "parallel")),
    )(g, g, sib)


def chips_add(pair, landed, *, name):
    nc, rh, cols = pair.shape
    tr = _half_tile(rh)

    def body(*refs):
        chip = 2 * lax.axis_index("x") + lax.axis_index("y")
        acc = refs[0][...]
        for j in range(1, nc):
            acc = jnp.where(chip == j, refs[j][...], acc)
        acc = acc.astype(F32)
        for r in refs[nc:-1]:
            acc = acc + r[...].astype(F32)
        refs[-1][...] = acc

    part = lambda q: pl.BlockSpec((None, tr, cols), lambda i, q=q: (q, i, 0))
    return pl.pallas_call(
        body, grid=(rh // tr,), in_specs=[part(q) for q in range(nc)] + [part(q) for q in range(landed.shape[0])],
        out_specs=pl.BlockSpec((tr, cols), lambda i: (i, 0)),
        out_shape=jax.ShapeDtypeStruct((rh, cols), F32), name=name, compiler_params=_params(("parallel",)),
    )(*[pair] * nc, *[landed] * landed.shape[0])


def sum8(g, *, name):
    def body(g_ref, o_ref):
        acc = g_ref[0]
        for d in range(1, N_DEV):
            acc = acc + g_ref[d]
        o_ref[...] = acc

    return pl.pallas_call(body, out_shape=jax.ShapeDtypeStruct(g.shape[1:], F32), name=name)(g)


def _dup_halves(a):
    t = a.shape[0]
    a = a.reshape(t, N_KV_B, HEAD_DIM_B)
    return jnp.concatenate([a, a], axis=-1).reshape(t, N_KV_B * LANES)


def _undup(a):
    t = a.shape[0]
    return a.reshape(t, N_KV_B, LANES)[:, :, :HEAD_DIM_B].reshape(t, N_KV_B * HEAD_DIM_B)


def _lane_pad(v, offset=0):
    return jnp.zeros((1, LANES), F32).at[0, offset:offset + v.shape[0]].set(v)


SHARD_UP = 2 * D_FF // N_CHIPS
SHARD_BIN = (N_HEADS_B + 2 * N_KV_B) * HEAD_DIM_B // N_CHIPS
SHARD_PROJ = D_MODEL // N_CHIPS


def local_step(x, p, tgt, sm, weight, on_grads):
    t = x.shape[0]
    rtm = min(256, t)
    hk = N_HEADS_A * HEAD_DIM_A
    qd_b = N_HEADS_B * HEAD_DIM_B
    kd_b = N_KV_B * HEAD_DIM_B
    gs = {}
    norm = lambda h, w, nm: tile_map(_f_norm, [(h, D_MODEL, 0)], [w], [(D_MODEL, BF16)], tm=rtm, ncol=1, name=nm)[0]

    spec = pl.BlockSpec
    mtm = _tile(D_MODEL, MM_TM_CAP)
    p_bf = p.astype(BF16)
    alog_p = _lane_pad(sm["a_log"][0], N_HEADS_A)
    dtb_p = _lane_pad(sm["a_dt_bias"][0], N_HEADS_A)
    sinks_p = _lane_pad(sm["b_sinks"][0])
    nw = lambda name, i: sm[name][i:i + 1]
    by_chip = lambda kdim, ns: dict(tn=ns, tk=kdim, b_spec=spec((None, kdim, ns), lambda r, j, kk: (j, kk, 0)))
    by_chip_t = lambda ndim, ns: dict(n=ndim, tn=ndim, tk=ns, b_spec=spec((None, ndim, ns), lambda r, j, kk: (kk, j, 0)))
    cache = {}

    def wgt(name, i, after):
        if (name, i) not in cache:
            cache[name, i] = weight(name, i, after)
        return cache[name, i]

    saved = []
    h = x
    hn_next = norm(h, nw("norm_mix", 0), "norm_mix0")
    for i in range(DEPTH):
        s = {"h0": h, "hn": hn_next}
        if i % 2 == 0:
            s["pm"] = mm(s["hn"], wgt("a_w_in", i, h), name="a_in")
            tail = (s["pm"], LANES, 4 * hk // LANES)
            s["c"] = conv_fwd(s["pm"], wgt("a_conv", i, h), name="a_conv")
            s["bg"] = tile_map(_f_betag, [tail], [alog_p, dtb_p], [(LANES, F32)], tm=rtm, ncol=1, name="a_betag")[0]
            s["prep"], s["tms"] = delta_prep(s["c"], s["bg"], name="a_prep")
            s["o"], s["s_in"] = delta_scan(*s["prep"], name="a_scan")
            s["on"] = gnorm_fwd(s["o"], s["pm"], sm["a_norm"], name="a_gnorm")
            h, s["hf"] = mm(s["on"], wgt("a_w_out", i, s["on"]), add=h, norm_w=nw("norm_ffn", i), name="a_out")
        else:
            s["pb"] = mm(s["hn"], wgt("b_w_in", i, s["hn"]), name="b_in", out_dtype=BF16, n=N_CHIPS * SHARD_BIN,
                         **by_chip(D_MODEL, SHARD_BIN))
            s["kd"], s["vd"] = _dup_halves(s["pb"][:, qd_b:qd_b + kd_b]), _dup_halves(s["pb"][:, qd_b + kd_b:])
            s["ao"] = swa_fwd(s["pb"], s["kd"], s["vd"], sinks_p, name="b_att")
            h, s["hf"] = mm(s["ao"], wgt("b_w_out", i, s["ao"]), add=h, norm_w=nw("norm_ffn", i), name="b_out")
        s["h1"] = h
        s["u"] = mm(s["hf"], wgt("f_w_up", i, s["hf"]), name=f"f_up{i}", out_dtype=BF16, n=2 * D_FF, **by_chip(D_MODEL, SHARD_UP))
        s["act"] = conv_act_fwd(s["u"], wgt("f_conv", i, s["hf"]), name=f"f_conv_act{i}")
        h, s["hp"] = mm(s["act"], wgt("f_w_down", i, s["act"]), add=h, norm_w=nw("norm_ple", i), name=f"f_down{i}")
        s["h2"] = h
        s["gl"] = mm(s["hp"], wgt("ple_w_gate", i, s["hp"]), name=f"ple_gate{i}")
        s["pe"] = mm(p_bf[i], wgt("ple_w_proj", i, s["hp"]), name=f"ple_proj{i}", n=D_MODEL, **by_chip(PLE_DIM, SHARD_PROJ))
        rows3 = [(h, D_MODEL, 0), (s["gl"], D_MODEL, 0), (s["pe"], D_MODEL, 0)]
        if i + 1 < DEPTH:
            def mix_norm(hv, g, e, wn):
                hn = hv + _f_ple(g, e)
                return hn, _f_norm(hn, wn)
            h, hn_next = tile_map(mix_norm, rows3, [nw("norm_mix", i + 1)], [(D_MODEL, F32), (D_MODEL, BF16)], tm=rtm, ncol=1,
                                  name=f"ple_mix{i}")
        else:
            h = tile_map(lambda hv, g, e: hv + _f_ple(g, e), rows3, [], [(D_MODEL, F32)], tm=rtm, ncol=1, name=f"ple_mix{i}")[0]
        saved.append(s)

    dh, gnf, loss = loss_head(h, tgt, sm["norm_final"][None, :], name="loss_head")
    gs["norm_final"] = gnf[0]

    g_mix, g_ffn, g_ple, g_conv = ([None] * DEPTH for _ in range(4))
    zero = jnp.zeros((1, 1), F32)
    for i in reversed(range(DEPTH)):
        s, gw = saved[i], {}
        by_rows = lambda g: g.reshape(N_CHIPS, g.shape[0] // N_CHIPS, g.shape[1])
        (dgl, dpe), _ = tile_vjp(_f_ple, [(s["gl"], D_MODEL, 0), (s["pe"], D_MODEL, 0)], [], [(dh, D_MODEL, 0)], n_diff=2,
                                 tm=rtm, ncol=1, name=f"ple_mix_bwd{i}", grad_dtypes=[BF16, BF16])
        gw["ple_w_proj"] = mm(p_bf[i], dpe, ta=True, name=f"ple_proj_dw{i}", out_dtype=BF16, tn=SHARD_PROJ,
                              o_shape=(N_CHIPS, PLE_DIM, SHARD_PROJ), o_spec=spec((None, PLE_DIM, SHARD_PROJ), lambda r, j, kk: (j, r, 0)))
        gw["ple_w_gate"] = by_rows(mm(s["hp"], dgl, ta=True, name=f"ple_gate_dw{i}", out_dtype=BF16))
        fused = dict(tb=True, tm_cap=MM_TM_CAP // 2)
        dh, g_ple[i] = mm(dgl, cache["ple_w_gate", i], name=f"ple_gate_dx{i}", norm_grad=(s["h2"], nw("norm_ple", i) + zero, dh), **fused)

        dact = mm(dh, cache["f_w_down", i], tb=True, name=f"f_down_dx{i}")
        gw["f_w_down"] = by_rows(mm(s["act"], dh, ta=True, name=f"f_down_dw{i}", out_dtype=BF16, tm_cap=D_FF // 2))
        du_halves = conv_act_bwd(s["u"], dact, cache["f_conv", i], name=f"f_conv_act_bwd{i}")
        g_conv[i] = jnp.concatenate(du_halves[2:], axis=1)
        dhf = g_up = None
        for half, du in enumerate(du_halves[:2]):
            c0 = half * (N_CHIPS // 2)
            g_up = mm(s["hf"], du, ta=True, name=f"f_up_dw{i}_{half}", out_dtype=BF16, tn=SHARD_UP, into=g_up,
                      o_shape=(N_CHIPS, D_MODEL, SHARD_UP), o_spec=spec((None, mtm, SHARD_UP), lambda r, j, kk, c0=c0: (c0 + j, r, 0)))
            last = dict(norm_grad=(s["h1"], nw("norm_ffn", i), dh), **fused) if half else dict(tb=True)
            dhf = mm(du, cache["f_w_up", i], name=f"f_up_dx{i}_{half}", n=D_MODEL, tn=D_MODEL, tk=SHARD_UP, add=dhf,
                     b_spec=spec((None, D_MODEL, SHARD_UP), lambda r, j, kk, c0=c0: (c0 + kk, j, 0)), **last)
        gw["f_w_up"] = g_up
        dh, g_ffn[i] = dhf
        token, gw = on_grads(i, "ffn", gw), {}
        w_out = cache["a_w_out" if i % 2 == 0 else "b_w_out", i]
        if token is not None:
            w_out = w_out + token[:1, :1].astype(BF16)

        if i % 2 == 0:
            don = mm(dh, w_out, tb=True, name="a_out_dx")
            gw["a_w_out"] = by_rows(mm(s["on"], dh, ta=True, name="a_out_dw", out_dtype=BF16))
            do, dz, gs["a_norm"] = gnorm_bwd(s["o"], s["pm"], sm["a_norm"], don, name="a_gnorm_bwd")
            dprep = delta_scan_bwd(do, *s["prep"], s["s_in"], name="a_scan_bwd")
            dc, dbg = delta_prep_bwd(s["c"], s["bg"], s["tms"], dprep, name="a_prep_bwd")
            (dpt,), (galog, gdtb) = tile_vjp(_f_betag, [(s["pm"], LANES, 4 * hk // LANES)], [alog_p, dtb_p], [(dbg, LANES, 0)], n_diff=1,
                                             tm=rtm, ncol=1, name="a_betag_bwd", grad_dtypes=[BF16])
            gs["a_log"] = galog[:, N_HEADS_A:2 * N_HEADS_A]
            gs["a_dt_bias"] = gdtb[:, N_HEADS_A:2 * N_HEADS_A]
            dqkv, gs["a_conv"] = conv_bwd(dc, s["pm"], cache["a_conv", i], name="a_conv_bwd")
            dpm = jnp.concatenate([dqkv, dz, dpt], axis=1)
            g_in = mm(s["hn"], dpm, ta=True, name="a_in_dw", out_dtype=BF16)[:, :4 * hk + 2 * N_HEADS_A]
            gw["a_w_in"] = g_in.reshape(D_MODEL, N_CHIPS, g_in.shape[1] // N_CHIPS).transpose(1, 0, 2)
            dh, g_mix[i] = mm(dpm, cache["a_w_in", i], name="a_in_dx", norm_grad=(s["h0"], nw("norm_mix", i), dh), **fused)
        else:
            dao = mm(dh, w_out, tb=True, name="b_out_dx")
            gw["b_w_out"] = by_rows(mm(s["ao"], dh, ta=True, name="b_out_dw", out_dtype=BF16))
            dq, dkd, dvd, gsk = swa_bwd(dao, s["pb"], s["kd"], s["vd"], sinks_p, name="b_att_bwd")
            gs["b_sinks"] = gsk[:, :N_HEADS_B]
            dpb = jnp.concatenate([dq, _undup(dkd), _undup(dvd)], axis=1)
            gw["b_w_in"] = mm(s["hn"], dpb, ta=True, name="b_in_dw", out_dtype=BF16, tn=SHARD_BIN,
                              o_shape=(N_CHIPS, D_MODEL, SHARD_BIN), o_spec=spec((None, mtm, SHARD_BIN), lambda r, j, kk: (j, r, 0)))
            dh, g_mix[i] = mm(dpb, cache["b_w_in", i], name="b_in_dx", norm_grad=(s["h0"], nw("norm_mix", i), dh), **fused,
                              **by_chip_t(D_MODEL, SHARD_BIN))
        token = on_grads(i, "mix", gw)
        if token is not None:
            zero = token[:1, :1]

    gs["norm_mix"], gs["norm_ffn"], gs["norm_ple"] = (jnp.concatenate(g, axis=0) for g in (g_mix, g_ffn, g_ple))
    gs["f_conv"] = jnp.stack(g_conv)
    return loss, dh, gs


BIG = ["a_w_in", "a_w_out", "b_w_in", "b_w_out", "f_w_up", "f_w_down", "ple_w_proj", "ple_w_gate"]
LAYERED = {"f_w_up", "f_w_down", "ple_w_proj", "ple_w_gate"}
BY_CHIP = {"b_w_in", "f_w_up", "ple_w_proj"}
LAYER_UNITS = [[("a_w_in", 0), ("a_w_out", 0)] + [(n, 0) for n in sorted(LAYERED)],
               [("b_w_in", 1), ("b_w_out", 1)] + [(n, 1) for n in sorted(LAYERED)]]
CONVS = ["a_conv", "f_conv"]
SMALL = ["norm_mix", "norm_ffn", "norm_ple", "norm_final", "a_log", "a_dt_bias", "a_norm", "b_sinks"]
SMALL_ROWS = 8
CONV_ROWS = 16
CONV_GRAD_ROWS = 48


def _pack_rows(arrs, rows, dtype):
    flat = jnp.concatenate([a.reshape(-1).astype(dtype) for a in arrs])
    return jnp.pad(flat, (0, rows * PACK_COLS - flat.shape[0])).reshape(rows, PACK_COLS)


def _unpack(flat, shapes):
    out, off = [], 0
    for shp in shapes:
        n = math.prod(shp)
        out.append(flat[off:off + n].reshape(shp))
        off += n
    return out


def _pack_small(d, loss=None):
    tail = jnp.concatenate([d["a_log"].reshape(-1), d["a_dt_bias"].reshape(-1), d["a_norm"].reshape(-1), d["b_sinks"].reshape(-1)])
    if loss is not None:
        tail = jnp.concatenate([tail, loss.reshape(-1)[:1]])
    tail = jnp.pad(tail, (0, PACK_COLS - tail.shape[0]))
    return jnp.concatenate([d["norm_mix"], d["norm_ffn"], d["norm_ple"], d["norm_final"][None, :], tail[None, :]], axis=0)


def _unpack_small(a, like):
    out = {"norm_mix": a[0:2], "norm_ffn": a[2:4], "norm_ple": a[4:6], "norm_final": a[6]}
    off = 0
    for nm in ("a_log", "a_dt_bias", "a_norm", "b_sinks"):
        n = like[nm].size
        out[nm] = a[7, off:off + n].reshape(like[nm].shape)
        off += n
    return out, a[7, off]


def _as2d(a):
    return a.reshape(-1, a.shape[-1])


def kernel(x, p, norm_mix, norm_ffn, norm_ple, norm_final, a_w_in, a_conv, a_log, a_dt_bias, a_norm, a_w_out, b_w_in, b_sinks, b_w_out, f_w_up, f_conv, f_w_down, ple_w_proj, ple_w_gate, loss_target, m_norm_mix, m_norm_ffn, m_norm_ple, m_norm_final, m_a_w_in, m_a_conv, m_a_log, m_a_dt_bias, m_a_norm, m_a_w_out, m_b_w_in, m_b_sinks, m_b_w_out, m_f_w_up, m_f_conv, m_f_w_down, m_ple_w_proj, m_ple_w_gate, v_norm_mix, v_norm_ffn, v_norm_ple, v_norm_final, v_a_w_in, v_a_conv, v_a_log, v_a_dt_bias, v_a_norm, v_a_w_out, v_b_w_in, v_b_sinks, v_b_w_out, v_f_w_up, v_f_conv, v_f_w_down, v_ple_w_proj, v_ple_w_gate):
    w = dict(norm_mix=norm_mix, norm_ffn=norm_ffn, norm_ple=norm_ple, norm_final=norm_final, a_w_in=a_w_in, a_conv=a_conv,
             a_log=a_log, a_dt_bias=a_dt_bias, a_norm=a_norm, a_w_out=a_w_out, b_w_in=b_w_in, b_sinks=b_sinks, b_w_out=b_w_out,
             f_w_up=f_w_up, f_conv=f_conv, f_w_down=f_w_down, ple_w_proj=ple_w_proj, ple_w_gate=ple_w_gate)
    m = dict(norm_mix=m_norm_mix, norm_ffn=m_norm_ffn, norm_ple=m_norm_ple, norm_final=m_norm_final, a_w_in=m_a_w_in,
             a_conv=m_a_conv, a_log=m_a_log, a_dt_bias=m_a_dt_bias, a_norm=m_a_norm, a_w_out=m_a_w_out, b_w_in=m_b_w_in,
             b_sinks=m_b_sinks, b_w_out=m_b_w_out, f_w_up=m_f_w_up, f_conv=m_f_conv, f_w_down=m_f_w_down,
             ple_w_proj=m_ple_w_proj, ple_w_gate=m_ple_w_gate)
    v = dict(norm_mix=v_norm_mix, norm_ffn=v_norm_ffn, norm_ple=v_norm_ple, norm_final=v_norm_final, a_w_in=v_a_w_in,
             a_conv=v_a_conv, a_log=v_a_log, a_dt_bias=v_a_dt_bias, a_norm=v_a_norm, a_w_out=v_a_w_out, b_w_in=v_b_w_in,
             b_sinks=v_b_sinks, b_w_out=v_b_w_out, f_w_up=v_f_w_up, f_conv=v_f_conv, f_w_down=v_f_w_down,
             ple_w_proj=v_ple_w_proj, ple_w_gate=v_ple_w_gate)
    xc, yc, cc = _place()
    my_chip = 2 * xc + yc

    shard = {(n, i): w[n][i if n in LAYERED else 0].astype(BF16) for n, i in LAYER_UNITS[0] + LAYER_UNITS[1]}
    first = shard["a_w_in", 0]
    (ga,) = gather_units([(first, False)], name="gather_first")
    ga = lax.dynamic_update_index_in_dim(ga, first, my_chip, 0)
    a_in = jnp.concatenate([ga[j] for j in range(N_CHIPS)], axis=1)
    n_main = 4 * N_HEADS_A * HEAD_DIM_A
    conv_shapes = [w[n].shape for n in CONVS]
    convs = allgather8(_pack_rows([w[n] for n in CONVS], CONV_ROWS, F32), name="gather_convs")
    conv_parts = [_unpack(convs[2 * j].reshape(-1), conv_shapes) for j in range(N_CHIPS)]
    a_conv_full, f_conv_full = (jnp.concatenate([conv_parts[j][q] for j in range(N_CHIPS)], axis=2) for q in range(2))
    ready = {("a_w_in", 0): jnp.pad(a_in, ((0, 0), (0, n_main + LANES - a_in.shape[1]))), ("a_conv", 0): a_conv_full[0], ("f_conv", 0): f_conv_full[0], ("f_conv", 1): f_conv_full[1]}
    later = [[k for k in units if k != ("a_w_in", 0)] for units in LAYER_UNITS]
    pending, after = [], ga
    for layer, keys in enumerate(later):
        pending.append(gather_start([shard[k] for k in keys], after, name=f"gather_start{layer}"))
        after = pending[-1][4]
    sm = {n: w[n] for n in SMALL}
    sm["norm_mix"] = sm["norm_mix"] + after[:1, :1]

    def weight(name, layer, act):
        if (name, layer) not in ready:
            landed = gather_wait(pending[layer], act, name=f"gather_wait{layer}")
            for k, g in zip(later[layer], landed):
                g = lax.dynamic_update_index_in_dim(g, shard[k], my_chip, 0)
                ready[k] = g if k[0] in BY_CHIP else g.reshape(N_CHIPS * g.shape[1], g.shape[2])
        return ready[name, layer]

    pairs, scattered, started = {}, {}, []

    def on_grads(layer, part, gw):
        keys = [k for k in LAYER_UNITS[layer] if (k[0] in LAYERED) == (part == "ffn")]
        from_sib = swap_units([gw[n] for n, _ in keys], name=f"rs_swap_{part}{layer}")
        for (n, _), sib in zip(keys, from_sib):
            pairs[n, layer] = pair_add(gw[n], sib, name=f"rs_add_pair_{n}{layer}")
        started.append((keys, scatter_start([pairs[k] for k in keys], name=f"rs_scatter_start_{part}{layer}"), f"{part}{layer}"))
        return started[-1][1][4]

    loss, grad_x, gs = local_step(x[0], p[:, 0], loss_target[0], sm, weight, on_grads)

    grads, delta, new_m, new_v, g_unit = {}, {}, {}, {}, {}

    def finish(keys, tag):
        halves = [chips_add(pairs[k], scattered[k], name=f"rs_add_chips_{k[0]}{k[1]}") for k in keys]
        g_unit.update(zip(keys, zip(halves, join_units(halves, name=f"rs_join_{tag}"))))
        for n in BIG:
            mine = [(n, i) for i in range(DEPTH) if (n, i) in LAYER_UNITS[i]]
            if n not in delta and all(k in g_unit for k in mine):
                g_layers = [g_unit[k] for k in mine]
                shape3 = (len(g_layers), 2 * g_layers[0][0].shape[0], g_layers[0][0].shape[1])
                res = adamw_layers(w[n].reshape(shape3), g_layers, m[n].reshape(shape3), v[n].reshape(shape3), name=f"adamw_{n}")
                delta[n], new_m[n], new_v[n], grads[n] = (r.reshape(w[n].shape) for r in res)

    last_keys, last_pending, last_tag = started[-1]
    for keys, pend, tag in started[:-1]:
        scattered.update(zip(keys, scatter_wait(pend, last_pending[4], name=f"rs_scatter_wait_{tag}")))
    finish([k for keys, _, _ in started[:-1] for k in keys], "first")

    conv_grads = _pack_rows([gs[n] for n in CONVS], CONV_GRAD_ROWS, F32)
    small_sum = sum8(allgather8(jnp.concatenate([_pack_small(gs, loss), conv_grads]), name="gather_small"), name="sum_small")
    g_sm, loss_sum = _unpack_small(small_sum[:SMALL_ROWS], sm)

    scattered.update(zip(last_keys, scatter_wait(last_pending, small_sum, name=f"rs_scatter_wait_{last_tag}")))
    finish(last_keys, "last")

    for n, full in zip(CONVS, _unpack(small_sum[SMALL_ROWS:].reshape(-1), [gs[n].shape for n in CONVS])):
        g2 = _as2d(lax.dynamic_slice_in_dim(full, my_chip * w[n].shape[-1], w[n].shape[-1], axis=full.ndim - 1))
        d2, m2, v2 = adamw(_as2d(w[n]), g2, _as2d(m[n]), _as2d(v[n]), name=f"adamw_{n}")
        grads[n], delta[n], new_m[n], new_v[n] = (r.reshape(w[n].shape) for r in (g2, d2, m2, v2))
    pk = lambda d: _pack_small(d)
    d2, m2, v2 = adamw(pk(sm), pk(g_sm), pk({n: m[n] for n in SMALL}), pk({n: v[n] for n in SMALL}), name="adamw_small")
    for src, dst in ((d2, delta), (m2, new_m), (v2, new_v)):
        dst.update(_unpack_small(src, sm)[0])
    grads.update(g_sm)

    order = ["norm_mix", "norm_ffn", "norm_ple", "norm_final", "a_w_in", "a_conv", "a_log", "a_dt_bias", "a_norm", "a_w_out",
             "b_w_in", "b_sinks", "b_w_out", "f_w_up", "f_conv", "f_w_down", "ple_w_proj", "ple_w_gate"]
    return (loss_sum, grad_x[None], *[grads[n] for n in order], *[delta[n] for n in order],
            *[new_m[n] for n in order], *[new_v[n] for n in order])
```

```python
import functools
import math

import jax
import jax.numpy as jnp
from jax import lax
from jax.experimental import pallas as pl
from jax.experimental.pallas import tpu as pltpu

F32 = jnp.float32
BF16 = jnp.bfloat16
MESH = pl.DeviceIdType.MESH

D_MODEL = 1024
N_HEADS_A = 8
HEAD_DIM_A = 128
CONV_A = 4
N_HEADS_B = 16
N_KV_B = 4
HEAD_DIM_B = 64
WINDOW = 128
D_FF = 2816
FFN_CONV = 3
PLE_DIM = 256
EPS = 1e-6
DEPTH = 2

ADAM_LR = 0.001
ADAM_B1 = 0.9
ADAM_B2 = 0.999
ADAM_EPS = 1e-08
ADAM_WD = 0.01
ADAM_STEP = 10

LANES = 128
SUBLANES = 8
BF16_ROWS = 16
CHUNK = 128
VMEM_LIMIT = 56 * 1024 * 1024
NEG = -1e30
N_CHIPS = 4
N_DEV = 8
PACK_COLS = 1024


def _params(sem=None):
    return pltpu.CompilerParams(dimension_semantics=sem, vmem_limit_bytes=VMEM_LIMIT)


def _tile(dim, cap):
    if dim % LANES:
        return dim
    best = LANES
    for t in range(LANES, min(dim, cap) + 1, LANES):
        if dim % t == 0:
            best = t
    return best


def _dot(a, b, dims=(((1,), (0,)), ((), ())), precision=None):
    return lax.dot_general(a, b, dims, precision=precision, preferred_element_type=F32)


NN = (((1,), (0,)), ((), ()))
NT = (((1,), (1,)), ((), ()))
TN = (((0,), (0,)), ((), ()))


MM_TM_CAP = 1024
MM_TK_CAP_TOKENS = 2048


def mm(a, b, *, name, ta=False, tb=False, out_dtype=F32, add=None, norm_w=None, norm_grad=None, tm_cap=MM_TM_CAP, tn_cap=1408,
       tk_cap=1408, n=None, tn=None, tk=None, b_spec=None, o_spec=None, o_shape=None, into=None):
    m, k = (a.shape[1], a.shape[0]) if ta else a.shape
    if b_spec is None:
        n = b.shape[0] if tb else b.shape[1]
        assert (b.shape[1] if tb else b.shape[0]) == k, (a.shape, b.shape, ta, tb)
    tm, tn, tk = _tile(m, tm_cap), tn or _tile(n, tn_cap), tk or _tile(k, MM_TK_CAP_TOKENS if ta else tk_cap)
    assert n % tn == 0 and k % tk == 0, (n, tn, k, tk)
    nk = k // tk
    dims = (((0 if ta else 1,), (1 if tb else 0,)), ((), ()))
    has_add, has_norm, has_grad = add is not None, norm_w is not None, norm_grad is not None
    assert not (has_norm or has_grad) or (tn == n and o_spec is None), "the norm epilogues need whole rows"
    n_in = 2 + has_add + has_norm + 3 * has_grad + (into is not None)

    def body(*refs):
        a_ref, b_ref = refs[0], refs[1]
        add_ref = refs[2] if has_add else None
        o_ref = refs[n_in]
        part = _dot(a_ref[...].astype(BF16), b_ref[...].astype(BF16), dims)
        first = pl.program_id(0) == 0

        def finish(r):
            if has_add:
                r = r + add_ref[...].astype(F32)
            if has_grad:
                h_ref, w_ref, prev_ref = refs[2 + has_add:5 + has_add]
                _, vjp = jax.vjp(_f_norm, h_ref[...], w_ref[...])
                r, dw = vjp(r)
                r = r + prev_ref[...]

                @pl.when(first)
                def _():
                    refs[n_in + 1][...] = dw

                @pl.when(jnp.logical_not(first))
                def _():
                    refs[n_in + 1][...] += dw
            o_ref[...] = r.astype(o_ref.dtype)
            if has_norm:
                refs[n_in + 1][...] = _f_norm(r, refs[2 + has_add][...]).astype(BF16)

        if nk == 1:
            finish(part)
            return
        acc = refs[-1]
        kk = pl.program_id(2)

        @pl.when(kk == 0)
        def _():
            acc[...] = part

        @pl.when(kk > 0)
        def _():
            acc[...] += part

        @pl.when(kk == nk - 1)
        def _():
            finish(acc[...])

    a_spec = pl.BlockSpec((tk, tm), lambda i, j, kk: (kk, i)) if ta else pl.BlockSpec((tm, tk), lambda i, j, kk: (i, kk))
    if b_spec is None:
        b_spec = pl.BlockSpec((tn, tk), lambda i, j, kk: (j, kk)) if tb else pl.BlockSpec((tk, tn), lambda i, j, kk: (kk, j))
    plain_o = pl.BlockSpec((tm, tn), lambda i, j, kk: (i, j))
    if o_spec is None:
        o_spec, o_shape = plain_o, (m, n)
    in_specs = [a_spec, b_spec] + ([plain_o] if has_add else [])
    args = (a, b) + ((add,) if has_add else ())
    out_specs, out_shapes = o_spec, jax.ShapeDtypeStruct(tuple(o_shape), out_dtype)
    one_row = pl.BlockSpec((1, n), lambda i, j, kk: (0, 0))
    if has_norm:
        in_specs.append(one_row)
        args += (norm_w,)
        out_specs, out_shapes = [o_spec, plain_o], [out_shapes, jax.ShapeDtypeStruct((m, n), BF16)]
    if has_grad:
        assert not has_norm
        in_specs += [plain_o, one_row, plain_o]
        args += tuple(norm_grad)
        out_specs, out_shapes = [o_spec, one_row], [out_shapes, jax.ShapeDtypeStruct((1, n), F32)]
    aliases = {}
    if into is not None:
        assert into.shape == tuple(o_shape) and into.dtype == out_dtype, (into.shape, o_shape)
        in_specs.append(pl.BlockSpec(memory_space=pl.ANY))
        args += (into,)
        aliases = {n_in - 1: 0}
    return pl.pallas_call(
        body, grid=(m // tm, n // tn, nk), in_specs=in_specs, out_specs=out_specs,
        out_shape=out_shapes, name=name, input_output_aliases=aliases,
        scratch_shapes=[pltpu.VMEM((tm, tn), F32)] if nk > 1 else [],
        compiler_params=_params(("arbitrary" if has_grad else "parallel", "parallel", "arbitrary")),
    )(*args)


def _row_spec(tm, cw, coff):
    return pl.BlockSpec((tm, cw), lambda i, j: (i, j + coff))


def _full_spec(shape):
    return pl.BlockSpec(shape, lambda i, j: (0,) * len(shape))


def tile_map(fn, rows, params, outs, *, tm, ncol, name):
    t = rows[0][0].shape[0]
    nin = len(rows) + len(params)

    def body(*refs):
        res = fn(*[r[...] for r in refs[:nin]])
        res = res if isinstance(res, (tuple, list)) else (res,)
        for o_ref, r in zip(refs[nin:], res):
            o_ref[...] = r.astype(o_ref.dtype)

    in_specs = [_row_spec(tm, cw, coff) for (_, cw, coff) in rows] + [_full_spec(p.shape) for p in params]
    res = pl.pallas_call(
        body, grid=(t // tm, ncol), in_specs=in_specs,
        out_specs=[_row_spec(tm, cw, 0) for (cw, _) in outs],
        out_shape=[jax.ShapeDtypeStruct((t, cw * ncol), dt) for (cw, dt) in outs], name=name,
        compiler_params=_params(("parallel", "parallel")),
    )(*[r[0] for r in rows], *params)
    return res


def tile_vjp(fn, rows, params, cts, *, n_diff, tm, ncol, name, grad_dtypes=None):
    t = rows[0][0].shape[0]
    nr, npar, nct = len(rows), len(params), len(cts)

    def body(*refs):
        vals = [r[...] for r in refs[:nr + npar + nct]]
        diff, rest, pars = vals[:n_diff], vals[n_diff:nr], vals[nr:nr + npar]
        ctv = vals[nr + npar:nr + npar + nct]
        outs_ref = refs[nr + npar + nct:]

        def f(*a):
            res = fn(*a[:n_diff], *rest, *a[n_diff:])
            return tuple(res) if isinstance(res, (tuple, list)) else (res,)

        primal, vjp = jax.vjp(f, *[d.astype(F32) for d in diff], *pars)
        grads = vjp(tuple(c.astype(o.dtype) for c, o in zip(ctv, primal)))
        for q in range(n_diff):
            outs_ref[q][...] = grads[q].astype(outs_ref[q].dtype)
        first = (pl.program_id(0) == 0) & (pl.program_id(1) == 0)
        for q in range(npar):
            o_ref, g = outs_ref[n_diff + q], grads[n_diff + q]

            @pl.when(first)
            def _(o_ref=o_ref, g=g):
                o_ref[...] = g

            @pl.when(jnp.logical_not(first))
            def _(o_ref=o_ref, g=g):
                o_ref[...] += g

    in_specs = [_row_spec(tm, cw, coff) for (_, cw, coff) in rows] + [_full_spec(p.shape) for p in params]
    in_specs += [_row_spec(tm, cw, coff) for (_, cw, coff) in cts]
    args = [r[0] for r in rows] + list(params) + [c[0] for c in cts]
    out_specs = [_row_spec(tm, rows[q][1], 0) for q in range(n_diff)] + [_full_spec(p.shape) for p in params]
    grad_dtypes = grad_dtypes or [F32] * n_diff
    out_shape = [jax.ShapeDtypeStruct((t, rows[q][1] * ncol), grad_dtypes[q]) for q in range(n_diff)]
    out_shape += [jax.ShapeDtypeStruct(p.shape, F32) for p in params]
    res = pl.pallas_call(
        body, grid=(t // tm, ncol), in_specs=in_specs, out_specs=out_specs, out_shape=out_shape, name=name,
        compiler_params=_params(("arbitrary", "arbitrary")),
    )(*args)
    return res[:n_diff], res[n_diff:]


def _silu(x):
    return x * jax.nn.sigmoid(x)


def _f_norm(h, w):
    return h * lax.rsqrt(jnp.mean(h * h, axis=-1, keepdims=True) + EPS) * w


def _f_gnorm(o, z, w):
    return _f_norm(o, w) * _silu(z)


def _f_act(gate, val):
    return _silu(gate) * val


def _f_ple(gl, pe):
    return jax.nn.sigmoid(gl) * pe


def _f_betag(pt, alog, dtb):
    lane = lax.broadcasted_iota(jnp.int32, (1, LANES), 1)
    z = pt + dtb
    softplus = jnp.maximum(z, 0.0) + jnp.log(1.0 + jnp.exp(-jnp.abs(z)))
    g = -jnp.exp(alog) * softplus
    return jnp.where(lane < N_HEADS_A, jax.nn.sigmoid(pt), jnp.where(lane < 2 * N_HEADS_A, g, 0.0))


CONV_TM = 256
CONV_CW = 1024


def _shift_down(x, prev, s, row):
    rp = jnp.tile(pltpu.roll(prev, s, 0), (x.shape[0] // SUBLANES, 1))
    return jnp.where(row < s, rp, pltpu.roll(x, s, 0))


def _shift_up(x, nxt, s, row):
    tm = x.shape[0]
    rn = jnp.tile(pltpu.roll(nxt, SUBLANES - s, 0), (tm // SUBLANES, 1))
    return jnp.where(row >= tm - s, rn, pltpu.roll(x, tm - s, 0))


def _conv_taps(x, prev, w_ref, cols, row):
    k = w_ref.shape[0]
    y = x * w_ref[pl.ds(k - 1, 1), cols]
    for s in range(1, k):
        y = y + _shift_down(x, prev, s, row) * w_ref[pl.ds(k - 1 - s, 1), cols]
    return y


def _lane_chunks(cw):
    return [slice(cb * LANES, (cb + 1) * LANES) for cb in range(cw // LANES)]


def conv_fwd(x, w, *, name):
    t = x.shape[0]
    k, c = w.shape
    tm, cw = min(CONV_TM, t), CONV_CW
    nb8 = tm // SUBLANES

    def body(x_ref, p_ref, w_ref, o_ref):
        first = pl.program_id(1) == 0
        row = lax.broadcasted_iota(jnp.int32, (tm, LANES), 0)
        for cols in _lane_chunks(cw):
            o_ref[:, cols] = _conv_taps(x_ref[:, cols], jnp.where(first, 0.0, p_ref[:, cols]), w_ref, cols, row)

    return pl.pallas_call(
        body, grid=(c // cw, t // tm),
        in_specs=[pl.BlockSpec((tm, cw), lambda j, i: (i, j)),
                  pl.BlockSpec((SUBLANES, cw), lambda j, i: (jnp.maximum(i * nb8 - 1, 0), j)),
                  pl.BlockSpec((k, cw), lambda j, i: (0, j))],
        out_specs=pl.BlockSpec((tm, cw), lambda j, i: (i, j)),
        out_shape=jax.ShapeDtypeStruct((t, c), F32), name=name,
        compiler_params=_params(("parallel", "parallel")),
    )(x, x, w)


def conv_bwd(dy, x, w, *, name):
    t = x.shape[0]
    k, c = w.shape
    tm, cw = min(CONV_TM, t), CONV_CW
    nb8 = tm // SUBLANES
    ni = t // tm

    def body(dy_ref, dn_ref, x_ref, p_ref, w_ref, dx_ref, dw_ref):
        i = pl.program_id(1)
        first, last = i == 0, i == ni - 1
        row = lax.broadcasted_iota(jnp.int32, (tm, LANES), 0)
        for cols in _lane_chunks(cw):
            dyv, xv = dy_ref[:, cols], x_ref[:, cols]
            nxt = jnp.where(last, 0.0, dn_ref[:, cols])
            prev = jnp.where(first, 0.0, p_ref[:, cols])
            dx = dyv * w_ref[pl.ds(k - 1, 1), cols]
            dws = [jnp.sum(dyv * xv, axis=0, keepdims=True)]
            for s in range(1, k):
                dx = dx + _shift_up(dyv, nxt, s, row) * w_ref[pl.ds(k - 1 - s, 1), cols]
                dws.append(jnp.sum(dyv * _shift_down(xv, prev, s, row), axis=0, keepdims=True))
            dx_ref[:, cols] = dx.astype(dx_ref.dtype)
            for s in range(k):
                @pl.when(first)
                def _(s=s, dws=dws, cols=cols):
                    dw_ref[pl.ds(k - 1 - s, 1), cols] = dws[s]

                @pl.when(jnp.logical_not(first))
                def _(s=s, dws=dws, cols=cols):
                    dw_ref[pl.ds(k - 1 - s, 1), cols] += dws[s]

    return pl.pallas_call(
        body, grid=(c // cw, ni),
        in_specs=[pl.BlockSpec((tm, cw), lambda j, i: (i, j)),
                  pl.BlockSpec((SUBLANES, cw), lambda j, i: (jnp.minimum((i + 1) * nb8, t // SUBLANES - 1), j)),
                  pl.BlockSpec((tm, cw), lambda j, i: (i, j)),
                  pl.BlockSpec((SUBLANES, cw), lambda j, i: (jnp.maximum(i * nb8 - 1, 0), j)),
                  pl.BlockSpec((k, cw), lambda j, i: (0, j))],
        out_specs=[pl.BlockSpec((tm, cw), lambda j, i: (i, j)), pl.BlockSpec((k, cw), lambda j, i: (0, j))],
        out_shape=[jax.ShapeDtypeStruct((t, c), BF16), jax.ShapeDtypeStruct((k, c), F32)], name=name,
        compiler_params=_params(("parallel", "arbitrary")),
    )(dy, dy, x, x, w)


FFN_TM = 128
FFN_CW = D_FF // 2


def _ffn_specs(t, tm, cw, k):
    ncol = D_FF // cw
    cur = lambda off: pl.BlockSpec((tm, cw), lambda j, i: (i, j + off))
    prev = lambda off, hr: pl.BlockSpec((hr, cw), lambda j, i: (jnp.maximum(i * (tm // hr) - 1, 0), j + off))
    nxt = lambda off, hr: pl.BlockSpec((hr, cw), lambda j, i: (jnp.minimum((i + 1) * (tm // hr), t // hr - 1), j + off))
    taps = lambda off: pl.BlockSpec((k, cw), lambda j, i: (0, j + off))
    return cur, prev, nxt, taps, ncol


def _rows_before(ref, cols, first):
    return jnp.where(first, 0.0, ref[ref.shape[0] - SUBLANES:, cols].astype(F32))


def conv_act_fwd(u, w, *, name):
    t, k = u.shape[0], w.shape[0]
    tm, cw = min(FFN_TM, t), FFN_CW
    cur, prev, _, taps, ncol = _ffn_specs(t, tm, cw, k)

    def body(ug_ref, pg_ref, uv_ref, pv_ref, wg_ref, wv_ref, o_ref):
        first = pl.program_id(1) == 0
        row = lax.broadcasted_iota(jnp.int32, (tm, LANES), 0)
        for cb in range(cw // LANES):
            cols = slice(cb * LANES, (cb + 1) * LANES)
            cg = _conv_taps(ug_ref[:, cols].astype(F32), _rows_before(pg_ref, cols, first), wg_ref, cols, row)
            cv = _conv_taps(uv_ref[:, cols].astype(F32), _rows_before(pv_ref, cols, first), wv_ref, cols, row)
            o_ref[:, cols] = _f_act(cg, cv).astype(o_ref.dtype)

    return pl.pallas_call(
        body, grid=(ncol, t // tm),
        in_specs=[cur(0), prev(0, BF16_ROWS), cur(ncol), prev(ncol, BF16_ROWS), taps(0), taps(ncol)],
        out_specs=cur(0), out_shape=jax.ShapeDtypeStruct((t, D_FF), BF16), name=name,
        compiler_params=_params(("parallel", "parallel")),
    )(u, u, u, u, w, w)


def conv_act_bwd(u, dact, w, *, name):
    t, k = u.shape[0], w.shape[0]
    tm, cw = min(FFN_TM, t), FFN_CW
    cur, prev, nxt, taps, ncol = _ffn_specs(t, tm, cw, k)
    ni = t // tm

    def body(ug_ref, pg_ref, ng_ref, uv_ref, pv_ref, nv_ref, d_ref, dn_ref, wg_ref, wv_ref, dg_ref, dv_ref, dwg_ref, dwv_ref):
        i = pl.program_id(1)
        first, last = i == 0, i == ni - 1
        row = lax.broadcasted_iota(jnp.int32, (tm, LANES), 0)
        row8 = lax.broadcasted_iota(jnp.int32, (SUBLANES, LANES), 0)
        for cb in range(cw // LANES):
            cols = slice(cb * LANES, (cb + 1) * LANES)
            ug, uv = ug_ref[:, cols].astype(F32), uv_ref[:, cols].astype(F32)
            pg, pv = _rows_before(pg_ref, cols, first), _rows_before(pv_ref, cols, first)
            sg = [ug] + [_shift_down(ug, pg, s, row) for s in range(1, k)]
            sv = [uv] + [_shift_down(uv, pv, s, row) for s in range(1, k)]
            taps = lambda xs, w_ref: sum(xs[s] * w_ref[pl.ds(k - 1 - s, 1), cols] for s in range(k))
            _, vjp = jax.vjp(_f_act, taps(sg, wg_ref), taps(sv, wv_ref))
            dcg, dcv = vjp(d_ref[:, cols])
            after = lambda ref: ref[:SUBLANES, cols].astype(F32)
            _, vjp_n = jax.vjp(_f_act, _conv_taps(after(ng_ref), ug[tm - SUBLANES:], wg_ref, cols, row8),
                               _conv_taps(after(nv_ref), uv[tm - SUBLANES:], wv_ref, cols, row8))
            dcgn, dcvn = vjp_n(jnp.where(last, 0.0, dn_ref[:, cols]))
            for dc, dcn, xs, w_ref, dx_ref, dw_ref in ((dcg, dcgn, sg, wg_ref, dg_ref, dwg_ref),
                                                       (dcv, dcvn, sv, wv_ref, dv_ref, dwv_ref)):
                dx = dc * w_ref[pl.ds(k - 1, 1), cols]
                dws = [jnp.sum(dc * xs[0], axis=0, keepdims=True)]
                for s in range(1, k):
                    dx = dx + _shift_up(dc, dcn, s, row) * w_ref[pl.ds(k - 1 - s, 1), cols]
                    dws.append(jnp.sum(dc * xs[s], axis=0, keepdims=True))
                dx_ref[:, cols] = dx.astype(dx_ref.dtype)
                for s in range(k):
                    @pl.when(first)
                    def _(s=s, dw_ref=dw_ref, dws=dws):
                        dw_ref[pl.ds(k - 1 - s, 1), cols] = dws[s]

                    @pl.when(jnp.logical_not(first))
                    def _(s=s, dw_ref=dw_ref, dws=dws):
                        dw_ref[pl.ds(k - 1 - s, 1), cols] += dws[s]

    half = jax.ShapeDtypeStruct((t, D_FF), BF16)
    dwh = jax.ShapeDtypeStruct((k, D_FF), F32)
    return pl.pallas_call(
        body, grid=(ncol, ni),
        in_specs=[cur(0), prev(0, BF16_ROWS), nxt(0, BF16_ROWS), cur(ncol), prev(ncol, BF16_ROWS), nxt(ncol, BF16_ROWS),
                  cur(0), nxt(0, SUBLANES), taps(0), taps(ncol)],
        out_specs=[cur(0), cur(0), taps(0), taps(0)], out_shape=[half, half, dwh, dwh], name=name,
        compiler_params=_params(("parallel", "arbitrary")),
    )(u, u, u, u, u, u, dact, dact, w, w)


def _each(f, *lists):
    return [f(*a) for a in zip(*lists)]


@jax.custom_vjp
def _inv_unit_lower(lms):
    return _inv_blocks(lms)


def _inv_blocks(lms):
    c = lms[0].shape[0]
    ri = lax.broadcasted_iota(jnp.int32, (c, c), 0)
    ci = lax.broadcasted_iota(jnp.int32, (c, c), 1)
    eye = (ri == ci).astype(F32)
    dms = _each(lambda lm: eye - jnp.where((ri >> 1) == (ci >> 1), lm, 0.0), lms)
    for lv in range(1, int(math.log2(c))):
        below = ((ri >> (lv + 1)) == (ci >> (lv + 1))) & ((ri >> lv) != (ci >> lv))
        dbs = _each(lambda dm: dm.astype(BF16), dms)
        ods = _each(lambda lm, db: _dot(jnp.where(below, lm, 0.0).astype(BF16), db).astype(BF16), lms, dbs)
        dms = _each(lambda dm, db, od: dm - _dot(db, od), dms, dbs, ods)
    return dms


def _inv_fwd(lms):
    tms = _inv_blocks(lms)
    return tms, tms


def _inv_bwd(tms, dts):
    tbs = _each(lambda tm: tm.astype(BF16), tms)
    mid = _each(lambda tb, dt: _dot(tb, dt.astype(BF16), TN).astype(BF16), tbs, dts)
    return (_each(lambda m, tb: -_dot(m, tb, NT), mid, tbs),)


_inv_unit_lower.defvjp(_inv_fwd, _inv_bwd)


@jax.custom_vjp
def _inv_known(lms, tms):
    return tms


_inv_known.defvjp(lambda lms, tms: (tms, tms), lambda tms, dts: _inv_bwd(tms, dts) + (_each(jnp.zeros_like, tms),))


def _l2n(x):
    return x * lax.rsqrt(jnp.sum(x * x, axis=-1, keepdims=True) + EPS)


def _prep_fn(cqs, cks, cvs, bg, sel_b, sel_g, tms=None):
    c = cqs[0].shape[0]
    ri = lax.broadcasted_iota(jnp.int32, (c, c), 0)
    ci = lax.broadcasted_iota(jnp.int32, (c, c), 1)
    eye = (ri == ci).astype(F32)
    incl, strict = ci <= ri, ci < ri
    last = lax.broadcasted_iota(jnp.int32, (c, 1), 0) == c - 1
    to_row = lambda col: jnp.sum(col * eye, axis=0, keepdims=True)
    qs = _each(lambda a: _l2n(_silu(a)) * (HEAD_DIM_A ** -0.5), cqs)
    ks = _each(lambda a: _l2n(_silu(a)), cks)
    vbs = _each(lambda a: _silu(a).astype(BF16), cvs)
    betas = _each(lambda m: jnp.sum(bg * m, axis=1, keepdims=True), sel_b)
    gs = _each(lambda m: jnp.sum(bg * m, axis=1, keepdims=True), sel_g)
    gcss = _each(lambda g: jnp.sum(jnp.where(incl, to_row(g), 0.0), axis=1, keepdims=True), gs)
    gtots = _each(lambda gcs: jnp.sum(jnp.where(last, gcs, 0.0), axis=0, keepdims=True), gcss)
    decays = _each(lambda gcs: jnp.exp(jnp.where(incl, gcs - to_row(gcs), NEG)), gcss)
    kbs = _each(lambda k: k.astype(BF16), ks)
    lms = _each(lambda beta, kb, dec: jnp.where(strict, beta * _dot(kb, kb, NT) * dec, 0.0), betas, kbs, decays)
    tms = _inv_unit_lower(lms) if tms is None else _inv_known(lms, tms)
    ams = _each(lambda tm, beta: (tm * to_row(beta)).astype(BF16), tms, betas)
    gams = _each(jnp.exp, gcss)
    u0s = _each(_dot, ams, vbs)
    wks = _each(lambda am, gam, k: _dot(am, (gam * k).astype(BF16)), ams, gams, ks)
    qks = _each(lambda q, kb, dec: _dot(q.astype(BF16), kb, NT) * dec, qs, kbs, decays)
    qds = _each(lambda q, gam: q * gam, qs, gams)
    kds = _each(lambda k, gtot, gcs: k * jnp.exp(gtot - gcs), ks, gtots, gcss)
    gls = _each(lambda gtot: jnp.exp(gtot) * jnp.ones((SUBLANES, LANES), F32), gtots)
    return u0s, wks, qds, kds, qks, gls, tms


def _head_masks(h):
    lane = lax.broadcasted_iota(jnp.int32, (1, LANES), 1)
    return (lane == h).astype(F32), (lane == h + N_HEADS_A).astype(F32)


def _hsl(j):
    return slice(j * HEAD_DIM_A, (j + 1) * HEAD_DIM_A)


def gnorm_fwd(o, zsrc, w, *, name):
    t, width = o.shape
    tm = min(256, t)
    zoff = zsrc.shape[1] // width - 1

    def body(o_ref, z_ref, w_ref, out_ref):
        for h in range(N_HEADS_A):
            out_ref[:, _hsl(h)] = _f_gnorm(o_ref[:, _hsl(h)], z_ref[:, _hsl(h)], w_ref[...]).astype(out_ref.dtype)

    rows = pl.BlockSpec((tm, width), lambda i: (i, 0))
    return pl.pallas_call(
        body, grid=(t // tm,),
        in_specs=[rows, pl.BlockSpec((tm, width), lambda i: (i, zoff)), pl.BlockSpec(w.shape, lambda i: (0, 0))],
        out_specs=rows, out_shape=jax.ShapeDtypeStruct((t, width), BF16), name=name, compiler_params=_params(("parallel",)),
    )(o, zsrc, w)


def gnorm_bwd(o, zsrc, w, don, *, name):
    t, width = o.shape
    tm = min(256, t)
    zoff = zsrc.shape[1] // width - 1

    def body(o_ref, z_ref, w_ref, d_ref, do_ref, dz_ref, dw_ref):
        dw = jnp.zeros(w.shape, F32)
        for h in range(N_HEADS_A):
            _, vjp = jax.vjp(_f_gnorm, o_ref[:, _hsl(h)], z_ref[:, _hsl(h)], w_ref[...])
            do, dz, dwh = vjp(d_ref[:, _hsl(h)])
            do_ref[:, _hsl(h)] = do.astype(do_ref.dtype)
            dz_ref[:, _hsl(h)] = dz.astype(dz_ref.dtype)
            dw = dw + dwh
        first = pl.program_id(0) == 0

        @pl.when(first)
        def _():
            dw_ref[...] = dw

        @pl.when(jnp.logical_not(first))
        def _():
            dw_ref[...] += dw

    rows = pl.BlockSpec((tm, width), lambda i: (i, 0))
    wspec = pl.BlockSpec(w.shape, lambda i: (0, 0))
    return pl.pallas_call(
        body, grid=(t // tm,),
        in_specs=[rows, pl.BlockSpec((tm, width), lambda i: (i, zoff)), wspec, rows],
        out_specs=[rows, rows, wspec],
        out_shape=[jax.ShapeDtypeStruct((t, width), BF16)] * 2 + [jax.ShapeDtypeStruct(w.shape, F32)], name=name,
        compiler_params=_params(("arbitrary",)),
    )(o, zsrc, w, don)


def delta_prep(cqkv, bg, *, name):
    t = cqkv.shape[0]
    nh, hd, n = N_HEADS_A, HEAD_DIM_A, t // CHUNK

    def body(cq_ref, ck_ref, cv_ref, bg_ref, u0_ref, wk_ref, qd_ref, kd_ref, qk_ref, tm_ref, gl_ref):
        heads = range(nh)
        masks = [_head_masks(j) for j in heads]
        res = _prep_fn([cq_ref[:, _hsl(j)] for j in heads], [ck_ref[:, _hsl(j)] for j in heads],
                       [cv_ref[:, _hsl(j)] for j in heads], bg_ref[...], [m[0] for m in masks], [m[1] for m in masks])
        for o_ref, rs in zip((u0_ref, wk_ref, qd_ref, kd_ref, qk_ref, tm_ref), res[:5] + (res[6],)):
            for j in heads:
                o_ref[:, _hsl(j)] = rs[j].astype(o_ref.dtype)
        for j in heads:
            gl_ref[j * SUBLANES:(j + 1) * SUBLANES, :] = res[5][j]

    blk = lambda off: pl.BlockSpec((CHUNK, nh * hd), lambda i: (i, off))
    res = pl.pallas_call(
        body, grid=(n,),
        in_specs=[blk(0), blk(1), blk(2), pl.BlockSpec((CHUNK, LANES), lambda i: (i, 0))],
        out_specs=[blk(0)] * 6 + [pl.BlockSpec((nh * SUBLANES, LANES), lambda i: (i, 0))],
        out_shape=[jax.ShapeDtypeStruct((t, nh * hd), dt) for dt in (F32, BF16, BF16, BF16, BF16, F32)]
        + [jax.ShapeDtypeStruct((n * nh * SUBLANES, LANES), F32)],
        name=name, compiler_params=_params(("parallel",)),
    )(cqkv, cqkv, cqkv, bg)
    return [*res[:5], res[6]], res[5]


def delta_prep_bwd(cqkv, bg, tms, cts, *, name):
    t = cqkv.shape[0]
    nh, hd, n = N_HEADS_A, HEAD_DIM_A, t // CHUNK

    def body(cq_ref, ck_ref, cv_ref, bg_ref, tm_ref, c0, c1, c2, c3, c4, c5, dc_ref, dbg_ref):
        heads = range(nh)
        masks = [_head_masks(j) for j in heads]
        known = [tm_ref[:, _hsl(j)] for j in heads]
        _, vjp = jax.vjp(lambda a, b, c, d: _prep_fn(a, b, c, d, [m[0] for m in masks], [m[1] for m in masks], known)[:6],
                         [cq_ref[:, _hsl(j)] for j in heads], [ck_ref[:, _hsl(j)] for j in heads],
                         [cv_ref[:, _hsl(j)] for j in heads], bg_ref[...])
        cts = tuple([c[:, _hsl(j)] for j in heads] for c in (c0, c1, c2, c3, c4))
        dqs, dks, dvs, dbg = vjp(cts + ([c5[j * SUBLANES:(j + 1) * SUBLANES, :] for j in heads],))
        for part, ds in enumerate((dqs, dks, dvs)):
            for j in heads:
                dc_ref[:, _hsl(part * nh + j)] = ds[j]
        dbg_ref[...] = dbg

    blk = lambda off: pl.BlockSpec((CHUNK, nh * hd), lambda i: (i, off))
    gl_spec = pl.BlockSpec((nh * SUBLANES, LANES), lambda i: (i, 0))
    bg_spec = pl.BlockSpec((CHUNK, LANES), lambda i: (i, 0))
    return pl.pallas_call(
        body, grid=(n,),
        in_specs=[blk(0), blk(1), blk(2), bg_spec] + [blk(0)] * 6 + [gl_spec],
        out_specs=[pl.BlockSpec((CHUNK, 3 * nh * hd), lambda i: (i, 0)), bg_spec],
        out_shape=[jax.ShapeDtypeStruct((t, 3 * nh * hd), F32), jax.ShapeDtypeStruct((t, LANES), F32)],
        name=name, compiler_params=_params(("parallel",)),
    )(cqkv, cqkv, cqkv, bg, tms, *cts)


def delta_scan(u0, wk, qd, kd, qk, gl, *, name):
    t = u0.shape[0]
    nh, hd, n = N_HEADS_A, HEAD_DIM_A, t // CHUNK

    def body(u0_ref, wk_ref, qd_ref, kd_ref, qk_ref, gl_ref, o_ref, sin_ref, s_ref):
        @pl.when(pl.program_id(0) == 0)
        def _():
            s_ref[...] = jnp.zeros_like(s_ref)

        heads = list(range(nh))
        cols = lambda ref: [ref[:, _hsl(h)].astype(BF16) for h in heads]
        ss = [s_ref[h] for h in heads]
        for h in heads:
            sin_ref[h] = ss[h]
        sbs = _each(lambda s: s.astype(BF16), ss)
        ubs = _each(lambda h, wkb, sb: (u0_ref[:, _hsl(h)] - _dot(wkb, sb)).astype(BF16), heads, cols(wk_ref), sbs)
        os_ = _each(lambda qdb, sb, qkb, ub: _dot(qdb, sb) + _dot(qkb, ub), cols(qd_ref), sbs, cols(qk_ref), ubs)
        sn = _each(lambda h, s, kdb, ub: gl_ref[pl.ds(h * SUBLANES, 1), :] * s + _dot(kdb, ub, TN), heads, ss, cols(kd_ref), ubs)
        for h in heads:
            o_ref[:, _hsl(h)] = os_[h]
            s_ref[h] = sn[h]

    blk = pl.BlockSpec((CHUNK, nh * hd), lambda i: (i, 0))
    return pl.pallas_call(
        body, grid=(n,),
        in_specs=[blk] * 5 + [pl.BlockSpec((nh * SUBLANES, LANES), lambda i: (i, 0))],
        out_specs=[blk, pl.BlockSpec((None, nh, hd, hd), lambda i: (i, 0, 0, 0))],
        out_shape=[jax.ShapeDtypeStruct((t, nh * hd), F32), jax.ShapeDtypeStruct((n, nh, hd, hd), F32)],
        scratch_shapes=[pltpu.VMEM((nh, hd, hd), F32)], name=name,
        compiler_params=_params(("arbitrary",)),
    )(u0, wk, qd, kd, qk, gl)


def delta_scan_bwd(do, u0, wk, qd, kd, qk, gl, s_in, *, name):
    t = u0.shape[0]
    nh, hd, n = N_HEADS_A, HEAD_DIM_A, t // CHUNK

    def body(do_ref, u0_ref, wk_ref, qd_ref, kd_ref, qk_ref, gl_ref, sin_ref,
             du0_ref, dwk_ref, dqd_ref, dkd_ref, dqk_ref, dgl_ref, ds_ref):
        @pl.when(pl.program_id(0) == 0)
        def _():
            ds_ref[...] = jnp.zeros_like(ds_ref)

        corner = (lax.broadcasted_iota(jnp.int32, (SUBLANES, LANES), 0) == 0) & (lax.broadcasted_iota(jnp.int32, (SUBLANES, LANES), 1) == 0)
        heads = list(range(nh))
        cols = lambda ref: [ref[:, _hsl(h)].astype(BF16) for h in heads]
        ss, dss = [sin_ref[h] for h in heads], [ds_ref[h] for h in heads]
        sbs, dsbs = _each(lambda s: s.astype(BF16), ss), _each(lambda d: d.astype(BF16), dss)
        dobs, wkbs, qdbs, kdbs, qkbs = cols(do_ref), cols(wk_ref), cols(qd_ref), cols(kd_ref), cols(qk_ref)
        ubs = _each(lambda h, wkb, sb: (u0_ref[:, _hsl(h)] - _dot(wkb, sb)).astype(BF16), heads, wkbs, sbs)
        dus = _each(lambda qkb, dob, kdb, dsb: _dot(qkb, dob, TN) + _dot(kdb, dsb), qkbs, dobs, kdbs, dsbs)
        dubs = _each(lambda du: du.astype(BF16), dus)
        dwks = _each(lambda dub, sb: -_dot(dub, sb, NT), dubs, sbs)
        dqds = _each(lambda dob, sb: _dot(dob, sb, NT), dobs, sbs)
        dkds = _each(lambda ub, dsb: _dot(ub, dsb, NT), ubs, dsbs)
        dqks = _each(lambda dob, ub: _dot(dob, ub, NT), dobs, ubs)
        dgls = _each(lambda s, d: jnp.sum(jnp.sum(s * d, axis=1, keepdims=True), axis=0, keepdims=True), ss, dss)
        dsn = _each(lambda h, d, qdb, dob, wkb, dub: gl_ref[pl.ds(h * SUBLANES, 1), :] * d + _dot(qdb, dob, TN) - _dot(wkb, dub, TN),
                    heads, dss, qdbs, dobs, wkbs, dubs)
        for h in heads:
            du0_ref[:, _hsl(h)] = dus[h]
            dwk_ref[:, _hsl(h)] = dwks[h]
            dqd_ref[:, _hsl(h)] = dqds[h]
            dkd_ref[:, _hsl(h)] = dkds[h]
            dqk_ref[:, _hsl(h)] = dqks[h]
            dgl_ref[h * SUBLANES:(h + 1) * SUBLANES, :] = jnp.where(corner, dgls[h], 0.0)
            ds_ref[h] = dsn[h]

    blk = pl.BlockSpec((CHUNK, nh * hd), lambda i: (n - 1 - i, 0))
    gl_spec = pl.BlockSpec((nh * SUBLANES, LANES), lambda i: (n - 1 - i, 0))
    return pl.pallas_call(
        body, grid=(n,),
        in_specs=[blk] * 6 + [gl_spec, pl.BlockSpec((None, nh, hd, hd), lambda i: (n - 1 - i, 0, 0, 0))],
        out_specs=[blk] * 5 + [gl_spec],
        out_shape=[jax.ShapeDtypeStruct((t, nh * hd), F32)] * 5 + [jax.ShapeDtypeStruct((n * nh * SUBLANES, LANES), F32)],
        scratch_shapes=[pltpu.VMEM((nh, hd, hd), F32)], name=name,
        compiler_params=_params(("arbitrary",)),
    )(do, u0, wk, qd, kd, qk, gl, s_in)


N_PAIRS = N_HEADS_B // 2
PAIRS_PER_KV = N_PAIRS // N_KV_B


def _psl(j):
    return slice(j * LANES, (j + 1) * LANES)


KV_STEP = 4


def _att_fn(qps, kcs, kps, vcs, vps, sinks, kv0, first):
    w = WINDOW
    lane = lax.broadcasted_iota(jnp.int32, (1, LANES), 1)
    lo = (lane < HEAD_DIM_B).astype(F32)
    qi = lax.broadcasted_iota(jnp.int32, (w, w), 0)
    kj = lax.broadcasted_iota(jnp.int32, (w, w), 1)
    dist_c = (qi - kj).astype(F32)
    valid_c = kj <= qi
    valid_p = (kj > qi) & (first < 0.5)
    bf = lambda xs: [a.astype(BF16) for a in xs]
    kcb, kpb, vcb, vpb = bf(kcs), bf(kps), bf(vcs), bf(vps)
    scale = HEAD_DIM_B ** -0.5
    heads = [(g, j, half) for g in range(len(kcs)) for j in range(PAIRS_PER_KV) for half in range(2)]
    kvs = [g for g, _, _ in heads]
    hmasks = [lo if half == 0 else 1.0 - lo for _, _, half in heads]
    hds = [2.0 * (PAIRS_PER_KV * (kv0 + g) + j) + half for g, j, half in heads]
    slopes = _each(lambda hd: jnp.exp(-(hd + 1.0) * (8.0 / N_HEADS_B * math.log(2.0))), hds)
    snks = _each(lambda hd: jnp.sum(sinks * (lane.astype(F32) == hd).astype(F32), axis=1, keepdims=True), hds)
    qhs = _each(lambda h, hm: (qps[h[0] * PAIRS_PER_KV + h[1]] * hm).astype(BF16), heads, hmasks)
    lcs = _each(lambda qh, g, sl: jnp.where(valid_c, _dot(qh, kcb[g], NT) * scale - sl * dist_c, NEG), qhs, kvs, slopes)
    lps = _each(lambda qh, g, sl: jnp.where(valid_p, _dot(qh, kpb[g], NT) * scale - sl * (dist_c + w), NEG), qhs, kvs, slopes)
    ms = _each(lambda lc, lp, sk: lax.stop_gradient(jnp.maximum(jnp.maximum(jnp.max(lc, axis=1, keepdims=True),
                                                                            jnp.max(lp, axis=1, keepdims=True)), sk)), lcs, lps, snks)
    ecs = _each(lambda lc, m: jnp.exp(lc - m), lcs, ms)
    eps = _each(lambda lp, m: jnp.exp(lp - m), lps, ms)
    invs = _each(lambda ec, ep, sk, m: 1.0 / (jnp.sum(ec, axis=1, keepdims=True) + jnp.sum(ep, axis=1, keepdims=True) + jnp.exp(sk - m)),
                 ecs, eps, snks, ms)
    ohs = _each(lambda ec, ep, inv, g, hm: (_dot((ec * inv).astype(BF16), vcb[g]) + _dot((ep * inv).astype(BF16), vpb[g])) * hm,
                ecs, eps, invs, kvs, hmasks)
    return [ohs[2 * j] + ohs[2 * j + 1] for j in range(len(qps))]


def _scalar11(v):
    return jnp.full((1, 1), v, F32)


def _att_specs(row_of):
    cur = pl.BlockSpec((WINDOW, KV_STEP * LANES), lambda i, kv: (row_of(i), kv))
    prev = pl.BlockSpec((WINDOW, KV_STEP * LANES), lambda i, kv: (jnp.maximum(row_of(i) - 1, 0), kv))
    qs = pl.BlockSpec((WINDOW, KV_STEP * PAIRS_PER_KV * LANES), lambda i, kv: (row_of(i), kv))
    return qs, cur, prev, pl.BlockSpec((1, LANES), lambda i, kv: (0, 0))


def swa_fwd(qsrc, kd, vd, sinks, *, name):
    t = kd.shape[0]
    nb = t // WINDOW
    npair = KV_STEP * PAIRS_PER_KV

    def body(q_ref, kc_ref, kp_ref, vc_ref, vp_ref, s_ref, o_ref):
        first = _scalar11((pl.program_id(0) == 0).astype(F32))
        kv0 = _scalar11((pl.program_id(1) * KV_STEP).astype(F32))
        per_kv = lambda ref: [ref[:, _psl(g)] for g in range(KV_STEP)]
        outs = _att_fn([q_ref[:, _psl(j)] for j in range(npair)], per_kv(kc_ref), per_kv(kp_ref), per_kv(vc_ref), per_kv(vp_ref),
                       s_ref[...], kv0, first)
        for j in range(npair):
            o_ref[:, _psl(j)] = outs[j].astype(o_ref.dtype)

    qs, cur, prev, sk = _att_specs(lambda i: i)
    return pl.pallas_call(
        body, grid=(nb, N_KV_B // KV_STEP), in_specs=[qs, cur, prev, cur, prev, sk],
        out_specs=qs, out_shape=jax.ShapeDtypeStruct((t, N_PAIRS * LANES), BF16), name=name,
        compiler_params=_params(("parallel", "parallel")),
    )(qsrc, kd, kd, vd, vd, sinks)


def swa_bwd(do, qsrc, kd, vd, sinks, *, name):
    t = kd.shape[0]
    nb = t // WINDOW

    npair = KV_STEP * PAIRS_PER_KV

    def body(do_ref, q_ref, kc_ref, kp_ref, vc_ref, vp_ref, s_ref, dq_ref, dk_ref, dv_ref, ds_ref, carry_k, carry_v):
        step, kvg = pl.program_id(0), pl.program_id(1)
        first = _scalar11((step == nb - 1).astype(F32))

        @pl.when((step == 0) & (kvg == 0))
        def _():
            carry_k[...] = jnp.zeros_like(carry_k)
            carry_v[...] = jnp.zeros_like(carry_v)
            ds_ref[...] = jnp.zeros_like(ds_ref)

        kv0 = _scalar11((kvg * KV_STEP).astype(F32))
        per_kv = lambda ref: [ref[:, _psl(g)].astype(F32) for g in range(KV_STEP)]
        _, vjp = jax.vjp(lambda *a: _att_fn(*a, kv0, first), [q_ref[:, _psl(j)].astype(F32) for j in range(npair)],
                         per_kv(kc_ref), per_kv(kp_ref), per_kv(vc_ref), per_kv(vp_ref), s_ref[...])
        dqs, dkc, dkp, dvc, dvp, dsk = vjp([do_ref[:, _psl(j)].astype(F32) for j in range(npair)])
        for j in range(npair):
            dq_ref[:, _psl(j)] = dqs[j].astype(dq_ref.dtype)
        ds_ref[...] += dsk
        fold = lambda g: g + pltpu.roll(g, HEAD_DIM_B, 1)
        for g in range(KV_STEP):
            kv = kvg * KV_STEP + g
            dk_ref[:, _psl(g)] = fold(dkc[g] + carry_k[kv]).astype(dk_ref.dtype)
            dv_ref[:, _psl(g)] = fold(dvc[g] + carry_v[kv]).astype(dv_ref.dtype)
            carry_k[kv] = dkp[g]
            carry_v[kv] = dvp[g]

    qs, cur, prev, sk = _att_specs(lambda i: nb - 1 - i)
    return pl.pallas_call(
        body, grid=(nb, N_KV_B // KV_STEP),
        in_specs=[qs, qs, cur, prev, cur, prev, sk],
        out_specs=[qs, cur, cur, sk],
        out_shape=[jax.ShapeDtypeStruct((t, N_PAIRS * LANES), BF16), jax.ShapeDtypeStruct((t, N_KV_B * LANES), BF16),
                   jax.ShapeDtypeStruct((t, N_KV_B * LANES), BF16), jax.ShapeDtypeStruct((1, LANES), F32)],
        scratch_shapes=[pltpu.VMEM((N_KV_B, WINDOW, LANES), F32), pltpu.VMEM((N_KV_B, WINDOW, LANES), F32)],
        name=name, compiler_params=_params(("arbitrary", "arbitrary")),
    )(do, qsrc, kd, kd, vd, vd, sinks)


def loss_head(h, tgt, w, *, name):
    t, d = h.shape
    tm = min(256, t)

    def body(h_ref, t_ref, w_ref, dh_ref, dw_ref, l_ref):
        tg = t_ref[...]

        def f(hv, wv):
            err = _f_norm(hv, wv) - tg
            return 0.5 * jnp.sum(jnp.sum(err * err, axis=1, keepdims=True), axis=0, keepdims=True) * (1.0 / d)

        lv, vjp = jax.vjp(f, h_ref[...], w_ref[...])
        dh, dw = vjp(jnp.ones((1, 1), F32))
        dh_ref[...] = dh
        first = pl.program_id(0) == 0

        @pl.when(first)
        def _():
            dw_ref[...] = dw
            l_ref[...] = lv * jnp.ones((1, LANES), F32)

        @pl.when(jnp.logical_not(first))
        def _():
            dw_ref[...] += dw
            l_ref[...] += lv * jnp.ones((1, LANES), F32)

    rows = pl.BlockSpec((tm, d), lambda i: (i, 0))
    one = lambda c: pl.BlockSpec((1, c), lambda i: (0, 0))
    return pl.pallas_call(
        body, grid=(t // tm,), in_specs=[rows, rows, one(d)], out_specs=[rows, one(d), one(LANES)],
        out_shape=[jax.ShapeDtypeStruct((t, d), F32), jax.ShapeDtypeStruct((1, d), F32), jax.ShapeDtypeStruct((1, LANES), F32)],
        name=name, compiler_params=_params(("arbitrary",)),
    )(h, tgt, w)


def _row_tile(r, cap=256):
    tr = r
    if r % SUBLANES == 0:
        for cand in range(SUBLANES, min(r, cap) + 1, SUBLANES):
            if r % cand == 0:
                tr = cand
    return tr


def _adamw_update(wv, gv, mv, vv):
    mn = ADAM_B1 * mv + (1.0 - ADAM_B1) * gv
    vn = ADAM_B2 * vv + (1.0 - ADAM_B2) * jnp.square(gv)
    m_hat = mn / (1.0 - ADAM_B1 ** ADAM_STEP)
    v_hat = vn / (1.0 - ADAM_B2 ** ADAM_STEP)
    return -ADAM_LR * (m_hat / (jnp.sqrt(v_hat) + ADAM_EPS) + ADAM_WD * wv), mn, vn


def adamw_layers(w, halves, m, v, *, name):
    nl, r, c = w.shape
    tr = _row_tile(r // 2)
    nbh = r // 2 // tr

    def body(w_ref, *rest):
        g_refs, m_ref, v_ref = rest[:2 * nl], rest[2 * nl], rest[2 * nl + 1]
        d_ref, mo_ref, vo_ref, go_ref = rest[2 * nl + 2:]
        layer, i = pl.program_id(0), pl.program_id(1)
        mine = (i < nbh) == (lax.axis_index("c") == 0)
        gv = jnp.where(mine, g_refs[0][...], g_refs[1][...])
        for k in range(1, nl):
            gv = jnp.where(layer == k, jnp.where(mine, g_refs[2 * k][...], g_refs[2 * k + 1][...]), gv)
        d_ref[...], mo_ref[...], vo_ref[...] = _adamw_update(w_ref[...], gv, m_ref[...], v_ref[...])
        go_ref[...] = gv

    spec3 = pl.BlockSpec((None, tr, c), lambda k, i: (k, i, 0))
    g_specs = [pl.BlockSpec((tr, c), lambda k, i, q=q: (jnp.where(k == q, i % nbh, 0), 0)) for q in range(nl) for _ in range(2)]
    return pl.pallas_call(
        body, grid=(nl, r // tr), in_specs=[spec3] + g_specs + [spec3, spec3], out_specs=[spec3] * 4,
        out_shape=[jax.ShapeDtypeStruct((nl, r, c), F32)] * 4, name=name, compiler_params=_params(("arbitrary", "arbitrary")),
    )(w, *[h for pair in halves for h in pair], m, v)


def adamw(w, g, m, v, *, name):
    r, c = w.shape
    tr = _row_tile(r)

    def body(w_ref, g_ref, m_ref, v_ref, d_ref, mo_ref, vo_ref):
        d_ref[...], mo_ref[...], vo_ref[...] = _adamw_update(w_ref[...], g_ref[...], m_ref[...], v_ref[...])

    spec = pl.BlockSpec((tr, c), lambda i: (i, 0))
    return pl.pallas_call(
        body, grid=(r // tr,), in_specs=[spec] * 4, out_specs=[spec] * 3,
        out_shape=[jax.ShapeDtypeStruct((r, c), F32)] * 3, name=name, compiler_params=_params(("parallel",)),
    )(w, g, m, v)


def _place():
    return lax.axis_index("x"), lax.axis_index("y"), lax.axis_index("c")


def allgather8(blk, *, name):
    def body(x_ref, out_ref, send_sems, recv_sems, local_sem):
        x, y, c = _place()
        me = 4 * x + 2 * y + c
        mine = pltpu.make_async_copy(x_ref, out_ref.at[me], local_sem)
        mine.start()
        sent = []
        for k in range(1, N_DEV):
            to = (x ^ ((k >> 2) & 1), y ^ ((k >> 1) & 1), c ^ (k & 1))
            cp = pltpu.make_async_remote_copy(src_ref=x_ref, dst_ref=out_ref.at[me], send_sem=send_sems.at[k - 1],
                                              recv_sem=recv_sems.at[k - 1], device_id=to, device_id_type=MESH)
            cp.start()
            sent.append(cp)
        for k in range(1, N_DEV):
            frm = me ^ k
            pltpu.make_async_remote_copy(src_ref=x_ref, dst_ref=out_ref.at[frm], send_sem=send_sems.at[k - 1],
                                         recv_sem=recv_sems.at[k - 1], device_id=(x, y, c), device_id_type=MESH).wait_recv()
        for cp in sent:
            cp.wait_send()
        mine.wait()

    vm = pl.BlockSpec(memory_space=pltpu.VMEM)
    return pl.pallas_call(
        body, in_specs=[vm], out_specs=vm, out_shape=jax.ShapeDtypeStruct((N_DEV,) + blk.shape, blk.dtype), name=name,
        scratch_shapes=[pltpu.SemaphoreType.DMA((N_DEV - 1,)), pltpu.SemaphoreType.DMA((N_DEV - 1,)), pltpu.SemaphoreType.DMA],
    )(blk)


def _other_chips(x, y):
    return [(1 - x, y), (x, 1 - y), (1 - x, 1 - y)]


def _hbm_call(body, ins, out_shapes, n_sems, name):
    hbm = pl.BlockSpec(memory_space=pl.ANY)
    return pl.pallas_call(
        body, in_specs=[hbm] * len(ins), out_specs=[hbm] * len(out_shapes), out_shape=out_shapes, name=name,
        scratch_shapes=[pltpu.SemaphoreType.DMA((n_sems,)), pltpu.SemaphoreType.DMA((n_sems,))],
    )(*ins)


def _half_rows(c, rh):
    return pl.ds(pl.multiple_of(c * rh, BF16_ROWS), rh)


def gather_units(units, *, name):
    nu = len(units)
    shapes = []
    for arr, layer_major in units:
        r, cols = arr.shape
        shapes.append(jax.ShapeDtypeStruct((2, N_CHIPS, r // 2, cols) if layer_major else (N_CHIPS, r, cols), arr.dtype))

    def body(*refs):
        in_refs, out_refs, send_sems, recv_sems = refs[:nu], refs[nu:2 * nu], refs[2 * nu], refs[2 * nu + 1]
        x, y, c = _place()
        me_chip = 2 * x + y
        sib = (x, y, 1 - c)
        chips = _other_chips(x, y)

        def copy(k, src, dst, to):
            return pltpu.make_async_remote_copy(src_ref=src, dst_ref=dst, send_sem=send_sems.at[k], recv_sem=recv_sems.at[k],
                                                device_id=to, device_id_type=MESH)

        first, passed, landing = [], [], []
        for u, (arr, layer_major) in enumerate(units):
            rh = arr.shape[0] // 2
            out_ref = out_refs[u]
            slot = (lambda chip, half, o=out_ref: o.at[half, chip]) if layer_major else \
                   (lambda chip, half, o=out_ref, rh=rh: o.at[chip, _half_rows(half, rh), :])
            my_half = in_refs[u].at[_half_rows(c, rh), :]
            for j, (cx, cy) in enumerate(chips):
                k = 6 * u + j
                first.append(copy(k, my_half, slot(me_chip, c), (cx, cy, c)))
                passed.append(copy(k + 3, slot(2 * cx + cy, c), slot(2 * cx + cy, c), sib))
                landing.append((copy(k, my_half, slot(2 * cx + cy, c), sib), copy(k + 3, my_half, slot(2 * cx + cy, 1 - c), sib)))
        for cp in first:
            cp.start()
        for (over_ici, _), fwd in zip(landing, passed):
            over_ici.wait_recv()
            fwd.start()
        for _, from_sibling in landing:
            from_sibling.wait_recv()
        for cp in first + passed:
            cp.wait_send()

    return _hbm_call(body, [a for a, _ in units], shapes, 6 * nu, name)


HBM_SPEC = pl.BlockSpec(memory_space=pltpu.HBM)
SEM_SPEC = pl.BlockSpec(memory_space=pltpu.SEMAPHORE)
ORDERED_EFFECT = pltpu.SideEffectType.DATAFLOW_SIDE_EFFECTING


def _split_start(body, srcs, land_shapes, after, *, name):
    nu = len(srcs)
    lands = [lax.empty(s.shape, s.dtype) for s in land_shapes]

    def whole(*refs):
        body(refs[:nu], refs[nu:2 * nu], refs[2 * nu + 1], refs[2 * nu + 2])
        refs[-1][...] = jnp.zeros((SUBLANES, LANES), F32)

    hbm = lambda a: pltpu.with_memory_space_constraint(a, pltpu.HBM)
    sems = pltpu.SemaphoreType.DMA((nu,))
    res = pl.pallas_call(
        whole, name=name, in_specs=[HBM_SPEC] * (2 * nu) + [pl.BlockSpec(memory_space=pl.ANY)],
        out_shape=[sems, sems] + [pltpu.HBM(a.shape, a.dtype) for a in srcs] + [pltpu.HBM(s.shape, s.dtype) for s in land_shapes]
        + [jax.ShapeDtypeStruct((SUBLANES, LANES), F32)],
        out_specs=[SEM_SPEC, SEM_SPEC] + [HBM_SPEC] * (2 * nu) + [pl.BlockSpec(memory_space=pltpu.VMEM)],
        input_output_aliases={q: 2 + q for q in range(2 * nu)},
        compiler_params=pltpu.CompilerParams(has_side_effects=ORDERED_EFFECT),
    )(*[hbm(a) for a in srcs], *[hbm(a) for a in lands], after)
    return res[0], res[1], res[2:2 + nu], res[2 + nu:2 + 2 * nu], res[-1]


def _split_wait(pending, moved, after, *, name):
    send_sems, recv_sems, srcs, lands, _ = pending
    nu = len(srcs)

    def body(*refs):
        land_refs, ssem, rsem = refs[nu:2 * nu], refs[2 * nu], refs[2 * nu + 1]
        x, y, c = _place()
        for u in range(nu):
            size = moved(land_refs[u])
            cp = pltpu.make_async_remote_copy(src_ref=size, dst_ref=size, send_sem=ssem.at[u], recv_sem=rsem.at[u],
                                              device_id=(x, y, c), device_id_type=MESH)
            cp.wait_send()
            cp.wait_recv()

    res = pl.pallas_call(
        body, name=name, in_specs=[HBM_SPEC] * (2 * nu) + [SEM_SPEC, SEM_SPEC, pl.BlockSpec(memory_space=pl.ANY)],
        out_shape=[pltpu.HBM(a.shape, a.dtype) for a in srcs] + [pltpu.HBM(a.shape, a.dtype) for a in lands],
        out_specs=[HBM_SPEC] * (2 * nu), input_output_aliases={q: q for q in range(2 * nu)},
        compiler_params=pltpu.CompilerParams(has_side_effects=ORDERED_EFFECT),
    )(*srcs, *lands, send_sems, recv_sems, after)
    return res[nu:]


def gather_start(shards, after, *, name):
    def body(src_refs, land_refs, send_sems, recv_sems):
        x, y, c = _place()
        for u, shard in enumerate(shards):
            rows = _half_rows(c, shard.shape[0] // 2)
            for cx, cy in _other_chips(x, y):
                for core in range(2):
                    pltpu.make_async_remote_copy(src_ref=src_refs[u].at[rows, :], dst_ref=land_refs[u].at[2 * x + y, rows, :],
                                                 send_sem=send_sems.at[u], recv_sem=recv_sems.at[u], device_id=(cx, cy, core),
                                                 device_id_type=MESH).start()

    return _split_start(body, shards, [jax.ShapeDtypeStruct((N_CHIPS,) + s.shape, s.dtype) for s in shards], after, name=name)


def gather_wait(pending, after, *, name):
    return _split_wait(pending, lambda land: land.at[pl.ds(0, N_CHIPS - 1)], after, name=name)


def scatter_start(grads, *, name):
    def body(src_refs, land_refs, send_sems, recv_sems):
        x, y, c = _place()
        for u, g in enumerate(grads):
            rh = g.shape[1] // 2
            for r in range(1, N_DEV):
                tx, ty, tc = x ^ ((r >> 2) & 1), y ^ ((r >> 1) & 1), c ^ (r & 1)
                pltpu.make_async_remote_copy(src_ref=src_refs[u].at[2 * tx + ty, _half_rows(tc, rh), :], dst_ref=land_refs[u].at[r - 1],
                                             send_sem=send_sems.at[u], recv_sem=recv_sems.at[u], device_id=(tx, ty, tc),
                                             device_id_type=MESH).start()

    lands = [jax.ShapeDtypeStruct((N_DEV - 1, g.shape[1] // 2, g.shape[2]), g.dtype) for g in grads]
    return _split_start(body, grads, lands, grads[0], name=name)


def scatter_wait(pending, after, *, name):
    return _split_wait(pending, lambda land: land, after, name=name)


def join_units(units, *, name):
    nu = len(units)

    def body(*refs):
        h_refs, out_refs, send_sems, recv_sems = refs[:nu], refs[nu:2 * nu], refs[2 * nu], refs[2 * nu + 1]
        x, y, c = _place()
        cps = [pltpu.make_async_remote_copy(src_ref=h_refs[u], dst_ref=out_refs[u], send_sem=send_sems.at[u], recv_sem=recv_sems.at[u],
                                            device_id=(x, y, 1 - c), device_id_type=MESH) for u in range(nu)]
        for cp in cps:
            cp.start()
        for cp in cps:
            cp.wait()

    return _hbm_call(body, units, [jax.ShapeDtypeStruct(h.shape, h.dtype) for h in units], nu, name)


def _half_tile(rh):
    tr = rh
    for cand in range(BF16_ROWS, min(rh, 512) + 1, BF16_ROWS):
        if rh % cand == 0:
            tr = cand
    return tr


def reduce_rows(g, landed, *, name):
    nc, rows, cols = g.shape
    rh = rows // 2
    tr = _half_tile(rh)
    nbh = rh // tr

    def body(*refs):
        mine = 2 * (2 * lax.axis_index("x") + lax.axis_index("y")) + lax.axis_index("c")
        acc = refs[0][...]
        for q in range(1, 2 * nc):
            acc = jnp.where(mine == q, refs[q][...], acc)
        acc = acc.astype(F32)
        for r in refs[2 * nc:-1]:
            acc = acc + r[...].astype(F32)
        refs[-1][...] = acc

    own = [pl.BlockSpec((None, tr, cols), lambda i, q=q: (q // 2, (q % 2) * nbh + i, 0)) for q in range(2 * nc)]
    got = [pl.BlockSpec((None, tr, cols), lambda i, q=q: (q, i, 0)) for q in range(landed.shape[0])]
    return pl.pallas_call(
        body, grid=(nbh,), in_specs=own + got, out_specs=pl.BlockSpec((tr, cols), lambda i: (i, 0)),
        out_shape=jax.ShapeDtypeStruct((rh, cols), F32), name=name, compiler_params=_params(("parallel",)),
    )(*[g] * (2 * nc), *[landed] * landed.shape[0])


def sum8(g, *, name):
    def body(g_ref, o_ref):
        acc = g_ref[0]
        for d in range(1, N_DEV):
            acc = acc + g_ref[d]
        o_ref[...] = acc

    return pl.pallas_call(body, out_shape=jax.ShapeDtypeStruct(g.shape[1:], F32), name=name)(g)


def _dup_halves(a):
    t = a.shape[0]
    a = a.reshape(t, N_KV_B, HEAD_DIM_B)
    return jnp.concatenate([a, a], axis=-1).reshape(t, N_KV_B * LANES)


def _undup(a):
    t = a.shape[0]
    return a.reshape(t, N_KV_B, LANES)[:, :, :HEAD_DIM_B].reshape(t, N_KV_B * HEAD_DIM_B)


def _lane_pad(v, offset=0):
    return jnp.zeros((1, LANES), F32).at[0, offset:offset + v.shape[0]].set(v)


SHARD_UP = 2 * D_FF // N_CHIPS
SHARD_BIN = (N_HEADS_B + 2 * N_KV_B) * HEAD_DIM_B // N_CHIPS
SHARD_PROJ = D_MODEL // N_CHIPS


def local_step(x, p, tgt, sm, weight, on_grads):
    t = x.shape[0]
    rtm = min(256, t)
    hk = N_HEADS_A * HEAD_DIM_A
    qd_b = N_HEADS_B * HEAD_DIM_B
    kd_b = N_KV_B * HEAD_DIM_B
    gs = {}
    norm = lambda h, w, nm: tile_map(_f_norm, [(h, D_MODEL, 0)], [w], [(D_MODEL, BF16)], tm=rtm, ncol=1, name=nm)[0]

    spec = pl.BlockSpec
    mtm = _tile(D_MODEL, MM_TM_CAP)
    p_bf = p.astype(BF16)
    alog_p = _lane_pad(sm["a_log"][0], N_HEADS_A)
    dtb_p = _lane_pad(sm["a_dt_bias"][0], N_HEADS_A)
    sinks_p = _lane_pad(sm["b_sinks"][0])
    nw = lambda name, i: sm[name][i:i + 1]
    by_chip = lambda kdim, ns: dict(tn=ns, tk=kdim, b_spec=spec((None, kdim, ns), lambda r, j, kk: (j, kk, 0)))
    by_chip_t = lambda ndim, ns: dict(n=ndim, tn=ndim, tk=ns, b_spec=spec((None, ndim, ns), lambda r, j, kk: (kk, j, 0)))
    cache = {}

    def wgt(name, i, after):
        if (name, i) not in cache:
            cache[name, i] = weight(name, i, after)
        return cache[name, i]

    saved = []
    h = x
    hn_next = norm(h, nw("norm_mix", 0), "norm_mix0")
    for i in range(DEPTH):
        s = {"h0": h, "hn": hn_next}
        if i % 2 == 0:
            s["pm"] = mm(s["hn"], wgt("a_w_in", i, h), name="a_in")
            tail = (s["pm"], LANES, 4 * hk // LANES)
            s["c"] = conv_fwd(s["pm"], wgt("a_conv", i, h), name="a_conv")
            s["bg"] = tile_map(_f_betag, [tail], [alog_p, dtb_p], [(LANES, F32)], tm=rtm, ncol=1, name="a_betag")[0]
            s["prep"], s["tms"] = delta_prep(s["c"], s["bg"], name="a_prep")
            s["o"], s["s_in"] = delta_scan(*s["prep"], name="a_scan")
            s["on"] = gnorm_fwd(s["o"], s["pm"], sm["a_norm"], name="a_gnorm")
            h, s["hf"] = mm(s["on"], wgt("a_w_out", i, s["on"]), add=h, norm_w=nw("norm_ffn", i), name="a_out")
        else:
            s["pb"] = mm(s["hn"], wgt("b_w_in", i, s["hn"]), name="b_in", out_dtype=BF16, n=N_CHIPS * SHARD_BIN,
                         **by_chip(D_MODEL, SHARD_BIN))
            s["kd"], s["vd"] = _dup_halves(s["pb"][:, qd_b:qd_b + kd_b]), _dup_halves(s["pb"][:, qd_b + kd_b:])
            s["ao"] = swa_fwd(s["pb"], s["kd"], s["vd"], sinks_p, name="b_att")
            h, s["hf"] = mm(s["ao"], wgt("b_w_out", i, s["ao"]), add=h, norm_w=nw("norm_ffn", i), name="b_out")
        s["h1"] = h
        s["u"] = mm(s["hf"], wgt("f_w_up", i, s["hf"]), name=f"f_up{i}", out_dtype=BF16, n=2 * D_FF, **by_chip(D_MODEL, SHARD_UP))
        s["act"] = conv_act_fwd(s["u"], wgt("f_conv", i, s["hf"]), name=f"f_conv_act{i}")
        h, s["hp"] = mm(s["act"], wgt("f_w_down", i, s["act"]), add=h, norm_w=nw("norm_ple", i), name=f"f_down{i}")
        s["h2"] = h
        s["gl"] = mm(s["hp"], wgt("ple_w_gate", i, s["hp"]), name=f"ple_gate{i}")
        s["pe"] = mm(p_bf[i], wgt("ple_w_proj", i, s["hp"]), name=f"ple_proj{i}", n=D_MODEL, **by_chip(PLE_DIM, SHARD_PROJ))
        rows3 = [(h, D_MODEL, 0), (s["gl"], D_MODEL, 0), (s["pe"], D_MODEL, 0)]
        if i + 1 < DEPTH:
            def mix_norm(hv, g, e, wn):
                hn = hv + _f_ple(g, e)
                return hn, _f_norm(hn, wn)
            h, hn_next = tile_map(mix_norm, rows3, [nw("norm_mix", i + 1)], [(D_MODEL, F32), (D_MODEL, BF16)], tm=rtm, ncol=1,
                                  name=f"ple_mix{i}")
        else:
            h = tile_map(lambda hv, g, e: hv + _f_ple(g, e), rows3, [], [(D_MODEL, F32)], tm=rtm, ncol=1, name=f"ple_mix{i}")[0]
        saved.append(s)

    dh, gnf, loss = loss_head(h, tgt, sm["norm_final"][None, :], name="loss_head")
    gs["norm_final"] = gnf[0]

    g_mix, g_ffn, g_ple, g_conv = ([None] * DEPTH for _ in range(4))
    zero = jnp.zeros((1, 1), F32)
    for i in reversed(range(DEPTH)):
        s, gw = saved[i], {}
        by_rows = lambda g: g.reshape(N_CHIPS, g.shape[0] // N_CHIPS, g.shape[1])
        (dgl, dpe), _ = tile_vjp(_f_ple, [(s["gl"], D_MODEL, 0), (s["pe"], D_MODEL, 0)], [], [(dh, D_MODEL, 0)], n_diff=2,
                                 tm=rtm, ncol=1, name=f"ple_mix_bwd{i}", grad_dtypes=[BF16, BF16])
        gw["ple_w_proj"] = mm(p_bf[i], dpe, ta=True, name=f"ple_proj_dw{i}", out_dtype=BF16, tn=SHARD_PROJ,
                              o_shape=(N_CHIPS, PLE_DIM, SHARD_PROJ), o_spec=spec((None, PLE_DIM, SHARD_PROJ), lambda r, j, kk: (j, r, 0)))
        gw["ple_w_gate"] = by_rows(mm(s["hp"], dgl, ta=True, name=f"ple_gate_dw{i}", out_dtype=BF16))
        fused = dict(tb=True, tm_cap=MM_TM_CAP // 2)
        dh, g_ple[i] = mm(dgl, cache["ple_w_gate", i], name=f"ple_gate_dx{i}", norm_grad=(s["h2"], nw("norm_ple", i) + zero, dh), **fused)

        dact = mm(dh, cache["f_w_down", i], tb=True, name=f"f_down_dx{i}")
        gw["f_w_down"] = by_rows(mm(s["act"], dh, ta=True, name=f"f_down_dw{i}", out_dtype=BF16, tm_cap=D_FF // 2))
        du_halves = conv_act_bwd(s["u"], dact, cache["f_conv", i], name=f"f_conv_act_bwd{i}")
        g_conv[i] = jnp.concatenate(du_halves[2:], axis=1)
        dhf = g_up = None
        for half, du in enumerate(du_halves[:2]):
            c0 = half * (N_CHIPS // 2)
            g_up = mm(s["hf"], du, ta=True, name=f"f_up_dw{i}_{half}", out_dtype=BF16, tn=SHARD_UP, into=g_up,
                      o_shape=(N_CHIPS, D_MODEL, SHARD_UP), o_spec=spec((None, mtm, SHARD_UP), lambda r, j, kk, c0=c0: (c0 + j, r, 0)))
            last = dict(norm_grad=(s["h1"], nw("norm_ffn", i), dh), **fused) if half else dict(tb=True)
            dhf = mm(du, cache["f_w_up", i], name=f"f_up_dx{i}_{half}", n=D_MODEL, tn=D_MODEL, tk=SHARD_UP, add=dhf,
                     b_spec=spec((None, D_MODEL, SHARD_UP), lambda r, j, kk, c0=c0: (c0 + kk, j, 0)), **last)
        gw["f_w_up"] = g_up
        dh, g_ffn[i] = dhf
        token, gw = on_grads(i, "ffn", gw), {}
        w_out = cache["a_w_out" if i % 2 == 0 else "b_w_out", i]
        if token is not None:
            w_out = w_out + token[:1, :1].astype(BF16)

        if i % 2 == 0:
            don = mm(dh, w_out, tb=True, name="a_out_dx")
            gw["a_w_out"] = by_rows(mm(s["on"], dh, ta=True, name="a_out_dw", out_dtype=BF16))
            do, dz, gs["a_norm"] = gnorm_bwd(s["o"], s["pm"], sm["a_norm"], don, name="a_gnorm_bwd")
            dprep = delta_scan_bwd(do, *s["prep"], s["s_in"], name="a_scan_bwd")
            dc, dbg = delta_prep_bwd(s["c"], s["bg"], s["tms"], dprep, name="a_prep_bwd")
            (dpt,), (galog, gdtb) = tile_vjp(_f_betag, [(s["pm"], LANES, 4 * hk // LANES)], [alog_p, dtb_p], [(dbg, LANES, 0)], n_diff=1,
                                             tm=rtm, ncol=1, name="a_betag_bwd", grad_dtypes=[BF16])
            gs["a_log"] = galog[:, N_HEADS_A:2 * N_HEADS_A]
            gs["a_dt_bias"] = gdtb[:, N_HEADS_A:2 * N_HEADS_A]
            dqkv, gs["a_conv"] = conv_bwd(dc, s["pm"], cache["a_conv", i], name="a_conv_bwd")
            dpm = jnp.concatenate([dqkv, dz, dpt], axis=1)
            g_in = mm(s["hn"], dpm, ta=True, name="a_in_dw", out_dtype=BF16)[:, :4 * hk + 2 * N_HEADS_A]
            gw["a_w_in"] = g_in.reshape(D_MODEL, N_CHIPS, g_in.shape[1] // N_CHIPS).transpose(1, 0, 2)
            dh, g_mix[i] = mm(dpm, cache["a_w_in", i], name="a_in_dx", norm_grad=(s["h0"], nw("norm_mix", i), dh), **fused)
        else:
            dao = mm(dh, w_out, tb=True, name="b_out_dx")
            gw["b_w_out"] = by_rows(mm(s["ao"], dh, ta=True, name="b_out_dw", out_dtype=BF16))
            dq, dkd, dvd, gsk = swa_bwd(dao, s["pb"], s["kd"], s["vd"], sinks_p, name="b_att_bwd")
            gs["b_sinks"] = gsk[:, :N_HEADS_B]
            dpb = jnp.concatenate([dq, _undup(dkd), _undup(dvd)], axis=1)
            gw["b_w_in"] = mm(s["hn"], dpb, ta=True, name="b_in_dw", out_dtype=BF16, tn=SHARD_BIN,
                              o_shape=(N_CHIPS, D_MODEL, SHARD_BIN), o_spec=spec((None, mtm, SHARD_BIN), lambda r, j, kk: (j, r, 0)))
            dh, g_mix[i] = mm(dpb, cache["b_w_in", i], name="b_in_dx", norm_grad=(s["h0"], nw("norm_mix", i), dh), **fused,
                              **by_chip_t(D_MODEL, SHARD_BIN))
        token = on_grads(i, "mix", gw)
        if token is not None:
            zero = token[:1, :1]

    gs["norm_mix"], gs["norm_ffn"], gs["norm_ple"] = (jnp.concatenate(g, axis=0) for g in (g_mix, g_ffn, g_ple))
    gs["f_conv"] = jnp.stack(g_conv)
    return loss, dh, gs


BIG = ["a_w_in", "a_w_out", "b_w_in", "b_w_out", "f_w_up", "f_w_down", "ple_w_proj", "ple_w_gate"]
LAYERED = {"f_w_up", "f_w_down", "ple_w_proj", "ple_w_gate"}
BY_CHIP = {"b_w_in", "f_w_up", "ple_w_proj"}
LAYER_UNITS = [[("a_w_in", 0), ("a_w_out", 0)] + [(n, 0) for n in sorted(LAYERED)],
               [("b_w_in", 1), ("b_w_out", 1)] + [(n, 1) for n in sorted(LAYERED)]]
CONVS = ["a_conv", "f_conv"]
SMALL = ["norm_mix", "norm_ffn", "norm_ple", "norm_final", "a_log", "a_dt_bias", "a_norm", "b_sinks"]
SMALL_ROWS = 8
CONV_ROWS = 16
CONV_GRAD_ROWS = 48


def _pack_rows(arrs, rows, dtype):
    flat = jnp.concatenate([a.reshape(-1).astype(dtype) for a in arrs])
    return jnp.pad(flat, (0, rows * PACK_COLS - flat.shape[0])).reshape(rows, PACK_COLS)


def _unpack(flat, shapes):
    out, off = [], 0
    for shp in shapes:
        n = math.prod(shp)
        out.append(flat[off:off + n].reshape(shp))
        off += n
    return out


def _pack_small(d, loss=None):
    tail = jnp.concatenate([d["a_log"].reshape(-1), d["a_dt_bias"].reshape(-1), d["a_norm"].reshape(-1), d["b_sinks"].reshape(-1)])
    if loss is not None:
        tail = jnp.concatenate([tail, loss.reshape(-1)[:1]])
    tail = jnp.pad(tail, (0, PACK_COLS - tail.shape[0]))
    return jnp.concatenate([d["norm_mix"], d["norm_ffn"], d["norm_ple"], d["norm_final"][None, :], tail[None, :]], axis=0)


def _unpack_small(a, like):
    out = {"norm_mix": a[0:2], "norm_ffn": a[2:4], "norm_ple": a[4:6], "norm_final": a[6]}
    off = 0
    for nm in ("a_log", "a_dt_bias", "a_norm", "b_sinks"):
        n = like[nm].size
        out[nm] = a[7, off:off + n].reshape(like[nm].shape)
        off += n
    return out, a[7, off]


def _as2d(a):
    return a.reshape(-1, a.shape[-1])


def kernel(x, p, norm_mix, norm_ffn, norm_ple, norm_final, a_w_in, a_conv, a_log, a_dt_bias, a_norm, a_w_out, b_w_in, b_sinks, b_w_out, f_w_up, f_conv, f_w_down, ple_w_proj, ple_w_gate, loss_target, m_norm_mix, m_norm_ffn, m_norm_ple, m_norm_final, m_a_w_in, m_a_conv, m_a_log, m_a_dt_bias, m_a_norm, m_a_w_out, m_b_w_in, m_b_sinks, m_b_w_out, m_f_w_up, m_f_conv, m_f_w_down, m_ple_w_proj, m_ple_w_gate, v_norm_mix, v_norm_ffn, v_norm_ple, v_norm_final, v_a_w_in, v_a_conv, v_a_log, v_a_dt_bias, v_a_norm, v_a_w_out, v_b_w_in, v_b_sinks, v_b_w_out, v_f_w_up, v_f_conv, v_f_w_down, v_ple_w_proj, v_ple_w_gate):
    w = dict(norm_mix=norm_mix, norm_ffn=norm_ffn, norm_ple=norm_ple, norm_final=norm_final, a_w_in=a_w_in, a_conv=a_conv,
             a_log=a_log, a_dt_bias=a_dt_bias, a_norm=a_norm, a_w_out=a_w_out, b_w_in=b_w_in, b_sinks=b_sinks, b_w_out=b_w_out,
             f_w_up=f_w_up, f_conv=f_conv, f_w_down=f_w_down, ple_w_proj=ple_w_proj, ple_w_gate=ple_w_gate)
    m = dict(norm_mix=m_norm_mix, norm_ffn=m_norm_ffn, norm_ple=m_norm_ple, norm_final=m_norm_final, a_w_in=m_a_w_in,
             a_conv=m_a_conv, a_log=m_a_log, a_dt_bias=m_a_dt_bias, a_norm=m_a_norm, a_w_out=m_a_w_out, b_w_in=m_b_w_in,
             b_sinks=m_b_sinks, b_w_out=m_b_w_out, f_w_up=m_f_w_up, f_conv=m_f_conv, f_w_down=m_f_w_down,
             ple_w_proj=m_ple_w_proj, ple_w_gate=m_ple_w_gate)
    v = dict(norm_mix=v_norm_mix, norm_ffn=v_norm_ffn, norm_ple=v_norm_ple, norm_final=v_norm_final, a_w_in=v_a_w_in,
             a_conv=v_a_conv, a_log=v_a_log, a_dt_bias=v_a_dt_bias, a_norm=v_a_norm, a_w_out=v_a_w_out, b_w_in=v_b_w_in,
             b_sinks=v_b_sinks, b_w_out=v_b_w_out, f_w_up=v_f_w_up, f_conv=v_f_conv, f_w_down=v_f_w_down,
             ple_w_proj=v_ple_w_proj, ple_w_gate=v_ple_w_gate)
    xc, yc, cc = _place()
    my_chip = 2 * xc + yc

    shard = {(n, i): w[n][i if n in LAYERED else 0].astype(BF16) for n, i in LAYER_UNITS[0] + LAYER_UNITS[1]}
    first = shard["a_w_in", 0]
    (ga,) = gather_units([(first, False)], name="gather_first")
    ga = lax.dynamic_update_index_in_dim(ga, first, my_chip, 0)
    a_in = jnp.concatenate([ga[j] for j in range(N_CHIPS)], axis=1)
    n_main = 4 * N_HEADS_A * HEAD_DIM_A
    conv_shapes = [w[n].shape for n in CONVS]
    convs = allgather8(_pack_rows([w[n] for n in CONVS], CONV_ROWS, F32), name="gather_convs")
    conv_parts = [_unpack(convs[2 * j].reshape(-1), conv_shapes) for j in range(N_CHIPS)]
    a_conv_full, f_conv_full = (jnp.concatenate([conv_parts[j][q] for j in range(N_CHIPS)], axis=2) for q in range(2))
    ready = {("a_w_in", 0): jnp.pad(a_in, ((0, 0), (0, n_main + LANES - a_in.shape[1]))), ("a_conv", 0): a_conv_full[0], ("f_conv", 0): f_conv_full[0], ("f_conv", 1): f_conv_full[1]}
    later = [[k for k in units if k != ("a_w_in", 0)] for units in LAYER_UNITS]
    pending, after = [], ga
    for layer, keys in enumerate(later):
        pending.append(gather_start([shard[k] for k in keys], after, name=f"gather_start{layer}"))
        after = pending[-1][4]
    sm = {n: w[n] for n in SMALL}
    sm["norm_mix"] = sm["norm_mix"] + after[:1, :1]

    def weight(name, layer, act):
        if (name, layer) not in ready:
            landed = gather_wait(pending[layer], act, name=f"gather_wait{layer}")
            for k, g in zip(later[layer], landed):
                g = lax.dynamic_update_index_in_dim(g, shard[k], my_chip, 0)
                ready[k] = g if k[0] in BY_CHIP else g.reshape(N_CHIPS * g.shape[1], g.shape[2])
        return ready[name, layer]

    pairs, scattered, started = {}, {}, []

    def on_grads(layer, part, gw):
        keys = [k for k in LAYER_UNITS[layer] if (k[0] in LAYERED) == (part == "ffn")]
        pairs.update({k: gw[k[0]] for k in keys})
        started.append((keys, scatter_start([pairs[k] for k in keys], name=f"rs_scatter_start_{part}{layer}"), f"{part}{layer}"))
        return started[-1][1][4]

    loss, grad_x, gs = local_step(x[0], p[:, 0], loss_target[0], sm, weight, on_grads)

    grads, delta, new_m, new_v, g_unit = {}, {}, {}, {}, {}

    def finish(keys, tag):
        halves = [reduce_rows(pairs[k], scattered[k], name=f"rs_reduce_{k[0]}{k[1]}") for k in keys]
        g_unit.update(zip(keys, zip(halves, join_units(halves, name=f"rs_join_{tag}"))))
        for n in BIG:
            mine = [(n, i) for i in range(DEPTH) if (n, i) in LAYER_UNITS[i]]
            if n not in delta and all(k in g_unit for k in mine):
                g_layers = [g_unit[k] for k in mine]
                shape3 = (len(g_layers), 2 * g_layers[0][0].shape[0], g_layers[0][0].shape[1])
                res = adamw_layers(w[n].reshape(shape3), g_layers, m[n].reshape(shape3), v[n].reshape(shape3), name=f"adamw_{n}")
                delta[n], new_m[n], new_v[n], grads[n] = (r.reshape(w[n].shape) for r in res)

    last_keys, last_pending, last_tag = started[-1]
    for keys, pend, tag in started[:-1]:
        scattered.update(zip(keys, scatter_wait(pend, last_pending[4], name=f"rs_scatter_wait_{tag}")))
    finish([k for keys, _, _ in started[:-1] for k in keys], "first")

    conv_grads = _pack_rows([gs[n] for n in CONVS], CONV_GRAD_ROWS, F32)
    small_sum = sum8(allgather8(jnp.concatenate([_pack_small(gs, loss), conv_grads]), name="gather_small"), name="sum_small")
    g_sm, loss_sum = _unpack_small(small_sum[:SMALL_ROWS], sm)

    scattered.update(zip(last_keys, scatter_wait(last_pending, small_sum, name=f"rs_scatter_wait_{last_tag}")))
    finish(last_keys, "last")

    for n, full in zip(CONVS, _unpack(small_sum[SMALL_ROWS:].reshape(-1), [gs[n].shape for n in CONVS])):
        g2 = _as2d(lax.dynamic_slice_in_dim(full, my_chip * w[n].shape[-1], w[n].shape[-1], axis=full.ndim - 1))
        d2, m2, v2 = adamw(_as2d(w[n]), g2, _as2d(m[n]), _as2d(v[n]), name=f"adamw_{n}")
        grads[n], delta[n], new_m[n], new_v[n] = (r.reshape(w[n].shape) for r in (g2, d2, m2, v2))
    pk = lambda d: _pack_small(d)
    d2, m2, v2 = adamw(pk(sm), pk(g_sm), pk({n: m[n] for n in SMALL}), pk({n: v[n] for n in SMALL}), name="adamw_small")
    for src, dst in ((d2, delta), (m2, new_m), (v2, new_v)):
        dst.update(_unpack_small(src, sm)[0])
    grads.update(g_sm)

    order = ["norm_mix", "norm_ffn", "norm_ple", "norm_final", "a_w_in", "a_conv", "a_log", "a_dt_bias", "a_norm", "a_w_out",
             "b_w_in", "b_sinks", "b_w_out", "f_w_up", "f_conv", "f_w_down", "ple_w_proj", "ple_w_gate"]
    return (loss_sum, grad_x[None], *[grads[n] for n in order], *[delta[n] for n in order],
            *[new_m[n] for n in order], *[new_v[n] for n in order])
```

```python
import functools
import math

import jax
import jax.numpy as jnp
from jax import lax
from jax.experimental import pallas as pl
from jax.experimental.pallas import tpu as pltpu

F32 = jnp.float32
BF16 = jnp.bfloat16
MESH = pl.DeviceIdType.MESH

D_MODEL = 1024
N_HEADS_A = 8
HEAD_DIM_A = 128
CONV_A = 4
N_HEADS_B = 16
N_KV_B = 4
HEAD_DIM_B = 64
WINDOW = 128
D_FF = 2816
FFN_CONV = 3
PLE_DIM = 256
EPS = 1e-6
DEPTH = 2

ADAM_LR = 0.001
ADAM_B1 = 0.9
ADAM_B2 = 0.999
ADAM_EPS = 1e-08
ADAM_WD = 0.01
ADAM_STEP = 10

LANES = 128
SUBLANES = 8
BF16_ROWS = 16
CHUNK = 128
VMEM_LIMIT = 56 * 1024 * 1024
NEG = -1e30
N_CHIPS = 4
N_DEV = 8
PACK_COLS = 1024


def _params(sem=None):
    return pltpu.CompilerParams(dimension_semantics=sem, vmem_limit_bytes=VMEM_LIMIT)


def _tile(dim, cap):
    if dim % LANES:
        return dim
    best = LANES
    for t in range(LANES, min(dim, cap) + 1, LANES):
        if dim % t == 0:
            best = t
    return best


def _dot(a, b, dims=(((1,), (0,)), ((), ())), precision=None):
    return lax.dot_general(a, b, dims, precision=precision, preferred_element_type=F32)


NN = (((1,), (0,)), ((), ()))
NT = (((1,), (1,)), ((), ()))
TN = (((0,), (0,)), ((), ()))


MM_TM_CAP = 1024
MM_TK_CAP_TOKENS = 2048


def mm(a, b, *, name, ta=False, tb=False, out_dtype=F32, add=None, norm_w=None, norm_grad=None, tm_cap=MM_TM_CAP, tn_cap=1408,
       tk_cap=1408, n=None, tn=None, tk=None, b_spec=None, o_spec=None, o_shape=None, into=None):
    m, k = (a.shape[1], a.shape[0]) if ta else a.shape
    if b_spec is None:
        n = b.shape[0] if tb else b.shape[1]
        assert (b.shape[1] if tb else b.shape[0]) == k, (a.shape, b.shape, ta, tb)
    tm, tn, tk = _tile(m, tm_cap), tn or _tile(n, tn_cap), tk or _tile(k, MM_TK_CAP_TOKENS if ta else tk_cap)
    assert n % tn == 0 and k % tk == 0, (n, tn, k, tk)
    nk = k // tk
    dims = (((0 if ta else 1,), (1 if tb else 0,)), ((), ()))
    has_add, has_norm, has_grad = add is not None, norm_w is not None, norm_grad is not None
    assert not (has_norm or has_grad) or (tn == n and o_spec is None), "the norm epilogues need whole rows"
    n_in = 2 + has_add + has_norm + 3 * has_grad + (into is not None)

    def body(*refs):
        a_ref, b_ref = refs[0], refs[1]
        add_ref = refs[2] if has_add else None
        o_ref = refs[n_in]
        part = _dot(a_ref[...].astype(BF16), b_ref[...].astype(BF16), dims)
        first = pl.program_id(0) == 0

        def finish(r):
            if has_add:
                r = r + add_ref[...].astype(F32)
            if has_grad:
                h_ref, w_ref, prev_ref = refs[2 + has_add:5 + has_add]
                _, vjp = jax.vjp(_f_norm, h_ref[...], w_ref[...])
                r, dw = vjp(r)
                r = r + prev_ref[...]

                @pl.when(first)
                def _():
                    refs[n_in + 1][...] = dw

                @pl.when(jnp.logical_not(first))
                def _():
                    refs[n_in + 1][...] += dw
            o_ref[...] = r.astype(o_ref.dtype)
            if has_norm:
                refs[n_in + 1][...] = _f_norm(r, refs[2 + has_add][...]).astype(BF16)

        if nk == 1:
            finish(part)
            return
        acc = refs[-1]
        kk = pl.program_id(2)

        @pl.when(kk == 0)
        def _():
            acc[...] = part

        @pl.when(kk > 0)
        def _():
            acc[...] += part

        @pl.when(kk == nk - 1)
        def _():
            finish(acc[...])

    a_spec = pl.BlockSpec((tk, tm), lambda i, j, kk: (kk, i)) if ta else pl.BlockSpec((tm, tk), lambda i, j, kk: (i, kk))
    if b_spec is None:
        b_spec = pl.BlockSpec((tn, tk), lambda i, j, kk: (j, kk)) if tb else pl.BlockSpec((tk, tn), lambda i, j, kk: (kk, j))
    plain_o = pl.BlockSpec((tm, tn), lambda i, j, kk: (i, j))
    if o_spec is None:
        o_spec, o_shape = plain_o, (m, n)
    in_specs = [a_spec, b_spec] + ([plain_o] if has_add else [])
    args = (a, b) + ((add,) if has_add else ())
    out_specs, out_shapes = o_spec, jax.ShapeDtypeStruct(tuple(o_shape), out_dtype)
    one_row = pl.BlockSpec((1, n), lambda i, j, kk: (0, 0))
    if has_norm:
        in_specs.append(one_row)
        args += (norm_w,)
        out_specs, out_shapes = [o_spec, plain_o], [out_shapes, jax.ShapeDtypeStruct((m, n), BF16)]
    if has_grad:
        assert not has_norm
        in_specs += [plain_o, one_row, plain_o]
        args += tuple(norm_grad)
        out_specs, out_shapes = [o_spec, one_row], [out_shapes, jax.ShapeDtypeStruct((1, n), F32)]
    aliases = {}
    if into is not None:
        assert into.shape == tuple(o_shape) and into.dtype == out_dtype, (into.shape, o_shape)
        in_specs.append(pl.BlockSpec(memory_space=pl.ANY))
        args += (into,)
        aliases = {n_in - 1: 0}
    return pl.pallas_call(
        body, grid=(m // tm, n // tn, nk), in_specs=in_specs, out_specs=out_specs,
        out_shape=out_shapes, name=name, input_output_aliases=aliases,
        scratch_shapes=[pltpu.VMEM((tm, tn), F32)] if nk > 1 else [],
        compiler_params=_params(("arbitrary" if has_grad else "parallel", "parallel", "arbitrary")),
    )(*args)


def _row_spec(tm, cw, coff):
    return pl.BlockSpec((tm, cw), lambda i, j: (i, j + coff))


def _full_spec(shape):
    return pl.BlockSpec(shape, lambda i, j: (0,) * len(shape))


def tile_map(fn, rows, params, outs, *, tm, ncol, name):
    t = rows[0][0].shape[0]
    nin = len(rows) + len(params)

    def body(*refs):
        res = fn(*[r[...] for r in refs[:nin]])
        res = res if isinstance(res, (tuple, list)) else (res,)
        for o_ref, r in zip(refs[nin:], res):
            o_ref[...] = r.astype(o_ref.dtype)

    in_specs = [_row_spec(tm, cw, coff) for (_, cw, coff) in rows] + [_full_spec(p.shape) for p in params]
    res = pl.pallas_call(
        body, grid=(t // tm, ncol), in_specs=in_specs,
        out_specs=[_row_spec(tm, cw, 0) for (cw, _) in outs],
        out_shape=[jax.ShapeDtypeStruct((t, cw * ncol), dt) for (cw, dt) in outs], name=name,
        compiler_params=_params(("parallel", "parallel")),
    )(*[r[0] for r in rows], *params)
    return res


def tile_vjp(fn, rows, params, cts, *, n_diff, tm, ncol, name, grad_dtypes=None):
    t = rows[0][0].shape[0]
    nr, npar, nct = len(rows), len(params), len(cts)

    def body(*refs):
        vals = [r[...] for r in refs[:nr + npar + nct]]
        diff, rest, pars = vals[:n_diff], vals[n_diff:nr], vals[nr:nr + npar]
        ctv = vals[nr + npar:nr + npar + nct]
        outs_ref = refs[nr + npar + nct:]

        def f(*a):
            res = fn(*a[:n_diff], *rest, *a[n_diff:])
            return tuple(res) if isinstance(res, (tuple, list)) else (res,)

        primal, vjp = jax.vjp(f, *[d.astype(F32) for d in diff], *pars)
        grads = vjp(tuple(c.astype(o.dtype) for c, o in zip(ctv, primal)))
        for q in range(n_diff):
            outs_ref[q][...] = grads[q].astype(outs_ref[q].dtype)
        first = (pl.program_id(0) == 0) & (pl.program_id(1) == 0)
        for q in range(npar):
            o_ref, g = outs_ref[n_diff + q], grads[n_diff + q]

            @pl.when(first)
            def _(o_ref=o_ref, g=g):
                o_ref[...] = g

            @pl.when(jnp.logical_not(first))
            def _(o_ref=o_ref, g=g):
                o_ref[...] += g

    in_specs = [_row_spec(tm, cw, coff) for (_, cw, coff) in rows] + [_full_spec(p.shape) for p in params]
    in_specs += [_row_spec(tm, cw, coff) for (_, cw, coff) in cts]
    args = [r[0] for r in rows] + list(params) + [c[0] for c in cts]
    out_specs = [_row_spec(tm, rows[q][1], 0) for q in range(n_diff)] + [_full_spec(p.shape) for p in params]
    grad_dtypes = grad_dtypes or [F32] * n_diff
    out_shape = [jax.ShapeDtypeStruct((t, rows[q][1] * ncol), grad_dtypes[q]) for q in range(n_diff)]
    out_shape += [jax.ShapeDtypeStruct(p.shape, F32) for p in params]
    res = pl.pallas_call(
        body, grid=(t // tm, ncol), in_specs=in_specs, out_specs=out_specs, out_shape=out_shape, name=name,
        compiler_params=_params(("arbitrary", "arbitrary")),
    )(*args)
    return res[:n_diff], res[n_diff:]


def _silu(x):
    return x * jax.nn.sigmoid(x)


def _f_norm(h, w):
    return h * lax.rsqrt(jnp.mean(h * h, axis=-1, keepdims=True) + EPS) * w


def _f_gnorm(o, z, w):
    return _f_norm(o, w) * _silu(z)


def _f_act(gate, val):
    return _silu(gate) * val


def _f_ple(gl, pe):
    return jax.nn.sigmoid(gl) * pe


def _f_betag(pt, alog, dtb):
    lane = lax.broadcasted_iota(jnp.int32, (1, LANES), 1)
    z = pt + dtb
    softplus = jnp.maximum(z, 0.0) + jnp.log(1.0 + jnp.exp(-jnp.abs(z)))
    g = -jnp.exp(alog) * softplus
    return jnp.where(lane < N_HEADS_A, jax.nn.sigmoid(pt), jnp.where(lane < 2 * N_HEADS_A, g, 0.0))


CONV_TM = 256
CONV_CW = 1024


def _shift_down(x, prev, s, row):
    rp = jnp.tile(pltpu.roll(prev, s, 0), (x.shape[0] // SUBLANES, 1))
    return jnp.where(row < s, rp, pltpu.roll(x, s, 0))


def _shift_up(x, nxt, s, row):
    tm = x.shape[0]
    rn = jnp.tile(pltpu.roll(nxt, SUBLANES - s, 0), (tm // SUBLANES, 1))
    return jnp.where(row >= tm - s, rn, pltpu.roll(x, tm - s, 0))


def _conv_taps(x, prev, w_ref, cols, row):
    k = w_ref.shape[0]
    y = x * w_ref[pl.ds(k - 1, 1), cols]
    for s in range(1, k):
        y = y + _shift_down(x, prev, s, row) * w_ref[pl.ds(k - 1 - s, 1), cols]
    return y


def _lane_chunks(cw):
    return [slice(cb * LANES, (cb + 1) * LANES) for cb in range(cw // LANES)]


def conv_fwd(x, w, *, name):
    t = x.shape[0]
    k, c = w.shape
    tm, cw = min(CONV_TM, t), CONV_CW
    nb8 = tm // SUBLANES

    def body(x_ref, p_ref, w_ref, o_ref):
        first = pl.program_id(1) == 0
        row = lax.broadcasted_iota(jnp.int32, (tm, LANES), 0)
        for cols in _lane_chunks(cw):
            o_ref[:, cols] = _conv_taps(x_ref[:, cols], jnp.where(first, 0.0, p_ref[:, cols]), w_ref, cols, row)

    return pl.pallas_call(
        body, grid=(c // cw, t // tm),
        in_specs=[pl.BlockSpec((tm, cw), lambda j, i: (i, j)),
                  pl.BlockSpec((SUBLANES, cw), lambda j, i: (jnp.maximum(i * nb8 - 1, 0), j)),
                  pl.BlockSpec((k, cw), lambda j, i: (0, j))],
        out_specs=pl.BlockSpec((tm, cw), lambda j, i: (i, j)),
        out_shape=jax.ShapeDtypeStruct((t, c), F32), name=name,
        compiler_params=_params(("parallel", "parallel")),
    )(x, x, w)


def conv_bwd(dy, x, w, *, name):
    t = x.shape[0]
    k, c = w.shape
    tm, cw = min(CONV_TM, t), CONV_CW
    nb8 = tm // SUBLANES
    ni = t // tm

    def body(dy_ref, dn_ref, x_ref, p_ref, w_ref, dx_ref, dw_ref):
        i = pl.program_id(1)
        first, last = i == 0, i == ni - 1
        row = lax.broadcasted_iota(jnp.int32, (tm, LANES), 0)
        for cols in _lane_chunks(cw):
            dyv, xv = dy_ref[:, cols], x_ref[:, cols]
            nxt = jnp.where(last, 0.0, dn_ref[:, cols])
            prev = jnp.where(first, 0.0, p_ref[:, cols])
            dx = dyv * w_ref[pl.ds(k - 1, 1), cols]
            dws = [jnp.sum(dyv * xv, axis=0, keepdims=True)]
            for s in range(1, k):
                dx = dx + _shift_up(dyv, nxt, s, row) * w_ref[pl.ds(k - 1 - s, 1), cols]
                dws.append(jnp.sum(dyv * _shift_down(xv, prev, s, row), axis=0, keepdims=True))
            dx_ref[:, cols] = dx.astype(dx_ref.dtype)
            for s in range(k):
                @pl.when(first)
                def _(s=s, dws=dws, cols=cols):
                    dw_ref[pl.ds(k - 1 - s, 1), cols] = dws[s]

                @pl.when(jnp.logical_not(first))
                def _(s=s, dws=dws, cols=cols):
                    dw_ref[pl.ds(k - 1 - s, 1), cols] += dws[s]

    return pl.pallas_call(
        body, grid=(c // cw, ni),
        in_specs=[pl.BlockSpec((tm, cw), lambda j, i: (i, j)),
                  pl.BlockSpec((SUBLANES, cw), lambda j, i: (jnp.minimum((i + 1) * nb8, t // SUBLANES - 1), j)),
                  pl.BlockSpec((tm, cw), lambda j, i: (i, j)),
                  pl.BlockSpec((SUBLANES, cw), lambda j, i: (jnp.maximum(i * nb8 - 1, 0), j)),
                  pl.BlockSpec((k, cw), lambda j, i: (0, j))],
        out_specs=[pl.BlockSpec((tm, cw), lambda j, i: (i, j)), pl.BlockSpec((k, cw), lambda j, i: (0, j))],
        out_shape=[jax.ShapeDtypeStruct((t, c), BF16), jax.ShapeDtypeStruct((k, c), F32)], name=name,
        compiler_params=_params(("parallel", "arbitrary")),
    )(dy, dy, x, x, w)


FFN_TM = 128
FFN_CW = D_FF // 2


def _ffn_specs(t, tm, cw, k):
    ncol = D_FF // cw
    cur = lambda off: pl.BlockSpec((tm, cw), lambda j, i: (i, j + off))
    prev = lambda off, hr: pl.BlockSpec((hr, cw), lambda j, i: (jnp.maximum(i * (tm // hr) - 1, 0), j + off))
    nxt = lambda off, hr: pl.BlockSpec((hr, cw), lambda j, i: (jnp.minimum((i + 1) * (tm // hr), t // hr - 1), j + off))
    taps = lambda off: pl.BlockSpec((k, cw), lambda j, i: (0, j + off))
    return cur, prev, nxt, taps, ncol


def _rows_before(ref, cols, first):
    return jnp.where(first, 0.0, ref[ref.shape[0] - SUBLANES:, cols].astype(F32))


def conv_act_fwd(u, w, *, name):
    t, k = u.shape[0], w.shape[0]
    tm, cw = min(FFN_TM, t), FFN_CW
    cur, prev, _, taps, ncol = _ffn_specs(t, tm, cw, k)

    def body(ug_ref, pg_ref, uv_ref, pv_ref, wg_ref, wv_ref, o_ref):
        first = pl.program_id(1) == 0
        row = lax.broadcasted_iota(jnp.int32, (tm, LANES), 0)
        for cb in range(cw // LANES):
            cols = slice(cb * LANES, (cb + 1) * LANES)
            cg = _conv_taps(ug_ref[:, cols].astype(F32), _rows_before(pg_ref, cols, first), wg_ref, cols, row)
            cv = _conv_taps(uv_ref[:, cols].astype(F32), _rows_before(pv_ref, cols, first), wv_ref, cols, row)
            o_ref[:, cols] = _f_act(cg, cv).astype(o_ref.dtype)

    return pl.pallas_call(
        body, grid=(ncol, t // tm),
        in_specs=[cur(0), prev(0, BF16_ROWS), cur(ncol), prev(ncol, BF16_ROWS), taps(0), taps(ncol)],
        out_specs=cur(0), out_shape=jax.ShapeDtypeStruct((t, D_FF), BF16), name=name,
        compiler_params=_params(("parallel", "parallel")),
    )(u, u, u, u, w, w)


def conv_act_bwd(u, dact, w, *, name):
    t, k = u.shape[0], w.shape[0]
    tm, cw = min(FFN_TM, t), FFN_CW
    cur, prev, nxt, taps, ncol = _ffn_specs(t, tm, cw, k)
    ni = t // tm

    def body(ug_ref, pg_ref, ng_ref, uv_ref, pv_ref, nv_ref, d_ref, dn_ref, wg_ref, wv_ref, dg_ref, dv_ref, dwg_ref, dwv_ref):
        i = pl.program_id(1)
        first, last = i == 0, i == ni - 1
        row = lax.broadcasted_iota(jnp.int32, (tm, LANES), 0)
        row8 = lax.broadcasted_iota(jnp.int32, (SUBLANES, LANES), 0)
        for cb in range(cw // LANES):
            cols = slice(cb * LANES, (cb + 1) * LANES)
            ug, uv = ug_ref[:, cols].astype(F32), uv_ref[:, cols].astype(F32)
            pg, pv = _rows_before(pg_ref, cols, first), _rows_before(pv_ref, cols, first)
            sg = [ug] + [_shift_down(ug, pg, s, row) for s in range(1, k)]
            sv = [uv] + [_shift_down(uv, pv, s, row) for s in range(1, k)]
            taps = lambda xs, w_ref: sum(xs[s] * w_ref[pl.ds(k - 1 - s, 1), cols] for s in range(k))
            _, vjp = jax.vjp(_f_act, taps(sg, wg_ref), taps(sv, wv_ref))
            dcg, dcv = vjp(d_ref[:, cols])
            after = lambda ref: ref[:SUBLANES, cols].astype(F32)
            _, vjp_n = jax.vjp(_f_act, _conv_taps(after(ng_ref), ug[tm - SUBLANES:], wg_ref, cols, row8),
                               _conv_taps(after(nv_ref), uv[tm - SUBLANES:], wv_ref, cols, row8))
            dcgn, dcvn = vjp_n(jnp.where(last, 0.0, dn_ref[:, cols]))
            for dc, dcn, xs, w_ref, dx_ref, dw_ref in ((dcg, dcgn, sg, wg_ref, dg_ref, dwg_ref),
                                                       (dcv, dcvn, sv, wv_ref, dv_ref, dwv_ref)):
                dx = dc * w_ref[pl.ds(k - 1, 1), cols]
                dws = [jnp.sum(dc * xs[0], axis=0, keepdims=True)]
                for s in range(1, k):
                    dx = dx + _shift_up(dc, dcn, s, row) * w_ref[pl.ds(k - 1 - s, 1), cols]
                    dws.append(jnp.sum(dc * xs[s], axis=0, keepdims=True))
                dx_ref[:, cols] = dx.astype(dx_ref.dtype)
                for s in range(k):
                    @pl.when(first)
                    def _(s=s, dw_ref=dw_ref, dws=dws):
                        dw_ref[pl.ds(k - 1 - s, 1), cols] = dws[s]

                    @pl.when(jnp.logical_not(first))
                    def _(s=s, dw_ref=dw_ref, dws=dws):
                        dw_ref[pl.ds(k - 1 - s, 1), cols] += dws[s]

    half = jax.ShapeDtypeStruct((t, D_FF), BF16)
    dwh = jax.ShapeDtypeStruct((k, D_FF), F32)
    return pl.pallas_call(
        body, grid=(ncol, ni),
        in_specs=[cur(0), prev(0, BF16_ROWS), nxt(0, BF16_ROWS), cur(ncol), prev(ncol, BF16_ROWS), nxt(ncol, BF16_ROWS),
                  cur(0), nxt(0, SUBLANES), taps(0), taps(ncol)],
        out_specs=[cur(0), cur(0), taps(0), taps(0)], out_shape=[half, half, dwh, dwh], name=name,
        compiler_params=_params(("parallel", "arbitrary")),
    )(u, u, u, u, u, u, dact, dact, w, w)


def _each(f, *lists):
    return [f(*a) for a in zip(*lists)]


@jax.custom_vjp
def _inv_unit_lower(lms):
    return _inv_blocks(lms)


def _inv_blocks(lms):
    c = lms[0].shape[0]
    ri = lax.broadcasted_iota(jnp.int32, (c, c), 0)
    ci = lax.broadcasted_iota(jnp.int32, (c, c), 1)
    eye = (ri == ci).astype(F32)
    dms = _each(lambda lm: eye - jnp.where((ri >> 1) == (ci >> 1), lm, 0.0), lms)
    for lv in range(1, int(math.log2(c))):
        below = ((ri >> (lv + 1)) == (ci >> (lv + 1))) & ((ri >> lv) != (ci >> lv))
        dbs = _each(lambda dm: dm.astype(BF16), dms)
        ods = _each(lambda lm, db: _dot(jnp.where(below, lm, 0.0).astype(BF16), db).astype(BF16), lms, dbs)
        dms = _each(lambda dm, db, od: dm - _dot(db, od), dms, dbs, ods)
    return dms


def _inv_fwd(lms):
    tms = _inv_blocks(lms)
    return tms, tms


def _inv_bwd(tms, dts):
    tbs = _each(lambda tm: tm.astype(BF16), tms)
    mid = _each(lambda tb, dt: _dot(tb, dt.astype(BF16), TN).astype(BF16), tbs, dts)
    return (_each(lambda m, tb: -_dot(m, tb, NT), mid, tbs),)


_inv_unit_lower.defvjp(_inv_fwd, _inv_bwd)


@jax.custom_vjp
def _inv_known(lms, tms):
    return tms


_inv_known.defvjp(lambda lms, tms: (tms, tms), lambda tms, dts: _inv_bwd(tms, dts) + (_each(jnp.zeros_like, tms),))


def _l2n(x):
    return x * lax.rsqrt(jnp.sum(x * x, axis=-1, keepdims=True) + EPS)


def _prep_fn(cqs, cks, cvs, bg, sel_b, sel_g, tms=None):
    c = cqs[0].shape[0]
    ri = lax.broadcasted_iota(jnp.int32, (c, c), 0)
    ci = lax.broadcasted_iota(jnp.int32, (c, c), 1)
    eye = (ri == ci).astype(F32)
    incl, strict = ci <= ri, ci < ri
    last = lax.broadcasted_iota(jnp.int32, (c, 1), 0) == c - 1
    to_row = lambda col: jnp.sum(col * eye, axis=0, keepdims=True)
    qs = _each(lambda a: _l2n(_silu(a)) * (HEAD_DIM_A ** -0.5), cqs)
    ks = _each(lambda a: _l2n(_silu(a)), cks)
    vbs = _each(lambda a: _silu(a).astype(BF16), cvs)
    betas = _each(lambda m: jnp.sum(bg * m, axis=1, keepdims=True), sel_b)
    gs = _each(lambda m: jnp.sum(bg * m, axis=1, keepdims=True), sel_g)
    gcss = _each(lambda g: jnp.sum(jnp.where(incl, to_row(g), 0.0), axis=1, keepdims=True), gs)
    gtots = _each(lambda gcs: jnp.sum(jnp.where(last, gcs, 0.0), axis=0, keepdims=True), gcss)
    decays = _each(lambda gcs: jnp.exp(jnp.where(incl, gcs - to_row(gcs), NEG)), gcss)
    kbs = _each(lambda k: k.astype(BF16), ks)
    lms = _each(lambda beta, kb, dec: jnp.where(strict, beta * _dot(kb, kb, NT) * dec, 0.0), betas, kbs, decays)
    tms = _inv_unit_lower(lms) if tms is None else _inv_known(lms, tms)
    ams = _each(lambda tm, beta: (tm * to_row(beta)).astype(BF16), tms, betas)
    gams = _each(jnp.exp, gcss)
    u0s = _each(_dot, ams, vbs)
    wks = _each(lambda am, gam, k: _dot(am, (gam * k).astype(BF16)), ams, gams, ks)
    qks = _each(lambda q, kb, dec: _dot(q.astype(BF16), kb, NT) * dec, qs, kbs, decays)
    qds = _each(lambda q, gam: q * gam, qs, gams)
    kds = _each(lambda k, gtot, gcs: k * jnp.exp(gtot - gcs), ks, gtots, gcss)
    gls = _each(lambda gtot: jnp.exp(gtot) * jnp.ones((SUBLANES, LANES), F32), gtots)
    return u0s, wks, qds, kds, qks, gls, tms


def _head_masks(h):
    lane = lax.broadcasted_iota(jnp.int32, (1, LANES), 1)
    return (lane == h).astype(F32), (lane == h + N_HEADS_A).astype(F32)


def _hsl(j):
    return slice(j * HEAD_DIM_A, (j + 1) * HEAD_DIM_A)


def gnorm_fwd(o, zsrc, w, *, name):
    t, width = o.shape
    tm = min(256, t)
    zoff = zsrc.shape[1] // width - 1

    def body(o_ref, z_ref, w_ref, out_ref):
        for h in range(N_HEADS_A):
            out_ref[:, _hsl(h)] = _f_gnorm(o_ref[:, _hsl(h)], z_ref[:, _hsl(h)], w_ref[...]).astype(out_ref.dtype)

    rows = pl.BlockSpec((tm, width), lambda i: (i, 0))
    return pl.pallas_call(
        body, grid=(t // tm,),
        in_specs=[rows, pl.BlockSpec((tm, width), lambda i: (i, zoff)), pl.BlockSpec(w.shape, lambda i: (0, 0))],
        out_specs=rows, out_shape=jax.ShapeDtypeStruct((t, width), BF16), name=name, compiler_params=_params(("parallel",)),
    )(o, zsrc, w)


def gnorm_bwd(o, zsrc, w, don, *, name):
    t, width = o.shape
    tm = min(256, t)
    zoff = zsrc.shape[1] // width - 1

    def body(o_ref, z_ref, w_ref, d_ref, do_ref, dz_ref, dw_ref):
        dw = jnp.zeros(w.shape, F32)
        for h in range(N_HEADS_A):
            _, vjp = jax.vjp(_f_gnorm, o_ref[:, _hsl(h)], z_ref[:, _hsl(h)], w_ref[...])
            do, dz, dwh = vjp(d_ref[:, _hsl(h)])
            do_ref[:, _hsl(h)] = do.astype(do_ref.dtype)
            dz_ref[:, _hsl(h)] = dz.astype(dz_ref.dtype)
            dw = dw + dwh
        first = pl.program_id(0) == 0

        @pl.when(first)
        def _():
            dw_ref[...] = dw

        @pl.when(jnp.logical_not(first))
        def _():
            dw_ref[...] += dw

    rows = pl.BlockSpec((tm, width), lambda i: (i, 0))
    wspec = pl.BlockSpec(w.shape, lambda i: (0, 0))
    return pl.pallas_call(
        body, grid=(t // tm,),
        in_specs=[rows, pl.BlockSpec((tm, width), lambda i: (i, zoff)), wspec, rows],
        out_specs=[rows, rows, wspec],
        out_shape=[jax.ShapeDtypeStruct((t, width), BF16)] * 2 + [jax.ShapeDtypeStruct(w.shape, F32)], name=name,
        compiler_params=_params(("arbitrary",)),
    )(o, zsrc, w, don)


def delta_prep(cqkv, bg, *, name):
    t = cqkv.shape[0]
    nh, hd, n = N_HEADS_A, HEAD_DIM_A, t // CHUNK

    def body(cq_ref, ck_ref, cv_ref, bg_ref, u0_ref, wk_ref, qd_ref, kd_ref, qk_ref, tm_ref, gl_ref):
        heads = range(nh)
        masks = [_head_masks(j) for j in heads]
        res = _prep_fn([cq_ref[:, _hsl(j)] for j in heads], [ck_ref[:, _hsl(j)] for j in heads],
                       [cv_ref[:, _hsl(j)] for j in heads], bg_ref[...], [m[0] for m in masks], [m[1] for m in masks])
        for o_ref, rs in zip((u0_ref, wk_ref, qd_ref, kd_ref, qk_ref, tm_ref), res[:5] + (res[6],)):
            for j in heads:
                o_ref[:, _hsl(j)] = rs[j].astype(o_ref.dtype)
        for j in heads:
            gl_ref[j * SUBLANES:(j + 1) * SUBLANES, :] = res[5][j]

    blk = lambda off: pl.BlockSpec((CHUNK, nh * hd), lambda i: (i, off))
    res = pl.pallas_call(
        body, grid=(n,),
        in_specs=[blk(0), blk(1), blk(2), pl.BlockSpec((CHUNK, LANES), lambda i: (i, 0))],
        out_specs=[blk(0)] * 6 + [pl.BlockSpec((nh * SUBLANES, LANES), lambda i: (i, 0))],
        out_shape=[jax.ShapeDtypeStruct((t, nh * hd), dt) for dt in (F32, BF16, BF16, BF16, BF16, F32)]
        + [jax.ShapeDtypeStruct((n * nh * SUBLANES, LANES), F32)],
        name=name, compiler_params=_params(("parallel",)),
    )(cqkv, cqkv, cqkv, bg)
    return [*res[:5], res[6]], res[5]


def delta_prep_bwd(cqkv, bg, tms, cts, *, name):
    t = cqkv.shape[0]
    nh, hd, n = N_HEADS_A, HEAD_DIM_A, t // CHUNK

    def body(cq_ref, ck_ref, cv_ref, bg_ref, tm_ref, c0, c1, c2, c3, c4, c5, dc_ref, dbg_ref):
        heads = range(nh)
        masks = [_head_masks(j) for j in heads]
        known = [tm_ref[:, _hsl(j)] for j in heads]
        _, vjp = jax.vjp(lambda a, b, c, d: _prep_fn(a, b, c, d, [m[0] for m in masks], [m[1] for m in masks], known)[:6],
                         [cq_ref[:, _hsl(j)] for j in heads], [ck_ref[:, _hsl(j)] for j in heads],
                         [cv_ref[:, _hsl(j)] for j in heads], bg_ref[...])
        cts = tuple([c[:, _hsl(j)] for j in heads] for c in (c0, c1, c2, c3, c4))
        dqs, dks, dvs, dbg = vjp(cts + ([c5[j * SUBLANES:(j + 1) * SUBLANES, :] for j in heads],))
        for part, ds in enumerate((dqs, dks, dvs)):
            for j in heads:
                dc_ref[:, _hsl(part * nh + j)] = ds[j]
        dbg_ref[...] = dbg

    blk = lambda off: pl.BlockSpec((CHUNK, nh * hd), lambda i: (i, off))
    gl_spec = pl.BlockSpec((nh * SUBLANES, LANES), lambda i: (i, 0))
    bg_spec = pl.BlockSpec((CHUNK, LANES), lambda i: (i, 0))
    return pl.pallas_call(
        body, grid=(n,),
        in_specs=[blk(0), blk(1), blk(2), bg_spec] + [blk(0)] * 6 + [gl_spec],
        out_specs=[pl.BlockSpec((CHUNK, 3 * nh * hd), lambda i: (i, 0)), bg_spec],
        out_shape=[jax.ShapeDtypeStruct((t, 3 * nh * hd), F32), jax.ShapeDtypeStruct((t, LANES), F32)],
        name=name, compiler_params=_params(("parallel",)),
    )(cqkv, cqkv, cqkv, bg, tms, *cts)


def delta_scan(u0, wk, qd, kd, qk, gl, *, name):
    t = u0.shape[0]
    nh, hd, n = N_HEADS_A, HEAD_DIM_A, t // CHUNK

    def body(u0_ref, wk_ref, qd_ref, kd_ref, qk_ref, gl_ref, o_ref, sin_ref, s_ref):
        @pl.when(pl.program_id(0) == 0)
        def _():
            s_ref[...] = jnp.zeros_like(s_ref)

        heads = list(range(nh))
        cols = lambda ref: [ref[:, _hsl(h)].astype(BF16) for h in heads]
        ss = [s_ref[h] for h in heads]
        for h in heads:
            sin_ref[h] = ss[h]
        sbs = _each(lambda s: s.astype(BF16), ss)
        ubs = _each(lambda h, wkb, sb: (u0_ref[:, _hsl(h)] - _dot(wkb, sb)).astype(BF16), heads, cols(wk_ref), sbs)
        os_ = _each(lambda qdb, sb, qkb, ub: _dot(qdb, sb) + _dot(qkb, ub), cols(qd_ref), sbs, cols(qk_ref), ubs)
        sn = _each(lambda h, s, kdb, ub: gl_ref[pl.ds(h * SUBLANES, 1), :] * s + _dot(kdb, ub, TN), heads, ss, cols(kd_ref), ubs)
        for h in heads:
            o_ref[:, _hsl(h)] = os_[h]
            s_ref[h] = sn[h]

    blk = pl.BlockSpec((CHUNK, nh * hd), lambda i: (i, 0))
    return pl.pallas_call(
        body, grid=(n,),
        in_specs=[blk] * 5 + [pl.BlockSpec((nh * SUBLANES, LANES), lambda i: (i, 0))],
        out_specs=[blk, pl.BlockSpec((None, nh, hd, hd), lambda i: (i, 0, 0, 0))],
        out_shape=[jax.ShapeDtypeStruct((t, nh * hd), F32), jax.ShapeDtypeStruct((n, nh, hd, hd), F32)],
        scratch_shapes=[pltpu.VMEM((nh, hd, hd), F32)], name=name,
        compiler_params=_params(("arbitrary",)),
    )(u0, wk, qd, kd, qk, gl)


def delta_scan_bwd(do, u0, wk, qd, kd, qk, gl, s_in, *, name):
    t = u0.shape[0]
    nh, hd, n = N_HEADS_A, HEAD_DIM_A, t // CHUNK

    def body(do_ref, u0_ref, wk_ref, qd_ref, kd_ref, qk_ref, gl_ref, sin_ref,
             du0_ref, dwk_ref, dqd_ref, dkd_ref, dqk_ref, dgl_ref, ds_ref):
        @pl.when(pl.program_id(0) == 0)
        def _():
            ds_ref[...] = jnp.zeros_like(ds_ref)

        corner = (lax.broadcasted_iota(jnp.int32, (SUBLANES, LANES), 0) == 0) & (lax.broadcasted_iota(jnp.int32, (SUBLANES, LANES), 1) == 0)
        heads = list(range(nh))
        cols = lambda ref: [ref[:, _hsl(h)].astype(BF16) for h in heads]
        ss, dss = [sin_ref[h] for h in heads], [ds_ref[h] for h in heads]
        sbs, dsbs = _each(lambda s: s.astype(BF16), ss), _each(lambda d: d.astype(BF16), dss)
        dobs, wkbs, qdbs, kdbs, qkbs = cols(do_ref), cols(wk_ref), cols(qd_ref), cols(kd_ref), cols(qk_ref)
        ubs = _each(lambda h, wkb, sb: (u0_ref[:, _hsl(h)] - _dot(wkb, sb)).astype(BF16), heads, wkbs, sbs)
        dus = _each(lambda qkb, dob, kdb, dsb: _dot(qkb, dob, TN) + _dot(kdb, dsb), qkbs, dobs, kdbs, dsbs)
        dubs = _each(lambda du: du.astype(BF16), dus)
        dwks = _each(lambda dub, sb: -_dot(dub, sb, NT), dubs, sbs)
        dqds = _each(lambda dob, sb: _dot(dob, sb, NT), dobs, sbs)
        dkds = _each(lambda ub, dsb: _dot(ub, dsb, NT), ubs, dsbs)
        dqks = _each(lambda dob, ub: _dot(dob, ub, NT), dobs, ubs)
        dgls = _each(lambda s, d: jnp.sum(jnp.sum(s * d, axis=1, keepdims=True), axis=0, keepdims=True), ss, dss)
        dsn = _each(lambda h, d, qdb, dob, wkb, dub: gl_ref[pl.ds(h * SUBLANES, 1), :] * d + _dot(qdb, dob, TN) - _dot(wkb, dub, TN),
                    heads, dss, qdbs, dobs, wkbs, dubs)
        for h in heads:
            du0_ref[:, _hsl(h)] = dus[h]
            dwk_ref[:, _hsl(h)] = dwks[h]
            dqd_ref[:, _hsl(h)] = dqds[h]
            dkd_ref[:, _hsl(h)] = dkds[h]
            dqk_ref[:, _hsl(h)] = dqks[h]
            dgl_ref[h * SUBLANES:(h + 1) * SUBLANES, :] = jnp.where(corner, dgls[h], 0.0)
            ds_ref[h] = dsn[h]

    blk = pl.BlockSpec((CHUNK, nh * hd), lambda i: (n - 1 - i, 0))
    gl_spec = pl.BlockSpec((nh * SUBLANES, LANES), lambda i: (n - 1 - i, 0))
    return pl.pallas_call(
        body, grid=(n,),
        in_specs=[blk] * 6 + [gl_spec, pl.BlockSpec((None, nh, hd, hd), lambda i: (n - 1 - i, 0, 0, 0))],
        out_specs=[blk] * 5 + [gl_spec],
        out_shape=[jax.ShapeDtypeStruct((t, nh * hd), F32)] * 5 + [jax.ShapeDtypeStruct((n * nh * SUBLANES, LANES), F32)],
        scratch_shapes=[pltpu.VMEM((nh, hd, hd), F32)], name=name,
        compiler_params=_params(("arbitrary",)),
    )(do, u0, wk, qd, kd, qk, gl, s_in)


N_PAIRS = N_HEADS_B // 2
PAIRS_PER_KV = N_PAIRS // N_KV_B


def _psl(j):
    return slice(j * LANES, (j + 1) * LANES)


KV_STEP = 4


def _att_fn(qps, kcs, kps, vcs, vps, sinks, kv0, first):
    w = WINDOW
    lane = lax.broadcasted_iota(jnp.int32, (1, LANES), 1)
    lo = (lane < HEAD_DIM_B).astype(F32)
    qi = lax.broadcasted_iota(jnp.int32, (w, w), 0)
    kj = lax.broadcasted_iota(jnp.int32, (w, w), 1)
    dist_c = (qi - kj).astype(F32)
    valid_c = kj <= qi
    valid_p = (kj > qi) & (first < 0.5)
    bf = lambda xs: [a.astype(BF16) for a in xs]
    kcb, kpb, vcb, vpb = bf(kcs), bf(kps), bf(vcs), bf(vps)
    scale = HEAD_DIM_B ** -0.5
    heads = [(g, j, half) for g in range(len(kcs)) for j in range(PAIRS_PER_KV) for half in range(2)]
    kvs = [g for g, _, _ in heads]
    hmasks = [lo if half == 0 else 1.0 - lo for _, _, half in heads]
    hds = [2.0 * (PAIRS_PER_KV * (kv0 + g) + j) + half for g, j, half in heads]
    slopes = _each(lambda hd: jnp.exp(-(hd + 1.0) * (8.0 / N_HEADS_B * math.log(2.0))), hds)
    snks = _each(lambda hd: jnp.sum(sinks * (lane.astype(F32) == hd).astype(F32), axis=1, keepdims=True), hds)
    qhs = _each(lambda h, hm: (qps[h[0] * PAIRS_PER_KV + h[1]] * hm).astype(BF16), heads, hmasks)
    lcs = _each(lambda qh, g, sl: jnp.where(valid_c, _dot(qh, kcb[g], NT) * scale - sl * dist_c, NEG), qhs, kvs, slopes)
    lps = _each(lambda qh, g, sl: jnp.where(valid_p, _dot(qh, kpb[g], NT) * scale - sl * (dist_c + w), NEG), qhs, kvs, slopes)
    ms = _each(lambda lc, lp, sk: lax.stop_gradient(jnp.maximum(jnp.maximum(jnp.max(lc, axis=1, keepdims=True),
                                                                            jnp.max(lp, axis=1, keepdims=True)), sk)), lcs, lps, snks)
    ecs = _each(lambda lc, m: jnp.exp(lc - m), lcs, ms)
    eps = _each(lambda lp, m: jnp.exp(lp - m), lps, ms)
    invs = _each(lambda ec, ep, sk, m: 1.0 / (jnp.sum(ec, axis=1, keepdims=True) + jnp.sum(ep, axis=1, keepdims=True) + jnp.exp(sk - m)),
                 ecs, eps, snks, ms)
    ohs = _each(lambda ec, ep, inv, g, hm: (_dot((ec * inv).astype(BF16), vcb[g]) + _dot((ep * inv).astype(BF16), vpb[g])) * hm,
                ecs, eps, invs, kvs, hmasks)
    return [ohs[2 * j] + ohs[2 * j + 1] for j in range(len(qps))]


def _scalar11(v):
    return jnp.full((1, 1), v, F32)


def _att_specs(row_of):
    cur = pl.BlockSpec((WINDOW, KV_STEP * LANES), lambda i, kv: (row_of(i), kv))
    prev = pl.BlockSpec((WINDOW, KV_STEP * LANES), lambda i, kv: (jnp.maximum(row_of(i) - 1, 0), kv))
    qs = pl.BlockSpec((WINDOW, KV_STEP * PAIRS_PER_KV * LANES), lambda i, kv: (row_of(i), kv))
    return qs, cur, prev, pl.BlockSpec((1, LANES), lambda i, kv: (0, 0))


def swa_fwd(qsrc, kd, vd, sinks, *, name):
    t = kd.shape[0]
    nb = t // WINDOW
    npair = KV_STEP * PAIRS_PER_KV

    def body(q_ref, kc_ref, kp_ref, vc_ref, vp_ref, s_ref, o_ref):
        first = _scalar11((pl.program_id(0) == 0).astype(F32))
        kv0 = _scalar11((pl.program_id(1) * KV_STEP).astype(F32))
        per_kv = lambda ref: [ref[:, _psl(g)] for g in range(KV_STEP)]
        outs = _att_fn([q_ref[:, _psl(j)] for j in range(npair)], per_kv(kc_ref), per_kv(kp_ref), per_kv(vc_ref), per_kv(vp_ref),
                       s_ref[...], kv0, first)
        for j in range(npair):
            o_ref[:, _psl(j)] = outs[j].astype(o_ref.dtype)

    qs, cur, prev, sk = _att_specs(lambda i: i)
    return pl.pallas_call(
        body, grid=(nb, N_KV_B // KV_STEP), in_specs=[qs, cur, prev, cur, prev, sk],
        out_specs=qs, out_shape=jax.ShapeDtypeStruct((t, N_PAIRS * LANES), BF16), name=name,
        compiler_params=_params(("parallel", "parallel")),
    )(qsrc, kd, kd, vd, vd, sinks)


def swa_bwd(do, qsrc, kd, vd, sinks, *, name):
    t = kd.shape[0]
    nb = t // WINDOW

    npair = KV_STEP * PAIRS_PER_KV

    def body(do_ref, q_ref, kc_ref, kp_ref, vc_ref, vp_ref, s_ref, dq_ref, dk_ref, dv_ref, ds_ref, carry_k, carry_v):
        step, kvg = pl.program_id(0), pl.program_id(1)
        first = _scalar11((step == nb - 1).astype(F32))

        @pl.when((step == 0) & (kvg == 0))
        def _():
            carry_k[...] = jnp.zeros_like(carry_k)
            carry_v[...] = jnp.zeros_like(carry_v)
            ds_ref[...] = jnp.zeros_like(ds_ref)

        kv0 = _scalar11((kvg * KV_STEP).astype(F32))
        per_kv = lambda ref: [ref[:, _psl(g)].astype(F32) for g in range(KV_STEP)]
        _, vjp = jax.vjp(lambda *a: _att_fn(*a, kv0, first), [q_ref[:, _psl(j)].astype(F32) for j in range(npair)],
                         per_kv(kc_ref), per_kv(kp_ref), per_kv(vc_ref), per_kv(vp_ref), s_ref[...])
        dqs, dkc, dkp, dvc, dvp, dsk = vjp([do_ref[:, _psl(j)].astype(F32) for j in range(npair)])
        for j in range(npair):
            dq_ref[:, _psl(j)] = dqs[j].astype(dq_ref.dtype)
        ds_ref[...] += dsk
        fold = lambda g: g + pltpu.roll(g, HEAD_DIM_B, 1)
        for g in range(KV_STEP):
            kv = kvg * KV_STEP + g
            dk_ref[:, _psl(g)] = fold(dkc[g] + carry_k[kv]).astype(dk_ref.dtype)
            dv_ref[:, _psl(g)] = fold(dvc[g] + carry_v[kv]).astype(dv_ref.dtype)
            carry_k[kv] = dkp[g]
            carry_v[kv] = dvp[g]

    qs, cur, prev, sk = _att_specs(lambda i: nb - 1 - i)
    return pl.pallas_call(
        body, grid=(nb, N_KV_B // KV_STEP),
        in_specs=[qs, qs, cur, prev, cur, prev, sk],
        out_specs=[qs, cur, cur, sk],
        out_shape=[jax.ShapeDtypeStruct((t, N_PAIRS * LANES), BF16), jax.ShapeDtypeStruct((t, N_KV_B * LANES), BF16),
                   jax.ShapeDtypeStruct((t, N_KV_B * LANES), BF16), jax.ShapeDtypeStruct((1, LANES), F32)],
        scratch_shapes=[pltpu.VMEM((N_KV_B, WINDOW, LANES), F32), pltpu.VMEM((N_KV_B, WINDOW, LANES), F32)],
        name=name, compiler_params=_params(("arbitrary", "arbitrary")),
    )(do, qsrc, kd, kd, vd, vd, sinks)


def loss_head(h, tgt, w, *, name):
    t, d = h.shape
    tm = min(256, t)

    def body(h_ref, t_ref, w_ref, dh_ref, dw_ref, l_ref):
        tg = t_ref[...]

        def f(hv, wv):
            err = _f_norm(hv, wv) - tg
            return 0.5 * jnp.sum(jnp.sum(err * err, axis=1, keepdims=True), axis=0, keepdims=True) * (1.0 / d)

        lv, vjp = jax.vjp(f, h_ref[...], w_ref[...])
        dh, dw = vjp(jnp.ones((1, 1), F32))
        dh_ref[...] = dh
        first = pl.program_id(0) == 0

        @pl.when(first)
        def _():
            dw_ref[...] = dw
            l_ref[...] = lv * jnp.ones((1, LANES), F32)

        @pl.when(jnp.logical_not(first))
        def _():
            dw_ref[...] += dw
            l_ref[...] += lv * jnp.ones((1, LANES), F32)

    rows = pl.BlockSpec((tm, d), lambda i: (i, 0))
    one = lambda c: pl.BlockSpec((1, c), lambda i: (0, 0))
    return pl.pallas_call(
        body, grid=(t // tm,), in_specs=[rows, rows, one(d)], out_specs=[rows, one(d), one(LANES)],
        out_shape=[jax.ShapeDtypeStruct((t, d), F32), jax.ShapeDtypeStruct((1, d), F32), jax.ShapeDtypeStruct((1, LANES), F32)],
        name=name, compiler_params=_params(("arbitrary",)),
    )(h, tgt, w)


def _row_tile(r, cap=256):
    tr = r
    if r % SUBLANES == 0:
        for cand in range(SUBLANES, min(r, cap) + 1, SUBLANES):
            if r % cand == 0:
                tr = cand
    return tr


def _adamw_update(wv, gv, mv, vv):
    mn = ADAM_B1 * mv + (1.0 - ADAM_B1) * gv
    vn = ADAM_B2 * vv + (1.0 - ADAM_B2) * jnp.square(gv)
    m_hat = mn / (1.0 - ADAM_B1 ** ADAM_STEP)
    v_hat = vn / (1.0 - ADAM_B2 ** ADAM_STEP)
    return -ADAM_LR * (m_hat / (jnp.sqrt(v_hat) + ADAM_EPS) + ADAM_WD * wv), mn, vn


def adamw_layers(w, halves, m, v, *, name):
    nl, r, c = w.shape
    tr = _row_tile(r // 2)
    nbh = r // 2 // tr

    def body(w_ref, *rest):
        g_refs, m_ref, v_ref = rest[:2 * nl], rest[2 * nl], rest[2 * nl + 1]
        d_ref, mo_ref, vo_ref, go_ref = rest[2 * nl + 2:]
        layer, i = pl.program_id(0), pl.program_id(1)
        mine = (i < nbh) == (lax.axis_index("c") == 0)
        gv = jnp.where(mine, g_refs[0][...], g_refs[1][...])
        for k in range(1, nl):
            gv = jnp.where(layer == k, jnp.where(mine, g_refs[2 * k][...], g_refs[2 * k + 1][...]), gv)
        d_ref[...], mo_ref[...], vo_ref[...] = _adamw_update(w_ref[...], gv, m_ref[...], v_ref[...])
        go_ref[...] = gv

    spec3 = pl.BlockSpec((None, tr, c), lambda k, i: (k, i, 0))
    g_specs = [pl.BlockSpec((tr, c), lambda k, i, q=q: (jnp.where(k == q, i % nbh, 0), 0)) for q in range(nl) for _ in range(2)]
    return pl.pallas_call(
        body, grid=(nl, r // tr), in_specs=[spec3] + g_specs + [spec3, spec3], out_specs=[spec3] * 4,
        out_shape=[jax.ShapeDtypeStruct((nl, r, c), F32)] * 4, name=name, compiler_params=_params(("arbitrary", "arbitrary")),
    )(w, *[h for pair in halves for h in pair], m, v)


def adamw(w, g, m, v, *, name):
    r, c = w.shape
    tr = _row_tile(r)

    def body(w_ref, g_ref, m_ref, v_ref, d_ref, mo_ref, vo_ref):
        d_ref[...], mo_ref[...], vo_ref[...] = _adamw_update(w_ref[...], g_ref[...], m_ref[...], v_ref[...])

    spec = pl.BlockSpec((tr, c), lambda i: (i, 0))
    return pl.pallas_call(
        body, grid=(r // tr,), in_specs=[spec] * 4, out_specs=[spec] * 3,
        out_shape=[jax.ShapeDtypeStruct((r, c), F32)] * 3, name=name, compiler_params=_params(("parallel",)),
    )(w, g, m, v)


def _place():
    return lax.axis_index("x"), lax.axis_index("y"), lax.axis_index("c")


def allgather8(blk, *, name):
    def body(x_ref, out_ref, send_sems, recv_sems, local_sem):
        x, y, c = _place()
        me = 4 * x + 2 * y + c
        mine = pltpu.make_async_copy(x_ref, out_ref.at[me], local_sem)
        mine.start()
        sent = []
        for k in range(1, N_DEV):
            to = (x ^ ((k >> 2) & 1), y ^ ((k >> 1) & 1), c ^ (k & 1))
            cp = pltpu.make_async_remote_copy(src_ref=x_ref, dst_ref=out_ref.at[me], send_sem=send_sems.at[k - 1],
                                              recv_sem=recv_sems.at[k - 1], device_id=to, device_id_type=MESH)
            cp.start()
            sent.append(cp)
        for k in range(1, N_DEV):
            frm = me ^ k
            pltpu.make_async_remote_copy(src_ref=x_ref, dst_ref=out_ref.at[frm], send_sem=send_sems.at[k - 1],
                                         recv_sem=recv_sems.at[k - 1], device_id=(x, y, c), device_id_type=MESH).wait_recv()
        for cp in sent:
            cp.wait_send()
        mine.wait()

    vm = pl.BlockSpec(memory_space=pltpu.VMEM)
    return pl.pallas_call(
        body, in_specs=[vm], out_specs=vm, out_shape=jax.ShapeDtypeStruct((N_DEV,) + blk.shape, blk.dtype), name=name,
        scratch_shapes=[pltpu.SemaphoreType.DMA((N_DEV - 1,)), pltpu.SemaphoreType.DMA((N_DEV - 1,)), pltpu.SemaphoreType.DMA],
    )(blk)


def _other_chips(x, y):
    return [(1 - x, y), (x, 1 - y), (1 - x, 1 - y)]


def _hbm_call(body, ins, out_shapes, n_sems, name):
    hbm = pl.BlockSpec(memory_space=pl.ANY)
    return pl.pallas_call(
        body, in_specs=[hbm] * len(ins), out_specs=[hbm] * len(out_shapes), out_shape=out_shapes, name=name,
        scratch_shapes=[pltpu.SemaphoreType.DMA((n_sems,)), pltpu.SemaphoreType.DMA((n_sems,))],
    )(*ins)


def _half_rows(c, rh):
    return pl.ds(pl.multiple_of(c * rh, BF16_ROWS), rh)


def gather_units(units, *, name):
    nu = len(units)
    shapes = []
    for arr, layer_major in units:
        r, cols = arr.shape
        shapes.append(jax.ShapeDtypeStruct((2, N_CHIPS, r // 2, cols) if layer_major else (N_CHIPS, r, cols), arr.dtype))

    def body(*refs):
        in_refs, out_refs, send_sems, recv_sems = refs[:nu], refs[nu:2 * nu], refs[2 * nu], refs[2 * nu + 1]
        x, y, c = _place()
        me_chip = 2 * x + y
        sib = (x, y, 1 - c)
        chips = _other_chips(x, y)

        def copy(k, src, dst, to):
            return pltpu.make_async_remote_copy(src_ref=src, dst_ref=dst, send_sem=send_sems.at[k], recv_sem=recv_sems.at[k],
                                                device_id=to, device_id_type=MESH)

        first, passed, landing = [], [], []
        for u, (arr, layer_major) in enumerate(units):
            rh = arr.shape[0] // 2
            out_ref = out_refs[u]
            slot = (lambda chip, half, o=out_ref: o.at[half, chip]) if layer_major else \
                   (lambda chip, half, o=out_ref, rh=rh: o.at[chip, _half_rows(half, rh), :])
            my_half = in_refs[u].at[_half_rows(c, rh), :]
            for j, (cx, cy) in enumerate(chips):
                k = 6 * u + j
                first.append(copy(k, my_half, slot(me_chip, c), (cx, cy, c)))
                passed.append(copy(k + 3, slot(2 * cx + cy, c), slot(2 * cx + cy, c), sib))
                landing.append((copy(k, my_half, slot(2 * cx + cy, c), sib), copy(k + 3, my_half, slot(2 * cx + cy, 1 - c), sib)))
        for cp in first:
            cp.start()
        for (over_ici, _), fwd in zip(landing, passed):
            over_ici.wait_recv()
            fwd.start()
        for _, from_sibling in landing:
            from_sibling.wait_recv()
        for cp in first + passed:
            cp.wait_send()

    return _hbm_call(body, [a for a, _ in units], shapes, 6 * nu, name)


HBM_SPEC = pl.BlockSpec(memory_space=pltpu.HBM)
SEM_SPEC = pl.BlockSpec(memory_space=pltpu.SEMAPHORE)
ORDERED_EFFECT = pltpu.SideEffectType.DATAFLOW_SIDE_EFFECTING


def _split_start(body, srcs, land_shapes, after, *, name):
    nu = len(srcs)
    lands = [lax.empty(s.shape, s.dtype) for s in land_shapes]

    def whole(*refs):
        body(refs[:nu], refs[nu:2 * nu], refs[2 * nu + 1], refs[2 * nu + 2])
        refs[-1][...] = jnp.zeros((SUBLANES, LANES), F32)

    hbm = lambda a: pltpu.with_memory_space_constraint(a, pltpu.HBM)
    sems = pltpu.SemaphoreType.DMA((nu,))
    res = pl.pallas_call(
        whole, name=name, in_specs=[HBM_SPEC] * (2 * nu) + [pl.BlockSpec(memory_space=pl.ANY)],
        out_shape=[sems, sems] + [pltpu.HBM(a.shape, a.dtype) for a in srcs] + [pltpu.HBM(s.shape, s.dtype) for s in land_shapes]
        + [jax.ShapeDtypeStruct((SUBLANES, LANES), F32)],
        out_specs=[SEM_SPEC, SEM_SPEC] + [HBM_SPEC] * (2 * nu) + [pl.BlockSpec(memory_space=pltpu.VMEM)],
        input_output_aliases={q: 2 + q for q in range(2 * nu)},
        compiler_params=pltpu.CompilerParams(has_side_effects=ORDERED_EFFECT),
    )(*[hbm(a) for a in srcs], *[hbm(a) for a in lands], after)
    return res[0], res[1], res[2:2 + nu], res[2 + nu:2 + 2 * nu], res[-1]


def _split_wait(pending, moved, after, *, name):
    send_sems, recv_sems, srcs, lands, _ = pending
    nu = len(srcs)

    def body(*refs):
        land_refs, ssem, rsem = refs[nu:2 * nu], refs[2 * nu], refs[2 * nu + 1]
        x, y, c = _place()
        for u in range(nu):
            size = moved(land_refs[u])
            cp = pltpu.make_async_remote_copy(src_ref=size, dst_ref=size, send_sem=ssem.at[u], recv_sem=rsem.at[u],
                                              device_id=(x, y, c), device_id_type=MESH)
            cp.wait_send()
            cp.wait_recv()

    res = pl.pallas_call(
        body, name=name, in_specs=[HBM_SPEC] * (2 * nu) + [SEM_SPEC, SEM_SPEC, pl.BlockSpec(memory_space=pl.ANY)],
        out_shape=[pltpu.HBM(a.shape, a.dtype) for a in srcs] + [pltpu.HBM(a.shape, a.dtype) for a in lands],
        out_specs=[HBM_SPEC] * (2 * nu), input_output_aliases={q: q for q in range(2 * nu)},
        compiler_params=pltpu.CompilerParams(has_side_effects=ORDERED_EFFECT),
    )(*srcs, *lands, send_sems, recv_sems, after)
    return res[nu:]


def gather_start(shards, after, *, name):
    def body(src_refs, land_refs, send_sems, recv_sems):
        x, y, c = _place()
        for u, shard in enumerate(shards):
            rows = _half_rows(c, shard.shape[0] // 2)
            for cx, cy in _other_chips(x, y):
                for core in range(2):
                    pltpu.make_async_remote_copy(src_ref=src_refs[u].at[rows, :], dst_ref=land_refs[u].at[2 * x + y, rows, :],
                                                 send_sem=send_sems.at[u], recv_sem=recv_sems.at[u], device_id=(cx, cy, core),
                                                 device_id_type=MESH).start()

    return _split_start(body, shards, [jax.ShapeDtypeStruct((N_CHIPS,) + s.shape, s.dtype) for s in shards], after, name=name)


def gather_wait(pending, after, *, name):
    return _split_wait(pending, lambda land: land.at[pl.ds(0, N_CHIPS - 1)], after, name=name)


def scatter_start(pairs, *, name):
    def body(src_refs, land_refs, send_sems, recv_sems):
        x, y, c = _place()
        for u in range(len(pairs)):
            for j, (cx, cy) in enumerate(_other_chips(x, y)):
                pltpu.make_async_remote_copy(src_ref=src_refs[u].at[2 * cx + cy], dst_ref=land_refs[u].at[j], send_sem=send_sems.at[u],
                                             recv_sem=recv_sems.at[u], device_id=(cx, cy, c), device_id_type=MESH).start()

    return _split_start(body, pairs, [jax.ShapeDtypeStruct((N_CHIPS - 1,) + p.shape[1:], p.dtype) for p in pairs], pairs[0], name=name)


def scatter_wait(pending, after, *, name):
    return _split_wait(pending, lambda land: land, after, name=name)


def swap_units(units, *, name):
    nu = len(units)

    def body(*refs):
        g_refs, out_refs, send_sems, recv_sems = refs[:nu], refs[nu:2 * nu], refs[2 * nu], refs[2 * nu + 1]
        x, y, c = _place()
        cps = [pltpu.make_async_remote_copy(src_ref=g_refs[u].at[:, _half_rows(1 - c, units[u].shape[1] // 2), :], dst_ref=out_refs[u],
                                            send_sem=send_sems.at[u], recv_sem=recv_sems.at[u], device_id=(x, y, 1 - c),
                                            device_id_type=MESH) for u in range(nu)]
        for cp in cps:
            cp.start()
        for cp in cps:
            cp.wait()

    shapes = [jax.ShapeDtypeStruct((N_CHIPS, g.shape[1] // 2, g.shape[2]), g.dtype) for g in units]
    return _hbm_call(body, units, shapes, nu, name)


def join_units(units, *, name):
    nu = len(units)

    def body(*refs):
        h_refs, out_refs, send_sems, recv_sems = refs[:nu], refs[nu:2 * nu], refs[2 * nu], refs[2 * nu + 1]
        x, y, c = _place()
        cps = [pltpu.make_async_remote_copy(src_ref=h_refs[u], dst_ref=out_refs[u], send_sem=send_sems.at[u], recv_sem=recv_sems.at[u],
                                            device_id=(x, y, 1 - c), device_id_type=MESH) for u in range(nu)]
        for cp in cps:
            cp.start()
        for cp in cps:
            cp.wait()

    return _hbm_call(body, units, [jax.ShapeDtypeStruct(h.shape, h.dtype) for h in units], nu, name)


def _half_tile(rh):
    tr = rh
    for cand in range(BF16_ROWS, min(rh, 512) + 1, BF16_ROWS):
        if rh % cand == 0:
            tr = cand
    return tr


def pair_add(g, sib, *, name):
    nc, rh, cols = sib.shape
    tr = _half_tile(rh)
    nbh = rh // tr

    def body(g0_ref, g1_ref, s_ref, o_ref):
        mine = jnp.where(lax.axis_index("c") == 0, g0_ref[...], g1_ref[...])
        o_ref[...] = (mine.astype(F32) + s_ref[...].astype(F32)).astype(o_ref.dtype)

    blk = lambda off: pl.BlockSpec((None, tr, cols), lambda j, i: (j, off + i, 0))
    return pl.pallas_call(
        body, grid=(nc, nbh), in_specs=[blk(0), blk(nbh), blk(0)], out_specs=blk(0),
        out_shape=jax.ShapeDtypeStruct(sib.shape, BF16), name=name, compiler_params=_params(("parallel", "parallel")),
    )(g, g, sib)


def chips_add(pair, landed, *, name):
    nc, rh, cols = pair.shape
    tr = _half_tile(rh)

    def body(*refs):
        chip = 2 * lax.axis_index("x") + lax.axis_index("y")
        acc = refs[0][...]
        for j in range(1, nc):
            acc = jnp.where(chip == j, refs[j][...], acc)
        acc = acc.astype(F32)
        for r in refs[nc:-1]:
            acc = acc + r[...].astype(F32)
        refs[-1][...] = acc

    part = lambda q: pl.BlockSpec((None, tr, cols), lambda i, q=q: (q, i, 0))
    return pl.pallas_call(
        body, grid=(rh // tr,), in_specs=[part(q) for q in range(nc)] + [part(q) for q in range(landed.shape[0])],
        out_specs=pl.BlockSpec((tr, cols), lambda i: (i, 0)),
        out_shape=jax.ShapeDtypeStruct((rh, cols), F32), name=name, compiler_params=_params(("parallel",)),
    )(*[pair] * nc, *[landed] * landed.shape[0])


def sum8(g, *, name):
    def body(g_ref, o_ref):
        acc = g_ref[0]
        for d in range(1, N_DEV):
            acc = acc + g_ref[d]
        o_ref[...] = acc

    return pl.pallas_call(body, out_shape=jax.ShapeDtypeStruct(g.shape[1:], F32), name=name)(g)


def _dup_halves(a):
    t = a.shape[0]
    a = a.reshape(t, N_KV_B, HEAD_DIM_B)
    return jnp.concatenate([a, a], axis=-1).reshape(t, N_KV_B * LANES)


def _undup(a):
    t = a.shape[0]
    return a.reshape(t, N_KV_B, LANES)[:, :, :HEAD_DIM_B].reshape(t, N_KV_B * HEAD_DIM_B)


def _lane_pad(v, offset=0):
    return jnp.zeros((1, LANES), F32).at[0, offset:offset + v.shape[0]].set(v)


SHARD_UP = 2 * D_FF // N_CHIPS
SHARD_BIN = (N_HEADS_B + 2 * N_KV_B) * HEAD_DIM_B // N_CHIPS
SHARD_PROJ = D_MODEL // N_CHIPS


def local_step(x, p, tgt, sm, weight, on_grads):
    t = x.shape[0]
    rtm = min(256, t)
    hk = N_HEADS_A * HEAD_DIM_A
    qd_b = N_HEADS_B * HEAD_DIM_B
    kd_b = N_KV_B * HEAD_DIM_B
    gs = {}
    norm = lambda h, w, nm: tile_map(_f_norm, [(h, D_MODEL, 0)], [w], [(D_MODEL, BF16)], tm=rtm, ncol=1, name=nm)[0]

    spec = pl.BlockSpec
    mtm = _tile(D_MODEL, MM_TM_CAP)
    p_bf = p.astype(BF16)
    alog_p = _lane_pad(sm["a_log"][0], N_HEADS_A)
    dtb_p = _lane_pad(sm["a_dt_bias"][0], N_HEADS_A)
    sinks_p = _lane_pad(sm["b_sinks"][0])
    nw = lambda name, i: sm[name][i:i + 1]
    by_chip = lambda kdim, ns: dict(tn=ns, tk=kdim, b_spec=spec((None, kdim, ns), lambda r, j, kk: (j, kk, 0)))
    by_chip_t = lambda ndim, ns: dict(n=ndim, tn=ndim, tk=ns, b_spec=spec((None, ndim, ns), lambda r, j, kk: (kk, j, 0)))
    cache = {}

    def wgt(name, i, after):
        if (name, i) not in cache:
            cache[name, i] = weight(name, i, after)
        return cache[name, i]

    saved = []
    h = x
    hn_next = norm(h, nw("norm_mix", 0), "norm_mix0")
    for i in range(DEPTH):
        s = {"h0": h, "hn": hn_next}
        if i % 2 == 0:
            s["pm"] = mm(s["hn"], wgt("a_w_in", i, h), name="a_in")
            tail = (s["pm"], LANES, 4 * hk // LANES)
            s["c"] = conv_fwd(s["pm"], wgt("a_conv", i, h), name="a_conv")
            s["bg"] = tile_map(_f_betag, [tail], [alog_p, dtb_p], [(LANES, F32)], tm=rtm, ncol=1, name="a_betag")[0]
            s["prep"], s["tms"] = delta_prep(s["c"], s["bg"], name="a_prep")
            s["o"], s["s_in"] = delta_scan(*s["prep"], name="a_scan")
            s["on"] = gnorm_fwd(s["o"], s["pm"], sm["a_norm"], name="a_gnorm")
            h, s["hf"] = mm(s["on"], wgt("a_w_out", i, s["on"]), add=h, norm_w=nw("norm_ffn", i), name="a_out")
        else:
            s["pb"] = mm(s["hn"], wgt("b_w_in", i, s["hn"]), name="b_in", out_dtype=BF16, n=N_CHIPS * SHARD_BIN,
                         **by_chip(D_MODEL, SHARD_BIN))
            s["kd"], s["vd"] = _dup_halves(s["pb"][:, qd_b:qd_b + kd_b]), _dup_halves(s["pb"][:, qd_b + kd_b:])
            s["ao"] = swa_fwd(s["pb"], s["kd"], s["vd"], sinks_p, name="b_att")
            h, s["hf"] = mm(s["ao"], wgt("b_w_out", i, s["ao"]), add=h, norm_w=nw("norm_ffn", i), name="b_out")
        s["h1"] = h
        s["u"] = mm(s["hf"], wgt("f_w_up", i, s["hf"]), name=f"f_up{i}", out_dtype=BF16, n=2 * D_FF, tm_cap=2 * MM_TM_CAP,
                    **by_chip(D_MODEL, SHARD_UP))
        s["act"] = conv_act_fwd(s["u"], wgt("f_conv", i, s["hf"]), name=f"f_conv_act{i}")
        h, s["hp"] = mm(s["act"], wgt("f_w_down", i, s["act"]), add=h, norm_w=nw("norm_ple", i), name=f"f_down{i}", tk=D_FF)
        s["h2"] = h
        s["gl"] = mm(s["hp"], wgt("ple_w_gate", i, s["hp"]), name=f"ple_gate{i}")
        s["pe"] = mm(p_bf[i], wgt("ple_w_proj", i, s["hp"]), name=f"ple_proj{i}", n=D_MODEL, **by_chip(PLE_DIM, SHARD_PROJ))
        rows3 = [(h, D_MODEL, 0), (s["gl"], D_MODEL, 0), (s["pe"], D_MODEL, 0)]
        if i + 1 < DEPTH:
            def mix_norm(hv, g, e, wn):
                hn = hv + _f_ple(g, e)
                return hn, _f_norm(hn, wn)
            h, hn_next = tile_map(mix_norm, rows3, [nw("norm_mix", i + 1)], [(D_MODEL, F32), (D_MODEL, BF16)], tm=rtm, ncol=1,
                                  name=f"ple_mix{i}")
        else:
            h = tile_map(lambda hv, g, e: hv + _f_ple(g, e), rows3, [], [(D_MODEL, F32)], tm=rtm, ncol=1, name=f"ple_mix{i}")[0]
        saved.append(s)

    dh, gnf, loss = loss_head(h, tgt, sm["norm_final"][None, :], name="loss_head")
    gs["norm_final"] = gnf[0]

    g_mix, g_ffn, g_ple, g_conv = ([None] * DEPTH for _ in range(4))
    zero = jnp.zeros((1, 1), F32)
    for i in reversed(range(DEPTH)):
        s, gw = saved[i], {}
        by_rows = lambda g: g.reshape(N_CHIPS, g.shape[0] // N_CHIPS, g.shape[1])
        (dgl, dpe), _ = tile_vjp(_f_ple, [(s["gl"], D_MODEL, 0), (s["pe"], D_MODEL, 0)], [], [(dh, D_MODEL, 0)], n_diff=2,
                                 tm=rtm, ncol=1, name=f"ple_mix_bwd{i}", grad_dtypes=[BF16, BF16])
        gw["ple_w_proj"] = mm(p_bf[i], dpe, ta=True, name=f"ple_proj_dw{i}", out_dtype=BF16, tn=SHARD_PROJ,
                              o_shape=(N_CHIPS, PLE_DIM, SHARD_PROJ), o_spec=spec((None, PLE_DIM, SHARD_PROJ), lambda r, j, kk: (j, r, 0)))
        gw["ple_w_gate"] = by_rows(mm(s["hp"], dgl, ta=True, name=f"ple_gate_dw{i}", out_dtype=BF16))
        fused = dict(tb=True, tm_cap=MM_TM_CAP // 2)
        dh, g_ple[i] = mm(dgl, cache["ple_w_gate", i], name=f"ple_gate_dx{i}", norm_grad=(s["h2"], nw("norm_ple", i) + zero, dh), **fused)

        dact = mm(dh, cache["f_w_down", i], tb=True, name=f"f_down_dx{i}")
        gw["f_w_down"] = by_rows(mm(s["act"], dh, ta=True, name=f"f_down_dw{i}", out_dtype=BF16, tm_cap=D_FF // 2))
        du_halves = conv_act_bwd(s["u"], dact, cache["f_conv", i], name=f"f_conv_act_bwd{i}")
        g_conv[i] = jnp.concatenate(du_halves[2:], axis=1)
        dhf = g_up = None
        for half, du in enumerate(du_halves[:2]):
            c0 = half * (N_CHIPS // 2)
            g_up = mm(s["hf"], du, ta=True, name=f"f_up_dw{i}_{half}", out_dtype=BF16, tn=SHARD_UP, into=g_up,
                      o_shape=(N_CHIPS, D_MODEL, SHARD_UP), o_spec=spec((None, mtm, SHARD_UP), lambda r, j, kk, c0=c0: (c0 + j, r, 0)))
            last = dict(norm_grad=(s["h1"], nw("norm_ffn", i), dh), **fused) if half else dict(tb=True)
            dhf = mm(du, cache["f_w_up", i], name=f"f_up_dx{i}_{half}", n=D_MODEL, tn=D_MODEL, tk=SHARD_UP, add=dhf,
                     b_spec=spec((None, D_MODEL, SHARD_UP), lambda r, j, kk, c0=c0: (c0 + kk, j, 0)), **last)
        gw["f_w_up"] = g_up
        dh, g_ffn[i] = dhf
        token, gw = on_grads(i, "ffn", gw), {}
        w_out = cache["a_w_out" if i % 2 == 0 else "b_w_out", i]
        if token is not None:
            w_out = w_out + token[:1, :1].astype(BF16)

        if i % 2 == 0:
            don = mm(dh, w_out, tb=True, name="a_out_dx")
            gw["a_w_out"] = by_rows(mm(s["on"], dh, ta=True, name="a_out_dw", out_dtype=BF16))
            do, dz, gs["a_norm"] = gnorm_bwd(s["o"], s["pm"], sm["a_norm"], don, name="a_gnorm_bwd")
            dprep = delta_scan_bwd(do, *s["prep"], s["s_in"], name="a_scan_bwd")
            dc, dbg = delta_prep_bwd(s["c"], s["bg"], s["tms"], dprep, name="a_prep_bwd")
            (dpt,), (galog, gdtb) = tile_vjp(_f_betag, [(s["pm"], LANES, 4 * hk // LANES)], [alog_p, dtb_p], [(dbg, LANES, 0)], n_diff=1,
                                             tm=rtm, ncol=1, name="a_betag_bwd", grad_dtypes=[BF16])
            gs["a_log"] = galog[:, N_HEADS_A:2 * N_HEADS_A]
            gs["a_dt_bias"] = gdtb[:, N_HEADS_A:2 * N_HEADS_A]
            dqkv, gs["a_conv"] = conv_bwd(dc, s["pm"], cache["a_conv", i], name="a_conv_bwd")
            dpm = jnp.concatenate([dqkv, dz, dpt], axis=1)
            g_in = mm(s["hn"], dpm, ta=True, name="a_in_dw", out_dtype=BF16)[:, :4 * hk + 2 * N_HEADS_A]
            gw["a_w_in"] = g_in.reshape(D_MODEL, N_CHIPS, g_in.shape[1] // N_CHIPS).transpose(1, 0, 2)
            dh, g_mix[i] = mm(dpm, cache["a_w_in", i], name="a_in_dx", norm_grad=(s["h0"], nw("norm_mix", i), dh), **fused)
        else:
            dao = mm(dh, w_out, tb=True, name="b_out_dx")
            gw["b_w_out"] = by_rows(mm(s["ao"], dh, ta=True, name="b_out_dw", out_dtype=BF16))
            dq, dkd, dvd, gsk = swa_bwd(dao, s["pb"], s["kd"], s["vd"], sinks_p, name="b_att_bwd")
            gs["b_sinks"] = gsk[:, :N_HEADS_B]
            dpb = jnp.concatenate([dq, _undup(dkd), _undup(dvd)], axis=1)
            gw["b_w_in"] = mm(s["hn"], dpb, ta=True, name="b_in_dw", out_dtype=BF16, tn=SHARD_BIN,
                              o_shape=(N_CHIPS, D_MODEL, SHARD_BIN), o_spec=spec((None, mtm, SHARD_BIN), lambda r, j, kk: (j, r, 0)))
            dh, g_mix[i] = mm(dpb, cache["b_w_in", i], name="b_in_dx", norm_grad=(s["h0"], nw("norm_mix", i), dh), **fused,
                              **by_chip_t(D_MODEL, SHARD_BIN))
        token = on_grads(i, "mix", gw)
        if token is not None:
            zero = token[:1, :1]

    gs["norm_mix"], gs["norm_ffn"], gs["norm_ple"] = (jnp.concatenate(g, axis=0) for g in (g_mix, g_ffn, g_ple))
    gs["f_conv"] = jnp.stack(g_conv)
    return loss, dh, gs


BIG = ["a_w_in", "a_w_out", "b_w_in", "b_w_out", "f_w_up", "f_w_down", "ple_w_proj", "ple_w_gate"]
LAYERED = {"f_w_up", "f_w_down", "ple_w_proj", "ple_w_gate"}
BY_CHIP = {"b_w_in", "f_w_up", "ple_w_proj"}
LAYER_UNITS = [[("a_w_in", 0), ("a_w_out", 0)] + [(n, 0) for n in sorted(LAYERED)],
               [("b_w_in", 1), ("b_w_out", 1)] + [(n, 1) for n in sorted(LAYERED)]]
CONVS = ["a_conv", "f_conv"]
SMALL = ["norm_mix", "norm_ffn", "norm_ple", "norm_final", "a_log", "a_dt_bias", "a_norm", "b_sinks"]
SMALL_ROWS = 8
CONV_ROWS = 16
CONV_GRAD_ROWS = 48


def _pack_rows(arrs, rows, dtype):
    flat = jnp.concatenate([a.reshape(-1).astype(dtype) for a in arrs])
    return jnp.pad(flat, (0, rows * PACK_COLS - flat.shape[0])).reshape(rows, PACK_COLS)


def _unpack(flat, shapes):
    out, off = [], 0
    for shp in shapes:
        n = math.prod(shp)
        out.append(flat[off:off + n].reshape(shp))
        off += n
    return out


def _pack_small(d, loss=None):
    tail = jnp.concatenate([d["a_log"].reshape(-1), d["a_dt_bias"].reshape(-1), d["a_norm"].reshape(-1), d["b_sinks"].reshape(-1)])
    if loss is not None:
        tail = jnp.concatenate([tail, loss.reshape(-1)[:1]])
    tail = jnp.pad(tail, (0, PACK_COLS - tail.shape[0]))
    return jnp.concatenate([d["norm_mix"], d["norm_ffn"], d["norm_ple"], d["norm_final"][None, :], tail[None, :]], axis=0)


def _unpack_small(a, like):
    out = {"norm_mix": a[0:2], "norm_ffn": a[2:4], "norm_ple": a[4:6], "norm_final": a[6]}
    off = 0
    for nm in ("a_log", "a_dt_bias", "a_norm", "b_sinks"):
        n = like[nm].size
        out[nm] = a[7, off:off + n].reshape(like[nm].shape)
        off += n
    return out, a[7, off]


def _as2d(a):
    return a.reshape(-1, a.shape[-1])


def kernel(x, p, norm_mix, norm_ffn, norm_ple, norm_final, a_w_in, a_conv, a_log, a_dt_bias, a_norm, a_w_out, b_w_in, b_sinks, b_w_out, f_w_up, f_conv, f_w_down, ple_w_proj, ple_w_gate, loss_target, m_norm_mix, m_norm_ffn, m_norm_ple, m_norm_final, m_a_w_in, m_a_conv, m_a_log, m_a_dt_bias, m_a_norm, m_a_w_out, m_b_w_in, m_b_sinks, m_b_w_out, m_f_w_up, m_f_conv, m_f_w_down, m_ple_w_proj, m_ple_w_gate, v_norm_mix, v_norm_ffn, v_norm_ple, v_norm_final, v_a_w_in, v_a_conv, v_a_log, v_a_dt_bias, v_a_norm, v_a_w_out, v_b_w_in, v_b_sinks, v_b_w_out, v_f_w_up, v_f_conv, v_f_w_down, v_ple_w_proj, v_ple_w_gate):
    w = dict(norm_mix=norm_mix, norm_ffn=norm_ffn, norm_ple=norm_ple, norm_final=norm_final, a_w_in=a_w_in, a_conv=a_conv,
             a_log=a_log, a_dt_bias=a_dt_bias, a_norm=a_norm, a_w_out=a_w_out, b_w_in=b_w_in, b_sinks=b_sinks, b_w_out=b_w_out,
             f_w_up=f_w_up, f_conv=f_conv, f_w_down=f_w_down, ple_w_proj=ple_w_proj, ple_w_gate=ple_w_gate)
    m = dict(norm_mix=m_norm_mix, norm_ffn=m_norm_ffn, norm_ple=m_norm_ple, norm_final=m_norm_final, a_w_in=m_a_w_in,
             a_conv=m_a_conv, a_log=m_a_log, a_dt_bias=m_a_dt_bias, a_norm=m_a_norm, a_w_out=m_a_w_out, b_w_in=m_b_w_in,
             b_sinks=m_b_sinks, b_w_out=m_b_w_out, f_w_up=m_f_w_up, f_conv=m_f_conv, f_w_down=m_f_w_down,
             ple_w_proj=m_ple_w_proj, ple_w_gate=m_ple_w_gate)
    v = dict(norm_mix=v_norm_mix, norm_ffn=v_norm_ffn, norm_ple=v_norm_ple, norm_final=v_norm_final, a_w_in=v_a_w_in,
             a_conv=v_a_conv, a_log=v_a_log, a_dt_bias=v_a_dt_bias, a_norm=v_a_norm, a_w_out=v_a_w_out, b_w_in=v_b_w_in,
             b_sinks=v_b_sinks, b_w_out=v_b_w_out, f_w_up=v_f_w_up, f_conv=v_f_conv, f_w_down=v_f_w_down,
             ple_w_proj=v_ple_w_proj, ple_w_gate=v_ple_w_gate)
    xc, yc, cc = _place()
    my_chip = 2 * xc + yc

    shard = {(n, i): w[n][i if n in LAYERED else 0].astype(BF16) for n, i in LAYER_UNITS[0] + LAYER_UNITS[1]}
    first = shard["a_w_in", 0]
    (ga,) = gather_units([(first, False)], name="gather_first")
    ga = lax.dynamic_update_index_in_dim(ga, first, my_chip, 0)
    a_in = jnp.concatenate([ga[j] for j in range(N_CHIPS)], axis=1)
    n_main = 4 * N_HEADS_A * HEAD_DIM_A
    conv_shapes = [w[n].shape for n in CONVS]
    convs = allgather8(_pack_rows([w[n] for n in CONVS], CONV_ROWS, F32), name="gather_convs")
    conv_parts = [_unpack(convs[2 * j].reshape(-1), conv_shapes) for j in range(N_CHIPS)]
    a_conv_full, f_conv_full = (jnp.concatenate([conv_parts[j][q] for j in range(N_CHIPS)], axis=2) for q in range(2))
    ready = {("a_w_in", 0): jnp.pad(a_in, ((0, 0), (0, n_main + LANES - a_in.shape[1]))), ("a_conv", 0): a_conv_full[0], ("f_conv", 0): f_conv_full[0], ("f_conv", 1): f_conv_full[1]}
    later = [[k for k in units if k != ("a_w_in", 0)] for units in LAYER_UNITS]
    pending, after = [], ga
    for layer, keys in enumerate(later):
        pending.append(gather_start([shard[k] for k in keys], after, name=f"gather_start{layer}"))
        after = pending[-1][4]
    sm = {n: w[n] for n in SMALL}
    sm["norm_mix"] = sm["norm_mix"] + after[:1, :1]

    def weight(name, layer, act):
        if (name, layer) not in ready:
            landed = gather_wait(pending[layer], act, name=f"gather_wait{layer}")
            for k, g in zip(later[layer], landed):
                g = lax.dynamic_update_index_in_dim(g, shard[k], my_chip, 0)
                ready[k] = g if k[0] in BY_CHIP else g.reshape(N_CHIPS * g.shape[1], g.shape[2])
        return ready[name, layer]

    pairs, scattered, started = {}, {}, []

    def on_grads(layer, part, gw):
        keys = [k for k in LAYER_UNITS[layer] if (k[0] in LAYERED) == (part == "ffn")]
        from_sib = swap_units([gw[n] for n, _ in keys], name=f"rs_swap_{part}{layer}")
        for (n, _), sib in zip(keys, from_sib):
            pairs[n, layer] = pair_add(gw[n], sib, name=f"rs_add_pair_{n}{layer}")
        started.append((keys, scatter_start([pairs[k] for k in keys], name=f"rs_scatter_start_{part}{layer}"), f"{part}{layer}"))
        return started[-1][1][4]

    loss, grad_x, gs = local_step(x[0], p[:, 0], loss_target[0], sm, weight, on_grads)

    grads, delta, new_m, new_v, g_unit = {}, {}, {}, {}, {}

    def finish(keys, tag):
        halves = [chips_add(pairs[k], scattered[k], name=f"rs_add_chips_{k[0]}{k[1]}") for k in keys]
        g_unit.update(zip(keys, zip(halves, join_units(halves, name=f"rs_join_{tag}"))))
        for n in BIG:
            mine = [(n, i) for i in range(DEPTH) if (n, i) in LAYER_UNITS[i]]
            if n not in delta and all(k in g_unit for k in mine):
                g_layers = [g_unit[k] for k in mine]
                shape3 = (len(g_layers), 2 * g_layers[0][0].shape[0], g_layers[0][0].shape[1])
                res = adamw_layers(w[n].reshape(shape3), g_layers, m[n].reshape(shape3), v[n].reshape(shape3), name=f"adamw_{n}")
                delta[n], new_m[n], new_v[n], grads[n] = (r.reshape(w[n].shape) for r in res)

    last_keys, last_pending, last_tag = started[-1]
    for keys, pend, tag in started[:-1]:
        scattered.update(zip(keys, scatter_wait(pend, last_pending[4], name=f"rs_scatter_wait_{tag}")))
    finish([k for keys, _, _ in started[:-1] for k in keys], "first")

    conv_grads = _pack_rows([gs[n] for n in CONVS], CONV_GRAD_ROWS, F32)
    small_sum = sum8(allgather8(jnp.concatenate([_pack_small(gs, loss), conv_grads]), name="gather_small"), name="sum_small")
    g_sm, loss_sum = _unpack_small(small_sum[:SMALL_ROWS], sm)

    scattered.update(zip(last_keys, scatter_wait(last_pending, small_sum, name=f"rs_scatter_wait_{last_tag}")))
    finish(last_keys, "last")

    for n, full in zip(CONVS, _unpack(small_sum[SMALL_ROWS:].reshape(-1), [gs[n].shape for n in CONVS])):
        g2 = _as2d(lax.dynamic_slice_in_dim(full, my_chip * w[n].shape[-1], w[n].shape[-1], axis=full.ndim - 1))
        d2, m2, v2 = adamw(_as2d(w[n]), g2, _as2d(m[n]), _as2d(v[n]), name=f"adamw_{n}")
        grads[n], delta[n], new_m[n], new_v[n] = (r.reshape(w[n].shape) for r in (g2, d2, m2, v2))
    pk = lambda d: _pack_small(d)
    d2, m2, v2 = adamw(pk(sm), pk(g_sm), pk({n: m[n] for n in SMALL}), pk({n: v[n] for n in SMALL}), name="adamw_small")
    for src, dst in ((d2, delta), (m2, new_m), (v2, new_v)):
        dst.update(_unpack_small(src, sm)[0])
    grads.update(g_sm)

    order = ["norm_mix", "norm_ffn", "norm_ple", "norm_final", "a_w_in", "a_conv", "a_log", "a_dt_bias", "a_norm", "a_w_out",
             "b_w_in", "b_sinks", "b_w_out", "f_w_up", "f_conv", "f_w_down", "ple_w_proj", "ple_w_gate"]
    return (loss_sum, grad_x[None], *[grads[n] for n in order], *[delta[n] for n in order],
            *[new_m[n] for n in order], *[new_v[n] for n in order])
```

```python
import functools
import math

import jax
import jax.numpy as jnp
from jax import lax
from jax.experimental import pallas as pl
from jax.experimental.pallas import tpu as pltpu

F32 = jnp.float32
BF16 = jnp.bfloat16
MESH = pl.DeviceIdType.MESH

D_MODEL = 1024
N_HEADS_A = 8
HEAD_DIM_A = 128
CONV_A = 4
N_HEADS_B = 16
N_KV_B = 4
HEAD_DIM_B = 64
WINDOW = 128
D_FF = 2816
FFN_CONV = 3
PLE_DIM = 256
EPS = 1e-6
DEPTH = 2

ADAM_LR = 0.001
ADAM_B1 = 0.9
ADAM_B2 = 0.999
ADAM_EPS = 1e-08
ADAM_WD = 0.01
ADAM_STEP = 10

LANES = 128
SUBLANES = 8
BF16_ROWS = 16
CHUNK = 128
VMEM_LIMIT = 56 * 1024 * 1024
NEG = -1e30
N_CHIPS = 4
N_DEV = 8
PACK_COLS = 1024


def _params(sem=None):
    return pltpu.CompilerParams(dimension_semantics=sem, vmem_limit_bytes=VMEM_LIMIT)


def _tile(dim, cap):
    if dim % LANES:
        return dim
    best = LANES
    for t in range(LANES, min(dim, cap) + 1, LANES):
        if dim % t == 0:
            best = t
    return best


def _dot(a, b, dims=(((1,), (0,)), ((), ())), precision=None):
    return lax.dot_general(a, b, dims, precision=precision, preferred_element_type=F32)


NN = (((1,), (0,)), ((), ()))
NT = (((1,), (1,)), ((), ()))
TN = (((0,), (0,)), ((), ()))


MM_TM_CAP = 1024
MM_TK_CAP_TOKENS = 2048


def mm(a, b, *, name, ta=False, tb=False, out_dtype=F32, add=None, norm_w=None, norm_grad=None, tm_cap=MM_TM_CAP, tn_cap=1408,
       tk_cap=1408, n=None, tn=None, tk=None, b_spec=None, o_spec=None, o_shape=None, into=None):
    m, k = (a.shape[1], a.shape[0]) if ta else a.shape
    if b_spec is None:
        n = b.shape[0] if tb else b.shape[1]
        assert (b.shape[1] if tb else b.shape[0]) == k, (a.shape, b.shape, ta, tb)
    tm, tn, tk = _tile(m, tm_cap), tn or _tile(n, tn_cap), tk or _tile(k, MM_TK_CAP_TOKENS if ta else tk_cap)
    assert n % tn == 0 and k % tk == 0, (n, tn, k, tk)
    nk = k // tk
    dims = (((0 if ta else 1,), (1 if tb else 0,)), ((), ()))
    has_add, has_norm, has_grad = add is not None, norm_w is not None, norm_grad is not None
    assert not (has_norm or has_grad) or (tn == n and o_spec is None), "the norm epilogues need whole rows"
    n_in = 2 + has_add + has_norm + 3 * has_grad + (into is not None)

    def body(*refs):
        a_ref, b_ref = refs[0], refs[1]
        add_ref = refs[2] if has_add else None
        o_ref = refs[n_in]
        part = _dot(a_ref[...].astype(BF16), b_ref[...].astype(BF16), dims)
        first = pl.program_id(0) == 0

        def finish(r):
            if has_add:
                r = r + add_ref[...].astype(F32)
            if has_grad:
                h_ref, w_ref, prev_ref = refs[2 + has_add:5 + has_add]
                _, vjp = jax.vjp(_f_norm, h_ref[...], w_ref[...])
                r, dw = vjp(r)
                r = r + prev_ref[...]

                @pl.when(first)
                def _():
                    refs[n_in + 1][...] = dw

                @pl.when(jnp.logical_not(first))
                def _():
                    refs[n_in + 1][...] += dw
            o_ref[...] = r.astype(o_ref.dtype)
            if has_norm:
                refs[n_in + 1][...] = _f_norm(r, refs[2 + has_add][...]).astype(BF16)

        if nk == 1:
            finish(part)
            return
        acc = refs[-1]
        kk = pl.program_id(2)

        @pl.when(kk == 0)
        def _():
            acc[...] = part

        @pl.when(kk > 0)
        def _():
            acc[...] += part

        @pl.when(kk == nk - 1)
        def _():
            finish(acc[...])

    a_spec = pl.BlockSpec((tk, tm), lambda i, j, kk: (kk, i)) if ta else pl.BlockSpec((tm, tk), lambda i, j, kk: (i, kk))
    if b_spec is None:
        b_spec = pl.BlockSpec((tn, tk), lambda i, j, kk: (j, kk)) if tb else pl.BlockSpec((tk, tn), lambda i, j, kk: (kk, j))
    plain_o = pl.BlockSpec((tm, tn), lambda i, j, kk: (i, j))
    if o_spec is None:
        o_spec, o_shape = plain_o, (m, n)
    in_specs = [a_spec, b_spec] + ([plain_o] if has_add else [])
    args = (a, b) + ((add,) if has_add else ())
    out_specs, out_shapes = o_spec, jax.ShapeDtypeStruct(tuple(o_shape), out_dtype)
    one_row = pl.BlockSpec((1, n), lambda i, j, kk: (0, 0))
    if has_norm:
        in_specs.append(one_row)
        args += (norm_w,)
        out_specs, out_shapes = [o_spec, plain_o], [out_shapes, jax.ShapeDtypeStruct((m, n), BF16)]
    if has_grad:
        assert not has_norm
        in_specs += [plain_o, one_row, plain_o]
        args += tuple(norm_grad)
        out_specs, out_shapes = [o_spec, one_row], [out_shapes, jax.ShapeDtypeStruct((1, n), F32)]
    aliases = {}
    if into is not None:
        assert into.shape == tuple(o_shape) and into.dtype == out_dtype, (into.shape, o_shape)
        in_specs.append(pl.BlockSpec(memory_space=pl.ANY))
        args += (into,)
        aliases = {n_in - 1: 0}
    return pl.pallas_call(
        body, grid=(m // tm, n // tn, nk), in_specs=in_specs, out_specs=out_specs,
        out_shape=out_shapes, name=name, input_output_aliases=aliases,
        scratch_shapes=[pltpu.VMEM((tm, tn), F32)] if nk > 1 else [],
        compiler_params=_params(("arbitrary" if has_grad else "parallel", "parallel", "arbitrary")),
    )(*args)


def _row_spec(tm, cw, coff):
    return pl.BlockSpec((tm, cw), lambda i, j: (i, j + coff))


def _full_spec(shape):
    return pl.BlockSpec(shape, lambda i, j: (0,) * len(shape))


def tile_map(fn, rows, params, outs, *, tm, ncol, name):
    t = rows[0][0].shape[0]
    nin = len(rows) + len(params)

    def body(*refs):
        res = fn(*[r[...] for r in refs[:nin]])
        res = res if isinstance(res, (tuple, list)) else (res,)
        for o_ref, r in zip(refs[nin:], res):
            o_ref[...] = r.astype(o_ref.dtype)

    in_specs = [_row_spec(tm, cw, coff) for (_, cw, coff) in rows] + [_full_spec(p.shape) for p in params]
    res = pl.pallas_call(
        body, grid=(t // tm, ncol), in_specs=in_specs,
        out_specs=[_row_spec(tm, cw, 0) for (cw, _) in outs],
        out_shape=[jax.ShapeDtypeStruct((t, cw * ncol), dt) for (cw, dt) in outs], name=name,
        compiler_params=_params(("parallel", "parallel")),
    )(*[r[0] for r in rows], *params)
    return res


def tile_vjp(fn, rows, params, cts, *, n_diff, tm, ncol, name, grad_dtypes=None):
    t = rows[0][0].shape[0]
    nr, npar, nct = len(rows), len(params), len(cts)

    def body(*refs):
        vals = [r[...] for r in refs[:nr + npar + nct]]
        diff, rest, pars = vals[:n_diff], vals[n_diff:nr], vals[nr:nr + npar]
        ctv = vals[nr + npar:nr + npar + nct]
        outs_ref = refs[nr + npar + nct:]

        def f(*a):
            res = fn(*a[:n_diff], *rest, *a[n_diff:])
            return tuple(res) if isinstance(res, (tuple, list)) else (res,)

        primal, vjp = jax.vjp(f, *[d.astype(F32) for d in diff], *pars)
        grads = vjp(tuple(c.astype(o.dtype) for c, o in zip(ctv, primal)))
        for q in range(n_diff):
            outs_ref[q][...] = grads[q].astype(outs_ref[q].dtype)
        first = (pl.program_id(0) == 0) & (pl.program_id(1) == 0)
        for q in range(npar):
            o_ref, g = outs_ref[n_diff + q], grads[n_diff + q]

            @pl.when(first)
            def _(o_ref=o_ref, g=g):
                o_ref[...] = g

            @pl.when(jnp.logical_not(first))
            def _(o_ref=o_ref, g=g):
                o_ref[...] += g

    in_specs = [_row_spec(tm, cw, coff) for (_, cw, coff) in rows] + [_full_spec(p.shape) for p in params]
    in_specs += [_row_spec(tm, cw, coff) for (_, cw, coff) in cts]
    args = [r[0] for r in rows] + list(params) + [c[0] for c in cts]
    out_specs = [_row_spec(tm, rows[q][1], 0) for q in range(n_diff)] + [_full_spec(p.shape) for p in params]
    grad_dtypes = grad_dtypes or [F32] * n_diff
    out_shape = [jax.ShapeDtypeStruct((t, rows[q][1] * ncol), grad_dtypes[q]) for q in range(n_diff)]
    out_shape += [jax.ShapeDtypeStruct(p.shape, F32) for p in params]
    res = pl.pallas_call(
        body, grid=(t // tm, ncol), in_specs=in_specs, out_specs=out_specs, out_shape=out_shape, name=name,
        compiler_params=_params(("arbitrary", "arbitrary")),
    )(*args)
    return res[:n_diff], res[n_diff:]


def _silu(x):
    return x * jax.nn.sigmoid(x)


def _f_norm(h, w):
    return h * lax.rsqrt(jnp.mean(h * h, axis=-1, keepdims=True) + EPS) * w


def _f_gnorm(o, z, w):
    return _f_norm(o, w) * _silu(z)


def _f_act(gate, val):
    return _silu(gate) * val


def _f_ple(gl, pe):
    return jax.nn.sigmoid(gl) * pe


def _f_betag(pt, alog, dtb):
    lane = lax.broadcasted_iota(jnp.int32, (1, LANES), 1)
    z = pt + dtb
    softplus = jnp.maximum(z, 0.0) + jnp.log(1.0 + jnp.exp(-jnp.abs(z)))
    g = -jnp.exp(alog) * softplus
    return jnp.where(lane < N_HEADS_A, jax.nn.sigmoid(pt), jnp.where(lane < 2 * N_HEADS_A, g, 0.0))


CONV_TM = 256
CONV_CW = 1024


def _shift_down(x, prev, s, row):
    rp = jnp.tile(pltpu.roll(prev, s, 0), (x.shape[0] // SUBLANES, 1))
    return jnp.where(row < s, rp, pltpu.roll(x, s, 0))


def _shift_up(x, nxt, s, row):
    tm = x.shape[0]
    rn = jnp.tile(pltpu.roll(nxt, SUBLANES - s, 0), (tm // SUBLANES, 1))
    return jnp.where(row >= tm - s, rn, pltpu.roll(x, tm - s, 0))


def _conv_taps(x, prev, w_ref, cols, row):
    k = w_ref.shape[0]
    y = x * w_ref[pl.ds(k - 1, 1), cols]
    for s in range(1, k):
        y = y + _shift_down(x, prev, s, row) * w_ref[pl.ds(k - 1 - s, 1), cols]
    return y


def _lane_chunks(cw):
    return [slice(cb * LANES, (cb + 1) * LANES) for cb in range(cw // LANES)]


def conv_fwd(x, w, *, name):
    t = x.shape[0]
    k, c = w.shape
    tm, cw = min(CONV_TM, t), CONV_CW
    nb8 = tm // SUBLANES

    def body(x_ref, p_ref, w_ref, o_ref):
        first = pl.program_id(1) == 0
        row = lax.broadcasted_iota(jnp.int32, (tm, LANES), 0)
        for cols in _lane_chunks(cw):
            o_ref[:, cols] = _conv_taps(x_ref[:, cols], jnp.where(first, 0.0, p_ref[:, cols]), w_ref, cols, row)

    return pl.pallas_call(
        body, grid=(c // cw, t // tm),
        in_specs=[pl.BlockSpec((tm, cw), lambda j, i: (i, j)),
                  pl.BlockSpec((SUBLANES, cw), lambda j, i: (jnp.maximum(i * nb8 - 1, 0), j)),
                  pl.BlockSpec((k, cw), lambda j, i: (0, j))],
        out_specs=pl.BlockSpec((tm, cw), lambda j, i: (i, j)),
        out_shape=jax.ShapeDtypeStruct((t, c), F32), name=name,
        compiler_params=_params(("parallel", "parallel")),
    )(x, x, w)


def conv_bwd(dy, x, w, *, name):
    t = x.shape[0]
    k, c = w.shape
    tm, cw = min(CONV_TM, t), CONV_CW
    nb8 = tm // SUBLANES
    ni = t // tm

    def body(dy_ref, dn_ref, x_ref, p_ref, w_ref, dx_ref, dw_ref):
        i = pl.program_id(1)
        first, last = i == 0, i == ni - 1
        row = lax.broadcasted_iota(jnp.int32, (tm, LANES), 0)
        for cols in _lane_chunks(cw):
            dyv, xv = dy_ref[:, cols], x_ref[:, cols]
            nxt = jnp.where(last, 0.0, dn_ref[:, cols])
            prev = jnp.where(first, 0.0, p_ref[:, cols])
            dx = dyv * w_ref[pl.ds(k - 1, 1), cols]
            dws = [jnp.sum(dyv * xv, axis=0, keepdims=True)]
            for s in range(1, k):
                dx = dx + _shift_up(dyv, nxt, s, row) * w_ref[pl.ds(k - 1 - s, 1), cols]
                dws.append(jnp.sum(dyv * _shift_down(xv, prev, s, row), axis=0, keepdims=True))
            dx_ref[:, cols] = dx.astype(dx_ref.dtype)
            for s in range(k):
                @pl.when(first)
                def _(s=s, dws=dws, cols=cols):
                    dw_ref[pl.ds(k - 1 - s, 1), cols] = dws[s]

                @pl.when(jnp.logical_not(first))
                def _(s=s, dws=dws, cols=cols):
                    dw_ref[pl.ds(k - 1 - s, 1), cols] += dws[s]

    return pl.pallas_call(
        body, grid=(c // cw, ni),
        in_specs=[pl.BlockSpec((tm, cw), lambda j, i: (i, j)),
                  pl.BlockSpec((SUBLANES, cw), lambda j, i: (jnp.minimum((i + 1) * nb8, t // SUBLANES - 1), j)),
                  pl.BlockSpec((tm, cw), lambda j, i: (i, j)),
                  pl.BlockSpec((SUBLANES, cw), lambda j, i: (jnp.maximum(i * nb8 - 1, 0), j)),
                  pl.BlockSpec((k, cw), lambda j, i: (0, j))],
        out_specs=[pl.BlockSpec((tm, cw), lambda j, i: (i, j)), pl.BlockSpec((k, cw), lambda j, i: (0, j))],
        out_shape=[jax.ShapeDtypeStruct((t, c), BF16), jax.ShapeDtypeStruct((k, c), F32)], name=name,
        compiler_params=_params(("parallel", "arbitrary")),
    )(dy, dy, x, x, w)


FFN_TM = 128
FFN_CW = D_FF // 2


def _ffn_specs(t, tm, cw, k):
    ncol = D_FF // cw
    cur = lambda off: pl.BlockSpec((tm, cw), lambda j, i: (i, j + off))
    prev = lambda off, hr: pl.BlockSpec((hr, cw), lambda j, i: (jnp.maximum(i * (tm // hr) - 1, 0), j + off))
    nxt = lambda off, hr: pl.BlockSpec((hr, cw), lambda j, i: (jnp.minimum((i + 1) * (tm // hr), t // hr - 1), j + off))
    taps = lambda off: pl.BlockSpec((k, cw), lambda j, i: (0, j + off))
    return cur, prev, nxt, taps, ncol


def _rows_before(ref, cols, first):
    return jnp.where(first, 0.0, ref[ref.shape[0] - SUBLANES:, cols].astype(F32))


def conv_act_fwd(u, w, *, name):
    t, k = u.shape[0], w.shape[0]
    tm, cw = min(FFN_TM, t), FFN_CW
    cur, prev, _, taps, ncol = _ffn_specs(t, tm, cw, k)

    def body(ug_ref, pg_ref, uv_ref, pv_ref, wg_ref, wv_ref, o_ref):
        first = pl.program_id(1) == 0
        row = lax.broadcasted_iota(jnp.int32, (tm, LANES), 0)
        for cb in range(cw // LANES):
            cols = slice(cb * LANES, (cb + 1) * LANES)
            cg = _conv_taps(ug_ref[:, cols].astype(F32), _rows_before(pg_ref, cols, first), wg_ref, cols, row)
            cv = _conv_taps(uv_ref[:, cols].astype(F32), _rows_before(pv_ref, cols, first), wv_ref, cols, row)
            o_ref[:, cols] = _f_act(cg, cv).astype(o_ref.dtype)

    return pl.pallas_call(
        body, grid=(ncol, t // tm),
        in_specs=[cur(0), prev(0, BF16_ROWS), cur(ncol), prev(ncol, BF16_ROWS), taps(0), taps(ncol)],
        out_specs=cur(0), out_shape=jax.ShapeDtypeStruct((t, D_FF), BF16), name=name,
        compiler_params=_params(("parallel", "parallel")),
    )(u, u, u, u, w, w)


def conv_act_bwd(u, dact, w, *, name):
    t, k = u.shape[0], w.shape[0]
    tm, cw = min(FFN_TM, t), FFN_CW
    cur, prev, nxt, taps, ncol = _ffn_specs(t, tm, cw, k)
    ni = t // tm

    def body(ug_ref, pg_ref, ng_ref, uv_ref, pv_ref, nv_ref, d_ref, dn_ref, wg_ref, wv_ref, dg_ref, dv_ref, dwg_ref, dwv_ref):
        i = pl.program_id(1)
        first, last = i == 0, i == ni - 1
        row = lax.broadcasted_iota(jnp.int32, (tm, LANES), 0)
        row8 = lax.broadcasted_iota(jnp.int32, (SUBLANES, LANES), 0)
        for cb in range(cw // LANES):
            cols = slice(cb * LANES, (cb + 1) * LANES)
            ug, uv = ug_ref[:, cols].astype(F32), uv_ref[:, cols].astype(F32)
            pg, pv = _rows_before(pg_ref, cols, first), _rows_before(pv_ref, cols, first)
            sg = [ug] + [_shift_down(ug, pg, s, row) for s in range(1, k)]
            sv = [uv] + [_shift_down(uv, pv, s, row) for s in range(1, k)]
            taps = lambda xs, w_ref: sum(xs[s] * w_ref[pl.ds(k - 1 - s, 1), cols] for s in range(k))
            _, vjp = jax.vjp(_f_act, taps(sg, wg_ref), taps(sv, wv_ref))
            dcg, dcv = vjp(d_ref[:, cols])
            after = lambda ref: ref[:SUBLANES, cols].astype(F32)
            _, vjp_n = jax.vjp(_f_act, _conv_taps(after(ng_ref), ug[tm - SUBLANES:], wg_ref, cols, row8),
                               _conv_taps(after(nv_ref), uv[tm - SUBLANES:], wv_ref, cols, row8))
            dcgn, dcvn = vjp_n(jnp.where(last, 0.0, dn_ref[:, cols]))
            for dc, dcn, xs, w_ref, dx_ref, dw_ref in ((dcg, dcgn, sg, wg_ref, dg_ref, dwg_ref),
                                                       (dcv, dcvn, sv, wv_ref, dv_ref, dwv_ref)):
                dx = dc * w_ref[pl.ds(k - 1, 1), cols]
                dws = [jnp.sum(dc * xs[0], axis=0, keepdims=True)]
                for s in range(1, k):
                    dx = dx + _shift_up(dc, dcn, s, row) * w_ref[pl.ds(k - 1 - s, 1), cols]
                    dws.append(jnp.sum(dc * xs[s], axis=0, keepdims=True))
                dx_ref[:, cols] = dx.astype(dx_ref.dtype)
                for s in range(k):
                    @pl.when(first)
                    def _(s=s, dw_ref=dw_ref, dws=dws):
                        dw_ref[pl.ds(k - 1 - s, 1), cols] = dws[s]

                    @pl.when(jnp.logical_not(first))
                    def _(s=s, dw_ref=dw_ref, dws=dws):
                        dw_ref[pl.ds(k - 1 - s, 1), cols] += dws[s]

    half = jax.ShapeDtypeStruct((t, D_FF), BF16)
    dwh = jax.ShapeDtypeStruct((k, D_FF), F32)
    return pl.pallas_call(
        body, grid=(ncol, ni),
        in_specs=[cur(0), prev(0, BF16_ROWS), nxt(0, BF16_ROWS), cur(ncol), prev(ncol, BF16_ROWS), nxt(ncol, BF16_ROWS),
                  cur(0), nxt(0, SUBLANES), taps(0), taps(ncol)],
        out_specs=[cur(0), cur(0), taps(0), taps(0)], out_shape=[half, half, dwh, dwh], name=name,
        compiler_params=_params(("parallel", "arbitrary")),
    )(u, u, u, u, u, u, dact, dact, w, w)


def _each(f, *lists):
    return [f(*a) for a in zip(*lists)]


@jax.custom_vjp
def _inv_unit_lower(lms):
    return _inv_blocks(lms)


def _inv_blocks(lms):
    c = lms[0].shape[0]
    ri = lax.broadcasted_iota(jnp.int32, (c, c), 0)
    ci = lax.broadcasted_iota(jnp.int32, (c, c), 1)
    eye = (ri == ci).astype(F32)
    dms = _each(lambda lm: eye - jnp.where((ri >> 1) == (ci >> 1), lm, 0.0), lms)
    for lv in range(1, int(math.log2(c))):
        below = ((ri >> (lv + 1)) == (ci >> (lv + 1))) & ((ri >> lv) != (ci >> lv))
        dbs = _each(lambda dm: dm.astype(BF16), dms)
        ods = _each(lambda lm, db: _dot(jnp.where(below, lm, 0.0).astype(BF16), db).astype(BF16), lms, dbs)
        dms = _each(lambda dm, db, od: dm - _dot(db, od), dms, dbs, ods)
    return dms


def _inv_fwd(lms):
    tms = _inv_blocks(lms)
    return tms, tms


def _inv_bwd(tms, dts):
    tbs = _each(lambda tm: tm.astype(BF16), tms)
    mid = _each(lambda tb, dt: _dot(tb, dt.astype(BF16), TN).astype(BF16), tbs, dts)
    return (_each(lambda m, tb: -_dot(m, tb, NT), mid, tbs),)


_inv_unit_lower.defvjp(_inv_fwd, _inv_bwd)


@jax.custom_vjp
def _inv_known(lms, tms):
    return tms


_inv_known.defvjp(lambda lms, tms: (tms, tms), lambda tms, dts: _inv_bwd(tms, dts) + (_each(jnp.zeros_like, tms),))


def _l2n(x):
    return x * lax.rsqrt(jnp.sum(x * x, axis=-1, keepdims=True) + EPS)


def _prep_fn(cqs, cks, cvs, bg, sel_b, sel_g, tms=None):
    c = cqs[0].shape[0]
    ri = lax.broadcasted_iota(jnp.int32, (c, c), 0)
    ci = lax.broadcasted_iota(jnp.int32, (c, c), 1)
    eye = (ri == ci).astype(F32)
    incl, strict = ci <= ri, ci < ri
    last = lax.broadcasted_iota(jnp.int32, (c, 1), 0) == c - 1
    to_row = lambda col: jnp.sum(col * eye, axis=0, keepdims=True)
    qs = _each(lambda a: _l2n(_silu(a)) * (HEAD_DIM_A ** -0.5), cqs)
    ks = _each(lambda a: _l2n(_silu(a)), cks)
    vbs = _each(lambda a: _silu(a).astype(BF16), cvs)
    betas = _each(lambda m: jnp.sum(bg * m, axis=1, keepdims=True), sel_b)
    gs = _each(lambda m: jnp.sum(bg * m, axis=1, keepdims=True), sel_g)
    gcss = _each(lambda g: jnp.sum(jnp.where(incl, to_row(g), 0.0), axis=1, keepdims=True), gs)
    gtots = _each(lambda gcs: jnp.sum(jnp.where(last, gcs, 0.0), axis=0, keepdims=True), gcss)
    decays = _each(lambda gcs: jnp.exp(jnp.where(incl, gcs - to_row(gcs), NEG)), gcss)
    kbs = _each(lambda k: k.astype(BF16), ks)
    lms = _each(lambda beta, kb, dec: jnp.where(strict, beta * _dot(kb, kb, NT) * dec, 0.0), betas, kbs, decays)
    tms = _inv_unit_lower(lms) if tms is None else _inv_known(lms, tms)
    ams = _each(lambda tm, beta: (tm * to_row(beta)).astype(BF16), tms, betas)
    gams = _each(jnp.exp, gcss)
    u0s = _each(_dot, ams, vbs)
    wks = _each(lambda am, gam, k: _dot(am, (gam * k).astype(BF16)), ams, gams, ks)
    qks = _each(lambda q, kb, dec: _dot(q.astype(BF16), kb, NT) * dec, qs, kbs, decays)
    qds = _each(lambda q, gam: q * gam, qs, gams)
    kds = _each(lambda k, gtot, gcs: k * jnp.exp(gtot - gcs), ks, gtots, gcss)
    gls = _each(lambda gtot: jnp.exp(gtot) * jnp.ones((SUBLANES, LANES), F32), gtots)
    return u0s, wks, qds, kds, qks, gls, tms


def _head_masks(h):
    lane = lax.broadcasted_iota(jnp.int32, (1, LANES), 1)
    return (lane == h).astype(F32), (lane == h + N_HEADS_A).astype(F32)


def _hsl(j):
    return slice(j * HEAD_DIM_A, (j + 1) * HEAD_DIM_A)


def gnorm_fwd(o, zsrc, w, *, name):
    t, width = o.shape
    tm = min(256, t)
    zoff = zsrc.shape[1] // width - 1

    def body(o_ref, z_ref, w_ref, out_ref):
        for h in range(N_HEADS_A):
            out_ref[:, _hsl(h)] = _f_gnorm(o_ref[:, _hsl(h)], z_ref[:, _hsl(h)], w_ref[...]).astype(out_ref.dtype)

    rows = pl.BlockSpec((tm, width), lambda i: (i, 0))
    return pl.pallas_call(
        body, grid=(t // tm,),
        in_specs=[rows, pl.BlockSpec((tm, width), lambda i: (i, zoff)), pl.BlockSpec(w.shape, lambda i: (0, 0))],
        out_specs=rows, out_shape=jax.ShapeDtypeStruct((t, width), BF16), name=name, compiler_params=_params(("parallel",)),
    )(o, zsrc, w)


def gnorm_bwd(o, zsrc, w, don, *, name):
    t, width = o.shape
    tm = min(256, t)
    zoff = zsrc.shape[1] // width - 1

    def body(o_ref, z_ref, w_ref, d_ref, do_ref, dz_ref, dw_ref):
        dw = jnp.zeros(w.shape, F32)
        for h in range(N_HEADS_A):
            _, vjp = jax.vjp(_f_gnorm, o_ref[:, _hsl(h)], z_ref[:, _hsl(h)], w_ref[...])
            do, dz, dwh = vjp(d_ref[:, _hsl(h)])
            do_ref[:, _hsl(h)] = do.astype(do_ref.dtype)
            dz_ref[:, _hsl(h)] = dz.astype(dz_ref.dtype)
            dw = dw + dwh
        first = pl.program_id(0) == 0

        @pl.when(first)
        def _():
            dw_ref[...] = dw

        @pl.when(jnp.logical_not(first))
        def _():
            dw_ref[...] += dw

    rows = pl.BlockSpec((tm, width), lambda i: (i, 0))
    wspec = pl.BlockSpec(w.shape, lambda i: (0, 0))
    return pl.pallas_call(
        body, grid=(t // tm,),
        in_specs=[rows, pl.BlockSpec((tm, width), lambda i: (i, zoff)), wspec, rows],
        out_specs=[rows, rows, wspec],
        out_shape=[jax.ShapeDtypeStruct((t, width), BF16)] * 2 + [jax.ShapeDtypeStruct(w.shape, F32)], name=name,
        compiler_params=_params(("arbitrary",)),
    )(o, zsrc, w, don)


def delta_prep(cqkv, bg, *, name):
    t = cqkv.shape[0]
    nh, hd, n = N_HEADS_A, HEAD_DIM_A, t // CHUNK

    def body(cq_ref, ck_ref, cv_ref, bg_ref, u0_ref, wk_ref, qd_ref, kd_ref, qk_ref, tm_ref, gl_ref):
        heads = range(nh)
        masks = [_head_masks(j) for j in heads]
        res = _prep_fn([cq_ref[:, _hsl(j)] for j in heads], [ck_ref[:, _hsl(j)] for j in heads],
                       [cv_ref[:, _hsl(j)] for j in heads], bg_ref[...], [m[0] for m in masks], [m[1] for m in masks])
        for o_ref, rs in zip((u0_ref, wk_ref, qd_ref, kd_ref, qk_ref, tm_ref), res[:5] + (res[6],)):
            for j in heads:
                o_ref[:, _hsl(j)] = rs[j].astype(o_ref.dtype)
        for j in heads:
            gl_ref[j * SUBLANES:(j + 1) * SUBLANES, :] = res[5][j]

    blk = lambda off: pl.BlockSpec((CHUNK, nh * hd), lambda i: (i, off))
    res = pl.pallas_call(
        body, grid=(n,),
        in_specs=[blk(0), blk(1), blk(2), pl.BlockSpec((CHUNK, LANES), lambda i: (i, 0))],
        out_specs=[blk(0)] * 6 + [pl.BlockSpec((nh * SUBLANES, LANES), lambda i: (i, 0))],
        out_shape=[jax.ShapeDtypeStruct((t, nh * hd), dt) for dt in (F32, BF16, BF16, BF16, BF16, F32)]
        + [jax.ShapeDtypeStruct((n * nh * SUBLANES, LANES), F32)],
        name=name, compiler_params=_params(("parallel",)),
    )(cqkv, cqkv, cqkv, bg)
    return [*res[:5], res[6]], res[5]


def delta_prep_bwd(cqkv, bg, tms, cts, *, name):
    t = cqkv.shape[0]
    nh, hd, n = N_HEADS_A, HEAD_DIM_A, t // CHUNK

    def body(cq_ref, ck_ref, cv_ref, bg_ref, tm_ref, c0, c1, c2, c3, c4, c5, dc_ref, dbg_ref):
        heads = range(nh)
        masks = [_head_masks(j) for j in heads]
        known = [tm_ref[:, _hsl(j)] for j in heads]
        _, vjp = jax.vjp(lambda a, b, c, d: _prep_fn(a, b, c, d, [m[0] for m in masks], [m[1] for m in masks], known)[:6],
                         [cq_ref[:, _hsl(j)] for j in heads], [ck_ref[:, _hsl(j)] for j in heads],
                         [cv_ref[:, _hsl(j)] for j in heads], bg_ref[...])
        cts = tuple([c[:, _hsl(j)] for j in heads] for c in (c0, c1, c2, c3, c4))
        dqs, dks, dvs, dbg = vjp(cts + ([c5[j * SUBLANES:(j + 1) * SUBLANES, :] for j in heads],))
        for part, ds in enumerate((dqs, dks, dvs)):
            for j in heads:
                dc_ref[:, _hsl(part * nh + j)] = ds[j]
        dbg_ref[...] = dbg

    blk = lambda off: pl.BlockSpec((CHUNK, nh * hd), lambda i: (i, off))
    gl_spec = pl.BlockSpec((nh * SUBLANES, LANES), lambda i: (i, 0))
    bg_spec = pl.BlockSpec((CHUNK, LANES), lambda i: (i, 0))
    return pl.pallas_call(
        body, grid=(n,),
        in_specs=[blk(0), blk(1), blk(2), bg_spec] + [blk(0)] * 6 + [gl_spec],
        out_specs=[pl.BlockSpec((CHUNK, 3 * nh * hd), lambda i: (i, 0)), bg_spec],
        out_shape=[jax.ShapeDtypeStruct((t, 3 * nh * hd), F32), jax.ShapeDtypeStruct((t, LANES), F32)],
        name=name, compiler_params=_params(("parallel",)),
    )(cqkv, cqkv, cqkv, bg, tms, *cts)


def delta_scan(u0, wk, qd, kd, qk, gl, *, name):
    t = u0.shape[0]
    nh, hd, n = N_HEADS_A, HEAD_DIM_A, t // CHUNK

    def body(u0_ref, wk_ref, qd_ref, kd_ref, qk_ref, gl_ref, o_ref, sin_ref, s_ref):
        @pl.when(pl.program_id(0) == 0)
        def _():
            s_ref[...] = jnp.zeros_like(s_ref)

        heads = list(range(nh))
        cols = lambda ref: [ref[:, _hsl(h)].astype(BF16) for h in heads]
        ss = [s_ref[h] for h in heads]
        for h in heads:
            sin_ref[h] = ss[h]
        sbs = _each(lambda s: s.astype(BF16), ss)
        ubs = _each(lambda h, wkb, sb: (u0_ref[:, _hsl(h)] - _dot(wkb, sb)).astype(BF16), heads, cols(wk_ref), sbs)
        os_ = _each(lambda qdb, sb, qkb, ub: _dot(qdb, sb) + _dot(qkb, ub), cols(qd_ref), sbs, cols(qk_ref), ubs)
        sn = _each(lambda h, s, kdb, ub: gl_ref[pl.ds(h * SUBLANES, 1), :] * s + _dot(kdb, ub, TN), heads, ss, cols(kd_ref), ubs)
        for h in heads:
            o_ref[:, _hsl(h)] = os_[h]
            s_ref[h] = sn[h]

    blk = pl.BlockSpec((CHUNK, nh * hd), lambda i: (i, 0))
    return pl.pallas_call(
        body, grid=(n,),
        in_specs=[blk] * 5 + [pl.BlockSpec((nh * SUBLANES, LANES), lambda i: (i, 0))],
        out_specs=[blk, pl.BlockSpec((None, nh, hd, hd), lambda i: (i, 0, 0, 0))],
        out_shape=[jax.ShapeDtypeStruct((t, nh * hd), F32), jax.ShapeDtypeStruct((n, nh, hd, hd), F32)],
        scratch_shapes=[pltpu.VMEM((nh, hd, hd), F32)], name=name,
        compiler_params=_params(("arbitrary",)),
    )(u0, wk, qd, kd, qk, gl)


def delta_scan_bwd(do, u0, wk, qd, kd, qk, gl, s_in, *, name):
    t = u0.shape[0]
    nh, hd, n = N_HEADS_A, HEAD_DIM_A, t // CHUNK

    def body(do_ref, u0_ref, wk_ref, qd_ref, kd_ref, qk_ref, gl_ref, sin_ref,
             du0_ref, dwk_ref, dqd_ref, dkd_ref, dqk_ref, dgl_ref, ds_ref):
        @pl.when(pl.program_id(0) == 0)
        def _():
            ds_ref[...] = jnp.zeros_like(ds_ref)

        corner = (lax.broadcasted_iota(jnp.int32, (SUBLANES, LANES), 0) == 0) & (lax.broadcasted_iota(jnp.int32, (SUBLANES, LANES), 1) == 0)
        heads = list(range(nh))
        cols = lambda ref: [ref[:, _hsl(h)].astype(BF16) for h in heads]
        ss, dss = [sin_ref[h] for h in heads], [ds_ref[h] for h in heads]
        sbs, dsbs = _each(lambda s: s.astype(BF16), ss), _each(lambda d: d.astype(BF16), dss)
        dobs, wkbs, qdbs, kdbs, qkbs = cols(do_ref), cols(wk_ref), cols(qd_ref), cols(kd_ref), cols(qk_ref)
        ubs = _each(lambda h, wkb, sb: (u0_ref[:, _hsl(h)] - _dot(wkb, sb)).astype(BF16), heads, wkbs, sbs)
        dus = _each(lambda qkb, dob, kdb, dsb: _dot(qkb, dob, TN) + _dot(kdb, dsb), qkbs, dobs, kdbs, dsbs)
        dubs = _each(lambda du: du.astype(BF16), dus)
        dwks = _each(lambda dub, sb: -_dot(dub, sb, NT), dubs, sbs)
        dqds = _each(lambda dob, sb: _dot(dob, sb, NT), dobs, sbs)
        dkds = _each(lambda ub, dsb: _dot(ub, dsb, NT), ubs, dsbs)
        dqks = _each(lambda dob, ub: _dot(dob, ub, NT), dobs, ubs)
        dgls = _each(lambda s, d: jnp.sum(jnp.sum(s * d, axis=1, keepdims=True), axis=0, keepdims=True), ss, dss)
        dsn = _each(lambda h, d, qdb, dob, wkb, dub: gl_ref[pl.ds(h * SUBLANES, 1), :] * d + _dot(qdb, dob, TN) - _dot(wkb, dub, TN),
                    heads, dss, qdbs, dobs, wkbs, dubs)
        for h in heads:
            du0_ref[:, _hsl(h)] = dus[h]
            dwk_ref[:, _hsl(h)] = dwks[h]
            dqd_ref[:, _hsl(h)] = dqds[h]
            dkd_ref[:, _hsl(h)] = dkds[h]
            dqk_ref[:, _hsl(h)] = dqks[h]
            dgl_ref[h * SUBLANES:(h + 1) * SUBLANES, :] = jnp.where(corner, dgls[h], 0.0)
            ds_ref[h] = dsn[h]

    blk = pl.BlockSpec((CHUNK, nh * hd), lambda i: (n - 1 - i, 0))
    gl_spec = pl.BlockSpec((nh * SUBLANES, LANES), lambda i: (n - 1 - i, 0))
    return pl.pallas_call(
        body, grid=(n,),
        in_specs=[blk] * 6 + [gl_spec, pl.BlockSpec((None, nh, hd, hd), lambda i: (n - 1 - i, 0, 0, 0))],
        out_specs=[blk] * 5 + [gl_spec],
        out_shape=[jax.ShapeDtypeStruct((t, nh * hd), F32)] * 5 + [jax.ShapeDtypeStruct((n * nh * SUBLANES, LANES), F32)],
        scratch_shapes=[pltpu.VMEM((nh, hd, hd), F32)], name=name,
        compiler_params=_params(("arbitrary",)),
    )(do, u0, wk, qd, kd, qk, gl, s_in)


N_PAIRS = N_HEADS_B // 2
PAIRS_PER_KV = N_PAIRS // N_KV_B


def _psl(j):
    return slice(j * LANES, (j + 1) * LANES)


KV_STEP = 4


def _att_fn(qps, kcs, kps, vcs, vps, sinks, kv0, first):
    w = WINDOW
    lane = lax.broadcasted_iota(jnp.int32, (1, LANES), 1)
    lo = (lane < HEAD_DIM_B).astype(F32)
    qi = lax.broadcasted_iota(jnp.int32, (w, w), 0)
    kj = lax.broadcasted_iota(jnp.int32, (w, w), 1)
    dist_c = (qi - kj).astype(F32)
    valid_c = kj <= qi
    valid_p = (kj > qi) & (first < 0.5)
    bf = lambda xs: [a.astype(BF16) for a in xs]
    kcb, kpb, vcb, vpb = bf(kcs), bf(kps), bf(vcs), bf(vps)
    scale = HEAD_DIM_B ** -0.5
    heads = [(g, j, half) for g in range(len(kcs)) for j in range(PAIRS_PER_KV) for half in range(2)]
    kvs = [g for g, _, _ in heads]
    hmasks = [lo if half == 0 else 1.0 - lo for _, _, half in heads]
    hds = [2.0 * (PAIRS_PER_KV * (kv0 + g) + j) + half for g, j, half in heads]
    slopes = _each(lambda hd: jnp.exp(-(hd + 1.0) * (8.0 / N_HEADS_B * math.log(2.0))), hds)
    snks = _each(lambda hd: jnp.sum(sinks * (lane.astype(F32) == hd).astype(F32), axis=1, keepdims=True), hds)
    qhs = _each(lambda h, hm: (qps[h[0] * PAIRS_PER_KV + h[1]] * hm).astype(BF16), heads, hmasks)
    lcs = _each(lambda qh, g, sl: jnp.where(valid_c, _dot(qh, kcb[g], NT) * scale - sl * dist_c, NEG), qhs, kvs, slopes)
    lps = _each(lambda qh, g, sl: jnp.where(valid_p, _dot(qh, kpb[g], NT) * scale - sl * (dist_c + w), NEG), qhs, kvs, slopes)
    ms = _each(lambda lc, lp, sk: lax.stop_gradient(jnp.maximum(jnp.maximum(jnp.max(lc, axis=1, keepdims=True),
                                                                            jnp.max(lp, axis=1, keepdims=True)), sk)), lcs, lps, snks)
    ecs = _each(lambda lc, m: jnp.exp(lc - m), lcs, ms)
    eps = _each(lambda lp, m: jnp.exp(lp - m), lps, ms)
    invs = _each(lambda ec, ep, sk, m: 1.0 / (jnp.sum(ec, axis=1, keepdims=True) + jnp.sum(ep, axis=1, keepdims=True) + jnp.exp(sk - m)),
                 ecs, eps, snks, ms)
    ohs = _each(lambda ec, ep, inv, g, hm: (_dot((ec * inv).astype(BF16), vcb[g]) + _dot((ep * inv).astype(BF16), vpb[g])) * hm,
                ecs, eps, invs, kvs, hmasks)
    return [ohs[2 * j] + ohs[2 * j + 1] for j in range(len(qps))]


def _scalar11(v):
    return jnp.full((1, 1), v, F32)


def _att_specs(row_of):
    cur = pl.BlockSpec((WINDOW, KV_STEP * LANES), lambda i, kv: (row_of(i), kv))
    prev = pl.BlockSpec((WINDOW, KV_STEP * LANES), lambda i, kv: (jnp.maximum(row_of(i) - 1, 0), kv))
    qs = pl.BlockSpec((WINDOW, KV_STEP * PAIRS_PER_KV * LANES), lambda i, kv: (row_of(i), kv))
    return qs, cur, prev, pl.BlockSpec((1, LANES), lambda i, kv: (0, 0))


def swa_fwd(qsrc, kd, vd, sinks, *, name):
    t = kd.shape[0]
    nb = t // WINDOW
    npair = KV_STEP * PAIRS_PER_KV

    def body(q_ref, kc_ref, kp_ref, vc_ref, vp_ref, s_ref, o_ref):
        first = _scalar11((pl.program_id(0) == 0).astype(F32))
        kv0 = _scalar11((pl.program_id(1) * KV_STEP).astype(F32))
        per_kv = lambda ref: [ref[:, _psl(g)] for g in range(KV_STEP)]
        outs = _att_fn([q_ref[:, _psl(j)] for j in range(npair)], per_kv(kc_ref), per_kv(kp_ref), per_kv(vc_ref), per_kv(vp_ref),
                       s_ref[...], kv0, first)
        for j in range(npair):
            o_ref[:, _psl(j)] = outs[j].astype(o_ref.dtype)

    qs, cur, prev, sk = _att_specs(lambda i: i)
    return pl.pallas_call(
        body, grid=(nb, N_KV_B // KV_STEP), in_specs=[qs, cur, prev, cur, prev, sk],
        out_specs=qs, out_shape=jax.ShapeDtypeStruct((t, N_PAIRS * LANES), BF16), name=name,
        compiler_params=_params(("parallel", "parallel")),
    )(qsrc, kd, kd, vd, vd, sinks)


def swa_bwd(do, qsrc, kd, vd, sinks, *, name):
    t = kd.shape[0]
    nb = t // WINDOW

    npair = KV_STEP * PAIRS_PER_KV

    def body(do_ref, q_ref, kc_ref, kp_ref, vc_ref, vp_ref, s_ref, dq_ref, dk_ref, dv_ref, ds_ref, carry_k, carry_v):
        step, kvg = pl.program_id(0), pl.program_id(1)
        first = _scalar11((step == nb - 1).astype(F32))

        @pl.when((step == 0) & (kvg == 0))
        def _():
            carry_k[...] = jnp.zeros_like(carry_k)
            carry_v[...] = jnp.zeros_like(carry_v)
            ds_ref[...] = jnp.zeros_like(ds_ref)

        kv0 = _scalar11((kvg * KV_STEP).astype(F32))
        per_kv = lambda ref: [ref[:, _psl(g)].astype(F32) for g in range(KV_STEP)]
        _, vjp = jax.vjp(lambda *a: _att_fn(*a, kv0, first), [q_ref[:, _psl(j)].astype(F32) for j in range(npair)],
                         per_kv(kc_ref), per_kv(kp_ref), per_kv(vc_ref), per_kv(vp_ref), s_ref[...])
        dqs, dkc, dkp, dvc, dvp, dsk = vjp([do_ref[:, _psl(j)].astype(F32) for j in range(npair)])
        for j in range(npair):
            dq_ref[:, _psl(j)] = dqs[j].astype(dq_ref.dtype)
        ds_ref[...] += dsk
        fold = lambda g: g + pltpu.roll(g, HEAD_DIM_B, 1)
        for g in range(KV_STEP):
            kv = kvg * KV_STEP + g
            dk_ref[:, _psl(g)] = fold(dkc[g] + carry_k[kv]).astype(dk_ref.dtype)
            dv_ref[:, _psl(g)] = fold(dvc[g] + carry_v[kv]).astype(dv_ref.dtype)
            carry_k[kv] = dkp[g]
            carry_v[kv] = dvp[g]

    qs, cur, prev, sk = _att_specs(lambda i: nb - 1 - i)
    return pl.pallas_call(
        body, grid=(nb, N_KV_B // KV_STEP),
        in_specs=[qs, qs, cur, prev, cur, prev, sk],
        out_specs=[qs, cur, cur, sk],
        out_shape=[jax.ShapeDtypeStruct((t, N_PAIRS * LANES), BF16), jax.ShapeDtypeStruct((t, N_KV_B * LANES), BF16),
                   jax.ShapeDtypeStruct((t, N_KV_B * LANES), BF16), jax.ShapeDtypeStruct((1, LANES), F32)],
        scratch_shapes=[pltpu.VMEM((N_KV_B, WINDOW, LANES), F32), pltpu.VMEM((N_KV_B, WINDOW, LANES), F32)],
        name=name, compiler_params=_params(("arbitrary", "arbitrary")),
    )(do, qsrc, kd, kd, vd, vd, sinks)


def loss_head(h, tgt, w, *, name):
    t, d = h.shape
    tm = min(256, t)

    def body(h_ref, t_ref, w_ref, dh_ref, dw_ref, l_ref):
        tg = t_ref[...]

        def f(hv, wv):
            err = _f_norm(hv, wv) - tg
            return 0.5 * jnp.sum(jnp.sum(err * err, axis=1, keepdims=True), axis=0, keepdims=True) * (1.0 / d)

        lv, vjp = jax.vjp(f, h_ref[...], w_ref[...])
        dh, dw = vjp(jnp.ones((1, 1), F32))
        dh_ref[...] = dh
        first = pl.program_id(0) == 0

        @pl.when(first)
        def _():
            dw_ref[...] = dw
            l_ref[...] = lv * jnp.ones((1, LANES), F32)

        @pl.when(jnp.logical_not(first))
        def _():
            dw_ref[...] += dw
            l_ref[...] += lv * jnp.ones((1, LANES), F32)

    rows = pl.BlockSpec((tm, d), lambda i: (i, 0))
    one = lambda c: pl.BlockSpec((1, c), lambda i: (0, 0))
    return pl.pallas_call(
        body, grid=(t // tm,), in_specs=[rows, rows, one(d)], out_specs=[rows, one(d), one(LANES)],
        out_shape=[jax.ShapeDtypeStruct((t, d), F32), jax.ShapeDtypeStruct((1, d), F32), jax.ShapeDtypeStruct((1, LANES), F32)],
        name=name, compiler_params=_params(("arbitrary",)),
    )(h, tgt, w)


def _row_tile(r, cap=256):
    tr = r
    if r % SUBLANES == 0:
        for cand in range(SUBLANES, min(r, cap) + 1, SUBLANES):
            if r % cand == 0:
                tr = cand
    return tr


def _adamw_update(wv, gv, mv, vv):
    mn = ADAM_B1 * mv + (1.0 - ADAM_B1) * gv
    vn = ADAM_B2 * vv + (1.0 - ADAM_B2) * jnp.square(gv)
    m_hat = mn / (1.0 - ADAM_B1 ** ADAM_STEP)
    v_hat = vn / (1.0 - ADAM_B2 ** ADAM_STEP)
    return -ADAM_LR * (m_hat / (jnp.sqrt(v_hat) + ADAM_EPS) + ADAM_WD * wv), mn, vn


def adamw_layers(w, halves, m, v, *, name):
    nl, r, c = w.shape
    tr = _row_tile(r // 2)
    nbh = r // 2 // tr

    def body(w_ref, *rest):
        g_refs, m_ref, v_ref = rest[:2 * nl], rest[2 * nl], rest[2 * nl + 1]
        d_ref, mo_ref, vo_ref, go_ref = rest[2 * nl + 2:]
        layer, i = pl.program_id(0), pl.program_id(1)
        mine = (i < nbh) == (lax.axis_index("c") == 0)
        gv = jnp.where(mine, g_refs[0][...], g_refs[1][...])
        for k in range(1, nl):
            gv = jnp.where(layer == k, jnp.where(mine, g_refs[2 * k][...], g_refs[2 * k + 1][...]), gv)
        d_ref[...], mo_ref[...], vo_ref[...] = _adamw_update(w_ref[...], gv, m_ref[...], v_ref[...])
        go_ref[...] = gv

    spec3 = pl.BlockSpec((None, tr, c), lambda k, i: (k, i, 0))
    g_specs = [pl.BlockSpec((tr, c), lambda k, i, q=q: (jnp.where(k == q, i % nbh, 0), 0)) for q in range(nl) for _ in range(2)]
    return pl.pallas_call(
        body, grid=(nl, r // tr), in_specs=[spec3] + g_specs + [spec3, spec3], out_specs=[spec3] * 4,
        out_shape=[jax.ShapeDtypeStruct((nl, r, c), F32)] * 4, name=name, compiler_params=_params(("arbitrary", "arbitrary")),
    )(w, *[h for pair in halves for h in pair], m, v)


def adamw(w, g, m, v, *, name):
    r, c = w.shape
    tr = _row_tile(r)

    def body(w_ref, g_ref, m_ref, v_ref, d_ref, mo_ref, vo_ref):
        d_ref[...], mo_ref[...], vo_ref[...] = _adamw_update(w_ref[...], g_ref[...], m_ref[...], v_ref[...])

    spec = pl.BlockSpec((tr, c), lambda i: (i, 0))
    return pl.pallas_call(
        body, grid=(r // tr,), in_specs=[spec] * 4, out_specs=[spec] * 3,
        out_shape=[jax.ShapeDtypeStruct((r, c), F32)] * 3, name=name, compiler_params=_params(("parallel",)),
    )(w, g, m, v)


def _place():
    return lax.axis_index("x"), lax.axis_index("y"), lax.axis_index("c")


def allgather8(blk, *, name):
    def body(x_ref, out_ref, send_sems, recv_sems, local_sem):
        x, y, c = _place()
        me = 4 * x + 2 * y + c
        mine = pltpu.make_async_copy(x_ref, out_ref.at[me], local_sem)
        mine.start()
        sent = []
        for k in range(1, N_DEV):
            to = (x ^ ((k >> 2) & 1), y ^ ((k >> 1) & 1), c ^ (k & 1))
            cp = pltpu.make_async_remote_copy(src_ref=x_ref, dst_ref=out_ref.at[me], send_sem=send_sems.at[k - 1],
                                              recv_sem=recv_sems.at[k - 1], device_id=to, device_id_type=MESH)
            cp.start()
            sent.append(cp)
        for k in range(1, N_DEV):
            frm = me ^ k
            pltpu.make_async_remote_copy(src_ref=x_ref, dst_ref=out_ref.at[frm], send_sem=send_sems.at[k - 1],
                                         recv_sem=recv_sems.at[k - 1], device_id=(x, y, c), device_id_type=MESH).wait_recv()
        for cp in sent:
            cp.wait_send()
        mine.wait()

    vm = pl.BlockSpec(memory_space=pltpu.VMEM)
    return pl.pallas_call(
        body, in_specs=[vm], out_specs=vm, out_shape=jax.ShapeDtypeStruct((N_DEV,) + blk.shape, blk.dtype), name=name,
        scratch_shapes=[pltpu.SemaphoreType.DMA((N_DEV - 1,)), pltpu.SemaphoreType.DMA((N_DEV - 1,)), pltpu.SemaphoreType.DMA],
    )(blk)


def _other_chips(x, y):
    return [(1 - x, y), (x, 1 - y), (1 - x, 1 - y)]


def _hbm_call(body, ins, out_shapes, n_sems, name):
    hbm = pl.BlockSpec(memory_space=pl.ANY)
    return pl.pallas_call(
        body, in_specs=[hbm] * len(ins), out_specs=[hbm] * len(out_shapes), out_shape=out_shapes, name=name,
        scratch_shapes=[pltpu.SemaphoreType.DMA((n_sems,)), pltpu.SemaphoreType.DMA((n_sems,))],
    )(*ins)


def _half_rows(c, rh):
    return pl.ds(pl.multiple_of(c * rh, BF16_ROWS), rh)


def gather_units(units, *, name):
    nu = len(units)
    shapes = []
    for arr, layer_major in units:
        r, cols = arr.shape
        shapes.append(jax.ShapeDtypeStruct((2, N_CHIPS, r // 2, cols) if layer_major else (N_CHIPS, r, cols), arr.dtype))

    def body(*refs):
        in_refs, out_refs, send_sems, recv_sems = refs[:nu], refs[nu:2 * nu], refs[2 * nu], refs[2 * nu + 1]
        x, y, c = _place()
        me_chip = 2 * x + y
        sib = (x, y, 1 - c)
        chips = _other_chips(x, y)

        def copy(k, src, dst, to):
            return pltpu.make_async_remote_copy(src_ref=src, dst_ref=dst, send_sem=send_sems.at[k], recv_sem=recv_sems.at[k],
                                                device_id=to, device_id_type=MESH)

        first, passed, landing = [], [], []
        for u, (arr, layer_major) in enumerate(units):
            rh = arr.shape[0] // 2
            out_ref = out_refs[u]
            slot = (lambda chip, half, o=out_ref: o.at[half, chip]) if layer_major else \
                   (lambda chip, half, o=out_ref, rh=rh: o.at[chip, _half_rows(half, rh), :])
            my_half = in_refs[u].at[_half_rows(c, rh), :]
            for j, (cx, cy) in enumerate(chips):
                k = 6 * u + j
                first.append(copy(k, my_half, slot(me_chip, c), (cx, cy, c)))
                passed.append(copy(k + 3, slot(2 * cx + cy, c), slot(2 * cx + cy, c), sib))
                landing.append((copy(k, my_half, slot(2 * cx + cy, c), sib), copy(k + 3, my_half, slot(2 * cx + cy, 1 - c), sib)))
        for cp in first:
            cp.start()
        for (over_ici, _), fwd in zip(landing, passed):
            over_ici.wait_recv()
            fwd.start()
        for _, from_sibling in landing:
            from_sibling.wait_recv()
        for cp in first + passed:
            cp.wait_send()

    return _hbm_call(body, [a for a, _ in units], shapes, 6 * nu, name)


HBM_SPEC = pl.BlockSpec(memory_space=pltpu.HBM)
SEM_SPEC = pl.BlockSpec(memory_space=pltpu.SEMAPHORE)
ORDERED_EFFECT = pltpu.SideEffectType.DATAFLOW_SIDE_EFFECTING


def _split_start(body, srcs, land_shapes, after, *, name):
    nu = len(srcs)
    lands = [lax.empty(s.shape, s.dtype) for s in land_shapes]

    def whole(*refs):
        body(refs[:nu], refs[nu:2 * nu], refs[2 * nu + 1], refs[2 * nu + 2])
        refs[-1][...] = jnp.zeros((SUBLANES, LANES), F32)

    hbm = lambda a: pltpu.with_memory_space_constraint(a, pltpu.HBM)
    sems = pltpu.SemaphoreType.DMA((nu,))
    res = pl.pallas_call(
        whole, name=name, in_specs=[HBM_SPEC] * (2 * nu) + [pl.BlockSpec(memory_space=pl.ANY)],
        out_shape=[sems, sems] + [pltpu.HBM(a.shape, a.dtype) for a in srcs] + [pltpu.HBM(s.shape, s.dtype) for s in land_shapes]
        + [jax.ShapeDtypeStruct((SUBLANES, LANES), F32)],
        out_specs=[SEM_SPEC, SEM_SPEC] + [HBM_SPEC] * (2 * nu) + [pl.BlockSpec(memory_space=pltpu.VMEM)],
        input_output_aliases={q: 2 + q for q in range(2 * nu)},
        compiler_params=pltpu.CompilerParams(has_side_effects=ORDERED_EFFECT),
    )(*[hbm(a) for a in srcs], *[hbm(a) for a in lands], after)
    return res[0], res[1], res[2:2 + nu], res[2 + nu:2 + 2 * nu], res[-1]


def _split_wait(pending, moved, after, *, name):
    send_sems, recv_sems, srcs, lands, _ = pending
    nu = len(srcs)

    def body(*refs):
        land_refs, ssem, rsem = refs[nu:2 * nu], refs[2 * nu], refs[2 * nu + 1]
        x, y, c = _place()
        for u in range(nu):
            size = moved(land_refs[u])
            cp = pltpu.make_async_remote_copy(src_ref=size, dst_ref=size, send_sem=ssem.at[u], recv_sem=rsem.at[u],
                                              device_id=(x, y, c), device_id_type=MESH)
            cp.wait_send()
            cp.wait_recv()

    res = pl.pallas_call(
        body, name=name, in_specs=[HBM_SPEC] * (2 * nu) + [SEM_SPEC, SEM_SPEC, pl.BlockSpec(memory_space=pl.ANY)],
        out_shape=[pltpu.HBM(a.shape, a.dtype) for a in srcs] + [pltpu.HBM(a.shape, a.dtype) for a in lands],
        out_specs=[HBM_SPEC] * (2 * nu), input_output_aliases={q: q for q in range(2 * nu)},
        compiler_params=pltpu.CompilerParams(has_side_effects=ORDERED_EFFECT),
    )(*srcs, *lands, send_sems, recv_sems, after)
    return res[nu:]


def gather_start(shards, after, *, name):
    def body(src_refs, land_refs, send_sems, recv_sems):
        x, y, c = _place()
        for u, shard in enumerate(shards):
            rows = _half_rows(c, shard.shape[0] // 2)
            for cx, cy in _other_chips(x, y):
                for core in range(2):
                    pltpu.make_async_remote_copy(src_ref=src_refs[u].at[rows, :], dst_ref=land_refs[u].at[2 * x + y, rows, :],
                                                 send_sem=send_sems.at[u], recv_sem=recv_sems.at[u], device_id=(cx, cy, core),
                                                 device_id_type=MESH).start()

    return _split_start(body, shards, [jax.ShapeDtypeStruct((N_CHIPS,) + s.shape, s.dtype) for s in shards], after, name=name)


def gather_wait(pending, after, *, name):
    return _split_wait(pending, lambda land: land.at[pl.ds(0, N_CHIPS - 1)], after, name=name)


def scatter_start(pairs, *, name):
    def body(src_refs, land_refs, send_sems, recv_sems):
        x, y, c = _place()
        for u in range(len(pairs)):
            for j, (cx, cy) in enumerate(_other_chips(x, y)):
                pltpu.make_async_remote_copy(src_ref=src_refs[u].at[2 * cx + cy], dst_ref=land_refs[u].at[j], send_sem=send_sems.at[u],
                                             recv_sem=recv_sems.at[u], device_id=(cx, cy, c), device_id_type=MESH).start()

    return _split_start(body, pairs, [jax.ShapeDtypeStruct((N_CHIPS - 1,) + p.shape[1:], p.dtype) for p in pairs], pairs[0], name=name)


def scatter_wait(pending, after, *, name):
    return _split_wait(pending, lambda land: land, after, name=name)


def swap_units(units, *, name):
    nu = len(units)

    def body(*refs):
        g_refs, out_refs, send_sems, recv_sems = refs[:nu], refs[nu:2 * nu], refs[2 * nu], refs[2 * nu + 1]
        x, y, c = _place()
        cps = [pltpu.make_async_remote_copy(src_ref=g_refs[u].at[:, _half_rows(1 - c, units[u].shape[1] // 2), :], dst_ref=out_refs[u],
                                            send_sem=send_sems.at[u], recv_sem=recv_sems.at[u], device_id=(x, y, 1 - c),
                                            device_id_type=MESH) for u in range(nu)]
        for cp in cps:
            cp.start()
        for cp in cps:
            cp.wait()

    shapes = [jax.ShapeDtypeStruct((N_CHIPS, g.shape[1] // 2, g.shape[2]), g.dtype) for g in units]
    return _hbm_call(body, units, shapes, nu, name)


def join_units(units, *, name):
    nu = len(units)

    def body(*refs):
        h_refs, out_refs, send_sems, recv_sems = refs[:nu], refs[nu:2 * nu], refs[2 * nu], refs[2 * nu + 1]
        x, y, c = _place()
        cps = [pltpu.make_async_remote_copy(src_ref=h_refs[u], dst_ref=out_refs[u], send_sem=send_sems.at[u], recv_sem=recv_sems.at[u],
                                            device_id=(x, y, 1 - c), device_id_type=MESH) for u in range(nu)]
        for cp in cps:
            cp.start()
        for cp in cps:
            cp.wait()

    return _hbm_call(body, units, [jax.ShapeDtypeStruct(h.shape, h.dtype) for h in units], nu, name)


def _half_tile(rh):
    tr = rh
    for cand in range(BF16_ROWS, min(rh, 512) + 1, BF16_ROWS):
        if rh % cand == 0:
            tr = cand
    return tr


def pair_add(g, sib, *, name):
    nc, rh, cols = sib.shape
    tr = _half_tile(rh)
    nbh = rh // tr

    def body(g0_ref, g1_ref, s_ref, o_ref):
        mine = jnp.where(lax.axis_index("c") == 0, g0_ref[...], g1_ref[...])
        o_ref[...] = (mine.astype(F32) + s_ref[...].astype(F32)).astype(o_ref.dtype)

    blk = lambda off: pl.BlockSpec((None, tr, cols), lambda j, i: (j, off + i, 0))
    return pl.pallas_call(
        body, grid=(nc, nbh), in_specs=[blk(0), blk(nbh), blk(0)], out_specs=blk(0),
        out_shape=jax.ShapeDtypeStruct(sib.shape, BF16), name=name, compiler_params=_params(("parallel", "parallel")),
    )(g, g, sib)


def chips_add(pair, landed, *, name):
    nc, rh, cols = pair.shape
    tr = _half_tile(rh)

    def body(*refs):
        chip = 2 * lax.axis_index("x") + lax.axis_index("y")
        acc = refs[0][...]
        for j in range(1, nc):
            acc = jnp.where(chip == j, refs[j][...], acc)
        acc = acc.astype(F32)
        for r in refs[nc:-1]:
            acc = acc + r[...].astype(F32)
        refs[-1][...] = acc

    part = lambda q: pl.BlockSpec((None, tr, cols), lambda i, q=q: (q, i, 0))
    return pl.pallas_call(
        body, grid=(rh // tr,), in_specs=[part(q) for q in range(nc)] + [part(q) for q in range(landed.shape[0])],
        out_specs=pl.BlockSpec((tr, cols), lambda i: (i, 0)),
        out_shape=jax.ShapeDtypeStruct((rh, cols), F32), name=name, compiler_params=_params(("parallel",)),
    )(*[pair] * nc, *[landed] * landed.shape[0])


def sum8(g, *, name):
    def body(g_ref, o_ref):
        acc = g_ref[0]
        for d in range(1, N_DEV):
            acc = acc + g_ref[d]
        o_ref[...] = acc

    return pl.pallas_call(body, out_shape=jax.ShapeDtypeStruct(g.shape[1:], F32), name=name)(g)


def _dup_halves(a):
    t = a.shape[0]
    a = a.reshape(t, N_KV_B, HEAD_DIM_B)
    return jnp.concatenate([a, a], axis=-1).reshape(t, N_KV_B * LANES)


def _undup(a):
    t = a.shape[0]
    return a.reshape(t, N_KV_B, LANES)[:, :, :HEAD_DIM_B].reshape(t, N_KV_B * HEAD_DIM_B)


def _lane_pad(v, offset=0):
    return jnp.zeros((1, LANES), F32).at[0, offset:offset + v.shape[0]].set(v)


SHARD_UP = 2 * D_FF // N_CHIPS
SHARD_BIN = (N_HEADS_B + 2 * N_KV_B) * HEAD_DIM_B // N_CHIPS
SHARD_PROJ = D_MODEL // N_CHIPS


def local_step(x, p, tgt, sm, weight, on_grads):
    t = x.shape[0]
    rtm = min(256, t)
    hk = N_HEADS_A * HEAD_DIM_A
    qd_b = N_HEADS_B * HEAD_DIM_B
    kd_b = N_KV_B * HEAD_DIM_B
    gs = {}
    norm = lambda h, w, nm: tile_map(_f_norm, [(h, D_MODEL, 0)], [w], [(D_MODEL, BF16)], tm=rtm, ncol=1, name=nm)[0]

    spec = pl.BlockSpec
    mtm = _tile(D_MODEL, MM_TM_CAP)
    p_bf = p.astype(BF16)
    alog_p = _lane_pad(sm["a_log"][0], N_HEADS_A)
    dtb_p = _lane_pad(sm["a_dt_bias"][0], N_HEADS_A)
    sinks_p = _lane_pad(sm["b_sinks"][0])
    nw = lambda name, i: sm[name][i:i + 1]
    by_chip = lambda kdim, ns: dict(tn=ns, tk=kdim, b_spec=spec((None, kdim, ns), lambda r, j, kk: (j, kk, 0)))
    by_chip_t = lambda ndim, ns: dict(n=ndim, tn=ndim, tk=ns, b_spec=spec((None, ndim, ns), lambda r, j, kk: (kk, j, 0)))
    cache = {}

    def wgt(name, i, after):
        if (name, i) not in cache:
            cache[name, i] = weight(name, i, after)
        return cache[name, i]

    saved = []
    h = x
    hn_next = norm(h, nw("norm_mix", 0), "norm_mix0")
    for i in range(DEPTH):
        s = {"h0": h, "hn": hn_next}
        if i % 2 == 0:
            s["pm"] = mm(s["hn"], wgt("a_w_in", i, h), name="a_in", tm_cap=2 * MM_TM_CAP)
            tail = (s["pm"], LANES, 4 * hk // LANES)
            s["c"] = conv_fwd(s["pm"], wgt("a_conv", i, h), name="a_conv")
            s["bg"] = tile_map(_f_betag, [tail], [alog_p, dtb_p], [(LANES, F32)], tm=rtm, ncol=1, name="a_betag")[0]
            s["prep"], s["tms"] = delta_prep(s["c"], s["bg"], name="a_prep")
            s["o"], s["s_in"] = delta_scan(*s["prep"], name="a_scan")
            s["on"] = gnorm_fwd(s["o"], s["pm"], sm["a_norm"], name="a_gnorm")
            h, s["hf"] = mm(s["on"], wgt("a_w_out", i, s["on"]), add=h, norm_w=nw("norm_ffn", i), name="a_out")
        else:
            s["pb"] = mm(s["hn"], wgt("b_w_in", i, s["hn"]), name="b_in", out_dtype=BF16, n=N_CHIPS * SHARD_BIN,
                         tm_cap=2 * MM_TM_CAP, **by_chip(D_MODEL, SHARD_BIN))
            s["kd"], s["vd"] = _dup_halves(s["pb"][:, qd_b:qd_b + kd_b]), _dup_halves(s["pb"][:, qd_b + kd_b:])
            s["ao"] = swa_fwd(s["pb"], s["kd"], s["vd"], sinks_p, name="b_att")
            h, s["hf"] = mm(s["ao"], wgt("b_w_out", i, s["ao"]), add=h, norm_w=nw("norm_ffn", i), name="b_out")
        s["h1"] = h
        s["u"] = mm(s["hf"], wgt("f_w_up", i, s["hf"]), name=f"f_up{i}", out_dtype=BF16, n=2 * D_FF, tm_cap=2 * MM_TM_CAP,
                    **by_chip(D_MODEL, SHARD_UP))
        s["act"] = conv_act_fwd(s["u"], wgt("f_conv", i, s["hf"]), name=f"f_conv_act{i}")
        h, s["hp"] = mm(s["act"], wgt("f_w_down", i, s["act"]), add=h, norm_w=nw("norm_ple", i), name=f"f_down{i}", tk=D_FF)
        s["h2"] = h
        s["gl"] = mm(s["hp"], wgt("ple_w_gate", i, s["hp"]), name=f"ple_gate{i}", tm_cap=2 * MM_TM_CAP)
        s["pe"] = mm(p_bf[i], wgt("ple_w_proj", i, s["hp"]), name=f"ple_proj{i}", n=D_MODEL, **by_chip(PLE_DIM, SHARD_PROJ))
        rows3 = [(h, D_MODEL, 0), (s["gl"], D_MODEL, 0), (s["pe"], D_MODEL, 0)]
        if i + 1 < DEPTH:
            def mix_norm(hv, g, e, wn):
                hn = hv + _f_ple(g, e)
                return hn, _f_norm(hn, wn)
            h, hn_next = tile_map(mix_norm, rows3, [nw("norm_mix", i + 1)], [(D_MODEL, F32), (D_MODEL, BF16)], tm=rtm, ncol=1,
                                  name=f"ple_mix{i}")
        else:
            h = tile_map(lambda hv, g, e: hv + _f_ple(g, e), rows3, [], [(D_MODEL, F32)], tm=rtm, ncol=1, name=f"ple_mix{i}")[0]
        saved.append(s)

    dh, gnf, loss = loss_head(h, tgt, sm["norm_final"][None, :], name="loss_head")
    gs["norm_final"] = gnf[0]

    g_mix, g_ffn, g_ple, g_conv = ([None] * DEPTH for _ in range(4))
    zero = jnp.zeros((1, 1), F32)
    for i in reversed(range(DEPTH)):
        s, gw = saved[i], {}
        by_rows = lambda g: g.reshape(N_CHIPS, g.shape[0] // N_CHIPS, g.shape[1])
        (dgl, dpe), _ = tile_vjp(_f_ple, [(s["gl"], D_MODEL, 0), (s["pe"], D_MODEL, 0)], [], [(dh, D_MODEL, 0)], n_diff=2,
                                 tm=rtm, ncol=1, name=f"ple_mix_bwd{i}", grad_dtypes=[BF16, BF16])
        gw["ple_w_proj"] = mm(p_bf[i], dpe, ta=True, name=f"ple_proj_dw{i}", out_dtype=BF16, tn=SHARD_PROJ,
                              o_shape=(N_CHIPS, PLE_DIM, SHARD_PROJ), o_spec=spec((None, PLE_DIM, SHARD_PROJ), lambda r, j, kk: (j, r, 0)))
        gw["ple_w_gate"] = by_rows(mm(s["hp"], dgl, ta=True, name=f"ple_gate_dw{i}", out_dtype=BF16))
        fused = dict(tb=True, tm_cap=MM_TM_CAP // 2)
        dh, g_ple[i] = mm(dgl, cache["ple_w_gate", i], name=f"ple_gate_dx{i}", norm_grad=(s["h2"], nw("norm_ple", i) + zero, dh), **fused)

        dact = mm(dh, cache["f_w_down", i], tb=True, name=f"f_down_dx{i}")
        gw["f_w_down"] = by_rows(mm(s["act"], dh, ta=True, name=f"f_down_dw{i}", out_dtype=BF16, tm_cap=D_FF // 2))
        du_halves = conv_act_bwd(s["u"], dact, cache["f_conv", i], name=f"f_conv_act_bwd{i}")
        g_conv[i] = jnp.concatenate(du_halves[2:], axis=1)
        dhf = g_up = None
        for half, du in enumerate(du_halves[:2]):
            c0 = half * (N_CHIPS // 2)
            g_up = mm(s["hf"], du, ta=True, name=f"f_up_dw{i}_{half}", out_dtype=BF16, tn=SHARD_UP, into=g_up,
                      o_shape=(N_CHIPS, D_MODEL, SHARD_UP), o_spec=spec((None, mtm, SHARD_UP), lambda r, j, kk, c0=c0: (c0 + j, r, 0)))
            last = dict(norm_grad=(s["h1"], nw("norm_ffn", i), dh), **fused) if half else dict(tb=True)
            dhf = mm(du, cache["f_w_up", i], name=f"f_up_dx{i}_{half}", n=D_MODEL, tn=D_MODEL, tk=SHARD_UP, add=dhf,
                     b_spec=spec((None, D_MODEL, SHARD_UP), lambda r, j, kk, c0=c0: (c0 + kk, j, 0)), **last)
        gw["f_w_up"] = g_up
        dh, g_ffn[i] = dhf
        token, gw = on_grads(i, "ffn", gw), {}
        w_out = cache["a_w_out" if i % 2 == 0 else "b_w_out", i]
        if token is not None:
            w_out = w_out + token[:1, :1].astype(BF16)

        if i % 2 == 0:
            don = mm(dh, w_out, tb=True, name="a_out_dx")
            gw["a_w_out"] = by_rows(mm(s["on"], dh, ta=True, name="a_out_dw", out_dtype=BF16))
            do, dz, gs["a_norm"] = gnorm_bwd(s["o"], s["pm"], sm["a_norm"], don, name="a_gnorm_bwd")
            dprep = delta_scan_bwd(do, *s["prep"], s["s_in"], name="a_scan_bwd")
            dc, dbg = delta_prep_bwd(s["c"], s["bg"], s["tms"], dprep, name="a_prep_bwd")
            (dpt,), (galog, gdtb) = tile_vjp(_f_betag, [(s["pm"], LANES, 4 * hk // LANES)], [alog_p, dtb_p], [(dbg, LANES, 0)], n_diff=1,
                                             tm=rtm, ncol=1, name="a_betag_bwd", grad_dtypes=[BF16])
            gs["a_log"] = galog[:, N_HEADS_A:2 * N_HEADS_A]
            gs["a_dt_bias"] = gdtb[:, N_HEADS_A:2 * N_HEADS_A]
            dqkv, gs["a_conv"] = conv_bwd(dc, s["pm"], cache["a_conv", i], name="a_conv_bwd")
            dpm = jnp.concatenate([dqkv, dz, dpt], axis=1)
            g_in = mm(s["hn"], dpm, ta=True, name="a_in_dw", out_dtype=BF16)[:, :4 * hk + 2 * N_HEADS_A]
            gw["a_w_in"] = g_in.reshape(D_MODEL, N_CHIPS, g_in.shape[1] // N_CHIPS).transpose(1, 0, 2)
            dh, g_mix[i] = mm(dpm, cache["a_w_in", i], name="a_in_dx", norm_grad=(s["h0"], nw("norm_mix", i), dh), **fused)
        else:
            dao = mm(dh, w_out, tb=True, name="b_out_dx")
            gw["b_w_out"] = by_rows(mm(s["ao"], dh, ta=True, name="b_out_dw", out_dtype=BF16))
            dq, dkd, dvd, gsk = swa_bwd(dao, s["pb"], s["kd"], s["vd"], sinks_p, name="b_att_bwd")
            gs["b_sinks"] = gsk[:, :N_HEADS_B]
            dpb = jnp.concatenate([dq, _undup(dkd), _undup(dvd)], axis=1)
            gw["b_w_in"] = mm(s["hn"], dpb, ta=True, name="b_in_dw", out_dtype=BF16, tn=SHARD_BIN,
                              o_shape=(N_CHIPS, D_MODEL, SHARD_BIN), o_spec=spec((None, mtm, SHARD_BIN), lambda r, j, kk: (j, r, 0)))
            dh, g_mix[i] = mm(dpb, cache["b_w_in", i], name="b_in_dx", norm_grad=(s["h0"], nw("norm_mix", i), dh), **fused,
                              **by_chip_t(D_MODEL, SHARD_BIN))
        token = on_grads(i, "mix", gw)
        if token is not None:
            zero = token[:1, :1]

    gs["norm_mix"], gs["norm_ffn"], gs["norm_ple"] = (jnp.concatenate(g, axis=0) for g in (g_mix, g_ffn, g_ple))
    gs["f_conv"] = jnp.stack(g_conv)
    return loss, dh, gs


BIG = ["a_w_in", "a_w_out", "b_w_in", "b_w_out", "f_w_up", "f_w_down", "ple_w_proj", "ple_w_gate"]
LAYERED = {"f_w_up", "f_w_down", "ple_w_proj", "ple_w_gate"}
BY_CHIP = {"b_w_in", "f_w_up", "ple_w_proj"}
LAYER_UNITS = [[("a_w_in", 0), ("a_w_out", 0)] + [(n, 0) for n in sorted(LAYERED)],
               [("b_w_in", 1), ("b_w_out", 1)] + [(n, 1) for n in sorted(LAYERED)]]
CONVS = ["a_conv", "f_conv"]
SMALL = ["norm_mix", "norm_ffn", "norm_ple", "norm_final", "a_log", "a_dt_bias", "a_norm", "b_sinks"]
SMALL_ROWS = 8
CONV_ROWS = 16
CONV_GRAD_ROWS = 48


def _pack_rows(arrs, rows, dtype):
    flat = jnp.concatenate([a.reshape(-1).astype(dtype) for a in arrs])
    return jnp.pad(flat, (0, rows * PACK_COLS - flat.shape[0])).reshape(rows, PACK_COLS)


def _unpack(flat, shapes):
    out, off = [], 0
    for shp in shapes:
        n = math.prod(shp)
        out.append(flat[off:off + n].reshape(shp))
        off += n
    return out


def _pack_small(d, loss=None):
    tail = jnp.concatenate([d["a_log"].reshape(-1), d["a_dt_bias"].reshape(-1), d["a_norm"].reshape(-1), d["b_sinks"].reshape(-1)])
    if loss is not None:
        tail = jnp.concatenate([tail, loss.reshape(-1)[:1]])
    tail = jnp.pad(tail, (0, PACK_COLS - tail.shape[0]))
    return jnp.concatenate([d["norm_mix"], d["norm_ffn"], d["norm_ple"], d["norm_final"][None, :], tail[None, :]], axis=0)


def _unpack_small(a, like):
    out = {"norm_mix": a[0:2], "norm_ffn": a[2:4], "norm_ple": a[4:6], "norm_final": a[6]}
    off = 0
    for nm in ("a_log", "a_dt_bias", "a_norm", "b_sinks"):
        n = like[nm].size
        out[nm] = a[7, off:off + n].reshape(like[nm].shape)
        off += n
    return out, a[7, off]


def _as2d(a):
    return a.reshape(-1, a.shape[-1])


def kernel(x, p, norm_mix, norm_ffn, norm_ple, norm_final, a_w_in, a_conv, a_log, a_dt_bias, a_norm, a_w_out, b_w_in, b_sinks, b_w_out, f_w_up, f_conv, f_w_down, ple_w_proj, ple_w_gate, loss_target, m_norm_mix, m_norm_ffn, m_norm_ple, m_norm_final, m_a_w_in, m_a_conv, m_a_log, m_a_dt_bias, m_a_norm, m_a_w_out, m_b_w_in, m_b_sinks, m_b_w_out, m_f_w_up, m_f_conv, m_f_w_down, m_ple_w_proj, m_ple_w_gate, v_norm_mix, v_norm_ffn, v_norm_ple, v_norm_final, v_a_w_in, v_a_conv, v_a_log, v_a_dt_bias, v_a_norm, v_a_w_out, v_b_w_in, v_b_sinks, v_b_w_out, v_f_w_up, v_f_conv, v_f_w_down, v_ple_w_proj, v_ple_w_gate):
    w = dict(norm_mix=norm_mix, norm_ffn=norm_ffn, norm_ple=norm_ple, norm_final=norm_final, a_w_in=a_w_in, a_conv=a_conv,
             a_log=a_log, a_dt_bias=a_dt_bias, a_norm=a_norm, a_w_out=a_w_out, b_w_in=b_w_in, b_sinks=b_sinks, b_w_out=b_w_out,
             f_w_up=f_w_up, f_conv=f_conv, f_w_down=f_w_down, ple_w_proj=ple_w_proj, ple_w_gate=ple_w_gate)
    m = dict(norm_mix=m_norm_mix, norm_ffn=m_norm_ffn, norm_ple=m_norm_ple, norm_final=m_norm_final, a_w_in=m_a_w_in,
             a_conv=m_a_conv, a_log=m_a_log, a_dt_bias=m_a_dt_bias, a_norm=m_a_norm, a_w_out=m_a_w_out, b_w_in=m_b_w_in,
             b_sinks=m_b_sinks, b_w_out=m_b_w_out, f_w_up=m_f_w_up, f_conv=m_f_conv, f_w_down=m_f_w_down,
             ple_w_proj=m_ple_w_proj, ple_w_gate=m_ple_w_gate)
    v = dict(norm_mix=v_norm_mix, norm_ffn=v_norm_ffn, norm_ple=v_norm_ple, norm_final=v_norm_final, a_w_in=v_a_w_in,
             a_conv=v_a_conv, a_log=v_a_log, a_dt_bias=v_a_dt_bias, a_norm=v_a_norm, a_w_out=v_a_w_out, b_w_in=v_b_w_in,
             b_sinks=v_b_sinks, b_w_out=v_b_w_out, f_w_up=v_f_w_up, f_conv=v_f_conv, f_w_down=v_f_w_down,
             ple_w_proj=v_ple_w_proj, ple_w_gate=v_ple_w_gate)
    xc, yc, cc = _place()
    my_chip = 2 * xc + yc

    shard = {(n, i): w[n][i if n in LAYERED else 0].astype(BF16) for n, i in LAYER_UNITS[0] + LAYER_UNITS[1]}
    first = shard["a_w_in", 0]
    (ga,) = gather_units([(first, False)], name="gather_first")
    ga = lax.dynamic_update_index_in_dim(ga, first, my_chip, 0)
    a_in = jnp.concatenate([ga[j] for j in range(N_CHIPS)], axis=1)
    n_main = 4 * N_HEADS_A * HEAD_DIM_A
    conv_shapes = [w[n].shape for n in CONVS]
    convs = allgather8(_pack_rows([w[n] for n in CONVS], CONV_ROWS, F32), name="gather_convs")
    conv_parts = [_unpack(convs[2 * j].reshape(-1), conv_shapes) for j in range(N_CHIPS)]
    a_conv_full, f_conv_full = (jnp.concatenate([conv_parts[j][q] for j in range(N_CHIPS)], axis=2) for q in range(2))
    ready = {("a_w_in", 0): jnp.pad(a_in, ((0, 0), (0, n_main + LANES - a_in.shape[1]))), ("a_conv", 0): a_conv_full[0], ("f_conv", 0): f_conv_full[0], ("f_conv", 1): f_conv_full[1]}
    later = [[k for k in units if k != ("a_w_in", 0)] for units in LAYER_UNITS]
    pending, after = [], ga
    for layer, keys in enumerate(later):
        pending.append(gather_start([shard[k] for k in keys], after, name=f"gather_start{layer}"))
        after = pending[-1][4]
    sm = {n: w[n] for n in SMALL}
    sm["norm_mix"] = sm["norm_mix"] + after[:1, :1]

    def weight(name, layer, act):
        if (name, layer) not in ready:
            landed = gather_wait(pending[layer], act, name=f"gather_wait{layer}")
            for k, g in zip(later[layer], landed):
                g = lax.dynamic_update_index_in_dim(g, shard[k], my_chip, 0)
                ready[k] = g if k[0] in BY_CHIP else g.reshape(N_CHIPS * g.shape[1], g.shape[2])
        return ready[name, layer]

    pairs, scattered, started = {}, {}, []

    def on_grads(layer, part, gw):
        keys = [k for k in LAYER_UNITS[layer] if (k[0] in LAYERED) == (part == "ffn")]
        from_sib = swap_units([gw[n] for n, _ in keys], name=f"rs_swap_{part}{layer}")
        for (n, _), sib in zip(keys, from_sib):
            pairs[n, layer] = pair_add(gw[n], sib, name=f"rs_add_pair_{n}{layer}")
        started.append((keys, scatter_start([pairs[k] for k in keys], name=f"rs_scatter_start_{part}{layer}"), f"{part}{layer}"))
        return started[-1][1][4]

    loss, grad_x, gs = local_step(x[0], p[:, 0], loss_target[0], sm, weight, on_grads)

    grads, delta, new_m, new_v, g_unit = {}, {}, {}, {}, {}

    def finish(keys, tag):
        halves = [chips_add(pairs[k], scattered[k], name=f"rs_add_chips_{k[0]}{k[1]}") for k in keys]
        g_unit.update(zip(keys, zip(halves, join_units(halves, name=f"rs_join_{tag}"))))
        for n in BIG:
            mine = [(n, i) for i in range(DEPTH) if (n, i) in LAYER_UNITS[i]]
            if n not in delta and all(k in g_unit for k in mine):
                g_layers = [g_unit[k] for k in mine]
                shape3 = (len(g_layers), 2 * g_layers[0][0].shape[0], g_layers[0][0].shape[1])
                res = adamw_layers(w[n].reshape(shape3), g_layers, m[n].reshape(shape3), v[n].reshape(shape3), name=f"adamw_{n}")
                delta[n], new_m[n], new_v[n], grads[n] = (r.reshape(w[n].shape) for r in res)

    last_keys, last_pending, last_tag = started[-1]
    for keys, pend, tag in started[:-1]:
        scattered.update(zip(keys, scatter_wait(pend, last_pending[4], name=f"rs_scatter_wait_{tag}")))
    finish([k for keys, _, _ in started[:-1] for k in keys], "first")

    conv_grads = _pack_rows([gs[n] for n in CONVS], CONV_GRAD_ROWS, F32)
    small_sum = sum8(allgather8(jnp.concatenate([_pack_small(gs, loss), conv_grads]), name="gather_small"), name="sum_small")
    g_sm, loss_sum = _unpack_small(small_sum[:SMALL_ROWS], sm)

    for n, full in zip(CONVS, _unpack(small_sum[SMALL_ROWS:].reshape(-1), [gs[n].shape for n in CONVS])):
        g2 = _as2d(lax.dynamic_slice_in_dim(full, my_chip * w[n].shape[-1], w[n].shape[-1], axis=full.ndim - 1))
        d2, m2, v2 = adamw(_as2d(w[n]), g2, _as2d(m[n]), _as2d(v[n]), name=f"adamw_{n}")
        grads[n], delta[n], new_m[n], new_v[n] = (r.reshape(w[n].shape) for r in (g2, d2, m2, v2))
    pk = lambda d: _pack_small(d)
    d2, m2, v2 = adamw(pk(sm), pk(g_sm), pk({n: m[n] for n in SMALL}), pk({n: v[n] for n in SMALL}), name="adamw_small")
    for src, dst in ((d2, delta), (m2, new_m), (v2, new_v)):
        dst.update(_unpack_small(src, sm)[0])
    grads.update(g_sm)

    scattered.update(zip(last_keys, scatter_wait(last_pending, d2, name=f"rs_scatter_wait_{last_tag}")))
    finish(last_keys, "last")

    order = ["norm_mix", "norm_ffn", "norm_ple", "norm_final", "a_w_in", "a_conv", "a_log", "a_dt_bias", "a_norm", "a_w_out",
             "b_w_in", "b_sinks", "b_w_out", "f_w_up", "f_conv", "f_w_down", "ple_w_proj", "ple_w_gate"]
    return (loss_sum, grad_x[None], *[grads[n] for n in order], *[delta[n] for n in order],
            *[new_m[n] for n in order], *[new_v[n] for n in order])
```

```python
import functools
import math

import jax
import jax.numpy as jnp
from jax import lax
from jax.experimental import pallas as pl
from jax.experimental.pallas import tpu as pltpu

F32 = jnp.float32
BF16 = jnp.bfloat16
MESH = pl.DeviceIdType.MESH

D_MODEL = 1024
N_HEADS_A = 8
HEAD_DIM_A = 128
CONV_A = 4
N_HEADS_B = 16
N_KV_B = 4
HEAD_DIM_B = 64
WINDOW = 128
D_FF = 2816
FFN_CONV = 3
PLE_DIM = 256
EPS = 1e-6
DEPTH = 2

ADAM_LR = 0.001
ADAM_B1 = 0.9
ADAM_B2 = 0.999
ADAM_EPS = 1e-08
ADAM_WD = 0.01
ADAM_STEP = 10

LANES = 128
SUBLANES = 8
BF16_ROWS = 16
CHUNK = 128
VMEM_LIMIT = 56 * 1024 * 1024
NEG = -1e30
N_CHIPS = 4
N_DEV = 8
PACK_COLS = 1024


def _params(sem=None):
    return pltpu.CompilerParams(dimension_semantics=sem, vmem_limit_bytes=VMEM_LIMIT)


def _tile(dim, cap):
    if dim % LANES:
        return dim
    best = LANES
    for t in range(LANES, min(dim, cap) + 1, LANES):
        if dim % t == 0:
            best = t
    return best


def _dot(a, b, dims=(((1,), (0,)), ((), ())), precision=None):
    return lax.dot_general(a, b, dims, precision=precision, preferred_element_type=F32)


NN = (((1,), (0,)), ((), ()))
NT = (((1,), (1,)), ((), ()))
TN = (((0,), (0,)), ((), ()))


MM_TM_CAP = 1024
MM_TK_CAP_TOKENS = 2048


def mm(a, b, *, name, ta=False, tb=False, out_dtype=F32, add=None, norm_w=None, norm_grad=None, tm_cap=MM_TM_CAP, tn_cap=1408,
       tk_cap=1408, n=None, tn=None, tk=None, b_spec=None, o_spec=None, o_shape=None, into=None):
    m, k = (a.shape[1], a.shape[0]) if ta else a.shape
    if b_spec is None:
        n = b.shape[0] if tb else b.shape[1]
        assert (b.shape[1] if tb else b.shape[0]) == k, (a.shape, b.shape, ta, tb)
    tm, tn, tk = _tile(m, tm_cap), tn or _tile(n, tn_cap), tk or _tile(k, MM_TK_CAP_TOKENS if ta else tk_cap)
    assert n % tn == 0 and k % tk == 0, (n, tn, k, tk)
    nk = k // tk
    dims = (((0 if ta else 1,), (1 if tb else 0,)), ((), ()))
    has_add, has_norm, has_grad = add is not None, norm_w is not None, norm_grad is not None
    assert not (has_norm or has_grad) or (tn == n and o_spec is None), "the norm epilogues need whole rows"
    n_in = 2 + has_add + has_norm + 3 * has_grad + (into is not None)

    def body(*refs):
        a_ref, b_ref = refs[0], refs[1]
        add_ref = refs[2] if has_add else None
        o_ref = refs[n_in]
        part = _dot(a_ref[...].astype(BF16), b_ref[...].astype(BF16), dims)
        first = pl.program_id(0) == 0

        def finish(r):
            if has_add:
                r = r + add_ref[...].astype(F32)
            if has_grad:
                h_ref, w_ref, prev_ref = refs[2 + has_add:5 + has_add]
                _, vjp = jax.vjp(_f_norm, h_ref[...], w_ref[...])
                r, dw = vjp(r)
                r = r + prev_ref[...]

                @pl.when(first)
                def _():
                    refs[n_in + 1][...] = dw

                @pl.when(jnp.logical_not(first))
                def _():
                    refs[n_in + 1][...] += dw
            o_ref[...] = r.astype(o_ref.dtype)
            if has_norm:
                refs[n_in + 1][...] = _f_norm(r, refs[2 + has_add][...]).astype(BF16)

        if nk == 1:
            finish(part)
            return
        acc = refs[-1]
        kk = pl.program_id(2)

        @pl.when(kk == 0)
        def _():
            acc[...] = part

        @pl.when(kk > 0)
        def _():
            acc[...] += part

        @pl.when(kk == nk - 1)
        def _():
            finish(acc[...])

    a_spec = pl.BlockSpec((tk, tm), lambda i, j, kk: (kk, i)) if ta else pl.BlockSpec((tm, tk), lambda i, j, kk: (i, kk))
    if b_spec is None:
        b_spec = pl.BlockSpec((tn, tk), lambda i, j, kk: (j, kk)) if tb else pl.BlockSpec((tk, tn), lambda i, j, kk: (kk, j))
    plain_o = pl.BlockSpec((tm, tn), lambda i, j, kk: (i, j))
    if o_spec is None:
        o_spec, o_shape = plain_o, (m, n)
    in_specs = [a_spec, b_spec] + ([plain_o] if has_add else [])
    args = (a, b) + ((add,) if has_add else ())
    out_specs, out_shapes = o_spec, jax.ShapeDtypeStruct(tuple(o_shape), out_dtype)
    one_row = pl.BlockSpec((1, n), lambda i, j, kk: (0, 0))
    if has_norm:
        in_specs.append(one_row)
        args += (norm_w,)
        out_specs, out_shapes = [o_spec, plain_o], [out_shapes, jax.ShapeDtypeStruct((m, n), BF16)]
    if has_grad:
        assert not has_norm
        in_specs += [plain_o, one_row, plain_o]
        args += tuple(norm_grad)
        out_specs, out_shapes = [o_spec, one_row], [out_shapes, jax.ShapeDtypeStruct((1, n), F32)]
    aliases = {}
    if into is not None:
        assert into.shape == tuple(o_shape) and into.dtype == out_dtype, (into.shape, o_shape)
        in_specs.append(pl.BlockSpec(memory_space=pl.ANY))
        args += (into,)
        aliases = {n_in - 1: 0}
    return pl.pallas_call(
        body, grid=(m // tm, n // tn, nk), in_specs=in_specs, out_specs=out_specs,
        out_shape=out_shapes, name=name, input_output_aliases=aliases,
        scratch_shapes=[pltpu.VMEM((tm, tn), F32)] if nk > 1 else [],
        compiler_params=_params(("arbitrary" if has_grad else "parallel", "parallel", "arbitrary")),
    )(*args)


def _row_spec(tm, cw, coff):
    return pl.BlockSpec((tm, cw), lambda i, j: (i, j + coff))


def _full_spec(shape):
    return pl.BlockSpec(shape, lambda i, j: (0,) * len(shape))


def tile_map(fn, rows, params, outs, *, tm, ncol, name):
    t = rows[0][0].shape[0]
    nin = len(rows) + len(params)

    def body(*refs):
        res = fn(*[r[...] for r in refs[:nin]])
        res = res if isinstance(res, (tuple, list)) else (res,)
        for o_ref, r in zip(refs[nin:], res):
            o_ref[...] = r.astype(o_ref.dtype)

    in_specs = [_row_spec(tm, cw, coff) for (_, cw, coff) in rows] + [_full_spec(p.shape) for p in params]
    res = pl.pallas_call(
        body, grid=(t // tm, ncol), in_specs=in_specs,
        out_specs=[_row_spec(tm, cw, 0) for (cw, _) in outs],
        out_shape=[jax.ShapeDtypeStruct((t, cw * ncol), dt) for (cw, dt) in outs], name=name,
        compiler_params=_params(("parallel", "parallel")),
    )(*[r[0] for r in rows], *params)
    return res


def tile_vjp(fn, rows, params, cts, *, n_diff, tm, ncol, name, grad_dtypes=None):
    t = rows[0][0].shape[0]
    nr, npar, nct = len(rows), len(params), len(cts)

    def body(*refs):
        vals = [r[...] for r in refs[:nr + npar + nct]]
        diff, rest, pars = vals[:n_diff], vals[n_diff:nr], vals[nr:nr + npar]
        ctv = vals[nr + npar:nr + npar + nct]
        outs_ref = refs[nr + npar + nct:]

        def f(*a):
            res = fn(*a[:n_diff], *rest, *a[n_diff:])
            return tuple(res) if isinstance(res, (tuple, list)) else (res,)

        primal, vjp = jax.vjp(f, *[d.astype(F32) for d in diff], *pars)
        grads = vjp(tuple(c.astype(o.dtype) for c, o in zip(ctv, primal)))
        for q in range(n_diff):
            outs_ref[q][...] = grads[q].astype(outs_ref[q].dtype)
        first = (pl.program_id(0) == 0) & (pl.program_id(1) == 0)
        for q in range(npar):
            o_ref, g = outs_ref[n_diff + q], grads[n_diff + q]

            @pl.when(first)
            def _(o_ref=o_ref, g=g):
                o_ref[...] = g

            @pl.when(jnp.logical_not(first))
            def _(o_ref=o_ref, g=g):
                o_ref[...] += g

    in_specs = [_row_spec(tm, cw, coff) for (_, cw, coff) in rows] + [_full_spec(p.shape) for p in params]
    in_specs += [_row_spec(tm, cw, coff) for (_, cw, coff) in cts]
    args = [r[0] for r in rows] + list(params) + [c[0] for c in cts]
    out_specs = [_row_spec(tm, rows[q][1], 0) for q in range(n_diff)] + [_full_spec(p.shape) for p in params]
    grad_dtypes = grad_dtypes or [F32] * n_diff
    out_shape = [jax.ShapeDtypeStruct((t, rows[q][1] * ncol), grad_dtypes[q]) for q in range(n_diff)]
    out_shape += [jax.ShapeDtypeStruct(p.shape, F32) for p in params]
    res = pl.pallas_call(
        body, grid=(t // tm, ncol), in_specs=in_specs, out_specs=out_specs, out_shape=out_shape, name=name,
        compiler_params=_params(("arbitrary", "arbitrary")),
    )(*args)
    return res[:n_diff], res[n_diff:]


def _silu(x):
    return x * jax.nn.sigmoid(x)


def _f_norm(h, w):
    return h * lax.rsqrt(jnp.mean(h * h, axis=-1, keepdims=True) + EPS) * w


def _f_gnorm(o, z, w):
    return _f_norm(o, w) * _silu(z)


def _f_act(gate, val):
    return _silu(gate) * val


def _f_ple(gl, pe):
    return jax.nn.sigmoid(gl) * pe


def _f_betag(pt, alog, dtb):
    lane = lax.broadcasted_iota(jnp.int32, (1, LANES), 1)
    z = pt + dtb
    softplus = jnp.maximum(z, 0.0) + jnp.log(1.0 + jnp.exp(-jnp.abs(z)))
    g = -jnp.exp(alog) * softplus
    return jnp.where(lane < N_HEADS_A, jax.nn.sigmoid(pt), jnp.where(lane < 2 * N_HEADS_A, g, 0.0))


CONV_TM = 256
CONV_CW = 1024


def _shift_down(x, prev, s, row):
    rp = jnp.tile(pltpu.roll(prev, s, 0), (x.shape[0] // SUBLANES, 1))
    return jnp.where(row < s, rp, pltpu.roll(x, s, 0))


def _shift_up(x, nxt, s, row):
    tm = x.shape[0]
    rn = jnp.tile(pltpu.roll(nxt, SUBLANES - s, 0), (tm // SUBLANES, 1))
    return jnp.where(row >= tm - s, rn, pltpu.roll(x, tm - s, 0))


def _conv_taps(x, prev, w_ref, cols, row):
    k = w_ref.shape[0]
    y = x * w_ref[pl.ds(k - 1, 1), cols]
    for s in range(1, k):
        y = y + _shift_down(x, prev, s, row) * w_ref[pl.ds(k - 1 - s, 1), cols]
    return y


def _lane_chunks(cw):
    return [slice(cb * LANES, (cb + 1) * LANES) for cb in range(cw // LANES)]


def conv_fwd(x, w, *, name):
    t = x.shape[0]
    k, c = w.shape
    tm, cw = min(CONV_TM, t), CONV_CW
    nb8 = tm // SUBLANES

    def body(x_ref, p_ref, w_ref, o_ref):
        first = pl.program_id(1) == 0
        row = lax.broadcasted_iota(jnp.int32, (tm, LANES), 0)
        for cols in _lane_chunks(cw):
            o_ref[:, cols] = _conv_taps(x_ref[:, cols], jnp.where(first, 0.0, p_ref[:, cols]), w_ref, cols, row)

    return pl.pallas_call(
        body, grid=(c // cw, t // tm),
        in_specs=[pl.BlockSpec((tm, cw), lambda j, i: (i, j)),
                  pl.BlockSpec((SUBLANES, cw), lambda j, i: (jnp.maximum(i * nb8 - 1, 0), j)),
                  pl.BlockSpec((k, cw), lambda j, i: (0, j))],
        out_specs=pl.BlockSpec((tm, cw), lambda j, i: (i, j)),
        out_shape=jax.ShapeDtypeStruct((t, c), F32), name=name,
        compiler_params=_params(("parallel", "parallel")),
    )(x, x, w)


def conv_bwd(dy, x, w, *, name):
    t = x.shape[0]
    k, c = w.shape
    tm, cw = min(CONV_TM, t), CONV_CW
    nb8 = tm // SUBLANES
    ni = t // tm

    def body(dy_ref, dn_ref, x_ref, p_ref, w_ref, dx_ref, dw_ref):
        i = pl.program_id(1)
        first, last = i == 0, i == ni - 1
        row = lax.broadcasted_iota(jnp.int32, (tm, LANES), 0)
        for cols in _lane_chunks(cw):
            dyv, xv = dy_ref[:, cols], x_ref[:, cols]
            nxt = jnp.where(last, 0.0, dn_ref[:, cols])
            prev = jnp.where(first, 0.0, p_ref[:, cols])
            dx = dyv * w_ref[pl.ds(k - 1, 1), cols]
            dws = [jnp.sum(dyv * xv, axis=0, keepdims=True)]
            for s in range(1, k):
                dx = dx + _shift_up(dyv, nxt, s, row) * w_ref[pl.ds(k - 1 - s, 1), cols]
                dws.append(jnp.sum(dyv * _shift_down(xv, prev, s, row), axis=0, keepdims=True))
            dx_ref[:, cols] = dx.astype(dx_ref.dtype)
            for s in range(k):
                @pl.when(first)
                def _(s=s, dws=dws, cols=cols):
                    dw_ref[pl.ds(k - 1 - s, 1), cols] = dws[s]

                @pl.when(jnp.logical_not(first))
                def _(s=s, dws=dws, cols=cols):
                    dw_ref[pl.ds(k - 1 - s, 1), cols] += dws[s]

    return pl.pallas_call(
        body, grid=(c // cw, ni),
        in_specs=[pl.BlockSpec((tm, cw), lambda j, i: (i, j)),
                  pl.BlockSpec((SUBLANES, cw), lambda j, i: (jnp.minimum((i + 1) * nb8, t // SUBLANES - 1), j)),
                  pl.BlockSpec((tm, cw), lambda j, i: (i, j)),
                  pl.BlockSpec((SUBLANES, cw), lambda j, i: (jnp.maximum(i * nb8 - 1, 0), j)),
                  pl.BlockSpec((k, cw), lambda j, i: (0, j))],
        out_specs=[pl.BlockSpec((tm, cw), lambda j, i: (i, j)), pl.BlockSpec((k, cw), lambda j, i: (0, j))],
        out_shape=[jax.ShapeDtypeStruct((t, c), BF16), jax.ShapeDtypeStruct((k, c), F32)], name=name,
        compiler_params=_params(("parallel", "arbitrary")),
    )(dy, dy, x, x, w)


FFN_TM = 128
FFN_TM_FWD = 256
FFN_CW = D_FF // 2


def _ffn_specs(t, tm, cw, k):
    ncol = D_FF // cw
    cur = lambda off: pl.BlockSpec((tm, cw), lambda j, i: (i, j + off))
    prev = lambda off, hr: pl.BlockSpec((hr, cw), lambda j, i: (jnp.maximum(i * (tm // hr) - 1, 0), j + off))
    nxt = lambda off, hr: pl.BlockSpec((hr, cw), lambda j, i: (jnp.minimum((i + 1) * (tm // hr), t // hr - 1), j + off))
    taps = lambda off: pl.BlockSpec((k, cw), lambda j, i: (0, j + off))
    return cur, prev, nxt, taps, ncol


def _rows_before(ref, cols, first):
    return jnp.where(first, 0.0, ref[ref.shape[0] - SUBLANES:, cols].astype(F32))


def conv_act_fwd(u, w, *, name):
    t, k = u.shape[0], w.shape[0]
    tm, cw = min(FFN_TM_FWD, t), FFN_CW
    cur, prev, _, taps, ncol = _ffn_specs(t, tm, cw, k)

    def body(ug_ref, pg_ref, uv_ref, pv_ref, wg_ref, wv_ref, o_ref):
        first = pl.program_id(1) == 0
        row = lax.broadcasted_iota(jnp.int32, (tm, LANES), 0)
        for cb in range(cw // LANES):
            cols = slice(cb * LANES, (cb + 1) * LANES)
            cg = _conv_taps(ug_ref[:, cols].astype(F32), _rows_before(pg_ref, cols, first), wg_ref, cols, row)
            cv = _conv_taps(uv_ref[:, cols].astype(F32), _rows_before(pv_ref, cols, first), wv_ref, cols, row)
            o_ref[:, cols] = _f_act(cg, cv).astype(o_ref.dtype)

    return pl.pallas_call(
        body, grid=(ncol, t // tm),
        in_specs=[cur(0), prev(0, BF16_ROWS), cur(ncol), prev(ncol, BF16_ROWS), taps(0), taps(ncol)],
        out_specs=cur(0), out_shape=jax.ShapeDtypeStruct((t, D_FF), BF16), name=name,
        compiler_params=_params(("parallel", "parallel")),
    )(u, u, u, u, w, w)


def conv_act_bwd(u, dact, w, *, name):
    t, k = u.shape[0], w.shape[0]
    tm, cw = min(FFN_TM, t), FFN_CW
    cur, prev, nxt, taps, ncol = _ffn_specs(t, tm, cw, k)
    ni = t // tm

    def body(ug_ref, pg_ref, ng_ref, uv_ref, pv_ref, nv_ref, d_ref, dn_ref, wg_ref, wv_ref, dg_ref, dv_ref, dwg_ref, dwv_ref):
        i = pl.program_id(1)
        first, last = i == 0, i == ni - 1
        row = lax.broadcasted_iota(jnp.int32, (tm, LANES), 0)
        row8 = lax.broadcasted_iota(jnp.int32, (SUBLANES, LANES), 0)
        for cb in range(cw // LANES):
            cols = slice(cb * LANES, (cb + 1) * LANES)
            ug, uv = ug_ref[:, cols].astype(F32), uv_ref[:, cols].astype(F32)
            pg, pv = _rows_before(pg_ref, cols, first), _rows_before(pv_ref, cols, first)
            sg = [ug] + [_shift_down(ug, pg, s, row) for s in range(1, k)]
            sv = [uv] + [_shift_down(uv, pv, s, row) for s in range(1, k)]
            taps = lambda xs, w_ref: sum(xs[s] * w_ref[pl.ds(k - 1 - s, 1), cols] for s in range(k))
            _, vjp = jax.vjp(_f_act, taps(sg, wg_ref), taps(sv, wv_ref))
            dcg, dcv = vjp(d_ref[:, cols])
            after = lambda ref: ref[:SUBLANES, cols].astype(F32)
            _, vjp_n = jax.vjp(_f_act, _conv_taps(after(ng_ref), ug[tm - SUBLANES:], wg_ref, cols, row8),
                               _conv_taps(after(nv_ref), uv[tm - SUBLANES:], wv_ref, cols, row8))
            dcgn, dcvn = vjp_n(jnp.where(last, 0.0, dn_ref[:, cols]))
            for dc, dcn, xs, w_ref, dx_ref, dw_ref in ((dcg, dcgn, sg, wg_ref, dg_ref, dwg_ref),
                                                       (dcv, dcvn, sv, wv_ref, dv_ref, dwv_ref)):
                dx = dc * w_ref[pl.ds(k - 1, 1), cols]
                dws = [jnp.sum(dc * xs[0], axis=0, keepdims=True)]
                for s in range(1, k):
                    dx = dx + _shift_up(dc, dcn, s, row) * w_ref[pl.ds(k - 1 - s, 1), cols]
                    dws.append(jnp.sum(dc * xs[s], axis=0, keepdims=True))
                dx_ref[:, cols] = dx.astype(dx_ref.dtype)
                for s in range(k):
                    @pl.when(first)
                    def _(s=s, dw_ref=dw_ref, dws=dws):
                        dw_ref[pl.ds(k - 1 - s, 1), cols] = dws[s]

                    @pl.when(jnp.logical_not(first))
                    def _(s=s, dw_ref=dw_ref, dws=dws):
                        dw_ref[pl.ds(k - 1 - s, 1), cols] += dws[s]

    half = jax.ShapeDtypeStruct((t, D_FF), BF16)
    dwh = jax.ShapeDtypeStruct((k, D_FF), F32)
    return pl.pallas_call(
        body, grid=(ncol, ni),
        in_specs=[cur(0), prev(0, BF16_ROWS), nxt(0, BF16_ROWS), cur(ncol), prev(ncol, BF16_ROWS), nxt(ncol, BF16_ROWS),
                  cur(0), nxt(0, SUBLANES), taps(0), taps(ncol)],
        out_specs=[cur(0), cur(0), taps(0), taps(0)], out_shape=[half, half, dwh, dwh], name=name,
        compiler_params=_params(("parallel", "arbitrary")),
    )(u, u, u, u, u, u, dact, dact, w, w)


def _each(f, *lists):
    return [f(*a) for a in zip(*lists)]


@jax.custom_vjp
def _inv_unit_lower(lms):
    return _inv_blocks(lms)


def _inv_blocks(lms):
    c = lms[0].shape[0]
    ri = lax.broadcasted_iota(jnp.int32, (c, c), 0)
    ci = lax.broadcasted_iota(jnp.int32, (c, c), 1)
    eye = (ri == ci).astype(F32)
    dms = _each(lambda lm: eye - jnp.where((ri >> 1) == (ci >> 1), lm, 0.0), lms)
    for lv in range(1, int(math.log2(c))):
        below = ((ri >> (lv + 1)) == (ci >> (lv + 1))) & ((ri >> lv) != (ci >> lv))
        dbs = _each(lambda dm: dm.astype(BF16), dms)
        ods = _each(lambda lm, db: _dot(jnp.where(below, lm, 0.0).astype(BF16), db).astype(BF16), lms, dbs)
        dms = _each(lambda dm, db, od: dm - _dot(db, od), dms, dbs, ods)
    return dms


def _inv_fwd(lms):
    tms = _inv_blocks(lms)
    return tms, tms


def _inv_bwd(tms, dts):
    tbs = _each(lambda tm: tm.astype(BF16), tms)
    mid = _each(lambda tb, dt: _dot(tb, dt.astype(BF16), TN).astype(BF16), tbs, dts)
    return (_each(lambda m, tb: -_dot(m, tb, NT), mid, tbs),)


_inv_unit_lower.defvjp(_inv_fwd, _inv_bwd)


@jax.custom_vjp
def _inv_known(lms, tms):
    return tms


_inv_known.defvjp(lambda lms, tms: (tms, tms), lambda tms, dts: _inv_bwd(tms, dts) + (_each(jnp.zeros_like, tms),))


def _l2n(x):
    return x * lax.rsqrt(jnp.sum(x * x, axis=-1, keepdims=True) + EPS)


def _prep_fn(cqs, cks, cvs, bg, sel_b, sel_g, tms=None):
    c = cqs[0].shape[0]
    ri = lax.broadcasted_iota(jnp.int32, (c, c), 0)
    ci = lax.broadcasted_iota(jnp.int32, (c, c), 1)
    eye = (ri == ci).astype(F32)
    incl, strict = ci <= ri, ci < ri
    last = lax.broadcasted_iota(jnp.int32, (c, 1), 0) == c - 1
    to_row = lambda col: jnp.sum(col * eye, axis=0, keepdims=True)
    qs = _each(lambda a: _l2n(_silu(a)) * (HEAD_DIM_A ** -0.5), cqs)
    ks = _each(lambda a: _l2n(_silu(a)), cks)
    vbs = _each(lambda a: _silu(a).astype(BF16), cvs)
    betas = _each(lambda m: jnp.sum(bg * m, axis=1, keepdims=True), sel_b)
    gs = _each(lambda m: jnp.sum(bg * m, axis=1, keepdims=True), sel_g)
    gcss = _each(lambda g: jnp.sum(jnp.where(incl, to_row(g), 0.0), axis=1, keepdims=True), gs)
    gtots = _each(lambda gcs: jnp.sum(jnp.where(last, gcs, 0.0), axis=0, keepdims=True), gcss)
    decays = _each(lambda gcs: jnp.exp(jnp.where(incl, gcs - to_row(gcs), NEG)), gcss)
    kbs = _each(lambda k: k.astype(BF16), ks)
    lms = _each(lambda beta, kb, dec: jnp.where(strict, beta * _dot(kb, kb, NT) * dec, 0.0), betas, kbs, decays)
    tms = _inv_unit_lower(lms) if tms is None else _inv_known(lms, tms)
    ams = _each(lambda tm, beta: (tm * to_row(beta)).astype(BF16), tms, betas)
    gams = _each(jnp.exp, gcss)
    u0s = _each(_dot, ams, vbs)
    wks = _each(lambda am, gam, k: _dot(am, (gam * k).astype(BF16)), ams, gams, ks)
    qks = _each(lambda q, kb, dec: _dot(q.astype(BF16), kb, NT) * dec, qs, kbs, decays)
    qds = _each(lambda q, gam: q * gam, qs, gams)
    kds = _each(lambda k, gtot, gcs: k * jnp.exp(gtot - gcs), ks, gtots, gcss)
    gls = _each(lambda gtot: jnp.exp(gtot) * jnp.ones((SUBLANES, LANES), F32), gtots)
    return u0s, wks, qds, kds, qks, gls, tms


def _head_masks(h):
    lane = lax.broadcasted_iota(jnp.int32, (1, LANES), 1)
    return (lane == h).astype(F32), (lane == h + N_HEADS_A).astype(F32)


def _hsl(j):
    return slice(j * HEAD_DIM_A, (j + 1) * HEAD_DIM_A)


def gnorm_fwd(o, zsrc, w, *, name):
    t, width = o.shape
    tm = min(256, t)
    zoff = zsrc.shape[1] // width - 1

    def body(o_ref, z_ref, w_ref, out_ref):
        for h in range(N_HEADS_A):
            out_ref[:, _hsl(h)] = _f_gnorm(o_ref[:, _hsl(h)], z_ref[:, _hsl(h)], w_ref[...]).astype(out_ref.dtype)

    rows = pl.BlockSpec((tm, width), lambda i: (i, 0))
    return pl.pallas_call(
        body, grid=(t // tm,),
        in_specs=[rows, pl.BlockSpec((tm, width), lambda i: (i, zoff)), pl.BlockSpec(w.shape, lambda i: (0, 0))],
        out_specs=rows, out_shape=jax.ShapeDtypeStruct((t, width), BF16), name=name, compiler_params=_params(("parallel",)),
    )(o, zsrc, w)


def gnorm_bwd(o, zsrc, w, don, *, name):
    t, width = o.shape
    tm = min(256, t)
    zoff = zsrc.shape[1] // width - 1

    def body(o_ref, z_ref, w_ref, d_ref, do_ref, dz_ref, dw_ref):
        dw = jnp.zeros(w.shape, F32)
        for h in range(N_HEADS_A):
            _, vjp = jax.vjp(_f_gnorm, o_ref[:, _hsl(h)], z_ref[:, _hsl(h)], w_ref[...])
            do, dz, dwh = vjp(d_ref[:, _hsl(h)])
            do_ref[:, _hsl(h)] = do.astype(do_ref.dtype)
            dz_ref[:, _hsl(h)] = dz.astype(dz_ref.dtype)
            dw = dw + dwh
        first = pl.program_id(0) == 0

        @pl.when(first)
        def _():
            dw_ref[...] = dw

        @pl.when(jnp.logical_not(first))
        def _():
            dw_ref[...] += dw

    rows = pl.BlockSpec((tm, width), lambda i: (i, 0))
    wspec = pl.BlockSpec(w.shape, lambda i: (0, 0))
    return pl.pallas_call(
        body, grid=(t // tm,),
        in_specs=[rows, pl.BlockSpec((tm, width), lambda i: (i, zoff)), wspec, rows],
        out_specs=[rows, rows, wspec],
        out_shape=[jax.ShapeDtypeStruct((t, width), BF16)] * 2 + [jax.ShapeDtypeStruct(w.shape, F32)], name=name,
        compiler_params=_params(("arbitrary",)),
    )(o, zsrc, w, don)


def delta_prep(cqkv, bg, *, name):
    t = cqkv.shape[0]
    nh, hd, n = N_HEADS_A, HEAD_DIM_A, t // CHUNK

    def body(cq_ref, ck_ref, cv_ref, bg_ref, u0_ref, wk_ref, qd_ref, kd_ref, qk_ref, tm_ref, gl_ref):
        heads = range(nh)
        masks = [_head_masks(j) for j in heads]
        res = _prep_fn([cq_ref[:, _hsl(j)] for j in heads], [ck_ref[:, _hsl(j)] for j in heads],
                       [cv_ref[:, _hsl(j)] for j in heads], bg_ref[...], [m[0] for m in masks], [m[1] for m in masks])
        for o_ref, rs in zip((u0_ref, wk_ref, qd_ref, kd_ref, qk_ref, tm_ref), res[:5] + (res[6],)):
            for j in heads:
                o_ref[:, _hsl(j)] = rs[j].astype(o_ref.dtype)
        for j in heads:
            gl_ref[j * SUBLANES:(j + 1) * SUBLANES, :] = res[5][j]

    blk = lambda off: pl.BlockSpec((CHUNK, nh * hd), lambda i: (i, off))
    res = pl.pallas_call(
        body, grid=(n,),
        in_specs=[blk(0), blk(1), blk(2), pl.BlockSpec((CHUNK, LANES), lambda i: (i, 0))],
        out_specs=[blk(0)] * 6 + [pl.BlockSpec((nh * SUBLANES, LANES), lambda i: (i, 0))],
        out_shape=[jax.ShapeDtypeStruct((t, nh * hd), dt) for dt in (F32, BF16, BF16, BF16, BF16, F32)]
        + [jax.ShapeDtypeStruct((n * nh * SUBLANES, LANES), F32)],
        name=name, compiler_params=_params(("parallel",)),
    )(cqkv, cqkv, cqkv, bg)
    return [*res[:5], res[6]], res[5]


def delta_prep_bwd(cqkv, bg, tms, cts, *, name):
    t = cqkv.shape[0]
    nh, hd, n = N_HEADS_A, HEAD_DIM_A, t // CHUNK

    def body(cq_ref, ck_ref, cv_ref, bg_ref, tm_ref, c0, c1, c2, c3, c4, c5, dc_ref, dbg_ref):
        heads = range(nh)
        masks = [_head_masks(j) for j in heads]
        known = [tm_ref[:, _hsl(j)] for j in heads]
        _, vjp = jax.vjp(lambda a, b, c, d: _prep_fn(a, b, c, d, [m[0] for m in masks], [m[1] for m in masks], known)[:6],
                         [cq_ref[:, _hsl(j)] for j in heads], [ck_ref[:, _hsl(j)] for j in heads],
                         [cv_ref[:, _hsl(j)] for j in heads], bg_ref[...])
        cts = tuple([c[:, _hsl(j)] for j in heads] for c in (c0, c1, c2, c3, c4))
        dqs, dks, dvs, dbg = vjp(cts + ([c5[j * SUBLANES:(j + 1) * SUBLANES, :] for j in heads],))
        for part, ds in enumerate((dqs, dks, dvs)):
            for j in heads:
                dc_ref[:, _hsl(part * nh + j)] = ds[j]
        dbg_ref[...] = dbg

    blk = lambda off: pl.BlockSpec((CHUNK, nh * hd), lambda i: (i, off))
    gl_spec = pl.BlockSpec((nh * SUBLANES, LANES), lambda i: (i, 0))
    bg_spec = pl.BlockSpec((CHUNK, LANES), lambda i: (i, 0))
    return pl.pallas_call(
        body, grid=(n,),
        in_specs=[blk(0), blk(1), blk(2), bg_spec] + [blk(0)] * 6 + [gl_spec],
        out_specs=[pl.BlockSpec((CHUNK, 3 * nh * hd), lambda i: (i, 0)), bg_spec],
        out_shape=[jax.ShapeDtypeStruct((t, 3 * nh * hd), F32), jax.ShapeDtypeStruct((t, LANES), F32)],
        name=name, compiler_params=_params(("parallel",)),
    )(cqkv, cqkv, cqkv, bg, tms, *cts)


def delta_scan(u0, wk, qd, kd, qk, gl, *, name):
    t = u0.shape[0]
    nh, hd, n = N_HEADS_A, HEAD_DIM_A, t // CHUNK

    def body(u0_ref, wk_ref, qd_ref, kd_ref, qk_ref, gl_ref, o_ref, sin_ref, s_ref):
        @pl.when(pl.program_id(0) == 0)
        def _():
            s_ref[...] = jnp.zeros_like(s_ref)

        heads = list(range(nh))
        cols = lambda ref: [ref[:, _hsl(h)].astype(BF16) for h in heads]
        ss = [s_ref[h] for h in heads]
        for h in heads:
            sin_ref[h] = ss[h]
        sbs = _each(lambda s: s.astype(BF16), ss)
        ubs = _each(lambda h, wkb, sb: (u0_ref[:, _hsl(h)] - _dot(wkb, sb)).astype(BF16), heads, cols(wk_ref), sbs)
        os_ = _each(lambda qdb, sb, qkb, ub: _dot(qdb, sb) + _dot(qkb, ub), cols(qd_ref), sbs, cols(qk_ref), ubs)
        sn = _each(lambda h, s, kdb, ub: gl_ref[pl.ds(h * SUBLANES, 1), :] * s + _dot(kdb, ub, TN), heads, ss, cols(kd_ref), ubs)
        for h in heads:
            o_ref[:, _hsl(h)] = os_[h]
            s_ref[h] = sn[h]

    blk = pl.BlockSpec((CHUNK, nh * hd), lambda i: (i, 0))
    return pl.pallas_call(
        body, grid=(n,),
        in_specs=[blk] * 5 + [pl.BlockSpec((nh * SUBLANES, LANES), lambda i: (i, 0))],
        out_specs=[blk, pl.BlockSpec((None, nh, hd, hd), lambda i: (i, 0, 0, 0))],
        out_shape=[jax.ShapeDtypeStruct((t, nh * hd), F32), jax.ShapeDtypeStruct((n, nh, hd, hd), F32)],
        scratch_shapes=[pltpu.VMEM((nh, hd, hd), F32)], name=name,
        compiler_params=_params(("arbitrary",)),
    )(u0, wk, qd, kd, qk, gl)


def delta_scan_bwd(do, u0, wk, qd, kd, qk, gl, s_in, *, name):
    t = u0.shape[0]
    nh, hd, n = N_HEADS_A, HEAD_DIM_A, t // CHUNK

    def body(do_ref, u0_ref, wk_ref, qd_ref, kd_ref, qk_ref, gl_ref, sin_ref,
             du0_ref, dwk_ref, dqd_ref, dkd_ref, dqk_ref, dgl_ref, ds_ref):
        @pl.when(pl.program_id(0) == 0)
        def _():
            ds_ref[...] = jnp.zeros_like(ds_ref)

        corner = (lax.broadcasted_iota(jnp.int32, (SUBLANES, LANES), 0) == 0) & (lax.broadcasted_iota(jnp.int32, (SUBLANES, LANES), 1) == 0)
        heads = list(range(nh))
        cols = lambda ref: [ref[:, _hsl(h)].astype(BF16) for h in heads]
        ss, dss = [sin_ref[h] for h in heads], [ds_ref[h] for h in heads]
        sbs, dsbs = _each(lambda s: s.astype(BF16), ss), _each(lambda d: d.astype(BF16), dss)
        dobs, wkbs, qdbs, kdbs, qkbs = cols(do_ref), cols(wk_ref), cols(qd_ref), cols(kd_ref), cols(qk_ref)
        ubs = _each(lambda h, wkb, sb: (u0_ref[:, _hsl(h)] - _dot(wkb, sb)).astype(BF16), heads, wkbs, sbs)
        dus = _each(lambda qkb, dob, kdb, dsb: _dot(qkb, dob, TN) + _dot(kdb, dsb), qkbs, dobs, kdbs, dsbs)
        dubs = _each(lambda du: du.astype(BF16), dus)
        dwks = _each(lambda dub, sb: -_dot(dub, sb, NT), dubs, sbs)
        dqds = _each(lambda dob, sb: _dot(dob, sb, NT), dobs, sbs)
        dkds = _each(lambda ub, dsb: _dot(ub, dsb, NT), ubs, dsbs)
        dqks = _each(lambda dob, ub: _dot(dob, ub, NT), dobs, ubs)
        dgls = _each(lambda s, d: jnp.sum(jnp.sum(s * d, axis=1, keepdims=True), axis=0, keepdims=True), ss, dss)
        dsn = _each(lambda h, d, qdb, dob, wkb, dub: gl_ref[pl.ds(h * SUBLANES, 1), :] * d + _dot(qdb, dob, TN) - _dot(wkb, dub, TN),
                    heads, dss, qdbs, dobs, wkbs, dubs)
        for h in heads:
            du0_ref[:, _hsl(h)] = dus[h]
            dwk_ref[:, _hsl(h)] = dwks[h]
            dqd_ref[:, _hsl(h)] = dqds[h]
            dkd_ref[:, _hsl(h)] = dkds[h]
            dqk_ref[:, _hsl(h)] = dqks[h]
            dgl_ref[h * SUBLANES:(h + 1) * SUBLANES, :] = jnp.where(corner, dgls[h], 0.0)
            ds_ref[h] = dsn[h]

    blk = pl.BlockSpec((CHUNK, nh * hd), lambda i: (n - 1 - i, 0))
    gl_spec = pl.BlockSpec((nh * SUBLANES, LANES), lambda i: (n - 1 - i, 0))
    return pl.pallas_call(
        body, grid=(n,),
        in_specs=[blk] * 6 + [gl_spec, pl.BlockSpec((None, nh, hd, hd), lambda i: (n - 1 - i, 0, 0, 0))],
        out_specs=[blk] * 5 + [gl_spec],
        out_shape=[jax.ShapeDtypeStruct((t, nh * hd), F32)] * 5 + [jax.ShapeDtypeStruct((n * nh * SUBLANES, LANES), F32)],
        scratch_shapes=[pltpu.VMEM((nh, hd, hd), F32)], name=name,
        compiler_params=_params(("arbitrary",)),
    )(do, u0, wk, qd, kd, qk, gl, s_in)


N_PAIRS = N_HEADS_B // 2
PAIRS_PER_KV = N_PAIRS // N_KV_B


def _psl(j):
    return slice(j * LANES, (j + 1) * LANES)


KV_STEP = 4


def _att_fn(qps, kcs, kps, vcs, vps, sinks, kv0, first):
    w = WINDOW
    lane = lax.broadcasted_iota(jnp.int32, (1, LANES), 1)
    lo = (lane < HEAD_DIM_B).astype(F32)
    qi = lax.broadcasted_iota(jnp.int32, (w, w), 0)
    kj = lax.broadcasted_iota(jnp.int32, (w, w), 1)
    dist_c = (qi - kj).astype(F32)
    valid_c = kj <= qi
    valid_p = (kj > qi) & (first < 0.5)
    bf = lambda xs: [a.astype(BF16) for a in xs]
    kcb, kpb, vcb, vpb = bf(kcs), bf(kps), bf(vcs), bf(vps)
    scale = HEAD_DIM_B ** -0.5
    heads = [(g, j, half) for g in range(len(kcs)) for j in range(PAIRS_PER_KV) for half in range(2)]
    kvs = [g for g, _, _ in heads]
    hmasks = [lo if half == 0 else 1.0 - lo for _, _, half in heads]
    hds = [2.0 * (PAIRS_PER_KV * (kv0 + g) + j) + half for g, j, half in heads]
    slopes = _each(lambda hd: jnp.exp(-(hd + 1.0) * (8.0 / N_HEADS_B * math.log(2.0))), hds)
    snks = _each(lambda hd: jnp.sum(sinks * (lane.astype(F32) == hd).astype(F32), axis=1, keepdims=True), hds)
    qhs = _each(lambda h, hm: (qps[h[0] * PAIRS_PER_KV + h[1]] * hm).astype(BF16), heads, hmasks)
    lcs = _each(lambda qh, g, sl: jnp.where(valid_c, _dot(qh, kcb[g], NT) * scale - sl * dist_c, NEG), qhs, kvs, slopes)
    lps = _each(lambda qh, g, sl: jnp.where(valid_p, _dot(qh, kpb[g], NT) * scale - sl * (dist_c + w), NEG), qhs, kvs, slopes)
    ms = _each(lambda lc, lp, sk: lax.stop_gradient(jnp.maximum(jnp.maximum(jnp.max(lc, axis=1, keepdims=True),
                                                                            jnp.max(lp, axis=1, keepdims=True)), sk)), lcs, lps, snks)
    ecs = _each(lambda lc, m: jnp.exp(lc - m), lcs, ms)
    eps = _each(lambda lp, m: jnp.exp(lp - m), lps, ms)
    invs = _each(lambda ec, ep, sk, m: 1.0 / (jnp.sum(ec, axis=1, keepdims=True) + jnp.sum(ep, axis=1, keepdims=True) + jnp.exp(sk - m)),
                 ecs, eps, snks, ms)
    ohs = _each(lambda ec, ep, inv, g, hm: (_dot((ec * inv).astype(BF16), vcb[g]) + _dot((ep * inv).astype(BF16), vpb[g])) * hm,
                ecs, eps, invs, kvs, hmasks)
    return [ohs[2 * j] + ohs[2 * j + 1] for j in range(len(qps))]


def _scalar11(v):
    return jnp.full((1, 1), v, F32)


def _att_specs(row_of):
    cur = pl.BlockSpec((WINDOW, KV_STEP * LANES), lambda i, kv: (row_of(i), kv))
    prev = pl.BlockSpec((WINDOW, KV_STEP * LANES), lambda i, kv: (jnp.maximum(row_of(i) - 1, 0), kv))
    qs = pl.BlockSpec((WINDOW, KV_STEP * PAIRS_PER_KV * LANES), lambda i, kv: (row_of(i), kv))
    return qs, cur, prev, pl.BlockSpec((1, LANES), lambda i, kv: (0, 0))


def swa_fwd(qsrc, kd, vd, sinks, *, name):
    t = kd.shape[0]
    nb = t // WINDOW
    npair = KV_STEP * PAIRS_PER_KV

    def body(q_ref, kc_ref, kp_ref, vc_ref, vp_ref, s_ref, o_ref):
        first = _scalar11((pl.program_id(0) == 0).astype(F32))
        kv0 = _scalar11((pl.program_id(1) * KV_STEP).astype(F32))
        per_kv = lambda ref: [ref[:, _psl(g)] for g in range(KV_STEP)]
        outs = _att_fn([q_ref[:, _psl(j)] for j in range(npair)], per_kv(kc_ref), per_kv(kp_ref), per_kv(vc_ref), per_kv(vp_ref),
                       s_ref[...], kv0, first)
        for j in range(npair):
            o_ref[:, _psl(j)] = outs[j].astype(o_ref.dtype)

    qs, cur, prev, sk = _att_specs(lambda i: i)
    return pl.pallas_call(
        body, grid=(nb, N_KV_B // KV_STEP), in_specs=[qs, cur, prev, cur, prev, sk],
        out_specs=qs, out_shape=jax.ShapeDtypeStruct((t, N_PAIRS * LANES), BF16), name=name,
        compiler_params=_params(("parallel", "parallel")),
    )(qsrc, kd, kd, vd, vd, sinks)


def swa_bwd(do, qsrc, kd, vd, sinks, *, name):
    t = kd.shape[0]
    nb = t // WINDOW

    npair = KV_STEP * PAIRS_PER_KV

    def body(do_ref, q_ref, kc_ref, kp_ref, vc_ref, vp_ref, s_ref, dq_ref, dk_ref, dv_ref, ds_ref, carry_k, carry_v):
        step, kvg = pl.program_id(0), pl.program_id(1)
        first = _scalar11((step == nb - 1).astype(F32))

        @pl.when((step == 0) & (kvg == 0))
        def _():
            carry_k[...] = jnp.zeros_like(carry_k)
            carry_v[...] = jnp.zeros_like(carry_v)
            ds_ref[...] = jnp.zeros_like(ds_ref)

        kv0 = _scalar11((kvg * KV_STEP).astype(F32))
        per_kv = lambda ref: [ref[:, _psl(g)].astype(F32) for g in range(KV_STEP)]
        _, vjp = jax.vjp(lambda *a: _att_fn(*a, kv0, first), [q_ref[:, _psl(j)].astype(F32) for j in range(npair)],
                         per_kv(kc_ref), per_kv(kp_ref), per_kv(vc_ref), per_kv(vp_ref), s_ref[...])
        dqs, dkc, dkp, dvc, dvp, dsk = vjp([do_ref[:, _psl(j)].astype(F32) for j in range(npair)])
        for j in range(npair):
            dq_ref[:, _psl(j)] = dqs[j].astype(dq_ref.dtype)
        ds_ref[...] += dsk
        fold = lambda g: g + pltpu.roll(g, HEAD_DIM_B, 1)
        for g in range(KV_STEP):
            kv = kvg * KV_STEP + g
            dk_ref[:, _psl(g)] = fold(dkc[g] + carry_k[kv]).astype(dk_ref.dtype)
            dv_ref[:, _psl(g)] = fold(dvc[g] + carry_v[kv]).astype(dv_ref.dtype)
            carry_k[kv] = dkp[g]
            carry_v[kv] = dvp[g]

    qs, cur, prev, sk = _att_specs(lambda i: nb - 1 - i)
    return pl.pallas_call(
        body, grid=(nb, N_KV_B // KV_STEP),
        in_specs=[qs, qs, cur, prev, cur, prev, sk],
        out_specs=[qs, cur, cur, sk],
        out_shape=[jax.ShapeDtypeStruct((t, N_PAIRS * LANES), BF16), jax.ShapeDtypeStruct((t, N_KV_B * LANES), BF16),
                   jax.ShapeDtypeStruct((t, N_KV_B * LANES), BF16), jax.ShapeDtypeStruct((1, LANES), F32)],
        scratch_shapes=[pltpu.VMEM((N_KV_B, WINDOW, LANES), F32), pltpu.VMEM((N_KV_B, WINDOW, LANES), F32)],
        name=name, compiler_params=_params(("arbitrary", "arbitrary")),
    )(do, qsrc, kd, kd, vd, vd, sinks)


def loss_head(h, tgt, w, *, name):
    t, d = h.shape
    tm = min(256, t)

    def body(h_ref, t_ref, w_ref, dh_ref, dw_ref, l_ref):
        tg = t_ref[...]

        def f(hv, wv):
            err = _f_norm(hv, wv) - tg
            return 0.5 * jnp.sum(jnp.sum(err * err, axis=1, keepdims=True), axis=0, keepdims=True) * (1.0 / d)

        lv, vjp = jax.vjp(f, h_ref[...], w_ref[...])
        dh, dw = vjp(jnp.ones((1, 1), F32))
        dh_ref[...] = dh
        first = pl.program_id(0) == 0

        @pl.when(first)
        def _():
            dw_ref[...] = dw
            l_ref[...] = lv * jnp.ones((1, LANES), F32)

        @pl.when(jnp.logical_not(first))
        def _():
            dw_ref[...] += dw
            l_ref[...] += lv * jnp.ones((1, LANES), F32)

    rows = pl.BlockSpec((tm, d), lambda i: (i, 0))
    one = lambda c: pl.BlockSpec((1, c), lambda i: (0, 0))
    return pl.pallas_call(
        body, grid=(t // tm,), in_specs=[rows, rows, one(d)], out_specs=[rows, one(d), one(LANES)],
        out_shape=[jax.ShapeDtypeStruct((t, d), F32), jax.ShapeDtypeStruct((1, d), F32), jax.ShapeDtypeStruct((1, LANES), F32)],
        name=name, compiler_params=_params(("arbitrary",)),
    )(h, tgt, w)


def _row_tile(r, cap=256):
    tr = r
    if r % SUBLANES == 0:
        for cand in range(SUBLANES, min(r, cap) + 1, SUBLANES):
            if r % cand == 0:
                tr = cand
    return tr


def _adamw_update(wv, gv, mv, vv):
    mn = ADAM_B1 * mv + (1.0 - ADAM_B1) * gv
    vn = ADAM_B2 * vv + (1.0 - ADAM_B2) * jnp.square(gv)
    m_hat = mn / (1.0 - ADAM_B1 ** ADAM_STEP)
    v_hat = vn / (1.0 - ADAM_B2 ** ADAM_STEP)
    return -ADAM_LR * (m_hat / (jnp.sqrt(v_hat) + ADAM_EPS) + ADAM_WD * wv), mn, vn


def adamw_layers(w, halves, m, v, *, name):
    nl, r, c = w.shape
    tr = _row_tile(r // 2)
    nbh = r // 2 // tr

    def body(w_ref, *rest):
        g_refs, m_ref, v_ref = rest[:2 * nl], rest[2 * nl], rest[2 * nl + 1]
        d_ref, mo_ref, vo_ref, go_ref = rest[2 * nl + 2:]
        layer, i = pl.program_id(0), pl.program_id(1)
        mine = (i < nbh) == (lax.axis_index("c") == 0)
        gv = jnp.where(mine, g_refs[0][...], g_refs[1][...])
        for k in range(1, nl):
            gv = jnp.where(layer == k, jnp.where(mine, g_refs[2 * k][...], g_refs[2 * k + 1][...]), gv)
        d_ref[...], mo_ref[...], vo_ref[...] = _adamw_update(w_ref[...], gv, m_ref[...], v_ref[...])
        go_ref[...] = gv

    spec3 = pl.BlockSpec((None, tr, c), lambda k, i: (k, i, 0))
    g_specs = [pl.BlockSpec((tr, c), lambda k, i, q=q: (jnp.where(k == q, i % nbh, 0), 0)) for q in range(nl) for _ in range(2)]
    return pl.pallas_call(
        body, grid=(nl, r // tr), in_specs=[spec3] + g_specs + [spec3, spec3], out_specs=[spec3] * 4,
        out_shape=[jax.ShapeDtypeStruct((nl, r, c), F32)] * 4, name=name, compiler_params=_params(("arbitrary", "arbitrary")),
    )(w, *[h for pair in halves for h in pair], m, v)


def adamw(w, g, m, v, *, name):
    r, c = w.shape
    tr = _row_tile(r)

    def body(w_ref, g_ref, m_ref, v_ref, d_ref, mo_ref, vo_ref):
        d_ref[...], mo_ref[...], vo_ref[...] = _adamw_update(w_ref[...], g_ref[...], m_ref[...], v_ref[...])

    spec = pl.BlockSpec((tr, c), lambda i: (i, 0))
    return pl.pallas_call(
        body, grid=(r // tr,), in_specs=[spec] * 4, out_specs=[spec] * 3,
        out_shape=[jax.ShapeDtypeStruct((r, c), F32)] * 3, name=name, compiler_params=_params(("parallel",)),
    )(w, g, m, v)


def _place():
    return lax.axis_index("x"), lax.axis_index("y"), lax.axis_index("c")


def allgather8(blk, *, name):
    def body(x_ref, out_ref, send_sems, recv_sems, local_sem):
        x, y, c = _place()
        me = 4 * x + 2 * y + c
        mine = pltpu.make_async_copy(x_ref, out_ref.at[me], local_sem)
        mine.start()
        sent = []
        for k in range(1, N_DEV):
            to = (x ^ ((k >> 2) & 1), y ^ ((k >> 1) & 1), c ^ (k & 1))
            cp = pltpu.make_async_remote_copy(src_ref=x_ref, dst_ref=out_ref.at[me], send_sem=send_sems.at[k - 1],
                                              recv_sem=recv_sems.at[k - 1], device_id=to, device_id_type=MESH)
            cp.start()
            sent.append(cp)
        for k in range(1, N_DEV):
            frm = me ^ k
            pltpu.make_async_remote_copy(src_ref=x_ref, dst_ref=out_ref.at[frm], send_sem=send_sems.at[k - 1],
                                         recv_sem=recv_sems.at[k - 1], device_id=(x, y, c), device_id_type=MESH).wait_recv()
        for cp in sent:
            cp.wait_send()
        mine.wait()

    vm = pl.BlockSpec(memory_space=pltpu.VMEM)
    return pl.pallas_call(
        body, in_specs=[vm], out_specs=vm, out_shape=jax.ShapeDtypeStruct((N_DEV,) + blk.shape, blk.dtype), name=name,
        scratch_shapes=[pltpu.SemaphoreType.DMA((N_DEV - 1,)), pltpu.SemaphoreType.DMA((N_DEV - 1,)), pltpu.SemaphoreType.DMA],
    )(blk)


def _other_chips(x, y):
    return [(1 - x, y), (x, 1 - y), (1 - x, 1 - y)]


def _hbm_call(body, ins, out_shapes, n_sems, name):
    hbm = pl.BlockSpec(memory_space=pl.ANY)
    return pl.pallas_call(
        body, in_specs=[hbm] * len(ins), out_specs=[hbm] * len(out_shapes), out_shape=out_shapes, name=name,
        scratch_shapes=[pltpu.SemaphoreType.DMA((n_sems,)), pltpu.SemaphoreType.DMA((n_sems,))],
    )(*ins)


def _half_rows(c, rh):
    return pl.ds(pl.multiple_of(c * rh, BF16_ROWS), rh)


def gather_units(units, *, name):
    nu = len(units)
    shapes = []
    for arr, layer_major in units:
        r, cols = arr.shape
        shapes.append(jax.ShapeDtypeStruct((2, N_CHIPS, r // 2, cols) if layer_major else (N_CHIPS, r, cols), arr.dtype))

    def body(*refs):
        in_refs, out_refs, send_sems, recv_sems = refs[:nu], refs[nu:2 * nu], refs[2 * nu], refs[2 * nu + 1]
        x, y, c = _place()
        me_chip = 2 * x + y
        sib = (x, y, 1 - c)
        chips = _other_chips(x, y)

        def copy(k, src, dst, to):
            return pltpu.make_async_remote_copy(src_ref=src, dst_ref=dst, send_sem=send_sems.at[k], recv_sem=recv_sems.at[k],
                                                device_id=to, device_id_type=MESH)

        first, passed, landing = [], [], []
        for u, (arr, layer_major) in enumerate(units):
            rh = arr.shape[0] // 2
            out_ref = out_refs[u]
            slot = (lambda chip, half, o=out_ref: o.at[half, chip]) if layer_major else \
                   (lambda chip, half, o=out_ref, rh=rh: o.at[chip, _half_rows(half, rh), :])
            my_half = in_refs[u].at[_half_rows(c, rh), :]
            for j, (cx, cy) in enumerate(chips):
                k = 6 * u + j
                first.append(copy(k, my_half, slot(me_chip, c), (cx, cy, c)))
                passed.append(copy(k + 3, slot(2 * cx + cy, c), slot(2 * cx + cy, c), sib))
                landing.append((copy(k, my_half, slot(2 * cx + cy, c), sib), copy(k + 3, my_half, slot(2 * cx + cy, 1 - c), sib)))
        for cp in first:
            cp.start()
        for (over_ici, _), fwd in zip(landing, passed):
            over_ici.wait_recv()
            fwd.start()
        for _, from_sibling in landing:
            from_sibling.wait_recv()
        for cp in first + passed:
            cp.wait_send()

    return _hbm_call(body, [a for a, _ in units], shapes, 6 * nu, name)


HBM_SPEC = pl.BlockSpec(memory_space=pltpu.HBM)
SEM_SPEC = pl.BlockSpec(memory_space=pltpu.SEMAPHORE)
ORDERED_EFFECT = pltpu.SideEffectType.DATAFLOW_SIDE_EFFECTING


def _split_start(body, srcs, land_shapes, after, *, name):
    nu = len(srcs)
    lands = [lax.empty(s.shape, s.dtype) for s in land_shapes]

    def whole(*refs):
        body(refs[:nu], refs[nu:2 * nu], refs[2 * nu + 1], refs[2 * nu + 2])
        refs[-1][...] = jnp.zeros((SUBLANES, LANES), F32)

    hbm = lambda a: pltpu.with_memory_space_constraint(a, pltpu.HBM)
    sems = pltpu.SemaphoreType.DMA((nu,))
    res = pl.pallas_call(
        whole, name=name, in_specs=[HBM_SPEC] * (2 * nu) + [pl.BlockSpec(memory_space=pl.ANY)],
        out_shape=[sems, sems] + [pltpu.HBM(a.shape, a.dtype) for a in srcs] + [pltpu.HBM(s.shape, s.dtype) for s in land_shapes]
        + [jax.ShapeDtypeStruct((SUBLANES, LANES), F32)],
        out_specs=[SEM_SPEC, SEM_SPEC] + [HBM_SPEC] * (2 * nu) + [pl.BlockSpec(memory_space=pltpu.VMEM)],
        input_output_aliases={q: 2 + q for q in range(2 * nu)},
        compiler_params=pltpu.CompilerParams(has_side_effects=ORDERED_EFFECT),
    )(*[hbm(a) for a in srcs], *[hbm(a) for a in lands], after)
    return res[0], res[1], res[2:2 + nu], res[2 + nu:2 + 2 * nu], res[-1]


def _split_wait(pending, moved, after, *, name):
    send_sems, recv_sems, srcs, lands, _ = pending
    nu = len(srcs)

    def body(*refs):
        land_refs, ssem, rsem = refs[nu:2 * nu], refs[2 * nu], refs[2 * nu + 1]
        x, y, c = _place()
        for u in range(nu):
            size = moved(land_refs[u])
            cp = pltpu.make_async_remote_copy(src_ref=size, dst_ref=size, send_sem=ssem.at[u], recv_sem=rsem.at[u],
                                              device_id=(x, y, c), device_id_type=MESH)
            cp.wait_send()
            cp.wait_recv()

    res = pl.pallas_call(
        body, name=name, in_specs=[HBM_SPEC] * (2 * nu) + [SEM_SPEC, SEM_SPEC, pl.BlockSpec(memory_space=pl.ANY)],
        out_shape=[pltpu.HBM(a.shape, a.dtype) for a in srcs] + [pltpu.HBM(a.shape, a.dtype) for a in lands],
        out_specs=[HBM_SPEC] * (2 * nu), input_output_aliases={q: q for q in range(2 * nu)},
        compiler_params=pltpu.CompilerParams(has_side_effects=ORDERED_EFFECT),
    )(*srcs, *lands, send_sems, recv_sems, after)
    return res[nu:]


def gather_start(shards, after, *, name):
    def body(src_refs, land_refs, send_sems, recv_sems):
        x, y, c = _place()
        for u, shard in enumerate(shards):
            rows = _half_rows(c, shard.shape[0] // 2)
            for cx, cy in _other_chips(x, y):
                for core in range(2):
                    pltpu.make_async_remote_copy(src_ref=src_refs[u].at[rows, :], dst_ref=land_refs[u].at[2 * x + y, rows, :],
                                                 send_sem=send_sems.at[u], recv_sem=recv_sems.at[u], device_id=(cx, cy, core),
                                                 device_id_type=MESH).start()

    return _split_start(body, shards, [jax.ShapeDtypeStruct((N_CHIPS,) + s.shape, s.dtype) for s in shards], after, name=name)


def gather_wait(pending, after, *, name):
    return _split_wait(pending, lambda land: land.at[pl.ds(0, N_CHIPS - 1)], after, name=name)


def scatter_start(pairs, *, name):
    def body(src_refs, land_refs, send_sems, recv_sems):
        x, y, c = _place()
        for u in range(len(pairs)):
            for j, (cx, cy) in enumerate(_other_chips(x, y)):
                pltpu.make_async_remote_copy(src_ref=src_refs[u].at[2 * cx + cy], dst_ref=land_refs[u].at[j], send_sem=send_sems.at[u],
                                             recv_sem=recv_sems.at[u], device_id=(cx, cy, c), device_id_type=MESH).start()

    return _split_start(body, pairs, [jax.ShapeDtypeStruct((N_CHIPS - 1,) + p.shape[1:], p.dtype) for p in pairs], pairs[0], name=name)


def scatter_wait(pending, after, *, name):
    return _split_wait(pending, lambda land: land, after, name=name)


def swap_units(units, *, name):
    nu = len(units)

    def body(*refs):
        g_refs, out_refs, send_sems, recv_sems = refs[:nu], refs[nu:2 * nu], refs[2 * nu], refs[2 * nu + 1]
        x, y, c = _place()
        cps = [pltpu.make_async_remote_copy(src_ref=g_refs[u].at[:, _half_rows(1 - c, units[u].shape[1] // 2), :], dst_ref=out_refs[u],
                                            send_sem=send_sems.at[u], recv_sem=recv_sems.at[u], device_id=(x, y, 1 - c),
                                            device_id_type=MESH) for u in range(nu)]
        for cp in cps:
            cp.start()
        for cp in cps:
            cp.wait()

    shapes = [jax.ShapeDtypeStruct((N_CHIPS, g.shape[1] // 2, g.shape[2]), g.dtype) for g in units]
    return _hbm_call(body, units, shapes, nu, name)


def join_units(units, *, name):
    nu = len(units)

    def body(*refs):
        h_refs, out_refs, send_sems, recv_sems = refs[:nu], refs[nu:2 * nu], refs[2 * nu], refs[2 * nu + 1]
        x, y, c = _place()
        cps = [pltpu.make_async_remote_copy(src_ref=h_refs[u], dst_ref=out_refs[u], send_sem=send_sems.at[u], recv_sem=recv_sems.at[u],
                                            device_id=(x, y, 1 - c), device_id_type=MESH) for u in range(nu)]
        for cp in cps:
            cp.start()
        for cp in cps:
            cp.wait()

    return _hbm_call(body, units, [jax.ShapeDtypeStruct(h.shape, h.dtype) for h in units], nu, name)


def _half_tile(rh):
    tr = rh
    for cand in range(BF16_ROWS, min(rh, 512) + 1, BF16_ROWS):
        if rh % cand == 0:
            tr = cand
    return tr


def pair_add(g, sib, *, name):
    nc, rh, cols = sib.shape
    tr = _half_tile(rh)
    nbh = rh // tr

    def body(g0_ref, g1_ref, s_ref, o_ref):
        mine = jnp.where(lax.axis_index("c") == 0, g0_ref[...], g1_ref[...])
        o_ref[...] = (mine.astype(F32) + s_ref[...].astype(F32)).astype(o_ref.dtype)

    blk = lambda off: pl.BlockSpec((None, tr, cols), lambda j, i: (j, off + i, 0))
    return pl.pallas_call(
        body, grid=(nc, nbh), in_specs=[blk(0), blk(nbh), blk(0)], out_specs=blk(0),
        out_shape=jax.ShapeDtypeStruct(sib.shape, BF16), name=name, compiler_params=_params(("parallel", "parallel")),
    )(g, g, sib)


def chips_add(pair, landed, *, name):
    nc, rh, cols = pair.shape
    tr = _half_tile(rh)

    def body(*refs):
        chip = 2 * lax.axis_index("x") + lax.axis_index("y")
        acc = refs[0][...]
        for j in range(1, nc):
            acc = jnp.where(chip == j, refs[j][...], acc)
        acc = acc.astype(F32)
        for r in refs[nc:-1]:
            acc = acc + r[...].astype(F32)
        refs[-1][...] = acc

    part = lambda q: pl.BlockSpec((None, tr, cols), lambda i, q=q: (q, i, 0))
    return pl.pallas_call(
        body, grid=(rh // tr,), in_specs=[part(q) for q in range(nc)] + [part(q) for q in range(landed.shape[0])],
        out_specs=pl.BlockSpec((tr, cols), lambda i: (i, 0)),
        out_shape=jax.ShapeDtypeStruct((rh, cols), F32), name=name, compiler_params=_params(("parallel",)),
    )(*[pair] * nc, *[landed] * landed.shape[0])


def sum8(g, *, name):
    def body(g_ref, o_ref):
        acc = g_ref[0]
        for d in range(1, N_DEV):
            acc = acc + g_ref[d]
        o_ref[...] = acc

    return pl.pallas_call(body, out_shape=jax.ShapeDtypeStruct(g.shape[1:], F32), name=name)(g)


def _dup_halves(a):
    t = a.shape[0]
    a = a.reshape(t, N_KV_B, HEAD_DIM_B)
    return jnp.concatenate([a, a], axis=-1).reshape(t, N_KV_B * LANES)


def _undup(a):
    t = a.shape[0]
    return a.reshape(t, N_KV_B, LANES)[:, :, :HEAD_DIM_B].reshape(t, N_KV_B * HEAD_DIM_B)


def _lane_pad(v, offset=0):
    return jnp.zeros((1, LANES), F32).at[0, offset:offset + v.shape[0]].set(v)


SHARD_UP = 2 * D_FF // N_CHIPS
SHARD_BIN = (N_HEADS_B + 2 * N_KV_B) * HEAD_DIM_B // N_CHIPS
SHARD_PROJ = D_MODEL // N_CHIPS


def local_step(x, p, tgt, sm, weight, on_grads):
    t = x.shape[0]
    rtm = min(256, t)
    hk = N_HEADS_A * HEAD_DIM_A
    qd_b = N_HEADS_B * HEAD_DIM_B
    kd_b = N_KV_B * HEAD_DIM_B
    gs = {}
    norm = lambda h, w, nm: tile_map(_f_norm, [(h, D_MODEL, 0)], [w], [(D_MODEL, BF16)], tm=rtm, ncol=1, name=nm)[0]

    spec = pl.BlockSpec
    mtm = _tile(D_MODEL, MM_TM_CAP)
    p_bf = p.astype(BF16)
    alog_p = _lane_pad(sm["a_log"][0], N_HEADS_A)
    dtb_p = _lane_pad(sm["a_dt_bias"][0], N_HEADS_A)
    sinks_p = _lane_pad(sm["b_sinks"][0])
    nw = lambda name, i: sm[name][i:i + 1]
    by_chip = lambda kdim, ns: dict(tn=ns, tk=kdim, b_spec=spec((None, kdim, ns), lambda r, j, kk: (j, kk, 0)))
    by_chip_t = lambda ndim, ns: dict(n=ndim, tn=ndim, tk=ns, b_spec=spec((None, ndim, ns), lambda r, j, kk: (kk, j, 0)))
    cache = {}

    def wgt(name, i, after):
        if (name, i) not in cache:
            cache[name, i] = weight(name, i, after)
        return cache[name, i]

    saved = []
    h = x
    hn_next = norm(h, nw("norm_mix", 0), "norm_mix0")
    for i in range(DEPTH):
        s = {"h0": h, "hn": hn_next}
        if i % 2 == 0:
            s["pm"] = mm(s["hn"], wgt("a_w_in", i, h), name="a_in")
            tail = (s["pm"], LANES, 4 * hk // LANES)
            s["c"] = conv_fwd(s["pm"], wgt("a_conv", i, h), name="a_conv")
            s["bg"] = tile_map(_f_betag, [tail], [alog_p, dtb_p], [(LANES, F32)], tm=rtm, ncol=1, name="a_betag")[0]
            s["prep"], s["tms"] = delta_prep(s["c"], s["bg"], name="a_prep")
            s["o"], s["s_in"] = delta_scan(*s["prep"], name="a_scan")
            s["on"] = gnorm_fwd(s["o"], s["pm"], sm["a_norm"], name="a_gnorm")
            h, s["hf"] = mm(s["on"], wgt("a_w_out", i, s["on"]), add=h, norm_w=nw("norm_ffn", i), name="a_out")
        else:
            s["pb"] = mm(s["hn"], wgt("b_w_in", i, s["hn"]), name="b_in", out_dtype=BF16, n=N_CHIPS * SHARD_BIN,
                         **by_chip(D_MODEL, SHARD_BIN))
            s["kd"], s["vd"] = _dup_halves(s["pb"][:, qd_b:qd_b + kd_b]), _dup_halves(s["pb"][:, qd_b + kd_b:])
            s["ao"] = swa_fwd(s["pb"], s["kd"], s["vd"], sinks_p, name="b_att")
            h, s["hf"] = mm(s["ao"], wgt("b_w_out", i, s["ao"]), add=h, norm_w=nw("norm_ffn", i), name="b_out")
        s["h1"] = h
        s["u"] = mm(s["hf"], wgt("f_w_up", i, s["hf"]), name=f"f_up{i}", out_dtype=BF16, n=2 * D_FF, tm_cap=2 * MM_TM_CAP,
                    **by_chip(D_MODEL, SHARD_UP))
        s["act"] = conv_act_fwd(s["u"], wgt("f_conv", i, s["hf"]), name=f"f_conv_act{i}")
        h, s["hp"] = mm(s["act"], wgt("f_w_down", i, s["act"]), add=h, norm_w=nw("norm_ple", i), name=f"f_down{i}", tk=D_FF)
        s["h2"] = h
        s["gl"] = mm(s["hp"], wgt("ple_w_gate", i, s["hp"]), name=f"ple_gate{i}")
        s["pe"] = mm(p_bf[i], wgt("ple_w_proj", i, s["hp"]), name=f"ple_proj{i}", n=D_MODEL, **by_chip(PLE_DIM, SHARD_PROJ))
        rows3 = [(h, D_MODEL, 0), (s["gl"], D_MODEL, 0), (s["pe"], D_MODEL, 0)]
        if i + 1 < DEPTH:
            def mix_norm(hv, g, e, wn):
                hn = hv + _f_ple(g, e)
                return hn, _f_norm(hn, wn)
            h, hn_next = tile_map(mix_norm, rows3, [nw("norm_mix", i + 1)], [(D_MODEL, F32), (D_MODEL, BF16)], tm=rtm, ncol=1,
                                  name=f"ple_mix{i}")
        else:
            h = tile_map(lambda hv, g, e: hv + _f_ple(g, e), rows3, [], [(D_MODEL, F32)], tm=rtm, ncol=1, name=f"ple_mix{i}")[0]
        saved.append(s)

    dh, gnf, loss = loss_head(h, tgt, sm["norm_final"][None, :], name="loss_head")
    gs["norm_final"] = gnf[0]

    g_mix, g_ffn, g_ple, g_conv = ([None] * DEPTH for _ in range(4))
    zero = jnp.zeros((1, 1), F32)
    for i in reversed(range(DEPTH)):
        s, gw = saved[i], {}
        by_rows = lambda g: g.reshape(N_CHIPS, g.shape[0] // N_CHIPS, g.shape[1])
        (dgl, dpe), _ = tile_vjp(_f_ple, [(s["gl"], D_MODEL, 0), (s["pe"], D_MODEL, 0)], [], [(dh, D_MODEL, 0)], n_diff=2,
                                 tm=rtm, ncol=1, name=f"ple_mix_bwd{i}", grad_dtypes=[BF16, BF16])
        gw["ple_w_proj"] = mm(p_bf[i], dpe, ta=True, name=f"ple_proj_dw{i}", out_dtype=BF16, tn=SHARD_PROJ,
                              o_shape=(N_CHIPS, PLE_DIM, SHARD_PROJ), o_spec=spec((None, PLE_DIM, SHARD_PROJ), lambda r, j, kk: (j, r, 0)))
        gw["ple_w_gate"] = by_rows(mm(s["hp"], dgl, ta=True, name=f"ple_gate_dw{i}", out_dtype=BF16))
        fused = dict(tb=True, tm_cap=MM_TM_CAP // 2)
        dh, g_ple[i] = mm(dgl, cache["ple_w_gate", i], name=f"ple_gate_dx{i}", norm_grad=(s["h2"], nw("norm_ple", i) + zero, dh), **fused)

        dact = mm(dh, cache["f_w_down", i], tb=True, name=f"f_down_dx{i}")
        gw["f_w_down"] = by_rows(mm(s["act"], dh, ta=True, name=f"f_down_dw{i}", out_dtype=BF16, tm_cap=D_FF // 2))
        du_halves = conv_act_bwd(s["u"], dact, cache["f_conv", i], name=f"f_conv_act_bwd{i}")
        g_conv[i] = jnp.concatenate(du_halves[2:], axis=1)
        dhf = g_up = None
        for half, du in enumerate(du_halves[:2]):
            c0 = half * (N_CHIPS // 2)
            g_up = mm(s["hf"], du, ta=True, name=f"f_up_dw{i}_{half}", out_dtype=BF16, tn=SHARD_UP, into=g_up,
                      o_shape=(N_CHIPS, D_MODEL, SHARD_UP), o_spec=spec((None, mtm, SHARD_UP), lambda r, j, kk, c0=c0: (c0 + j, r, 0)))
            last = dict(norm_grad=(s["h1"], nw("norm_ffn", i), dh), **fused) if half else dict(tb=True)
            dhf = mm(du, cache["f_w_up", i], name=f"f_up_dx{i}_{half}", n=D_MODEL, tn=D_MODEL, tk=SHARD_UP, add=dhf,
                     b_spec=spec((None, D_MODEL, SHARD_UP), lambda r, j, kk, c0=c0: (c0 + kk, j, 0)), **last)
        gw["f_w_up"] = g_up
        dh, g_ffn[i] = dhf
        token, gw = on_grads(i, "ffn", gw), {}
        w_out = cache["a_w_out" if i % 2 == 0 else "b_w_out", i]
        if token is not None:
            w_out = w_out + token[:1, :1].astype(BF16)

        if i % 2 == 0:
            don = mm(dh, w_out, tb=True, name="a_out_dx")
            gw["a_w_out"] = by_rows(mm(s["on"], dh, ta=True, name="a_out_dw", out_dtype=BF16))
            do, dz, gs["a_norm"] = gnorm_bwd(s["o"], s["pm"], sm["a_norm"], don, name="a_gnorm_bwd")
            dprep = delta_scan_bwd(do, *s["prep"], s["s_in"], name="a_scan_bwd")
            dc, dbg = delta_prep_bwd(s["c"], s["bg"], s["tms"], dprep, name="a_prep_bwd")
            (dpt,), (galog, gdtb) = tile_vjp(_f_betag, [(s["pm"], LANES, 4 * hk // LANES)], [alog_p, dtb_p], [(dbg, LANES, 0)], n_diff=1,
                                             tm=rtm, ncol=1, name="a_betag_bwd", grad_dtypes=[BF16])
            gs["a_log"] = galog[:, N_HEADS_A:2 * N_HEADS_A]
            gs["a_dt_bias"] = gdtb[:, N_HEADS_A:2 * N_HEADS_A]
            dqkv, gs["a_conv"] = conv_bwd(dc, s["pm"], cache["a_conv", i], name="a_conv_bwd")
            dpm = jnp.concatenate([dqkv, dz, dpt], axis=1)
            g_in = mm(s["hn"], dpm, ta=True, name="a_in_dw", out_dtype=BF16)[:, :4 * hk + 2 * N_HEADS_A]
            gw["a_w_in"] = g_in.reshape(D_MODEL, N_CHIPS, g_in.shape[1] // N_CHIPS).transpose(1, 0, 2)
            dh, g_mix[i] = mm(dpm, cache["a_w_in", i], name="a_in_dx", norm_grad=(s["h0"], nw("norm_mix", i), dh), **fused)
        else:
            dao = mm(dh, w_out, tb=True, name="b_out_dx")
            gw["b_w_out"] = by_rows(mm(s["ao"], dh, ta=True, name="b_out_dw", out_dtype=BF16))
            dq, dkd, dvd, gsk = swa_bwd(dao, s["pb"], s["kd"], s["vd"], sinks_p, name="b_att_bwd")
            gs["b_sinks"] = gsk[:, :N_HEADS_B]
            dpb = jnp.concatenate([dq, _undup(dkd), _undup(dvd)], axis=1)
            gw["b_w_in"] = mm(s["hn"], dpb, ta=True, name="b_in_dw", out_dtype=BF16, tn=SHARD_BIN,
                              o_shape=(N_CHIPS, D_MODEL, SHARD_BIN), o_spec=spec((None, mtm, SHARD_BIN), lambda r, j, kk: (j, r, 0)))
            dh, g_mix[i] = mm(dpb, cache["b_w_in", i], name="b_in_dx", norm_grad=(s["h0"], nw("norm_mix", i), dh), **fused,
                              **by_chip_t(D_MODEL, SHARD_BIN))
        token = on_grads(i, "mix", gw)
        if token is not None:
            zero = token[:1, :1]

    gs["norm_mix"], gs["norm_ffn"], gs["norm_ple"] = (jnp.concatenate(g, axis=0) for g in (g_mix, g_ffn, g_ple))
    gs["f_conv"] = jnp.stack(g_conv)
    return loss, dh, gs


BIG = ["a_w_in", "a_w_out", "b_w_in", "b_w_out", "f_w_up", "f_w_down", "ple_w_proj", "ple_w_gate"]
LAYERED = {"f_w_up", "f_w_down", "ple_w_proj", "ple_w_gate"}
BY_CHIP = {"b_w_in", "f_w_up", "ple_w_proj"}
LAYER_UNITS = [[("a_w_in", 0), ("a_w_out", 0)] + [(n, 0) for n in sorted(LAYERED)],
               [("b_w_in", 1), ("b_w_out", 1)] + [(n, 1) for n in sorted(LAYERED)]]
CONVS = ["a_conv", "f_conv"]
SMALL = ["norm_mix", "norm_ffn", "norm_ple", "norm_final", "a_log", "a_dt_bias", "a_norm", "b_sinks"]
SMALL_ROWS = 8
CONV_ROWS = 16
CONV_GRAD_ROWS = 48


def _pack_rows(arrs, rows, dtype):
    flat = jnp.concatenate([a.reshape(-1).astype(dtype) for a in arrs])
    return jnp.pad(flat, (0, rows * PACK_COLS - flat.shape[0])).reshape(rows, PACK_COLS)


def _unpack(flat, shapes):
    out, off = [], 0
    for shp in shapes:
        n = math.prod(shp)
        out.append(flat[off:off + n].reshape(shp))
        off += n
    return out


def _pack_small(d, loss=None):
    tail = jnp.concatenate([d["a_log"].reshape(-1), d["a_dt_bias"].reshape(-1), d["a_norm"].reshape(-1), d["b_sinks"].reshape(-1)])
    if loss is not None:
        tail = jnp.concatenate([tail, loss.reshape(-1)[:1]])
    tail = jnp.pad(tail, (0, PACK_COLS - tail.shape[0]))
    return jnp.concatenate([d["norm_mix"], d["norm_ffn"], d["norm_ple"], d["norm_final"][None, :], tail[None, :]], axis=0)


def _unpack_small(a, like):
    out = {"norm_mix": a[0:2], "norm_ffn": a[2:4], "norm_ple": a[4:6], "norm_final": a[6]}
    off = 0
    for nm in ("a_log", "a_dt_bias", "a_norm", "b_sinks"):
        n = like[nm].size
        out[nm] = a[7, off:off + n].reshape(like[nm].shape)
        off += n
    return out, a[7, off]


def _as2d(a):
    return a.reshape(-1, a.shape[-1])


def kernel(x, p, norm_mix, norm_ffn, norm_ple, norm_final, a_w_in, a_conv, a_log, a_dt_bias, a_norm, a_w_out, b_w_in, b_sinks, b_w_out, f_w_up, f_conv, f_w_down, ple_w_proj, ple_w_gate, loss_target, m_norm_mix, m_norm_ffn, m_norm_ple, m_norm_final, m_a_w_in, m_a_conv, m_a_log, m_a_dt_bias, m_a_norm, m_a_w_out, m_b_w_in, m_b_sinks, m_b_w_out, m_f_w_up, m_f_conv, m_f_w_down, m_ple_w_proj, m_ple_w_gate, v_norm_mix, v_norm_ffn, v_norm_ple, v_norm_final, v_a_w_in, v_a_conv, v_a_log, v_a_dt_bias, v_a_norm, v_a_w_out, v_b_w_in, v_b_sinks, v_b_w_out, v_f_w_up, v_f_conv, v_f_w_down, v_ple_w_proj, v_ple_w_gate):
    w = dict(norm_mix=norm_mix, norm_ffn=norm_ffn, norm_ple=norm_ple, norm_final=norm_final, a_w_in=a_w_in, a_conv=a_conv,
             a_log=a_log, a_dt_bias=a_dt_bias, a_norm=a_norm, a_w_out=a_w_out, b_w_in=b_w_in, b_sinks=b_sinks, b_w_out=b_w_out,
             f_w_up=f_w_up, f_conv=f_conv, f_w_down=f_w_down, ple_w_proj=ple_w_proj, ple_w_gate=ple_w_gate)
    m = dict(norm_mix=m_norm_mix, norm_ffn=m_norm_ffn, norm_ple=m_norm_ple, norm_final=m_norm_final, a_w_in=m_a_w_in,
             a_conv=m_a_conv, a_log=m_a_log, a_dt_bias=m_a_dt_bias, a_norm=m_a_norm, a_w_out=m_a_w_out, b_w_in=m_b_w_in,
             b_sinks=m_b_sinks, b_w_out=m_b_w_out, f_w_up=m_f_w_up, f_conv=m_f_conv, f_w_down=m_f_w_down,
             ple_w_proj=m_ple_w_proj, ple_w_gate=m_ple_w_gate)
    v = dict(norm_mix=v_norm_mix, norm_ffn=v_norm_ffn, norm_ple=v_norm_ple, norm_final=v_norm_final, a_w_in=v_a_w_in,
             a_conv=v_a_conv, a_log=v_a_log, a_dt_bias=v_a_dt_bias, a_norm=v_a_norm, a_w_out=v_a_w_out, b_w_in=v_b_w_in,
             b_sinks=v_b_sinks, b_w_out=v_b_w_out, f_w_up=v_f_w_up, f_conv=v_f_conv, f_w_down=v_f_w_down,
             ple_w_proj=v_ple_w_proj, ple_w_gate=v_ple_w_gate)
    xc, yc, cc = _place()
    my_chip = 2 * xc + yc

    shard = {(n, i): w[n][i if n in LAYERED else 0].astype(BF16) for n, i in LAYER_UNITS[0] + LAYER_UNITS[1]}
    first = shard["a_w_in", 0]
    (ga,) = gather_units([(first, False)], name="gather_first")
    ga = lax.dynamic_update_index_in_dim(ga, first, my_chip, 0)
    a_in = jnp.concatenate([ga[j] for j in range(N_CHIPS)], axis=1)
    n_main = 4 * N_HEADS_A * HEAD_DIM_A
    conv_shapes = [w[n].shape for n in CONVS]
    convs = allgather8(_pack_rows([w[n] for n in CONVS], CONV_ROWS, F32), name="gather_convs")
    conv_parts = [_unpack(convs[2 * j].reshape(-1), conv_shapes) for j in range(N_CHIPS)]
    a_conv_full, f_conv_full = (jnp.concatenate([conv_parts[j][q] for j in range(N_CHIPS)], axis=2) for q in range(2))
    ready = {("a_w_in", 0): jnp.pad(a_in, ((0, 0), (0, n_main + LANES - a_in.shape[1]))), ("a_conv", 0): a_conv_full[0], ("f_conv", 0): f_conv_full[0], ("f_conv", 1): f_conv_full[1]}
    later = [[k for k in units if k != ("a_w_in", 0)] for units in LAYER_UNITS]
    pending, after = [], ga
    for layer, keys in enumerate(later):
        pending.append(gather_start([shard[k] for k in keys], after, name=f"gather_start{layer}"))
        after = pending[-1][4]
    sm = {n: w[n] for n in SMALL}
    sm["norm_mix"] = sm["norm_mix"] + after[:1, :1]

    def weight(name, layer, act):
        if (name, layer) not in ready:
            landed = gather_wait(pending[layer], act, name=f"gather_wait{layer}")
            for k, g in zip(later[layer], landed):
                g = lax.dynamic_update_index_in_dim(g, shard[k], my_chip, 0)
                ready[k] = g if k[0] in BY_CHIP else g.reshape(N_CHIPS * g.shape[1], g.shape[2])
        return ready[name, layer]

    pairs, scattered, started = {}, {}, []

    def on_grads(layer, part, gw):
        keys = [k for k in LAYER_UNITS[layer] if (k[0] in LAYERED) == (part == "ffn")]
        from_sib = swap_units([gw[n] for n, _ in keys], name=f"rs_swap_{part}{layer}")
        for (n, _), sib in zip(keys, from_sib):
            pairs[n, layer] = pair_add(gw[n], sib, name=f"rs_add_pair_{n}{layer}")
        started.append((keys, scatter_start([pairs[k] for k in keys], name=f"rs_scatter_start_{part}{layer}"), f"{part}{layer}"))
        return started[-1][1][4]

    loss, grad_x, gs = local_step(x[0], p[:, 0], loss_target[0], sm, weight, on_grads)

    grads, delta, new_m, new_v, g_unit = {}, {}, {}, {}, {}

    def finish(keys, tag):
        halves = [chips_add(pairs[k], scattered[k], name=f"rs_add_chips_{k[0]}{k[1]}") for k in keys]
        g_unit.update(zip(keys, zip(halves, join_units(halves, name=f"rs_join_{tag}"))))
        for n in BIG:
            mine = [(n, i) for i in range(DEPTH) if (n, i) in LAYER_UNITS[i]]
            if n not in delta and all(k in g_unit for k in mine):
                g_layers = [g_unit[k] for k in mine]
                shape3 = (len(g_layers), 2 * g_layers[0][0].shape[0], g_layers[0][0].shape[1])
                res = adamw_layers(w[n].reshape(shape3), g_layers, m[n].reshape(shape3), v[n].reshape(shape3), name=f"adamw_{n}")
                delta[n], new_m[n], new_v[n], grads[n] = (r.reshape(w[n].shape) for r in res)

    last_keys, last_pending, last_tag = started[-1]
    for keys, pend, tag in started[:-1]:
        scattered.update(zip(keys, scatter_wait(pend, last_pending[4], name=f"rs_scatter_wait_{tag}")))
    finish([k for keys, _, _ in started[:-1] for k in keys], "first")

    conv_grads = _pack_rows([gs[n] for n in CONVS], CONV_GRAD_ROWS, F32)
    small_sum = sum8(allgather8(jnp.concatenate([_pack_small(gs, loss), conv_grads]), name="gather_small"), name="sum_small")
    g_sm, loss_sum = _unpack_small(small_sum[:SMALL_ROWS], sm)

    scattered.update(zip(last_keys, scatter_wait(last_pending, small_sum, name=f"rs_scatter_wait_{last_tag}")))
    finish(last_keys, "last")

    for n, full in zip(CONVS, _unpack(small_sum[SMALL_ROWS:].reshape(-1), [gs[n].shape for n in CONVS])):
        g2 = _as2d(lax.dynamic_slice_in_dim(full, my_chip * w[n].shape[-1], w[n].shape[-1], axis=full.ndim - 1))
        d2, m2, v2 = adamw(_as2d(w[n]), g2, _as2d(m[n]), _as2d(v[n]), name=f"adamw_{n}")
        grads[n], delta[n], new_m[n], new_v[n] = (r.reshape(w[n].shape) for r in (g2, d2, m2, v2))
    pk = lambda d: _pack_small(d)
    d2, m2, v2 = adamw(pk(sm), pk(g_sm), pk({n: m[n] for n in SMALL}), pk({n: v[n] for n in SMALL}), name="adamw_small")
    for src, dst in ((d2, delta), (m2, new_m), (v2, new_v)):
        dst.update(_unpack_small(src, sm)[0])
    grads.update(g_sm)

    order = ["norm_mix", "norm_ffn", "norm_ple", "norm_final", "a_w_in", "a_conv", "a_log", "a_dt_bias", "a_norm", "a_w_out",
             "b_w_in", "b_sinks", "b_w_out", "f_w_up", "f_conv", "f_w_down", "ple_w_proj", "ple_w_gate"]
    return (loss_sum, grad_x[None], *[grads[n] for n in order], *[delta[n] for n in order],
            *[new_m[n] for n in order], *[new_v[n] for n in order])
```

```python
import functools
import math

import jax
import jax.numpy as jnp
from jax import lax
from jax.experimental import pallas as pl
from jax.experimental.pallas import tpu as pltpu

F32 = jnp.float32
BF16 = jnp.bfloat16
MESH = pl.DeviceIdType.MESH

D_MODEL = 1024
N_HEADS_A = 8
HEAD_DIM_A = 128
CONV_A = 4
N_HEADS_B = 16
N_KV_B = 4
HEAD_DIM_B = 64
WINDOW = 128
D_FF = 2816
FFN_CONV = 3
PLE_DIM = 256
EPS = 1e-6
DEPTH = 2

ADAM_LR = 0.001
ADAM_B1 = 0.9
ADAM_B2 = 0.999
ADAM_EPS = 1e-08
ADAM_WD = 0.01
ADAM_STEP = 10

LANES = 128
SUBLANES = 8
BF16_ROWS = 16
CHUNK = 128
VMEM_LIMIT = 56 * 1024 * 1024
NEG = -1e30
N_CHIPS = 4
N_DEV = 8
PACK_COLS = 1024


def _params(sem=None):
    return pltpu.CompilerParams(dimension_semantics=sem, vmem_limit_bytes=VMEM_LIMIT)


def _tile(dim, cap):
    if dim % LANES:
        return dim
    best = LANES
    for t in range(LANES, min(dim, cap) + 1, LANES):
        if dim % t == 0:
            best = t
    return best


def _dot(a, b, dims=(((1,), (0,)), ((), ())), precision=None):
    return lax.dot_general(a, b, dims, precision=precision, preferred_element_type=F32)


NN = (((1,), (0,)), ((), ()))
NT = (((1,), (1,)), ((), ()))
TN = (((0,), (0,)), ((), ()))


MM_TM_CAP = 1024
MM_TK_CAP_TOKENS = 2048


def mm(a, b, *, name, ta=False, tb=False, out_dtype=F32, add=None, norm_w=None, norm_grad=None, tm_cap=MM_TM_CAP, tn_cap=1408,
       tk_cap=1408, n=None, tn=None, tk=None, b_spec=None, o_spec=None, o_shape=None, into=None):
    m, k = (a.shape[1], a.shape[0]) if ta else a.shape
    if b_spec is None:
        n = b.shape[0] if tb else b.shape[1]
        assert (b.shape[1] if tb else b.shape[0]) == k, (a.shape, b.shape, ta, tb)
    tm, tn, tk = _tile(m, tm_cap), tn or _tile(n, tn_cap), tk or _tile(k, MM_TK_CAP_TOKENS if ta else tk_cap)
    assert n % tn == 0 and k % tk == 0, (n, tn, k, tk)
    nk = k // tk
    dims = (((0 if ta else 1,), (1 if tb else 0,)), ((), ()))
    has_add, has_norm, has_grad = add is not None, norm_w is not None, norm_grad is not None
    assert not (has_norm or has_grad) or (tn == n and o_spec is None), "the norm epilogues need whole rows"
    n_in = 2 + has_add + has_norm + 3 * has_grad + (into is not None)

    def body(*refs):
        a_ref, b_ref = refs[0], refs[1]
        add_ref = refs[2] if has_add else None
        o_ref = refs[n_in]
        part = _dot(a_ref[...].astype(BF16), b_ref[...].astype(BF16), dims)
        first = pl.program_id(0) == 0

        def finish(r):
            if has_add:
                r = r + add_ref[...].astype(F32)
            if has_grad:
                h_ref, w_ref, prev_ref = refs[2 + has_add:5 + has_add]
                _, vjp = jax.vjp(_f_norm, h_ref[...], w_ref[...])
                r, dw = vjp(r)
                r = r + prev_ref[...]

                @pl.when(first)
                def _():
                    refs[n_in + 1][...] = dw

                @pl.when(jnp.logical_not(first))
                def _():
                    refs[n_in + 1][...] += dw
            o_ref[...] = r.astype(o_ref.dtype)
            if has_norm:
                refs[n_in + 1][...] = _f_norm(r, refs[2 + has_add][...]).astype(BF16)

        if nk == 1:
            finish(part)
            return
        acc = refs[-1]
        kk = pl.program_id(2)

        @pl.when(kk == 0)
        def _():
            acc[...] = part

        @pl.when(kk > 0)
        def _():
            acc[...] += part

        @pl.when(kk == nk - 1)
        def _():
            finish(acc[...])

    a_spec = pl.BlockSpec((tk, tm), lambda i, j, kk: (kk, i)) if ta else pl.BlockSpec((tm, tk), lambda i, j, kk: (i, kk))
    if b_spec is None:
        b_spec = pl.BlockSpec((tn, tk), lambda i, j, kk: (j, kk)) if tb else pl.BlockSpec((tk, tn), lambda i, j, kk: (kk, j))
    plain_o = pl.BlockSpec((tm, tn), lambda i, j, kk: (i, j))
    if o_spec is None:
        o_spec, o_shape = plain_o, (m, n)
    in_specs = [a_spec, b_spec] + ([plain_o] if has_add else [])
    args = (a, b) + ((add,) if has_add else ())
    out_specs, out_shapes = o_spec, jax.ShapeDtypeStruct(tuple(o_shape), out_dtype)
    one_row = pl.BlockSpec((1, n), lambda i, j, kk: (0, 0))
    if has_norm:
        in_specs.append(one_row)
        args += (norm_w,)
        out_specs, out_shapes = [o_spec, plain_o], [out_shapes, jax.ShapeDtypeStruct((m, n), BF16)]
    if has_grad:
        assert not has_norm
        in_specs += [plain_o, one_row, plain_o]
        args += tuple(norm_grad)
        out_specs, out_shapes = [o_spec, one_row], [out_shapes, jax.ShapeDtypeStruct((1, n), F32)]
    aliases = {}
    if into is not None:
        assert into.shape == tuple(o_shape) and into.dtype == out_dtype, (into.shape, o_shape)
        in_specs.append(pl.BlockSpec(memory_space=pl.ANY))
        args += (into,)
        aliases = {n_in - 1: 0}
    return pl.pallas_call(
        body, grid=(m // tm, n // tn, nk), in_specs=in_specs, out_specs=out_specs,
        out_shape=out_shapes, name=name, input_output_aliases=aliases,
        scratch_shapes=[pltpu.VMEM((tm, tn), F32)] if nk > 1 else [],
        compiler_params=_params(("arbitrary" if has_grad else "parallel", "parallel", "arbitrary")),
    )(*args)


def _row_spec(tm, cw, coff):
    return pl.BlockSpec((tm, cw), lambda i, j: (i, j + coff))


def _full_spec(shape):
    return pl.BlockSpec(shape, lambda i, j: (0,) * len(shape))


def tile_map(fn, rows, params, outs, *, tm, ncol, name):
    t = rows[0][0].shape[0]
    nin = len(rows) + len(params)

    def body(*refs):
        res = fn(*[r[...] for r in refs[:nin]])
        res = res if isinstance(res, (tuple, list)) else (res,)
        for o_ref, r in zip(refs[nin:], res):
            o_ref[...] = r.astype(o_ref.dtype)

    in_specs = [_row_spec(tm, cw, coff) for (_, cw, coff) in rows] + [_full_spec(p.shape) for p in params]
    res = pl.pallas_call(
        body, grid=(t // tm, ncol), in_specs=in_specs,
        out_specs=[_row_spec(tm, cw, 0) for (cw, _) in outs],
        out_shape=[jax.ShapeDtypeStruct((t, cw * ncol), dt) for (cw, dt) in outs], name=name,
        compiler_params=_params(("parallel", "parallel")),
    )(*[r[0] for r in rows], *params)
    return res


def tile_vjp(fn, rows, params, cts, *, n_diff, tm, ncol, name, grad_dtypes=None):
    t = rows[0][0].shape[0]
    nr, npar, nct = len(rows), len(params), len(cts)

    def body(*refs):
        vals = [r[...] for r in refs[:nr + npar + nct]]
        diff, rest, pars = vals[:n_diff], vals[n_diff:nr], vals[nr:nr + npar]
        ctv = vals[nr + npar:nr + npar + nct]
        outs_ref = refs[nr + npar + nct:]

        def f(*a):
            res = fn(*a[:n_diff], *rest, *a[n_diff:])
            return tuple(res) if isinstance(res, (tuple, list)) else (res,)

        primal, vjp = jax.vjp(f, *[d.astype(F32) for d in diff], *pars)
        grads = vjp(tuple(c.astype(o.dtype) for c, o in zip(ctv, primal)))
        for q in range(n_diff):
            outs_ref[q][...] = grads[q].astype(outs_ref[q].dtype)
        first = (pl.program_id(0) == 0) & (pl.program_id(1) == 0)
        for q in range(npar):
            o_ref, g = outs_ref[n_diff + q], grads[n_diff + q]

            @pl.when(first)
            def _(o_ref=o_ref, g=g):
                o_ref[...] = g

            @pl.when(jnp.logical_not(first))
            def _(o_ref=o_ref, g=g):
                o_ref[...] += g

    in_specs = [_row_spec(tm, cw, coff) for (_, cw, coff) in rows] + [_full_spec(p.shape) for p in params]
    in_specs += [_row_spec(tm, cw, coff) for (_, cw, coff) in cts]
    args = [r[0] for r in rows] + list(params) + [c[0] for c in cts]
    out_specs = [_row_spec(tm, rows[q][1], 0) for q in range(n_diff)] + [_full_spec(p.shape) for p in params]
    grad_dtypes = grad_dtypes or [F32] * n_diff
    out_shape = [jax.ShapeDtypeStruct((t, rows[q][1] * ncol), grad_dtypes[q]) for q in range(n_diff)]
    out_shape += [jax.ShapeDtypeStruct(p.shape, F32) for p in params]
    res = pl.pallas_call(
        body, grid=(t // tm, ncol), in_specs=in_specs, out_specs=out_specs, out_shape=out_shape, name=name,
        compiler_params=_params(("arbitrary", "arbitrary")),
    )(*args)
    return res[:n_diff], res[n_diff:]


def _silu(x):
    return x * jax.nn.sigmoid(x)


def _f_norm(h, w):
    return h * lax.rsqrt(jnp.mean(h * h, axis=-1, keepdims=True) + EPS) * w


def _f_gnorm(o, z, w):
    return _f_norm(o, w) * _silu(z)


def _f_act(gate, val):
    return _silu(gate) * val


def _f_ple(gl, pe):
    return jax.nn.sigmoid(gl) * pe


def _f_betag(pt, alog, dtb):
    lane = lax.broadcasted_iota(jnp.int32, (1, LANES), 1)
    z = pt + dtb
    softplus = jnp.maximum(z, 0.0) + jnp.log(1.0 + jnp.exp(-jnp.abs(z)))
    g = -jnp.exp(alog) * softplus
    return jnp.where(lane < N_HEADS_A, jax.nn.sigmoid(pt), jnp.where(lane < 2 * N_HEADS_A, g, 0.0))


CONV_TM = 512
CONV_CW = 1024


def _shift_down(x, prev, s, row):
    rp = jnp.tile(pltpu.roll(prev, s, 0), (x.shape[0] // SUBLANES, 1))
    return jnp.where(row < s, rp, pltpu.roll(x, s, 0))


def _shift_up(x, nxt, s, row):
    tm = x.shape[0]
    rn = jnp.tile(pltpu.roll(nxt, SUBLANES - s, 0), (tm // SUBLANES, 1))
    return jnp.where(row >= tm - s, rn, pltpu.roll(x, tm - s, 0))


def _conv_taps(x, prev, w_ref, cols, row):
    k = w_ref.shape[0]
    y = x * w_ref[pl.ds(k - 1, 1), cols]
    for s in range(1, k):
        y = y + _shift_down(x, prev, s, row) * w_ref[pl.ds(k - 1 - s, 1), cols]
    return y


def _lane_chunks(cw):
    return [slice(cb * LANES, (cb + 1) * LANES) for cb in range(cw // LANES)]


def conv_fwd(x, w, *, name):
    t = x.shape[0]
    k, c = w.shape
    tm, cw = min(CONV_TM, t), CONV_CW
    nb8 = tm // SUBLANES

    def body(x_ref, p_ref, w_ref, o_ref):
        first = pl.program_id(1) == 0
        row = lax.broadcasted_iota(jnp.int32, (tm, LANES), 0)
        for cols in _lane_chunks(cw):
            o_ref[:, cols] = _conv_taps(x_ref[:, cols], jnp.where(first, 0.0, p_ref[:, cols]), w_ref, cols, row)

    return pl.pallas_call(
        body, grid=(c // cw, t // tm),
        in_specs=[pl.BlockSpec((tm, cw), lambda j, i: (i, j)),
                  pl.BlockSpec((SUBLANES, cw), lambda j, i: (jnp.maximum(i * nb8 - 1, 0), j)),
                  pl.BlockSpec((k, cw), lambda j, i: (0, j))],
        out_specs=pl.BlockSpec((tm, cw), lambda j, i: (i, j)),
        out_shape=jax.ShapeDtypeStruct((t, c), F32), name=name,
        compiler_params=_params(("parallel", "parallel")),
    )(x, x, w)


def conv_bwd(dy, x, w, *, name):
    t = x.shape[0]
    k, c = w.shape
    tm, cw = min(CONV_TM, t), CONV_CW
    nb8 = tm // SUBLANES
    ni = t // tm

    def body(dy_ref, dn_ref, x_ref, p_ref, w_ref, dx_ref, dw_ref):
        i = pl.program_id(1)
        first, last = i == 0, i == ni - 1
        row = lax.broadcasted_iota(jnp.int32, (tm, LANES), 0)
        for cols in _lane_chunks(cw):
            dyv, xv = dy_ref[:, cols], x_ref[:, cols]
            nxt = jnp.where(last, 0.0, dn_ref[:, cols])
            prev = jnp.where(first, 0.0, p_ref[:, cols])
            dx = dyv * w_ref[pl.ds(k - 1, 1), cols]
            dws = [jnp.sum(dyv * xv, axis=0, keepdims=True)]
            for s in range(1, k):
                dx = dx + _shift_up(dyv, nxt, s, row) * w_ref[pl.ds(k - 1 - s, 1), cols]
                dws.append(jnp.sum(dyv * _shift_down(xv, prev, s, row), axis=0, keepdims=True))
            dx_ref[:, cols] = dx.astype(dx_ref.dtype)
            for s in range(k):
                @pl.when(first)
                def _(s=s, dws=dws, cols=cols):
                    dw_ref[pl.ds(k - 1 - s, 1), cols] = dws[s]

                @pl.when(jnp.logical_not(first))
                def _(s=s, dws=dws, cols=cols):
                    dw_ref[pl.ds(k - 1 - s, 1), cols] += dws[s]

    return pl.pallas_call(
        body, grid=(c // cw, ni),
        in_specs=[pl.BlockSpec((tm, cw), lambda j, i: (i, j)),
                  pl.BlockSpec((SUBLANES, cw), lambda j, i: (jnp.minimum((i + 1) * nb8, t // SUBLANES - 1), j)),
                  pl.BlockSpec((tm, cw), lambda j, i: (i, j)),
                  pl.BlockSpec((SUBLANES, cw), lambda j, i: (jnp.maximum(i * nb8 - 1, 0), j)),
                  pl.BlockSpec((k, cw), lambda j, i: (0, j))],
        out_specs=[pl.BlockSpec((tm, cw), lambda j, i: (i, j)), pl.BlockSpec((k, cw), lambda j, i: (0, j))],
        out_shape=[jax.ShapeDtypeStruct((t, c), BF16), jax.ShapeDtypeStruct((k, c), F32)], name=name,
        compiler_params=_params(("parallel", "arbitrary")),
    )(dy, dy, x, x, w)


FFN_TM = 256
FFN_CW = D_FF // 2


def _ffn_specs(t, tm, cw, k):
    ncol = D_FF // cw
    cur = lambda off: pl.BlockSpec((tm, cw), lambda j, i: (i, j + off))
    prev = lambda off, hr: pl.BlockSpec((hr, cw), lambda j, i: (jnp.maximum(i * (tm // hr) - 1, 0), j + off))
    nxt = lambda off, hr: pl.BlockSpec((hr, cw), lambda j, i: (jnp.minimum((i + 1) * (tm // hr), t // hr - 1), j + off))
    taps = lambda off: pl.BlockSpec((k, cw), lambda j, i: (0, j + off))
    return cur, prev, nxt, taps, ncol


def _rows_before(ref, cols, first):
    return jnp.where(first, 0.0, ref[ref.shape[0] - SUBLANES:, cols].astype(F32))


def conv_act_fwd(u, w, *, name):
    t, k = u.shape[0], w.shape[0]
    tm, cw = min(FFN_TM, t), FFN_CW
    cur, prev, _, taps, ncol = _ffn_specs(t, tm, cw, k)

    def body(ug_ref, pg_ref, uv_ref, pv_ref, wg_ref, wv_ref, o_ref):
        first = pl.program_id(1) == 0
        row = lax.broadcasted_iota(jnp.int32, (tm, LANES), 0)
        for cb in range(cw // LANES):
            cols = slice(cb * LANES, (cb + 1) * LANES)
            cg = _conv_taps(ug_ref[:, cols].astype(F32), _rows_before(pg_ref, cols, first), wg_ref, cols, row)
            cv = _conv_taps(uv_ref[:, cols].astype(F32), _rows_before(pv_ref, cols, first), wv_ref, cols, row)
            o_ref[:, cols] = _f_act(cg, cv).astype(o_ref.dtype)

    return pl.pallas_call(
        body, grid=(ncol, t // tm),
        in_specs=[cur(0), prev(0, BF16_ROWS), cur(ncol), prev(ncol, BF16_ROWS), taps(0), taps(ncol)],
        out_specs=cur(0), out_shape=jax.ShapeDtypeStruct((t, D_FF), BF16), name=name,
        compiler_params=_params(("parallel", "parallel")),
    )(u, u, u, u, w, w)


def conv_act_bwd(u, dact, w, *, name):
    t, k = u.shape[0], w.shape[0]
    tm, cw = min(FFN_TM, t), FFN_CW
    cur, prev, nxt, taps, ncol = _ffn_specs(t, tm, cw, k)
    ni = t // tm

    def body(ug_ref, pg_ref, ng_ref, uv_ref, pv_ref, nv_ref, d_ref, dn_ref, wg_ref, wv_ref, dg_ref, dv_ref, dwg_ref, dwv_ref):
        i = pl.program_id(1)
        first, last = i == 0, i == ni - 1
        row = lax.broadcasted_iota(jnp.int32, (tm, LANES), 0)
        row8 = lax.broadcasted_iota(jnp.int32, (SUBLANES, LANES), 0)
        for cb in range(cw // LANES):
            cols = slice(cb * LANES, (cb + 1) * LANES)
            ug, uv = ug_ref[:, cols].astype(F32), uv_ref[:, cols].astype(F32)
            pg, pv = _rows_before(pg_ref, cols, first), _rows_before(pv_ref, cols, first)
            sg = [ug] + [_shift_down(ug, pg, s, row) for s in range(1, k)]
            sv = [uv] + [_shift_down(uv, pv, s, row) for s in range(1, k)]
            taps = lambda xs, w_ref: sum(xs[s] * w_ref[pl.ds(k - 1 - s, 1), cols] for s in range(k))
            _, vjp = jax.vjp(_f_act, taps(sg, wg_ref), taps(sv, wv_ref))
            dcg, dcv = vjp(d_ref[:, cols])
            after = lambda ref: ref[:SUBLANES, cols].astype(F32)
            _, vjp_n = jax.vjp(_f_act, _conv_taps(after(ng_ref), ug[tm - SUBLANES:], wg_ref, cols, row8),
                               _conv_taps(after(nv_ref), uv[tm - SUBLANES:], wv_ref, cols, row8))
            dcgn, dcvn = vjp_n(jnp.where(last, 0.0, dn_ref[:, cols]))
            for dc, dcn, xs, w_ref, dx_ref, dw_ref in ((dcg, dcgn, sg, wg_ref, dg_ref, dwg_ref),
                                                       (dcv, dcvn, sv, wv_ref, dv_ref, dwv_ref)):
                dx = dc * w_ref[pl.ds(k - 1, 1), cols]
                dws = [jnp.sum(dc * xs[0], axis=0, keepdims=True)]
                for s in range(1, k):
                    dx = dx + _shift_up(dc, dcn, s, row) * w_ref[pl.ds(k - 1 - s, 1), cols]
                    dws.append(jnp.sum(dc * xs[s], axis=0, keepdims=True))
                dx_ref[:, cols] = dx.astype(dx_ref.dtype)
                for s in range(k):
                    @pl.when(first)
                    def _(s=s, dw_ref=dw_ref, dws=dws):
                        dw_ref[pl.ds(k - 1 - s, 1), cols] = dws[s]

                    @pl.when(jnp.logical_not(first))
                    def _(s=s, dw_ref=dw_ref, dws=dws):
                        dw_ref[pl.ds(k - 1 - s, 1), cols] += dws[s]

    half = jax.ShapeDtypeStruct((t, D_FF), BF16)
    dwh = jax.ShapeDtypeStruct((k, D_FF), F32)
    return pl.pallas_call(
        body, grid=(ncol, ni),
        in_specs=[cur(0), prev(0, BF16_ROWS), nxt(0, BF16_ROWS), cur(ncol), prev(ncol, BF16_ROWS), nxt(ncol, BF16_ROWS),
                  cur(0), nxt(0, SUBLANES), taps(0), taps(ncol)],
        out_specs=[cur(0), cur(0), taps(0), taps(0)], out_shape=[half, half, dwh, dwh], name=name,
        compiler_params=_params(("parallel", "arbitrary")),
    )(u, u, u, u, u, u, dact, dact, w, w)


def _each(f, *lists):
    return [f(*a) for a in zip(*lists)]


@jax.custom_vjp
def _inv_unit_lower(lms):
    return _inv_blocks(lms)


def _inv_blocks(lms):
    c = lms[0].shape[0]
    ri = lax.broadcasted_iota(jnp.int32, (c, c), 0)
    ci = lax.broadcasted_iota(jnp.int32, (c, c), 1)
    eye = (ri == ci).astype(F32)
    dms = _each(lambda lm: eye - jnp.where((ri >> 1) == (ci >> 1), lm, 0.0), lms)
    for lv in range(1, int(math.log2(c))):
        below = ((ri >> (lv + 1)) == (ci >> (lv + 1))) & ((ri >> lv) != (ci >> lv))
        dbs = _each(lambda dm: dm.astype(BF16), dms)
        ods = _each(lambda lm, db: _dot(jnp.where(below, lm, 0.0).astype(BF16), db).astype(BF16), lms, dbs)
        dms = _each(lambda dm, db, od: dm - _dot(db, od), dms, dbs, ods)
    return dms


def _inv_fwd(lms):
    tms = _inv_blocks(lms)
    return tms, tms


def _inv_bwd(tms, dts):
    tbs = _each(lambda tm: tm.astype(BF16), tms)
    mid = _each(lambda tb, dt: _dot(tb, dt.astype(BF16), TN).astype(BF16), tbs, dts)
    return (_each(lambda m, tb: -_dot(m, tb, NT), mid, tbs),)


_inv_unit_lower.defvjp(_inv_fwd, _inv_bwd)


@jax.custom_vjp
def _inv_known(lms, tms):
    return tms


_inv_known.defvjp(lambda lms, tms: (tms, tms), lambda tms, dts: _inv_bwd(tms, dts) + (_each(jnp.zeros_like, tms),))


def _l2n(x):
    return x * lax.rsqrt(jnp.sum(x * x, axis=-1, keepdims=True) + EPS)


def _prep_fn(cqs, cks, cvs, bg, sel_b, sel_g, tms=None):
    c = cqs[0].shape[0]
    ri = lax.broadcasted_iota(jnp.int32, (c, c), 0)
    ci = lax.broadcasted_iota(jnp.int32, (c, c), 1)
    eye = (ri == ci).astype(F32)
    incl, strict = ci <= ri, ci < ri
    last = lax.broadcasted_iota(jnp.int32, (c, 1), 0) == c - 1
    to_row = lambda col: jnp.sum(col * eye, axis=0, keepdims=True)
    qs = _each(lambda a: _l2n(_silu(a)) * (HEAD_DIM_A ** -0.5), cqs)
    ks = _each(lambda a: _l2n(_silu(a)), cks)
    vbs = _each(lambda a: _silu(a).astype(BF16), cvs)
    betas = _each(lambda m: jnp.sum(bg * m, axis=1, keepdims=True), sel_b)
    gs = _each(lambda m: jnp.sum(bg * m, axis=1, keepdims=True), sel_g)
    gcss = _each(lambda g: jnp.sum(jnp.where(incl, to_row(g), 0.0), axis=1, keepdims=True), gs)
    gtots = _each(lambda gcs: jnp.sum(jnp.where(last, gcs, 0.0), axis=0, keepdims=True), gcss)
    decays = _each(lambda gcs: jnp.exp(jnp.where(incl, gcs - to_row(gcs), NEG)), gcss)
    kbs = _each(lambda k: k.astype(BF16), ks)
    lms = _each(lambda beta, kb, dec: jnp.where(strict, beta * _dot(kb, kb, NT) * dec, 0.0), betas, kbs, decays)
    tms = _inv_unit_lower(lms) if tms is None else _inv_known(lms, tms)
    ams = _each(lambda tm, beta: (tm * to_row(beta)).astype(BF16), tms, betas)
    gams = _each(jnp.exp, gcss)
    u0s = _each(_dot, ams, vbs)
    wks = _each(lambda am, gam, k: _dot(am, (gam * k).astype(BF16)), ams, gams, ks)
    qks = _each(lambda q, kb, dec: _dot(q.astype(BF16), kb, NT) * dec, qs, kbs, decays)
    qds = _each(lambda q, gam: q * gam, qs, gams)
    kds = _each(lambda k, gtot, gcs: k * jnp.exp(gtot - gcs), ks, gtots, gcss)
    gls = _each(lambda gtot: jnp.exp(gtot) * jnp.ones((SUBLANES, LANES), F32), gtots)
    return u0s, wks, qds, kds, qks, gls, tms


def _head_masks(h):
    lane = lax.broadcasted_iota(jnp.int32, (1, LANES), 1)
    return (lane == h).astype(F32), (lane == h + N_HEADS_A).astype(F32)


def _hsl(j):
    return slice(j * HEAD_DIM_A, (j + 1) * HEAD_DIM_A)


def gnorm_fwd(o, zsrc, w, *, name):
    t, width = o.shape
    tm = min(256, t)
    zoff = zsrc.shape[1] // width - 1

    def body(o_ref, z_ref, w_ref, out_ref):
        for h in range(N_HEADS_A):
            out_ref[:, _hsl(h)] = _f_gnorm(o_ref[:, _hsl(h)], z_ref[:, _hsl(h)], w_ref[...]).astype(out_ref.dtype)

    rows = pl.BlockSpec((tm, width), lambda i: (i, 0))
    return pl.pallas_call(
        body, grid=(t // tm,),
        in_specs=[rows, pl.BlockSpec((tm, width), lambda i: (i, zoff)), pl.BlockSpec(w.shape, lambda i: (0, 0))],
        out_specs=rows, out_shape=jax.ShapeDtypeStruct((t, width), BF16), name=name, compiler_params=_params(("parallel",)),
    )(o, zsrc, w)


def gnorm_bwd(o, zsrc, w, don, *, name):
    t, width = o.shape
    tm = min(256, t)
    zoff = zsrc.shape[1] // width - 1

    def body(o_ref, z_ref, w_ref, d_ref, do_ref, dz_ref, dw_ref):
        dw = jnp.zeros(w.shape, F32)
        for h in range(N_HEADS_A):
            _, vjp = jax.vjp(_f_gnorm, o_ref[:, _hsl(h)], z_ref[:, _hsl(h)], w_ref[...])
            do, dz, dwh = vjp(d_ref[:, _hsl(h)])
            do_ref[:, _hsl(h)] = do.astype(do_ref.dtype)
            dz_ref[:, _hsl(h)] = dz.astype(dz_ref.dtype)
            dw = dw + dwh
        first = pl.program_id(0) == 0

        @pl.when(first)
        def _():
            dw_ref[...] = dw

        @pl.when(jnp.logical_not(first))
        def _():
            dw_ref[...] += dw

    rows = pl.BlockSpec((tm, width), lambda i: (i, 0))
    wspec = pl.BlockSpec(w.shape, lambda i: (0, 0))
    return pl.pallas_call(
        body, grid=(t // tm,),
        in_specs=[rows, pl.BlockSpec((tm, width), lambda i: (i, zoff)), wspec, rows],
        out_specs=[rows, rows, wspec],
        out_shape=[jax.ShapeDtypeStruct((t, width), BF16)] * 2 + [jax.ShapeDtypeStruct(w.shape, F32)], name=name,
        compiler_params=_params(("arbitrary",)),
    )(o, zsrc, w, don)


def delta_prep(cqkv, bg, *, name):
    t = cqkv.shape[0]
    nh, hd, n = N_HEADS_A, HEAD_DIM_A, t // CHUNK

    def body(cq_ref, ck_ref, cv_ref, bg_ref, u0_ref, wk_ref, qd_ref, kd_ref, qk_ref, tm_ref, gl_ref):
        heads = range(nh)
        masks = [_head_masks(j) for j in heads]
        res = _prep_fn([cq_ref[:, _hsl(j)] for j in heads], [ck_ref[:, _hsl(j)] for j in heads],
                       [cv_ref[:, _hsl(j)] for j in heads], bg_ref[...], [m[0] for m in masks], [m[1] for m in masks])
        for o_ref, rs in zip((u0_ref, wk_ref, qd_ref, kd_ref, qk_ref, tm_ref), res[:5] + (res[6],)):
            for j in heads:
                o_ref[:, _hsl(j)] = rs[j].astype(o_ref.dtype)
        for j in heads:
            gl_ref[j * SUBLANES:(j + 1) * SUBLANES, :] = res[5][j]

    blk = lambda off: pl.BlockSpec((CHUNK, nh * hd), lambda i: (i, off))
    res = pl.pallas_call(
        body, grid=(n,),
        in_specs=[blk(0), blk(1), blk(2), pl.BlockSpec((CHUNK, LANES), lambda i: (i, 0))],
        out_specs=[blk(0)] * 6 + [pl.BlockSpec((nh * SUBLANES, LANES), lambda i: (i, 0))],
        out_shape=[jax.ShapeDtypeStruct((t, nh * hd), dt) for dt in (F32, BF16, BF16, BF16, BF16, F32)]
        + [jax.ShapeDtypeStruct((n * nh * SUBLANES, LANES), F32)],
        name=name, compiler_params=_params(("parallel",)),
    )(cqkv, cqkv, cqkv, bg)
    return [*res[:5], res[6]], res[5]


def delta_prep_bwd(cqkv, bg, tms, cts, *, name):
    t = cqkv.shape[0]
    nh, hd, n = N_HEADS_A, HEAD_DIM_A, t // CHUNK

    def body(cq_ref, ck_ref, cv_ref, bg_ref, tm_ref, c0, c1, c2, c3, c4, c5, dc_ref, dbg_ref):
        heads = range(nh)
        masks = [_head_masks(j) for j in heads]
        known = [tm_ref[:, _hsl(j)] for j in heads]
        _, vjp = jax.vjp(lambda a, b, c, d: _prep_fn(a, b, c, d, [m[0] for m in masks], [m[1] for m in masks], known)[:6],
                         [cq_ref[:, _hsl(j)] for j in heads], [ck_ref[:, _hsl(j)] for j in heads],
                         [cv_ref[:, _hsl(j)] for j in heads], bg_ref[...])
        cts = tuple([c[:, _hsl(j)] for j in heads] for c in (c0, c1, c2, c3, c4))
        dqs, dks, dvs, dbg = vjp(cts + ([c5[j * SUBLANES:(j + 1) * SUBLANES, :] for j in heads],))
        for part, ds in enumerate((dqs, dks, dvs)):
            for j in heads:
                dc_ref[:, _hsl(part * nh + j)] = ds[j]
        dbg_ref[...] = dbg

    blk = lambda off: pl.BlockSpec((CHUNK, nh * hd), lambda i: (i, off))
    gl_spec = pl.BlockSpec((nh * SUBLANES, LANES), lambda i: (i, 0))
    bg_spec = pl.BlockSpec((CHUNK, LANES), lambda i: (i, 0))
    return pl.pallas_call(
        body, grid=(n,),
        in_specs=[blk(0), blk(1), blk(2), bg_spec] + [blk(0)] * 6 + [gl_spec],
        out_specs=[pl.BlockSpec((CHUNK, 3 * nh * hd), lambda i: (i, 0)), bg_spec],
        out_shape=[jax.ShapeDtypeStruct((t, 3 * nh * hd), F32), jax.ShapeDtypeStruct((t, LANES), F32)],
        name=name, compiler_params=_params(("parallel",)),
    )(cqkv, cqkv, cqkv, bg, tms, *cts)


def delta_scan(u0, wk, qd, kd, qk, gl, *, name):
    t = u0.shape[0]
    nh, hd, n = N_HEADS_A, HEAD_DIM_A, t // CHUNK

    def body(u0_ref, wk_ref, qd_ref, kd_ref, qk_ref, gl_ref, o_ref, sin_ref, s_ref):
        @pl.when(pl.program_id(0) == 0)
        def _():
            s_ref[...] = jnp.zeros_like(s_ref)

        heads = list(range(nh))
        cols = lambda ref: [ref[:, _hsl(h)].astype(BF16) for h in heads]
        ss = [s_ref[h] for h in heads]
        for h in heads:
            sin_ref[h] = ss[h]
        sbs = _each(lambda s: s.astype(BF16), ss)
        ubs = _each(lambda h, wkb, sb: (u0_ref[:, _hsl(h)] - _dot(wkb, sb)).astype(BF16), heads, cols(wk_ref), sbs)
        os_ = _each(lambda qdb, sb, qkb, ub: _dot(qdb, sb) + _dot(qkb, ub), cols(qd_ref), sbs, cols(qk_ref), ubs)
        sn = _each(lambda h, s, kdb, ub: gl_ref[pl.ds(h * SUBLANES, 1), :] * s + _dot(kdb, ub, TN), heads, ss, cols(kd_ref), ubs)
        for h in heads:
            o_ref[:, _hsl(h)] = os_[h]
            s_ref[h] = sn[h]

    blk = pl.BlockSpec((CHUNK, nh * hd), lambda i: (i, 0))
    return pl.pallas_call(
        body, grid=(n,),
        in_specs=[blk] * 5 + [pl.BlockSpec((nh * SUBLANES, LANES), lambda i: (i, 0))],
        out_specs=[blk, pl.BlockSpec((None, nh, hd, hd), lambda i: (i, 0, 0, 0))],
        out_shape=[jax.ShapeDtypeStruct((t, nh * hd), F32), jax.ShapeDtypeStruct((n, nh, hd, hd), F32)],
        scratch_shapes=[pltpu.VMEM((nh, hd, hd), F32)], name=name,
        compiler_params=_params(("arbitrary",)),
    )(u0, wk, qd, kd, qk, gl)


def delta_scan_bwd(do, u0, wk, qd, kd, qk, gl, s_in, *, name):
    t = u0.shape[0]
    nh, hd, n = N_HEADS_A, HEAD_DIM_A, t // CHUNK

    def body(do_ref, u0_ref, wk_ref, qd_ref, kd_ref, qk_ref, gl_ref, sin_ref,
             du0_ref, dwk_ref, dqd_ref, dkd_ref, dqk_ref, dgl_ref, ds_ref):
        @pl.when(pl.program_id(0) == 0)
        def _():
            ds_ref[...] = jnp.zeros_like(ds_ref)

        corner = (lax.broadcasted_iota(jnp.int32, (SUBLANES, LANES), 0) == 0) & (lax.broadcasted_iota(jnp.int32, (SUBLANES, LANES), 1) == 0)
        heads = list(range(nh))
        cols = lambda ref: [ref[:, _hsl(h)].astype(BF16) for h in heads]
        ss, dss = [sin_ref[h] for h in heads], [ds_ref[h] for h in heads]
        sbs, dsbs = _each(lambda s: s.astype(BF16), ss), _each(lambda d: d.astype(BF16), dss)
        dobs, wkbs, qdbs, kdbs, qkbs = cols(do_ref), cols(wk_ref), cols(qd_ref), cols(kd_ref), cols(qk_ref)
        ubs = _each(lambda h, wkb, sb: (u0_ref[:, _hsl(h)] - _dot(wkb, sb)).astype(BF16), heads, wkbs, sbs)
        dus = _each(lambda qkb, dob, kdb, dsb: _dot(qkb, dob, TN) + _dot(kdb, dsb), qkbs, dobs, kdbs, dsbs)
        dubs = _each(lambda du: du.astype(BF16), dus)
        dwks = _each(lambda dub, sb: -_dot(dub, sb, NT), dubs, sbs)
        dqds = _each(lambda dob, sb: _dot(dob, sb, NT), dobs, sbs)
        dkds = _each(lambda ub, dsb: _dot(ub, dsb, NT), ubs, dsbs)
        dqks = _each(lambda dob, ub: _dot(dob, ub, NT), dobs, ubs)
        dgls = _each(lambda s, d: jnp.sum(jnp.sum(s * d, axis=1, keepdims=True), axis=0, keepdims=True), ss, dss)
        dsn = _each(lambda h, d, qdb, dob, wkb, dub: gl_ref[pl.ds(h * SUBLANES, 1), :] * d + _dot(qdb, dob, TN) - _dot(wkb, dub, TN),
                    heads, dss, qdbs, dobs, wkbs, dubs)
        for h in heads:
            du0_ref[:, _hsl(h)] = dus[h]
            dwk_ref[:, _hsl(h)] = dwks[h]
            dqd_ref[:, _hsl(h)] = dqds[h]
            dkd_ref[:, _hsl(h)] = dkds[h]
            dqk_ref[:, _hsl(h)] = dqks[h]
            dgl_ref[h * SUBLANES:(h + 1) * SUBLANES, :] = jnp.where(corner, dgls[h], 0.0)
            ds_ref[h] = dsn[h]

    blk = pl.BlockSpec((CHUNK, nh * hd), lambda i: (n - 1 - i, 0))
    gl_spec = pl.BlockSpec((nh * SUBLANES, LANES), lambda i: (n - 1 - i, 0))
    return pl.pallas_call(
        body, grid=(n,),
        in_specs=[blk] * 6 + [gl_spec, pl.BlockSpec((None, nh, hd, hd), lambda i: (n - 1 - i, 0, 0, 0))],
        out_specs=[blk] * 5 + [gl_spec],
        out_shape=[jax.ShapeDtypeStruct((t, nh * hd), F32)] * 5 + [jax.ShapeDtypeStruct((n * nh * SUBLANES, LANES), F32)],
        scratch_shapes=[pltpu.VMEM((nh, hd, hd), F32)], name=name,
        compiler_params=_params(("arbitrary",)),
    )(do, u0, wk, qd, kd, qk, gl, s_in)


N_PAIRS = N_HEADS_B // 2
PAIRS_PER_KV = N_PAIRS // N_KV_B


def _psl(j):
    return slice(j * LANES, (j + 1) * LANES)


KV_STEP = 4


def _att_fn(qps, kcs, kps, vcs, vps, sinks, kv0, first):
    w = WINDOW
    lane = lax.broadcasted_iota(jnp.int32, (1, LANES), 1)
    lo = (lane < HEAD_DIM_B).astype(F32)
    qi = lax.broadcasted_iota(jnp.int32, (w, w), 0)
    kj = lax.broadcasted_iota(jnp.int32, (w, w), 1)
    dist_c = (qi - kj).astype(F32)
    valid_c = kj <= qi
    valid_p = (kj > qi) & (first < 0.5)
    bf = lambda xs: [a.astype(BF16) for a in xs]
    kcb, kpb, vcb, vpb = bf(kcs), bf(kps), bf(vcs), bf(vps)
    scale = HEAD_DIM_B ** -0.5
    heads = [(g, j, half) for g in range(len(kcs)) for j in range(PAIRS_PER_KV) for half in range(2)]
    kvs = [g for g, _, _ in heads]
    hmasks = [lo if half == 0 else 1.0 - lo for _, _, half in heads]
    hds = [2.0 * (PAIRS_PER_KV * (kv0 + g) + j) + half for g, j, half in heads]
    slopes = _each(lambda hd: jnp.exp(-(hd + 1.0) * (8.0 / N_HEADS_B * math.log(2.0))), hds)
    snks = _each(lambda hd: jnp.sum(sinks * (lane.astype(F32) == hd).astype(F32), axis=1, keepdims=True), hds)
    qhs = _each(lambda h, hm: (qps[h[0] * PAIRS_PER_KV + h[1]] * hm).astype(BF16), heads, hmasks)
    lcs = _each(lambda qh, g, sl: jnp.where(valid_c, _dot(qh, kcb[g], NT) * scale - sl * dist_c, NEG), qhs, kvs, slopes)
    lps = _each(lambda qh, g, sl: jnp.where(valid_p, _dot(qh, kpb[g], NT) * scale - sl * (dist_c + w), NEG), qhs, kvs, slopes)
    ms = _each(lambda lc, lp, sk: lax.stop_gradient(jnp.maximum(jnp.maximum(jnp.max(lc, axis=1, keepdims=True),
                                                                            jnp.max(lp, axis=1, keepdims=True)), sk)), lcs, lps, snks)
    ecs = _each(lambda lc, m: jnp.exp(lc - m), lcs, ms)
    eps = _each(lambda lp, m: jnp.exp(lp - m), lps, ms)
    invs = _each(lambda ec, ep, sk, m: 1.0 / (jnp.sum(ec, axis=1, keepdims=True) + jnp.sum(ep, axis=1, keepdims=True) + jnp.exp(sk - m)),
                 ecs, eps, snks, ms)
    ohs = _each(lambda ec, ep, inv, g, hm: (_dot((ec * inv).astype(BF16), vcb[g]) + _dot((ep * inv).astype(BF16), vpb[g])) * hm,
                ecs, eps, invs, kvs, hmasks)
    return [ohs[2 * j] + ohs[2 * j + 1] for j in range(len(qps))]


def _scalar11(v):
    return jnp.full((1, 1), v, F32)


def _att_specs(row_of):
    cur = pl.BlockSpec((WINDOW, KV_STEP * LANES), lambda i, kv: (row_of(i), kv))
    prev = pl.BlockSpec((WINDOW, KV_STEP * LANES), lambda i, kv: (jnp.maximum(row_of(i) - 1, 0), kv))
    qs = pl.BlockSpec((WINDOW, KV_STEP * PAIRS_PER_KV * LANES), lambda i, kv: (row_of(i), kv))
    return qs, cur, prev, pl.BlockSpec((1, LANES), lambda i, kv: (0, 0))


def swa_fwd(qsrc, kd, vd, sinks, *, name):
    t = kd.shape[0]
    nb = t // WINDOW
    npair = KV_STEP * PAIRS_PER_KV

    def body(q_ref, kc_ref, kp_ref, vc_ref, vp_ref, s_ref, o_ref):
        first = _scalar11((pl.program_id(0) == 0).astype(F32))
        kv0 = _scalar11((pl.program_id(1) * KV_STEP).astype(F32))
        per_kv = lambda ref: [ref[:, _psl(g)] for g in range(KV_STEP)]
        outs = _att_fn([q_ref[:, _psl(j)] for j in range(npair)], per_kv(kc_ref), per_kv(kp_ref), per_kv(vc_ref), per_kv(vp_ref),
                       s_ref[...], kv0, first)
        for j in range(npair):
            o_ref[:, _psl(j)] = outs[j].astype(o_ref.dtype)

    qs, cur, prev, sk = _att_specs(lambda i: i)
    return pl.pallas_call(
        body, grid=(nb, N_KV_B // KV_STEP), in_specs=[qs, cur, prev, cur, prev, sk],
        out_specs=qs, out_shape=jax.ShapeDtypeStruct((t, N_PAIRS * LANES), BF16), name=name,
        compiler_params=_params(("parallel", "parallel")),
    )(qsrc, kd, kd, vd, vd, sinks)


def swa_bwd(do, qsrc, kd, vd, sinks, *, name):
    t = kd.shape[0]
    nb = t // WINDOW

    npair = KV_STEP * PAIRS_PER_KV

    def body(do_ref, q_ref, kc_ref, kp_ref, vc_ref, vp_ref, s_ref, dq_ref, dk_ref, dv_ref, ds_ref, carry_k, carry_v):
        step, kvg = pl.program_id(0), pl.program_id(1)
        first = _scalar11((step == nb - 1).astype(F32))

        @pl.when((step == 0) & (kvg == 0))
        def _():
            carry_k[...] = jnp.zeros_like(carry_k)
            carry_v[...] = jnp.zeros_like(carry_v)
            ds_ref[...] = jnp.zeros_like(ds_ref)

        kv0 = _scalar11((kvg * KV_STEP).astype(F32))
        per_kv = lambda ref: [ref[:, _psl(g)].astype(F32) for g in range(KV_STEP)]
        _, vjp = jax.vjp(lambda *a: _att_fn(*a, kv0, first), [q_ref[:, _psl(j)].astype(F32) for j in range(npair)],
                         per_kv(kc_ref), per_kv(kp_ref), per_kv(vc_ref), per_kv(vp_ref), s_ref[...])
        dqs, dkc, dkp, dvc, dvp, dsk = vjp([do_ref[:, _psl(j)].astype(F32) for j in range(npair)])
        for j in range(npair):
            dq_ref[:, _psl(j)] = dqs[j].astype(dq_ref.dtype)
        ds_ref[...] += dsk
        fold = lambda g: g + pltpu.roll(g, HEAD_DIM_B, 1)
        for g in range(KV_STEP):
            kv = kvg * KV_STEP + g
            dk_ref[:, _psl(g)] = fold(dkc[g] + carry_k[kv]).astype(dk_ref.dtype)
            dv_ref[:, _psl(g)] = fold(dvc[g] + carry_v[kv]).astype(dv_ref.dtype)
            carry_k[kv] = dkp[g]
            carry_v[kv] = dvp[g]

    qs, cur, prev, sk = _att_specs(lambda i: nb - 1 - i)
    return pl.pallas_call(
        body, grid=(nb, N_KV_B // KV_STEP),
        in_specs=[qs, qs, cur, prev, cur, prev, sk],
        out_specs=[qs, cur, cur, sk],
        out_shape=[jax.ShapeDtypeStruct((t, N_PAIRS * LANES), BF16), jax.ShapeDtypeStruct((t, N_KV_B * LANES), BF16),
                   jax.ShapeDtypeStruct((t, N_KV_B * LANES), BF16), jax.ShapeDtypeStruct((1, LANES), F32)],
        scratch_shapes=[pltpu.VMEM((N_KV_B, WINDOW, LANES), F32), pltpu.VMEM((N_KV_B, WINDOW, LANES), F32)],
        name=name, compiler_params=_params(("arbitrary", "arbitrary")),
    )(do, qsrc, kd, kd, vd, vd, sinks)


def loss_head(h, tgt, w, *, name):
    t, d = h.shape
    tm = min(256, t)

    def body(h_ref, t_ref, w_ref, dh_ref, dw_ref, l_ref):
        tg = t_ref[...]

        def f(hv, wv):
            err = _f_norm(hv, wv) - tg
            return 0.5 * jnp.sum(jnp.sum(err * err, axis=1, keepdims=True), axis=0, keepdims=True) * (1.0 / d)

        lv, vjp = jax.vjp(f, h_ref[...], w_ref[...])
        dh, dw = vjp(jnp.ones((1, 1), F32))
        dh_ref[...] = dh
        first = pl.program_id(0) == 0

        @pl.when(first)
        def _():
            dw_ref[...] = dw
            l_ref[...] = lv * jnp.ones((1, LANES), F32)

        @pl.when(jnp.logical_not(first))
        def _():
            dw_ref[...] += dw
            l_ref[...] += lv * jnp.ones((1, LANES), F32)

    rows = pl.BlockSpec((tm, d), lambda i: (i, 0))
    one = lambda c: pl.BlockSpec((1, c), lambda i: (0, 0))
    return pl.pallas_call(
        body, grid=(t // tm,), in_specs=[rows, rows, one(d)], out_specs=[rows, one(d), one(LANES)],
        out_shape=[jax.ShapeDtypeStruct((t, d), F32), jax.ShapeDtypeStruct((1, d), F32), jax.ShapeDtypeStruct((1, LANES), F32)],
        name=name, compiler_params=_params(("arbitrary",)),
    )(h, tgt, w)


def _row_tile(r, cap=256):
    tr = r
    if r % SUBLANES == 0:
        for cand in range(SUBLANES, min(r, cap) + 1, SUBLANES):
            if r % cand == 0:
                tr = cand
    return tr


def _adamw_update(wv, gv, mv, vv):
    mn = ADAM_B1 * mv + (1.0 - ADAM_B1) * gv
    vn = ADAM_B2 * vv + (1.0 - ADAM_B2) * jnp.square(gv)
    m_hat = mn / (1.0 - ADAM_B1 ** ADAM_STEP)
    v_hat = vn / (1.0 - ADAM_B2 ** ADAM_STEP)
    return -ADAM_LR * (m_hat / (jnp.sqrt(v_hat) + ADAM_EPS) + ADAM_WD * wv), mn, vn


def adamw_layers(w, halves, m, v, *, name):
    nl, r, c = w.shape
    tr = _row_tile(r // 2)
    nbh = r // 2 // tr

    def body(w_ref, *rest):
        g_refs, m_ref, v_ref = rest[:2 * nl], rest[2 * nl], rest[2 * nl + 1]
        d_ref, mo_ref, vo_ref, go_ref = rest[2 * nl + 2:]
        layer, i = pl.program_id(0), pl.program_id(1)
        mine = (i < nbh) == (lax.axis_index("c") == 0)
        gv = jnp.where(mine, g_refs[0][...], g_refs[1][...])
        for k in range(1, nl):
            gv = jnp.where(layer == k, jnp.where(mine, g_refs[2 * k][...], g_refs[2 * k + 1][...]), gv)
        d_ref[...], mo_ref[...], vo_ref[...] = _adamw_update(w_ref[...], gv, m_ref[...], v_ref[...])
        go_ref[...] = gv

    spec3 = pl.BlockSpec((None, tr, c), lambda k, i: (k, i, 0))
    g_specs = [pl.BlockSpec((tr, c), lambda k, i, q=q: (jnp.where(k == q, i % nbh, 0), 0)) for q in range(nl) for _ in range(2)]
    return pl.pallas_call(
        body, grid=(nl, r // tr), in_specs=[spec3] + g_specs + [spec3, spec3], out_specs=[spec3] * 4,
        out_shape=[jax.ShapeDtypeStruct((nl, r, c), F32)] * 4, name=name, compiler_params=_params(("arbitrary", "arbitrary")),
    )(w, *[h for pair in halves for h in pair], m, v)


def adamw(w, g, m, v, *, name):
    r, c = w.shape
    tr = _row_tile(r)

    def body(w_ref, g_ref, m_ref, v_ref, d_ref, mo_ref, vo_ref):
        d_ref[...], mo_ref[...], vo_ref[...] = _adamw_update(w_ref[...], g_ref[...], m_ref[...], v_ref[...])

    spec = pl.BlockSpec((tr, c), lambda i: (i, 0))
    return pl.pallas_call(
        body, grid=(r // tr,), in_specs=[spec] * 4, out_specs=[spec] * 3,
        out_shape=[jax.ShapeDtypeStruct((r, c), F32)] * 3, name=name, compiler_params=_params(("parallel",)),
    )(w, g, m, v)


def _place():
    return lax.axis_index("x"), lax.axis_index("y"), lax.axis_index("c")


def allgather8(blk, *, name):
    def body(x_ref, out_ref, send_sems, recv_sems, local_sem):
        x, y, c = _place()
        me = 4 * x + 2 * y + c
        mine = pltpu.make_async_copy(x_ref, out_ref.at[me], local_sem)
        mine.start()
        sent = []
        for k in range(1, N_DEV):
            to = (x ^ ((k >> 2) & 1), y ^ ((k >> 1) & 1), c ^ (k & 1))
            cp = pltpu.make_async_remote_copy(src_ref=x_ref, dst_ref=out_ref.at[me], send_sem=send_sems.at[k - 1],
                                              recv_sem=recv_sems.at[k - 1], device_id=to, device_id_type=MESH)
            cp.start()
            sent.append(cp)
        for k in range(1, N_DEV):
            frm = me ^ k
            pltpu.make_async_remote_copy(src_ref=x_ref, dst_ref=out_ref.at[frm], send_sem=send_sems.at[k - 1],
                                         recv_sem=recv_sems.at[k - 1], device_id=(x, y, c), device_id_type=MESH).wait_recv()
        for cp in sent:
            cp.wait_send()
        mine.wait()

    vm = pl.BlockSpec(memory_space=pltpu.VMEM)
    return pl.pallas_call(
        body, in_specs=[vm], out_specs=vm, out_shape=jax.ShapeDtypeStruct((N_DEV,) + blk.shape, blk.dtype), name=name,
        scratch_shapes=[pltpu.SemaphoreType.DMA((N_DEV - 1,)), pltpu.SemaphoreType.DMA((N_DEV - 1,)), pltpu.SemaphoreType.DMA],
    )(blk)


def _other_chips(x, y):
    return [(1 - x, y), (x, 1 - y), (1 - x, 1 - y)]


def _hbm_call(body, ins, out_shapes, n_sems, name):
    hbm = pl.BlockSpec(memory_space=pl.ANY)
    return pl.pallas_call(
        body, in_specs=[hbm] * len(ins), out_specs=[hbm] * len(out_shapes), out_shape=out_shapes, name=name,
        scratch_shapes=[pltpu.SemaphoreType.DMA((n_sems,)), pltpu.SemaphoreType.DMA((n_sems,))],
    )(*ins)


def _half_rows(c, rh):
    return pl.ds(pl.multiple_of(c * rh, BF16_ROWS), rh)


def gather_units(units, *, name):
    nu = len(units)
    shapes = []
    for arr, layer_major in units:
        r, cols = arr.shape
        shapes.append(jax.ShapeDtypeStruct((2, N_CHIPS, r // 2, cols) if layer_major else (N_CHIPS, r, cols), arr.dtype))

    def body(*refs):
        in_refs, out_refs, send_sems, recv_sems = refs[:nu], refs[nu:2 * nu], refs[2 * nu], refs[2 * nu + 1]
        x, y, c = _place()
        me_chip = 2 * x + y
        sib = (x, y, 1 - c)
        chips = _other_chips(x, y)

        def copy(k, src, dst, to):
            return pltpu.make_async_remote_copy(src_ref=src, dst_ref=dst, send_sem=send_sems.at[k], recv_sem=recv_sems.at[k],
                                                device_id=to, device_id_type=MESH)

        first, passed, landing = [], [], []
        for u, (arr, layer_major) in enumerate(units):
            rh = arr.shape[0] // 2
            out_ref = out_refs[u]
            slot = (lambda chip, half, o=out_ref: o.at[half, chip]) if layer_major else \
                   (lambda chip, half, o=out_ref, rh=rh: o.at[chip, _half_rows(half, rh), :])
            my_half = in_refs[u].at[_half_rows(c, rh), :]
            for j, (cx, cy) in enumerate(chips):
                k = 6 * u + j
                first.append(copy(k, my_half, slot(me_chip, c), (cx, cy, c)))
                passed.append(copy(k + 3, slot(2 * cx + cy, c), slot(2 * cx + cy, c), sib))
                landing.append((copy(k, my_half, slot(2 * cx + cy, c), sib), copy(k + 3, my_half, slot(2 * cx + cy, 1 - c), sib)))
        for cp in first:
            cp.start()
        for (over_ici, _), fwd in zip(landing, passed):
            over_ici.wait_recv()
            fwd.start()
        for _, from_sibling in landing:
            from_sibling.wait_recv()
        for cp in first + passed:
            cp.wait_send()

    return _hbm_call(body, [a for a, _ in units], shapes, 6 * nu, name)


HBM_SPEC = pl.BlockSpec(memory_space=pltpu.HBM)
SEM_SPEC = pl.BlockSpec(memory_space=pltpu.SEMAPHORE)
ORDERED_EFFECT = pltpu.SideEffectType.DATAFLOW_SIDE_EFFECTING


def _split_start(body, srcs, land_shapes, after, *, name):
    nu = len(srcs)
    lands = [lax.empty(s.shape, s.dtype) for s in land_shapes]

    def whole(*refs):
        body(refs[:nu], refs[nu:2 * nu], refs[2 * nu + 1], refs[2 * nu + 2])
        refs[-1][...] = jnp.zeros((SUBLANES, LANES), F32)

    hbm = lambda a: pltpu.with_memory_space_constraint(a, pltpu.HBM)
    sems = pltpu.SemaphoreType.DMA((nu,))
    res = pl.pallas_call(
        whole, name=name, in_specs=[HBM_SPEC] * (2 * nu) + [pl.BlockSpec(memory_space=pl.ANY)],
        out_shape=[sems, sems] + [pltpu.HBM(a.shape, a.dtype) for a in srcs] + [pltpu.HBM(s.shape, s.dtype) for s in land_shapes]
        + [jax.ShapeDtypeStruct((SUBLANES, LANES), F32)],
        out_specs=[SEM_SPEC, SEM_SPEC] + [HBM_SPEC] * (2 * nu) + [pl.BlockSpec(memory_space=pltpu.VMEM)],
        input_output_aliases={q: 2 + q for q in range(2 * nu)},
        compiler_params=pltpu.CompilerParams(has_side_effects=ORDERED_EFFECT),
    )(*[hbm(a) for a in srcs], *[hbm(a) for a in lands], after)
    return res[0], res[1], res[2:2 + nu], res[2 + nu:2 + 2 * nu], res[-1]


def _split_wait(pending, moved, after, *, name):
    send_sems, recv_sems, srcs, lands, _ = pending
    nu = len(srcs)

    def body(*refs):
        land_refs, ssem, rsem = refs[nu:2 * nu], refs[2 * nu], refs[2 * nu + 1]
        x, y, c = _place()
        for u in range(nu):
            size = moved(land_refs[u])
            cp = pltpu.make_async_remote_copy(src_ref=size, dst_ref=size, send_sem=ssem.at[u], recv_sem=rsem.at[u],
                                              device_id=(x, y, c), device_id_type=MESH)
            cp.wait_send()
            cp.wait_recv()

    res = pl.pallas_call(
        body, name=name, in_specs=[HBM_SPEC] * (2 * nu) + [SEM_SPEC, SEM_SPEC, pl.BlockSpec(memory_space=pl.ANY)],
        out_shape=[pltpu.HBM(a.shape, a.dtype) for a in srcs] + [pltpu.HBM(a.shape, a.dtype) for a in lands],
        out_specs=[HBM_SPEC] * (2 * nu), input_output_aliases={q: q for q in range(2 * nu)},
        compiler_params=pltpu.CompilerParams(has_side_effects=ORDERED_EFFECT),
    )(*srcs, *lands, send_sems, recv_sems, after)
    return res[nu:]


def gather_start(shards, after, *, name):
    def body(src_refs, land_refs, send_sems, recv_sems):
        x, y, c = _place()
        for u, shard in enumerate(shards):
            rows = _half_rows(c, shard.shape[0] // 2)
            for cx, cy in _other_chips(x, y):
                for core in range(2):
                    pltpu.make_async_remote_copy(src_ref=src_refs[u].at[rows, :], dst_ref=land_refs[u].at[2 * x + y, rows, :],
                                                 send_sem=send_sems.at[u], recv_sem=recv_sems.at[u], device_id=(cx, cy, core),
                                                 device_id_type=MESH).start()

    return _split_start(body, shards, [jax.ShapeDtypeStruct((N_CHIPS,) + s.shape, s.dtype) for s in shards], after, name=name)


def gather_wait(pending, after, *, name):
    return _split_wait(pending, lambda land: land.at[pl.ds(0, N_CHIPS - 1)], after, name=name)


def scatter_start(pairs, *, name):
    def body(src_refs, land_refs, send_sems, recv_sems):
        x, y, c = _place()
        for u in range(len(pairs)):
            for j, (cx, cy) in enumerate(_other_chips(x, y)):
                pltpu.make_async_remote_copy(src_ref=src_refs[u].at[2 * cx + cy], dst_ref=land_refs[u].at[j], send_sem=send_sems.at[u],
                                             recv_sem=recv_sems.at[u], device_id=(cx, cy, c), device_id_type=MESH).start()

    return _split_start(body, pairs, [jax.ShapeDtypeStruct((N_CHIPS - 1,) + p.shape[1:], p.dtype) for p in pairs], pairs[0], name=name)


def scatter_wait(pending, after, *, name):
    return _split_wait(pending, lambda land: land, after, name=name)


def swap_units(units, *, name):
    nu = len(units)

    def body(*refs):
        g_refs, out_refs, send_sems, recv_sems = refs[:nu], refs[nu:2 * nu], refs[2 * nu], refs[2 * nu + 1]
        x, y, c = _place()
        cps = [pltpu.make_async_remote_copy(src_ref=g_refs[u].at[:, _half_rows(1 - c, units[u].shape[1] // 2), :], dst_ref=out_refs[u],
                                            send_sem=send_sems.at[u], recv_sem=recv_sems.at[u], device_id=(x, y, 1 - c),
                                            device_id_type=MESH) for u in range(nu)]
        for cp in cps:
            cp.start()
        for cp in cps:
            cp.wait()

    shapes = [jax.ShapeDtypeStruct((N_CHIPS, g.shape[1] // 2, g.shape[2]), g.dtype) for g in units]
    return _hbm_call(body, units, shapes, nu, name)


def join_units(units, *, name):
    nu = len(units)

    def body(*refs):
        h_refs, out_refs, send_sems, recv_sems = refs[:nu], refs[nu:2 * nu], refs[2 * nu], refs[2 * nu + 1]
        x, y, c = _place()
        cps = [pltpu.make_async_remote_copy(src_ref=h_refs[u], dst_ref=out_refs[u], send_sem=send_sems.at[u], recv_sem=recv_sems.at[u],
                                            device_id=(x, y, 1 - c), device_id_type=MESH) for u in range(nu)]
        for cp in cps:
            cp.start()
        for cp in cps:
            cp.wait()

    return _hbm_call(body, units, [jax.ShapeDtypeStruct(h.shape, h.dtype) for h in units], nu, name)


def _half_tile(rh):
    tr = rh
    for cand in range(BF16_ROWS, min(rh, 512) + 1, BF16_ROWS):
        if rh % cand == 0:
            tr = cand
    return tr


def pair_add(g, sib, *, name):
    nc, rh, cols = sib.shape
    tr = _half_tile(rh)
    nbh = rh // tr

    def body(g0_ref, g1_ref, s_ref, o_ref):
        mine = jnp.where(lax.axis_index("c") == 0, g0_ref[...], g1_ref[...])
        o_ref[...] = (mine.astype(F32) + s_ref[...].astype(F32)).astype(o_ref.dtype)

    blk = lambda off: pl.BlockSpec((None, tr, cols), lambda j, i: (j, off + i, 0))
    return pl.pallas_call(
        body, grid=(nc, nbh), in_specs=[blk(0), blk(nbh), blk(0)], out_specs=blk(0),
        out_shape=jax.ShapeDtypeStruct(sib.shape, BF16), name=name, compiler_params=_params(("parallel", "parallel")),
    )(g, g, sib)


def chips_add(pair, landed, *, name):
    nc, rh, cols = pair.shape
    tr = _half_tile(rh)

    def body(*refs):
        chip = 2 * lax.axis_index("x") + lax.axis_index("y")
        acc = refs[0][...]
        for j in range(1, nc):
            acc = jnp.where(chip == j, refs[j][...], acc)
        acc = acc.astype(F32)
        for r in refs[nc:-1]:
            acc = acc + r[...].astype(F32)
        refs[-1][...] = acc

    part = lambda q: pl.BlockSpec((None, tr, cols), lambda i, q=q: (q, i, 0))
    return pl.pallas_call(
        body, grid=(rh // tr,), in_specs=[part(q) for q in range(nc)] + [part(q) for q in range(landed.shape[0])],
        out_specs=pl.BlockSpec((tr, cols), lambda i: (i, 0)),
        out_shape=jax.ShapeDtypeStruct((rh, cols), F32), name=name, compiler_params=_params(("parallel",)),
    )(*[pair] * nc, *[landed] * landed.shape[0])


def sum8(g, *, name):
    def body(g_ref, o_ref):
        acc = g_ref[0]
        for d in range(1, N_DEV):
            acc = acc + g_ref[d]
        o_ref[...] = acc

    return pl.pallas_call(body, out_shape=jax.ShapeDtypeStruct(g.shape[1:], F32), name=name)(g)


def _dup_halves(a):
    t = a.shape[0]
    a = a.reshape(t, N_KV_B, HEAD_DIM_B)
    return jnp.concatenate([a, a], axis=-1).reshape(t, N_KV_B * LANES)


def _undup(a):
    t = a.shape[0]
    return a.reshape(t, N_KV_B, LANES)[:, :, :HEAD_DIM_B].reshape(t, N_KV_B * HEAD_DIM_B)


def _lane_pad(v, offset=0):
    return jnp.zeros((1, LANES), F32).at[0, offset:offset + v.shape[0]].set(v)


SHARD_UP = 2 * D_FF // N_CHIPS
SHARD_BIN = (N_HEADS_B + 2 * N_KV_B) * HEAD_DIM_B // N_CHIPS
SHARD_PROJ = D_MODEL // N_CHIPS


def local_step(x, p, tgt, sm, weight, on_grads):
    t = x.shape[0]
    rtm = min(256, t)
    hk = N_HEADS_A * HEAD_DIM_A
    qd_b = N_HEADS_B * HEAD_DIM_B
    kd_b = N_KV_B * HEAD_DIM_B
    gs = {}
    norm = lambda h, w, nm: tile_map(_f_norm, [(h, D_MODEL, 0)], [w], [(D_MODEL, BF16)], tm=rtm, ncol=1, name=nm)[0]

    spec = pl.BlockSpec
    mtm = _tile(D_MODEL, MM_TM_CAP)
    p_bf = p.astype(BF16)
    alog_p = _lane_pad(sm["a_log"][0], N_HEADS_A)
    dtb_p = _lane_pad(sm["a_dt_bias"][0], N_HEADS_A)
    sinks_p = _lane_pad(sm["b_sinks"][0])
    nw = lambda name, i: sm[name][i:i + 1]
    by_chip = lambda kdim, ns: dict(tn=ns, tk=kdim, b_spec=spec((None, kdim, ns), lambda r, j, kk: (j, kk, 0)))
    by_chip_t = lambda ndim, ns: dict(n=ndim, tn=ndim, tk=ns, b_spec=spec((None, ndim, ns), lambda r, j, kk: (kk, j, 0)))
    cache = {}

    def wgt(name, i, after):
        if (name, i) not in cache:
            cache[name, i] = weight(name, i, after)
        return cache[name, i]

    saved = []
    h = x
    hn_next = norm(h, nw("norm_mix", 0), "norm_mix0")
    for i in range(DEPTH):
        s = {"h0": h, "hn": hn_next}
        if i % 2 == 0:
            s["pm"] = mm(s["hn"], wgt("a_w_in", i, h), name="a_in")
            tail = (s["pm"], LANES, 4 * hk // LANES)
            s["c"] = conv_fwd(s["pm"], wgt("a_conv", i, h), name="a_conv")
            s["bg"] = tile_map(_f_betag, [tail], [alog_p, dtb_p], [(LANES, F32)], tm=rtm, ncol=1, name="a_betag")[0]
            s["prep"], s["tms"] = delta_prep(s["c"], s["bg"], name="a_prep")
            s["o"], s["s_in"] = delta_scan(*s["prep"], name="a_scan")
            s["on"] = gnorm_fwd(s["o"], s["pm"], sm["a_norm"], name="a_gnorm")
            h, s["hf"] = mm(s["on"], wgt("a_w_out", i, s["on"]), add=h, norm_w=nw("norm_ffn", i), name="a_out")
        else:
            s["pb"] = mm(s["hn"], wgt("b_w_in", i, s["hn"]), name="b_in", out_dtype=BF16, n=N_CHIPS * SHARD_BIN,
                         **by_chip(D_MODEL, SHARD_BIN))
            s["kd"], s["vd"] = _dup_halves(s["pb"][:, qd_b:qd_b + kd_b]), _dup_halves(s["pb"][:, qd_b + kd_b:])
            s["ao"] = swa_fwd(s["pb"], s["kd"], s["vd"], sinks_p, name="b_att")
            h, s["hf"] = mm(s["ao"], wgt("b_w_out", i, s["ao"]), add=h, norm_w=nw("norm_ffn", i), name="b_out")
        s["h1"] = h
        s["u"] = mm(s["hf"], wgt("f_w_up", i, s["hf"]), name=f"f_up{i}", out_dtype=BF16, n=2 * D_FF, tm_cap=2 * MM_TM_CAP,
                    **by_chip(D_MODEL, SHARD_UP))
        s["act"] = conv_act_fwd(s["u"], wgt("f_conv", i, s["hf"]), name=f"f_conv_act{i}")
        h, s["hp"] = mm(s["act"], wgt("f_w_down", i, s["act"]), add=h, norm_w=nw("norm_ple", i), name=f"f_down{i}", tk=D_FF)
        s["h2"] = h
        s["gl"] = mm(s["hp"], wgt("ple_w_gate", i, s["hp"]), name=f"ple_gate{i}")
        s["pe"] = mm(p_bf[i], wgt("ple_w_proj", i, s["hp"]), name=f"ple_proj{i}", n=D_MODEL, **by_chip(PLE_DIM, SHARD_PROJ))
        rows3 = [(h, D_MODEL, 0), (s["gl"], D_MODEL, 0), (s["pe"], D_MODEL, 0)]
        if i + 1 < DEPTH:
            def mix_norm(hv, g, e, wn):
                hn = hv + _f_ple(g, e)
                return hn, _f_norm(hn, wn)
            h, hn_next = tile_map(mix_norm, rows3, [nw("norm_mix", i + 1)], [(D_MODEL, F32), (D_MODEL, BF16)], tm=rtm, ncol=1,
                                  name=f"ple_mix{i}")
        else:
            h = tile_map(lambda hv, g, e: hv + _f_ple(g, e), rows3, [], [(D_MODEL, F32)], tm=rtm, ncol=1, name=f"ple_mix{i}")[0]
        saved.append(s)

    dh, gnf, loss = loss_head(h, tgt, sm["norm_final"][None, :], name="loss_head")
    gs["norm_final"] = gnf[0]

    g_mix, g_ffn, g_ple, g_conv = ([None] * DEPTH for _ in range(4))
    zero = jnp.zeros((1, 1), F32)
    for i in reversed(range(DEPTH)):
        s, gw = saved[i], {}
        by_rows = lambda g: g.reshape(N_CHIPS, g.shape[0] // N_CHIPS, g.shape[1])
        (dgl, dpe), _ = tile_vjp(_f_ple, [(s["gl"], D_MODEL, 0), (s["pe"], D_MODEL, 0)], [], [(dh, D_MODEL, 0)], n_diff=2,
                                 tm=rtm, ncol=1, name=f"ple_mix_bwd{i}", grad_dtypes=[BF16, BF16])
        gw["ple_w_proj"] = mm(p_bf[i], dpe, ta=True, name=f"ple_proj_dw{i}", out_dtype=BF16, tn=SHARD_PROJ,
                              o_shape=(N_CHIPS, PLE_DIM, SHARD_PROJ), o_spec=spec((None, PLE_DIM, SHARD_PROJ), lambda r, j, kk: (j, r, 0)))
        gw["ple_w_gate"] = by_rows(mm(s["hp"], dgl, ta=True, name=f"ple_gate_dw{i}", out_dtype=BF16))
        fused = dict(tb=True, tm_cap=MM_TM_CAP // 2)
        dh, g_ple[i] = mm(dgl, cache["ple_w_gate", i], name=f"ple_gate_dx{i}", norm_grad=(s["h2"], nw("norm_ple", i) + zero, dh), **fused)

        dact = mm(dh, cache["f_w_down", i], tb=True, name=f"f_down_dx{i}")
        gw["f_w_down"] = by_rows(mm(s["act"], dh, ta=True, name=f"f_down_dw{i}", out_dtype=BF16, tm_cap=D_FF // 2))
        du_halves = conv_act_bwd(s["u"], dact, cache["f_conv", i], name=f"f_conv_act_bwd{i}")
        g_conv[i] = jnp.concatenate(du_halves[2:], axis=1)
        dhf = g_up = None
        for half, du in enumerate(du_halves[:2]):
            c0 = half * (N_CHIPS // 2)
            g_up = mm(s["hf"], du, ta=True, name=f"f_up_dw{i}_{half}", out_dtype=BF16, tn=SHARD_UP, into=g_up,
                      o_shape=(N_CHIPS, D_MODEL, SHARD_UP), o_spec=spec((None, mtm, SHARD_UP), lambda r, j, kk, c0=c0: (c0 + j, r, 0)))
            last = dict(norm_grad=(s["h1"], nw("norm_ffn", i), dh), **fused) if half else dict(tb=True)
            dhf = mm(du, cache["f_w_up", i], name=f"f_up_dx{i}_{half}", n=D_MODEL, tn=D_MODEL, tk=SHARD_UP, add=dhf,
                     b_spec=spec((None, D_MODEL, SHARD_UP), lambda r, j, kk, c0=c0: (c0 + kk, j, 0)), **last)
        gw["f_w_up"] = g_up
        dh, g_ffn[i] = dhf
        token, gw = on_grads(i, "ffn", gw), {}
        w_out = cache["a_w_out" if i % 2 == 0 else "b_w_out", i]
        if token is not None:
            w_out = w_out + token[:1, :1].astype(BF16)

        if i % 2 == 0:
            don = mm(dh, w_out, tb=True, name="a_out_dx")
            gw["a_w_out"] = by_rows(mm(s["on"], dh, ta=True, name="a_out_dw", out_dtype=BF16))
            do, dz, gs["a_norm"] = gnorm_bwd(s["o"], s["pm"], sm["a_norm"], don, name="a_gnorm_bwd")
            dprep = delta_scan_bwd(do, *s["prep"], s["s_in"], name="a_scan_bwd")
            dc, dbg = delta_prep_bwd(s["c"], s["bg"], s["tms"], dprep, name="a_prep_bwd")
            (dpt,), (galog, gdtb) = tile_vjp(_f_betag, [(s["pm"], LANES, 4 * hk // LANES)], [alog_p, dtb_p], [(dbg, LANES, 0)], n_diff=1,
                                             tm=rtm, ncol=1, name="a_betag_bwd", grad_dtypes=[BF16])
            gs["a_log"] = galog[:, N_HEADS_A:2 * N_HEADS_A]
            gs["a_dt_bias"] = gdtb[:, N_HEADS_A:2 * N_HEADS_A]
            dqkv, gs["a_conv"] = conv_bwd(dc, s["pm"], cache["a_conv", i], name="a_conv_bwd")
            dpm = jnp.concatenate([dqkv, dz, dpt], axis=1)
            g_in = mm(s["hn"], dpm, ta=True, name="a_in_dw", out_dtype=BF16)[:, :4 * hk + 2 * N_HEADS_A]
            gw["a_w_in"] = g_in.reshape(D_MODEL, N_CHIPS, g_in.shape[1] // N_CHIPS).transpose(1, 0, 2)
            dh, g_mix[i] = mm(dpm, cache["a_w_in", i], name="a_in_dx", norm_grad=(s["h0"], nw("norm_mix", i), dh), **fused)
        else:
            dao = mm(dh, w_out, tb=True, name="b_out_dx")
            gw["b_w_out"] = by_rows(mm(s["ao"], dh, ta=True, name="b_out_dw", out_dtype=BF16))
            dq, dkd, dvd, gsk = swa_bwd(dao, s["pb"], s["kd"], s["vd"], sinks_p, name="b_att_bwd")
            gs["b_sinks"] = gsk[:, :N_HEADS_B]
            dpb = jnp.concatenate([dq, _undup(dkd), _undup(dvd)], axis=1)
            gw["b_w_in"] = mm(s["hn"], dpb, ta=True, name="b_in_dw", out_dtype=BF16, tn=SHARD_BIN,
                              o_shape=(N_CHIPS, D_MODEL, SHARD_BIN), o_spec=spec((None, mtm, SHARD_BIN), lambda r, j, kk: (j, r, 0)))
            dh, g_mix[i] = mm(dpb, cache["b_w_in", i], name="b_in_dx", norm_grad=(s["h0"], nw("norm_mix", i), dh), **fused,
                              **by_chip_t(D_MODEL, SHARD_BIN))
        token = on_grads(i, "mix", gw)
        if token is not None:
            zero = token[:1, :1]

    gs["norm_mix"], gs["norm_ffn"], gs["norm_ple"] = (jnp.concatenate(g, axis=0) for g in (g_mix, g_ffn, g_ple))
    gs["f_conv"] = jnp.stack(g_conv)
    return loss, dh, gs


BIG = ["a_w_in", "a_w_out", "b_w_in", "b_w_out", "f_w_up", "f_w_down", "ple_w_proj", "ple_w_gate"]
LAYERED = {"f_w_up", "f_w_down", "ple_w_proj", "ple_w_gate"}
BY_CHIP = {"b_w_in", "f_w_up", "ple_w_proj"}
LAYER_UNITS = [[("a_w_in", 0), ("a_w_out", 0)] + [(n, 0) for n in sorted(LAYERED)],
               [("b_w_in", 1), ("b_w_out", 1)] + [(n, 1) for n in sorted(LAYERED)]]
CONVS = ["a_conv", "f_conv"]
SMALL = ["norm_mix", "norm_ffn", "norm_ple", "norm_final", "a_log", "a_dt_bias", "a_norm", "b_sinks"]
SMALL_ROWS = 8
CONV_ROWS = 16
CONV_GRAD_ROWS = 48


def _pack_rows(arrs, rows, dtype):
    flat = jnp.concatenate([a.reshape(-1).astype(dtype) for a in arrs])
    return jnp.pad(flat, (0, rows * PACK_COLS - flat.shape[0])).reshape(rows, PACK_COLS)


def _unpack(flat, shapes):
    out, off = [], 0
    for shp in shapes:
        n = math.prod(shp)
        out.append(flat[off:off + n].reshape(shp))
        off += n
    return out


def _pack_small(d, loss=None):
    tail = jnp.concatenate([d["a_log"].reshape(-1), d["a_dt_bias"].reshape(-1), d["a_norm"].reshape(-1), d["b_sinks"].reshape(-1)])
    if loss is not None:
        tail = jnp.concatenate([tail, loss.reshape(-1)[:1]])
    tail = jnp.pad(tail, (0, PACK_COLS - tail.shape[0]))
    return jnp.concatenate([d["norm_mix"], d["norm_ffn"], d["norm_ple"], d["norm_final"][None, :], tail[None, :]], axis=0)


def _unpack_small(a, like):
    out = {"norm_mix": a[0:2], "norm_ffn": a[2:4], "norm_ple": a[4:6], "norm_final": a[6]}
    off = 0
    for nm in ("a_log", "a_dt_bias", "a_norm", "b_sinks"):
        n = like[nm].size
        out[nm] = a[7, off:off + n].reshape(like[nm].shape)
        off += n
    return out, a[7, off]


def _as2d(a):
    return a.reshape(-1, a.shape[-1])


def kernel(x, p, norm_mix, norm_ffn, norm_ple, norm_final, a_w_in, a_conv, a_log, a_dt_bias, a_norm, a_w_out, b_w_in, b_sinks, b_w_out, f_w_up, f_conv, f_w_down, ple_w_proj, ple_w_gate, loss_target, m_norm_mix, m_norm_ffn, m_norm_ple, m_norm_final, m_a_w_in, m_a_conv, m_a_log, m_a_dt_bias, m_a_norm, m_a_w_out, m_b_w_in, m_b_sinks, m_b_w_out, m_f_w_up, m_f_conv, m_f_w_down, m_ple_w_proj, m_ple_w_gate, v_norm_mix, v_norm_ffn, v_norm_ple, v_norm_final, v_a_w_in, v_a_conv, v_a_log, v_a_dt_bias, v_a_norm, v_a_w_out, v_b_w_in, v_b_sinks, v_b_w_out, v_f_w_up, v_f_conv, v_f_w_down, v_ple_w_proj, v_ple_w_gate):
    w = dict(norm_mix=norm_mix, norm_ffn=norm_ffn, norm_ple=norm_ple, norm_final=norm_final, a_w_in=a_w_in, a_conv=a_conv,
             a_log=a_log, a_dt_bias=a_dt_bias, a_norm=a_norm, a_w_out=a_w_out, b_w_in=b_w_in, b_sinks=b_sinks, b_w_out=b_w_out,
             f_w_up=f_w_up, f_conv=f_conv, f_w_down=f_w_down, ple_w_proj=ple_w_proj, ple_w_gate=ple_w_gate)
    m = dict(norm_mix=m_norm_mix, norm_ffn=m_norm_ffn, norm_ple=m_norm_ple, norm_final=m_norm_final, a_w_in=m_a_w_in,
             a_conv=m_a_conv, a_log=m_a_log, a_dt_bias=m_a_dt_bias, a_norm=m_a_norm, a_w_out=m_a_w_out, b_w_in=m_b_w_in,
             b_sinks=m_b_sinks, b_w_out=m_b_w_out, f_w_up=m_f_w_up, f_conv=m_f_conv, f_w_down=m_f_w_down,
             ple_w_proj=m_ple_w_proj, ple_w_gate=m_ple_w_gate)
    v = dict(norm_mix=v_norm_mix, norm_ffn=v_norm_ffn, norm_ple=v_norm_ple, norm_final=v_norm_final, a_w_in=v_a_w_in,
             a_conv=v_a_conv, a_log=v_a_log, a_dt_bias=v_a_dt_bias, a_norm=v_a_norm, a_w_out=v_a_w_out, b_w_in=v_b_w_in,
             b_sinks=v_b_sinks, b_w_out=v_b_w_out, f_w_up=v_f_w_up, f_conv=v_f_conv, f_w_down=v_f_w_down,
             ple_w_proj=v_ple_w_proj, ple_w_gate=v_ple_w_gate)
    xc, yc, cc = _place()
    my_chip = 2 * xc + yc

    shard = {(n, i): w[n][i if n in LAYERED else 0].astype(BF16) for n, i in LAYER_UNITS[0] + LAYER_UNITS[1]}
    first = shard["a_w_in", 0]
    (ga,) = gather_units([(first, False)], name="gather_first")
    ga = lax.dynamic_update_index_in_dim(ga, first, my_chip, 0)
    a_in = jnp.concatenate([ga[j] for j in range(N_CHIPS)], axis=1)
    n_main = 4 * N_HEADS_A * HEAD_DIM_A
    conv_shapes = [w[n].shape for n in CONVS]
    convs = allgather8(_pack_rows([w[n] for n in CONVS], CONV_ROWS, F32), name="gather_convs")
    conv_parts = [_unpack(convs[2 * j].reshape(-1), conv_shapes) for j in range(N_CHIPS)]
    a_conv_full, f_conv_full = (jnp.concatenate([conv_parts[j][q] for j in range(N_CHIPS)], axis=2) for q in range(2))
    ready = {("a_w_in", 0): jnp.pad(a_in, ((0, 0), (0, n_main + LANES - a_in.shape[1]))), ("a_conv", 0): a_conv_full[0], ("f_conv", 0): f_conv_full[0], ("f_conv", 1): f_conv_full[1]}
    later = [[k for k in units if k != ("a_w_in", 0)] for units in LAYER_UNITS]
    pending, after = [], ga
    for layer, keys in enumerate(later):
        pending.append(gather_start([shard[k] for k in keys], after, name=f"gather_start{layer}"))
        after = pending[-1][4]
    sm = {n: w[n] for n in SMALL}
    sm["norm_mix"] = sm["norm_mix"] + after[:1, :1]

    def weight(name, layer, act):
        if (name, layer) not in ready:
            landed = gather_wait(pending[layer], act, name=f"gather_wait{layer}")
            for k, g in zip(later[layer], landed):
                g = lax.dynamic_update_index_in_dim(g, shard[k], my_chip, 0)
                ready[k] = g if k[0] in BY_CHIP else g.reshape(N_CHIPS * g.shape[1], g.shape[2])
        return ready[name, layer]

    pairs, scattered, started = {}, {}, []

    def on_grads(layer, part, gw):
        keys = [k for k in LAYER_UNITS[layer] if (k[0] in LAYERED) == (part == "ffn")]
        from_sib = swap_units([gw[n] for n, _ in keys], name=f"rs_swap_{part}{layer}")
        for (n, _), sib in zip(keys, from_sib):
            pairs[n, layer] = pair_add(gw[n], sib, name=f"rs_add_pair_{n}{layer}")
        started.append((keys, scatter_start([pairs[k] for k in keys], name=f"rs_scatter_start_{part}{layer}"), f"{part}{layer}"))
        return started[-1][1][4]

    loss, grad_x, gs = local_step(x[0], p[:, 0], loss_target[0], sm, weight, on_grads)

    grads, delta, new_m, new_v, g_unit = {}, {}, {}, {}, {}

    def finish(keys, tag):
        halves = [chips_add(pairs[k], scattered[k], name=f"rs_add_chips_{k[0]}{k[1]}") for k in keys]
        g_unit.update(zip(keys, zip(halves, join_units(halves, name=f"rs_join_{tag}"))))
        for n in BIG:
            mine = [(n, i) for i in range(DEPTH) if (n, i) in LAYER_UNITS[i]]
            if n not in delta and all(k in g_unit for k in mine):
                g_layers = [g_unit[k] for k in mine]
                shape3 = (len(g_layers), 2 * g_layers[0][0].shape[0], g_layers[0][0].shape[1])
                res = adamw_layers(w[n].reshape(shape3), g_layers, m[n].reshape(shape3), v[n].reshape(shape3), name=f"adamw_{n}")
                delta[n], new_m[n], new_v[n], grads[n] = (r.reshape(w[n].shape) for r in res)

    last_keys, last_pending, last_tag = started[-1]
    for keys, pend, tag in started[:-1]:
        scattered.update(zip(keys, scatter_wait(pend, last_pending[4], name=f"rs_scatter_wait_{tag}")))
    finish([k for keys, _, _ in started[:-1] for k in keys], "first")

    conv_grads = _pack_rows([gs[n] for n in CONVS], CONV_GRAD_ROWS, F32)
    small_sum = sum8(allgather8(jnp.concatenate([_pack_small(gs, loss), conv_grads]), name="gather_small"), name="sum_small")
    g_sm, loss_sum = _unpack_small(small_sum[:SMALL_ROWS], sm)

    scattered.update(zip(last_keys, scatter_wait(last_pending, small_sum, name=f"rs_scatter_wait_{last_tag}")))
    finish(last_keys, "last")

    for n, full in zip(CONVS, _unpack(small_sum[SMALL_ROWS:].reshape(-1), [gs[n].shape for n in CONVS])):
        g2 = _as2d(lax.dynamic_slice_in_dim(full, my_chip * w[n].shape[-1], w[n].shape[-1], axis=full.ndim - 1))
        d2, m2, v2 = adamw(_as2d(w[n]), g2, _as2d(m[n]), _as2d(v[n]), name=f"adamw_{n}")
        grads[n], delta[n], new_m[n], new_v[n] = (r.reshape(w[n].shape) for r in (g2, d2, m2, v2))
    pk = lambda d: _pack_small(d)
    d2, m2, v2 = adamw(pk(sm), pk(g_sm), pk({n: m[n] for n in SMALL}), pk({n: v[n] for n in SMALL}), name="adamw_small")
    for src, dst in ((d2, delta), (m2, new_m), (v2, new_v)):
        dst.update(_unpack_small(src, sm)[0])
    grads.update(g_sm)

    order = ["norm_mix", "norm_ffn", "norm_ple", "norm_final", "a_w_in", "a_conv", "a_log", "a_dt_bias", "a_norm", "a_w_out",
             "b_w_in", "b_sinks", "b_w_out", "f_w_up", "f_conv", "f_w_down", "ple_w_proj", "ple_w_gate"]
    return (loss_sum, grad_x[None], *[grads[n] for n in order], *[delta[n] for n in order],
            *[new_m[n] for n in order], *[new_v[n] for n in order])
```

```python
import functools
import math

import jax
import jax.numpy as jnp
from jax import lax
from jax.experimental import pallas as pl
from jax.experimental.pallas import tpu as pltpu

F32 = jnp.float32
BF16 = jnp.bfloat16
MESH = pl.DeviceIdType.MESH

D_MODEL = 1024
N_HEADS_A = 8
HEAD_DIM_A = 128
CONV_A = 4
N_HEADS_B = 16
N_KV_B = 4
HEAD_DIM_B = 64
WINDOW = 128
D_FF = 2816
FFN_CONV = 3
PLE_DIM = 256
EPS = 1e-6
DEPTH = 2

ADAM_LR = 0.001
ADAM_B1 = 0.9
ADAM_B2 = 0.999
ADAM_EPS = 1e-08
ADAM_WD = 0.01
ADAM_STEP = 10

LANES = 128
SUBLANES = 8
BF16_ROWS = 16
CHUNK = 128
VMEM_LIMIT = 56 * 1024 * 1024
NEG = -1e30
N_CHIPS = 4
N_DEV = 8
PACK_COLS = 1024


def _params(sem=None):
    return pltpu.CompilerParams(dimension_semantics=sem, vmem_limit_bytes=VMEM_LIMIT)


def _tile(dim, cap):
    if dim % LANES:
        return dim
    best = LANES
    for t in range(LANES, min(dim, cap) + 1, LANES):
        if dim % t == 0:
            best = t
    return best


def _dot(a, b, dims=(((1,), (0,)), ((), ())), precision=None):
    return lax.dot_general(a, b, dims, precision=precision, preferred_element_type=F32)


NN = (((1,), (0,)), ((), ()))
NT = (((1,), (1,)), ((), ()))
TN = (((0,), (0,)), ((), ()))


ROW_TM = 512
MM_TM_CAP = 1024
MM_TK_CAP_TOKENS = 2048


def mm(a, b, *, name, ta=False, tb=False, out_dtype=F32, add=None, norm_w=None, norm_grad=None, tm_cap=MM_TM_CAP, tn_cap=1408,
       tk_cap=1408, n=None, tn=None, tk=None, b_spec=None, o_spec=None, o_shape=None, into=None):
    m, k = (a.shape[1], a.shape[0]) if ta else a.shape
    if b_spec is None:
        n = b.shape[0] if tb else b.shape[1]
        assert (b.shape[1] if tb else b.shape[0]) == k, (a.shape, b.shape, ta, tb)
    tm, tn, tk = _tile(m, tm_cap), tn or _tile(n, tn_cap), tk or _tile(k, MM_TK_CAP_TOKENS if ta else tk_cap)
    assert n % tn == 0 and k % tk == 0, (n, tn, k, tk)
    nk = k // tk
    dims = (((0 if ta else 1,), (1 if tb else 0,)), ((), ()))
    has_add, has_norm, has_grad = add is not None, norm_w is not None, norm_grad is not None
    assert not (has_norm or has_grad) or (tn == n and o_spec is None), "the norm epilogues need whole rows"
    n_in = 2 + has_add + has_norm + 3 * has_grad + (into is not None)

    def body(*refs):
        a_ref, b_ref = refs[0], refs[1]
        add_ref = refs[2] if has_add else None
        o_ref = refs[n_in]
        part = _dot(a_ref[...].astype(BF16), b_ref[...].astype(BF16), dims)
        first = pl.program_id(0) == 0

        def finish(r):
            if has_add:
                r = r + add_ref[...].astype(F32)
            if has_grad:
                h_ref, w_ref, prev_ref = refs[2 + has_add:5 + has_add]
                _, vjp = jax.vjp(_f_norm, h_ref[...], w_ref[...])
                r, dw = vjp(r)
                r = r + prev_ref[...]

                @pl.when(first)
                def _():
                    refs[n_in + 1][...] = dw

                @pl.when(jnp.logical_not(first))
                def _():
                    refs[n_in + 1][...] += dw
            o_ref[...] = r.astype(o_ref.dtype)
            if has_norm:
                refs[n_in + 1][...] = _f_norm(r, refs[2 + has_add][...]).astype(BF16)

        if nk == 1:
            finish(part)
            return
        acc = refs[-1]
        kk = pl.program_id(2)

        @pl.when(kk == 0)
        def _():
            acc[...] = part

        @pl.when(kk > 0)
        def _():
            acc[...] += part

        @pl.when(kk == nk - 1)
        def _():
            finish(acc[...])

    a_spec = pl.BlockSpec((tk, tm), lambda i, j, kk: (kk, i)) if ta else pl.BlockSpec((tm, tk), lambda i, j, kk: (i, kk))
    if b_spec is None:
        b_spec = pl.BlockSpec((tn, tk), lambda i, j, kk: (j, kk)) if tb else pl.BlockSpec((tk, tn), lambda i, j, kk: (kk, j))
    plain_o = pl.BlockSpec((tm, tn), lambda i, j, kk: (i, j))
    if o_spec is None:
        o_spec, o_shape = plain_o, (m, n)
    in_specs = [a_spec, b_spec] + ([plain_o] if has_add else [])
    args = (a, b) + ((add,) if has_add else ())
    out_specs, out_shapes = o_spec, jax.ShapeDtypeStruct(tuple(o_shape), out_dtype)
    one_row = pl.BlockSpec((1, n), lambda i, j, kk: (0, 0))
    if has_norm:
        in_specs.append(one_row)
        args += (norm_w,)
        out_specs, out_shapes = [o_spec, plain_o], [out_shapes, jax.ShapeDtypeStruct((m, n), BF16)]
    if has_grad:
        assert not has_norm
        in_specs += [plain_o, one_row, plain_o]
        args += tuple(norm_grad)
        out_specs, out_shapes = [o_spec, one_row], [out_shapes, jax.ShapeDtypeStruct((1, n), F32)]
    aliases = {}
    if into is not None:
        assert into.shape == tuple(o_shape) and into.dtype == out_dtype, (into.shape, o_shape)
        in_specs.append(pl.BlockSpec(memory_space=pl.ANY))
        args += (into,)
        aliases = {n_in - 1: 0}
    return pl.pallas_call(
        body, grid=(m // tm, n // tn, nk), in_specs=in_specs, out_specs=out_specs,
        out_shape=out_shapes, name=name, input_output_aliases=aliases,
        scratch_shapes=[pltpu.VMEM((tm, tn), F32)] if nk > 1 else [],
        compiler_params=_params(("arbitrary" if has_grad else "parallel", "parallel", "arbitrary")),
    )(*args)


def _row_spec(tm, cw, coff):
    return pl.BlockSpec((tm, cw), lambda i, j: (i, j + coff))


def _full_spec(shape):
    return pl.BlockSpec(shape, lambda i, j: (0,) * len(shape))


def tile_map(fn, rows, params, outs, *, tm, ncol, name):
    t = rows[0][0].shape[0]
    nin = len(rows) + len(params)

    def body(*refs):
        res = fn(*[r[...] for r in refs[:nin]])
        res = res if isinstance(res, (tuple, list)) else (res,)
        for o_ref, r in zip(refs[nin:], res):
            o_ref[...] = r.astype(o_ref.dtype)

    in_specs = [_row_spec(tm, cw, coff) for (_, cw, coff) in rows] + [_full_spec(p.shape) for p in params]
    res = pl.pallas_call(
        body, grid=(t // tm, ncol), in_specs=in_specs,
        out_specs=[_row_spec(tm, cw, 0) for (cw, _) in outs],
        out_shape=[jax.ShapeDtypeStruct((t, cw * ncol), dt) for (cw, dt) in outs], name=name,
        compiler_params=_params(("parallel", "parallel")),
    )(*[r[0] for r in rows], *params)
    return res


def tile_vjp(fn, rows, params, cts, *, n_diff, tm, ncol, name, grad_dtypes=None):
    t = rows[0][0].shape[0]
    nr, npar, nct = len(rows), len(params), len(cts)

    def body(*refs):
        vals = [r[...] for r in refs[:nr + npar + nct]]
        diff, rest, pars = vals[:n_diff], vals[n_diff:nr], vals[nr:nr + npar]
        ctv = vals[nr + npar:nr + npar + nct]
        outs_ref = refs[nr + npar + nct:]

        def f(*a):
            res = fn(*a[:n_diff], *rest, *a[n_diff:])
            return tuple(res) if isinstance(res, (tuple, list)) else (res,)

        primal, vjp = jax.vjp(f, *[d.astype(F32) for d in diff], *pars)
        grads = vjp(tuple(c.astype(o.dtype) for c, o in zip(ctv, primal)))
        for q in range(n_diff):
            outs_ref[q][...] = grads[q].astype(outs_ref[q].dtype)
        first = (pl.program_id(0) == 0) & (pl.program_id(1) == 0)
        for q in range(npar):
            o_ref, g = outs_ref[n_diff + q], grads[n_diff + q]

            @pl.when(first)
            def _(o_ref=o_ref, g=g):
                o_ref[...] = g

            @pl.when(jnp.logical_not(first))
            def _(o_ref=o_ref, g=g):
                o_ref[...] += g

    in_specs = [_row_spec(tm, cw, coff) for (_, cw, coff) in rows] + [_full_spec(p.shape) for p in params]
    in_specs += [_row_spec(tm, cw, coff) for (_, cw, coff) in cts]
    args = [r[0] for r in rows] + list(params) + [c[0] for c in cts]
    out_specs = [_row_spec(tm, rows[q][1], 0) for q in range(n_diff)] + [_full_spec(p.shape) for p in params]
    grad_dtypes = grad_dtypes or [F32] * n_diff
    out_shape = [jax.ShapeDtypeStruct((t, rows[q][1] * ncol), grad_dtypes[q]) for q in range(n_diff)]
    out_shape += [jax.ShapeDtypeStruct(p.shape, F32) for p in params]
    res = pl.pallas_call(
        body, grid=(t // tm, ncol), in_specs=in_specs, out_specs=out_specs, out_shape=out_shape, name=name,
        compiler_params=_params(("arbitrary", "arbitrary")),
    )(*args)
    return res[:n_diff], res[n_diff:]


def _silu(x):
    return x * jax.nn.sigmoid(x)


def _f_norm(h, w):
    return h * lax.rsqrt(jnp.mean(h * h, axis=-1, keepdims=True) + EPS) * w


def _f_gnorm(o, z, w):
    return _f_norm(o, w) * _silu(z)


def _f_act(gate, val):
    return _silu(gate) * val


def _f_ple(gl, pe):
    return jax.nn.sigmoid(gl) * pe


def _f_betag(pt, alog, dtb):
    lane = lax.broadcasted_iota(jnp.int32, (1, LANES), 1)
    z = pt + dtb
    softplus = jnp.maximum(z, 0.0) + jnp.log(1.0 + jnp.exp(-jnp.abs(z)))
    g = -jnp.exp(alog) * softplus
    return jnp.where(lane < N_HEADS_A, jax.nn.sigmoid(pt), jnp.where(lane < 2 * N_HEADS_A, g, 0.0))


CONV_TM = 1024
CONV_CW = 1024


def _shift_down(x, prev, s, row):
    rp = jnp.tile(pltpu.roll(prev, s, 0), (x.shape[0] // SUBLANES, 1))
    return jnp.where(row < s, rp, pltpu.roll(x, s, 0))


def _shift_up(x, nxt, s, row):
    tm = x.shape[0]
    rn = jnp.tile(pltpu.roll(nxt, SUBLANES - s, 0), (tm // SUBLANES, 1))
    return jnp.where(row >= tm - s, rn, pltpu.roll(x, tm - s, 0))


def _conv_taps(x, prev, w_ref, cols, row):
    k = w_ref.shape[0]
    y = x * w_ref[pl.ds(k - 1, 1), cols]
    for s in range(1, k):
        y = y + _shift_down(x, prev, s, row) * w_ref[pl.ds(k - 1 - s, 1), cols]
    return y


def _lane_chunks(cw):
    return [slice(cb * LANES, (cb + 1) * LANES) for cb in range(cw // LANES)]


def conv_fwd(x, w, *, name):
    t = x.shape[0]
    k, c = w.shape
    tm, cw = min(CONV_TM, t), CONV_CW
    nb8 = tm // SUBLANES

    def body(x_ref, p_ref, w_ref, o_ref):
        first = pl.program_id(1) == 0
        row = lax.broadcasted_iota(jnp.int32, (tm, LANES), 0)
        for cols in _lane_chunks(cw):
            o_ref[:, cols] = _conv_taps(x_ref[:, cols], jnp.where(first, 0.0, p_ref[:, cols]), w_ref, cols, row)

    return pl.pallas_call(
        body, grid=(c // cw, t // tm),
        in_specs=[pl.BlockSpec((tm, cw), lambda j, i: (i, j)),
                  pl.BlockSpec((SUBLANES, cw), lambda j, i: (jnp.maximum(i * nb8 - 1, 0), j)),
                  pl.BlockSpec((k, cw), lambda j, i: (0, j))],
        out_specs=pl.BlockSpec((tm, cw), lambda j, i: (i, j)),
        out_shape=jax.ShapeDtypeStruct((t, c), F32), name=name,
        compiler_params=_params(("parallel", "parallel")),
    )(x, x, w)


def conv_bwd(dy, x, w, *, name):
    t = x.shape[0]
    k, c = w.shape
    tm, cw = min(CONV_TM, t), CONV_CW
    nb8 = tm // SUBLANES
    ni = t // tm

    def body(dy_ref, dn_ref, x_ref, p_ref, w_ref, dx_ref, dw_ref):
        i = pl.program_id(1)
        first, last = i == 0, i == ni - 1
        row = lax.broadcasted_iota(jnp.int32, (tm, LANES), 0)
        for cols in _lane_chunks(cw):
            dyv, xv = dy_ref[:, cols], x_ref[:, cols]
            nxt = jnp.where(last, 0.0, dn_ref[:, cols])
            prev = jnp.where(first, 0.0, p_ref[:, cols])
            dx = dyv * w_ref[pl.ds(k - 1, 1), cols]
            dws = [jnp.sum(dyv * xv, axis=0, keepdims=True)]
            for s in range(1, k):
                dx = dx + _shift_up(dyv, nxt, s, row) * w_ref[pl.ds(k - 1 - s, 1), cols]
                dws.append(jnp.sum(dyv * _shift_down(xv, prev, s, row), axis=0, keepdims=True))
            dx_ref[:, cols] = dx.astype(dx_ref.dtype)
            for s in range(k):
                @pl.when(first)
                def _(s=s, dws=dws, cols=cols):
                    dw_ref[pl.ds(k - 1 - s, 1), cols] = dws[s]

                @pl.when(jnp.logical_not(first))
                def _(s=s, dws=dws, cols=cols):
                    dw_ref[pl.ds(k - 1 - s, 1), cols] += dws[s]

    return pl.pallas_call(
        body, grid=(c // cw, ni),
        in_specs=[pl.BlockSpec((tm, cw), lambda j, i: (i, j)),
                  pl.BlockSpec((SUBLANES, cw), lambda j, i: (jnp.minimum((i + 1) * nb8, t // SUBLANES - 1), j)),
                  pl.BlockSpec((tm, cw), lambda j, i: (i, j)),
                  pl.BlockSpec((SUBLANES, cw), lambda j, i: (jnp.maximum(i * nb8 - 1, 0), j)),
                  pl.BlockSpec((k, cw), lambda j, i: (0, j))],
        out_specs=[pl.BlockSpec((tm, cw), lambda j, i: (i, j)), pl.BlockSpec((k, cw), lambda j, i: (0, j))],
        out_shape=[jax.ShapeDtypeStruct((t, c), BF16), jax.ShapeDtypeStruct((k, c), F32)], name=name,
        compiler_params=_params(("parallel", "arbitrary")),
    )(dy, dy, x, x, w)


FFN_TM = 512
FFN_CW = D_FF // 2


def _ffn_specs(t, tm, cw, k):
    ncol = D_FF // cw
    cur = lambda off: pl.BlockSpec((tm, cw), lambda j, i: (i, j + off))
    prev = lambda off, hr: pl.BlockSpec((hr, cw), lambda j, i: (jnp.maximum(i * (tm // hr) - 1, 0), j + off))
    nxt = lambda off, hr: pl.BlockSpec((hr, cw), lambda j, i: (jnp.minimum((i + 1) * (tm // hr), t // hr - 1), j + off))
    taps = lambda off: pl.BlockSpec((k, cw), lambda j, i: (0, j + off))
    return cur, prev, nxt, taps, ncol


def _rows_before(ref, cols, first):
    return jnp.where(first, 0.0, ref[ref.shape[0] - SUBLANES:, cols].astype(F32))


def conv_act_fwd(u, w, *, name):
    t, k = u.shape[0], w.shape[0]
    tm, cw = min(FFN_TM, t), FFN_CW
    cur, prev, _, taps, ncol = _ffn_specs(t, tm, cw, k)

    def body(ug_ref, pg_ref, uv_ref, pv_ref, wg_ref, wv_ref, o_ref):
        first = pl.program_id(1) == 0
        row = lax.broadcasted_iota(jnp.int32, (tm, LANES), 0)
        for cb in range(cw // LANES):
            cols = slice(cb * LANES, (cb + 1) * LANES)
            cg = _conv_taps(ug_ref[:, cols].astype(F32), _rows_before(pg_ref, cols, first), wg_ref, cols, row)
            cv = _conv_taps(uv_ref[:, cols].astype(F32), _rows_before(pv_ref, cols, first), wv_ref, cols, row)
            o_ref[:, cols] = _f_act(cg, cv).astype(o_ref.dtype)

    return pl.pallas_call(
        body, grid=(ncol, t // tm),
        in_specs=[cur(0), prev(0, BF16_ROWS), cur(ncol), prev(ncol, BF16_ROWS), taps(0), taps(ncol)],
        out_specs=cur(0), out_shape=jax.ShapeDtypeStruct((t, D_FF), BF16), name=name,
        compiler_params=_params(("parallel", "parallel")),
    )(u, u, u, u, w, w)


def conv_act_bwd(u, dact, w, *, name):
    t, k = u.shape[0], w.shape[0]
    tm, cw = min(FFN_TM, t), FFN_CW
    cur, prev, nxt, taps, ncol = _ffn_specs(t, tm, cw, k)
    ni = t // tm

    def body(ug_ref, pg_ref, ng_ref, uv_ref, pv_ref, nv_ref, d_ref, dn_ref, wg_ref, wv_ref, dg_ref, dv_ref, dwg_ref, dwv_ref):
        i = pl.program_id(1)
        first, last = i == 0, i == ni - 1
        row = lax.broadcasted_iota(jnp.int32, (tm, LANES), 0)
        row8 = lax.broadcasted_iota(jnp.int32, (SUBLANES, LANES), 0)
        for cb in range(cw // LANES):
            cols = slice(cb * LANES, (cb + 1) * LANES)
            ug, uv = ug_ref[:, cols].astype(F32), uv_ref[:, cols].astype(F32)
            pg, pv = _rows_before(pg_ref, cols, first), _rows_before(pv_ref, cols, first)
            sg = [ug] + [_shift_down(ug, pg, s, row) for s in range(1, k)]
            sv = [uv] + [_shift_down(uv, pv, s, row) for s in range(1, k)]
            taps = lambda xs, w_ref: sum(xs[s] * w_ref[pl.ds(k - 1 - s, 1), cols] for s in range(k))
            _, vjp = jax.vjp(_f_act, taps(sg, wg_ref), taps(sv, wv_ref))
            dcg, dcv = vjp(d_ref[:, cols])
            after = lambda ref: ref[:SUBLANES, cols].astype(F32)
            _, vjp_n = jax.vjp(_f_act, _conv_taps(after(ng_ref), ug[tm - SUBLANES:], wg_ref, cols, row8),
                               _conv_taps(after(nv_ref), uv[tm - SUBLANES:], wv_ref, cols, row8))
            dcgn, dcvn = vjp_n(jnp.where(last, 0.0, dn_ref[:, cols]))
            for dc, dcn, xs, w_ref, dx_ref, dw_ref in ((dcg, dcgn, sg, wg_ref, dg_ref, dwg_ref),
                                                       (dcv, dcvn, sv, wv_ref, dv_ref, dwv_ref)):
                dx = dc * w_ref[pl.ds(k - 1, 1), cols]
                dws = [jnp.sum(dc * xs[0], axis=0, keepdims=True)]
                for s in range(1, k):
                    dx = dx + _shift_up(dc, dcn, s, row) * w_ref[pl.ds(k - 1 - s, 1), cols]
                    dws.append(jnp.sum(dc * xs[s], axis=0, keepdims=True))
                dx_ref[:, cols] = dx.astype(dx_ref.dtype)
                for s in range(k):
                    @pl.when(first)
                    def _(s=s, dw_ref=dw_ref, dws=dws):
                        dw_ref[pl.ds(k - 1 - s, 1), cols] = dws[s]

                    @pl.when(jnp.logical_not(first))
                    def _(s=s, dw_ref=dw_ref, dws=dws):
                        dw_ref[pl.ds(k - 1 - s, 1), cols] += dws[s]

    half = jax.ShapeDtypeStruct((t, D_FF), BF16)
    dwh = jax.ShapeDtypeStruct((k, D_FF), F32)
    return pl.pallas_call(
        body, grid=(ncol, ni),
        in_specs=[cur(0), prev(0, BF16_ROWS), nxt(0, BF16_ROWS), cur(ncol), prev(ncol, BF16_ROWS), nxt(ncol, BF16_ROWS),
                  cur(0), nxt(0, SUBLANES), taps(0), taps(ncol)],
        out_specs=[cur(0), cur(0), taps(0), taps(0)], out_shape=[half, half, dwh, dwh], name=name,
        compiler_params=_params(("parallel", "arbitrary")),
    )(u, u, u, u, u, u, dact, dact, w, w)


def _each(f, *lists):
    return [f(*a) for a in zip(*lists)]


@jax.custom_vjp
def _inv_unit_lower(lms):
    return _inv_blocks(lms)


def _inv_blocks(lms):
    c = lms[0].shape[0]
    ri = lax.broadcasted_iota(jnp.int32, (c, c), 0)
    ci = lax.broadcasted_iota(jnp.int32, (c, c), 1)
    eye = (ri == ci).astype(F32)
    dms = _each(lambda lm: eye - jnp.where((ri >> 1) == (ci >> 1), lm, 0.0), lms)
    for lv in range(1, int(math.log2(c))):
        below = ((ri >> (lv + 1)) == (ci >> (lv + 1))) & ((ri >> lv) != (ci >> lv))
        dbs = _each(lambda dm: dm.astype(BF16), dms)
        ods = _each(lambda lm, db: _dot(jnp.where(below, lm, 0.0).astype(BF16), db).astype(BF16), lms, dbs)
        dms = _each(lambda dm, db, od: dm - _dot(db, od), dms, dbs, ods)
    return dms


def _inv_fwd(lms):
    tms = _inv_blocks(lms)
    return tms, tms


def _inv_bwd(tms, dts):
    tbs = _each(lambda tm: tm.astype(BF16), tms)
    mid = _each(lambda tb, dt: _dot(tb, dt.astype(BF16), TN).astype(BF16), tbs, dts)
    return (_each(lambda m, tb: -_dot(m, tb, NT), mid, tbs),)


_inv_unit_lower.defvjp(_inv_fwd, _inv_bwd)


@jax.custom_vjp
def _inv_known(lms, tms):
    return tms


_inv_known.defvjp(lambda lms, tms: (tms, tms), lambda tms, dts: _inv_bwd(tms, dts) + (_each(jnp.zeros_like, tms),))


def _l2n(x):
    return x * lax.rsqrt(jnp.sum(x * x, axis=-1, keepdims=True) + EPS)


def _prep_fn(cqs, cks, cvs, bg, sel_b, sel_g, tms=None):
    c = cqs[0].shape[0]
    ri = lax.broadcasted_iota(jnp.int32, (c, c), 0)
    ci = lax.broadcasted_iota(jnp.int32, (c, c), 1)
    eye = (ri == ci).astype(F32)
    incl, strict = ci <= ri, ci < ri
    last = lax.broadcasted_iota(jnp.int32, (c, 1), 0) == c - 1
    to_row = lambda col: jnp.sum(col * eye, axis=0, keepdims=True)
    qs = _each(lambda a: _l2n(_silu(a)) * (HEAD_DIM_A ** -0.5), cqs)
    ks = _each(lambda a: _l2n(_silu(a)), cks)
    vbs = _each(lambda a: _silu(a).astype(BF16), cvs)
    betas = _each(lambda m: jnp.sum(bg * m, axis=1, keepdims=True), sel_b)
    gs = _each(lambda m: jnp.sum(bg * m, axis=1, keepdims=True), sel_g)
    gcss = _each(lambda g: jnp.sum(jnp.where(incl, to_row(g), 0.0), axis=1, keepdims=True), gs)
    gtots = _each(lambda gcs: jnp.sum(jnp.where(last, gcs, 0.0), axis=0, keepdims=True), gcss)
    decays = _each(lambda gcs: jnp.exp(jnp.where(incl, gcs - to_row(gcs), NEG)), gcss)
    kbs = _each(lambda k: k.astype(BF16), ks)
    lms = _each(lambda beta, kb, dec: jnp.where(strict, beta * _dot(kb, kb, NT) * dec, 0.0), betas, kbs, decays)
    tms = _inv_unit_lower(lms) if tms is None else _inv_known(lms, tms)
    ams = _each(lambda tm, beta: (tm * to_row(beta)).astype(BF16), tms, betas)
    gams = _each(jnp.exp, gcss)
    u0s = _each(_dot, ams, vbs)
    wks = _each(lambda am, gam, k: _dot(am, (gam * k).astype(BF16)), ams, gams, ks)
    qks = _each(lambda q, kb, dec: _dot(q.astype(BF16), kb, NT) * dec, qs, kbs, decays)
    qds = _each(lambda q, gam: q * gam, qs, gams)
    kds = _each(lambda k, gtot, gcs: k * jnp.exp(gtot - gcs), ks, gtots, gcss)
    gls = _each(lambda gtot: jnp.exp(gtot) * jnp.ones((SUBLANES, LANES), F32), gtots)
    return u0s, wks, qds, kds, qks, gls, tms


def _head_masks(h):
    lane = lax.broadcasted_iota(jnp.int32, (1, LANES), 1)
    return (lane == h).astype(F32), (lane == h + N_HEADS_A).astype(F32)


def _hsl(j):
    return slice(j * HEAD_DIM_A, (j + 1) * HEAD_DIM_A)


def gnorm_fwd(o, zsrc, w, *, name):
    t, width = o.shape
    tm = min(ROW_TM, t)
    zoff = zsrc.shape[1] // width - 1

    def body(o_ref, z_ref, w_ref, out_ref):
        for h in range(N_HEADS_A):
            out_ref[:, _hsl(h)] = _f_gnorm(o_ref[:, _hsl(h)], z_ref[:, _hsl(h)], w_ref[...]).astype(out_ref.dtype)

    rows = pl.BlockSpec((tm, width), lambda i: (i, 0))
    return pl.pallas_call(
        body, grid=(t // tm,),
        in_specs=[rows, pl.BlockSpec((tm, width), lambda i: (i, zoff)), pl.BlockSpec(w.shape, lambda i: (0, 0))],
        out_specs=rows, out_shape=jax.ShapeDtypeStruct((t, width), BF16), name=name, compiler_params=_params(("parallel",)),
    )(o, zsrc, w)


def gnorm_bwd(o, zsrc, w, don, *, name):
    t, width = o.shape
    tm = min(ROW_TM, t)
    zoff = zsrc.shape[1] // width - 1

    def body(o_ref, z_ref, w_ref, d_ref, do_ref, dz_ref, dw_ref):
        dw = jnp.zeros(w.shape, F32)
        for h in range(N_HEADS_A):
            _, vjp = jax.vjp(_f_gnorm, o_ref[:, _hsl(h)], z_ref[:, _hsl(h)], w_ref[...])
            do, dz, dwh = vjp(d_ref[:, _hsl(h)])
            do_ref[:, _hsl(h)] = do.astype(do_ref.dtype)
            dz_ref[:, _hsl(h)] = dz.astype(dz_ref.dtype)
            dw = dw + dwh
        first = pl.program_id(0) == 0

        @pl.when(first)
        def _():
            dw_ref[...] = dw

        @pl.when(jnp.logical_not(first))
        def _():
            dw_ref[...] += dw

    rows = pl.BlockSpec((tm, width), lambda i: (i, 0))
    wspec = pl.BlockSpec(w.shape, lambda i: (0, 0))
    return pl.pallas_call(
        body, grid=(t // tm,),
        in_specs=[rows, pl.BlockSpec((tm, width), lambda i: (i, zoff)), wspec, rows],
        out_specs=[rows, rows, wspec],
        out_shape=[jax.ShapeDtypeStruct((t, width), BF16)] * 2 + [jax.ShapeDtypeStruct(w.shape, F32)], name=name,
        compiler_params=_params(("arbitrary",)),
    )(o, zsrc, w, don)


def delta_prep(cqkv, bg, *, name):
    t = cqkv.shape[0]
    nh, hd, n = N_HEADS_A, HEAD_DIM_A, t // CHUNK

    def body(cq_ref, ck_ref, cv_ref, bg_ref, u0_ref, wk_ref, qd_ref, kd_ref, qk_ref, tm_ref, gl_ref):
        heads = range(nh)
        masks = [_head_masks(j) for j in heads]
        res = _prep_fn([cq_ref[:, _hsl(j)] for j in heads], [ck_ref[:, _hsl(j)] for j in heads],
                       [cv_ref[:, _hsl(j)] for j in heads], bg_ref[...], [m[0] for m in masks], [m[1] for m in masks])
        for o_ref, rs in zip((u0_ref, wk_ref, qd_ref, kd_ref, qk_ref, tm_ref), res[:5] + (res[6],)):
            for j in heads:
                o_ref[:, _hsl(j)] = rs[j].astype(o_ref.dtype)
        for j in heads:
            gl_ref[j * SUBLANES:(j + 1) * SUBLANES, :] = res[5][j]

    blk = lambda off: pl.BlockSpec((CHUNK, nh * hd), lambda i: (i, off))
    res = pl.pallas_call(
        body, grid=(n,),
        in_specs=[blk(0), blk(1), blk(2), pl.BlockSpec((CHUNK, LANES), lambda i: (i, 0))],
        out_specs=[blk(0)] * 6 + [pl.BlockSpec((nh * SUBLANES, LANES), lambda i: (i, 0))],
        out_shape=[jax.ShapeDtypeStruct((t, nh * hd), dt) for dt in (F32, BF16, BF16, BF16, BF16, F32)]
        + [jax.ShapeDtypeStruct((n * nh * SUBLANES, LANES), F32)],
        name=name, compiler_params=_params(("parallel",)),
    )(cqkv, cqkv, cqkv, bg)
    return [*res[:5], res[6]], res[5]


def delta_prep_bwd(cqkv, bg, tms, cts, *, name):
    t = cqkv.shape[0]
    nh, hd, n = N_HEADS_A, HEAD_DIM_A, t // CHUNK

    def body(cq_ref, ck_ref, cv_ref, bg_ref, tm_ref, c0, c1, c2, c3, c4, c5, dc_ref, dbg_ref):
        heads = range(nh)
        masks = [_head_masks(j) for j in heads]
        known = [tm_ref[:, _hsl(j)] for j in heads]
        _, vjp = jax.vjp(lambda a, b, c, d: _prep_fn(a, b, c, d, [m[0] for m in masks], [m[1] for m in masks], known)[:6],
                         [cq_ref[:, _hsl(j)] for j in heads], [ck_ref[:, _hsl(j)] for j in heads],
                         [cv_ref[:, _hsl(j)] for j in heads], bg_ref[...])
        cts = tuple([c[:, _hsl(j)] for j in heads] for c in (c0, c1, c2, c3, c4))
        dqs, dks, dvs, dbg = vjp(cts + ([c5[j * SUBLANES:(j + 1) * SUBLANES, :] for j in heads],))
        for part, ds in enumerate((dqs, dks, dvs)):
            for j in heads:
                dc_ref[:, _hsl(part * nh + j)] = ds[j]
        dbg_ref[...] = dbg

    blk = lambda off: pl.BlockSpec((CHUNK, nh * hd), lambda i: (i, off))
    gl_spec = pl.BlockSpec((nh * SUBLANES, LANES), lambda i: (i, 0))
    bg_spec = pl.BlockSpec((CHUNK, LANES), lambda i: (i, 0))
    return pl.pallas_call(
        body, grid=(n,),
        in_specs=[blk(0), blk(1), blk(2), bg_spec] + [blk(0)] * 6 + [gl_spec],
        out_specs=[pl.BlockSpec((CHUNK, 3 * nh * hd), lambda i: (i, 0)), bg_spec],
        out_shape=[jax.ShapeDtypeStruct((t, 3 * nh * hd), F32), jax.ShapeDtypeStruct((t, LANES), F32)],
        name=name, compiler_params=_params(("parallel",)),
    )(cqkv, cqkv, cqkv, bg, tms, *cts)


def delta_scan(u0, wk, qd, kd, qk, gl, *, name):
    t = u0.shape[0]
    nh, hd, n = N_HEADS_A, HEAD_DIM_A, t // CHUNK

    def body(u0_ref, wk_ref, qd_ref, kd_ref, qk_ref, gl_ref, o_ref, sin_ref, s_ref):
        @pl.when(pl.program_id(0) == 0)
        def _():
            s_ref[...] = jnp.zeros_like(s_ref)

        heads = list(range(nh))
        cols = lambda ref: [ref[:, _hsl(h)].astype(BF16) for h in heads]
        ss = [s_ref[h] for h in heads]
        for h in heads:
            sin_ref[h] = ss[h]
        sbs = _each(lambda s: s.astype(BF16), ss)
        ubs = _each(lambda h, wkb, sb: (u0_ref[:, _hsl(h)] - _dot(wkb, sb)).astype(BF16), heads, cols(wk_ref), sbs)
        os_ = _each(lambda qdb, sb, qkb, ub: _dot(qdb, sb) + _dot(qkb, ub), cols(qd_ref), sbs, cols(qk_ref), ubs)
        sn = _each(lambda h, s, kdb, ub: gl_ref[pl.ds(h * SUBLANES, 1), :] * s + _dot(kdb, ub, TN), heads, ss, cols(kd_ref), ubs)
        for h in heads:
            o_ref[:, _hsl(h)] = os_[h]
            s_ref[h] = sn[h]

    blk = pl.BlockSpec((CHUNK, nh * hd), lambda i: (i, 0))
    return pl.pallas_call(
        body, grid=(n,),
        in_specs=[blk] * 5 + [pl.BlockSpec((nh * SUBLANES, LANES), lambda i: (i, 0))],
        out_specs=[blk, pl.BlockSpec((None, nh, hd, hd), lambda i: (i, 0, 0, 0))],
        out_shape=[jax.ShapeDtypeStruct((t, nh * hd), F32), jax.ShapeDtypeStruct((n, nh, hd, hd), F32)],
        scratch_shapes=[pltpu.VMEM((nh, hd, hd), F32)], name=name,
        compiler_params=_params(("arbitrary",)),
    )(u0, wk, qd, kd, qk, gl)


def delta_scan_bwd(do, u0, wk, qd, kd, qk, gl, s_in, *, name):
    t = u0.shape[0]
    nh, hd, n = N_HEADS_A, HEAD_DIM_A, t // CHUNK

    def body(do_ref, u0_ref, wk_ref, qd_ref, kd_ref, qk_ref, gl_ref, sin_ref,
             du0_ref, dwk_ref, dqd_ref, dkd_ref, dqk_ref, dgl_ref, ds_ref):
        @pl.when(pl.program_id(0) == 0)
        def _():
            ds_ref[...] = jnp.zeros_like(ds_ref)

        corner = (lax.broadcasted_iota(jnp.int32, (SUBLANES, LANES), 0) == 0) & (lax.broadcasted_iota(jnp.int32, (SUBLANES, LANES), 1) == 0)
        heads = list(range(nh))
        cols = lambda ref: [ref[:, _hsl(h)].astype(BF16) for h in heads]
        ss, dss = [sin_ref[h] for h in heads], [ds_ref[h] for h in heads]
        sbs, dsbs = _each(lambda s: s.astype(BF16), ss), _each(lambda d: d.astype(BF16), dss)
        dobs, wkbs, qdbs, kdbs, qkbs = cols(do_ref), cols(wk_ref), cols(qd_ref), cols(kd_ref), cols(qk_ref)
        ubs = _each(lambda h, wkb, sb: (u0_ref[:, _hsl(h)] - _dot(wkb, sb)).astype(BF16), heads, wkbs, sbs)
        dus = _each(lambda qkb, dob, kdb, dsb: _dot(qkb, dob, TN) + _dot(kdb, dsb), qkbs, dobs, kdbs, dsbs)
        dubs = _each(lambda du: du.astype(BF16), dus)
        dwks = _each(lambda dub, sb: -_dot(dub, sb, NT), dubs, sbs)
        dqds = _each(lambda dob, sb: _dot(dob, sb, NT), dobs, sbs)
        dkds = _each(lambda ub, dsb: _dot(ub, dsb, NT), ubs, dsbs)
        dqks = _each(lambda dob, ub: _dot(dob, ub, NT), dobs, ubs)
        dgls = _each(lambda s, d: jnp.sum(jnp.sum(s * d, axis=1, keepdims=True), axis=0, keepdims=True), ss, dss)
        dsn = _each(lambda h, d, qdb, dob, wkb, dub: gl_ref[pl.ds(h * SUBLANES, 1), :] * d + _dot(qdb, dob, TN) - _dot(wkb, dub, TN),
                    heads, dss, qdbs, dobs, wkbs, dubs)
        for h in heads:
            du0_ref[:, _hsl(h)] = dus[h]
            dwk_ref[:, _hsl(h)] = dwks[h]
            dqd_ref[:, _hsl(h)] = dqds[h]
            dkd_ref[:, _hsl(h)] = dkds[h]
            dqk_ref[:, _hsl(h)] = dqks[h]
            dgl_ref[h * SUBLANES:(h + 1) * SUBLANES, :] = jnp.where(corner, dgls[h], 0.0)
            ds_ref[h] = dsn[h]

    blk = pl.BlockSpec((CHUNK, nh * hd), lambda i: (n - 1 - i, 0))
    gl_spec = pl.BlockSpec((nh * SUBLANES, LANES), lambda i: (n - 1 - i, 0))
    return pl.pallas_call(
        body, grid=(n,),
        in_specs=[blk] * 6 + [gl_spec, pl.BlockSpec((None, nh, hd, hd), lambda i: (n - 1 - i, 0, 0, 0))],
        out_specs=[blk] * 5 + [gl_spec],
        out_shape=[jax.ShapeDtypeStruct((t, nh * hd), F32)] * 5 + [jax.ShapeDtypeStruct((n * nh * SUBLANES, LANES), F32)],
        scratch_shapes=[pltpu.VMEM((nh, hd, hd), F32)], name=name,
        compiler_params=_params(("arbitrary",)),
    )(do, u0, wk, qd, kd, qk, gl, s_in)


N_PAIRS = N_HEADS_B // 2
PAIRS_PER_KV = N_PAIRS // N_KV_B


def _psl(j):
    return slice(j * LANES, (j + 1) * LANES)


KV_STEP = 4


def _att_fn(qps, kcs, kps, vcs, vps, sinks, kv0, first):
    w = WINDOW
    lane = lax.broadcasted_iota(jnp.int32, (1, LANES), 1)
    lo = (lane < HEAD_DIM_B).astype(F32)
    qi = lax.broadcasted_iota(jnp.int32, (w, w), 0)
    kj = lax.broadcasted_iota(jnp.int32, (w, w), 1)
    dist_c = (qi - kj).astype(F32)
    valid_c = kj <= qi
    valid_p = (kj > qi) & (first < 0.5)
    bf = lambda xs: [a.astype(BF16) for a in xs]
    kcb, kpb, vcb, vpb = bf(kcs), bf(kps), bf(vcs), bf(vps)
    scale = HEAD_DIM_B ** -0.5
    heads = [(g, j, half) for g in range(len(kcs)) for j in range(PAIRS_PER_KV) for half in range(2)]
    kvs = [g for g, _, _ in heads]
    hmasks = [lo if half == 0 else 1.0 - lo for _, _, half in heads]
    hds = [2.0 * (PAIRS_PER_KV * (kv0 + g) + j) + half for g, j, half in heads]
    slopes = _each(lambda hd: jnp.exp(-(hd + 1.0) * (8.0 / N_HEADS_B * math.log(2.0))), hds)
    snks = _each(lambda hd: jnp.sum(sinks * (lane.astype(F32) == hd).astype(F32), axis=1, keepdims=True), hds)
    qhs = _each(lambda h, hm: (qps[h[0] * PAIRS_PER_KV + h[1]] * hm).astype(BF16), heads, hmasks)
    lcs = _each(lambda qh, g, sl: jnp.where(valid_c, _dot(qh, kcb[g], NT) * scale - sl * dist_c, NEG), qhs, kvs, slopes)
    lps = _each(lambda qh, g, sl: jnp.where(valid_p, _dot(qh, kpb[g], NT) * scale - sl * (dist_c + w), NEG), qhs, kvs, slopes)
    ms = _each(lambda lc, lp, sk: lax.stop_gradient(jnp.maximum(jnp.maximum(jnp.max(lc, axis=1, keepdims=True),
                                                                            jnp.max(lp, axis=1, keepdims=True)), sk)), lcs, lps, snks)
    ecs = _each(lambda lc, m: jnp.exp(lc - m), lcs, ms)
    eps = _each(lambda lp, m: jnp.exp(lp - m), lps, ms)
    invs = _each(lambda ec, ep, sk, m: 1.0 / (jnp.sum(ec, axis=1, keepdims=True) + jnp.sum(ep, axis=1, keepdims=True) + jnp.exp(sk - m)),
                 ecs, eps, snks, ms)
    ohs = _each(lambda ec, ep, inv, g, hm: (_dot((ec * inv).astype(BF16), vcb[g]) + _dot((ep * inv).astype(BF16), vpb[g])) * hm,
                ecs, eps, invs, kvs, hmasks)
    return [ohs[2 * j] + ohs[2 * j + 1] for j in range(len(qps))]


def _scalar11(v):
    return jnp.full((1, 1), v, F32)


def _att_specs(row_of):
    cur = pl.BlockSpec((WINDOW, KV_STEP * LANES), lambda i, kv: (row_of(i), kv))
    prev = pl.BlockSpec((WINDOW, KV_STEP * LANES), lambda i, kv: (jnp.maximum(row_of(i) - 1, 0), kv))
    qs = pl.BlockSpec((WINDOW, KV_STEP * PAIRS_PER_KV * LANES), lambda i, kv: (row_of(i), kv))
    return qs, cur, prev, pl.BlockSpec((1, LANES), lambda i, kv: (0, 0))


def swa_fwd(qsrc, kd, vd, sinks, *, name):
    t = kd.shape[0]
    nb = t // WINDOW
    npair = KV_STEP * PAIRS_PER_KV

    def body(q_ref, kc_ref, kp_ref, vc_ref, vp_ref, s_ref, o_ref):
        first = _scalar11((pl.program_id(0) == 0).astype(F32))
        kv0 = _scalar11((pl.program_id(1) * KV_STEP).astype(F32))
        per_kv = lambda ref: [ref[:, _psl(g)] for g in range(KV_STEP)]
        outs = _att_fn([q_ref[:, _psl(j)] for j in range(npair)], per_kv(kc_ref), per_kv(kp_ref), per_kv(vc_ref), per_kv(vp_ref),
                       s_ref[...], kv0, first)
        for j in range(npair):
            o_ref[:, _psl(j)] = outs[j].astype(o_ref.dtype)

    qs, cur, prev, sk = _att_specs(lambda i: i)
    return pl.pallas_call(
        body, grid=(nb, N_KV_B // KV_STEP), in_specs=[qs, cur, prev, cur, prev, sk],
        out_specs=qs, out_shape=jax.ShapeDtypeStruct((t, N_PAIRS * LANES), BF16), name=name,
        compiler_params=_params(("parallel", "parallel")),
    )(qsrc, kd, kd, vd, vd, sinks)


def swa_bwd(do, qsrc, kd, vd, sinks, *, name):
    t = kd.shape[0]
    nb = t // WINDOW

    npair = KV_STEP * PAIRS_PER_KV

    def body(do_ref, q_ref, kc_ref, kp_ref, vc_ref, vp_ref, s_ref, dq_ref, dk_ref, dv_ref, ds_ref, carry_k, carry_v):
        step, kvg = pl.program_id(0), pl.program_id(1)
        first = _scalar11((step == nb - 1).astype(F32))

        @pl.when((step == 0) & (kvg == 0))
        def _():
            carry_k[...] = jnp.zeros_like(carry_k)
            carry_v[...] = jnp.zeros_like(carry_v)
            ds_ref[...] = jnp.zeros_like(ds_ref)

        kv0 = _scalar11((kvg * KV_STEP).astype(F32))
        per_kv = lambda ref: [ref[:, _psl(g)].astype(F32) for g in range(KV_STEP)]
        _, vjp = jax.vjp(lambda *a: _att_fn(*a, kv0, first), [q_ref[:, _psl(j)].astype(F32) for j in range(npair)],
                         per_kv(kc_ref), per_kv(kp_ref), per_kv(vc_ref), per_kv(vp_ref), s_ref[...])
        dqs, dkc, dkp, dvc, dvp, dsk = vjp([do_ref[:, _psl(j)].astype(F32) for j in range(npair)])
        for j in range(npair):
            dq_ref[:, _psl(j)] = dqs[j].astype(dq_ref.dtype)
        ds_ref[...] += dsk
        fold = lambda g: g + pltpu.roll(g, HEAD_DIM_B, 1)
        for g in range(KV_STEP):
            kv = kvg * KV_STEP + g
            dk_ref[:, _psl(g)] = fold(dkc[g] + carry_k[kv]).astype(dk_ref.dtype)
            dv_ref[:, _psl(g)] = fold(dvc[g] + carry_v[kv]).astype(dv_ref.dtype)
            carry_k[kv] = dkp[g]
            carry_v[kv] = dvp[g]

    qs, cur, prev, sk = _att_specs(lambda i: nb - 1 - i)
    return pl.pallas_call(
        body, grid=(nb, N_KV_B // KV_STEP),
        in_specs=[qs, qs, cur, prev, cur, prev, sk],
        out_specs=[qs, cur, cur, sk],
        out_shape=[jax.ShapeDtypeStruct((t, N_PAIRS * LANES), BF16), jax.ShapeDtypeStruct((t, N_KV_B * LANES), BF16),
                   jax.ShapeDtypeStruct((t, N_KV_B * LANES), BF16), jax.ShapeDtypeStruct((1, LANES), F32)],
        scratch_shapes=[pltpu.VMEM((N_KV_B, WINDOW, LANES), F32), pltpu.VMEM((N_KV_B, WINDOW, LANES), F32)],
        name=name, compiler_params=_params(("arbitrary", "arbitrary")),
    )(do, qsrc, kd, kd, vd, vd, sinks)


def loss_head(h, tgt, w, *, name):
    t, d = h.shape
    tm = min(ROW_TM, t)

    def body(h_ref, t_ref, w_ref, dh_ref, dw_ref, l_ref):
        tg = t_ref[...]

        def f(hv, wv):
            err = _f_norm(hv, wv) - tg
            return 0.5 * jnp.sum(jnp.sum(err * err, axis=1, keepdims=True), axis=0, keepdims=True) * (1.0 / d)

        lv, vjp = jax.vjp(f, h_ref[...], w_ref[...])
        dh, dw = vjp(jnp.ones((1, 1), F32))
        dh_ref[...] = dh
        first = pl.program_id(0) == 0

        @pl.when(first)
        def _():
            dw_ref[...] = dw
            l_ref[...] = lv * jnp.ones((1, LANES), F32)

        @pl.when(jnp.logical_not(first))
        def _():
            dw_ref[...] += dw
            l_ref[...] += lv * jnp.ones((1, LANES), F32)

    rows = pl.BlockSpec((tm, d), lambda i: (i, 0))
    one = lambda c: pl.BlockSpec((1, c), lambda i: (0, 0))
    return pl.pallas_call(
        body, grid=(t // tm,), in_specs=[rows, rows, one(d)], out_specs=[rows, one(d), one(LANES)],
        out_shape=[jax.ShapeDtypeStruct((t, d), F32), jax.ShapeDtypeStruct((1, d), F32), jax.ShapeDtypeStruct((1, LANES), F32)],
        name=name, compiler_params=_params(("arbitrary",)),
    )(h, tgt, w)


def _row_tile(r, cap=256):
    tr = r
    if r % SUBLANES == 0:
        for cand in range(SUBLANES, min(r, cap) + 1, SUBLANES):
            if r % cand == 0:
                tr = cand
    return tr


def _adamw_update(wv, gv, mv, vv):
    mn = ADAM_B1 * mv + (1.0 - ADAM_B1) * gv
    vn = ADAM_B2 * vv + (1.0 - ADAM_B2) * jnp.square(gv)
    m_hat = mn / (1.0 - ADAM_B1 ** ADAM_STEP)
    v_hat = vn / (1.0 - ADAM_B2 ** ADAM_STEP)
    return -ADAM_LR * (m_hat / (jnp.sqrt(v_hat) + ADAM_EPS) + ADAM_WD * wv), mn, vn


def adamw_layers(w, halves, m, v, *, name):
    nl, r, c = w.shape
    tr = _row_tile(r // 2)
    nbh = r // 2 // tr

    def body(w_ref, *rest):
        g_refs, m_ref, v_ref = rest[:2 * nl], rest[2 * nl], rest[2 * nl + 1]
        d_ref, mo_ref, vo_ref, go_ref = rest[2 * nl + 2:]
        layer, i = pl.program_id(0), pl.program_id(1)
        mine = (i < nbh) == (lax.axis_index("c") == 0)
        gv = jnp.where(mine, g_refs[0][...], g_refs[1][...])
        for k in range(1, nl):
            gv = jnp.where(layer == k, jnp.where(mine, g_refs[2 * k][...], g_refs[2 * k + 1][...]), gv)
        d_ref[...], mo_ref[...], vo_ref[...] = _adamw_update(w_ref[...], gv, m_ref[...], v_ref[...])
        go_ref[...] = gv

    spec3 = pl.BlockSpec((None, tr, c), lambda k, i: (k, i, 0))
    g_specs = [pl.BlockSpec((tr, c), lambda k, i, q=q: (jnp.where(k == q, i % nbh, 0), 0)) for q in range(nl) for _ in range(2)]
    return pl.pallas_call(
        body, grid=(nl, r // tr), in_specs=[spec3] + g_specs + [spec3, spec3], out_specs=[spec3] * 4,
        out_shape=[jax.ShapeDtypeStruct((nl, r, c), F32)] * 4, name=name, compiler_params=_params(("arbitrary", "arbitrary")),
    )(w, *[h for pair in halves for h in pair], m, v)


def adamw(w, g, m, v, *, name):
    r, c = w.shape
    tr = _row_tile(r)

    def body(w_ref, g_ref, m_ref, v_ref, d_ref, mo_ref, vo_ref):
        d_ref[...], mo_ref[...], vo_ref[...] = _adamw_update(w_ref[...], g_ref[...], m_ref[...], v_ref[...])

    spec = pl.BlockSpec((tr, c), lambda i: (i, 0))
    return pl.pallas_call(
        body, grid=(r // tr,), in_specs=[spec] * 4, out_specs=[spec] * 3,
        out_shape=[jax.ShapeDtypeStruct((r, c), F32)] * 3, name=name, compiler_params=_params(("parallel",)),
    )(w, g, m, v)


def _place():
    return lax.axis_index("x"), lax.axis_index("y"), lax.axis_index("c")


def allgather8(blk, *, name):
    def body(x_ref, out_ref, send_sems, recv_sems, local_sem):
        x, y, c = _place()
        me = 4 * x + 2 * y + c
        mine = pltpu.make_async_copy(x_ref, out_ref.at[me], local_sem)
        mine.start()
        sent = []
        for k in range(1, N_DEV):
            to = (x ^ ((k >> 2) & 1), y ^ ((k >> 1) & 1), c ^ (k & 1))
            cp = pltpu.make_async_remote_copy(src_ref=x_ref, dst_ref=out_ref.at[me], send_sem=send_sems.at[k - 1],
                                              recv_sem=recv_sems.at[k - 1], device_id=to, device_id_type=MESH)
            cp.start()
            sent.append(cp)
        for k in range(1, N_DEV):
            frm = me ^ k
            pltpu.make_async_remote_copy(src_ref=x_ref, dst_ref=out_ref.at[frm], send_sem=send_sems.at[k - 1],
                                         recv_sem=recv_sems.at[k - 1], device_id=(x, y, c), device_id_type=MESH).wait_recv()
        for cp in sent:
            cp.wait_send()
        mine.wait()

    vm = pl.BlockSpec(memory_space=pltpu.VMEM)
    return pl.pallas_call(
        body, in_specs=[vm], out_specs=vm, out_shape=jax.ShapeDtypeStruct((N_DEV,) + blk.shape, blk.dtype), name=name,
        scratch_shapes=[pltpu.SemaphoreType.DMA((N_DEV - 1,)), pltpu.SemaphoreType.DMA((N_DEV - 1,)), pltpu.SemaphoreType.DMA],
    )(blk)


def _other_chips(x, y):
    return [(1 - x, y), (x, 1 - y), (1 - x, 1 - y)]


def _hbm_call(body, ins, out_shapes, n_sems, name):
    hbm = pl.BlockSpec(memory_space=pl.ANY)
    return pl.pallas_call(
        body, in_specs=[hbm] * len(ins), out_specs=[hbm] * len(out_shapes), out_shape=out_shapes, name=name,
        scratch_shapes=[pltpu.SemaphoreType.DMA((n_sems,)), pltpu.SemaphoreType.DMA((n_sems,))],
    )(*ins)


def _half_rows(c, rh):
    return pl.ds(pl.multiple_of(c * rh, BF16_ROWS), rh)


def gather_units(units, *, name):
    nu = len(units)
    shapes = []
    for arr, layer_major in units:
        r, cols = arr.shape
        shapes.append(jax.ShapeDtypeStruct((2, N_CHIPS, r // 2, cols) if layer_major else (N_CHIPS, r, cols), arr.dtype))

    def body(*refs):
        in_refs, out_refs, send_sems, recv_sems = refs[:nu], refs[nu:2 * nu], refs[2 * nu], refs[2 * nu + 1]
        x, y, c = _place()
        me_chip = 2 * x + y
        sib = (x, y, 1 - c)
        chips = _other_chips(x, y)

        def copy(k, src, dst, to):
            return pltpu.make_async_remote_copy(src_ref=src, dst_ref=dst, send_sem=send_sems.at[k], recv_sem=recv_sems.at[k],
                                                device_id=to, device_id_type=MESH)

        first, passed, landing = [], [], []
        for u, (arr, layer_major) in enumerate(units):
            rh = arr.shape[0] // 2
            out_ref = out_refs[u]
            slot = (lambda chip, half, o=out_ref: o.at[half, chip]) if layer_major else \
                   (lambda chip, half, o=out_ref, rh=rh: o.at[chip, _half_rows(half, rh), :])
            my_half = in_refs[u].at[_half_rows(c, rh), :]
            for j, (cx, cy) in enumerate(chips):
                k = 6 * u + j
                first.append(copy(k, my_half, slot(me_chip, c), (cx, cy, c)))
                passed.append(copy(k + 3, slot(2 * cx + cy, c), slot(2 * cx + cy, c), sib))
                landing.append((copy(k, my_half, slot(2 * cx + cy, c), sib), copy(k + 3, my_half, slot(2 * cx + cy, 1 - c), sib)))
        for cp in first:
            cp.start()
        for (over_ici, _), fwd in zip(landing, passed):
            over_ici.wait_recv()
            fwd.start()
        for _, from_sibling in landing:
            from_sibling.wait_recv()
        for cp in first + passed:
            cp.wait_send()

    return _hbm_call(body, [a for a, _ in units], shapes, 6 * nu, name)


HBM_SPEC = pl.BlockSpec(memory_space=pltpu.HBM)
SEM_SPEC = pl.BlockSpec(memory_space=pltpu.SEMAPHORE)
ORDERED_EFFECT = pltpu.SideEffectType.DATAFLOW_SIDE_EFFECTING


def _split_start(body, srcs, land_shapes, after, *, name):
    nu = len(srcs)
    lands = [lax.empty(s.shape, s.dtype) for s in land_shapes]

    def whole(*refs):
        body(refs[:nu], refs[nu:2 * nu], refs[2 * nu + 1], refs[2 * nu + 2])
        refs[-1][...] = jnp.zeros((SUBLANES, LANES), F32)

    hbm = lambda a: pltpu.with_memory_space_constraint(a, pltpu.HBM)
    sems = pltpu.SemaphoreType.DMA((nu,))
    res = pl.pallas_call(
        whole, name=name, in_specs=[HBM_SPEC] * (2 * nu) + [pl.BlockSpec(memory_space=pl.ANY)],
        out_shape=[sems, sems] + [pltpu.HBM(a.shape, a.dtype) for a in srcs] + [pltpu.HBM(s.shape, s.dtype) for s in land_shapes]
        + [jax.ShapeDtypeStruct((SUBLANES, LANES), F32)],
        out_specs=[SEM_SPEC, SEM_SPEC] + [HBM_SPEC] * (2 * nu) + [pl.BlockSpec(memory_space=pltpu.VMEM)],
        input_output_aliases={q: 2 + q for q in range(2 * nu)},
        compiler_params=pltpu.CompilerParams(has_side_effects=ORDERED_EFFECT),
    )(*[hbm(a) for a in srcs], *[hbm(a) for a in lands], after)
    return res[0], res[1], res[2:2 + nu], res[2 + nu:2 + 2 * nu], res[-1]


def _split_wait(pending, moved, after, *, name):
    send_sems, recv_sems, srcs, lands, _ = pending
    nu = len(srcs)

    def body(*refs):
        land_refs, ssem, rsem = refs[nu:2 * nu], refs[2 * nu], refs[2 * nu + 1]
        x, y, c = _place()
        for u in range(nu):
            size = moved(land_refs[u])
            cp = pltpu.make_async_remote_copy(src_ref=size, dst_ref=size, send_sem=ssem.at[u], recv_sem=rsem.at[u],
                                              device_id=(x, y, c), device_id_type=MESH)
            cp.wait_send()
            cp.wait_recv()

    res = pl.pallas_call(
        body, name=name, in_specs=[HBM_SPEC] * (2 * nu) + [SEM_SPEC, SEM_SPEC, pl.BlockSpec(memory_space=pl.ANY)],
        out_shape=[pltpu.HBM(a.shape, a.dtype) for a in srcs] + [pltpu.HBM(a.shape, a.dtype) for a in lands],
        out_specs=[HBM_SPEC] * (2 * nu), input_output_aliases={q: q for q in range(2 * nu)},
        compiler_params=pltpu.CompilerParams(has_side_effects=ORDERED_EFFECT),
    )(*srcs, *lands, send_sems, recv_sems, after)
    return res[nu:]


def gather_start(shards, after, *, name):
    def body(src_refs, land_refs, send_sems, recv_sems):
        x, y, c = _place()
        for u, shard in enumerate(shards):
            rows = _half_rows(c, shard.shape[0] // 2)
            for cx, cy in _other_chips(x, y):
                for core in range(2):
                    pltpu.make_async_remote_copy(src_ref=src_refs[u].at[rows, :], dst_ref=land_refs[u].at[2 * x + y, rows, :],
                                                 send_sem=send_sems.at[u], recv_sem=recv_sems.at[u], device_id=(cx, cy, core),
                                                 device_id_type=MESH).start()

    return _split_start(body, shards, [jax.ShapeDtypeStruct((N_CHIPS,) + s.shape, s.dtype) for s in shards], after, name=name)


def gather_wait(pending, after, *, name):
    return _split_wait(pending, lambda land: land.at[pl.ds(0, N_CHIPS - 1)], after, name=name)


def scatter_start(pairs, *, name):
    def body(src_refs, land_refs, send_sems, recv_sems):
        x, y, c = _place()
        for u in range(len(pairs)):
            for j, (cx, cy) in enumerate(_other_chips(x, y)):
                pltpu.make_async_remote_copy(src_ref=src_refs[u].at[2 * cx + cy], dst_ref=land_refs[u].at[j], send_sem=send_sems.at[u],
                                             recv_sem=recv_sems.at[u], device_id=(cx, cy, c), device_id_type=MESH).start()

    return _split_start(body, pairs, [jax.ShapeDtypeStruct((N_CHIPS - 1,) + p.shape[1:], p.dtype) for p in pairs], pairs[0], name=name)


def scatter_wait(pending, after, *, name):
    return _split_wait(pending, lambda land: land, after, name=name)


def swap_units(units, *, name):
    nu = len(units)

    def body(*refs):
        g_refs, out_refs, send_sems, recv_sems = refs[:nu], refs[nu:2 * nu], refs[2 * nu], refs[2 * nu + 1]
        x, y, c = _place()
        cps = [pltpu.make_async_remote_copy(src_ref=g_refs[u].at[:, _half_rows(1 - c, units[u].shape[1] // 2), :], dst_ref=out_refs[u],
                                            send_sem=send_sems.at[u], recv_sem=recv_sems.at[u], device_id=(x, y, 1 - c),
                                            device_id_type=MESH) for u in range(nu)]
        for cp in cps:
            cp.start()
        for cp in cps:
            cp.wait()

    shapes = [jax.ShapeDtypeStruct((N_CHIPS, g.shape[1] // 2, g.shape[2]), g.dtype) for g in units]
    return _hbm_call(body, units, shapes, nu, name)


def join_units(units, *, name):
    nu = len(units)

    def body(*refs):
        h_refs, out_refs, send_sems, recv_sems = refs[:nu], refs[nu:2 * nu], refs[2 * nu], refs[2 * nu + 1]
        x, y, c = _place()
        cps = [pltpu.make_async_remote_copy(src_ref=h_refs[u], dst_ref=out_refs[u], send_sem=send_sems.at[u], recv_sem=recv_sems.at[u],
                                            device_id=(x, y, 1 - c), device_id_type=MESH) for u in range(nu)]
        for cp in cps:
            cp.start()
        for cp in cps:
            cp.wait()

    return _hbm_call(body, units, [jax.ShapeDtypeStruct(h.shape, h.dtype) for h in units], nu, name)


def _half_tile(rh):
    tr = rh
    for cand in range(BF16_ROWS, min(rh, 512) + 1, BF16_ROWS):
        if rh % cand == 0:
            tr = cand
    return tr


def pair_add(g, sib, *, name):
    nc, rh, cols = sib.shape
    tr = _half_tile(rh)
    nbh = rh // tr

    def body(g0_ref, g1_ref, s_ref, o_ref):
        mine = jnp.where(lax.axis_index("c") == 0, g0_ref[...], g1_ref[...])
        o_ref[...] = (mine.astype(F32) + s_ref[...].astype(F32)).astype(o_ref.dtype)

    blk = lambda off: pl.BlockSpec((None, tr, cols), lambda j, i: (j, off + i, 0))
    return pl.pallas_call(
        body, grid=(nc, nbh), in_specs=[blk(0), blk(nbh), blk(0)], out_specs=blk(0),
        out_shape=jax.ShapeDtypeStruct(sib.shape, BF16), name=name, compiler_params=_params(("parallel", "parallel")),
    )(g, g, sib)


def chips_add(pair, landed, *, name):
    nc, rh, cols = pair.shape
    tr = _half_tile(rh)

    def body(*refs):
        chip = 2 * lax.axis_index("x") + lax.axis_index("y")
        acc = refs[0][...]
        for j in range(1, nc):
            acc = jnp.where(chip == j, refs[j][...], acc)
        acc = acc.astype(F32)
        for r in refs[nc:-1]:
            acc = acc + r[...].astype(F32)
        refs[-1][...] = acc

    part = lambda q: pl.BlockSpec((None, tr, cols), lambda i, q=q: (q, i, 0))
    return pl.pallas_call(
        body, grid=(rh // tr,), in_specs=[part(q) for q in range(nc)] + [part(q) for q in range(landed.shape[0])],
        out_specs=pl.BlockSpec((tr, cols), lambda i: (i, 0)),
        out_shape=jax.ShapeDtypeStruct((rh, cols), F32), name=name, compiler_params=_params(("parallel",)),
    )(*[pair] * nc, *[landed] * landed.shape[0])


def sum8(g, *, name):
    def body(g_ref, o_ref):
        acc = g_ref[0]
        for d in range(1, N_DEV):
            acc = acc + g_ref[d]
        o_ref[...] = acc

    return pl.pallas_call(body, out_shape=jax.ShapeDtypeStruct(g.shape[1:], F32), name=name)(g)


def _dup_halves(a):
    t = a.shape[0]
    a = a.reshape(t, N_KV_B, HEAD_DIM_B)
    return jnp.concatenate([a, a], axis=-1).reshape(t, N_KV_B * LANES)


def _undup(a):
    t = a.shape[0]
    return a.reshape(t, N_KV_B, LANES)[:, :, :HEAD_DIM_B].reshape(t, N_KV_B * HEAD_DIM_B)


def _lane_pad(v, offset=0):
    return jnp.zeros((1, LANES), F32).at[0, offset:offset + v.shape[0]].set(v)


SHARD_UP = 2 * D_FF // N_CHIPS
SHARD_BIN = (N_HEADS_B + 2 * N_KV_B) * HEAD_DIM_B // N_CHIPS
SHARD_PROJ = D_MODEL // N_CHIPS


def local_step(x, p, tgt, sm, weight, on_grads):
    t = x.shape[0]
    rtm = min(ROW_TM, t)
    hk = N_HEADS_A * HEAD_DIM_A
    qd_b = N_HEADS_B * HEAD_DIM_B
    kd_b = N_KV_B * HEAD_DIM_B
    gs = {}
    norm = lambda h, w, nm: tile_map(_f_norm, [(h, D_MODEL, 0)], [w], [(D_MODEL, BF16)], tm=rtm, ncol=1, name=nm)[0]

    spec = pl.BlockSpec
    mtm = _tile(D_MODEL, MM_TM_CAP)
    p_bf = p.astype(BF16)
    alog_p = _lane_pad(sm["a_log"][0], N_HEADS_A)
    dtb_p = _lane_pad(sm["a_dt_bias"][0], N_HEADS_A)
    sinks_p = _lane_pad(sm["b_sinks"][0])
    nw = lambda name, i: sm[name][i:i + 1]
    by_chip = lambda kdim, ns: dict(tn=ns, tk=kdim, b_spec=spec((None, kdim, ns), lambda r, j, kk: (j, kk, 0)))
    by_chip_t = lambda ndim, ns: dict(n=ndim, tn=ndim, tk=ns, b_spec=spec((None, ndim, ns), lambda r, j, kk: (kk, j, 0)))
    cache = {}

    def wgt(name, i, after):
        if (name, i) not in cache:
            cache[name, i] = weight(name, i, after)
        return cache[name, i]

    saved = []
    h = x
    hn_next = norm(h, nw("norm_mix", 0), "norm_mix0")
    for i in range(DEPTH):
        s = {"h0": h, "hn": hn_next}
        if i % 2 == 0:
            s["pm"] = mm(s["hn"], wgt("a_w_in", i, h), name="a_in")
            tail = (s["pm"], LANES, 4 * hk // LANES)
            s["c"] = conv_fwd(s["pm"], wgt("a_conv", i, h), name="a_conv")
            s["bg"] = tile_map(_f_betag, [tail], [alog_p, dtb_p], [(LANES, F32)], tm=rtm, ncol=1, name="a_betag")[0]
            s["prep"], s["tms"] = delta_prep(s["c"], s["bg"], name="a_prep")
            s["o"], s["s_in"] = delta_scan(*s["prep"], name="a_scan")
            s["on"] = gnorm_fwd(s["o"], s["pm"], sm["a_norm"], name="a_gnorm")
            h, s["hf"] = mm(s["on"], wgt("a_w_out", i, s["on"]), add=h, norm_w=nw("norm_ffn", i), name="a_out")
        else:
            s["pb"] = mm(s["hn"], wgt("b_w_in", i, s["hn"]), name="b_in", out_dtype=BF16, n=N_CHIPS * SHARD_BIN,
                         **by_chip(D_MODEL, SHARD_BIN))
            s["kd"], s["vd"] = _dup_halves(s["pb"][:, qd_b:qd_b + kd_b]), _dup_halves(s["pb"][:, qd_b + kd_b:])
            s["ao"] = swa_fwd(s["pb"], s["kd"], s["vd"], sinks_p, name="b_att")
            h, s["hf"] = mm(s["ao"], wgt("b_w_out", i, s["ao"]), add=h, norm_w=nw("norm_ffn", i), name="b_out")
        s["h1"] = h
        s["u"] = mm(s["hf"], wgt("f_w_up", i, s["hf"]), name=f"f_up{i}", out_dtype=BF16, n=2 * D_FF, tm_cap=2 * MM_TM_CAP,
                    **by_chip(D_MODEL, SHARD_UP))
        s["act"] = conv_act_fwd(s["u"], wgt("f_conv", i, s["hf"]), name=f"f_conv_act{i}")
        h, s["hp"] = mm(s["act"], wgt("f_w_down", i, s["act"]), add=h, norm_w=nw("norm_ple", i), name=f"f_down{i}", tk=D_FF)
        s["h2"] = h
        s["gl"] = mm(s["hp"], wgt("ple_w_gate", i, s["hp"]), name=f"ple_gate{i}")
        s["pe"] = mm(p_bf[i], wgt("ple_w_proj", i, s["hp"]), name=f"ple_proj{i}", n=D_MODEL, **by_chip(PLE_DIM, SHARD_PROJ))
        rows3 = [(h, D_MODEL, 0), (s["gl"], D_MODEL, 0), (s["pe"], D_MODEL, 0)]
        if i + 1 < DEPTH:
            def mix_norm(hv, g, e, wn):
                hn = hv + _f_ple(g, e)
                return hn, _f_norm(hn, wn)
            h, hn_next = tile_map(mix_norm, rows3, [nw("norm_mix", i + 1)], [(D_MODEL, F32), (D_MODEL, BF16)], tm=rtm, ncol=1,
                                  name=f"ple_mix{i}")
        else:
            h = tile_map(lambda hv, g, e: hv + _f_ple(g, e), rows3, [], [(D_MODEL, F32)], tm=rtm, ncol=1, name=f"ple_mix{i}")[0]
        saved.append(s)

    dh, gnf, loss = loss_head(h, tgt, sm["norm_final"][None, :], name="loss_head")
    gs["norm_final"] = gnf[0]

    g_mix, g_ffn, g_ple, g_conv = ([None] * DEPTH for _ in range(4))
    zero = jnp.zeros((1, 1), F32)
    for i in reversed(range(DEPTH)):
        s, gw = saved[i], {}
        by_rows = lambda g: g.reshape(N_CHIPS, g.shape[0] // N_CHIPS, g.shape[1])
        (dgl, dpe), _ = tile_vjp(_f_ple, [(s["gl"], D_MODEL, 0), (s["pe"], D_MODEL, 0)], [], [(dh, D_MODEL, 0)], n_diff=2,
                                 tm=rtm, ncol=1, name=f"ple_mix_bwd{i}", grad_dtypes=[BF16, BF16])
        gw["ple_w_proj"] = mm(p_bf[i], dpe, ta=True, name=f"ple_proj_dw{i}", out_dtype=BF16, tn=SHARD_PROJ,
                              o_shape=(N_CHIPS, PLE_DIM, SHARD_PROJ), o_spec=spec((None, PLE_DIM, SHARD_PROJ), lambda r, j, kk: (j, r, 0)))
        gw["ple_w_gate"] = by_rows(mm(s["hp"], dgl, ta=True, name=f"ple_gate_dw{i}", out_dtype=BF16))
        fused = dict(tb=True, tm_cap=MM_TM_CAP // 2)
        dh, g_ple[i] = mm(dgl, cache["ple_w_gate", i], name=f"ple_gate_dx{i}", norm_grad=(s["h2"], nw("norm_ple", i) + zero, dh), **fused)

        dact = mm(dh, cache["f_w_down", i], tb=True, name=f"f_down_dx{i}")
        gw["f_w_down"] = by_rows(mm(s["act"], dh, ta=True, name=f"f_down_dw{i}", out_dtype=BF16, tm_cap=D_FF // 2))
        du_halves = conv_act_bwd(s["u"], dact, cache["f_conv", i], name=f"f_conv_act_bwd{i}")
        g_conv[i] = jnp.concatenate(du_halves[2:], axis=1)
        dhf = g_up = None
        for half, du in enumerate(du_halves[:2]):
            c0 = half * (N_CHIPS // 2)
            g_up = mm(s["hf"], du, ta=True, name=f"f_up_dw{i}_{half}", out_dtype=BF16, tn=SHARD_UP, into=g_up,
                      o_shape=(N_CHIPS, D_MODEL, SHARD_UP), o_spec=spec((None, mtm, SHARD_UP), lambda r, j, kk, c0=c0: (c0 + j, r, 0)))
            last = dict(norm_grad=(s["h1"], nw("norm_ffn", i), dh), **fused) if half else dict(tb=True)
            dhf = mm(du, cache["f_w_up", i], name=f"f_up_dx{i}_{half}", n=D_MODEL, tn=D_MODEL, tk=SHARD_UP, add=dhf,
                     b_spec=spec((None, D_MODEL, SHARD_UP), lambda r, j, kk, c0=c0: (c0 + kk, j, 0)), **last)
        gw["f_w_up"] = g_up
        dh, g_ffn[i] = dhf
        token, gw = on_grads(i, "ffn", gw), {}
        w_out = cache["a_w_out" if i % 2 == 0 else "b_w_out", i]
        if token is not None:
            w_out = w_out + token[:1, :1].astype(BF16)

        if i % 2 == 0:
            don = mm(dh, w_out, tb=True, name="a_out_dx")
            gw["a_w_out"] = by_rows(mm(s["on"], dh, ta=True, name="a_out_dw", out_dtype=BF16))
            do, dz, gs["a_norm"] = gnorm_bwd(s["o"], s["pm"], sm["a_norm"], don, name="a_gnorm_bwd")
            dprep = delta_scan_bwd(do, *s["prep"], s["s_in"], name="a_scan_bwd")
            dc, dbg = delta_prep_bwd(s["c"], s["bg"], s["tms"], dprep, name="a_prep_bwd")
            (dpt,), (galog, gdtb) = tile_vjp(_f_betag, [(s["pm"], LANES, 4 * hk // LANES)], [alog_p, dtb_p], [(dbg, LANES, 0)], n_diff=1,
                                             tm=rtm, ncol=1, name="a_betag_bwd", grad_dtypes=[BF16])
            gs["a_log"] = galog[:, N_HEADS_A:2 * N_HEADS_A]
            gs["a_dt_bias"] = gdtb[:, N_HEADS_A:2 * N_HEADS_A]
            dqkv, gs["a_conv"] = conv_bwd(dc, s["pm"], cache["a_conv", i], name="a_conv_bwd")
            dpm = jnp.concatenate([dqkv, dz, dpt], axis=1)
            g_in = mm(s["hn"], dpm, ta=True, name="a_in_dw", out_dtype=BF16)[:, :4 * hk + 2 * N_HEADS_A]
            gw["a_w_in"] = g_in.reshape(D_MODEL, N_CHIPS, g_in.shape[1] // N_CHIPS).transpose(1, 0, 2)
            dh, g_mix[i] = mm(dpm, cache["a_w_in", i], name="a_in_dx", norm_grad=(s["h0"], nw("norm_mix", i), dh), **fused)
        else:
            dao = mm(dh, w_out, tb=True, name="b_out_dx")
            gw["b_w_out"] = by_rows(mm(s["ao"], dh, ta=True, name="b_out_dw", out_dtype=BF16))
            dq, dkd, dvd, gsk = swa_bwd(dao, s["pb"], s["kd"], s["vd"], sinks_p, name="b_att_bwd")
            gs["b_sinks"] = gsk[:, :N_HEADS_B]
            dpb = jnp.concatenate([dq, _undup(dkd), _undup(dvd)], axis=1)
            gw["b_w_in"] = mm(s["hn"], dpb, ta=True, name="b_in_dw", out_dtype=BF16, tn=SHARD_BIN,
                              o_shape=(N_CHIPS, D_MODEL, SHARD_BIN), o_spec=spec((None, mtm, SHARD_BIN), lambda r, j, kk: (j, r, 0)))
            dh, g_mix[i] = mm(dpb, cache["b_w_in", i], name="b_in_dx", norm_grad=(s["h0"], nw("norm_mix", i), dh), **fused,
                              **by_chip_t(D_MODEL, SHARD_BIN))
        token = on_grads(i, "mix", gw)
        if token is not None:
            zero = token[:1, :1]

    gs["norm_mix"], gs["norm_ffn"], gs["norm_ple"] = (jnp.concatenate(g, axis=0) for g in (g_mix, g_ffn, g_ple))
    gs["f_conv"] = jnp.stack(g_conv)
    return loss, dh, gs


BIG = ["a_w_in", "a_w_out", "b_w_in", "b_w_out", "f_w_up", "f_w_down", "ple_w_proj", "ple_w_gate"]
LAYERED = {"f_w_up", "f_w_down", "ple_w_proj", "ple_w_gate"}
BY_CHIP = {"b_w_in", "f_w_up", "ple_w_proj"}
LAYER_UNITS = [[("a_w_in", 0), ("a_w_out", 0)] + [(n, 0) for n in sorted(LAYERED)],
               [("b_w_in", 1), ("b_w_out", 1)] + [(n, 1) for n in sorted(LAYERED)]]
CONVS = ["a_conv", "f_conv"]
SMALL = ["norm_mix", "norm_ffn", "norm_ple", "norm_final", "a_log", "a_dt_bias", "a_norm", "b_sinks"]
SMALL_ROWS = 8
CONV_ROWS = 16
CONV_GRAD_ROWS = 48


def _pack_rows(arrs, rows, dtype):
    flat = jnp.concatenate([a.reshape(-1).astype(dtype) for a in arrs])
    return jnp.pad(flat, (0, rows * PACK_COLS - flat.shape[0])).reshape(rows, PACK_COLS)


def _unpack(flat, shapes):
    out, off = [], 0
    for shp in shapes:
        n = math.prod(shp)
        out.append(flat[off:off + n].reshape(shp))
        off += n
    return out


def _pack_small(d, loss=None):
    tail = jnp.concatenate([d["a_log"].reshape(-1), d["a_dt_bias"].reshape(-1), d["a_norm"].reshape(-1), d["b_sinks"].reshape(-1)])
    if loss is not None:
        tail = jnp.concatenate([tail, loss.reshape(-1)[:1]])
    tail = jnp.pad(tail, (0, PACK_COLS - tail.shape[0]))
    return jnp.concatenate([d["norm_mix"], d["norm_ffn"], d["norm_ple"], d["norm_final"][None, :], tail[None, :]], axis=0)


def _unpack_small(a, like):
    out = {"norm_mix": a[0:2], "norm_ffn": a[2:4], "norm_ple": a[4:6], "norm_final": a[6]}
    off = 0
    for nm in ("a_log", "a_dt_bias", "a_norm", "b_sinks"):
        n = like[nm].size
        out[nm] = a[7, off:off + n].reshape(like[nm].shape)
        off += n
    return out, a[7, off]


def _as2d(a):
    return a.reshape(-1, a.shape[-1])


def kernel(x, p, norm_mix, norm_ffn, norm_ple, norm_final, a_w_in, a_conv, a_log, a_dt_bias, a_norm, a_w_out, b_w_in, b_sinks, b_w_out, f_w_up, f_conv, f_w_down, ple_w_proj, ple_w_gate, loss_target, m_norm_mix, m_norm_ffn, m_norm_ple, m_norm_final, m_a_w_in, m_a_conv, m_a_log, m_a_dt_bias, m_a_norm, m_a_w_out, m_b_w_in, m_b_sinks, m_b_w_out, m_f_w_up, m_f_conv, m_f_w_down, m_ple_w_proj, m_ple_w_gate, v_norm_mix, v_norm_ffn, v_norm_ple, v_norm_final, v_a_w_in, v_a_conv, v_a_log, v_a_dt_bias, v_a_norm, v_a_w_out, v_b_w_in, v_b_sinks, v_b_w_out, v_f_w_up, v_f_conv, v_f_w_down, v_ple_w_proj, v_ple_w_gate):
    w = dict(norm_mix=norm_mix, norm_ffn=norm_ffn, norm_ple=norm_ple, norm_final=norm_final, a_w_in=a_w_in, a_conv=a_conv,
             a_log=a_log, a_dt_bias=a_dt_bias, a_norm=a_norm, a_w_out=a_w_out, b_w_in=b_w_in, b_sinks=b_sinks, b_w_out=b_w_out,
             f_w_up=f_w_up, f_conv=f_conv, f_w_down=f_w_down, ple_w_proj=ple_w_proj, ple_w_gate=ple_w_gate)
    m = dict(norm_mix=m_norm_mix, norm_ffn=m_norm_ffn, norm_ple=m_norm_ple, norm_final=m_norm_final, a_w_in=m_a_w_in,
             a_conv=m_a_conv, a_log=m_a_log, a_dt_bias=m_a_dt_bias, a_norm=m_a_norm, a_w_out=m_a_w_out, b_w_in=m_b_w_in,
             b_sinks=m_b_sinks, b_w_out=m_b_w_out, f_w_up=m_f_w_up, f_conv=m_f_conv, f_w_down=m_f_w_down,
             ple_w_proj=m_ple_w_proj, ple_w_gate=m_ple_w_gate)
    v = dict(norm_mix=v_norm_mix, norm_ffn=v_norm_ffn, norm_ple=v_norm_ple, norm_final=v_norm_final, a_w_in=v_a_w_in,
             a_conv=v_a_conv, a_log=v_a_log, a_dt_bias=v_a_dt_bias, a_norm=v_a_norm, a_w_out=v_a_w_out, b_w_in=v_b_w_in,
             b_sinks=v_b_sinks, b_w_out=v_b_w_out, f_w_up=v_f_w_up, f_conv=v_f_conv, f_w_down=v_f_w_down,
             ple_w_proj=v_ple_w_proj, ple_w_gate=v_ple_w_gate)
    xc, yc, cc = _place()
    my_chip = 2 * xc + yc

    shard = {(n, i): w[n][i if n in LAYERED else 0].astype(BF16) for n, i in LAYER_UNITS[0] + LAYER_UNITS[1]}
    first = shard["a_w_in", 0]
    (ga,) = gather_units([(first, False)], name="gather_first")
    ga = lax.dynamic_update_index_in_dim(ga, first, my_chip, 0)
    a_in = jnp.concatenate([ga[j] for j in range(N_CHIPS)], axis=1)
    n_main = 4 * N_HEADS_A * HEAD_DIM_A
    conv_shapes = [w[n].shape for n in CONVS]
    convs = allgather8(_pack_rows([w[n] for n in CONVS], CONV_ROWS, F32), name="gather_convs")
    conv_parts = [_unpack(convs[2 * j].reshape(-1), conv_shapes) for j in range(N_CHIPS)]
    a_conv_full, f_conv_full = (jnp.concatenate([conv_parts[j][q] for j in range(N_CHIPS)], axis=2) for q in range(2))
    ready = {("a_w_in", 0): jnp.pad(a_in, ((0, 0), (0, n_main + LANES - a_in.shape[1]))), ("a_conv", 0): a_conv_full[0], ("f_conv", 0): f_conv_full[0], ("f_conv", 1): f_conv_full[1]}
    later = [[k for k in units if k != ("a_w_in", 0)] for units in LAYER_UNITS]
    pending, after = [], ga
    for layer, keys in enumerate(later):
        pending.append(gather_start([shard[k] for k in keys], after, name=f"gather_start{layer}"))
        after = pending[-1][4]
    sm = {n: w[n] for n in SMALL}
    sm["norm_mix"] = sm["norm_mix"] + after[:1, :1]

    def weight(name, layer, act):
        if (name, layer) not in ready:
            landed = gather_wait(pending[layer], act, name=f"gather_wait{layer}")
            for k, g in zip(later[layer], landed):
                g = lax.dynamic_update_index_in_dim(g, shard[k], my_chip, 0)
                ready[k] = g if k[0] in BY_CHIP else g.reshape(N_CHIPS * g.shape[1], g.shape[2])
        return ready[name, layer]

    pairs, scattered, started = {}, {}, []

    def on_grads(layer, part, gw):
        keys = [k for k in LAYER_UNITS[layer] if (k[0] in LAYERED) == (part == "ffn")]
        from_sib = swap_units([gw[n] for n, _ in keys], name=f"rs_swap_{part}{layer}")
        for (n, _), sib in zip(keys, from_sib):
            pairs[n, layer] = pair_add(gw[n], sib, name=f"rs_add_pair_{n}{layer}")
        started.append((keys, scatter_start([pairs[k] for k in keys], name=f"rs_scatter_start_{part}{layer}"), f"{part}{layer}"))
        return started[-1][1][4]

    loss, grad_x, gs = local_step(x[0], p[:, 0], loss_target[0], sm, weight, on_grads)

    grads, delta, new_m, new_v, g_unit = {}, {}, {}, {}, {}

    def finish(keys, tag):
        halves = [chips_add(pairs[k], scattered[k], name=f"rs_add_chips_{k[0]}{k[1]}") for k in keys]
        g_unit.update(zip(keys, zip(halves, join_units(halves, name=f"rs_join_{tag}"))))
        for n in BIG:
            mine = [(n, i) for i in range(DEPTH) if (n, i) in LAYER_UNITS[i]]
            if n not in delta and all(k in g_unit for k in mine):
                g_layers = [g_unit[k] for k in mine]
                shape3 = (len(g_layers), 2 * g_layers[0][0].shape[0], g_layers[0][0].shape[1])
                res = adamw_layers(w[n].reshape(shape3), g_layers, m[n].reshape(shape3), v[n].reshape(shape3), name=f"adamw_{n}")
                delta[n], new_m[n], new_v[n], grads[n] = (r.reshape(w[n].shape) for r in res)

    last_keys, last_pending, last_tag = started[-1]
    for keys, pend, tag in started[:-1]:
        scattered.update(zip(keys, scatter_wait(pend, last_pending[4], name=f"rs_scatter_wait_{tag}")))
    finish([k for keys, _, _ in started[:-1] for k in keys], "first")

    conv_grads = _pack_rows([gs[n] for n in CONVS], CONV_GRAD_ROWS, F32)
    small_sum = sum8(allgather8(jnp.concatenate([_pack_small(gs, loss), conv_grads]), name="gather_small"), name="sum_small")
    g_sm, loss_sum = _unpack_small(small_sum[:SMALL_ROWS], sm)

    scattered.update(zip(last_keys, scatter_wait(last_pending, small_sum, name=f"rs_scatter_wait_{last_tag}")))
    finish(last_keys, "last")

    for n, full in zip(CONVS, _unpack(small_sum[SMALL_ROWS:].reshape(-1), [gs[n].shape for n in CONVS])):
        g2 = _as2d(lax.dynamic_slice_in_dim(full, my_chip * w[n].shape[-1], w[n].shape[-1], axis=full.ndim - 1))
        d2, m2, v2 = adamw(_as2d(w[n]), g2, _as2d(m[n]), _as2d(v[n]), name=f"adamw_{n}")
        grads[n], delta[n], new_m[n], new_v[n] = (r.reshape(w[n].shape) for r in (g2, d2, m2, v2))
    pk = lambda d: _pack_small(d)
    d2, m2, v2 = adamw(pk(sm), pk(g_sm), pk({n: m[n] for n in SMALL}), pk({n: v[n] for n in SMALL}), name="adamw_small")
    for src, dst in ((d2, delta), (m2, new_m), (v2, new_v)):
        dst.update(_unpack_small(src, sm)[0])
    grads.update(g_sm)

    order = ["norm_mix", "norm_ffn", "norm_ple", "norm_final", "a_w_in", "a_conv", "a_log", "a_dt_bias", "a_norm", "a_w_out",
             "b_w_in", "b_sinks", "b_w_out", "f_w_up", "f_conv", "f_w_down", "ple_w_proj", "ple_w_gate"]
    return (loss_sum, grad_x[None], *[grads[n] for n in order], *[delta[n] for n in order],
            *[new_m[n] for n in order], *[new_v[n] for n in order])
```

```python
import functools
import math

import jax
import jax.numpy as jnp
from jax import lax
from jax.experimental import pallas as pl
from jax.experimental.pallas import tpu as pltpu

F32 = jnp.float32
BF16 = jnp.bfloat16
MESH = pl.DeviceIdType.MESH

D_MODEL = 1024
N_HEADS_A = 8
HEAD_DIM_A = 128
CONV_A = 4
N_HEADS_B = 16
N_KV_B = 4
HEAD_DIM_B = 64
WINDOW = 128
D_FF = 2816
FFN_CONV = 3
PLE_DIM = 256
EPS = 1e-6
DEPTH = 2

ADAM_LR = 0.001
ADAM_B1 = 0.9
ADAM_B2 = 0.999
ADAM_EPS = 1e-08
ADAM_WD = 0.01
ADAM_STEP = 10

LANES = 128
SUBLANES = 8
BF16_ROWS = 16
CHUNK = 128
VMEM_LIMIT = 56 * 1024 * 1024
NEG = -1e30
N_CHIPS = 4
N_DEV = 8
PACK_COLS = 1024


def _params(sem=None):
    return pltpu.CompilerParams(dimension_semantics=sem, vmem_limit_bytes=VMEM_LIMIT)


def _tile(dim, cap):
    if dim % LANES:
        return dim
    best = LANES
    for t in range(LANES, min(dim, cap) + 1, LANES):
        if dim % t == 0:
            best = t
    return best


def _dot(a, b, dims=(((1,), (0,)), ((), ())), precision=None):
    return lax.dot_general(a, b, dims, precision=precision, preferred_element_type=F32)


NN = (((1,), (0,)), ((), ()))
NT = (((1,), (1,)), ((), ()))
TN = (((0,), (0,)), ((), ()))


ROW_TM = 1024
MM_TM_CAP = 1024
MM_TK_CAP_TOKENS = 2048


def mm(a, b, *, name, ta=False, tb=False, out_dtype=F32, add=None, norm_w=None, norm_grad=None, tm_cap=MM_TM_CAP, tn_cap=1408,
       tk_cap=1408, n=None, tn=None, tk=None, b_spec=None, o_spec=None, o_shape=None, into=None):
    m, k = (a.shape[1], a.shape[0]) if ta else a.shape
    if b_spec is None:
        n = b.shape[0] if tb else b.shape[1]
        assert (b.shape[1] if tb else b.shape[0]) == k, (a.shape, b.shape, ta, tb)
    tm, tn, tk = _tile(m, tm_cap), tn or _tile(n, tn_cap), tk or _tile(k, MM_TK_CAP_TOKENS if ta else tk_cap)
    assert n % tn == 0 and k % tk == 0, (n, tn, k, tk)
    nk = k // tk
    dims = (((0 if ta else 1,), (1 if tb else 0,)), ((), ()))
    has_add, has_norm, has_grad = add is not None, norm_w is not None, norm_grad is not None
    assert not (has_norm or has_grad) or (tn == n and o_spec is None), "the norm epilogues need whole rows"
    n_in = 2 + has_add + has_norm + 3 * has_grad + (into is not None)

    def body(*refs):
        a_ref, b_ref = refs[0], refs[1]
        add_ref = refs[2] if has_add else None
        o_ref = refs[n_in]
        part = _dot(a_ref[...].astype(BF16), b_ref[...].astype(BF16), dims)
        first = pl.program_id(0) == 0

        def finish(r):
            if has_add:
                r = r + add_ref[...].astype(F32)
            if has_grad:
                h_ref, w_ref, prev_ref = refs[2 + has_add:5 + has_add]
                _, vjp = jax.vjp(_f_norm, h_ref[...], w_ref[...])
                r, dw = vjp(r)
                r = r + prev_ref[...]

                @pl.when(first)
                def _():
                    refs[n_in + 1][...] = dw

                @pl.when(jnp.logical_not(first))
                def _():
                    refs[n_in + 1][...] += dw
            o_ref[...] = r.astype(o_ref.dtype)
            if has_norm:
                refs[n_in + 1][...] = _f_norm(r, refs[2 + has_add][...]).astype(BF16)

        if nk == 1:
            finish(part)
            return
        acc = refs[-1]
        kk = pl.program_id(2)

        @pl.when(kk == 0)
        def _():
            acc[...] = part

        @pl.when(kk > 0)
        def _():
            acc[...] += part

        @pl.when(kk == nk - 1)
        def _():
            finish(acc[...])

    a_spec = pl.BlockSpec((tk, tm), lambda i, j, kk: (kk, i)) if ta else pl.BlockSpec((tm, tk), lambda i, j, kk: (i, kk))
    if b_spec is None:
        b_spec = pl.BlockSpec((tn, tk), lambda i, j, kk: (j, kk)) if tb else pl.BlockSpec((tk, tn), lambda i, j, kk: (kk, j))
    plain_o = pl.BlockSpec((tm, tn), lambda i, j, kk: (i, j))
    if o_spec is None:
        o_spec, o_shape = plain_o, (m, n)
    in_specs = [a_spec, b_spec] + ([plain_o] if has_add else [])
    args = (a, b) + ((add,) if has_add else ())
    out_specs, out_shapes = o_spec, jax.ShapeDtypeStruct(tuple(o_shape), out_dtype)
    one_row = pl.BlockSpec((1, n), lambda i, j, kk: (0, 0))
    if has_norm:
        in_specs.append(one_row)
        args += (norm_w,)
        out_specs, out_shapes = [o_spec, plain_o], [out_shapes, jax.ShapeDtypeStruct((m, n), BF16)]
    if has_grad:
        assert not has_norm
        in_specs += [plain_o, one_row, plain_o]
        args += tuple(norm_grad)
        out_specs, out_shapes = [o_spec, one_row], [out_shapes, jax.ShapeDtypeStruct((1, n), F32)]
    aliases = {}
    if into is not None:
        assert into.shape == tuple(o_shape) and into.dtype == out_dtype, (into.shape, o_shape)
        in_specs.append(pl.BlockSpec(memory_space=pl.ANY))
        args += (into,)
        aliases = {n_in - 1: 0}
    return pl.pallas_call(
        body, grid=(m // tm, n // tn, nk), in_specs=in_specs, out_specs=out_specs,
        out_shape=out_shapes, name=name, input_output_aliases=aliases,
        scratch_shapes=[pltpu.VMEM((tm, tn), F32)] if nk > 1 else [],
        compiler_params=_params(("arbitrary" if has_grad else "parallel", "parallel", "arbitrary")),
    )(*args)


def _row_spec(tm, cw, coff):
    return pl.BlockSpec((tm, cw), lambda i, j: (i, j + coff))


def _full_spec(shape):
    return pl.BlockSpec(shape, lambda i, j: (0,) * len(shape))


def tile_map(fn, rows, params, outs, *, tm, ncol, name):
    t = rows[0][0].shape[0]
    nin = len(rows) + len(params)

    def body(*refs):
        res = fn(*[r[...] for r in refs[:nin]])
        res = res if isinstance(res, (tuple, list)) else (res,)
        for o_ref, r in zip(refs[nin:], res):
            o_ref[...] = r.astype(o_ref.dtype)

    in_specs = [_row_spec(tm, cw, coff) for (_, cw, coff) in rows] + [_full_spec(p.shape) for p in params]
    res = pl.pallas_call(
        body, grid=(t // tm, ncol), in_specs=in_specs,
        out_specs=[_row_spec(tm, cw, 0) for (cw, _) in outs],
        out_shape=[jax.ShapeDtypeStruct((t, cw * ncol), dt) for (cw, dt) in outs], name=name,
        compiler_params=_params(("parallel", "parallel")),
    )(*[r[0] for r in rows], *params)
    return res


def tile_vjp(fn, rows, params, cts, *, n_diff, tm, ncol, name, grad_dtypes=None):
    t = rows[0][0].shape[0]
    nr, npar, nct = len(rows), len(params), len(cts)

    def body(*refs):
        vals = [r[...] for r in refs[:nr + npar + nct]]
        diff, rest, pars = vals[:n_diff], vals[n_diff:nr], vals[nr:nr + npar]
        ctv = vals[nr + npar:nr + npar + nct]
        outs_ref = refs[nr + npar + nct:]

        def f(*a):
            res = fn(*a[:n_diff], *rest, *a[n_diff:])
            return tuple(res) if isinstance(res, (tuple, list)) else (res,)

        primal, vjp = jax.vjp(f, *[d.astype(F32) for d in diff], *pars)
        grads = vjp(tuple(c.astype(o.dtype) for c, o in zip(ctv, primal)))
        for q in range(n_diff):
            outs_ref[q][...] = grads[q].astype(outs_ref[q].dtype)
        first = (pl.program_id(0) == 0) & (pl.program_id(1) == 0)
        for q in range(npar):
            o_ref, g = outs_ref[n_diff + q], grads[n_diff + q]

            @pl.when(first)
            def _(o_ref=o_ref, g=g):
                o_ref[...] = g

            @pl.when(jnp.logical_not(first))
            def _(o_ref=o_ref, g=g):
                o_ref[...] += g

    in_specs = [_row_spec(tm, cw, coff) for (_, cw, coff) in rows] + [_full_spec(p.shape) for p in params]
    in_specs += [_row_spec(tm, cw, coff) for (_, cw, coff) in cts]
    args = [r[0] for r in rows] + list(params) + [c[0] for c in cts]
    out_specs = [_row_spec(tm, rows[q][1], 0) for q in range(n_diff)] + [_full_spec(p.shape) for p in params]
    grad_dtypes = grad_dtypes or [F32] * n_diff
    out_shape = [jax.ShapeDtypeStruct((t, rows[q][1] * ncol), grad_dtypes[q]) for q in range(n_diff)]
    out_shape += [jax.ShapeDtypeStruct(p.shape, F32) for p in params]
    res = pl.pallas_call(
        body, grid=(t // tm, ncol), in_specs=in_specs, out_specs=out_specs, out_shape=out_shape, name=name,
        compiler_params=_params(("arbitrary", "arbitrary")),
    )(*args)
    return res[:n_diff], res[n_diff:]


def _silu(x):
    return x * jax.nn.sigmoid(x)


def _f_norm(h, w):
    return h * lax.rsqrt(jnp.mean(h * h, axis=-1, keepdims=True) + EPS) * w


def _f_gnorm(o, z, w):
    return _f_norm(o, w) * _silu(z)


def _f_act(gate, val):
    return _silu(gate) * val


def _f_ple(gl, pe):
    return jax.nn.sigmoid(gl) * pe


def _f_betag(pt, alog, dtb):
    lane = lax.broadcasted_iota(jnp.int32, (1, LANES), 1)
    z = pt + dtb
    softplus = jnp.maximum(z, 0.0) + jnp.log(1.0 + jnp.exp(-jnp.abs(z)))
    g = -jnp.exp(alog) * softplus
    return jnp.where(lane < N_HEADS_A, jax.nn.sigmoid(pt), jnp.where(lane < 2 * N_HEADS_A, g, 0.0))


CONV_TM = 2048
CONV_CW = 1024


def _shift_down(x, prev, s, row):
    rp = jnp.tile(pltpu.roll(prev, s, 0), (x.shape[0] // SUBLANES, 1))
    return jnp.where(row < s, rp, pltpu.roll(x, s, 0))


def _shift_up(x, nxt, s, row):
    tm = x.shape[0]
    rn = jnp.tile(pltpu.roll(nxt, SUBLANES - s, 0), (tm // SUBLANES, 1))
    return jnp.where(row >= tm - s, rn, pltpu.roll(x, tm - s, 0))


def _conv_taps(x, prev, w_ref, cols, row):
    k = w_ref.shape[0]
    y = x * w_ref[pl.ds(k - 1, 1), cols]
    for s in range(1, k):
        y = y + _shift_down(x, prev, s, row) * w_ref[pl.ds(k - 1 - s, 1), cols]
    return y


def _lane_chunks(cw):
    return [slice(cb * LANES, (cb + 1) * LANES) for cb in range(cw // LANES)]


def conv_fwd(x, w, *, name):
    t = x.shape[0]
    k, c = w.shape
    tm, cw = min(CONV_TM, t), CONV_CW
    nb8 = tm // SUBLANES

    def body(x_ref, p_ref, w_ref, o_ref):
        first = pl.program_id(1) == 0
        row = lax.broadcasted_iota(jnp.int32, (tm, LANES), 0)
        for cols in _lane_chunks(cw):
            o_ref[:, cols] = _conv_taps(x_ref[:, cols], jnp.where(first, 0.0, p_ref[:, cols]), w_ref, cols, row)

    return pl.pallas_call(
        body, grid=(c // cw, t // tm),
        in_specs=[pl.BlockSpec((tm, cw), lambda j, i: (i, j)),
                  pl.BlockSpec((SUBLANES, cw), lambda j, i: (jnp.maximum(i * nb8 - 1, 0), j)),
                  pl.BlockSpec((k, cw), lambda j, i: (0, j))],
        out_specs=pl.BlockSpec((tm, cw), lambda j, i: (i, j)),
        out_shape=jax.ShapeDtypeStruct((t, c), F32), name=name,
        compiler_params=_params(("parallel", "parallel")),
    )(x, x, w)


def conv_bwd(dy, x, w, *, name):
    t = x.shape[0]
    k, c = w.shape
    tm, cw = min(CONV_TM, t), CONV_CW
    nb8 = tm // SUBLANES
    ni = t // tm

    def body(dy_ref, dn_ref, x_ref, p_ref, w_ref, dx_ref, dw_ref):
        i = pl.program_id(1)
        first, last = i == 0, i == ni - 1
        row = lax.broadcasted_iota(jnp.int32, (tm, LANES), 0)
        for cols in _lane_chunks(cw):
            dyv, xv = dy_ref[:, cols], x_ref[:, cols]
            nxt = jnp.where(last, 0.0, dn_ref[:, cols])
            prev = jnp.where(first, 0.0, p_ref[:, cols])
            dx = dyv * w_ref[pl.ds(k - 1, 1), cols]
            dws = [jnp.sum(dyv * xv, axis=0, keepdims=True)]
            for s in range(1, k):
                dx = dx + _shift_up(dyv, nxt, s, row) * w_ref[pl.ds(k - 1 - s, 1), cols]
                dws.append(jnp.sum(dyv * _shift_down(xv, prev, s, row), axis=0, keepdims=True))
            dx_ref[:, cols] = dx.astype(dx_ref.dtype)
            for s in range(k):
                @pl.when(first)
                def _(s=s, dws=dws, cols=cols):
                    dw_ref[pl.ds(k - 1 - s, 1), cols] = dws[s]

                @pl.when(jnp.logical_not(first))
                def _(s=s, dws=dws, cols=cols):
                    dw_ref[pl.ds(k - 1 - s, 1), cols] += dws[s]

    return pl.pallas_call(
        body, grid=(c // cw, ni),
        in_specs=[pl.BlockSpec((tm, cw), lambda j, i: (i, j)),
                  pl.BlockSpec((SUBLANES, cw), lambda j, i: (jnp.minimum((i + 1) * nb8, t // SUBLANES - 1), j)),
                  pl.BlockSpec((tm, cw), lambda j, i: (i, j)),
                  pl.BlockSpec((SUBLANES, cw), lambda j, i: (jnp.maximum(i * nb8 - 1, 0), j)),
                  pl.BlockSpec((k, cw), lambda j, i: (0, j))],
        out_specs=[pl.BlockSpec((tm, cw), lambda j, i: (i, j)), pl.BlockSpec((k, cw), lambda j, i: (0, j))],
        out_shape=[jax.ShapeDtypeStruct((t, c), BF16), jax.ShapeDtypeStruct((k, c), F32)], name=name,
        compiler_params=_params(("parallel", "arbitrary")),
    )(dy, dy, x, x, w)


FFN_TM = 1024
FFN_CW = D_FF // 2


def _ffn_specs(t, tm, cw, k):
    ncol = D_FF // cw
    cur = lambda off: pl.BlockSpec((tm, cw), lambda j, i: (i, j + off))
    prev = lambda off, hr: pl.BlockSpec((hr, cw), lambda j, i: (jnp.maximum(i * (tm // hr) - 1, 0), j + off))
    nxt = lambda off, hr: pl.BlockSpec((hr, cw), lambda j, i: (jnp.minimum((i + 1) * (tm // hr), t // hr - 1), j + off))
    taps = lambda off: pl.BlockSpec((k, cw), lambda j, i: (0, j + off))
    return cur, prev, nxt, taps, ncol


def _rows_before(ref, cols, first):
    return jnp.where(first, 0.0, ref[ref.shape[0] - SUBLANES:, cols].astype(F32))


def conv_act_fwd(u, w, *, name):
    t, k = u.shape[0], w.shape[0]
    tm, cw = min(FFN_TM, t), FFN_CW
    cur, prev, _, taps, ncol = _ffn_specs(t, tm, cw, k)

    def body(ug_ref, pg_ref, uv_ref, pv_ref, wg_ref, wv_ref, o_ref):
        first = pl.program_id(1) == 0
        row = lax.broadcasted_iota(jnp.int32, (tm, LANES), 0)
        for cb in range(cw // LANES):
            cols = slice(cb * LANES, (cb + 1) * LANES)
            cg = _conv_taps(ug_ref[:, cols].astype(F32), _rows_before(pg_ref, cols, first), wg_ref, cols, row)
            cv = _conv_taps(uv_ref[:, cols].astype(F32), _rows_before(pv_ref, cols, first), wv_ref, cols, row)
            o_ref[:, cols] = _f_act(cg, cv).astype(o_ref.dtype)

    return pl.pallas_call(
        body, grid=(ncol, t // tm),
        in_specs=[cur(0), prev(0, BF16_ROWS), cur(ncol), prev(ncol, BF16_ROWS), taps(0), taps(ncol)],
        out_specs=cur(0), out_shape=jax.ShapeDtypeStruct((t, D_FF), BF16), name=name,
        compiler_params=_params(("parallel", "parallel")),
    )(u, u, u, u, w, w)


def conv_act_bwd(u, dact, w, *, name):
    t, k = u.shape[0], w.shape[0]
    tm, cw = min(FFN_TM, t), FFN_CW
    cur, prev, nxt, taps, ncol = _ffn_specs(t, tm, cw, k)
    ni = t // tm

    def body(ug_ref, pg_ref, ng_ref, uv_ref, pv_ref, nv_ref, d_ref, dn_ref, wg_ref, wv_ref, dg_ref, dv_ref, dwg_ref, dwv_ref):
        i = pl.program_id(1)
        first, last = i == 0, i == ni - 1
        row = lax.broadcasted_iota(jnp.int32, (tm, LANES), 0)
        row8 = lax.broadcasted_iota(jnp.int32, (SUBLANES, LANES), 0)
        for cb in range(cw // LANES):
            cols = slice(cb * LANES, (cb + 1) * LANES)
            ug, uv = ug_ref[:, cols].astype(F32), uv_ref[:, cols].astype(F32)
            pg, pv = _rows_before(pg_ref, cols, first), _rows_before(pv_ref, cols, first)
            sg = [ug] + [_shift_down(ug, pg, s, row) for s in range(1, k)]
            sv = [uv] + [_shift_down(uv, pv, s, row) for s in range(1, k)]
            taps = lambda xs, w_ref: sum(xs[s] * w_ref[pl.ds(k - 1 - s, 1), cols] for s in range(k))
            _, vjp = jax.vjp(_f_act, taps(sg, wg_ref), taps(sv, wv_ref))
            dcg, dcv = vjp(d_ref[:, cols])
            after = lambda ref: ref[:SUBLANES, cols].astype(F32)
            _, vjp_n = jax.vjp(_f_act, _conv_taps(after(ng_ref), ug[tm - SUBLANES:], wg_ref, cols, row8),
                               _conv_taps(after(nv_ref), uv[tm - SUBLANES:], wv_ref, cols, row8))
            dcgn, dcvn = vjp_n(jnp.where(last, 0.0, dn_ref[:, cols]))
            for dc, dcn, xs, w_ref, dx_ref, dw_ref in ((dcg, dcgn, sg, wg_ref, dg_ref, dwg_ref),
                                                       (dcv, dcvn, sv, wv_ref, dv_ref, dwv_ref)):
                dx = dc * w_ref[pl.ds(k - 1, 1), cols]
                dws = [jnp.sum(dc * xs[0], axis=0, keepdims=True)]
                for s in range(1, k):
                    dx = dx + _shift_up(dc, dcn, s, row) * w_ref[pl.ds(k - 1 - s, 1), cols]
                    dws.append(jnp.sum(dc * xs[s], axis=0, keepdims=True))
                dx_ref[:, cols] = dx.astype(dx_ref.dtype)
                for s in range(k):
                    @pl.when(first)
                    def _(s=s, dw_ref=dw_ref, dws=dws):
                        dw_ref[pl.ds(k - 1 - s, 1), cols] = dws[s]

                    @pl.when(jnp.logical_not(first))
                    def _(s=s, dw_ref=dw_ref, dws=dws):
                        dw_ref[pl.ds(k - 1 - s, 1), cols] += dws[s]

    half = jax.ShapeDtypeStruct((t, D_FF), BF16)
    dwh = jax.ShapeDtypeStruct((k, D_FF), F32)
    return pl.pallas_call(
        body, grid=(ncol, ni),
        in_specs=[cur(0), prev(0, BF16_ROWS), nxt(0, BF16_ROWS), cur(ncol), prev(ncol, BF16_ROWS), nxt(ncol, BF16_ROWS),
                  cur(0), nxt(0, SUBLANES), taps(0), taps(ncol)],
        out_specs=[cur(0), cur(0), taps(0), taps(0)], out_shape=[half, half, dwh, dwh], name=name,
        compiler_params=_params(("parallel", "arbitrary")),
    )(u, u, u, u, u, u, dact, dact, w, w)


def _each(f, *lists):
    return [f(*a) for a in zip(*lists)]


@jax.custom_vjp
def _inv_unit_lower(lms):
    return _inv_blocks(lms)


def _inv_blocks(lms):
    c = lms[0].shape[0]
    ri = lax.broadcasted_iota(jnp.int32, (c, c), 0)
    ci = lax.broadcasted_iota(jnp.int32, (c, c), 1)
    eye = (ri == ci).astype(F32)
    dms = _each(lambda lm: eye - jnp.where((ri >> 1) == (ci >> 1), lm, 0.0), lms)
    for lv in range(1, int(math.log2(c))):
        below = ((ri >> (lv + 1)) == (ci >> (lv + 1))) & ((ri >> lv) != (ci >> lv))
        dbs = _each(lambda dm: dm.astype(BF16), dms)
        ods = _each(lambda lm, db: _dot(jnp.where(below, lm, 0.0).astype(BF16), db).astype(BF16), lms, dbs)
        dms = _each(lambda dm, db, od: dm - _dot(db, od), dms, dbs, ods)
    return dms


def _inv_fwd(lms):
    tms = _inv_blocks(lms)
    return tms, tms


def _inv_bwd(tms, dts):
    tbs = _each(lambda tm: tm.astype(BF16), tms)
    mid = _each(lambda tb, dt: _dot(tb, dt.astype(BF16), TN).astype(BF16), tbs, dts)
    return (_each(lambda m, tb: -_dot(m, tb, NT), mid, tbs),)


_inv_unit_lower.defvjp(_inv_fwd, _inv_bwd)


@jax.custom_vjp
def _inv_known(lms, tms):
    return tms


_inv_known.defvjp(lambda lms, tms: (tms, tms), lambda tms, dts: _inv_bwd(tms, dts) + (_each(jnp.zeros_like, tms),))


def _l2n(x):
    return x * lax.rsqrt(jnp.sum(x * x, axis=-1, keepdims=True) + EPS)


def _prep_fn(cqs, cks, cvs, bg, sel_b, sel_g, tms=None):
    c = cqs[0].shape[0]
    ri = lax.broadcasted_iota(jnp.int32, (c, c), 0)
    ci = lax.broadcasted_iota(jnp.int32, (c, c), 1)
    eye = (ri == ci).astype(F32)
    incl, strict = ci <= ri, ci < ri
    last = lax.broadcasted_iota(jnp.int32, (c, 1), 0) == c - 1
    to_row = lambda col: jnp.sum(col * eye, axis=0, keepdims=True)
    qs = _each(lambda a: _l2n(_silu(a)) * (HEAD_DIM_A ** -0.5), cqs)
    ks = _each(lambda a: _l2n(_silu(a)), cks)
    vbs = _each(lambda a: _silu(a).astype(BF16), cvs)
    betas = _each(lambda m: jnp.sum(bg * m, axis=1, keepdims=True), sel_b)
    gs = _each(lambda m: jnp.sum(bg * m, axis=1, keepdims=True), sel_g)
    gcss = _each(lambda g: jnp.sum(jnp.where(incl, to_row(g), 0.0), axis=1, keepdims=True), gs)
    gtots = _each(lambda gcs: jnp.sum(jnp.where(last, gcs, 0.0), axis=0, keepdims=True), gcss)
    decays = _each(lambda gcs: jnp.exp(jnp.where(incl, gcs - to_row(gcs), NEG)), gcss)
    kbs = _each(lambda k: k.astype(BF16), ks)
    lms = _each(lambda beta, kb, dec: jnp.where(strict, beta * _dot(kb, kb, NT) * dec, 0.0), betas, kbs, decays)
    tms = _inv_unit_lower(lms) if tms is None else _inv_known(lms, tms)
    ams = _each(lambda tm, beta: (tm * to_row(beta)).astype(BF16), tms, betas)
    gams = _each(jnp.exp, gcss)
    u0s = _each(_dot, ams, vbs)
    wks = _each(lambda am, gam, k: _dot(am, (gam * k).astype(BF16)), ams, gams, ks)
    qks = _each(lambda q, kb, dec: _dot(q.astype(BF16), kb, NT) * dec, qs, kbs, decays)
    qds = _each(lambda q, gam: q * gam, qs, gams)
    kds = _each(lambda k, gtot, gcs: k * jnp.exp(gtot - gcs), ks, gtots, gcss)
    gls = _each(lambda gtot: jnp.exp(gtot) * jnp.ones((SUBLANES, LANES), F32), gtots)
    return u0s, wks, qds, kds, qks, gls, tms


def _head_masks(h):
    lane = lax.broadcasted_iota(jnp.int32, (1, LANES), 1)
    return (lane == h).astype(F32), (lane == h + N_HEADS_A).astype(F32)


def _hsl(j):
    return slice(j * HEAD_DIM_A, (j + 1) * HEAD_DIM_A)


def gnorm_fwd(o, zsrc, w, *, name):
    t, width = o.shape
    tm = min(ROW_TM, t)
    zoff = zsrc.shape[1] // width - 1

    def body(o_ref, z_ref, w_ref, out_ref):
        for h in range(N_HEADS_A):
            out_ref[:, _hsl(h)] = _f_gnorm(o_ref[:, _hsl(h)], z_ref[:, _hsl(h)], w_ref[...]).astype(out_ref.dtype)

    rows = pl.BlockSpec((tm, width), lambda i: (i, 0))
    return pl.pallas_call(
        body, grid=(t // tm,),
        in_specs=[rows, pl.BlockSpec((tm, width), lambda i: (i, zoff)), pl.BlockSpec(w.shape, lambda i: (0, 0))],
        out_specs=rows, out_shape=jax.ShapeDtypeStruct((t, width), BF16), name=name, compiler_params=_params(("parallel",)),
    )(o, zsrc, w)


def gnorm_bwd(o, zsrc, w, don, *, name):
    t, width = o.shape
    tm = min(ROW_TM, t)
    zoff = zsrc.shape[1] // width - 1

    def body(o_ref, z_ref, w_ref, d_ref, do_ref, dz_ref, dw_ref):
        dw = jnp.zeros(w.shape, F32)
        for h in range(N_HEADS_A):
            _, vjp = jax.vjp(_f_gnorm, o_ref[:, _hsl(h)], z_ref[:, _hsl(h)], w_ref[...])
            do, dz, dwh = vjp(d_ref[:, _hsl(h)])
            do_ref[:, _hsl(h)] = do.astype(do_ref.dtype)
            dz_ref[:, _hsl(h)] = dz.astype(dz_ref.dtype)
            dw = dw + dwh
        first = pl.program_id(0) == 0

        @pl.when(first)
        def _():
            dw_ref[...] = dw

        @pl.when(jnp.logical_not(first))
        def _():
            dw_ref[...] += dw

    rows = pl.BlockSpec((tm, width), lambda i: (i, 0))
    wspec = pl.BlockSpec(w.shape, lambda i: (0, 0))
    return pl.pallas_call(
        body, grid=(t // tm,),
        in_specs=[rows, pl.BlockSpec((tm, width), lambda i: (i, zoff)), wspec, rows],
        out_specs=[rows, rows, wspec],
        out_shape=[jax.ShapeDtypeStruct((t, width), BF16)] * 2 + [jax.ShapeDtypeStruct(w.shape, F32)], name=name,
        compiler_params=_params(("arbitrary",)),
    )(o, zsrc, w, don)


def delta_prep(cqkv, bg, *, name):
    t = cqkv.shape[0]
    nh, hd, n = N_HEADS_A, HEAD_DIM_A, t // CHUNK

    def body(cq_ref, ck_ref, cv_ref, bg_ref, u0_ref, wk_ref, qd_ref, kd_ref, qk_ref, tm_ref, gl_ref):
        heads = range(nh)
        masks = [_head_masks(j) for j in heads]
        res = _prep_fn([cq_ref[:, _hsl(j)] for j in heads], [ck_ref[:, _hsl(j)] for j in heads],
                       [cv_ref[:, _hsl(j)] for j in heads], bg_ref[...], [m[0] for m in masks], [m[1] for m in masks])
        for o_ref, rs in zip((u0_ref, wk_ref, qd_ref, kd_ref, qk_ref, tm_ref), res[:5] + (res[6],)):
            for j in heads:
                o_ref[:, _hsl(j)] = rs[j].astype(o_ref.dtype)
        for j in heads:
            gl_ref[j * SUBLANES:(j + 1) * SUBLANES, :] = res[5][j]

    blk = lambda off: pl.BlockSpec((CHUNK, nh * hd), lambda i: (i, off))
    res = pl.pallas_call(
        body, grid=(n,),
        in_specs=[blk(0), blk(1), blk(2), pl.BlockSpec((CHUNK, LANES), lambda i: (i, 0))],
        out_specs=[blk(0)] * 6 + [pl.BlockSpec((nh * SUBLANES, LANES), lambda i: (i, 0))],
        out_shape=[jax.ShapeDtypeStruct((t, nh * hd), dt) for dt in (F32, BF16, BF16, BF16, BF16, F32)]
        + [jax.ShapeDtypeStruct((n * nh * SUBLANES, LANES), F32)],
        name=name, compiler_params=_params(("parallel",)),
    )(cqkv, cqkv, cqkv, bg)
    return [*res[:5], res[6]], res[5]


def delta_prep_bwd(cqkv, bg, tms, cts, *, name):
    t = cqkv.shape[0]
    nh, hd, n = N_HEADS_A, HEAD_DIM_A, t // CHUNK

    def body(cq_ref, ck_ref, cv_ref, bg_ref, tm_ref, c0, c1, c2, c3, c4, c5, dc_ref, dbg_ref):
        heads = range(nh)
        masks = [_head_masks(j) for j in heads]
        known = [tm_ref[:, _hsl(j)] for j in heads]
        _, vjp = jax.vjp(lambda a, b, c, d: _prep_fn(a, b, c, d, [m[0] for m in masks], [m[1] for m in masks], known)[:6],
                         [cq_ref[:, _hsl(j)] for j in heads], [ck_ref[:, _hsl(j)] for j in heads],
                         [cv_ref[:, _hsl(j)] for j in heads], bg_ref[...])
        cts = tuple([c[:, _hsl(j)] for j in heads] for c in (c0, c1, c2, c3, c4))
        dqs, dks, dvs, dbg = vjp(cts + ([c5[j * SUBLANES:(j + 1) * SUBLANES, :] for j in heads],))
        for part, ds in enumerate((dqs, dks, dvs)):
            for j in heads:
                dc_ref[:, _hsl(part * nh + j)] = ds[j]
        dbg_ref[...] = dbg

    blk = lambda off: pl.BlockSpec((CHUNK, nh * hd), lambda i: (i, off))
    gl_spec = pl.BlockSpec((nh * SUBLANES, LANES), lambda i: (i, 0))
    bg_spec = pl.BlockSpec((CHUNK, LANES), lambda i: (i, 0))
    return pl.pallas_call(
        body, grid=(n,),
        in_specs=[blk(0), blk(1), blk(2), bg_spec] + [blk(0)] * 6 + [gl_spec],
        out_specs=[pl.BlockSpec((CHUNK, 3 * nh * hd), lambda i: (i, 0)), bg_spec],
        out_shape=[jax.ShapeDtypeStruct((t, 3 * nh * hd), F32), jax.ShapeDtypeStruct((t, LANES), F32)],
        name=name, compiler_params=_params(("parallel",)),
    )(cqkv, cqkv, cqkv, bg, tms, *cts)


def delta_scan(u0, wk, qd, kd, qk, gl, *, name):
    t = u0.shape[0]
    nh, hd, n = N_HEADS_A, HEAD_DIM_A, t // CHUNK

    def body(u0_ref, wk_ref, qd_ref, kd_ref, qk_ref, gl_ref, o_ref, sin_ref, s_ref):
        @pl.when(pl.program_id(0) == 0)
        def _():
            s_ref[...] = jnp.zeros_like(s_ref)

        heads = list(range(nh))
        cols = lambda ref: [ref[:, _hsl(h)].astype(BF16) for h in heads]
        ss = [s_ref[h] for h in heads]
        for h in heads:
            sin_ref[h] = ss[h]
        sbs = _each(lambda s: s.astype(BF16), ss)
        ubs = _each(lambda h, wkb, sb: (u0_ref[:, _hsl(h)] - _dot(wkb, sb)).astype(BF16), heads, cols(wk_ref), sbs)
        os_ = _each(lambda qdb, sb, qkb, ub: _dot(qdb, sb) + _dot(qkb, ub), cols(qd_ref), sbs, cols(qk_ref), ubs)
        sn = _each(lambda h, s, kdb, ub: gl_ref[pl.ds(h * SUBLANES, 1), :] * s + _dot(kdb, ub, TN), heads, ss, cols(kd_ref), ubs)
        for h in heads:
            o_ref[:, _hsl(h)] = os_[h]
            s_ref[h] = sn[h]

    blk = pl.BlockSpec((CHUNK, nh * hd), lambda i: (i, 0))
    return pl.pallas_call(
        body, grid=(n,),
        in_specs=[blk] * 5 + [pl.BlockSpec((nh * SUBLANES, LANES), lambda i: (i, 0))],
        out_specs=[blk, pl.BlockSpec((None, nh, hd, hd), lambda i: (i, 0, 0, 0))],
        out_shape=[jax.ShapeDtypeStruct((t, nh * hd), F32), jax.ShapeDtypeStruct((n, nh, hd, hd), F32)],
        scratch_shapes=[pltpu.VMEM((nh, hd, hd), F32)], name=name,
        compiler_params=_params(("arbitrary",)),
    )(u0, wk, qd, kd, qk, gl)


def delta_scan_bwd(do, u0, wk, qd, kd, qk, gl, s_in, *, name):
    t = u0.shape[0]
    nh, hd, n = N_HEADS_A, HEAD_DIM_A, t // CHUNK

    def body(do_ref, u0_ref, wk_ref, qd_ref, kd_ref, qk_ref, gl_ref, sin_ref,
             du0_ref, dwk_ref, dqd_ref, dkd_ref, dqk_ref, dgl_ref, ds_ref):
        @pl.when(pl.program_id(0) == 0)
        def _():
            ds_ref[...] = jnp.zeros_like(ds_ref)

        corner = (lax.broadcasted_iota(jnp.int32, (SUBLANES, LANES), 0) == 0) & (lax.broadcasted_iota(jnp.int32, (SUBLANES, LANES), 1) == 0)
        heads = list(range(nh))
        cols = lambda ref: [ref[:, _hsl(h)].astype(BF16) for h in heads]
        ss, dss = [sin_ref[h] for h in heads], [ds_ref[h] for h in heads]
        sbs, dsbs = _each(lambda s: s.astype(BF16), ss), _each(lambda d: d.astype(BF16), dss)
        dobs, wkbs, qdbs, kdbs, qkbs = cols(do_ref), cols(wk_ref), cols(qd_ref), cols(kd_ref), cols(qk_ref)
        ubs = _each(lambda h, wkb, sb: (u0_ref[:, _hsl(h)] - _dot(wkb, sb)).astype(BF16), heads, wkbs, sbs)
        dus = _each(lambda qkb, dob, kdb, dsb: _dot(qkb, dob, TN) + _dot(kdb, dsb), qkbs, dobs, kdbs, dsbs)
        dubs = _each(lambda du: du.astype(BF16), dus)
        dwks = _each(lambda dub, sb: -_dot(dub, sb, NT), dubs, sbs)
        dqds = _each(lambda dob, sb: _dot(dob, sb, NT), dobs, sbs)
        dkds = _each(lambda ub, dsb: _dot(ub, dsb, NT), ubs, dsbs)
        dqks = _each(lambda dob, ub: _dot(dob, ub, NT), dobs, ubs)
        dgls = _each(lambda s, d: jnp.sum(jnp.sum(s * d, axis=1, keepdims=True), axis=0, keepdims=True), ss, dss)
        dsn = _each(lambda h, d, qdb, dob, wkb, dub: gl_ref[pl.ds(h * SUBLANES, 1), :] * d + _dot(qdb, dob, TN) - _dot(wkb, dub, TN),
                    heads, dss, qdbs, dobs, wkbs, dubs)
        for h in heads:
            du0_ref[:, _hsl(h)] = dus[h]
            dwk_ref[:, _hsl(h)] = dwks[h]
            dqd_ref[:, _hsl(h)] = dqds[h]
            dkd_ref[:, _hsl(h)] = dkds[h]
            dqk_ref[:, _hsl(h)] = dqks[h]
            dgl_ref[h * SUBLANES:(h + 1) * SUBLANES, :] = jnp.where(corner, dgls[h], 0.0)
            ds_ref[h] = dsn[h]

    blk = pl.BlockSpec((CHUNK, nh * hd), lambda i: (n - 1 - i, 0))
    gl_spec = pl.BlockSpec((nh * SUBLANES, LANES), lambda i: (n - 1 - i, 0))
    return pl.pallas_call(
        body, grid=(n,),
        in_specs=[blk] * 6 + [gl_spec, pl.BlockSpec((None, nh, hd, hd), lambda i: (n - 1 - i, 0, 0, 0))],
        out_specs=[blk] * 5 + [gl_spec],
        out_shape=[jax.ShapeDtypeStruct((t, nh * hd), F32)] * 5 + [jax.ShapeDtypeStruct((n * nh * SUBLANES, LANES), F32)],
        scratch_shapes=[pltpu.VMEM((nh, hd, hd), F32)], name=name,
        compiler_params=_params(("arbitrary",)),
    )(do, u0, wk, qd, kd, qk, gl, s_in)


N_PAIRS = N_HEADS_B // 2
PAIRS_PER_KV = N_PAIRS // N_KV_B


def _psl(j):
    return slice(j * LANES, (j + 1) * LANES)


KV_STEP = 4


def _att_fn(qps, kcs, kps, vcs, vps, sinks, kv0, first):
    w = WINDOW
    lane = lax.broadcasted_iota(jnp.int32, (1, LANES), 1)
    lo = (lane < HEAD_DIM_B).astype(F32)
    qi = lax.broadcasted_iota(jnp.int32, (w, w), 0)
    kj = lax.broadcasted_iota(jnp.int32, (w, w), 1)
    dist_c = (qi - kj).astype(F32)
    valid_c = kj <= qi
    valid_p = (kj > qi) & (first < 0.5)
    bf = lambda xs: [a.astype(BF16) for a in xs]
    kcb, kpb, vcb, vpb = bf(kcs), bf(kps), bf(vcs), bf(vps)
    scale = HEAD_DIM_B ** -0.5
    heads = [(g, j, half) for g in range(len(kcs)) for j in range(PAIRS_PER_KV) for half in range(2)]
    kvs = [g for g, _, _ in heads]
    hmasks = [lo if half == 0 else 1.0 - lo for _, _, half in heads]
    hds = [2.0 * (PAIRS_PER_KV * (kv0 + g) + j) + half for g, j, half in heads]
    slopes = _each(lambda hd: jnp.exp(-(hd + 1.0) * (8.0 / N_HEADS_B * math.log(2.0))), hds)
    snks = _each(lambda hd: jnp.sum(sinks * (lane.astype(F32) == hd).astype(F32), axis=1, keepdims=True), hds)
    qhs = _each(lambda h, hm: (qps[h[0] * PAIRS_PER_KV + h[1]] * hm).astype(BF16), heads, hmasks)
    lcs = _each(lambda qh, g, sl: jnp.where(valid_c, _dot(qh, kcb[g], NT) * scale - sl * dist_c, NEG), qhs, kvs, slopes)
    lps = _each(lambda qh, g, sl: jnp.where(valid_p, _dot(qh, kpb[g], NT) * scale - sl * (dist_c + w), NEG), qhs, kvs, slopes)
    ms = _each(lambda lc, lp, sk: lax.stop_gradient(jnp.maximum(jnp.maximum(jnp.max(lc, axis=1, keepdims=True),
                                                                            jnp.max(lp, axis=1, keepdims=True)), sk)), lcs, lps, snks)
    ecs = _each(lambda lc, m: jnp.exp(lc - m), lcs, ms)
    eps = _each(lambda lp, m: jnp.exp(lp - m), lps, ms)
    invs = _each(lambda ec, ep, sk, m: 1.0 / (jnp.sum(ec, axis=1, keepdims=True) + jnp.sum(ep, axis=1, keepdims=True) + jnp.exp(sk - m)),
                 ecs, eps, snks, ms)
    ohs = _each(lambda ec, ep, inv, g, hm: (_dot((ec * inv).astype(BF16), vcb[g]) + _dot((ep * inv).astype(BF16), vpb[g])) * hm,
                ecs, eps, invs, kvs, hmasks)
    return [ohs[2 * j] + ohs[2 * j + 1] for j in range(len(qps))]


def _scalar11(v):
    return jnp.full((1, 1), v, F32)


def _att_specs(row_of):
    cur = pl.BlockSpec((WINDOW, KV_STEP * LANES), lambda i, kv: (row_of(i), kv))
    prev = pl.BlockSpec((WINDOW, KV_STEP * LANES), lambda i, kv: (jnp.maximum(row_of(i) - 1, 0), kv))
    qs = pl.BlockSpec((WINDOW, KV_STEP * PAIRS_PER_KV * LANES), lambda i, kv: (row_of(i), kv))
    return qs, cur, prev, pl.BlockSpec((1, LANES), lambda i, kv: (0, 0))


def swa_fwd(qsrc, kd, vd, sinks, *, name):
    t = kd.shape[0]
    nb = t // WINDOW
    npair = KV_STEP * PAIRS_PER_KV

    def body(q_ref, kc_ref, kp_ref, vc_ref, vp_ref, s_ref, o_ref):
        first = _scalar11((pl.program_id(0) == 0).astype(F32))
        kv0 = _scalar11((pl.program_id(1) * KV_STEP).astype(F32))
        per_kv = lambda ref: [ref[:, _psl(g)] for g in range(KV_STEP)]
        outs = _att_fn([q_ref[:, _psl(j)] for j in range(npair)], per_kv(kc_ref), per_kv(kp_ref), per_kv(vc_ref), per_kv(vp_ref),
                       s_ref[...], kv0, first)
        for j in range(npair):
            o_ref[:, _psl(j)] = outs[j].astype(o_ref.dtype)

    qs, cur, prev, sk = _att_specs(lambda i: i)
    return pl.pallas_call(
        body, grid=(nb, N_KV_B // KV_STEP), in_specs=[qs, cur, prev, cur, prev, sk],
        out_specs=qs, out_shape=jax.ShapeDtypeStruct((t, N_PAIRS * LANES), BF16), name=name,
        compiler_params=_params(("parallel", "parallel")),
    )(qsrc, kd, kd, vd, vd, sinks)


def swa_bwd(do, qsrc, kd, vd, sinks, *, name):
    t = kd.shape[0]
    nb = t // WINDOW

    npair = KV_STEP * PAIRS_PER_KV

    def body(do_ref, q_ref, kc_ref, kp_ref, vc_ref, vp_ref, s_ref, dq_ref, dk_ref, dv_ref, ds_ref, carry_k, carry_v):
        step, kvg = pl.program_id(0), pl.program_id(1)
        first = _scalar11((step == nb - 1).astype(F32))

        @pl.when((step == 0) & (kvg == 0))
        def _():
            carry_k[...] = jnp.zeros_like(carry_k)
            carry_v[...] = jnp.zeros_like(carry_v)
            ds_ref[...] = jnp.zeros_like(ds_ref)

        kv0 = _scalar11((kvg * KV_STEP).astype(F32))
        per_kv = lambda ref: [ref[:, _psl(g)].astype(F32) for g in range(KV_STEP)]
        _, vjp = jax.vjp(lambda *a: _att_fn(*a, kv0, first), [q_ref[:, _psl(j)].astype(F32) for j in range(npair)],
                         per_kv(kc_ref), per_kv(kp_ref), per_kv(vc_ref), per_kv(vp_ref), s_ref[...])
        dqs, dkc, dkp, dvc, dvp, dsk = vjp([do_ref[:, _psl(j)].astype(F32) for j in range(npair)])
        for j in range(npair):
            dq_ref[:, _psl(j)] = dqs[j].astype(dq_ref.dtype)
        ds_ref[...] += dsk
        fold = lambda g: g + pltpu.roll(g, HEAD_DIM_B, 1)
        for g in range(KV_STEP):
            kv = kvg * KV_STEP + g
            dk_ref[:, _psl(g)] = fold(dkc[g] + carry_k[kv]).astype(dk_ref.dtype)
            dv_ref[:, _psl(g)] = fold(dvc[g] + carry_v[kv]).astype(dv_ref.dtype)
            carry_k[kv] = dkp[g]
            carry_v[kv] = dvp[g]

    qs, cur, prev, sk = _att_specs(lambda i: nb - 1 - i)
    return pl.pallas_call(
        body, grid=(nb, N_KV_B // KV_STEP),
        in_specs=[qs, qs, cur, prev, cur, prev, sk],
        out_specs=[qs, cur, cur, sk],
        out_shape=[jax.ShapeDtypeStruct((t, N_PAIRS * LANES), BF16), jax.ShapeDtypeStruct((t, N_KV_B * LANES), BF16),
                   jax.ShapeDtypeStruct((t, N_KV_B * LANES), BF16), jax.ShapeDtypeStruct((1, LANES), F32)],
        scratch_shapes=[pltpu.VMEM((N_KV_B, WINDOW, LANES), F32), pltpu.VMEM((N_KV_B, WINDOW, LANES), F32)],
        name=name, compiler_params=_params(("arbitrary", "arbitrary")),
    )(do, qsrc, kd, kd, vd, vd, sinks)


def loss_head(h, tgt, w, *, name):
    t, d = h.shape
    tm = min(ROW_TM, t)

    def body(h_ref, t_ref, w_ref, dh_ref, dw_ref, l_ref):
        tg = t_ref[...]

        def f(hv, wv):
            err = _f_norm(hv, wv) - tg
            return 0.5 * jnp.sum(jnp.sum(err * err, axis=1, keepdims=True), axis=0, keepdims=True) * (1.0 / d)

        lv, vjp = jax.vjp(f, h_ref[...], w_ref[...])
        dh, dw = vjp(jnp.ones((1, 1), F32))
        dh_ref[...] = dh
        first = pl.program_id(0) == 0

        @pl.when(first)
        def _():
            dw_ref[...] = dw
            l_ref[...] = lv * jnp.ones((1, LANES), F32)

        @pl.when(jnp.logical_not(first))
        def _():
            dw_ref[...] += dw
            l_ref[...] += lv * jnp.ones((1, LANES), F32)

    rows = pl.BlockSpec((tm, d), lambda i: (i, 0))
    one = lambda c: pl.BlockSpec((1, c), lambda i: (0, 0))
    return pl.pallas_call(
        body, grid=(t // tm,), in_specs=[rows, rows, one(d)], out_specs=[rows, one(d), one(LANES)],
        out_shape=[jax.ShapeDtypeStruct((t, d), F32), jax.ShapeDtypeStruct((1, d), F32), jax.ShapeDtypeStruct((1, LANES), F32)],
        name=name, compiler_params=_params(("arbitrary",)),
    )(h, tgt, w)


def _row_tile(r, cap=256):
    tr = r
    if r % SUBLANES == 0:
        for cand in range(SUBLANES, min(r, cap) + 1, SUBLANES):
            if r % cand == 0:
                tr = cand
    return tr


def _adamw_update(wv, gv, mv, vv):
    mn = ADAM_B1 * mv + (1.0 - ADAM_B1) * gv
    vn = ADAM_B2 * vv + (1.0 - ADAM_B2) * jnp.square(gv)
    m_hat = mn / (1.0 - ADAM_B1 ** ADAM_STEP)
    v_hat = vn / (1.0 - ADAM_B2 ** ADAM_STEP)
    return -ADAM_LR * (m_hat / (jnp.sqrt(v_hat) + ADAM_EPS) + ADAM_WD * wv), mn, vn


def adamw_layers(w, halves, m, v, *, name):
    nl, r, c = w.shape
    tr = _row_tile(r // 2)
    nbh = r // 2 // tr

    def body(w_ref, *rest):
        g_refs, m_ref, v_ref = rest[:2 * nl], rest[2 * nl], rest[2 * nl + 1]
        d_ref, mo_ref, vo_ref, go_ref = rest[2 * nl + 2:]
        layer, i = pl.program_id(0), pl.program_id(1)
        mine = (i < nbh) == (lax.axis_index("c") == 0)
        gv = jnp.where(mine, g_refs[0][...], g_refs[1][...])
        for k in range(1, nl):
            gv = jnp.where(layer == k, jnp.where(mine, g_refs[2 * k][...], g_refs[2 * k + 1][...]), gv)
        d_ref[...], mo_ref[...], vo_ref[...] = _adamw_update(w_ref[...], gv, m_ref[...], v_ref[...])
        go_ref[...] = gv

    spec3 = pl.BlockSpec((None, tr, c), lambda k, i: (k, i, 0))
    g_specs = [pl.BlockSpec((tr, c), lambda k, i, q=q: (jnp.where(k == q, i % nbh, 0), 0)) for q in range(nl) for _ in range(2)]
    return pl.pallas_call(
        body, grid=(nl, r // tr), in_specs=[spec3] + g_specs + [spec3, spec3], out_specs=[spec3] * 4,
        out_shape=[jax.ShapeDtypeStruct((nl, r, c), F32)] * 4, name=name, compiler_params=_params(("arbitrary", "arbitrary")),
    )(w, *[h for pair in halves for h in pair], m, v)


def adamw(w, g, m, v, *, name):
    r, c = w.shape
    tr = _row_tile(r)

    def body(w_ref, g_ref, m_ref, v_ref, d_ref, mo_ref, vo_ref):
        d_ref[...], mo_ref[...], vo_ref[...] = _adamw_update(w_ref[...], g_ref[...], m_ref[...], v_ref[...])

    spec = pl.BlockSpec((tr, c), lambda i: (i, 0))
    return pl.pallas_call(
        body, grid=(r // tr,), in_specs=[spec] * 4, out_specs=[spec] * 3,
        out_shape=[jax.ShapeDtypeStruct((r, c), F32)] * 3, name=name, compiler_params=_params(("parallel",)),
    )(w, g, m, v)


def _place():
    return lax.axis_index("x"), lax.axis_index("y"), lax.axis_index("c")


def allgather8(blk, *, name):
    def body(x_ref, out_ref, send_sems, recv_sems, local_sem):
        x, y, c = _place()
        me = 4 * x + 2 * y + c
        mine = pltpu.make_async_copy(x_ref, out_ref.at[me], local_sem)
        mine.start()
        sent = []
        for k in range(1, N_DEV):
            to = (x ^ ((k >> 2) & 1), y ^ ((k >> 1) & 1), c ^ (k & 1))
            cp = pltpu.make_async_remote_copy(src_ref=x_ref, dst_ref=out_ref.at[me], send_sem=send_sems.at[k - 1],
                                              recv_sem=recv_sems.at[k - 1], device_id=to, device_id_type=MESH)
            cp.start()
            sent.append(cp)
        for k in range(1, N_DEV):
            frm = me ^ k
            pltpu.make_async_remote_copy(src_ref=x_ref, dst_ref=out_ref.at[frm], send_sem=send_sems.at[k - 1],
                                         recv_sem=recv_sems.at[k - 1], device_id=(x, y, c), device_id_type=MESH).wait_recv()
        for cp in sent:
            cp.wait_send()
        mine.wait()

    vm = pl.BlockSpec(memory_space=pltpu.VMEM)
    return pl.pallas_call(
        body, in_specs=[vm], out_specs=vm, out_shape=jax.ShapeDtypeStruct((N_DEV,) + blk.shape, blk.dtype), name=name,
        scratch_shapes=[pltpu.SemaphoreType.DMA((N_DEV - 1,)), pltpu.SemaphoreType.DMA((N_DEV - 1,)), pltpu.SemaphoreType.DMA],
    )(blk)


def _other_chips(x, y):
    return [(1 - x, y), (x, 1 - y), (1 - x, 1 - y)]


def _hbm_call(body, ins, out_shapes, n_sems, name):
    hbm = pl.BlockSpec(memory_space=pl.ANY)
    return pl.pallas_call(
        body, in_specs=[hbm] * len(ins), out_specs=[hbm] * len(out_shapes), out_shape=out_shapes, name=name,
        scratch_shapes=[pltpu.SemaphoreType.DMA((n_sems,)), pltpu.SemaphoreType.DMA((n_sems,))],
    )(*ins)


def _half_rows(c, rh):
    return pl.ds(pl.multiple_of(c * rh, BF16_ROWS), rh)


def gather_units(units, *, name):
    nu = len(units)
    shapes = []
    for arr, layer_major in units:
        r, cols = arr.shape
        shapes.append(jax.ShapeDtypeStruct((2, N_CHIPS, r // 2, cols) if layer_major else (N_CHIPS, r, cols), arr.dtype))

    def body(*refs):
        in_refs, out_refs, send_sems, recv_sems = refs[:nu], refs[nu:2 * nu], refs[2 * nu], refs[2 * nu + 1]
        x, y, c = _place()
        me_chip = 2 * x + y
        sib = (x, y, 1 - c)
        chips = _other_chips(x, y)

        def copy(k, src, dst, to):
            return pltpu.make_async_remote_copy(src_ref=src, dst_ref=dst, send_sem=send_sems.at[k], recv_sem=recv_sems.at[k],
                                                device_id=to, device_id_type=MESH)

        first, passed, landing = [], [], []
        for u, (arr, layer_major) in enumerate(units):
            rh = arr.shape[0] // 2
            out_ref = out_refs[u]
            slot = (lambda chip, half, o=out_ref: o.at[half, chip]) if layer_major else \
                   (lambda chip, half, o=out_ref, rh=rh: o.at[chip, _half_rows(half, rh), :])
            my_half = in_refs[u].at[_half_rows(c, rh), :]
            for j, (cx, cy) in enumerate(chips):
                k = 6 * u + j
                first.append(copy(k, my_half, slot(me_chip, c), (cx, cy, c)))
                passed.append(copy(k + 3, slot(2 * cx + cy, c), slot(2 * cx + cy, c), sib))
                landing.append((copy(k, my_half, slot(2 * cx + cy, c), sib), copy(k + 3, my_half, slot(2 * cx + cy, 1 - c), sib)))
        for cp in first:
            cp.start()
        for (over_ici, _), fwd in zip(landing, passed):
            over_ici.wait_recv()
            fwd.start()
        for _, from_sibling in landing:
            from_sibling.wait_recv()
        for cp in first + passed:
            cp.wait_send()

    return _hbm_call(body, [a for a, _ in units], shapes, 6 * nu, name)


HBM_SPEC = pl.BlockSpec(memory_space=pltpu.HBM)
SEM_SPEC = pl.BlockSpec(memory_space=pltpu.SEMAPHORE)
ORDERED_EFFECT = pltpu.SideEffectType.DATAFLOW_SIDE_EFFECTING


def _split_start(body, srcs, land_shapes, after, *, name):
    nu = len(srcs)
    lands = [lax.empty(s.shape, s.dtype) for s in land_shapes]

    def whole(*refs):
        body(refs[:nu], refs[nu:2 * nu], refs[2 * nu + 1], refs[2 * nu + 2])
        refs[-1][...] = jnp.zeros((SUBLANES, LANES), F32)

    hbm = lambda a: pltpu.with_memory_space_constraint(a, pltpu.HBM)
    sems = pltpu.SemaphoreType.DMA((nu,))
    res = pl.pallas_call(
        whole, name=name, in_specs=[HBM_SPEC] * (2 * nu) + [pl.BlockSpec(memory_space=pl.ANY)],
        out_shape=[sems, sems] + [pltpu.HBM(a.shape, a.dtype) for a in srcs] + [pltpu.HBM(s.shape, s.dtype) for s in land_shapes]
        + [jax.ShapeDtypeStruct((SUBLANES, LANES), F32)],
        out_specs=[SEM_SPEC, SEM_SPEC] + [HBM_SPEC] * (2 * nu) + [pl.BlockSpec(memory_space=pltpu.VMEM)],
        input_output_aliases={q: 2 + q for q in range(2 * nu)},
        compiler_params=pltpu.CompilerParams(has_side_effects=ORDERED_EFFECT),
    )(*[hbm(a) for a in srcs], *[hbm(a) for a in lands], after)
    return res[0], res[1], res[2:2 + nu], res[2 + nu:2 + 2 * nu], res[-1]


def _split_wait(pending, moved, after, *, name):
    send_sems, recv_sems, srcs, lands, _ = pending
    nu = len(srcs)

    def body(*refs):
        land_refs, ssem, rsem = refs[nu:2 * nu], refs[2 * nu], refs[2 * nu + 1]
        x, y, c = _place()
        for u in range(nu):
            size = moved(land_refs[u])
            cp = pltpu.make_async_remote_copy(src_ref=size, dst_ref=size, send_sem=ssem.at[u], recv_sem=rsem.at[u],
                                              device_id=(x, y, c), device_id_type=MESH)
            cp.wait_send()
            cp.wait_recv()

    res = pl.pallas_call(
        body, name=name, in_specs=[HBM_SPEC] * (2 * nu) + [SEM_SPEC, SEM_SPEC, pl.BlockSpec(memory_space=pl.ANY)],
        out_shape=[pltpu.HBM(a.shape, a.dtype) for a in srcs] + [pltpu.HBM(a.shape, a.dtype) for a in lands],
        out_specs=[HBM_SPEC] * (2 * nu), input_output_aliases={q: q for q in range(2 * nu)},
        compiler_params=pltpu.CompilerParams(has_side_effects=ORDERED_EFFECT),
    )(*srcs, *lands, send_sems, recv_sems, after)
    return res[nu:]


def gather_start(shards, after, *, name):
    def body(src_refs, land_refs, send_sems, recv_sems):
        x, y, c = _place()
        for u, shard in enumerate(shards):
            rows = _half_rows(c, shard.shape[0] // 2)
            for cx, cy in _other_chips(x, y):
                for core in range(2):
                    pltpu.make_async_remote_copy(src_ref=src_refs[u].at[rows, :], dst_ref=land_refs[u].at[2 * x + y, rows, :],
                                                 send_sem=send_sems.at[u], recv_sem=recv_sems.at[u], device_id=(cx, cy, core),
                                                 device_id_type=MESH).start()

    return _split_start(body, shards, [jax.ShapeDtypeStruct((N_CHIPS,) + s.shape, s.dtype) for s in shards], after, name=name)


def gather_wait(pending, after, *, name):
    return _split_wait(pending, lambda land: land.at[pl.ds(0, N_CHIPS - 1)], after, name=name)


def scatter_start(pairs, *, name):
    def body(src_refs, land_refs, send_sems, recv_sems):
        x, y, c = _place()
        for u in range(len(pairs)):
            for j, (cx, cy) in enumerate(_other_chips(x, y)):
                pltpu.make_async_remote_copy(src_ref=src_refs[u].at[2 * cx + cy], dst_ref=land_refs[u].at[j], send_sem=send_sems.at[u],
                                             recv_sem=recv_sems.at[u], device_id=(cx, cy, c), device_id_type=MESH).start()

    return _split_start(body, pairs, [jax.ShapeDtypeStruct((N_CHIPS - 1,) + p.shape[1:], p.dtype) for p in pairs], pairs[0], name=name)


def scatter_wait(pending, after, *, name):
    return _split_wait(pending, lambda land: land, after, name=name)


def swap_units(units, *, name):
    nu = len(units)

    def body(*refs):
        g_refs, out_refs, send_sems, recv_sems = refs[:nu], refs[nu:2 * nu], refs[2 * nu], refs[2 * nu + 1]
        x, y, c = _place()
        cps = [pltpu.make_async_remote_copy(src_ref=g_refs[u].at[:, _half_rows(1 - c, units[u].shape[1] // 2), :], dst_ref=out_refs[u],
                                            send_sem=send_sems.at[u], recv_sem=recv_sems.at[u], device_id=(x, y, 1 - c),
                                            device_id_type=MESH) for u in range(nu)]
        for cp in cps:
            cp.start()
        for cp in cps:
            cp.wait()

    shapes = [jax.ShapeDtypeStruct((N_CHIPS, g.shape[1] // 2, g.shape[2]), g.dtype) for g in units]
    return _hbm_call(body, units, shapes, nu, name)


def join_units(units, *, name):
    nu = len(units)

    def body(*refs):
        h_refs, out_refs, send_sems, recv_sems = refs[:nu], refs[nu:2 * nu], refs[2 * nu], refs[2 * nu + 1]
        x, y, c = _place()
        cps = [pltpu.make_async_remote_copy(src_ref=h_refs[u], dst_ref=out_refs[u], send_sem=send_sems.at[u], recv_sem=recv_sems.at[u],
                                            device_id=(x, y, 1 - c), device_id_type=MESH) for u in range(nu)]
        for cp in cps:
            cp.start()
        for cp in cps:
            cp.wait()

    return _hbm_call(body, units, [jax.ShapeDtypeStruct(h.shape, h.dtype) for h in units], nu, name)


def _half_tile(rh):
    tr = rh
    for cand in range(BF16_ROWS, min(rh, 512) + 1, BF16_ROWS):
        if rh % cand == 0:
            tr = cand
    return tr


def pair_add(g, sib, *, name):
    nc, rh, cols = sib.shape
    tr = _half_tile(rh)
    nbh = rh // tr

    def body(g0_ref, g1_ref, s_ref, o_ref):
        mine = jnp.where(lax.axis_index("c") == 0, g0_ref[...], g1_ref[...])
        o_ref[...] = (mine.astype(F32) + s_ref[...].astype(F32)).astype(o_ref.dtype)

    blk = lambda off: pl.BlockSpec((None, tr, cols), lambda j, i: (j, off + i, 0))
    return pl.pallas_call(
        body, grid=(nc, nbh), in_specs=[blk(0), blk(nbh), blk(0)], out_specs=blk(0),
        out_shape=jax.ShapeDtypeStruct(sib.shape, BF16), name=name, compiler_params=_params(("parallel", "parallel")),
    )(g, g, sib)


def chips_add(pair, landed, *, name):
    nc, rh, cols = pair.shape
    tr = _half_tile(rh)

    def body(*refs):
        chip = 2 * lax.axis_index("x") + lax.axis_index("y")
        acc = refs[0][...]
        for j in range(1, nc):
            acc = jnp.where(chip == j, refs[j][...], acc)
        acc = acc.astype(F32)
        for r in refs[nc:-1]:
            acc = acc + r[...].astype(F32)
        refs[-1][...] = acc

    part = lambda q: pl.BlockSpec((None, tr, cols), lambda i, q=q: (q, i, 0))
    return pl.pallas_call(
        body, grid=(rh // tr,), in_specs=[part(q) for q in range(nc)] + [part(q) for q in range(landed.shape[0])],
        out_specs=pl.BlockSpec((tr, cols), lambda i: (i, 0)),
        out_shape=jax.ShapeDtypeStruct((rh, cols), F32), name=name, compiler_params=_params(("parallel",)),
    )(*[pair] * nc, *[landed] * landed.shape[0])


def sum8(g, *, name):
    def body(g_ref, o_ref):
        acc = g_ref[0]
        for d in range(1, N_DEV):
            acc = acc + g_ref[d]
        o_ref[...] = acc

    return pl.pallas_call(body, out_shape=jax.ShapeDtypeStruct(g.shape[1:], F32), name=name)(g)


def _dup_halves(a):
    t = a.shape[0]
    a = a.reshape(t, N_KV_B, HEAD_DIM_B)
    return jnp.concatenate([a, a], axis=-1).reshape(t, N_KV_B * LANES)


def _undup(a):
    t = a.shape[0]
    return a.reshape(t, N_KV_B, LANES)[:, :, :HEAD_DIM_B].reshape(t, N_KV_B * HEAD_DIM_B)


def _lane_pad(v, offset=0):
    return jnp.zeros((1, LANES), F32).at[0, offset:offset + v.shape[0]].set(v)


SHARD_UP = 2 * D_FF // N_CHIPS
SHARD_BIN = (N_HEADS_B + 2 * N_KV_B) * HEAD_DIM_B // N_CHIPS
SHARD_PROJ = D_MODEL // N_CHIPS


def local_step(x, p, tgt, sm, weight, on_grads):
    t = x.shape[0]
    rtm = min(ROW_TM, t)
    hk = N_HEADS_A * HEAD_DIM_A
    qd_b = N_HEADS_B * HEAD_DIM_B
    kd_b = N_KV_B * HEAD_DIM_B
    gs = {}
    norm = lambda h, w, nm: tile_map(_f_norm, [(h, D_MODEL, 0)], [w], [(D_MODEL, BF16)], tm=rtm, ncol=1, name=nm)[0]

    spec = pl.BlockSpec
    mtm = _tile(D_MODEL, MM_TM_CAP)
    p_bf = p.astype(BF16)
    alog_p = _lane_pad(sm["a_log"][0], N_HEADS_A)
    dtb_p = _lane_pad(sm["a_dt_bias"][0], N_HEADS_A)
    sinks_p = _lane_pad(sm["b_sinks"][0])
    nw = lambda name, i: sm[name][i:i + 1]
    by_chip = lambda kdim, ns: dict(tn=ns, tk=kdim, b_spec=spec((None, kdim, ns), lambda r, j, kk: (j, kk, 0)))
    by_chip_t = lambda ndim, ns: dict(n=ndim, tn=ndim, tk=ns, b_spec=spec((None, ndim, ns), lambda r, j, kk: (kk, j, 0)))
    cache = {}

    def wgt(name, i, after):
        if (name, i) not in cache:
            cache[name, i] = weight(name, i, after)
        return cache[name, i]

    saved = []
    h = x
    hn_next = norm(h, nw("norm_mix", 0), "norm_mix0")
    for i in range(DEPTH):
        s = {"h0": h, "hn": hn_next}
        if i % 2 == 0:
            s["pm"] = mm(s["hn"], wgt("a_w_in", i, h), name="a_in")
            tail = (s["pm"], LANES, 4 * hk // LANES)
            s["c"] = conv_fwd(s["pm"], wgt("a_conv", i, h), name="a_conv")
            s["bg"] = tile_map(_f_betag, [tail], [alog_p, dtb_p], [(LANES, F32)], tm=rtm, ncol=1, name="a_betag")[0]
            s["prep"], s["tms"] = delta_prep(s["c"], s["bg"], name="a_prep")
            s["o"], s["s_in"] = delta_scan(*s["prep"], name="a_scan")
            s["on"] = gnorm_fwd(s["o"], s["pm"], sm["a_norm"], name="a_gnorm")
            h, s["hf"] = mm(s["on"], wgt("a_w_out", i, s["on"]), add=h, norm_w=nw("norm_ffn", i), name="a_out")
        else:
            s["pb"] = mm(s["hn"], wgt("b_w_in", i, s["hn"]), name="b_in", out_dtype=BF16, n=N_CHIPS * SHARD_BIN,
                         **by_chip(D_MODEL, SHARD_BIN))
            s["kd"], s["vd"] = _dup_halves(s["pb"][:, qd_b:qd_b + kd_b]), _dup_halves(s["pb"][:, qd_b + kd_b:])
            s["ao"] = swa_fwd(s["pb"], s["kd"], s["vd"], sinks_p, name="b_att")
            h, s["hf"] = mm(s["ao"], wgt("b_w_out", i, s["ao"]), add=h, norm_w=nw("norm_ffn", i), name="b_out")
        s["h1"] = h
        s["u"] = mm(s["hf"], wgt("f_w_up", i, s["hf"]), name=f"f_up{i}", out_dtype=BF16, n=2 * D_FF, tm_cap=2 * MM_TM_CAP,
                    **by_chip(D_MODEL, SHARD_UP))
        s["act"] = conv_act_fwd(s["u"], wgt("f_conv", i, s["hf"]), name=f"f_conv_act{i}")
        h, s["hp"] = mm(s["act"], wgt("f_w_down", i, s["act"]), add=h, norm_w=nw("norm_ple", i), name=f"f_down{i}", tk=D_FF)
        s["h2"] = h
        s["gl"] = mm(s["hp"], wgt("ple_w_gate", i, s["hp"]), name=f"ple_gate{i}")
        s["pe"] = mm(p_bf[i], wgt("ple_w_proj", i, s["hp"]), name=f"ple_proj{i}", n=D_MODEL, **by_chip(PLE_DIM, SHARD_PROJ))
        rows3 = [(h, D_MODEL, 0), (s["gl"], D_MODEL, 0), (s["pe"], D_MODEL, 0)]
        if i + 1 < DEPTH:
            def mix_norm(hv, g, e, wn):
                hn = hv + _f_ple(g, e)
                return hn, _f_norm(hn, wn)
            h, hn_next = tile_map(mix_norm, rows3, [nw("norm_mix", i + 1)], [(D_MODEL, F32), (D_MODEL, BF16)], tm=rtm, ncol=1,
                                  name=f"ple_mix{i}")
        else:
            h = tile_map(lambda hv, g, e: hv + _f_ple(g, e), rows3, [], [(D_MODEL, F32)], tm=rtm, ncol=1, name=f"ple_mix{i}")[0]
        saved.append(s)

    dh, gnf, loss = loss_head(h, tgt, sm["norm_final"][None, :], name="loss_head")
    gs["norm_final"] = gnf[0]

    g_mix, g_ffn, g_ple, g_conv = ([None] * DEPTH for _ in range(4))
    zero = jnp.zeros((1, 1), F32)
    for i in reversed(range(DEPTH)):
        s, gw = saved[i], {}
        by_rows = lambda g: g.reshape(N_CHIPS, g.shape[0] // N_CHIPS, g.shape[1])
        (dgl, dpe), _ = tile_vjp(_f_ple, [(s["gl"], D_MODEL, 0), (s["pe"], D_MODEL, 0)], [], [(dh, D_MODEL, 0)], n_diff=2,
                                 tm=rtm, ncol=1, name=f"ple_mix_bwd{i}", grad_dtypes=[BF16, BF16])
        gw["ple_w_proj"] = mm(p_bf[i], dpe, ta=True, name=f"ple_proj_dw{i}", out_dtype=BF16, tn=SHARD_PROJ,
                              o_shape=(N_CHIPS, PLE_DIM, SHARD_PROJ), o_spec=spec((None, PLE_DIM, SHARD_PROJ), lambda r, j, kk: (j, r, 0)))
        gw["ple_w_gate"] = by_rows(mm(s["hp"], dgl, ta=True, name=f"ple_gate_dw{i}", out_dtype=BF16))
        fused = dict(tb=True, tm_cap=MM_TM_CAP // 2)
        dh, g_ple[i] = mm(dgl, cache["ple_w_gate", i], name=f"ple_gate_dx{i}", norm_grad=(s["h2"], nw("norm_ple", i) + zero, dh), **fused)

        dact = mm(dh, cache["f_w_down", i], tb=True, name=f"f_down_dx{i}")
        gw["f_w_down"] = by_rows(mm(s["act"], dh, ta=True, name=f"f_down_dw{i}", out_dtype=BF16, tm_cap=D_FF // 2))
        du_halves = conv_act_bwd(s["u"], dact, cache["f_conv", i], name=f"f_conv_act_bwd{i}")
        g_conv[i] = jnp.concatenate(du_halves[2:], axis=1)
        dhf = g_up = None
        for half, du in enumerate(du_halves[:2]):
            c0 = half * (N_CHIPS // 2)
            g_up = mm(s["hf"], du, ta=True, name=f"f_up_dw{i}_{half}", out_dtype=BF16, tn=SHARD_UP, into=g_up,
                      o_shape=(N_CHIPS, D_MODEL, SHARD_UP), o_spec=spec((None, mtm, SHARD_UP), lambda r, j, kk, c0=c0: (c0 + j, r, 0)))
            last = dict(norm_grad=(s["h1"], nw("norm_ffn", i), dh), **fused) if half else dict(tb=True)
            dhf = mm(du, cache["f_w_up", i], name=f"f_up_dx{i}_{half}", n=D_MODEL, tn=D_MODEL, tk=SHARD_UP, add=dhf,
                     b_spec=spec((None, D_MODEL, SHARD_UP), lambda r, j, kk, c0=c0: (c0 + kk, j, 0)), **last)
        gw["f_w_up"] = g_up
        dh, g_ffn[i] = dhf
        token, gw = on_grads(i, "ffn", gw), {}
        w_out = cache["a_w_out" if i % 2 == 0 else "b_w_out", i]
        if token is not None:
            w_out = w_out + token[:1, :1].astype(BF16)

        if i % 2 == 0:
            don = mm(dh, w_out, tb=True, name="a_out_dx")
            gw["a_w_out"] = by_rows(mm(s["on"], dh, ta=True, name="a_out_dw", out_dtype=BF16))
            do, dz, gs["a_norm"] = gnorm_bwd(s["o"], s["pm"], sm["a_norm"], don, name="a_gnorm_bwd")
            dprep = delta_scan_bwd(do, *s["prep"], s["s_in"], name="a_scan_bwd")
            dc, dbg = delta_prep_bwd(s["c"], s["bg"], s["tms"], dprep, name="a_prep_bwd")
            (dpt,), (galog, gdtb) = tile_vjp(_f_betag, [(s["pm"], LANES, 4 * hk // LANES)], [alog_p, dtb_p], [(dbg, LANES, 0)], n_diff=1,
                                             tm=rtm, ncol=1, name="a_betag_bwd", grad_dtypes=[BF16])
            gs["a_log"] = galog[:, N_HEADS_A:2 * N_HEADS_A]
            gs["a_dt_bias"] = gdtb[:, N_HEADS_A:2 * N_HEADS_A]
            dqkv, gs["a_conv"] = conv_bwd(dc, s["pm"], cache["a_conv", i], name="a_conv_bwd")
            dpm = jnp.concatenate([dqkv, dz, dpt], axis=1)
            g_in = mm(s["hn"], dpm, ta=True, name="a_in_dw", out_dtype=BF16)[:, :4 * hk + 2 * N_HEADS_A]
            gw["a_w_in"] = g_in.reshape(D_MODEL, N_CHIPS, g_in.shape[1] // N_CHIPS).transpose(1, 0, 2)
            dh, g_mix[i] = mm(dpm, cache["a_w_in", i], name="a_in_dx", norm_grad=(s["h0"], nw("norm_mix", i), dh), **fused)
        else:
            dao = mm(dh, w_out, tb=True, name="b_out_dx")
            gw["b_w_out"] = by_rows(mm(s["ao"], dh, ta=True, name="b_out_dw", out_dtype=BF16))
            dq, dkd, dvd, gsk = swa_bwd(dao, s["pb"], s["kd"], s["vd"], sinks_p, name="b_att_bwd")
            gs["b_sinks"] = gsk[:, :N_HEADS_B]
            dpb = jnp.concatenate([dq, _undup(dkd), _undup(dvd)], axis=1)
            gw["b_w_in"] = mm(s["hn"], dpb, ta=True, name="b_in_dw", out_dtype=BF16, tn=SHARD_BIN,
                              o_shape=(N_CHIPS, D_MODEL, SHARD_BIN), o_spec=spec((None, mtm, SHARD_BIN), lambda r, j, kk: (j, r, 0)))
            dh, g_mix[i] = mm(dpb, cache["b_w_in", i], name="b_in_dx", norm_grad=(s["h0"], nw("norm_mix", i), dh), **fused,
                              **by_chip_t(D_MODEL, SHARD_BIN))
        token = on_grads(i, "mix", gw)
        if token is not None:
            zero = token[:1, :1]

    gs["norm_mix"], gs["norm_ffn"], gs["norm_ple"] = (jnp.concatenate(g, axis=0) for g in (g_mix, g_ffn, g_ple))
    gs["f_conv"] = jnp.stack(g_conv)
    return loss, dh, gs


BIG = ["a_w_in", "a_w_out", "b_w_in", "b_w_out", "f_w_up", "f_w_down", "ple_w_proj", "ple_w_gate"]
LAYERED = {"f_w_up", "f_w_down", "ple_w_proj", "ple_w_gate"}
BY_CHIP = {"b_w_in", "f_w_up", "ple_w_proj"}
LAYER_UNITS = [[("a_w_in", 0), ("a_w_out", 0)] + [(n, 0) for n in sorted(LAYERED)],
               [("b_w_in", 1), ("b_w_out", 1)] + [(n, 1) for n in sorted(LAYERED)]]
CONVS = ["a_conv", "f_conv"]
SMALL = ["norm_mix", "norm_ffn", "norm_ple", "norm_final", "a_log", "a_dt_bias", "a_norm", "b_sinks"]
SMALL_ROWS = 8
CONV_ROWS = 16
CONV_GRAD_ROWS = 48


def _pack_rows(arrs, rows, dtype):
    flat = jnp.concatenate([a.reshape(-1).astype(dtype) for a in arrs])
    return jnp.pad(flat, (0, rows * PACK_COLS - flat.shape[0])).reshape(rows, PACK_COLS)


def _unpack(flat, shapes):
    out, off = [], 0
    for shp in shapes:
        n = math.prod(shp)
        out.append(flat[off:off + n].reshape(shp))
        off += n
    return out


def _pack_small(d, loss=None):
    tail = jnp.concatenate([d["a_log"].reshape(-1), d["a_dt_bias"].reshape(-1), d["a_norm"].reshape(-1), d["b_sinks"].reshape(-1)])
    if loss is not None:
        tail = jnp.concatenate([tail, loss.reshape(-1)[:1]])
    tail = jnp.pad(tail, (0, PACK_COLS - tail.shape[0]))
    return jnp.concatenate([d["norm_mix"], d["norm_ffn"], d["norm_ple"], d["norm_final"][None, :], tail[None, :]], axis=0)


def _unpack_small(a, like):
    out = {"norm_mix": a[0:2], "norm_ffn": a[2:4], "norm_ple": a[4:6], "norm_final": a[6]}
    off = 0
    for nm in ("a_log", "a_dt_bias", "a_norm", "b_sinks"):
        n = like[nm].size
        out[nm] = a[7, off:off + n].reshape(like[nm].shape)
        off += n
    return out, a[7, off]


def _as2d(a):
    return a.reshape(-1, a.shape[-1])


def kernel(x, p, norm_mix, norm_ffn, norm_ple, norm_final, a_w_in, a_conv, a_log, a_dt_bias, a_norm, a_w_out, b_w_in, b_sinks, b_w_out, f_w_up, f_conv, f_w_down, ple_w_proj, ple_w_gate, loss_target, m_norm_mix, m_norm_ffn, m_norm_ple, m_norm_final, m_a_w_in, m_a_conv, m_a_log, m_a_dt_bias, m_a_norm, m_a_w_out, m_b_w_in, m_b_sinks, m_b_w_out, m_f_w_up, m_f_conv, m_f_w_down, m_ple_w_proj, m_ple_w_gate, v_norm_mix, v_norm_ffn, v_norm_ple, v_norm_final, v_a_w_in, v_a_conv, v_a_log, v_a_dt_bias, v_a_norm, v_a_w_out, v_b_w_in, v_b_sinks, v_b_w_out, v_f_w_up, v_f_conv, v_f_w_down, v_ple_w_proj, v_ple_w_gate):
    w = dict(norm_mix=norm_mix, norm_ffn=norm_ffn, norm_ple=norm_ple, norm_final=norm_final, a_w_in=a_w_in, a_conv=a_conv,
             a_log=a_log, a_dt_bias=a_dt_bias, a_norm=a_norm, a_w_out=a_w_out, b_w_in=b_w_in, b_sinks=b_sinks, b_w_out=b_w_out,
             f_w_up=f_w_up, f_conv=f_conv, f_w_down=f_w_down, ple_w_proj=ple_w_proj, ple_w_gate=ple_w_gate)
    m = dict(norm_mix=m_norm_mix, norm_ffn=m_norm_ffn, norm_ple=m_norm_ple, norm_final=m_norm_final, a_w_in=m_a_w_in,
             a_conv=m_a_conv, a_log=m_a_log, a_dt_bias=m_a_dt_bias, a_norm=m_a_norm, a_w_out=m_a_w_out, b_w_in=m_b_w_in,
             b_sinks=m_b_sinks, b_w_out=m_b_w_out, f_w_up=m_f_w_up, f_conv=m_f_conv, f_w_down=m_f_w_down,
             ple_w_proj=m_ple_w_proj, ple_w_gate=m_ple_w_gate)
    v = dict(norm_mix=v_norm_mix, norm_ffn=v_norm_ffn, norm_ple=v_norm_ple, norm_final=v_norm_final, a_w_in=v_a_w_in,
             a_conv=v_a_conv, a_log=v_a_log, a_dt_bias=v_a_dt_bias, a_norm=v_a_norm, a_w_out=v_a_w_out, b_w_in=v_b_w_in,
             b_sinks=v_b_sinks, b_w_out=v_b_w_out, f_w_up=v_f_w_up, f_conv=v_f_conv, f_w_down=v_f_w_down,
             ple_w_proj=v_ple_w_proj, ple_w_gate=v_ple_w_gate)
    xc, yc, cc = _place()
    my_chip = 2 * xc + yc

    shard = {(n, i): w[n][i if n in LAYERED else 0].astype(BF16) for n, i in LAYER_UNITS[0] + LAYER_UNITS[1]}
    first = shard["a_w_in", 0]
    (ga,) = gather_units([(first, False)], name="gather_first")
    ga = lax.dynamic_update_index_in_dim(ga, first, my_chip, 0)
    a_in = jnp.concatenate([ga[j] for j in range(N_CHIPS)], axis=1)
    n_main = 4 * N_HEADS_A * HEAD_DIM_A
    conv_shapes = [w[n].shape for n in CONVS]
    convs = allgather8(_pack_rows([w[n] for n in CONVS], CONV_ROWS, F32), name="gather_convs")
    conv_parts = [_unpack(convs[2 * j].reshape(-1), conv_shapes) for j in range(N_CHIPS)]
    a_conv_full, f_conv_full = (jnp.concatenate([conv_parts[j][q] for j in range(N_CHIPS)], axis=2) for q in range(2))
    ready = {("a_w_in", 0): jnp.pad(a_in, ((0, 0), (0, n_main + LANES - a_in.shape[1]))), ("a_conv", 0): a_conv_full[0], ("f_conv", 0): f_conv_full[0], ("f_conv", 1): f_conv_full[1]}
    later = [[k for k in units if k != ("a_w_in", 0)] for units in LAYER_UNITS]
    pending, after = [], ga
    for layer, keys in enumerate(later):
        pending.append(gather_start([shard[k] for k in keys], after, name=f"gather_start{layer}"))
        after = pending[-1][4]
    sm = {n: w[n] for n in SMALL}
    sm["norm_mix"] = sm["norm_mix"] + after[:1, :1]

    def weight(name, layer, act):
        if (name, layer) not in ready:
            landed = gather_wait(pending[layer], act, name=f"gather_wait{layer}")
            for k, g in zip(later[layer], landed):
                g = lax.dynamic_update_index_in_dim(g, shard[k], my_chip, 0)
                ready[k] = g if k[0] in BY_CHIP else g.reshape(N_CHIPS * g.shape[1], g.shape[2])
        return ready[name, layer]

    pairs, scattered, started = {}, {}, []

    def on_grads(layer, part, gw):
        keys = [k for k in LAYER_UNITS[layer] if (k[0] in LAYERED) == (part == "ffn")]
        from_sib = swap_units([gw[n] for n, _ in keys], name=f"rs_swap_{part}{layer}")
        for (n, _), sib in zip(keys, from_sib):
            pairs[n, layer] = pair_add(gw[n], sib, name=f"rs_add_pair_{n}{layer}")
        started.append((keys, scatter_start([pairs[k] for k in keys], name=f"rs_scatter_start_{part}{layer}"), f"{part}{layer}"))
        return started[-1][1][4]

    loss, grad_x, gs = local_step(x[0], p[:, 0], loss_target[0], sm, weight, on_grads)

    grads, delta, new_m, new_v, g_unit = {}, {}, {}, {}, {}

    def finish(keys, tag):
        halves = [chips_add(pairs[k], scattered[k], name=f"rs_add_chips_{k[0]}{k[1]}") for k in keys]
        g_unit.update(zip(keys, zip(halves, join_units(halves, name=f"rs_join_{tag}"))))
        for n in BIG:
            mine = [(n, i) for i in range(DEPTH) if (n, i) in LAYER_UNITS[i]]
            if n not in delta and all(k in g_unit for k in mine):
                g_layers = [g_unit[k] for k in mine]
                shape3 = (len(g_layers), 2 * g_layers[0][0].shape[0], g_layers[0][0].shape[1])
                res = adamw_layers(w[n].reshape(shape3), g_layers, m[n].reshape(shape3), v[n].reshape(shape3), name=f"adamw_{n}")
                delta[n], new_m[n], new_v[n], grads[n] = (r.reshape(w[n].shape) for r in res)

    last_keys, last_pending, last_tag = started[-1]
    for keys, pend, tag in started[:-1]:
        scattered.update(zip(keys, scatter_wait(pend, last_pending[4], name=f"rs_scatter_wait_{tag}")))
    finish([k for keys, _, _ in started[:-1] for k in keys], "first")

    conv_grads = _pack_rows([gs[n] for n in CONVS], CONV_GRAD_ROWS, F32)
    small_sum = sum8(allgather8(jnp.concatenate([_pack_small(gs, loss), conv_grads]), name="gather_small"), name="sum_small")
    g_sm, loss_sum = _unpack_small(small_sum[:SMALL_ROWS], sm)

    scattered.update(zip(last_keys, scatter_wait(last_pending, small_sum, name=f"rs_scatter_wait_{last_tag}")))
    finish(last_keys, "last")

    for n, full in zip(CONVS, _unpack(small_sum[SMALL_ROWS:].reshape(-1), [gs[n].shape for n in CONVS])):
        g2 = _as2d(lax.dynamic_slice_in_dim(full, my_chip * w[n].shape[-1], w[n].shape[-1], axis=full.ndim - 1))
        d2, m2, v2 = adamw(_as2d(w[n]), g2, _as2d(m[n]), _as2d(v[n]), name=f"adamw_{n}")
        grads[n], delta[n], new_m[n], new_v[n] = (r.reshape(w[n].shape) for r in (g2, d2, m2, v2))
    pk = lambda d: _pack_small(d)
    d2, m2, v2 = adamw(pk(sm), pk(g_sm), pk({n: m[n] for n in SMALL}), pk({n: v[n] for n in SMALL}), name="adamw_small")
    for src, dst in ((d2, delta), (m2, new_m), (v2, new_v)):
        dst.update(_unpack_small(src, sm)[0])
    grads.update(g_sm)

    order = ["norm_mix", "norm_ffn", "norm_ple", "norm_final", "a_w_in", "a_conv", "a_log", "a_dt_bias", "a_norm", "a_w_out",
             "b_w_in", "b_sinks", "b_w_out", "f_w_up", "f_conv", "f_w_down", "ple_w_proj", "ple_w_gate"]
    return (loss_sum, grad_x[None], *[grads[n] for n in order], *[delta[n] for n in order],
            *[new_m[n] for n in order], *[new_v[n] for n in order])
```

```python
import functools
import math

import jax
import jax.numpy as jnp
from jax import lax
from jax.experimental import pallas as pl
from jax.experimental.pallas import tpu as pltpu

F32 = jnp.float32
BF16 = jnp.bfloat16
MESH = pl.DeviceIdType.MESH

D_MODEL = 1024
N_HEADS_A = 8
HEAD_DIM_A = 128
CONV_A = 4
N_HEADS_B = 16
N_KV_B = 4
HEAD_DIM_B = 64
WINDOW = 128
D_FF = 2816
FFN_CONV = 3
PLE_DIM = 256
EPS = 1e-6
DEPTH = 2

ADAM_LR = 0.001
ADAM_B1 = 0.9
ADAM_B2 = 0.999
ADAM_EPS = 1e-08
ADAM_WD = 0.01
ADAM_STEP = 10

LANES = 128
SUBLANES = 8
BF16_ROWS = 16
CHUNK = 128
VMEM_LIMIT = 56 * 1024 * 1024
NEG = -1e30
N_CHIPS = 4
N_DEV = 8
PACK_COLS = 1024


def _params(sem=None):
    return pltpu.CompilerParams(dimension_semantics=sem, vmem_limit_bytes=VMEM_LIMIT)


def _tile(dim, cap):
    if dim % LANES:
        return dim
    best = LANES
    for t in range(LANES, min(dim, cap) + 1, LANES):
        if dim % t == 0:
            best = t
    return best


def _dot(a, b, dims=(((1,), (0,)), ((), ())), precision=None):
    return lax.dot_general(a, b, dims, precision=precision, preferred_element_type=F32)


NN = (((1,), (0,)), ((), ()))
NT = (((1,), (1,)), ((), ()))
TN = (((0,), (0,)), ((), ()))


ROW_TM = 512
MM_TM_CAP = 1024
MM_TK_CAP_TOKENS = 2048


def mm(a, b, *, name, ta=False, tb=False, out_dtype=F32, add=None, norm_w=None, norm_grad=None, tm_cap=MM_TM_CAP, tn_cap=1408,
       tk_cap=1408, n=None, tn=None, tk=None, b_spec=None, o_spec=None, o_shape=None, into=None):
    m, k = (a.shape[1], a.shape[0]) if ta else a.shape
    if b_spec is None:
        n = b.shape[0] if tb else b.shape[1]
        assert (b.shape[1] if tb else b.shape[0]) == k, (a.shape, b.shape, ta, tb)
    tm, tn, tk = _tile(m, tm_cap), tn or _tile(n, tn_cap), tk or _tile(k, MM_TK_CAP_TOKENS if ta else tk_cap)
    assert n % tn == 0 and k % tk == 0, (n, tn, k, tk)
    nk = k // tk
    dims = (((0 if ta else 1,), (1 if tb else 0,)), ((), ()))
    has_add, has_norm, has_grad = add is not None, norm_w is not None, norm_grad is not None
    assert not (has_norm or has_grad) or (tn == n and o_spec is None), "the norm epilogues need whole rows"
    n_in = 2 + has_add + has_norm + 3 * has_grad + (into is not None)

    def body(*refs):
        a_ref, b_ref = refs[0], refs[1]
        add_ref = refs[2] if has_add else None
        o_ref = refs[n_in]
        part = _dot(a_ref[...].astype(BF16), b_ref[...].astype(BF16), dims)
        first = pl.program_id(0) == 0

        def finish(r):
            if has_add:
                r = r + add_ref[...].astype(F32)
            if has_grad:
                h_ref, w_ref, prev_ref = refs[2 + has_add:5 + has_add]
                _, vjp = jax.vjp(_f_norm, h_ref[...], w_ref[...])
                r, dw = vjp(r)
                r = r + prev_ref[...]

                @pl.when(first)
                def _():
                    refs[n_in + 1][...] = dw

                @pl.when(jnp.logical_not(first))
                def _():
                    refs[n_in + 1][...] += dw
            o_ref[...] = r.astype(o_ref.dtype)
            if has_norm:
                refs[n_in + 1][...] = _f_norm(r, refs[2 + has_add][...]).astype(BF16)

        if nk == 1:
            finish(part)
            return
        acc = refs[-1]
        kk = pl.program_id(2)

        @pl.when(kk == 0)
        def _():
            acc[...] = part

        @pl.when(kk > 0)
        def _():
            acc[...] += part

        @pl.when(kk == nk - 1)
        def _():
            finish(acc[...])

    a_spec = pl.BlockSpec((tk, tm), lambda i, j, kk: (kk, i)) if ta else pl.BlockSpec((tm, tk), lambda i, j, kk: (i, kk))
    if b_spec is None:
        b_spec = pl.BlockSpec((tn, tk), lambda i, j, kk: (j, kk)) if tb else pl.BlockSpec((tk, tn), lambda i, j, kk: (kk, j))
    plain_o = pl.BlockSpec((tm, tn), lambda i, j, kk: (i, j))
    if o_spec is None:
        o_spec, o_shape = plain_o, (m, n)
    in_specs = [a_spec, b_spec] + ([plain_o] if has_add else [])
    args = (a, b) + ((add,) if has_add else ())
    out_specs, out_shapes = o_spec, jax.ShapeDtypeStruct(tuple(o_shape), out_dtype)
    one_row = pl.BlockSpec((1, n), lambda i, j, kk: (0, 0))
    if has_norm:
        in_specs.append(one_row)
        args += (norm_w,)
        out_specs, out_shapes = [o_spec, plain_o], [out_shapes, jax.ShapeDtypeStruct((m, n), BF16)]
    if has_grad:
        assert not has_norm
        in_specs += [plain_o, one_row, plain_o]
        args += tuple(norm_grad)
        out_specs, out_shapes = [o_spec, one_row], [out_shapes, jax.ShapeDtypeStruct((1, n), F32)]
    aliases = {}
    if into is not None:
        assert into.shape == tuple(o_shape) and into.dtype == out_dtype, (into.shape, o_shape)
        in_specs.append(pl.BlockSpec(memory_space=pl.ANY))
        args += (into,)
        aliases = {n_in - 1: 0}
    return pl.pallas_call(
        body, grid=(m // tm, n // tn, nk), in_specs=in_specs, out_specs=out_specs,
        out_shape=out_shapes, name=name, input_output_aliases=aliases,
        scratch_shapes=[pltpu.VMEM((tm, tn), F32)] if nk > 1 else [],
        compiler_params=_params(("arbitrary" if has_grad else "parallel", "parallel", "arbitrary")),
    )(*args)


def _row_spec(tm, cw, coff):
    return pl.BlockSpec((tm, cw), lambda i, j: (i, j + coff))


def _full_spec(shape):
    return pl.BlockSpec(shape, lambda i, j: (0,) * len(shape))


def tile_map(fn, rows, params, outs, *, tm, ncol, name):
    t = rows[0][0].shape[0]
    nin = len(rows) + len(params)

    def body(*refs):
        res = fn(*[r[...] for r in refs[:nin]])
        res = res if isinstance(res, (tuple, list)) else (res,)
        for o_ref, r in zip(refs[nin:], res):
            o_ref[...] = r.astype(o_ref.dtype)

    in_specs = [_row_spec(tm, cw, coff) for (_, cw, coff) in rows] + [_full_spec(p.shape) for p in params]
    res = pl.pallas_call(
        body, grid=(t // tm, ncol), in_specs=in_specs,
        out_specs=[_row_spec(tm, cw, 0) for (cw, _) in outs],
        out_shape=[jax.ShapeDtypeStruct((t, cw * ncol), dt) for (cw, dt) in outs], name=name,
        compiler_params=_params(("parallel", "parallel")),
    )(*[r[0] for r in rows], *params)
    return res


def tile_vjp(fn, rows, params, cts, *, n_diff, tm, ncol, name, grad_dtypes=None):
    t = rows[0][0].shape[0]
    nr, npar, nct = len(rows), len(params), len(cts)

    def body(*refs):
        vals = [r[...] for r in refs[:nr + npar + nct]]
        diff, rest, pars = vals[:n_diff], vals[n_diff:nr], vals[nr:nr + npar]
        ctv = vals[nr + npar:nr + npar + nct]
        outs_ref = refs[nr + npar + nct:]

        def f(*a):
            res = fn(*a[:n_diff], *rest, *a[n_diff:])
            return tuple(res) if isinstance(res, (tuple, list)) else (res,)

        primal, vjp = jax.vjp(f, *[d.astype(F32) for d in diff], *pars)
        grads = vjp(tuple(c.astype(o.dtype) for c, o in zip(ctv, primal)))
        for q in range(n_diff):
            outs_ref[q][...] = grads[q].astype(outs_ref[q].dtype)
        first = (pl.program_id(0) == 0) & (pl.program_id(1) == 0)
        for q in range(npar):
            o_ref, g = outs_ref[n_diff + q], grads[n_diff + q]

            @pl.when(first)
            def _(o_ref=o_ref, g=g):
                o_ref[...] = g

            @pl.when(jnp.logical_not(first))
            def _(o_ref=o_ref, g=g):
                o_ref[...] += g

    in_specs = [_row_spec(tm, cw, coff) for (_, cw, coff) in rows] + [_full_spec(p.shape) for p in params]
    in_specs += [_row_spec(tm, cw, coff) for (_, cw, coff) in cts]
    args = [r[0] for r in rows] + list(params) + [c[0] for c in cts]
    out_specs = [_row_spec(tm, rows[q][1], 0) for q in range(n_diff)] + [_full_spec(p.shape) for p in params]
    grad_dtypes = grad_dtypes or [F32] * n_diff
    out_shape = [jax.ShapeDtypeStruct((t, rows[q][1] * ncol), grad_dtypes[q]) for q in range(n_diff)]
    out_shape += [jax.ShapeDtypeStruct(p.shape, F32) for p in params]
    res = pl.pallas_call(
        body, grid=(t // tm, ncol), in_specs=in_specs, out_specs=out_specs, out_shape=out_shape, name=name,
        compiler_params=_params(("arbitrary", "arbitrary")),
    )(*args)
    return res[:n_diff], res[n_diff:]


def _silu(x):
    return x * jax.nn.sigmoid(x)


def _f_norm(h, w):
    return h * lax.rsqrt(jnp.mean(h * h, axis=-1, keepdims=True) + EPS) * w


def _f_gnorm(o, z, w):
    return _f_norm(o, w) * _silu(z)


def _f_act(gate, val):
    return _silu(gate) * val


def _f_ple(gl, pe):
    return jax.nn.sigmoid(gl) * pe


def _f_betag(pt, alog, dtb):
    lane = lax.broadcasted_iota(jnp.int32, (1, LANES), 1)
    z = pt + dtb
    softplus = jnp.maximum(z, 0.0) + jnp.log(1.0 + jnp.exp(-jnp.abs(z)))
    g = -jnp.exp(alog) * softplus
    return jnp.where(lane < N_HEADS_A, jax.nn.sigmoid(pt), jnp.where(lane < 2 * N_HEADS_A, g, 0.0))


CONV_TM = 1024
CONV_CW = 1024


def _shift_down(x, prev, s, row):
    rp = jnp.tile(pltpu.roll(prev, s, 0), (x.shape[0] // SUBLANES, 1))
    return jnp.where(row < s, rp, pltpu.roll(x, s, 0))


def _shift_up(x, nxt, s, row):
    tm = x.shape[0]
    rn = jnp.tile(pltpu.roll(nxt, SUBLANES - s, 0), (tm // SUBLANES, 1))
    return jnp.where(row >= tm - s, rn, pltpu.roll(x, tm - s, 0))


def _conv_taps(x, prev, w_ref, cols, row):
    k = w_ref.shape[0]
    y = x * w_ref[pl.ds(k - 1, 1), cols]
    for s in range(1, k):
        y = y + _shift_down(x, prev, s, row) * w_ref[pl.ds(k - 1 - s, 1), cols]
    return y


def _lane_chunks(cw):
    return [slice(cb * LANES, (cb + 1) * LANES) for cb in range(cw // LANES)]


def conv_fwd(x, w, *, name):
    t = x.shape[0]
    k, c = w.shape
    tm, cw = min(CONV_TM, t), CONV_CW
    nb8 = tm // SUBLANES

    def body(x_ref, p_ref, w_ref, o_ref):
        first = pl.program_id(1) == 0
        row = lax.broadcasted_iota(jnp.int32, (tm, LANES), 0)
        for cols in _lane_chunks(cw):
            o_ref[:, cols] = _conv_taps(x_ref[:, cols], jnp.where(first, 0.0, p_ref[:, cols]), w_ref, cols, row)

    return pl.pallas_call(
        body, grid=(c // cw, t // tm),
        in_specs=[pl.BlockSpec((tm, cw), lambda j, i: (i, j)),
                  pl.BlockSpec((SUBLANES, cw), lambda j, i: (jnp.maximum(i * nb8 - 1, 0), j)),
                  pl.BlockSpec((k, cw), lambda j, i: (0, j))],
        out_specs=pl.BlockSpec((tm, cw), lambda j, i: (i, j)),
        out_shape=jax.ShapeDtypeStruct((t, c), F32), name=name,
        compiler_params=_params(("parallel", "parallel")),
    )(x, x, w)


def conv_bwd(dy, x, w, *, name):
    t = x.shape[0]
    k, c = w.shape
    tm, cw = min(CONV_TM, t), CONV_CW
    nb8 = tm // SUBLANES
    ni = t // tm

    def body(dy_ref, dn_ref, x_ref, p_ref, w_ref, dx_ref, dw_ref):
        i = pl.program_id(1)
        first, last = i == 0, i == ni - 1
        row = lax.broadcasted_iota(jnp.int32, (tm, LANES), 0)
        for cols in _lane_chunks(cw):
            dyv, xv = dy_ref[:, cols], x_ref[:, cols]
            nxt = jnp.where(last, 0.0, dn_ref[:, cols])
            prev = jnp.where(first, 0.0, p_ref[:, cols])
            dx = dyv * w_ref[pl.ds(k - 1, 1), cols]
            dws = [jnp.sum(dyv * xv, axis=0, keepdims=True)]
            for s in range(1, k):
                dx = dx + _shift_up(dyv, nxt, s, row) * w_ref[pl.ds(k - 1 - s, 1), cols]
                dws.append(jnp.sum(dyv * _shift_down(xv, prev, s, row), axis=0, keepdims=True))
            dx_ref[:, cols] = dx.astype(dx_ref.dtype)
            for s in range(k):
                @pl.when(first)
                def _(s=s, dws=dws, cols=cols):
                    dw_ref[pl.ds(k - 1 - s, 1), cols] = dws[s]

                @pl.when(jnp.logical_not(first))
                def _(s=s, dws=dws, cols=cols):
                    dw_ref[pl.ds(k - 1 - s, 1), cols] += dws[s]

    return pl.pallas_call(
        body, grid=(c // cw, ni),
        in_specs=[pl.BlockSpec((tm, cw), lambda j, i: (i, j)),
                  pl.BlockSpec((SUBLANES, cw), lambda j, i: (jnp.minimum((i + 1) * nb8, t // SUBLANES - 1), j)),
                  pl.BlockSpec((tm, cw), lambda j, i: (i, j)),
                  pl.BlockSpec((SUBLANES, cw), lambda j, i: (jnp.maximum(i * nb8 - 1, 0), j)),
                  pl.BlockSpec((k, cw), lambda j, i: (0, j))],
        out_specs=[pl.BlockSpec((tm, cw), lambda j, i: (i, j)), pl.BlockSpec((k, cw), lambda j, i: (0, j))],
        out_shape=[jax.ShapeDtypeStruct((t, c), BF16), jax.ShapeDtypeStruct((k, c), F32)], name=name,
        compiler_params=_params(("parallel", "arbitrary")),
    )(dy, dy, x, x, w)


FFN_TM = 512
FFN_CW = D_FF // 2


def _ffn_specs(t, tm, cw, k):
    ncol = D_FF // cw
    cur = lambda off: pl.BlockSpec((tm, cw), lambda j, i: (i, j + off))
    prev = lambda off, hr: pl.BlockSpec((hr, cw), lambda j, i: (jnp.maximum(i * (tm // hr) - 1, 0), j + off))
    nxt = lambda off, hr: pl.BlockSpec((hr, cw), lambda j, i: (jnp.minimum((i + 1) * (tm // hr), t // hr - 1), j + off))
    taps = lambda off: pl.BlockSpec((k, cw), lambda j, i: (0, j + off))
    return cur, prev, nxt, taps, ncol


def _rows_before(ref, cols, first):
    return jnp.where(first, 0.0, ref[ref.shape[0] - SUBLANES:, cols].astype(F32))


def conv_act_fwd(u, w, *, name):
    t, k = u.shape[0], w.shape[0]
    tm, cw = min(FFN_TM, t), FFN_CW
    cur, prev, _, taps, ncol = _ffn_specs(t, tm, cw, k)

    def body(ug_ref, pg_ref, uv_ref, pv_ref, wg_ref, wv_ref, o_ref):
        first = pl.program_id(1) == 0
        row = lax.broadcasted_iota(jnp.int32, (tm, LANES), 0)
        for cb in range(cw // LANES):
            cols = slice(cb * LANES, (cb + 1) * LANES)
            cg = _conv_taps(ug_ref[:, cols].astype(F32), _rows_before(pg_ref, cols, first), wg_ref, cols, row)
            cv = _conv_taps(uv_ref[:, cols].astype(F32), _rows_before(pv_ref, cols, first), wv_ref, cols, row)
            o_ref[:, cols] = _f_act(cg, cv).astype(o_ref.dtype)

    return pl.pallas_call(
        body, grid=(ncol, t // tm),
        in_specs=[cur(0), prev(0, BF16_ROWS), cur(ncol), prev(ncol, BF16_ROWS), taps(0), taps(ncol)],
        out_specs=cur(0), out_shape=jax.ShapeDtypeStruct((t, D_FF), BF16), name=name,
        compiler_params=_params(("parallel", "parallel")),
    )(u, u, u, u, w, w)


def conv_act_bwd(u, dact, w, *, name):
    t, k = u.shape[0], w.shape[0]
    tm, cw = min(FFN_TM, t), FFN_CW
    cur, prev, nxt, taps, ncol = _ffn_specs(t, tm, cw, k)
    ni = t // tm

    def body(ug_ref, pg_ref, ng_ref, uv_ref, pv_ref, nv_ref, d_ref, dn_ref, wg_ref, wv_ref, dg_ref, dv_ref, dwg_ref, dwv_ref):
        i = pl.program_id(1)
        first, last = i == 0, i == ni - 1
        row = lax.broadcasted_iota(jnp.int32, (tm, LANES), 0)
        row8 = lax.broadcasted_iota(jnp.int32, (SUBLANES, LANES), 0)
        for cb in range(cw // LANES):
            cols = slice(cb * LANES, (cb + 1) * LANES)
            ug, uv = ug_ref[:, cols].astype(F32), uv_ref[:, cols].astype(F32)
            pg, pv = _rows_before(pg_ref, cols, first), _rows_before(pv_ref, cols, first)
            sg = [ug] + [_shift_down(ug, pg, s, row) for s in range(1, k)]
            sv = [uv] + [_shift_down(uv, pv, s, row) for s in range(1, k)]
            taps = lambda xs, w_ref: sum(xs[s] * w_ref[pl.ds(k - 1 - s, 1), cols] for s in range(k))
            _, vjp = jax.vjp(_f_act, taps(sg, wg_ref), taps(sv, wv_ref))
            dcg, dcv = vjp(d_ref[:, cols])
            after = lambda ref: ref[:SUBLANES, cols].astype(F32)
            _, vjp_n = jax.vjp(_f_act, _conv_taps(after(ng_ref), ug[tm - SUBLANES:], wg_ref, cols, row8),
                               _conv_taps(after(nv_ref), uv[tm - SUBLANES:], wv_ref, cols, row8))
            dcgn, dcvn = vjp_n(jnp.where(last, 0.0, dn_ref[:, cols]))
            for dc, dcn, xs, w_ref, dx_ref, dw_ref in ((dcg, dcgn, sg, wg_ref, dg_ref, dwg_ref),
                                                       (dcv, dcvn, sv, wv_ref, dv_ref, dwv_ref)):
                dx = dc * w_ref[pl.ds(k - 1, 1), cols]
                dws = [jnp.sum(dc * xs[0], axis=0, keepdims=True)]
                for s in range(1, k):
                    dx = dx + _shift_up(dc, dcn, s, row) * w_ref[pl.ds(k - 1 - s, 1), cols]
                    dws.append(jnp.sum(dc * xs[s], axis=0, keepdims=True))
                dx_ref[:, cols] = dx.astype(dx_ref.dtype)
                for s in range(k):
                    @pl.when(first)
                    def _(s=s, dw_ref=dw_ref, dws=dws):
                        dw_ref[pl.ds(k - 1 - s, 1), cols] = dws[s]

                    @pl.when(jnp.logical_not(first))
                    def _(s=s, dw_ref=dw_ref, dws=dws):
                        dw_ref[pl.ds(k - 1 - s, 1), cols] += dws[s]

    half = jax.ShapeDtypeStruct((t, D_FF), BF16)
    dwh = jax.ShapeDtypeStruct((k, D_FF), F32)
    return pl.pallas_call(
        body, grid=(ncol, ni),
        in_specs=[cur(0), prev(0, BF16_ROWS), nxt(0, BF16_ROWS), cur(ncol), prev(ncol, BF16_ROWS), nxt(ncol, BF16_ROWS),
                  cur(0), nxt(0, SUBLANES), taps(0), taps(ncol)],
        out_specs=[cur(0), cur(0), taps(0), taps(0)], out_shape=[half, half, dwh, dwh], name=name,
        compiler_params=_params(("parallel", "arbitrary")),
    )(u, u, u, u, u, u, dact, dact, w, w)


def _each(f, *lists):
    return [f(*a) for a in zip(*lists)]


@jax.custom_vjp
def _inv_unit_lower(lms):
    return _inv_blocks(lms)


def _inv_blocks(lms):
    c = lms[0].shape[0]
    ri = lax.broadcasted_iota(jnp.int32, (c, c), 0)
    ci = lax.broadcasted_iota(jnp.int32, (c, c), 1)
    eye = (ri == ci).astype(F32)
    dms = _each(lambda lm: eye - jnp.where((ri >> 1) == (ci >> 1), lm, 0.0), lms)
    for lv in range(1, int(math.log2(c))):
        below = ((ri >> (lv + 1)) == (ci >> (lv + 1))) & ((ri >> lv) != (ci >> lv))
        dbs = _each(lambda dm: dm.astype(BF16), dms)
        ods = _each(lambda lm, db: _dot(jnp.where(below, lm, 0.0).astype(BF16), db).astype(BF16), lms, dbs)
        dms = _each(lambda dm, db, od: dm - _dot(db, od), dms, dbs, ods)
    return dms


def _inv_fwd(lms):
    tms = _inv_blocks(lms)
    return tms, tms


def _inv_bwd(tms, dts):
    tbs = _each(lambda tm: tm.astype(BF16), tms)
    mid = _each(lambda tb, dt: _dot(tb, dt.astype(BF16), TN).astype(BF16), tbs, dts)
    return (_each(lambda m, tb: -_dot(m, tb, NT), mid, tbs),)


_inv_unit_lower.defvjp(_inv_fwd, _inv_bwd)


@jax.custom_vjp
def _inv_known(lms, tms):
    return tms


_inv_known.defvjp(lambda lms, tms: (tms, tms), lambda tms, dts: _inv_bwd(tms, dts) + (_each(jnp.zeros_like, tms),))


def _l2n(x):
    return x * lax.rsqrt(jnp.sum(x * x, axis=-1, keepdims=True) + EPS)


def _prep_fn(cqs, cks, cvs, bg, sel_b, sel_g, tms=None):
    c = cqs[0].shape[0]
    ri = lax.broadcasted_iota(jnp.int32, (c, c), 0)
    ci = lax.broadcasted_iota(jnp.int32, (c, c), 1)
    eye = (ri == ci).astype(F32)
    incl, strict = ci <= ri, ci < ri
    last = lax.broadcasted_iota(jnp.int32, (c, 1), 0) == c - 1
    to_row = lambda col: jnp.sum(col * eye, axis=0, keepdims=True)
    qs = _each(lambda a: _l2n(_silu(a)) * (HEAD_DIM_A ** -0.5), cqs)
    ks = _each(lambda a: _l2n(_silu(a)), cks)
    vbs = _each(lambda a: _silu(a).astype(BF16), cvs)
    betas = _each(lambda m: jnp.sum(bg * m, axis=1, keepdims=True), sel_b)
    gs = _each(lambda m: jnp.sum(bg * m, axis=1, keepdims=True), sel_g)
    gcss = _each(lambda g: jnp.sum(jnp.where(incl, to_row(g), 0.0), axis=1, keepdims=True), gs)
    gtots = _each(lambda gcs: jnp.sum(jnp.where(last, gcs, 0.0), axis=0, keepdims=True), gcss)
    decays = _each(lambda gcs: jnp.exp(jnp.where(incl, gcs - to_row(gcs), NEG)), gcss)
    kbs = _each(lambda k: k.astype(BF16), ks)
    lms = _each(lambda beta, kb, dec: jnp.where(strict, beta * _dot(kb, kb, NT) * dec, 0.0), betas, kbs, decays)
    tms = _inv_unit_lower(lms) if tms is None else _inv_known(lms, tms)
    ams = _each(lambda tm, beta: (tm * to_row(beta)).astype(BF16), tms, betas)
    gams = _each(jnp.exp, gcss)
    u0s = _each(_dot, ams, vbs)
    wks = _each(lambda am, gam, k: _dot(am, (gam * k).astype(BF16)), ams, gams, ks)
    qks = _each(lambda q, kb, dec: _dot(q.astype(BF16), kb, NT) * dec, qs, kbs, decays)
    qds = _each(lambda q, gam: q * gam, qs, gams)
    kds = _each(lambda k, gtot, gcs: k * jnp.exp(gtot - gcs), ks, gtots, gcss)
    gls = _each(lambda gtot: jnp.exp(gtot) * jnp.ones((SUBLANES, LANES), F32), gtots)
    return u0s, wks, qds, kds, qks, gls, tms


def _head_masks(h):
    lane = lax.broadcasted_iota(jnp.int32, (1, LANES), 1)
    return (lane == h).astype(F32), (lane == h + N_HEADS_A).astype(F32)


def _hsl(j):
    return slice(j * HEAD_DIM_A, (j + 1) * HEAD_DIM_A)


def gnorm_fwd(o, zsrc, w, *, name):
    t, width = o.shape
    tm = min(ROW_TM, t)
    zoff = zsrc.shape[1] // width - 1

    def body(o_ref, z_ref, w_ref, out_ref):
        for h in range(N_HEADS_A):
            out_ref[:, _hsl(h)] = _f_gnorm(o_ref[:, _hsl(h)], z_ref[:, _hsl(h)], w_ref[...]).astype(out_ref.dtype)

    rows = pl.BlockSpec((tm, width), lambda i: (i, 0))
    return pl.pallas_call(
        body, grid=(t // tm,),
        in_specs=[rows, pl.BlockSpec((tm, width), lambda i: (i, zoff)), pl.BlockSpec(w.shape, lambda i: (0, 0))],
        out_specs=rows, out_shape=jax.ShapeDtypeStruct((t, width), BF16), name=name, compiler_params=_params(("parallel",)),
    )(o, zsrc, w)


def gnorm_bwd(o, zsrc, w, don, *, name):
    t, width = o.shape
    tm = min(ROW_TM, t)
    zoff = zsrc.shape[1] // width - 1

    def body(o_ref, z_ref, w_ref, d_ref, do_ref, dz_ref, dw_ref):
        dw = jnp.zeros(w.shape, F32)
        for h in range(N_HEADS_A):
            _, vjp = jax.vjp(_f_gnorm, o_ref[:, _hsl(h)], z_ref[:, _hsl(h)], w_ref[...])
            do, dz, dwh = vjp(d_ref[:, _hsl(h)])
            do_ref[:, _hsl(h)] = do.astype(do_ref.dtype)
            dz_ref[:, _hsl(h)] = dz.astype(dz_ref.dtype)
            dw = dw + dwh
        first = pl.program_id(0) == 0

        @pl.when(first)
        def _():
            dw_ref[...] = dw

        @pl.when(jnp.logical_not(first))
        def _():
            dw_ref[...] += dw

    rows = pl.BlockSpec((tm, width), lambda i: (i, 0))
    wspec = pl.BlockSpec(w.shape, lambda i: (0, 0))
    return pl.pallas_call(
        body, grid=(t // tm,),
        in_specs=[rows, pl.BlockSpec((tm, width), lambda i: (i, zoff)), wspec, rows],
        out_specs=[rows, rows, wspec],
        out_shape=[jax.ShapeDtypeStruct((t, width), BF16)] * 2 + [jax.ShapeDtypeStruct(w.shape, F32)], name=name,
        compiler_params=_params(("arbitrary",)),
    )(o, zsrc, w, don)


def delta_prep(cqkv, bg, *, name):
    t = cqkv.shape[0]
    nh, hd, n = N_HEADS_A, HEAD_DIM_A, t // CHUNK

    def body(cq_ref, ck_ref, cv_ref, bg_ref, u0_ref, wk_ref, qd_ref, kd_ref, qk_ref, tm_ref, gl_ref):
        heads = range(nh)
        masks = [_head_masks(j) for j in heads]
        res = _prep_fn([cq_ref[:, _hsl(j)] for j in heads], [ck_ref[:, _hsl(j)] for j in heads],
                       [cv_ref[:, _hsl(j)] for j in heads], bg_ref[...], [m[0] for m in masks], [m[1] for m in masks])
        for o_ref, rs in zip((u0_ref, wk_ref, qd_ref, kd_ref, qk_ref, tm_ref), res[:5] + (res[6],)):
            for j in heads:
                o_ref[:, _hsl(j)] = rs[j].astype(o_ref.dtype)
        for j in heads:
            gl_ref[j * SUBLANES:(j + 1) * SUBLANES, :] = res[5][j]

    blk = lambda off: pl.BlockSpec((CHUNK, nh * hd), lambda i: (i, off))
    res = pl.pallas_call(
        body, grid=(n,),
        in_specs=[blk(0), blk(1), blk(2), pl.BlockSpec((CHUNK, LANES), lambda i: (i, 0))],
        out_specs=[blk(0)] * 6 + [pl.BlockSpec((nh * SUBLANES, LANES), lambda i: (i, 0))],
        out_shape=[jax.ShapeDtypeStruct((t, nh * hd), dt) for dt in (F32, BF16, BF16, BF16, BF16, F32)]
        + [jax.ShapeDtypeStruct((n * nh * SUBLANES, LANES), F32)],
        name=name, compiler_params=_params(("parallel",)),
    )(cqkv, cqkv, cqkv, bg)
    return [*res[:5], res[6]], res[5]


def delta_prep_bwd(cqkv, bg, tms, cts, *, name):
    t = cqkv.shape[0]
    nh, hd, n = N_HEADS_A, HEAD_DIM_A, t // CHUNK

    def body(cq_ref, ck_ref, cv_ref, bg_ref, tm_ref, c0, c1, c2, c3, c4, c5, dc_ref, dbg_ref):
        heads = range(nh)
        masks = [_head_masks(j) for j in heads]
        known = [tm_ref[:, _hsl(j)] for j in heads]
        _, vjp = jax.vjp(lambda a, b, c, d: _prep_fn(a, b, c, d, [m[0] for m in masks], [m[1] for m in masks], known)[:6],
                         [cq_ref[:, _hsl(j)] for j in heads], [ck_ref[:, _hsl(j)] for j in heads],
                         [cv_ref[:, _hsl(j)] for j in heads], bg_ref[...])
        cts = tuple([c[:, _hsl(j)] for j in heads] for c in (c0, c1, c2, c3, c4))
        dqs, dks, dvs, dbg = vjp(cts + ([c5[j * SUBLANES:(j + 1) * SUBLANES, :] for j in heads],))
        for part, ds in enumerate((dqs, dks, dvs)):
            for j in heads:
                dc_ref[:, _hsl(part * nh + j)] = ds[j]
        dbg_ref[...] = dbg

    blk = lambda off: pl.BlockSpec((CHUNK, nh * hd), lambda i: (i, off))
    gl_spec = pl.BlockSpec((nh * SUBLANES, LANES), lambda i: (i, 0))
    bg_spec = pl.BlockSpec((CHUNK, LANES), lambda i: (i, 0))
    return pl.pallas_call(
        body, grid=(n,),
        in_specs=[blk(0), blk(1), blk(2), bg_spec] + [blk(0)] * 6 + [gl_spec],
        out_specs=[pl.BlockSpec((CHUNK, 3 * nh * hd), lambda i: (i, 0)), bg_spec],
        out_shape=[jax.ShapeDtypeStruct((t, 3 * nh * hd), F32), jax.ShapeDtypeStruct((t, LANES), F32)],
        name=name, compiler_params=_params(("parallel",)),
    )(cqkv, cqkv, cqkv, bg, tms, *cts)


def delta_scan(u0, wk, qd, kd, qk, gl, *, name):
    t = u0.shape[0]
    nh, hd, n = N_HEADS_A, HEAD_DIM_A, t // CHUNK

    def body(u0_ref, wk_ref, qd_ref, kd_ref, qk_ref, gl_ref, o_ref, sin_ref, s_ref):
        @pl.when(pl.program_id(0) == 0)
        def _():
            s_ref[...] = jnp.zeros_like(s_ref)

        heads = list(range(nh))
        cols = lambda ref: [ref[:, _hsl(h)].astype(BF16) for h in heads]
        ss = [s_ref[h] for h in heads]
        for h in heads:
            sin_ref[h] = ss[h]
        sbs = _each(lambda s: s.astype(BF16), ss)
        ubs = _each(lambda h, wkb, sb: (u0_ref[:, _hsl(h)] - _dot(wkb, sb)).astype(BF16), heads, cols(wk_ref), sbs)
        os_ = _each(lambda qdb, sb, qkb, ub: _dot(qdb, sb) + _dot(qkb, ub), cols(qd_ref), sbs, cols(qk_ref), ubs)
        sn = _each(lambda h, s, kdb, ub: gl_ref[pl.ds(h * SUBLANES, 1), :] * s + _dot(kdb, ub, TN), heads, ss, cols(kd_ref), ubs)
        for h in heads:
            o_ref[:, _hsl(h)] = os_[h]
            s_ref[h] = sn[h]

    blk = pl.BlockSpec((CHUNK, nh * hd), lambda i: (i, 0))
    return pl.pallas_call(
        body, grid=(n,),
        in_specs=[blk] * 5 + [pl.BlockSpec((nh * SUBLANES, LANES), lambda i: (i, 0))],
        out_specs=[blk, pl.BlockSpec((None, nh, hd, hd), lambda i: (i, 0, 0, 0))],
        out_shape=[jax.ShapeDtypeStruct((t, nh * hd), F32), jax.ShapeDtypeStruct((n, nh, hd, hd), F32)],
        scratch_shapes=[pltpu.VMEM((nh, hd, hd), F32)], name=name,
        compiler_params=_params(("arbitrary",)),
    )(u0, wk, qd, kd, qk, gl)


def delta_scan_bwd(do, u0, wk, qd, kd, qk, gl, s_in, *, name):
    t = u0.shape[0]
    nh, hd, n = N_HEADS_A, HEAD_DIM_A, t // CHUNK

    def body(do_ref, u0_ref, wk_ref, qd_ref, kd_ref, qk_ref, gl_ref, sin_ref,
             du0_ref, dwk_ref, dqd_ref, dkd_ref, dqk_ref, dgl_ref, ds_ref):
        @pl.when(pl.program_id(0) == 0)
        def _():
            ds_ref[...] = jnp.zeros_like(ds_ref)

        corner = (lax.broadcasted_iota(jnp.int32, (SUBLANES, LANES), 0) == 0) & (lax.broadcasted_iota(jnp.int32, (SUBLANES, LANES), 1) == 0)
        heads = list(range(nh))
        cols = lambda ref: [ref[:, _hsl(h)].astype(BF16) for h in heads]
        ss, dss = [sin_ref[h] for h in heads], [ds_ref[h] for h in heads]
        sbs, dsbs = _each(lambda s: s.astype(BF16), ss), _each(lambda d: d.astype(BF16), dss)
        dobs, wkbs, qdbs, kdbs, qkbs = cols(do_ref), cols(wk_ref), cols(qd_ref), cols(kd_ref), cols(qk_ref)
        ubs = _each(lambda h, wkb, sb: (u0_ref[:, _hsl(h)] - _dot(wkb, sb)).astype(BF16), heads, wkbs, sbs)
        dus = _each(lambda qkb, dob, kdb, dsb: _dot(qkb, dob, TN) + _dot(kdb, dsb), qkbs, dobs, kdbs, dsbs)
        dubs = _each(lambda du: du.astype(BF16), dus)
        dwks = _each(lambda dub, sb: -_dot(dub, sb, NT), dubs, sbs)
        dqds = _each(lambda dob, sb: _dot(dob, sb, NT), dobs, sbs)
        dkds = _each(lambda ub, dsb: _dot(ub, dsb, NT), ubs, dsbs)
        dqks = _each(lambda dob, ub: _dot(dob, ub, NT), dobs, ubs)
        dgls = _each(lambda s, d: jnp.sum(jnp.sum(s * d, axis=1, keepdims=True), axis=0, keepdims=True), ss, dss)
        dsn = _each(lambda h, d, qdb, dob, wkb, dub: gl_ref[pl.ds(h * SUBLANES, 1), :] * d + _dot(qdb, dob, TN) - _dot(wkb, dub, TN),
                    heads, dss, qdbs, dobs, wkbs, dubs)
        for h in heads:
            du0_ref[:, _hsl(h)] = dus[h]
            dwk_ref[:, _hsl(h)] = dwks[h]
            dqd_ref[:, _hsl(h)] = dqds[h]
            dkd_ref[:, _hsl(h)] = dkds[h]
            dqk_ref[:, _hsl(h)] = dqks[h]
            dgl_ref[h * SUBLANES:(h + 1) * SUBLANES, :] = jnp.where(corner, dgls[h], 0.0)
            ds_ref[h] = dsn[h]

    blk = pl.BlockSpec((CHUNK, nh * hd), lambda i: (n - 1 - i, 0))
    gl_spec = pl.BlockSpec((nh * SUBLANES, LANES), lambda i: (n - 1 - i, 0))
    return pl.pallas_call(
        body, grid=(n,),
        in_specs=[blk] * 6 + [gl_spec, pl.BlockSpec((None, nh, hd, hd), lambda i: (n - 1 - i, 0, 0, 0))],
        out_specs=[blk] * 5 + [gl_spec],
        out_shape=[jax.ShapeDtypeStruct((t, nh * hd), F32)] * 5 + [jax.ShapeDtypeStruct((n * nh * SUBLANES, LANES), F32)],
        scratch_shapes=[pltpu.VMEM((nh, hd, hd), F32)], name=name,
        compiler_params=_params(("arbitrary",)),
    )(do, u0, wk, qd, kd, qk, gl, s_in)


N_PAIRS = N_HEADS_B // 2
PAIRS_PER_KV = N_PAIRS // N_KV_B


def _psl(j):
    return slice(j * LANES, (j + 1) * LANES)


KV_STEP = 4


def _att_fn(qps, kcs, kps, vcs, vps, sinks, kv0, first):
    w = WINDOW
    lane = lax.broadcasted_iota(jnp.int32, (1, LANES), 1)
    lo = (lane < HEAD_DIM_B).astype(F32)
    qi = lax.broadcasted_iota(jnp.int32, (w, w), 0)
    kj = lax.broadcasted_iota(jnp.int32, (w, w), 1)
    dist_c = (qi - kj).astype(F32)
    valid_c = kj <= qi
    valid_p = (kj > qi) & (first < 0.5)
    bf = lambda xs: [a.astype(BF16) for a in xs]
    kcb, kpb, vcb, vpb = bf(kcs), bf(kps), bf(vcs), bf(vps)
    scale = HEAD_DIM_B ** -0.5
    heads = [(g, j, half) for g in range(len(kcs)) for j in range(PAIRS_PER_KV) for half in range(2)]
    kvs = [g for g, _, _ in heads]
    hmasks = [lo if half == 0 else 1.0 - lo for _, _, half in heads]
    hds = [2.0 * (PAIRS_PER_KV * (kv0 + g) + j) + half for g, j, half in heads]
    slopes = _each(lambda hd: jnp.exp(-(hd + 1.0) * (8.0 / N_HEADS_B * math.log(2.0))), hds)
    snks = _each(lambda hd: jnp.sum(sinks * (lane.astype(F32) == hd).astype(F32), axis=1, keepdims=True), hds)
    qhs = _each(lambda h, hm: (qps[h[0] * PAIRS_PER_KV + h[1]] * hm).astype(BF16), heads, hmasks)
    lcs = _each(lambda qh, g, sl: jnp.where(valid_c, _dot(qh, kcb[g], NT) * scale - sl * dist_c, NEG), qhs, kvs, slopes)
    lps = _each(lambda qh, g, sl: jnp.where(valid_p, _dot(qh, kpb[g], NT) * scale - sl * (dist_c + w), NEG), qhs, kvs, slopes)
    ms = _each(lambda lc, lp, sk: lax.stop_gradient(jnp.maximum(jnp.maximum(jnp.max(lc, axis=1, keepdims=True),
                                                                            jnp.max(lp, axis=1, keepdims=True)), sk)), lcs, lps, snks)
    ecs = _each(lambda lc, m: jnp.exp(lc - m), lcs, ms)
    eps = _each(lambda lp, m: jnp.exp(lp - m), lps, ms)
    invs = _each(lambda ec, ep, sk, m: 1.0 / (jnp.sum(ec, axis=1, keepdims=True) + jnp.sum(ep, axis=1, keepdims=True) + jnp.exp(sk - m)),
                 ecs, eps, snks, ms)
    ohs = _each(lambda ec, ep, inv, g, hm: (_dot((ec * inv).astype(BF16), vcb[g]) + _dot((ep * inv).astype(BF16), vpb[g])) * hm,
                ecs, eps, invs, kvs, hmasks)
    return [ohs[2 * j] + ohs[2 * j + 1] for j in range(len(qps))]


def _scalar11(v):
    return jnp.full((1, 1), v, F32)


def _att_specs(row_of):
    cur = pl.BlockSpec((WINDOW, KV_STEP * LANES), lambda i, kv: (row_of(i), kv))
    prev = pl.BlockSpec((WINDOW, KV_STEP * LANES), lambda i, kv: (jnp.maximum(row_of(i) - 1, 0), kv))
    qs = pl.BlockSpec((WINDOW, KV_STEP * PAIRS_PER_KV * LANES), lambda i, kv: (row_of(i), kv))
    return qs, cur, prev, pl.BlockSpec((1, LANES), lambda i, kv: (0, 0))


def swa_fwd(qsrc, kd, vd, sinks, *, name):
    t = kd.shape[0]
    nb = t // WINDOW
    npair = KV_STEP * PAIRS_PER_KV

    def body(q_ref, kc_ref, kp_ref, vc_ref, vp_ref, s_ref, o_ref):
        first = _scalar11((pl.program_id(0) == 0).astype(F32))
        kv0 = _scalar11((pl.program_id(1) * KV_STEP).astype(F32))
        per_kv = lambda ref: [ref[:, _psl(g)] for g in range(KV_STEP)]
        outs = _att_fn([q_ref[:, _psl(j)] for j in range(npair)], per_kv(kc_ref), per_kv(kp_ref), per_kv(vc_ref), per_kv(vp_ref),
                       s_ref[...], kv0, first)
        for j in range(npair):
            o_ref[:, _psl(j)] = outs[j].astype(o_ref.dtype)

    qs, cur, prev, sk = _att_specs(lambda i: i)
    return pl.pallas_call(
        body, grid=(nb, N_KV_B // KV_STEP), in_specs=[qs, cur, prev, cur, prev, sk],
        out_specs=qs, out_shape=jax.ShapeDtypeStruct((t, N_PAIRS * LANES), BF16), name=name,
        compiler_params=_params(("parallel", "parallel")),
    )(qsrc, kd, kd, vd, vd, sinks)


def swa_bwd(do, qsrc, kd, vd, sinks, *, name):
    t = kd.shape[0]
    nb = t // WINDOW

    npair = KV_STEP * PAIRS_PER_KV

    def body(do_ref, q_ref, kc_ref, kp_ref, vc_ref, vp_ref, s_ref, dq_ref, dk_ref, dv_ref, ds_ref, carry_k, carry_v):
        step, kvg = pl.program_id(0), pl.program_id(1)
        first = _scalar11((step == nb - 1).astype(F32))

        @pl.when((step == 0) & (kvg == 0))
        def _():
            carry_k[...] = jnp.zeros_like(carry_k)
            carry_v[...] = jnp.zeros_like(carry_v)
            ds_ref[...] = jnp.zeros_like(ds_ref)

        kv0 = _scalar11((kvg * KV_STEP).astype(F32))
        per_kv = lambda ref: [ref[:, _psl(g)].astype(F32) for g in range(KV_STEP)]
        _, vjp = jax.vjp(lambda *a: _att_fn(*a, kv0, first), [q_ref[:, _psl(j)].astype(F32) for j in range(npair)],
                         per_kv(kc_ref), per_kv(kp_ref), per_kv(vc_ref), per_kv(vp_ref), s_ref[...])
        dqs, dkc, dkp, dvc, dvp, dsk = vjp([do_ref[:, _psl(j)].astype(F32) for j in range(npair)])
        for j in range(npair):
            dq_ref[:, _psl(j)] = dqs[j].astype(dq_ref.dtype)
        ds_ref[...] += dsk
        fold = lambda g: g + pltpu.roll(g, HEAD_DIM_B, 1)
        for g in range(KV_STEP):
            kv = kvg * KV_STEP + g
            dk_ref[:, _psl(g)] = fold(dkc[g] + carry_k[kv]).astype(dk_ref.dtype)
            dv_ref[:, _psl(g)] = fold(dvc[g] + carry_v[kv]).astype(dv_ref.dtype)
            carry_k[kv] = dkp[g]
            carry_v[kv] = dvp[g]

    qs, cur, prev, sk = _att_specs(lambda i: nb - 1 - i)
    return pl.pallas_call(
        body, grid=(nb, N_KV_B // KV_STEP),
        in_specs=[qs, qs, cur, prev, cur, prev, sk],
        out_specs=[qs, cur, cur, sk],
        out_shape=[jax.ShapeDtypeStruct((t, N_PAIRS * LANES), BF16), jax.ShapeDtypeStruct((t, N_KV_B * LANES), BF16),
                   jax.ShapeDtypeStruct((t, N_KV_B * LANES), BF16), jax.ShapeDtypeStruct((1, LANES), F32)],
        scratch_shapes=[pltpu.VMEM((N_KV_B, WINDOW, LANES), F32), pltpu.VMEM((N_KV_B, WINDOW, LANES), F32)],
        name=name, compiler_params=_params(("arbitrary", "arbitrary")),
    )(do, qsrc, kd, kd, vd, vd, sinks)


def loss_head(h, tgt, w, *, name):
    t, d = h.shape
    tm = min(ROW_TM, t)

    def body(h_ref, t_ref, w_ref, dh_ref, dw_ref, l_ref):
        tg = t_ref[...]

        def f(hv, wv):
            err = _f_norm(hv, wv) - tg
            return 0.5 * jnp.sum(jnp.sum(err * err, axis=1, keepdims=True), axis=0, keepdims=True) * (1.0 / d)

        lv, vjp = jax.vjp(f, h_ref[...], w_ref[...])
        dh, dw = vjp(jnp.ones((1, 1), F32))
        dh_ref[...] = dh
        first = pl.program_id(0) == 0

        @pl.when(first)
        def _():
            dw_ref[...] = dw
            l_ref[...] = lv * jnp.ones((1, LANES), F32)

        @pl.when(jnp.logical_not(first))
        def _():
            dw_ref[...] += dw
            l_ref[...] += lv * jnp.ones((1, LANES), F32)

    rows = pl.BlockSpec((tm, d), lambda i: (i, 0))
    one = lambda c: pl.BlockSpec((1, c), lambda i: (0, 0))
    return pl.pallas_call(
        body, grid=(t // tm,), in_specs=[rows, rows, one(d)], out_specs=[rows, one(d), one(LANES)],
        out_shape=[jax.ShapeDtypeStruct((t, d), F32), jax.ShapeDtypeStruct((1, d), F32), jax.ShapeDtypeStruct((1, LANES), F32)],
        name=name, compiler_params=_params(("arbitrary",)),
    )(h, tgt, w)


def _row_tile(r, cap=256):
    tr = r
    if r % SUBLANES == 0:
        for cand in range(SUBLANES, min(r, cap) + 1, SUBLANES):
            if r % cand == 0:
                tr = cand
    return tr


def _adamw_update(wv, gv, mv, vv):
    mn = ADAM_B1 * mv + (1.0 - ADAM_B1) * gv
    vn = ADAM_B2 * vv + (1.0 - ADAM_B2) * jnp.square(gv)
    m_hat = mn / (1.0 - ADAM_B1 ** ADAM_STEP)
    v_hat = vn / (1.0 - ADAM_B2 ** ADAM_STEP)
    return -ADAM_LR * (m_hat / (jnp.sqrt(v_hat) + ADAM_EPS) + ADAM_WD * wv), mn, vn


def adamw_layers(w, halves, m, v, *, name):
    nl, r, c = w.shape
    tr = _row_tile(r // 2)
    nbh = r // 2 // tr

    def body(w_ref, *rest):
        g_refs, m_ref, v_ref = rest[:2 * nl], rest[2 * nl], rest[2 * nl + 1]
        d_ref, mo_ref, vo_ref, go_ref = rest[2 * nl + 2:]
        layer, i = pl.program_id(0), pl.program_id(1)
        mine = (i < nbh) == (lax.axis_index("c") == 0)
        gv = jnp.where(mine, g_refs[0][...], g_refs[1][...])
        for k in range(1, nl):
            gv = jnp.where(layer == k, jnp.where(mine, g_refs[2 * k][...], g_refs[2 * k + 1][...]), gv)
        d_ref[...], mo_ref[...], vo_ref[...] = _adamw_update(w_ref[...], gv, m_ref[...], v_ref[...])
        go_ref[...] = gv

    spec3 = pl.BlockSpec((None, tr, c), lambda k, i: (k, i, 0))
    g_specs = [pl.BlockSpec((tr, c), lambda k, i, q=q: (jnp.where(k == q, i % nbh, 0), 0)) for q in range(nl) for _ in range(2)]
    return pl.pallas_call(
        body, grid=(nl, r // tr), in_specs=[spec3] + g_specs + [spec3, spec3], out_specs=[spec3] * 4,
        out_shape=[jax.ShapeDtypeStruct((nl, r, c), F32)] * 4, name=name, compiler_params=_params(("arbitrary", "arbitrary")),
    )(w, *[h for pair in halves for h in pair], m, v)


def adamw(w, g, m, v, *, name):
    r, c = w.shape
    tr = _row_tile(r)

    def body(w_ref, g_ref, m_ref, v_ref, d_ref, mo_ref, vo_ref):
        d_ref[...], mo_ref[...], vo_ref[...] = _adamw_update(w_ref[...], g_ref[...], m_ref[...], v_ref[...])

    spec = pl.BlockSpec((tr, c), lambda i: (i, 0))
    return pl.pallas_call(
        body, grid=(r // tr,), in_specs=[spec] * 4, out_specs=[spec] * 3,
        out_shape=[jax.ShapeDtypeStruct((r, c), F32)] * 3, name=name, compiler_params=_params(("parallel",)),
    )(w, g, m, v)


def _place():
    return lax.axis_index("x"), lax.axis_index("y"), lax.axis_index("c")


def allgather8(blk, *, name):
    def body(x_ref, out_ref, send_sems, recv_sems, local_sem):
        x, y, c = _place()
        me = 4 * x + 2 * y + c
        mine = pltpu.make_async_copy(x_ref, out_ref.at[me], local_sem)
        mine.start()
        sent = []
        for k in range(1, N_DEV):
            to = (x ^ ((k >> 2) & 1), y ^ ((k >> 1) & 1), c ^ (k & 1))
            cp = pltpu.make_async_remote_copy(src_ref=x_ref, dst_ref=out_ref.at[me], send_sem=send_sems.at[k - 1],
                                              recv_sem=recv_sems.at[k - 1], device_id=to, device_id_type=MESH)
            cp.start()
            sent.append(cp)
        for k in range(1, N_DEV):
            frm = me ^ k
            pltpu.make_async_remote_copy(src_ref=x_ref, dst_ref=out_ref.at[frm], send_sem=send_sems.at[k - 1],
                                         recv_sem=recv_sems.at[k - 1], device_id=(x, y, c), device_id_type=MESH).wait_recv()
        for cp in sent:
            cp.wait_send()
        mine.wait()

    vm = pl.BlockSpec(memory_space=pltpu.VMEM)
    return pl.pallas_call(
        body, in_specs=[vm], out_specs=vm, out_shape=jax.ShapeDtypeStruct((N_DEV,) + blk.shape, blk.dtype), name=name,
        scratch_shapes=[pltpu.SemaphoreType.DMA((N_DEV - 1,)), pltpu.SemaphoreType.DMA((N_DEV - 1,)), pltpu.SemaphoreType.DMA],
    )(blk)


def _other_chips(x, y):
    return [(1 - x, y), (x, 1 - y), (1 - x, 1 - y)]


def _hbm_call(body, ins, out_shapes, n_sems, name):
    hbm = pl.BlockSpec(memory_space=pl.ANY)
    return pl.pallas_call(
        body, in_specs=[hbm] * len(ins), out_specs=[hbm] * len(out_shapes), out_shape=out_shapes, name=name,
        scratch_shapes=[pltpu.SemaphoreType.DMA((n_sems,)), pltpu.SemaphoreType.DMA((n_sems,))],
    )(*ins)


def _half_rows(c, rh):
    return pl.ds(pl.multiple_of(c * rh, BF16_ROWS), rh)


def gather_units(units, *, name):
    nu = len(units)
    shapes = []
    for arr, layer_major in units:
        r, cols = arr.shape
        shapes.append(jax.ShapeDtypeStruct((2, N_CHIPS, r // 2, cols) if layer_major else (N_CHIPS, r, cols), arr.dtype))

    def body(*refs):
        in_refs, out_refs, send_sems, recv_sems = refs[:nu], refs[nu:2 * nu], refs[2 * nu], refs[2 * nu + 1]
        x, y, c = _place()
        me_chip = 2 * x + y
        sib = (x, y, 1 - c)
        chips = _other_chips(x, y)

        def copy(k, src, dst, to):
            return pltpu.make_async_remote_copy(src_ref=src, dst_ref=dst, send_sem=send_sems.at[k], recv_sem=recv_sems.at[k],
                                                device_id=to, device_id_type=MESH)

        first, passed, landing = [], [], []
        for u, (arr, layer_major) in enumerate(units):
            rh = arr.shape[0] // 2
            out_ref = out_refs[u]
            slot = (lambda chip, half, o=out_ref: o.at[half, chip]) if layer_major else \
                   (lambda chip, half, o=out_ref, rh=rh: o.at[chip, _half_rows(half, rh), :])
            my_half = in_refs[u].at[_half_rows(c, rh), :]
            for j, (cx, cy) in enumerate(chips):
                k = 6 * u + j
                first.append(copy(k, my_half, slot(me_chip, c), (cx, cy, c)))
                passed.append(copy(k + 3, slot(2 * cx + cy, c), slot(2 * cx + cy, c), sib))
                landing.append((copy(k, my_half, slot(2 * cx + cy, c), sib), copy(k + 3, my_half, slot(2 * cx + cy, 1 - c), sib)))
        for cp in first:
            cp.start()
        for (over_ici, _), fwd in zip(landing, passed):
            over_ici.wait_recv()
            fwd.start()
        for _, from_sibling in landing:
            from_sibling.wait_recv()
        for cp in first + passed:
            cp.wait_send()

    return _hbm_call(body, [a for a, _ in units], shapes, 6 * nu, name)


HBM_SPEC = pl.BlockSpec(memory_space=pltpu.HBM)
SEM_SPEC = pl.BlockSpec(memory_space=pltpu.SEMAPHORE)
ORDERED_EFFECT = pltpu.SideEffectType.DATAFLOW_SIDE_EFFECTING


def _split_start(body, srcs, land_shapes, after, *, name):
    nu = len(srcs)
    lands = [lax.empty(s.shape, s.dtype) for s in land_shapes]

    def whole(*refs):
        body(refs[:nu], refs[nu:2 * nu], refs[2 * nu + 1], refs[2 * nu + 2])
        refs[-1][...] = jnp.zeros((SUBLANES, LANES), F32)

    hbm = lambda a: pltpu.with_memory_space_constraint(a, pltpu.HBM)
    sems = pltpu.SemaphoreType.DMA((nu,))
    res = pl.pallas_call(
        whole, name=name, in_specs=[HBM_SPEC] * (2 * nu) + [pl.BlockSpec(memory_space=pl.ANY)],
        out_shape=[sems, sems] + [pltpu.HBM(a.shape, a.dtype) for a in srcs] + [pltpu.HBM(s.shape, s.dtype) for s in land_shapes]
        + [jax.ShapeDtypeStruct((SUBLANES, LANES), F32)],
        out_specs=[SEM_SPEC, SEM_SPEC] + [HBM_SPEC] * (2 * nu) + [pl.BlockSpec(memory_space=pltpu.VMEM)],
        input_output_aliases={q: 2 + q for q in range(2 * nu)},
        compiler_params=pltpu.CompilerParams(has_side_effects=ORDERED_EFFECT),
    )(*[hbm(a) for a in srcs], *[hbm(a) for a in lands], after)
    return res[0], res[1], res[2:2 + nu], res[2 + nu:2 + 2 * nu], res[-1]


def _split_wait(pending, moved, after, *, name):
    send_sems, recv_sems, srcs, lands, _ = pending
    nu = len(srcs)

    def body(*refs):
        land_refs, ssem, rsem = refs[nu:2 * nu], refs[2 * nu], refs[2 * nu + 1]
        x, y, c = _place()
        for u in range(nu):
            size = moved(land_refs[u])
            cp = pltpu.make_async_remote_copy(src_ref=size, dst_ref=size, send_sem=ssem.at[u], recv_sem=rsem.at[u],
                                              device_id=(x, y, c), device_id_type=MESH)
            cp.wait_send()
            cp.wait_recv()

    res = pl.pallas_call(
        body, name=name, in_specs=[HBM_SPEC] * (2 * nu) + [SEM_SPEC, SEM_SPEC, pl.BlockSpec(memory_space=pl.ANY)],
        out_shape=[pltpu.HBM(a.shape, a.dtype) for a in srcs] + [pltpu.HBM(a.shape, a.dtype) for a in lands],
        out_specs=[HBM_SPEC] * (2 * nu), input_output_aliases={q: q for q in range(2 * nu)},
        compiler_params=pltpu.CompilerParams(has_side_effects=ORDERED_EFFECT),
    )(*srcs, *lands, send_sems, recv_sems, after)
    return res[nu:]


def gather_start(shards, after, *, name):
    def body(src_refs, land_refs, send_sems, recv_sems):
        x, y, c = _place()
        for u, shard in enumerate(shards):
            rows = _half_rows(c, shard.shape[0] // 2)
            for cx, cy in _other_chips(x, y):
                for core in range(2):
                    pltpu.make_async_remote_copy(src_ref=src_refs[u].at[rows, :], dst_ref=land_refs[u].at[2 * x + y, rows, :],
                                                 send_sem=send_sems.at[u], recv_sem=recv_sems.at[u], device_id=(cx, cy, core),
                                                 device_id_type=MESH).start()

    return _split_start(body, shards, [jax.ShapeDtypeStruct((N_CHIPS,) + s.shape, s.dtype) for s in shards], after, name=name)


def gather_wait(pending, after, *, name):
    return _split_wait(pending, lambda land: land.at[pl.ds(0, N_CHIPS - 1)], after, name=name)


def scatter_start(pairs, *, name):
    def body(src_refs, land_refs, send_sems, recv_sems):
        x, y, c = _place()
        for u in range(len(pairs)):
            for j, (cx, cy) in enumerate(_other_chips(x, y)):
                pltpu.make_async_remote_copy(src_ref=src_refs[u].at[2 * cx + cy], dst_ref=land_refs[u].at[j], send_sem=send_sems.at[u],
                                             recv_sem=recv_sems.at[u], device_id=(cx, cy, c), device_id_type=MESH).start()

    return _split_start(body, pairs, [jax.ShapeDtypeStruct((N_CHIPS - 1,) + p.shape[1:], p.dtype) for p in pairs], pairs[0], name=name)


def scatter_wait(pending, after, *, name):
    return _split_wait(pending, lambda land: land, after, name=name)


def swap_units(units, *, name):
    nu = len(units)

    def body(*refs):
        g_refs, out_refs, send_sems, recv_sems = refs[:nu], refs[nu:2 * nu], refs[2 * nu], refs[2 * nu + 1]
        x, y, c = _place()
        cps = [pltpu.make_async_remote_copy(src_ref=g_refs[u].at[:, _half_rows(1 - c, units[u].shape[1] // 2), :], dst_ref=out_refs[u],
                                            send_sem=send_sems.at[u], recv_sem=recv_sems.at[u], device_id=(x, y, 1 - c),
                                            device_id_type=MESH) for u in range(nu)]
        for cp in cps:
            cp.start()
        for cp in cps:
            cp.wait()

    shapes = [jax.ShapeDtypeStruct((N_CHIPS, g.shape[1] // 2, g.shape[2]), g.dtype) for g in units]
    return _hbm_call(body, units, shapes, nu, name)


def join_units(units, *, name):
    nu = len(units)

    def body(*refs):
        h_refs, out_refs, send_sems, recv_sems = refs[:nu], refs[nu:2 * nu], refs[2 * nu], refs[2 * nu + 1]
        x, y, c = _place()
        cps = [pltpu.make_async_remote_copy(src_ref=h_refs[u], dst_ref=out_refs[u], send_sem=send_sems.at[u], recv_sem=recv_sems.at[u],
                                            device_id=(x, y, 1 - c), device_id_type=MESH) for u in range(nu)]
        for cp in cps:
            cp.start()
        for cp in cps:
            cp.wait()

    return _hbm_call(body, units, [jax.ShapeDtypeStruct(h.shape, h.dtype) for h in units], nu, name)


def _half_tile(rh):
    tr = rh
    for cand in range(BF16_ROWS, min(rh, 512) + 1, BF16_ROWS):
        if rh % cand == 0:
            tr = cand
    return tr


def pair_add(g, sib, *, name):
    nc, rh, cols = sib.shape
    tr = _half_tile(rh)
    nbh = rh // tr

    def body(g0_ref, g1_ref, s_ref, o_ref):
        mine = jnp.where(lax.axis_index("c") == 0, g0_ref[...], g1_ref[...])
        o_ref[...] = (mine.astype(F32) + s_ref[...].astype(F32)).astype(o_ref.dtype)

    blk = lambda off: pl.BlockSpec((None, tr, cols), lambda j, i: (j, off + i, 0))
    return pl.pallas_call(
        body, grid=(nc, nbh), in_specs=[blk(0), blk(nbh), blk(0)], out_specs=blk(0),
        out_shape=jax.ShapeDtypeStruct(sib.shape, BF16), name=name, compiler_params=_params(("parallel", "parallel")),
    )(g, g, sib)


def chips_add(pair, landed, *, name):
    nc, rh, cols = pair.shape
    tr = _half_tile(rh)

    def body(*refs):
        chip = 2 * lax.axis_index("x") + lax.axis_index("y")
        acc = refs[0][...]
        for j in range(1, nc):
            acc = jnp.where(chip == j, refs[j][...], acc)
        acc = acc.astype(F32)
        for r in refs[nc:-1]:
            acc = acc + r[...].astype(F32)
        refs[-1][...] = acc

    part = lambda q: pl.BlockSpec((None, tr, cols), lambda i, q=q: (q, i, 0))
    return pl.pallas_call(
        body, grid=(rh // tr,), in_specs=[part(q) for q in range(nc)] + [part(q) for q in range(landed.shape[0])],
        out_specs=pl.BlockSpec((tr, cols), lambda i: (i, 0)),
        out_shape=jax.ShapeDtypeStruct((rh, cols), F32), name=name, compiler_params=_params(("parallel",)),
    )(*[pair] * nc, *[landed] * landed.shape[0])


def sum8(g, *, name):
    def body(g_ref, o_ref):
        acc = g_ref[0]
        for d in range(1, N_DEV):
            acc = acc + g_ref[d]
        o_ref[...] = acc

    return pl.pallas_call(body, out_shape=jax.ShapeDtypeStruct(g.shape[1:], F32), name=name)(g)


def _dup_halves(a):
    t = a.shape[0]
    a = a.reshape(t, N_KV_B, HEAD_DIM_B)
    return jnp.concatenate([a, a], axis=-1).reshape(t, N_KV_B * LANES)


def _undup(a):
    t = a.shape[0]
    return a.reshape(t, N_KV_B, LANES)[:, :, :HEAD_DIM_B].reshape(t, N_KV_B * HEAD_DIM_B)


def _lane_pad(v, offset=0):
    return jnp.zeros((1, LANES), F32).at[0, offset:offset + v.shape[0]].set(v)


SHARD_UP = 2 * D_FF // N_CHIPS
SHARD_BIN = (N_HEADS_B + 2 * N_KV_B) * HEAD_DIM_B // N_CHIPS
SHARD_PROJ = D_MODEL // N_CHIPS


def local_step(x, p, tgt, sm, weight, on_grads):
    t = x.shape[0]
    rtm = min(ROW_TM, t)
    hk = N_HEADS_A * HEAD_DIM_A
    qd_b = N_HEADS_B * HEAD_DIM_B
    kd_b = N_KV_B * HEAD_DIM_B
    gs = {}
    norm = lambda h, w, nm: tile_map(_f_norm, [(h, D_MODEL, 0)], [w], [(D_MODEL, BF16)], tm=rtm, ncol=1, name=nm)[0]

    spec = pl.BlockSpec
    mtm = _tile(D_MODEL, MM_TM_CAP)
    p_bf = p.astype(BF16)
    alog_p = _lane_pad(sm["a_log"][0], N_HEADS_A)
    dtb_p = _lane_pad(sm["a_dt_bias"][0], N_HEADS_A)
    sinks_p = _lane_pad(sm["b_sinks"][0])
    nw = lambda name, i: sm[name][i:i + 1]
    by_chip = lambda kdim, ns: dict(tn=ns, tk=kdim, b_spec=spec((None, kdim, ns), lambda r, j, kk: (j, kk, 0)))
    by_chip_t = lambda ndim, ns: dict(n=ndim, tn=ndim, tk=ns, b_spec=spec((None, ndim, ns), lambda r, j, kk: (kk, j, 0)))
    cache = {}

    def wgt(name, i, after):
        if (name, i) not in cache:
            cache[name, i] = weight(name, i, after)
        return cache[name, i]

    saved = []
    h = x
    hn_next = norm(h, nw("norm_mix", 0), "norm_mix0")
    for i in range(DEPTH):
        s = {"h0": h, "hn": hn_next}
        if i % 2 == 0:
            s["pm"] = mm(s["hn"], wgt("a_w_in", i, h), name="a_in")
            tail = (s["pm"], LANES, 4 * hk // LANES)
            s["c"] = conv_fwd(s["pm"], wgt("a_conv", i, h), name="a_conv")
            s["bg"] = tile_map(_f_betag, [tail], [alog_p, dtb_p], [(LANES, F32)], tm=rtm, ncol=1, name="a_betag")[0]
            s["prep"], s["tms"] = delta_prep(s["c"], s["bg"], name="a_prep")
            s["o"], s["s_in"] = delta_scan(*s["prep"], name="a_scan")
            s["on"] = gnorm_fwd(s["o"], s["pm"], sm["a_norm"], name="a_gnorm")
            h, s["hf"] = mm(s["on"], wgt("a_w_out", i, s["on"]), add=h, norm_w=nw("norm_ffn", i), name="a_out")
        else:
            s["pb"] = mm(s["hn"], wgt("b_w_in", i, s["hn"]), name="b_in", out_dtype=BF16, n=N_CHIPS * SHARD_BIN,
                         **by_chip(D_MODEL, SHARD_BIN))
            s["kd"], s["vd"] = _dup_halves(s["pb"][:, qd_b:qd_b + kd_b]), _dup_halves(s["pb"][:, qd_b + kd_b:])
            s["ao"] = swa_fwd(s["pb"], s["kd"], s["vd"], sinks_p, name="b_att")
            h, s["hf"] = mm(s["ao"], wgt("b_w_out", i, s["ao"]), add=h, norm_w=nw("norm_ffn", i), name="b_out")
        s["h1"] = h
        s["u"] = mm(s["hf"], wgt("f_w_up", i, s["hf"]), name=f"f_up{i}", out_dtype=BF16, n=2 * D_FF, tm_cap=2 * MM_TM_CAP,
                    **by_chip(D_MODEL, SHARD_UP))
        s["act"] = conv_act_fwd(s["u"], wgt("f_conv", i, s["hf"]), name=f"f_conv_act{i}")
        h, s["hp"] = mm(s["act"], wgt("f_w_down", i, s["act"]), add=h, norm_w=nw("norm_ple", i), name=f"f_down{i}", tk=D_FF)
        s["h2"] = h
        s["gl"] = mm(s["hp"], wgt("ple_w_gate", i, s["hp"]), name=f"ple_gate{i}")
        s["pe"] = mm(p_bf[i], wgt("ple_w_proj", i, s["hp"]), name=f"ple_proj{i}", n=D_MODEL, **by_chip(PLE_DIM, SHARD_PROJ))
        rows3 = [(h, D_MODEL, 0), (s["gl"], D_MODEL, 0), (s["pe"], D_MODEL, 0)]
        if i + 1 < DEPTH:
            def mix_norm(hv, g, e, wn):
                hn = hv + _f_ple(g, e)
                return hn, _f_norm(hn, wn)
            h, hn_next = tile_map(mix_norm, rows3, [nw("norm_mix", i + 1)], [(D_MODEL, F32), (D_MODEL, BF16)], tm=rtm, ncol=1,
                                  name=f"ple_mix{i}")
        else:
            h = tile_map(lambda hv, g, e: hv + _f_ple(g, e), rows3, [], [(D_MODEL, F32)], tm=rtm, ncol=1, name=f"ple_mix{i}")[0]
        saved.append(s)

    dh, gnf, loss = loss_head(h, tgt, sm["norm_final"][None, :], name="loss_head")
    gs["norm_final"] = gnf[0]

    g_mix, g_ffn, g_ple, g_conv = ([None] * DEPTH for _ in range(4))
    zero = jnp.zeros((1, 1), F32)
    for i in reversed(range(DEPTH)):
        s, gw = saved[i], {}
        by_rows = lambda g: g.reshape(N_CHIPS, g.shape[0] // N_CHIPS, g.shape[1])
        (dgl, dpe), _ = tile_vjp(_f_ple, [(s["gl"], D_MODEL, 0), (s["pe"], D_MODEL, 0)], [], [(dh, D_MODEL, 0)], n_diff=2,
                                 tm=rtm, ncol=1, name=f"ple_mix_bwd{i}", grad_dtypes=[BF16, BF16])
        gw["ple_w_proj"] = mm(p_bf[i], dpe, ta=True, name=f"ple_proj_dw{i}", out_dtype=BF16, tn=SHARD_PROJ,
                              o_shape=(N_CHIPS, PLE_DIM, SHARD_PROJ), o_spec=spec((None, PLE_DIM, SHARD_PROJ), lambda r, j, kk: (j, r, 0)))
        gw["ple_w_gate"] = by_rows(mm(s["hp"], dgl, ta=True, name=f"ple_gate_dw{i}", out_dtype=BF16))
        fused = dict(tb=True, tm_cap=MM_TM_CAP // 2)
        small = dict(tb=True)
        dh, g_ple[i] = mm(dgl, cache["ple_w_gate", i], name=f"ple_gate_dx{i}", norm_grad=(s["h2"], nw("norm_ple", i) + zero, dh), **small)

        dact = mm(dh, cache["f_w_down", i], tb=True, name=f"f_down_dx{i}")
        gw["f_w_down"] = by_rows(mm(s["act"], dh, ta=True, name=f"f_down_dw{i}", out_dtype=BF16, tm_cap=D_FF // 2))
        du_halves = conv_act_bwd(s["u"], dact, cache["f_conv", i], name=f"f_conv_act_bwd{i}")
        g_conv[i] = jnp.concatenate(du_halves[2:], axis=1)
        dhf = g_up = None
        for half, du in enumerate(du_halves[:2]):
            c0 = half * (N_CHIPS // 2)
            g_up = mm(s["hf"], du, ta=True, name=f"f_up_dw{i}_{half}", out_dtype=BF16, tn=SHARD_UP, into=g_up,
                      o_shape=(N_CHIPS, D_MODEL, SHARD_UP), o_spec=spec((None, mtm, SHARD_UP), lambda r, j, kk, c0=c0: (c0 + j, r, 0)))
            last = dict(norm_grad=(s["h1"], nw("norm_ffn", i), dh), **fused) if half else dict(tb=True)
            dhf = mm(du, cache["f_w_up", i], name=f"f_up_dx{i}_{half}", n=D_MODEL, tn=D_MODEL, tk=SHARD_UP, add=dhf,
                     b_spec=spec((None, D_MODEL, SHARD_UP), lambda r, j, kk, c0=c0: (c0 + kk, j, 0)), **last)
        gw["f_w_up"] = g_up
        dh, g_ffn[i] = dhf
        token, gw = on_grads(i, "ffn", gw), {}
        w_out = cache["a_w_out" if i % 2 == 0 else "b_w_out", i]
        if token is not None:
            w_out = w_out + token[:1, :1].astype(BF16)

        if i % 2 == 0:
            don = mm(dh, w_out, tb=True, name="a_out_dx")
            gw["a_w_out"] = by_rows(mm(s["on"], dh, ta=True, name="a_out_dw", out_dtype=BF16))
            do, dz, gs["a_norm"] = gnorm_bwd(s["o"], s["pm"], sm["a_norm"], don, name="a_gnorm_bwd")
            dprep = delta_scan_bwd(do, *s["prep"], s["s_in"], name="a_scan_bwd")
            dc, dbg = delta_prep_bwd(s["c"], s["bg"], s["tms"], dprep, name="a_prep_bwd")
            (dpt,), (galog, gdtb) = tile_vjp(_f_betag, [(s["pm"], LANES, 4 * hk // LANES)], [alog_p, dtb_p], [(dbg, LANES, 0)], n_diff=1,
                                             tm=rtm, ncol=1, name="a_betag_bwd", grad_dtypes=[BF16])
            gs["a_log"] = galog[:, N_HEADS_A:2 * N_HEADS_A]
            gs["a_dt_bias"] = gdtb[:, N_HEADS_A:2 * N_HEADS_A]
            dqkv, gs["a_conv"] = conv_bwd(dc, s["pm"], cache["a_conv", i], name="a_conv_bwd")
            dpm = jnp.concatenate([dqkv, dz, dpt], axis=1)
            g_in = mm(s["hn"], dpm, ta=True, name="a_in_dw", out_dtype=BF16)[:, :4 * hk + 2 * N_HEADS_A]
            gw["a_w_in"] = g_in.reshape(D_MODEL, N_CHIPS, g_in.shape[1] // N_CHIPS).transpose(1, 0, 2)
            dh, g_mix[i] = mm(dpm, cache["a_w_in", i], name="a_in_dx", norm_grad=(s["h0"], nw("norm_mix", i), dh), **fused)
        else:
            dao = mm(dh, w_out, tb=True, name="b_out_dx")
            gw["b_w_out"] = by_rows(mm(s["ao"], dh, ta=True, name="b_out_dw", out_dtype=BF16))
            dq, dkd, dvd, gsk = swa_bwd(dao, s["pb"], s["kd"], s["vd"], sinks_p, name="b_att_bwd")
            gs["b_sinks"] = gsk[:, :N_HEADS_B]
            dpb = jnp.concatenate([dq, _undup(dkd), _undup(dvd)], axis=1)
            gw["b_w_in"] = mm(s["hn"], dpb, ta=True, name="b_in_dw", out_dtype=BF16, tn=SHARD_BIN,
                              o_shape=(N_CHIPS, D_MODEL, SHARD_BIN), o_spec=spec((None, mtm, SHARD_BIN), lambda r, j, kk: (j, r, 0)))
            dh, g_mix[i] = mm(dpb, cache["b_w_in", i], name="b_in_dx", norm_grad=(s["h0"], nw("norm_mix", i), dh), **small,
                              **by_chip_t(D_MODEL, SHARD_BIN))
        token = on_grads(i, "mix", gw)
        if token is not None:
            zero = token[:1, :1]

    gs["norm_mix"], gs["norm_ffn"], gs["norm_ple"] = (jnp.concatenate(g, axis=0) for g in (g_mix, g_ffn, g_ple))
    gs["f_conv"] = jnp.stack(g_conv)
    return loss, dh, gs


BIG = ["a_w_in", "a_w_out", "b_w_in", "b_w_out", "f_w_up", "f_w_down", "ple_w_proj", "ple_w_gate"]
LAYERED = {"f_w_up", "f_w_down", "ple_w_proj", "ple_w_gate"}
BY_CHIP = {"b_w_in", "f_w_up", "ple_w_proj"}
LAYER_UNITS = [[("a_w_in", 0), ("a_w_out", 0)] + [(n, 0) for n in sorted(LAYERED)],
               [("b_w_in", 1), ("b_w_out", 1)] + [(n, 1) for n in sorted(LAYERED)]]
CONVS = ["a_conv", "f_conv"]
SMALL = ["norm_mix", "norm_ffn", "norm_ple", "norm_final", "a_log", "a_dt_bias", "a_norm", "b_sinks"]
SMALL_ROWS = 8
CONV_ROWS = 16
CONV_GRAD_ROWS = 48


def _pack_rows(arrs, rows, dtype):
    flat = jnp.concatenate([a.reshape(-1).astype(dtype) for a in arrs])
    return jnp.pad(flat, (0, rows * PACK_COLS - flat.shape[0])).reshape(rows, PACK_COLS)


def _unpack(flat, shapes):
    out, off = [], 0
    for shp in shapes:
        n = math.prod(shp)
        out.append(flat[off:off + n].reshape(shp))
        off += n
    return out


def _pack_small(d, loss=None):
    tail = jnp.concatenate([d["a_log"].reshape(-1), d["a_dt_bias"].reshape(-1), d["a_norm"].reshape(-1), d["b_sinks"].reshape(-1)])
    if loss is not None:
        tail = jnp.concatenate([tail, loss.reshape(-1)[:1]])
    tail = jnp.pad(tail, (0, PACK_COLS - tail.shape[0]))
    return jnp.concatenate([d["norm_mix"], d["norm_ffn"], d["norm_ple"], d["norm_final"][None, :], tail[None, :]], axis=0)


def _unpack_small(a, like):
    out = {"norm_mix": a[0:2], "norm_ffn": a[2:4], "norm_ple": a[4:6], "norm_final": a[6]}
    off = 0
    for nm in ("a_log", "a_dt_bias", "a_norm", "b_sinks"):
        n = like[nm].size
        out[nm] = a[7, off:off + n].reshape(like[nm].shape)
        off += n
    return out, a[7, off]


def _as2d(a):
    return a.reshape(-1, a.shape[-1])


def kernel(x, p, norm_mix, norm_ffn, norm_ple, norm_final, a_w_in, a_conv, a_log, a_dt_bias, a_norm, a_w_out, b_w_in, b_sinks, b_w_out, f_w_up, f_conv, f_w_down, ple_w_proj, ple_w_gate, loss_target, m_norm_mix, m_norm_ffn, m_norm_ple, m_norm_final, m_a_w_in, m_a_conv, m_a_log, m_a_dt_bias, m_a_norm, m_a_w_out, m_b_w_in, m_b_sinks, m_b_w_out, m_f_w_up, m_f_conv, m_f_w_down, m_ple_w_proj, m_ple_w_gate, v_norm_mix, v_norm_ffn, v_norm_ple, v_norm_final, v_a_w_in, v_a_conv, v_a_log, v_a_dt_bias, v_a_norm, v_a_w_out, v_b_w_in, v_b_sinks, v_b_w_out, v_f_w_up, v_f_conv, v_f_w_down, v_ple_w_proj, v_ple_w_gate):
    w = dict(norm_mix=norm_mix, norm_ffn=norm_ffn, norm_ple=norm_ple, norm_final=norm_final, a_w_in=a_w_in, a_conv=a_conv,
             a_log=a_log, a_dt_bias=a_dt_bias, a_norm=a_norm, a_w_out=a_w_out, b_w_in=b_w_in, b_sinks=b_sinks, b_w_out=b_w_out,
             f_w_up=f_w_up, f_conv=f_conv, f_w_down=f_w_down, ple_w_proj=ple_w_proj, ple_w_gate=ple_w_gate)
    m = dict(norm_mix=m_norm_mix, norm_ffn=m_norm_ffn, norm_ple=m_norm_ple, norm_final=m_norm_final, a_w_in=m_a_w_in,
             a_conv=m_a_conv, a_log=m_a_log, a_dt_bias=m_a_dt_bias, a_norm=m_a_norm, a_w_out=m_a_w_out, b_w_in=m_b_w_in,
             b_sinks=m_b_sinks, b_w_out=m_b_w_out, f_w_up=m_f_w_up, f_conv=m_f_conv, f_w_down=m_f_w_down,
             ple_w_proj=m_ple_w_proj, ple_w_gate=m_ple_w_gate)
    v = dict(norm_mix=v_norm_mix, norm_ffn=v_norm_ffn, norm_ple=v_norm_ple, norm_final=v_norm_final, a_w_in=v_a_w_in,
             a_conv=v_a_conv, a_log=v_a_log, a_dt_bias=v_a_dt_bias, a_norm=v_a_norm, a_w_out=v_a_w_out, b_w_in=v_b_w_in,
             b_sinks=v_b_sinks, b_w_out=v_b_w_out, f_w_up=v_f_w_up, f_conv=v_f_conv, f_w_down=v_f_w_down,
             ple_w_proj=v_ple_w_proj, ple_w_gate=v_ple_w_gate)
    xc, yc, cc = _place()
    my_chip = 2 * xc + yc

    shard = {(n, i): w[n][i if n in LAYERED else 0].astype(BF16) for n, i in LAYER_UNITS[0] + LAYER_UNITS[1]}
    first = shard["a_w_in", 0]
    (ga,) = gather_units([(first, False)], name="gather_first")
    ga = lax.dynamic_update_index_in_dim(ga, first, my_chip, 0)
    a_in = jnp.concatenate([ga[j] for j in range(N_CHIPS)], axis=1)
    n_main = 4 * N_HEADS_A * HEAD_DIM_A
    conv_shapes = [w[n].shape for n in CONVS]
    convs = allgather8(_pack_rows([w[n] for n in CONVS], CONV_ROWS, F32), name="gather_convs")
    conv_parts = [_unpack(convs[2 * j].reshape(-1), conv_shapes) for j in range(N_CHIPS)]
    a_conv_full, f_conv_full = (jnp.concatenate([conv_parts[j][q] for j in range(N_CHIPS)], axis=2) for q in range(2))
    ready = {("a_w_in", 0): jnp.pad(a_in, ((0, 0), (0, n_main + LANES - a_in.shape[1]))), ("a_conv", 0): a_conv_full[0], ("f_conv", 0): f_conv_full[0], ("f_conv", 1): f_conv_full[1]}
    later = [[k for k in units if k != ("a_w_in", 0)] for units in LAYER_UNITS]
    pending, after = [], ga
    for layer, keys in enumerate(later):
        pending.append(gather_start([shard[k] for k in keys], after, name=f"gather_start{layer}"))
        after = pending[-1][4]
    sm = {n: w[n] for n in SMALL}
    sm["norm_mix"] = sm["norm_mix"] + after[:1, :1]

    def weight(name, layer, act):
        if (name, layer) not in ready:
            landed = gather_wait(pending[layer], act, name=f"gather_wait{layer}")
            for k, g in zip(later[layer], landed):
                g = lax.dynamic_update_index_in_dim(g, shard[k], my_chip, 0)
                ready[k] = g if k[0] in BY_CHIP else g.reshape(N_CHIPS * g.shape[1], g.shape[2])
        return ready[name, layer]

    pairs, scattered, started = {}, {}, []

    def on_grads(layer, part, gw):
        keys = [k for k in LAYER_UNITS[layer] if (k[0] in LAYERED) == (part == "ffn")]
        from_sib = swap_units([gw[n] for n, _ in keys], name=f"rs_swap_{part}{layer}")
        for (n, _), sib in zip(keys, from_sib):
            pairs[n, layer] = pair_add(gw[n], sib, name=f"rs_add_pair_{n}{layer}")
        started.append((keys, scatter_start([pairs[k] for k in keys], name=f"rs_scatter_start_{part}{layer}"), f"{part}{layer}"))
        return started[-1][1][4]

    loss, grad_x, gs = local_step(x[0], p[:, 0], loss_target[0], sm, weight, on_grads)

    grads, delta, new_m, new_v, g_unit = {}, {}, {}, {}, {}

    def finish(keys, tag):
        halves = [chips_add(pairs[k], scattered[k], name=f"rs_add_chips_{k[0]}{k[1]}") for k in keys]
        g_unit.update(zip(keys, zip(halves, join_units(halves, name=f"rs_join_{tag}"))))
        for n in BIG:
            mine = [(n, i) for i in range(DEPTH) if (n, i) in LAYER_UNITS[i]]
            if n not in delta and all(k in g_unit for k in mine):
                g_layers = [g_unit[k] for k in mine]
                shape3 = (len(g_layers), 2 * g_layers[0][0].shape[0], g_layers[0][0].shape[1])
                res = adamw_layers(w[n].reshape(shape3), g_layers, m[n].reshape(shape3), v[n].reshape(shape3), name=f"adamw_{n}")
                delta[n], new_m[n], new_v[n], grads[n] = (r.reshape(w[n].shape) for r in res)

    last_keys, last_pending, last_tag = started[-1]
    for keys, pend, tag in started[:-1]:
        scattered.update(zip(keys, scatter_wait(pend, last_pending[4], name=f"rs_scatter_wait_{tag}")))
    finish([k for keys, _, _ in started[:-1] for k in keys], "first")

    conv_grads = _pack_rows([gs[n] for n in CONVS], CONV_GRAD_ROWS, F32)
    small_sum = sum8(allgather8(jnp.concatenate([_pack_small(gs, loss), conv_grads]), name="gather_small"), name="sum_small")
    g_sm, loss_sum = _unpack_small(small_sum[:SMALL_ROWS], sm)

    scattered.update(zip(last_keys, scatter_wait(last_pending, small_sum, name=f"rs_scatter_wait_{last_tag}")))
    finish(last_keys, "last")

    for n, full in zip(CONVS, _unpack(small_sum[SMALL_ROWS:].reshape(-1), [gs[n].shape for n in CONVS])):
        g2 = _as2d(lax.dynamic_slice_in_dim(full, my_chip * w[n].shape[-1], w[n].shape[-1], axis=full.ndim - 1))
        d2, m2, v2 = adamw(_as2d(w[n]), g2, _as2d(m[n]), _as2d(v[n]), name=f"adamw_{n}")
        grads[n], delta[n], new_m[n], new_v[n] = (r.reshape(w[n].shape) for r in (g2, d2, m2, v2))
    pk = lambda d: _pack_small(d)
    d2, m2, v2 = adamw(pk(sm), pk(g_sm), pk({n: m[n] for n in SMALL}), pk({n: v[n] for n in SMALL}), name="adamw_small")
    for src, dst in ((d2, delta), (m2, new_m), (v2, new_v)):
        dst.update(_unpack_small(src, sm)[0])
    grads.update(g_sm)

    order = ["norm_mix", "norm_ffn", "norm_ple", "norm_final", "a_w_in", "a_conv", "a_log", "a_dt_bias", "a_norm", "a_w_out",
             "b_w_in", "b_sinks", "b_w_out", "f_w_up", "f_conv", "f_w_down", "ple_w_proj", "ple_w_gate"]
    return (loss_sum, grad_x[None], *[grads[n] for n in order], *[delta[n] for n in order],
            *[new_m[n] for n in order], *[new_v[n] for n in order])
```

```python
import functools
import math

import jax
import jax.numpy as jnp
from jax import lax
from jax.experimental import pallas as pl
from jax.experimental.pallas import tpu as pltpu

F32 = jnp.float32
BF16 = jnp.bfloat16
MESH = pl.DeviceIdType.MESH

D_MODEL = 1024
N_HEADS_A = 8
HEAD_DIM_A = 128
CONV_A = 4
N_HEADS_B = 16
N_KV_B = 4
HEAD_DIM_B = 64
WINDOW = 128
D_FF = 2816
FFN_CONV = 3
PLE_DIM = 256
EPS = 1e-6
DEPTH = 2

ADAM_LR = 0.001
ADAM_B1 = 0.9
ADAM_B2 = 0.999
ADAM_EPS = 1e-08
ADAM_WD = 0.01
ADAM_STEP = 10

LANES = 128
SUBLANES = 8
BF16_ROWS = 16
CHUNK = 128
VMEM_LIMIT = 56 * 1024 * 1024
NEG = -1e30
N_CHIPS = 4
N_DEV = 8
PACK_COLS = 1024


def _params(sem=None):
    return pltpu.CompilerParams(dimension_semantics=sem, vmem_limit_bytes=VMEM_LIMIT)


def _tile(dim, cap):
    if dim % LANES:
        return dim
    best = LANES
    for t in range(LANES, min(dim, cap) + 1, LANES):
        if dim % t == 0:
            best = t
    return best


def _dot(a, b, dims=(((1,), (0,)), ((), ())), precision=None):
    return lax.dot_general(a, b, dims, precision=precision, preferred_element_type=F32)


NN = (((1,), (0,)), ((), ()))
NT = (((1,), (1,)), ((), ()))
TN = (((0,), (0,)), ((), ()))


ROW_TM = 512
MM_TM_CAP = 1024
MM_TK_CAP_TOKENS = 2048


def mm(a, b, *, name, ta=False, tb=False, out_dtype=F32, add=None, norm_w=None, norm_grad=None, tm_cap=MM_TM_CAP, tn_cap=1408,
       tk_cap=1408, n=None, tn=None, tk=None, b_spec=None, o_spec=None, o_shape=None, into=None):
    m, k = (a.shape[1], a.shape[0]) if ta else a.shape
    if b_spec is None:
        n = b.shape[0] if tb else b.shape[1]
        assert (b.shape[1] if tb else b.shape[0]) == k, (a.shape, b.shape, ta, tb)
    tm, tn, tk = _tile(m, tm_cap), tn or _tile(n, tn_cap), tk or _tile(k, MM_TK_CAP_TOKENS if ta else tk_cap)
    assert n % tn == 0 and k % tk == 0, (n, tn, k, tk)
    nk = k // tk
    dims = (((0 if ta else 1,), (1 if tb else 0,)), ((), ()))
    has_add, has_norm, has_grad = add is not None, norm_w is not None, norm_grad is not None
    assert not (has_norm or has_grad) or (tn == n and o_spec is None), "the norm epilogues need whole rows"
    n_in = 2 + has_add + has_norm + 3 * has_grad + (into is not None)

    def body(*refs):
        a_ref, b_ref = refs[0], refs[1]
        add_ref = refs[2] if has_add else None
        o_ref = refs[n_in]
        part = _dot(a_ref[...].astype(BF16), b_ref[...].astype(BF16), dims)
        first = pl.program_id(0) == 0

        def finish(r):
            if has_add:
                r = r + add_ref[...].astype(F32)
            if has_grad:
                h_ref, w_ref, prev_ref = refs[2 + has_add:5 + has_add]
                _, vjp = jax.vjp(_f_norm, h_ref[...], w_ref[...])
                r, dw = vjp(r)
                r = r + prev_ref[...]

                @pl.when(first)
                def _():
                    refs[n_in + 1][...] = dw

                @pl.when(jnp.logical_not(first))
                def _():
                    refs[n_in + 1][...] += dw
            o_ref[...] = r.astype(o_ref.dtype)
            if has_norm:
                refs[n_in + 1][...] = _f_norm(r, refs[2 + has_add][...]).astype(BF16)

        if nk == 1:
            finish(part)
            return
        acc = refs[-1]
        kk = pl.program_id(2)

        @pl.when(kk == 0)
        def _():
            acc[...] = part

        @pl.when(kk > 0)
        def _():
            acc[...] += part

        @pl.when(kk == nk - 1)
        def _():
            finish(acc[...])

    a_spec = pl.BlockSpec((tk, tm), lambda i, j, kk: (kk, i)) if ta else pl.BlockSpec((tm, tk), lambda i, j, kk: (i, kk))
    if b_spec is None:
        b_spec = pl.BlockSpec((tn, tk), lambda i, j, kk: (j, kk)) if tb else pl.BlockSpec((tk, tn), lambda i, j, kk: (kk, j))
    plain_o = pl.BlockSpec((tm, tn), lambda i, j, kk: (i, j))
    if o_spec is None:
        o_spec, o_shape = plain_o, (m, n)
    in_specs = [a_spec, b_spec] + ([plain_o] if has_add else [])
    args = (a, b) + ((add,) if has_add else ())
    out_specs, out_shapes = o_spec, jax.ShapeDtypeStruct(tuple(o_shape), out_dtype)
    one_row = pl.BlockSpec((1, n), lambda i, j, kk: (0, 0))
    if has_norm:
        in_specs.append(one_row)
        args += (norm_w,)
        out_specs, out_shapes = [o_spec, plain_o], [out_shapes, jax.ShapeDtypeStruct((m, n), BF16)]
    if has_grad:
        assert not has_norm
        in_specs += [plain_o, one_row, plain_o]
        args += tuple(norm_grad)
        out_specs, out_shapes = [o_spec, one_row], [out_shapes, jax.ShapeDtypeStruct((1, n), F32)]
    aliases = {}
    if into is not None:
        assert into.shape == tuple(o_shape) and into.dtype == out_dtype, (into.shape, o_shape)
        in_specs.append(pl.BlockSpec(memory_space=pl.ANY))
        args += (into,)
        aliases = {n_in - 1: 0}
    return pl.pallas_call(
        body, grid=(m // tm, n // tn, nk), in_specs=in_specs, out_specs=out_specs,
        out_shape=out_shapes, name=name, input_output_aliases=aliases,
        scratch_shapes=[pltpu.VMEM((tm, tn), F32)] if nk > 1 else [],
        compiler_params=_params(("arbitrary" if has_grad else "parallel", "parallel", "arbitrary")),
    )(*args)


def _row_spec(tm, cw, coff):
    return pl.BlockSpec((tm, cw), lambda i, j: (i, j + coff))


def _full_spec(shape):
    return pl.BlockSpec(shape, lambda i, j: (0,) * len(shape))


def tile_map(fn, rows, params, outs, *, tm, ncol, name):
    t = rows[0][0].shape[0]
    nin = len(rows) + len(params)

    def body(*refs):
        res = fn(*[r[...] for r in refs[:nin]])
        res = res if isinstance(res, (tuple, list)) else (res,)
        for o_ref, r in zip(refs[nin:], res):
            o_ref[...] = r.astype(o_ref.dtype)

    in_specs = [_row_spec(tm, cw, coff) for (_, cw, coff) in rows] + [_full_spec(p.shape) for p in params]
    res = pl.pallas_call(
        body, grid=(t // tm, ncol), in_specs=in_specs,
        out_specs=[_row_spec(tm, cw, 0) for (cw, _) in outs],
        out_shape=[jax.ShapeDtypeStruct((t, cw * ncol), dt) for (cw, dt) in outs], name=name,
        compiler_params=_params(("parallel", "parallel")),
    )(*[r[0] for r in rows], *params)
    return res


def tile_vjp(fn, rows, params, cts, *, n_diff, tm, ncol, name, grad_dtypes=None):
    t = rows[0][0].shape[0]
    nr, npar, nct = len(rows), len(params), len(cts)

    def body(*refs):
        vals = [r[...] for r in refs[:nr + npar + nct]]
        diff, rest, pars = vals[:n_diff], vals[n_diff:nr], vals[nr:nr + npar]
        ctv = vals[nr + npar:nr + npar + nct]
        outs_ref = refs[nr + npar + nct:]

        def f(*a):
            res = fn(*a[:n_diff], *rest, *a[n_diff:])
            return tuple(res) if isinstance(res, (tuple, list)) else (res,)

        primal, vjp = jax.vjp(f, *[d.astype(F32) for d in diff], *pars)
        grads = vjp(tuple(c.astype(o.dtype) for c, o in zip(ctv, primal)))
        for q in range(n_diff):
            outs_ref[q][...] = grads[q].astype(outs_ref[q].dtype)
        first = (pl.program_id(0) == 0) & (pl.program_id(1) == 0)
        for q in range(npar):
            o_ref, g = outs_ref[n_diff + q], grads[n_diff + q]

            @pl.when(first)
            def _(o_ref=o_ref, g=g):
                o_ref[...] = g

            @pl.when(jnp.logical_not(first))
            def _(o_ref=o_ref, g=g):
                o_ref[...] += g

    in_specs = [_row_spec(tm, cw, coff) for (_, cw, coff) in rows] + [_full_spec(p.shape) for p in params]
    in_specs += [_row_spec(tm, cw, coff) for (_, cw, coff) in cts]
    args = [r[0] for r in rows] + list(params) + [c[0] for c in cts]
    out_specs = [_row_spec(tm, rows[q][1], 0) for q in range(n_diff)] + [_full_spec(p.shape) for p in params]
    grad_dtypes = grad_dtypes or [F32] * n_diff
    out_shape = [jax.ShapeDtypeStruct((t, rows[q][1] * ncol), grad_dtypes[q]) for q in range(n_diff)]
    out_shape += [jax.ShapeDtypeStruct(p.shape, F32) for p in params]
    res = pl.pallas_call(
        body, grid=(t // tm, ncol), in_specs=in_specs, out_specs=out_specs, out_shape=out_shape, name=name,
        compiler_params=_params(("arbitrary", "arbitrary")),
    )(*args)
    return res[:n_diff], res[n_diff:]


def _silu(x):
    return x * jax.nn.sigmoid(x)


def _f_norm(h, w):
    return h * lax.rsqrt(jnp.mean(h * h, axis=-1, keepdims=True) + EPS) * w


def _f_gnorm(o, z, w):
    return _f_norm(o, w) * _silu(z)


def _f_act(gate, val):
    return _silu(gate) * val


def _f_ple(gl, pe):
    return jax.nn.sigmoid(gl) * pe


def _f_betag(pt, alog, dtb):
    lane = lax.broadcasted_iota(jnp.int32, (1, LANES), 1)
    z = pt + dtb
    softplus = jnp.maximum(z, 0.0) + jnp.log(1.0 + jnp.exp(-jnp.abs(z)))
    g = -jnp.exp(alog) * softplus
    return jnp.where(lane < N_HEADS_A, jax.nn.sigmoid(pt), jnp.where(lane < 2 * N_HEADS_A, g, 0.0))


CONV_TM = 1024
CONV_CW = 1024


def _shift_down(x, prev, s, row):
    rp = jnp.tile(pltpu.roll(prev, s, 0), (x.shape[0] // SUBLANES, 1))
    return jnp.where(row < s, rp, pltpu.roll(x, s, 0))


def _shift_up(x, nxt, s, row):
    tm = x.shape[0]
    rn = jnp.tile(pltpu.roll(nxt, SUBLANES - s, 0), (tm // SUBLANES, 1))
    return jnp.where(row >= tm - s, rn, pltpu.roll(x, tm - s, 0))


def _conv_taps(x, prev, w_ref, cols, row):
    k = w_ref.shape[0]
    y = x * w_ref[pl.ds(k - 1, 1), cols]
    for s in range(1, k):
        y = y + _shift_down(x, prev, s, row) * w_ref[pl.ds(k - 1 - s, 1), cols]
    return y


def _lane_chunks(cw):
    return [slice(cb * LANES, (cb + 1) * LANES) for cb in range(cw // LANES)]


def conv_fwd(x, w, *, name):
    t = x.shape[0]
    k, c = w.shape
    tm, cw = min(CONV_TM, t), CONV_CW
    nb8 = tm // SUBLANES

    def body(x_ref, p_ref, w_ref, o_ref):
        first = pl.program_id(1) == 0
        row = lax.broadcasted_iota(jnp.int32, (tm, LANES), 0)
        for cols in _lane_chunks(cw):
            o_ref[:, cols] = _conv_taps(x_ref[:, cols], jnp.where(first, 0.0, p_ref[:, cols]), w_ref, cols, row)

    return pl.pallas_call(
        body, grid=(c // cw, t // tm),
        in_specs=[pl.BlockSpec((tm, cw), lambda j, i: (i, j)),
                  pl.BlockSpec((SUBLANES, cw), lambda j, i: (jnp.maximum(i * nb8 - 1, 0), j)),
                  pl.BlockSpec((k, cw), lambda j, i: (0, j))],
        out_specs=pl.BlockSpec((tm, cw), lambda j, i: (i, j)),
        out_shape=jax.ShapeDtypeStruct((t, c), F32), name=name,
        compiler_params=_params(("parallel", "parallel")),
    )(x, x, w)


def conv_bwd(dy, x, w, *, name):
    t = x.shape[0]
    k, c = w.shape
    tm, cw = min(CONV_TM, t), CONV_CW
    nb8 = tm // SUBLANES
    ni = t // tm

    def body(dy_ref, dn_ref, x_ref, p_ref, w_ref, dx_ref, dw_ref):
        i = pl.program_id(1)
        first, last = i == 0, i == ni - 1
        row = lax.broadcasted_iota(jnp.int32, (tm, LANES), 0)
        for cols in _lane_chunks(cw):
            dyv, xv = dy_ref[:, cols], x_ref[:, cols]
            nxt = jnp.where(last, 0.0, dn_ref[:, cols])
            prev = jnp.where(first, 0.0, p_ref[:, cols])
            dx = dyv * w_ref[pl.ds(k - 1, 1), cols]
            dws = [jnp.sum(dyv * xv, axis=0, keepdims=True)]
            for s in range(1, k):
                dx = dx + _shift_up(dyv, nxt, s, row) * w_ref[pl.ds(k - 1 - s, 1), cols]
                dws.append(jnp.sum(dyv * _shift_down(xv, prev, s, row), axis=0, keepdims=True))
            dx_ref[:, cols] = dx.astype(dx_ref.dtype)
            for s in range(k):
                @pl.when(first)
                def _(s=s, dws=dws, cols=cols):
                    dw_ref[pl.ds(k - 1 - s, 1), cols] = dws[s]

                @pl.when(jnp.logical_not(first))
                def _(s=s, dws=dws, cols=cols):
                    dw_ref[pl.ds(k - 1 - s, 1), cols] += dws[s]

    return pl.pallas_call(
        body, grid=(c // cw, ni),
        in_specs=[pl.BlockSpec((tm, cw), lambda j, i: (i, j)),
                  pl.BlockSpec((SUBLANES, cw), lambda j, i: (jnp.minimum((i + 1) * nb8, t // SUBLANES - 1), j)),
                  pl.BlockSpec((tm, cw), lambda j, i: (i, j)),
                  pl.BlockSpec((SUBLANES, cw), lambda j, i: (jnp.maximum(i * nb8 - 1, 0), j)),
                  pl.BlockSpec((k, cw), lambda j, i: (0, j))],
        out_specs=[pl.BlockSpec((tm, cw), lambda j, i: (i, j)), pl.BlockSpec((k, cw), lambda j, i: (0, j))],
        out_shape=[jax.ShapeDtypeStruct((t, c), BF16), jax.ShapeDtypeStruct((k, c), F32)], name=name,
        compiler_params=_params(("parallel", "arbitrary")),
    )(dy, dy, x, x, w)


FFN_TM = 512
FFN_CW = D_FF // 2


def _ffn_specs(t, tm, cw, k):
    ncol = D_FF // cw
    cur = lambda off: pl.BlockSpec((tm, cw), lambda j, i: (i, j + off))
    prev = lambda off, hr: pl.BlockSpec((hr, cw), lambda j, i: (jnp.maximum(i * (tm // hr) - 1, 0), j + off))
    nxt = lambda off, hr: pl.BlockSpec((hr, cw), lambda j, i: (jnp.minimum((i + 1) * (tm // hr), t // hr - 1), j + off))
    taps = lambda off: pl.BlockSpec((k, cw), lambda j, i: (0, j + off))
    return cur, prev, nxt, taps, ncol


def _rows_before(ref, cols, first):
    return jnp.where(first, 0.0, ref[ref.shape[0] - SUBLANES:, cols].astype(F32))


def conv_act_fwd(u, w, *, name):
    t, k = u.shape[0], w.shape[0]
    tm, cw = min(FFN_TM, t), FFN_CW
    cur, prev, _, taps, ncol = _ffn_specs(t, tm, cw, k)

    def body(ug_ref, pg_ref, uv_ref, pv_ref, wg_ref, wv_ref, o_ref):
        first = pl.program_id(1) == 0
        row = lax.broadcasted_iota(jnp.int32, (tm, LANES), 0)
        for cb in range(cw // LANES):
            cols = slice(cb * LANES, (cb + 1) * LANES)
            cg = _conv_taps(ug_ref[:, cols].astype(F32), _rows_before(pg_ref, cols, first), wg_ref, cols, row)
            cv = _conv_taps(uv_ref[:, cols].astype(F32), _rows_before(pv_ref, cols, first), wv_ref, cols, row)
            o_ref[:, cols] = _f_act(cg, cv).astype(o_ref.dtype)

    return pl.pallas_call(
        body, grid=(ncol, t // tm),
        in_specs=[cur(0), prev(0, BF16_ROWS), cur(ncol), prev(ncol, BF16_ROWS), taps(0), taps(ncol)],
        out_specs=cur(0), out_shape=jax.ShapeDtypeStruct((t, D_FF), BF16), name=name,
        compiler_params=_params(("parallel", "parallel")),
    )(u, u, u, u, w, w)


def conv_act_bwd(u, dact, w, *, name):
    t, k = u.shape[0], w.shape[0]
    tm, cw = min(FFN_TM, t), FFN_CW
    cur, prev, nxt, taps, ncol = _ffn_specs(t, tm, cw, k)
    ni = t // tm

    def body(ug_ref, pg_ref, ng_ref, uv_ref, pv_ref, nv_ref, d_ref, dn_ref, wg_ref, wv_ref, dg_ref, dv_ref, dwg_ref, dwv_ref):
        i = pl.program_id(1)
        first, last = i == 0, i == ni - 1
        row = lax.broadcasted_iota(jnp.int32, (tm, LANES), 0)
        row8 = lax.broadcasted_iota(jnp.int32, (SUBLANES, LANES), 0)
        for cb in range(cw // LANES):
            cols = slice(cb * LANES, (cb + 1) * LANES)
            ug, uv = ug_ref[:, cols].astype(F32), uv_ref[:, cols].astype(F32)
            pg, pv = _rows_before(pg_ref, cols, first), _rows_before(pv_ref, cols, first)
            sg = [ug] + [_shift_down(ug, pg, s, row) for s in range(1, k)]
            sv = [uv] + [_shift_down(uv, pv, s, row) for s in range(1, k)]
            taps = lambda xs, w_ref: sum(xs[s] * w_ref[pl.ds(k - 1 - s, 1), cols] for s in range(k))
            _, vjp = jax.vjp(_f_act, taps(sg, wg_ref), taps(sv, wv_ref))
            dcg, dcv = vjp(d_ref[:, cols])
            after = lambda ref: ref[:SUBLANES, cols].astype(F32)
            _, vjp_n = jax.vjp(_f_act, _conv_taps(after(ng_ref), ug[tm - SUBLANES:], wg_ref, cols, row8),
                               _conv_taps(after(nv_ref), uv[tm - SUBLANES:], wv_ref, cols, row8))
            dcgn, dcvn = vjp_n(jnp.where(last, 0.0, dn_ref[:, cols]))
            for dc, dcn, xs, w_ref, dx_ref, dw_ref in ((dcg, dcgn, sg, wg_ref, dg_ref, dwg_ref),
                                                       (dcv, dcvn, sv, wv_ref, dv_ref, dwv_ref)):
                dx = dc * w_ref[pl.ds(k - 1, 1), cols]
                dws = [jnp.sum(dc * xs[0], axis=0, keepdims=True)]
                for s in range(1, k):
                    dx = dx + _shift_up(dc, dcn, s, row) * w_ref[pl.ds(k - 1 - s, 1), cols]
                    dws.append(jnp.sum(dc * xs[s], axis=0, keepdims=True))
                dx_ref[:, cols] = dx.astype(dx_ref.dtype)
                for s in range(k):
                    @pl.when(first)
                    def _(s=s, dw_ref=dw_ref, dws=dws):
                        dw_ref[pl.ds(k - 1 - s, 1), cols] = dws[s]

                    @pl.when(jnp.logical_not(first))
                    def _(s=s, dw_ref=dw_ref, dws=dws):
                        dw_ref[pl.ds(k - 1 - s, 1), cols] += dws[s]

    half = jax.ShapeDtypeStruct((t, D_FF), BF16)
    dwh = jax.ShapeDtypeStruct((k, D_FF), F32)
    return pl.pallas_call(
        body, grid=(ncol, ni),
        in_specs=[cur(0), prev(0, BF16_ROWS), nxt(0, BF16_ROWS), cur(ncol), prev(ncol, BF16_ROWS), nxt(ncol, BF16_ROWS),
                  cur(0), nxt(0, SUBLANES), taps(0), taps(ncol)],
        out_specs=[cur(0), cur(0), taps(0), taps(0)], out_shape=[half, half, dwh, dwh], name=name,
        compiler_params=_params(("parallel", "arbitrary")),
    )(u, u, u, u, u, u, dact, dact, w, w)


def _each(f, *lists):
    return [f(*a) for a in zip(*lists)]


@jax.custom_vjp
def _inv_unit_lower(lms):
    return _inv_blocks(lms)


def _inv_blocks(lms):
    c = lms[0].shape[0]
    ri = lax.broadcasted_iota(jnp.int32, (c, c), 0)
    ci = lax.broadcasted_iota(jnp.int32, (c, c), 1)
    eye = (ri == ci).astype(F32)
    dms = _each(lambda lm: eye - jnp.where((ri >> 1) == (ci >> 1), lm, 0.0), lms)
    for lv in range(1, int(math.log2(c))):
        below = ((ri >> (lv + 1)) == (ci >> (lv + 1))) & ((ri >> lv) != (ci >> lv))
        dbs = _each(lambda dm: dm.astype(BF16), dms)
        ods = _each(lambda lm, db: _dot(jnp.where(below, lm, 0.0).astype(BF16), db).astype(BF16), lms, dbs)
        dms = _each(lambda dm, db, od: dm - _dot(db, od), dms, dbs, ods)
    return dms


def _inv_fwd(lms):
    tms = _inv_blocks(lms)
    return tms, tms


def _inv_bwd(tms, dts):
    tbs = _each(lambda tm: tm.astype(BF16), tms)
    mid = _each(lambda tb, dt: _dot(tb, dt.astype(BF16), TN).astype(BF16), tbs, dts)
    return (_each(lambda m, tb: -_dot(m, tb, NT), mid, tbs),)


_inv_unit_lower.defvjp(_inv_fwd, _inv_bwd)


@jax.custom_vjp
def _inv_known(lms, tms):
    return tms


_inv_known.defvjp(lambda lms, tms: (tms, tms), lambda tms, dts: _inv_bwd(tms, dts) + (_each(jnp.zeros_like, tms),))


def _l2n(x):
    return x * lax.rsqrt(jnp.sum(x * x, axis=-1, keepdims=True) + EPS)


def _prep_fn(cqs, cks, cvs, bg, sel_b, sel_g, tms=None):
    c = cqs[0].shape[0]
    ri = lax.broadcasted_iota(jnp.int32, (c, c), 0)
    ci = lax.broadcasted_iota(jnp.int32, (c, c), 1)
    eye = (ri == ci).astype(F32)
    incl, strict = ci <= ri, ci < ri
    last = lax.broadcasted_iota(jnp.int32, (c, 1), 0) == c - 1
    to_row = lambda col: jnp.sum(col * eye, axis=0, keepdims=True)
    qs = _each(lambda a: _l2n(_silu(a)) * (HEAD_DIM_A ** -0.5), cqs)
    ks = _each(lambda a: _l2n(_silu(a)), cks)
    vbs = _each(lambda a: _silu(a).astype(BF16), cvs)
    betas = _each(lambda m: jnp.sum(bg * m, axis=1, keepdims=True), sel_b)
    gs = _each(lambda m: jnp.sum(bg * m, axis=1, keepdims=True), sel_g)
    gcss = _each(lambda g: jnp.sum(jnp.where(incl, to_row(g), 0.0), axis=1, keepdims=True), gs)
    gtots = _each(lambda gcs: jnp.sum(jnp.where(last, gcs, 0.0), axis=0, keepdims=True), gcss)
    decays = _each(lambda gcs: jnp.exp(jnp.where(incl, gcs - to_row(gcs), NEG)), gcss)
    kbs = _each(lambda k: k.astype(BF16), ks)
    lms = _each(lambda beta, kb, dec: jnp.where(strict, beta * _dot(kb, kb, NT) * dec, 0.0), betas, kbs, decays)
    tms = _inv_unit_lower(lms) if tms is None else _inv_known(lms, tms)
    ams = _each(lambda tm, beta: (tm * to_row(beta)).astype(BF16), tms, betas)
    gams = _each(jnp.exp, gcss)
    u0s = _each(_dot, ams, vbs)
    wks = _each(lambda am, gam, k: _dot(am, (gam * k).astype(BF16)), ams, gams, ks)
    qks = _each(lambda q, kb, dec: _dot(q.astype(BF16), kb, NT) * dec, qs, kbs, decays)
    qds = _each(lambda q, gam: q * gam, qs, gams)
    kds = _each(lambda k, gtot, gcs: k * jnp.exp(gtot - gcs), ks, gtots, gcss)
    gls = _each(lambda gtot: jnp.exp(gtot) * jnp.ones((SUBLANES, LANES), F32), gtots)
    return u0s, wks, qds, kds, qks, gls, tms


def _head_masks(h):
    lane = lax.broadcasted_iota(jnp.int32, (1, LANES), 1)
    return (lane == h).astype(F32), (lane == h + N_HEADS_A).astype(F32)


def _hsl(j):
    return slice(j * HEAD_DIM_A, (j + 1) * HEAD_DIM_A)


def gnorm_fwd(o, zsrc, w, *, name):
    t, width = o.shape
    tm = min(ROW_TM, t)
    zoff = zsrc.shape[1] // width - 1

    def body(o_ref, z_ref, w_ref, out_ref):
        for h in range(N_HEADS_A):
            out_ref[:, _hsl(h)] = _f_gnorm(o_ref[:, _hsl(h)], z_ref[:, _hsl(h)], w_ref[...]).astype(out_ref.dtype)

    rows = pl.BlockSpec((tm, width), lambda i: (i, 0))
    return pl.pallas_call(
        body, grid=(t // tm,),
        in_specs=[rows, pl.BlockSpec((tm, width), lambda i: (i, zoff)), pl.BlockSpec(w.shape, lambda i: (0, 0))],
        out_specs=rows, out_shape=jax.ShapeDtypeStruct((t, width), BF16), name=name, compiler_params=_params(("parallel",)),
    )(o, zsrc, w)


def gnorm_bwd(o, zsrc, w, don, *, name):
    t, width = o.shape
    tm = min(ROW_TM, t)
    zoff = zsrc.shape[1] // width - 1

    def body(o_ref, z_ref, w_ref, d_ref, do_ref, dz_ref, dw_ref):
        dw = jnp.zeros(w.shape, F32)
        for h in range(N_HEADS_A):
            _, vjp = jax.vjp(_f_gnorm, o_ref[:, _hsl(h)], z_ref[:, _hsl(h)], w_ref[...])
            do, dz, dwh = vjp(d_ref[:, _hsl(h)])
            do_ref[:, _hsl(h)] = do.astype(do_ref.dtype)
            dz_ref[:, _hsl(h)] = dz.astype(dz_ref.dtype)
            dw = dw + dwh
        first = pl.program_id(0) == 0

        @pl.when(first)
        def _():
            dw_ref[...] = dw

        @pl.when(jnp.logical_not(first))
        def _():
            dw_ref[...] += dw

    rows = pl.BlockSpec((tm, width), lambda i: (i, 0))
    wspec = pl.BlockSpec(w.shape, lambda i: (0, 0))
    return pl.pallas_call(
        body, grid=(t // tm,),
        in_specs=[rows, pl.BlockSpec((tm, width), lambda i: (i, zoff)), wspec, rows],
        out_specs=[rows, rows, wspec],
        out_shape=[jax.ShapeDtypeStruct((t, width), BF16)] * 2 + [jax.ShapeDtypeStruct(w.shape, F32)], name=name,
        compiler_params=_params(("arbitrary",)),
    )(o, zsrc, w, don)


def delta_prep(cqkv, bg, *, name):
    t = cqkv.shape[0]
    nh, hd, n = N_HEADS_A, HEAD_DIM_A, t // CHUNK

    def body(cq_ref, ck_ref, cv_ref, bg_ref, u0_ref, wk_ref, qd_ref, kd_ref, qk_ref, tm_ref, gl_ref):
        heads = range(nh)
        masks = [_head_masks(j) for j in heads]
        res = _prep_fn([cq_ref[:, _hsl(j)] for j in heads], [ck_ref[:, _hsl(j)] for j in heads],
                       [cv_ref[:, _hsl(j)] for j in heads], bg_ref[...], [m[0] for m in masks], [m[1] for m in masks])
        for o_ref, rs in zip((u0_ref, wk_ref, qd_ref, kd_ref, qk_ref, tm_ref), res[:5] + (res[6],)):
            for j in heads:
                o_ref[:, _hsl(j)] = rs[j].astype(o_ref.dtype)
        for j in heads:
            gl_ref[j * SUBLANES:(j + 1) * SUBLANES, :] = res[5][j]

    blk = lambda off: pl.BlockSpec((CHUNK, nh * hd), lambda i: (i, off))
    res = pl.pallas_call(
        body, grid=(n,),
        in_specs=[blk(0), blk(1), blk(2), pl.BlockSpec((CHUNK, LANES), lambda i: (i, 0))],
        out_specs=[blk(0)] * 6 + [pl.BlockSpec((nh * SUBLANES, LANES), lambda i: (i, 0))],
        out_shape=[jax.ShapeDtypeStruct((t, nh * hd), dt) for dt in (F32, BF16, BF16, BF16, BF16, F32)]
        + [jax.ShapeDtypeStruct((n * nh * SUBLANES, LANES), F32)],
        name=name, compiler_params=_params(("parallel",)),
    )(cqkv, cqkv, cqkv, bg)
    return [*res[:5], res[6]], res[5]


def delta_prep_bwd(cqkv, bg, tms, cts, *, name):
    t = cqkv.shape[0]
    nh, hd, n = N_HEADS_A, HEAD_DIM_A, t // CHUNK

    def body(cq_ref, ck_ref, cv_ref, bg_ref, tm_ref, c0, c1, c2, c3, c4, c5, dc_ref, dbg_ref):
        heads = range(nh)
        masks = [_head_masks(j) for j in heads]
        known = [tm_ref[:, _hsl(j)] for j in heads]
        _, vjp = jax.vjp(lambda a, b, c, d: _prep_fn(a, b, c, d, [m[0] for m in masks], [m[1] for m in masks], known)[:6],
                         [cq_ref[:, _hsl(j)] for j in heads], [ck_ref[:, _hsl(j)] for j in heads],
                         [cv_ref[:, _hsl(j)] for j in heads], bg_ref[...])
        cts = tuple([c[:, _hsl(j)] for j in heads] for c in (c0, c1, c2, c3, c4))
        dqs, dks, dvs, dbg = vjp(cts + ([c5[j * SUBLANES:(j + 1) * SUBLANES, :] for j in heads],))
        for part, ds in enumerate((dqs, dks, dvs)):
            for j in heads:
                dc_ref[:, _hsl(part * nh + j)] = ds[j]
        dbg_ref[...] = dbg

    blk = lambda off: pl.BlockSpec((CHUNK, nh * hd), lambda i: (i, off))
    gl_spec = pl.BlockSpec((nh * SUBLANES, LANES), lambda i: (i, 0))
    bg_spec = pl.BlockSpec((CHUNK, LANES), lambda i: (i, 0))
    return pl.pallas_call(
        body, grid=(n,),
        in_specs=[blk(0), blk(1), blk(2), bg_spec] + [blk(0)] * 6 + [gl_spec],
        out_specs=[pl.BlockSpec((CHUNK, 3 * nh * hd), lambda i: (i, 0)), bg_spec],
        out_shape=[jax.ShapeDtypeStruct((t, 3 * nh * hd), F32), jax.ShapeDtypeStruct((t, LANES), F32)],
        name=name, compiler_params=_params(("parallel",)),
    )(cqkv, cqkv, cqkv, bg, tms, *cts)


def delta_scan(u0, wk, qd, kd, qk, gl, *, name):
    t = u0.shape[0]
    nh, hd, n = N_HEADS_A, HEAD_DIM_A, t // CHUNK

    def body(u0_ref, wk_ref, qd_ref, kd_ref, qk_ref, gl_ref, o_ref, sin_ref, s_ref):
        @pl.when(pl.program_id(0) == 0)
        def _():
            s_ref[...] = jnp.zeros_like(s_ref)

        heads = list(range(nh))
        cols = lambda ref: [ref[:, _hsl(h)].astype(BF16) for h in heads]
        ss = [s_ref[h] for h in heads]
        for h in heads:
            sin_ref[h] = ss[h]
        sbs = _each(lambda s: s.astype(BF16), ss)
        ubs = _each(lambda h, wkb, sb: (u0_ref[:, _hsl(h)] - _dot(wkb, sb)).astype(BF16), heads, cols(wk_ref), sbs)
        os_ = _each(lambda qdb, sb, qkb, ub: _dot(qdb, sb) + _dot(qkb, ub), cols(qd_ref), sbs, cols(qk_ref), ubs)
        sn = _each(lambda h, s, kdb, ub: gl_ref[pl.ds(h * SUBLANES, 1), :] * s + _dot(kdb, ub, TN), heads, ss, cols(kd_ref), ubs)
        for h in heads:
            o_ref[:, _hsl(h)] = os_[h]
            s_ref[h] = sn[h]

    blk = pl.BlockSpec((CHUNK, nh * hd), lambda i: (i, 0))
    return pl.pallas_call(
        body, grid=(n,),
        in_specs=[blk] * 5 + [pl.BlockSpec((nh * SUBLANES, LANES), lambda i: (i, 0))],
        out_specs=[blk, pl.BlockSpec((None, nh, hd, hd), lambda i: (i, 0, 0, 0))],
        out_shape=[jax.ShapeDtypeStruct((t, nh * hd), F32), jax.ShapeDtypeStruct((n, nh, hd, hd), F32)],
        scratch_shapes=[pltpu.VMEM((nh, hd, hd), F32)], name=name,
        compiler_params=_params(("arbitrary",)),
    )(u0, wk, qd, kd, qk, gl)


def delta_scan_bwd(do, u0, wk, qd, kd, qk, gl, s_in, *, name):
    t = u0.shape[0]
    nh, hd, n = N_HEADS_A, HEAD_DIM_A, t // CHUNK

    def body(do_ref, u0_ref, wk_ref, qd_ref, kd_ref, qk_ref, gl_ref, sin_ref,
             du0_ref, dwk_ref, dqd_ref, dkd_ref, dqk_ref, dgl_ref, ds_ref):
        @pl.when(pl.program_id(0) == 0)
        def _():
            ds_ref[...] = jnp.zeros_like(ds_ref)

        corner = (lax.broadcasted_iota(jnp.int32, (SUBLANES, LANES), 0) == 0) & (lax.broadcasted_iota(jnp.int32, (SUBLANES, LANES), 1) == 0)
        heads = list(range(nh))
        cols = lambda ref: [ref[:, _hsl(h)].astype(BF16) for h in heads]
        ss, dss = [sin_ref[h] for h in heads], [ds_ref[h] for h in heads]
        sbs, dsbs = _each(lambda s: s.astype(BF16), ss), _each(lambda d: d.astype(BF16), dss)
        dobs, wkbs, qdbs, kdbs, qkbs = cols(do_ref), cols(wk_ref), cols(qd_ref), cols(kd_ref), cols(qk_ref)
        ubs = _each(lambda h, wkb, sb: (u0_ref[:, _hsl(h)] - _dot(wkb, sb)).astype(BF16), heads, wkbs, sbs)
        dus = _each(lambda qkb, dob, kdb, dsb: _dot(qkb, dob, TN) + _dot(kdb, dsb), qkbs, dobs, kdbs, dsbs)
        dubs = _each(lambda du: du.astype(BF16), dus)
        dwks = _each(lambda dub, sb: -_dot(dub, sb, NT), dubs, sbs)
        dqds = _each(lambda dob, sb: _dot(dob, sb, NT), dobs, sbs)
        dkds = _each(lambda ub, dsb: _dot(ub, dsb, NT), ubs, dsbs)
        dqks = _each(lambda dob, ub: _dot(dob, ub, NT), dobs, ubs)
        dgls = _each(lambda s, d: jnp.sum(jnp.sum(s * d, axis=1, keepdims=True), axis=0, keepdims=True), ss, dss)
        dsn = _each(lambda h, d, qdb, dob, wkb, dub: gl_ref[pl.ds(h * SUBLANES, 1), :] * d + _dot(qdb, dob, TN) - _dot(wkb, dub, TN),
                    heads, dss, qdbs, dobs, wkbs, dubs)
        for h in heads:
            du0_ref[:, _hsl(h)] = dus[h]
            dwk_ref[:, _hsl(h)] = dwks[h]
            dqd_ref[:, _hsl(h)] = dqds[h]
            dkd_ref[:, _hsl(h)] = dkds[h]
            dqk_ref[:, _hsl(h)] = dqks[h]
            dgl_ref[h * SUBLANES:(h + 1) * SUBLANES, :] = jnp.where(corner, dgls[h], 0.0)
            ds_ref[h] = dsn[h]

    blk = pl.BlockSpec((CHUNK, nh * hd), lambda i: (n - 1 - i, 0))
    gl_spec = pl.BlockSpec((nh * SUBLANES, LANES), lambda i: (n - 1 - i, 0))
    return pl.pallas_call(
        body, grid=(n,),
        in_specs=[blk] * 6 + [gl_spec, pl.BlockSpec((None, nh, hd, hd), lambda i: (n - 1 - i, 0, 0, 0))],
        out_specs=[blk] * 5 + [gl_spec],
        out_shape=[jax.ShapeDtypeStruct((t, nh * hd), F32)] * 5 + [jax.ShapeDtypeStruct((n * nh * SUBLANES, LANES), F32)],
        scratch_shapes=[pltpu.VMEM((nh, hd, hd), F32)], name=name,
        compiler_params=_params(("arbitrary",)),
    )(do, u0, wk, qd, kd, qk, gl, s_in)


N_PAIRS = N_HEADS_B // 2
PAIRS_PER_KV = N_PAIRS // N_KV_B


def _psl(j):
    return slice(j * LANES, (j + 1) * LANES)


KV_STEP = 4


def _att_fn(qps, kcs, kps, vcs, vps, sinks, kv0, first):
    w = WINDOW
    lane = lax.broadcasted_iota(jnp.int32, (1, LANES), 1)
    lo = (lane < HEAD_DIM_B).astype(F32)
    qi = lax.broadcasted_iota(jnp.int32, (w, w), 0)
    kj = lax.broadcasted_iota(jnp.int32, (w, w), 1)
    dist_c = (qi - kj).astype(F32)
    valid_c = kj <= qi
    valid_p = (kj > qi) & (first < 0.5)
    bf = lambda xs: [a.astype(BF16) for a in xs]
    kcb, kpb, vcb, vpb = bf(kcs), bf(kps), bf(vcs), bf(vps)
    scale = HEAD_DIM_B ** -0.5
    heads = [(g, j, half) for g in range(len(kcs)) for j in range(PAIRS_PER_KV) for half in range(2)]
    kvs = [g for g, _, _ in heads]
    hmasks = [lo if half == 0 else 1.0 - lo for _, _, half in heads]
    hds = [2.0 * (PAIRS_PER_KV * (kv0 + g) + j) + half for g, j, half in heads]
    slopes = _each(lambda hd: jnp.exp(-(hd + 1.0) * (8.0 / N_HEADS_B * math.log(2.0))), hds)
    snks = _each(lambda hd: jnp.sum(sinks * (lane.astype(F32) == hd).astype(F32), axis=1, keepdims=True), hds)
    qhs = _each(lambda h, hm: (qps[h[0] * PAIRS_PER_KV + h[1]] * hm).astype(BF16), heads, hmasks)
    lcs = _each(lambda qh, g, sl: jnp.where(valid_c, _dot(qh, kcb[g], NT) * scale - sl * dist_c, NEG), qhs, kvs, slopes)
    lps = _each(lambda qh, g, sl: jnp.where(valid_p, _dot(qh, kpb[g], NT) * scale - sl * (dist_c + w), NEG), qhs, kvs, slopes)
    ms = _each(lambda lc, lp, sk: lax.stop_gradient(jnp.maximum(jnp.maximum(jnp.max(lc, axis=1, keepdims=True),
                                                                            jnp.max(lp, axis=1, keepdims=True)), sk)), lcs, lps, snks)
    ecs = _each(lambda lc, m: jnp.exp(lc - m), lcs, ms)
    eps = _each(lambda lp, m: jnp.exp(lp - m), lps, ms)
    invs = _each(lambda ec, ep, sk, m: 1.0 / (jnp.sum(ec, axis=1, keepdims=True) + jnp.sum(ep, axis=1, keepdims=True) + jnp.exp(sk - m)),
                 ecs, eps, snks, ms)
    ohs = _each(lambda ec, ep, inv, g, hm: (_dot((ec * inv).astype(BF16), vcb[g]) + _dot((ep * inv).astype(BF16), vpb[g])) * hm,
                ecs, eps, invs, kvs, hmasks)
    return [ohs[2 * j] + ohs[2 * j + 1] for j in range(len(qps))]


def _scalar11(v):
    return jnp.full((1, 1), v, F32)


def _att_specs(row_of):
    cur = pl.BlockSpec((WINDOW, KV_STEP * LANES), lambda i, kv: (row_of(i), kv))
    prev = pl.BlockSpec((WINDOW, KV_STEP * LANES), lambda i, kv: (jnp.maximum(row_of(i) - 1, 0), kv))
    qs = pl.BlockSpec((WINDOW, KV_STEP * PAIRS_PER_KV * LANES), lambda i, kv: (row_of(i), kv))
    return qs, cur, prev, pl.BlockSpec((1, LANES), lambda i, kv: (0, 0))


def swa_fwd(qsrc, kd, vd, sinks, *, name):
    t = kd.shape[0]
    nb = t // WINDOW
    npair = KV_STEP * PAIRS_PER_KV

    def body(q_ref, kc_ref, kp_ref, vc_ref, vp_ref, s_ref, o_ref):
        first = _scalar11((pl.program_id(0) == 0).astype(F32))
        kv0 = _scalar11((pl.program_id(1) * KV_STEP).astype(F32))
        per_kv = lambda ref: [ref[:, _psl(g)] for g in range(KV_STEP)]
        outs = _att_fn([q_ref[:, _psl(j)] for j in range(npair)], per_kv(kc_ref), per_kv(kp_ref), per_kv(vc_ref), per_kv(vp_ref),
                       s_ref[...], kv0, first)
        for j in range(npair):
            o_ref[:, _psl(j)] = outs[j].astype(o_ref.dtype)

    qs, cur, prev, sk = _att_specs(lambda i: i)
    return pl.pallas_call(
        body, grid=(nb, N_KV_B // KV_STEP), in_specs=[qs, cur, prev, cur, prev, sk],
        out_specs=qs, out_shape=jax.ShapeDtypeStruct((t, N_PAIRS * LANES), BF16), name=name,
        compiler_params=_params(("parallel", "parallel")),
    )(qsrc, kd, kd, vd, vd, sinks)


def swa_bwd(do, qsrc, kd, vd, sinks, *, name):
    t = kd.shape[0]
    nb = t // WINDOW

    npair = KV_STEP * PAIRS_PER_KV

    def body(do_ref, q_ref, kc_ref, kp_ref, vc_ref, vp_ref, s_ref, dq_ref, dk_ref, dv_ref, ds_ref, carry_k, carry_v):
        step, kvg = pl.program_id(0), pl.program_id(1)
        first = _scalar11((step == nb - 1).astype(F32))

        @pl.when((step == 0) & (kvg == 0))
        def _():
            carry_k[...] = jnp.zeros_like(carry_k)
            carry_v[...] = jnp.zeros_like(carry_v)
            ds_ref[...] = jnp.zeros_like(ds_ref)

        kv0 = _scalar11((kvg * KV_STEP).astype(F32))
        per_kv = lambda ref: [ref[:, _psl(g)].astype(F32) for g in range(KV_STEP)]
        _, vjp = jax.vjp(lambda *a: _att_fn(*a, kv0, first), [q_ref[:, _psl(j)].astype(F32) for j in range(npair)],
                         per_kv(kc_ref), per_kv(kp_ref), per_kv(vc_ref), per_kv(vp_ref), s_ref[...])
        dqs, dkc, dkp, dvc, dvp, dsk = vjp([do_ref[:, _psl(j)].astype(F32) for j in range(npair)])
        for j in range(npair):
            dq_ref[:, _psl(j)] = dqs[j].astype(dq_ref.dtype)
        ds_ref[...] += dsk
        fold = lambda g: g + pltpu.roll(g, HEAD_DIM_B, 1)
        for g in range(KV_STEP):
            kv = kvg * KV_STEP + g
            dk_ref[:, _psl(g)] = fold(dkc[g] + carry_k[kv]).astype(dk_ref.dtype)
            dv_ref[:, _psl(g)] = fold(dvc[g] + carry_v[kv]).astype(dv_ref.dtype)
            carry_k[kv] = dkp[g]
            carry_v[kv] = dvp[g]

    qs, cur, prev, sk = _att_specs(lambda i: nb - 1 - i)
    return pl.pallas_call(
        body, grid=(nb, N_KV_B // KV_STEP),
        in_specs=[qs, qs, cur, prev, cur, prev, sk],
        out_specs=[qs, cur, cur, sk],
        out_shape=[jax.ShapeDtypeStruct((t, N_PAIRS * LANES), BF16), jax.ShapeDtypeStruct((t, N_KV_B * LANES), BF16),
                   jax.ShapeDtypeStruct((t, N_KV_B * LANES), BF16), jax.ShapeDtypeStruct((1, LANES), F32)],
        scratch_shapes=[pltpu.VMEM((N_KV_B, WINDOW, LANES), F32), pltpu.VMEM((N_KV_B, WINDOW, LANES), F32)],
        name=name, compiler_params=_params(("arbitrary", "arbitrary")),
    )(do, qsrc, kd, kd, vd, vd, sinks)


def loss_head(h, tgt, w, *, name):
    t, d = h.shape
    tm = min(ROW_TM, t)

    def body(h_ref, t_ref, w_ref, dh_ref, dw_ref, l_ref):
        tg = t_ref[...]

        def f(hv, wv):
            err = _f_norm(hv, wv) - tg
            return 0.5 * jnp.sum(jnp.sum(err * err, axis=1, keepdims=True), axis=0, keepdims=True) * (1.0 / d)

        lv, vjp = jax.vjp(f, h_ref[...], w_ref[...])
        dh, dw = vjp(jnp.ones((1, 1), F32))
        dh_ref[...] = dh
        first = pl.program_id(0) == 0

        @pl.when(first)
        def _():
            dw_ref[...] = dw
            l_ref[...] = lv * jnp.ones((1, LANES), F32)

        @pl.when(jnp.logical_not(first))
        def _():
            dw_ref[...] += dw
            l_ref[...] += lv * jnp.ones((1, LANES), F32)

    rows = pl.BlockSpec((tm, d), lambda i: (i, 0))
    one = lambda c: pl.BlockSpec((1, c), lambda i: (0, 0))
    return pl.pallas_call(
        body, grid=(t // tm,), in_specs=[rows, rows, one(d)], out_specs=[rows, one(d), one(LANES)],
        out_shape=[jax.ShapeDtypeStruct((t, d), F32), jax.ShapeDtypeStruct((1, d), F32), jax.ShapeDtypeStruct((1, LANES), F32)],
        name=name, compiler_params=_params(("arbitrary",)),
    )(h, tgt, w)


def _row_tile(r, cap=256):
    tr = r
    if r % SUBLANES == 0:
        for cand in range(SUBLANES, min(r, cap) + 1, SUBLANES):
            if r % cand == 0:
                tr = cand
    return tr


def _adamw_update(wv, gv, mv, vv):
    mn = ADAM_B1 * mv + (1.0 - ADAM_B1) * gv
    vn = ADAM_B2 * vv + (1.0 - ADAM_B2) * jnp.square(gv)
    m_hat = mn / (1.0 - ADAM_B1 ** ADAM_STEP)
    v_hat = vn / (1.0 - ADAM_B2 ** ADAM_STEP)
    return -ADAM_LR * (m_hat / (jnp.sqrt(v_hat) + ADAM_EPS) + ADAM_WD * wv), mn, vn


def adamw_layers(w, halves, m, v, *, name):
    nl, r, c = w.shape
    tr = _row_tile(r // 2)
    nbh = r // 2 // tr

    def body(w_ref, *rest):
        g_refs, m_ref, v_ref = rest[:2 * nl], rest[2 * nl], rest[2 * nl + 1]
        d_ref, mo_ref, vo_ref, go_ref = rest[2 * nl + 2:]
        layer, i = pl.program_id(0), pl.program_id(1)
        mine = (i < nbh) == (lax.axis_index("c") == 0)
        gv = jnp.where(mine, g_refs[0][...], g_refs[1][...])
        for k in range(1, nl):
            gv = jnp.where(layer == k, jnp.where(mine, g_refs[2 * k][...], g_refs[2 * k + 1][...]), gv)
        d_ref[...], mo_ref[...], vo_ref[...] = _adamw_update(w_ref[...], gv, m_ref[...], v_ref[...])
        go_ref[...] = gv

    spec3 = pl.BlockSpec((None, tr, c), lambda k, i: (k, i, 0))
    g_specs = [pl.BlockSpec((tr, c), lambda k, i, q=q: (jnp.where(k == q, i % nbh, 0), 0)) for q in range(nl) for _ in range(2)]
    return pl.pallas_call(
        body, grid=(nl, r // tr), in_specs=[spec3] + g_specs + [spec3, spec3], out_specs=[spec3] * 4,
        out_shape=[jax.ShapeDtypeStruct((nl, r, c), F32)] * 4, name=name, compiler_params=_params(("arbitrary", "arbitrary")),
    )(w, *[h for pair in halves for h in pair], m, v)


def adamw(w, g, m, v, *, name):
    r, c = w.shape
    tr = _row_tile(r)

    def body(w_ref, g_ref, m_ref, v_ref, d_ref, mo_ref, vo_ref):
        d_ref[...], mo_ref[...], vo_ref[...] = _adamw_update(w_ref[...], g_ref[...], m_ref[...], v_ref[...])

    spec = pl.BlockSpec((tr, c), lambda i: (i, 0))
    return pl.pallas_call(
        body, grid=(r // tr,), in_specs=[spec] * 4, out_specs=[spec] * 3,
        out_shape=[jax.ShapeDtypeStruct((r, c), F32)] * 3, name=name, compiler_params=_params(("parallel",)),
    )(w, g, m, v)


def _place():
    return lax.axis_index("x"), lax.axis_index("y"), lax.axis_index("c")


def allgather8(blk, *, name):
    def body(x_ref, out_ref, send_sems, recv_sems, local_sem):
        x, y, c = _place()
        me = 4 * x + 2 * y + c
        mine = pltpu.make_async_copy(x_ref, out_ref.at[me], local_sem)
        mine.start()
        sent = []
        for k in range(1, N_DEV):
            to = (x ^ ((k >> 2) & 1), y ^ ((k >> 1) & 1), c ^ (k & 1))
            cp = pltpu.make_async_remote_copy(src_ref=x_ref, dst_ref=out_ref.at[me], send_sem=send_sems.at[k - 1],
                                              recv_sem=recv_sems.at[k - 1], device_id=to, device_id_type=MESH)
            cp.start()
            sent.append(cp)
        for k in range(1, N_DEV):
            frm = me ^ k
            pltpu.make_async_remote_copy(src_ref=x_ref, dst_ref=out_ref.at[frm], send_sem=send_sems.at[k - 1],
                                         recv_sem=recv_sems.at[k - 1], device_id=(x, y, c), device_id_type=MESH).wait_recv()
        for cp in sent:
            cp.wait_send()
        mine.wait()

    vm = pl.BlockSpec(memory_space=pltpu.VMEM)
    return pl.pallas_call(
        body, in_specs=[vm], out_specs=vm, out_shape=jax.ShapeDtypeStruct((N_DEV,) + blk.shape, blk.dtype), name=name,
        scratch_shapes=[pltpu.SemaphoreType.DMA((N_DEV - 1,)), pltpu.SemaphoreType.DMA((N_DEV - 1,)), pltpu.SemaphoreType.DMA],
    )(blk)


def _other_chips(x, y):
    return [(1 - x, y), (x, 1 - y), (1 - x, 1 - y)]


def _hbm_call(body, ins, out_shapes, n_sems, name):
    hbm = pl.BlockSpec(memory_space=pl.ANY)
    return pl.pallas_call(
        body, in_specs=[hbm] * len(ins), out_specs=[hbm] * len(out_shapes), out_shape=out_shapes, name=name,
        scratch_shapes=[pltpu.SemaphoreType.DMA((n_sems,)), pltpu.SemaphoreType.DMA((n_sems,))],
    )(*ins)


def _half_rows(c, rh):
    return pl.ds(pl.multiple_of(c * rh, BF16_ROWS), rh)


def gather_units(units, *, name):
    nu = len(units)
    shapes = []
    for arr, layer_major in units:
        r, cols = arr.shape
        shapes.append(jax.ShapeDtypeStruct((2, N_CHIPS, r // 2, cols) if layer_major else (N_CHIPS, r, cols), arr.dtype))

    def body(*refs):
        in_refs, out_refs, send_sems, recv_sems = refs[:nu], refs[nu:2 * nu], refs[2 * nu], refs[2 * nu + 1]
        x, y, c = _place()
        me_chip = 2 * x + y
        sib = (x, y, 1 - c)
        chips = _other_chips(x, y)

        def copy(k, src, dst, to):
            return pltpu.make_async_remote_copy(src_ref=src, dst_ref=dst, send_sem=send_sems.at[k], recv_sem=recv_sems.at[k],
                                                device_id=to, device_id_type=MESH)

        first, passed, landing = [], [], []
        for u, (arr, layer_major) in enumerate(units):
            rh = arr.shape[0] // 2
            out_ref = out_refs[u]
            slot = (lambda chip, half, o=out_ref: o.at[half, chip]) if layer_major else \
                   (lambda chip, half, o=out_ref, rh=rh: o.at[chip, _half_rows(half, rh), :])
            my_half = in_refs[u].at[_half_rows(c, rh), :]
            for j, (cx, cy) in enumerate(chips):
                k = 6 * u + j
                first.append(copy(k, my_half, slot(me_chip, c), (cx, cy, c)))
                passed.append(copy(k + 3, slot(2 * cx + cy, c), slot(2 * cx + cy, c), sib))
                landing.append((copy(k, my_half, slot(2 * cx + cy, c), sib), copy(k + 3, my_half, slot(2 * cx + cy, 1 - c), sib)))
        for cp in first:
            cp.start()
        for (over_ici, _), fwd in zip(landing, passed):
            over_ici.wait_recv()
            fwd.start()
        for _, from_sibling in landing:
            from_sibling.wait_recv()
        for cp in first + passed:
            cp.wait_send()

    return _hbm_call(body, [a for a, _ in units], shapes, 6 * nu, name)


HBM_SPEC = pl.BlockSpec(memory_space=pltpu.HBM)
SEM_SPEC = pl.BlockSpec(memory_space=pltpu.SEMAPHORE)
ORDERED_EFFECT = pltpu.SideEffectType.DATAFLOW_SIDE_EFFECTING


def _split_start(body, srcs, land_shapes, after, *, name):
    nu = len(srcs)
    lands = [lax.empty(s.shape, s.dtype) for s in land_shapes]

    def whole(*refs):
        body(refs[:nu], refs[nu:2 * nu], refs[2 * nu + 1], refs[2 * nu + 2])
        refs[-1][...] = jnp.zeros((SUBLANES, LANES), F32)

    hbm = lambda a: pltpu.with_memory_space_constraint(a, pltpu.HBM)
    sems = pltpu.SemaphoreType.DMA((nu,))
    res = pl.pallas_call(
        whole, name=name, in_specs=[HBM_SPEC] * (2 * nu) + [pl.BlockSpec(memory_space=pl.ANY)],
        out_shape=[sems, sems] + [pltpu.HBM(a.shape, a.dtype) for a in srcs] + [pltpu.HBM(s.shape, s.dtype) for s in land_shapes]
        + [jax.ShapeDtypeStruct((SUBLANES, LANES), F32)],
        out_specs=[SEM_SPEC, SEM_SPEC] + [HBM_SPEC] * (2 * nu) + [pl.BlockSpec(memory_space=pltpu.VMEM)],
        input_output_aliases={q: 2 + q for q in range(2 * nu)},
        compiler_params=pltpu.CompilerParams(has_side_effects=ORDERED_EFFECT),
    )(*[hbm(a) for a in srcs], *[hbm(a) for a in lands], after)
    return res[0], res[1], res[2:2 + nu], res[2 + nu:2 + 2 * nu], res[-1]


def _split_wait(pending, moved, after, *, name):
    send_sems, recv_sems, srcs, lands, _ = pending
    nu = len(srcs)

    def body(*refs):
        land_refs, ssem, rsem = refs[nu:2 * nu], refs[2 * nu], refs[2 * nu + 1]
        x, y, c = _place()
        for u in range(nu):
            size = moved(land_refs[u])
            cp = pltpu.make_async_remote_copy(src_ref=size, dst_ref=size, send_sem=ssem.at[u], recv_sem=rsem.at[u],
                                              device_id=(x, y, c), device_id_type=MESH)
            cp.wait_send()
            cp.wait_recv()

    res = pl.pallas_call(
        body, name=name, in_specs=[HBM_SPEC] * (2 * nu) + [SEM_SPEC, SEM_SPEC, pl.BlockSpec(memory_space=pl.ANY)],
        out_shape=[pltpu.HBM(a.shape, a.dtype) for a in srcs] + [pltpu.HBM(a.shape, a.dtype) for a in lands],
        out_specs=[HBM_SPEC] * (2 * nu), input_output_aliases={q: q for q in range(2 * nu)},
        compiler_params=pltpu.CompilerParams(has_side_effects=ORDERED_EFFECT),
    )(*srcs, *lands, send_sems, recv_sems, after)
    return res[nu:]


def gather_start(shards, after, *, name):
    def body(src_refs, land_refs, send_sems, recv_sems):
        x, y, c = _place()
        for u, shard in enumerate(shards):
            rows = _half_rows(c, shard.shape[0] // 2)
            for cx, cy in _other_chips(x, y):
                for core in range(2):
                    pltpu.make_async_remote_copy(src_ref=src_refs[u].at[rows, :], dst_ref=land_refs[u].at[2 * x + y, rows, :],
                                                 send_sem=send_sems.at[u], recv_sem=recv_sems.at[u], device_id=(cx, cy, core),
                                                 device_id_type=MESH).start()

    return _split_start(body, shards, [jax.ShapeDtypeStruct((N_CHIPS,) + s.shape, s.dtype) for s in shards], after, name=name)


def gather_wait(pending, after, *, name):
    return _split_wait(pending, lambda land: land.at[pl.ds(0, N_CHIPS - 1)], after, name=name)


def scatter_start(pairs, *, name):
    def body(src_refs, land_refs, send_sems, recv_sems):
        x, y, c = _place()
        for u in range(len(pairs)):
            for j, (cx, cy) in enumerate(_other_chips(x, y)):
                pltpu.make_async_remote_copy(src_ref=src_refs[u].at[2 * cx + cy], dst_ref=land_refs[u].at[j], send_sem=send_sems.at[u],
                                             recv_sem=recv_sems.at[u], device_id=(cx, cy, c), device_id_type=MESH).start()

    return _split_start(body, pairs, [jax.ShapeDtypeStruct((N_CHIPS - 1,) + p.shape[1:], p.dtype) for p in pairs], pairs[0], name=name)


def scatter_wait(pending, after, *, name):
    return _split_wait(pending, lambda land: land, after, name=name)


def swap_units(units, *, name):
    nu = len(units)

    def body(*refs):
        g_refs, out_refs, send_sems, recv_sems = refs[:nu], refs[nu:2 * nu], refs[2 * nu], refs[2 * nu + 1]
        x, y, c = _place()
        cps = [pltpu.make_async_remote_copy(src_ref=g_refs[u].at[:, _half_rows(1 - c, units[u].shape[1] // 2), :], dst_ref=out_refs[u],
                                            send_sem=send_sems.at[u], recv_sem=recv_sems.at[u], device_id=(x, y, 1 - c),
                                            device_id_type=MESH) for u in range(nu)]
        for cp in cps:
            cp.start()
        for cp in cps:
            cp.wait()

    shapes = [jax.ShapeDtypeStruct((N_CHIPS, g.shape[1] // 2, g.shape[2]), g.dtype) for g in units]
    return _hbm_call(body, units, shapes, nu, name)


def join_units(units, *, name):
    nu = len(units)

    def body(*refs):
        h_refs, out_refs, send_sems, recv_sems = refs[:nu], refs[nu:2 * nu], refs[2 * nu], refs[2 * nu + 1]
        x, y, c = _place()
        cps = [pltpu.make_async_remote_copy(src_ref=h_refs[u], dst_ref=out_refs[u], send_sem=send_sems.at[u], recv_sem=recv_sems.at[u],
                                            device_id=(x, y, 1 - c), device_id_type=MESH) for u in range(nu)]
        for cp in cps:
            cp.start()
        for cp in cps:
            cp.wait()

    return _hbm_call(body, units, [jax.ShapeDtypeStruct(h.shape, h.dtype) for h in units], nu, name)


def _half_tile(rh):
    tr = rh
    for cand in range(BF16_ROWS, min(rh, 512) + 1, BF16_ROWS):
        if rh % cand == 0:
            tr = cand
    return tr


def pair_add(g, sib, *, name):
    nc, rh, cols = sib.shape
    tr = _half_tile(rh)
    nbh = rh // tr

    def body(g0_ref, g1_ref, s_ref, o_ref):
        mine = jnp.where(lax.axis_index("c") == 0, g0_ref[...], g1_ref[...])
        o_ref[...] = (mine.astype(F32) + s_ref[...].astype(F32)).astype(o_ref.dtype)

    blk = lambda off: pl.BlockSpec((None, tr, cols), lambda j, i: (j, off + i, 0))
    return pl.pallas_call(
        body, grid=(nc, nbh), in_specs=[blk(0), blk(nbh), blk(0)], out_specs=blk(0),
        out_shape=jax.ShapeDtypeStruct(sib.shape, BF16), name=name, compiler_params=_params(("parallel", "parallel")),
    )(g, g, sib)


def chips_add(pair, landed, *, name):
    nc, rh, cols = pair.shape
    tr = _half_tile(rh)

    def body(*refs):
        chip = 2 * lax.axis_index("x") + lax.axis_index("y")
        acc = refs[0][...]
        for j in range(1, nc):
            acc = jnp.where(chip == j, refs[j][...], acc)
        acc = acc.astype(F32)
        for r in refs[nc:-1]:
            acc = acc + r[...].astype(F32)
        refs[-1][...] = acc

    part = lambda q: pl.BlockSpec((None, tr, cols), lambda i, q=q: (q, i, 0))
    return pl.pallas_call(
        body, grid=(rh // tr,), in_specs=[part(q) for q in range(nc)] + [part(q) for q in range(landed.shape[0])],
        out_specs=pl.BlockSpec((tr, cols), lambda i: (i, 0)),
        out_shape=jax.ShapeDtypeStruct((rh, cols), F32), name=name, compiler_params=_params(("parallel",)),
    )(*[pair] * nc, *[landed] * landed.shape[0])


def sum8(g, *, name):
    def body(g_ref, o_ref):
        acc = g_ref[0]
        for d in range(1, N_DEV):
            acc = acc + g_ref[d]
        o_ref[...] = acc

    return pl.pallas_call(body, out_shape=jax.ShapeDtypeStruct(g.shape[1:], F32), name=name)(g)


def _dup_halves(a):
    t = a.shape[0]
    a = a.reshape(t, N_KV_B, HEAD_DIM_B)
    return jnp.concatenate([a, a], axis=-1).reshape(t, N_KV_B * LANES)


def _undup(a):
    t = a.shape[0]
    return a.reshape(t, N_KV_B, LANES)[:, :, :HEAD_DIM_B].reshape(t, N_KV_B * HEAD_DIM_B)


def _lane_pad(v, offset=0):
    return jnp.zeros((1, LANES), F32).at[0, offset:offset + v.shape[0]].set(v)


SHARD_UP = 2 * D_FF // N_CHIPS
SHARD_BIN = (N_HEADS_B + 2 * N_KV_B) * HEAD_DIM_B // N_CHIPS
SHARD_PROJ = D_MODEL // N_CHIPS


def local_step(x, p, tgt, sm, weight, on_grads):
    t = x.shape[0]
    rtm = min(ROW_TM, t)
    hk = N_HEADS_A * HEAD_DIM_A
    qd_b = N_HEADS_B * HEAD_DIM_B
    kd_b = N_KV_B * HEAD_DIM_B
    gs = {}
    norm = lambda h, w, nm: tile_map(_f_norm, [(h, D_MODEL, 0)], [w], [(D_MODEL, BF16)], tm=rtm, ncol=1, name=nm)[0]

    spec = pl.BlockSpec
    mtm = _tile(D_MODEL, MM_TM_CAP)
    p_bf = p.astype(BF16)
    alog_p = _lane_pad(sm["a_log"][0], N_HEADS_A)
    dtb_p = _lane_pad(sm["a_dt_bias"][0], N_HEADS_A)
    sinks_p = _lane_pad(sm["b_sinks"][0])
    nw = lambda name, i: sm[name][i:i + 1]
    by_chip = lambda kdim, ns: dict(tn=ns, tk=kdim, b_spec=spec((None, kdim, ns), lambda r, j, kk: (j, kk, 0)))
    by_chip_t = lambda ndim, ns: dict(n=ndim, tn=ndim, tk=ns, b_spec=spec((None, ndim, ns), lambda r, j, kk: (kk, j, 0)))
    cache = {}

    def wgt(name, i, after):
        if (name, i) not in cache:
            cache[name, i] = weight(name, i, after)
        return cache[name, i]

    saved = []
    h = x
    hn_next = norm(h, nw("norm_mix", 0), "norm_mix0")
    for i in range(DEPTH):
        s = {"h0": h, "hn": hn_next}
        if i % 2 == 0:
            s["pm"] = mm(s["hn"], wgt("a_w_in", i, h), name="a_in")
            tail = (s["pm"], LANES, 4 * hk // LANES)
            s["c"] = conv_fwd(s["pm"], wgt("a_conv", i, h), name="a_conv")
            s["bg"] = tile_map(_f_betag, [tail], [alog_p, dtb_p], [(LANES, F32)], tm=rtm, ncol=1, name="a_betag")[0]
            s["prep"], s["tms"] = delta_prep(s["c"], s["bg"], name="a_prep")
            s["o"], s["s_in"] = delta_scan(*s["prep"], name="a_scan")
            s["on"] = gnorm_fwd(s["o"], s["pm"], sm["a_norm"], name="a_gnorm")
            h, s["hf"] = mm(s["on"], wgt("a_w_out", i, s["on"]), add=h, norm_w=nw("norm_ffn", i), name="a_out")
        else:
            s["pb"] = mm(s["hn"], wgt("b_w_in", i, s["hn"]), name="b_in", out_dtype=BF16, n=N_CHIPS * SHARD_BIN,
                         **by_chip(D_MODEL, SHARD_BIN))
            s["kd"], s["vd"] = _dup_halves(s["pb"][:, qd_b:qd_b + kd_b]), _dup_halves(s["pb"][:, qd_b + kd_b:])
            s["ao"] = swa_fwd(s["pb"], s["kd"], s["vd"], sinks_p, name="b_att")
            h, s["hf"] = mm(s["ao"], wgt("b_w_out", i, s["ao"]), add=h, norm_w=nw("norm_ffn", i), name="b_out")
        s["h1"] = h
        s["u"] = mm(s["hf"], wgt("f_w_up", i, s["hf"]), name=f"f_up{i}", out_dtype=BF16, n=2 * D_FF, tm_cap=2 * MM_TM_CAP,
                    **by_chip(D_MODEL, SHARD_UP))
        s["act"] = conv_act_fwd(s["u"], wgt("f_conv", i, s["hf"]), name=f"f_conv_act{i}")
        h, s["hp"] = mm(s["act"], wgt("f_w_down", i, s["act"]), add=h, norm_w=nw("norm_ple", i), name=f"f_down{i}", tk=D_FF)
        s["h2"] = h
        s["gl"] = mm(s["hp"], wgt("ple_w_gate", i, s["hp"]), name=f"ple_gate{i}")
        s["pe"] = mm(p_bf[i], wgt("ple_w_proj", i, s["hp"]), name=f"ple_proj{i}", n=D_MODEL, **by_chip(PLE_DIM, SHARD_PROJ))
        rows3 = [(h, D_MODEL, 0), (s["gl"], D_MODEL, 0), (s["pe"], D_MODEL, 0)]
        if i + 1 < DEPTH:
            def mix_norm(hv, g, e, wn):
                hn = hv + _f_ple(g, e)
                return hn, _f_norm(hn, wn)
            h, hn_next = tile_map(mix_norm, rows3, [nw("norm_mix", i + 1)], [(D_MODEL, F32), (D_MODEL, BF16)], tm=rtm, ncol=1,
                                  name=f"ple_mix{i}")
        else:
            h = tile_map(lambda hv, g, e: hv + _f_ple(g, e), rows3, [], [(D_MODEL, F32)], tm=rtm, ncol=1, name=f"ple_mix{i}")[0]
        saved.append(s)

    dh, gnf, loss = loss_head(h, tgt, sm["norm_final"][None, :], name="loss_head")
    gs["norm_final"] = gnf[0]

    g_mix, g_ffn, g_ple, g_conv = ([None] * DEPTH for _ in range(4))
    zero = jnp.zeros((1, 1), F32)
    for i in reversed(range(DEPTH)):
        s, gw = saved[i], {}
        by_rows = lambda g: g.reshape(N_CHIPS, g.shape[0] // N_CHIPS, g.shape[1])
        (dgl, dpe), _ = tile_vjp(_f_ple, [(s["gl"], D_MODEL, 0), (s["pe"], D_MODEL, 0)], [], [(dh, D_MODEL, 0)], n_diff=2,
                                 tm=rtm, ncol=1, name=f"ple_mix_bwd{i}", grad_dtypes=[BF16, BF16])
        gw["ple_w_proj"] = mm(p_bf[i], dpe, ta=True, name=f"ple_proj_dw{i}", out_dtype=BF16, tn=SHARD_PROJ,
                              o_shape=(N_CHIPS, PLE_DIM, SHARD_PROJ), o_spec=spec((None, PLE_DIM, SHARD_PROJ), lambda r, j, kk: (j, r, 0)))
        gw["ple_w_gate"] = by_rows(mm(s["hp"], dgl, ta=True, name=f"ple_gate_dw{i}", out_dtype=BF16))
        fused = dict(tb=True, tm_cap=MM_TM_CAP // 2)
        small = dict(tb=True)
        dh, g_ple[i] = mm(dgl, cache["ple_w_gate", i], name=f"ple_gate_dx{i}", norm_grad=(s["h2"], nw("norm_ple", i) + zero, dh), **small)

        dact = mm(dh, cache["f_w_down", i], tb=True, name=f"f_down_dx{i}")
        gw["f_w_down"] = by_rows(mm(s["act"], dh, ta=True, name=f"f_down_dw{i}", out_dtype=BF16, tm_cap=D_FF // 2))
        du_halves = conv_act_bwd(s["u"], dact, cache["f_conv", i], name=f"f_conv_act_bwd{i}")
        g_conv[i] = jnp.concatenate(du_halves[2:], axis=1)
        dhf = g_up = None
        for half, du in enumerate(du_halves[:2]):
            c0 = half * (N_CHIPS // 2)
            g_up = mm(s["hf"], du, ta=True, name=f"f_up_dw{i}_{half}", out_dtype=BF16, tn=SHARD_UP, into=g_up,
                      o_shape=(N_CHIPS, D_MODEL, SHARD_UP), o_spec=spec((None, mtm, SHARD_UP), lambda r, j, kk, c0=c0: (c0 + j, r, 0)))
            last = dict(norm_grad=(s["h1"], nw("norm_ffn", i), dh), **fused) if half else dict(tb=True)
            dhf = mm(du, cache["f_w_up", i], name=f"f_up_dx{i}_{half}", n=D_MODEL, tn=D_MODEL, tk=SHARD_UP, add=dhf,
                     b_spec=spec((None, D_MODEL, SHARD_UP), lambda r, j, kk, c0=c0: (c0 + kk, j, 0)), **last)
        gw["f_w_up"] = g_up
        dh, g_ffn[i] = dhf
        token, gw = on_grads(i, "ffn", gw), {}
        w_out = cache["a_w_out" if i % 2 == 0 else "b_w_out", i]
        if token is not None:
            w_out = w_out + token[:1, :1].astype(BF16)

        if i % 2 == 0:
            don = mm(dh, w_out, tb=True, name="a_out_dx")
            gw["a_w_out"] = by_rows(mm(s["on"], dh, ta=True, name="a_out_dw", out_dtype=BF16))
            do, dz, gs["a_norm"] = gnorm_bwd(s["o"], s["pm"], sm["a_norm"], don, name="a_gnorm_bwd")
            dprep = delta_scan_bwd(do, *s["prep"], s["s_in"], name="a_scan_bwd")
            dc, dbg = delta_prep_bwd(s["c"], s["bg"], s["tms"], dprep, name="a_prep_bwd")
            (dpt,), (galog, gdtb) = tile_vjp(_f_betag, [(s["pm"], LANES, 4 * hk // LANES)], [alog_p, dtb_p], [(dbg, LANES, 0)], n_diff=1,
                                             tm=rtm, ncol=1, name="a_betag_bwd", grad_dtypes=[BF16])
            gs["a_log"] = galog[:, N_HEADS_A:2 * N_HEADS_A]
            gs["a_dt_bias"] = gdtb[:, N_HEADS_A:2 * N_HEADS_A]
            dqkv, gs["a_conv"] = conv_bwd(dc, s["pm"], cache["a_conv", i], name="a_conv_bwd")
            dpm = jnp.concatenate([dqkv, dz, dpt], axis=1)
            g_in = mm(s["hn"], dpm, ta=True, name="a_in_dw", out_dtype=BF16)[:, :4 * hk + 2 * N_HEADS_A]
            gw["a_w_in"] = g_in.reshape(D_MODEL, N_CHIPS, g_in.shape[1] // N_CHIPS).transpose(1, 0, 2)
            dh, g_mix[i] = mm(dpm, cache["a_w_in", i], name="a_in_dx", norm_grad=(s["h0"], nw("norm_mix", i), dh), **small)
        else:
            dao = mm(dh, w_out, tb=True, name="b_out_dx")
            gw["b_w_out"] = by_rows(mm(s["ao"], dh, ta=True, name="b_out_dw", out_dtype=BF16))
            dq, dkd, dvd, gsk = swa_bwd(dao, s["pb"], s["kd"], s["vd"], sinks_p, name="b_att_bwd")
            gs["b_sinks"] = gsk[:, :N_HEADS_B]
            dpb = jnp.concatenate([dq, _undup(dkd), _undup(dvd)], axis=1)
            gw["b_w_in"] = mm(s["hn"], dpb, ta=True, name="b_in_dw", out_dtype=BF16, tn=SHARD_BIN,
                              o_shape=(N_CHIPS, D_MODEL, SHARD_BIN), o_spec=spec((None, mtm, SHARD_BIN), lambda r, j, kk: (j, r, 0)))
            dh, g_mix[i] = mm(dpb, cache["b_w_in", i], name="b_in_dx", norm_grad=(s["h0"], nw("norm_mix", i), dh), **small,
                              **by_chip_t(D_MODEL, SHARD_BIN))
        token = on_grads(i, "mix", gw)
        if token is not None:
            zero = token[:1, :1]

    gs["norm_mix"], gs["norm_ffn"], gs["norm_ple"] = (jnp.concatenate(g, axis=0) for g in (g_mix, g_ffn, g_ple))
    gs["f_conv"] = jnp.stack(g_conv)
    return loss, dh, gs


BIG = ["a_w_in", "a_w_out", "b_w_in", "b_w_out", "f_w_up", "f_w_down", "ple_w_proj", "ple_w_gate"]
LAYERED = {"f_w_up", "f_w_down", "ple_w_proj", "ple_w_gate"}
BY_CHIP = {"b_w_in", "f_w_up", "ple_w_proj"}
LAYER_UNITS = [[("a_w_in", 0), ("a_w_out", 0)] + [(n, 0) for n in sorted(LAYERED)],
               [("b_w_in", 1), ("b_w_out", 1)] + [(n, 1) for n in sorted(LAYERED)]]
CONVS = ["a_conv", "f_conv"]
SMALL = ["norm_mix", "norm_ffn", "norm_ple", "norm_final", "a_log", "a_dt_bias", "a_norm", "b_sinks"]
SMALL_ROWS = 8
CONV_ROWS = 16
CONV_GRAD_ROWS = 48


def _pack_rows(arrs, rows, dtype):
    flat = jnp.concatenate([a.reshape(-1).astype(dtype) for a in arrs])
    return jnp.pad(flat, (0, rows * PACK_COLS - flat.shape[0])).reshape(rows, PACK_COLS)


def _unpack(flat, shapes):
    out, off = [], 0
    for shp in shapes:
        n = math.prod(shp)
        out.append(flat[off:off + n].reshape(shp))
        off += n
    return out


def _pack_small(d, loss=None):
    tail = jnp.concatenate([d["a_log"].reshape(-1), d["a_dt_bias"].reshape(-1), d["a_norm"].reshape(-1), d["b_sinks"].reshape(-1)])
    if loss is not None:
        tail = jnp.concatenate([tail, loss.reshape(-1)[:1]])
    tail = jnp.pad(tail, (0, PACK_COLS - tail.shape[0]))
    return jnp.concatenate([d["norm_mix"], d["norm_ffn"], d["norm_ple"], d["norm_final"][None, :], tail[None, :]], axis=0)


def _unpack_small(a, like):
    out = {"norm_mix": a[0:2], "norm_ffn": a[2:4], "norm_ple": a[4:6], "norm_final": a[6]}
    off = 0
    for nm in ("a_log", "a_dt_bias", "a_norm", "b_sinks"):
        n = like[nm].size
        out[nm] = a[7, off:off + n].reshape(like[nm].shape)
        off += n
    return out, a[7, off]


def _as2d(a):
    return a.reshape(-1, a.shape[-1])


def kernel(x, p, norm_mix, norm_ffn, norm_ple, norm_final, a_w_in, a_conv, a_log, a_dt_bias, a_norm, a_w_out, b_w_in, b_sinks, b_w_out, f_w_up, f_conv, f_w_down, ple_w_proj, ple_w_gate, loss_target, m_norm_mix, m_norm_ffn, m_norm_ple, m_norm_final, m_a_w_in, m_a_conv, m_a_log, m_a_dt_bias, m_a_norm, m_a_w_out, m_b_w_in, m_b_sinks, m_b_w_out, m_f_w_up, m_f_conv, m_f_w_down, m_ple_w_proj, m_ple_w_gate, v_norm_mix, v_norm_ffn, v_norm_ple, v_norm_final, v_a_w_in, v_a_conv, v_a_log, v_a_dt_bias, v_a_norm, v_a_w_out, v_b_w_in, v_b_sinks, v_b_w_out, v_f_w_up, v_f_conv, v_f_w_down, v_ple_w_proj, v_ple_w_gate):
    w = dict(norm_mix=norm_mix, norm_ffn=norm_ffn, norm_ple=norm_ple, norm_final=norm_final, a_w_in=a_w_in, a_conv=a_conv,
             a_log=a_log, a_dt_bias=a_dt_bias, a_norm=a_norm, a_w_out=a_w_out, b_w_in=b_w_in, b_sinks=b_sinks, b_w_out=b_w_out,
             f_w_up=f_w_up, f_conv=f_conv, f_w_down=f_w_down, ple_w_proj=ple_w_proj, ple_w_gate=ple_w_gate)
    m = dict(norm_mix=m_norm_mix, norm_ffn=m_norm_ffn, norm_ple=m_norm_ple, norm_final=m_norm_final, a_w_in=m_a_w_in,
             a_conv=m_a_conv, a_log=m_a_log, a_dt_bias=m_a_dt_bias, a_norm=m_a_norm, a_w_out=m_a_w_out, b_w_in=m_b_w_in,
             b_sinks=m_b_sinks, b_w_out=m_b_w_out, f_w_up=m_f_w_up, f_conv=m_f_conv, f_w_down=m_f_w_down,
             ple_w_proj=m_ple_w_proj, ple_w_gate=m_ple_w_gate)
    v = dict(norm_mix=v_norm_mix, norm_ffn=v_norm_ffn, norm_ple=v_norm_ple, norm_final=v_norm_final, a_w_in=v_a_w_in,
             a_conv=v_a_conv, a_log=v_a_log, a_dt_bias=v_a_dt_bias, a_norm=v_a_norm, a_w_out=v_a_w_out, b_w_in=v_b_w_in,
             b_sinks=v_b_sinks, b_w_out=v_b_w_out, f_w_up=v_f_w_up, f_conv=v_f_conv, f_w_down=v_f_w_down,
             ple_w_proj=v_ple_w_proj, ple_w_gate=v_ple_w_gate)
    xc, yc, cc = _place()
    my_chip = 2 * xc + yc

    shard = {(n, i): w[n][i if n in LAYERED else 0].astype(BF16) for n, i in LAYER_UNITS[0] + LAYER_UNITS[1]}
    first = shard["a_w_in", 0]
    (ga,) = gather_units([(first, False)], name="gather_first")
    ga = lax.dynamic_update_index_in_dim(ga, first, my_chip, 0)
    a_in = jnp.concatenate([ga[j] for j in range(N_CHIPS)], axis=1)
    n_main = 4 * N_HEADS_A * HEAD_DIM_A
    conv_shapes = [w[n].shape for n in CONVS]
    convs = allgather8(_pack_rows([w[n] for n in CONVS], CONV_ROWS, F32), name="gather_convs")
    conv_parts = [_unpack(convs[2 * j].reshape(-1), conv_shapes) for j in range(N_CHIPS)]
    a_conv_full, f_conv_full = (jnp.concatenate([conv_parts[j][q] for j in range(N_CHIPS)], axis=2) for q in range(2))
    ready = {("a_w_in", 0): jnp.pad(a_in, ((0, 0), (0, n_main + LANES - a_in.shape[1]))), ("a_conv", 0): a_conv_full[0], ("f_conv", 0): f_conv_full[0], ("f_conv", 1): f_conv_full[1]}
    later = [[k for k in units if k != ("a_w_in", 0)] for units in LAYER_UNITS]
    pending, after = [], ga
    for layer, keys in enumerate(later):
        pending.append(gather_start([shard[k] for k in keys], after, name=f"gather_start{layer}"))
        after = pending[-1][4]
    sm = {n: w[n] for n in SMALL}
    sm["norm_mix"] = sm["norm_mix"] + after[:1, :1]

    def weight(name, layer, act):
        if (name, layer) not in ready:
            landed = gather_wait(pending[layer], act, name=f"gather_wait{layer}")
            for k, g in zip(later[layer], landed):
                g = lax.dynamic_update_index_in_dim(g, shard[k], my_chip, 0)
                ready[k] = g if k[0] in BY_CHIP else g.reshape(N_CHIPS * g.shape[1], g.shape[2])
        return ready[name, layer]

    pairs, scattered, started = {}, {}, []

    def on_grads(layer, part, gw):
        keys = [k for k in LAYER_UNITS[layer] if (k[0] in LAYERED) == (part == "ffn")]
        from_sib = swap_units([gw[n] for n, _ in keys], name=f"rs_swap_{part}{layer}")
        for (n, _), sib in zip(keys, from_sib):
            pairs[n, layer] = pair_add(gw[n], sib, name=f"rs_add_pair_{n}{layer}")
        started.append((keys, scatter_start([pairs[k] for k in keys], name=f"rs_scatter_start_{part}{layer}"), f"{part}{layer}"))
        return started[-1][1][4]

    loss, grad_x, gs = local_step(x[0], p[:, 0], loss_target[0], sm, weight, on_grads)

    grads, delta, new_m, new_v, g_unit = {}, {}, {}, {}, {}

    def finish(keys, tag):
        halves = [chips_add(pairs[k], scattered[k], name=f"rs_add_chips_{k[0]}{k[1]}") for k in keys]
        g_unit.update(zip(keys, zip(halves, join_units(halves, name=f"rs_join_{tag}"))))
        for n in BIG:
            mine = [(n, i) for i in range(DEPTH) if (n, i) in LAYER_UNITS[i]]
            if n not in delta and all(k in g_unit for k in mine):
                g_layers = [g_unit[k] for k in mine]
                shape3 = (len(g_layers), 2 * g_layers[0][0].shape[0], g_layers[0][0].shape[1])
                res = adamw_layers(w[n].reshape(shape3), g_layers, m[n].reshape(shape3), v[n].reshape(shape3), name=f"adamw_{n}")
                delta[n], new_m[n], new_v[n], grads[n] = (r.reshape(w[n].shape) for r in res)

    last_keys, last_pending, last_tag = started[-1]
    for keys, pend, tag in started[:-1]:
        scattered.update(zip(keys, scatter_wait(pend, last_pending[4], name=f"rs_scatter_wait_{tag}")))
    finish([k for keys, _, _ in started[:-1] for k in keys], "first")

    conv_grads = _pack_rows([gs[n] for n in CONVS], CONV_GRAD_ROWS, F32)
    small_sum = sum8(allgather8(jnp.concatenate([_pack_small(gs, loss), conv_grads]), name="gather_small"), name="sum_small")
    g_sm, loss_sum = _unpack_small(small_sum[:SMALL_ROWS], sm)

    scattered.update(zip(last_keys, scatter_wait(last_pending, small_sum, name=f"rs_scatter_wait_{last_tag}")))
    finish(last_keys, "last")

    for n, full in zip(CONVS, _unpack(small_sum[SMALL_ROWS:].reshape(-1), [gs[n].shape for n in CONVS])):
        g2 = _as2d(lax.dynamic_slice_in_dim(full, my_chip * w[n].shape[-1], w[n].shape[-1], axis=full.ndim - 1))
        d2, m2, v2 = adamw(_as2d(w[n]), g2, _as2d(m[n]), _as2d(v[n]), name=f"adamw_{n}")
        grads[n], delta[n], new_m[n], new_v[n] = (r.reshape(w[n].shape) for r in (g2, d2, m2, v2))
    pk = lambda d: _pack_small(d)
    d2, m2, v2 = adamw(pk(sm), pk(g_sm), pk({n: m[n] for n in SMALL}), pk({n: v[n] for n in SMALL}), name="adamw_small")
    for src, dst in ((d2, delta), (m2, new_m), (v2, new_v)):
        dst.update(_unpack_small(src, sm)[0])
    grads.update(g_sm)

    order = ["norm_mix", "norm_ffn", "norm_ple", "norm_final", "a_w_in", "a_conv", "a_log", "a_dt_bias", "a_norm", "a_w_out",
             "b_w_in", "b_sinks", "b_w_out", "f_w_up", "f_conv", "f_w_down", "ple_w_proj", "ple_w_gate"]
    return (loss_sum, grad_x[None], *[grads[n] for n in order], *[delta[n] for n in order],
            *[new_m[n] for n in order], *[new_v[n] for n in order])
```

```python
import functools
import math

import jax
import jax.numpy as jnp
from jax import lax
from jax.experimental import pallas as pl
from jax.experimental.pallas import tpu as pltpu

F32 = jnp.float32
BF16 = jnp.bfloat16
MESH = pl.DeviceIdType.MESH

D_MODEL = 1024
N_HEADS_A = 8
HEAD_DIM_A = 128
CONV_A = 4
N_HEADS_B = 16
N_KV_B = 4
HEAD_DIM_B = 64
WINDOW = 128
D_FF = 2816
FFN_CONV = 3
PLE_DIM = 256
EPS = 1e-6
DEPTH = 2

ADAM_LR = 0.001
ADAM_B1 = 0.9
ADAM_B2 = 0.999
ADAM_EPS = 1e-08
ADAM_WD = 0.01
ADAM_STEP = 10

LANES = 128
SUBLANES = 8
BF16_ROWS = 16
CHUNK = 128
VMEM_LIMIT = 56 * 1024 * 1024
NEG = -1e30
N_CHIPS = 4
N_DEV = 8
PACK_COLS = 1024


def _params(sem=None):
    return pltpu.CompilerParams(dimension_semantics=sem, vmem_limit_bytes=VMEM_LIMIT)


def _tile(dim, cap):
    if dim % LANES:
        return dim
    best = LANES
    for t in range(LANES, min(dim, cap) + 1, LANES):
        if dim % t == 0:
            best = t
    return best


def _dot(a, b, dims=(((1,), (0,)), ((), ())), precision=None):
    return lax.dot_general(a, b, dims, precision=precision, preferred_element_type=F32)


NN = (((1,), (0,)), ((), ()))
NT = (((1,), (1,)), ((), ()))
TN = (((0,), (0,)), ((), ()))


ROW_TM = 512
MM_TM_CAP = 1024
MM_TK_CAP_TOKENS = 2048


def mm(a, b, *, name, ta=False, tb=False, out_dtype=F32, add=None, norm_w=None, norm_grad=None, tm_cap=MM_TM_CAP, tn_cap=1408,
       tk_cap=1408, n=None, tn=None, tk=None, b_spec=None, o_spec=None, o_shape=None, into=None):
    m, k = (a.shape[1], a.shape[0]) if ta else a.shape
    if b_spec is None:
        n = b.shape[0] if tb else b.shape[1]
        assert (b.shape[1] if tb else b.shape[0]) == k, (a.shape, b.shape, ta, tb)
    tm, tn, tk = _tile(m, tm_cap), tn or _tile(n, tn_cap), tk or _tile(k, MM_TK_CAP_TOKENS if ta else tk_cap)
    assert n % tn == 0 and k % tk == 0, (n, tn, k, tk)
    nk = k // tk
    dims = (((0 if ta else 1,), (1 if tb else 0,)), ((), ()))
    has_add, has_norm, has_grad = add is not None, norm_w is not None, norm_grad is not None
    assert not (has_norm or has_grad) or (tn == n and o_spec is None), "the norm epilogues need whole rows"
    n_in = 2 + has_add + has_norm + 3 * has_grad + (into is not None)

    def body(*refs):
        a_ref, b_ref = refs[0], refs[1]
        add_ref = refs[2] if has_add else None
        o_ref = refs[n_in]
        part = _dot(a_ref[...].astype(BF16), b_ref[...].astype(BF16), dims)
        first = pl.program_id(0) == 0

        def finish(r):
            if has_add:
                r = r + add_ref[...].astype(F32)
            if has_grad:
                h_ref, w_ref, prev_ref = refs[2 + has_add:5 + has_add]
                _, vjp = jax.vjp(_f_norm, h_ref[...], w_ref[...])
                r, dw = vjp(r)
                r = r + prev_ref[...]

                @pl.when(first)
                def _():
                    refs[n_in + 1][...] = dw

                @pl.when(jnp.logical_not(first))
                def _():
                    refs[n_in + 1][...] += dw
            o_ref[...] = r.astype(o_ref.dtype)
            if has_norm:
                refs[n_in + 1][...] = _f_norm(r, refs[2 + has_add][...]).astype(BF16)

        if nk == 1:
            finish(part)
            return
        acc = refs[-1]
        kk = pl.program_id(2)

        @pl.when(kk == 0)
        def _():
            acc[...] = part

        @pl.when(kk > 0)
        def _():
            acc[...] += part

        @pl.when(kk == nk - 1)
        def _():
            finish(acc[...])

    a_spec = pl.BlockSpec((tk, tm), lambda i, j, kk: (kk, i)) if ta else pl.BlockSpec((tm, tk), lambda i, j, kk: (i, kk))
    if b_spec is None:
        b_spec = pl.BlockSpec((tn, tk), lambda i, j, kk: (j, kk)) if tb else pl.BlockSpec((tk, tn), lambda i, j, kk: (kk, j))
    plain_o = pl.BlockSpec((tm, tn), lambda i, j, kk: (i, j))
    if o_spec is None:
        o_spec, o_shape = plain_o, (m, n)
    in_specs = [a_spec, b_spec] + ([plain_o] if has_add else [])
    args = (a, b) + ((add,) if has_add else ())
    out_specs, out_shapes = o_spec, jax.ShapeDtypeStruct(tuple(o_shape), out_dtype)
    one_row = pl.BlockSpec((1, n), lambda i, j, kk: (0, 0))
    if has_norm:
        in_specs.append(one_row)
        args += (norm_w,)
        out_specs, out_shapes = [o_spec, plain_o], [out_shapes, jax.ShapeDtypeStruct((m, n), BF16)]
    if has_grad:
        assert not has_norm
        in_specs += [plain_o, one_row, plain_o]
        args += tuple(norm_grad)
        out_specs, out_shapes = [o_spec, one_row], [out_shapes, jax.ShapeDtypeStruct((1, n), F32)]
    aliases = {}
    if into is not None:
        assert into.shape == tuple(o_shape) and into.dtype == out_dtype, (into.shape, o_shape)
        in_specs.append(pl.BlockSpec(memory_space=pl.ANY))
        args += (into,)
        aliases = {n_in - 1: 0}
    return pl.pallas_call(
        body, grid=(m // tm, n // tn, nk), in_specs=in_specs, out_specs=out_specs,
        out_shape=out_shapes, name=name, input_output_aliases=aliases,
        scratch_shapes=[pltpu.VMEM((tm, tn), F32)] if nk > 1 else [],
        compiler_params=_params(("arbitrary" if has_grad else "parallel", "parallel", "arbitrary")),
    )(*args)


def _row_spec(tm, cw, coff):
    return pl.BlockSpec((tm, cw), lambda i, j: (i, j + coff))


def _full_spec(shape):
    return pl.BlockSpec(shape, lambda i, j: (0,) * len(shape))


def tile_map(fn, rows, params, outs, *, tm, ncol, name):
    t = rows[0][0].shape[0]
    nin = len(rows) + len(params)

    def body(*refs):
        res = fn(*[r[...] for r in refs[:nin]])
        res = res if isinstance(res, (tuple, list)) else (res,)
        for o_ref, r in zip(refs[nin:], res):
            o_ref[...] = r.astype(o_ref.dtype)

    in_specs = [_row_spec(tm, cw, coff) for (_, cw, coff) in rows] + [_full_spec(p.shape) for p in params]
    res = pl.pallas_call(
        body, grid=(t // tm, ncol), in_specs=in_specs,
        out_specs=[_row_spec(tm, cw, 0) for (cw, _) in outs],
        out_shape=[jax.ShapeDtypeStruct((t, cw * ncol), dt) for (cw, dt) in outs], name=name,
        compiler_params=_params(("parallel", "parallel")),
    )(*[r[0] for r in rows], *params)
    return res


def tile_vjp(fn, rows, params, cts, *, n_diff, tm, ncol, name, grad_dtypes=None):
    t = rows[0][0].shape[0]
    nr, npar, nct = len(rows), len(params), len(cts)

    def body(*refs):
        vals = [r[...] for r in refs[:nr + npar + nct]]
        diff, rest, pars = vals[:n_diff], vals[n_diff:nr], vals[nr:nr + npar]
        ctv = vals[nr + npar:nr + npar + nct]
        outs_ref = refs[nr + npar + nct:]

        def f(*a):
            res = fn(*a[:n_diff], *rest, *a[n_diff:])
            return tuple(res) if isinstance(res, (tuple, list)) else (res,)

        primal, vjp = jax.vjp(f, *[d.astype(F32) for d in diff], *pars)
        grads = vjp(tuple(c.astype(o.dtype) for c, o in zip(ctv, primal)))
        for q in range(n_diff):
            outs_ref[q][...] = grads[q].astype(outs_ref[q].dtype)
        first = (pl.program_id(0) == 0) & (pl.program_id(1) == 0)
        for q in range(npar):
            o_ref, g = outs_ref[n_diff + q], grads[n_diff + q]

            @pl.when(first)
            def _(o_ref=o_ref, g=g):
                o_ref[...] = g

            @pl.when(jnp.logical_not(first))
            def _(o_ref=o_ref, g=g):
                o_ref[...] += g

    in_specs = [_row_spec(tm, cw, coff) for (_, cw, coff) in rows] + [_full_spec(p.shape) for p in params]
    in_specs += [_row_spec(tm, cw, coff) for (_, cw, coff) in cts]
    args = [r[0] for r in rows] + list(params) + [c[0] for c in cts]
    out_specs = [_row_spec(tm, rows[q][1], 0) for q in range(n_diff)] + [_full_spec(p.shape) for p in params]
    grad_dtypes = grad_dtypes or [F32] * n_diff
    out_shape = [jax.ShapeDtypeStruct((t, rows[q][1] * ncol), grad_dtypes[q]) for q in range(n_diff)]
    out_shape += [jax.ShapeDtypeStruct(p.shape, F32) for p in params]
    res = pl.pallas_call(
        body, grid=(t // tm, ncol), in_specs=in_specs, out_specs=out_specs, out_shape=out_shape, name=name,
        compiler_params=_params(("arbitrary", "arbitrary")),
    )(*args)
    return res[:n_diff], res[n_diff:]


def _silu(x):
    return x * jax.nn.sigmoid(x)


def _f_norm(h, w):
    return h * lax.rsqrt(jnp.mean(h * h, axis=-1, keepdims=True) + EPS) * w


def _f_gnorm(o, z, w):
    return _f_norm(o, w) * _silu(z)


def _f_act(gate, val):
    return _silu(gate) * val


def _f_ple(gl, pe):
    return jax.nn.sigmoid(gl) * pe


def _f_betag(pt, alog, dtb):
    lane = lax.broadcasted_iota(jnp.int32, (1, LANES), 1)
    z = pt + dtb
    softplus = jnp.maximum(z, 0.0) + jnp.log(1.0 + jnp.exp(-jnp.abs(z)))
    g = -jnp.exp(alog) * softplus
    return jnp.where(lane < N_HEADS_A, jax.nn.sigmoid(pt), jnp.where(lane < 2 * N_HEADS_A, g, 0.0))


CONV_TM = 1024
CONV_CW = 1024


def _shift_down(x, prev, s, row):
    rp = jnp.tile(pltpu.roll(prev, s, 0), (x.shape[0] // SUBLANES, 1))
    return jnp.where(row < s, rp, pltpu.roll(x, s, 0))


def _shift_up(x, nxt, s, row):
    tm = x.shape[0]
    rn = jnp.tile(pltpu.roll(nxt, SUBLANES - s, 0), (tm // SUBLANES, 1))
    return jnp.where(row >= tm - s, rn, pltpu.roll(x, tm - s, 0))


def _conv_taps(x, prev, w_ref, cols, row):
    k = w_ref.shape[0]
    y = x * w_ref[pl.ds(k - 1, 1), cols]
    for s in range(1, k):
        y = y + _shift_down(x, prev, s, row) * w_ref[pl.ds(k - 1 - s, 1), cols]
    return y


def _lane_chunks(cw):
    return [slice(cb * LANES, (cb + 1) * LANES) for cb in range(cw // LANES)]


def conv_fwd(x, w, *, name):
    t = x.shape[0]
    k, c = w.shape
    tm, cw = min(CONV_TM, t), CONV_CW
    nb8 = tm // SUBLANES

    def body(x_ref, p_ref, w_ref, o_ref):
        first = pl.program_id(1) == 0
        row = lax.broadcasted_iota(jnp.int32, (tm, LANES), 0)
        for cols in _lane_chunks(cw):
            o_ref[:, cols] = _conv_taps(x_ref[:, cols], jnp.where(first, 0.0, p_ref[:, cols]), w_ref, cols, row)

    return pl.pallas_call(
        body, grid=(c // cw, t // tm),
        in_specs=[pl.BlockSpec((tm, cw), lambda j, i: (i, j)),
                  pl.BlockSpec((SUBLANES, cw), lambda j, i: (jnp.maximum(i * nb8 - 1, 0), j)),
                  pl.BlockSpec((k, cw), lambda j, i: (0, j))],
        out_specs=pl.BlockSpec((tm, cw), lambda j, i: (i, j)),
        out_shape=jax.ShapeDtypeStruct((t, c), F32), name=name,
        compiler_params=_params(("parallel", "parallel")),
    )(x, x, w)


def conv_bwd(dy, x, w, *, name):
    t = x.shape[0]
    k, c = w.shape
    tm, cw = min(CONV_TM, t), CONV_CW
    nb8 = tm // SUBLANES
    ni = t // tm

    def body(dy_ref, dn_ref, x_ref, p_ref, w_ref, dx_ref, dw_ref):
        i = pl.program_id(1)
        first, last = i == 0, i == ni - 1
        row = lax.broadcasted_iota(jnp.int32, (tm, LANES), 0)
        for cols in _lane_chunks(cw):
            dyv, xv = dy_ref[:, cols], x_ref[:, cols]
            nxt = jnp.where(last, 0.0, dn_ref[:, cols])
            prev = jnp.where(first, 0.0, p_ref[:, cols])
            dx = dyv * w_ref[pl.ds(k - 1, 1), cols]
            dws = [jnp.sum(dyv * xv, axis=0, keepdims=True)]
            for s in range(1, k):
                dx = dx + _shift_up(dyv, nxt, s, row) * w_ref[pl.ds(k - 1 - s, 1), cols]
                dws.append(jnp.sum(dyv * _shift_down(xv, prev, s, row), axis=0, keepdims=True))
            dx_ref[:, cols] = dx.astype(dx_ref.dtype)
            for s in range(k):
                @pl.when(first)
                def _(s=s, dws=dws, cols=cols):
                    dw_ref[pl.ds(k - 1 - s, 1), cols] = dws[s]

                @pl.when(jnp.logical_not(first))
                def _(s=s, dws=dws, cols=cols):
                    dw_ref[pl.ds(k - 1 - s, 1), cols] += dws[s]

    return pl.pallas_call(
        body, grid=(c // cw, ni),
        in_specs=[pl.BlockSpec((tm, cw), lambda j, i: (i, j)),
                  pl.BlockSpec((SUBLANES, cw), lambda j, i: (jnp.minimum((i + 1) * nb8, t // SUBLANES - 1), j)),
                  pl.BlockSpec((tm, cw), lambda j, i: (i, j)),
                  pl.BlockSpec((SUBLANES, cw), lambda j, i: (jnp.maximum(i * nb8 - 1, 0), j)),
                  pl.BlockSpec((k, cw), lambda j, i: (0, j))],
        out_specs=[pl.BlockSpec((tm, cw), lambda j, i: (i, j)), pl.BlockSpec((k, cw), lambda j, i: (0, j))],
        out_shape=[jax.ShapeDtypeStruct((t, c), BF16), jax.ShapeDtypeStruct((k, c), F32)], name=name,
        compiler_params=_params(("parallel", "arbitrary")),
    )(dy, dy, x, x, w)


FFN_TM = 512
FFN_CW = D_FF // 2


def _ffn_specs(t, tm, cw, k):
    ncol = D_FF // cw
    cur = lambda off: pl.BlockSpec((tm, cw), lambda j, i: (i, j + off))
    prev = lambda off, hr: pl.BlockSpec((hr, cw), lambda j, i: (jnp.maximum(i * (tm // hr) - 1, 0), j + off))
    nxt = lambda off, hr: pl.BlockSpec((hr, cw), lambda j, i: (jnp.minimum((i + 1) * (tm // hr), t // hr - 1), j + off))
    taps = lambda off: pl.BlockSpec((k, cw), lambda j, i: (0, j + off))
    return cur, prev, nxt, taps, ncol


def _rows_before(ref, cols, first):
    return jnp.where(first, 0.0, ref[ref.shape[0] - SUBLANES:, cols].astype(F32))


def conv_act_fwd(u, w, *, name):
    t, k = u.shape[0], w.shape[0]
    tm, cw = min(FFN_TM, t), FFN_CW
    cur, prev, _, taps, ncol = _ffn_specs(t, tm, cw, k)

    def body(ug_ref, pg_ref, uv_ref, pv_ref, wg_ref, wv_ref, o_ref):
        first = pl.program_id(1) == 0
        row = lax.broadcasted_iota(jnp.int32, (tm, LANES), 0)
        for cb in range(cw // LANES):
            cols = slice(cb * LANES, (cb + 1) * LANES)
            cg = _conv_taps(ug_ref[:, cols].astype(F32), _rows_before(pg_ref, cols, first), wg_ref, cols, row)
            cv = _conv_taps(uv_ref[:, cols].astype(F32), _rows_before(pv_ref, cols, first), wv_ref, cols, row)
            o_ref[:, cols] = _f_act(cg, cv).astype(o_ref.dtype)

    return pl.pallas_call(
        body, grid=(ncol, t // tm),
        in_specs=[cur(0), prev(0, BF16_ROWS), cur(ncol), prev(ncol, BF16_ROWS), taps(0), taps(ncol)],
        out_specs=cur(0), out_shape=jax.ShapeDtypeStruct((t, D_FF), BF16), name=name,
        compiler_params=_params(("parallel", "parallel")),
    )(u, u, u, u, w, w)


def conv_act_bwd(u, dact, w, *, name):
    t, k = u.shape[0], w.shape[0]
    tm, cw = min(FFN_TM, t), FFN_CW
    cur, prev, nxt, taps, ncol = _ffn_specs(t, tm, cw, k)
    ni = t // tm

    def body(ug_ref, pg_ref, ng_ref, uv_ref, pv_ref, nv_ref, d_ref, dn_ref, wg_ref, wv_ref, dg_ref, dv_ref, dwg_ref, dwv_ref):
        i = pl.program_id(1)
        first, last = i == 0, i == ni - 1
        row = lax.broadcasted_iota(jnp.int32, (tm, LANES), 0)
        row8 = lax.broadcasted_iota(jnp.int32, (SUBLANES, LANES), 0)
        for cb in range(cw // LANES):
            cols = slice(cb * LANES, (cb + 1) * LANES)
            ug, uv = ug_ref[:, cols].astype(F32), uv_ref[:, cols].astype(F32)
            pg, pv = _rows_before(pg_ref, cols, first), _rows_before(pv_ref, cols, first)
            sg = [ug] + [_shift_down(ug, pg, s, row) for s in range(1, k)]
            sv = [uv] + [_shift_down(uv, pv, s, row) for s in range(1, k)]
            taps = lambda xs, w_ref: sum(xs[s] * w_ref[pl.ds(k - 1 - s, 1), cols] for s in range(k))
            _, vjp = jax.vjp(_f_act, taps(sg, wg_ref), taps(sv, wv_ref))
            dcg, dcv = vjp(d_ref[:, cols])
            after = lambda ref: ref[:SUBLANES, cols].astype(F32)
            _, vjp_n = jax.vjp(_f_act, _conv_taps(after(ng_ref), ug[tm - SUBLANES:], wg_ref, cols, row8),
                               _conv_taps(after(nv_ref), uv[tm - SUBLANES:], wv_ref, cols, row8))
            dcgn, dcvn = vjp_n(jnp.where(last, 0.0, dn_ref[:, cols]))
            for dc, dcn, xs, w_ref, dx_ref, dw_ref in ((dcg, dcgn, sg, wg_ref, dg_ref, dwg_ref),
                                                       (dcv, dcvn, sv, wv_ref, dv_ref, dwv_ref)):
                dx = dc * w_ref[pl.ds(k - 1, 1), cols]
                dws = [jnp.sum(dc * xs[0], axis=0, keepdims=True)]
                for s in range(1, k):
                    dx = dx + _shift_up(dc, dcn, s, row) * w_ref[pl.ds(k - 1 - s, 1), cols]
                    dws.append(jnp.sum(dc * xs[s], axis=0, keepdims=True))
                dx_ref[:, cols] = dx.astype(dx_ref.dtype)
                for s in range(k):
                    @pl.when(first)
                    def _(s=s, dw_ref=dw_ref, dws=dws):
                        dw_ref[pl.ds(k - 1 - s, 1), cols] = dws[s]

                    @pl.when(jnp.logical_not(first))
                    def _(s=s, dw_ref=dw_ref, dws=dws):
                        dw_ref[pl.ds(k - 1 - s, 1), cols] += dws[s]

    half = jax.ShapeDtypeStruct((t, D_FF), BF16)
    dwh = jax.ShapeDtypeStruct((k, D_FF), F32)
    return pl.pallas_call(
        body, grid=(ncol, ni),
        in_specs=[cur(0), prev(0, BF16_ROWS), nxt(0, BF16_ROWS), cur(ncol), prev(ncol, BF16_ROWS), nxt(ncol, BF16_ROWS),
                  cur(0), nxt(0, SUBLANES), taps(0), taps(ncol)],
        out_specs=[cur(0), cur(0), taps(0), taps(0)], out_shape=[half, half, dwh, dwh], name=name,
        compiler_params=_params(("parallel", "arbitrary")),
    )(u, u, u, u, u, u, dact, dact, w, w)


def _each(f, *lists):
    return [f(*a) for a in zip(*lists)]


@jax.custom_vjp
def _inv_unit_lower(lms):
    return _inv_blocks(lms)


def _inv_blocks(lms):
    c = lms[0].shape[0]
    ri = lax.broadcasted_iota(jnp.int32, (c, c), 0)
    ci = lax.broadcasted_iota(jnp.int32, (c, c), 1)
    eye = (ri == ci).astype(F32)
    dms = _each(lambda lm: eye - jnp.where((ri >> 1) == (ci >> 1), lm, 0.0), lms)
    for lv in range(1, int(math.log2(c))):
        below = ((ri >> (lv + 1)) == (ci >> (lv + 1))) & ((ri >> lv) != (ci >> lv))
        dbs = _each(lambda dm: dm.astype(BF16), dms)
        ods = _each(lambda lm, db: _dot(jnp.where(below, lm, 0.0).astype(BF16), db).astype(BF16), lms, dbs)
        dms = _each(lambda dm, db, od: dm - _dot(db, od), dms, dbs, ods)
    return dms


def _inv_fwd(lms):
    tms = _inv_blocks(lms)
    return tms, tms


def _inv_bwd(tms, dts):
    tbs = _each(lambda tm: tm.astype(BF16), tms)
    mid = _each(lambda tb, dt: _dot(tb, dt.astype(BF16), TN).astype(BF16), tbs, dts)
    return (_each(lambda m, tb: -_dot(m, tb, NT), mid, tbs),)


_inv_unit_lower.defvjp(_inv_fwd, _inv_bwd)


@jax.custom_vjp
def _inv_known(lms, tms):
    return tms


_inv_known.defvjp(lambda lms, tms: (tms, tms), lambda tms, dts: _inv_bwd(tms, dts) + (_each(jnp.zeros_like, tms),))


def _l2n(x):
    return x * lax.rsqrt(jnp.sum(x * x, axis=-1, keepdims=True) + EPS)


def _prep_fn(cqs, cks, cvs, bg, sel_b, sel_g, tms=None):
    c = cqs[0].shape[0]
    ri = lax.broadcasted_iota(jnp.int32, (c, c), 0)
    ci = lax.broadcasted_iota(jnp.int32, (c, c), 1)
    eye = (ri == ci).astype(F32)
    incl, strict = ci <= ri, ci < ri
    last = lax.broadcasted_iota(jnp.int32, (c, 1), 0) == c - 1
    to_row = lambda col: jnp.sum(col * eye, axis=0, keepdims=True)
    qs = _each(lambda a: _l2n(_silu(a)) * (HEAD_DIM_A ** -0.5), cqs)
    ks = _each(lambda a: _l2n(_silu(a)), cks)
    vbs = _each(lambda a: _silu(a).astype(BF16), cvs)
    betas = _each(lambda m: jnp.sum(bg * m, axis=1, keepdims=True), sel_b)
    gs = _each(lambda m: jnp.sum(bg * m, axis=1, keepdims=True), sel_g)
    gcss = _each(lambda g: jnp.sum(jnp.where(incl, to_row(g), 0.0), axis=1, keepdims=True), gs)
    gtots = _each(lambda gcs: jnp.sum(jnp.where(last, gcs, 0.0), axis=0, keepdims=True), gcss)
    decays = _each(lambda gcs: jnp.exp(jnp.where(incl, gcs - to_row(gcs), NEG)), gcss)
    kbs = _each(lambda k: k.astype(BF16), ks)
    lms = _each(lambda beta, kb, dec: jnp.where(strict, beta * _dot(kb, kb, NT) * dec, 0.0), betas, kbs, decays)
    tms = _inv_unit_lower(lms) if tms is None else _inv_known(lms, tms)
    ams = _each(lambda tm, beta: (tm * to_row(beta)).astype(BF16), tms, betas)
    gams = _each(jnp.exp, gcss)
    u0s = _each(_dot, ams, vbs)
    wks = _each(lambda am, gam, k: _dot(am, (gam * k).astype(BF16)), ams, gams, ks)
    qks = _each(lambda q, kb, dec: _dot(q.astype(BF16), kb, NT) * dec, qs, kbs, decays)
    qds = _each(lambda q, gam: q * gam, qs, gams)
    kds = _each(lambda k, gtot, gcs: k * jnp.exp(gtot - gcs), ks, gtots, gcss)
    gls = _each(lambda gtot: jnp.exp(gtot) * jnp.ones((SUBLANES, LANES), F32), gtots)
    return u0s, wks, qds, kds, qks, gls, tms


def _head_masks(h):
    lane = lax.broadcasted_iota(jnp.int32, (1, LANES), 1)
    return (lane == h).astype(F32), (lane == h + N_HEADS_A).astype(F32)


def _hsl(j):
    return slice(j * HEAD_DIM_A, (j + 1) * HEAD_DIM_A)


def gnorm_bwd(o, zsrc, w, don, *, name):
    t, width = o.shape
    tm = min(ROW_TM, t)
    zoff = zsrc.shape[1] // width - 1

    def body(o_ref, z_ref, w_ref, d_ref, do_ref, dz_ref, dw_ref):
        dw = jnp.zeros(w.shape, F32)
        for h in range(N_HEADS_A):
            _, vjp = jax.vjp(_f_gnorm, o_ref[:, _hsl(h)], z_ref[:, _hsl(h)], w_ref[...])
            do, dz, dwh = vjp(d_ref[:, _hsl(h)])
            do_ref[:, _hsl(h)] = do.astype(do_ref.dtype)
            dz_ref[:, _hsl(h)] = dz.astype(dz_ref.dtype)
            dw = dw + dwh
        first = pl.program_id(0) == 0

        @pl.when(first)
        def _():
            dw_ref[...] = dw

        @pl.when(jnp.logical_not(first))
        def _():
            dw_ref[...] += dw

    rows = pl.BlockSpec((tm, width), lambda i: (i, 0))
    wspec = pl.BlockSpec(w.shape, lambda i: (0, 0))
    return pl.pallas_call(
        body, grid=(t // tm,),
        in_specs=[rows, pl.BlockSpec((tm, width), lambda i: (i, zoff)), wspec, rows],
        out_specs=[rows, rows, wspec],
        out_shape=[jax.ShapeDtypeStruct((t, width), BF16)] * 2 + [jax.ShapeDtypeStruct(w.shape, F32)], name=name,
        compiler_params=_params(("arbitrary",)),
    )(o, zsrc, w, don)


def delta_prep(cqkv, bg, *, name):
    t = cqkv.shape[0]
    nh, hd, n = N_HEADS_A, HEAD_DIM_A, t // CHUNK

    def body(cq_ref, ck_ref, cv_ref, bg_ref, u0_ref, wk_ref, qd_ref, kd_ref, qk_ref, tm_ref, gl_ref):
        heads = range(nh)
        masks = [_head_masks(j) for j in heads]
        res = _prep_fn([cq_ref[:, _hsl(j)] for j in heads], [ck_ref[:, _hsl(j)] for j in heads],
                       [cv_ref[:, _hsl(j)] for j in heads], bg_ref[...], [m[0] for m in masks], [m[1] for m in masks])
        for o_ref, rs in zip((u0_ref, wk_ref, qd_ref, kd_ref, qk_ref, tm_ref), res[:5] + (res[6],)):
            for j in heads:
                o_ref[:, _hsl(j)] = rs[j].astype(o_ref.dtype)
        for j in heads:
            gl_ref[j * SUBLANES:(j + 1) * SUBLANES, :] = res[5][j]

    blk = lambda off: pl.BlockSpec((CHUNK, nh * hd), lambda i: (i, off))
    res = pl.pallas_call(
        body, grid=(n,),
        in_specs=[blk(0), blk(1), blk(2), pl.BlockSpec((CHUNK, LANES), lambda i: (i, 0))],
        out_specs=[blk(0)] * 6 + [pl.BlockSpec((nh * SUBLANES, LANES), lambda i: (i, 0))],
        out_shape=[jax.ShapeDtypeStruct((t, nh * hd), dt) for dt in (F32, BF16, BF16, BF16, BF16, F32)]
        + [jax.ShapeDtypeStruct((n * nh * SUBLANES, LANES), F32)],
        name=name, compiler_params=_params(("parallel",)),
    )(cqkv, cqkv, cqkv, bg)
    return [*res[:5], res[6]], res[5]


def delta_prep_bwd(cqkv, bg, tms, cts, *, name):
    t = cqkv.shape[0]
    nh, hd, n = N_HEADS_A, HEAD_DIM_A, t // CHUNK

    def body(cq_ref, ck_ref, cv_ref, bg_ref, tm_ref, c0, c1, c2, c3, c4, c5, dc_ref, dbg_ref):
        heads = range(nh)
        masks = [_head_masks(j) for j in heads]
        known = [tm_ref[:, _hsl(j)] for j in heads]
        _, vjp = jax.vjp(lambda a, b, c, d: _prep_fn(a, b, c, d, [m[0] for m in masks], [m[1] for m in masks], known)[:6],
                         [cq_ref[:, _hsl(j)] for j in heads], [ck_ref[:, _hsl(j)] for j in heads],
                         [cv_ref[:, _hsl(j)] for j in heads], bg_ref[...])
        cts = tuple([c[:, _hsl(j)] for j in heads] for c in (c0, c1, c2, c3, c4))
        dqs, dks, dvs, dbg = vjp(cts + ([c5[j * SUBLANES:(j + 1) * SUBLANES, :] for j in heads],))
        for part, ds in enumerate((dqs, dks, dvs)):
            for j in heads:
                dc_ref[:, _hsl(part * nh + j)] = ds[j]
        dbg_ref[...] = dbg

    blk = lambda off: pl.BlockSpec((CHUNK, nh * hd), lambda i: (i, off))
    gl_spec = pl.BlockSpec((nh * SUBLANES, LANES), lambda i: (i, 0))
    bg_spec = pl.BlockSpec((CHUNK, LANES), lambda i: (i, 0))
    return pl.pallas_call(
        body, grid=(n,),
        in_specs=[blk(0), blk(1), blk(2), bg_spec] + [blk(0)] * 6 + [gl_spec],
        out_specs=[pl.BlockSpec((CHUNK, 3 * nh * hd), lambda i: (i, 0)), bg_spec],
        out_shape=[jax.ShapeDtypeStruct((t, 3 * nh * hd), F32), jax.ShapeDtypeStruct((t, LANES), F32)],
        name=name, compiler_params=_params(("parallel",)),
    )(cqkv, cqkv, cqkv, bg, tms, *cts)


def delta_scan(u0, wk, qd, kd, qk, gl, zsrc, norm_w, *, name):
    t = u0.shape[0]
    nh, hd, n = N_HEADS_A, HEAD_DIM_A, t // CHUNK
    zoff = zsrc.shape[1] // (nh * hd) - 1

    def body(u0_ref, wk_ref, qd_ref, kd_ref, qk_ref, gl_ref, z_ref, w_ref, o_ref, sin_ref, on_ref, s_ref):
        @pl.when(pl.program_id(0) == 0)
        def _():
            s_ref[...] = jnp.zeros_like(s_ref)

        heads = list(range(nh))
        cols = lambda ref: [ref[:, _hsl(h)].astype(BF16) for h in heads]
        ss = [s_ref[h] for h in heads]
        for h in heads:
            sin_ref[h] = ss[h]
        sbs = _each(lambda s: s.astype(BF16), ss)
        ubs = _each(lambda h, wkb, sb: (u0_ref[:, _hsl(h)] - _dot(wkb, sb)).astype(BF16), heads, cols(wk_ref), sbs)
        os_ = _each(lambda qdb, sb, qkb, ub: _dot(qdb, sb) + _dot(qkb, ub), cols(qd_ref), sbs, cols(qk_ref), ubs)
        sn = _each(lambda h, s, kdb, ub: gl_ref[pl.ds(h * SUBLANES, 1), :] * s + _dot(kdb, ub, TN), heads, ss, cols(kd_ref), ubs)
        ons = _each(lambda h, o: _f_gnorm(o, z_ref[:, _hsl(h)], w_ref[...]), heads, os_)
        for h in heads:
            o_ref[:, _hsl(h)] = os_[h]
            on_ref[:, _hsl(h)] = ons[h].astype(on_ref.dtype)
            s_ref[h] = sn[h]

    blk = pl.BlockSpec((CHUNK, nh * hd), lambda i: (i, 0))
    return pl.pallas_call(
        body, grid=(n,),
        in_specs=[blk] * 5 + [pl.BlockSpec((nh * SUBLANES, LANES), lambda i: (i, 0)),
                              pl.BlockSpec((CHUNK, nh * hd), lambda i: (i, zoff)), pl.BlockSpec(norm_w.shape, lambda i: (0, 0))],
        out_specs=[blk, pl.BlockSpec((None, nh, hd, hd), lambda i: (i, 0, 0, 0)), blk],
        out_shape=[jax.ShapeDtypeStruct((t, nh * hd), F32), jax.ShapeDtypeStruct((n, nh, hd, hd), F32),
                   jax.ShapeDtypeStruct((t, nh * hd), BF16)],
        scratch_shapes=[pltpu.VMEM((nh, hd, hd), F32)], name=name,
        compiler_params=_params(("arbitrary",)),
    )(u0, wk, qd, kd, qk, gl, zsrc, norm_w)


def delta_scan_bwd(do, u0, wk, qd, kd, qk, gl, s_in, *, name):
    t = u0.shape[0]
    nh, hd, n = N_HEADS_A, HEAD_DIM_A, t // CHUNK

    def body(do_ref, u0_ref, wk_ref, qd_ref, kd_ref, qk_ref, gl_ref, sin_ref,
             du0_ref, dwk_ref, dqd_ref, dkd_ref, dqk_ref, dgl_ref, ds_ref):
        @pl.when(pl.program_id(0) == 0)
        def _():
            ds_ref[...] = jnp.zeros_like(ds_ref)

        corner = (lax.broadcasted_iota(jnp.int32, (SUBLANES, LANES), 0) == 0) & (lax.broadcasted_iota(jnp.int32, (SUBLANES, LANES), 1) == 0)
        heads = list(range(nh))
        cols = lambda ref: [ref[:, _hsl(h)].astype(BF16) for h in heads]
        ss, dss = [sin_ref[h] for h in heads], [ds_ref[h] for h in heads]
        sbs, dsbs = _each(lambda s: s.astype(BF16), ss), _each(lambda d: d.astype(BF16), dss)
        dobs, wkbs, qdbs, kdbs, qkbs = cols(do_ref), cols(wk_ref), cols(qd_ref), cols(kd_ref), cols(qk_ref)
        ubs = _each(lambda h, wkb, sb: (u0_ref[:, _hsl(h)] - _dot(wkb, sb)).astype(BF16), heads, wkbs, sbs)
        dus = _each(lambda qkb, dob, kdb, dsb: _dot(qkb, dob, TN) + _dot(kdb, dsb), qkbs, dobs, kdbs, dsbs)
        dubs = _each(lambda du: du.astype(BF16), dus)
        dwks = _each(lambda dub, sb: -_dot(dub, sb, NT), dubs, sbs)
        dqds = _each(lambda dob, sb: _dot(dob, sb, NT), dobs, sbs)
        dkds = _each(lambda ub, dsb: _dot(ub, dsb, NT), ubs, dsbs)
        dqks = _each(lambda dob, ub: _dot(dob, ub, NT), dobs, ubs)
        dgls = _each(lambda s, d: jnp.sum(jnp.sum(s * d, axis=1, keepdims=True), axis=0, keepdims=True), ss, dss)
        dsn = _each(lambda h, d, qdb, dob, wkb, dub: gl_ref[pl.ds(h * SUBLANES, 1), :] * d + _dot(qdb, dob, TN) - _dot(wkb, dub, TN),
                    heads, dss, qdbs, dobs, wkbs, dubs)
        for h in heads:
            du0_ref[:, _hsl(h)] = dus[h]
            dwk_ref[:, _hsl(h)] = dwks[h]
            dqd_ref[:, _hsl(h)] = dqds[h]
            dkd_ref[:, _hsl(h)] = dkds[h]
            dqk_ref[:, _hsl(h)] = dqks[h]
            dgl_ref[h * SUBLANES:(h + 1) * SUBLANES, :] = jnp.where(corner, dgls[h], 0.0)
            ds_ref[h] = dsn[h]

    blk = pl.BlockSpec((CHUNK, nh * hd), lambda i: (n - 1 - i, 0))
    gl_spec = pl.BlockSpec((nh * SUBLANES, LANES), lambda i: (n - 1 - i, 0))
    return pl.pallas_call(
        body, grid=(n,),
        in_specs=[blk] * 6 + [gl_spec, pl.BlockSpec((None, nh, hd, hd), lambda i: (n - 1 - i, 0, 0, 0))],
        out_specs=[blk] * 5 + [gl_spec],
        out_shape=[jax.ShapeDtypeStruct((t, nh * hd), F32)] * 5 + [jax.ShapeDtypeStruct((n * nh * SUBLANES, LANES), F32)],
        scratch_shapes=[pltpu.VMEM((nh, hd, hd), F32)], name=name,
        compiler_params=_params(("arbitrary",)),
    )(do, u0, wk, qd, kd, qk, gl, s_in)


N_PAIRS = N_HEADS_B // 2
PAIRS_PER_KV = N_PAIRS // N_KV_B


def _psl(j):
    return slice(j * LANES, (j + 1) * LANES)


KV_STEP = 4


def _att_fn(qps, kcs, kps, vcs, vps, sinks, kv0, first):
    w = WINDOW
    lane = lax.broadcasted_iota(jnp.int32, (1, LANES), 1)
    lo = (lane < HEAD_DIM_B).astype(F32)
    qi = lax.broadcasted_iota(jnp.int32, (w, w), 0)
    kj = lax.broadcasted_iota(jnp.int32, (w, w), 1)
    dist_c = (qi - kj).astype(F32)
    valid_c = kj <= qi
    valid_p = (kj > qi) & (first < 0.5)
    bf = lambda xs: [a.astype(BF16) for a in xs]
    kcb, kpb, vcb, vpb = bf(kcs), bf(kps), bf(vcs), bf(vps)
    scale = HEAD_DIM_B ** -0.5
    heads = [(g, j, half) for g in range(len(kcs)) for j in range(PAIRS_PER_KV) for half in range(2)]
    kvs = [g for g, _, _ in heads]
    hmasks = [lo if half == 0 else 1.0 - lo for _, _, half in heads]
    hds = [2.0 * (PAIRS_PER_KV * (kv0 + g) + j) + half for g, j, half in heads]
    slopes = _each(lambda hd: jnp.exp(-(hd + 1.0) * (8.0 / N_HEADS_B * math.log(2.0))), hds)
    snks = _each(lambda hd: jnp.sum(sinks * (lane.astype(F32) == hd).astype(F32), axis=1, keepdims=True), hds)
    qhs = _each(lambda h, hm: (qps[h[0] * PAIRS_PER_KV + h[1]] * hm).astype(BF16), heads, hmasks)
    lcs = _each(lambda qh, g, sl: jnp.where(valid_c, _dot(qh, kcb[g], NT) * scale - sl * dist_c, NEG), qhs, kvs, slopes)
    lps = _each(lambda qh, g, sl: jnp.where(valid_p, _dot(qh, kpb[g], NT) * scale - sl * (dist_c + w), NEG), qhs, kvs, slopes)
    ms = _each(lambda lc, lp, sk: lax.stop_gradient(jnp.maximum(jnp.maximum(jnp.max(lc, axis=1, keepdims=True),
                                                                            jnp.max(lp, axis=1, keepdims=True)), sk)), lcs, lps, snks)
    ecs = _each(lambda lc, m: jnp.exp(lc - m), lcs, ms)
    eps = _each(lambda lp, m: jnp.exp(lp - m), lps, ms)
    invs = _each(lambda ec, ep, sk, m: 1.0 / (jnp.sum(ec, axis=1, keepdims=True) + jnp.sum(ep, axis=1, keepdims=True) + jnp.exp(sk - m)),
                 ecs, eps, snks, ms)
    ohs = _each(lambda ec, ep, inv, g, hm: (_dot((ec * inv).astype(BF16), vcb[g]) + _dot((ep * inv).astype(BF16), vpb[g])) * hm,
                ecs, eps, invs, kvs, hmasks)
    return [ohs[2 * j] + ohs[2 * j + 1] for j in range(len(qps))]


def _scalar11(v):
    return jnp.full((1, 1), v, F32)


def _att_specs(row_of):
    cur = pl.BlockSpec((WINDOW, KV_STEP * LANES), lambda i, kv: (row_of(i), kv))
    prev = pl.BlockSpec((WINDOW, KV_STEP * LANES), lambda i, kv: (jnp.maximum(row_of(i) - 1, 0), kv))
    qs = pl.BlockSpec((WINDOW, KV_STEP * PAIRS_PER_KV * LANES), lambda i, kv: (row_of(i), kv))
    return qs, cur, prev, pl.BlockSpec((1, LANES), lambda i, kv: (0, 0))


def swa_fwd(qsrc, kd, vd, sinks, *, name):
    t = kd.shape[0]
    nb = t // WINDOW
    npair = KV_STEP * PAIRS_PER_KV

    def body(q_ref, kc_ref, kp_ref, vc_ref, vp_ref, s_ref, o_ref):
        first = _scalar11((pl.program_id(0) == 0).astype(F32))
        kv0 = _scalar11((pl.program_id(1) * KV_STEP).astype(F32))
        per_kv = lambda ref: [ref[:, _psl(g)] for g in range(KV_STEP)]
        outs = _att_fn([q_ref[:, _psl(j)] for j in range(npair)], per_kv(kc_ref), per_kv(kp_ref), per_kv(vc_ref), per_kv(vp_ref),
                       s_ref[...], kv0, first)
        for j in range(npair):
            o_ref[:, _psl(j)] = outs[j].astype(o_ref.dtype)

    qs, cur, prev, sk = _att_specs(lambda i: i)
    return pl.pallas_call(
        body, grid=(nb, N_KV_B // KV_STEP), in_specs=[qs, cur, prev, cur, prev, sk],
        out_specs=qs, out_shape=jax.ShapeDtypeStruct((t, N_PAIRS * LANES), BF16), name=name,
        compiler_params=_params(("parallel", "parallel")),
    )(qsrc, kd, kd, vd, vd, sinks)


def swa_bwd(do, qsrc, kd, vd, sinks, *, name):
    t = kd.shape[0]
    nb = t // WINDOW

    npair = KV_STEP * PAIRS_PER_KV

    def body(do_ref, q_ref, kc_ref, kp_ref, vc_ref, vp_ref, s_ref, dq_ref, dk_ref, dv_ref, ds_ref, carry_k, carry_v):
        step, kvg = pl.program_id(0), pl.program_id(1)
        first = _scalar11((step == nb - 1).astype(F32))

        @pl.when((step == 0) & (kvg == 0))
        def _():
            carry_k[...] = jnp.zeros_like(carry_k)
            carry_v[...] = jnp.zeros_like(carry_v)
            ds_ref[...] = jnp.zeros_like(ds_ref)

        kv0 = _scalar11((kvg * KV_STEP).astype(F32))
        per_kv = lambda ref: [ref[:, _psl(g)].astype(F32) for g in range(KV_STEP)]
        _, vjp = jax.vjp(lambda *a: _att_fn(*a, kv0, first), [q_ref[:, _psl(j)].astype(F32) for j in range(npair)],
                         per_kv(kc_ref), per_kv(kp_ref), per_kv(vc_ref), per_kv(vp_ref), s_ref[...])
        dqs, dkc, dkp, dvc, dvp, dsk = vjp([do_ref[:, _psl(j)].astype(F32) for j in range(npair)])
        for j in range(npair):
            dq_ref[:, _psl(j)] = dqs[j].astype(dq_ref.dtype)
        ds_ref[...] += dsk
        fold = lambda g: g + pltpu.roll(g, HEAD_DIM_B, 1)
        for g in range(KV_STEP):
            kv = kvg * KV_STEP + g
            dk_ref[:, _psl(g)] = fold(dkc[g] + carry_k[kv]).astype(dk_ref.dtype)
            dv_ref[:, _psl(g)] = fold(dvc[g] + carry_v[kv]).astype(dv_ref.dtype)
            carry_k[kv] = dkp[g]
            carry_v[kv] = dvp[g]

    qs, cur, prev, sk = _att_specs(lambda i: nb - 1 - i)
    return pl.pallas_call(
        body, grid=(nb, N_KV_B // KV_STEP),
        in_specs=[qs, qs, cur, prev, cur, prev, sk],
        out_specs=[qs, cur, cur, sk],
        out_shape=[jax.ShapeDtypeStruct((t, N_PAIRS * LANES), BF16), jax.ShapeDtypeStruct((t, N_KV_B * LANES), BF16),
                   jax.ShapeDtypeStruct((t, N_KV_B * LANES), BF16), jax.ShapeDtypeStruct((1, LANES), F32)],
        scratch_shapes=[pltpu.VMEM((N_KV_B, WINDOW, LANES), F32), pltpu.VMEM((N_KV_B, WINDOW, LANES), F32)],
        name=name, compiler_params=_params(("arbitrary", "arbitrary")),
    )(do, qsrc, kd, kd, vd, vd, sinks)


def loss_head(h, tgt, w, *, name):
    t, d = h.shape
    tm = min(ROW_TM, t)

    def body(h_ref, t_ref, w_ref, dh_ref, dw_ref, l_ref):
        tg = t_ref[...]

        def f(hv, wv):
            err = _f_norm(hv, wv) - tg
            return 0.5 * jnp.sum(jnp.sum(err * err, axis=1, keepdims=True), axis=0, keepdims=True) * (1.0 / d)

        lv, vjp = jax.vjp(f, h_ref[...], w_ref[...])
        dh, dw = vjp(jnp.ones((1, 1), F32))
        dh_ref[...] = dh
        first = pl.program_id(0) == 0

        @pl.when(first)
        def _():
            dw_ref[...] = dw
            l_ref[...] = lv * jnp.ones((1, LANES), F32)

        @pl.when(jnp.logical_not(first))
        def _():
            dw_ref[...] += dw
            l_ref[...] += lv * jnp.ones((1, LANES), F32)

    rows = pl.BlockSpec((tm, d), lambda i: (i, 0))
    one = lambda c: pl.BlockSpec((1, c), lambda i: (0, 0))
    return pl.pallas_call(
        body, grid=(t // tm,), in_specs=[rows, rows, one(d)], out_specs=[rows, one(d), one(LANES)],
        out_shape=[jax.ShapeDtypeStruct((t, d), F32), jax.ShapeDtypeStruct((1, d), F32), jax.ShapeDtypeStruct((1, LANES), F32)],
        name=name, compiler_params=_params(("arbitrary",)),
    )(h, tgt, w)


def _row_tile(r, cap=256):
    tr = r
    if r % SUBLANES == 0:
        for cand in range(SUBLANES, min(r, cap) + 1, SUBLANES):
            if r % cand == 0:
                tr = cand
    return tr


def _adamw_update(wv, gv, mv, vv):
    mn = ADAM_B1 * mv + (1.0 - ADAM_B1) * gv
    vn = ADAM_B2 * vv + (1.0 - ADAM_B2) * jnp.square(gv)
    m_hat = mn / (1.0 - ADAM_B1 ** ADAM_STEP)
    v_hat = vn / (1.0 - ADAM_B2 ** ADAM_STEP)
    return -ADAM_LR * (m_hat / (jnp.sqrt(v_hat) + ADAM_EPS) + ADAM_WD * wv), mn, vn


def adamw_layers(w, halves, m, v, *, name):
    nl, r, c = w.shape
    tr = _row_tile(r // 2)
    nbh = r // 2 // tr

    def body(w_ref, *rest):
        g_refs, m_ref, v_ref = rest[:2 * nl], rest[2 * nl], rest[2 * nl + 1]
        d_ref, mo_ref, vo_ref, go_ref = rest[2 * nl + 2:]
        layer, i = pl.program_id(0), pl.program_id(1)
        mine = (i < nbh) == (lax.axis_index("c") == 0)
        gv = jnp.where(mine, g_refs[0][...], g_refs[1][...])
        for k in range(1, nl):
            gv = jnp.where(layer == k, jnp.where(mine, g_refs[2 * k][...], g_refs[2 * k + 1][...]), gv)
        d_ref[...], mo_ref[...], vo_ref[...] = _adamw_update(w_ref[...], gv, m_ref[...], v_ref[...])
        go_ref[...] = gv

    spec3 = pl.BlockSpec((None, tr, c), lambda k, i: (k, i, 0))
    g_specs = [pl.BlockSpec((tr, c), lambda k, i, q=q: (jnp.where(k == q, i % nbh, 0), 0)) for q in range(nl) for _ in range(2)]
    return pl.pallas_call(
        body, grid=(nl, r // tr), in_specs=[spec3] + g_specs + [spec3, spec3], out_specs=[spec3] * 4,
        out_shape=[jax.ShapeDtypeStruct((nl, r, c), F32)] * 4, name=name, compiler_params=_params(("arbitrary", "arbitrary")),
    )(w, *[h for pair in halves for h in pair], m, v)


def adamw(w, g, m, v, *, name):
    r, c = w.shape
    tr = _row_tile(r)

    def body(w_ref, g_ref, m_ref, v_ref, d_ref, mo_ref, vo_ref):
        d_ref[...], mo_ref[...], vo_ref[...] = _adamw_update(w_ref[...], g_ref[...], m_ref[...], v_ref[...])

    spec = pl.BlockSpec((tr, c), lambda i: (i, 0))
    return pl.pallas_call(
        body, grid=(r // tr,), in_specs=[spec] * 4, out_specs=[spec] * 3,
        out_shape=[jax.ShapeDtypeStruct((r, c), F32)] * 3, name=name, compiler_params=_params(("parallel",)),
    )(w, g, m, v)


def _place():
    return lax.axis_index("x"), lax.axis_index("y"), lax.axis_index("c")


def allgather8(blk, *, name):
    def body(x_ref, out_ref, send_sems, recv_sems, local_sem):
        x, y, c = _place()
        me = 4 * x + 2 * y + c
        mine = pltpu.make_async_copy(x_ref, out_ref.at[me], local_sem)
        mine.start()
        sent = []
        for k in range(1, N_DEV):
            to = (x ^ ((k >> 2) & 1), y ^ ((k >> 1) & 1), c ^ (k & 1))
            cp = pltpu.make_async_remote_copy(src_ref=x_ref, dst_ref=out_ref.at[me], send_sem=send_sems.at[k - 1],
                                              recv_sem=recv_sems.at[k - 1], device_id=to, device_id_type=MESH)
            cp.start()
            sent.append(cp)
        for k in range(1, N_DEV):
            frm = me ^ k
            pltpu.make_async_remote_copy(src_ref=x_ref, dst_ref=out_ref.at[frm], send_sem=send_sems.at[k - 1],
                                         recv_sem=recv_sems.at[k - 1], device_id=(x, y, c), device_id_type=MESH).wait_recv()
        for cp in sent:
            cp.wait_send()
        mine.wait()

    vm = pl.BlockSpec(memory_space=pltpu.VMEM)
    return pl.pallas_call(
        body, in_specs=[vm], out_specs=vm, out_shape=jax.ShapeDtypeStruct((N_DEV,) + blk.shape, blk.dtype), name=name,
        scratch_shapes=[pltpu.SemaphoreType.DMA((N_DEV - 1,)), pltpu.SemaphoreType.DMA((N_DEV - 1,)), pltpu.SemaphoreType.DMA],
    )(blk)


def _other_chips(x, y):
    return [(1 - x, y), (x, 1 - y), (1 - x, 1 - y)]


def _hbm_call(body, ins, out_shapes, n_sems, name):
    hbm = pl.BlockSpec(memory_space=pl.ANY)
    return pl.pallas_call(
        body, in_specs=[hbm] * len(ins), out_specs=[hbm] * len(out_shapes), out_shape=out_shapes, name=name,
        scratch_shapes=[pltpu.SemaphoreType.DMA((n_sems,)), pltpu.SemaphoreType.DMA((n_sems,))],
    )(*ins)


def _half_rows(c, rh):
    return pl.ds(pl.multiple_of(c * rh, BF16_ROWS), rh)


def gather_units(units, *, name):
    nu = len(units)
    shapes = []
    for arr, layer_major in units:
        r, cols = arr.shape
        shapes.append(jax.ShapeDtypeStruct((2, N_CHIPS, r // 2, cols) if layer_major else (N_CHIPS, r, cols), arr.dtype))

    def body(*refs):
        in_refs, out_refs, send_sems, recv_sems = refs[:nu], refs[nu:2 * nu], refs[2 * nu], refs[2 * nu + 1]
        x, y, c = _place()
        me_chip = 2 * x + y
        sib = (x, y, 1 - c)
        chips = _other_chips(x, y)

        def copy(k, src, dst, to):
            return pltpu.make_async_remote_copy(src_ref=src, dst_ref=dst, send_sem=send_sems.at[k], recv_sem=recv_sems.at[k],
                                                device_id=to, device_id_type=MESH)

        first, passed, landing = [], [], []
        for u, (arr, layer_major) in enumerate(units):
            rh = arr.shape[0] // 2
            out_ref = out_refs[u]
            slot = (lambda chip, half, o=out_ref: o.at[half, chip]) if layer_major else \
                   (lambda chip, half, o=out_ref, rh=rh: o.at[chip, _half_rows(half, rh), :])
            my_half = in_refs[u].at[_half_rows(c, rh), :]
            for j, (cx, cy) in enumerate(chips):
                k = 6 * u + j
                first.append(copy(k, my_half, slot(me_chip, c), (cx, cy, c)))
                passed.append(copy(k + 3, slot(2 * cx + cy, c), slot(2 * cx + cy, c), sib))
                landing.append((copy(k, my_half, slot(2 * cx + cy, c), sib), copy(k + 3, my_half, slot(2 * cx + cy, 1 - c), sib)))
        for cp in first:
            cp.start()
        for (over_ici, _), fwd in zip(landing, passed):
            over_ici.wait_recv()
            fwd.start()
        for _, from_sibling in landing:
            from_sibling.wait_recv()
        for cp in first + passed:
            cp.wait_send()

    return _hbm_call(body, [a for a, _ in units], shapes, 6 * nu, name)


HBM_SPEC = pl.BlockSpec(memory_space=pltpu.HBM)
SEM_SPEC = pl.BlockSpec(memory_space=pltpu.SEMAPHORE)
ORDERED_EFFECT = pltpu.SideEffectType.DATAFLOW_SIDE_EFFECTING


def _split_start(body, srcs, land_shapes, after, *, name):
    nu = len(srcs)
    lands = [lax.empty(s.shape, s.dtype) for s in land_shapes]

    def whole(*refs):
        body(refs[:nu], refs[nu:2 * nu], refs[2 * nu + 1], refs[2 * nu + 2])
        refs[-1][...] = jnp.zeros((SUBLANES, LANES), F32)

    hbm = lambda a: pltpu.with_memory_space_constraint(a, pltpu.HBM)
    sems = pltpu.SemaphoreType.DMA((nu,))
    res = pl.pallas_call(
        whole, name=name, in_specs=[HBM_SPEC] * (2 * nu) + [pl.BlockSpec(memory_space=pl.ANY)],
        out_shape=[sems, sems] + [pltpu.HBM(a.shape, a.dtype) for a in srcs] + [pltpu.HBM(s.shape, s.dtype) for s in land_shapes]
        + [jax.ShapeDtypeStruct((SUBLANES, LANES), F32)],
        out_specs=[SEM_SPEC, SEM_SPEC] + [HBM_SPEC] * (2 * nu) + [pl.BlockSpec(memory_space=pltpu.VMEM)],
        input_output_aliases={q: 2 + q for q in range(2 * nu)},
        compiler_params=pltpu.CompilerParams(has_side_effects=ORDERED_EFFECT),
    )(*[hbm(a) for a in srcs], *[hbm(a) for a in lands], after)
    return res[0], res[1], res[2:2 + nu], res[2 + nu:2 + 2 * nu], res[-1]


def _split_wait(pending, moved, after, *, name):
    send_sems, recv_sems, srcs, lands, _ = pending
    nu = len(srcs)

    def body(*refs):
        land_refs, ssem, rsem = refs[nu:2 * nu], refs[2 * nu], refs[2 * nu + 1]
        x, y, c = _place()
        for u in range(nu):
            size = moved(land_refs[u])
            cp = pltpu.make_async_remote_copy(src_ref=size, dst_ref=size, send_sem=ssem.at[u], recv_sem=rsem.at[u],
                                              device_id=(x, y, c), device_id_type=MESH)
            cp.wait_send()
            cp.wait_recv()

    res = pl.pallas_call(
        body, name=name, in_specs=[HBM_SPEC] * (2 * nu) + [SEM_SPEC, SEM_SPEC, pl.BlockSpec(memory_space=pl.ANY)],
        out_shape=[pltpu.HBM(a.shape, a.dtype) for a in srcs] + [pltpu.HBM(a.shape, a.dtype) for a in lands],
        out_specs=[HBM_SPEC] * (2 * nu), input_output_aliases={q: q for q in range(2 * nu)},
        compiler_params=pltpu.CompilerParams(has_side_effects=ORDERED_EFFECT),
    )(*srcs, *lands, send_sems, recv_sems, after)
    return res[nu:]


def gather_start(shards, after, *, name):
    def body(src_refs, land_refs, send_sems, recv_sems):
        x, y, c = _place()
        for u, shard in enumerate(shards):
            rows = _half_rows(c, shard.shape[0] // 2)
            for cx, cy in _other_chips(x, y):
                for core in range(2):
                    pltpu.make_async_remote_copy(src_ref=src_refs[u].at[rows, :], dst_ref=land_refs[u].at[2 * x + y, rows, :],
                                                 send_sem=send_sems.at[u], recv_sem=recv_sems.at[u], device_id=(cx, cy, core),
                                                 device_id_type=MESH).start()

    return _split_start(body, shards, [jax.ShapeDtypeStruct((N_CHIPS,) + s.shape, s.dtype) for s in shards], after, name=name)


def gather_wait(pending, after, *, name):
    return _split_wait(pending, lambda land: land.at[pl.ds(0, N_CHIPS - 1)], after, name=name)


def scatter_start(pairs, *, name):
    def body(src_refs, land_refs, send_sems, recv_sems):
        x, y, c = _place()
        for u in range(len(pairs)):
            for j, (cx, cy) in enumerate(_other_chips(x, y)):
                pltpu.make_async_remote_copy(src_ref=src_refs[u].at[2 * cx + cy], dst_ref=land_refs[u].at[j], send_sem=send_sems.at[u],
                                             recv_sem=recv_sems.at[u], device_id=(cx, cy, c), device_id_type=MESH).start()

    return _split_start(body, pairs, [jax.ShapeDtypeStruct((N_CHIPS - 1,) + p.shape[1:], p.dtype) for p in pairs], pairs[0], name=name)


def scatter_wait(pending, after, *, name):
    return _split_wait(pending, lambda land: land, after, name=name)


def swap_units(units, *, name):
    nu = len(units)

    def body(*refs):
        g_refs, out_refs, send_sems, recv_sems = refs[:nu], refs[nu:2 * nu], refs[2 * nu], refs[2 * nu + 1]
        x, y, c = _place()
        cps = [pltpu.make_async_remote_copy(src_ref=g_refs[u].at[:, _half_rows(1 - c, units[u].shape[1] // 2), :], dst_ref=out_refs[u],
                                            send_sem=send_sems.at[u], recv_sem=recv_sems.at[u], device_id=(x, y, 1 - c),
                                            device_id_type=MESH) for u in range(nu)]
        for cp in cps:
            cp.start()
        for cp in cps:
            cp.wait()

    shapes = [jax.ShapeDtypeStruct((N_CHIPS, g.shape[1] // 2, g.shape[2]), g.dtype) for g in units]
    return _hbm_call(body, units, shapes, nu, name)


def join_units(units, *, name):
    nu = len(units)

    def body(*refs):
        h_refs, out_refs, send_sems, recv_sems = refs[:nu], refs[nu:2 * nu], refs[2 * nu], refs[2 * nu + 1]
        x, y, c = _place()
        cps = [pltpu.make_async_remote_copy(src_ref=h_refs[u], dst_ref=out_refs[u], send_sem=send_sems.at[u], recv_sem=recv_sems.at[u],
                                            device_id=(x, y, 1 - c), device_id_type=MESH) for u in range(nu)]
        for cp in cps:
            cp.start()
        for cp in cps:
            cp.wait()

    return _hbm_call(body, units, [jax.ShapeDtypeStruct(h.shape, h.dtype) for h in units], nu, name)


def _half_tile(rh):
    tr = rh
    for cand in range(BF16_ROWS, min(rh, 512) + 1, BF16_ROWS):
        if rh % cand == 0:
            tr = cand
    return tr


def pair_add(g, sib, *, name):
    nc, rh, cols = sib.shape
    tr = _half_tile(rh)
    nbh = rh // tr

    def body(g0_ref, g1_ref, s_ref, o_ref):
        mine = jnp.where(lax.axis_index("c") == 0, g0_ref[...], g1_ref[...])
        o_ref[...] = (mine.astype(F32) + s_ref[...].astype(F32)).astype(o_ref.dtype)

    blk = lambda off: pl.BlockSpec((None, tr, cols), lambda j, i: (j, off + i, 0))
    return pl.pallas_call(
        body, grid=(nc, nbh), in_specs=[blk(0), blk(nbh), blk(0)], out_specs=blk(0),
        out_shape=jax.ShapeDtypeStruct(sib.shape, BF16), name=name, compiler_params=_params(("parallel", "parallel")),
    )(g, g, sib)


def chips_add(pair, landed, *, name):
    nc, rh, cols = pair.shape
    tr = _half_tile(rh)

    def body(*refs):
        chip = 2 * lax.axis_index("x") + lax.axis_index("y")
        acc = refs[0][...]
        for j in range(1, nc):
            acc = jnp.where(chip == j, refs[j][...], acc)
        acc = acc.astype(F32)
        for r in refs[nc:-1]:
            acc = acc + r[...].astype(F32)
        refs[-1][...] = acc

    part = lambda q: pl.BlockSpec((None, tr, cols), lambda i, q=q: (q, i, 0))
    return pl.pallas_call(
        body, grid=(rh // tr,), in_specs=[part(q) for q in range(nc)] + [part(q) for q in range(landed.shape[0])],
        out_specs=pl.BlockSpec((tr, cols), lambda i: (i, 0)),
        out_shape=jax.ShapeDtypeStruct((rh, cols), F32), name=name, compiler_params=_params(("parallel",)),
    )(*[pair] * nc, *[landed] * landed.shape[0])


def sum8(g, *, name):
    def body(g_ref, o_ref):
        acc = g_ref[0]
        for d in range(1, N_DEV):
            acc = acc + g_ref[d]
        o_ref[...] = acc

    return pl.pallas_call(body, out_shape=jax.ShapeDtypeStruct(g.shape[1:], F32), name=name)(g)


def _dup_halves(a):
    t = a.shape[0]
    a = a.reshape(t, N_KV_B, HEAD_DIM_B)
    return jnp.concatenate([a, a], axis=-1).reshape(t, N_KV_B * LANES)


def _undup(a):
    t = a.shape[0]
    return a.reshape(t, N_KV_B, LANES)[:, :, :HEAD_DIM_B].reshape(t, N_KV_B * HEAD_DIM_B)


def _lane_pad(v, offset=0):
    return jnp.zeros((1, LANES), F32).at[0, offset:offset + v.shape[0]].set(v)


SHARD_UP = 2 * D_FF // N_CHIPS
SHARD_BIN = (N_HEADS_B + 2 * N_KV_B) * HEAD_DIM_B // N_CHIPS
SHARD_PROJ = D_MODEL // N_CHIPS


def local_step(x, p, tgt, sm, weight, on_grads):
    t = x.shape[0]
    rtm = min(ROW_TM, t)
    hk = N_HEADS_A * HEAD_DIM_A
    qd_b = N_HEADS_B * HEAD_DIM_B
    kd_b = N_KV_B * HEAD_DIM_B
    gs = {}
    norm = lambda h, w, nm: tile_map(_f_norm, [(h, D_MODEL, 0)], [w], [(D_MODEL, BF16)], tm=rtm, ncol=1, name=nm)[0]

    spec = pl.BlockSpec
    mtm = _tile(D_MODEL, MM_TM_CAP)
    p_bf = p.astype(BF16)
    alog_p = _lane_pad(sm["a_log"][0], N_HEADS_A)
    dtb_p = _lane_pad(sm["a_dt_bias"][0], N_HEADS_A)
    sinks_p = _lane_pad(sm["b_sinks"][0])
    nw = lambda name, i: sm[name][i:i + 1]
    by_chip = lambda kdim, ns: dict(tn=ns, tk=kdim, b_spec=spec((None, kdim, ns), lambda r, j, kk: (j, kk, 0)))
    by_chip_t = lambda ndim, ns: dict(n=ndim, tn=ndim, tk=ns, b_spec=spec((None, ndim, ns), lambda r, j, kk: (kk, j, 0)))
    cache = {}

    def wgt(name, i, after):
        if (name, i) not in cache:
            cache[name, i] = weight(name, i, after)
        return cache[name, i]

    saved = []
    h = x
    hn_next = norm(h, nw("norm_mix", 0), "norm_mix0")
    for i in range(DEPTH):
        s = {"h0": h, "hn": hn_next}
        if i % 2 == 0:
            s["pm"] = mm(s["hn"], wgt("a_w_in", i, h), name="a_in")
            tail = (s["pm"], LANES, 4 * hk // LANES)
            s["c"] = conv_fwd(s["pm"], wgt("a_conv", i, h), name="a_conv")
            s["bg"] = tile_map(_f_betag, [tail], [alog_p, dtb_p], [(LANES, F32)], tm=rtm, ncol=1, name="a_betag")[0]
            s["prep"], s["tms"] = delta_prep(s["c"], s["bg"], name="a_prep")
            s["o"], s["s_in"], s["on"] = delta_scan(*s["prep"], s["pm"], sm["a_norm"], name="a_scan")
            h, s["hf"] = mm(s["on"], wgt("a_w_out", i, s["on"]), add=h, norm_w=nw("norm_ffn", i), name="a_out")
        else:
            s["pb"] = mm(s["hn"], wgt("b_w_in", i, s["hn"]), name="b_in", out_dtype=BF16, n=N_CHIPS * SHARD_BIN,
                         **by_chip(D_MODEL, SHARD_BIN))
            s["kd"], s["vd"] = _dup_halves(s["pb"][:, qd_b:qd_b + kd_b]), _dup_halves(s["pb"][:, qd_b + kd_b:])
            s["ao"] = swa_fwd(s["pb"], s["kd"], s["vd"], sinks_p, name="b_att")
            h, s["hf"] = mm(s["ao"], wgt("b_w_out", i, s["ao"]), add=h, norm_w=nw("norm_ffn", i), name="b_out")
        s["h1"] = h
        s["u"] = mm(s["hf"], wgt("f_w_up", i, s["hf"]), name=f"f_up{i}", out_dtype=BF16, n=2 * D_FF, tm_cap=2 * MM_TM_CAP,
                    **by_chip(D_MODEL, SHARD_UP))
        s["act"] = conv_act_fwd(s["u"], wgt("f_conv", i, s["hf"]), name=f"f_conv_act{i}")
        h, s["hp"] = mm(s["act"], wgt("f_w_down", i, s["act"]), add=h, norm_w=nw("norm_ple", i), name=f"f_down{i}", tk=D_FF)
        s["h2"] = h
        s["gl"] = mm(s["hp"], wgt("ple_w_gate", i, s["hp"]), name=f"ple_gate{i}")
        s["pe"] = mm(p_bf[i], wgt("ple_w_proj", i, s["hp"]), name=f"ple_proj{i}", n=D_MODEL, **by_chip(PLE_DIM, SHARD_PROJ))
        rows3 = [(h, D_MODEL, 0), (s["gl"], D_MODEL, 0), (s["pe"], D_MODEL, 0)]
        if i + 1 < DEPTH:
            def mix_norm(hv, g, e, wn):
                hn = hv + _f_ple(g, e)
                return hn, _f_norm(hn, wn)
            h, hn_next = tile_map(mix_norm, rows3, [nw("norm_mix", i + 1)], [(D_MODEL, F32), (D_MODEL, BF16)], tm=rtm, ncol=1,
                                  name=f"ple_mix{i}")
        else:
            h = tile_map(lambda hv, g, e: hv + _f_ple(g, e), rows3, [], [(D_MODEL, F32)], tm=rtm, ncol=1, name=f"ple_mix{i}")[0]
        saved.append(s)

    dh, gnf, loss = loss_head(h, tgt, sm["norm_final"][None, :], name="loss_head")
    gs["norm_final"] = gnf[0]

    g_mix, g_ffn, g_ple, g_conv = ([None] * DEPTH for _ in range(4))
    zero = jnp.zeros((1, 1), F32)
    for i in reversed(range(DEPTH)):
        s, gw = saved[i], {}
        by_rows = lambda g: g.reshape(N_CHIPS, g.shape[0] // N_CHIPS, g.shape[1])
        (dgl, dpe), _ = tile_vjp(_f_ple, [(s["gl"], D_MODEL, 0), (s["pe"], D_MODEL, 0)], [], [(dh, D_MODEL, 0)], n_diff=2,
                                 tm=rtm, ncol=1, name=f"ple_mix_bwd{i}", grad_dtypes=[BF16, BF16])
        gw["ple_w_proj"] = mm(p_bf[i], dpe, ta=True, name=f"ple_proj_dw{i}", out_dtype=BF16, tn=SHARD_PROJ,
                              o_shape=(N_CHIPS, PLE_DIM, SHARD_PROJ), o_spec=spec((None, PLE_DIM, SHARD_PROJ), lambda r, j, kk: (j, r, 0)))
        gw["ple_w_gate"] = by_rows(mm(s["hp"], dgl, ta=True, name=f"ple_gate_dw{i}", out_dtype=BF16))
        fused = dict(tb=True, tm_cap=MM_TM_CAP // 2)
        small = dict(tb=True)
        dh, g_ple[i] = mm(dgl, cache["ple_w_gate", i], name=f"ple_gate_dx{i}", norm_grad=(s["h2"], nw("norm_ple", i) + zero, dh), **small)

        dact = mm(dh, cache["f_w_down", i], tb=True, name=f"f_down_dx{i}")
        gw["f_w_down"] = by_rows(mm(s["act"], dh, ta=True, name=f"f_down_dw{i}", out_dtype=BF16, tm_cap=D_FF // 2))
        du_halves = conv_act_bwd(s["u"], dact, cache["f_conv", i], name=f"f_conv_act_bwd{i}")
        g_conv[i] = jnp.concatenate(du_halves[2:], axis=1)
        dhf = g_up = None
        for half, du in enumerate(du_halves[:2]):
            c0 = half * (N_CHIPS // 2)
            g_up = mm(s["hf"], du, ta=True, name=f"f_up_dw{i}_{half}", out_dtype=BF16, tn=SHARD_UP, into=g_up,
                      o_shape=(N_CHIPS, D_MODEL, SHARD_UP), o_spec=spec((None, mtm, SHARD_UP), lambda r, j, kk, c0=c0: (c0 + j, r, 0)))
            last = dict(norm_grad=(s["h1"], nw("norm_ffn", i), dh), **fused) if half else dict(tb=True)
            dhf = mm(du, cache["f_w_up", i], name=f"f_up_dx{i}_{half}", n=D_MODEL, tn=D_MODEL, tk=SHARD_UP, add=dhf,
                     b_spec=spec((None, D_MODEL, SHARD_UP), lambda r, j, kk, c0=c0: (c0 + kk, j, 0)), **last)
        gw["f_w_up"] = g_up
        dh, g_ffn[i] = dhf
        token, gw = on_grads(i, "ffn", gw), {}
        w_out = cache["a_w_out" if i % 2 == 0 else "b_w_out", i]
        if token is not None:
            w_out = w_out + token[:1, :1].astype(BF16)

        if i % 2 == 0:
            don = mm(dh, w_out, tb=True, name="a_out_dx")
            gw["a_w_out"] = by_rows(mm(s["on"], dh, ta=True, name="a_out_dw", out_dtype=BF16))
            do, dz, gs["a_norm"] = gnorm_bwd(s["o"], s["pm"], sm["a_norm"], don, name="a_gnorm_bwd")
            dprep = delta_scan_bwd(do, *s["prep"], s["s_in"], name="a_scan_bwd")
            dc, dbg = delta_prep_bwd(s["c"], s["bg"], s["tms"], dprep, name="a_prep_bwd")
            (dpt,), (galog, gdtb) = tile_vjp(_f_betag, [(s["pm"], LANES, 4 * hk // LANES)], [alog_p, dtb_p], [(dbg, LANES, 0)], n_diff=1,
                                             tm=rtm, ncol=1, name="a_betag_bwd", grad_dtypes=[BF16])
            gs["a_log"] = galog[:, N_HEADS_A:2 * N_HEADS_A]
            gs["a_dt_bias"] = gdtb[:, N_HEADS_A:2 * N_HEADS_A]
            dqkv, gs["a_conv"] = conv_bwd(dc, s["pm"], cache["a_conv", i], name="a_conv_bwd")
            dpm = jnp.concatenate([dqkv, dz, dpt], axis=1)
            g_in = mm(s["hn"], dpm, ta=True, name="a_in_dw", out_dtype=BF16)[:, :4 * hk + 2 * N_HEADS_A]
            gw["a_w_in"] = g_in.reshape(D_MODEL, N_CHIPS, g_in.shape[1] // N_CHIPS).transpose(1, 0, 2)
            dh, g_mix[i] = mm(dpm, cache["a_w_in", i], name="a_in_dx", norm_grad=(s["h0"], nw("norm_mix", i), dh), **small)
        else:
            dao = mm(dh, w_out, tb=True, name="b_out_dx")
            gw["b_w_out"] = by_rows(mm(s["ao"], dh, ta=True, name="b_out_dw", out_dtype=BF16))
            dq, dkd, dvd, gsk = swa_bwd(dao, s["pb"], s["kd"], s["vd"], sinks_p, name="b_att_bwd")
            gs["b_sinks"] = gsk[:, :N_HEADS_B]
            dpb = jnp.concatenate([dq, _undup(dkd), _undup(dvd)], axis=1)
            gw["b_w_in"] = mm(s["hn"], dpb, ta=True, name="b_in_dw", out_dtype=BF16, tn=SHARD_BIN,
                              o_shape=(N_CHIPS, D_MODEL, SHARD_BIN), o_spec=spec((None, mtm, SHARD_BIN), lambda r, j, kk: (j, r, 0)))
            dh, g_mix[i] = mm(dpb, cache["b_w_in", i], name="b_in_dx", norm_grad=(s["h0"], nw("norm_mix", i), dh), **small,
                              **by_chip_t(D_MODEL, SHARD_BIN))
        token = on_grads(i, "mix", gw)
        if token is not None:
            zero = token[:1, :1]

    gs["norm_mix"], gs["norm_ffn"], gs["norm_ple"] = (jnp.concatenate(g, axis=0) for g in (g_mix, g_ffn, g_ple))
    gs["f_conv"] = jnp.stack(g_conv)
    return loss, dh, gs


BIG = ["a_w_in", "a_w_out", "b_w_in", "b_w_out", "f_w_up", "f_w_down", "ple_w_proj", "ple_w_gate"]
LAYERED = {"f_w_up", "f_w_down", "ple_w_proj", "ple_w_gate"}
BY_CHIP = {"b_w_in", "f_w_up", "ple_w_proj"}
LAYER_UNITS = [[("a_w_in", 0), ("a_w_out", 0)] + [(n, 0) for n in sorted(LAYERED)],
               [("b_w_in", 1), ("b_w_out", 1)] + [(n, 1) for n in sorted(LAYERED)]]
CONVS = ["a_conv", "f_conv"]
SMALL = ["norm_mix", "norm_ffn", "norm_ple", "norm_final", "a_log", "a_dt_bias", "a_norm", "b_sinks"]
SMALL_ROWS = 8
CONV_ROWS = 16
CONV_GRAD_ROWS = 48


def _pack_rows(arrs, rows, dtype):
    flat = jnp.concatenate([a.reshape(-1).astype(dtype) for a in arrs])
    return jnp.pad(flat, (0, rows * PACK_COLS - flat.shape[0])).reshape(rows, PACK_COLS)


def _unpack(flat, shapes):
    out, off = [], 0
    for shp in shapes:
        n = math.prod(shp)
        out.append(flat[off:off + n].reshape(shp))
        off += n
    return out


def _pack_small(d, loss=None):
    tail = jnp.concatenate([d["a_log"].reshape(-1), d["a_dt_bias"].reshape(-1), d["a_norm"].reshape(-1), d["b_sinks"].reshape(-1)])
    if loss is not None:
        tail = jnp.concatenate([tail, loss.reshape(-1)[:1]])
    tail = jnp.pad(tail, (0, PACK_COLS - tail.shape[0]))
    return jnp.concatenate([d["norm_mix"], d["norm_ffn"], d["norm_ple"], d["norm_final"][None, :], tail[None, :]], axis=0)


def _unpack_small(a, like):
    out = {"norm_mix": a[0:2], "norm_ffn": a[2:4], "norm_ple": a[4:6], "norm_final": a[6]}
    off = 0
    for nm in ("a_log", "a_dt_bias", "a_norm", "b_sinks"):
        n = like[nm].size
        out[nm] = a[7, off:off + n].reshape(like[nm].shape)
        off += n
    return out, a[7, off]


def _as2d(a):
    return a.reshape(-1, a.shape[-1])


def kernel(x, p, norm_mix, norm_ffn, norm_ple, norm_final, a_w_in, a_conv, a_log, a_dt_bias, a_norm, a_w_out, b_w_in, b_sinks, b_w_out, f_w_up, f_conv, f_w_down, ple_w_proj, ple_w_gate, loss_target, m_norm_mix, m_norm_ffn, m_norm_ple, m_norm_final, m_a_w_in, m_a_conv, m_a_log, m_a_dt_bias, m_a_norm, m_a_w_out, m_b_w_in, m_b_sinks, m_b_w_out, m_f_w_up, m_f_conv, m_f_w_down, m_ple_w_proj, m_ple_w_gate, v_norm_mix, v_norm_ffn, v_norm_ple, v_norm_final, v_a_w_in, v_a_conv, v_a_log, v_a_dt_bias, v_a_norm, v_a_w_out, v_b_w_in, v_b_sinks, v_b_w_out, v_f_w_up, v_f_conv, v_f_w_down, v_ple_w_proj, v_ple_w_gate):
    w = dict(norm_mix=norm_mix, norm_ffn=norm_ffn, norm_ple=norm_ple, norm_final=norm_final, a_w_in=a_w_in, a_conv=a_conv,
             a_log=a_log, a_dt_bias=a_dt_bias, a_norm=a_norm, a_w_out=a_w_out, b_w_in=b_w_in, b_sinks=b_sinks, b_w_out=b_w_out,
             f_w_up=f_w_up, f_conv=f_conv, f_w_down=f_w_down, ple_w_proj=ple_w_proj, ple_w_gate=ple_w_gate)
    m = dict(norm_mix=m_norm_mix, norm_ffn=m_norm_ffn, norm_ple=m_norm_ple, norm_final=m_norm_final, a_w_in=m_a_w_in,
             a_conv=m_a_conv, a_log=m_a_log, a_dt_bias=m_a_dt_bias, a_norm=m_a_norm, a_w_out=m_a_w_out, b_w_in=m_b_w_in,
             b_sinks=m_b_sinks, b_w_out=m_b_w_out, f_w_up=m_f_w_up, f_conv=m_f_conv, f_w_down=m_f_w_down,
             ple_w_proj=m_ple_w_proj, ple_w_gate=m_ple_w_gate)
    v = dict(norm_mix=v_norm_mix, norm_ffn=v_norm_ffn, norm_ple=v_norm_ple, norm_final=v_norm_final, a_w_in=v_a_w_in,
             a_conv=v_a_conv, a_log=v_a_log, a_dt_bias=v_a_dt_bias, a_norm=v_a_norm, a_w_out=v_a_w_out, b_w_in=v_b_w_in,
             b_sinks=v_b_sinks, b_w_out=v_b_w_out, f_w_up=v_f_w_up, f_conv=v_f_conv, f_w_down=v_f_w_down,
             ple_w_proj=v_ple_w_proj, ple_w_gate=v_ple_w_gate)
    xc, yc, cc = _place()
    my_chip = 2 * xc + yc

    shard = {(n, i): w[n][i if n in LAYERED else 0].astype(BF16) for n, i in LAYER_UNITS[0] + LAYER_UNITS[1]}
    first = shard["a_w_in", 0]
    (ga,) = gather_units([(first, False)], name="gather_first")
    ga = lax.dynamic_update_index_in_dim(ga, first, my_chip, 0)
    a_in = jnp.concatenate([ga[j] for j in range(N_CHIPS)], axis=1)
    n_main = 4 * N_HEADS_A * HEAD_DIM_A
    conv_shapes = [w[n].shape for n in CONVS]
    convs = allgather8(_pack_rows([w[n] for n in CONVS], CONV_ROWS, F32), name="gather_convs")
    conv_parts = [_unpack(convs[2 * j].reshape(-1), conv_shapes) for j in range(N_CHIPS)]
    a_conv_full, f_conv_full = (jnp.concatenate([conv_parts[j][q] for j in range(N_CHIPS)], axis=2) for q in range(2))
    ready = {("a_w_in", 0): jnp.pad(a_in, ((0, 0), (0, n_main + LANES - a_in.shape[1]))), ("a_conv", 0): a_conv_full[0], ("f_conv", 0): f_conv_full[0], ("f_conv", 1): f_conv_full[1]}
    later = [[k for k in units if k != ("a_w_in", 0)] for units in LAYER_UNITS]
    pending, after = [], ga
    for layer, keys in enumerate(later):
        pending.append(gather_start([shard[k] for k in keys], after, name=f"gather_start{layer}"))
        after = pending[-1][4]
    sm = {n: w[n] for n in SMALL}
    sm["norm_mix"] = sm["norm_mix"] + after[:1, :1]

    def weight(name, layer, act):
        if (name, layer) not in ready:
            landed = gather_wait(pending[layer], act, name=f"gather_wait{layer}")
            for k, g in zip(later[layer], landed):
                g = lax.dynamic_update_index_in_dim(g, shard[k], my_chip, 0)
                ready[k] = g if k[0] in BY_CHIP else g.reshape(N_CHIPS * g.shape[1], g.shape[2])
        return ready[name, layer]

    pairs, scattered, started = {}, {}, []

    def on_grads(layer, part, gw):
        keys = [k for k in LAYER_UNITS[layer] if (k[0] in LAYERED) == (part == "ffn")]
        from_sib = swap_units([gw[n] for n, _ in keys], name=f"rs_swap_{part}{layer}")
        for (n, _), sib in zip(keys, from_sib):
            pairs[n, layer] = pair_add(gw[n], sib, name=f"rs_add_pair_{n}{layer}")
        started.append((keys, scatter_start([pairs[k] for k in keys], name=f"rs_scatter_start_{part}{layer}"), f"{part}{layer}"))
        return started[-1][1][4]

    loss, grad_x, gs = local_step(x[0], p[:, 0], loss_target[0], sm, weight, on_grads)

    grads, delta, new_m, new_v, g_unit = {}, {}, {}, {}, {}

    def finish(keys, tag):
        halves = [chips_add(pairs[k], scattered[k], name=f"rs_add_chips_{k[0]}{k[1]}") for k in keys]
        g_unit.update(zip(keys, zip(halves, join_units(halves, name=f"rs_join_{tag}"))))
        for n in BIG:
            mine = [(n, i) for i in range(DEPTH) if (n, i) in LAYER_UNITS[i]]
            if n not in delta and all(k in g_unit for k in mine):
                g_layers = [g_unit[k] for k in mine]
                shape3 = (len(g_layers), 2 * g_layers[0][0].shape[0], g_layers[0][0].shape[1])
                res = adamw_layers(w[n].reshape(shape3), g_layers, m[n].reshape(shape3), v[n].reshape(shape3), name=f"adamw_{n}")
                delta[n], new_m[n], new_v[n], grads[n] = (r.reshape(w[n].shape) for r in res)

    last_keys, last_pending, last_tag = started[-1]
    for keys, pend, tag in started[:-1]:
        scattered.update(zip(keys, scatter_wait(pend, last_pending[4], name=f"rs_scatter_wait_{tag}")))
    finish([k for keys, _, _ in started[:-1] for k in keys], "first")

    conv_grads = _pack_rows([gs[n] for n in CONVS], CONV_GRAD_ROWS, F32)
    small_sum = sum8(allgather8(jnp.concatenate([_pack_small(gs, loss), conv_grads]), name="gather_small"), name="sum_small")
    g_sm, loss_sum = _unpack_small(small_sum[:SMALL_ROWS], sm)

    scattered.update(zip(last_keys, scatter_wait(last_pending, small_sum, name=f"rs_scatter_wait_{last_tag}")))
    finish(last_keys, "last")

    for n, full in zip(CONVS, _unpack(small_sum[SMALL_ROWS:].reshape(-1), [gs[n].shape for n in CONVS])):
        g2 = _as2d(lax.dynamic_slice_in_dim(full, my_chip * w[n].shape[-1], w[n].shape[-1], axis=full.ndim - 1))
        d2, m2, v2 = adamw(_as2d(w[n]), g2, _as2d(m[n]), _as2d(v[n]), name=f"adamw_{n}")
        grads[n], delta[n], new_m[n], new_v[n] = (r.reshape(w[n].shape) for r in (g2, d2, m2, v2))
    pk = lambda d: _pack_small(d)
    d2, m2, v2 = adamw(pk(sm), pk(g_sm), pk({n: m[n] for n in SMALL}), pk({n: v[n] for n in SMALL}), name="adamw_small")
    for src, dst in ((d2, delta), (m2, new_m), (v2, new_v)):
        dst.update(_unpack_small(src, sm)[0])
    grads.update(g_sm)

    order = ["norm_mix", "norm_ffn", "norm_ple", "norm_final", "a_w_in", "a_conv", "a_log", "a_dt_bias", "a_norm", "a_w_out",
             "b_w_in", "b_sinks", "b_w_out", "f_w_up", "f_conv", "f_w_down", "ple_w_proj", "ple_w_gate"]
    return (loss_sum, grad_x[None], *[grads[n] for n in order], *[delta[n] for n in order],
            *[new_m[n] for n in order], *[new_v[n] for n in order])
```
